```python
import jax, jax.numpy as jnp
from jax import lax
import numpy as np

D_MODEL = 1024
BATCH = 8
SEQ = 4096
DEPTH = 1

HGRN_HEADS = 4
HGRN_KEY_DIM = 128
HGRN_VAL_DIM = (D_MODEL // 2) // HGRN_HEADS
HGRN_KEY_WIDTH = HGRN_HEADS * HGRN_KEY_DIM
HGRN_WIDTH = HGRN_HEADS * HGRN_VAL_DIM
CHUNK = 64
POOL_WINDOWS = (2, 4, 8, 16)
POOL_GROUPS = len(POOL_WINDOWS)
POOL_WIDTH = D_MODEL - HGRN_WIDTH
POOL_GROUP_DIM = POOL_WIDTH // POOL_GROUPS
POOL_MAX_W = max(POOL_WINDOWS)
MIX_WIDTH = HGRN_WIDTH + POOL_WIDTH
IN_WIDTH = 2 * HGRN_KEY_WIDTH + 2 * HGRN_WIDTH + POOL_WIDTH
MEM_LEN = 256
XATTN_HEADS = 4
XATTN_HEAD_DIM = D_MODEL // XATTN_HEADS
D_FF = 4 * D_MODEL
EPS = 1e-6

kernel_name = "hymba_style_hgrn2_pool_hybrid"


def rmsnorm(x, g):
    xf = x.astype(jnp.float32)
    y = xf * lax.rsqrt(jnp.mean(xf * xf, axis=-1, keepdims=True) + EPS)
    return (y * g.astype(jnp.float32)).astype(x.dtype)


def hgrn2_chunkwise(q, k, v, log_f):
    B, S, H, DK = q.shape
    DV = v.shape[-1]
    n = S // CHUNK

    def to_chunks(a):
        return a.reshape(B, n, CHUNK, H, a.shape[-1]).transpose(1, 0, 3, 2, 4)

    qc, kc, vc, fc = to_chunks(q), to_chunks(k), to_chunks(v), to_chunks(log_f)
    causal = jnp.tril(jnp.ones((CHUNK, CHUNK), dtype=bool))

    def step(state, inp):
        q_, k_, v_, lf = inp
        b = jnp.cumsum(lf, axis=2)
        diff = b[:, :, :, None, :] - b[:, :, None, :, :]
        decay = jnp.exp(jnp.where(causal[None, None, :, :, None], diff, -jnp.inf))
        scores = jnp.einsum('bhtk,bhsk,bhtsk->bhts', q_, k_, decay)
        o_intra = jnp.einsum('bhts,bhsv->bhtv', scores, v_)
        o_inter = jnp.einsum('bhtk,bhkv->bhtv', q_ * jnp.exp(b), state)
        b_last = b[:, :, -1:, :]
        k_dec = k_ * jnp.exp(b_last - b)
        state = jnp.exp(b_last[:, :, 0, :])[..., None] * state + jnp.einsum('bhsk,bhsv->bhkv', k_dec, v_)
        return state, o_intra + o_inter

    s0 = jnp.zeros((B, H, DK, DV), jnp.float32)
    _, o = lax.scan(step, s0, (qc, kc, vc, fc))
    return o.transpose(1, 0, 3, 2, 4).reshape(B, S, H, DV)


def multiscale_pool(p, w_pool, pool_scale):
    B, S, _ = p.shape
    pg = p.astype(jnp.float32).reshape(B, S, POOL_GROUPS, POOL_GROUP_DIM)
    cs = jnp.cumsum(pg, axis=1)
    cs = jnp.pad(cs, ((0, 0), (POOL_MAX_W, 0), (0, 0), (0, 0)))
    pos = (jnp.arange(S) + 1)
    outs = []
    for gi, w in enumerate(POOL_WINDOWS):
        win = cs[:, POOL_MAX_W:, gi] - cs[:, POOL_MAX_W - w:POOL_MAX_W - w + S, gi]
        cnt = jnp.minimum(pos, w).astype(jnp.float32)[None, :, None]
        outs.append(win / cnt - pg[:, :, gi])
    pooled = jnp.stack(outs, axis=2)
    y = jnp.einsum('bsgc,gcd->bsgd', pooled, w_pool.astype(jnp.float32))
    return y.reshape(B, S, POOL_WIDTH) * pool_scale.astype(jnp.float32)


def _fwd_setup_inputs(seed: int = 0) -> dict:
    key = jax.random.key(seed)
    ks = jax.random.split(key, 24)
    f32 = jnp.float32

    def dense(k, shape, fan_in):
        return jax.random.normal(k, shape, f32) * (fan_in ** -0.5)

    def gain(k, shape):
        return 1.0 + 0.02 * jax.random.normal(k, shape, f32)

    return {
        "x": jax.random.normal(ks[0], (BATCH, SEQ, D_MODEL), f32),
        "mem": jax.random.normal(ks[1], (BATCH, MEM_LEN, D_MODEL), f32),
        "norm_mix_g": gain(ks[2], (DEPTH, D_MODEL)),
        "w_in": dense(ks[3], (DEPTH, D_MODEL, IN_WIDTH), D_MODEL),
        "lb_logits": 0.5 * jax.random.normal(ks[4], (DEPTH + 1, HGRN_KEY_WIDTH), f32),
        "hgrn_norm_g": gain(ks[5], (DEPTH, HGRN_HEADS, HGRN_VAL_DIM)),
        "w_pool": dense(ks[6], (DEPTH, POOL_GROUPS, POOL_GROUP_DIM, POOL_GROUP_DIM), POOL_GROUP_DIM),
        "pool_scale": gain(ks[7], (DEPTH, POOL_WIDTH)),
        "w_out": dense(ks[8], (DEPTH, MIX_WIDTH, D_MODEL), MIX_WIDTH),
        "norm_x_g": gain(ks[9], (DEPTH, D_MODEL)),
        "norm_mem_g": gain(ks[10], (DEPTH, D_MODEL)),
        "w_xq": dense(ks[11], (DEPTH, D_MODEL, XATTN_HEADS, XATTN_HEAD_DIM), D_MODEL),
        "w_xk": dense(ks[12], (DEPTH, D_MODEL, XATTN_HEADS, XATTN_HEAD_DIM), D_MODEL),
        "w_xv": dense(ks[13], (DEPTH, D_MODEL, XATTN_HEADS, XATTN_HEAD_DIM), D_MODEL),
        "w_xo": dense(ks[14], (DEPTH, XATTN_HEADS, XATTN_HEAD_DIM, D_MODEL), D_MODEL),
        "norm_ffn_g": gain(ks[15], (DEPTH, D_MODEL)),
        "w_ff1": dense(ks[16], (DEPTH, D_MODEL, D_FF), D_MODEL),
        "w_ff2": dense(ks[17], (DEPTH, D_FF, D_MODEL), D_FF),
        "final_norm_g": gain(ks[18], (D_MODEL,)),
    }


def _fwd_reference(x, mem, norm_mix_g, w_in, lb_logits, hgrn_norm_g, w_pool, pool_scale, w_out,
              norm_x_g, norm_mem_g, w_xq, w_xk, w_xv, w_xo, norm_ffn_g, w_ff1, w_ff2, final_norm_g):
    B, S, _ = x.shape
    f32 = jnp.float32
    lower_bounds = jnp.cumsum(jax.nn.softmax(lb_logits.astype(f32), axis=0), axis=0)
    split_at = [HGRN_KEY_WIDTH, 2 * HGRN_KEY_WIDTH,
                2 * HGRN_KEY_WIDTH + HGRN_WIDTH, 2 * HGRN_KEY_WIDTH + 2 * HGRN_WIDTH]
    for l in range(DEPTH):
        h = rmsnorm(x, norm_mix_g[l])
        z = jnp.einsum('bsd,de->bse', h, w_in[l])
        q_pre, f_pre, i_pre, g_pre, p = jnp.split(z, split_at, axis=-1)

        lb = lower_bounds[l]
        f = lb + (1.0 - lb) * jax.nn.sigmoid(f_pre.astype(f32))
        log_f = jnp.log(f)
        k = 1.0 - f
        q = jax.nn.silu(q_pre.astype(f32))
        hs = lambda a, d: a.reshape(B, S, HGRN_HEADS, d)
        o = hgrn2_chunkwise(hs(q, HGRN_KEY_DIM), hs(k, HGRN_KEY_DIM),
                            hs(i_pre.astype(f32), HGRN_VAL_DIM), hs(log_f, HGRN_KEY_DIM))
        o = o * lax.rsqrt(jnp.mean(o * o, axis=-1, keepdims=True) + EPS) * hgrn_norm_g[l].astype(f32)
        o_a = o.reshape(B, S, HGRN_WIDTH) * jax.nn.silu(g_pre.astype(f32))

        o_b = multiscale_pool(p, w_pool[l], pool_scale[l])

        mixed = jnp.concatenate([o_a, o_b], axis=-1).astype(x.dtype)
        x = x + jnp.einsum('bse,ed->bsd', mixed, w_out[l])

        hq = rmsnorm(x, norm_x_g[l])
        hm = rmsnorm(mem, norm_mem_g[l])
        xq = jnp.einsum('bsd,dhe->bshe', hq, w_xq[l])
        xk = jnp.einsum('bmd,dhe->bmhe', hm, w_xk[l])
        xv = jnp.einsum('bmd,dhe->bmhe', hm, w_xv[l])
        scores = jnp.einsum('bshe,bmhe->bhsm', xq, xk).astype(f32) * (XATTN_HEAD_DIM ** -0.5)
        probs = jax.nn.softmax(scores, axis=-1).astype(x.dtype)
        att = jnp.einsum('bhsm,bmhe->bshe', probs, xv)
        x = x + jnp.einsum('bshe,hed->bsd', att, w_xo[l])

        hf = rmsnorm(x, norm_ffn_g[l])
        u = jnp.square(jax.nn.relu(jnp.einsum('bsd,df->bsf', hf, w_ff1[l])))
        x = x + jnp.einsum('bsf,fd->bsd', u, w_ff2[l])
    return rmsnorm(x, final_norm_g)


import jax as _jax
import jax.numpy as _jnp

TWIN_FORMAT = 'train_step'
FWD_PARAMS = ['x', 'mem', 'norm_mix_g', 'w_in', 'lb_logits', 'hgrn_norm_g', 'w_pool', 'pool_scale', 'w_out', 'norm_x_g', 'norm_mem_g', 'w_xq', 'w_xk', 'w_xv', 'w_xo', 'norm_ffn_g', 'w_ff1', 'w_ff2', 'final_norm_g']
TWIN_WEIGHTS = ['norm_mix_g', 'w_in', 'lb_logits', 'hgrn_norm_g', 'w_pool', 'pool_scale', 'w_out', 'norm_x_g', 'norm_mem_g', 'w_xq', 'w_xk', 'w_xv', 'w_xo', 'norm_ffn_g', 'w_ff1', 'w_ff2', 'final_norm_g']
TWIN_DIFF_INPUT = 'x'
TWIN_INPUTS = ['x', 'mem', 'norm_mix_g', 'w_in', 'lb_logits', 'hgrn_norm_g', 'w_pool', 'pool_scale', 'w_out', 'norm_x_g', 'norm_mem_g', 'w_xq', 'w_xk', 'w_xv', 'w_xo', 'norm_ffn_g', 'w_ff1', 'w_ff2', 'final_norm_g', 'loss_target', 'm_norm_mix_g', 'm_w_in', 'm_lb_logits', 'm_hgrn_norm_g', 'm_w_pool', 'm_pool_scale', 'm_w_out', 'm_norm_x_g', 'm_norm_mem_g', 'm_w_xq', 'm_w_xk', 'm_w_xv', 'm_w_xo', 'm_norm_ffn_g', 'm_w_ff1', 'm_w_ff2', 'm_final_norm_g', 'v_norm_mix_g', 'v_w_in', 'v_lb_logits', 'v_hgrn_norm_g', 'v_w_pool', 'v_pool_scale', 'v_w_out', 'v_norm_x_g', 'v_norm_mem_g', 'v_w_xq', 'v_w_xk', 'v_w_xv', 'v_w_xo', 'v_norm_ffn_g', 'v_w_ff1', 'v_w_ff2', 'v_final_norm_g']
TWIN_OUTPUTS = ['loss', 'grad_x', 'grad_norm_mix_g', 'grad_w_in', 'grad_lb_logits', 'grad_hgrn_norm_g', 'grad_w_pool', 'grad_pool_scale', 'grad_w_out', 'grad_norm_x_g', 'grad_norm_mem_g', 'grad_w_xq', 'grad_w_xk', 'grad_w_xv', 'grad_w_xo', 'grad_norm_ffn_g', 'grad_w_ff1', 'grad_w_ff2', 'grad_final_norm_g', 'delta_norm_mix_g', 'delta_w_in', 'delta_lb_logits', 'delta_hgrn_norm_g', 'delta_w_pool', 'delta_pool_scale', 'delta_w_out', 'delta_norm_x_g', 'delta_norm_mem_g', 'delta_w_xq', 'delta_w_xk', 'delta_w_xv', 'delta_w_xo', 'delta_norm_ffn_g', 'delta_w_ff1', 'delta_w_ff2', 'delta_final_norm_g', 'new_m_norm_mix_g', 'new_m_w_in', 'new_m_lb_logits', 'new_m_hgrn_norm_g', 'new_m_w_pool', 'new_m_pool_scale', 'new_m_w_out', 'new_m_norm_x_g', 'new_m_norm_mem_g', 'new_m_w_xq', 'new_m_w_xk', 'new_m_w_xv', 'new_m_w_xo', 'new_m_norm_ffn_g', 'new_m_w_ff1', 'new_m_w_ff2', 'new_m_final_norm_g', 'new_v_norm_mix_g', 'new_v_w_in', 'new_v_lb_logits', 'new_v_hgrn_norm_g', 'new_v_w_pool', 'new_v_pool_scale', 'new_v_w_out', 'new_v_norm_x_g', 'new_v_norm_mem_g', 'new_v_w_xq', 'new_v_w_xk', 'new_v_w_xv', 'new_v_w_xo', 'new_v_norm_ffn_g', 'new_v_w_ff1', 'new_v_w_ff2', 'new_v_final_norm_g']
TWIN_LEAF_KINDS = {'loss': 'loss', 'grad_x': 'grad_x', 'grad_norm_mix_g': 'grad_w', 'grad_w_in': 'grad_w', 'grad_lb_logits': 'grad_w', 'grad_hgrn_norm_g': 'grad_w', 'grad_w_pool': 'grad_w', 'grad_pool_scale': 'grad_w', 'grad_w_out': 'grad_w', 'grad_norm_x_g': 'grad_w', 'grad_norm_mem_g': 'grad_w', 'grad_w_xq': 'grad_w', 'grad_w_xk': 'grad_w', 'grad_w_xv': 'grad_w', 'grad_w_xo': 'grad_w', 'grad_norm_ffn_g': 'grad_w', 'grad_w_ff1': 'grad_w', 'grad_w_ff2': 'grad_w', 'grad_final_norm_g': 'grad_w', 'delta_norm_mix_g': 'delta_w', 'delta_w_in': 'delta_w', 'delta_lb_logits': 'delta_w', 'delta_hgrn_norm_g': 'delta_w', 'delta_w_pool': 'delta_w', 'delta_pool_scale': 'delta_w', 'delta_w_out': 'delta_w', 'delta_norm_x_g': 'delta_w', 'delta_norm_mem_g': 'delta_w', 'delta_w_xq': 'delta_w', 'delta_w_xk': 'delta_w', 'delta_w_xv': 'delta_w', 'delta_w_xo': 'delta_w', 'delta_norm_ffn_g': 'delta_w', 'delta_w_ff1': 'delta_w', 'delta_w_ff2': 'delta_w', 'delta_final_norm_g': 'delta_w', 'new_m_norm_mix_g': 'new_m', 'new_m_w_in': 'new_m', 'new_m_lb_logits': 'new_m', 'new_m_hgrn_norm_g': 'new_m', 'new_m_w_pool': 'new_m', 'new_m_pool_scale': 'new_m', 'new_m_w_out': 'new_m', 'new_m_norm_x_g': 'new_m', 'new_m_norm_mem_g': 'new_m', 'new_m_w_xq': 'new_m', 'new_m_w_xk': 'new_m', 'new_m_w_xv': 'new_m', 'new_m_w_xo': 'new_m', 'new_m_norm_ffn_g': 'new_m', 'new_m_w_ff1': 'new_m', 'new_m_w_ff2': 'new_m', 'new_m_final_norm_g': 'new_m', 'new_v_norm_mix_g': 'new_v', 'new_v_w_in': 'new_v', 'new_v_lb_logits': 'new_v', 'new_v_hgrn_norm_g': 'new_v', 'new_v_w_pool': 'new_v', 'new_v_pool_scale': 'new_v', 'new_v_w_out': 'new_v', 'new_v_norm_x_g': 'new_v', 'new_v_norm_mem_g': 'new_v', 'new_v_w_xq': 'new_v', 'new_v_w_xk': 'new_v', 'new_v_w_xv': 'new_v', 'new_v_w_xo': 'new_v', 'new_v_norm_ffn_g': 'new_v', 'new_v_w_ff1': 'new_v', 'new_v_w_ff2': 'new_v', 'new_v_final_norm_g': 'new_v'}


def _forward(args):
    return _fwd_reference(*[args[k] for k in FWD_PARAMS])


def _output_shape():
    def fwd():
        inp = _fwd_setup_inputs(0)
        return _fwd_reference(*[inp[k] for k in FWD_PARAMS])
    out = _jax.eval_shape(fwd)
    return out.shape, out.dtype

N_MICROBATCH = 1
ADAM_LR = 0.001
ADAM_B1 = 0.9
ADAM_B2 = 0.999
ADAM_EPS = 1e-08
ADAM_WD = 0.01
ADAM_STEP = 10
PER_EXAMPLE_BATCH_AXIS = {'x': 0, 'mem': 0, 'loss_target': 0}
SHARED_INPUTS = []
_WEIGHT_DTYPES = {'norm_mix_g': _jnp.float32, 'w_in': _jnp.float32, 'lb_logits': _jnp.float32, 'hgrn_norm_g': _jnp.float32, 'w_pool': _jnp.float32, 'pool_scale': _jnp.float32, 'w_out': _jnp.float32, 'norm_x_g': _jnp.float32, 'norm_mem_g': _jnp.float32, 'w_xq': _jnp.float32, 'w_xk': _jnp.float32, 'w_xv': _jnp.float32, 'w_xo': _jnp.float32, 'norm_ffn_g': _jnp.float32, 'w_ff1': _jnp.float32, 'w_ff2': _jnp.float32, 'final_norm_g': _jnp.float32}
MOMENT_SCALE = {'norm_mix_g': 1.353683e-01, 'w_in': 8.434803e-02, 'lb_logits': 7.981345e-03, 'hgrn_norm_g': 9.122659e-02, 'w_pool': 1.352918e-01, 'pool_scale': 1.334384e-01, 'w_out': 1.153276e-01, 'norm_x_g': 1.628051e-02, 'norm_mem_g': 2.412716e-02, 'w_xq': 1.594867e-02, 'w_xk': 1.600230e-02, 'w_xv': 1.643548e-02, 'w_xo': 1.655193e-02, 'norm_ffn_g': 1.437071e-01, 'w_ff1': 7.193671e-02, 'w_ff2': 1.429890e-01, 'final_norm_g': 3.223504e+01}


def _to_microbatches(a, axis):
    t = _jnp.moveaxis(a, axis, 0)
    t = t.reshape((N_MICROBATCH, t.shape[0] // N_MICROBATCH) + t.shape[1:])
    return _jnp.moveaxis(t, 1, axis + 1)


def setup_inputs(seed: int = 0) -> dict:
    inp = _fwd_setup_inputs(seed)
    key = _jax.random.fold_in(_jax.random.key(seed), 7919)
    shape, _ = _output_shape()
    out = dict(inp)
    out["loss_target"] = _jax.random.normal(_jax.random.fold_in(key, 0), shape, _jnp.float32)
    for i, name in enumerate(TWIN_WEIGHTS):
        w = inp[name].astype(_jnp.float32)
        if MOMENT_SCALE is None:
            s = _jnp.sqrt(_jnp.mean(_jnp.square(w)) + 1e-30)
        else:
            s = MOMENT_SCALE[name]
        km, kv = _jax.random.split(_jax.random.fold_in(key, i + 1))
        out[name] = w
        out["m_" + name] = s * _jax.random.normal(km, w.shape, _jnp.float32)
        out["v_" + name] = (s * s) * _jax.random.uniform(kv, w.shape, _jnp.float32, 0.5, 1.5)
    if N_MICROBATCH > 1:
        for name, axis in PER_EXAMPLE_BATCH_AXIS.items():
            out[name] = _to_microbatches(out[name], axis)
    return {'x': out['x'], 'mem': out['mem'], 'norm_mix_g': out['norm_mix_g'], 'w_in': out['w_in'], 'lb_logits': out['lb_logits'], 'hgrn_norm_g': out['hgrn_norm_g'], 'w_pool': out['w_pool'], 'pool_scale': out['pool_scale'], 'w_out': out['w_out'], 'norm_x_g': out['norm_x_g'], 'norm_mem_g': out['norm_mem_g'], 'w_xq': out['w_xq'], 'w_xk': out['w_xk'], 'w_xv': out['w_xv'], 'w_xo': out['w_xo'], 'norm_ffn_g': out['norm_ffn_g'], 'w_ff1': out['w_ff1'], 'w_ff2': out['w_ff2'], 'final_norm_g': out['final_norm_g'], 'loss_target': out['loss_target'], 'm_norm_mix_g': out['m_norm_mix_g'], 'm_w_in': out['m_w_in'], 'm_lb_logits': out['m_lb_logits'], 'm_hgrn_norm_g': out['m_hgrn_norm_g'], 'm_w_pool': out['m_w_pool'], 'm_pool_scale': out['m_pool_scale'], 'm_w_out': out['m_w_out'], 'm_norm_x_g': out['m_norm_x_g'], 'm_norm_mem_g': out['m_norm_mem_g'], 'm_w_xq': out['m_w_xq'], 'm_w_xk': out['m_w_xk'], 'm_w_xv': out['m_w_xv'], 'm_w_xo': out['m_w_xo'], 'm_norm_ffn_g': out['m_norm_ffn_g'], 'm_w_ff1': out['m_w_ff1'], 'm_w_ff2': out['m_w_ff2'], 'm_final_norm_g': out['m_final_norm_g'], 'v_norm_mix_g': out['v_norm_mix_g'], 'v_w_in': out['v_w_in'], 'v_lb_logits': out['v_lb_logits'], 'v_hgrn_norm_g': out['v_hgrn_norm_g'], 'v_w_pool': out['v_w_pool'], 'v_pool_scale': out['v_pool_scale'], 'v_w_out': out['v_w_out'], 'v_norm_x_g': out['v_norm_x_g'], 'v_norm_mem_g': out['v_norm_mem_g'], 'v_w_xq': out['v_w_xq'], 'v_w_xk': out['v_w_xk'], 'v_w_xv': out['v_w_xv'], 'v_w_xo': out['v_w_xo'], 'v_norm_ffn_g': out['v_norm_ffn_g'], 'v_w_ff1': out['v_w_ff1'], 'v_w_ff2': out['v_w_ff2'], 'v_final_norm_g': out['v_final_norm_g']}


def _loss(weights, diff, rest, loss_target):
    with _jax.named_scope("forward"):
        args = {**rest, TWIN_DIFF_INPUT: diff, **{k: w.astype(_WEIGHT_DTYPES[k]) for k, w in weights.items()}}
        y = _forward(args)
    with _jax.named_scope("loss_head"):
        err = _jnp.square(y.astype(_jnp.float32) - loss_target)
        return 0.5 * _jnp.sum(_jnp.mean(err, axis=-1)) if err.ndim else 0.5 * err


def _adamw(w, g, m, v):
    m = ADAM_B1 * m + (1.0 - ADAM_B1) * g
    v = ADAM_B2 * v + (1.0 - ADAM_B2) * _jnp.square(g)
    m_hat = m / (1.0 - ADAM_B1 ** ADAM_STEP)
    v_hat = v / (1.0 - ADAM_B2 ** ADAM_STEP)
    delta = -ADAM_LR * (m_hat / (_jnp.sqrt(v_hat) + ADAM_EPS) + ADAM_WD * w)
    return delta, m, v


def reference(x, mem, norm_mix_g, w_in, lb_logits, hgrn_norm_g, w_pool, pool_scale, w_out, norm_x_g, norm_mem_g, w_xq, w_xk, w_xv, w_xo, norm_ffn_g, w_ff1, w_ff2, final_norm_g, loss_target, m_norm_mix_g, m_w_in, m_lb_logits, m_hgrn_norm_g, m_w_pool, m_pool_scale, m_w_out, m_norm_x_g, m_norm_mem_g, m_w_xq, m_w_xk, m_w_xv, m_w_xo, m_norm_ffn_g, m_w_ff1, m_w_ff2, m_final_norm_g, v_norm_mix_g, v_w_in, v_lb_logits, v_hgrn_norm_g, v_w_pool, v_pool_scale, v_w_out, v_norm_x_g, v_norm_mem_g, v_w_xq, v_w_xk, v_w_xv, v_w_xo, v_norm_ffn_g, v_w_ff1, v_w_ff2, v_final_norm_g):
    given = dict(x=x, mem=mem, norm_mix_g=norm_mix_g, w_in=w_in, lb_logits=lb_logits, hgrn_norm_g=hgrn_norm_g, w_pool=w_pool, pool_scale=pool_scale, w_out=w_out, norm_x_g=norm_x_g, norm_mem_g=norm_mem_g, w_xq=w_xq, w_xk=w_xk, w_xv=w_xv, w_xo=w_xo, norm_ffn_g=norm_ffn_g, w_ff1=w_ff1, w_ff2=w_ff2, final_norm_g=final_norm_g, loss_target=loss_target, m_norm_mix_g=m_norm_mix_g, m_w_in=m_w_in, m_lb_logits=m_lb_logits, m_hgrn_norm_g=m_hgrn_norm_g, m_w_pool=m_w_pool, m_pool_scale=m_pool_scale, m_w_out=m_w_out, m_norm_x_g=m_norm_x_g, m_norm_mem_g=m_norm_mem_g, m_w_xq=m_w_xq, m_w_xk=m_w_xk, m_w_xv=m_w_xv, m_w_xo=m_w_xo, m_norm_ffn_g=m_norm_ffn_g, m_w_ff1=m_w_ff1, m_w_ff2=m_w_ff2, m_final_norm_g=m_final_norm_g, v_norm_mix_g=v_norm_mix_g, v_w_in=v_w_in, v_lb_logits=v_lb_logits, v_hgrn_norm_g=v_hgrn_norm_g, v_w_pool=v_w_pool, v_pool_scale=v_pool_scale, v_w_out=v_w_out, v_norm_x_g=v_norm_x_g, v_norm_mem_g=v_norm_mem_g, v_w_xq=v_w_xq, v_w_xk=v_w_xk, v_w_xv=v_w_xv, v_w_xo=v_w_xo, v_norm_ffn_g=v_norm_ffn_g, v_w_ff1=v_w_ff1, v_w_ff2=v_w_ff2, v_final_norm_g=v_final_norm_g)
    weights = {n: given[n] for n in TWIN_WEIGHTS}
    shared = {n: given[n] for n in SHARED_INPUTS}
    per_example = {n: given[n] for n in ['x', 'mem']}
    grad_fn = _jax.value_and_grad(_loss, argnums=(0, 1))

    def one_microbatch(ex, loss_target):
        ex = dict(ex)
        diff = ex.pop(TWIN_DIFF_INPUT)
        return grad_fn(weights, diff, {**shared, **ex}, loss_target)

    if N_MICROBATCH == 1:
        loss, (grad_w, grad_x) = one_microbatch(per_example, given["loss_target"])
    else:
        def body(carry, xs):
            loss_sum, grad_sum = carry
            l_k, (gw_k, gx_k) = one_microbatch(xs[0], xs[1])
            with _jax.named_scope("update"):
                return (loss_sum + l_k, _jax.tree.map(_jnp.add, grad_sum, gw_k)), gx_k

        init = (_jnp.zeros((), _jnp.float32), _jax.tree.map(_jnp.zeros_like, weights))
        (loss, grad_w), grad_x = _jax.lax.scan(body, init, (per_example, given["loss_target"]))
    with _jax.named_scope("update"):
        delta_w, new_m, new_v = {}, {}, {}
        for n in TWIN_WEIGHTS:
            delta_w[n], new_m[n], new_v[n] = _adamw(weights[n], grad_w[n], given["m_" + n], given["v_" + n])
    return (loss, grad_x, *[grad_w[n] for n in TWIN_WEIGHTS], *[delta_w[n] for n in TWIN_WEIGHTS],
            *[new_m[n] for n in TWIN_WEIGHTS], *[new_v[n] for n in TWIN_WEIGHTS])
```

```python
import functools

import jax
import jax.numpy as jnp
from jax import lax
from jax.experimental import pallas as pl
from jax.experimental.pallas import tpu as pltpu

F32 = jnp.float32
BF16 = jnp.bfloat16

D_MODEL = 1024
N_DEV = 8
HEADS = 4
HD = 128
HW = HEADS * HD
IN_WIDTH = 5 * HW
XHD = 256
MEM_LEN = 256
D_FF = 4096
FF_BLK = D_FF // N_DEV
POOL_WINDOWS = (2, 4, 8, 16)
POOL_HALO = 16
CHUNK = 64
SUB = 16
N_SUB = CHUNK // SUB
EXP_CAP = 80.0
EPS = 1e-6
ROW_TILE = 512
V7X_VMEM_LIMIT = 56 * 1024 * 1024

ADAM_LR = 0.001
ADAM_B1 = 0.9
ADAM_B2 = 0.999
ADAM_EPS = 1e-08
ADAM_WD = 0.01
ADAM_STEP = 10

MESH_AXES = ("x", "y", "c")
MESH_ID = pl.DeviceIdType.MESH

PACK_ROWS = (("w_in", 320), ("w_out", 128), ("w_xq", 128), ("w_xk", 128), ("w_xv", 128), ("w_xo", 128),
             ("w_ff1", 512), ("w_ff2", 512))
SHARD_2D = {"w_in": (1024, 320), "w_out": (128, 1024), "w_xq": (128, 1024), "w_xk": (128, 1024), "w_xv": (128, 1024),
            "w_xo": (1024, 128), "w_ff1": (1024, 512), "w_ff2": (512, 1024)}
SMALL_ROWS = 128


def _params(sem=None, vmem=V7X_VMEM_LIMIT):
    return pltpu.CompilerParams(dimension_semantics=sem, vmem_limit_bytes=vmem)


def _mm(a, b):
    return lax.dot_general(a.astype(BF16), b.astype(BF16), (((1,), (0,)), ((), ())), preferred_element_type=F32)


def _mm_nt(a, b):
    return lax.dot_general(a.astype(BF16), b.astype(BF16), (((1,), (1,)), ((), ())), preferred_element_type=F32)


def _mm_tn(a, b):
    return lax.dot_general(a.astype(BF16), b.astype(BF16), (((0,), (0,)), ((), ())), preferred_element_type=F32)


def _sigmoid(x):
    return 1.0 / (1.0 + jnp.exp(-x))


def _rms(x):
    return lax.rsqrt(jnp.mean(x * x, axis=-1, keepdims=True) + EPS)


def _rms_bwd(x, g, dh):
    r = _rms(x)
    n = x * r
    dn = dh * g
    dx = r * (dn - n * jnp.mean(dn * n, axis=-1, keepdims=True))
    return dx, jnp.sum(dh * n, axis=0, keepdims=True)


def _tri_dot(tri, x, passes):
    acc = None
    rest = x
    for _ in range(passes):
        piece = rest.astype(BF16)
        part = lax.dot_general(tri, piece, (((1,), (0,)), ((), ())), preferred_element_type=F32)
        acc = part if acc is None else acc + part
        rest = rest - piece.astype(F32)
    return acc


def _full(shape):
    return pl.BlockSpec(shape, lambda *_: (0,) * len(shape))


VMEM_WHOLE = pl.BlockSpec(memory_space=pltpu.VMEM)
ANY_SPACE = pl.BlockSpec(memory_space=pl.ANY)


def _mesh_pos():
    return lax.axis_index("x"), lax.axis_index("y"), lax.axis_index("c")


def _flat(px, py, pc):
    return 4 * px + 2 * py + pc


def _all_gather_weights(shard):
    rows = shard.shape[0]
    step = 64

    def body(x_ref, out_ref, buf, send_sems, recv_sems, local_sem):
        x, y, c = _mesh_pos()
        me, sibling = (x, y, c), (x, y, 1 - c)
        chips = [(1 - x, y), (x, 1 - y), (1 - x, 1 - y)]

        def block(px, py, pc):
            return out_ref.at[_flat(px, py, pc)]

        def copy(k, blk, to, src=None):
            return pltpu.make_async_remote_copy(
                src_ref=block(*blk) if src is None else src, dst_ref=block(*blk),
                send_sem=send_sems.at[k], recv_sem=recv_sems.at[k], device_id=to, device_id_type=MESH_ID)

        def cast(i, carry):
            r0 = pl.multiple_of(i * step, step)
            buf[pl.ds(r0, step), :] = x_ref[pl.ds(r0, step), :].astype(BF16)
            return carry
        lax.fori_loop(0, rows // step, cast, 0)

        mine = pltpu.make_async_copy(buf, block(*me), local_sem)
        mine.start()
        first = [copy(0, me, sibling, src=buf)]
        first += [copy(1 + j, me, (*chip, c), src=buf) for j, chip in enumerate(chips)]
        for cp in first:
            cp.start()
        passed = [copy(4 + j, (*chip, c), sibling) for j, chip in enumerate(chips)]
        for j, chip in enumerate(chips):
            copy(1 + j, (*chip, c), me).wait_recv()
            passed[j].start()
        copy(0, sibling, me).wait_recv()
        for j, chip in enumerate(chips):
            copy(4 + j, (*chip, 1 - c), me).wait_recv()
        for cp in first + passed:
            cp.wait_send()
        mine.wait()

    return pl.pallas_call(
        body, name="all_gather_weights",
        out_shape=jax.ShapeDtypeStruct((N_DEV, rows, D_MODEL), BF16),
        in_specs=[VMEM_WHOLE], out_specs=ANY_SPACE,
        scratch_shapes=[pltpu.VMEM((rows, D_MODEL), BF16), pltpu.SemaphoreType.DMA((7,)),
                        pltpu.SemaphoreType.DMA((7,)), pltpu.SemaphoreType.DMA],
        compiler_params=_params(),
    )(shard)


def _exchange_grads(gpack, small):
    rows, srows = gpack.shape[1], small.shape[0]

    def body(g_ref, s_ref, recv_ref, srecv_ref, send_sems, recv_sems, local_sems):
        x, y, c = _mesh_pos()
        me = _flat(x, y, c)
        copies = []
        for k in range(1, N_DEV):
            px = 1 - x if k & 4 else x
            py = 1 - y if k & 2 else y
            pc = 1 - c if k & 1 else c
            peer = (px, py, pc)
            copies.append(pltpu.make_async_remote_copy(
                src_ref=g_ref.at[_flat(*peer)], dst_ref=recv_ref.at[me],
                send_sem=send_sems.at[k - 1], recv_sem=recv_sems.at[k - 1], device_id=peer, device_id_type=MESH_ID))
            copies.append(pltpu.make_async_remote_copy(
                src_ref=s_ref, dst_ref=srecv_ref.at[me],
                send_sem=send_sems.at[6 + k], recv_sem=recv_sems.at[6 + k], device_id=peer, device_id_type=MESH_ID))
        own = [pltpu.make_async_copy(g_ref.at[me], recv_ref.at[me], local_sems.at[0]),
               pltpu.make_async_copy(s_ref, srecv_ref.at[me], local_sems.at[1])]
        for cp in own + copies:
            cp.start()
        for cp in copies:
            cp.wait_recv()
        for cp in copies:
            cp.wait_send()
        for cp in own:
            cp.wait()

    return pl.pallas_call(
        body, name="exchange_grads",
        out_shape=[jax.ShapeDtypeStruct((N_DEV, rows, D_MODEL), BF16),
                   jax.ShapeDtypeStruct((N_DEV, srows, D_MODEL), F32)],
        in_specs=[ANY_SPACE, ANY_SPACE], out_specs=[ANY_SPACE, ANY_SPACE],
        scratch_shapes=[pltpu.SemaphoreType.DMA((14,)), pltpu.SemaphoreType.DMA((14,)), pltpu.SemaphoreType.DMA((2,))],
        compiler_params=_params(),
    )(gpack, small)


def _sum_sources(recv, name, tile):
    _, rows, cols = recv.shape

    def body(r_ref, o_ref):
        acc = r_ref[0].astype(F32)
        for d in range(1, N_DEV):
            acc = acc + r_ref[d].astype(F32)
        o_ref[...] = acc

    return pl.pallas_call(
        body, name=name, grid=(rows // tile,),
        out_shape=jax.ShapeDtypeStruct((rows, cols), F32),
        in_specs=[pl.BlockSpec((N_DEV, tile, cols), lambda i: (0, i, 0))],
        out_specs=pl.BlockSpec((tile, cols), lambda i: (i, 0)),
        compiler_params=_params(("parallel",)),
    )(recv)


def _in_proj(x, g, w):
    s = x.shape[0]
    tm = min(ROW_TILE, s)

    def body(x_ref, g_ref, w_ref, z_ref, h_ref):
        xv = x_ref[...]
        h = (xv * _rms(xv) * g_ref[...]).astype(BF16)
        h_ref[...] = h
        z_ref[...] = _mm(h, w_ref[...])

    return pl.pallas_call(
        body, name="in_proj", grid=(s // tm,),
        out_shape=[jax.ShapeDtypeStruct((s, IN_WIDTH), F32), jax.ShapeDtypeStruct((s, D_MODEL), BF16)],
        in_specs=[pl.BlockSpec((tm, D_MODEL), lambda i: (i, 0)), _full((1, D_MODEL)), VMEM_WHOLE],
        out_specs=[pl.BlockSpec((tm, IN_WIDTH), lambda i: (i, 0)), pl.BlockSpec((tm, D_MODEL), lambda i: (i, 0))],
        compiler_params=_params(("parallel",)),
    )(x, g, w)


def _chunk_masks():
    row = lax.broadcasted_iota(jnp.int32, (CHUNK, CHUNK), 0)
    col = lax.broadcasted_iota(jnp.int32, (CHUNK, CHUNK), 1)
    return row, col


def _ones_where(mask):
    return jnp.where(mask, 1.0, 0.0).astype(BF16)


def _hgrn_gates(zq, zf, lb):
    sq = _sigmoid(zq)
    sig = _sigmoid(zf)
    f = lb + (1.0 - lb) * sig
    return zq * sq, sq, sig, f


def _hgrn_intra_factors(b_scr, bh, qh, kh, sl):
    trow = lax.broadcasted_iota(jnp.int32, (CHUNK, HD), 0)
    eq, ek = [], []
    for j in range(N_SUB):
        if j == 0:
            base = jnp.zeros((1, HD), F32)
        else:
            base = b_scr[SUB * j - 1:SUB * j, sl]
        in_j = (trow >= SUB * j) & (trow < SUB * (j + 1))
        eq.append(jnp.where(in_j, jnp.exp(bh - base), 0.0))
        ek.append(jnp.where(trow < SUB * (j + 1), jnp.exp(jnp.minimum(base - bh, EXP_CAP)), 0.0))
    eqcat = jnp.concatenate(eq, axis=1)
    ekcat = jnp.concatenate(ek, axis=1)
    qcat = jnp.concatenate([qh] * N_SUB, axis=1) * eqcat
    kecat = jnp.concatenate([kh] * N_SUB, axis=1) * ekcat
    return qcat, kecat, eqcat, ekcat


def _sum_lane_blocks(a):
    out = a[:, 0:HD]
    for j in range(1, N_SUB):
        out = out + a[:, HD * j:HD * (j + 1)]
    return out


def _hgrn_fwd(z, lb_logits, gn):
    s = z.shape[0]
    n_chunks = s // CHUNK

    def body(zq_ref, zf_ref, zi_ref, zg_ref, lbl_ref, gn_ref, oa_ref, o_ref, st_ref, state, b_scr):
        @pl.when(pl.program_id(0) == 0)
        def _():
            state[...] = jnp.zeros_like(state)

        st_ref[0] = state[...]
        lb = _sigmoid(lbl_ref[0:1, :] - lbl_ref[1:2, :])
        q, _, _, f = _hgrn_gates(zq_ref[...], zf_ref[...], lb)
        kk = 1.0 - f
        row, col = _chunk_masks()
        causal = col <= row
        b_scr[...] = _tri_dot(_ones_where(causal), jnp.log(f), 3)
        for h in range(HEADS):
            sl = slice(HD * h, HD * (h + 1))
            bh = b_scr[:, sl]
            qh, kh, vh = q[:, sl], kk[:, sl], zi_ref[:, sl]
            st = state[h]
            b_last = b_scr[CHUNK - 1:CHUNK, sl]
            qcat, kecat, _, _ = _hgrn_intra_factors(b_scr, bh, qh, kh, sl)
            a = jnp.where(causal, _mm_nt(qcat, kecat), 0.0)
            o = _mm(a, vh) + _mm_nt(qh * jnp.exp(bh), st)
            state[h] = st * jnp.exp(b_last) + _mm_tn(vh, kh * jnp.exp(b_last - bh))
            o_ref[:, sl] = o
            zg = zg_ref[:, sl]
            oa_ref[:, sl] = (o * _rms(o) * gn_ref[:, sl] * zg * _sigmoid(zg)).astype(BF16)

    zspec = lambda cb: pl.BlockSpec((CHUNK, HW), lambda i, cb=cb: (i, cb))
    return pl.pallas_call(
        body, name="hgrn_fwd", grid=(n_chunks,),
        out_shape=[jax.ShapeDtypeStruct((s, HW), BF16), jax.ShapeDtypeStruct((s, HW), F32),
                   jax.ShapeDtypeStruct((n_chunks, HEADS, HD, HD), F32)],
        in_specs=[zspec(0), zspec(1), zspec(2), zspec(3), _full((2, HW)), _full((1, HW))],
        out_specs=[pl.BlockSpec((CHUNK, HW), lambda i: (i, 0)), pl.BlockSpec((CHUNK, HW), lambda i: (i, 0)),
                   pl.BlockSpec((1, HEADS, HD, HD), lambda i: (i, 0, 0, 0))],
        scratch_shapes=[pltpu.VMEM((HEADS, HD, HD), F32), pltpu.VMEM((CHUNK, HW), F32)],
        compiler_params=_params(("arbitrary",)),
    )(z, z, z, z, lb_logits, gn)


def _pool_counts(tile_idx, tm):
    t = tile_idx * tm + lax.broadcasted_iota(jnp.int32, (tm, 1), 0)
    return [1.0 / jnp.minimum(t + 1, w).astype(F32) for w in POOL_WINDOWS]


def _pool_fwd(z, w_pool, scale):
    s = z.shape[0]
    tm = min(ROW_TILE, s)

    def body(p_ref, w_ref, sc_ref, ob_ref, pooled_ref, ext):
        i = pl.program_id(0)

        @pl.when(i == 0)
        def _():
            ext[0:POOL_HALO, :] = jnp.zeros((POOL_HALO, HW), F32)

        @pl.when(i > 0)
        def _():
            ext[0:POOL_HALO, :] = ext[tm:tm + POOL_HALO, :]

        ext[POOL_HALO:POOL_HALO + tm, :] = p_ref[...]
        inv = _pool_counts(i, tm)
        for g, w in enumerate(POOL_WINDOWS):
            sl = slice(HD * g, HD * (g + 1))
            p = ext[POOL_HALO:POOL_HALO + tm, sl]
            win = p
            for d in range(1, w):
                win = win + ext[POOL_HALO - d:POOL_HALO - d + tm, sl]
            pooled = (win * inv[g] - p).astype(BF16)
            pooled_ref[:, sl] = pooled
            ob_ref[:, sl] = (_mm(pooled, w_ref[g]) * sc_ref[:, sl]).astype(BF16)

    return pl.pallas_call(
        body, name="pool_fwd", grid=(s // tm,),
        out_shape=[jax.ShapeDtypeStruct((s, HW), BF16), jax.ShapeDtypeStruct((s, HW), BF16)],
        in_specs=[pl.BlockSpec((tm, HW), lambda i: (i, 4)), _full((HEADS, HD, HD)), _full((1, HW))],
        out_specs=[pl.BlockSpec((tm, HW), lambda i: (i, 0)), pl.BlockSpec((tm, HW), lambda i: (i, 0))],
        scratch_shapes=[pltpu.VMEM((tm + POOL_HALO, HW), F32)],
        compiler_params=_params(("arbitrary",)),
    )(z, w_pool, scale)


def _out_proj(x, oa, ob, w_out):
    s = x.shape[0]
    tm = min(ROW_TILE, s)

    def body(x_ref, oa_ref, ob_ref, w_ref, o_ref):
        o_ref[...] = x_ref[...] + _mm(oa_ref[...], w_ref[0:HW, :]) + _mm(ob_ref[...], w_ref[HW:2 * HW, :])

    return pl.pallas_call(
        body, name="out_proj", grid=(s // tm,),
        out_shape=jax.ShapeDtypeStruct((s, D_MODEL), F32),
        in_specs=[pl.BlockSpec((tm, D_MODEL), lambda i: (i, 0)), pl.BlockSpec((tm, HW), lambda i: (i, 0)),
                  pl.BlockSpec((tm, HW), lambda i: (i, 0)), VMEM_WHOLE],
        out_specs=pl.BlockSpec((tm, D_MODEL), lambda i: (i, 0)),
        compiler_params=_params(("parallel",)),
    )(x, oa, ob, w_out)


def _mem_kv(mem, g, wk, wv):
    def body(m_ref, g_ref, wk_ref, wv_ref, hm_ref, k_ref, v_ref):
        m = m_ref[...]
        hm = (m * _rms(m) * g_ref[...]).astype(BF16)
        hm_ref[...] = hm
        k_ref[...] = _mm(hm, wk_ref[...]).astype(BF16)
        v_ref[...] = _mm(hm, wv_ref[...]).astype(BF16)

    shp = jax.ShapeDtypeStruct((MEM_LEN, D_MODEL), BF16)
    return pl.pallas_call(
        body, name="mem_kv", out_shape=[shp, shp, shp],
        in_specs=[VMEM_WHOLE] * 4, out_specs=[VMEM_WHOLE] * 3,
        compiler_params=_params(),
    )(mem, g, wk, wv)


def _softmax_rows(sc):
    e = jnp.exp(sc - jnp.max(sc, axis=-1, keepdims=True))
    return e / jnp.sum(e, axis=-1, keepdims=True)


def _xattn_fwd(x, g, wq, xk, xv, wo):
    s = x.shape[0]
    tm = min(ROW_TILE, s)
    scale = XHD ** -0.5

    def body(x_ref, g_ref, wq_ref, k_ref, v_ref, wo_ref, o_ref, hq_ref, q_ref, att_ref):
        xv_ = x_ref[...]
        hq = (xv_ * _rms(xv_) * g_ref[...]).astype(BF16)
        hq_ref[...] = hq
        q_ref[...] = (_mm(hq, wq_ref[...]) * scale).astype(BF16)
        for h in range(HEADS):
            sl = slice(XHD * h, XHD * (h + 1))
            p = _softmax_rows(_mm_nt(q_ref[:, sl], k_ref[:, sl]))
            att_ref[:, sl] = _mm(p, v_ref[:, sl]).astype(BF16)
        o_ref[...] = xv_ + _mm(att_ref[...], wo_ref[...])

    row_f32 = pl.BlockSpec((tm, D_MODEL), lambda i: (i, 0))
    bshape = jax.ShapeDtypeStruct((s, D_MODEL), BF16)
    return pl.pallas_call(
        body, name="xattn_fwd", grid=(s // tm,),
        out_shape=[jax.ShapeDtypeStruct((s, D_MODEL), F32), bshape, bshape, bshape],
        in_specs=[row_f32, _full((1, D_MODEL)), VMEM_WHOLE, VMEM_WHOLE, VMEM_WHOLE, VMEM_WHOLE],
        out_specs=[row_f32] * 4,
        compiler_params=_params(("parallel",)),
    )(x, g, wq, xk, xv, wo)


def _mlp_fwd_loss(x, g, w1, w2, gf, target):
    s = x.shape[0]
    tm = min(ROW_TILE, s)

    def body(x_ref, g_ref, w1_ref, w2_ref, gf_ref, t_ref, dx_ref, u_ref, hf_ref, loss_ref, dgf_ref):
        @pl.when(pl.program_id(0) == 0)
        def _():
            loss_ref[...] = jnp.zeros_like(loss_ref)
            dgf_ref[...] = jnp.zeros_like(dgf_ref)

        xv = x_ref[...]
        hf = (xv * _rms(xv) * g_ref[...]).astype(BF16)
        hf_ref[...] = hf
        acc = xv
        for j in range(N_DEV):
            a = jnp.maximum(_mm(hf, w1_ref[j]), 0.0)
            u = (a * a).astype(BF16)
            u_ref[:, FF_BLK * j:FF_BLK * (j + 1)] = u
            acc = acc + _mm(u, w2_ref[j])
        gfv = gf_ref[...]
        r = _rms(acc)
        n = acc * r
        err = n * gfv - t_ref[...]
        loss_ref[...] += jnp.sum(jnp.mean(err * err, axis=-1, keepdims=True), axis=0, keepdims=True) * 0.5
        dy = err * (1.0 / D_MODEL)
        dgf_ref[...] += jnp.sum(dy * n, axis=0, keepdims=True)
        dn = dy * gfv
        dx_ref[...] = r * (dn - n * jnp.mean(dn * n, axis=-1, keepdims=True))

    row_f32 = pl.BlockSpec((tm, D_MODEL), lambda i: (i, 0))
    return pl.pallas_call(
        body, name="mlp_fwd_loss", grid=(s // tm,),
        out_shape=[jax.ShapeDtypeStruct((s, D_MODEL), F32), jax.ShapeDtypeStruct((s, D_FF), BF16),
                   jax.ShapeDtypeStruct((s, D_MODEL), BF16), jax.ShapeDtypeStruct((8, 128), F32),
                   jax.ShapeDtypeStruct((1, D_MODEL), F32)],
        in_specs=[row_f32, _full((1, D_MODEL)), VMEM_WHOLE, VMEM_WHOLE, _full((1, D_MODEL)), row_f32],
        out_specs=[row_f32, pl.BlockSpec((tm, D_FF), lambda i: (i, 0)), row_f32, _full((8, 128)), _full((1, D_MODEL))],
        compiler_params=_params(("arbitrary",)),
    )(x, g, w1, w2, gf, target)


def _mlp_bwd(dx3, u, x2, g, w1, w2):
    s = x2.shape[0]
    tm = min(ROW_TILE, s)

    def body(d_ref, u_ref, x_ref, g_ref, w1_ref, w2_ref, da_ref, dx_ref, dg_ref):
        @pl.when(pl.program_id(0) == 0)
        def _():
            dg_ref[...] = jnp.zeros_like(dg_ref)

        d = d_ref[...]
        d16 = d.astype(BF16)
        dhf = jnp.zeros((tm, D_MODEL), F32)
        for j in range(N_DEV):
            sl = slice(FF_BLK * j, FF_BLK * (j + 1))
            da = (_mm_nt(d16, w2_ref[j]) * (2.0 * jnp.sqrt(u_ref[:, sl].astype(F32)))).astype(BF16)
            da_ref[:, sl] = da
            dhf = dhf + _mm_nt(da, w1_ref[j])
        dx, dg = _rms_bwd(x_ref[...], g_ref[...], dhf)
        dx_ref[...] = d + dx
        dg_ref[...] += dg

    row_f32 = pl.BlockSpec((tm, D_MODEL), lambda i: (i, 0))
    return pl.pallas_call(
        body, name="mlp_bwd", grid=(s // tm,),
        out_shape=[jax.ShapeDtypeStruct((s, D_FF), BF16), jax.ShapeDtypeStruct((s, D_MODEL), F32),
                   jax.ShapeDtypeStruct((1, D_MODEL), F32)],
        in_specs=[row_f32, pl.BlockSpec((tm, D_FF), lambda i: (i, 0)), row_f32, _full((1, D_MODEL)),
                  VMEM_WHOLE, VMEM_WHOLE],
        out_specs=[pl.BlockSpec((tm, D_FF), lambda i: (i, 0)), row_f32, _full((1, D_MODEL))],
        compiler_params=_params(("arbitrary",)),
    )(dx3, u, x2, g, w1, w2)


def _wgrad(a, b, name):
    s, m = a.shape
    n = b.shape[1]
    tm = min(1024, m)
    tn = 1280 if n % 1280 == 0 else min(1024, n)
    ts = min(ROW_TILE, s)
    n_s = s // ts

    def body(a_ref, b_ref, o_ref, acc):
        k = pl.program_id(2)

        @pl.when(k == 0)
        def _():
            acc[...] = jnp.zeros_like(acc)

        acc[...] += _mm_tn(a_ref[...], b_ref[...])

        @pl.when(k == n_s - 1)
        def _():
            o_ref[...] = acc[...].astype(BF16)

    return pl.pallas_call(
        body, name=name, grid=(m // tm, n // tn, n_s),
        out_shape=jax.ShapeDtypeStruct((m, n), BF16),
        in_specs=[pl.BlockSpec((ts, tm), lambda i, j, k: (k, i)), pl.BlockSpec((ts, tn), lambda i, j, k: (k, j))],
        out_specs=pl.BlockSpec((tm, tn), lambda i, j, k: (i, j)),
        scratch_shapes=[pltpu.VMEM((tm, tn), F32)],
        compiler_params=_params(("parallel", "parallel", "arbitrary")),
    )(a, b)


def _xattn_bwd(dx2, x1, g, q, xk, xv, wq, wo):
    s = x1.shape[0]
    tm = min(ROW_TILE, s)
    scale = XHD ** -0.5

    def body(d_ref, x_ref, g_ref, q_ref, k_ref, v_ref, wq_ref, wo_ref, dx_ref, dq_ref, dk_ref, dv_ref, dg_ref, datt):
        @pl.when(pl.program_id(0) == 0)
        def _():
            dk_ref[...] = jnp.zeros_like(dk_ref)
            dv_ref[...] = jnp.zeros_like(dv_ref)
            dg_ref[...] = jnp.zeros_like(dg_ref)

        d = d_ref[...]
        datt[...] = _mm_nt(d, wo_ref[...]).astype(BF16)
        for h in range(HEADS):
            sl = slice(XHD * h, XHD * (h + 1))
            qh, kh, vh, dah = q_ref[:, sl], k_ref[:, sl], v_ref[:, sl], datt[:, sl]
            p = _softmax_rows(_mm_nt(qh, kh))
            dp = _mm_nt(dah, vh)
            ds = (p * (dp - jnp.sum(dp * p, axis=-1, keepdims=True))).astype(BF16)
            dq_ref[:, sl] = (_mm(ds, kh) * scale).astype(BF16)
            dk_ref[:, sl] += _mm_tn(ds, qh)
            dv_ref[:, sl] += _mm_tn(p, dah)
        dx, dg = _rms_bwd(x_ref[...], g_ref[...], _mm_nt(dq_ref[...], wq_ref[...]))
        dx_ref[...] = d + dx
        dg_ref[...] += dg

    row_f32 = pl.BlockSpec((tm, D_MODEL), lambda i: (i, 0))
    kv = jax.ShapeDtypeStruct((MEM_LEN, D_MODEL), F32)
    return pl.pallas_call(
        body, name="xattn_bwd", grid=(s // tm,),
        out_shape=[jax.ShapeDtypeStruct((s, D_MODEL), F32), jax.ShapeDtypeStruct((s, D_MODEL), BF16), kv, kv,
                   jax.ShapeDtypeStruct((1, D_MODEL), F32)],
        in_specs=[row_f32, row_f32, _full((1, D_MODEL)), row_f32, VMEM_WHOLE, VMEM_WHOLE, VMEM_WHOLE, VMEM_WHOLE],
        out_specs=[row_f32, row_f32, _full((MEM_LEN, D_MODEL)), _full((MEM_LEN, D_MODEL)), _full((1, D_MODEL))],
        scratch_shapes=[pltpu.VMEM((tm, D_MODEL), BF16)],
        compiler_params=_params(("arbitrary",)),
    )(dx2, x1, g, q, xk, xv, wq, wo)


def _mem_bwd(mem, g, hm, dxk, dxv, wk, wv):
    def body(m_ref, g_ref, hm_ref, dk_ref, dv_ref, wk_ref, wv_ref, dwk_ref, dwv_ref, dg_ref):
        dk, dv = dk_ref[...], dv_ref[...]
        hm_ = hm_ref[...]
        dwk_ref[...] = _mm_tn(hm_, dk).astype(BF16)
        dwv_ref[...] = _mm_tn(hm_, dv).astype(BF16)
        _, dg = _rms_bwd(m_ref[...], g_ref[...], _mm_nt(dk, wk_ref[...]) + _mm_nt(dv, wv_ref[...]))
        dg_ref[...] = dg

    wshape = jax.ShapeDtypeStruct((D_MODEL, D_MODEL), BF16)
    return pl.pallas_call(
        body, name="mem_bwd", out_shape=[wshape, wshape, jax.ShapeDtypeStruct((1, D_MODEL), F32)],
        in_specs=[VMEM_WHOLE] * 7, out_specs=[VMEM_WHOLE] * 3,
        compiler_params=_params(),
    )(mem, g, hm, dxk, dxv, wk, wv)


def _matmul_nt(a, w, name):
    s, k = a.shape
    n = w.shape[0]
    tm = min(ROW_TILE, s)

    def body(a_ref, w_ref, o_ref):
        o_ref[...] = _mm_nt(a_ref[...], w_ref[...])

    return pl.pallas_call(
        body, name=name, grid=(s // tm,),
        out_shape=jax.ShapeDtypeStruct((s, n), F32),
        in_specs=[pl.BlockSpec((tm, k), lambda i: (i, 0)), VMEM_WHOLE],
        out_specs=pl.BlockSpec((tm, n), lambda i: (i, 0)),
        compiler_params=_params(("parallel",)),
    )(a, w)


def _pool_bwd(dmix, pooled, w_pool, scale):
    s = dmix.shape[0]
    tm = min(ROW_TILE, s)
    n_t = s // tm

    def body(do_ref, pl_ref, w_ref, sc_ref, dz_ref, dw_ref, dsc_ref, ext):
        i = pl.program_id(0)
        tile = n_t - 1 - i

        @pl.when(i == 0)
        def _():
            dw_ref[...] = jnp.zeros_like(dw_ref)
            dsc_ref[...] = jnp.zeros_like(dsc_ref)
            ext[tm:tm + POOL_HALO, :] = jnp.zeros((POOL_HALO, HW), F32)

        @pl.when(i > 0)
        def _():
            ext[tm:tm + POOL_HALO, :] = ext[0:POOL_HALO, :]

        inv = _pool_counts(tile, tm)
        dpooled = []
        for g in range(HEADS):
            sl = slice(HD * g, HD * (g + 1))
            pooled_g = pl_ref[:, sl]
            do = do_ref[:, sl]
            dsc_ref[:, sl] += jnp.sum(_mm(pooled_g, w_ref[g]) * do, axis=0, keepdims=True)
            dy = (do * sc_ref[:, sl]).astype(BF16)
            dw_ref[g] += _mm_tn(pooled_g, dy)
            dpo = _mm_nt(dy, w_ref[g])
            dpooled.append(dpo)
            ext[0:tm, sl] = dpo * inv[g]
        for g, w in enumerate(POOL_WINDOWS):
            sl = slice(HD * g, HD * (g + 1))
            win = ext[0:tm, sl]
            for d in range(1, w):
                win = win + ext[d:d + tm, sl]
            dz_ref[:, sl] = win - dpooled[g]

    return pl.pallas_call(
        body, name="pool_bwd", grid=(n_t,),
        out_shape=[jax.ShapeDtypeStruct((s, IN_WIDTH), F32), jax.ShapeDtypeStruct((HEADS, HD, HD), F32),
                   jax.ShapeDtypeStruct((1, HW), F32)],
        in_specs=[pl.BlockSpec((tm, HW), lambda i: (n_t - 1 - i, 1)), pl.BlockSpec((tm, HW), lambda i: (n_t - 1 - i, 0)),
                  _full((HEADS, HD, HD)), _full((1, HW))],
        out_specs=[pl.BlockSpec((tm, HW), lambda i: (n_t - 1 - i, 4)), _full((HEADS, HD, HD)), _full((1, HW))],
        scratch_shapes=[pltpu.VMEM((tm + POOL_HALO, HW), F32)],
        compiler_params=_params(("arbitrary",)),
    )(dmix, pooled, w_pool, scale)


def _hgrn_bwd(z, o, dmix, states, lb_logits, gn, dz_in):
    s = z.shape[0]
    n_chunks = s // CHUNK

    def body(zq_ref, zf_ref, zi_ref, zg_ref, o_ref, do_ref, st_ref, lbl_ref, gn_ref, dzin_ref,
             dz_ref, dlb_ref, dgn_ref, dstate, b_scr, dlb_acc, dgn_acc):
        i = pl.program_id(0)

        @pl.when(i == 0)
        def _():
            dstate[...] = jnp.zeros_like(dstate)
            dlb_acc[...] = jnp.zeros_like(dlb_acc)
            dgn_acc[...] = jnp.zeros_like(dgn_acc)

        lb = _sigmoid(lbl_ref[0:1, :] - lbl_ref[1:2, :])
        zq = zq_ref[...]
        q, sq, sig, f = _hgrn_gates(zq, zf_ref[...], lb)
        kk = 1.0 - f
        row, col = _chunk_masks()
        causal = col <= row
        upper = _ones_where(col >= row)
        strict_lower = _ones_where(col < row)
        b_scr[...] = _tri_dot(_ones_where(causal), jnp.log(f), 3)
        for h in range(HEADS):
            sl = slice(HD * h, HD * (h + 1))
            oh = o_ref[:, sl]
            gnh = gn_ref[:, sl]
            zg = zg_ref[:, sl]
            sg = _sigmoid(zg)
            doa = do_ref[:, sl]
            don = doa * (zg * sg)
            d_o, dgn = _rms_bwd(oh, gnh, don)
            dgn_acc[:, sl] += dgn
            dz_ref[:, 3 * HW + HD * h:3 * HW + HD * (h + 1)] = doa * (oh * _rms(oh) * gnh) * (sg * (1.0 + zg * (1.0 - sg)))
            bh = b_scr[:, sl]
            qh, kh, vh = q[:, sl], kk[:, sl], zi_ref[:, sl]
            st0 = st_ref[0, h]
            ds1 = dstate[h]
            b_last = b_scr[CHUNK - 1:CHUNK, sl]
            lam = jnp.exp(bh)
            e_last = jnp.exp(b_last - bh)
            lam_last = jnp.exp(b_last)
            qcat, kecat, eqcat, ekcat = _hgrn_intra_factors(b_scr, bh, qh, kh, sl)
            a = jnp.where(causal, _mm_nt(qcat, kecat), 0.0)
            da = jnp.where(causal, _mm_nt(d_o, vh), 0.0)
            dz_ref[:, 2 * HW + HD * h:2 * HW + HD * (h + 1)] = _mm_tn(a, d_o) + _mm_nt(kh * e_last, ds1)
            q16, ke16 = qcat.astype(BF16), kecat.astype(BF16)
            gq = _mm(da, ke16)
            gk = _mm_tn(da, q16)
            dq_inter = lam * _mm(d_o, st0)
            dq = _sum_lane_blocks(eqcat * gq) + dq_inter
            dk_intra = _sum_lane_blocks(ekcat * gk)
            dk_state = _mm(vh, ds1) * e_last
            state_term = lam_last * jnp.sum(st0 * ds1, axis=0, keepdims=True)
            dstate[h] = ds1 * lam_last + _mm_tn(d_o, qh * lam)
            db_intra = _sum_lane_blocks(q16.astype(F32) * gq - ke16.astype(F32) * gk)
            dlf = (_tri_dot(upper, db_intra + qh * dq_inter, 2) + _tri_dot(strict_lower, kh * dk_state, 2)
                   + state_term)
            sigh = sig[:, sl]
            df = dlf / f[:, sl] - (dk_intra + dk_state)
            dlb_acc[:, sl] += jnp.sum(df * (1.0 - sigh), axis=0, keepdims=True)
            dz_ref[:, HW + HD * h:HW + HD * (h + 1)] = df * (1.0 - lb[:, sl]) * sigh * (1.0 - sigh)
            sqh = sq[:, sl]
            dz_ref[:, sl] = dq * (sqh * (1.0 + zq[:, sl] * (1.0 - sqh)))

        @pl.when(i == n_chunks - 1)
        def _():
            dl0 = dlb_acc[...] * lb * (1.0 - lb)
            dlb_ref[0:1, :] = dl0
            dlb_ref[1:2, :] = -dl0
            dgn_ref[...] = dgn_acc[...]

    rev = lambda i: n_chunks - 1 - i
    zspec = lambda cb: pl.BlockSpec((CHUNK, HW), lambda i, cb=cb: (rev(i), cb))
    return pl.pallas_call(
        body, name="hgrn_bwd", grid=(n_chunks,),
        out_shape=[jax.ShapeDtypeStruct((s, IN_WIDTH), F32), jax.ShapeDtypeStruct((2, HW), F32),
                   jax.ShapeDtypeStruct((1, HW), F32)],
        in_specs=[zspec(0), zspec(1), zspec(2), zspec(3), pl.BlockSpec((CHUNK, HW), lambda i: (rev(i), 0)),
                  pl.BlockSpec((CHUNK, HW), lambda i: (rev(i), 0)),
                  pl.BlockSpec((1, HEADS, HD, HD), lambda i: (rev(i), 0, 0, 0)), _full((2, HW)), _full((1, HW)), ANY_SPACE],
        out_specs=[pl.BlockSpec((CHUNK, 4 * HW), lambda i: (rev(i), 0)), _full((2, HW)), _full((1, HW))],
        scratch_shapes=[pltpu.VMEM((HEADS, HD, HD), F32), pltpu.VMEM((CHUNK, HW), F32), pltpu.VMEM((1, HW), F32),
                        pltpu.VMEM((1, HW), F32)],
        input_output_aliases={9: 0},
        compiler_params=_params(("arbitrary",)),
    )(z, z, z, z, o, dmix, states, lb_logits, gn, dz_in)


def _in_bwd(dz, w_in, x0, g, dx1):
    s = x0.shape[0]
    tm = min(ROW_TILE, s)

    def body(dz_ref, w_ref, x_ref, g_ref, d_ref, dx_ref, dg_ref):
        @pl.when(pl.program_id(0) == 0)
        def _():
            dg_ref[...] = jnp.zeros_like(dg_ref)

        dx, dg = _rms_bwd(x_ref[...], g_ref[...], _mm_nt(dz_ref[...], w_ref[...]))
        dx_ref[...] = d_ref[...] + dx
        dg_ref[...] += dg

    row_f32 = pl.BlockSpec((tm, D_MODEL), lambda i: (i, 0))
    return pl.pallas_call(
        body, name="in_bwd", grid=(s // tm,),
        out_shape=[jax.ShapeDtypeStruct((s, D_MODEL), F32), jax.ShapeDtypeStruct((1, D_MODEL), F32)],
        in_specs=[pl.BlockSpec((tm, IN_WIDTH), lambda i: (i, 0)), VMEM_WHOLE, row_f32, _full((1, D_MODEL)), row_f32],
        out_specs=[row_f32, _full((1, D_MODEL))],
        compiler_params=_params(("arbitrary",)),
    )(dz, w_in, x0, g, dx1)


def _adamw(g, w, m, v, name):
    rows, cols = g.shape
    tr = rows
    for cand in (256, 128, 64, 32, 16, 8):
        if rows % cand == 0:
            tr = cand
            break
    c1 = 1.0 - ADAM_B1 ** ADAM_STEP
    c2 = 1.0 - ADAM_B2 ** ADAM_STEP

    def body(g_ref, w_ref, m_ref, v_ref, d_ref, nm_ref, nv_ref):
        gv = g_ref[...]
        nm = ADAM_B1 * m_ref[...] + (1.0 - ADAM_B1) * gv
        nv = ADAM_B2 * v_ref[...] + (1.0 - ADAM_B2) * (gv * gv)
        nm_ref[...] = nm
        nv_ref[...] = nv
        d_ref[...] = -ADAM_LR * ((nm / c1) / (jnp.sqrt(nv / c2) + ADAM_EPS) + ADAM_WD * w_ref[...])

    spec = pl.BlockSpec((tr, cols), lambda i: (i, 0))
    shp = jax.ShapeDtypeStruct((rows, cols), F32)
    return pl.pallas_call(
        body, name=name, grid=(rows // tr,), out_shape=[shp, shp, shp],
        in_specs=[spec] * 4, out_specs=[spec] * 3,
        compiler_params=_params(("parallel",)),
    )(g, w, m, v)


def _cols_to_blocks(w, width):
    k = w.shape[0]
    return w.reshape(k, N_DEV, width).transpose(1, 0, 2).reshape(N_DEV, k * width // D_MODEL, D_MODEL)


def _blocks_to_cols(blocks, k, width):
    return blocks.reshape(N_DEV, k, width).transpose(1, 0, 2).reshape(k, N_DEV * width)


def _pad_rows(a, rows):
    a = a.reshape(64, D_MODEL) if a.size == 64 * D_MODEL else a.reshape(1, -1)
    return jnp.pad(a, ((0, rows - a.shape[0]), (0, D_MODEL - a.shape[1])))


def kernel(x, mem, norm_mix_g, w_in, lb_logits, hgrn_norm_g, w_pool, pool_scale, w_out, norm_x_g, norm_mem_g, w_xq, w_xk, w_xv, w_xo, norm_ffn_g, w_ff1, w_ff2, final_norm_g, loss_target, m_norm_mix_g, m_w_in, m_lb_logits, m_hgrn_norm_g, m_w_pool, m_pool_scale, m_w_out, m_norm_x_g, m_norm_mem_g, m_w_xq, m_w_xk, m_w_xv, m_w_xo, m_norm_ffn_g, m_w_ff1, m_w_ff2, m_final_norm_g, v_norm_mix_g, v_w_in, v_lb_logits, v_hgrn_norm_g, v_w_pool, v_pool_scale, v_w_out, v_norm_x_g, v_norm_mem_g, v_w_xq, v_w_xk, v_w_xv, v_w_xo, v_norm_ffn_g, v_w_ff1, v_w_ff2, v_final_norm_g):
    x0 = x[0]
    mem0 = mem[0]
    tgt = loss_target[0]
    gn = hgrn_norm_g.reshape(1, HW)
    gfin = final_norm_g.reshape(1, D_MODEL)
    wp = w_pool[0]

    big = {"w_in": w_in, "w_out": w_out, "w_xq": w_xq, "w_xk": w_xk, "w_xv": w_xv, "w_xo": w_xo,
           "w_ff1": w_ff1, "w_ff2": w_ff2}
    shard_pack = jnp.concatenate([big[n].reshape(r, D_MODEL) for n, r in PACK_ROWS], axis=0)
    gathered = _all_gather_weights(shard_pack)
    parts, r0 = {}, 0
    for n, r in PACK_ROWS:
        parts[n] = gathered[:, r0:r0 + r, :]
        r0 += r
    win_f = _blocks_to_cols(parts["w_in"], D_MODEL, IN_WIDTH // N_DEV)
    wout_f = parts["w_out"].reshape(D_MODEL, D_MODEL)
    wq_f = parts["w_xq"].reshape(D_MODEL, D_MODEL)
    wk_f = parts["w_xk"].reshape(D_MODEL, D_MODEL)
    wv_f = parts["w_xv"].reshape(D_MODEL, D_MODEL)
    wo_f = _blocks_to_cols(parts["w_xo"], D_MODEL, D_MODEL // N_DEV)
    w1_b = parts["w_ff1"].reshape(N_DEV, D_MODEL, FF_BLK)
    w2_b = parts["w_ff2"]

    z, h = _in_proj(x0, norm_mix_g, win_f)
    oa, o_pre, states = _hgrn_fwd(z, lb_logits, gn)
    ob, pooled = _pool_fwd(z, wp, pool_scale)
    x1 = _out_proj(x0, oa, ob, wout_f)
    hm, xk, xv = _mem_kv(mem0, norm_mem_g, wk_f, wv_f)
    x2, hq, xq, att = _xattn_fwd(x1, norm_x_g, wq_f, xk, xv, wo_f)
    dx3, u, hf, loss_part, d_gfin = _mlp_fwd_loss(x2, norm_ffn_g, w1_b, w2_b, gfin, tgt)
    loss = lax.psum(loss_part[0, 0], MESH_AXES)

    da, dx2, d_gffn = _mlp_bwd(dx3, u, x2, norm_ffn_g, w1_b, w2_b)
    dw2 = _wgrad(u, dx3, "wgrad_ff2")
    dw1 = _wgrad(hf, da, "wgrad_ff1")
    dx1, dxq, dxk, dxv, d_gx = _xattn_bwd(dx2, x1, norm_x_g, xq, xk, xv, wq_f, wo_f)
    dwo = _wgrad(att, dx2, "wgrad_xo")
    dwq = _wgrad(hq, dxq, "wgrad_xq")
    dwk, dwv, d_gmem = _mem_bwd(mem0, norm_mem_g, hm, dxk, dxv, wk_f, wv_f)
    dmix = _matmul_nt(dx1, wout_f, "out_proj_bwd")
    dwout = jnp.concatenate([_wgrad(oa, dx1, "wgrad_out_a"), _wgrad(ob, dx1, "wgrad_out_b")], axis=0)
    dz_pool, d_wpool, d_pscale = _pool_bwd(dmix, pooled, wp, pool_scale)
    dz, d_lb, d_gn = _hgrn_bwd(z, o_pre, dmix, states, lb_logits, gn, dz_pool)
    dwin = _wgrad(h, dz, "wgrad_in")
    grad_x, d_gmix = _in_bwd(dz, win_f, x0, norm_mix_g, dx1)

    gpack = jnp.concatenate([
        _cols_to_blocks(dwin, IN_WIDTH // N_DEV),
        dwout.reshape(N_DEV, 128, D_MODEL), dwq.reshape(N_DEV, 128, D_MODEL), dwk.reshape(N_DEV, 128, D_MODEL),
        dwv.reshape(N_DEV, 128, D_MODEL), _cols_to_blocks(dwo, D_MODEL // N_DEV),
        _cols_to_blocks(dw1, FF_BLK), dw2.reshape(N_DEV, FF_BLK, D_MODEL)], axis=1)
    small_names = ["norm_mix_g", "lb_logits", "hgrn_norm_g", "pool_scale", "norm_x_g", "norm_mem_g", "norm_ffn_g",
                   "final_norm_g", "w_pool"]
    small_g = {"norm_mix_g": d_gmix, "lb_logits": d_lb, "hgrn_norm_g": d_gn, "pool_scale": d_pscale,
               "norm_x_g": d_gx, "norm_mem_g": d_gmem, "norm_ffn_g": d_gffn, "final_norm_g": d_gfin, "w_pool": d_wpool}
    small_rows = {n: (64 if n == "w_pool" else 8) for n in small_names}
    small_pack = jnp.concatenate([_pad_rows(small_g[n], small_rows[n]) for n in small_names], axis=0)
    recv, srecv = _exchange_grads(gpack, small_pack)
    g_big = _sum_sources(recv, "sum_big_grads", 64)
    g_small = _sum_sources(srecv, "sum_small_grads", SMALL_ROWS)

    given = dict(norm_mix_g=norm_mix_g, w_in=w_in, lb_logits=lb_logits, hgrn_norm_g=hgrn_norm_g, w_pool=w_pool,
                 pool_scale=pool_scale, w_out=w_out, norm_x_g=norm_x_g, norm_mem_g=norm_mem_g, w_xq=w_xq, w_xk=w_xk,
                 w_xv=w_xv, w_xo=w_xo, norm_ffn_g=norm_ffn_g, w_ff1=w_ff1, w_ff2=w_ff2, final_norm_g=final_norm_g)
    moments_m = dict(norm_mix_g=m_norm_mix_g, w_in=m_w_in, lb_logits=m_lb_logits, hgrn_norm_g=m_hgrn_norm_g,
                     w_pool=m_w_pool, pool_scale=m_pool_scale, w_out=m_w_out, norm_x_g=m_norm_x_g,
                     norm_mem_g=m_norm_mem_g, w_xq=m_w_xq, w_xk=m_w_xk, w_xv=m_w_xv, w_xo=m_w_xo,
                     norm_ffn_g=m_norm_ffn_g, w_ff1=m_w_ff1, w_ff2=m_w_ff2, final_norm_g=m_final_norm_g)
    moments_v = dict(norm_mix_g=v_norm_mix_g, w_in=v_w_in, lb_logits=v_lb_logits, hgrn_norm_g=v_hgrn_norm_g,
                     w_pool=v_w_pool, pool_scale=v_pool_scale, w_out=v_w_out, norm_x_g=v_norm_x_g,
                     norm_mem_g=v_norm_mem_g, w_xq=v_w_xq, w_xk=v_w_xk, w_xv=v_w_xv, w_xo=v_w_xo,
                     norm_ffn_g=v_norm_ffn_g, w_ff1=v_w_ff1, w_ff2=v_w_ff2, final_norm_g=v_final_norm_g)
    order = ["norm_mix_g", "w_in", "lb_logits", "hgrn_norm_g", "w_pool", "pool_scale", "w_out", "norm_x_g",
             "norm_mem_g", "w_xq", "w_xk", "w_xv", "w_xo", "norm_ffn_g", "w_ff1", "w_ff2", "final_norm_g"]
    grads, deltas, new_m, new_v = {}, {}, {}, {}

    r0 = 0
    for n, r in PACK_ROWS:
        shape, two_d = given[n].shape, SHARD_2D[n]
        g2 = g_big[r0:r0 + r, :].reshape(two_d)
        r0 += r
        d2, m2, v2 = _adamw(g2, given[n].reshape(two_d), moments_m[n].reshape(two_d), moments_v[n].reshape(two_d),
                            "adamw_" + n)
        grads[n], deltas[n], new_m[n], new_v[n] = (t.reshape(shape) for t in (g2, d2, m2, v2))

    small_w = jnp.concatenate([_pad_rows(given[n], small_rows[n]) for n in small_names], axis=0)
    small_m = jnp.concatenate([_pad_rows(moments_m[n], small_rows[n]) for n in small_names], axis=0)
    small_v = jnp.concatenate([_pad_rows(moments_v[n], small_rows[n]) for n in small_names], axis=0)
    sd, sm, sv = _adamw(g_small, small_w, small_m, small_v, "adamw_small")
    r0 = 0
    for n in small_names:
        shape = given[n].shape
        size = 1
        for dim in shape:
            size *= dim
        rows_used, cols_used = (64, D_MODEL) if n == "w_pool" else (1, size)
        for store, packed in ((grads, g_small), (deltas, sd), (new_m, sm), (new_v, sv)):
            store[n] = packed[r0:r0 + rows_used, 0:cols_used].reshape(shape)
        r0 += small_rows[n]

    return (loss, grad_x.reshape(x.shape), *[grads[n] for n in order], *[deltas[n] for n in order],
            *[new_m[n] for n in order], *[new_v[n] for n in order])
```

```python
import jax
import jax.numpy as jnp
from jax import lax
from jax.experimental import pallas as pl
from jax.experimental.pallas import tpu as pltpu

F32 = jnp.float32
BF16 = jnp.bfloat16

D_MODEL = 1024
N_DEV = 8
HEADS = 4
HD = 128
HW = HEADS * HD
IN_WIDTH = 5 * HW
XHD = 256
MEM_LEN = 256
D_FF = 4096
FF_BLK = D_FF // N_DEV
POOL_WINDOWS = (2, 4, 8, 16)
POOL_HALO = 16
CHUNK = 64
SUB = 16
N_SUB = CHUNK // SUB
EXP_CAP = 80.0
EPS = 1e-6
ROW_TILE = 512
SLOT = 8
V7X_VMEM_LIMIT = 56 * 1024 * 1024

ADAM_LR = 0.001
ADAM_B1 = 0.9
ADAM_B2 = 0.999
ADAM_EPS = 1e-08
ADAM_WD = 0.01
ADAM_STEP = 10

MESH_ID = pl.DeviceIdType.MESH


def _params(sem=None, vmem=V7X_VMEM_LIMIT):
    return pltpu.CompilerParams(dimension_semantics=sem, vmem_limit_bytes=vmem)


def _mm(a, b):
    return lax.dot_general(a.astype(BF16), b.astype(BF16), (((1,), (0,)), ((), ())), preferred_element_type=F32)


def _mm_nt(a, b):
    return lax.dot_general(a.astype(BF16), b.astype(BF16), (((1,), (1,)), ((), ())), preferred_element_type=F32)


def _mm_tn(a, b):
    return lax.dot_general(a.astype(BF16), b.astype(BF16), (((0,), (0,)), ((), ())), preferred_element_type=F32)


def _sigmoid(x):
    return 1.0 / (1.0 + jnp.exp(-x))


def _rms(x):
    return lax.rsqrt(jnp.mean(x * x, axis=-1, keepdims=True) + EPS)


def _rms_bwd(x, g, dh):
    r = _rms(x)
    n = x * r
    dn = dh * g
    dx = r * (dn - n * jnp.mean(dn * n, axis=-1, keepdims=True))
    return dx, jnp.sum(dh * n, axis=0, keepdims=True)


def _tri_dot(tri, x, passes):
    acc = None
    rest = x
    for _ in range(passes):
        piece = rest.astype(BF16)
        part = lax.dot_general(tri, piece, (((1,), (0,)), ((), ())), preferred_element_type=F32)
        acc = part if acc is None else acc + part
        rest = rest - piece.astype(F32)
    return acc


def _adam_update(g, w, m, v):
    nm = ADAM_B1 * m + (1.0 - ADAM_B1) * g
    nv = ADAM_B2 * v + (1.0 - ADAM_B2) * (g * g)
    m_hat = nm / (1.0 - ADAM_B1 ** ADAM_STEP)
    v_hat = nv / (1.0 - ADAM_B2 ** ADAM_STEP)
    return -ADAM_LR * (m_hat / (jnp.sqrt(v_hat) + ADAM_EPS) + ADAM_WD * w), nm, nv


def _full(shape):
    return pl.BlockSpec(shape, lambda *_: (0,) * len(shape))


VMEM_WHOLE = pl.BlockSpec(memory_space=pltpu.VMEM)
ANY_SPACE = pl.BlockSpec(memory_space=pl.ANY)


def _mesh_pos():
    return lax.axis_index("x"), lax.axis_index("y"), lax.axis_index("c")


def _flat(px, py, pc):
    return 4 * px + 2 * py + pc


def _all_gather_weights(shards):
    n = len(shards)
    step = 64

    def body(*refs):
        x_refs, out_refs, bufs = refs[:n], refs[n:2 * n], refs[2 * n:3 * n]
        send_sems, recv_sems, local_sems = refs[3 * n:]
        x, y, c = _mesh_pos()
        me, sibling = (x, y, c), (x, y, 1 - c)
        chips = [(1 - x, y), (x, 1 - y), (1 - x, 1 - y)]

        def copy(a, k, blk, to, src=None):
            rows = out_refs[a].at[_flat(*blk)]
            return pltpu.make_async_remote_copy(
                src_ref=rows if src is None else src, dst_ref=rows,
                send_sem=send_sems.at[7 * a + k], recv_sem=recv_sems.at[7 * a + k], device_id=to, device_id_type=MESH_ID)

        first, mine = [], []
        for a in range(n):
            def cast(i, carry, a=a):
                r0 = pl.multiple_of(i * step, step)
                bufs[a][pl.ds(r0, step), :] = x_refs[a][pl.ds(r0, step), :].astype(BF16)
                return carry
            lax.fori_loop(0, shards[a].shape[0] // step, cast, 0)
            mine.append(pltpu.make_async_copy(bufs[a], out_refs[a].at[_flat(*me)], local_sems.at[a]))
            first.append(copy(a, 0, me, sibling, src=bufs[a]))
            first += [copy(a, 1 + j, me, (*chip, c), src=bufs[a]) for j, chip in enumerate(chips)]
            for cp in [mine[-1]] + first[-4:]:
                cp.start()
        passed = []
        for j, chip in enumerate(chips):
            for a in range(n):
                copy(a, 1 + j, (*chip, c), me).wait_recv()
                passed.append(copy(a, 4 + j, (*chip, c), sibling))
                passed[-1].start()
        for a in range(n):
            copy(a, 0, sibling, me).wait_recv()
            for j, chip in enumerate(chips):
                copy(a, 4 + j, (*chip, 1 - c), me).wait_recv()
        for cp in first + passed:
            cp.wait_send()
        for cp in mine:
            cp.wait()

    return pl.pallas_call(
        body, name="all_gather_weights",
        out_shape=[jax.ShapeDtypeStruct((N_DEV,) + s.shape, BF16) for s in shards],
        in_specs=[VMEM_WHOLE] * n, out_specs=[ANY_SPACE] * n,
        scratch_shapes=[pltpu.VMEM(s.shape, BF16) for s in shards]
        + [pltpu.SemaphoreType.DMA((7 * n,)), pltpu.SemaphoreType.DMA((7 * n,)), pltpu.SemaphoreType.DMA((n,))],
        compiler_params=_params(),
    )(*shards)


def _exchange_grads(scattered, broadcast):
    ns, nb = len(scattered), len(broadcast)
    n = ns + nb

    def body(*refs):
        in_refs, out_refs = refs[:n], refs[n:2 * n]
        send_sems, recv_sems, local_sems = refs[2 * n:]
        x, y, c = _mesh_pos()
        me = _flat(x, y, c)
        copies = []
        for k in range(1, N_DEV):
            peer = (1 - x if k & 4 else x, 1 - y if k & 2 else y, 1 - c if k & 1 else c)
            for a in range(n):
                src = in_refs[a].at[_flat(*peer)] if a < ns else in_refs[a]
                copies.append(pltpu.make_async_remote_copy(
                    src_ref=src, dst_ref=out_refs[a].at[me], send_sem=send_sems.at[7 * a + k - 1],
                    recv_sem=recv_sems.at[7 * a + k - 1], device_id=peer, device_id_type=MESH_ID))
        own = [pltpu.make_async_copy(in_refs[a].at[me] if a < ns else in_refs[a], out_refs[a].at[me], local_sems.at[a])
               for a in range(n)]
        for cp in own + copies:
            cp.start()
        for cp in copies:
            cp.wait_recv()
        for cp in copies:
            cp.wait_send()
        for cp in own:
            cp.wait()

    return pl.pallas_call(
        body, name="exchange_grads",
        out_shape=[jax.ShapeDtypeStruct(a.shape, a.dtype) for a in scattered]
        + [jax.ShapeDtypeStruct((N_DEV,) + a.shape, a.dtype) for a in broadcast],
        in_specs=[ANY_SPACE] * n, out_specs=[ANY_SPACE] * n,
        scratch_shapes=[pltpu.SemaphoreType.DMA((7 * n,)), pltpu.SemaphoreType.DMA((7 * n,)),
                        pltpu.SemaphoreType.DMA((n,))],
        compiler_params=_params(),
    )(*scattered, *broadcast)


def _row_tile(rows):
    for cand in (256, 128, 64, 32, 16):
        if rows % cand == 0:
            return cand
    return rows


def _sum_sources(recv, name):
    _, rows, cols = recv.shape
    tile = _row_tile(rows)

    def body(r_ref, o_ref):
        acc = r_ref[0].astype(F32)
        for d in range(1, N_DEV):
            acc = acc + r_ref[d].astype(F32)
        o_ref[...] = acc

    return pl.pallas_call(
        body, name=name, grid=(rows // tile,),
        out_shape=jax.ShapeDtypeStruct((rows, cols), F32),
        in_specs=[pl.BlockSpec((N_DEV, tile, cols), lambda i: (0, i, 0))],
        out_specs=pl.BlockSpec((tile, cols), lambda i: (i, 0)),
        compiler_params=_params(("parallel",)),
    )(recv)


def _adamw(g, w, m, v, name):
    rows, cols = g.shape
    tile = _row_tile(rows)

    def body(g_ref, w_ref, m_ref, v_ref, d_ref, nm_ref, nv_ref):
        d_ref[...], nm_ref[...], nv_ref[...] = _adam_update(g_ref[...], w_ref[...], m_ref[...], v_ref[...])

    spec = pl.BlockSpec((tile, cols), lambda i: (i, 0))
    shp = jax.ShapeDtypeStruct((rows, cols), F32)
    return pl.pallas_call(
        body, name=name, grid=(rows // tile,), out_shape=[shp, shp, shp],
        in_specs=[spec] * 4, out_specs=[spec] * 3,
        compiler_params=_params(("parallel",)),
    )(g, w, m, v)


def _sum_adamw(recv, w, m, v, name):
    _, rows, cols = recv.shape
    tile = _row_tile(rows)

    def body(r_ref, w_ref, m_ref, v_ref, g_ref, d_ref, nm_ref, nv_ref):
        acc = r_ref[0].astype(F32)
        for d in range(1, N_DEV):
            acc = acc + r_ref[d].astype(F32)
        g_ref[...] = acc
        d_ref[...], nm_ref[...], nv_ref[...] = _adam_update(acc, w_ref[...], m_ref[...], v_ref[...])

    spec = pl.BlockSpec((tile, cols), lambda i: (i, 0))
    shp = jax.ShapeDtypeStruct((rows, cols), F32)
    return pl.pallas_call(
        body, name=name, grid=(rows // tile,), out_shape=[shp] * 4,
        in_specs=[pl.BlockSpec((N_DEV, tile, cols), lambda i: (0, i, 0)), spec, spec, spec], out_specs=[spec] * 4,
        compiler_params=_params(("parallel",)),
    )(recv, w, m, v)


SMALL_SLOTS = {"norm_mix_g": (0, 1, D_MODEL), "lb_logits": (8, 2, HW), "hgrn_norm_g": (16, HEADS, HD),
               "pool_scale": (24, 1, HW), "norm_x_g": (32, 1, D_MODEL), "norm_mem_g": (40, 1, D_MODEL),
               "norm_ffn_g": (48, 1, D_MODEL), "final_norm_g": (56, 1, D_MODEL)}
LOSS_ROW = 57
SMALL_ORDER = ("norm_mix_g", "lb_logits", "hgrn_norm_g", "pool_scale", "norm_x_g", "norm_mem_g", "norm_ffn_g",
               "final_norm_g", "w_pool")


def _small_update(srecv, wprecv, params):
    flat = [t for n in SMALL_ORDER for t in params[n]]

    def body(*refs):
        s_ref, wp_ref = refs[0], refs[1]
        in_refs = refs[2:2 + 3 * len(SMALL_ORDER)]
        loss_ref = refs[2 + 3 * len(SMALL_ORDER)]
        out_refs = refs[3 + 3 * len(SMALL_ORDER):-1]
        acc = refs[-1]
        total = s_ref[0]
        for d in range(1, N_DEV):
            total = total + s_ref[d]
        acc[...] = total
        loss_ref[...] = acc[LOSS_ROW:LOSS_ROW + 1, 0:1]
        for i, name in enumerate(SMALL_ORDER):
            w_ref, m_ref, v_ref = in_refs[3 * i:3 * i + 3]
            g_ref, d_ref, nm_ref, nv_ref = out_refs[4 * i:4 * i + 4]
            if name == "w_pool":
                g = wp_ref[0]
                for d in range(1, N_DEV):
                    g = g + wp_ref[d]
            else:
                r0, nr, nc = SMALL_SLOTS[name]
                g = acc[r0:r0 + nr, 0:nc]
            g_ref[...] = g
            d_ref[...], nm_ref[...], nv_ref[...] = _adam_update(g, w_ref[...], m_ref[...], v_ref[...])

    out_shape = [jax.ShapeDtypeStruct((1, 1), F32)]
    for n in SMALL_ORDER:
        out_shape += [jax.ShapeDtypeStruct(params[n][0].shape, F32)] * 4
    outs = pl.pallas_call(
        body, name="small_update", out_shape=out_shape,
        in_specs=[VMEM_WHOLE] * (2 + len(flat)), out_specs=[VMEM_WHOLE] * len(out_shape),
        scratch_shapes=[pltpu.VMEM(srecv.shape[1:], F32)],
        compiler_params=_params(),
    )(srecv, wprecv, *flat)
    return outs[0], {n: outs[1 + 4 * i:5 + 4 * i] for i, n in enumerate(SMALL_ORDER)}


def _in_proj(x, g, w_t):
    s = x.shape[0]
    tm = min(ROW_TILE, s)

    def body(x_ref, g_ref, w_ref, z_ref, h_ref):
        xv = x_ref[...]
        h = (xv * _rms(xv) * g_ref[...]).astype(BF16)
        h_ref[...] = h
        z_ref[...] = _mm_nt(h, w_ref[...])

    return pl.pallas_call(
        body, name="in_proj", grid=(s // tm,),
        out_shape=[jax.ShapeDtypeStruct((s, IN_WIDTH), F32), jax.ShapeDtypeStruct((s, D_MODEL), BF16)],
        in_specs=[pl.BlockSpec((tm, D_MODEL), lambda i: (i, 0)), _full((1, D_MODEL)), VMEM_WHOLE],
        out_specs=[pl.BlockSpec((tm, IN_WIDTH), lambda i: (i, 0)), pl.BlockSpec((tm, D_MODEL), lambda i: (i, 0))],
        compiler_params=_params(("parallel",)),
    )(x, g, w_t)


def _chunk_masks():
    row = lax.broadcasted_iota(jnp.int32, (CHUNK, CHUNK), 0)
    col = lax.broadcasted_iota(jnp.int32, (CHUNK, CHUNK), 1)
    return row, col


def _ones_where(mask):
    return jnp.where(mask, 1.0, 0.0).astype(BF16)


def _hgrn_gates(zq, zf, lb):
    sq = _sigmoid(zq)
    sig = _sigmoid(zf)
    f = lb + (1.0 - lb) * sig
    return zq * sq, sq, sig, f


def _hgrn_intra_factors(b_scr, bh, qh, kh, sl):
    trow = lax.broadcasted_iota(jnp.int32, (CHUNK, HD), 0)
    eq, ek = [], []
    for j in range(N_SUB):
        if j == 0:
            base = jnp.zeros((1, HD), F32)
        else:
            base = b_scr[SUB * j - 1:SUB * j, sl]
        in_j = (trow >= SUB * j) & (trow < SUB * (j + 1))
        eq.append(jnp.where(in_j, jnp.exp(bh - base), 0.0))
        ek.append(jnp.where(trow < SUB * (j + 1), jnp.exp(jnp.minimum(base - bh, EXP_CAP)), 0.0))
    eqcat = jnp.concatenate(eq, axis=1)
    ekcat = jnp.concatenate(ek, axis=1)
    qcat = jnp.concatenate([qh] * N_SUB, axis=1) * eqcat
    kecat = jnp.concatenate([kh] * N_SUB, axis=1) * ekcat
    return qcat, kecat, eqcat, ekcat


def _sum_lane_blocks(a):
    out = a[:, 0:HD]
    for j in range(1, N_SUB):
        out = out + a[:, HD * j:HD * (j + 1)]
    return out


def _hgrn_fwd(z, lb_logits, gn, mixed_in):
    s = z.shape[0]
    n_chunks = s // CHUNK

    def body(zq_ref, zf_ref, zi_ref, zg_ref, lbl_ref, gn_ref, mixin_ref, oa_ref, o_ref, st_ref, state, b_scr):
        @pl.when(pl.program_id(0) == 0)
        def _():
            state[...] = jnp.zeros_like(state)

        st_ref[0] = state[...]
        lb = _sigmoid(lbl_ref[0:1, :] - lbl_ref[1:2, :])
        q, _, _, f = _hgrn_gates(zq_ref[...], zf_ref[...], lb)
        kk = 1.0 - f
        row, col = _chunk_masks()
        causal = col <= row
        b_scr[...] = _tri_dot(_ones_where(causal), jnp.log(f), 3)
        for h in range(HEADS):
            sl = slice(HD * h, HD * (h + 1))
            bh = b_scr[:, sl]
            qh, kh, vh = q[:, sl], kk[:, sl], zi_ref[:, sl]
            st = state[h]
            b_last = b_scr[CHUNK - 1:CHUNK, sl]
            qcat, kecat, _, _ = _hgrn_intra_factors(b_scr, bh, qh, kh, sl)
            a = jnp.where(causal, _mm_nt(qcat, kecat), 0.0)
            o = _mm(a, vh) + _mm_nt(qh * jnp.exp(bh), st)
            state[h] = st * jnp.exp(b_last) + _mm_tn(vh, kh * jnp.exp(b_last - bh))
            o_ref[:, sl] = o
            zg = zg_ref[:, sl]
            oa_ref[:, sl] = (o * _rms(o) * gn_ref[h:h + 1, :] * zg * _sigmoid(zg)).astype(BF16)

    zspec = lambda cb: pl.BlockSpec((CHUNK, HW), lambda i, cb=cb: (i, cb))
    return pl.pallas_call(
        body, name="hgrn_fwd", grid=(n_chunks,),
        out_shape=[jax.ShapeDtypeStruct((s, 2 * HW), BF16), jax.ShapeDtypeStruct((s, HW), F32),
                   jax.ShapeDtypeStruct((n_chunks, HEADS, HD, HD), F32)],
        in_specs=[zspec(0), zspec(1), zspec(2), zspec(3), _full((2, HW)), _full((HEADS, HD)), ANY_SPACE],
        out_specs=[pl.BlockSpec((CHUNK, HW), lambda i: (i, 0)), pl.BlockSpec((CHUNK, HW), lambda i: (i, 0)),
                   pl.BlockSpec((1, HEADS, HD, HD), lambda i: (i, 0, 0, 0))],
        scratch_shapes=[pltpu.VMEM((HEADS, HD, HD), F32), pltpu.VMEM((CHUNK, HW), F32)],
        input_output_aliases={6: 0},
        compiler_params=_params(("arbitrary",)),
    )(z, z, z, z, lb_logits, gn, mixed_in)


def _pool_counts(tile_idx, tm):
    t = tile_idx * tm + lax.broadcasted_iota(jnp.int32, (tm, 1), 0)
    return [1.0 / jnp.minimum(t + 1, w).astype(F32) for w in POOL_WINDOWS]


def _pool_fwd(z, w_pool, scale):
    s = z.shape[0]
    tm = min(ROW_TILE, s)

    def body(p_ref, w_ref, sc_ref, ob_ref, pooled_ref, ext):
        i = pl.program_id(0)

        @pl.when(i == 0)
        def _():
            ext[0:POOL_HALO, :] = jnp.zeros((POOL_HALO, HW), F32)

        @pl.when(i > 0)
        def _():
            ext[0:POOL_HALO, :] = ext[tm:tm + POOL_HALO, :]

        ext[POOL_HALO:POOL_HALO + tm, :] = p_ref[...]
        inv = _pool_counts(i, tm)
        for g, w in enumerate(POOL_WINDOWS):
            sl = slice(HD * g, HD * (g + 1))
            p = ext[POOL_HALO:POOL_HALO + tm, sl]
            win = p
            for d in range(1, w):
                win = win + ext[POOL_HALO - d:POOL_HALO - d + tm, sl]
            pooled = (win * inv[g] - p).astype(BF16)
            pooled_ref[:, sl] = pooled
            ob_ref[:, sl] = (_mm(pooled, w_ref[g]) * sc_ref[:, sl]).astype(BF16)

    return pl.pallas_call(
        body, name="pool_fwd", grid=(s // tm,),
        out_shape=[jax.ShapeDtypeStruct((s, 2 * HW), BF16), jax.ShapeDtypeStruct((s, HW), BF16)],
        in_specs=[pl.BlockSpec((tm, HW), lambda i: (i, 4)), _full((HEADS, HD, HD)), _full((1, HW))],
        out_specs=[pl.BlockSpec((tm, HW), lambda i: (i, 1)), pl.BlockSpec((tm, HW), lambda i: (i, 0))],
        scratch_shapes=[pltpu.VMEM((tm + POOL_HALO, HW), F32)],
        compiler_params=_params(("arbitrary",)),
    )(z, w_pool, scale)


def _out_proj(x, mixed, w_out):
    s = x.shape[0]
    tm = min(ROW_TILE, s)

    def body(x_ref, a_ref, w_ref, o_ref):
        o_ref[...] = x_ref[...] + _mm(a_ref[...], w_ref[...])

    return pl.pallas_call(
        body, name="out_proj", grid=(s // tm,),
        out_shape=jax.ShapeDtypeStruct((s, D_MODEL), F32),
        in_specs=[pl.BlockSpec((tm, D_MODEL), lambda i: (i, 0)), pl.BlockSpec((tm, D_MODEL), lambda i: (i, 0)), VMEM_WHOLE],
        out_specs=pl.BlockSpec((tm, D_MODEL), lambda i: (i, 0)),
        compiler_params=_params(("parallel",)),
    )(x, mixed, w_out)


def _mem_kv(mem, g, wk, wv):
    def body(m_ref, g_ref, wk_ref, wv_ref, hm_ref, k_ref, v_ref):
        m = m_ref[...]
        hm = (m * _rms(m) * g_ref[...]).astype(BF16)
        hm_ref[...] = hm
        k_ref[...] = _mm(hm, wk_ref[...]).astype(BF16)
        v_ref[...] = _mm(hm, wv_ref[...]).astype(BF16)

    shp = jax.ShapeDtypeStruct((MEM_LEN, D_MODEL), BF16)
    return pl.pallas_call(
        body, name="mem_kv", out_shape=[shp, shp, shp],
        in_specs=[VMEM_WHOLE] * 4, out_specs=[VMEM_WHOLE] * 3,
        compiler_params=_params(),
    )(mem, g, wk, wv)


def _softmax_rows(sc):
    e = jnp.exp(sc - jnp.max(sc, axis=-1, keepdims=True))
    return e / jnp.sum(e, axis=-1, keepdims=True)


def _xattn_fwd(x, g, wq, xk, xv, wo_t):
    s = x.shape[0]
    tm = min(ROW_TILE, s)
    scale = XHD ** -0.5

    def body(x_ref, g_ref, wq_ref, k_ref, v_ref, wo_ref, o_ref, hq_ref, q_ref, att_ref):
        xv_ = x_ref[...]
        hq = (xv_ * _rms(xv_) * g_ref[...]).astype(BF16)
        hq_ref[...] = hq
        q_ref[...] = (_mm(hq, wq_ref[...]) * scale).astype(BF16)
        for h in range(HEADS):
            sl = slice(XHD * h, XHD * (h + 1))
            p = _softmax_rows(_mm_nt(q_ref[:, sl], k_ref[:, sl]))
            att_ref[:, sl] = _mm(p, v_ref[:, sl]).astype(BF16)
        o_ref[...] = xv_ + _mm_nt(att_ref[...], wo_ref[...])

    row_f32 = pl.BlockSpec((tm, D_MODEL), lambda i: (i, 0))
    bshape = jax.ShapeDtypeStruct((s, D_MODEL), BF16)
    return pl.pallas_call(
        body, name="xattn_fwd", grid=(s // tm,),
        out_shape=[jax.ShapeDtypeStruct((s, D_MODEL), F32), bshape, bshape, bshape],
        in_specs=[row_f32, _full((1, D_MODEL)), VMEM_WHOLE, VMEM_WHOLE, VMEM_WHOLE, VMEM_WHOLE],
        out_specs=[row_f32] * 4,
        compiler_params=_params(("parallel",)),
    )(x, g, wq, xk, xv, wo_t)


def _mlp_fwd_loss(x, g, w1, w2, gf, target):
    s = x.shape[0]
    tm = min(ROW_TILE, s)

    def body(x_ref, g_ref, w1_ref, w2_ref, gf_ref, t_ref, dx_ref, u_ref, hf_ref, slot_ref):
        @pl.when(pl.program_id(0) == 0)
        def _():
            slot_ref[...] = jnp.zeros_like(slot_ref)

        xv = x_ref[...]
        hf = (xv * _rms(xv) * g_ref[...]).astype(BF16)
        hf_ref[...] = hf
        acc = xv
        for j in range(N_DEV):
            a = jnp.maximum(_mm(hf, w1_ref[j]), 0.0)
            u = (a * a).astype(BF16)
            u_ref[:, FF_BLK * j:FF_BLK * (j + 1)] = u
            acc = acc + _mm(u, w2_ref[j])
        gfv = gf_ref[...]
        r = _rms(acc)
        n = acc * r
        err = n * gfv - t_ref[...]
        slot_ref[1:2, :] += jnp.sum(jnp.mean(err * err, axis=-1, keepdims=True), axis=0, keepdims=True) * 0.5
        dy = err * (1.0 / D_MODEL)
        slot_ref[0:1, :] += jnp.sum(dy * n, axis=0, keepdims=True)
        dn = dy * gfv
        dx_ref[...] = r * (dn - n * jnp.mean(dn * n, axis=-1, keepdims=True))

    row_f32 = pl.BlockSpec((tm, D_MODEL), lambda i: (i, 0))
    return pl.pallas_call(
        body, name="mlp_fwd_loss", grid=(s // tm,),
        out_shape=[jax.ShapeDtypeStruct((s, D_MODEL), F32), jax.ShapeDtypeStruct((s, D_FF), BF16),
                   jax.ShapeDtypeStruct((s, D_MODEL), BF16), jax.ShapeDtypeStruct((SLOT, D_MODEL), F32)],
        in_specs=[row_f32, _full((1, D_MODEL)), VMEM_WHOLE, VMEM_WHOLE, _full((1, D_MODEL)), row_f32],
        out_specs=[row_f32, pl.BlockSpec((tm, D_FF), lambda i: (i, 0)), row_f32, _full((SLOT, D_MODEL))],
        compiler_params=_params(("arbitrary",)),
    )(x, g, w1, w2, gf, target)


def _zero_slot(slot_ref):
    @pl.when(pl.program_id(0) == 0)
    def _():
        slot_ref[...] = jnp.zeros_like(slot_ref)


def _mlp_bwd(dx3, u, x2, g, w1, w2):
    s = x2.shape[0]
    tm = min(ROW_TILE, s)

    def body(d_ref, u_ref, x_ref, g_ref, w1_ref, w2_ref, da_ref, dx_ref, slot_ref):
        _zero_slot(slot_ref)
        d = d_ref[...]
        d16 = d.astype(BF16)
        dhf = jnp.zeros((tm, D_MODEL), F32)
        for j in range(N_DEV):
            sl = slice(FF_BLK * j, FF_BLK * (j + 1))
            da = (_mm_nt(d16, w2_ref[j]) * (2.0 * jnp.sqrt(u_ref[:, sl].astype(F32)))).astype(BF16)
            da_ref[:, sl] = da
            dhf = dhf + _mm_nt(da, w1_ref[j])
        dx, dg = _rms_bwd(x_ref[...], g_ref[...], dhf)
        dx_ref[...] = d + dx
        slot_ref[0:1, :] += dg

    row_f32 = pl.BlockSpec((tm, D_MODEL), lambda i: (i, 0))
    return pl.pallas_call(
        body, name="mlp_bwd", grid=(s // tm,),
        out_shape=[jax.ShapeDtypeStruct((s, D_FF), BF16), jax.ShapeDtypeStruct((s, D_MODEL), F32),
                   jax.ShapeDtypeStruct((SLOT, D_MODEL), F32)],
        in_specs=[row_f32, pl.BlockSpec((tm, D_FF), lambda i: (i, 0)), row_f32, _full((1, D_MODEL)),
                  VMEM_WHOLE, VMEM_WHOLE],
        out_specs=[pl.BlockSpec((tm, D_FF), lambda i: (i, 0)), row_f32, _full((SLOT, D_MODEL))],
        compiler_params=_params(("arbitrary",)),
    )(dx3, u, x2, g, w1, w2)


def _wgrad(a, b, name, col_blocks=False):
    s, m = a.shape
    n = b.shape[1]
    tm = 1280 if m % 1280 == 0 else min(1024, m)
    tn = n // N_DEV if col_blocks else min(1024, n)
    ts = min(ROW_TILE, s)
    n_s = s // ts

    def body(a_ref, b_ref, o_ref, acc):
        k = pl.program_id(2)

        @pl.when(k == 0)
        def _():
            acc[...] = jnp.zeros_like(acc)

        acc[...] += _mm_tn(a_ref[...], b_ref[...])

        @pl.when(k == n_s - 1)
        def _():
            o_ref[...] = acc[...].astype(BF16)

    if col_blocks:
        out_shape = jax.ShapeDtypeStruct((N_DEV, m, tn), BF16)
        out_spec = pl.BlockSpec((None, tm, tn), lambda i, j, k: (j, i, 0))
    else:
        out_shape = jax.ShapeDtypeStruct((m, n), BF16)
        out_spec = pl.BlockSpec((tm, tn), lambda i, j, k: (i, j))
    return pl.pallas_call(
        body, name=name, grid=(m // tm, n // tn, n_s), out_shape=out_shape,
        in_specs=[pl.BlockSpec((ts, tm), lambda i, j, k: (k, i)), pl.BlockSpec((ts, tn), lambda i, j, k: (k, j))],
        out_specs=out_spec,
        scratch_shapes=[pltpu.VMEM((tm, tn), F32)],
        compiler_params=_params(("parallel", "parallel", "arbitrary")),
    )(a, b)


def _xattn_bwd(dx2, x1, g, q, xk, xv, wq, wo_t):
    s = x1.shape[0]
    tm = min(ROW_TILE, s)
    scale = XHD ** -0.5

    def body(d_ref, x_ref, g_ref, q_ref, k_ref, v_ref, wq_ref, wo_ref, dx_ref, dq_ref, dk_ref, dv_ref, slot_ref, datt):
        _zero_slot(slot_ref)

        @pl.when(pl.program_id(0) == 0)
        def _():
            dk_ref[...] = jnp.zeros_like(dk_ref)
            dv_ref[...] = jnp.zeros_like(dv_ref)

        d = d_ref[...]
        datt[...] = _mm(d, wo_ref[...]).astype(BF16)
        for h in range(HEADS):
            sl = slice(XHD * h, XHD * (h + 1))
            qh, kh, vh, dah = q_ref[:, sl], k_ref[:, sl], v_ref[:, sl], datt[:, sl]
            p = _softmax_rows(_mm_nt(qh, kh))
            dp = _mm_nt(dah, vh)
            ds = (p * (dp - jnp.sum(dp * p, axis=-1, keepdims=True))).astype(BF16)
            dq_ref[:, sl] = (_mm(ds, kh) * scale).astype(BF16)
            dk_ref[:, sl] += _mm_tn(ds, qh)
            dv_ref[:, sl] += _mm_tn(p, dah)
        dx, dg = _rms_bwd(x_ref[...], g_ref[...], _mm_nt(dq_ref[...], wq_ref[...]))
        dx_ref[...] = d + dx
        slot_ref[0:1, :] += dg

    row_f32 = pl.BlockSpec((tm, D_MODEL), lambda i: (i, 0))
    kv = jax.ShapeDtypeStruct((MEM_LEN, D_MODEL), F32)
    return pl.pallas_call(
        body, name="xattn_bwd", grid=(s // tm,),
        out_shape=[jax.ShapeDtypeStruct((s, D_MODEL), F32), jax.ShapeDtypeStruct((s, D_MODEL), BF16), kv, kv,
                   jax.ShapeDtypeStruct((SLOT, D_MODEL), F32)],
        in_specs=[row_f32, row_f32, _full((1, D_MODEL)), row_f32, VMEM_WHOLE, VMEM_WHOLE, VMEM_WHOLE, VMEM_WHOLE],
        out_specs=[row_f32, row_f32, _full((MEM_LEN, D_MODEL)), _full((MEM_LEN, D_MODEL)), _full((SLOT, D_MODEL))],
        scratch_shapes=[pltpu.VMEM((tm, D_MODEL), BF16)],
        compiler_params=_params(("arbitrary",)),
    )(dx2, x1, g, q, xk, xv, wq, wo_t)


def _mem_bwd(mem, g, hm, dxk, dxv, wk, wv):
    def body(m_ref, g_ref, hm_ref, dk_ref, dv_ref, wk_ref, wv_ref, dwk_ref, dwv_ref, slot_ref):
        dk, dv = dk_ref[...], dv_ref[...]
        hm_ = hm_ref[...]
        dwk_ref[...] = _mm_tn(hm_, dk).astype(BF16)
        dwv_ref[...] = _mm_tn(hm_, dv).astype(BF16)
        _, dg = _rms_bwd(m_ref[...], g_ref[...], _mm_nt(dk, wk_ref[...]) + _mm_nt(dv, wv_ref[...]))
        slot_ref[...] = jnp.zeros_like(slot_ref)
        slot_ref[0:1, :] = dg

    wshape = jax.ShapeDtypeStruct((D_MODEL, D_MODEL), BF16)
    return pl.pallas_call(
        body, name="mem_bwd", out_shape=[wshape, wshape, jax.ShapeDtypeStruct((SLOT, D_MODEL), F32)],
        in_specs=[VMEM_WHOLE] * 7, out_specs=[VMEM_WHOLE] * 3,
        compiler_params=_params(),
    )(mem, g, hm, dxk, dxv, wk, wv)


def _matmul_nt(a, w, name):
    s, k = a.shape
    n = w.shape[0]
    tm = min(ROW_TILE, s)

    def body(a_ref, w_ref, o_ref):
        o_ref[...] = _mm_nt(a_ref[...], w_ref[...])

    return pl.pallas_call(
        body, name=name, grid=(s // tm,),
        out_shape=jax.ShapeDtypeStruct((s, n), F32),
        in_specs=[pl.BlockSpec((tm, k), lambda i: (i, 0)), VMEM_WHOLE],
        out_specs=pl.BlockSpec((tm, n), lambda i: (i, 0)),
        compiler_params=_params(("parallel",)),
    )(a, w)


def _pool_bwd(dmix, pooled, w_pool, scale):
    s = dmix.shape[0]
    tm = min(ROW_TILE, s)
    n_t = s // tm

    def body(do_ref, pl_ref, w_ref, sc_ref, dz_ref, dw_ref, slot_ref, ext):
        i = pl.program_id(0)
        tile = n_t - 1 - i
        _zero_slot(slot_ref)

        @pl.when(i == 0)
        def _():
            dw_ref[...] = jnp.zeros_like(dw_ref)
            ext[tm:tm + POOL_HALO, :] = jnp.zeros((POOL_HALO, HW), F32)

        @pl.when(i > 0)
        def _():
            ext[tm:tm + POOL_HALO, :] = ext[0:POOL_HALO, :]

        inv = _pool_counts(tile, tm)
        dpooled = []
        for g in range(HEADS):
            sl = slice(HD * g, HD * (g + 1))
            pooled_g = pl_ref[:, sl]
            do = do_ref[:, sl]
            slot_ref[0:1, sl] += jnp.sum(_mm(pooled_g, w_ref[g]) * do, axis=0, keepdims=True)
            dy = (do * sc_ref[:, sl]).astype(BF16)
            dw_ref[g] += _mm_tn(pooled_g, dy)
            dpo = _mm_nt(dy, w_ref[g])
            dpooled.append(dpo)
            ext[0:tm, sl] = dpo * inv[g]
        for g, w in enumerate(POOL_WINDOWS):
            sl = slice(HD * g, HD * (g + 1))
            win = ext[0:tm, sl]
            for d in range(1, w):
                win = win + ext[d:d + tm, sl]
            dz_ref[:, sl] = win - dpooled[g]

    return pl.pallas_call(
        body, name="pool_bwd", grid=(n_t,),
        out_shape=[jax.ShapeDtypeStruct((s, IN_WIDTH), F32), jax.ShapeDtypeStruct((HEADS, HD, HD), F32),
                   jax.ShapeDtypeStruct((SLOT, D_MODEL), F32)],
        in_specs=[pl.BlockSpec((tm, HW), lambda i: (n_t - 1 - i, 1)), pl.BlockSpec((tm, HW), lambda i: (n_t - 1 - i, 0)),
                  _full((HEADS, HD, HD)), _full((1, HW))],
        out_specs=[pl.BlockSpec((tm, HW), lambda i: (n_t - 1 - i, 4)), _full((HEADS, HD, HD)), _full((SLOT, D_MODEL))],
        scratch_shapes=[pltpu.VMEM((tm + POOL_HALO, HW), F32)],
        compiler_params=_params(("arbitrary",)),
    )(dmix, pooled, w_pool, scale)


def _hgrn_bwd(z, o, dmix, states, lb_logits, gn, dz_in):
    s = z.shape[0]
    n_chunks = s // CHUNK

    def body(zq_ref, zf_ref, zi_ref, zg_ref, o_ref, do_ref, st_ref, lbl_ref, gn_ref, dzin_ref,
             dz_ref, dlb_ref, dgn_ref, dstate, b_scr, dlb_acc):
        i = pl.program_id(0)

        @pl.when(i == 0)
        def _():
            dstate[...] = jnp.zeros_like(dstate)
            dlb_acc[...] = jnp.zeros_like(dlb_acc)
            dgn_ref[...] = jnp.zeros_like(dgn_ref)
            dlb_ref[...] = jnp.zeros_like(dlb_ref)

        lb = _sigmoid(lbl_ref[0:1, :] - lbl_ref[1:2, :])
        zq = zq_ref[...]
        q, sq, sig, f = _hgrn_gates(zq, zf_ref[...], lb)
        kk = 1.0 - f
        row, col = _chunk_masks()
        causal = col <= row
        upper = _ones_where(col >= row)
        strict_lower = _ones_where(col < row)
        b_scr[...] = _tri_dot(_ones_where(causal), jnp.log(f), 3)
        for h in range(HEADS):
            sl = slice(HD * h, HD * (h + 1))
            oh = o_ref[:, sl]
            gnh = gn_ref[h:h + 1, :]
            zg = zg_ref[:, sl]
            sg = _sigmoid(zg)
            doa = do_ref[:, sl]
            don = doa * (zg * sg)
            d_o, dgn = _rms_bwd(oh, gnh, don)
            dgn_ref[h:h + 1, 0:HD] += dgn
            dz_ref[:, 3 * HW + HD * h:3 * HW + HD * (h + 1)] = doa * (oh * _rms(oh) * gnh) * (sg * (1.0 + zg * (1.0 - sg)))
            bh = b_scr[:, sl]
            qh, kh, vh = q[:, sl], kk[:, sl], zi_ref[:, sl]
            st0 = st_ref[0, h]
            ds1 = dstate[h]
            b_last = b_scr[CHUNK - 1:CHUNK, sl]
            lam = jnp.exp(bh)
            e_last = jnp.exp(b_last - bh)
            lam_last = jnp.exp(b_last)
            qcat, kecat, eqcat, ekcat = _hgrn_intra_factors(b_scr, bh, qh, kh, sl)
            a = jnp.where(causal, _mm_nt(qcat, kecat), 0.0)
            da = jnp.where(causal, _mm_nt(d_o, vh), 0.0)
            dz_ref[:, 2 * HW + HD * h:2 * HW + HD * (h + 1)] = _mm_tn(a, d_o) + _mm_nt(kh * e_last, ds1)
            q16, ke16 = qcat.astype(BF16), kecat.astype(BF16)
            gq = _mm(da, ke16)
            gk = _mm_tn(da, q16)
            dq_inter = lam * _mm(d_o, st0)
            dq = _sum_lane_blocks(eqcat * gq) + dq_inter
            dk_intra = _sum_lane_blocks(ekcat * gk)
            dk_state = _mm(vh, ds1) * e_last
            state_term = lam_last * jnp.sum(st0 * ds1, axis=0, keepdims=True)
            dstate[h] = ds1 * lam_last + _mm_tn(d_o, qh * lam)
            db_intra = _sum_lane_blocks(q16.astype(F32) * gq - ke16.astype(F32) * gk)
            dlf = (_tri_dot(upper, db_intra + qh * dq_inter, 2) + _tri_dot(strict_lower, kh * dk_state, 2)
                   + state_term)
            sigh = sig[:, sl]
            df = dlf / f[:, sl] - (dk_intra + dk_state)
            dlb_acc[:, sl] += jnp.sum(df * (1.0 - sigh), axis=0, keepdims=True)
            dz_ref[:, HW + HD * h:HW + HD * (h + 1)] = df * (1.0 - lb[:, sl]) * sigh * (1.0 - sigh)
            sqh = sq[:, sl]
            dz_ref[:, sl] = dq * (sqh * (1.0 + zq[:, sl] * (1.0 - sqh)))

        @pl.when(i == n_chunks - 1)
        def _():
            dl0 = dlb_acc[...] * lb * (1.0 - lb)
            dlb_ref[0:1, 0:HW] = dl0
            dlb_ref[1:2, 0:HW] = -dl0

    rev = lambda i: n_chunks - 1 - i
    zspec = lambda cb: pl.BlockSpec((CHUNK, HW), lambda i, cb=cb: (rev(i), cb))
    slot = jax.ShapeDtypeStruct((SLOT, D_MODEL), F32)
    return pl.pallas_call(
        body, name="hgrn_bwd", grid=(n_chunks,),
        out_shape=[jax.ShapeDtypeStruct((s, IN_WIDTH), F32), slot, slot],
        in_specs=[zspec(0), zspec(1), zspec(2), zspec(3), pl.BlockSpec((CHUNK, HW), lambda i: (rev(i), 0)),
                  pl.BlockSpec((CHUNK, HW), lambda i: (rev(i), 0)),
                  pl.BlockSpec((1, HEADS, HD, HD), lambda i: (rev(i), 0, 0, 0)), _full((2, HW)), _full((HEADS, HD)), ANY_SPACE],
        out_specs=[pl.BlockSpec((CHUNK, 4 * HW), lambda i: (rev(i), 0)), _full((SLOT, D_MODEL)), _full((SLOT, D_MODEL))],
        scratch_shapes=[pltpu.VMEM((HEADS, HD, HD), F32), pltpu.VMEM((CHUNK, HW), F32), pltpu.VMEM((1, HW), F32)],
        input_output_aliases={9: 0},
        compiler_params=_params(("arbitrary",)),
    )(z, z, z, z, o, dmix, states, lb_logits, gn, dz_in)


def _in_bwd(dz, w_t, x0, g, dx1):
    s = x0.shape[0]
    tm = min(ROW_TILE, s)

    def body(dz_ref, w_ref, x_ref, g_ref, d_ref, dx_ref, slot_ref):
        _zero_slot(slot_ref)
        dx, dg = _rms_bwd(x_ref[...], g_ref[...], _mm(dz_ref[...], w_ref[...]))
        dx_ref[...] = d_ref[...] + dx
        slot_ref[0:1, :] += dg

    row_f32 = pl.BlockSpec((tm, D_MODEL), lambda i: (i, 0))
    return pl.pallas_call(
        body, name="in_bwd", grid=(s // tm,),
        out_shape=[jax.ShapeDtypeStruct((s, D_MODEL), F32), jax.ShapeDtypeStruct((SLOT, D_MODEL), F32)],
        in_specs=[pl.BlockSpec((tm, IN_WIDTH), lambda i: (i, 0)), VMEM_WHOLE, row_f32, _full((1, D_MODEL)), row_f32],
        out_specs=[row_f32, _full((SLOT, D_MODEL))],
        compiler_params=_params(("arbitrary",)),
    )(dz, w_t, x0, g, dx1)


def kernel(x, mem, norm_mix_g, w_in, lb_logits, hgrn_norm_g, w_pool, pool_scale, w_out, norm_x_g, norm_mem_g, w_xq, w_xk, w_xv, w_xo, norm_ffn_g, w_ff1, w_ff2, final_norm_g, loss_target, m_norm_mix_g, m_w_in, m_lb_logits, m_hgrn_norm_g, m_w_pool, m_pool_scale, m_w_out, m_norm_x_g, m_norm_mem_g, m_w_xq, m_w_xk, m_w_xv, m_w_xo, m_norm_ffn_g, m_w_ff1, m_w_ff2, m_final_norm_g, v_norm_mix_g, v_w_in, v_lb_logits, v_hgrn_norm_g, v_w_pool, v_pool_scale, v_w_out, v_norm_x_g, v_norm_mem_g, v_w_xq, v_w_xk, v_w_xv, v_w_xo, v_norm_ffn_g, v_w_ff1, v_w_ff2, v_final_norm_g):
    x0 = x[0]
    mem0 = mem[0]
    tgt = loss_target[0]
    gn = hgrn_norm_g[0]
    gfin = final_norm_g.reshape(1, D_MODEL)
    wp = w_pool[0]
    heads_2d = lambda w: w.reshape(D_MODEL // N_DEV, D_MODEL)
    xo_2d = lambda w: w.reshape(D_MODEL, D_MODEL // N_DEV)

    gathered = _all_gather_weights([w_in[0].T, w_out[0], heads_2d(w_xq), heads_2d(w_xk), heads_2d(w_xv),
                                    xo_2d(w_xo).T, w_ff1[0], w_ff2[0]])
    win_t = gathered[0].reshape(IN_WIDTH, D_MODEL)
    wout_f, wq_f, wk_f, wv_f, wo_t = (t.reshape(D_MODEL, D_MODEL) for t in gathered[1:6])
    w1_b, w2_b = gathered[6], gathered[7]

    z, h = _in_proj(x0, norm_mix_g, win_t)
    mixed_b, pooled = _pool_fwd(z, wp, pool_scale)
    mixed, o_pre, states = _hgrn_fwd(z, lb_logits, gn, mixed_b)
    x1 = _out_proj(x0, mixed, wout_f)
    hm, xk, xv = _mem_kv(mem0, norm_mem_g, wk_f, wv_f)
    x2, hq, xq, att = _xattn_fwd(x1, norm_x_g, wq_f, xk, xv, wo_t)
    dx3, u, hf, slot_fin = _mlp_fwd_loss(x2, norm_ffn_g, w1_b, w2_b, gfin, tgt)

    da, dx2, slot_ffn = _mlp_bwd(dx3, u, x2, norm_ffn_g, w1_b, w2_b)
    dw2 = _wgrad(u, dx3, "wgrad_ff2")
    dw1 = _wgrad(hf, da, "wgrad_ff1", col_blocks=True)
    dx1, dxq, dxk, dxv, slot_x = _xattn_bwd(dx2, x1, norm_x_g, xq, xk, xv, wq_f, wo_t)
    dwo_t = _wgrad(dx2, att, "wgrad_xo")
    dwq = _wgrad(hq, dxq, "wgrad_xq")
    dwk, dwv, slot_mem = _mem_bwd(mem0, norm_mem_g, hm, dxk, dxv, wk_f, wv_f)
    dmix = _matmul_nt(dx1, wout_f, "out_proj_bwd")
    dwout = _wgrad(mixed, dx1, "wgrad_out")
    dz_pool, d_wpool, slot_ps = _pool_bwd(dmix, pooled, wp, pool_scale)
    dz, slot_lb, slot_gn = _hgrn_bwd(z, o_pre, dmix, states, lb_logits, gn, dz_pool)
    dwin_t = _wgrad(dz, h, "wgrad_in")
    grad_x, slot_mix = _in_bwd(dz, win_t, x0, norm_mix_g, dx1)

    rows = lambda t, r: t.reshape(N_DEV, r, D_MODEL)
    small = jnp.concatenate([slot_mix, slot_lb, slot_gn, slot_ps, slot_x, slot_mem, slot_ffn, slot_fin], axis=0)
    recv = _exchange_grads(
        [rows(dwin_t, 320), rows(dwout, 128), rows(dwq, 128), rows(dwk, 128), rows(dwv, 128), rows(dwo_t, 128),
         dw1, rows(dw2, FF_BLK)],
        [small, d_wpool])
    r_in, r_out, r_q, r_k, r_v, r_o, r_1, r_2, r_small, r_wpool = recv

    out = {}
    g_in = _sum_sources(r_in, "sum_grad_in").T
    out["w_in"] = (g_in, *_adamw(g_in, w_in[0], m_w_in[0], v_w_in[0], "adamw_in"))
    g_xo = _sum_sources(r_o, "sum_grad_xo").T
    out["w_xo"] = (g_xo, *_adamw(g_xo, xo_2d(w_xo), xo_2d(m_w_xo), xo_2d(v_w_xo), "adamw_xo"))
    out["w_out"] = _sum_adamw(r_out, w_out[0], m_w_out[0], v_w_out[0], "adamw_out")
    out["w_xq"] = _sum_adamw(r_q, heads_2d(w_xq), heads_2d(m_w_xq), heads_2d(v_w_xq), "adamw_xq")
    out["w_xk"] = _sum_adamw(r_k, heads_2d(w_xk), heads_2d(m_w_xk), heads_2d(v_w_xk), "adamw_xk")
    out["w_xv"] = _sum_adamw(r_v, heads_2d(w_xv), heads_2d(m_w_xv), heads_2d(v_w_xv), "adamw_xv")
    out["w_ff1"] = _sum_adamw(r_1, w_ff1[0], m_w_ff1[0], v_w_ff1[0], "adamw_ff1")
    out["w_ff2"] = _sum_adamw(r_2, w_ff2[0], m_w_ff2[0], v_w_ff2[0], "adamw_ff2")
    row = lambda t: t.reshape(1, -1)
    small_params = {
        "norm_mix_g": (norm_mix_g, m_norm_mix_g, v_norm_mix_g),
        "lb_logits": (lb_logits, m_lb_logits, v_lb_logits),
        "hgrn_norm_g": (hgrn_norm_g[0], m_hgrn_norm_g[0], v_hgrn_norm_g[0]),
        "pool_scale": (pool_scale, m_pool_scale, v_pool_scale),
        "norm_x_g": (norm_x_g, m_norm_x_g, v_norm_x_g),
        "norm_mem_g": (norm_mem_g, m_norm_mem_g, v_norm_mem_g),
        "norm_ffn_g": (norm_ffn_g, m_norm_ffn_g, v_norm_ffn_g),
        "final_norm_g": (row(final_norm_g), row(m_final_norm_g), row(v_final_norm_g)),
        "w_pool": (wp, m_w_pool[0], v_w_pool[0]),
    }
    loss, small_out = _small_update(r_small, r_wpool, small_params)
    out.update(small_out)

    shapes = dict(norm_mix_g=norm_mix_g, w_in=w_in, lb_logits=lb_logits, hgrn_norm_g=hgrn_norm_g, w_pool=w_pool,
                  pool_scale=pool_scale, w_out=w_out, norm_x_g=norm_x_g, norm_mem_g=norm_mem_g, w_xq=w_xq, w_xk=w_xk,
                  w_xv=w_xv, w_xo=w_xo, norm_ffn_g=norm_ffn_g, w_ff1=w_ff1, w_ff2=w_ff2, final_norm_g=final_norm_g)
    order = list(shapes)
    group = lambda k: [out[n][k].reshape(shapes[n].shape) for n in order]
    return (loss.reshape(()), grad_x.reshape(x.shape), *group(0), *group(1), *group(2), *group(3))
```

```python
import jax
import jax.numpy as jnp
from jax import lax
from jax.experimental import pallas as pl
from jax.experimental.pallas import tpu as pltpu

F32 = jnp.float32
BF16 = jnp.bfloat16

D_MODEL = 1024
N_DEV = 8
HEADS = 4
HD = 128
HW = HEADS * HD
IN_WIDTH = 5 * HW
XHD = 256
MEM_LEN = 256
D_FF = 4096
FF_BLK = D_FF // N_DEV
POOL_WINDOWS = (2, 4, 8, 16)
POOL_HALO = 16
CHUNK = 64
SUB = 16
N_SUB = CHUNK // SUB
EXP_CAP = 80.0
EPS = 1e-6
ROW_TILE = 512
SLOT = 8
V7X_VMEM_LIMIT = 56 * 1024 * 1024

ADAM_LR = 0.001
ADAM_B1 = 0.9
ADAM_B2 = 0.999
ADAM_EPS = 1e-08
ADAM_WD = 0.01
ADAM_STEP = 10

MESH_ID = pl.DeviceIdType.MESH


def _params(sem=None, vmem=V7X_VMEM_LIMIT):
    return pltpu.CompilerParams(dimension_semantics=sem, vmem_limit_bytes=vmem)


def _mm(a, b):
    return lax.dot_general(a.astype(BF16), b.astype(BF16), (((1,), (0,)), ((), ())), preferred_element_type=F32)


def _mm_nt(a, b):
    return lax.dot_general(a.astype(BF16), b.astype(BF16), (((1,), (1,)), ((), ())), preferred_element_type=F32)


def _mm_tn(a, b):
    return lax.dot_general(a.astype(BF16), b.astype(BF16), (((0,), (0,)), ((), ())), preferred_element_type=F32)


def _sigmoid(x):
    return 1.0 / (1.0 + jnp.exp(-x))


def _rms(x):
    return lax.rsqrt(jnp.mean(x * x, axis=-1, keepdims=True) + EPS)


def _rms_bwd(x, g, dh):
    r = _rms(x)
    n = x * r
    dn = dh * g
    dx = r * (dn - n * jnp.mean(dn * n, axis=-1, keepdims=True))
    return dx, jnp.sum(dh * n, axis=0, keepdims=True)


def _tri_dot(tri, x, passes):
    acc = None
    rest = x
    for _ in range(passes):
        piece = rest.astype(BF16)
        part = lax.dot_general(tri, piece, (((1,), (0,)), ((), ())), preferred_element_type=F32)
        acc = part if acc is None else acc + part
        rest = rest - piece.astype(F32)
    return acc


def _adam_update(g, w, m, v):
    nm = ADAM_B1 * m + (1.0 - ADAM_B1) * g
    nv = ADAM_B2 * v + (1.0 - ADAM_B2) * (g * g)
    m_hat = nm / (1.0 - ADAM_B1 ** ADAM_STEP)
    v_hat = nv / (1.0 - ADAM_B2 ** ADAM_STEP)
    return -ADAM_LR * (m_hat / (jnp.sqrt(v_hat) + ADAM_EPS) + ADAM_WD * w), nm, nv


def _full(shape):
    return pl.BlockSpec(shape, lambda *_: (0,) * len(shape))


VMEM_WHOLE = pl.BlockSpec(memory_space=pltpu.VMEM)
ANY_SPACE = pl.BlockSpec(memory_space=pl.ANY)


def _mesh_pos():
    return lax.axis_index("x"), lax.axis_index("y"), lax.axis_index("c")


def _flat(px, py, pc):
    return 4 * px + 2 * py + pc


def _all_gather_weights(shards, cast_only):
    n, nc = len(shards), len(cast_only)
    step = 64

    def body(*refs):
        x_refs, c_refs = refs[:n], refs[n:n + nc]
        out_refs, cast_refs = refs[n + nc:2 * n + nc], refs[2 * n + nc:2 * n + 2 * nc]
        bufs = refs[2 * n + 2 * nc:3 * n + 2 * nc]
        send_sems, recv_sems, local_sems = refs[3 * n + 2 * nc:]
        x, y, c = _mesh_pos()
        me, sibling = (x, y, c), (x, y, 1 - c)
        chips = [(1 - x, y), (x, 1 - y), (1 - x, 1 - y)]

        def copy(a, k, blk, to, src=None):
            rows = out_refs[a].at[_flat(*blk)]
            return pltpu.make_async_remote_copy(
                src_ref=rows if src is None else src, dst_ref=rows,
                send_sem=send_sems.at[7 * a + k], recv_sem=recv_sems.at[7 * a + k], device_id=to, device_id_type=MESH_ID)

        def cast_rows(src, dst, rows):
            def cast(i, carry):
                r0 = pl.multiple_of(i * step, step)
                dst[pl.ds(r0, step), :] = src[pl.ds(r0, step), :].astype(BF16)
                return carry
            lax.fori_loop(0, rows // step, cast, 0)

        first, mine = [], []
        for a in range(n):
            cast_rows(x_refs[a], bufs[a], shards[a].shape[0])
            mine.append(pltpu.make_async_copy(bufs[a], out_refs[a].at[_flat(*me)], local_sems.at[a]))
            first.append(copy(a, 0, me, sibling, src=bufs[a]))
            first += [copy(a, 1 + j, me, (*chip, c), src=bufs[a]) for j, chip in enumerate(chips)]
            for cp in [mine[-1]] + first[-4:]:
                cp.start()
        for a in range(nc):
            cast_rows(c_refs[a], cast_refs[a], cast_only[a].shape[0])
        passed = []
        for j, chip in enumerate(chips):
            for a in range(n):
                copy(a, 1 + j, (*chip, c), me).wait_recv()
                passed.append(copy(a, 4 + j, (*chip, c), sibling))
                passed[-1].start()
        for a in range(n):
            copy(a, 0, sibling, me).wait_recv()
            for j, chip in enumerate(chips):
                copy(a, 4 + j, (*chip, 1 - c), me).wait_recv()
        for cp in first + passed:
            cp.wait_send()
        for cp in mine:
            cp.wait()

    return pl.pallas_call(
        body, name="all_gather_w_in",
        out_shape=[jax.ShapeDtypeStruct((N_DEV,) + s.shape, BF16) for s in shards]
        + [jax.ShapeDtypeStruct(s.shape, BF16) for s in cast_only],
        in_specs=[VMEM_WHOLE] * (n + nc), out_specs=[ANY_SPACE] * n + [VMEM_WHOLE] * nc,
        scratch_shapes=[pltpu.VMEM(s.shape, BF16) for s in shards]
        + [pltpu.SemaphoreType.DMA((7 * n,)), pltpu.SemaphoreType.DMA((7 * n,)), pltpu.SemaphoreType.DMA((n,))],
        compiler_params=_params(),
    )(*shards, *cast_only)


HBM_SPEC = pl.BlockSpec(memory_space=pltpu.HBM)
SEM_SPEC = pl.BlockSpec(memory_space=pltpu.SEMAPHORE)
EFFECT = pltpu.SideEffectType.DATAFLOW_SIDE_EFFECTING
TOKEN = jax.ShapeDtypeStruct((8, 128), F32)


def _in_hbm(a):
    return pltpu.with_memory_space_constraint(a, pltpu.HBM)


def _split_start(copies_of, srcs, lands, n_sems, name):
    ns, nl, k = len(srcs), len(lands), len(n_sems)

    def body(*refs):
        src_refs, land_refs = refs[:ns], refs[ns:ns + nl]
        sems = refs[ns + nl:ns + nl + k]
        token = refs[-1]
        for cp in copies_of(src_refs, land_refs, sems):
            cp.start()
        token[...] = jnp.zeros_like(token)

    outs = pl.pallas_call(
        body, name=name,
        out_shape=[pltpu.SemaphoreType.DMA((q,)) for q in n_sems]
        + [pltpu.HBM(a.shape, a.dtype) for a in list(srcs) + list(lands)] + [TOKEN],
        in_specs=[HBM_SPEC] * (ns + nl),
        out_specs=[SEM_SPEC] * k + [HBM_SPEC] * (ns + nl) + [VMEM_WHOLE],
        input_output_aliases={i: k + i for i in range(ns + nl)},
        compiler_params=pltpu.CompilerParams(has_side_effects=EFFECT),
    )(*[_in_hbm(a) for a in list(srcs) + list(lands)])
    return outs[:k], outs[k:k + ns], outs[k + ns:k + ns + nl], outs[-1]


def _split_wait(copies_of, handle, after, name):
    sems, srcs, lands, _ = handle
    ns, nl, k = len(srcs), len(lands), len(sems)

    def body(*refs):
        src_refs, land_refs = refs[:ns], refs[ns:ns + nl]
        sem_refs = refs[ns + nl:ns + nl + k]
        for cp in copies_of(src_refs, land_refs, sem_refs):
            cp.wait()

    outs = pl.pallas_call(
        body, name=name,
        out_shape=[pltpu.HBM(a.shape, a.dtype) for a in list(srcs) + list(lands)],
        in_specs=[HBM_SPEC] * (ns + nl) + [SEM_SPEC] * k + [ANY_SPACE],
        out_specs=[HBM_SPEC] * (ns + nl),
        input_output_aliases={i: i for i in range(ns + nl)},
        compiler_params=pltpu.CompilerParams(has_side_effects=EFFECT),
    )(*srcs, *lands, *sems, after)
    return outs[ns:]


def _gather_first_copies(shard_refs, land_refs, sems):
    send_sems, recv_sems, local_sems = sems
    x, y, c = _mesh_pos()
    me = _flat(x, y, c)
    peers = [(x, y, 1 - c), (1 - x, y, c), (x, 1 - y, c), (1 - x, 1 - y, c)]
    copies = []
    for a, (shard, land) in enumerate(zip(shard_refs, land_refs)):
        copies.append(pltpu.make_async_copy(shard, land.at[me], local_sems.at[a]))
        for k, peer in enumerate(peers):
            copies.append(pltpu.make_async_remote_copy(
                src_ref=shard, dst_ref=land.at[me], send_sem=send_sems.at[4 * a + k], recv_sem=recv_sems.at[4 * a + k],
                device_id=peer, device_id_type=MESH_ID))
    return copies


def _gather_forward_copies(src_refs, land_refs, sems):
    del src_refs
    send_sems, recv_sems = sems
    x, y, c = _mesh_pos()
    chips = [(1 - x, y), (x, 1 - y), (1 - x, 1 - y)]
    copies = []
    for a, land in enumerate(land_refs):
        for j, chip in enumerate(chips):
            rows = land.at[_flat(*chip, c)]
            copies.append(pltpu.make_async_remote_copy(
                src_ref=rows, dst_ref=rows, send_sem=send_sems.at[3 * a + j], recv_sem=recv_sems.at[3 * a + j],
                device_id=(x, y, 1 - c), device_id_type=MESH_ID))
    return copies


def _gather_first_start(shards, name):
    lands = [lax.empty((N_DEV,) + s.shape, s.dtype) for s in shards]
    n = len(shards)
    return _split_start(_gather_first_copies, shards, lands, (4 * n, 4 * n, n), name)


def _gather_forward_start(lands, name):
    n = len(lands)
    return _split_start(_gather_forward_copies, [], lands, (3 * n, 3 * n), name)


def _all_to_all_copies(n_scattered):
    def copies_of(src_refs, land_refs, sems):
        send_sems, recv_sems, local_sems = sems
        x, y, c = _mesh_pos()
        me = _flat(x, y, c)
        copies = []
        for a, (src, land) in enumerate(zip(src_refs, land_refs)):
            scattered = a < n_scattered
            copies.append(pltpu.make_async_copy(src.at[me] if scattered else src, land.at[me], local_sems.at[a]))
            for k in range(1, N_DEV):
                peer = (1 - x if k & 4 else x, 1 - y if k & 2 else y, 1 - c if k & 1 else c)
                copies.append(pltpu.make_async_remote_copy(
                    src_ref=src.at[_flat(*peer)] if scattered else src, dst_ref=land.at[me],
                    send_sem=send_sems.at[7 * a + k - 1], recv_sem=recv_sems.at[7 * a + k - 1],
                    device_id=peer, device_id_type=MESH_ID))
        return copies
    return copies_of


def _all_to_all_start(scattered, broadcast, name):
    srcs = list(scattered) + list(broadcast)
    lands = [lax.empty(a.shape, a.dtype) for a in scattered] + [lax.empty((N_DEV,) + a.shape, a.dtype) for a in broadcast]
    n = len(srcs)
    return _split_start(_all_to_all_copies(len(scattered)), srcs, lands, (7 * n, 7 * n, n), name)


def _after(a, token):
    return a + token[0, 0]


def _row_tile(rows):
    for cand in (256, 128, 64, 32, 16):
        if rows % cand == 0:
            return cand
    return rows


def _sum_sources(recv, name):
    _, rows, cols = recv.shape
    tile = _row_tile(rows)

    def body(r_ref, o_ref):
        acc = r_ref[0].astype(F32)
        for d in range(1, N_DEV):
            acc = acc + r_ref[d].astype(F32)
        o_ref[...] = acc

    return pl.pallas_call(
        body, name=name, grid=(rows // tile,),
        out_shape=jax.ShapeDtypeStruct((rows, cols), F32),
        in_specs=[pl.BlockSpec((N_DEV, tile, cols), lambda i: (0, i, 0))],
        out_specs=pl.BlockSpec((tile, cols), lambda i: (i, 0)),
        compiler_params=_params(("parallel",)),
    )(recv)


def _adamw(g, w, m, v, name):
    rows, cols = g.shape
    tile = _row_tile(rows)

    def body(g_ref, w_ref, m_ref, v_ref, d_ref, nm_ref, nv_ref):
        d_ref[...], nm_ref[...], nv_ref[...] = _adam_update(g_ref[...], w_ref[...], m_ref[...], v_ref[...])

    spec = pl.BlockSpec((tile, cols), lambda i: (i, 0))
    shp = jax.ShapeDtypeStruct((rows, cols), F32)
    return pl.pallas_call(
        body, name=name, grid=(rows // tile,), out_shape=[shp, shp, shp],
        in_specs=[spec] * 4, out_specs=[spec] * 3,
        compiler_params=_params(("parallel",)),
    )(g, w, m, v)


def _sum_adamw(recv, w, m, v, name):
    _, rows, cols = recv.shape
    tile = _row_tile(rows)

    def body(r_ref, w_ref, m_ref, v_ref, g_ref, d_ref, nm_ref, nv_ref):
        acc = r_ref[0].astype(F32)
        for d in range(1, N_DEV):
            acc = acc + r_ref[d].astype(F32)
        g_ref[...] = acc
        d_ref[...], nm_ref[...], nv_ref[...] = _adam_update(acc, w_ref[...], m_ref[...], v_ref[...])

    spec = pl.BlockSpec((tile, cols), lambda i: (i, 0))
    shp = jax.ShapeDtypeStruct((rows, cols), F32)
    return pl.pallas_call(
        body, name=name, grid=(rows // tile,), out_shape=[shp] * 4,
        in_specs=[pl.BlockSpec((N_DEV, tile, cols), lambda i: (0, i, 0)), spec, spec, spec], out_specs=[spec] * 4,
        compiler_params=_params(("parallel",)),
    )(recv, w, m, v)


SMALL_SLOTS = {"norm_x_g": (0, 0, 1, D_MODEL), "norm_mem_g": (0, 8, 1, D_MODEL), "norm_ffn_g": (0, 16, 1, D_MODEL),
               "final_norm_g": (0, 24, 1, D_MODEL), "pool_scale": (0, 32, 1, HW),
               "norm_mix_g": (1, 0, 1, D_MODEL), "lb_logits": (1, 8, 2, HW), "hgrn_norm_g": (1, 16, HEADS, HD)}
LOSS_ROW = 25
SMALL_ORDER = ("norm_mix_g", "lb_logits", "hgrn_norm_g", "pool_scale", "norm_x_g", "norm_mem_g", "norm_ffn_g",
               "final_norm_g", "w_pool")


def _small_update(srecv0, srecv1, wprecv, params):
    flat = [t for n in SMALL_ORDER for t in params[n]]
    n_in = 3 + len(flat)

    def body(*refs):
        s_refs, wp_ref = refs[0:2], refs[2]
        in_refs = refs[3:n_in]
        loss_ref = refs[n_in]
        out_refs = refs[n_in + 1:-2]
        accs = refs[-2:]
        for s_ref, acc in zip(s_refs, accs):
            total = s_ref[0]
            for d in range(1, N_DEV):
                total = total + s_ref[d]
            acc[...] = total
        loss_ref[...] = accs[0][LOSS_ROW:LOSS_ROW + 1, 0:1]
        for i, name in enumerate(SMALL_ORDER):
            w_ref, m_ref, v_ref = in_refs[3 * i:3 * i + 3]
            g_ref, d_ref, nm_ref, nv_ref = out_refs[4 * i:4 * i + 4]
            if name == "w_pool":
                g = wp_ref[0]
                for d in range(1, N_DEV):
                    g = g + wp_ref[d]
            else:
                buf, r0, nr, nc = SMALL_SLOTS[name]
                g = accs[buf][r0:r0 + nr, 0:nc]
            g_ref[...] = g
            d_ref[...], nm_ref[...], nv_ref[...] = _adam_update(g, w_ref[...], m_ref[...], v_ref[...])

    out_shape = [jax.ShapeDtypeStruct((1, 1), F32)]
    for n in SMALL_ORDER:
        out_shape += [jax.ShapeDtypeStruct(params[n][0].shape, F32)] * 4
    outs = pl.pallas_call(
        body, name="small_update", out_shape=out_shape,
        in_specs=[VMEM_WHOLE] * n_in, out_specs=[VMEM_WHOLE] * len(out_shape),
        scratch_shapes=[pltpu.VMEM(srecv0.shape[1:], F32), pltpu.VMEM(srecv1.shape[1:], F32)],
        compiler_params=_params(),
    )(srecv0, srecv1, wprecv, *flat)
    return outs[0], {n: outs[1 + 4 * i:5 + 4 * i] for i, n in enumerate(SMALL_ORDER)}


def _in_proj(x, g, w_t):
    s = x.shape[0]
    tm = min(ROW_TILE, s)

    def body(x_ref, g_ref, w_ref, z_ref, h_ref):
        xv = x_ref[...]
        h = (xv * _rms(xv) * g_ref[...]).astype(BF16)
        h_ref[...] = h
        z_ref[...] = _mm_nt(h, w_ref[...])

    return pl.pallas_call(
        body, name="in_proj", grid=(s // tm,),
        out_shape=[jax.ShapeDtypeStruct((s, IN_WIDTH), F32), jax.ShapeDtypeStruct((s, D_MODEL), BF16)],
        in_specs=[pl.BlockSpec((tm, D_MODEL), lambda i: (i, 0)), _full((1, D_MODEL)), VMEM_WHOLE],
        out_specs=[pl.BlockSpec((tm, IN_WIDTH), lambda i: (i, 0)), pl.BlockSpec((tm, D_MODEL), lambda i: (i, 0))],
        compiler_params=_params(("parallel",)),
    )(x, g, w_t)


def _chunk_masks():
    row = lax.broadcasted_iota(jnp.int32, (CHUNK, CHUNK), 0)
    col = lax.broadcasted_iota(jnp.int32, (CHUNK, CHUNK), 1)
    return row, col


def _ones_where(mask):
    return jnp.where(mask, 1.0, 0.0).astype(BF16)


def _hgrn_gates(zq, zf, lb):
    sq = _sigmoid(zq)
    sig = _sigmoid(zf)
    f = lb + (1.0 - lb) * sig
    return zq * sq, sq, sig, f


def _hgrn_intra_factors(b_scr, bh, qh, kh, sl):
    trow = lax.broadcasted_iota(jnp.int32, (CHUNK, HD), 0)
    eq, ek = [], []
    for j in range(N_SUB):
        if j == 0:
            base = jnp.zeros((1, HD), F32)
        else:
            base = b_scr[SUB * j - 1:SUB * j, sl]
        in_j = (trow >= SUB * j) & (trow < SUB * (j + 1))
        eq.append(jnp.where(in_j, jnp.exp(bh - base), 0.0))
        ek.append(jnp.where(trow < SUB * (j + 1), jnp.exp(jnp.minimum(base - bh, EXP_CAP)), 0.0))
    eqcat = jnp.concatenate(eq, axis=1)
    ekcat = jnp.concatenate(ek, axis=1)
    qcat = jnp.concatenate([qh] * N_SUB, axis=1) * eqcat
    kecat = jnp.concatenate([kh] * N_SUB, axis=1) * ekcat
    return qcat, kecat, eqcat, ekcat


def _sum_lane_blocks(a):
    out = a[:, 0:HD]
    for j in range(1, N_SUB):
        out = out + a[:, HD * j:HD * (j + 1)]
    return out


def _hgrn_fwd(z, lb_logits, gn):
    s = z.shape[0]
    n_chunks = s // CHUNK

    def body(zq_ref, zf_ref, zi_ref, zg_ref, lbl_ref, gn_ref, oa_ref, o_ref, st_ref, state, b_scr):
        @pl.when(pl.program_id(0) == 0)
        def _():
            state[...] = jnp.zeros_like(state)

        st_ref[0] = state[...]
        lb = _sigmoid(lbl_ref[0:1, :] - lbl_ref[1:2, :])
        q, _, _, f = _hgrn_gates(zq_ref[...], zf_ref[...], lb)
        kk = 1.0 - f
        row, col = _chunk_masks()
        causal = col <= row
        b_scr[...] = _tri_dot(_ones_where(causal), jnp.log(f), 3)
        for h in range(HEADS):
            sl = slice(HD * h, HD * (h + 1))
            bh = b_scr[:, sl]
            qh, kh, vh = q[:, sl], kk[:, sl], zi_ref[:, sl]
            st = state[h]
            b_last = b_scr[CHUNK - 1:CHUNK, sl]
            qcat, kecat, _, _ = _hgrn_intra_factors(b_scr, bh, qh, kh, sl)
            a = jnp.where(causal, _mm_nt(qcat, kecat), 0.0)
            o = _mm(a, vh) + _mm_nt(qh * jnp.exp(bh), st)
            state[h] = st * jnp.exp(b_last) + _mm_tn(vh, kh * jnp.exp(b_last - bh))
            o_ref[:, sl] = o
            zg = zg_ref[:, sl]
            oa_ref[:, sl] = (o * _rms(o) * gn_ref[h:h + 1, :] * zg * _sigmoid(zg)).astype(BF16)

    zspec = lambda cb: pl.BlockSpec((CHUNK, HW), lambda i, cb=cb: (i, cb))
    return pl.pallas_call(
        body, name="hgrn_fwd", grid=(n_chunks,),
        out_shape=[jax.ShapeDtypeStruct((s, 2 * HW), BF16), jax.ShapeDtypeStruct((s, HW), F32),
                   jax.ShapeDtypeStruct((n_chunks, HEADS, HD, HD), F32)],
        in_specs=[zspec(0), zspec(1), zspec(2), zspec(3), _full((2, HW)), _full((HEADS, HD))],
        out_specs=[pl.BlockSpec((CHUNK, HW), lambda i: (i, 0)), pl.BlockSpec((CHUNK, HW), lambda i: (i, 0)),
                   pl.BlockSpec((1, HEADS, HD, HD), lambda i: (i, 0, 0, 0))],
        scratch_shapes=[pltpu.VMEM((HEADS, HD, HD), F32), pltpu.VMEM((CHUNK, HW), F32)],
        compiler_params=_params(("arbitrary",)),
    )(z, z, z, z, lb_logits, gn)


def _pool_counts(tile_idx, tm):
    t = tile_idx * tm + lax.broadcasted_iota(jnp.int32, (tm, 1), 0)
    return [1.0 / jnp.minimum(t + 1, w).astype(F32) for w in POOL_WINDOWS]


def _pool_fwd(z, w_pool, scale, mixed_in):
    s = z.shape[0]
    tm = min(ROW_TILE, s)

    def body(p_ref, w_ref, sc_ref, mixin_ref, ob_ref, pooled_ref, ext):
        i = pl.program_id(0)

        @pl.when(i == 0)
        def _():
            ext[0:POOL_HALO, :] = jnp.zeros((POOL_HALO, HW), F32)

        @pl.when(i > 0)
        def _():
            ext[0:POOL_HALO, :] = ext[tm:tm + POOL_HALO, :]

        ext[POOL_HALO:POOL_HALO + tm, :] = p_ref[...]
        inv = _pool_counts(i, tm)
        for g, w in enumerate(POOL_WINDOWS):
            sl = slice(HD * g, HD * (g + 1))
            p = ext[POOL_HALO:POOL_HALO + tm, sl]
            win = p
            for d in range(1, w):
                win = win + ext[POOL_HALO - d:POOL_HALO - d + tm, sl]
            pooled = (win * inv[g] - p).astype(BF16)
            pooled_ref[:, sl] = pooled
            ob_ref[:, sl] = (_mm(pooled, w_ref[g]) * sc_ref[:, sl]).astype(BF16)

    return pl.pallas_call(
        body, name="pool_fwd", grid=(s // tm,),
        out_shape=[jax.ShapeDtypeStruct((s, 2 * HW), BF16), jax.ShapeDtypeStruct((s, HW), BF16)],
        in_specs=[pl.BlockSpec((tm, HW), lambda i: (i, 4)), _full((HEADS, HD, HD)), _full((1, HW)), ANY_SPACE],
        out_specs=[pl.BlockSpec((tm, HW), lambda i: (i, 1)), pl.BlockSpec((tm, HW), lambda i: (i, 0))],
        scratch_shapes=[pltpu.VMEM((tm + POOL_HALO, HW), F32)],
        input_output_aliases={3: 0},
        compiler_params=_params(("arbitrary",)),
    )(z, w_pool, scale, mixed_in)


def _out_proj(x, mixed, w_out):
    s = x.shape[0]
    tm = min(ROW_TILE, s)

    def body(x_ref, a_ref, w_ref, o_ref):
        o_ref[...] = x_ref[...] + _mm(a_ref[...], w_ref[...])

    return pl.pallas_call(
        body, name="out_proj", grid=(s // tm,),
        out_shape=jax.ShapeDtypeStruct((s, D_MODEL), F32),
        in_specs=[pl.BlockSpec((tm, D_MODEL), lambda i: (i, 0)), pl.BlockSpec((tm, D_MODEL), lambda i: (i, 0)), VMEM_WHOLE],
        out_specs=pl.BlockSpec((tm, D_MODEL), lambda i: (i, 0)),
        compiler_params=_params(("parallel",)),
    )(x, mixed, w_out)


def _mem_kv(mem, g, wk, wv):
    def body(m_ref, g_ref, wk_ref, wv_ref, hm_ref, k_ref, v_ref):
        m = m_ref[...]
        hm = (m * _rms(m) * g_ref[...]).astype(BF16)
        hm_ref[...] = hm
        k_ref[...] = _mm(hm, wk_ref[...]).astype(BF16)
        v_ref[...] = _mm(hm, wv_ref[...]).astype(BF16)

    shp = jax.ShapeDtypeStruct((MEM_LEN, D_MODEL), BF16)
    return pl.pallas_call(
        body, name="mem_kv", out_shape=[shp, shp, shp],
        in_specs=[VMEM_WHOLE] * 4, out_specs=[VMEM_WHOLE] * 3,
        compiler_params=_params(),
    )(mem, g, wk, wv)


def _softmax_rows(sc):
    e = jnp.exp(sc - jnp.max(sc, axis=-1, keepdims=True))
    return e / jnp.sum(e, axis=-1, keepdims=True)


def _xattn_fwd(x, g, wq, xk, xv, wo_t):
    s = x.shape[0]
    tm = min(ROW_TILE, s)
    scale = XHD ** -0.5

    def body(x_ref, g_ref, wq_ref, k_ref, v_ref, wo_ref, o_ref, hq_ref, q_ref, att_ref):
        xv_ = x_ref[...]
        hq = (xv_ * _rms(xv_) * g_ref[...]).astype(BF16)
        hq_ref[...] = hq
        q_ref[...] = (_mm(hq, wq_ref[...]) * scale).astype(BF16)
        for h in range(HEADS):
            sl = slice(XHD * h, XHD * (h + 1))
            p = _softmax_rows(_mm_nt(q_ref[:, sl], k_ref[:, sl]))
            att_ref[:, sl] = _mm(p, v_ref[:, sl]).astype(BF16)
        o_ref[...] = xv_ + _mm_nt(att_ref[...], wo_ref[...])

    row_f32 = pl.BlockSpec((tm, D_MODEL), lambda i: (i, 0))
    bshape = jax.ShapeDtypeStruct((s, D_MODEL), BF16)
    return pl.pallas_call(
        body, name="xattn_fwd", grid=(s // tm,),
        out_shape=[jax.ShapeDtypeStruct((s, D_MODEL), F32), bshape, bshape, bshape],
        in_specs=[row_f32, _full((1, D_MODEL)), VMEM_WHOLE, VMEM_WHOLE, VMEM_WHOLE, VMEM_WHOLE],
        out_specs=[row_f32] * 4,
        compiler_params=_params(("parallel",)),
    )(x, g, wq, xk, xv, wo_t)


def _mlp_fwd_loss(x, g, w1, w2, gf, target):
    s = x.shape[0]
    tm = min(ROW_TILE, s)

    def body(x_ref, g_ref, w1_ref, w2_ref, gf_ref, t_ref, dx_ref, u_ref, hf_ref, slot_ref):
        @pl.when(pl.program_id(0) == 0)
        def _():
            slot_ref[...] = jnp.zeros_like(slot_ref)

        xv = x_ref[...]
        hf = (xv * _rms(xv) * g_ref[...]).astype(BF16)
        hf_ref[...] = hf
        acc = xv
        for j in range(N_DEV):
            a = jnp.maximum(_mm(hf, w1_ref[j]), 0.0)
            u = (a * a).astype(BF16)
            u_ref[:, FF_BLK * j:FF_BLK * (j + 1)] = u
            acc = acc + _mm(u, w2_ref[j])
        gfv = gf_ref[...]
        r = _rms(acc)
        n = acc * r
        err = n * gfv - t_ref[...]
        slot_ref[1:2, :] += jnp.sum(jnp.mean(err * err, axis=-1, keepdims=True), axis=0, keepdims=True) * 0.5
        dy = err * (1.0 / D_MODEL)
        slot_ref[0:1, :] += jnp.sum(dy * n, axis=0, keepdims=True)
        dn = dy * gfv
        dx_ref[...] = r * (dn - n * jnp.mean(dn * n, axis=-1, keepdims=True))

    row_f32 = pl.BlockSpec((tm, D_MODEL), lambda i: (i, 0))
    return pl.pallas_call(
        body, name="mlp_fwd_loss", grid=(s // tm,),
        out_shape=[jax.ShapeDtypeStruct((s, D_MODEL), F32), jax.ShapeDtypeStruct((s, D_FF), BF16),
                   jax.ShapeDtypeStruct((s, D_MODEL), BF16), jax.ShapeDtypeStruct((SLOT, D_MODEL), F32)],
        in_specs=[row_f32, _full((1, D_MODEL)), VMEM_WHOLE, VMEM_WHOLE, _full((1, D_MODEL)), row_f32],
        out_specs=[row_f32, pl.BlockSpec((tm, D_FF), lambda i: (i, 0)), row_f32, _full((SLOT, D_MODEL))],
        compiler_params=_params(("arbitrary",)),
    )(x, g, w1, w2, gf, target)


def _zero_slot(slot_ref):
    @pl.when(pl.program_id(0) == 0)
    def _():
        slot_ref[...] = jnp.zeros_like(slot_ref)


def _mlp_bwd(dx3, u, x2, g, w1, w2):
    s = x2.shape[0]
    tm = min(ROW_TILE, s)

    def body(d_ref, u_ref, x_ref, g_ref, w1_ref, w2_ref, da_ref, dx_ref, slot_ref):
        _zero_slot(slot_ref)
        d = d_ref[...]
        d16 = d.astype(BF16)
        dhf = jnp.zeros((tm, D_MODEL), F32)
        for j in range(N_DEV):
            sl = slice(FF_BLK * j, FF_BLK * (j + 1))
            da = (_mm_nt(d16, w2_ref[j]) * (2.0 * jnp.sqrt(u_ref[:, sl].astype(F32)))).astype(BF16)
            da_ref[:, sl] = da
            dhf = dhf + _mm_nt(da, w1_ref[j])
        dx, dg = _rms_bwd(x_ref[...], g_ref[...], dhf)
        dx_ref[...] = d + dx
        slot_ref[0:1, :] += dg

    row_f32 = pl.BlockSpec((tm, D_MODEL), lambda i: (i, 0))
    return pl.pallas_call(
        body, name="mlp_bwd", grid=(s // tm,),
        out_shape=[jax.ShapeDtypeStruct((s, D_FF), BF16), jax.ShapeDtypeStruct((s, D_MODEL), F32),
                   jax.ShapeDtypeStruct((SLOT, D_MODEL), F32)],
        in_specs=[row_f32, pl.BlockSpec((tm, D_FF), lambda i: (i, 0)), row_f32, _full((1, D_MODEL)),
                  VMEM_WHOLE, VMEM_WHOLE],
        out_specs=[pl.BlockSpec((tm, D_FF), lambda i: (i, 0)), row_f32, _full((SLOT, D_MODEL))],
        compiler_params=_params(("arbitrary",)),
    )(dx3, u, x2, g, w1, w2)


def _wgrad(a, b, name, col_blocks=False):
    s, m = a.shape
    n = b.shape[1]
    tm = 1280 if m % 1280 == 0 else min(1024, m)
    tn = n // N_DEV if col_blocks else min(1024, n)
    ts = min(ROW_TILE, s)
    n_s = s // ts

    def body(a_ref, b_ref, o_ref, acc):
        k = pl.program_id(2)

        @pl.when(k == 0)
        def _():
            acc[...] = jnp.zeros_like(acc)

        acc[...] += _mm_tn(a_ref[...], b_ref[...])

        @pl.when(k == n_s - 1)
        def _():
            o_ref[...] = acc[...].astype(BF16)

    if col_blocks:
        out_shape = jax.ShapeDtypeStruct((N_DEV, m, tn), BF16)
        out_spec = pl.BlockSpec((None, tm, tn), lambda i, j, k: (j, i, 0))
    else:
        out_shape = jax.ShapeDtypeStruct((m, n), BF16)
        out_spec = pl.BlockSpec((tm, tn), lambda i, j, k: (i, j))
    return pl.pallas_call(
        body, name=name, grid=(m // tm, n // tn, n_s), out_shape=out_shape,
        in_specs=[pl.BlockSpec((ts, tm), lambda i, j, k: (k, i)), pl.BlockSpec((ts, tn), lambda i, j, k: (k, j))],
        out_specs=out_spec,
        scratch_shapes=[pltpu.VMEM((tm, tn), F32)],
        compiler_params=_params(("parallel", "parallel", "arbitrary")),
    )(a, b)


def _xattn_bwd(dx2, x1, g, q, xk, xv, wq, wo_t):
    s = x1.shape[0]
    tm = min(ROW_TILE, s)
    scale = XHD ** -0.5

    def body(d_ref, x_ref, g_ref, q_ref, k_ref, v_ref, wq_ref, wo_ref, dx_ref, dq_ref, dk_ref, dv_ref, slot_ref, datt):
        _zero_slot(slot_ref)

        @pl.when(pl.program_id(0) == 0)
        def _():
            dk_ref[...] = jnp.zeros_like(dk_ref)
            dv_ref[...] = jnp.zeros_like(dv_ref)

        d = d_ref[...]
        datt[...] = _mm(d, wo_ref[...]).astype(BF16)
        for h in range(HEADS):
            sl = slice(XHD * h, XHD * (h + 1))
            qh, kh, vh, dah = q_ref[:, sl], k_ref[:, sl], v_ref[:, sl], datt[:, sl]
            p = _softmax_rows(_mm_nt(qh, kh))
            dp = _mm_nt(dah, vh)
            ds = (p * (dp - jnp.sum(dp * p, axis=-1, keepdims=True))).astype(BF16)
            dq_ref[:, sl] = (_mm(ds, kh) * scale).astype(BF16)
            dk_ref[:, sl] += _mm_tn(ds, qh)
            dv_ref[:, sl] += _mm_tn(p, dah)
        dx, dg = _rms_bwd(x_ref[...], g_ref[...], _mm_nt(dq_ref[...], wq_ref[...]))
        dx_ref[...] = d + dx
        slot_ref[0:1, :] += dg

    row_f32 = pl.BlockSpec((tm, D_MODEL), lambda i: (i, 0))
    kv = jax.ShapeDtypeStruct((MEM_LEN, D_MODEL), F32)
    return pl.pallas_call(
        body, name="xattn_bwd", grid=(s // tm,),
        out_shape=[jax.ShapeDtypeStruct((s, D_MODEL), F32), jax.ShapeDtypeStruct((s, D_MODEL), BF16), kv, kv,
                   jax.ShapeDtypeStruct((SLOT, D_MODEL), F32)],
        in_specs=[row_f32, row_f32, _full((1, D_MODEL)), row_f32, VMEM_WHOLE, VMEM_WHOLE, VMEM_WHOLE, VMEM_WHOLE],
        out_specs=[row_f32, row_f32, _full((MEM_LEN, D_MODEL)), _full((MEM_LEN, D_MODEL)), _full((SLOT, D_MODEL))],
        scratch_shapes=[pltpu.VMEM((tm, D_MODEL), BF16)],
        compiler_params=_params(("arbitrary",)),
    )(dx2, x1, g, q, xk, xv, wq, wo_t)


def _mem_bwd(mem, g, hm, dxk, dxv, wk, wv):
    def body(m_ref, g_ref, hm_ref, dk_ref, dv_ref, wk_ref, wv_ref, dwk_ref, dwv_ref, slot_ref):
        dk, dv = dk_ref[...], dv_ref[...]
        hm_ = hm_ref[...]
        dwk_ref[...] = _mm_tn(hm_, dk).astype(BF16)
        dwv_ref[...] = _mm_tn(hm_, dv).astype(BF16)
        _, dg = _rms_bwd(m_ref[...], g_ref[...], _mm_nt(dk, wk_ref[...]) + _mm_nt(dv, wv_ref[...]))
        slot_ref[...] = jnp.zeros_like(slot_ref)
        slot_ref[0:1, :] = dg

    wshape = jax.ShapeDtypeStruct((D_MODEL, D_MODEL), BF16)
    return pl.pallas_call(
        body, name="mem_bwd", out_shape=[wshape, wshape, jax.ShapeDtypeStruct((SLOT, D_MODEL), F32)],
        in_specs=[VMEM_WHOLE] * 7, out_specs=[VMEM_WHOLE] * 3,
        compiler_params=_params(),
    )(mem, g, hm, dxk, dxv, wk, wv)


def _matmul_nt(a, w, name, dep):
    s, k = a.shape
    n = w.shape[0]
    tm = min(ROW_TILE, s)

    def body(a_ref, w_ref, dep_ref, o_ref):
        o_ref[...] = _mm_nt(a_ref[...], w_ref[...])

    return pl.pallas_call(
        body, name=name, grid=(s // tm,),
        out_shape=jax.ShapeDtypeStruct((s, n), F32),
        in_specs=[pl.BlockSpec((tm, k), lambda i: (i, 0)), VMEM_WHOLE, ANY_SPACE],
        out_specs=pl.BlockSpec((tm, n), lambda i: (i, 0)),
        compiler_params=_params(("parallel",)),
    )(a, w, dep)


def _pool_bwd(dmix, pooled, w_pool, scale):
    s = dmix.shape[0]
    tm = min(ROW_TILE, s)
    n_t = s // tm

    def body(do_ref, pl_ref, w_ref, sc_ref, dz_ref, dw_ref, slot_ref, ext):
        i = pl.program_id(0)
        tile = n_t - 1 - i
        _zero_slot(slot_ref)

        @pl.when(i == 0)
        def _():
            dw_ref[...] = jnp.zeros_like(dw_ref)
            ext[tm:tm + POOL_HALO, :] = jnp.zeros((POOL_HALO, HW), F32)

        @pl.when(i > 0)
        def _():
            ext[tm:tm + POOL_HALO, :] = ext[0:POOL_HALO, :]

        inv = _pool_counts(tile, tm)
        dpooled = []
        for g in range(HEADS):
            sl = slice(HD * g, HD * (g + 1))
            pooled_g = pl_ref[:, sl]
            do = do_ref[:, sl]
            slot_ref[0:1, sl] += jnp.sum(_mm(pooled_g, w_ref[g]) * do, axis=0, keepdims=True)
            dy = (do * sc_ref[:, sl]).astype(BF16)
            dw_ref[g] += _mm_tn(pooled_g, dy)
            dpo = _mm_nt(dy, w_ref[g])
            dpooled.append(dpo)
            ext[0:tm, sl] = dpo * inv[g]
        for g, w in enumerate(POOL_WINDOWS):
            sl = slice(HD * g, HD * (g + 1))
            win = ext[0:tm, sl]
            for d in range(1, w):
                win = win + ext[d:d + tm, sl]
            dz_ref[:, sl] = win - dpooled[g]

    return pl.pallas_call(
        body, name="pool_bwd", grid=(n_t,),
        out_shape=[jax.ShapeDtypeStruct((s, IN_WIDTH), F32), jax.ShapeDtypeStruct((HEADS, HD, HD), F32),
                   jax.ShapeDtypeStruct((SLOT, D_MODEL), F32)],
        in_specs=[pl.BlockSpec((tm, HW), lambda i: (n_t - 1 - i, 1)), pl.BlockSpec((tm, HW), lambda i: (n_t - 1 - i, 0)),
                  _full((HEADS, HD, HD)), _full((1, HW))],
        out_specs=[pl.BlockSpec((tm, HW), lambda i: (n_t - 1 - i, 4)), _full((HEADS, HD, HD)), _full((SLOT, D_MODEL))],
        scratch_shapes=[pltpu.VMEM((tm + POOL_HALO, HW), F32)],
        compiler_params=_params(("arbitrary",)),
    )(dmix, pooled, w_pool, scale)


def _hgrn_bwd(z, o, dmix, states, lb_logits, gn, dz_in):
    s = z.shape[0]
    n_chunks = s // CHUNK

    def body(zq_ref, zf_ref, zi_ref, zg_ref, o_ref, do_ref, st_ref, lbl_ref, gn_ref, dzin_ref,
             dz_ref, dlb_ref, dgn_ref, dstate, b_scr, dlb_acc):
        i = pl.program_id(0)

        @pl.when(i == 0)
        def _():
            dstate[...] = jnp.zeros_like(dstate)
            dlb_acc[...] = jnp.zeros_like(dlb_acc)
            dgn_ref[...] = jnp.zeros_like(dgn_ref)
            dlb_ref[...] = jnp.zeros_like(dlb_ref)

        lb = _sigmoid(lbl_ref[0:1, :] - lbl_ref[1:2, :])
        zq = zq_ref[...]
        q, sq, sig, f = _hgrn_gates(zq, zf_ref[...], lb)
        kk = 1.0 - f
        row, col = _chunk_masks()
        causal = col <= row
        upper = _ones_where(col >= row)
        strict_lower = _ones_where(col < row)
        b_scr[...] = _tri_dot(_ones_where(causal), jnp.log(f), 3)
        for h in range(HEADS):
            sl = slice(HD * h, HD * (h + 1))
            oh = o_ref[:, sl]
            gnh = gn_ref[h:h + 1, :]
            zg = zg_ref[:, sl]
            sg = _sigmoid(zg)
            doa = do_ref[:, sl]
            don = doa * (zg * sg)
            d_o, dgn = _rms_bwd(oh, gnh, don)
            dgn_ref[h:h + 1, 0:HD] += dgn
            dz_ref[:, 3 * HW + HD * h:3 * HW + HD * (h + 1)] = doa * (oh * _rms(oh) * gnh) * (sg * (1.0 + zg * (1.0 - sg)))
            bh = b_scr[:, sl]
            qh, kh, vh = q[:, sl], kk[:, sl], zi_ref[:, sl]
            st0 = st_ref[0, h]
            ds1 = dstate[h]
            b_last = b_scr[CHUNK - 1:CHUNK, sl]
            lam = jnp.exp(bh)
            e_last = jnp.exp(b_last - bh)
            lam_last = jnp.exp(b_last)
            qcat, kecat, eqcat, ekcat = _hgrn_intra_factors(b_scr, bh, qh, kh, sl)
            a = jnp.where(causal, _mm_nt(qcat, kecat), 0.0)
            da = jnp.where(causal, _mm_nt(d_o, vh), 0.0)
            dz_ref[:, 2 * HW + HD * h:2 * HW + HD * (h + 1)] = _mm_tn(a, d_o) + _mm_nt(kh * e_last, ds1)
            q16, ke16 = qcat.astype(BF16), kecat.astype(BF16)
            gq = _mm(da, ke16)
            gk = _mm_tn(da, q16)
            dq_inter = lam * _mm(d_o, st0)
            dq = _sum_lane_blocks(eqcat * gq) + dq_inter
            dk_intra = _sum_lane_blocks(ekcat * gk)
            dk_state = _mm(vh, ds1) * e_last
            state_term = lam_last * jnp.sum(st0 * ds1, axis=0, keepdims=True)
            dstate[h] = ds1 * lam_last + _mm_tn(d_o, qh * lam)
            db_intra = _sum_lane_blocks(q16.astype(F32) * gq - ke16.astype(F32) * gk)
            dlf = (_tri_dot(upper, db_intra + qh * dq_inter, 2) + _tri_dot(strict_lower, kh * dk_state, 2)
                   + state_term)
            sigh = sig[:, sl]
            df = dlf / f[:, sl] - (dk_intra + dk_state)
            dlb_acc[:, sl] += jnp.sum(df * (1.0 - sigh), axis=0, keepdims=True)
            dz_ref[:, HW + HD * h:HW + HD * (h + 1)] = df * (1.0 - lb[:, sl]) * sigh * (1.0 - sigh)
            sqh = sq[:, sl]
            dz_ref[:, sl] = dq * (sqh * (1.0 + zq[:, sl] * (1.0 - sqh)))

        @pl.when(i == n_chunks - 1)
        def _():
            dl0 = dlb_acc[...] * lb * (1.0 - lb)
            dlb_ref[0:1, 0:HW] = dl0
            dlb_ref[1:2, 0:HW] = -dl0

    rev = lambda i: n_chunks - 1 - i
    zspec = lambda cb: pl.BlockSpec((CHUNK, HW), lambda i, cb=cb: (rev(i), cb))
    slot = jax.ShapeDtypeStruct((SLOT, D_MODEL), F32)
    return pl.pallas_call(
        body, name="hgrn_bwd", grid=(n_chunks,),
        out_shape=[jax.ShapeDtypeStruct((s, IN_WIDTH), F32), slot, slot],
        in_specs=[zspec(0), zspec(1), zspec(2), zspec(3), pl.BlockSpec((CHUNK, HW), lambda i: (rev(i), 0)),
                  pl.BlockSpec((CHUNK, HW), lambda i: (rev(i), 0)),
                  pl.BlockSpec((1, HEADS, HD, HD), lambda i: (rev(i), 0, 0, 0)), _full((2, HW)), _full((HEADS, HD)), ANY_SPACE],
        out_specs=[pl.BlockSpec((CHUNK, 4 * HW), lambda i: (rev(i), 0)), _full((SLOT, D_MODEL)), _full((SLOT, D_MODEL))],
        scratch_shapes=[pltpu.VMEM((HEADS, HD, HD), F32), pltpu.VMEM((CHUNK, HW), F32), pltpu.VMEM((1, HW), F32)],
        input_output_aliases={9: 0},
        compiler_params=_params(("arbitrary",)),
    )(z, z, z, z, o, dmix, states, lb_logits, gn, dz_in)


def _in_bwd(dz, w_t, x0, g, dx1):
    s = x0.shape[0]
    tm = min(ROW_TILE, s)

    def body(dz_ref, w_ref, x_ref, g_ref, d_ref, dx_ref, slot_ref):
        _zero_slot(slot_ref)
        dx, dg = _rms_bwd(x_ref[...], g_ref[...], _mm(dz_ref[...], w_ref[...]))
        dx_ref[...] = d_ref[...] + dx
        slot_ref[0:1, :] += dg

    row_f32 = pl.BlockSpec((tm, D_MODEL), lambda i: (i, 0))
    return pl.pallas_call(
        body, name="in_bwd", grid=(s // tm,),
        out_shape=[jax.ShapeDtypeStruct((s, D_MODEL), F32), jax.ShapeDtypeStruct((SLOT, D_MODEL), F32)],
        in_specs=[pl.BlockSpec((tm, IN_WIDTH), lambda i: (i, 0)), VMEM_WHOLE, row_f32, _full((1, D_MODEL)), row_f32],
        out_specs=[row_f32, _full((SLOT, D_MODEL))],
        compiler_params=_params(("arbitrary",)),
    )(dz, w_t, x0, g, dx1)


def kernel(x, mem, norm_mix_g, w_in, lb_logits, hgrn_norm_g, w_pool, pool_scale, w_out, norm_x_g, norm_mem_g, w_xq, w_xk, w_xv, w_xo, norm_ffn_g, w_ff1, w_ff2, final_norm_g, loss_target, m_norm_mix_g, m_w_in, m_lb_logits, m_hgrn_norm_g, m_w_pool, m_pool_scale, m_w_out, m_norm_x_g, m_norm_mem_g, m_w_xq, m_w_xk, m_w_xv, m_w_xo, m_norm_ffn_g, m_w_ff1, m_w_ff2, m_final_norm_g, v_norm_mix_g, v_w_in, v_lb_logits, v_hgrn_norm_g, v_w_pool, v_pool_scale, v_w_out, v_norm_x_g, v_norm_mem_g, v_w_xq, v_w_xk, v_w_xv, v_w_xo, v_norm_ffn_g, v_w_ff1, v_w_ff2, v_final_norm_g):
    x0 = x[0]
    mem0 = mem[0]
    tgt = loss_target[0]
    gn = hgrn_norm_g[0]
    gfin = final_norm_g.reshape(1, D_MODEL)
    wp = w_pool[0]
    heads_2d = lambda w: w.reshape(D_MODEL // N_DEV, D_MODEL)
    xo_2d = lambda w: w.reshape(D_MODEL, D_MODEL // N_DEV)

    first = _all_gather_weights([w_in[0].T], [w_out[0], heads_2d(w_xq), heads_2d(w_xk), heads_2d(w_xv), xo_2d(w_xo).T,
                                              w_ff1[0], w_ff2[0]])
    win_t = first[0].reshape(IN_WIDTH, D_MODEL)
    ga_attn = _gather_first_start(first[1:6], "gather_attn_first_start")
    ga_mlp = _gather_first_start(first[6:8], "gather_mlp_first_start")

    z, h = _in_proj(x0, _after(_after(norm_mix_g, ga_attn[3]), ga_mlp[3]), win_t)
    mixed_a, o_pre, states = _hgrn_fwd(z, lb_logits, gn)
    lands = _split_wait(_gather_first_copies, ga_attn, o_pre, "gather_attn_first_wait")
    gb_attn = _gather_forward_start(lands, "gather_attn_forward_start")
    mixed, pooled = _pool_fwd(z, wp, _after(pool_scale, gb_attn[3]), mixed_a)
    lands = _split_wait(_gather_forward_copies, gb_attn, pooled, "gather_attn_forward_wait")
    wout_f, wq_f, wk_f, wv_f, wo_t = (t.reshape(D_MODEL, D_MODEL) for t in lands)
    x1 = _out_proj(x0, mixed, wout_f)
    lands = _split_wait(_gather_first_copies, ga_mlp, x1, "gather_mlp_first_wait")
    gb_mlp = _gather_forward_start(lands, "gather_mlp_forward_start")
    hm, xk, xv = _mem_kv(mem0, _after(norm_mem_g, gb_mlp[3]), wk_f, wv_f)
    x2, hq, xq, att = _xattn_fwd(x1, norm_x_g, wq_f, xk, xv, wo_t)
    w1_b, w2_b = _split_wait(_gather_forward_copies, gb_mlp, x2, "gather_mlp_forward_wait")
    dx3, u, hf, slot_fin = _mlp_fwd_loss(x2, norm_ffn_g, w1_b, w2_b, gfin, tgt)

    rows = lambda t, r: t.reshape(N_DEV, r, D_MODEL)
    da, dx2, slot_ffn = _mlp_bwd(dx3, u, x2, norm_ffn_g, w1_b, w2_b)
    dw2 = _wgrad(u, dx3, "wgrad_ff2")
    dw1 = _wgrad(hf, da, "wgrad_ff1", col_blocks=True)
    ex_ff = _all_to_all_start([dw1, rows(dw2, FF_BLK)], [], "exchange_ff_start")
    dx1, dxq, dxk, dxv, slot_x = _xattn_bwd(dx2, x1, _after(norm_x_g, ex_ff[3]), xq, xk, xv, wq_f, wo_t)
    dwo_t = _wgrad(dx2, att, "wgrad_xo")
    dwq = _wgrad(hq, dxq, "wgrad_xq")
    dwk, dwv, slot_mem = _mem_bwd(mem0, norm_mem_g, hm, dxk, dxv, wk_f, wv_f)
    ex_attn = _all_to_all_start([rows(dwq, 128), rows(dwk, 128), rows(dwv, 128), rows(dwo_t, 128)], [],
                                "exchange_attn_start")
    dmix = _matmul_nt(dx1, wout_f, "out_proj_bwd", dep=ex_attn[3])
    dwout = _wgrad(mixed, dx1, "wgrad_out")
    dz_pool, d_wpool, slot_ps = _pool_bwd(dmix, pooled, wp, pool_scale)
    small0 = jnp.concatenate([slot_x, slot_mem, slot_ffn, slot_fin, slot_ps], axis=0)
    ex_out = _all_to_all_start([rows(dwout, 128)], [small0, d_wpool], "exchange_out_start")
    dz, slot_lb, slot_gn = _hgrn_bwd(z, o_pre, dmix, states, _after(lb_logits, ex_out[3]), gn, dz_pool)
    grad_x, slot_mix = _in_bwd(dz, win_t, x0, norm_mix_g, dx1)
    dwin_t = _wgrad(dz, h, "wgrad_in")
    small1 = jnp.concatenate([slot_mix, slot_lb, slot_gn], axis=0)
    ex_in = _all_to_all_start([rows(dwin_t, 320)], [small1], "exchange_in_start")

    out = {}
    r_1, r_2 = _split_wait(_all_to_all_copies(2), ex_ff, ex_in[3], "exchange_ff_wait")
    out["w_ff1"] = _sum_adamw(r_1, w_ff1[0], m_w_ff1[0], v_w_ff1[0], "adamw_ff1")
    out["w_ff2"] = _sum_adamw(r_2, w_ff2[0], m_w_ff2[0], v_w_ff2[0], "adamw_ff2")
    r_q, r_k, r_v, r_o = _split_wait(_all_to_all_copies(4), ex_attn, out["w_ff2"][1], "exchange_attn_wait")
    out["w_xq"] = _sum_adamw(r_q, heads_2d(w_xq), heads_2d(m_w_xq), heads_2d(v_w_xq), "adamw_xq")
    out["w_xk"] = _sum_adamw(r_k, heads_2d(w_xk), heads_2d(m_w_xk), heads_2d(v_w_xk), "adamw_xk")
    out["w_xv"] = _sum_adamw(r_v, heads_2d(w_xv), heads_2d(m_w_xv), heads_2d(v_w_xv), "adamw_xv")
    g_xo = _sum_sources(r_o, "sum_grad_xo").T
    out["w_xo"] = (g_xo, *_adamw(g_xo, xo_2d(w_xo), xo_2d(m_w_xo), xo_2d(v_w_xo), "adamw_xo"))
    r_out, r_small0, r_wpool = _split_wait(_all_to_all_copies(1), ex_out, out["w_xo"][1], "exchange_out_wait")
    out["w_out"] = _sum_adamw(r_out, w_out[0], m_w_out[0], v_w_out[0], "adamw_out")
    r_in, r_small1 = _split_wait(_all_to_all_copies(1), ex_in, out["w_out"][1], "exchange_in_wait")
    g_in = _sum_sources(r_in, "sum_grad_in").T
    out["w_in"] = (g_in, *_adamw(g_in, w_in[0], m_w_in[0], v_w_in[0], "adamw_in"))
    row = lambda t: t.reshape(1, -1)
    small_params = {
        "norm_mix_g": (norm_mix_g, m_norm_mix_g, v_norm_mix_g),
        "lb_logits": (lb_logits, m_lb_logits, v_lb_logits),
        "hgrn_norm_g": (hgrn_norm_g[0], m_hgrn_norm_g[0], v_hgrn_norm_g[0]),
        "pool_scale": (pool_scale, m_pool_scale, v_pool_scale),
        "norm_x_g": (norm_x_g, m_norm_x_g, v_norm_x_g),
        "norm_mem_g": (norm_mem_g, m_norm_mem_g, v_norm_mem_g),
        "norm_ffn_g": (norm_ffn_g, m_norm_ffn_g, v_norm_ffn_g),
        "final_norm_g": (row(final_norm_g), row(m_final_norm_g), row(v_final_norm_g)),
        "w_pool": (wp, m_w_pool[0], v_w_pool[0]),
    }
    loss, small_out = _small_update(r_small0, r_small1, r_wpool, small_params)
    out.update(small_out)

    shapes = dict(norm_mix_g=norm_mix_g, w_in=w_in, lb_logits=lb_logits, hgrn_norm_g=hgrn_norm_g, w_pool=w_pool,
                  pool_scale=pool_scale, w_out=w_out, norm_x_g=norm_x_g, norm_mem_g=norm_mem_g, w_xq=w_xq, w_xk=w_xk,
                  w_xv=w_xv, w_xo=w_xo, norm_ffn_g=norm_ffn_g, w_ff1=w_ff1, w_ff2=w_ff2, final_norm_g=final_norm_g)
    order = list(shapes)
    group = lambda k: [out[n][k].reshape(shapes[n].shape) for n in order]
    return (loss.reshape(()), grad_x.reshape(x.shape), *group(0), *group(1), *group(2), *group(3))
```

```python
import jax
import jax.numpy as jnp
from jax import lax
from jax.experimental import pallas as pl
from jax.experimental.pallas import tpu as pltpu

F32 = jnp.float32
BF16 = jnp.bfloat16

D_MODEL = 1024
N_DEV = 8
HEADS = 4
HD = 128
HW = HEADS * HD
IN_WIDTH = 5 * HW
XHD = 256
MEM_LEN = 256
D_FF = 4096
FF_BLK = D_FF // N_DEV
POOL_WINDOWS = (2, 4, 8, 16)
POOL_HALO = 16
CHUNK = 64
CHUNKS_PER_STEP = 2
SUB = 16
N_SUB = CHUNK // SUB
EXP_CAP = 80.0
EPS = 1e-6
ROW_TILE = 512
SLOT = 8
V7X_VMEM_LIMIT = 56 * 1024 * 1024

ADAM_LR = 0.001
ADAM_B1 = 0.9
ADAM_B2 = 0.999
ADAM_EPS = 1e-08
ADAM_WD = 0.01
ADAM_STEP = 10

MESH_ID = pl.DeviceIdType.MESH


def _params(sem=None, vmem=V7X_VMEM_LIMIT):
    return pltpu.CompilerParams(dimension_semantics=sem, vmem_limit_bytes=vmem)


def _mm(a, b):
    return lax.dot_general(a.astype(BF16), b.astype(BF16), (((1,), (0,)), ((), ())), preferred_element_type=F32)


def _mm_nt(a, b):
    return lax.dot_general(a.astype(BF16), b.astype(BF16), (((1,), (1,)), ((), ())), preferred_element_type=F32)


def _mm_tn(a, b):
    return lax.dot_general(a.astype(BF16), b.astype(BF16), (((0,), (0,)), ((), ())), preferred_element_type=F32)


def _sigmoid(x):
    return 1.0 / (1.0 + jnp.exp(-x))


def _rms(x):
    return lax.rsqrt(jnp.mean(x * x, axis=-1, keepdims=True) + EPS)


def _rms_bwd(x, g, dh):
    r = _rms(x)
    n = x * r
    dn = dh * g
    dx = r * (dn - n * jnp.mean(dn * n, axis=-1, keepdims=True))
    return dx, jnp.sum(dh * n, axis=0, keepdims=True)


def _tri_dot(tri, x, passes):
    acc = None
    rest = x
    for _ in range(passes):
        piece = rest.astype(BF16)
        part = lax.dot_general(tri, piece, (((1,), (0,)), ((), ())), preferred_element_type=F32)
        acc = part if acc is None else acc + part
        rest = rest - piece.astype(F32)
    return acc


def _adam_update(g, w, m, v):
    nm = ADAM_B1 * m + (1.0 - ADAM_B1) * g
    nv = ADAM_B2 * v + (1.0 - ADAM_B2) * (g * g)
    m_hat = nm / (1.0 - ADAM_B1 ** ADAM_STEP)
    v_hat = nv / (1.0 - ADAM_B2 ** ADAM_STEP)
    return -ADAM_LR * (m_hat / (jnp.sqrt(v_hat) + ADAM_EPS) + ADAM_WD * w), nm, nv


def _full(shape):
    return pl.BlockSpec(shape, lambda *_: (0,) * len(shape))


VMEM_WHOLE = pl.BlockSpec(memory_space=pltpu.VMEM)
ANY_SPACE = pl.BlockSpec(memory_space=pl.ANY)


def _mesh_pos():
    return lax.axis_index("x"), lax.axis_index("y"), lax.axis_index("c")


def _flat(px, py, pc):
    return 4 * px + 2 * py + pc


def _all_gather_weights(shards, cast_only):
    n, nc = len(shards), len(cast_only)
    step = 64

    def body(*refs):
        x_refs, c_refs = refs[:n], refs[n:n + nc]
        out_refs, cast_refs = refs[n + nc:2 * n + nc], refs[2 * n + nc:2 * n + 2 * nc]
        bufs = refs[2 * n + 2 * nc:3 * n + 2 * nc]
        send_sems, recv_sems, local_sems = refs[3 * n + 2 * nc:]
        x, y, c = _mesh_pos()
        me, sibling = (x, y, c), (x, y, 1 - c)
        chips = [(1 - x, y), (x, 1 - y), (1 - x, 1 - y)]

        def copy(a, k, blk, to, src=None):
            rows = out_refs[a].at[_flat(*blk)]
            return pltpu.make_async_remote_copy(
                src_ref=rows if src is None else src, dst_ref=rows,
                send_sem=send_sems.at[7 * a + k], recv_sem=recv_sems.at[7 * a + k], device_id=to, device_id_type=MESH_ID)

        def cast_rows(src, dst, rows):
            def cast(i, carry):
                r0 = pl.multiple_of(i * step, step)
                dst[pl.ds(r0, step), :] = src[pl.ds(r0, step), :].astype(BF16)
                return carry
            lax.fori_loop(0, rows // step, cast, 0)

        first, mine = [], []
        for a in range(n):
            cast_rows(x_refs[a], bufs[a], shards[a].shape[0])
            mine.append(pltpu.make_async_copy(bufs[a], out_refs[a].at[_flat(*me)], local_sems.at[a]))
            first.append(copy(a, 0, me, sibling, src=bufs[a]))
            first += [copy(a, 1 + j, me, (*chip, c), src=bufs[a]) for j, chip in enumerate(chips)]
            for cp in [mine[-1]] + first[-4:]:
                cp.start()
        for a in range(nc):
            cast_rows(c_refs[a], cast_refs[a], cast_only[a].shape[0])
        passed = []
        for j, chip in enumerate(chips):
            for a in range(n):
                copy(a, 1 + j, (*chip, c), me).wait_recv()
                passed.append(copy(a, 4 + j, (*chip, c), sibling))
                passed[-1].start()
        for a in range(n):
            copy(a, 0, sibling, me).wait_recv()
            for j, chip in enumerate(chips):
                copy(a, 4 + j, (*chip, 1 - c), me).wait_recv()
        for cp in first + passed:
            cp.wait_send()
        for cp in mine:
            cp.wait()

    return pl.pallas_call(
        body, name="all_gather_w_in",
        out_shape=[jax.ShapeDtypeStruct((N_DEV,) + s.shape, BF16) for s in shards]
        + [jax.ShapeDtypeStruct(s.shape, BF16) for s in cast_only],
        in_specs=[VMEM_WHOLE] * (n + nc), out_specs=[ANY_SPACE] * n + [VMEM_WHOLE] * nc,
        scratch_shapes=[pltpu.VMEM(s.shape, BF16) for s in shards]
        + [pltpu.SemaphoreType.DMA((7 * n,)), pltpu.SemaphoreType.DMA((7 * n,)), pltpu.SemaphoreType.DMA((n,))],
        compiler_params=_params(),
    )(*shards, *cast_only)


HBM_SPEC = pl.BlockSpec(memory_space=pltpu.HBM)
SEM_SPEC = pl.BlockSpec(memory_space=pltpu.SEMAPHORE)
EFFECT = pltpu.SideEffectType.DATAFLOW_SIDE_EFFECTING
TOKEN = jax.ShapeDtypeStruct((8, 128), F32)


def _in_hbm(a):
    return pltpu.with_memory_space_constraint(a, pltpu.HBM)


def _split_start(copies_of, srcs, lands, n_sems, name):
    ns, nl, k = len(srcs), len(lands), len(n_sems)

    def body(*refs):
        src_refs, land_refs = refs[:ns], refs[ns:ns + nl]
        sems = refs[ns + nl:ns + nl + k]
        token = refs[-1]
        for cp in copies_of(src_refs, land_refs, sems):
            cp.start()
        token[...] = jnp.zeros_like(token)

    outs = pl.pallas_call(
        body, name=name,
        out_shape=[pltpu.SemaphoreType.DMA((q,)) for q in n_sems]
        + [pltpu.HBM(a.shape, a.dtype) for a in list(srcs) + list(lands)] + [TOKEN],
        in_specs=[HBM_SPEC] * (ns + nl),
        out_specs=[SEM_SPEC] * k + [HBM_SPEC] * (ns + nl) + [VMEM_WHOLE],
        input_output_aliases={i: k + i for i in range(ns + nl)},
        compiler_params=pltpu.CompilerParams(has_side_effects=EFFECT),
    )(*[_in_hbm(a) for a in list(srcs) + list(lands)])
    return outs[:k], outs[k:k + ns], outs[k + ns:k + ns + nl], outs[-1]


def _split_wait(copies_of, handle, after, name):
    sems, srcs, lands, _ = handle
    ns, nl, k = len(srcs), len(lands), len(sems)

    def body(*refs):
        src_refs, land_refs = refs[:ns], refs[ns:ns + nl]
        sem_refs = refs[ns + nl:ns + nl + k]
        for cp in copies_of(src_refs, land_refs, sem_refs):
            cp.wait()

    outs = pl.pallas_call(
        body, name=name,
        out_shape=[pltpu.HBM(a.shape, a.dtype) for a in list(srcs) + list(lands)],
        in_specs=[HBM_SPEC] * (ns + nl) + [SEM_SPEC] * k + [ANY_SPACE],
        out_specs=[HBM_SPEC] * (ns + nl),
        input_output_aliases={i: i for i in range(ns + nl)},
        compiler_params=pltpu.CompilerParams(has_side_effects=EFFECT),
    )(*srcs, *lands, *sems, after)
    return outs[ns:]


def _gather_first_copies(shard_refs, land_refs, sems):
    send_sems, recv_sems, local_sems = sems
    x, y, c = _mesh_pos()
    me = _flat(x, y, c)
    peers = [(x, y, 1 - c), (1 - x, y, c), (x, 1 - y, c), (1 - x, 1 - y, c)]
    copies = []
    for a, (shard, land) in enumerate(zip(shard_refs, land_refs)):
        copies.append(pltpu.make_async_copy(shard, land.at[me], local_sems.at[a]))
        for k, peer in enumerate(peers):
            copies.append(pltpu.make_async_remote_copy(
                src_ref=shard, dst_ref=land.at[me], send_sem=send_sems.at[4 * a + k], recv_sem=recv_sems.at[4 * a + k],
                device_id=peer, device_id_type=MESH_ID))
    return copies


def _gather_forward_copies(src_refs, land_refs, sems):
    del src_refs
    send_sems, recv_sems = sems
    x, y, c = _mesh_pos()
    chips = [(1 - x, y), (x, 1 - y), (1 - x, 1 - y)]
    copies = []
    for a, land in enumerate(land_refs):
        for j, chip in enumerate(chips):
            rows = land.at[_flat(*chip, c)]
            copies.append(pltpu.make_async_remote_copy(
                src_ref=rows, dst_ref=rows, send_sem=send_sems.at[3 * a + j], recv_sem=recv_sems.at[3 * a + j],
                device_id=(x, y, 1 - c), device_id_type=MESH_ID))
    return copies


def _gather_first_start(shards, name):
    lands = [lax.empty((N_DEV,) + s.shape, s.dtype) for s in shards]
    n = len(shards)
    return _split_start(_gather_first_copies, shards, lands, (4 * n, 4 * n, n), name)


def _gather_forward_start(lands, name):
    n = len(lands)
    return _split_start(_gather_forward_copies, [], lands, (3 * n, 3 * n), name)


def _all_to_all_copies(n_scattered):
    def copies_of(src_refs, land_refs, sems):
        send_sems, recv_sems, local_sems = sems
        x, y, c = _mesh_pos()
        me = _flat(x, y, c)
        copies = []
        for a, (src, land) in enumerate(zip(src_refs, land_refs)):
            scattered = a < n_scattered
            copies.append(pltpu.make_async_copy(src.at[me] if scattered else src, land.at[me], local_sems.at[a]))
            for k in range(1, N_DEV):
                peer = (1 - x if k & 4 else x, 1 - y if k & 2 else y, 1 - c if k & 1 else c)
                copies.append(pltpu.make_async_remote_copy(
                    src_ref=src.at[_flat(*peer)] if scattered else src, dst_ref=land.at[me],
                    send_sem=send_sems.at[7 * a + k - 1], recv_sem=recv_sems.at[7 * a + k - 1],
                    device_id=peer, device_id_type=MESH_ID))
        return copies
    return copies_of


def _all_to_all_start(scattered, broadcast, name):
    srcs = list(scattered) + list(broadcast)
    lands = [lax.empty(a.shape, a.dtype) for a in scattered] + [lax.empty((N_DEV,) + a.shape, a.dtype) for a in broadcast]
    n = len(srcs)
    return _split_start(_all_to_all_copies(len(scattered)), srcs, lands, (7 * n, 7 * n, n), name)


def _after(a, token):
    return a + token[0, 0]


def _row_tile(rows):
    for cand in (256, 128, 64, 32, 16):
        if rows % cand == 0:
            return cand
    return rows


def _sum_sources(recv, name):
    _, rows, cols = recv.shape
    tile = _row_tile(rows)

    def body(r_ref, o_ref):
        acc = r_ref[0].astype(F32)
        for d in range(1, N_DEV):
            acc = acc + r_ref[d].astype(F32)
        o_ref[...] = acc

    return pl.pallas_call(
        body, name=name, grid=(rows // tile,),
        out_shape=jax.ShapeDtypeStruct((rows, cols), F32),
        in_specs=[pl.BlockSpec((N_DEV, tile, cols), lambda i: (0, i, 0))],
        out_specs=pl.BlockSpec((tile, cols), lambda i: (i, 0)),
        compiler_params=_params(("parallel",)),
    )(recv)


def _adamw(g, w, m, v, name):
    rows, cols = g.shape
    tile = _row_tile(rows)

    def body(g_ref, w_ref, m_ref, v_ref, d_ref, nm_ref, nv_ref):
        d_ref[...], nm_ref[...], nv_ref[...] = _adam_update(g_ref[...], w_ref[...], m_ref[...], v_ref[...])

    spec = pl.BlockSpec((tile, cols), lambda i: (i, 0))
    shp = jax.ShapeDtypeStruct((rows, cols), F32)
    return pl.pallas_call(
        body, name=name, grid=(rows // tile,), out_shape=[shp, shp, shp],
        in_specs=[spec] * 4, out_specs=[spec] * 3,
        compiler_params=_params(("parallel",)),
    )(g, w, m, v)


def _sum_adamw(recv, w, m, v, name):
    _, rows, cols = recv.shape
    tile = _row_tile(rows)

    def body(r_ref, w_ref, m_ref, v_ref, g_ref, d_ref, nm_ref, nv_ref):
        acc = r_ref[0].astype(F32)
        for d in range(1, N_DEV):
            acc = acc + r_ref[d].astype(F32)
        g_ref[...] = acc
        d_ref[...], nm_ref[...], nv_ref[...] = _adam_update(acc, w_ref[...], m_ref[...], v_ref[...])

    spec = pl.BlockSpec((tile, cols), lambda i: (i, 0))
    shp = jax.ShapeDtypeStruct((rows, cols), F32)
    return pl.pallas_call(
        body, name=name, grid=(rows // tile,), out_shape=[shp] * 4,
        in_specs=[pl.BlockSpec((N_DEV, tile, cols), lambda i: (0, i, 0)), spec, spec, spec], out_specs=[spec] * 4,
        compiler_params=_params(("parallel",)),
    )(recv, w, m, v)


SMALL_SLOTS = {"norm_x_g": (0, 0, 1, D_MODEL), "norm_mem_g": (0, 8, 1, D_MODEL), "norm_ffn_g": (0, 16, 1, D_MODEL),
               "final_norm_g": (0, 24, 1, D_MODEL), "pool_scale": (0, 32, 1, HW),
               "lb_logits": (1, 0, 2, HW), "hgrn_norm_g": (1, 8, HEADS, HD), "norm_mix_g": (2, 0, 1, D_MODEL)}
LOSS_ROW = 25
SMALL_ORDER = ("norm_mix_g", "lb_logits", "hgrn_norm_g", "pool_scale", "norm_x_g", "norm_mem_g", "norm_ffn_g",
               "final_norm_g", "w_pool")


def _small_update(srecvs, wprecv, params):
    flat = [t for n in SMALL_ORDER for t in params[n]]
    nb = len(srecvs)
    n_in = nb + 1 + len(flat)

    def body(*refs):
        s_refs, wp_ref = refs[0:nb], refs[nb]
        in_refs = refs[nb + 1:n_in]
        loss_ref = refs[n_in]
        out_refs = refs[n_in + 1:-nb]
        accs = refs[-nb:]
        for s_ref, acc in zip(s_refs, accs):
            total = s_ref[0]
            for d in range(1, N_DEV):
                total = total + s_ref[d]
            acc[...] = total
        loss_ref[...] = accs[0][LOSS_ROW:LOSS_ROW + 1, 0:1]
        for i, name in enumerate(SMALL_ORDER):
            w_ref, m_ref, v_ref = in_refs[3 * i:3 * i + 3]
            g_ref, d_ref, nm_ref, nv_ref = out_refs[4 * i:4 * i + 4]
            if name == "w_pool":
                g = wp_ref[0]
                for d in range(1, N_DEV):
                    g = g + wp_ref[d]
            else:
                buf, r0, nr, nc = SMALL_SLOTS[name]
                g = accs[buf][r0:r0 + nr, 0:nc]
            g_ref[...] = g
            d_ref[...], nm_ref[...], nv_ref[...] = _adam_update(g, w_ref[...], m_ref[...], v_ref[...])

    out_shape = [jax.ShapeDtypeStruct((1, 1), F32)]
    for n in SMALL_ORDER:
        out_shape += [jax.ShapeDtypeStruct(params[n][0].shape, F32)] * 4
    outs = pl.pallas_call(
        body, name="small_update", out_shape=out_shape,
        in_specs=[VMEM_WHOLE] * n_in, out_specs=[VMEM_WHOLE] * len(out_shape),
        scratch_shapes=[pltpu.VMEM(r.shape[1:], F32) for r in srecvs],
        compiler_params=_params(),
    )(*srecvs, wprecv, *flat)
    return outs[0], {n: outs[1 + 4 * i:5 + 4 * i] for i, n in enumerate(SMALL_ORDER)}


def _in_proj(x, g, w_t):
    s = x.shape[0]
    tm = min(ROW_TILE, s)

    def body(x_ref, g_ref, w_ref, z_ref, h_ref):
        xv = x_ref[...]
        h = (xv * _rms(xv) * g_ref[...]).astype(BF16)
        h_ref[...] = h
        z_ref[...] = _mm_nt(h, w_ref[...])

    return pl.pallas_call(
        body, name="in_proj", grid=(s // tm,),
        out_shape=[jax.ShapeDtypeStruct((s, IN_WIDTH), F32), jax.ShapeDtypeStruct((s, D_MODEL), BF16)],
        in_specs=[pl.BlockSpec((tm, D_MODEL), lambda i: (i, 0)), _full((1, D_MODEL)), VMEM_WHOLE],
        out_specs=[pl.BlockSpec((tm, IN_WIDTH), lambda i: (i, 0)), pl.BlockSpec((tm, D_MODEL), lambda i: (i, 0))],
        compiler_params=_params(("parallel",)),
    )(x, g, w_t)


def _chunk_masks():
    row = lax.broadcasted_iota(jnp.int32, (CHUNK, CHUNK), 0)
    col = lax.broadcasted_iota(jnp.int32, (CHUNK, CHUNK), 1)
    return row, col


def _ones_where(mask):
    return jnp.where(mask, 1.0, 0.0).astype(BF16)


def _hgrn_gates(zq, zf, lb):
    sq = _sigmoid(zq)
    sig = _sigmoid(zf)
    f = lb + (1.0 - lb) * sig
    return zq * sq, sq, sig, f


def _hgrn_intra_factors(b_scr, r0, bh, qh, kh, sl):
    trow = lax.broadcasted_iota(jnp.int32, (CHUNK, HD), 0)
    eq, ek = [], []
    for j in range(N_SUB):
        if j == 0:
            base = jnp.zeros((1, HD), F32)
        else:
            base = b_scr[r0 + SUB * j - 1:r0 + SUB * j, sl]
        in_j = (trow >= SUB * j) & (trow < SUB * (j + 1))
        eq.append(jnp.where(in_j, jnp.exp(bh - base), 0.0))
        ek.append(jnp.where(trow < SUB * (j + 1), jnp.exp(jnp.minimum(base - bh, EXP_CAP)), 0.0))
    eqcat = jnp.concatenate(eq, axis=1)
    ekcat = jnp.concatenate(ek, axis=1)
    qcat = jnp.concatenate([qh] * N_SUB, axis=1) * eqcat
    kecat = jnp.concatenate([kh] * N_SUB, axis=1) * ekcat
    return qcat, kecat, eqcat, ekcat


def _sum_lane_blocks(a):
    out = a[:, 0:HD]
    for j in range(1, N_SUB):
        out = out + a[:, HD * j:HD * (j + 1)]
    return out


def _hgrn_fwd(z, lb_logits, gn):
    s = z.shape[0]
    n_chunks = s // CHUNK

    def body(zq_ref, zf_ref, zi_ref, zg_ref, lbl_ref, gn_ref, oa_ref, o_ref, st_ref, state, b_scr):
        @pl.when(pl.program_id(0) == 0)
        def _():
            state[...] = jnp.zeros_like(state)

        lb = _sigmoid(lbl_ref[0:1, :] - lbl_ref[1:2, :])
        row, col = _chunk_masks()
        causal = col <= row
        tri = _ones_where(causal)
        for c in range(CHUNKS_PER_STEP):
            r0 = CHUNK * c
            rs = slice(r0, r0 + CHUNK)
            st_ref[c] = state[...]
            q, _, _, f = _hgrn_gates(zq_ref[rs, :], zf_ref[rs, :], lb)
            kk = 1.0 - f
            b_scr[rs, :] = _tri_dot(tri, jnp.log(f), 3)
            for h in range(HEADS):
                sl = slice(HD * h, HD * (h + 1))
                bh = b_scr[rs, sl]
                qh, kh, vh = q[:, sl], kk[:, sl], zi_ref[rs, sl]
                st = state[h]
                b_last = b_scr[r0 + CHUNK - 1:r0 + CHUNK, sl]
                qcat, kecat, _, _ = _hgrn_intra_factors(b_scr, r0, bh, qh, kh, sl)
                a = jnp.where(causal, _mm_nt(qcat, kecat), 0.0)
                o = _mm(a, vh) + _mm_nt(qh * jnp.exp(bh), st)
                state[h] = st * jnp.exp(b_last) + _mm_tn(vh, kh * jnp.exp(b_last - bh))
                o_ref[rs, sl] = o
                zg = zg_ref[rs, sl]
                oa_ref[rs, sl] = (o * _rms(o) * gn_ref[h:h + 1, :] * zg * _sigmoid(zg)).astype(BF16)

    rows = CHUNK * CHUNKS_PER_STEP
    zspec = lambda cb: pl.BlockSpec((rows, HW), lambda i, cb=cb: (i, cb))
    return pl.pallas_call(
        body, name="hgrn_fwd", grid=(s // rows,),
        out_shape=[jax.ShapeDtypeStruct((s, 2 * HW), BF16), jax.ShapeDtypeStruct((s, HW), F32),
                   jax.ShapeDtypeStruct((n_chunks, HEADS, HD, HD), F32)],
        in_specs=[zspec(0), zspec(1), zspec(2), zspec(3), _full((2, HW)), _full((HEADS, HD))],
        out_specs=[pl.BlockSpec((rows, HW), lambda i: (i, 0)), pl.BlockSpec((rows, HW), lambda i: (i, 0)),
                   pl.BlockSpec((CHUNKS_PER_STEP, HEADS, HD, HD), lambda i: (i, 0, 0, 0))],
        scratch_shapes=[pltpu.VMEM((HEADS, HD, HD), F32), pltpu.VMEM((rows, HW), F32)],
        compiler_params=_params(("arbitrary",)),
    )(z, z, z, z, lb_logits, gn)


def _pool_counts(tile_idx, tm):
    t = tile_idx * tm + lax.broadcasted_iota(jnp.int32, (tm, 1), 0)
    return [1.0 / jnp.minimum(t + 1, w).astype(F32) for w in POOL_WINDOWS]


def _pool_fwd(z, w_pool, scale, mixed_in):
    s = z.shape[0]
    tm = min(ROW_TILE, s)

    def body(p_ref, w_ref, sc_ref, mixin_ref, ob_ref, pooled_ref, ext):
        i = pl.program_id(0)

        @pl.when(i == 0)
        def _():
            ext[0:POOL_HALO, :] = jnp.zeros((POOL_HALO, HW), F32)

        @pl.when(i > 0)
        def _():
            ext[0:POOL_HALO, :] = ext[tm:tm + POOL_HALO, :]

        ext[POOL_HALO:POOL_HALO + tm, :] = p_ref[...]
        inv = _pool_counts(i, tm)
        for g, w in enumerate(POOL_WINDOWS):
            sl = slice(HD * g, HD * (g + 1))
            p = ext[POOL_HALO:POOL_HALO + tm, sl]
            win = p
            for d in range(1, w):
                win = win + ext[POOL_HALO - d:POOL_HALO - d + tm, sl]
            pooled = (win * inv[g] - p).astype(BF16)
            pooled_ref[:, sl] = pooled
            ob_ref[:, sl] = (_mm(pooled, w_ref[g]) * sc_ref[:, sl]).astype(BF16)

    return pl.pallas_call(
        body, name="pool_fwd", grid=(s // tm,),
        out_shape=[jax.ShapeDtypeStruct((s, 2 * HW), BF16), jax.ShapeDtypeStruct((s, HW), BF16)],
        in_specs=[pl.BlockSpec((tm, HW), lambda i: (i, 4)), _full((HEADS, HD, HD)), _full((1, HW)), ANY_SPACE],
        out_specs=[pl.BlockSpec((tm, HW), lambda i: (i, 1)), pl.BlockSpec((tm, HW), lambda i: (i, 0))],
        scratch_shapes=[pltpu.VMEM((tm + POOL_HALO, HW), F32)],
        input_output_aliases={3: 0},
        compiler_params=_params(("arbitrary",)),
    )(z, w_pool, scale, mixed_in)


def _out_proj(x, mixed, w_out):
    s = x.shape[0]
    tm = min(ROW_TILE, s)

    def body(x_ref, a_ref, w_ref, o_ref):
        o_ref[...] = x_ref[...] + _mm(a_ref[...], w_ref[...])

    return pl.pallas_call(
        body, name="out_proj", grid=(s // tm,),
        out_shape=jax.ShapeDtypeStruct((s, D_MODEL), F32),
        in_specs=[pl.BlockSpec((tm, D_MODEL), lambda i: (i, 0)), pl.BlockSpec((tm, D_MODEL), lambda i: (i, 0)), VMEM_WHOLE],
        out_specs=pl.BlockSpec((tm, D_MODEL), lambda i: (i, 0)),
        compiler_params=_params(("parallel",)),
    )(x, mixed, w_out)


def _mem_kv(mem, g, wk, wv):
    def body(m_ref, g_ref, wk_ref, wv_ref, hm_ref, k_ref, v_ref):
        m = m_ref[...]
        hm = (m * _rms(m) * g_ref[...]).astype(BF16)
        hm_ref[...] = hm
        k_ref[...] = _mm(hm, wk_ref[...]).astype(BF16)
        v_ref[...] = _mm(hm, wv_ref[...]).astype(BF16)

    shp = jax.ShapeDtypeStruct((MEM_LEN, D_MODEL), BF16)
    return pl.pallas_call(
        body, name="mem_kv", out_shape=[shp, shp, shp],
        in_specs=[VMEM_WHOLE] * 4, out_specs=[VMEM_WHOLE] * 3,
        compiler_params=_params(),
    )(mem, g, wk, wv)


def _softmax_rows(sc):
    e = jnp.exp(sc - jnp.max(sc, axis=-1, keepdims=True))
    return e / jnp.sum(e, axis=-1, keepdims=True)


def _xattn_fwd(x, g, wq, xk, xv, wo_t):
    s = x.shape[0]
    tm = min(ROW_TILE, s)
    scale = XHD ** -0.5

    def body(x_ref, g_ref, wq_ref, k_ref, v_ref, wo_ref, o_ref, hq_ref, q_ref, att_ref):
        xv_ = x_ref[...]
        hq = (xv_ * _rms(xv_) * g_ref[...]).astype(BF16)
        hq_ref[...] = hq
        q_ref[...] = (_mm(hq, wq_ref[...]) * scale).astype(BF16)
        for h in range(HEADS):
            sl = slice(XHD * h, XHD * (h + 1))
            p = _softmax_rows(_mm_nt(q_ref[:, sl], k_ref[:, sl]))
            att_ref[:, sl] = _mm(p, v_ref[:, sl]).astype(BF16)
        o_ref[...] = xv_ + _mm_nt(att_ref[...], wo_ref[...])

    row_f32 = pl.BlockSpec((tm, D_MODEL), lambda i: (i, 0))
    bshape = jax.ShapeDtypeStruct((s, D_MODEL), BF16)
    return pl.pallas_call(
        body, name="xattn_fwd", grid=(s // tm,),
        out_shape=[jax.ShapeDtypeStruct((s, D_MODEL), F32), bshape, bshape, bshape],
        in_specs=[row_f32, _full((1, D_MODEL)), VMEM_WHOLE, VMEM_WHOLE, VMEM_WHOLE, VMEM_WHOLE],
        out_specs=[row_f32] * 4,
        compiler_params=_params(("parallel",)),
    )(x, g, wq, xk, xv, wo_t)


def _mlp_fwd_loss(x, g, w1, w2, gf, target):
    s = x.shape[0]
    tm = min(ROW_TILE, s)

    def body(x_ref, g_ref, w1_ref, w2_ref, gf_ref, t_ref, dx_ref, u_ref, hf_ref, slot_ref):
        @pl.when(pl.program_id(0) == 0)
        def _():
            slot_ref[...] = jnp.zeros_like(slot_ref)

        xv = x_ref[...]
        hf = (xv * _rms(xv) * g_ref[...]).astype(BF16)
        hf_ref[...] = hf
        acc = xv
        for j in range(N_DEV):
            a = jnp.maximum(_mm(hf, w1_ref[j]), 0.0)
            u = (a * a).astype(BF16)
            u_ref[:, FF_BLK * j:FF_BLK * (j + 1)] = u
            acc = acc + _mm(u, w2_ref[j])
        gfv = gf_ref[...]
        r = _rms(acc)
        n = acc * r
        err = n * gfv - t_ref[...]
        slot_ref[1:2, :] += jnp.sum(jnp.mean(err * err, axis=-1, keepdims=True), axis=0, keepdims=True) * 0.5
        dy = err * (1.0 / D_MODEL)
        slot_ref[0:1, :] += jnp.sum(dy * n, axis=0, keepdims=True)
        dn = dy * gfv
        dx_ref[...] = r * (dn - n * jnp.mean(dn * n, axis=-1, keepdims=True))

    row_f32 = pl.BlockSpec((tm, D_MODEL), lambda i: (i, 0))
    return pl.pallas_call(
        body, name="mlp_fwd_loss", grid=(s // tm,),
        out_shape=[jax.ShapeDtypeStruct((s, D_MODEL), F32), jax.ShapeDtypeStruct((s, D_FF), BF16),
                   jax.ShapeDtypeStruct((s, D_MODEL), BF16), jax.ShapeDtypeStruct((SLOT, D_MODEL), F32)],
        in_specs=[row_f32, _full((1, D_MODEL)), VMEM_WHOLE, VMEM_WHOLE, _full((1, D_MODEL)), row_f32],
        out_specs=[row_f32, pl.BlockSpec((tm, D_FF), lambda i: (i, 0)), row_f32, _full((SLOT, D_MODEL))],
        compiler_params=_params(("arbitrary",)),
    )(x, g, w1, w2, gf, target)


def _zero_slot(slot_ref):
    @pl.when(pl.program_id(0) == 0)
    def _():
        slot_ref[...] = jnp.zeros_like(slot_ref)


def _mlp_bwd(dx3, u, x2, g, w1, w2):
    s = x2.shape[0]
    tm = min(ROW_TILE, s)

    def body(d_ref, u_ref, x_ref, g_ref, w1_ref, w2_ref, da_ref, dx_ref, slot_ref):
        _zero_slot(slot_ref)
        d = d_ref[...]
        d16 = d.astype(BF16)
        dhf = jnp.zeros((tm, D_MODEL), F32)
        for j in range(N_DEV):
            sl = slice(FF_BLK * j, FF_BLK * (j + 1))
            da = (_mm_nt(d16, w2_ref[j]) * (2.0 * jnp.sqrt(u_ref[:, sl].astype(F32)))).astype(BF16)
            da_ref[:, sl] = da
            dhf = dhf + _mm_nt(da, w1_ref[j])
        dx, dg = _rms_bwd(x_ref[...], g_ref[...], dhf)
        dx_ref[...] = d + dx
        slot_ref[0:1, :] += dg

    row_f32 = pl.BlockSpec((tm, D_MODEL), lambda i: (i, 0))
    return pl.pallas_call(
        body, name="mlp_bwd", grid=(s // tm,),
        out_shape=[jax.ShapeDtypeStruct((s, D_FF), BF16), jax.ShapeDtypeStruct((s, D_MODEL), F32),
                   jax.ShapeDtypeStruct((SLOT, D_MODEL), F32)],
        in_specs=[row_f32, pl.BlockSpec((tm, D_FF), lambda i: (i, 0)), row_f32, _full((1, D_MODEL)),
                  VMEM_WHOLE, VMEM_WHOLE],
        out_specs=[pl.BlockSpec((tm, D_FF), lambda i: (i, 0)), row_f32, _full((SLOT, D_MODEL))],
        compiler_params=_params(("arbitrary",)),
    )(dx3, u, x2, g, w1, w2)


def _wgrad(a, b, name, col_blocks=False):
    s, m = a.shape
    n = b.shape[1]
    tm = 1280 if m % 1280 == 0 else min(1024, m)
    tn = min(1024, n)
    blk = n // N_DEV
    per_step = tn // blk if col_blocks else 1
    ts = min(ROW_TILE, s)
    n_s = s // ts

    def body(a_ref, b_ref, o_ref, acc):
        k = pl.program_id(2)

        @pl.when(k == 0)
        def _():
            acc[...] = jnp.zeros_like(acc)

        acc[...] += _mm_tn(a_ref[...], b_ref[...])

        @pl.when(k == n_s - 1)
        def _():
            if col_blocks:
                for p in range(per_step):
                    o_ref[p] = acc[:, blk * p:blk * (p + 1)].astype(BF16)
            else:
                o_ref[...] = acc[...].astype(BF16)

    if col_blocks:
        out_shape = jax.ShapeDtypeStruct((N_DEV, m, blk), BF16)
        out_spec = pl.BlockSpec((per_step, tm, blk), lambda i, j, k: (j, i, 0))
    else:
        out_shape = jax.ShapeDtypeStruct((m, n), BF16)
        out_spec = pl.BlockSpec((tm, tn), lambda i, j, k: (i, j))
    return pl.pallas_call(
        body, name=name, grid=(m // tm, n // tn, n_s), out_shape=out_shape,
        in_specs=[pl.BlockSpec((ts, tm), lambda i, j, k: (k, i)), pl.BlockSpec((ts, tn), lambda i, j, k: (k, j))],
        out_specs=out_spec,
        scratch_shapes=[pltpu.VMEM((tm, tn), F32)],
        compiler_params=_params(("parallel", "parallel", "arbitrary")),
    )(a, b)


def _xattn_bwd(dx2, x1, g, q, xk, xv, wq, wo_t):
    s = x1.shape[0]
    tm = min(ROW_TILE, s)
    scale = XHD ** -0.5

    def body(d_ref, x_ref, g_ref, q_ref, k_ref, v_ref, wq_ref, wo_ref, dx_ref, dq_ref, dk_ref, dv_ref, slot_ref, datt):
        _zero_slot(slot_ref)

        @pl.when(pl.program_id(0) == 0)
        def _():
            dk_ref[...] = jnp.zeros_like(dk_ref)
            dv_ref[...] = jnp.zeros_like(dv_ref)

        d = d_ref[...]
        datt[...] = _mm(d, wo_ref[...]).astype(BF16)
        for h in range(HEADS):
            sl = slice(XHD * h, XHD * (h + 1))
            qh, kh, vh, dah = q_ref[:, sl], k_ref[:, sl], v_ref[:, sl], datt[:, sl]
            p = _softmax_rows(_mm_nt(qh, kh))
            dp = _mm_nt(dah, vh)
            ds = (p * (dp - jnp.sum(dp * p, axis=-1, keepdims=True))).astype(BF16)
            dq_ref[:, sl] = (_mm(ds, kh) * scale).astype(BF16)
            dk_ref[:, sl] += _mm_tn(ds, qh)
            dv_ref[:, sl] += _mm_tn(p, dah)
        dx, dg = _rms_bwd(x_ref[...], g_ref[...], _mm_nt(dq_ref[...], wq_ref[...]))
        dx_ref[...] = d + dx
        slot_ref[0:1, :] += dg

    row_f32 = pl.BlockSpec((tm, D_MODEL), lambda i: (i, 0))
    kv = jax.ShapeDtypeStruct((MEM_LEN, D_MODEL), F32)
    return pl.pallas_call(
        body, name="xattn_bwd", grid=(s // tm,),
        out_shape=[jax.ShapeDtypeStruct((s, D_MODEL), F32), jax.ShapeDtypeStruct((s, D_MODEL), BF16), kv, kv,
                   jax.ShapeDtypeStruct((SLOT, D_MODEL), F32)],
        in_specs=[row_f32, row_f32, _full((1, D_MODEL)), row_f32, VMEM_WHOLE, VMEM_WHOLE, VMEM_WHOLE, VMEM_WHOLE],
        out_specs=[row_f32, row_f32, _full((MEM_LEN, D_MODEL)), _full((MEM_LEN, D_MODEL)), _full((SLOT, D_MODEL))],
        scratch_shapes=[pltpu.VMEM((tm, D_MODEL), BF16)],
        compiler_params=_params(("arbitrary",)),
    )(dx2, x1, g, q, xk, xv, wq, wo_t)


def _mem_bwd(mem, g, hm, dxk, dxv, wk, wv):
    def body(m_ref, g_ref, hm_ref, dk_ref, dv_ref, wk_ref, wv_ref, dwk_ref, dwv_ref, slot_ref):
        dk, dv = dk_ref[...], dv_ref[...]
        hm_ = hm_ref[...]
        dwk_ref[...] = _mm_tn(hm_, dk).astype(BF16)
        dwv_ref[...] = _mm_tn(hm_, dv).astype(BF16)
        _, dg = _rms_bwd(m_ref[...], g_ref[...], _mm_nt(dk, wk_ref[...]) + _mm_nt(dv, wv_ref[...]))
        slot_ref[...] = jnp.zeros_like(slot_ref)
        slot_ref[0:1, :] = dg

    wshape = jax.ShapeDtypeStruct((D_MODEL, D_MODEL), BF16)
    return pl.pallas_call(
        body, name="mem_bwd", out_shape=[wshape, wshape, jax.ShapeDtypeStruct((SLOT, D_MODEL), F32)],
        in_specs=[VMEM_WHOLE] * 7, out_specs=[VMEM_WHOLE] * 3,
        compiler_params=_params(),
    )(mem, g, hm, dxk, dxv, wk, wv)


def _matmul_nt(a, w, name, dep):
    s, k = a.shape
    n = w.shape[0]
    tm = min(ROW_TILE, s)

    def body(a_ref, w_ref, dep_ref, o_ref):
        o_ref[...] = _mm_nt(a_ref[...], w_ref[...])

    return pl.pallas_call(
        body, name=name, grid=(s // tm,),
        out_shape=jax.ShapeDtypeStruct((s, n), F32),
        in_specs=[pl.BlockSpec((tm, k), lambda i: (i, 0)), VMEM_WHOLE, ANY_SPACE],
        out_specs=pl.BlockSpec((tm, n), lambda i: (i, 0)),
        compiler_params=_params(("parallel",)),
    )(a, w, dep)


def _pool_bwd(dmix, pooled, w_pool, scale):
    s = dmix.shape[0]
    tm = min(ROW_TILE, s)
    n_t = s // tm

    def body(do_ref, pl_ref, w_ref, sc_ref, dz_ref, dw_ref, slot_ref, ext):
        i = pl.program_id(0)
        tile = n_t - 1 - i
        _zero_slot(slot_ref)

        @pl.when(i == 0)
        def _():
            dw_ref[...] = jnp.zeros_like(dw_ref)
            ext[tm:tm + POOL_HALO, :] = jnp.zeros((POOL_HALO, HW), F32)

        @pl.when(i > 0)
        def _():
            ext[tm:tm + POOL_HALO, :] = ext[0:POOL_HALO, :]

        inv = _pool_counts(tile, tm)
        dpooled = []
        for g in range(HEADS):
            sl = slice(HD * g, HD * (g + 1))
            pooled_g = pl_ref[:, sl]
            do = do_ref[:, sl]
            slot_ref[0:1, sl] += jnp.sum(_mm(pooled_g, w_ref[g]) * do, axis=0, keepdims=True)
            dy = (do * sc_ref[:, sl]).astype(BF16)
            dw_ref[g] += _mm_tn(pooled_g, dy)
            dpo = _mm_nt(dy, w_ref[g])
            dpooled.append(dpo)
            ext[0:tm, sl] = dpo * inv[g]
        for g, w in enumerate(POOL_WINDOWS):
            sl = slice(HD * g, HD * (g + 1))
            win = ext[0:tm, sl]
            for d in range(1, w):
                win = win + ext[d:d + tm, sl]
            dz_ref[:, sl] = win - dpooled[g]

    return pl.pallas_call(
        body, name="pool_bwd", grid=(n_t,),
        out_shape=[jax.ShapeDtypeStruct((s, IN_WIDTH), F32), jax.ShapeDtypeStruct((HEADS, HD, HD), F32),
                   jax.ShapeDtypeStruct((SLOT, D_MODEL), F32)],
        in_specs=[pl.BlockSpec((tm, HW), lambda i: (n_t - 1 - i, 1)), pl.BlockSpec((tm, HW), lambda i: (n_t - 1 - i, 0)),
                  _full((HEADS, HD, HD)), _full((1, HW))],
        out_specs=[pl.BlockSpec((tm, HW), lambda i: (n_t - 1 - i, 4)), _full((HEADS, HD, HD)), _full((SLOT, D_MODEL))],
        scratch_shapes=[pltpu.VMEM((tm + POOL_HALO, HW), F32)],
        compiler_params=_params(("arbitrary",)),
    )(dmix, pooled, w_pool, scale)


def _hgrn_bwd(z, o, dmix, states, lb_logits, gn, dz_in):
    s = z.shape[0]
    n_chunks = s // CHUNK

    def body(zq_ref, zf_ref, zi_ref, zg_ref, o_ref, do_ref, st_ref, lbl_ref, gn_ref, dzin_ref,
             dz_ref, dlb_ref, dgn_ref, dstate, b_scr, dlb_acc):
        i = pl.program_id(0)

        @pl.when(i == 0)
        def _():
            dstate[...] = jnp.zeros_like(dstate)
            dlb_acc[...] = jnp.zeros_like(dlb_acc)
            dgn_ref[...] = jnp.zeros_like(dgn_ref)
            dlb_ref[...] = jnp.zeros_like(dlb_ref)

        lb = _sigmoid(lbl_ref[0:1, :] - lbl_ref[1:2, :])
        row, col = _chunk_masks()
        causal = col <= row
        tri = _ones_where(causal)
        upper = _ones_where(col >= row)
        strict_lower = _ones_where(col < row)
        for c in reversed(range(CHUNKS_PER_STEP)):
            r0 = CHUNK * c
            rs = slice(r0, r0 + CHUNK)
            zq = zq_ref[rs, :]
            q, sq, sig, f = _hgrn_gates(zq, zf_ref[rs, :], lb)
            kk = 1.0 - f
            b_scr[rs, :] = _tri_dot(tri, jnp.log(f), 3)
            for h in range(HEADS):
                sl = slice(HD * h, HD * (h + 1))
                oh = o_ref[rs, sl]
                gnh = gn_ref[h:h + 1, :]
                zg = zg_ref[rs, sl]
                sg = _sigmoid(zg)
                doa = do_ref[rs, sl]
                don = doa * (zg * sg)
                d_o, dgn = _rms_bwd(oh, gnh, don)
                dgn_ref[h:h + 1, 0:HD] += dgn
                dz_ref[rs, 3 * HW + HD * h:3 * HW + HD * (h + 1)] = (
                    doa * (oh * _rms(oh) * gnh) * (sg * (1.0 + zg * (1.0 - sg))))
                bh = b_scr[rs, sl]
                qh, kh, vh = q[:, sl], kk[:, sl], zi_ref[rs, sl]
                st0 = st_ref[c, h]
                ds1 = dstate[h]
                b_last = b_scr[r0 + CHUNK - 1:r0 + CHUNK, sl]
                lam = jnp.exp(bh)
                e_last = jnp.exp(b_last - bh)
                lam_last = jnp.exp(b_last)
                qcat, kecat, eqcat, ekcat = _hgrn_intra_factors(b_scr, r0, bh, qh, kh, sl)
                a = jnp.where(causal, _mm_nt(qcat, kecat), 0.0)
                da = jnp.where(causal, _mm_nt(d_o, vh), 0.0)
                dz_ref[rs, 2 * HW + HD * h:2 * HW + HD * (h + 1)] = _mm_tn(a, d_o) + _mm_nt(kh * e_last, ds1)
                q16, ke16 = qcat.astype(BF16), kecat.astype(BF16)
                gq = _mm(da, ke16)
                gk = _mm_tn(da, q16)
                dq_inter = lam * _mm(d_o, st0)
                dq = _sum_lane_blocks(eqcat * gq) + dq_inter
                dk_intra = _sum_lane_blocks(ekcat * gk)
                dk_state = _mm(vh, ds1) * e_last
                state_term = lam_last * jnp.sum(st0 * ds1, axis=0, keepdims=True)
                dstate[h] = ds1 * lam_last + _mm_tn(d_o, qh * lam)
                db_intra = _sum_lane_blocks(q16.astype(F32) * gq - ke16.astype(F32) * gk)
                dlf = (_tri_dot(upper, db_intra + qh * dq_inter, 2) + _tri_dot(strict_lower, kh * dk_state, 2)
                       + state_term)
                sigh = sig[:, sl]
                df = dlf / f[:, sl] - (dk_intra + dk_state)
                dlb_acc[:, sl] += jnp.sum(df * (1.0 - sigh), axis=0, keepdims=True)
                dz_ref[rs, HW + HD * h:HW + HD * (h + 1)] = df * (1.0 - lb[:, sl]) * sigh * (1.0 - sigh)
                sqh = sq[:, sl]
                dz_ref[rs, sl] = dq * (sqh * (1.0 + zq[:, sl] * (1.0 - sqh)))

        @pl.when(i == n_steps - 1)
        def _():
            dl0 = dlb_acc[...] * lb * (1.0 - lb)
            dlb_ref[0:1, 0:HW] = dl0
            dlb_ref[1:2, 0:HW] = -dl0

    rows = CHUNK * CHUNKS_PER_STEP
    n_steps = s // rows
    rev = lambda i: n_steps - 1 - i
    zspec = lambda cb: pl.BlockSpec((rows, HW), lambda i, cb=cb: (rev(i), cb))
    slot = jax.ShapeDtypeStruct((SLOT, D_MODEL), F32)
    return pl.pallas_call(
        body, name="hgrn_bwd", grid=(n_steps,),
        out_shape=[jax.ShapeDtypeStruct((s, IN_WIDTH), F32), slot, slot],
        in_specs=[zspec(0), zspec(1), zspec(2), zspec(3), pl.BlockSpec((rows, HW), lambda i: (rev(i), 0)),
                  pl.BlockSpec((rows, HW), lambda i: (rev(i), 0)),
                  pl.BlockSpec((CHUNKS_PER_STEP, HEADS, HD, HD), lambda i: (rev(i), 0, 0, 0)), _full((2, HW)),
                  _full((HEADS, HD)), ANY_SPACE],
        out_specs=[pl.BlockSpec((rows, 4 * HW), lambda i: (rev(i), 0)), _full((SLOT, D_MODEL)), _full((SLOT, D_MODEL))],
        scratch_shapes=[pltpu.VMEM((HEADS, HD, HD), F32), pltpu.VMEM((rows, HW), F32), pltpu.VMEM((1, HW), F32)],
        input_output_aliases={9: 0},
        compiler_params=_params(("arbitrary",)),
    )(z, z, z, z, o, dmix, states, lb_logits, gn, dz_in)


def _in_bwd(dz, w_t, x0, g, dx1):
    s = x0.shape[0]
    tm = min(ROW_TILE, s)

    def body(dz_ref, w_ref, x_ref, g_ref, d_ref, dx_ref, slot_ref):
        _zero_slot(slot_ref)
        dx, dg = _rms_bwd(x_ref[...], g_ref[...], _mm(dz_ref[...], w_ref[...]))
        dx_ref[...] = d_ref[...] + dx
        slot_ref[0:1, :] += dg

    row_f32 = pl.BlockSpec((tm, D_MODEL), lambda i: (i, 0))
    return pl.pallas_call(
        body, name="in_bwd", grid=(s // tm,),
        out_shape=[jax.ShapeDtypeStruct((s, D_MODEL), F32), jax.ShapeDtypeStruct((SLOT, D_MODEL), F32)],
        in_specs=[pl.BlockSpec((tm, IN_WIDTH), lambda i: (i, 0)), VMEM_WHOLE, row_f32, _full((1, D_MODEL)), row_f32],
        out_specs=[row_f32, _full((SLOT, D_MODEL))],
        compiler_params=_params(("arbitrary",)),
    )(dz, w_t, x0, g, dx1)


def kernel(x, mem, norm_mix_g, w_in, lb_logits, hgrn_norm_g, w_pool, pool_scale, w_out, norm_x_g, norm_mem_g, w_xq, w_xk, w_xv, w_xo, norm_ffn_g, w_ff1, w_ff2, final_norm_g, loss_target, m_norm_mix_g, m_w_in, m_lb_logits, m_hgrn_norm_g, m_w_pool, m_pool_scale, m_w_out, m_norm_x_g, m_norm_mem_g, m_w_xq, m_w_xk, m_w_xv, m_w_xo, m_norm_ffn_g, m_w_ff1, m_w_ff2, m_final_norm_g, v_norm_mix_g, v_w_in, v_lb_logits, v_hgrn_norm_g, v_w_pool, v_pool_scale, v_w_out, v_norm_x_g, v_norm_mem_g, v_w_xq, v_w_xk, v_w_xv, v_w_xo, v_norm_ffn_g, v_w_ff1, v_w_ff2, v_final_norm_g):
    x0 = x[0]
    mem0 = mem[0]
    tgt = loss_target[0]
    gn = hgrn_norm_g[0]
    gfin = final_norm_g.reshape(1, D_MODEL)
    wp = w_pool[0]
    heads_2d = lambda w: w.reshape(D_MODEL // N_DEV, D_MODEL)
    xo_2d = lambda w: w.reshape(D_MODEL, D_MODEL // N_DEV)

    first = _all_gather_weights([w_in[0].T], [w_out[0], heads_2d(w_xq), heads_2d(w_xk), heads_2d(w_xv), xo_2d(w_xo).T,
                                              w_ff1[0], w_ff2[0]])
    win_t = first[0].reshape(IN_WIDTH, D_MODEL)
    ga_attn = _gather_first_start(first[1:6], "gather_attn_first_start")
    ga_mlp = _gather_first_start(first[6:8], "gather_mlp_first_start")

    z, h = _in_proj(x0, _after(_after(norm_mix_g, ga_attn[3]), ga_mlp[3]), win_t)
    mixed_a, o_pre, states = _hgrn_fwd(z, lb_logits, gn)
    lands = _split_wait(_gather_first_copies, ga_attn, o_pre, "gather_attn_first_wait")
    gb_attn = _gather_forward_start(lands, "gather_attn_forward_start")
    mixed, pooled = _pool_fwd(z, wp, _after(pool_scale, gb_attn[3]), mixed_a)
    lands = _split_wait(_gather_forward_copies, gb_attn, pooled, "gather_attn_forward_wait")
    wout_f, wq_f, wk_f, wv_f, wo_t = (t.reshape(D_MODEL, D_MODEL) for t in lands)
    x1 = _out_proj(x0, mixed, wout_f)
    lands = _split_wait(_gather_first_copies, ga_mlp, x1, "gather_mlp_first_wait")
    gb_mlp = _gather_forward_start(lands, "gather_mlp_forward_start")
    hm, xk, xv = _mem_kv(mem0, _after(norm_mem_g, gb_mlp[3]), wk_f, wv_f)
    x2, hq, xq, att = _xattn_fwd(x1, norm_x_g, wq_f, xk, xv, wo_t)
    w1_b, w2_b = _split_wait(_gather_forward_copies, gb_mlp, x2, "gather_mlp_forward_wait")
    dx3, u, hf, slot_fin = _mlp_fwd_loss(x2, norm_ffn_g, w1_b, w2_b, gfin, tgt)

    rows = lambda t, r: t.reshape(N_DEV, r, D_MODEL)
    da, dx2, slot_ffn = _mlp_bwd(dx3, u, x2, norm_ffn_g, w1_b, w2_b)
    dw2 = _wgrad(u, dx3, "wgrad_ff2")
    dw1 = _wgrad(hf, da, "wgrad_ff1", col_blocks=True)
    ex_ff = _all_to_all_start([dw1, rows(dw2, FF_BLK)], [], "exchange_ff_start")
    dx1, dxq, dxk, dxv, slot_x = _xattn_bwd(dx2, x1, _after(norm_x_g, ex_ff[3]), xq, xk, xv, wq_f, wo_t)
    dwo_t = _wgrad(dx2, att, "wgrad_xo")
    dwq = _wgrad(hq, dxq, "wgrad_xq")
    dwk, dwv, slot_mem = _mem_bwd(mem0, norm_mem_g, hm, dxk, dxv, wk_f, wv_f)
    ex_attn = _all_to_all_start([rows(dwq, 128), rows(dwk, 128), rows(dwv, 128), rows(dwo_t, 128)], [],
                                "exchange_attn_start")
    dmix = _matmul_nt(dx1, wout_f, "out_proj_bwd", dep=ex_attn[3])
    dwout = _wgrad(mixed, dx1, "wgrad_out")
    dz_pool, d_wpool, slot_ps = _pool_bwd(dmix, pooled, wp, pool_scale)
    small0 = jnp.concatenate([slot_x, slot_mem, slot_ffn, slot_fin, slot_ps], axis=0)
    ex_out = _all_to_all_start([rows(dwout, 128)], [small0, d_wpool], "exchange_out_start")
    dz, slot_lb, slot_gn = _hgrn_bwd(z, o_pre, dmix, states, _after(lb_logits, ex_out[3]), gn, dz_pool)
    dwin_t = _wgrad(dz, h, "wgrad_in")
    small1 = jnp.concatenate([slot_lb, slot_gn], axis=0)
    ex_in = _all_to_all_start([rows(dwin_t, 320)], [small1], "exchange_in_start")
    grad_x, slot_mix = _in_bwd(dz, win_t, x0, _after(norm_mix_g, ex_in[3]), dx1)
    ex_mix = _all_to_all_start([], [slot_mix], "exchange_mix_start")

    out = {}
    r_1, r_2 = _split_wait(_all_to_all_copies(2), ex_ff, ex_mix[3], "exchange_ff_wait")
    out["w_ff1"] = _sum_adamw(r_1, w_ff1[0], m_w_ff1[0], v_w_ff1[0], "adamw_ff1")
    out["w_ff2"] = _sum_adamw(r_2, w_ff2[0], m_w_ff2[0], v_w_ff2[0], "adamw_ff2")
    r_q, r_k, r_v, r_o = _split_wait(_all_to_all_copies(4), ex_attn, out["w_ff2"][1], "exchange_attn_wait")
    out["w_xq"] = _sum_adamw(r_q, heads_2d(w_xq), heads_2d(m_w_xq), heads_2d(v_w_xq), "adamw_xq")
    out["w_xk"] = _sum_adamw(r_k, heads_2d(w_xk), heads_2d(m_w_xk), heads_2d(v_w_xk), "adamw_xk")
    out["w_xv"] = _sum_adamw(r_v, heads_2d(w_xv), heads_2d(m_w_xv), heads_2d(v_w_xv), "adamw_xv")
    g_xo = _sum_sources(r_o, "sum_grad_xo").T
    out["w_xo"] = (g_xo, *_adamw(g_xo, xo_2d(w_xo), xo_2d(m_w_xo), xo_2d(v_w_xo), "adamw_xo"))
    r_out, r_small0, r_wpool = _split_wait(_all_to_all_copies(1), ex_out, out["w_xo"][1], "exchange_out_wait")
    out["w_out"] = _sum_adamw(r_out, w_out[0], m_w_out[0], v_w_out[0], "adamw_out")
    r_in, r_small1 = _split_wait(_all_to_all_copies(1), ex_in, out["w_out"][1], "exchange_in_wait")
    g_in = _sum_sources(r_in, "sum_grad_in").T
    out["w_in"] = (g_in, *_adamw(g_in, w_in[0], m_w_in[0], v_w_in[0], "adamw_in"))
    (r_small2,) = _split_wait(_all_to_all_copies(0), ex_mix, out["w_in"][1], "exchange_mix_wait")
    row = lambda t: t.reshape(1, -1)
    small_params = {
        "norm_mix_g": (norm_mix_g, m_norm_mix_g, v_norm_mix_g),
        "lb_logits": (lb_logits, m_lb_logits, v_lb_logits),
        "hgrn_norm_g": (hgrn_norm_g[0], m_hgrn_norm_g[0], v_hgrn_norm_g[0]),
        "pool_scale": (pool_scale, m_pool_scale, v_pool_scale),
        "norm_x_g": (norm_x_g, m_norm_x_g, v_norm_x_g),
        "norm_mem_g": (norm_mem_g, m_norm_mem_g, v_norm_mem_g),
        "norm_ffn_g": (norm_ffn_g, m_norm_ffn_g, v_norm_ffn_g),
        "final_norm_g": (row(final_norm_g), row(m_final_norm_g), row(v_final_norm_g)),
        "w_pool": (wp, m_w_pool[0], v_w_pool[0]),
    }
    loss, small_out = _small_update([r_small0, r_small1, r_small2], r_wpool, small_params)
    out.update(small_out)

    shapes = dict(norm_mix_g=norm_mix_g, w_in=w_in, lb_logits=lb_logits, hgrn_norm_g=hgrn_norm_g, w_pool=w_pool,
                  pool_scale=pool_scale, w_out=w_out, norm_x_g=norm_x_g, norm_mem_g=norm_mem_g, w_xq=w_xq, w_xk=w_xk,
                  w_xv=w_xv, w_xo=w_xo, norm_ffn_g=norm_ffn_g, w_ff1=w_ff1, w_ff2=w_ff2, final_norm_g=final_norm_g)
    order = list(shapes)
    group = lambda k: [out[n][k].reshape(shapes[n].shape) for n in order]
    return (loss.reshape(()), grad_x.reshape(x.shape), *group(0), *group(1), *group(2), *group(3))
```

```python
import jax
import jax.numpy as jnp
from jax import lax
from jax.experimental import pallas as pl
from jax.experimental.pallas import tpu as pltpu

F32 = jnp.float32
BF16 = jnp.bfloat16

D_MODEL = 1024
N_DEV = 8
HEADS = 4
HD = 128
HW = HEADS * HD
IN_WIDTH = 5 * HW
XHD = 256
MEM_LEN = 256
D_FF = 4096
FF_BLK = D_FF // N_DEV
POOL_WINDOWS = (2, 4, 8, 16)
POOL_HALO = 16
CHUNK = 64
CHUNKS_PER_STEP = 4
SUB = 16
N_SUB = CHUNK // SUB
EXP_CAP = 80.0
EPS = 1e-6
ROW_TILE = 512
SLOT = 8
V7X_VMEM_LIMIT = 56 * 1024 * 1024

ADAM_LR = 0.001
ADAM_B1 = 0.9
ADAM_B2 = 0.999
ADAM_EPS = 1e-08
ADAM_WD = 0.01
ADAM_STEP = 10

MESH_ID = pl.DeviceIdType.MESH


def _params(sem=None, vmem=V7X_VMEM_LIMIT):
    return pltpu.CompilerParams(dimension_semantics=sem, vmem_limit_bytes=vmem)


def _mm(a, b):
    return lax.dot_general(a.astype(BF16), b.astype(BF16), (((1,), (0,)), ((), ())), preferred_element_type=F32)


def _mm_nt(a, b):
    return lax.dot_general(a.astype(BF16), b.astype(BF16), (((1,), (1,)), ((), ())), preferred_element_type=F32)


def _mm_tn(a, b):
    return lax.dot_general(a.astype(BF16), b.astype(BF16), (((0,), (0,)), ((), ())), preferred_element_type=F32)


def _sigmoid(x):
    return 1.0 / (1.0 + jnp.exp(-x))


def _rms(x):
    return lax.rsqrt(jnp.mean(x * x, axis=-1, keepdims=True) + EPS)


def _rms_bwd(x, g, dh):
    r = _rms(x)
    n = x * r
    dn = dh * g
    dx = r * (dn - n * jnp.mean(dn * n, axis=-1, keepdims=True))
    return dx, jnp.sum(dh * n, axis=0, keepdims=True)


def _tri_dot(tri, x, passes):
    acc = None
    rest = x
    for _ in range(passes):
        piece = rest.astype(BF16)
        part = lax.dot_general(tri, piece, (((1,), (0,)), ((), ())), preferred_element_type=F32)
        acc = part if acc is None else acc + part
        rest = rest - piece.astype(F32)
    return acc


def _adam_update(g, w, m, v):
    nm = ADAM_B1 * m + (1.0 - ADAM_B1) * g
    nv = ADAM_B2 * v + (1.0 - ADAM_B2) * (g * g)
    m_hat = nm / (1.0 - ADAM_B1 ** ADAM_STEP)
    v_hat = nv / (1.0 - ADAM_B2 ** ADAM_STEP)
    return -ADAM_LR * (m_hat / (jnp.sqrt(v_hat) + ADAM_EPS) + ADAM_WD * w), nm, nv


def _full(shape):
    return pl.BlockSpec(shape, lambda *_: (0,) * len(shape))


VMEM_WHOLE = pl.BlockSpec(memory_space=pltpu.VMEM)
ANY_SPACE = pl.BlockSpec(memory_space=pl.ANY)


def _mesh_pos():
    return lax.axis_index("x"), lax.axis_index("y"), lax.axis_index("c")


def _flat(px, py, pc):
    return 4 * px + 2 * py + pc


def _all_gather_weights(shards, cast_only):
    n, nc = len(shards), len(cast_only)
    step = 64

    def body(*refs):
        x_refs, c_refs = refs[:n], refs[n:n + nc]
        out_refs, cast_refs = refs[n + nc:2 * n + nc], refs[2 * n + nc:2 * n + 2 * nc]
        bufs = refs[2 * n + 2 * nc:3 * n + 2 * nc]
        send_sems, recv_sems, local_sems = refs[3 * n + 2 * nc:]
        x, y, c = _mesh_pos()
        me, sibling = (x, y, c), (x, y, 1 - c)
        chips = [(1 - x, y), (x, 1 - y), (1 - x, 1 - y)]

        def copy(a, k, blk, to, src=None):
            rows = out_refs[a].at[_flat(*blk)]
            return pltpu.make_async_remote_copy(
                src_ref=rows if src is None else src, dst_ref=rows,
                send_sem=send_sems.at[7 * a + k], recv_sem=recv_sems.at[7 * a + k], device_id=to, device_id_type=MESH_ID)

        def cast_rows(src, dst, rows):
            def cast(i, carry):
                r0 = pl.multiple_of(i * step, step)
                dst[pl.ds(r0, step), :] = src[pl.ds(r0, step), :].astype(BF16)
                return carry
            lax.fori_loop(0, rows // step, cast, 0)

        first, mine = [], []
        for a in range(n):
            cast_rows(x_refs[a], bufs[a], shards[a].shape[0])
            mine.append(pltpu.make_async_copy(bufs[a], out_refs[a].at[_flat(*me)], local_sems.at[a]))
            first.append(copy(a, 0, me, sibling, src=bufs[a]))
            first += [copy(a, 1 + j, me, (*chip, c), src=bufs[a]) for j, chip in enumerate(chips)]
            for cp in [mine[-1]] + first[-4:]:
                cp.start()
        for a in range(nc):
            cast_rows(c_refs[a], cast_refs[a], cast_only[a].shape[0])
        passed = []
        for j, chip in enumerate(chips):
            for a in range(n):
                copy(a, 1 + j, (*chip, c), me).wait_recv()
                passed.append(copy(a, 4 + j, (*chip, c), sibling))
                passed[-1].start()
        for a in range(n):
            copy(a, 0, sibling, me).wait_recv()
            for j, chip in enumerate(chips):
                copy(a, 4 + j, (*chip, 1 - c), me).wait_recv()
        for cp in first + passed:
            cp.wait_send()
        for cp in mine:
            cp.wait()

    return pl.pallas_call(
        body, name="all_gather_w_in",
        out_shape=[jax.ShapeDtypeStruct((N_DEV,) + s.shape, BF16) for s in shards]
        + [jax.ShapeDtypeStruct(s.shape, BF16) for s in cast_only],
        in_specs=[VMEM_WHOLE] * (n + nc), out_specs=[ANY_SPACE] * n + [VMEM_WHOLE] * nc,
        scratch_shapes=[pltpu.VMEM(s.shape, BF16) for s in shards]
        + [pltpu.SemaphoreType.DMA((7 * n,)), pltpu.SemaphoreType.DMA((7 * n,)), pltpu.SemaphoreType.DMA((n,))],
        compiler_params=_params(),
    )(*shards, *cast_only)


HBM_SPEC = pl.BlockSpec(memory_space=pltpu.HBM)
SEM_SPEC = pl.BlockSpec(memory_space=pltpu.SEMAPHORE)
EFFECT = pltpu.SideEffectType.DATAFLOW_SIDE_EFFECTING
TOKEN = jax.ShapeDtypeStruct((8, 128), F32)


def _in_hbm(a):
    return pltpu.with_memory_space_constraint(a, pltpu.HBM)


def _split_start(copies_of, srcs, lands, n_sems, name):
    ns, nl, k = len(srcs), len(lands), len(n_sems)

    def body(*refs):
        src_refs, land_refs = refs[:ns], refs[ns:ns + nl]
        sems = refs[ns + nl:ns + nl + k]
        token = refs[-1]
        for cp in copies_of(src_refs, land_refs, sems):
            cp.start()
        token[...] = jnp.zeros_like(token)

    outs = pl.pallas_call(
        body, name=name,
        out_shape=[pltpu.SemaphoreType.DMA((q,)) for q in n_sems]
        + [pltpu.HBM(a.shape, a.dtype) for a in list(srcs) + list(lands)] + [TOKEN],
        in_specs=[HBM_SPEC] * (ns + nl),
        out_specs=[SEM_SPEC] * k + [HBM_SPEC] * (ns + nl) + [VMEM_WHOLE],
        input_output_aliases={i: k + i for i in range(ns + nl)},
        compiler_params=pltpu.CompilerParams(has_side_effects=EFFECT),
    )(*[_in_hbm(a) for a in list(srcs) + list(lands)])
    return outs[:k], outs[k:k + ns], outs[k + ns:k + ns + nl], outs[-1]


def _split_wait(copies_of, handle, after, name):
    sems, srcs, lands, _ = handle
    ns, nl, k = len(srcs), len(lands), len(sems)

    def body(*refs):
        src_refs, land_refs = refs[:ns], refs[ns:ns + nl]
        sem_refs = refs[ns + nl:ns + nl + k]
        for cp in copies_of(src_refs, land_refs, sem_refs):
            cp.wait()

    outs = pl.pallas_call(
        body, name=name,
        out_shape=[pltpu.HBM(a.shape, a.dtype) for a in list(srcs) + list(lands)],
        in_specs=[HBM_SPEC] * (ns + nl) + [SEM_SPEC] * k + [ANY_SPACE],
        out_specs=[HBM_SPEC] * (ns + nl),
        input_output_aliases={i: i for i in range(ns + nl)},
        compiler_params=pltpu.CompilerParams(has_side_effects=EFFECT),
    )(*srcs, *lands, *sems, after)
    return outs[ns:]


def _gather_first_copies(shard_refs, land_refs, sems):
    send_sems, recv_sems, local_sems = sems
    x, y, c = _mesh_pos()
    me = _flat(x, y, c)
    peers = [(x, y, 1 - c), (1 - x, y, c), (x, 1 - y, c), (1 - x, 1 - y, c)]
    copies = []
    for a, (shard, land) in enumerate(zip(shard_refs, land_refs)):
        copies.append(pltpu.make_async_copy(shard, land.at[me], local_sems.at[a]))
        for k, peer in enumerate(peers):
            copies.append(pltpu.make_async_remote_copy(
                src_ref=shard, dst_ref=land.at[me], send_sem=send_sems.at[4 * a + k], recv_sem=recv_sems.at[4 * a + k],
                device_id=peer, device_id_type=MESH_ID))
    return copies


def _gather_forward_copies(src_refs, land_refs, sems):
    del src_refs
    send_sems, recv_sems = sems
    x, y, c = _mesh_pos()
    chips = [(1 - x, y), (x, 1 - y), (1 - x, 1 - y)]
    copies = []
    for a, land in enumerate(land_refs):
        for j, chip in enumerate(chips):
            rows = land.at[_flat(*chip, c)]
            copies.append(pltpu.make_async_remote_copy(
                src_ref=rows, dst_ref=rows, send_sem=send_sems.at[3 * a + j], recv_sem=recv_sems.at[3 * a + j],
                device_id=(x, y, 1 - c), device_id_type=MESH_ID))
    return copies


def _gather_first_start(groups, name):
    shards = [s for g in groups for s in g]
    lands = [lax.empty((N_DEV,) + s.shape, s.dtype) for s in shards]
    bounds = [sum(len(g) for g in groups[:i]) for i in range(len(groups) + 1)]

    def copies_of(src_refs, land_refs, sems):
        copies = []
        for i in range(len(groups)):
            lo, hi = bounds[i], bounds[i + 1]
            copies += _gather_first_copies(src_refs[lo:hi], land_refs[lo:hi], sems[3 * i:3 * i + 3])
        return copies

    n_sems = tuple(q for g in groups for q in (4 * len(g), 4 * len(g), len(g)))
    sems, srcs, lands, token = _split_start(copies_of, shards, lands, n_sems, name)
    return [(sems[3 * i:3 * i + 3], srcs[bounds[i]:bounds[i + 1]], lands[bounds[i]:bounds[i + 1]], token)
            for i in range(len(groups))]


def _gather_forward_start(lands, name):
    n = len(lands)
    return _split_start(_gather_forward_copies, [], lands, (3 * n, 3 * n), name)


def _all_to_all_copies(n_scattered):
    def copies_of(src_refs, land_refs, sems):
        send_sems, recv_sems, local_sems = sems
        x, y, c = _mesh_pos()
        me = _flat(x, y, c)
        copies = []
        for a, (src, land) in enumerate(zip(src_refs, land_refs)):
            scattered = a < n_scattered
            copies.append(pltpu.make_async_copy(src.at[me] if scattered else src, land.at[me], local_sems.at[a]))
            for k in range(1, N_DEV):
                peer = (1 - x if k & 4 else x, 1 - y if k & 2 else y, 1 - c if k & 1 else c)
                copies.append(pltpu.make_async_remote_copy(
                    src_ref=src.at[_flat(*peer)] if scattered else src, dst_ref=land.at[me],
                    send_sem=send_sems.at[7 * a + k - 1], recv_sem=recv_sems.at[7 * a + k - 1],
                    device_id=peer, device_id_type=MESH_ID))
        return copies
    return copies_of


def _all_to_all_start(scattered, broadcast, name):
    srcs = list(scattered) + list(broadcast)
    lands = [lax.empty(a.shape, a.dtype) for a in scattered] + [lax.empty((N_DEV,) + a.shape, a.dtype) for a in broadcast]
    n = len(srcs)
    return _split_start(_all_to_all_copies(len(scattered)), srcs, lands, (7 * n, 7 * n, n), name)


def _after(a, token):
    return lax.optimization_barrier((a, token))[0]


def _row_tile(rows):
    for cand in (256, 128, 64, 32, 16):
        if rows % cand == 0:
            return cand
    return rows


def _sum_sources(recv, name):
    _, rows, cols = recv.shape
    tile = _row_tile(rows)

    def body(r_ref, o_ref):
        acc = r_ref[0].astype(F32)
        for d in range(1, N_DEV):
            acc = acc + r_ref[d].astype(F32)
        o_ref[...] = acc

    return pl.pallas_call(
        body, name=name, grid=(rows // tile,),
        out_shape=jax.ShapeDtypeStruct((rows, cols), F32),
        in_specs=[pl.BlockSpec((N_DEV, tile, cols), lambda i: (0, i, 0))],
        out_specs=pl.BlockSpec((tile, cols), lambda i: (i, 0)),
        compiler_params=_params(("parallel",)),
    )(recv)


def _adamw(g, w, m, v, name):
    rows, cols = g.shape
    tile = _row_tile(rows)

    def body(g_ref, w_ref, m_ref, v_ref, d_ref, nm_ref, nv_ref):
        d_ref[...], nm_ref[...], nv_ref[...] = _adam_update(g_ref[...], w_ref[...], m_ref[...], v_ref[...])

    spec = pl.BlockSpec((tile, cols), lambda i: (i, 0))
    shp = jax.ShapeDtypeStruct((rows, cols), F32)
    return pl.pallas_call(
        body, name=name, grid=(rows // tile,), out_shape=[shp, shp, shp],
        in_specs=[spec] * 4, out_specs=[spec] * 3,
        compiler_params=_params(("parallel",)),
    )(g, w, m, v)


def _adamw_whole(g, w, m, v, name):
    def body(g_ref, w_ref, m_ref, v_ref, d_ref, nm_ref, nv_ref):
        d_ref[...], nm_ref[...], nv_ref[...] = _adam_update(g_ref[...], w_ref[...], m_ref[...], v_ref[...])

    shp = jax.ShapeDtypeStruct(g.shape, F32)
    return pl.pallas_call(
        body, name=name, out_shape=[shp, shp, shp], in_specs=[VMEM_WHOLE] * 4, out_specs=[VMEM_WHOLE] * 3,
        compiler_params=_params(),
    )(g, w, m, v)


def _sum_adamw(recv, w, m, v, name):
    _, rows, cols = recv.shape
    tile = _row_tile(rows)

    def body(r_ref, w_ref, m_ref, v_ref, g_ref, d_ref, nm_ref, nv_ref):
        acc = r_ref[0].astype(F32)
        for d in range(1, N_DEV):
            acc = acc + r_ref[d].astype(F32)
        g_ref[...] = acc
        d_ref[...], nm_ref[...], nv_ref[...] = _adam_update(acc, w_ref[...], m_ref[...], v_ref[...])

    spec = pl.BlockSpec((tile, cols), lambda i: (i, 0))
    shp = jax.ShapeDtypeStruct((rows, cols), F32)
    return pl.pallas_call(
        body, name=name, grid=(rows // tile,), out_shape=[shp] * 4,
        in_specs=[pl.BlockSpec((N_DEV, tile, cols), lambda i: (0, i, 0)), spec, spec, spec], out_specs=[spec] * 4,
        compiler_params=_params(("parallel",)),
    )(recv, w, m, v)


SMALL_SLOTS = {"norm_x_g": (0, 0, 1, D_MODEL), "norm_mem_g": (0, 8, 1, D_MODEL), "norm_ffn_g": (0, 16, 1, D_MODEL),
               "final_norm_g": (0, 24, 1, D_MODEL), "pool_scale": (0, 32, 1, HW),
               "lb_logits": (1, 0, 2, HW), "hgrn_norm_g": (1, 8, HEADS, HD), "norm_mix_g": (2, 0, 1, D_MODEL)}
LOSS_ROW = 25
SMALL_ORDER = ("norm_mix_g", "lb_logits", "hgrn_norm_g", "pool_scale", "norm_x_g", "norm_mem_g", "norm_ffn_g",
               "final_norm_g", "w_pool")


def _small_update(srecvs, wprecv, params):
    flat = [t for n in SMALL_ORDER for t in params[n]]
    nb = len(srecvs)
    n_in = nb + 1 + len(flat)

    def body(*refs):
        s_refs, wp_ref = refs[0:nb], refs[nb]
        in_refs = refs[nb + 1:n_in]
        loss_ref = refs[n_in]
        out_refs = refs[n_in + 1:-nb]
        accs = refs[-nb:]
        for s_ref, acc in zip(s_refs, accs):
            total = s_ref[0]
            for d in range(1, N_DEV):
                total = total + s_ref[d]
            acc[...] = total
        loss_ref[...] = accs[0][LOSS_ROW:LOSS_ROW + 1, 0:1]
        for i, name in enumerate(SMALL_ORDER):
            w_ref, m_ref, v_ref = in_refs[3 * i:3 * i + 3]
            g_ref, d_ref, nm_ref, nv_ref = out_refs[4 * i:4 * i + 4]
            if name == "w_pool":
                g = wp_ref[0]
                for d in range(1, N_DEV):
                    g = g + wp_ref[d]
            else:
                buf, r0, nr, nc = SMALL_SLOTS[name]
                g = accs[buf][r0:r0 + nr, 0:nc]
            g_ref[...] = g
            d_ref[...], nm_ref[...], nv_ref[...] = _adam_update(g, w_ref[...], m_ref[...], v_ref[...])

    out_shape = [jax.ShapeDtypeStruct((1, 1), F32)]
    for n in SMALL_ORDER:
        out_shape += [jax.ShapeDtypeStruct(params[n][0].shape, F32)] * 4
    outs = pl.pallas_call(
        body, name="small_update", out_shape=out_shape,
        in_specs=[VMEM_WHOLE] * n_in, out_specs=[VMEM_WHOLE] * len(out_shape),
        scratch_shapes=[pltpu.VMEM(r.shape[1:], F32) for r in srecvs],
        compiler_params=_params(),
    )(*srecvs, wprecv, *flat)
    return outs[0], {n: outs[1 + 4 * i:5 + 4 * i] for i, n in enumerate(SMALL_ORDER)}


def _in_proj(x, g, w_t):
    s = x.shape[0]
    tm = min(ROW_TILE, s)

    def body(x_ref, g_ref, w_ref, z_ref, h_ref):
        xv = x_ref[...]
        h = (xv * _rms(xv) * g_ref[...]).astype(BF16)
        h_ref[...] = h
        z_ref[...] = _mm_nt(h, w_ref[...])

    return pl.pallas_call(
        body, name="in_proj", grid=(s // tm,),
        out_shape=[jax.ShapeDtypeStruct((s, IN_WIDTH), F32), jax.ShapeDtypeStruct((s, D_MODEL), BF16)],
        in_specs=[pl.BlockSpec((tm, D_MODEL), lambda i: (i, 0)), _full((1, D_MODEL)), VMEM_WHOLE],
        out_specs=[pl.BlockSpec((tm, IN_WIDTH), lambda i: (i, 0)), pl.BlockSpec((tm, D_MODEL), lambda i: (i, 0))],
        compiler_params=_params(("parallel",)),
    )(x, g, w_t)


def _chunk_masks():
    row = lax.broadcasted_iota(jnp.int32, (CHUNK, CHUNK), 0)
    col = lax.broadcasted_iota(jnp.int32, (CHUNK, CHUNK), 1)
    return row, col


def _ones_where(mask):
    return jnp.where(mask, 1.0, 0.0).astype(BF16)


def _hgrn_gates(zq, zf, lb):
    sq = _sigmoid(zq)
    sig = _sigmoid(zf)
    f = lb + (1.0 - lb) * sig
    return zq * sq, sq, sig, f


def _hgrn_intra_factors(b_scr, r0, bh, qh, kh, sl):
    trow = lax.broadcasted_iota(jnp.int32, (CHUNK, HD), 0)
    eq, ek = [], []
    for j in range(N_SUB):
        if j == 0:
            base = jnp.zeros((1, HD), F32)
        else:
            base = b_scr[r0 + SUB * j - 1:r0 + SUB * j, sl]
        in_j = (trow >= SUB * j) & (trow < SUB * (j + 1))
        eq.append(jnp.where(in_j, jnp.exp(bh - base), 0.0))
        ek.append(jnp.where(trow < SUB * (j + 1), jnp.exp(jnp.minimum(base - bh, EXP_CAP)), 0.0))
    eqcat = jnp.concatenate(eq, axis=1)
    ekcat = jnp.concatenate(ek, axis=1)
    qcat = jnp.concatenate([qh] * N_SUB, axis=1) * eqcat
    kecat = jnp.concatenate([kh] * N_SUB, axis=1) * ekcat
    return qcat, kecat, eqcat, ekcat


def _sum_lane_blocks(a):
    out = a[:, 0:HD]
    for j in range(1, N_SUB):
        out = out + a[:, HD * j:HD * (j + 1)]
    return out


def _hgrn_fwd(z, lb_logits, gn):
    s = z.shape[0]
    n_chunks = s // CHUNK

    def body(zq_ref, zf_ref, zi_ref, zg_ref, lbl_ref, gn_ref, oa_ref, o_ref, st_ref, state, b_scr):
        @pl.when(pl.program_id(0) == 0)
        def _():
            state[...] = jnp.zeros_like(state)

        lb = _sigmoid(lbl_ref[0:1, :] - lbl_ref[1:2, :])
        row, col = _chunk_masks()
        causal = col <= row
        tri = _ones_where(causal)
        for c in range(CHUNKS_PER_STEP):
            r0 = CHUNK * c
            rs = slice(r0, r0 + CHUNK)
            st_ref[c] = state[...]
            q, _, _, f = _hgrn_gates(zq_ref[rs, :], zf_ref[rs, :], lb)
            kk = 1.0 - f
            b_scr[rs, :] = _tri_dot(tri, jnp.log(f), 3)
            for h in range(HEADS):
                sl = slice(HD * h, HD * (h + 1))
                bh = b_scr[rs, sl]
                qh, kh, vh = q[:, sl], kk[:, sl], zi_ref[rs, sl]
                st = state[h]
                b_last = b_scr[r0 + CHUNK - 1:r0 + CHUNK, sl]
                qcat, kecat, _, _ = _hgrn_intra_factors(b_scr, r0, bh, qh, kh, sl)
                a = jnp.where(causal, _mm_nt(qcat, kecat), 0.0)
                o = _mm(a, vh) + _mm_nt(qh * jnp.exp(bh), st)
                state[h] = st * jnp.exp(b_last) + _mm_tn(vh, kh * jnp.exp(b_last - bh))
                o_ref[rs, sl] = o
                zg = zg_ref[rs, sl]
                oa_ref[rs, sl] = (o * _rms(o) * gn_ref[h:h + 1, :] * zg * _sigmoid(zg)).astype(BF16)

    rows = CHUNK * CHUNKS_PER_STEP
    zspec = lambda cb: pl.BlockSpec((rows, HW), lambda i, cb=cb: (i, cb))
    return pl.pallas_call(
        body, name="hgrn_fwd", grid=(s // rows,),
        out_shape=[jax.ShapeDtypeStruct((s, 2 * HW), BF16), jax.ShapeDtypeStruct((s, HW), F32),
                   jax.ShapeDtypeStruct((n_chunks, HEADS, HD, HD), F32)],
        in_specs=[zspec(0), zspec(1), zspec(2), zspec(3), _full((2, HW)), _full((HEADS, HD))],
        out_specs=[pl.BlockSpec((rows, HW), lambda i: (i, 0)), pl.BlockSpec((rows, HW), lambda i: (i, 0)),
                   pl.BlockSpec((CHUNKS_PER_STEP, HEADS, HD, HD), lambda i: (i, 0, 0, 0))],
        scratch_shapes=[pltpu.VMEM((HEADS, HD, HD), F32), pltpu.VMEM((rows, HW), F32)],
        compiler_params=_params(("arbitrary",)),
    )(z, z, z, z, lb_logits, gn)


def _pool_counts(tile_idx, tm):
    t = tile_idx * tm + lax.broadcasted_iota(jnp.int32, (tm, 1), 0)
    return [1.0 / jnp.minimum(t + 1, w).astype(F32) for w in POOL_WINDOWS]


def _pool_fwd(z, w_pool, scale, mixed_in):
    s = z.shape[0]
    tm = min(ROW_TILE, s)

    def body(p_ref, w_ref, sc_ref, mixin_ref, ob_ref, pooled_ref, ext):
        i = pl.program_id(0)

        @pl.when(i == 0)
        def _():
            ext[0:POOL_HALO, :] = jnp.zeros((POOL_HALO, HW), F32)

        @pl.when(i > 0)
        def _():
            ext[0:POOL_HALO, :] = ext[tm:tm + POOL_HALO, :]

        ext[POOL_HALO:POOL_HALO + tm, :] = p_ref[...]
        inv = _pool_counts(i, tm)
        for g, w in enumerate(POOL_WINDOWS):
            sl = slice(HD * g, HD * (g + 1))
            p = ext[POOL_HALO:POOL_HALO + tm, sl]
            win = p
            for d in range(1, w):
                win = win + ext[POOL_HALO - d:POOL_HALO - d + tm, sl]
            pooled = (win * inv[g] - p).astype(BF16)
            pooled_ref[:, sl] = pooled
            ob_ref[:, sl] = (_mm(pooled, w_ref[g]) * sc_ref[:, sl]).astype(BF16)

    return pl.pallas_call(
        body, name="pool_fwd", grid=(s // tm,),
        out_shape=[jax.ShapeDtypeStruct((s, 2 * HW), BF16), jax.ShapeDtypeStruct((s, HW), BF16)],
        in_specs=[pl.BlockSpec((tm, HW), lambda i: (i, 4)), _full((HEADS, HD, HD)), _full((1, HW)), ANY_SPACE],
        out_specs=[pl.BlockSpec((tm, HW), lambda i: (i, 1)), pl.BlockSpec((tm, HW), lambda i: (i, 0))],
        scratch_shapes=[pltpu.VMEM((tm + POOL_HALO, HW), F32)],
        input_output_aliases={3: 0},
        compiler_params=_params(("arbitrary",)),
    )(z, w_pool, scale, mixed_in)


def _out_proj(x, mixed, w_out):
    s = x.shape[0]
    tm = min(ROW_TILE, s)

    def body(x_ref, a_ref, w_ref, o_ref):
        o_ref[...] = x_ref[...] + _mm(a_ref[...], w_ref[...])

    return pl.pallas_call(
        body, name="out_proj", grid=(s // tm,),
        out_shape=jax.ShapeDtypeStruct((s, D_MODEL), F32),
        in_specs=[pl.BlockSpec((tm, D_MODEL), lambda i: (i, 0)), pl.BlockSpec((tm, D_MODEL), lambda i: (i, 0)), VMEM_WHOLE],
        out_specs=pl.BlockSpec((tm, D_MODEL), lambda i: (i, 0)),
        compiler_params=_params(("parallel",)),
    )(x, mixed, w_out)


def _mem_kv(mem, g, wk, wv):
    def body(m_ref, g_ref, wk_ref, wv_ref, hm_ref, k_ref, v_ref):
        m = m_ref[...]
        hm = (m * _rms(m) * g_ref[...]).astype(BF16)
        hm_ref[...] = hm
        k_ref[...] = _mm(hm, wk_ref[...]).astype(BF16)
        v_ref[...] = _mm(hm, wv_ref[...]).astype(BF16)

    shp = jax.ShapeDtypeStruct((MEM_LEN, D_MODEL), BF16)
    return pl.pallas_call(
        body, name="mem_kv", out_shape=[shp, shp, shp],
        in_specs=[VMEM_WHOLE] * 4, out_specs=[VMEM_WHOLE] * 3,
        compiler_params=_params(),
    )(mem, g, wk, wv)


def _softmax_rows(sc):
    e = jnp.exp(sc - jnp.max(sc, axis=-1, keepdims=True))
    return e / jnp.sum(e, axis=-1, keepdims=True)


def _xattn_fwd(x, g, wq, xk, xv, wo_t):
    s = x.shape[0]
    tm = min(ROW_TILE, s)
    scale = XHD ** -0.5

    def body(x_ref, g_ref, wq_ref, k_ref, v_ref, wo_ref, o_ref, hq_ref, q_ref, att_ref):
        xv_ = x_ref[...]
        hq = (xv_ * _rms(xv_) * g_ref[...]).astype(BF16)
        hq_ref[...] = hq
        q_ref[...] = (_mm(hq, wq_ref[...]) * scale).astype(BF16)
        for h in range(HEADS):
            sl = slice(XHD * h, XHD * (h + 1))
            p = _softmax_rows(_mm_nt(q_ref[:, sl], k_ref[:, sl]))
            att_ref[:, sl] = _mm(p, v_ref[:, sl]).astype(BF16)
        o_ref[...] = xv_ + _mm_nt(att_ref[...], wo_ref[...])

    row_f32 = pl.BlockSpec((tm, D_MODEL), lambda i: (i, 0))
    bshape = jax.ShapeDtypeStruct((s, D_MODEL), BF16)
    return pl.pallas_call(
        body, name="xattn_fwd", grid=(s // tm,),
        out_shape=[jax.ShapeDtypeStruct((s, D_MODEL), F32), bshape, bshape, bshape],
        in_specs=[row_f32, _full((1, D_MODEL)), VMEM_WHOLE, VMEM_WHOLE, VMEM_WHOLE, VMEM_WHOLE],
        out_specs=[row_f32] * 4,
        compiler_params=_params(("parallel",)),
    )(x, g, wq, xk, xv, wo_t)


def _mlp_fwd_loss(x, g, w1, w2, gf, target):
    s = x.shape[0]
    tm = min(ROW_TILE, s)

    def body(x_ref, g_ref, w1_ref, w2_ref, gf_ref, t_ref, dx_ref, u_ref, hf_ref, slot_ref):
        @pl.when(pl.program_id(0) == 0)
        def _():
            slot_ref[...] = jnp.zeros_like(slot_ref)

        xv = x_ref[...]
        hf = (xv * _rms(xv) * g_ref[...]).astype(BF16)
        hf_ref[...] = hf
        acc = xv
        for j in range(N_DEV):
            a = jnp.maximum(_mm(hf, w1_ref[j]), 0.0)
            u = (a * a).astype(BF16)
            u_ref[:, FF_BLK * j:FF_BLK * (j + 1)] = u
            acc = acc + _mm(u, w2_ref[j])
        gfv = gf_ref[...]
        r = _rms(acc)
        n = acc * r
        err = n * gfv - t_ref[...]
        slot_ref[1:2, :] += jnp.sum(jnp.mean(err * err, axis=-1, keepdims=True), axis=0, keepdims=True) * 0.5
        dy = err * (1.0 / D_MODEL)
        slot_ref[0:1, :] += jnp.sum(dy * n, axis=0, keepdims=True)
        dn = dy * gfv
        dx_ref[...] = r * (dn - n * jnp.mean(dn * n, axis=-1, keepdims=True))

    row_f32 = pl.BlockSpec((tm, D_MODEL), lambda i: (i, 0))
    return pl.pallas_call(
        body, name="mlp_fwd_loss", grid=(s // tm,),
        out_shape=[jax.ShapeDtypeStruct((s, D_MODEL), F32), jax.ShapeDtypeStruct((s, D_FF), BF16),
                   jax.ShapeDtypeStruct((s, D_MODEL), BF16), jax.ShapeDtypeStruct((SLOT, D_MODEL), F32)],
        in_specs=[row_f32, _full((1, D_MODEL)), VMEM_WHOLE, VMEM_WHOLE, _full((1, D_MODEL)), row_f32],
        out_specs=[row_f32, pl.BlockSpec((tm, D_FF), lambda i: (i, 0)), row_f32, _full((SLOT, D_MODEL))],
        compiler_params=_params(("arbitrary",)),
    )(x, g, w1, w2, gf, target)


def _zero_slot(slot_ref):
    @pl.when(pl.program_id(0) == 0)
    def _():
        slot_ref[...] = jnp.zeros_like(slot_ref)


def _mlp_bwd(dx3, u, x2, g, w1, w2):
    s = x2.shape[0]
    tm = min(ROW_TILE, s)

    def body(d_ref, u_ref, x_ref, g_ref, w1_ref, w2_ref, da_ref, dx_ref, slot_ref):
        _zero_slot(slot_ref)
        d = d_ref[...]
        d16 = d.astype(BF16)
        dhf = jnp.zeros((tm, D_MODEL), F32)
        for j in range(N_DEV):
            sl = slice(FF_BLK * j, FF_BLK * (j + 1))
            da = (_mm_nt(d16, w2_ref[j]) * (2.0 * jnp.sqrt(u_ref[:, sl].astype(F32)))).astype(BF16)
            da_ref[:, sl] = da
            dhf = dhf + _mm_nt(da, w1_ref[j])
        dx, dg = _rms_bwd(x_ref[...], g_ref[...], dhf)
        dx_ref[...] = d + dx
        slot_ref[0:1, :] += dg

    row_f32 = pl.BlockSpec((tm, D_MODEL), lambda i: (i, 0))
    return pl.pallas_call(
        body, name="mlp_bwd", grid=(s // tm,),
        out_shape=[jax.ShapeDtypeStruct((s, D_FF), BF16), jax.ShapeDtypeStruct((s, D_MODEL), F32),
                   jax.ShapeDtypeStruct((SLOT, D_MODEL), F32)],
        in_specs=[row_f32, pl.BlockSpec((tm, D_FF), lambda i: (i, 0)), row_f32, _full((1, D_MODEL)),
                  VMEM_WHOLE, VMEM_WHOLE],
        out_specs=[pl.BlockSpec((tm, D_FF), lambda i: (i, 0)), row_f32, _full((SLOT, D_MODEL))],
        compiler_params=_params(("arbitrary",)),
    )(dx3, u, x2, g, w1, w2)


def _wgrad(a, b, name, col_blocks=False):
    s, m = a.shape
    n = b.shape[1]
    tm = 1280 if m % 1280 == 0 else min(1024, m)
    tn = min(1024, n)
    blk = n // N_DEV
    per_step = tn // blk if col_blocks else 1
    ts = min(2 * ROW_TILE, s)
    n_s = s // ts

    def body(a_ref, b_ref, o_ref, acc):
        k = pl.program_id(2)

        @pl.when(k == 0)
        def _():
            acc[...] = jnp.zeros_like(acc)

        acc[...] += _mm_tn(a_ref[...], b_ref[...])

        @pl.when(k == n_s - 1)
        def _():
            if col_blocks:
                for p in range(per_step):
                    o_ref[p] = acc[:, blk * p:blk * (p + 1)].astype(BF16)
            else:
                o_ref[...] = acc[...].astype(BF16)

    if col_blocks:
        out_shape = jax.ShapeDtypeStruct((N_DEV, m, blk), BF16)
        out_spec = pl.BlockSpec((per_step, tm, blk), lambda i, j, k: (j, i, 0))
    else:
        out_shape = jax.ShapeDtypeStruct((m, n), BF16)
        out_spec = pl.BlockSpec((tm, tn), lambda i, j, k: (i, j))
    return pl.pallas_call(
        body, name=name, grid=(m // tm, n // tn, n_s), out_shape=out_shape,
        in_specs=[pl.BlockSpec((ts, tm), lambda i, j, k: (k, i)), pl.BlockSpec((ts, tn), lambda i, j, k: (k, j))],
        out_specs=out_spec,
        scratch_shapes=[pltpu.VMEM((tm, tn), F32)],
        compiler_params=_params(("parallel", "parallel", "arbitrary")),
    )(a, b)


def _xattn_bwd(dx2, x1, g, q, xk, xv, wq, wo_t):
    s = x1.shape[0]
    tm = min(ROW_TILE, s)
    scale = XHD ** -0.5

    def body(d_ref, x_ref, g_ref, q_ref, k_ref, v_ref, wq_ref, wo_ref, dx_ref, dq_ref, dk_ref, dv_ref, slot_ref, datt):
        _zero_slot(slot_ref)

        @pl.when(pl.program_id(0) == 0)
        def _():
            dk_ref[...] = jnp.zeros_like(dk_ref)
            dv_ref[...] = jnp.zeros_like(dv_ref)

        d = d_ref[...]
        datt[...] = _mm(d, wo_ref[...]).astype(BF16)
        for h in range(HEADS):
            sl = slice(XHD * h, XHD * (h + 1))
            qh, kh, vh, dah = q_ref[:, sl], k_ref[:, sl], v_ref[:, sl], datt[:, sl]
            p = _softmax_rows(_mm_nt(qh, kh))
            dp = _mm_nt(dah, vh)
            ds = (p * (dp - jnp.sum(dp * p, axis=-1, keepdims=True))).astype(BF16)
            dq_ref[:, sl] = (_mm(ds, kh) * scale).astype(BF16)
            dk_ref[:, sl] += _mm_tn(ds, qh)
            dv_ref[:, sl] += _mm_tn(p, dah)
        dx, dg = _rms_bwd(x_ref[...], g_ref[...], _mm_nt(dq_ref[...], wq_ref[...]))
        dx_ref[...] = d + dx
        slot_ref[0:1, :] += dg

    row_f32 = pl.BlockSpec((tm, D_MODEL), lambda i: (i, 0))
    kv = jax.ShapeDtypeStruct((MEM_LEN, D_MODEL), F32)
    return pl.pallas_call(
        body, name="xattn_bwd", grid=(s // tm,),
        out_shape=[jax.ShapeDtypeStruct((s, D_MODEL), F32), jax.ShapeDtypeStruct((s, D_MODEL), BF16), kv, kv,
                   jax.ShapeDtypeStruct((SLOT, D_MODEL), F32)],
        in_specs=[row_f32, row_f32, _full((1, D_MODEL)), row_f32, VMEM_WHOLE, VMEM_WHOLE, VMEM_WHOLE, VMEM_WHOLE],
        out_specs=[row_f32, row_f32, _full((MEM_LEN, D_MODEL)), _full((MEM_LEN, D_MODEL)), _full((SLOT, D_MODEL))],
        scratch_shapes=[pltpu.VMEM((tm, D_MODEL), BF16)],
        compiler_params=_params(("arbitrary",)),
    )(dx2, x1, g, q, xk, xv, wq, wo_t)


def _mem_bwd(mem, g, hm, dxk, dxv, wk, wv):
    def body(m_ref, g_ref, hm_ref, dk_ref, dv_ref, wk_ref, wv_ref, dwk_ref, dwv_ref, slot_ref):
        dk, dv = dk_ref[...], dv_ref[...]
        hm_ = hm_ref[...]
        dwk_ref[...] = _mm_tn(hm_, dk).astype(BF16)
        dwv_ref[...] = _mm_tn(hm_, dv).astype(BF16)
        _, dg = _rms_bwd(m_ref[...], g_ref[...], _mm_nt(dk, wk_ref[...]) + _mm_nt(dv, wv_ref[...]))
        slot_ref[...] = jnp.zeros_like(slot_ref)
        slot_ref[0:1, :] = dg

    wshape = jax.ShapeDtypeStruct((D_MODEL, D_MODEL), BF16)
    return pl.pallas_call(
        body, name="mem_bwd", out_shape=[wshape, wshape, jax.ShapeDtypeStruct((SLOT, D_MODEL), F32)],
        in_specs=[VMEM_WHOLE] * 7, out_specs=[VMEM_WHOLE] * 3,
        compiler_params=_params(),
    )(mem, g, hm, dxk, dxv, wk, wv)


def _matmul_nt(a, w, name, dep):
    s, k = a.shape
    n = w.shape[0]
    tm = min(ROW_TILE, s)

    def body(a_ref, w_ref, dep_ref, o_ref):
        o_ref[...] = _mm_nt(a_ref[...], w_ref[...])

    return pl.pallas_call(
        body, name=name, grid=(s // tm,),
        out_shape=jax.ShapeDtypeStruct((s, n), F32),
        in_specs=[pl.BlockSpec((tm, k), lambda i: (i, 0)), VMEM_WHOLE, ANY_SPACE],
        out_specs=pl.BlockSpec((tm, n), lambda i: (i, 0)),
        compiler_params=_params(("parallel",)),
    )(a, w, dep)


def _pool_bwd(dmix, pooled, w_pool, scale):
    s = dmix.shape[0]
    tm = min(ROW_TILE, s)
    n_t = s // tm

    def body(do_ref, pl_ref, w_ref, sc_ref, dz_ref, dw_ref, slot_ref, ext):
        i = pl.program_id(0)
        tile = n_t - 1 - i
        _zero_slot(slot_ref)

        @pl.when(i == 0)
        def _():
            dw_ref[...] = jnp.zeros_like(dw_ref)
            ext[tm:tm + POOL_HALO, :] = jnp.zeros((POOL_HALO, HW), F32)

        @pl.when(i > 0)
        def _():
            ext[tm:tm + POOL_HALO, :] = ext[0:POOL_HALO, :]

        inv = _pool_counts(tile, tm)
        dpooled = []
        for g in range(HEADS):
            sl = slice(HD * g, HD * (g + 1))
            pooled_g = pl_ref[:, sl]
            do = do_ref[:, sl]
            slot_ref[0:1, sl] += jnp.sum(_mm(pooled_g, w_ref[g]) * do, axis=0, keepdims=True)
            dy = (do * sc_ref[:, sl]).astype(BF16)
            dw_ref[g] += _mm_tn(pooled_g, dy)
            dpo = _mm_nt(dy, w_ref[g])
            dpooled.append(dpo)
            ext[0:tm, sl] = dpo * inv[g]
        for g, w in enumerate(POOL_WINDOWS):
            sl = slice(HD * g, HD * (g + 1))
            win = ext[0:tm, sl]
            for d in range(1, w):
                win = win + ext[d:d + tm, sl]
            dz_ref[:, sl] = win - dpooled[g]

    return pl.pallas_call(
        body, name="pool_bwd", grid=(n_t,),
        out_shape=[jax.ShapeDtypeStruct((s, IN_WIDTH), F32), jax.ShapeDtypeStruct((HEADS, HD, HD), F32),
                   jax.ShapeDtypeStruct((SLOT, D_MODEL), F32)],
        in_specs=[pl.BlockSpec((tm, HW), lambda i: (n_t - 1 - i, 1)), pl.BlockSpec((tm, HW), lambda i: (n_t - 1 - i, 0)),
                  _full((HEADS, HD, HD)), _full((1, HW))],
        out_specs=[pl.BlockSpec((tm, HW), lambda i: (n_t - 1 - i, 4)), _full((HEADS, HD, HD)), _full((SLOT, D_MODEL))],
        scratch_shapes=[pltpu.VMEM((tm + POOL_HALO, HW), F32)],
        compiler_params=_params(("arbitrary",)),
    )(dmix, pooled, w_pool, scale)


def _hgrn_bwd(z, o, dmix, states, lb_logits, gn, dz_in):
    s = z.shape[0]
    n_chunks = s // CHUNK

    def body(zq_ref, zf_ref, zi_ref, zg_ref, o_ref, do_ref, st_ref, lbl_ref, gn_ref, dzin_ref,
             dz_ref, dlb_ref, dgn_ref, dstate, b_scr, dlb_acc):
        i = pl.program_id(0)

        @pl.when(i == 0)
        def _():
            dstate[...] = jnp.zeros_like(dstate)
            dlb_acc[...] = jnp.zeros_like(dlb_acc)
            dgn_ref[...] = jnp.zeros_like(dgn_ref)
            dlb_ref[...] = jnp.zeros_like(dlb_ref)

        lb = _sigmoid(lbl_ref[0:1, :] - lbl_ref[1:2, :])
        row, col = _chunk_masks()
        causal = col <= row
        tri = _ones_where(causal)
        upper = _ones_where(col >= row)
        strict_lower = _ones_where(col < row)
        for c in reversed(range(CHUNKS_PER_STEP)):
            r0 = CHUNK * c
            rs = slice(r0, r0 + CHUNK)
            zq = zq_ref[rs, :]
            q, sq, sig, f = _hgrn_gates(zq, zf_ref[rs, :], lb)
            kk = 1.0 - f
            b_scr[rs, :] = _tri_dot(tri, jnp.log(f), 3)
            for h in range(HEADS):
                sl = slice(HD * h, HD * (h + 1))
                oh = o_ref[rs, sl]
                gnh = gn_ref[h:h + 1, :]
                zg = zg_ref[rs, sl]
                sg = _sigmoid(zg)
                doa = do_ref[rs, sl]
                don = doa * (zg * sg)
                d_o, dgn = _rms_bwd(oh, gnh, don)
                dgn_ref[h:h + 1, 0:HD] += dgn
                dz_ref[rs, 3 * HW + HD * h:3 * HW + HD * (h + 1)] = (
                    doa * (oh * _rms(oh) * gnh) * (sg * (1.0 + zg * (1.0 - sg))))
                bh = b_scr[rs, sl]
                qh, kh, vh = q[:, sl], kk[:, sl], zi_ref[rs, sl]
                st0 = st_ref[c, h]
                ds1 = dstate[h]
                b_last = b_scr[r0 + CHUNK - 1:r0 + CHUNK, sl]
                lam = jnp.exp(bh)
                e_last = jnp.exp(b_last - bh)
                lam_last = jnp.exp(b_last)
                qcat, kecat, eqcat, ekcat = _hgrn_intra_factors(b_scr, r0, bh, qh, kh, sl)
                a = jnp.where(causal, _mm_nt(qcat, kecat), 0.0)
                da = jnp.where(causal, _mm_nt(d_o, vh), 0.0)
                dz_ref[rs, 2 * HW + HD * h:2 * HW + HD * (h + 1)] = _mm_tn(a, d_o) + _mm_nt(kh * e_last, ds1)
                q16, ke16 = qcat.astype(BF16), kecat.astype(BF16)
                gq = _mm(da, ke16)
                gk = _mm_tn(da, q16)
                dq_inter = lam * _mm(d_o, st0)
                dq = _sum_lane_blocks(eqcat * gq) + dq_inter
                dk_intra = _sum_lane_blocks(ekcat * gk)
                dk_state = _mm(vh, ds1) * e_last
                state_term = lam_last * jnp.sum(st0 * ds1, axis=0, keepdims=True)
                dstate[h] = ds1 * lam_last + _mm_tn(d_o, qh * lam)
                db_intra = _sum_lane_blocks(q16.astype(F32) * gq - ke16.astype(F32) * gk)
                dlf = (_tri_dot(upper, db_intra + qh * dq_inter, 2) + _tri_dot(strict_lower, kh * dk_state, 2)
                       + state_term)
                sigh = sig[:, sl]
                df = dlf / f[:, sl] - (dk_intra + dk_state)
                dlb_acc[:, sl] += jnp.sum(df * (1.0 - sigh), axis=0, keepdims=True)
                dz_ref[rs, HW + HD * h:HW + HD * (h + 1)] = df * (1.0 - lb[:, sl]) * sigh * (1.0 - sigh)
                sqh = sq[:, sl]
                dz_ref[rs, sl] = dq * (sqh * (1.0 + zq[:, sl] * (1.0 - sqh)))

        @pl.when(i == n_steps - 1)
        def _():
            dl0 = dlb_acc[...] * lb * (1.0 - lb)
            dlb_ref[0:1, 0:HW] = dl0
            dlb_ref[1:2, 0:HW] = -dl0

    rows = CHUNK * CHUNKS_PER_STEP
    n_steps = s // rows
    rev = lambda i: n_steps - 1 - i
    zspec = lambda cb: pl.BlockSpec((rows, HW), lambda i, cb=cb: (rev(i), cb))
    slot = jax.ShapeDtypeStruct((SLOT, D_MODEL), F32)
    return pl.pallas_call(
        body, name="hgrn_bwd", grid=(n_steps,),
        out_shape=[jax.ShapeDtypeStruct((s, IN_WIDTH), F32), slot, slot],
        in_specs=[zspec(0), zspec(1), zspec(2), zspec(3), pl.BlockSpec((rows, HW), lambda i: (rev(i), 0)),
                  pl.BlockSpec((rows, HW), lambda i: (rev(i), 0)),
                  pl.BlockSpec((CHUNKS_PER_STEP, HEADS, HD, HD), lambda i: (rev(i), 0, 0, 0)), _full((2, HW)),
                  _full((HEADS, HD)), ANY_SPACE],
        out_specs=[pl.BlockSpec((rows, 4 * HW), lambda i: (rev(i), 0)), _full((SLOT, D_MODEL)), _full((SLOT, D_MODEL))],
        scratch_shapes=[pltpu.VMEM((HEADS, HD, HD), F32), pltpu.VMEM((rows, HW), F32), pltpu.VMEM((1, HW), F32)],
        input_output_aliases={9: 0},
        compiler_params=_params(("arbitrary",)),
    )(z, z, z, z, o, dmix, states, lb_logits, gn, dz_in)


def _in_bwd(dz, w_t, x0, g, dx1):
    s = x0.shape[0]
    tm = min(ROW_TILE, s)

    def body(dz_ref, w_ref, x_ref, g_ref, d_ref, dx_ref, slot_ref):
        _zero_slot(slot_ref)
        dx, dg = _rms_bwd(x_ref[...], g_ref[...], _mm(dz_ref[...], w_ref[...]))
        dx_ref[...] = d_ref[...] + dx
        slot_ref[0:1, :] += dg

    row_f32 = pl.BlockSpec((tm, D_MODEL), lambda i: (i, 0))
    return pl.pallas_call(
        body, name="in_bwd", grid=(s // tm,),
        out_shape=[jax.ShapeDtypeStruct((s, D_MODEL), F32), jax.ShapeDtypeStruct((SLOT, D_MODEL), F32)],
        in_specs=[pl.BlockSpec((tm, IN_WIDTH), lambda i: (i, 0)), VMEM_WHOLE, row_f32, _full((1, D_MODEL)), row_f32],
        out_specs=[row_f32, _full((SLOT, D_MODEL))],
        compiler_params=_params(("arbitrary",)),
    )(dz, w_t, x0, g, dx1)


def kernel(x, mem, norm_mix_g, w_in, lb_logits, hgrn_norm_g, w_pool, pool_scale, w_out, norm_x_g, norm_mem_g, w_xq, w_xk, w_xv, w_xo, norm_ffn_g, w_ff1, w_ff2, final_norm_g, loss_target, m_norm_mix_g, m_w_in, m_lb_logits, m_hgrn_norm_g, m_w_pool, m_pool_scale, m_w_out, m_norm_x_g, m_norm_mem_g, m_w_xq, m_w_xk, m_w_xv, m_w_xo, m_norm_ffn_g, m_w_ff1, m_w_ff2, m_final_norm_g, v_norm_mix_g, v_w_in, v_lb_logits, v_hgrn_norm_g, v_w_pool, v_pool_scale, v_w_out, v_norm_x_g, v_norm_mem_g, v_w_xq, v_w_xk, v_w_xv, v_w_xo, v_norm_ffn_g, v_w_ff1, v_w_ff2, v_final_norm_g):
    x0 = x[0]
    mem0 = mem[0]
    tgt = loss_target[0]
    gn = hgrn_norm_g[0]
    gfin = final_norm_g.reshape(1, D_MODEL)
    wp = w_pool[0]
    heads_2d = lambda w: w.reshape(D_MODEL // N_DEV, D_MODEL)
    xo_2d = lambda w: w.reshape(D_MODEL, D_MODEL // N_DEV)

    first = _all_gather_weights([w_in[0].T], [w_out[0], heads_2d(w_xq), heads_2d(w_xk), heads_2d(w_xv), xo_2d(w_xo).T,
                                              w_ff1[0], w_ff2[0]])
    win_t = first[0].reshape(IN_WIDTH, D_MODEL)
    ga_attn, ga_mlp = _gather_first_start([first[1:6], first[6:8]], "gather_first_start")

    z, h = _in_proj(x0, _after(norm_mix_g, ga_attn[3]), win_t)
    mixed_a, o_pre, states = _hgrn_fwd(z, lb_logits, gn)
    lands = _split_wait(_gather_first_copies, ga_attn, o_pre, "gather_attn_first_wait")
    gb_attn = _gather_forward_start(lands, "gather_attn_forward_start")
    mixed, pooled = _pool_fwd(z, wp, _after(pool_scale, gb_attn[3]), mixed_a)
    lands = _split_wait(_gather_forward_copies, gb_attn, pooled, "gather_attn_forward_wait")
    wout_f, wq_f, wk_f, wv_f, wo_t = (t.reshape(D_MODEL, D_MODEL) for t in lands)
    x1 = _out_proj(x0, mixed, wout_f)
    lands = _split_wait(_gather_first_copies, ga_mlp, x1, "gather_mlp_first_wait")
    gb_mlp = _gather_forward_start(lands, "gather_mlp_forward_start")
    hm, xk, xv = _mem_kv(mem0, _after(norm_mem_g, gb_mlp[3]), wk_f, wv_f)
    x2, hq, xq, att = _xattn_fwd(x1, norm_x_g, wq_f, xk, xv, wo_t)
    w1_b, w2_b = _split_wait(_gather_forward_copies, gb_mlp, x2, "gather_mlp_forward_wait")
    dx3, u, hf, slot_fin = _mlp_fwd_loss(x2, norm_ffn_g, w1_b, w2_b, gfin, tgt)

    rows = lambda t, r: t.reshape(N_DEV, r, D_MODEL)
    da, dx2, slot_ffn = _mlp_bwd(dx3, u, x2, norm_ffn_g, w1_b, w2_b)
    dw2 = _wgrad(u, dx3, "wgrad_ff2")
    dw1 = _wgrad(hf, da, "wgrad_ff1", col_blocks=True)
    ex_ff = _all_to_all_start([dw1, rows(dw2, FF_BLK)], [], "exchange_ff_start")
    dx1, dxq, dxk, dxv, slot_x = _xattn_bwd(dx2, x1, _after(norm_x_g, ex_ff[3]), xq, xk, xv, wq_f, wo_t)
    dwo_t = _wgrad(dx2, att, "wgrad_xo")
    dwq = _wgrad(hq, dxq, "wgrad_xq")
    dwk, dwv, slot_mem = _mem_bwd(mem0, norm_mem_g, hm, dxk, dxv, wk_f, wv_f)
    ex_attn = _all_to_all_start([rows(dwq, 128), rows(dwk, 128), rows(dwv, 128), rows(dwo_t, 128)], [],
                                "exchange_attn_start")
    dmix = _matmul_nt(dx1, wout_f, "out_proj_bwd", dep=ex_attn[3])
    dwout = _wgrad(mixed, dx1, "wgrad_out")
    dz_pool, d_wpool, slot_ps = _pool_bwd(dmix, pooled, wp, pool_scale)
    small0 = jnp.concatenate([slot_x, slot_mem, slot_ffn, slot_fin, slot_ps], axis=0)
    ex_out = _all_to_all_start([rows(dwout, 128)], [small0, d_wpool], "exchange_out_start")
    dz, slot_lb, slot_gn = _hgrn_bwd(z, o_pre, dmix, states, _after(lb_logits, ex_out[3]), gn, dz_pool)
    dwin_t = _wgrad(dz, h, "wgrad_in")
    small1 = jnp.concatenate([slot_lb, slot_gn], axis=0)
    ex_in = _all_to_all_start([rows(dwin_t, 320)], [small1], "exchange_in_start")
    grad_x, slot_mix = _in_bwd(dz, win_t, x0, _after(norm_mix_g, ex_in[3]), dx1)
    ex_mix = _all_to_all_start([], [slot_mix], "exchange_mix_start")

    out = {}
    r_1, r_2 = _split_wait(_all_to_all_copies(2), ex_ff, ex_mix[3], "exchange_ff_wait")
    out["w_ff1"] = _sum_adamw(r_1, w_ff1[0], m_w_ff1[0], v_w_ff1[0], "adamw_ff1")
    out["w_ff2"] = _sum_adamw(r_2, w_ff2[0], m_w_ff2[0], v_w_ff2[0], "adamw_ff2")
    r_q, r_k, r_v, r_o = _split_wait(_all_to_all_copies(4), ex_attn, out["w_ff2"][1], "exchange_attn_wait")
    for n, r, (w, m, v) in (("w_xq", r_q, (w_xq, m_w_xq, v_w_xq)), ("w_xk", r_k, (w_xk, m_w_xk, v_w_xk)),
                            ("w_xv", r_v, (w_xv, m_w_xv, v_w_xv))):
        g = _sum_sources(r, "sum_grad_" + n).reshape(w.shape)
        out[n] = (g, *_adamw_whole(g, w, m, v, "adamw_" + n))
    g_xo = _sum_sources(r_o, "sum_grad_xo").T
    out["w_xo"] = (g_xo, *_adamw(g_xo, xo_2d(w_xo), xo_2d(m_w_xo), xo_2d(v_w_xo), "adamw_xo"))
    r_out, r_small0, r_wpool = _split_wait(_all_to_all_copies(1), ex_out, out["w_xo"][1], "exchange_out_wait")
    out["w_out"] = _sum_adamw(r_out, w_out[0], m_w_out[0], v_w_out[0], "adamw_out")
    r_in, r_small1 = _split_wait(_all_to_all_copies(1), ex_in, out["w_out"][1], "exchange_in_wait")
    g_in = _sum_sources(r_in, "sum_grad_in").T
    out["w_in"] = (g_in, *_adamw(g_in, w_in[0], m_w_in[0], v_w_in[0], "adamw_in"))
    (r_small2,) = _split_wait(_all_to_all_copies(0), ex_mix, out["w_in"][1], "exchange_mix_wait")
    row = lambda t: t.reshape(1, -1)
    small_params = {
        "norm_mix_g": (norm_mix_g, m_norm_mix_g, v_norm_mix_g),
        "lb_logits": (lb_logits, m_lb_logits, v_lb_logits),
        "hgrn_norm_g": (hgrn_norm_g[0], m_hgrn_norm_g[0], v_hgrn_norm_g[0]),
        "pool_scale": (pool_scale, m_pool_scale, v_pool_scale),
        "norm_x_g": (norm_x_g, m_norm_x_g, v_norm_x_g),
        "norm_mem_g": (norm_mem_g, m_norm_mem_g, v_norm_mem_g),
        "norm_ffn_g": (norm_ffn_g, m_norm_ffn_g, v_norm_ffn_g),
        "final_norm_g": (row(final_norm_g), row(m_final_norm_g), row(v_final_norm_g)),
        "w_pool": (wp, m_w_pool[0], v_w_pool[0]),
    }
    loss, small_out = _small_update([r_small0, r_small1, r_small2], r_wpool, small_params)
    out.update(small_out)

    shapes = dict(norm_mix_g=norm_mix_g, w_in=w_in, lb_logits=lb_logits, hgrn_norm_g=hgrn_norm_g, w_pool=w_pool,
                  pool_scale=pool_scale, w_out=w_out, norm_x_g=norm_x_g, norm_mem_g=norm_mem_g, w_xq=w_xq, w_xk=w_xk,
                  w_xv=w_xv, w_xo=w_xo, norm_ffn_g=norm_ffn_g, w_ff1=w_ff1, w_ff2=w_ff2, final_norm_g=final_norm_g)
    order = list(shapes)
    group = lambda k: [out[n][k].reshape(shapes[n].shape) for n in order]
    return (loss.reshape(()), grad_x.reshape(x.shape), *group(0), *group(1), *group(2), *group(3))
```

```python
import jax
import jax.numpy as jnp
from jax import lax
from jax.experimental import pallas as pl
from jax.experimental.pallas import tpu as pltpu

F32 = jnp.float32
BF16 = jnp.bfloat16

D_MODEL = 1024
N_DEV = 8
HEADS = 4
HD = 128
HW = HEADS * HD
IN_WIDTH = 5 * HW
XHD = 256
MEM_LEN = 256
D_FF = 4096
FF_BLK = D_FF // N_DEV
POOL_WINDOWS = (2, 4, 8, 16)
POOL_HALO = 16
CHUNK = 64
CHUNKS_PER_STEP = 4
SUB = 16
N_SUB = CHUNK // SUB
EXP_CAP = 80.0
EPS = 1e-6
ROW_TILE = 512
SLOT = 8
V7X_VMEM_LIMIT = 56 * 1024 * 1024

ADAM_LR = 0.001
ADAM_B1 = 0.9
ADAM_B2 = 0.999
ADAM_EPS = 1e-08
ADAM_WD = 0.01
ADAM_STEP = 10

MESH_ID = pl.DeviceIdType.MESH


def _params(sem=None, vmem=V7X_VMEM_LIMIT):
    return pltpu.CompilerParams(dimension_semantics=sem, vmem_limit_bytes=vmem)


def _mm(a, b):
    return lax.dot_general(a.astype(BF16), b.astype(BF16), (((1,), (0,)), ((), ())), preferred_element_type=F32)


def _mm_nt(a, b):
    return lax.dot_general(a.astype(BF16), b.astype(BF16), (((1,), (1,)), ((), ())), preferred_element_type=F32)


def _mm_tn(a, b):
    return lax.dot_general(a.astype(BF16), b.astype(BF16), (((0,), (0,)), ((), ())), preferred_element_type=F32)


def _sigmoid(x):
    return 1.0 / (1.0 + jnp.exp(-x))


def _rms(x):
    return lax.rsqrt(jnp.mean(x * x, axis=-1, keepdims=True) + EPS)


def _rms_bwd(x, g, dh):
    r = _rms(x)
    n = x * r
    dn = dh * g
    dx = r * (dn - n * jnp.mean(dn * n, axis=-1, keepdims=True))
    return dx, jnp.sum(dh * n, axis=0, keepdims=True)


def _tri_dot(tri, x, passes):
    acc = None
    rest = x
    for _ in range(passes):
        piece = rest.astype(BF16)
        part = lax.dot_general(tri, piece, (((1,), (0,)), ((), ())), preferred_element_type=F32)
        acc = part if acc is None else acc + part
        rest = rest - piece.astype(F32)
    return acc


def _adam_update(g, w, m, v):
    nm = ADAM_B1 * m + (1.0 - ADAM_B1) * g
    nv = ADAM_B2 * v + (1.0 - ADAM_B2) * (g * g)
    m_hat = nm / (1.0 - ADAM_B1 ** ADAM_STEP)
    v_hat = nv / (1.0 - ADAM_B2 ** ADAM_STEP)
    return -ADAM_LR * (m_hat / (jnp.sqrt(v_hat) + ADAM_EPS) + ADAM_WD * w), nm, nv


def _full(shape):
    return pl.BlockSpec(shape, lambda *_: (0,) * len(shape))


VMEM_WHOLE = pl.BlockSpec(memory_space=pltpu.VMEM)
ANY_SPACE = pl.BlockSpec(memory_space=pl.ANY)


def _mesh_pos():
    return lax.axis_index("x"), lax.axis_index("y"), lax.axis_index("c")


def _flat(px, py, pc):
    return 4 * px + 2 * py + pc


def _all_gather_weights(shards, cast_only):
    n, nc = len(shards), len(cast_only)
    step = 64

    def body(*refs):
        x_refs, c_refs = refs[:n], refs[n:n + nc]
        out_refs, cast_refs = refs[n + nc:2 * n + nc], refs[2 * n + nc:2 * n + 2 * nc]
        bufs = refs[2 * n + 2 * nc:3 * n + 2 * nc]
        send_sems, recv_sems, local_sems = refs[3 * n + 2 * nc:]
        x, y, c = _mesh_pos()
        me, sibling = (x, y, c), (x, y, 1 - c)
        chips = [(1 - x, y), (x, 1 - y), (1 - x, 1 - y)]

        def copy(a, k, blk, to, src=None):
            rows = out_refs[a].at[_flat(*blk)]
            return pltpu.make_async_remote_copy(
                src_ref=rows if src is None else src, dst_ref=rows,
                send_sem=send_sems.at[7 * a + k], recv_sem=recv_sems.at[7 * a + k], device_id=to, device_id_type=MESH_ID)

        def cast_rows(src, dst, rows):
            def cast(i, carry):
                r0 = pl.multiple_of(i * step, step)
                dst[pl.ds(r0, step), :] = src[pl.ds(r0, step), :].astype(BF16)
                return carry
            lax.fori_loop(0, rows // step, cast, 0)

        first, mine = [], []
        for a in range(n):
            cast_rows(x_refs[a], bufs[a], shards[a].shape[0])
            mine.append(pltpu.make_async_copy(bufs[a], out_refs[a].at[_flat(*me)], local_sems.at[a]))
            first.append(copy(a, 0, me, sibling, src=bufs[a]))
            first += [copy(a, 1 + j, me, (*chip, c), src=bufs[a]) for j, chip in enumerate(chips)]
            for cp in [mine[-1]] + first[-4:]:
                cp.start()
        for a in range(nc):
            cast_rows(c_refs[a], cast_refs[a], cast_only[a].shape[0])
        passed = []
        for j, chip in enumerate(chips):
            for a in range(n):
                copy(a, 1 + j, (*chip, c), me).wait_recv()
                passed.append(copy(a, 4 + j, (*chip, c), sibling))
                passed[-1].start()
        for a in range(n):
            copy(a, 0, sibling, me).wait_recv()
            for j, chip in enumerate(chips):
                copy(a, 4 + j, (*chip, 1 - c), me).wait_recv()
        for cp in first + passed:
            cp.wait_send()
        for cp in mine:
            cp.wait()

    return pl.pallas_call(
        body, name="all_gather_w_in",
        out_shape=[jax.ShapeDtypeStruct((N_DEV,) + s.shape, BF16) for s in shards]
        + [jax.ShapeDtypeStruct(s.shape, BF16) for s in cast_only],
        in_specs=[VMEM_WHOLE] * (n + nc), out_specs=[ANY_SPACE] * n + [VMEM_WHOLE] * nc,
        scratch_shapes=[pltpu.VMEM(s.shape, BF16) for s in shards]
        + [pltpu.SemaphoreType.DMA((7 * n,)), pltpu.SemaphoreType.DMA((7 * n,)), pltpu.SemaphoreType.DMA((n,))],
        compiler_params=_params(),
    )(*shards, *cast_only)


HBM_SPEC = pl.BlockSpec(memory_space=pltpu.HBM)
SEM_SPEC = pl.BlockSpec(memory_space=pltpu.SEMAPHORE)
EFFECT = pltpu.SideEffectType.DATAFLOW_SIDE_EFFECTING
TOKEN = jax.ShapeDtypeStruct((8, 128), F32)


def _in_hbm(a):
    return pltpu.with_memory_space_constraint(a, pltpu.HBM)


def _split_start(copies_of, srcs, lands, n_sems, name):
    ns, nl, k = len(srcs), len(lands), len(n_sems)

    def body(*refs):
        src_refs, land_refs = refs[:ns], refs[ns:ns + nl]
        sems = refs[ns + nl:ns + nl + k]
        token = refs[-1]
        for cp in copies_of(src_refs, land_refs, sems):
            cp.start()
        token[...] = jnp.zeros_like(token)

    outs = pl.pallas_call(
        body, name=name,
        out_shape=[pltpu.SemaphoreType.DMA((q,)) for q in n_sems]
        + [pltpu.HBM(a.shape, a.dtype) for a in list(srcs) + list(lands)] + [TOKEN],
        in_specs=[HBM_SPEC] * (ns + nl),
        out_specs=[SEM_SPEC] * k + [HBM_SPEC] * (ns + nl) + [VMEM_WHOLE],
        input_output_aliases={i: k + i for i in range(ns + nl)},
        compiler_params=pltpu.CompilerParams(has_side_effects=EFFECT),
    )(*[_in_hbm(a) for a in list(srcs) + list(lands)])
    return outs[:k], outs[k:k + ns], outs[k + ns:k + ns + nl], outs[-1]


def _split_wait(copies_of, handle, after, name):
    sems, srcs, lands, _ = handle
    ns, nl, k = len(srcs), len(lands), len(sems)

    def body(*refs):
        src_refs, land_refs = refs[:ns], refs[ns:ns + nl]
        sem_refs = refs[ns + nl:ns + nl + k]
        for cp in copies_of(src_refs, land_refs, sem_refs):
            cp.wait()

    outs = pl.pallas_call(
        body, name=name,
        out_shape=[pltpu.HBM(a.shape, a.dtype) for a in list(srcs) + list(lands)],
        in_specs=[HBM_SPEC] * (ns + nl) + [SEM_SPEC] * k + [ANY_SPACE],
        out_specs=[HBM_SPEC] * (ns + nl),
        input_output_aliases={i: i for i in range(ns + nl)},
        compiler_params=pltpu.CompilerParams(has_side_effects=EFFECT),
    )(*srcs, *lands, *sems, after)
    return outs[ns:]


def _gather_first_copies(shard_refs, land_refs, sems):
    send_sems, recv_sems, local_sems = sems
    x, y, c = _mesh_pos()
    me = _flat(x, y, c)
    peers = [(x, y, 1 - c), (1 - x, y, c), (x, 1 - y, c), (1 - x, 1 - y, c)]
    copies = []
    for a, (shard, land) in enumerate(zip(shard_refs, land_refs)):
        copies.append(pltpu.make_async_copy(shard, land.at[me], local_sems.at[a]))
        for k, peer in enumerate(peers):
            copies.append(pltpu.make_async_remote_copy(
                src_ref=shard, dst_ref=land.at[me], send_sem=send_sems.at[4 * a + k], recv_sem=recv_sems.at[4 * a + k],
                device_id=peer, device_id_type=MESH_ID))
    return copies


def _gather_forward_copies(src_refs, land_refs, sems):
    del src_refs
    send_sems, recv_sems = sems
    x, y, c = _mesh_pos()
    chips = [(1 - x, y), (x, 1 - y), (1 - x, 1 - y)]
    copies = []
    for a, land in enumerate(land_refs):
        for j, chip in enumerate(chips):
            rows = land.at[_flat(*chip, c)]
            copies.append(pltpu.make_async_remote_copy(
                src_ref=rows, dst_ref=rows, send_sem=send_sems.at[3 * a + j], recv_sem=recv_sems.at[3 * a + j],
                device_id=(x, y, 1 - c), device_id_type=MESH_ID))
    return copies


def _gather_first_start(groups, name):
    shards = [s for g in groups for s in g]
    lands = [lax.empty((N_DEV,) + s.shape, s.dtype) for s in shards]
    bounds = [sum(len(g) for g in groups[:i]) for i in range(len(groups) + 1)]

    def copies_of(src_refs, land_refs, sems):
        copies = []
        for i in range(len(groups)):
            lo, hi = bounds[i], bounds[i + 1]
            copies += _gather_first_copies(src_refs[lo:hi], land_refs[lo:hi], sems[3 * i:3 * i + 3])
        return copies

    n_sems = tuple(q for g in groups for q in (4 * len(g), 4 * len(g), len(g)))
    sems, srcs, lands, token = _split_start(copies_of, shards, lands, n_sems, name)
    return [(sems[3 * i:3 * i + 3], srcs[bounds[i]:bounds[i + 1]], lands[bounds[i]:bounds[i + 1]], token)
            for i in range(len(groups))]


def _gather_forward_start(lands, name):
    n = len(lands)
    return _split_start(_gather_forward_copies, [], lands, (3 * n, 3 * n), name)


def _all_to_all_copies(n_scattered):
    def copies_of(src_refs, land_refs, sems):
        send_sems, recv_sems, local_sems = sems
        x, y, c = _mesh_pos()
        me = _flat(x, y, c)
        copies = []
        for a, (src, land) in enumerate(zip(src_refs, land_refs)):
            scattered = a < n_scattered
            copies.append(pltpu.make_async_copy(src.at[me] if scattered else src, land.at[me], local_sems.at[a]))
            for k in range(1, N_DEV):
                peer = (1 - x if k & 4 else x, 1 - y if k & 2 else y, 1 - c if k & 1 else c)
                copies.append(pltpu.make_async_remote_copy(
                    src_ref=src.at[_flat(*peer)] if scattered else src, dst_ref=land.at[me],
                    send_sem=send_sems.at[7 * a + k - 1], recv_sem=recv_sems.at[7 * a + k - 1],
                    device_id=peer, device_id_type=MESH_ID))
        return copies
    return copies_of


def _all_to_all_start(scattered, broadcast, name):
    srcs = list(scattered) + list(broadcast)
    lands = [lax.empty(a.shape, a.dtype) for a in scattered] + [lax.empty((N_DEV,) + a.shape, a.dtype) for a in broadcast]
    n = len(srcs)
    return _split_start(_all_to_all_copies(len(scattered)), srcs, lands, (7 * n, 7 * n, n), name)


def _call_behind(deps, body, *, in_specs, **kwargs):
    n_in, n_dep = len(in_specs), len(deps)

    def body_without_deps(*refs):
        return body(*refs[:n_in], *refs[n_in + n_dep:])

    call = pl.pallas_call(body_without_deps, in_specs=list(in_specs) + [ANY_SPACE] * n_dep, **kwargs)
    return lambda *operands: call(*operands, *deps)


def _row_tile(rows):
    for cand in (256, 128, 64, 32, 16):
        if rows % cand == 0:
            return cand
    return rows


def _sum_sources(recv, name):
    _, rows, cols = recv.shape
    tile = _row_tile(rows)

    def body(r_ref, o_ref):
        acc = r_ref[0].astype(F32)
        for d in range(1, N_DEV):
            acc = acc + r_ref[d].astype(F32)
        o_ref[...] = acc

    return pl.pallas_call(
        body, name=name, grid=(rows // tile,),
        out_shape=jax.ShapeDtypeStruct((rows, cols), F32),
        in_specs=[pl.BlockSpec((N_DEV, tile, cols), lambda i: (0, i, 0))],
        out_specs=pl.BlockSpec((tile, cols), lambda i: (i, 0)),
        compiler_params=_params(("parallel",)),
    )(recv)


def _adamw(g, w, m, v, name):
    rows, cols = g.shape
    tile = _row_tile(rows)

    def body(g_ref, w_ref, m_ref, v_ref, d_ref, nm_ref, nv_ref):
        d_ref[...], nm_ref[...], nv_ref[...] = _adam_update(g_ref[...], w_ref[...], m_ref[...], v_ref[...])

    spec = pl.BlockSpec((tile, cols), lambda i: (i, 0))
    shp = jax.ShapeDtypeStruct((rows, cols), F32)
    return pl.pallas_call(
        body, name=name, grid=(rows // tile,), out_shape=[shp, shp, shp],
        in_specs=[spec] * 4, out_specs=[spec] * 3,
        compiler_params=_params(("parallel",)),
    )(g, w, m, v)


def _adamw_whole(g, w, m, v, name):
    def body(g_ref, w_ref, m_ref, v_ref, d_ref, nm_ref, nv_ref):
        d_ref[...], nm_ref[...], nv_ref[...] = _adam_update(g_ref[...], w_ref[...], m_ref[...], v_ref[...])

    shp = jax.ShapeDtypeStruct(g.shape, F32)
    return pl.pallas_call(
        body, name=name, out_shape=[shp, shp, shp], in_specs=[VMEM_WHOLE] * 4, out_specs=[VMEM_WHOLE] * 3,
        compiler_params=_params(),
    )(g, w, m, v)


def _sum_adamw(recv, w, m, v, name):
    _, rows, cols = recv.shape
    tile = _row_tile(rows)

    def body(r_ref, w_ref, m_ref, v_ref, g_ref, d_ref, nm_ref, nv_ref):
        acc = r_ref[0].astype(F32)
        for d in range(1, N_DEV):
            acc = acc + r_ref[d].astype(F32)
        g_ref[...] = acc
        d_ref[...], nm_ref[...], nv_ref[...] = _adam_update(acc, w_ref[...], m_ref[...], v_ref[...])

    spec = pl.BlockSpec((tile, cols), lambda i: (i, 0))
    shp = jax.ShapeDtypeStruct((rows, cols), F32)
    return pl.pallas_call(
        body, name=name, grid=(rows // tile,), out_shape=[shp] * 4,
        in_specs=[pl.BlockSpec((N_DEV, tile, cols), lambda i: (0, i, 0)), spec, spec, spec], out_specs=[spec] * 4,
        compiler_params=_params(("parallel",)),
    )(recv, w, m, v)


SMALL_SLOTS = {"norm_x_g": (0, 0, 1, D_MODEL), "norm_mem_g": (0, 8, 1, D_MODEL), "norm_ffn_g": (0, 16, 1, D_MODEL),
               "final_norm_g": (0, 24, 1, D_MODEL), "pool_scale": (0, 32, 1, HW),
               "lb_logits": (1, 0, 2, HW), "hgrn_norm_g": (1, 8, HEADS, HD), "norm_mix_g": (2, 0, 1, D_MODEL)}
LOSS_ROW = 25
SMALL_ORDER = ("norm_mix_g", "lb_logits", "hgrn_norm_g", "pool_scale", "norm_x_g", "norm_mem_g", "norm_ffn_g",
               "final_norm_g", "w_pool")


def _small_update(srecvs, wprecv, params):
    flat = [t for n in SMALL_ORDER for t in params[n]]
    nb = len(srecvs)
    n_in = nb + 1 + len(flat)

    def body(*refs):
        s_refs, wp_ref = refs[0:nb], refs[nb]
        in_refs = refs[nb + 1:n_in]
        loss_ref = refs[n_in]
        out_refs = refs[n_in + 1:-nb]
        accs = refs[-nb:]
        for s_ref, acc in zip(s_refs, accs):
            total = s_ref[0]
            for d in range(1, N_DEV):
                total = total + s_ref[d]
            acc[...] = total
        loss_ref[...] = accs[0][LOSS_ROW:LOSS_ROW + 1, 0:1]
        for i, name in enumerate(SMALL_ORDER):
            w_ref, m_ref, v_ref = in_refs[3 * i:3 * i + 3]
            g_ref, d_ref, nm_ref, nv_ref = out_refs[4 * i:4 * i + 4]
            if name == "w_pool":
                g = wp_ref[0]
                for d in range(1, N_DEV):
                    g = g + wp_ref[d]
            else:
                buf, r0, nr, nc = SMALL_SLOTS[name]
                g = accs[buf][r0:r0 + nr, 0:nc]
            g_ref[...] = g
            d_ref[...], nm_ref[...], nv_ref[...] = _adam_update(g, w_ref[...], m_ref[...], v_ref[...])

    out_shape = [jax.ShapeDtypeStruct((1, 1), F32)]
    for n in SMALL_ORDER:
        out_shape += [jax.ShapeDtypeStruct(params[n][0].shape, F32)] * 4
    outs = pl.pallas_call(
        body, name="small_update", out_shape=out_shape,
        in_specs=[VMEM_WHOLE] * n_in, out_specs=[VMEM_WHOLE] * len(out_shape),
        scratch_shapes=[pltpu.VMEM(r.shape[1:], F32) for r in srecvs],
        compiler_params=_params(),
    )(*srecvs, wprecv, *flat)
    return outs[0], {n: outs[1 + 4 * i:5 + 4 * i] for i, n in enumerate(SMALL_ORDER)}


def _in_proj(x, g, w_t, deps):
    s = x.shape[0]
    tm = min(ROW_TILE, s)

    def body(x_ref, g_ref, w_ref, z_ref, h_ref):
        xv = x_ref[...]
        h = (xv * _rms(xv) * g_ref[...]).astype(BF16)
        h_ref[...] = h
        z_ref[...] = _mm_nt(h, w_ref[...])

    return _call_behind(
        deps, body, name="in_proj", grid=(s // tm,),
        out_shape=[jax.ShapeDtypeStruct((s, IN_WIDTH), F32), jax.ShapeDtypeStruct((s, D_MODEL), BF16)],
        in_specs=[pl.BlockSpec((tm, D_MODEL), lambda i: (i, 0)), _full((1, D_MODEL)), VMEM_WHOLE],
        out_specs=[pl.BlockSpec((tm, IN_WIDTH), lambda i: (i, 0)), pl.BlockSpec((tm, D_MODEL), lambda i: (i, 0))],
        compiler_params=_params(("parallel",)),
    )(x, g, w_t)


def _chunk_masks():
    row = lax.broadcasted_iota(jnp.int32, (CHUNK, CHUNK), 0)
    col = lax.broadcasted_iota(jnp.int32, (CHUNK, CHUNK), 1)
    return row, col


def _ones_where(mask):
    return jnp.where(mask, 1.0, 0.0).astype(BF16)


def _hgrn_gates(zq, zf, lb):
    sq = _sigmoid(zq)
    sig = _sigmoid(zf)
    f = lb + (1.0 - lb) * sig
    return zq * sq, sq, sig, f


def _hgrn_intra_factors(b_scr, r0, bh, qh, kh, sl):
    trow = lax.broadcasted_iota(jnp.int32, (CHUNK, HD), 0)
    eq, ek = [], []
    for j in range(N_SUB):
        if j == 0:
            base = jnp.zeros((1, HD), F32)
        else:
            base = b_scr[r0 + SUB * j - 1:r0 + SUB * j, sl]
        in_j = (trow >= SUB * j) & (trow < SUB * (j + 1))
        eq.append(jnp.where(in_j, jnp.exp(bh - base), 0.0))
        ek.append(jnp.where(trow < SUB * (j + 1), jnp.exp(jnp.minimum(base - bh, EXP_CAP)), 0.0))
    eqcat = jnp.concatenate(eq, axis=1)
    ekcat = jnp.concatenate(ek, axis=1)
    qcat = jnp.concatenate([qh] * N_SUB, axis=1) * eqcat
    kecat = jnp.concatenate([kh] * N_SUB, axis=1) * ekcat
    return qcat, kecat, eqcat, ekcat


def _sum_lane_blocks(a):
    out = a[:, 0:HD]
    for j in range(1, N_SUB):
        out = out + a[:, HD * j:HD * (j + 1)]
    return out


def _hgrn_fwd(z, lb_logits, gn):
    s = z.shape[0]
    n_chunks = s // CHUNK

    def body(zq_ref, zf_ref, zi_ref, zg_ref, lbl_ref, gn_ref, oa_ref, o_ref, st_ref, state, b_scr):
        @pl.when(pl.program_id(0) == 0)
        def _():
            state[...] = jnp.zeros_like(state)

        lb = _sigmoid(lbl_ref[0:1, :] - lbl_ref[1:2, :])
        row, col = _chunk_masks()
        causal = col <= row
        tri = _ones_where(causal)
        for c in range(CHUNKS_PER_STEP):
            r0 = CHUNK * c
            rs = slice(r0, r0 + CHUNK)
            st_ref[c] = state[...]
            q, _, _, f = _hgrn_gates(zq_ref[rs, :], zf_ref[rs, :], lb)
            kk = 1.0 - f
            b_scr[rs, :] = _tri_dot(tri, jnp.log(f), 3)
            for h in range(HEADS):
                sl = slice(HD * h, HD * (h + 1))
                bh = b_scr[rs, sl]
                qh, kh, vh = q[:, sl], kk[:, sl], zi_ref[rs, sl]
                st = state[h]
                b_last = b_scr[r0 + CHUNK - 1:r0 + CHUNK, sl]
                qcat, kecat, _, _ = _hgrn_intra_factors(b_scr, r0, bh, qh, kh, sl)
                a = jnp.where(causal, _mm_nt(qcat, kecat), 0.0)
                o = _mm(a, vh) + _mm_nt(qh * jnp.exp(bh), st)
                state[h] = st * jnp.exp(b_last) + _mm_tn(vh, kh * jnp.exp(b_last - bh))
                o_ref[rs, sl] = o
                zg = zg_ref[rs, sl]
                oa_ref[rs, sl] = (o * _rms(o) * gn_ref[h:h + 1, :] * zg * _sigmoid(zg)).astype(BF16)

    rows = CHUNK * CHUNKS_PER_STEP
    zspec = lambda cb: pl.BlockSpec((rows, HW), lambda i, cb=cb: (i, cb))
    return pl.pallas_call(
        body, name="hgrn_fwd", grid=(s // rows,),
        out_shape=[jax.ShapeDtypeStruct((s, 2 * HW), BF16), jax.ShapeDtypeStruct((s, HW), F32),
                   jax.ShapeDtypeStruct((n_chunks, HEADS, HD, HD), F32)],
        in_specs=[zspec(0), zspec(1), zspec(2), zspec(3), _full((2, HW)), _full((HEADS, HD))],
        out_specs=[pl.BlockSpec((rows, HW), lambda i: (i, 0)), pl.BlockSpec((rows, HW), lambda i: (i, 0)),
                   pl.BlockSpec((CHUNKS_PER_STEP, HEADS, HD, HD), lambda i: (i, 0, 0, 0))],
        scratch_shapes=[pltpu.VMEM((HEADS, HD, HD), F32), pltpu.VMEM((rows, HW), F32)],
        compiler_params=_params(("arbitrary",)),
    )(z, z, z, z, lb_logits, gn)


def _pool_counts(tile_idx, tm):
    t = tile_idx * tm + lax.broadcasted_iota(jnp.int32, (tm, 1), 0)
    return [1.0 / jnp.minimum(t + 1, w).astype(F32) for w in POOL_WINDOWS]


def _pool_fwd(z, w_pool, scale, mixed_in, deps):
    s = z.shape[0]
    tm = min(ROW_TILE, s)

    def body(p_ref, w_ref, sc_ref, mixin_ref, ob_ref, pooled_ref, ext):
        i = pl.program_id(0)

        @pl.when(i == 0)
        def _():
            ext[0:POOL_HALO, :] = jnp.zeros((POOL_HALO, HW), F32)

        @pl.when(i > 0)
        def _():
            ext[0:POOL_HALO, :] = ext[tm:tm + POOL_HALO, :]

        ext[POOL_HALO:POOL_HALO + tm, :] = p_ref[...]
        inv = _pool_counts(i, tm)
        for g, w in enumerate(POOL_WINDOWS):
            sl = slice(HD * g, HD * (g + 1))
            p = ext[POOL_HALO:POOL_HALO + tm, sl]
            win = p
            for d in range(1, w):
                win = win + ext[POOL_HALO - d:POOL_HALO - d + tm, sl]
            pooled = (win * inv[g] - p).astype(BF16)
            pooled_ref[:, sl] = pooled
            ob_ref[:, sl] = (_mm(pooled, w_ref[g]) * sc_ref[:, sl]).astype(BF16)

    return _call_behind(
        deps, body, name="pool_fwd", grid=(s // tm,),
        out_shape=[jax.ShapeDtypeStruct((s, 2 * HW), BF16), jax.ShapeDtypeStruct((s, HW), BF16)],
        in_specs=[pl.BlockSpec((tm, HW), lambda i: (i, 4)), _full((HEADS, HD, HD)), _full((1, HW)), ANY_SPACE],
        out_specs=[pl.BlockSpec((tm, HW), lambda i: (i, 1)), pl.BlockSpec((tm, HW), lambda i: (i, 0))],
        scratch_shapes=[pltpu.VMEM((tm + POOL_HALO, HW), F32)],
        input_output_aliases={3: 0},
        compiler_params=_params(("arbitrary",)),
    )(z, w_pool, scale, mixed_in)


def _out_proj(x, mixed, w_out):
    s = x.shape[0]
    tm = min(ROW_TILE, s)

    def body(x_ref, a_ref, w_ref, o_ref):
        o_ref[...] = x_ref[...] + _mm(a_ref[...], w_ref[...])

    return pl.pallas_call(
        body, name="out_proj", grid=(s // tm,),
        out_shape=jax.ShapeDtypeStruct((s, D_MODEL), F32),
        in_specs=[pl.BlockSpec((tm, D_MODEL), lambda i: (i, 0)), pl.BlockSpec((tm, D_MODEL), lambda i: (i, 0)), VMEM_WHOLE],
        out_specs=pl.BlockSpec((tm, D_MODEL), lambda i: (i, 0)),
        compiler_params=_params(("parallel",)),
    )(x, mixed, w_out)


def _mem_kv(mem, g, wk, wv, deps):
    def body(m_ref, g_ref, wk_ref, wv_ref, hm_ref, k_ref, v_ref):
        m = m_ref[...]
        hm = (m * _rms(m) * g_ref[...]).astype(BF16)
        hm_ref[...] = hm
        k_ref[...] = _mm(hm, wk_ref[...]).astype(BF16)
        v_ref[...] = _mm(hm, wv_ref[...]).astype(BF16)

    shp = jax.ShapeDtypeStruct((MEM_LEN, D_MODEL), BF16)
    return _call_behind(
        deps, body, name="mem_kv", out_shape=[shp, shp, shp],
        in_specs=[VMEM_WHOLE] * 4, out_specs=[VMEM_WHOLE] * 3,
        compiler_params=_params(),
    )(mem, g, wk, wv)


def _softmax_rows(sc):
    e = jnp.exp(sc - jnp.max(sc, axis=-1, keepdims=True))
    return e / jnp.sum(e, axis=-1, keepdims=True)


def _xattn_fwd(x, g, wq, xk, xv, wo_t):
    s = x.shape[0]
    tm = min(ROW_TILE, s)
    scale = XHD ** -0.5

    def body(x_ref, g_ref, wq_ref, k_ref, v_ref, wo_ref, o_ref, hq_ref, q_ref, att_ref):
        xv_ = x_ref[...]
        hq = (xv_ * _rms(xv_) * g_ref[...]).astype(BF16)
        hq_ref[...] = hq
        q_ref[...] = (_mm(hq, wq_ref[...]) * scale).astype(BF16)
        for h in range(HEADS):
            sl = slice(XHD * h, XHD * (h + 1))
            p = _softmax_rows(_mm_nt(q_ref[:, sl], k_ref[:, sl]))
            att_ref[:, sl] = _mm(p, v_ref[:, sl]).astype(BF16)
        o_ref[...] = xv_ + _mm_nt(att_ref[...], wo_ref[...])

    row_f32 = pl.BlockSpec((tm, D_MODEL), lambda i: (i, 0))
    bshape = jax.ShapeDtypeStruct((s, D_MODEL), BF16)
    return pl.pallas_call(
        body, name="xattn_fwd", grid=(s // tm,),
        out_shape=[jax.ShapeDtypeStruct((s, D_MODEL), F32), bshape, bshape, bshape],
        in_specs=[row_f32, _full((1, D_MODEL)), VMEM_WHOLE, VMEM_WHOLE, VMEM_WHOLE, VMEM_WHOLE],
        out_specs=[row_f32] * 4,
        compiler_params=_params(("parallel",)),
    )(x, g, wq, xk, xv, wo_t)


def _mlp_fwd_loss(x, g, w1, w2, gf, target):
    s = x.shape[0]
    tm = min(ROW_TILE, s)

    def body(x_ref, g_ref, w1_ref, w2_ref, gf_ref, t_ref, dx_ref, u_ref, hf_ref, slot_ref):
        @pl.when(pl.program_id(0) == 0)
        def _():
            slot_ref[...] = jnp.zeros_like(slot_ref)

        xv = x_ref[...]
        hf = (xv * _rms(xv) * g_ref[...]).astype(BF16)
        hf_ref[...] = hf
        acc = xv
        for j in range(N_DEV):
            a = jnp.maximum(_mm(hf, w1_ref[j]), 0.0)
            u = (a * a).astype(BF16)
            u_ref[:, FF_BLK * j:FF_BLK * (j + 1)] = u
            acc = acc + _mm(u, w2_ref[j])
        gfv = gf_ref[...]
        r = _rms(acc)
        n = acc * r
        err = n * gfv - t_ref[...]
        slot_ref[1:2, :] += jnp.sum(jnp.mean(err * err, axis=-1, keepdims=True), axis=0, keepdims=True) * 0.5
        dy = err * (1.0 / D_MODEL)
        slot_ref[0:1, :] += jnp.sum(dy * n, axis=0, keepdims=True)
        dn = dy * gfv
        dx_ref[...] = r * (dn - n * jnp.mean(dn * n, axis=-1, keepdims=True))

    row_f32 = pl.BlockSpec((tm, D_MODEL), lambda i: (i, 0))
    return pl.pallas_call(
        body, name="mlp_fwd_loss", grid=(s // tm,),
        out_shape=[jax.ShapeDtypeStruct((s, D_MODEL), F32), jax.ShapeDtypeStruct((s, D_FF), BF16),
                   jax.ShapeDtypeStruct((s, D_MODEL), BF16), jax.ShapeDtypeStruct((SLOT, D_MODEL), F32)],
        in_specs=[row_f32, _full((1, D_MODEL)), VMEM_WHOLE, VMEM_WHOLE, _full((1, D_MODEL)), row_f32],
        out_specs=[row_f32, pl.BlockSpec((tm, D_FF), lambda i: (i, 0)), row_f32, _full((SLOT, D_MODEL))],
        compiler_params=_params(("arbitrary",)),
    )(x, g, w1, w2, gf, target)


def _zero_slot(slot_ref):
    @pl.when(pl.program_id(0) == 0)
    def _():
        slot_ref[...] = jnp.zeros_like(slot_ref)


def _mlp_bwd(dx3, u, x2, g, w1, w2):
    s = x2.shape[0]
    tm = min(ROW_TILE, s)

    def body(d_ref, u_ref, x_ref, g_ref, w1_ref, w2_ref, da_ref, dx_ref, slot_ref):
        _zero_slot(slot_ref)
        d = d_ref[...]
        d16 = d.astype(BF16)
        dhf = jnp.zeros((tm, D_MODEL), F32)
        for j in range(N_DEV):
            sl = slice(FF_BLK * j, FF_BLK * (j + 1))
            da = (_mm_nt(d16, w2_ref[j]) * (2.0 * jnp.sqrt(u_ref[:, sl].astype(F32)))).astype(BF16)
            da_ref[:, sl] = da
            dhf = dhf + _mm_nt(da, w1_ref[j])
        dx, dg = _rms_bwd(x_ref[...], g_ref[...], dhf)
        dx_ref[...] = d + dx
        slot_ref[0:1, :] += dg

    row_f32 = pl.BlockSpec((tm, D_MODEL), lambda i: (i, 0))
    return pl.pallas_call(
        body, name="mlp_bwd", grid=(s // tm,),
        out_shape=[jax.ShapeDtypeStruct((s, D_FF), BF16), jax.ShapeDtypeStruct((s, D_MODEL), F32),
                   jax.ShapeDtypeStruct((SLOT, D_MODEL), F32)],
        in_specs=[row_f32, pl.BlockSpec((tm, D_FF), lambda i: (i, 0)), row_f32, _full((1, D_MODEL)),
                  VMEM_WHOLE, VMEM_WHOLE],
        out_specs=[pl.BlockSpec((tm, D_FF), lambda i: (i, 0)), row_f32, _full((SLOT, D_MODEL))],
        compiler_params=_params(("arbitrary",)),
    )(dx3, u, x2, g, w1, w2)


def _wgrad(a, b, name, col_blocks=False):
    s, m = a.shape
    n = b.shape[1]
    tm = 1280 if m % 1280 == 0 else min(1024, m)
    tn = min(1024, n)
    blk = n // N_DEV
    per_step = tn // blk if col_blocks else 1
    ts = min(2 * ROW_TILE, s)
    n_s = s // ts

    def body(a_ref, b_ref, o_ref, acc):
        k = pl.program_id(2)

        @pl.when(k == 0)
        def _():
            acc[...] = jnp.zeros_like(acc)

        acc[...] += _mm_tn(a_ref[...], b_ref[...])

        @pl.when(k == n_s - 1)
        def _():
            if col_blocks:
                for p in range(per_step):
                    o_ref[p] = acc[:, blk * p:blk * (p + 1)].astype(BF16)
            else:
                o_ref[...] = acc[...].astype(BF16)

    if col_blocks:
        out_shape = jax.ShapeDtypeStruct((N_DEV, m, blk), BF16)
        out_spec = pl.BlockSpec((per_step, tm, blk), lambda i, j, k: (j, i, 0))
    else:
        out_shape = jax.ShapeDtypeStruct((m, n), BF16)
        out_spec = pl.BlockSpec((tm, tn), lambda i, j, k: (i, j))
    return pl.pallas_call(
        body, name=name, grid=(m // tm, n // tn, n_s), out_shape=out_shape,
        in_specs=[pl.BlockSpec((ts, tm), lambda i, j, k: (k, i)), pl.BlockSpec((ts, tn), lambda i, j, k: (k, j))],
        out_specs=out_spec,
        scratch_shapes=[pltpu.VMEM((tm, tn), F32)],
        compiler_params=_params(("parallel", "parallel", "arbitrary")),
    )(a, b)


def _xattn_bwd(dx2, x1, g, q, xk, xv, wq, wo_t, deps):
    s = x1.shape[0]
    tm = min(ROW_TILE, s)
    scale = XHD ** -0.5

    def body(d_ref, x_ref, g_ref, q_ref, k_ref, v_ref, wq_ref, wo_ref, dx_ref, dq_ref, dk_ref, dv_ref, slot_ref, datt):
        _zero_slot(slot_ref)

        @pl.when(pl.program_id(0) == 0)
        def _():
            dk_ref[...] = jnp.zeros_like(dk_ref)
            dv_ref[...] = jnp.zeros_like(dv_ref)

        d = d_ref[...]
        datt[...] = _mm(d, wo_ref[...]).astype(BF16)
        for h in range(HEADS):
            sl = slice(XHD * h, XHD * (h + 1))
            qh, kh, vh, dah = q_ref[:, sl], k_ref[:, sl], v_ref[:, sl], datt[:, sl]
            p = _softmax_rows(_mm_nt(qh, kh))
            dp = _mm_nt(dah, vh)
            ds = (p * (dp - jnp.sum(dp * p, axis=-1, keepdims=True))).astype(BF16)
            dq_ref[:, sl] = (_mm(ds, kh) * scale).astype(BF16)
            dk_ref[:, sl] += _mm_tn(ds, qh)
            dv_ref[:, sl] += _mm_tn(p, dah)
        dx, dg = _rms_bwd(x_ref[...], g_ref[...], _mm_nt(dq_ref[...], wq_ref[...]))
        dx_ref[...] = d + dx
        slot_ref[0:1, :] += dg

    row_f32 = pl.BlockSpec((tm, D_MODEL), lambda i: (i, 0))
    kv = jax.ShapeDtypeStruct((MEM_LEN, D_MODEL), F32)
    return _call_behind(
        deps, body, name="xattn_bwd", grid=(s // tm,),
        out_shape=[jax.ShapeDtypeStruct((s, D_MODEL), F32), jax.ShapeDtypeStruct((s, D_MODEL), BF16), kv, kv,
                   jax.ShapeDtypeStruct((SLOT, D_MODEL), F32)],
        in_specs=[row_f32, row_f32, _full((1, D_MODEL)), row_f32, VMEM_WHOLE, VMEM_WHOLE, VMEM_WHOLE, VMEM_WHOLE],
        out_specs=[row_f32, row_f32, _full((MEM_LEN, D_MODEL)), _full((MEM_LEN, D_MODEL)), _full((SLOT, D_MODEL))],
        scratch_shapes=[pltpu.VMEM((tm, D_MODEL), BF16)],
        compiler_params=_params(("arbitrary",)),
    )(dx2, x1, g, q, xk, xv, wq, wo_t)


def _mem_bwd(mem, g, hm, dxk, dxv, wk, wv):
    def body(m_ref, g_ref, hm_ref, dk_ref, dv_ref, wk_ref, wv_ref, dwk_ref, dwv_ref, slot_ref):
        dk, dv = dk_ref[...], dv_ref[...]
        hm_ = hm_ref[...]
        dwk_ref[...] = _mm_tn(hm_, dk).astype(BF16)
        dwv_ref[...] = _mm_tn(hm_, dv).astype(BF16)
        _, dg = _rms_bwd(m_ref[...], g_ref[...], _mm_nt(dk, wk_ref[...]) + _mm_nt(dv, wv_ref[...]))
        slot_ref[...] = jnp.zeros_like(slot_ref)
        slot_ref[0:1, :] = dg

    wshape = jax.ShapeDtypeStruct((D_MODEL, D_MODEL), BF16)
    return pl.pallas_call(
        body, name="mem_bwd", out_shape=[wshape, wshape, jax.ShapeDtypeStruct((SLOT, D_MODEL), F32)],
        in_specs=[VMEM_WHOLE] * 7, out_specs=[VMEM_WHOLE] * 3,
        compiler_params=_params(),
    )(mem, g, hm, dxk, dxv, wk, wv)


def _matmul_nt(a, w, name, deps):
    s, k = a.shape
    n = w.shape[0]
    tm = min(ROW_TILE, s)

    def body(a_ref, w_ref, o_ref):
        o_ref[...] = _mm_nt(a_ref[...], w_ref[...])

    return _call_behind(
        deps, body, name=name, grid=(s // tm,),
        out_shape=jax.ShapeDtypeStruct((s, n), F32),
        in_specs=[pl.BlockSpec((tm, k), lambda i: (i, 0)), VMEM_WHOLE],
        out_specs=pl.BlockSpec((tm, n), lambda i: (i, 0)),
        compiler_params=_params(("parallel",)),
    )(a, w)


def _pool_bwd(dmix, pooled, w_pool, scale):
    s = dmix.shape[0]
    tm = min(ROW_TILE, s)
    n_t = s // tm

    def body(do_ref, pl_ref, w_ref, sc_ref, dz_ref, dw_ref, slot_ref, ext):
        i = pl.program_id(0)
        tile = n_t - 1 - i
        _zero_slot(slot_ref)

        @pl.when(i == 0)
        def _():
            dw_ref[...] = jnp.zeros_like(dw_ref)
            ext[tm:tm + POOL_HALO, :] = jnp.zeros((POOL_HALO, HW), F32)

        @pl.when(i > 0)
        def _():
            ext[tm:tm + POOL_HALO, :] = ext[0:POOL_HALO, :]

        inv = _pool_counts(tile, tm)
        dpooled = []
        for g in range(HEADS):
            sl = slice(HD * g, HD * (g + 1))
            pooled_g = pl_ref[:, sl]
            do = do_ref[:, sl]
            slot_ref[0:1, sl] += jnp.sum(_mm(pooled_g, w_ref[g]) * do, axis=0, keepdims=True)
            dy = (do * sc_ref[:, sl]).astype(BF16)
            dw_ref[g] += _mm_tn(pooled_g, dy)
            dpo = _mm_nt(dy, w_ref[g])
            dpooled.append(dpo)
            ext[0:tm, sl] = dpo * inv[g]
        for g, w in enumerate(POOL_WINDOWS):
            sl = slice(HD * g, HD * (g + 1))
            win = ext[0:tm, sl]
            for d in range(1, w):
                win = win + ext[d:d + tm, sl]
            dz_ref[:, sl] = win - dpooled[g]

    return pl.pallas_call(
        body, name="pool_bwd", grid=(n_t,),
        out_shape=[jax.ShapeDtypeStruct((s, IN_WIDTH), F32), jax.ShapeDtypeStruct((HEADS, HD, HD), F32),
                   jax.ShapeDtypeStruct((SLOT, D_MODEL), F32)],
        in_specs=[pl.BlockSpec((tm, HW), lambda i: (n_t - 1 - i, 1)), pl.BlockSpec((tm, HW), lambda i: (n_t - 1 - i, 0)),
                  _full((HEADS, HD, HD)), _full((1, HW))],
        out_specs=[pl.BlockSpec((tm, HW), lambda i: (n_t - 1 - i, 4)), _full((HEADS, HD, HD)), _full((SLOT, D_MODEL))],
        scratch_shapes=[pltpu.VMEM((tm + POOL_HALO, HW), F32)],
        compiler_params=_params(("arbitrary",)),
    )(dmix, pooled, w_pool, scale)


def _hgrn_bwd(z, o, dmix, states, lb_logits, gn, dz_in, deps):
    s = z.shape[0]
    n_chunks = s // CHUNK

    def body(zq_ref, zf_ref, zi_ref, zg_ref, o_ref, do_ref, st_ref, lbl_ref, gn_ref, dzin_ref,
             dz_ref, dlb_ref, dgn_ref, dstate, b_scr, dlb_acc):
        i = pl.program_id(0)

        @pl.when(i == 0)
        def _():
            dstate[...] = jnp.zeros_like(dstate)
            dlb_acc[...] = jnp.zeros_like(dlb_acc)
            dgn_ref[...] = jnp.zeros_like(dgn_ref)
            dlb_ref[...] = jnp.zeros_like(dlb_ref)

        lb = _sigmoid(lbl_ref[0:1, :] - lbl_ref[1:2, :])
        row, col = _chunk_masks()
        causal = col <= row
        tri = _ones_where(causal)
        upper = _ones_where(col >= row)
        strict_lower = _ones_where(col < row)
        for c in reversed(range(CHUNKS_PER_STEP)):
            r0 = CHUNK * c
            rs = slice(r0, r0 + CHUNK)
            zq = zq_ref[rs, :]
            q, sq, sig, f = _hgrn_gates(zq, zf_ref[rs, :], lb)
            kk = 1.0 - f
            b_scr[rs, :] = _tri_dot(tri, jnp.log(f), 3)
            for h in range(HEADS):
                sl = slice(HD * h, HD * (h + 1))
                oh = o_ref[rs, sl]
                gnh = gn_ref[h:h + 1, :]
                zg = zg_ref[rs, sl]
                sg = _sigmoid(zg)
                doa = do_ref[rs, sl]
                don = doa * (zg * sg)
                d_o, dgn = _rms_bwd(oh, gnh, don)
                dgn_ref[h:h + 1, 0:HD] += dgn
                dz_ref[rs, 3 * HW + HD * h:3 * HW + HD * (h + 1)] = (
                    doa * (oh * _rms(oh) * gnh) * (sg * (1.0 + zg * (1.0 - sg))))
                bh = b_scr[rs, sl]
                qh, kh, vh = q[:, sl], kk[:, sl], zi_ref[rs, sl]
                st0 = st_ref[c, h]
                ds1 = dstate[h]
                b_last = b_scr[r0 + CHUNK - 1:r0 + CHUNK, sl]
                lam = jnp.exp(bh)
                e_last = jnp.exp(b_last - bh)
                lam_last = jnp.exp(b_last)
                qcat, kecat, eqcat, ekcat = _hgrn_intra_factors(b_scr, r0, bh, qh, kh, sl)
                a = jnp.where(causal, _mm_nt(qcat, kecat), 0.0)
                da = jnp.where(causal, _mm_nt(d_o, vh), 0.0)
                dz_ref[rs, 2 * HW + HD * h:2 * HW + HD * (h + 1)] = _mm_tn(a, d_o) + _mm_nt(kh * e_last, ds1)
                q16, ke16 = qcat.astype(BF16), kecat.astype(BF16)
                gq = _mm(da, ke16)
                gk = _mm_tn(da, q16)
                dq_inter = lam * _mm(d_o, st0)
                dq = _sum_lane_blocks(eqcat * gq) + dq_inter
                dk_intra = _sum_lane_blocks(ekcat * gk)
                dk_state = _mm(vh, ds1) * e_last
                state_term = lam_last * jnp.sum(st0 * ds1, axis=0, keepdims=True)
                dstate[h] = ds1 * lam_last + _mm_tn(d_o, qh * lam)
                db_intra = _sum_lane_blocks(q16.astype(F32) * gq - ke16.astype(F32) * gk)
                dlf = (_tri_dot(upper, db_intra + qh * dq_inter, 2) + _tri_dot(strict_lower, kh * dk_state, 2)
                       + state_term)
                sigh = sig[:, sl]
                df = dlf / f[:, sl] - (dk_intra + dk_state)
                dlb_acc[:, sl] += jnp.sum(df * (1.0 - sigh), axis=0, keepdims=True)
                dz_ref[rs, HW + HD * h:HW + HD * (h + 1)] = df * (1.0 - lb[:, sl]) * sigh * (1.0 - sigh)
                sqh = sq[:, sl]
                dz_ref[rs, sl] = dq * (sqh * (1.0 + zq[:, sl] * (1.0 - sqh)))

        @pl.when(i == n_steps - 1)
        def _():
            dl0 = dlb_acc[...] * lb * (1.0 - lb)
            dlb_ref[0:1, 0:HW] = dl0
            dlb_ref[1:2, 0:HW] = -dl0

    rows = CHUNK * CHUNKS_PER_STEP
    n_steps = s // rows
    rev = lambda i: n_steps - 1 - i
    zspec = lambda cb: pl.BlockSpec((rows, HW), lambda i, cb=cb: (rev(i), cb))
    slot = jax.ShapeDtypeStruct((SLOT, D_MODEL), F32)
    return _call_behind(
        deps, body, name="hgrn_bwd", grid=(n_steps,),
        out_shape=[jax.ShapeDtypeStruct((s, IN_WIDTH), F32), slot, slot],
        in_specs=[zspec(0), zspec(1), zspec(2), zspec(3), pl.BlockSpec((rows, HW), lambda i: (rev(i), 0)),
                  pl.BlockSpec((rows, HW), lambda i: (rev(i), 0)),
                  pl.BlockSpec((CHUNKS_PER_STEP, HEADS, HD, HD), lambda i: (rev(i), 0, 0, 0)), _full((2, HW)),
                  _full((HEADS, HD)), ANY_SPACE],
        out_specs=[pl.BlockSpec((rows, 4 * HW), lambda i: (rev(i), 0)), _full((SLOT, D_MODEL)), _full((SLOT, D_MODEL))],
        scratch_shapes=[pltpu.VMEM((HEADS, HD, HD), F32), pltpu.VMEM((rows, HW), F32), pltpu.VMEM((1, HW), F32)],
        input_output_aliases={9: 0},
        compiler_params=_params(("arbitrary",)),
    )(z, z, z, z, o, dmix, states, lb_logits, gn, dz_in)


def _in_bwd(dz, w_t, x0, g, dx1, deps):
    s = x0.shape[0]
    tm = min(ROW_TILE, s)

    def body(dz_ref, w_ref, x_ref, g_ref, d_ref, dx_ref, slot_ref):
        _zero_slot(slot_ref)
        dx, dg = _rms_bwd(x_ref[...], g_ref[...], _mm(dz_ref[...], w_ref[...]))
        dx_ref[...] = d_ref[...] + dx
        slot_ref[0:1, :] += dg

    row_f32 = pl.BlockSpec((tm, D_MODEL), lambda i: (i, 0))
    return _call_behind(
        deps, body, name="in_bwd", grid=(s // tm,),
        out_shape=[jax.ShapeDtypeStruct((s, D_MODEL), F32), jax.ShapeDtypeStruct((SLOT, D_MODEL), F32)],
        in_specs=[pl.BlockSpec((tm, IN_WIDTH), lambda i: (i, 0)), VMEM_WHOLE, row_f32, _full((1, D_MODEL)), row_f32],
        out_specs=[row_f32, _full((SLOT, D_MODEL))],
        compiler_params=_params(("arbitrary",)),
    )(dz, w_t, x0, g, dx1)


def kernel(x, mem, norm_mix_g, w_in, lb_logits, hgrn_norm_g, w_pool, pool_scale, w_out, norm_x_g, norm_mem_g, w_xq, w_xk, w_xv, w_xo, norm_ffn_g, w_ff1, w_ff2, final_norm_g, loss_target, m_norm_mix_g, m_w_in, m_lb_logits, m_hgrn_norm_g, m_w_pool, m_pool_scale, m_w_out, m_norm_x_g, m_norm_mem_g, m_w_xq, m_w_xk, m_w_xv, m_w_xo, m_norm_ffn_g, m_w_ff1, m_w_ff2, m_final_norm_g, v_norm_mix_g, v_w_in, v_lb_logits, v_hgrn_norm_g, v_w_pool, v_pool_scale, v_w_out, v_norm_x_g, v_norm_mem_g, v_w_xq, v_w_xk, v_w_xv, v_w_xo, v_norm_ffn_g, v_w_ff1, v_w_ff2, v_final_norm_g):
    x0 = x[0]
    mem0 = mem[0]
    tgt = loss_target[0]
    gn = hgrn_norm_g[0]
    gfin = final_norm_g.reshape(1, D_MODEL)
    wp = w_pool[0]
    heads_2d = lambda w: w.reshape(D_MODEL // N_DEV, D_MODEL)
    xo_2d = lambda w: w.reshape(D_MODEL, D_MODEL // N_DEV)

    first = _all_gather_weights([w_in[0].T], [w_out[0], heads_2d(w_xq), heads_2d(w_xk), heads_2d(w_xv), xo_2d(w_xo).T,
                                              w_ff1[0], w_ff2[0]])
    win_t = first[0].reshape(IN_WIDTH, D_MODEL)
    ga_attn, ga_mlp = _gather_first_start([first[1:6], first[6:8]], "gather_first_start")

    z, h = _in_proj(x0, norm_mix_g, win_t, deps=[ga_attn[3]])
    mixed_a, o_pre, states = _hgrn_fwd(z, lb_logits, gn)
    lands = _split_wait(_gather_first_copies, ga_attn, o_pre, "gather_attn_first_wait")
    gb_attn = _gather_forward_start(lands, "gather_attn_forward_start")
    mixed, pooled = _pool_fwd(z, wp, pool_scale, mixed_a, deps=[gb_attn[3]])
    lands = _split_wait(_gather_forward_copies, gb_attn, pooled, "gather_attn_forward_wait")
    wout_f, wq_f, wk_f, wv_f, wo_t = (t.reshape(D_MODEL, D_MODEL) for t in lands)
    x1 = _out_proj(x0, mixed, wout_f)
    lands = _split_wait(_gather_first_copies, ga_mlp, x1, "gather_mlp_first_wait")
    gb_mlp = _gather_forward_start(lands, "gather_mlp_forward_start")
    hm, xk, xv = _mem_kv(mem0, norm_mem_g, wk_f, wv_f, deps=[gb_mlp[3]])
    x2, hq, xq, att = _xattn_fwd(x1, norm_x_g, wq_f, xk, xv, wo_t)
    w1_b, w2_b = _split_wait(_gather_forward_copies, gb_mlp, x2, "gather_mlp_forward_wait")
    dx3, u, hf, slot_fin = _mlp_fwd_loss(x2, norm_ffn_g, w1_b, w2_b, gfin, tgt)

    rows = lambda t, r: t.reshape(N_DEV, r, D_MODEL)
    da, dx2, slot_ffn = _mlp_bwd(dx3, u, x2, norm_ffn_g, w1_b, w2_b)
    dw2 = _wgrad(u, dx3, "wgrad_ff2")
    dw1 = _wgrad(hf, da, "wgrad_ff1", col_blocks=True)
    ex_ff = _all_to_all_start([dw1, rows(dw2, FF_BLK)], [], "exchange_ff_start")
    dx1, dxq, dxk, dxv, slot_x = _xattn_bwd(dx2, x1, norm_x_g, xq, xk, xv, wq_f, wo_t, deps=[ex_ff[3]])
    dwo_t = _wgrad(dx2, att, "wgrad_xo")
    dwq = _wgrad(hq, dxq, "wgrad_xq")
    dwk, dwv, slot_mem = _mem_bwd(mem0, norm_mem_g, hm, dxk, dxv, wk_f, wv_f)
    ex_attn = _all_to_all_start([rows(dwq, 128), rows(dwk, 128), rows(dwv, 128), rows(dwo_t, 128)], [],
                                "exchange_attn_start")
    dmix = _matmul_nt(dx1, wout_f, "out_proj_bwd", deps=[ex_attn[3]])
    dwout = _wgrad(mixed, dx1, "wgrad_out")
    dz_pool, d_wpool, slot_ps = _pool_bwd(dmix, pooled, wp, pool_scale)
    small0 = jnp.concatenate([slot_x, slot_mem, slot_ffn, slot_fin, slot_ps], axis=0)
    ex_out = _all_to_all_start([rows(dwout, 128)], [small0, d_wpool], "exchange_out_start")
    dz, slot_lb, slot_gn = _hgrn_bwd(z, o_pre, dmix, states, lb_logits, gn, dz_pool, deps=[ex_out[3]])
    dwin_t = _wgrad(dz, h, "wgrad_in")
    small1 = jnp.concatenate([slot_lb, slot_gn], axis=0)
    ex_in = _all_to_all_start([rows(dwin_t, 320)], [small1], "exchange_in_start")
    grad_x, slot_mix = _in_bwd(dz, win_t, x0, norm_mix_g, dx1, deps=[ex_in[3]])
    ex_mix = _all_to_all_start([], [slot_mix], "exchange_mix_start")

    out = {}
    r_1, r_2 = _split_wait(_all_to_all_copies(2), ex_ff, ex_mix[3], "exchange_ff_wait")
    out["w_ff1"] = _sum_adamw(r_1, w_ff1[0], m_w_ff1[0], v_w_ff1[0], "adamw_ff1")
    out["w_ff2"] = _sum_adamw(r_2, w_ff2[0], m_w_ff2[0], v_w_ff2[0], "adamw_ff2")
    r_q, r_k, r_v, r_o = _split_wait(_all_to_all_copies(4), ex_attn, out["w_ff2"][1], "exchange_attn_wait")
    for n, r, (w, m, v) in (("w_xq", r_q, (w_xq, m_w_xq, v_w_xq)), ("w_xk", r_k, (w_xk, m_w_xk, v_w_xk)),
                            ("w_xv", r_v, (w_xv, m_w_xv, v_w_xv))):
        g = _sum_sources(r, "sum_grad_" + n).reshape(w.shape)
        out[n] = (g, *_adamw_whole(g, w, m, v, "adamw_" + n))
    g_xo = _sum_sources(r_o, "sum_grad_xo").T
    out["w_xo"] = (g_xo, *_adamw(g_xo, xo_2d(w_xo), xo_2d(m_w_xo), xo_2d(v_w_xo), "adamw_xo"))
    r_out, r_small0, r_wpool = _split_wait(_all_to_all_copies(1), ex_out, out["w_xo"][1], "exchange_out_wait")
    out["w_out"] = _sum_adamw(r_out, w_out[0], m_w_out[0], v_w_out[0], "adamw_out")
    r_in, r_small1 = _split_wait(_all_to_all_copies(1), ex_in, out["w_out"][1], "exchange_in_wait")
    g_in = _sum_sources(r_in, "sum_grad_in").T
    out["w_in"] = (g_in, *_adamw(g_in, w_in[0], m_w_in[0], v_w_in[0], "adamw_in"))
    (r_small2,) = _split_wait(_all_to_all_copies(0), ex_mix, out["w_in"][1], "exchange_mix_wait")
    row = lambda t: t.reshape(1, -1)
    small_params = {
        "norm_mix_g": (norm_mix_g, m_norm_mix_g, v_norm_mix_g),
        "lb_logits": (lb_logits, m_lb_logits, v_lb_logits),
        "hgrn_norm_g": (hgrn_norm_g[0], m_hgrn_norm_g[0], v_hgrn_norm_g[0]),
        "pool_scale": (pool_scale, m_pool_scale, v_pool_scale),
        "norm_x_g": (norm_x_g, m_norm_x_g, v_norm_x_g),
        "norm_mem_g": (norm_mem_g, m_norm_mem_g, v_norm_mem_g),
        "norm_ffn_g": (norm_ffn_g, m_norm_ffn_g, v_norm_ffn_g),
        "final_norm_g": (row(final_norm_g), row(m_final_norm_g), row(v_final_norm_g)),
        "w_pool": (wp, m_w_pool[0], v_w_pool[0]),
    }
    loss, small_out = _small_update([r_small0, r_small1, r_small2], r_wpool, small_params)
    out.update(small_out)

    shapes = dict(norm_mix_g=norm_mix_g, w_in=w_in, lb_logits=lb_logits, hgrn_norm_g=hgrn_norm_g, w_pool=w_pool,
                  pool_scale=pool_scale, w_out=w_out, norm_x_g=norm_x_g, norm_mem_g=norm_mem_g, w_xq=w_xq, w_xk=w_xk,
                  w_xv=w_xv, w_xo=w_xo, norm_ffn_g=norm_ffn_g, w_ff1=w_ff1, w_ff2=w_ff2, final_norm_g=final_norm_g)
    order = list(shapes)
    group = lambda k: [out[n][k].reshape(shapes[n].shape) for n in order]
    return (loss.reshape(()), grad_x.reshape(x.shape), *group(0), *group(1), *group(2), *group(3))
```

```python
import jax
import jax.numpy as jnp
from jax import lax
from jax.experimental import pallas as pl
from jax.experimental.pallas import tpu as pltpu

F32 = jnp.float32
BF16 = jnp.bfloat16

D_MODEL = 1024
N_DEV = 8
HEADS = 4
HD = 128
HW = HEADS * HD
IN_WIDTH = 5 * HW
XHD = 256
MEM_LEN = 256
D_FF = 4096
FF_BLK = D_FF // N_DEV
POOL_WINDOWS = (2, 4, 8, 16)
POOL_HALO = 16
CHUNK = 64
CHUNKS_PER_STEP = 4
SUB = 16
N_SUB = CHUNK // SUB
EXP_CAP = 80.0
EPS = 1e-6
ROW_TILE = 512
SLOT = 8
V7X_VMEM_LIMIT = 56 * 1024 * 1024

ADAM_LR = 0.001
ADAM_B1 = 0.9
ADAM_B2 = 0.999
ADAM_EPS = 1e-08
ADAM_WD = 0.01
ADAM_STEP = 10

MESH_ID = pl.DeviceIdType.MESH


def _params(sem=None, vmem=V7X_VMEM_LIMIT):
    return pltpu.CompilerParams(dimension_semantics=sem, vmem_limit_bytes=vmem)


def _mm(a, b):
    return lax.dot_general(a.astype(BF16), b.astype(BF16), (((1,), (0,)), ((), ())), preferred_element_type=F32)


def _mm_nt(a, b):
    return lax.dot_general(a.astype(BF16), b.astype(BF16), (((1,), (1,)), ((), ())), preferred_element_type=F32)


def _mm_tn(a, b):
    return lax.dot_general(a.astype(BF16), b.astype(BF16), (((0,), (0,)), ((), ())), preferred_element_type=F32)


def _sigmoid(x):
    return 1.0 / (1.0 + jnp.exp(-x))


def _rms(x):
    return lax.rsqrt(jnp.mean(x * x, axis=-1, keepdims=True) + EPS)


def _rms_bwd(x, g, dh):
    r = _rms(x)
    n = x * r
    dn = dh * g
    dx = r * (dn - n * jnp.mean(dn * n, axis=-1, keepdims=True))
    return dx, jnp.sum(dh * n, axis=0, keepdims=True)


def _tri_dot(tri, x, passes):
    acc = None
    rest = x
    for _ in range(passes):
        piece = rest.astype(BF16)
        part = lax.dot_general(tri, piece, (((1,), (0,)), ((), ())), preferred_element_type=F32)
        acc = part if acc is None else acc + part
        rest = rest - piece.astype(F32)
    return acc


def _adam_update(g, w, m, v):
    nm = ADAM_B1 * m + (1.0 - ADAM_B1) * g
    nv = ADAM_B2 * v + (1.0 - ADAM_B2) * (g * g)
    m_hat = nm / (1.0 - ADAM_B1 ** ADAM_STEP)
    v_hat = nv / (1.0 - ADAM_B2 ** ADAM_STEP)
    return -ADAM_LR * (m_hat / (jnp.sqrt(v_hat) + ADAM_EPS) + ADAM_WD * w), nm, nv


def _full(shape):
    return pl.BlockSpec(shape, lambda *_: (0,) * len(shape))


VMEM_WHOLE = pl.BlockSpec(memory_space=pltpu.VMEM)
ANY_SPACE = pl.BlockSpec(memory_space=pl.ANY)


def _mesh_pos():
    return lax.axis_index("x"), lax.axis_index("y"), lax.axis_index("c")


def _flat(px, py, pc):
    return 4 * px + 2 * py + pc


def _all_gather_weights(shards, cast_only):
    n, nc = len(shards), len(cast_only)
    step = 64

    def body(*refs):
        x_refs, c_refs = refs[:n], refs[n:n + nc]
        out_refs, cast_refs = refs[n + nc:2 * n + nc], refs[2 * n + nc:2 * n + 2 * nc]
        bufs = refs[2 * n + 2 * nc:3 * n + 2 * nc]
        send_sems, recv_sems, local_sems = refs[3 * n + 2 * nc:]
        x, y, c = _mesh_pos()
        me, sibling = (x, y, c), (x, y, 1 - c)
        chips = [(1 - x, y), (x, 1 - y), (1 - x, 1 - y)]

        def copy(a, k, blk, to, src=None):
            rows = out_refs[a].at[_flat(*blk)]
            return pltpu.make_async_remote_copy(
                src_ref=rows if src is None else src, dst_ref=rows,
                send_sem=send_sems.at[7 * a + k], recv_sem=recv_sems.at[7 * a + k], device_id=to, device_id_type=MESH_ID)

        def cast_rows(src, dst, rows):
            def cast(i, carry):
                r0 = pl.multiple_of(i * step, step)
                dst[pl.ds(r0, step), :] = src[pl.ds(r0, step), :].astype(BF16)
                return carry
            lax.fori_loop(0, rows // step, cast, 0)

        first, mine = [], []
        for a in range(n):
            cast_rows(x_refs[a], bufs[a], shards[a].shape[0])
            mine.append(pltpu.make_async_copy(bufs[a], out_refs[a].at[_flat(*me)], local_sems.at[a]))
            first.append(copy(a, 0, me, sibling, src=bufs[a]))
            first += [copy(a, 1 + j, me, (*chip, c), src=bufs[a]) for j, chip in enumerate(chips)]
            for cp in [mine[-1]] + first[-4:]:
                cp.start()
        for a in range(nc):
            cast_rows(c_refs[a], cast_refs[a], cast_only[a].shape[0])
        passed = []
        for j, chip in enumerate(chips):
            for a in range(n):
                copy(a, 1 + j, (*chip, c), me).wait_recv()
                passed.append(copy(a, 4 + j, (*chip, c), sibling))
                passed[-1].start()
        for a in range(n):
            copy(a, 0, sibling, me).wait_recv()
            for j, chip in enumerate(chips):
                copy(a, 4 + j, (*chip, 1 - c), me).wait_recv()
        for cp in first + passed:
            cp.wait_send()
        for cp in mine:
            cp.wait()

    return pl.pallas_call(
        body, name="all_gather_w_in",
        out_shape=[jax.ShapeDtypeStruct((N_DEV,) + s.shape, BF16) for s in shards]
        + [jax.ShapeDtypeStruct(s.shape, BF16) for s in cast_only],
        in_specs=[VMEM_WHOLE] * (n + nc), out_specs=[ANY_SPACE] * n + [VMEM_WHOLE] * nc,
        scratch_shapes=[pltpu.VMEM(s.shape, BF16) for s in shards]
        + [pltpu.SemaphoreType.DMA((7 * n,)), pltpu.SemaphoreType.DMA((7 * n,)), pltpu.SemaphoreType.DMA((n,))],
        compiler_params=_params(),
    )(*shards, *cast_only)


HBM_SPEC = pl.BlockSpec(memory_space=pltpu.HBM)
SEM_SPEC = pl.BlockSpec(memory_space=pltpu.SEMAPHORE)
EFFECT = pltpu.SideEffectType.DATAFLOW_SIDE_EFFECTING
TOKEN = jax.ShapeDtypeStruct((8, 128), F32)


def _in_hbm(a):
    return pltpu.with_memory_space_constraint(a, pltpu.HBM)


def _split_start(copies_of, srcs, lands, n_sems, name):
    ns, nl, k = len(srcs), len(lands), len(n_sems)

    def body(*refs):
        src_refs, land_refs = refs[:ns], refs[ns:ns + nl]
        sems = refs[ns + nl:ns + nl + k]
        token = refs[-1]
        for cp in copies_of(src_refs, land_refs, sems):
            cp.start()
        token[...] = jnp.zeros_like(token)

    outs = pl.pallas_call(
        body, name=name,
        out_shape=[pltpu.SemaphoreType.DMA((q,)) for q in n_sems]
        + [pltpu.HBM(a.shape, a.dtype) for a in list(srcs) + list(lands)] + [TOKEN],
        in_specs=[HBM_SPEC] * (ns + nl),
        out_specs=[SEM_SPEC] * k + [HBM_SPEC] * (ns + nl) + [VMEM_WHOLE],
        input_output_aliases={i: k + i for i in range(ns + nl)},
        compiler_params=pltpu.CompilerParams(has_side_effects=EFFECT),
    )(*[_in_hbm(a) for a in list(srcs) + list(lands)])
    return outs[:k], outs[k:k + ns], outs[k + ns:k + ns + nl], outs[-1]


def _split_wait(copies_of, handle, after, name):
    sems, srcs, lands, _ = handle
    ns, nl, k = len(srcs), len(lands), len(sems)

    def body(*refs):
        src_refs, land_refs = refs[:ns], refs[ns:ns + nl]
        sem_refs = refs[ns + nl:ns + nl + k]
        for cp in copies_of(src_refs, land_refs, sem_refs):
            cp.wait()

    outs = pl.pallas_call(
        body, name=name,
        out_shape=[pltpu.HBM(a.shape, a.dtype) for a in list(srcs) + list(lands)],
        in_specs=[HBM_SPEC] * (ns + nl) + [SEM_SPEC] * k + [ANY_SPACE],
        out_specs=[HBM_SPEC] * (ns + nl),
        input_output_aliases={i: i for i in range(ns + nl)},
        compiler_params=pltpu.CompilerParams(has_side_effects=EFFECT),
    )(*srcs, *lands, *sems, after)
    return outs[ns:]


def _gather_first_copies(shard_refs, land_refs, sems):
    send_sems, recv_sems, local_sems = sems
    x, y, c = _mesh_pos()
    me = _flat(x, y, c)
    peers = [(x, y, 1 - c), (1 - x, y, c), (x, 1 - y, c), (1 - x, 1 - y, c)]
    copies = []
    for a, (shard, land) in enumerate(zip(shard_refs, land_refs)):
        copies.append(pltpu.make_async_copy(shard, land.at[me], local_sems.at[a]))
        for k, peer in enumerate(peers):
            copies.append(pltpu.make_async_remote_copy(
                src_ref=shard, dst_ref=land.at[me], send_sem=send_sems.at[4 * a + k], recv_sem=recv_sems.at[4 * a + k],
                device_id=peer, device_id_type=MESH_ID))
    return copies


def _gather_forward_copies(src_refs, land_refs, sems):
    del src_refs
    send_sems, recv_sems = sems
    x, y, c = _mesh_pos()
    chips = [(1 - x, y), (x, 1 - y), (1 - x, 1 - y)]
    copies = []
    for a, land in enumerate(land_refs):
        for j, chip in enumerate(chips):
            rows = land.at[_flat(*chip, c)]
            copies.append(pltpu.make_async_remote_copy(
                src_ref=rows, dst_ref=rows, send_sem=send_sems.at[3 * a + j], recv_sem=recv_sems.at[3 * a + j],
                device_id=(x, y, 1 - c), device_id_type=MESH_ID))
    return copies


def _gather_first_start(groups, name):
    shards = [s for g in groups for s in g]
    lands = [lax.empty((N_DEV,) + s.shape, s.dtype) for s in shards]
    bounds = [sum(len(g) for g in groups[:i]) for i in range(len(groups) + 1)]

    def copies_of(src_refs, land_refs, sems):
        copies = []
        for i in range(len(groups)):
            lo, hi = bounds[i], bounds[i + 1]
            copies += _gather_first_copies(src_refs[lo:hi], land_refs[lo:hi], sems[3 * i:3 * i + 3])
        return copies

    n_sems = tuple(q for g in groups for q in (4 * len(g), 4 * len(g), len(g)))
    sems, srcs, lands, token = _split_start(copies_of, shards, lands, n_sems, name)
    return [(sems[3 * i:3 * i + 3], srcs[bounds[i]:bounds[i + 1]], lands[bounds[i]:bounds[i + 1]], token)
            for i in range(len(groups))]


def _gather_forward_start(lands, name):
    n = len(lands)
    return _split_start(_gather_forward_copies, [], lands, (3 * n, 3 * n), name)


def _all_to_all_copies(n_scattered):
    def copies_of(src_refs, land_refs, sems):
        send_sems, recv_sems, local_sems = sems
        x, y, c = _mesh_pos()
        me = _flat(x, y, c)
        copies = []
        for a, (src, land) in enumerate(zip(src_refs, land_refs)):
            scattered = a < n_scattered
            copies.append(pltpu.make_async_copy(src.at[me] if scattered else src, land.at[me], local_sems.at[a]))
            for k in range(1, N_DEV):
                peer = (1 - x if k & 4 else x, 1 - y if k & 2 else y, 1 - c if k & 1 else c)
                copies.append(pltpu.make_async_remote_copy(
                    src_ref=src.at[_flat(*peer)] if scattered else src, dst_ref=land.at[me],
                    send_sem=send_sems.at[7 * a + k - 1], recv_sem=recv_sems.at[7 * a + k - 1],
                    device_id=peer, device_id_type=MESH_ID))
        return copies
    return copies_of


def _all_to_all_start(scattered, broadcast, name):
    srcs = list(scattered) + list(broadcast)
    lands = [lax.empty(a.shape, a.dtype) for a in scattered] + [lax.empty((N_DEV,) + a.shape, a.dtype) for a in broadcast]
    n = len(srcs)
    return _split_start(_all_to_all_copies(len(scattered)), srcs, lands, (7 * n, 7 * n, n), name)


def _call_behind(deps, body, *, in_specs, **kwargs):
    n_in, n_dep = len(in_specs), len(deps)

    def body_without_deps(*refs):
        return body(*refs[:n_in], *refs[n_in + n_dep:])

    call = pl.pallas_call(body_without_deps, in_specs=list(in_specs) + [ANY_SPACE] * n_dep, **kwargs)
    return lambda *operands: call(*operands, *deps)


def _row_tile(rows):
    for cand in (256, 128, 64, 32, 16):
        if rows % cand == 0:
            return cand
    return rows


def _sum_sources(recv, name):
    _, rows, cols = recv.shape
    tile = _row_tile(rows)

    def body(r_ref, o_ref):
        acc = r_ref[0].astype(F32)
        for d in range(1, N_DEV):
            acc = acc + r_ref[d].astype(F32)
        o_ref[...] = acc

    return pl.pallas_call(
        body, name=name, grid=(rows // tile,),
        out_shape=jax.ShapeDtypeStruct((rows, cols), F32),
        in_specs=[pl.BlockSpec((N_DEV, tile, cols), lambda i: (0, i, 0))],
        out_specs=pl.BlockSpec((tile, cols), lambda i: (i, 0)),
        compiler_params=_params(("parallel",)),
    )(recv)


def _adamw(g, w, m, v, name):
    rows, cols = g.shape
    tile = _row_tile(rows)

    def body(g_ref, w_ref, m_ref, v_ref, d_ref, nm_ref, nv_ref):
        d_ref[...], nm_ref[...], nv_ref[...] = _adam_update(g_ref[...], w_ref[...], m_ref[...], v_ref[...])

    spec = pl.BlockSpec((tile, cols), lambda i: (i, 0))
    shp = jax.ShapeDtypeStruct((rows, cols), F32)
    return pl.pallas_call(
        body, name=name, grid=(rows // tile,), out_shape=[shp, shp, shp],
        in_specs=[spec] * 4, out_specs=[spec] * 3,
        compiler_params=_params(("parallel",)),
    )(g, w, m, v)


def _adamw_whole(g, w, m, v, name):
    def body(g_ref, w_ref, m_ref, v_ref, d_ref, nm_ref, nv_ref):
        d_ref[...], nm_ref[...], nv_ref[...] = _adam_update(g_ref[...], w_ref[...], m_ref[...], v_ref[...])

    shp = jax.ShapeDtypeStruct(g.shape, F32)
    return pl.pallas_call(
        body, name=name, out_shape=[shp, shp, shp], in_specs=[VMEM_WHOLE] * 4, out_specs=[VMEM_WHOLE] * 3,
        compiler_params=_params(),
    )(g, w, m, v)


def _sum_adamw(recv, w, m, v, name):
    _, rows, cols = recv.shape
    tile = _row_tile(rows)

    def body(r_ref, w_ref, m_ref, v_ref, g_ref, d_ref, nm_ref, nv_ref):
        acc = r_ref[0].astype(F32)
        for d in range(1, N_DEV):
            acc = acc + r_ref[d].astype(F32)
        g_ref[...] = acc
        d_ref[...], nm_ref[...], nv_ref[...] = _adam_update(acc, w_ref[...], m_ref[...], v_ref[...])

    spec = pl.BlockSpec((tile, cols), lambda i: (i, 0))
    shp = jax.ShapeDtypeStruct((rows, cols), F32)
    return pl.pallas_call(
        body, name=name, grid=(rows // tile,), out_shape=[shp] * 4,
        in_specs=[pl.BlockSpec((N_DEV, tile, cols), lambda i: (0, i, 0)), spec, spec, spec], out_specs=[spec] * 4,
        compiler_params=_params(("parallel",)),
    )(recv, w, m, v)


SMALL_SLOTS = {"norm_x_g": (0, 0, 1, D_MODEL), "norm_mem_g": (0, 8, 1, D_MODEL), "norm_ffn_g": (0, 16, 1, D_MODEL),
               "final_norm_g": (0, 24, 1, D_MODEL), "pool_scale": (0, 32, 1, HW),
               "lb_logits": (1, 0, 2, HW), "hgrn_norm_g": (1, 8, HEADS, HD), "norm_mix_g": (2, 0, 1, D_MODEL)}
LOSS_ROW = 25
SMALL_ORDER = ("norm_mix_g", "lb_logits", "hgrn_norm_g", "pool_scale", "norm_x_g", "norm_mem_g", "norm_ffn_g",
               "final_norm_g", "w_pool")


def _small_update(srecvs, wprecv, params):
    flat = [t for n in SMALL_ORDER for t in params[n]]
    nb = len(srecvs)
    n_in = nb + 1 + len(flat)

    def body(*refs):
        s_refs, wp_ref = refs[0:nb], refs[nb]
        in_refs = refs[nb + 1:n_in]
        loss_ref = refs[n_in]
        out_refs = refs[n_in + 1:-nb]
        accs = refs[-nb:]
        for s_ref, acc in zip(s_refs, accs):
            total = s_ref[0]
            for d in range(1, N_DEV):
                total = total + s_ref[d]
            acc[...] = total
        loss_ref[...] = accs[0][LOSS_ROW:LOSS_ROW + 1, 0:1]
        for i, name in enumerate(SMALL_ORDER):
            w_ref, m_ref, v_ref = in_refs[3 * i:3 * i + 3]
            g_ref, d_ref, nm_ref, nv_ref = out_refs[4 * i:4 * i + 4]
            if name == "w_pool":
                g = wp_ref[0]
                for d in range(1, N_DEV):
                    g = g + wp_ref[d]
            else:
                buf, r0, nr, nc = SMALL_SLOTS[name]
                g = accs[buf][r0:r0 + nr, 0:nc]
            g_ref[...] = g
            d_ref[...], nm_ref[...], nv_ref[...] = _adam_update(g, w_ref[...], m_ref[...], v_ref[...])

    out_shape = [jax.ShapeDtypeStruct((1, 1), F32)]
    for n in SMALL_ORDER:
        out_shape += [jax.ShapeDtypeStruct(params[n][0].shape, F32)] * 4
    outs = pl.pallas_call(
        body, name="small_update", out_shape=out_shape,
        in_specs=[VMEM_WHOLE] * n_in, out_specs=[VMEM_WHOLE] * len(out_shape),
        scratch_shapes=[pltpu.VMEM(r.shape[1:], F32) for r in srecvs],
        compiler_params=_params(),
    )(*srecvs, wprecv, *flat)
    return outs[0], {n: outs[1 + 4 * i:5 + 4 * i] for i, n in enumerate(SMALL_ORDER)}


def _in_proj(x, g, w_t, deps):
    s = x.shape[0]
    tm = min(ROW_TILE, s)

    def body(x_ref, g_ref, w_ref, z_ref, h_ref):
        xv = x_ref[...]
        h = (xv * _rms(xv) * g_ref[...]).astype(BF16)
        h_ref[...] = h
        z_ref[...] = _mm_nt(h, w_ref[...])

    return _call_behind(
        deps, body, name="in_proj", grid=(s // tm,),
        out_shape=[jax.ShapeDtypeStruct((s, IN_WIDTH), F32), jax.ShapeDtypeStruct((s, D_MODEL), BF16)],
        in_specs=[pl.BlockSpec((tm, D_MODEL), lambda i: (i, 0)), _full((1, D_MODEL)), VMEM_WHOLE],
        out_specs=[pl.BlockSpec((tm, IN_WIDTH), lambda i: (i, 0)), pl.BlockSpec((tm, D_MODEL), lambda i: (i, 0))],
        compiler_params=_params(("parallel",)),
    )(x, g, w_t)


def _chunk_masks():
    row = lax.broadcasted_iota(jnp.int32, (CHUNK, CHUNK), 0)
    col = lax.broadcasted_iota(jnp.int32, (CHUNK, CHUNK), 1)
    return row, col


def _ones_where(mask):
    return jnp.where(mask, 1.0, 0.0).astype(BF16)


def _hgrn_gates(zq, zf, lb):
    sq = _sigmoid(zq)
    sig = _sigmoid(zf)
    f = lb + (1.0 - lb) * sig
    return zq * sq, sq, sig, f


def _hgrn_intra_factors(b_scr, r0, bh, qh, kh, sl):
    trow = lax.broadcasted_iota(jnp.int32, (CHUNK, HD), 0)
    eq, ek = [], []
    for j in range(N_SUB):
        if j == 0:
            base = jnp.zeros((1, HD), F32)
        else:
            base = b_scr[r0 + SUB * j - 1:r0 + SUB * j, sl]
        in_j = (trow >= SUB * j) & (trow < SUB * (j + 1))
        eq.append(jnp.where(in_j, jnp.exp(bh - base), 0.0))
        ek.append(jnp.where(trow < SUB * (j + 1), jnp.exp(jnp.minimum(base - bh, EXP_CAP)), 0.0))
    eqcat = jnp.concatenate(eq, axis=1)
    ekcat = jnp.concatenate(ek, axis=1)
    qcat = jnp.concatenate([qh] * N_SUB, axis=1) * eqcat
    kecat = jnp.concatenate([kh] * N_SUB, axis=1) * ekcat
    return qcat, kecat, eqcat, ekcat


def _sum_lane_blocks(a):
    out = a[:, 0:HD]
    for j in range(1, N_SUB):
        out = out + a[:, HD * j:HD * (j + 1)]
    return out


def _hgrn_fwd(z, lb_logits, gn):
    s = z.shape[0]
    n_chunks = s // CHUNK

    def body(zq_ref, zf_ref, zi_ref, zg_ref, lbl_ref, gn_ref, oa_ref, o_ref, st_ref, state, b_scr):
        @pl.when(pl.program_id(0) == 0)
        def _():
            state[...] = jnp.zeros_like(state)

        lb = _sigmoid(lbl_ref[0:1, :] - lbl_ref[1:2, :])
        row, col = _chunk_masks()
        causal = col <= row
        tri = _ones_where(causal)
        for c in range(CHUNKS_PER_STEP):
            r0 = CHUNK * c
            rs = slice(r0, r0 + CHUNK)
            st_ref[c] = state[...]
            q, _, _, f = _hgrn_gates(zq_ref[rs, :], zf_ref[rs, :], lb)
            kk = 1.0 - f
            b_scr[rs, :] = _tri_dot(tri, jnp.log(f), 3)
            for h in range(HEADS):
                sl = slice(HD * h, HD * (h + 1))
                bh = b_scr[rs, sl]
                qh, kh, vh = q[:, sl], kk[:, sl], zi_ref[rs, sl]
                st = state[h]
                b_last = b_scr[r0 + CHUNK - 1:r0 + CHUNK, sl]
                qcat, kecat, _, _ = _hgrn_intra_factors(b_scr, r0, bh, qh, kh, sl)
                a = jnp.where(causal, _mm_nt(qcat, kecat), 0.0)
                o = _mm(a, vh) + _mm_nt(qh * jnp.exp(bh), st)
                state[h] = st * jnp.exp(b_last) + _mm_tn(vh, kh * jnp.exp(b_last - bh))
                o_ref[rs, sl] = o
                zg = zg_ref[rs, sl]
                oa_ref[rs, sl] = (o * _rms(o) * gn_ref[h:h + 1, :] * zg * _sigmoid(zg)).astype(BF16)

    rows = CHUNK * CHUNKS_PER_STEP
    zspec = lambda cb: pl.BlockSpec((rows, HW), lambda i, cb=cb: (i, cb))
    return pl.pallas_call(
        body, name="hgrn_fwd", grid=(s // rows,),
        out_shape=[jax.ShapeDtypeStruct((s, 2 * HW), BF16), jax.ShapeDtypeStruct((s, HW), F32),
                   jax.ShapeDtypeStruct((n_chunks, HEADS, HD, HD), F32)],
        in_specs=[zspec(0), zspec(1), zspec(2), zspec(3), _full((2, HW)), _full((HEADS, HD))],
        out_specs=[pl.BlockSpec((rows, HW), lambda i: (i, 0)), pl.BlockSpec((rows, HW), lambda i: (i, 0)),
                   pl.BlockSpec((CHUNKS_PER_STEP, HEADS, HD, HD), lambda i: (i, 0, 0, 0))],
        scratch_shapes=[pltpu.VMEM((HEADS, HD, HD), F32), pltpu.VMEM((rows, HW), F32)],
        compiler_params=_params(("arbitrary",)),
    )(z, z, z, z, lb_logits, gn)


def _pool_counts(tile_idx, tm):
    t = tile_idx * tm + lax.broadcasted_iota(jnp.int32, (tm, 1), 0)
    return [1.0 / jnp.minimum(t + 1, w).astype(F32) for w in POOL_WINDOWS]


def _pool_fwd(z, w_pool, scale, mixed_in, deps):
    s = z.shape[0]
    tm = min(ROW_TILE, s)

    def body(p_ref, w_ref, sc_ref, mixin_ref, ob_ref, pooled_ref, ext):
        i = pl.program_id(0)

        @pl.when(i == 0)
        def _():
            ext[0:POOL_HALO, :] = jnp.zeros((POOL_HALO, HW), F32)

        @pl.when(i > 0)
        def _():
            ext[0:POOL_HALO, :] = ext[tm:tm + POOL_HALO, :]

        ext[POOL_HALO:POOL_HALO + tm, :] = p_ref[...]
        inv = _pool_counts(i, tm)
        for g, w in enumerate(POOL_WINDOWS):
            sl = slice(HD * g, HD * (g + 1))
            p = ext[POOL_HALO:POOL_HALO + tm, sl]
            win = p
            for d in range(1, w):
                win = win + ext[POOL_HALO - d:POOL_HALO - d + tm, sl]
            pooled = (win * inv[g] - p).astype(BF16)
            pooled_ref[:, sl] = pooled
            ob_ref[:, sl] = (_mm(pooled, w_ref[g]) * sc_ref[:, sl]).astype(BF16)

    return _call_behind(
        deps, body, name="pool_fwd", grid=(s // tm,),
        out_shape=[jax.ShapeDtypeStruct((s, 2 * HW), BF16), jax.ShapeDtypeStruct((s, HW), BF16)],
        in_specs=[pl.BlockSpec((tm, HW), lambda i: (i, 4)), _full((HEADS, HD, HD)), _full((1, HW)), ANY_SPACE],
        out_specs=[pl.BlockSpec((tm, HW), lambda i: (i, 1)), pl.BlockSpec((tm, HW), lambda i: (i, 0))],
        scratch_shapes=[pltpu.VMEM((tm + POOL_HALO, HW), F32)],
        input_output_aliases={3: 0},
        compiler_params=_params(("arbitrary",)),
    )(z, w_pool, scale, mixed_in)


def _out_proj(x, mixed, w_out):
    s = x.shape[0]
    tm = min(ROW_TILE, s)

    def body(x_ref, a_ref, w_ref, o_ref):
        o_ref[...] = x_ref[...] + _mm(a_ref[...], w_ref[...])

    return pl.pallas_call(
        body, name="out_proj", grid=(s // tm,),
        out_shape=jax.ShapeDtypeStruct((s, D_MODEL), F32),
        in_specs=[pl.BlockSpec((tm, D_MODEL), lambda i: (i, 0)), pl.BlockSpec((tm, D_MODEL), lambda i: (i, 0)), VMEM_WHOLE],
        out_specs=pl.BlockSpec((tm, D_MODEL), lambda i: (i, 0)),
        compiler_params=_params(("parallel",)),
    )(x, mixed, w_out)


def _mem_kv(mem, g, wk, wv, deps):
    def body(m_ref, g_ref, wk_ref, wv_ref, hm_ref, k_ref, v_ref):
        m = m_ref[...]
        hm = (m * _rms(m) * g_ref[...]).astype(BF16)
        hm_ref[...] = hm
        k_ref[...] = _mm(hm, wk_ref[...]).astype(BF16)
        v_ref[...] = _mm(hm, wv_ref[...]).astype(BF16)

    shp = jax.ShapeDtypeStruct((MEM_LEN, D_MODEL), BF16)
    return _call_behind(
        deps, body, name="mem_kv", out_shape=[shp, shp, shp],
        in_specs=[VMEM_WHOLE] * 4, out_specs=[VMEM_WHOLE] * 3,
        compiler_params=_params(),
    )(mem, g, wk, wv)


def _softmax_rows(sc):
    e = jnp.exp(sc - jnp.max(sc, axis=-1, keepdims=True))
    return e / jnp.sum(e, axis=-1, keepdims=True)


def _xattn_fwd(x, g, wq, xk, xv, wo_t, deps):
    s = x.shape[0]
    tm = min(ROW_TILE, s)
    scale = XHD ** -0.5

    def body(x_ref, g_ref, wq_ref, k_ref, v_ref, wo_ref, o_ref, hq_ref, q_ref, att_ref):
        xv_ = x_ref[...]
        hq = (xv_ * _rms(xv_) * g_ref[...]).astype(BF16)
        hq_ref[...] = hq
        q_ref[...] = (_mm(hq, wq_ref[...]) * scale).astype(BF16)
        for h in range(HEADS):
            sl = slice(XHD * h, XHD * (h + 1))
            p = _softmax_rows(_mm_nt(q_ref[:, sl], k_ref[:, sl]))
            att_ref[:, sl] = _mm(p, v_ref[:, sl]).astype(BF16)
        o_ref[...] = xv_ + _mm_nt(att_ref[...], wo_ref[...])

    row_f32 = pl.BlockSpec((tm, D_MODEL), lambda i: (i, 0))
    bshape = jax.ShapeDtypeStruct((s, D_MODEL), BF16)
    return _call_behind(
        deps, body, name="xattn_fwd", grid=(s // tm,),
        out_shape=[jax.ShapeDtypeStruct((s, D_MODEL), F32), bshape, bshape, bshape],
        in_specs=[row_f32, _full((1, D_MODEL)), VMEM_WHOLE, VMEM_WHOLE, VMEM_WHOLE, VMEM_WHOLE],
        out_specs=[row_f32] * 4,
        compiler_params=_params(("parallel",)),
    )(x, g, wq, xk, xv, wo_t)


def _mlp_fwd_loss(x, g, w1, w2, gf, target):
    s = x.shape[0]
    tm = min(ROW_TILE, s)

    def body(x_ref, g_ref, w1_ref, w2_ref, gf_ref, t_ref, dx_ref, u_ref, hf_ref, slot_ref):
        @pl.when(pl.program_id(0) == 0)
        def _():
            slot_ref[...] = jnp.zeros_like(slot_ref)

        xv = x_ref[...]
        hf = (xv * _rms(xv) * g_ref[...]).astype(BF16)
        hf_ref[...] = hf
        acc = xv
        for j in range(N_DEV):
            a = jnp.maximum(_mm(hf, w1_ref[j]), 0.0)
            u = (a * a).astype(BF16)
            u_ref[:, FF_BLK * j:FF_BLK * (j + 1)] = u
            acc = acc + _mm(u, w2_ref[j])
        gfv = gf_ref[...]
        r = _rms(acc)
        n = acc * r
        err = n * gfv - t_ref[...]
        slot_ref[1:2, :] += jnp.sum(jnp.mean(err * err, axis=-1, keepdims=True), axis=0, keepdims=True) * 0.5
        dy = err * (1.0 / D_MODEL)
        slot_ref[0:1, :] += jnp.sum(dy * n, axis=0, keepdims=True)
        dn = dy * gfv
        dx_ref[...] = r * (dn - n * jnp.mean(dn * n, axis=-1, keepdims=True))

    row_f32 = pl.BlockSpec((tm, D_MODEL), lambda i: (i, 0))
    return pl.pallas_call(
        body, name="mlp_fwd_loss", grid=(s // tm,),
        out_shape=[jax.ShapeDtypeStruct((s, D_MODEL), F32), jax.ShapeDtypeStruct((s, D_FF), BF16),
                   jax.ShapeDtypeStruct((s, D_MODEL), BF16), jax.ShapeDtypeStruct((SLOT, D_MODEL), F32)],
        in_specs=[row_f32, _full((1, D_MODEL)), VMEM_WHOLE, VMEM_WHOLE, _full((1, D_MODEL)), row_f32],
        out_specs=[row_f32, pl.BlockSpec((tm, D_FF), lambda i: (i, 0)), row_f32, _full((SLOT, D_MODEL))],
        compiler_params=_params(("arbitrary",)),
    )(x, g, w1, w2, gf, target)


def _zero_slot(slot_ref):
    @pl.when(pl.program_id(0) == 0)
    def _():
        slot_ref[...] = jnp.zeros_like(slot_ref)


def _mlp_bwd(dx3, u, x2, g, w1, w2, deps):
    s = x2.shape[0]
    tm = min(ROW_TILE, s)

    def body(d_ref, u_ref, x_ref, g_ref, w1_ref, w2_ref, da_ref, dx_ref, slot_ref):
        _zero_slot(slot_ref)
        d = d_ref[...]
        d16 = d.astype(BF16)
        dhf = jnp.zeros((tm, D_MODEL), F32)
        for j in range(N_DEV):
            sl = slice(FF_BLK * j, FF_BLK * (j + 1))
            da = (_mm_nt(d16, w2_ref[j]) * (2.0 * jnp.sqrt(u_ref[:, sl].astype(F32)))).astype(BF16)
            da_ref[:, sl] = da
            dhf = dhf + _mm_nt(da, w1_ref[j])
        dx, dg = _rms_bwd(x_ref[...], g_ref[...], dhf)
        dx_ref[...] = d + dx
        slot_ref[0:1, :] += dg

    row_f32 = pl.BlockSpec((tm, D_MODEL), lambda i: (i, 0))
    return _call_behind(
        deps, body, name="mlp_bwd", grid=(s // tm,),
        out_shape=[jax.ShapeDtypeStruct((s, D_FF), BF16), jax.ShapeDtypeStruct((s, D_MODEL), F32),
                   jax.ShapeDtypeStruct((SLOT, D_MODEL), F32)],
        in_specs=[row_f32, pl.BlockSpec((tm, D_FF), lambda i: (i, 0)), row_f32, _full((1, D_MODEL)),
                  VMEM_WHOLE, VMEM_WHOLE],
        out_specs=[pl.BlockSpec((tm, D_FF), lambda i: (i, 0)), row_f32, _full((SLOT, D_MODEL))],
        compiler_params=_params(("arbitrary",)),
    )(dx3, u, x2, g, w1, w2)


def _wgrad(a, b, name, col_blocks=False):
    s, m = a.shape
    n = b.shape[1]
    tm = 1280 if m % 1280 == 0 else min(1024, m)
    tn = min(1024, n)
    blk = n // N_DEV
    per_step = tn // blk if col_blocks else 1
    ts = min(2 * ROW_TILE, s)
    n_s = s // ts

    def body(a_ref, b_ref, o_ref, acc):
        k = pl.program_id(2)

        @pl.when(k == 0)
        def _():
            acc[...] = jnp.zeros_like(acc)

        acc[...] += _mm_tn(a_ref[...], b_ref[...])

        @pl.when(k == n_s - 1)
        def _():
            if col_blocks:
                for p in range(per_step):
                    o_ref[p] = acc[:, blk * p:blk * (p + 1)].astype(BF16)
            else:
                o_ref[...] = acc[...].astype(BF16)

    if col_blocks:
        out_shape = jax.ShapeDtypeStruct((N_DEV, m, blk), BF16)
        out_spec = pl.BlockSpec((per_step, tm, blk), lambda i, j, k: (j, i, 0))
    else:
        out_shape = jax.ShapeDtypeStruct((m, n), BF16)
        out_spec = pl.BlockSpec((tm, tn), lambda i, j, k: (i, j))
    return pl.pallas_call(
        body, name=name, grid=(m // tm, n // tn, n_s), out_shape=out_shape,
        in_specs=[pl.BlockSpec((ts, tm), lambda i, j, k: (k, i)), pl.BlockSpec((ts, tn), lambda i, j, k: (k, j))],
        out_specs=out_spec,
        scratch_shapes=[pltpu.VMEM((tm, tn), F32)],
        compiler_params=_params(("parallel", "parallel", "arbitrary")),
    )(a, b)


def _xattn_bwd(dx2, x1, g, q, xk, xv, wq, wo_t, deps):
    s = x1.shape[0]
    tm = min(ROW_TILE, s)
    scale = XHD ** -0.5

    def body(d_ref, x_ref, g_ref, q_ref, k_ref, v_ref, wq_ref, wo_ref, dx_ref, dq_ref, dk_ref, dv_ref, slot_ref, datt):
        _zero_slot(slot_ref)

        @pl.when(pl.program_id(0) == 0)
        def _():
            dk_ref[...] = jnp.zeros_like(dk_ref)
            dv_ref[...] = jnp.zeros_like(dv_ref)

        d = d_ref[...]
        datt[...] = _mm(d, wo_ref[...]).astype(BF16)
        for h in range(HEADS):
            sl = slice(XHD * h, XHD * (h + 1))
            qh, kh, vh, dah = q_ref[:, sl], k_ref[:, sl], v_ref[:, sl], datt[:, sl]
            p = _softmax_rows(_mm_nt(qh, kh))
            dp = _mm_nt(dah, vh)
            ds = (p * (dp - jnp.sum(dp * p, axis=-1, keepdims=True))).astype(BF16)
            dq_ref[:, sl] = (_mm(ds, kh) * scale).astype(BF16)
            dk_ref[:, sl] += _mm_tn(ds, qh)
            dv_ref[:, sl] += _mm_tn(p, dah)
        dx, dg = _rms_bwd(x_ref[...], g_ref[...], _mm_nt(dq_ref[...], wq_ref[...]))
        dx_ref[...] = d + dx
        slot_ref[0:1, :] += dg

    row_f32 = pl.BlockSpec((tm, D_MODEL), lambda i: (i, 0))
    kv = jax.ShapeDtypeStruct((MEM_LEN, D_MODEL), F32)
    return _call_behind(
        deps, body, name="xattn_bwd", grid=(s // tm,),
        out_shape=[jax.ShapeDtypeStruct((s, D_MODEL), F32), jax.ShapeDtypeStruct((s, D_MODEL), BF16), kv, kv,
                   jax.ShapeDtypeStruct((SLOT, D_MODEL), F32)],
        in_specs=[row_f32, row_f32, _full((1, D_MODEL)), row_f32, VMEM_WHOLE, VMEM_WHOLE, VMEM_WHOLE, VMEM_WHOLE],
        out_specs=[row_f32, row_f32, _full((MEM_LEN, D_MODEL)), _full((MEM_LEN, D_MODEL)), _full((SLOT, D_MODEL))],
        scratch_shapes=[pltpu.VMEM((tm, D_MODEL), BF16)],
        compiler_params=_params(("arbitrary",)),
    )(dx2, x1, g, q, xk, xv, wq, wo_t)


def _mem_bwd(mem, g, hm, dxk, dxv, wk, wv):
    def body(m_ref, g_ref, hm_ref, dk_ref, dv_ref, wk_ref, wv_ref, dwk_ref, dwv_ref, slot_ref):
        dk, dv = dk_ref[...], dv_ref[...]
        hm_ = hm_ref[...]
        dwk_ref[...] = _mm_tn(hm_, dk).astype(BF16)
        dwv_ref[...] = _mm_tn(hm_, dv).astype(BF16)
        _, dg = _rms_bwd(m_ref[...], g_ref[...], _mm_nt(dk, wk_ref[...]) + _mm_nt(dv, wv_ref[...]))
        slot_ref[...] = jnp.zeros_like(slot_ref)
        slot_ref[0:1, :] = dg

    wshape = jax.ShapeDtypeStruct((D_MODEL, D_MODEL), BF16)
    return pl.pallas_call(
        body, name="mem_bwd", out_shape=[wshape, wshape, jax.ShapeDtypeStruct((SLOT, D_MODEL), F32)],
        in_specs=[VMEM_WHOLE] * 7, out_specs=[VMEM_WHOLE] * 3,
        compiler_params=_params(),
    )(mem, g, hm, dxk, dxv, wk, wv)


def _matmul_nt(a, w, name, deps):
    s, k = a.shape
    n = w.shape[0]
    tm = min(ROW_TILE, s)

    def body(a_ref, w_ref, o_ref):
        o_ref[...] = _mm_nt(a_ref[...], w_ref[...])

    return _call_behind(
        deps, body, name=name, grid=(s // tm,),
        out_shape=jax.ShapeDtypeStruct((s, n), F32),
        in_specs=[pl.BlockSpec((tm, k), lambda i: (i, 0)), VMEM_WHOLE],
        out_specs=pl.BlockSpec((tm, n), lambda i: (i, 0)),
        compiler_params=_params(("parallel",)),
    )(a, w)


def _pool_bwd(dmix, pooled, w_pool, scale):
    s = dmix.shape[0]
    tm = min(ROW_TILE, s)
    n_t = s // tm

    def body(do_ref, pl_ref, w_ref, sc_ref, dz_ref, dw_ref, slot_ref, ext):
        i = pl.program_id(0)
        tile = n_t - 1 - i
        _zero_slot(slot_ref)

        @pl.when(i == 0)
        def _():
            dw_ref[...] = jnp.zeros_like(dw_ref)
            ext[tm:tm + POOL_HALO, :] = jnp.zeros((POOL_HALO, HW), F32)

        @pl.when(i > 0)
        def _():
            ext[tm:tm + POOL_HALO, :] = ext[0:POOL_HALO, :]

        inv = _pool_counts(tile, tm)
        dpooled = []
        for g in range(HEADS):
            sl = slice(HD * g, HD * (g + 1))
            pooled_g = pl_ref[:, sl]
            do = do_ref[:, sl]
            slot_ref[0:1, sl] += jnp.sum(_mm(pooled_g, w_ref[g]) * do, axis=0, keepdims=True)
            dy = (do * sc_ref[:, sl]).astype(BF16)
            dw_ref[g] += _mm_tn(pooled_g, dy)
            dpo = _mm_nt(dy, w_ref[g])
            dpooled.append(dpo)
            ext[0:tm, sl] = dpo * inv[g]
        for g, w in enumerate(POOL_WINDOWS):
            sl = slice(HD * g, HD * (g + 1))
            win = ext[0:tm, sl]
            for d in range(1, w):
                win = win + ext[d:d + tm, sl]
            dz_ref[:, sl] = win - dpooled[g]

    return pl.pallas_call(
        body, name="pool_bwd", grid=(n_t,),
        out_shape=[jax.ShapeDtypeStruct((s, IN_WIDTH), F32), jax.ShapeDtypeStruct((HEADS, HD, HD), F32),
                   jax.ShapeDtypeStruct((SLOT, D_MODEL), F32)],
        in_specs=[pl.BlockSpec((tm, HW), lambda i: (n_t - 1 - i, 1)), pl.BlockSpec((tm, HW), lambda i: (n_t - 1 - i, 0)),
                  _full((HEADS, HD, HD)), _full((1, HW))],
        out_specs=[pl.BlockSpec((tm, HW), lambda i: (n_t - 1 - i, 4)), _full((HEADS, HD, HD)), _full((SLOT, D_MODEL))],
        scratch_shapes=[pltpu.VMEM((tm + POOL_HALO, HW), F32)],
        compiler_params=_params(("arbitrary",)),
    )(dmix, pooled, w_pool, scale)


def _hgrn_bwd(z, o, dmix, states, lb_logits, gn, dz_in, deps):
    s = z.shape[0]
    n_chunks = s // CHUNK

    def body(zq_ref, zf_ref, zi_ref, zg_ref, o_ref, do_ref, st_ref, lbl_ref, gn_ref, dzin_ref,
             dz_ref, dlb_ref, dgn_ref, dstate, b_scr, dlb_acc):
        i = pl.program_id(0)

        @pl.when(i == 0)
        def _():
            dstate[...] = jnp.zeros_like(dstate)
            dlb_acc[...] = jnp.zeros_like(dlb_acc)
            dgn_ref[...] = jnp.zeros_like(dgn_ref)
            dlb_ref[...] = jnp.zeros_like(dlb_ref)

        lb = _sigmoid(lbl_ref[0:1, :] - lbl_ref[1:2, :])
        row, col = _chunk_masks()
        causal = col <= row
        tri = _ones_where(causal)
        upper = _ones_where(col >= row)
        strict_lower = _ones_where(col < row)
        for c in reversed(range(CHUNKS_PER_STEP)):
            r0 = CHUNK * c
            rs = slice(r0, r0 + CHUNK)
            zq = zq_ref[rs, :]
            q, sq, sig, f = _hgrn_gates(zq, zf_ref[rs, :], lb)
            kk = 1.0 - f
            b_scr[rs, :] = _tri_dot(tri, jnp.log(f), 3)
            for h in range(HEADS):
                sl = slice(HD * h, HD * (h + 1))
                oh = o_ref[rs, sl]
                gnh = gn_ref[h:h + 1, :]
                zg = zg_ref[rs, sl]
                sg = _sigmoid(zg)
                doa = do_ref[rs, sl]
                don = doa * (zg * sg)
                d_o, dgn = _rms_bwd(oh, gnh, don)
                dgn_ref[h:h + 1, 0:HD] += dgn
                dz_ref[rs, 3 * HW + HD * h:3 * HW + HD * (h + 1)] = (
                    doa * (oh * _rms(oh) * gnh) * (sg * (1.0 + zg * (1.0 - sg))))
                bh = b_scr[rs, sl]
                qh, kh, vh = q[:, sl], kk[:, sl], zi_ref[rs, sl]
                st0 = st_ref[c, h]
                ds1 = dstate[h]
                b_last = b_scr[r0 + CHUNK - 1:r0 + CHUNK, sl]
                lam = jnp.exp(bh)
                e_last = jnp.exp(b_last - bh)
                lam_last = jnp.exp(b_last)
                qcat, kecat, eqcat, ekcat = _hgrn_intra_factors(b_scr, r0, bh, qh, kh, sl)
                a = jnp.where(causal, _mm_nt(qcat, kecat), 0.0)
                da = jnp.where(causal, _mm_nt(d_o, vh), 0.0)
                dz_ref[rs, 2 * HW + HD * h:2 * HW + HD * (h + 1)] = _mm_tn(a, d_o) + _mm_nt(kh * e_last, ds1)
                q16, ke16 = qcat.astype(BF16), kecat.astype(BF16)
                gq = _mm(da, ke16)
                gk = _mm_tn(da, q16)
                dq_inter = lam * _mm(d_o, st0)
                dq = _sum_lane_blocks(eqcat * gq) + dq_inter
                dk_intra = _sum_lane_blocks(ekcat * gk)
                dk_state = _mm(vh, ds1) * e_last
                state_term = lam_last * jnp.sum(st0 * ds1, axis=0, keepdims=True)
                dstate[h] = ds1 * lam_last + _mm_tn(d_o, qh * lam)
                db_intra = _sum_lane_blocks(q16.astype(F32) * gq - ke16.astype(F32) * gk)
                dlf = (_tri_dot(upper, db_intra + qh * dq_inter, 2) + _tri_dot(strict_lower, kh * dk_state, 2)
                       + state_term)
                sigh = sig[:, sl]
                df = dlf / f[:, sl] - (dk_intra + dk_state)
                dlb_acc[:, sl] += jnp.sum(df * (1.0 - sigh), axis=0, keepdims=True)
                dz_ref[rs, HW + HD * h:HW + HD * (h + 1)] = df * (1.0 - lb[:, sl]) * sigh * (1.0 - sigh)
                sqh = sq[:, sl]
                dz_ref[rs, sl] = dq * (sqh * (1.0 + zq[:, sl] * (1.0 - sqh)))

        @pl.when(i == n_steps - 1)
        def _():
            dl0 = dlb_acc[...] * lb * (1.0 - lb)
            dlb_ref[0:1, 0:HW] = dl0
            dlb_ref[1:2, 0:HW] = -dl0

    rows = CHUNK * CHUNKS_PER_STEP
    n_steps = s // rows
    rev = lambda i: n_steps - 1 - i
    zspec = lambda cb: pl.BlockSpec((rows, HW), lambda i, cb=cb: (rev(i), cb))
    slot = jax.ShapeDtypeStruct((SLOT, D_MODEL), F32)
    return _call_behind(
        deps, body, name="hgrn_bwd", grid=(n_steps,),
        out_shape=[jax.ShapeDtypeStruct((s, IN_WIDTH), F32), slot, slot],
        in_specs=[zspec(0), zspec(1), zspec(2), zspec(3), pl.BlockSpec((rows, HW), lambda i: (rev(i), 0)),
                  pl.BlockSpec((rows, HW), lambda i: (rev(i), 0)),
                  pl.BlockSpec((CHUNKS_PER_STEP, HEADS, HD, HD), lambda i: (rev(i), 0, 0, 0)), _full((2, HW)),
                  _full((HEADS, HD)), ANY_SPACE],
        out_specs=[pl.BlockSpec((rows, 4 * HW), lambda i: (rev(i), 0)), _full((SLOT, D_MODEL)), _full((SLOT, D_MODEL))],
        scratch_shapes=[pltpu.VMEM((HEADS, HD, HD), F32), pltpu.VMEM((rows, HW), F32), pltpu.VMEM((1, HW), F32)],
        input_output_aliases={9: 0},
        compiler_params=_params(("arbitrary",)),
    )(z, z, z, z, o, dmix, states, lb_logits, gn, dz_in)


def _in_bwd(dz, w_t, x0, g, dx1, deps):
    s = x0.shape[0]
    tm = min(ROW_TILE, s)

    def body(dz_ref, w_ref, x_ref, g_ref, d_ref, dx_ref, slot_ref):
        _zero_slot(slot_ref)
        dx, dg = _rms_bwd(x_ref[...], g_ref[...], _mm(dz_ref[...], w_ref[...]))
        dx_ref[...] = d_ref[...] + dx
        slot_ref[0:1, :] += dg

    row_f32 = pl.BlockSpec((tm, D_MODEL), lambda i: (i, 0))
    return _call_behind(
        deps, body, name="in_bwd", grid=(s // tm,),
        out_shape=[jax.ShapeDtypeStruct((s, D_MODEL), F32), jax.ShapeDtypeStruct((SLOT, D_MODEL), F32)],
        in_specs=[pl.BlockSpec((tm, IN_WIDTH), lambda i: (i, 0)), VMEM_WHOLE, row_f32, _full((1, D_MODEL)), row_f32],
        out_specs=[row_f32, _full((SLOT, D_MODEL))],
        compiler_params=_params(("arbitrary",)),
    )(dz, w_t, x0, g, dx1)


def kernel(x, mem, norm_mix_g, w_in, lb_logits, hgrn_norm_g, w_pool, pool_scale, w_out, norm_x_g, norm_mem_g, w_xq, w_xk, w_xv, w_xo, norm_ffn_g, w_ff1, w_ff2, final_norm_g, loss_target, m_norm_mix_g, m_w_in, m_lb_logits, m_hgrn_norm_g, m_w_pool, m_pool_scale, m_w_out, m_norm_x_g, m_norm_mem_g, m_w_xq, m_w_xk, m_w_xv, m_w_xo, m_norm_ffn_g, m_w_ff1, m_w_ff2, m_final_norm_g, v_norm_mix_g, v_w_in, v_lb_logits, v_hgrn_norm_g, v_w_pool, v_pool_scale, v_w_out, v_norm_x_g, v_norm_mem_g, v_w_xq, v_w_xk, v_w_xv, v_w_xo, v_norm_ffn_g, v_w_ff1, v_w_ff2, v_final_norm_g):
    x0 = x[0]
    mem0 = mem[0]
    tgt = loss_target[0]
    gn = hgrn_norm_g[0]
    gfin = final_norm_g.reshape(1, D_MODEL)
    wp = w_pool[0]
    heads_2d = lambda w: w.reshape(D_MODEL // N_DEV, D_MODEL)
    xo_2d = lambda w: w.reshape(D_MODEL, D_MODEL // N_DEV)

    first = _all_gather_weights([w_in[0].T], [w_out[0], heads_2d(w_xq), heads_2d(w_xk), heads_2d(w_xv), xo_2d(w_xo).T,
                                              w_ff1[0], w_ff2[0]])
    win_t = first[0].reshape(IN_WIDTH, D_MODEL)
    ga_attn, ga_mlp = _gather_first_start([first[1:6], first[6:8]], "gather_first_start")

    z, h = _in_proj(x0, norm_mix_g, win_t, deps=[ga_attn[3]])
    mixed_a, o_pre, states = _hgrn_fwd(z, lb_logits, gn)
    lands = _split_wait(_gather_first_copies, ga_attn, o_pre, "gather_attn_first_wait")
    gb_attn = _gather_forward_start(lands, "gather_attn_forward_start")
    mixed, pooled = _pool_fwd(z, wp, pool_scale, mixed_a, deps=[gb_attn[3]])
    lands = _split_wait(_gather_forward_copies, gb_attn, pooled, "gather_attn_forward_wait")
    wout_f, wq_f, wk_f, wv_f, wo_t = (t.reshape(D_MODEL, D_MODEL) for t in lands)
    x1 = _out_proj(x0, mixed, wout_f)
    hm, xk, xv = _mem_kv(mem0, norm_mem_g, wk_f, wv_f, deps=[x1])
    lands = _split_wait(_gather_first_copies, ga_mlp, xk, "gather_mlp_first_wait")
    gb_mlp = _gather_forward_start(lands, "gather_mlp_forward_start")
    x2, hq, xq, att = _xattn_fwd(x1, norm_x_g, wq_f, xk, xv, wo_t, deps=[gb_mlp[3]])
    w1_b, w2_b = _split_wait(_gather_forward_copies, gb_mlp, x2, "gather_mlp_forward_wait")
    dx3, u, hf, slot_fin = _mlp_fwd_loss(x2, norm_ffn_g, w1_b, w2_b, gfin, tgt)

    rows = lambda t, r: t.reshape(N_DEV, r, D_MODEL)
    dw2 = _wgrad(u, dx3, "wgrad_ff2")
    ex_ff2 = _all_to_all_start([rows(dw2, FF_BLK)], [], "exchange_ff2_start")
    da, dx2, slot_ffn = _mlp_bwd(dx3, u, x2, norm_ffn_g, w1_b, w2_b, deps=[ex_ff2[3]])
    dw1 = _wgrad(hf, da, "wgrad_ff1", col_blocks=True)
    ex_ff1 = _all_to_all_start([dw1], [], "exchange_ff1_start")
    dx1, dxq, dxk, dxv, slot_x = _xattn_bwd(dx2, x1, norm_x_g, xq, xk, xv, wq_f, wo_t, deps=[ex_ff1[3]])
    dwo_t = _wgrad(dx2, att, "wgrad_xo")
    dwq = _wgrad(hq, dxq, "wgrad_xq")
    dwk, dwv, slot_mem = _mem_bwd(mem0, norm_mem_g, hm, dxk, dxv, wk_f, wv_f)
    ex_attn = _all_to_all_start([rows(dwq, 128), rows(dwk, 128), rows(dwv, 128), rows(dwo_t, 128)], [],
                                "exchange_attn_start")
    dmix = _matmul_nt(dx1, wout_f, "out_proj_bwd", deps=[ex_attn[3]])
    dwout = _wgrad(mixed, dx1, "wgrad_out")
    dz_pool, d_wpool, slot_ps = _pool_bwd(dmix, pooled, wp, pool_scale)
    small0 = jnp.concatenate([slot_x, slot_mem, slot_ffn, slot_fin, slot_ps], axis=0)
    ex_out = _all_to_all_start([rows(dwout, 128)], [small0, d_wpool], "exchange_out_start")
    dz, slot_lb, slot_gn = _hgrn_bwd(z, o_pre, dmix, states, lb_logits, gn, dz_pool, deps=[ex_out[3]])
    dwin_t = _wgrad(dz, h, "wgrad_in")
    small1 = jnp.concatenate([slot_lb, slot_gn], axis=0)
    ex_in = _all_to_all_start([rows(dwin_t, 320)], [small1], "exchange_in_start")
    grad_x, slot_mix = _in_bwd(dz, win_t, x0, norm_mix_g, dx1, deps=[ex_in[3]])
    ex_mix = _all_to_all_start([], [slot_mix], "exchange_mix_start")

    out = {}
    (r_2,) = _split_wait(_all_to_all_copies(1), ex_ff2, ex_mix[3], "exchange_ff2_wait")
    out["w_ff2"] = _sum_adamw(r_2, w_ff2[0], m_w_ff2[0], v_w_ff2[0], "adamw_ff2")
    (r_1,) = _split_wait(_all_to_all_copies(1), ex_ff1, out["w_ff2"][1], "exchange_ff1_wait")
    out["w_ff1"] = _sum_adamw(r_1, w_ff1[0], m_w_ff1[0], v_w_ff1[0], "adamw_ff1")
    r_q, r_k, r_v, r_o = _split_wait(_all_to_all_copies(4), ex_attn, out["w_ff1"][1], "exchange_attn_wait")
    for n, r, (w, m, v) in (("w_xq", r_q, (w_xq, m_w_xq, v_w_xq)), ("w_xk", r_k, (w_xk, m_w_xk, v_w_xk)),
                            ("w_xv", r_v, (w_xv, m_w_xv, v_w_xv))):
        g = _sum_sources(r, "sum_grad_" + n).reshape(w.shape)
        out[n] = (g, *_adamw_whole(g, w, m, v, "adamw_" + n))
    g_xo = _sum_sources(r_o, "sum_grad_xo").T
    out["w_xo"] = (g_xo, *_adamw(g_xo, xo_2d(w_xo), xo_2d(m_w_xo), xo_2d(v_w_xo), "adamw_xo"))
    r_out, r_small0, r_wpool = _split_wait(_all_to_all_copies(1), ex_out, out["w_xo"][1], "exchange_out_wait")
    out["w_out"] = _sum_adamw(r_out, w_out[0], m_w_out[0], v_w_out[0], "adamw_out")
    r_in, r_small1 = _split_wait(_all_to_all_copies(1), ex_in, out["w_out"][1], "exchange_in_wait")
    g_in = _sum_sources(r_in, "sum_grad_in").T
    out["w_in"] = (g_in, *_adamw(g_in, w_in[0], m_w_in[0], v_w_in[0], "adamw_in"))
    (r_small2,) = _split_wait(_all_to_all_copies(0), ex_mix, out["w_in"][1], "exchange_mix_wait")
    row = lambda t: t.reshape(1, -1)
    small_params = {
        "norm_mix_g": (norm_mix_g, m_norm_mix_g, v_norm_mix_g),
        "lb_logits": (lb_logits, m_lb_logits, v_lb_logits),
        "hgrn_norm_g": (hgrn_norm_g[0], m_hgrn_norm_g[0], v_hgrn_norm_g[0]),
        "pool_scale": (pool_scale, m_pool_scale, v_pool_scale),
        "norm_x_g": (norm_x_g, m_norm_x_g, v_norm_x_g),
        "norm_mem_g": (norm_mem_g, m_norm_mem_g, v_norm_mem_g),
        "norm_ffn_g": (norm_ffn_g, m_norm_ffn_g, v_norm_ffn_g),
        "final_norm_g": (row(final_norm_g), row(m_final_norm_g), row(v_final_norm_g)),
        "w_pool": (wp, m_w_pool[0], v_w_pool[0]),
    }
    loss, small_out = _small_update([r_small0, r_small1, r_small2], r_wpool, small_params)
    out.update(small_out)

    shapes = dict(norm_mix_g=norm_mix_g, w_in=w_in, lb_logits=lb_logits, hgrn_norm_g=hgrn_norm_g, w_pool=w_pool,
                  pool_scale=pool_scale, w_out=w_out, norm_x_g=norm_x_g, norm_mem_g=norm_mem_g, w_xq=w_xq, w_xk=w_xk,
                  w_xv=w_xv, w_xo=w_xo, norm_ffn_g=norm_ffn_g, w_ff1=w_ff1, w_ff2=w_ff2, final_norm_g=final_norm_g)
    order = list(shapes)
    group = lambda k: [out[n][k].reshape(shapes[n].shape) for n in order]
    return (loss.reshape(()), grad_x.reshape(x.shape), *group(0), *group(1), *group(2), *group(3))
```

```python
import jax
import jax.numpy as jnp
from jax import lax
from jax.experimental import pallas as pl
from jax.experimental.pallas import tpu as pltpu

F32 = jnp.float32
BF16 = jnp.bfloat16

D_MODEL = 1024
N_DEV = 8
HEADS = 4
HD = 128
HW = HEADS * HD
IN_WIDTH = 5 * HW
XHD = 256
MEM_LEN = 256
D_FF = 4096
FF_BLK = D_FF // N_DEV
POOL_WINDOWS = (2, 4, 8, 16)
POOL_HALO = 16
CHUNK = 64
CHUNKS_PER_STEP = 4
SUB = 16
N_SUB = CHUNK // SUB
EXP_CAP = 80.0
EPS = 1e-6
TINY = 1e-30
ROW_TILE = 512
SLOT = 8
V7X_VMEM_LIMIT = 56 * 1024 * 1024

ADAM_LR = 0.001
ADAM_B1 = 0.9
ADAM_B2 = 0.999
ADAM_EPS = 1e-08
ADAM_WD = 0.01
ADAM_STEP = 10

MESH_ID = pl.DeviceIdType.MESH


def _params(sem=None, vmem=V7X_VMEM_LIMIT):
    return pltpu.CompilerParams(dimension_semantics=sem, vmem_limit_bytes=vmem)


def _mm(a, b):
    return lax.dot_general(a.astype(BF16), b.astype(BF16), (((1,), (0,)), ((), ())), preferred_element_type=F32)


def _mm_nt(a, b):
    return lax.dot_general(a.astype(BF16), b.astype(BF16), (((1,), (1,)), ((), ())), preferred_element_type=F32)


def _mm_tn(a, b):
    return lax.dot_general(a.astype(BF16), b.astype(BF16), (((0,), (0,)), ((), ())), preferred_element_type=F32)


def _sigmoid(x):
    return 1.0 / (1.0 + jnp.exp(-x))


def _rms(x):
    return lax.rsqrt(jnp.mean(x * x, axis=-1, keepdims=True) + EPS)


def _rms_bwd(x, g, dh):
    r = _rms(x)
    n = x * r
    dn = dh * g
    dx = r * (dn - n * jnp.mean(dn * n, axis=-1, keepdims=True))
    return dx, jnp.sum(dh * n, axis=0, keepdims=True)


def _tri_dot(tri, x, passes):
    acc = None
    rest = x
    for _ in range(passes):
        piece = rest.astype(BF16)
        part = lax.dot_general(tri, piece, (((1,), (0,)), ((), ())), preferred_element_type=F32)
        acc = part if acc is None else acc + part
        rest = rest - piece.astype(F32)
    return acc


def _adam_update(g, w, m, v):
    nm = ADAM_B1 * m + (1.0 - ADAM_B1) * g
    nv = ADAM_B2 * v + (1.0 - ADAM_B2) * (g * g)
    m_hat = nm / (1.0 - ADAM_B1 ** ADAM_STEP)
    v_hat = nv / (1.0 - ADAM_B2 ** ADAM_STEP)
    return -ADAM_LR * (m_hat / (jnp.sqrt(v_hat) + ADAM_EPS) + ADAM_WD * w), nm, nv


def _full(shape):
    return pl.BlockSpec(shape, lambda *_: (0,) * len(shape))


VMEM_WHOLE = pl.BlockSpec(memory_space=pltpu.VMEM)
ANY_SPACE = pl.BlockSpec(memory_space=pl.ANY)


def _mesh_pos():
    return lax.axis_index("x"), lax.axis_index("y"), lax.axis_index("c")


def _flat(px, py, pc):
    return 4 * px + 2 * py + pc


def _all_gather_weights(shards, cast_only):
    n, nc = len(shards), len(cast_only)
    step = 64

    def body(*refs):
        x_refs, c_refs = refs[:n], refs[n:n + nc]
        out_refs, cast_refs = refs[n + nc:2 * n + nc], refs[2 * n + nc:2 * n + 2 * nc]
        bufs = refs[2 * n + 2 * nc:3 * n + 2 * nc]
        send_sems, recv_sems, local_sems = refs[3 * n + 2 * nc:]
        x, y, c = _mesh_pos()
        me, sibling = (x, y, c), (x, y, 1 - c)
        chips = [(1 - x, y), (x, 1 - y), (1 - x, 1 - y)]

        def copy(a, k, blk, to, src=None):
            rows = out_refs[a].at[_flat(*blk)]
            return pltpu.make_async_remote_copy(
                src_ref=rows if src is None else src, dst_ref=rows,
                send_sem=send_sems.at[7 * a + k], recv_sem=recv_sems.at[7 * a + k], device_id=to, device_id_type=MESH_ID)

        def cast_rows(src, dst, rows):
            def cast(i, carry):
                r0 = pl.multiple_of(i * step, step)
                dst[pl.ds(r0, step), :] = src[pl.ds(r0, step), :].astype(BF16)
                return carry
            lax.fori_loop(0, rows // step, cast, 0)

        first, mine = [], []
        for a in range(n):
            cast_rows(x_refs[a], bufs[a], shards[a].shape[0])
            mine.append(pltpu.make_async_copy(bufs[a], out_refs[a].at[_flat(*me)], local_sems.at[a]))
            first.append(copy(a, 0, me, sibling, src=bufs[a]))
            first += [copy(a, 1 + j, me, (*chip, c), src=bufs[a]) for j, chip in enumerate(chips)]
            for cp in [mine[-1]] + first[-4:]:
                cp.start()
        for a in range(nc):
            cast_rows(c_refs[a], cast_refs[a], cast_only[a].shape[0])
        passed = []
        for j, chip in enumerate(chips):
            for a in range(n):
                copy(a, 1 + j, (*chip, c), me).wait_recv()
                passed.append(copy(a, 4 + j, (*chip, c), sibling))
                passed[-1].start()
        for a in range(n):
            copy(a, 0, sibling, me).wait_recv()
            for j, chip in enumerate(chips):
                copy(a, 4 + j, (*chip, 1 - c), me).wait_recv()
        for cp in first + passed:
            cp.wait_send()
        for cp in mine:
            cp.wait()

    return pl.pallas_call(
        body, name="all_gather_w_in",
        out_shape=[jax.ShapeDtypeStruct((N_DEV,) + s.shape, BF16) for s in shards]
        + [jax.ShapeDtypeStruct(s.shape, BF16) for s in cast_only],
        in_specs=[VMEM_WHOLE] * (n + nc), out_specs=[ANY_SPACE] * n + [VMEM_WHOLE] * nc,
        scratch_shapes=[pltpu.VMEM(s.shape, BF16) for s in shards]
        + [pltpu.SemaphoreType.DMA((7 * n,)), pltpu.SemaphoreType.DMA((7 * n,)), pltpu.SemaphoreType.DMA((n,))],
        compiler_params=_params(),
    )(*shards, *cast_only)


HBM_SPEC = pl.BlockSpec(memory_space=pltpu.HBM)
SEM_SPEC = pl.BlockSpec(memory_space=pltpu.SEMAPHORE)
EFFECT = pltpu.SideEffectType.DATAFLOW_SIDE_EFFECTING
TOKEN = jax.ShapeDtypeStruct((8, 128), F32)


def _in_hbm(a):
    return pltpu.with_memory_space_constraint(a, pltpu.HBM)


def _split_start(copies_of, srcs, lands, n_sems, name):
    ns, nl, k = len(srcs), len(lands), len(n_sems)

    def body(*refs):
        src_refs, land_refs = refs[:ns], refs[ns:ns + nl]
        sems = refs[ns + nl:ns + nl + k]
        token = refs[-1]
        for cp in copies_of(src_refs, land_refs, sems):
            cp.start()
        token[...] = jnp.zeros_like(token)

    outs = pl.pallas_call(
        body, name=name,
        out_shape=[pltpu.SemaphoreType.DMA((q,)) for q in n_sems]
        + [pltpu.HBM(a.shape, a.dtype) for a in list(srcs) + list(lands)] + [TOKEN],
        in_specs=[HBM_SPEC] * (ns + nl),
        out_specs=[SEM_SPEC] * k + [HBM_SPEC] * (ns + nl) + [VMEM_WHOLE],
        input_output_aliases={i: k + i for i in range(ns + nl)},
        compiler_params=pltpu.CompilerParams(has_side_effects=EFFECT),
    )(*[_in_hbm(a) for a in list(srcs) + list(lands)])
    return outs[:k], outs[k:k + ns], outs[k + ns:k + ns + nl], outs[-1]


def _split_wait(copies_of, handle, after, name):
    sems, srcs, lands, _ = handle
    ns, nl, k = len(srcs), len(lands), len(sems)

    def body(*refs):
        src_refs, land_refs = refs[:ns], refs[ns:ns + nl]
        sem_refs = refs[ns + nl:ns + nl + k]
        for cp in copies_of(src_refs, land_refs, sem_refs):
            cp.wait()

    outs = pl.pallas_call(
        body, name=name,
        out_shape=[pltpu.HBM(a.shape, a.dtype) for a in list(srcs) + list(lands)],
        in_specs=[HBM_SPEC] * (ns + nl) + [SEM_SPEC] * k + [ANY_SPACE],
        out_specs=[HBM_SPEC] * (ns + nl),
        input_output_aliases={i: i for i in range(ns + nl)},
        compiler_params=pltpu.CompilerParams(has_side_effects=EFFECT),
    )(*srcs, *lands, *sems, after)
    return outs[ns:]


def _gather_first_copies(shard_refs, land_refs, sems):
    send_sems, recv_sems, local_sems = sems
    x, y, c = _mesh_pos()
    me = _flat(x, y, c)
    peers = [(x, y, 1 - c), (1 - x, y, c), (x, 1 - y, c), (1 - x, 1 - y, c)]
    copies = []
    for a, (shard, land) in enumerate(zip(shard_refs, land_refs)):
        copies.append(pltpu.make_async_copy(shard, land.at[me], local_sems.at[a]))
        for k, peer in enumerate(peers):
            copies.append(pltpu.make_async_remote_copy(
                src_ref=shard, dst_ref=land.at[me], send_sem=send_sems.at[4 * a + k], recv_sem=recv_sems.at[4 * a + k],
                device_id=peer, device_id_type=MESH_ID))
    return copies


def _gather_forward_copies(src_refs, land_refs, sems):
    del src_refs
    send_sems, recv_sems = sems
    x, y, c = _mesh_pos()
    chips = [(1 - x, y), (x, 1 - y), (1 - x, 1 - y)]
    copies = []
    for a, land in enumerate(land_refs):
        for j, chip in enumerate(chips):
            rows = land.at[_flat(*chip, c)]
            copies.append(pltpu.make_async_remote_copy(
                src_ref=rows, dst_ref=rows, send_sem=send_sems.at[3 * a + j], recv_sem=recv_sems.at[3 * a + j],
                device_id=(x, y, 1 - c), device_id_type=MESH_ID))
    return copies


def _gather_first_start(groups, name):
    shards = [s for g in groups for s in g]
    lands = [lax.empty((N_DEV,) + s.shape, s.dtype) for s in shards]
    bounds = [sum(len(g) for g in groups[:i]) for i in range(len(groups) + 1)]

    def copies_of(src_refs, land_refs, sems):
        copies = []
        for i in range(len(groups)):
            lo, hi = bounds[i], bounds[i + 1]
            copies += _gather_first_copies(src_refs[lo:hi], land_refs[lo:hi], sems[3 * i:3 * i + 3])
        return copies

    n_sems = tuple(q for g in groups for q in (4 * len(g), 4 * len(g), len(g)))
    sems, srcs, lands, token = _split_start(copies_of, shards, lands, n_sems, name)
    return [(sems[3 * i:3 * i + 3], srcs[bounds[i]:bounds[i + 1]], lands[bounds[i]:bounds[i + 1]], token)
            for i in range(len(groups))]


def _gather_forward_start(lands, name):
    n = len(lands)
    return _split_start(_gather_forward_copies, [], lands, (3 * n, 3 * n), name)


def _all_to_all_copies(n_scattered):
    def copies_of(src_refs, land_refs, sems):
        send_sems, recv_sems, local_sems = sems
        x, y, c = _mesh_pos()
        me = _flat(x, y, c)
        copies = []
        for a, (src, land) in enumerate(zip(src_refs, land_refs)):
            scattered = a < n_scattered
            copies.append(pltpu.make_async_copy(src.at[me] if scattered else src, land.at[me], local_sems.at[a]))
            for k in range(1, N_DEV):
                peer = (1 - x if k & 4 else x, 1 - y if k & 2 else y, 1 - c if k & 1 else c)
                copies.append(pltpu.make_async_remote_copy(
                    src_ref=src.at[_flat(*peer)] if scattered else src, dst_ref=land.at[me],
                    send_sem=send_sems.at[7 * a + k - 1], recv_sem=recv_sems.at[7 * a + k - 1],
                    device_id=peer, device_id_type=MESH_ID))
        return copies
    return copies_of


def _all_to_all_start(scattered, broadcast, name):
    srcs = list(scattered) + list(broadcast)
    lands = [lax.empty(a.shape, a.dtype) for a in scattered] + [lax.empty((N_DEV,) + a.shape, a.dtype) for a in broadcast]
    n = len(srcs)
    return _split_start(_all_to_all_copies(len(scattered)), srcs, lands, (7 * n, 7 * n, n), name)


def _call_behind(deps, body, *, in_specs, **kwargs):
    n_in, n_dep = len(in_specs), len(deps)

    def body_without_deps(*refs):
        return body(*refs[:n_in], *refs[n_in + n_dep:])

    call = pl.pallas_call(body_without_deps, in_specs=list(in_specs) + [ANY_SPACE] * n_dep, **kwargs)
    return lambda *operands: call(*operands, *deps)


def _row_tile(rows):
    for cand in (256, 128, 64, 32, 16):
        if rows % cand == 0:
            return cand
    return rows


def _sum_sources(recv, name):
    _, rows, cols = recv.shape
    tile = _row_tile(rows)

    def body(r_ref, o_ref):
        acc = r_ref[0].astype(F32)
        for d in range(1, N_DEV):
            acc = acc + r_ref[d].astype(F32)
        o_ref[...] = acc

    return pl.pallas_call(
        body, name=name, grid=(rows // tile,),
        out_shape=jax.ShapeDtypeStruct((rows, cols), F32),
        in_specs=[pl.BlockSpec((N_DEV, tile, cols), lambda i: (0, i, 0))],
        out_specs=pl.BlockSpec((tile, cols), lambda i: (i, 0)),
        compiler_params=_params(("parallel",)),
    )(recv)


def _adamw(g, w, m, v, name):
    rows, cols = g.shape
    tile = _row_tile(rows)

    def body(g_ref, w_ref, m_ref, v_ref, d_ref, nm_ref, nv_ref):
        d_ref[...], nm_ref[...], nv_ref[...] = _adam_update(g_ref[...], w_ref[...], m_ref[...], v_ref[...])

    spec = pl.BlockSpec((tile, cols), lambda i: (i, 0))
    shp = jax.ShapeDtypeStruct((rows, cols), F32)
    return pl.pallas_call(
        body, name=name, grid=(rows // tile,), out_shape=[shp, shp, shp],
        in_specs=[spec] * 4, out_specs=[spec] * 3,
        compiler_params=_params(("parallel",)),
    )(g, w, m, v)


def _adamw_whole(g, w, m, v, name):
    def body(g_ref, w_ref, m_ref, v_ref, d_ref, nm_ref, nv_ref):
        d_ref[...], nm_ref[...], nv_ref[...] = _adam_update(g_ref[...], w_ref[...], m_ref[...], v_ref[...])

    shp = jax.ShapeDtypeStruct(g.shape, F32)
    return pl.pallas_call(
        body, name=name, out_shape=[shp, shp, shp], in_specs=[VMEM_WHOLE] * 4, out_specs=[VMEM_WHOLE] * 3,
        compiler_params=_params(),
    )(g, w, m, v)


def _sum_adamw(recv, w, m, v, name):
    _, rows, cols = recv.shape
    tile = _row_tile(rows)

    def body(r_ref, w_ref, m_ref, v_ref, g_ref, d_ref, nm_ref, nv_ref):
        acc = r_ref[0].astype(F32)
        for d in range(1, N_DEV):
            acc = acc + r_ref[d].astype(F32)
        g_ref[...] = acc
        d_ref[...], nm_ref[...], nv_ref[...] = _adam_update(acc, w_ref[...], m_ref[...], v_ref[...])

    spec = pl.BlockSpec((tile, cols), lambda i: (i, 0))
    shp = jax.ShapeDtypeStruct((rows, cols), F32)
    return pl.pallas_call(
        body, name=name, grid=(rows // tile,), out_shape=[shp] * 4,
        in_specs=[pl.BlockSpec((N_DEV, tile, cols), lambda i: (0, i, 0)), spec, spec, spec], out_specs=[spec] * 4,
        compiler_params=_params(("parallel",)),
    )(recv, w, m, v)


SMALL_SLOTS = {"norm_x_g": (0, 0, 1, D_MODEL), "norm_mem_g": (0, 8, 1, D_MODEL), "norm_ffn_g": (0, 16, 1, D_MODEL),
               "final_norm_g": (0, 24, 1, D_MODEL), "pool_scale": (0, 32, 1, HW),
               "lb_logits": (1, 0, 2, HW), "hgrn_norm_g": (1, 8, HEADS, HD), "norm_mix_g": (2, 0, 1, D_MODEL)}
LOSS_ROW = 25
SMALL_ORDER = ("norm_mix_g", "lb_logits", "hgrn_norm_g", "pool_scale", "norm_x_g", "norm_mem_g", "norm_ffn_g",
               "final_norm_g", "w_pool")


def _small_update(srecvs, wprecv, params):
    flat = [t for n in SMALL_ORDER for t in params[n]]
    nb = len(srecvs)
    n_in = nb + 1 + len(flat)

    def body(*refs):
        s_refs, wp_ref = refs[0:nb], refs[nb]
        in_refs = refs[nb + 1:n_in]
        loss_ref = refs[n_in]
        out_refs = refs[n_in + 1:-nb]
        accs = refs[-nb:]
        for s_ref, acc in zip(s_refs, accs):
            total = s_ref[0]
            for d in range(1, N_DEV):
                total = total + s_ref[d]
            acc[...] = total
        loss_ref[...] = accs[0][LOSS_ROW:LOSS_ROW + 1, 0:1]
        for i, name in enumerate(SMALL_ORDER):
            w_ref, m_ref, v_ref = in_refs[3 * i:3 * i + 3]
            g_ref, d_ref, nm_ref, nv_ref = out_refs[4 * i:4 * i + 4]
            if name == "w_pool":
                g = wp_ref[0]
                for d in range(1, N_DEV):
                    g = g + wp_ref[d]
            else:
                buf, r0, nr, nc = SMALL_SLOTS[name]
                g = accs[buf][r0:r0 + nr, 0:nc]
            g_ref[...] = g
            d_ref[...], nm_ref[...], nv_ref[...] = _adam_update(g, w_ref[...], m_ref[...], v_ref[...])

    out_shape = [jax.ShapeDtypeStruct((1, 1), F32)]
    for n in SMALL_ORDER:
        out_shape += [jax.ShapeDtypeStruct(params[n][0].shape, F32)] * 4
    outs = pl.pallas_call(
        body, name="small_update", out_shape=out_shape,
        in_specs=[VMEM_WHOLE] * n_in, out_specs=[VMEM_WHOLE] * len(out_shape),
        scratch_shapes=[pltpu.VMEM(r.shape[1:], F32) for r in srecvs],
        compiler_params=_params(),
    )(*srecvs, wprecv, *flat)
    return outs[0], {n: outs[1 + 4 * i:5 + 4 * i] for i, n in enumerate(SMALL_ORDER)}


def _in_proj(x, g, w_t, deps):
    s = x.shape[0]
    tm = min(ROW_TILE, s)

    def body(x_ref, g_ref, w_ref, z_ref, h_ref):
        xv = x_ref[...]
        h = (xv * _rms(xv) * g_ref[...]).astype(BF16)
        h_ref[...] = h
        z_ref[...] = _mm_nt(h, w_ref[...])

    return _call_behind(
        deps, body, name="in_proj", grid=(s // tm,),
        out_shape=[jax.ShapeDtypeStruct((s, IN_WIDTH), F32), jax.ShapeDtypeStruct((s, D_MODEL), BF16)],
        in_specs=[pl.BlockSpec((tm, D_MODEL), lambda i: (i, 0)), _full((1, D_MODEL)), VMEM_WHOLE],
        out_specs=[pl.BlockSpec((tm, IN_WIDTH), lambda i: (i, 0)), pl.BlockSpec((tm, D_MODEL), lambda i: (i, 0))],
        compiler_params=_params(("parallel",)),
    )(x, g, w_t)


def _chunk_masks():
    row = lax.broadcasted_iota(jnp.int32, (CHUNK, CHUNK), 0)
    col = lax.broadcasted_iota(jnp.int32, (CHUNK, CHUNK), 1)
    return row, col


def _ones_where(mask):
    return jnp.where(mask, 1.0, 0.0).astype(BF16)


def _hgrn_gates(zq, zf, lb):
    sq = _sigmoid(zq)
    sig = _sigmoid(zf)
    f = lb + (1.0 - lb) * sig
    return zq * sq, sq, sig, f


def _hgrn_intra_factors(b_scr, r0, bh, qh, kh, sl):
    trow = lax.broadcasted_iota(jnp.int32, (CHUNK, HD), 0)
    eq, ek = [], []
    for j in range(N_SUB):
        if j == 0:
            base = jnp.zeros((1, HD), F32)
        else:
            base = b_scr[r0 + SUB * j - 1:r0 + SUB * j, sl]
        in_j = (trow >= SUB * j) & (trow < SUB * (j + 1))
        eq.append(jnp.where(in_j, jnp.exp(bh - base), 0.0))
        ek.append(jnp.where(trow < SUB * (j + 1), jnp.exp(jnp.minimum(base - bh, EXP_CAP)), 0.0))
    eqcat = jnp.concatenate(eq, axis=1)
    ekcat = jnp.concatenate(ek, axis=1)
    qcat = jnp.concatenate([qh] * N_SUB, axis=1) * eqcat
    kecat = jnp.concatenate([kh] * N_SUB, axis=1) * ekcat
    return qcat, kecat, eqcat, ekcat


def _sum_lane_blocks(a):
    out = a[:, 0:HD]
    for j in range(1, N_SUB):
        out = out + a[:, HD * j:HD * (j + 1)]
    return out


def _hgrn_fwd(z, lb_logits, gn):
    s = z.shape[0]
    n_chunks = s // CHUNK

    def body(zq_ref, zf_ref, zi_ref, zg_ref, lbl_ref, gn_ref, oa_ref, o_ref, st_ref, state, b_scr):
        @pl.when(pl.program_id(0) == 0)
        def _():
            state[...] = jnp.zeros_like(state)

        lb = _sigmoid(lbl_ref[0:1, :] - lbl_ref[1:2, :])
        row, col = _chunk_masks()
        causal = col <= row
        tri = _ones_where(causal)
        for c in range(CHUNKS_PER_STEP):
            r0 = CHUNK * c
            rs = slice(r0, r0 + CHUNK)
            st_ref[c] = state[...]
            q, _, _, f = _hgrn_gates(zq_ref[rs, :], zf_ref[rs, :], lb)
            kk = 1.0 - f
            b_scr[rs, :] = _tri_dot(tri, jnp.log(f), 3)
            for h in range(HEADS):
                sl = slice(HD * h, HD * (h + 1))
                bh = b_scr[rs, sl]
                qh, kh, vh = q[:, sl], kk[:, sl], zi_ref[rs, sl]
                st = state[h]
                b_last = b_scr[r0 + CHUNK - 1:r0 + CHUNK, sl]
                qcat, kecat, _, _ = _hgrn_intra_factors(b_scr, r0, bh, qh, kh, sl)
                a = jnp.where(causal, _mm_nt(qcat, kecat), 0.0)
                o = _mm(a, vh) + _mm_nt(qh * jnp.exp(bh), st)
                state[h] = st * jnp.exp(b_last) + _mm_tn(vh, kh * jnp.exp(b_last - bh))
                o_ref[rs, sl] = o
                zg = zg_ref[rs, sl]
                oa_ref[rs, sl] = (o * _rms(o) * gn_ref[h:h + 1, :] * zg * _sigmoid(zg)).astype(BF16)

    rows = CHUNK * CHUNKS_PER_STEP
    zspec = lambda cb: pl.BlockSpec((rows, HW), lambda i, cb=cb: (i, cb))
    return pl.pallas_call(
        body, name="hgrn_fwd", grid=(s // rows,),
        out_shape=[jax.ShapeDtypeStruct((s, 2 * HW), BF16), jax.ShapeDtypeStruct((s, HW), F32),
                   jax.ShapeDtypeStruct((n_chunks, HEADS, HD, HD), F32)],
        in_specs=[zspec(0), zspec(1), zspec(2), zspec(3), _full((2, HW)), _full((HEADS, HD))],
        out_specs=[pl.BlockSpec((rows, HW), lambda i: (i, 0)), pl.BlockSpec((rows, HW), lambda i: (i, 0)),
                   pl.BlockSpec((CHUNKS_PER_STEP, HEADS, HD, HD), lambda i: (i, 0, 0, 0))],
        scratch_shapes=[pltpu.VMEM((HEADS, HD, HD), F32), pltpu.VMEM((rows, HW), F32)],
        compiler_params=_params(("arbitrary",)),
    )(z, z, z, z, lb_logits, gn)


def _pool_counts(tile_idx, tm):
    t = tile_idx * tm + lax.broadcasted_iota(jnp.int32, (tm, 1), 0)
    return [1.0 / jnp.minimum(t + 1, w).astype(F32) for w in POOL_WINDOWS]


def _pool_fwd(z, w_pool, scale, mixed_in, deps):
    s = z.shape[0]
    tm = min(ROW_TILE, s)

    def body(p_ref, w_ref, sc_ref, mixin_ref, ob_ref, pooled_ref, ext):
        i = pl.program_id(0)

        @pl.when(i == 0)
        def _():
            ext[0:POOL_HALO, :] = jnp.zeros((POOL_HALO, HW), F32)

        @pl.when(i > 0)
        def _():
            ext[0:POOL_HALO, :] = ext[tm:tm + POOL_HALO, :]

        ext[POOL_HALO:POOL_HALO + tm, :] = p_ref[...]
        inv = _pool_counts(i, tm)
        for g, w in enumerate(POOL_WINDOWS):
            sl = slice(HD * g, HD * (g + 1))
            p = ext[POOL_HALO:POOL_HALO + tm, sl]
            win = p
            for d in range(1, w):
                win = win + ext[POOL_HALO - d:POOL_HALO - d + tm, sl]
            pooled = (win * inv[g] - p).astype(BF16)
            pooled_ref[:, sl] = pooled
            ob_ref[:, sl] = (_mm(pooled, w_ref[g]) * sc_ref[:, sl]).astype(BF16)

    return _call_behind(
        deps, body, name="pool_fwd", grid=(s // tm,),
        out_shape=[jax.ShapeDtypeStruct((s, 2 * HW), BF16), jax.ShapeDtypeStruct((s, HW), BF16)],
        in_specs=[pl.BlockSpec((tm, HW), lambda i: (i, 4)), _full((HEADS, HD, HD)), _full((1, HW)), ANY_SPACE],
        out_specs=[pl.BlockSpec((tm, HW), lambda i: (i, 1)), pl.BlockSpec((tm, HW), lambda i: (i, 0))],
        scratch_shapes=[pltpu.VMEM((tm + POOL_HALO, HW), F32)],
        input_output_aliases={3: 0},
        compiler_params=_params(("arbitrary",)),
    )(z, w_pool, scale, mixed_in)


def _out_proj(x, mixed, w_out):
    s = x.shape[0]
    tm = min(ROW_TILE, s)

    def body(x_ref, a_ref, w_ref, o_ref):
        o_ref[...] = x_ref[...] + _mm(a_ref[...], w_ref[...])

    return pl.pallas_call(
        body, name="out_proj", grid=(s // tm,),
        out_shape=jax.ShapeDtypeStruct((s, D_MODEL), F32),
        in_specs=[pl.BlockSpec((tm, D_MODEL), lambda i: (i, 0)), pl.BlockSpec((tm, D_MODEL), lambda i: (i, 0)), VMEM_WHOLE],
        out_specs=pl.BlockSpec((tm, D_MODEL), lambda i: (i, 0)),
        compiler_params=_params(("parallel",)),
    )(x, mixed, w_out)


def _mem_kv(mem, g, wk, wv, deps):
    def body(m_ref, g_ref, wk_ref, wv_ref, hm_ref, k_ref, v_ref):
        m = m_ref[...]
        hm = (m * _rms(m) * g_ref[...]).astype(BF16)
        hm_ref[...] = hm
        k_ref[...] = _mm(hm, wk_ref[...]).astype(BF16)
        v_ref[...] = _mm(hm, wv_ref[...]).astype(BF16)

    shp = jax.ShapeDtypeStruct((MEM_LEN, D_MODEL), BF16)
    return _call_behind(
        deps, body, name="mem_kv", out_shape=[shp, shp, shp],
        in_specs=[VMEM_WHOLE] * 4, out_specs=[VMEM_WHOLE] * 3,
        compiler_params=_params(),
    )(mem, g, wk, wv)


def _softmax_rows(sc):
    e = jnp.exp(sc - jnp.max(sc, axis=-1, keepdims=True))
    return e / jnp.sum(e, axis=-1, keepdims=True)


def _xattn_fwd(x, g, wq, xk, xv, wo_t, deps):
    s = x.shape[0]
    tm = min(ROW_TILE, s)
    scale = XHD ** -0.5

    def body(x_ref, g_ref, wq_ref, k_ref, v_ref, wo_ref, o_ref, hq_ref, q_ref, att_ref):
        xv_ = x_ref[...]
        hq = (xv_ * _rms(xv_) * g_ref[...]).astype(BF16)
        hq_ref[...] = hq
        q_ref[...] = (_mm(hq, wq_ref[...]) * scale).astype(BF16)
        for h in range(HEADS):
            sl = slice(XHD * h, XHD * (h + 1))
            p = _softmax_rows(_mm_nt(q_ref[:, sl], k_ref[:, sl]))
            att_ref[:, sl] = _mm(p, v_ref[:, sl]).astype(BF16)
        o_ref[...] = xv_ + _mm_nt(att_ref[...], wo_ref[...])

    row_f32 = pl.BlockSpec((tm, D_MODEL), lambda i: (i, 0))
    bshape = jax.ShapeDtypeStruct((s, D_MODEL), BF16)
    return _call_behind(
        deps, body, name="xattn_fwd", grid=(s // tm,),
        out_shape=[jax.ShapeDtypeStruct((s, D_MODEL), F32), bshape, bshape, bshape],
        in_specs=[row_f32, _full((1, D_MODEL)), VMEM_WHOLE, VMEM_WHOLE, VMEM_WHOLE, VMEM_WHOLE],
        out_specs=[row_f32] * 4,
        compiler_params=_params(("parallel",)),
    )(x, g, wq, xk, xv, wo_t)


def _mlp_fwd_loss(x, g, w1, w2, gf, target):
    s = x.shape[0]
    tm = min(ROW_TILE, s)

    def body(x_ref, g_ref, w1_ref, w2_ref, gf_ref, t_ref, dx_ref, u_ref, hf_ref, slot_ref):
        @pl.when(pl.program_id(0) == 0)
        def _():
            slot_ref[...] = jnp.zeros_like(slot_ref)

        xv = x_ref[...]
        hf = (xv * _rms(xv) * g_ref[...]).astype(BF16)
        hf_ref[...] = hf
        acc = xv
        for j in range(N_DEV):
            a = jnp.maximum(_mm(hf, w1_ref[j]), 0.0)
            u = (a * a).astype(BF16)
            u_ref[:, FF_BLK * j:FF_BLK * (j + 1)] = u
            acc = acc + _mm(u, w2_ref[j])
        gfv = gf_ref[...]
        r = _rms(acc)
        n = acc * r
        err = n * gfv - t_ref[...]
        slot_ref[1:2, :] += jnp.sum(jnp.mean(err * err, axis=-1, keepdims=True), axis=0, keepdims=True) * 0.5
        dy = err * (1.0 / D_MODEL)
        slot_ref[0:1, :] += jnp.sum(dy * n, axis=0, keepdims=True)
        dn = dy * gfv
        dx_ref[...] = r * (dn - n * jnp.mean(dn * n, axis=-1, keepdims=True))

    row_f32 = pl.BlockSpec((tm, D_MODEL), lambda i: (i, 0))
    return pl.pallas_call(
        body, name="mlp_fwd_loss", grid=(s // tm,),
        out_shape=[jax.ShapeDtypeStruct((s, D_MODEL), F32), jax.ShapeDtypeStruct((s, D_FF), BF16),
                   jax.ShapeDtypeStruct((s, D_MODEL), BF16), jax.ShapeDtypeStruct((SLOT, D_MODEL), F32)],
        in_specs=[row_f32, _full((1, D_MODEL)), VMEM_WHOLE, VMEM_WHOLE, _full((1, D_MODEL)), row_f32],
        out_specs=[row_f32, pl.BlockSpec((tm, D_FF), lambda i: (i, 0)), row_f32, _full((SLOT, D_MODEL))],
        compiler_params=_params(("arbitrary",)),
    )(x, g, w1, w2, gf, target)


def _zero_slot(slot_ref):
    @pl.when(pl.program_id(0) == 0)
    def _():
        slot_ref[...] = jnp.zeros_like(slot_ref)


def _mlp_bwd(dx3, u, x2, g, w1, w2, deps):
    s = x2.shape[0]
    tm = min(ROW_TILE, s)

    def body(d_ref, u_ref, x_ref, g_ref, w1_ref, w2_ref, da_ref, dx_ref, slot_ref):
        _zero_slot(slot_ref)
        d = d_ref[...]
        d16 = d.astype(BF16)
        dhf = jnp.zeros((tm, D_MODEL), F32)
        for j in range(N_DEV):
            sl = slice(FF_BLK * j, FF_BLK * (j + 1))
            u = u_ref[:, sl].astype(F32)
            da = (_mm_nt(d16, w2_ref[j]) * (2.0 * u * lax.rsqrt(jnp.maximum(u, TINY)))).astype(BF16)
            da_ref[:, sl] = da
            dhf = dhf + _mm_nt(da, w1_ref[j])
        dx, dg = _rms_bwd(x_ref[...], g_ref[...], dhf)
        dx_ref[...] = d + dx
        slot_ref[0:1, :] += dg

    row_f32 = pl.BlockSpec((tm, D_MODEL), lambda i: (i, 0))
    return _call_behind(
        deps, body, name="mlp_bwd", grid=(s // tm,),
        out_shape=[jax.ShapeDtypeStruct((s, D_FF), BF16), jax.ShapeDtypeStruct((s, D_MODEL), F32),
                   jax.ShapeDtypeStruct((SLOT, D_MODEL), F32)],
        in_specs=[row_f32, pl.BlockSpec((tm, D_FF), lambda i: (i, 0)), row_f32, _full((1, D_MODEL)),
                  VMEM_WHOLE, VMEM_WHOLE],
        out_specs=[pl.BlockSpec((tm, D_FF), lambda i: (i, 0)), row_f32, _full((SLOT, D_MODEL))],
        compiler_params=_params(("arbitrary",)),
    )(dx3, u, x2, g, w1, w2)


def _wgrad(a, b, name, col_blocks=False):
    s, m = a.shape
    n = b.shape[1]
    tm = 1280 if m % 1280 == 0 else min(1024, m)
    tn = min(1024, n)
    blk = n // N_DEV
    per_step = tn // blk if col_blocks else 1
    ts = min(2 * ROW_TILE, s)
    n_s = s // ts

    def body(a_ref, b_ref, o_ref, acc):
        k = pl.program_id(2)

        @pl.when(k == 0)
        def _():
            acc[...] = jnp.zeros_like(acc)

        acc[...] += _mm_tn(a_ref[...], b_ref[...])

        @pl.when(k == n_s - 1)
        def _():
            if col_blocks:
                for p in range(per_step):
                    o_ref[p] = acc[:, blk * p:blk * (p + 1)].astype(BF16)
            else:
                o_ref[...] = acc[...].astype(BF16)

    if col_blocks:
        out_shape = jax.ShapeDtypeStruct((N_DEV, m, blk), BF16)
        out_spec = pl.BlockSpec((per_step, tm, blk), lambda i, j, k: (j, i, 0))
    else:
        out_shape = jax.ShapeDtypeStruct((m, n), BF16)
        out_spec = pl.BlockSpec((tm, tn), lambda i, j, k: (i, j))
    return pl.pallas_call(
        body, name=name, grid=(m // tm, n // tn, n_s), out_shape=out_shape,
        in_specs=[pl.BlockSpec((ts, tm), lambda i, j, k: (k, i)), pl.BlockSpec((ts, tn), lambda i, j, k: (k, j))],
        out_specs=out_spec,
        scratch_shapes=[pltpu.VMEM((tm, tn), F32)],
        compiler_params=_params(("parallel", "parallel", "arbitrary")),
    )(a, b)


def _xattn_bwd(dx2, x1, g, q, xk, xv, wq, wo_t, deps):
    s = x1.shape[0]
    tm = min(ROW_TILE, s)
    scale = XHD ** -0.5

    def body(d_ref, x_ref, g_ref, q_ref, k_ref, v_ref, wq_ref, wo_ref, dx_ref, dq_ref, dk_ref, dv_ref, slot_ref, datt):
        _zero_slot(slot_ref)

        @pl.when(pl.program_id(0) == 0)
        def _():
            dk_ref[...] = jnp.zeros_like(dk_ref)
            dv_ref[...] = jnp.zeros_like(dv_ref)

        d = d_ref[...]
        datt[...] = _mm(d, wo_ref[...]).astype(BF16)
        for h in range(HEADS):
            sl = slice(XHD * h, XHD * (h + 1))
            qh, kh, vh, dah = q_ref[:, sl], k_ref[:, sl], v_ref[:, sl], datt[:, sl]
            p = _softmax_rows(_mm_nt(qh, kh))
            dp = _mm_nt(dah, vh)
            ds = (p * (dp - jnp.sum(dp * p, axis=-1, keepdims=True))).astype(BF16)
            dq_ref[:, sl] = (_mm(ds, kh) * scale).astype(BF16)
            dk_ref[:, sl] += _mm_tn(ds, qh)
            dv_ref[:, sl] += _mm_tn(p, dah)
        dx, dg = _rms_bwd(x_ref[...], g_ref[...], _mm_nt(dq_ref[...], wq_ref[...]))
        dx_ref[...] = d + dx
        slot_ref[0:1, :] += dg

    row_f32 = pl.BlockSpec((tm, D_MODEL), lambda i: (i, 0))
    kv = jax.ShapeDtypeStruct((MEM_LEN, D_MODEL), F32)
    return _call_behind(
        deps, body, name="xattn_bwd", grid=(s // tm,),
        out_shape=[jax.ShapeDtypeStruct((s, D_MODEL), F32), jax.ShapeDtypeStruct((s, D_MODEL), BF16), kv, kv,
                   jax.ShapeDtypeStruct((SLOT, D_MODEL), F32)],
        in_specs=[row_f32, row_f32, _full((1, D_MODEL)), row_f32, VMEM_WHOLE, VMEM_WHOLE, VMEM_WHOLE, VMEM_WHOLE],
        out_specs=[row_f32, row_f32, _full((MEM_LEN, D_MODEL)), _full((MEM_LEN, D_MODEL)), _full((SLOT, D_MODEL))],
        scratch_shapes=[pltpu.VMEM((tm, D_MODEL), BF16)],
        compiler_params=_params(("arbitrary",)),
    )(dx2, x1, g, q, xk, xv, wq, wo_t)


def _mem_bwd(mem, g, hm, dxk, dxv, wk, wv):
    def body(m_ref, g_ref, hm_ref, dk_ref, dv_ref, wk_ref, wv_ref, dwk_ref, dwv_ref, slot_ref):
        dk, dv = dk_ref[...], dv_ref[...]
        hm_ = hm_ref[...]
        dwk_ref[...] = _mm_tn(hm_, dk).astype(BF16)
        dwv_ref[...] = _mm_tn(hm_, dv).astype(BF16)
        _, dg = _rms_bwd(m_ref[...], g_ref[...], _mm_nt(dk, wk_ref[...]) + _mm_nt(dv, wv_ref[...]))
        slot_ref[...] = jnp.zeros_like(slot_ref)
        slot_ref[0:1, :] = dg

    wshape = jax.ShapeDtypeStruct((D_MODEL, D_MODEL), BF16)
    return pl.pallas_call(
        body, name="mem_bwd", out_shape=[wshape, wshape, jax.ShapeDtypeStruct((SLOT, D_MODEL), F32)],
        in_specs=[VMEM_WHOLE] * 7, out_specs=[VMEM_WHOLE] * 3,
        compiler_params=_params(),
    )(mem, g, hm, dxk, dxv, wk, wv)


def _pool_bwd(dx1, w_out, pooled, w_pool, scale, deps):
    s = dx1.shape[0]
    tm = min(ROW_TILE, s)
    n_t = s // tm

    def body(dx_ref, wo_ref, pl_ref, w_ref, sc_ref, dz_ref, dw_ref, slot_ref, ext, do_ref):
        i = pl.program_id(0)
        tile = n_t - 1 - i
        _zero_slot(slot_ref)
        do_ref[...] = _mm_nt(dx_ref[...], wo_ref[HW:2 * HW, :])

        @pl.when(i == 0)
        def _():
            dw_ref[...] = jnp.zeros_like(dw_ref)
            ext[tm:tm + POOL_HALO, :] = jnp.zeros((POOL_HALO, HW), F32)

        @pl.when(i > 0)
        def _():
            ext[tm:tm + POOL_HALO, :] = ext[0:POOL_HALO, :]

        inv = _pool_counts(tile, tm)
        dpooled = []
        for g in range(HEADS):
            sl = slice(HD * g, HD * (g + 1))
            pooled_g = pl_ref[:, sl]
            do = do_ref[:, sl]
            slot_ref[0:1, sl] += jnp.sum(_mm(pooled_g, w_ref[g]) * do, axis=0, keepdims=True)
            dy = (do * sc_ref[:, sl]).astype(BF16)
            dw_ref[g] += _mm_tn(pooled_g, dy)
            dpo = _mm_nt(dy, w_ref[g])
            dpooled.append(dpo)
            ext[0:tm, sl] = dpo * inv[g]
        for g, w in enumerate(POOL_WINDOWS):
            sl = slice(HD * g, HD * (g + 1))
            win = ext[0:tm, sl]
            for d in range(1, w):
                win = win + ext[d:d + tm, sl]
            dz_ref[:, sl] = win - dpooled[g]

    return _call_behind(
        deps, body, name="pool_bwd", grid=(n_t,),
        out_shape=[jax.ShapeDtypeStruct((s, IN_WIDTH), F32), jax.ShapeDtypeStruct((HEADS, HD, HD), F32),
                   jax.ShapeDtypeStruct((SLOT, D_MODEL), F32)],
        in_specs=[pl.BlockSpec((tm, D_MODEL), lambda i: (n_t - 1 - i, 0)), VMEM_WHOLE,
                  pl.BlockSpec((tm, HW), lambda i: (n_t - 1 - i, 0)), _full((HEADS, HD, HD)), _full((1, HW))],
        out_specs=[pl.BlockSpec((tm, HW), lambda i: (n_t - 1 - i, 4)), _full((HEADS, HD, HD)), _full((SLOT, D_MODEL))],
        scratch_shapes=[pltpu.VMEM((tm + POOL_HALO, HW), F32), pltpu.VMEM((tm, HW), F32)],
        compiler_params=_params(("arbitrary",)),
    )(dx1, w_out, pooled, w_pool, scale)


def _hgrn_bwd(z, o, dx1, w_out, states, lb_logits, gn, dz_in, deps):
    s = z.shape[0]
    n_chunks = s // CHUNK

    def body(zq_ref, zf_ref, zi_ref, zg_ref, o_ref, dx_ref, wo_ref, st_ref, lbl_ref, gn_ref, dzin_ref,
             dz_ref, dlb_ref, dgn_ref, dstate, b_scr, dlb_acc, do_ref):
        i = pl.program_id(0)

        @pl.when(i == 0)
        def _():
            dstate[...] = jnp.zeros_like(dstate)
            dlb_acc[...] = jnp.zeros_like(dlb_acc)
            dgn_ref[...] = jnp.zeros_like(dgn_ref)
            dlb_ref[...] = jnp.zeros_like(dlb_ref)

        do_ref[...] = _mm_nt(dx_ref[...], wo_ref[0:HW, :])
        lb = _sigmoid(lbl_ref[0:1, :] - lbl_ref[1:2, :])
        row, col = _chunk_masks()
        causal = col <= row
        tri = _ones_where(causal)
        upper = _ones_where(col >= row)
        strict_lower = _ones_where(col < row)
        for c in reversed(range(CHUNKS_PER_STEP)):
            r0 = CHUNK * c
            rs = slice(r0, r0 + CHUNK)
            zq = zq_ref[rs, :]
            q, sq, sig, f = _hgrn_gates(zq, zf_ref[rs, :], lb)
            kk = 1.0 - f
            b_scr[rs, :] = _tri_dot(tri, jnp.log(f), 3)
            for h in range(HEADS):
                sl = slice(HD * h, HD * (h + 1))
                oh = o_ref[rs, sl]
                gnh = gn_ref[h:h + 1, :]
                zg = zg_ref[rs, sl]
                sg = _sigmoid(zg)
                doa = do_ref[rs, sl]
                don = doa * (zg * sg)
                d_o, dgn = _rms_bwd(oh, gnh, don)
                dgn_ref[h:h + 1, 0:HD] += dgn
                dz_ref[rs, 3 * HW + HD * h:3 * HW + HD * (h + 1)] = (
                    doa * (oh * _rms(oh) * gnh) * (sg * (1.0 + zg * (1.0 - sg))))
                bh = b_scr[rs, sl]
                qh, kh, vh = q[:, sl], kk[:, sl], zi_ref[rs, sl]
                st0 = st_ref[c, h]
                ds1 = dstate[h]
                b_last = b_scr[r0 + CHUNK - 1:r0 + CHUNK, sl]
                lam = jnp.exp(bh)
                e_last = jnp.exp(b_last - bh)
                lam_last = jnp.exp(b_last)
                qcat, kecat, eqcat, ekcat = _hgrn_intra_factors(b_scr, r0, bh, qh, kh, sl)
                a = jnp.where(causal, _mm_nt(qcat, kecat), 0.0)
                da = jnp.where(causal, _mm_nt(d_o, vh), 0.0)
                dz_ref[rs, 2 * HW + HD * h:2 * HW + HD * (h + 1)] = _mm_tn(a, d_o) + _mm_nt(kh * e_last, ds1)
                q16, ke16 = qcat.astype(BF16), kecat.astype(BF16)
                gq = _mm(da, ke16)
                gk = _mm_tn(da, q16)
                dq_inter = lam * _mm(d_o, st0)
                dq = _sum_lane_blocks(eqcat * gq) + dq_inter
                dk_intra = _sum_lane_blocks(ekcat * gk)
                dk_state = _mm(vh, ds1) * e_last
                state_term = lam_last * jnp.sum(st0 * ds1, axis=0, keepdims=True)
                dstate[h] = ds1 * lam_last + _mm_tn(d_o, qh * lam)
                db_intra = _sum_lane_blocks(q16.astype(F32) * gq - ke16.astype(F32) * gk)
                dlf = (_tri_dot(upper, db_intra + qh * dq_inter, 2) + _tri_dot(strict_lower, kh * dk_state, 2)
                       + state_term)
                sigh = sig[:, sl]
                df = dlf / f[:, sl] - (dk_intra + dk_state)
                dlb_acc[:, sl] += jnp.sum(df * (1.0 - sigh), axis=0, keepdims=True)
                dz_ref[rs, HW + HD * h:HW + HD * (h + 1)] = df * (1.0 - lb[:, sl]) * sigh * (1.0 - sigh)
                sqh = sq[:, sl]
                dz_ref[rs, sl] = dq * (sqh * (1.0 + zq[:, sl] * (1.0 - sqh)))

        @pl.when(i == n_steps - 1)
        def _():
            dl0 = dlb_acc[...] * lb * (1.0 - lb)
            dlb_ref[0:1, 0:HW] = dl0
            dlb_ref[1:2, 0:HW] = -dl0

    rows = CHUNK * CHUNKS_PER_STEP
    n_steps = s // rows
    rev = lambda i: n_steps - 1 - i
    zspec = lambda cb: pl.BlockSpec((rows, HW), lambda i, cb=cb: (rev(i), cb))
    slot = jax.ShapeDtypeStruct((SLOT, D_MODEL), F32)
    return _call_behind(
        deps, body, name="hgrn_bwd", grid=(n_steps,),
        out_shape=[jax.ShapeDtypeStruct((s, IN_WIDTH), F32), slot, slot],
        in_specs=[zspec(0), zspec(1), zspec(2), zspec(3), pl.BlockSpec((rows, HW), lambda i: (rev(i), 0)),
                  pl.BlockSpec((rows, D_MODEL), lambda i: (rev(i), 0)), VMEM_WHOLE,
                  pl.BlockSpec((CHUNKS_PER_STEP, HEADS, HD, HD), lambda i: (rev(i), 0, 0, 0)), _full((2, HW)),
                  _full((HEADS, HD)), ANY_SPACE],
        out_specs=[pl.BlockSpec((rows, 4 * HW), lambda i: (rev(i), 0)), _full((SLOT, D_MODEL)), _full((SLOT, D_MODEL))],
        scratch_shapes=[pltpu.VMEM((HEADS, HD, HD), F32), pltpu.VMEM((rows, HW), F32), pltpu.VMEM((1, HW), F32),
                        pltpu.VMEM((rows, HW), F32)],
        input_output_aliases={10: 0},
        compiler_params=_params(("arbitrary",)),
    )(z, z, z, z, o, dx1, w_out, states, lb_logits, gn, dz_in)


def _in_bwd(dz, w_t, x0, g, dx1, deps):
    s = x0.shape[0]
    tm = min(ROW_TILE, s)

    def body(dz_ref, w_ref, x_ref, g_ref, d_ref, dx_ref, slot_ref):
        _zero_slot(slot_ref)
        dx, dg = _rms_bwd(x_ref[...], g_ref[...], _mm(dz_ref[...], w_ref[...]))
        dx_ref[...] = d_ref[...] + dx
        slot_ref[0:1, :] += dg

    row_f32 = pl.BlockSpec((tm, D_MODEL), lambda i: (i, 0))
    return _call_behind(
        deps, body, name="in_bwd", grid=(s // tm,),
        out_shape=[jax.ShapeDtypeStruct((s, D_MODEL), F32), jax.ShapeDtypeStruct((SLOT, D_MODEL), F32)],
        in_specs=[pl.BlockSpec((tm, IN_WIDTH), lambda i: (i, 0)), VMEM_WHOLE, row_f32, _full((1, D_MODEL)), row_f32],
        out_specs=[row_f32, _full((SLOT, D_MODEL))],
        compiler_params=_params(("arbitrary",)),
    )(dz, w_t, x0, g, dx1)


def kernel(x, mem, norm_mix_g, w_in, lb_logits, hgrn_norm_g, w_pool, pool_scale, w_out, norm_x_g, norm_mem_g, w_xq, w_xk, w_xv, w_xo, norm_ffn_g, w_ff1, w_ff2, final_norm_g, loss_target, m_norm_mix_g, m_w_in, m_lb_logits, m_hgrn_norm_g, m_w_pool, m_pool_scale, m_w_out, m_norm_x_g, m_norm_mem_g, m_w_xq, m_w_xk, m_w_xv, m_w_xo, m_norm_ffn_g, m_w_ff1, m_w_ff2, m_final_norm_g, v_norm_mix_g, v_w_in, v_lb_logits, v_hgrn_norm_g, v_w_pool, v_pool_scale, v_w_out, v_norm_x_g, v_norm_mem_g, v_w_xq, v_w_xk, v_w_xv, v_w_xo, v_norm_ffn_g, v_w_ff1, v_w_ff2, v_final_norm_g):
    x0 = x[0]
    mem0 = mem[0]
    tgt = loss_target[0]
    gn = hgrn_norm_g[0]
    gfin = final_norm_g.reshape(1, D_MODEL)
    wp = w_pool[0]
    heads_2d = lambda w: w.reshape(D_MODEL // N_DEV, D_MODEL)
    xo_2d = lambda w: w.reshape(D_MODEL, D_MODEL // N_DEV)

    first = _all_gather_weights([w_in[0].T], [w_out[0], heads_2d(w_xq), heads_2d(w_xk), heads_2d(w_xv), xo_2d(w_xo).T,
                                              w_ff1[0], w_ff2[0]])
    win_t = first[0].reshape(IN_WIDTH, D_MODEL)
    ga_attn, ga_mlp = _gather_first_start([first[1:6], first[6:8]], "gather_first_start")

    z, h = _in_proj(x0, norm_mix_g, win_t, deps=[ga_attn[3]])
    mixed_a, o_pre, states = _hgrn_fwd(z, lb_logits, gn)
    lands = _split_wait(_gather_first_copies, ga_attn, o_pre, "gather_attn_first_wait")
    gb_attn = _gather_forward_start(lands, "gather_attn_forward_start")
    mixed, pooled = _pool_fwd(z, wp, pool_scale, mixed_a, deps=[gb_attn[3]])
    lands = _split_wait(_gather_forward_copies, gb_attn, pooled, "gather_attn_forward_wait")
    wout_f, wq_f, wk_f, wv_f, wo_t = (t.reshape(D_MODEL, D_MODEL) for t in lands)
    x1 = _out_proj(x0, mixed, wout_f)
    hm, xk, xv = _mem_kv(mem0, norm_mem_g, wk_f, wv_f, deps=[x1])
    lands = _split_wait(_gather_first_copies, ga_mlp, xk, "gather_mlp_first_wait")
    gb_mlp = _gather_forward_start(lands, "gather_mlp_forward_start")
    x2, hq, xq, att = _xattn_fwd(x1, norm_x_g, wq_f, xk, xv, wo_t, deps=[gb_mlp[3]])
    w1_b, w2_b = _split_wait(_gather_forward_copies, gb_mlp, x2, "gather_mlp_forward_wait")
    dx3, u, hf, slot_fin = _mlp_fwd_loss(x2, norm_ffn_g, w1_b, w2_b, gfin, tgt)

    rows = lambda t, r: t.reshape(N_DEV, r, D_MODEL)
    dw2 = _wgrad(u, dx3, "wgrad_ff2")
    ex_ff2 = _all_to_all_start([rows(dw2, FF_BLK)], [], "exchange_ff2_start")
    da, dx2, slot_ffn = _mlp_bwd(dx3, u, x2, norm_ffn_g, w1_b, w2_b, deps=[ex_ff2[3]])
    dw1 = _wgrad(hf, da, "wgrad_ff1", col_blocks=True)
    ex_ff1 = _all_to_all_start([dw1], [], "exchange_ff1_start")
    dx1, dxq, dxk, dxv, slot_x = _xattn_bwd(dx2, x1, norm_x_g, xq, xk, xv, wq_f, wo_t, deps=[ex_ff1[3]])
    dwo_t = _wgrad(dx2, att, "wgrad_xo")
    dwq = _wgrad(hq, dxq, "wgrad_xq")
    dwk, dwv, slot_mem = _mem_bwd(mem0, norm_mem_g, hm, dxk, dxv, wk_f, wv_f)
    ex_attn = _all_to_all_start([rows(dwq, 128), rows(dwk, 128), rows(dwv, 128), rows(dwo_t, 128)], [],
                                "exchange_attn_start")
    dwout = _wgrad(mixed, dx1, "wgrad_out")
    dz_pool, d_wpool, slot_ps = _pool_bwd(dx1, wout_f, pooled, wp, pool_scale, deps=[ex_attn[3]])
    small0 = jnp.concatenate([slot_x, slot_mem, slot_ffn, slot_fin, slot_ps], axis=0)
    ex_out = _all_to_all_start([rows(dwout, 128)], [small0, d_wpool], "exchange_out_start")
    dz, slot_lb, slot_gn = _hgrn_bwd(z, o_pre, dx1, wout_f, states, lb_logits, gn, dz_pool, deps=[ex_out[3]])
    dwin_t = _wgrad(dz, h, "wgrad_in")
    small1 = jnp.concatenate([slot_lb, slot_gn], axis=0)
    ex_in = _all_to_all_start([rows(dwin_t, 320)], [small1], "exchange_in_start")
    grad_x, slot_mix = _in_bwd(dz, win_t, x0, norm_mix_g, dx1, deps=[ex_in[3]])
    ex_mix = _all_to_all_start([], [slot_mix], "exchange_mix_start")

    out = {}
    (r_2,) = _split_wait(_all_to_all_copies(1), ex_ff2, ex_mix[3], "exchange_ff2_wait")
    out["w_ff2"] = _sum_adamw(r_2, w_ff2[0], m_w_ff2[0], v_w_ff2[0], "adamw_ff2")
    (r_1,) = _split_wait(_all_to_all_copies(1), ex_ff1, out["w_ff2"][1], "exchange_ff1_wait")
    out["w_ff1"] = _sum_adamw(r_1, w_ff1[0], m_w_ff1[0], v_w_ff1[0], "adamw_ff1")
    r_q, r_k, r_v, r_o = _split_wait(_all_to_all_copies(4), ex_attn, out["w_ff1"][1], "exchange_attn_wait")
    for n, r, (w, m, v) in (("w_xq", r_q, (w_xq, m_w_xq, v_w_xq)), ("w_xk", r_k, (w_xk, m_w_xk, v_w_xk)),
                            ("w_xv", r_v, (w_xv, m_w_xv, v_w_xv))):
        g = _sum_sources(r, "sum_grad_" + n).reshape(w.shape)
        out[n] = (g, *_adamw_whole(g, w, m, v, "adamw_" + n))
    g_xo = _sum_sources(r_o, "sum_grad_xo").T
    out["w_xo"] = (g_xo, *_adamw(g_xo, xo_2d(w_xo), xo_2d(m_w_xo), xo_2d(v_w_xo), "adamw_xo"))
    r_out, r_small0, r_wpool = _split_wait(_all_to_all_copies(1), ex_out, out["w_xo"][1], "exchange_out_wait")
    out["w_out"] = _sum_adamw(r_out, w_out[0], m_w_out[0], v_w_out[0], "adamw_out")
    r_in, r_small1 = _split_wait(_all_to_all_copies(1), ex_in, out["w_out"][1], "exchange_in_wait")
    g_in = _sum_sources(r_in, "sum_grad_in").T
    out["w_in"] = (g_in, *_adamw(g_in, w_in[0], m_w_in[0], v_w_in[0], "adamw_in"))
    (r_small2,) = _split_wait(_all_to_all_copies(0), ex_mix, out["w_in"][1], "exchange_mix_wait")
    row = lambda t: t.reshape(1, -1)
    small_params = {
        "norm_mix_g": (norm_mix_g, m_norm_mix_g, v_norm_mix_g),
        "lb_logits": (lb_logits, m_lb_logits, v_lb_logits),
        "hgrn_norm_g": (hgrn_norm_g[0], m_hgrn_norm_g[0], v_hgrn_norm_g[0]),
        "pool_scale": (pool_scale, m_pool_scale, v_pool_scale),
        "norm_x_g": (norm_x_g, m_norm_x_g, v_norm_x_g),
        "norm_mem_g": (norm_mem_g, m_norm_mem_g, v_norm_mem_g),
        "norm_ffn_g": (norm_ffn_g, m_norm_ffn_g, v_norm_ffn_g),
        "final_norm_g": (row(final_norm_g), row(m_final_norm_g), row(v_final_norm_g)),
        "w_pool": (wp, m_w_pool[0], v_w_pool[0]),
    }
    loss, small_out = _small_update([r_small0, r_small1, r_small2], r_wpool, small_params)
    out.update(small_out)

    shapes = dict(norm_mix_g=norm_mix_g, w_in=w_in, lb_logits=lb_logits, hgrn_norm_g=hgrn_norm_g, w_pool=w_pool,
                  pool_scale=pool_scale, w_out=w_out, norm_x_g=norm_x_g, norm_mem_g=norm_mem_g, w_xq=w_xq, w_xk=w_xk,
                  w_xv=w_xv, w_xo=w_xo, norm_ffn_g=norm_ffn_g, w_ff1=w_ff1, w_ff2=w_ff2, final_norm_g=final_norm_g)
    order = list(shapes)
    group = lambda k: [out[n][k].reshape(shapes[n].shape) for n in order]
    return (loss.reshape(()), grad_x.reshape(x.shape), *group(0), *group(1), *group(2), *group(3))
```

```python
import jax
import jax.numpy as jnp
from jax import lax
from jax.experimental import pallas as pl
from jax.experimental.pallas import tpu as pltpu

F32 = jnp.float32
BF16 = jnp.bfloat16

D_MODEL = 1024
N_DEV = 8
HEADS = 4
HD = 128
HW = HEADS * HD
IN_WIDTH = 5 * HW
XHD = 256
MEM_LEN = 256
D_FF = 4096
FF_BLK = D_FF // N_DEV
POOL_WINDOWS = (2, 4, 8, 16)
POOL_HALO = 16
CHUNK = 64
CHUNKS_PER_STEP = 4
SUB = 16
N_SUB = CHUNK // SUB
EXP_CAP = 80.0
EPS = 1e-6
TINY = 1e-30
ROW_TILE = 512
WIDE_ROW_TILE = 1024
SLOT = 8
V7X_VMEM_LIMIT = 56 * 1024 * 1024

ADAM_LR = 0.001
ADAM_B1 = 0.9
ADAM_B2 = 0.999
ADAM_EPS = 1e-08
ADAM_WD = 0.01
ADAM_STEP = 10

MESH_ID = pl.DeviceIdType.MESH


def _params(sem=None, vmem=V7X_VMEM_LIMIT):
    return pltpu.CompilerParams(dimension_semantics=sem, vmem_limit_bytes=vmem)


def _mm(a, b):
    return lax.dot_general(a.astype(BF16), b.astype(BF16), (((1,), (0,)), ((), ())), preferred_element_type=F32)


def _mm_nt(a, b):
    return lax.dot_general(a.astype(BF16), b.astype(BF16), (((1,), (1,)), ((), ())), preferred_element_type=F32)


def _mm_tn(a, b):
    return lax.dot_general(a.astype(BF16), b.astype(BF16), (((0,), (0,)), ((), ())), preferred_element_type=F32)


def _sigmoid(x):
    return 1.0 / (1.0 + jnp.exp(-x))


def _rms(x):
    return lax.rsqrt(jnp.mean(x * x, axis=-1, keepdims=True) + EPS)


def _rms_bwd(x, g, dh):
    r = _rms(x)
    n = x * r
    dn = dh * g
    dx = r * (dn - n * jnp.mean(dn * n, axis=-1, keepdims=True))
    return dx, jnp.sum(dh * n, axis=0, keepdims=True)


def _tri_dot(tri, x, passes):
    acc = None
    rest = x
    for _ in range(passes):
        piece = rest.astype(BF16)
        part = lax.dot_general(tri, piece, (((1,), (0,)), ((), ())), preferred_element_type=F32)
        acc = part if acc is None else acc + part
        rest = rest - piece.astype(F32)
    return acc


def _adam_update(g, w, m, v):
    nm = ADAM_B1 * m + (1.0 - ADAM_B1) * g
    nv = ADAM_B2 * v + (1.0 - ADAM_B2) * (g * g)
    m_hat = nm / (1.0 - ADAM_B1 ** ADAM_STEP)
    v_hat = nv / (1.0 - ADAM_B2 ** ADAM_STEP)
    return -ADAM_LR * (m_hat / (jnp.sqrt(v_hat) + ADAM_EPS) + ADAM_WD * w), nm, nv


def _full(shape):
    return pl.BlockSpec(shape, lambda *_: (0,) * len(shape))


VMEM_WHOLE = pl.BlockSpec(memory_space=pltpu.VMEM)
ANY_SPACE = pl.BlockSpec(memory_space=pl.ANY)


def _mesh_pos():
    return lax.axis_index("x"), lax.axis_index("y"), lax.axis_index("c")


def _flat(px, py, pc):
    return 4 * px + 2 * py + pc


def _all_gather_weights(shards, cast_only):
    n, nc = len(shards), len(cast_only)
    step = 64

    def body(*refs):
        x_refs, c_refs = refs[:n], refs[n:n + nc]
        out_refs, cast_refs = refs[n + nc:2 * n + nc], refs[2 * n + nc:2 * n + 2 * nc]
        bufs = refs[2 * n + 2 * nc:3 * n + 2 * nc]
        send_sems, recv_sems, local_sems = refs[3 * n + 2 * nc:]
        x, y, c = _mesh_pos()
        me, sibling = (x, y, c), (x, y, 1 - c)
        chips = [(1 - x, y), (x, 1 - y), (1 - x, 1 - y)]

        def copy(a, k, blk, to, src=None):
            rows = out_refs[a].at[_flat(*blk)]
            return pltpu.make_async_remote_copy(
                src_ref=rows if src is None else src, dst_ref=rows,
                send_sem=send_sems.at[7 * a + k], recv_sem=recv_sems.at[7 * a + k], device_id=to, device_id_type=MESH_ID)

        def cast_rows(src, dst, rows):
            def cast(i, carry):
                r0 = pl.multiple_of(i * step, step)
                dst[pl.ds(r0, step), :] = src[pl.ds(r0, step), :].astype(BF16)
                return carry
            lax.fori_loop(0, rows // step, cast, 0)

        first, mine = [], []
        for a in range(n):
            cast_rows(x_refs[a], bufs[a], shards[a].shape[0])
            mine.append(pltpu.make_async_copy(bufs[a], out_refs[a].at[_flat(*me)], local_sems.at[a]))
            first.append(copy(a, 0, me, sibling, src=bufs[a]))
            first += [copy(a, 1 + j, me, (*chip, c), src=bufs[a]) for j, chip in enumerate(chips)]
            for cp in [mine[-1]] + first[-4:]:
                cp.start()
        for a in range(nc):
            cast_rows(c_refs[a], cast_refs[a], cast_only[a].shape[0])
        passed = []
        for j, chip in enumerate(chips):
            for a in range(n):
                copy(a, 1 + j, (*chip, c), me).wait_recv()
                passed.append(copy(a, 4 + j, (*chip, c), sibling))
                passed[-1].start()
        for a in range(n):
            copy(a, 0, sibling, me).wait_recv()
            for j, chip in enumerate(chips):
                copy(a, 4 + j, (*chip, 1 - c), me).wait_recv()
        for cp in first + passed:
            cp.wait_send()
        for cp in mine:
            cp.wait()

    return pl.pallas_call(
        body, name="all_gather_w_in",
        out_shape=[jax.ShapeDtypeStruct((N_DEV,) + s.shape, BF16) for s in shards]
        + [jax.ShapeDtypeStruct(s.shape, BF16) for s in cast_only],
        in_specs=[VMEM_WHOLE] * (n + nc), out_specs=[ANY_SPACE] * n + [VMEM_WHOLE] * nc,
        scratch_shapes=[pltpu.VMEM(s.shape, BF16) for s in shards]
        + [pltpu.SemaphoreType.DMA((7 * n,)), pltpu.SemaphoreType.DMA((7 * n,)), pltpu.SemaphoreType.DMA((n,))],
        compiler_params=_params(),
    )(*shards, *cast_only)


HBM_SPEC = pl.BlockSpec(memory_space=pltpu.HBM)
SEM_SPEC = pl.BlockSpec(memory_space=pltpu.SEMAPHORE)
EFFECT = pltpu.SideEffectType.DATAFLOW_SIDE_EFFECTING
TOKEN = jax.ShapeDtypeStruct((8, 128), F32)


def _in_hbm(a):
    return pltpu.with_memory_space_constraint(a, pltpu.HBM)


def _split_start(copies_of, srcs, lands, n_sems, name):
    ns, nl, k = len(srcs), len(lands), len(n_sems)

    def body(*refs):
        src_refs, land_refs = refs[:ns], refs[ns:ns + nl]
        sems = refs[ns + nl:ns + nl + k]
        token = refs[-1]
        for cp in copies_of(src_refs, land_refs, sems):
            cp.start()
        token[...] = jnp.zeros_like(token)

    outs = pl.pallas_call(
        body, name=name,
        out_shape=[pltpu.SemaphoreType.DMA((q,)) for q in n_sems]
        + [pltpu.HBM(a.shape, a.dtype) for a in list(srcs) + list(lands)] + [TOKEN],
        in_specs=[HBM_SPEC] * (ns + nl),
        out_specs=[SEM_SPEC] * k + [HBM_SPEC] * (ns + nl) + [VMEM_WHOLE],
        input_output_aliases={i: k + i for i in range(ns + nl)},
        compiler_params=pltpu.CompilerParams(has_side_effects=EFFECT),
    )(*[_in_hbm(a) for a in list(srcs) + list(lands)])
    return outs[:k], outs[k:k + ns], outs[k + ns:k + ns + nl], outs[-1]


def _split_wait(copies_of, handle, after, name):
    sems, srcs, lands, _ = handle
    ns, nl, k = len(srcs), len(lands), len(sems)

    def body(*refs):
        src_refs, land_refs = refs[:ns], refs[ns:ns + nl]
        sem_refs = refs[ns + nl:ns + nl + k]
        for cp in copies_of(src_refs, land_refs, sem_refs):
            cp.wait()

    outs = pl.pallas_call(
        body, name=name,
        out_shape=[pltpu.HBM(a.shape, a.dtype) for a in list(srcs) + list(lands)],
        in_specs=[HBM_SPEC] * (ns + nl) + [SEM_SPEC] * k + [ANY_SPACE],
        out_specs=[HBM_SPEC] * (ns + nl),
        input_output_aliases={i: i for i in range(ns + nl)},
        compiler_params=pltpu.CompilerParams(has_side_effects=EFFECT),
    )(*srcs, *lands, *sems, after)
    return outs[ns:]


def _gather_first_copies(shard_refs, land_refs, sems):
    send_sems, recv_sems, local_sems = sems
    x, y, c = _mesh_pos()
    me = _flat(x, y, c)
    peers = [(x, y, 1 - c), (1 - x, y, c), (x, 1 - y, c), (1 - x, 1 - y, c)]
    copies = []
    for a, (shard, land) in enumerate(zip(shard_refs, land_refs)):
        copies.append(pltpu.make_async_copy(shard, land.at[me], local_sems.at[a]))
        for k, peer in enumerate(peers):
            copies.append(pltpu.make_async_remote_copy(
                src_ref=shard, dst_ref=land.at[me], send_sem=send_sems.at[4 * a + k], recv_sem=recv_sems.at[4 * a + k],
                device_id=peer, device_id_type=MESH_ID))
    return copies


def _gather_forward_copies(src_refs, land_refs, sems):
    del src_refs
    send_sems, recv_sems = sems
    x, y, c = _mesh_pos()
    chips = [(1 - x, y), (x, 1 - y), (1 - x, 1 - y)]
    copies = []
    for a, land in enumerate(land_refs):
        for j, chip in enumerate(chips):
            rows = land.at[_flat(*chip, c)]
            copies.append(pltpu.make_async_remote_copy(
                src_ref=rows, dst_ref=rows, send_sem=send_sems.at[3 * a + j], recv_sem=recv_sems.at[3 * a + j],
                device_id=(x, y, 1 - c), device_id_type=MESH_ID))
    return copies


def _gather_first_start(groups, name):
    shards = [s for g in groups for s in g]
    lands = [lax.empty((N_DEV,) + s.shape, s.dtype) for s in shards]
    bounds = [sum(len(g) for g in groups[:i]) for i in range(len(groups) + 1)]

    def copies_of(src_refs, land_refs, sems):
        copies = []
        for i in range(len(groups)):
            lo, hi = bounds[i], bounds[i + 1]
            copies += _gather_first_copies(src_refs[lo:hi], land_refs[lo:hi], sems[3 * i:3 * i + 3])
        return copies

    n_sems = tuple(q for g in groups for q in (4 * len(g), 4 * len(g), len(g)))
    sems, srcs, lands, token = _split_start(copies_of, shards, lands, n_sems, name)
    return [(sems[3 * i:3 * i + 3], srcs[bounds[i]:bounds[i + 1]], lands[bounds[i]:bounds[i + 1]], token)
            for i in range(len(groups))]


def _gather_forward_start(lands, name):
    n = len(lands)
    return _split_start(_gather_forward_copies, [], lands, (3 * n, 3 * n), name)


def _all_to_all_copies(n_scattered):
    def copies_of(src_refs, land_refs, sems):
        send_sems, recv_sems, local_sems = sems
        x, y, c = _mesh_pos()
        me = _flat(x, y, c)
        copies = []
        for a, (src, land) in enumerate(zip(src_refs, land_refs)):
            scattered = a < n_scattered
            copies.append(pltpu.make_async_copy(src.at[me] if scattered else src, land.at[me], local_sems.at[a]))
            for k in range(1, N_DEV):
                peer = (1 - x if k & 4 else x, 1 - y if k & 2 else y, 1 - c if k & 1 else c)
                copies.append(pltpu.make_async_remote_copy(
                    src_ref=src.at[_flat(*peer)] if scattered else src, dst_ref=land.at[me],
                    send_sem=send_sems.at[7 * a + k - 1], recv_sem=recv_sems.at[7 * a + k - 1],
                    device_id=peer, device_id_type=MESH_ID))
        return copies
    return copies_of


def _all_to_all_start(scattered, broadcast, name):
    srcs = list(scattered) + list(broadcast)
    lands = [lax.empty(a.shape, a.dtype) for a in scattered] + [lax.empty((N_DEV,) + a.shape, a.dtype) for a in broadcast]
    n = len(srcs)
    return _split_start(_all_to_all_copies(len(scattered)), srcs, lands, (7 * n, 7 * n, n), name)


def _call_behind(deps, body, *, in_specs, **kwargs):
    n_in, n_dep = len(in_specs), len(deps)

    def body_without_deps(*refs):
        return body(*refs[:n_in], *refs[n_in + n_dep:])

    call = pl.pallas_call(body_without_deps, in_specs=list(in_specs) + [ANY_SPACE] * n_dep, **kwargs)
    return lambda *operands: call(*operands, *deps)


def _row_tile(rows):
    for cand in (256, 128, 64, 32, 16):
        if rows % cand == 0:
            return cand
    return rows


def _sum_sources(recv, name):
    _, rows, cols = recv.shape
    tile = _row_tile(rows)

    def body(r_ref, o_ref):
        acc = r_ref[0].astype(F32)
        for d in range(1, N_DEV):
            acc = acc + r_ref[d].astype(F32)
        o_ref[...] = acc

    return pl.pallas_call(
        body, name=name, grid=(rows // tile,),
        out_shape=jax.ShapeDtypeStruct((rows, cols), F32),
        in_specs=[pl.BlockSpec((N_DEV, tile, cols), lambda i: (0, i, 0))],
        out_specs=pl.BlockSpec((tile, cols), lambda i: (i, 0)),
        compiler_params=_params(("parallel",)),
    )(recv)


def _adamw(g, w, m, v, name):
    rows, cols = g.shape
    tile = _row_tile(rows)

    def body(g_ref, w_ref, m_ref, v_ref, d_ref, nm_ref, nv_ref):
        d_ref[...], nm_ref[...], nv_ref[...] = _adam_update(g_ref[...], w_ref[...], m_ref[...], v_ref[...])

    spec = pl.BlockSpec((tile, cols), lambda i: (i, 0))
    shp = jax.ShapeDtypeStruct((rows, cols), F32)
    return pl.pallas_call(
        body, name=name, grid=(rows // tile,), out_shape=[shp, shp, shp],
        in_specs=[spec] * 4, out_specs=[spec] * 3,
        compiler_params=_params(("parallel",)),
    )(g, w, m, v)


def _adamw_whole(g, w, m, v, name):
    def body(g_ref, w_ref, m_ref, v_ref, d_ref, nm_ref, nv_ref):
        d_ref[...], nm_ref[...], nv_ref[...] = _adam_update(g_ref[...], w_ref[...], m_ref[...], v_ref[...])

    shp = jax.ShapeDtypeStruct(g.shape, F32)
    return pl.pallas_call(
        body, name=name, out_shape=[shp, shp, shp], in_specs=[VMEM_WHOLE] * 4, out_specs=[VMEM_WHOLE] * 3,
        compiler_params=_params(),
    )(g, w, m, v)


def _sum_adamw(recv, w, m, v, name):
    _, rows, cols = recv.shape
    tile = _row_tile(rows)

    def body(r_ref, w_ref, m_ref, v_ref, g_ref, d_ref, nm_ref, nv_ref):
        acc = r_ref[0].astype(F32)
        for d in range(1, N_DEV):
            acc = acc + r_ref[d].astype(F32)
        g_ref[...] = acc
        d_ref[...], nm_ref[...], nv_ref[...] = _adam_update(acc, w_ref[...], m_ref[...], v_ref[...])

    spec = pl.BlockSpec((tile, cols), lambda i: (i, 0))
    shp = jax.ShapeDtypeStruct((rows, cols), F32)
    return pl.pallas_call(
        body, name=name, grid=(rows // tile,), out_shape=[shp] * 4,
        in_specs=[pl.BlockSpec((N_DEV, tile, cols), lambda i: (0, i, 0)), spec, spec, spec], out_specs=[spec] * 4,
        compiler_params=_params(("parallel",)),
    )(recv, w, m, v)


SMALL_SLOTS = {"norm_x_g": (0, 0, 1, D_MODEL), "norm_mem_g": (0, 8, 1, D_MODEL), "norm_ffn_g": (0, 16, 1, D_MODEL),
               "final_norm_g": (0, 24, 1, D_MODEL), "pool_scale": (0, 32, 1, HW),
               "lb_logits": (1, 0, 2, HW), "hgrn_norm_g": (1, 8, HEADS, HD), "norm_mix_g": (2, 0, 1, D_MODEL)}
LOSS_ROW = 25
SMALL_ORDER = ("norm_mix_g", "lb_logits", "hgrn_norm_g", "pool_scale", "norm_x_g", "norm_mem_g", "norm_ffn_g",
               "final_norm_g", "w_pool")


def _small_update(srecvs, wprecv, params):
    flat = [t for n in SMALL_ORDER for t in params[n]]
    nb = len(srecvs)
    n_in = nb + 1 + len(flat)

    def body(*refs):
        s_refs, wp_ref = refs[0:nb], refs[nb]
        in_refs = refs[nb + 1:n_in]
        loss_ref = refs[n_in]
        out_refs = refs[n_in + 1:-nb]
        accs = refs[-nb:]
        for s_ref, acc in zip(s_refs, accs):
            total = s_ref[0]
            for d in range(1, N_DEV):
                total = total + s_ref[d]
            acc[...] = total
        loss_ref[...] = accs[0][LOSS_ROW:LOSS_ROW + 1, 0:1]
        for i, name in enumerate(SMALL_ORDER):
            w_ref, m_ref, v_ref = in_refs[3 * i:3 * i + 3]
            g_ref, d_ref, nm_ref, nv_ref = out_refs[4 * i:4 * i + 4]
            if name == "w_pool":
                g = wp_ref[0]
                for d in range(1, N_DEV):
                    g = g + wp_ref[d]
            else:
                buf, r0, nr, nc = SMALL_SLOTS[name]
                g = accs[buf][r0:r0 + nr, 0:nc]
            g_ref[...] = g
            d_ref[...], nm_ref[...], nv_ref[...] = _adam_update(g, w_ref[...], m_ref[...], v_ref[...])

    out_shape = [jax.ShapeDtypeStruct((1, 1), F32)]
    for n in SMALL_ORDER:
        out_shape += [jax.ShapeDtypeStruct(params[n][0].shape, F32)] * 4
    outs = pl.pallas_call(
        body, name="small_update", out_shape=out_shape,
        in_specs=[VMEM_WHOLE] * n_in, out_specs=[VMEM_WHOLE] * len(out_shape),
        scratch_shapes=[pltpu.VMEM(r.shape[1:], F32) for r in srecvs],
        compiler_params=_params(),
    )(*srecvs, wprecv, *flat)
    return outs[0], {n: outs[1 + 4 * i:5 + 4 * i] for i, n in enumerate(SMALL_ORDER)}


def _in_proj(x, g, w_t, deps):
    s = x.shape[0]
    tm = min(WIDE_ROW_TILE, s)

    def body(x_ref, g_ref, w_ref, z_ref, h_ref):
        xv = x_ref[...]
        h = (xv * _rms(xv) * g_ref[...]).astype(BF16)
        h_ref[...] = h
        z_ref[...] = _mm_nt(h, w_ref[...])

    return _call_behind(
        deps, body, name="in_proj", grid=(s // tm,),
        out_shape=[jax.ShapeDtypeStruct((s, IN_WIDTH), F32), jax.ShapeDtypeStruct((s, D_MODEL), BF16)],
        in_specs=[pl.BlockSpec((tm, D_MODEL), lambda i: (i, 0)), _full((1, D_MODEL)), VMEM_WHOLE],
        out_specs=[pl.BlockSpec((tm, IN_WIDTH), lambda i: (i, 0)), pl.BlockSpec((tm, D_MODEL), lambda i: (i, 0))],
        compiler_params=_params(("parallel",)),
    )(x, g, w_t)


def _chunk_masks():
    row = lax.broadcasted_iota(jnp.int32, (CHUNK, CHUNK), 0)
    col = lax.broadcasted_iota(jnp.int32, (CHUNK, CHUNK), 1)
    return row, col


def _ones_where(mask):
    return jnp.where(mask, 1.0, 0.0).astype(BF16)


def _hgrn_gates(zq, zf, lb):
    sq = _sigmoid(zq)
    sig = _sigmoid(zf)
    f = lb + (1.0 - lb) * sig
    return zq * sq, sq, sig, f


def _hgrn_intra_factors(b_scr, r0, bh, qh, kh, sl):
    trow = lax.broadcasted_iota(jnp.int32, (CHUNK, HD), 0)
    eq, ek = [], []
    for j in range(N_SUB):
        if j == 0:
            base = jnp.zeros((1, HD), F32)
        else:
            base = b_scr[r0 + SUB * j - 1:r0 + SUB * j, sl]
        in_j = (trow >= SUB * j) & (trow < SUB * (j + 1))
        eq.append(jnp.where(in_j, jnp.exp(bh - base), 0.0))
        ek.append(jnp.where(trow < SUB * (j + 1), jnp.exp(jnp.minimum(base - bh, EXP_CAP)), 0.0))
    eqcat = jnp.concatenate(eq, axis=1)
    ekcat = jnp.concatenate(ek, axis=1)
    qcat = jnp.concatenate([qh] * N_SUB, axis=1) * eqcat
    kecat = jnp.concatenate([kh] * N_SUB, axis=1) * ekcat
    return qcat, kecat, eqcat, ekcat


def _sum_lane_blocks(a):
    out = a[:, 0:HD]
    for j in range(1, N_SUB):
        out = out + a[:, HD * j:HD * (j + 1)]
    return out


def _hgrn_fwd(z, lb_logits, gn):
    s = z.shape[0]
    n_chunks = s // CHUNK

    def body(zq_ref, zf_ref, zi_ref, zg_ref, lbl_ref, gn_ref, oa_ref, o_ref, st_ref, state, b_scr):
        @pl.when(pl.program_id(0) == 0)
        def _():
            state[...] = jnp.zeros_like(state)

        lb = _sigmoid(lbl_ref[0:1, :] - lbl_ref[1:2, :])
        row, col = _chunk_masks()
        causal = col <= row
        tri = _ones_where(causal)
        for c in range(CHUNKS_PER_STEP):
            r0 = CHUNK * c
            rs = slice(r0, r0 + CHUNK)
            st_ref[c] = state[...]
            q, _, _, f = _hgrn_gates(zq_ref[rs, :], zf_ref[rs, :], lb)
            kk = 1.0 - f
            b_scr[rs, :] = _tri_dot(tri, jnp.log(f), 3)
            for h in range(HEADS):
                sl = slice(HD * h, HD * (h + 1))
                bh = b_scr[rs, sl]
                qh, kh, vh = q[:, sl], kk[:, sl], zi_ref[rs, sl]
                st = state[h]
                b_last = b_scr[r0 + CHUNK - 1:r0 + CHUNK, sl]
                qcat, kecat, _, _ = _hgrn_intra_factors(b_scr, r0, bh, qh, kh, sl)
                a = jnp.where(causal, _mm_nt(qcat, kecat), 0.0)
                o = _mm(a, vh) + _mm_nt(qh * jnp.exp(bh), st)
                state[h] = st * jnp.exp(b_last) + _mm_tn(vh, kh * jnp.exp(b_last - bh))
                o_ref[rs, sl] = o
                zg = zg_ref[rs, sl]
                oa_ref[rs, sl] = (o * _rms(o) * gn_ref[h:h + 1, :] * zg * _sigmoid(zg)).astype(BF16)

    rows = CHUNK * CHUNKS_PER_STEP
    zspec = lambda cb: pl.BlockSpec((rows, HW), lambda i, cb=cb: (i, cb))
    return pl.pallas_call(
        body, name="hgrn_fwd", grid=(s // rows,),
        out_shape=[jax.ShapeDtypeStruct((s, 2 * HW), BF16), jax.ShapeDtypeStruct((s, HW), F32),
                   jax.ShapeDtypeStruct((n_chunks, HEADS, HD, HD), F32)],
        in_specs=[zspec(0), zspec(1), zspec(2), zspec(3), _full((2, HW)), _full((HEADS, HD))],
        out_specs=[pl.BlockSpec((rows, HW), lambda i: (i, 0)), pl.BlockSpec((rows, HW), lambda i: (i, 0)),
                   pl.BlockSpec((CHUNKS_PER_STEP, HEADS, HD, HD), lambda i: (i, 0, 0, 0))],
        scratch_shapes=[pltpu.VMEM((HEADS, HD, HD), F32), pltpu.VMEM((rows, HW), F32)],
        compiler_params=_params(("arbitrary",)),
    )(z, z, z, z, lb_logits, gn)


def _pool_counts(tile_idx, tm):
    t = tile_idx * tm + lax.broadcasted_iota(jnp.int32, (tm, 1), 0)
    return [1.0 / jnp.minimum(t + 1, w).astype(F32) for w in POOL_WINDOWS]


def _pool_fwd(z, w_pool, scale, mixed_in, deps):
    s = z.shape[0]
    tm = min(ROW_TILE, s)

    def body(p_ref, w_ref, sc_ref, mixin_ref, ob_ref, pooled_ref, ext):
        i = pl.program_id(0)

        @pl.when(i == 0)
        def _():
            ext[0:POOL_HALO, :] = jnp.zeros((POOL_HALO, HW), F32)

        @pl.when(i > 0)
        def _():
            ext[0:POOL_HALO, :] = ext[tm:tm + POOL_HALO, :]

        ext[POOL_HALO:POOL_HALO + tm, :] = p_ref[...]
        inv = _pool_counts(i, tm)
        for g, w in enumerate(POOL_WINDOWS):
            sl = slice(HD * g, HD * (g + 1))
            p = ext[POOL_HALO:POOL_HALO + tm, sl]
            win = p
            for d in range(1, w):
                win = win + ext[POOL_HALO - d:POOL_HALO - d + tm, sl]
            pooled = (win * inv[g] - p).astype(BF16)
            pooled_ref[:, sl] = pooled
            ob_ref[:, sl] = (_mm(pooled, w_ref[g]) * sc_ref[:, sl]).astype(BF16)

    return _call_behind(
        deps, body, name="pool_fwd", grid=(s // tm,),
        out_shape=[jax.ShapeDtypeStruct((s, 2 * HW), BF16), jax.ShapeDtypeStruct((s, HW), BF16)],
        in_specs=[pl.BlockSpec((tm, HW), lambda i: (i, 4)), _full((HEADS, HD, HD)), _full((1, HW)), ANY_SPACE],
        out_specs=[pl.BlockSpec((tm, HW), lambda i: (i, 1)), pl.BlockSpec((tm, HW), lambda i: (i, 0))],
        scratch_shapes=[pltpu.VMEM((tm + POOL_HALO, HW), F32)],
        input_output_aliases={3: 0},
        compiler_params=_params(("arbitrary",)),
    )(z, w_pool, scale, mixed_in)


def _out_proj(x, mixed, w_out):
    s = x.shape[0]
    tm = min(WIDE_ROW_TILE, s)

    def body(x_ref, a_ref, w_ref, o_ref):
        o_ref[...] = x_ref[...] + _mm(a_ref[...], w_ref[...])

    return pl.pallas_call(
        body, name="out_proj", grid=(s // tm,),
        out_shape=jax.ShapeDtypeStruct((s, D_MODEL), F32),
        in_specs=[pl.BlockSpec((tm, D_MODEL), lambda i: (i, 0)), pl.BlockSpec((tm, D_MODEL), lambda i: (i, 0)), VMEM_WHOLE],
        out_specs=pl.BlockSpec((tm, D_MODEL), lambda i: (i, 0)),
        compiler_params=_params(("parallel",)),
    )(x, mixed, w_out)


def _mem_kv(mem, g, wk, wv, deps):
    def body(m_ref, g_ref, wk_ref, wv_ref, hm_ref, k_ref, v_ref):
        m = m_ref[...]
        hm = (m * _rms(m) * g_ref[...]).astype(BF16)
        hm_ref[...] = hm
        k_ref[...] = _mm(hm, wk_ref[...]).astype(BF16)
        v_ref[...] = _mm(hm, wv_ref[...]).astype(BF16)

    shp = jax.ShapeDtypeStruct((MEM_LEN, D_MODEL), BF16)
    return _call_behind(
        deps, body, name="mem_kv", out_shape=[shp, shp, shp],
        in_specs=[VMEM_WHOLE] * 4, out_specs=[VMEM_WHOLE] * 3,
        compiler_params=_params(),
    )(mem, g, wk, wv)


def _softmax_rows(sc):
    e = jnp.exp(sc - jnp.max(sc, axis=-1, keepdims=True))
    return e / jnp.sum(e, axis=-1, keepdims=True)


def _xattn_fwd(x, g, wq, xk, xv, wo_t, deps):
    s = x.shape[0]
    tm = min(ROW_TILE, s)
    scale = XHD ** -0.5

    def body(x_ref, g_ref, wq_ref, k_ref, v_ref, wo_ref, o_ref, hq_ref, q_ref, att_ref):
        xv_ = x_ref[...]
        hq = (xv_ * _rms(xv_) * g_ref[...]).astype(BF16)
        hq_ref[...] = hq
        q_ref[...] = (_mm(hq, wq_ref[...]) * scale).astype(BF16)
        for h in range(HEADS):
            sl = slice(XHD * h, XHD * (h + 1))
            p = _softmax_rows(_mm_nt(q_ref[:, sl], k_ref[:, sl]))
            att_ref[:, sl] = _mm(p, v_ref[:, sl]).astype(BF16)
        o_ref[...] = xv_ + _mm_nt(att_ref[...], wo_ref[...])

    row_f32 = pl.BlockSpec((tm, D_MODEL), lambda i: (i, 0))
    bshape = jax.ShapeDtypeStruct((s, D_MODEL), BF16)
    return _call_behind(
        deps, body, name="xattn_fwd", grid=(s // tm,),
        out_shape=[jax.ShapeDtypeStruct((s, D_MODEL), F32), bshape, bshape, bshape],
        in_specs=[row_f32, _full((1, D_MODEL)), VMEM_WHOLE, VMEM_WHOLE, VMEM_WHOLE, VMEM_WHOLE],
        out_specs=[row_f32] * 4,
        compiler_params=_params(("parallel",)),
    )(x, g, wq, xk, xv, wo_t)


def _mlp_fwd_loss(x, g, w1, w2, gf, target):
    s = x.shape[0]
    tm = min(ROW_TILE, s)

    def body(x_ref, g_ref, w1_ref, w2_ref, gf_ref, t_ref, dx_ref, u_ref, hf_ref, slot_ref):
        @pl.when(pl.program_id(0) == 0)
        def _():
            slot_ref[...] = jnp.zeros_like(slot_ref)

        xv = x_ref[...]
        hf = (xv * _rms(xv) * g_ref[...]).astype(BF16)
        hf_ref[...] = hf
        for j in range(N_DEV):
            a = jnp.maximum(_mm(hf, w1_ref[j]), 0.0)
            u_ref[:, FF_BLK * j:FF_BLK * (j + 1)] = (a * a).astype(BF16)
        acc = xv + _mm(u_ref[...], w2_ref[...])
        gfv = gf_ref[...]
        r = _rms(acc)
        n = acc * r
        err = n * gfv - t_ref[...]
        slot_ref[1:2, :] += jnp.sum(jnp.mean(err * err, axis=-1, keepdims=True), axis=0, keepdims=True) * 0.5
        dy = err * (1.0 / D_MODEL)
        slot_ref[0:1, :] += jnp.sum(dy * n, axis=0, keepdims=True)
        dn = dy * gfv
        dx_ref[...] = r * (dn - n * jnp.mean(dn * n, axis=-1, keepdims=True))

    row_f32 = pl.BlockSpec((tm, D_MODEL), lambda i: (i, 0))
    return pl.pallas_call(
        body, name="mlp_fwd_loss", grid=(s // tm,),
        out_shape=[jax.ShapeDtypeStruct((s, D_MODEL), F32), jax.ShapeDtypeStruct((s, D_FF), BF16),
                   jax.ShapeDtypeStruct((s, D_MODEL), BF16), jax.ShapeDtypeStruct((SLOT, D_MODEL), F32)],
        in_specs=[row_f32, _full((1, D_MODEL)), VMEM_WHOLE, VMEM_WHOLE, _full((1, D_MODEL)), row_f32],
        out_specs=[row_f32, pl.BlockSpec((tm, D_FF), lambda i: (i, 0)), row_f32, _full((SLOT, D_MODEL))],
        compiler_params=_params(("arbitrary",)),
    )(x, g, w1, w2, gf, target)


def _zero_slot(slot_ref):
    @pl.when(pl.program_id(0) == 0)
    def _():
        slot_ref[...] = jnp.zeros_like(slot_ref)


def _mlp_bwd(dx3, u, x2, g, w1, w2, deps):
    s = x2.shape[0]
    tm = min(ROW_TILE, s)

    def body(d_ref, u_ref, x_ref, g_ref, w1_ref, w2_ref, da_ref, dx_ref, slot_ref):
        _zero_slot(slot_ref)
        d = d_ref[...]
        d16 = d.astype(BF16)
        dhf = jnp.zeros((tm, D_MODEL), F32)
        for j in range(N_DEV):
            sl = slice(FF_BLK * j, FF_BLK * (j + 1))
            u = u_ref[:, sl].astype(F32)
            da = (_mm_nt(d16, w2_ref[j]) * (2.0 * u * lax.rsqrt(jnp.maximum(u, TINY)))).astype(BF16)
            da_ref[:, sl] = da
            dhf = dhf + _mm_nt(da, w1_ref[j])
        dx, dg = _rms_bwd(x_ref[...], g_ref[...], dhf)
        dx_ref[...] = d + dx
        slot_ref[0:1, :] += dg

    row_f32 = pl.BlockSpec((tm, D_MODEL), lambda i: (i, 0))
    return _call_behind(
        deps, body, name="mlp_bwd", grid=(s // tm,),
        out_shape=[jax.ShapeDtypeStruct((s, D_FF), BF16), jax.ShapeDtypeStruct((s, D_MODEL), F32),
                   jax.ShapeDtypeStruct((SLOT, D_MODEL), F32)],
        in_specs=[row_f32, pl.BlockSpec((tm, D_FF), lambda i: (i, 0)), row_f32, _full((1, D_MODEL)),
                  VMEM_WHOLE, VMEM_WHOLE],
        out_specs=[pl.BlockSpec((tm, D_FF), lambda i: (i, 0)), row_f32, _full((SLOT, D_MODEL))],
        compiler_params=_params(("arbitrary",)),
    )(dx3, u, x2, g, w1, w2)


def _wgrad(a, b, name, col_blocks=False):
    s, m = a.shape
    n = b.shape[1]
    tm = 1280 if m % 1280 == 0 else min(1024, m)
    tn = min(1024, n)
    blk = n // N_DEV
    per_step = tn // blk if col_blocks else 1
    ts = min(2 * ROW_TILE, s)
    n_s = s // ts

    def body(a_ref, b_ref, o_ref, acc):
        k = pl.program_id(2)

        @pl.when(k == 0)
        def _():
            acc[...] = jnp.zeros_like(acc)

        acc[...] += _mm_tn(a_ref[...], b_ref[...])

        @pl.when(k == n_s - 1)
        def _():
            if col_blocks:
                for p in range(per_step):
                    o_ref[p] = acc[:, blk * p:blk * (p + 1)].astype(BF16)
            else:
                o_ref[...] = acc[...].astype(BF16)

    if col_blocks:
        out_shape = jax.ShapeDtypeStruct((N_DEV, m, blk), BF16)
        out_spec = pl.BlockSpec((per_step, tm, blk), lambda i, j, k: (j, i, 0))
    else:
        out_shape = jax.ShapeDtypeStruct((m, n), BF16)
        out_spec = pl.BlockSpec((tm, tn), lambda i, j, k: (i, j))
    return pl.pallas_call(
        body, name=name, grid=(m // tm, n // tn, n_s), out_shape=out_shape,
        in_specs=[pl.BlockSpec((ts, tm), lambda i, j, k: (k, i)), pl.BlockSpec((ts, tn), lambda i, j, k: (k, j))],
        out_specs=out_spec,
        scratch_shapes=[pltpu.VMEM((tm, tn), F32)],
        compiler_params=_params(("parallel", "parallel", "arbitrary")),
    )(a, b)


def _xattn_bwd(dx2, x1, g, q, xk, xv, wq, wo_t, deps):
    s = x1.shape[0]
    tm = min(ROW_TILE, s)
    scale = XHD ** -0.5

    def body(d_ref, x_ref, g_ref, q_ref, k_ref, v_ref, wq_ref, wo_ref, dx_ref, dq_ref, dk_ref, dv_ref, slot_ref, datt):
        _zero_slot(slot_ref)

        @pl.when(pl.program_id(0) == 0)
        def _():
            dk_ref[...] = jnp.zeros_like(dk_ref)
            dv_ref[...] = jnp.zeros_like(dv_ref)

        d = d_ref[...]
        datt[...] = _mm(d, wo_ref[...]).astype(BF16)
        for h in range(HEADS):
            sl = slice(XHD * h, XHD * (h + 1))
            qh, kh, vh, dah = q_ref[:, sl], k_ref[:, sl], v_ref[:, sl], datt[:, sl]
            p = _softmax_rows(_mm_nt(qh, kh))
            dp = _mm_nt(dah, vh)
            ds = (p * (dp - jnp.sum(dp * p, axis=-1, keepdims=True))).astype(BF16)
            dq_ref[:, sl] = (_mm(ds, kh) * scale).astype(BF16)
            dk_ref[:, sl] += _mm_tn(ds, qh)
            dv_ref[:, sl] += _mm_tn(p, dah)
        dx, dg = _rms_bwd(x_ref[...], g_ref[...], _mm_nt(dq_ref[...], wq_ref[...]))
        dx_ref[...] = d + dx
        slot_ref[0:1, :] += dg

    row_f32 = pl.BlockSpec((tm, D_MODEL), lambda i: (i, 0))
    kv = jax.ShapeDtypeStruct((MEM_LEN, D_MODEL), F32)
    return _call_behind(
        deps, body, name="xattn_bwd", grid=(s // tm,),
        out_shape=[jax.ShapeDtypeStruct((s, D_MODEL), F32), jax.ShapeDtypeStruct((s, D_MODEL), BF16), kv, kv,
                   jax.ShapeDtypeStruct((SLOT, D_MODEL), F32)],
        in_specs=[row_f32, row_f32, _full((1, D_MODEL)), row_f32, VMEM_WHOLE, VMEM_WHOLE, VMEM_WHOLE, VMEM_WHOLE],
        out_specs=[row_f32, row_f32, _full((MEM_LEN, D_MODEL)), _full((MEM_LEN, D_MODEL)), _full((SLOT, D_MODEL))],
        scratch_shapes=[pltpu.VMEM((tm, D_MODEL), BF16)],
        compiler_params=_params(("arbitrary",)),
    )(dx2, x1, g, q, xk, xv, wq, wo_t)


def _mem_bwd(mem, g, hm, dxk, dxv, wk, wv):
    def body(m_ref, g_ref, hm_ref, dk_ref, dv_ref, wk_ref, wv_ref, dwk_ref, dwv_ref, slot_ref):
        dk, dv = dk_ref[...], dv_ref[...]
        hm_ = hm_ref[...]
        dwk_ref[...] = _mm_tn(hm_, dk).astype(BF16)
        dwv_ref[...] = _mm_tn(hm_, dv).astype(BF16)
        _, dg = _rms_bwd(m_ref[...], g_ref[...], _mm_nt(dk, wk_ref[...]) + _mm_nt(dv, wv_ref[...]))
        slot_ref[...] = jnp.zeros_like(slot_ref)
        slot_ref[0:1, :] = dg

    wshape = jax.ShapeDtypeStruct((D_MODEL, D_MODEL), BF16)
    return pl.pallas_call(
        body, name="mem_bwd", out_shape=[wshape, wshape, jax.ShapeDtypeStruct((SLOT, D_MODEL), F32)],
        in_specs=[VMEM_WHOLE] * 7, out_specs=[VMEM_WHOLE] * 3,
        compiler_params=_params(),
    )(mem, g, hm, dxk, dxv, wk, wv)


def _pool_bwd(dx1, w_out, pooled, w_pool, scale, deps):
    s = dx1.shape[0]
    tm = min(ROW_TILE, s)
    n_t = s // tm

    def body(dx_ref, wo_ref, pl_ref, w_ref, sc_ref, dz_ref, dw_ref, slot_ref, ext, do_ref):
        i = pl.program_id(0)
        tile = n_t - 1 - i
        _zero_slot(slot_ref)
        do_ref[...] = _mm_nt(dx_ref[...], wo_ref[HW:2 * HW, :])

        @pl.when(i == 0)
        def _():
            dw_ref[...] = jnp.zeros_like(dw_ref)
            ext[tm:tm + POOL_HALO, :] = jnp.zeros((POOL_HALO, HW), F32)

        @pl.when(i > 0)
        def _():
            ext[tm:tm + POOL_HALO, :] = ext[0:POOL_HALO, :]

        inv = _pool_counts(tile, tm)
        dpooled = []
        for g in range(HEADS):
            sl = slice(HD * g, HD * (g + 1))
            pooled_g = pl_ref[:, sl]
            do = do_ref[:, sl]
            slot_ref[0:1, sl] += jnp.sum(_mm(pooled_g, w_ref[g]) * do, axis=0, keepdims=True)
            dy = (do * sc_ref[:, sl]).astype(BF16)
            dw_ref[g] += _mm_tn(pooled_g, dy)
            dpo = _mm_nt(dy, w_ref[g])
            dpooled.append(dpo)
            ext[0:tm, sl] = dpo * inv[g]
        for g, w in enumerate(POOL_WINDOWS):
            sl = slice(HD * g, HD * (g + 1))
            win = ext[0:tm, sl]
            for d in range(1, w):
                win = win + ext[d:d + tm, sl]
            dz_ref[:, sl] = win - dpooled[g]

    return _call_behind(
        deps, body, name="pool_bwd", grid=(n_t,),
        out_shape=[jax.ShapeDtypeStruct((s, IN_WIDTH), F32), jax.ShapeDtypeStruct((HEADS, HD, HD), F32),
                   jax.ShapeDtypeStruct((SLOT, D_MODEL), F32)],
        in_specs=[pl.BlockSpec((tm, D_MODEL), lambda i: (n_t - 1 - i, 0)), VMEM_WHOLE,
                  pl.BlockSpec((tm, HW), lambda i: (n_t - 1 - i, 0)), _full((HEADS, HD, HD)), _full((1, HW))],
        out_specs=[pl.BlockSpec((tm, HW), lambda i: (n_t - 1 - i, 4)), _full((HEADS, HD, HD)), _full((SLOT, D_MODEL))],
        scratch_shapes=[pltpu.VMEM((tm + POOL_HALO, HW), F32), pltpu.VMEM((tm, HW), F32)],
        compiler_params=_params(("arbitrary",)),
    )(dx1, w_out, pooled, w_pool, scale)


def _hgrn_bwd(z, o, dx1, w_out, states, lb_logits, gn, dz_in, deps):
    s = z.shape[0]
    n_chunks = s // CHUNK

    def body(zq_ref, zf_ref, zi_ref, zg_ref, o_ref, dx_ref, wo_ref, st_ref, lbl_ref, gn_ref, dzin_ref,
             dz_ref, dlb_ref, dgn_ref, dstate, b_scr, dlb_acc, do_ref):
        i = pl.program_id(0)

        @pl.when(i == 0)
        def _():
            dstate[...] = jnp.zeros_like(dstate)
            dlb_acc[...] = jnp.zeros_like(dlb_acc)
            dgn_ref[...] = jnp.zeros_like(dgn_ref)
            dlb_ref[...] = jnp.zeros_like(dlb_ref)

        do_ref[...] = _mm_nt(dx_ref[...], wo_ref[0:HW, :])
        lb = _sigmoid(lbl_ref[0:1, :] - lbl_ref[1:2, :])
        row, col = _chunk_masks()
        causal = col <= row
        tri = _ones_where(causal)
        upper = _ones_where(col >= row)
        strict_lower = _ones_where(col < row)
        for c in reversed(range(CHUNKS_PER_STEP)):
            r0 = CHUNK * c
            rs = slice(r0, r0 + CHUNK)
            zq = zq_ref[rs, :]
            q, sq, sig, f = _hgrn_gates(zq, zf_ref[rs, :], lb)
            kk = 1.0 - f
            b_scr[rs, :] = _tri_dot(tri, jnp.log(f), 3)
            for h in range(HEADS):
                sl = slice(HD * h, HD * (h + 1))
                oh = o_ref[rs, sl]
                gnh = gn_ref[h:h + 1, :]
                zg = zg_ref[rs, sl]
                sg = _sigmoid(zg)
                doa = do_ref[rs, sl]
                don = doa * (zg * sg)
                d_o, dgn = _rms_bwd(oh, gnh, don)
                dgn_ref[h:h + 1, 0:HD] += dgn
                dz_ref[rs, 3 * HW + HD * h:3 * HW + HD * (h + 1)] = (
                    doa * (oh * _rms(oh) * gnh) * (sg * (1.0 + zg * (1.0 - sg))))
                bh = b_scr[rs, sl]
                qh, kh, vh = q[:, sl], kk[:, sl], zi_ref[rs, sl]
                st0 = st_ref[c, h]
                ds1 = dstate[h]
                b_last = b_scr[r0 + CHUNK - 1:r0 + CHUNK, sl]
                lam = jnp.exp(bh)
                e_last = jnp.exp(b_last - bh)
                lam_last = jnp.exp(b_last)
                qcat, kecat, eqcat, ekcat = _hgrn_intra_factors(b_scr, r0, bh, qh, kh, sl)
                a = jnp.where(causal, _mm_nt(qcat, kecat), 0.0)
                da = jnp.where(causal, _mm_nt(d_o, vh), 0.0)
                dz_ref[rs, 2 * HW + HD * h:2 * HW + HD * (h + 1)] = _mm_tn(a, d_o) + _mm_nt(kh * e_last, ds1)
                q16, ke16 = qcat.astype(BF16), kecat.astype(BF16)
                gq = _mm(da, ke16)
                gk = _mm_tn(da, q16)
                dq_inter = lam * _mm(d_o, st0)
                dq = _sum_lane_blocks(eqcat * gq) + dq_inter
                dk_intra = _sum_lane_blocks(ekcat * gk)
                dk_state = _mm(vh, ds1) * e_last
                state_term = lam_last * jnp.sum(st0 * ds1, axis=0, keepdims=True)
                dstate[h] = ds1 * lam_last + _mm_tn(d_o, qh * lam)
                db_intra = _sum_lane_blocks(q16.astype(F32) * gq - ke16.astype(F32) * gk)
                dlf = (_tri_dot(upper, db_intra + qh * dq_inter, 2) + _tri_dot(strict_lower, kh * dk_state, 2)
                       + state_term)
                sigh = sig[:, sl]
                df = dlf / f[:, sl] - (dk_intra + dk_state)
                dlb_acc[:, sl] += jnp.sum(df * (1.0 - sigh), axis=0, keepdims=True)
                dz_ref[rs, HW + HD * h:HW + HD * (h + 1)] = df * (1.0 - lb[:, sl]) * sigh * (1.0 - sigh)
                sqh = sq[:, sl]
                dz_ref[rs, sl] = dq * (sqh * (1.0 + zq[:, sl] * (1.0 - sqh)))

        @pl.when(i == n_steps - 1)
        def _():
            dl0 = dlb_acc[...] * lb * (1.0 - lb)
            dlb_ref[0:1, 0:HW] = dl0
            dlb_ref[1:2, 0:HW] = -dl0

    rows = CHUNK * CHUNKS_PER_STEP
    n_steps = s // rows
    rev = lambda i: n_steps - 1 - i
    zspec = lambda cb: pl.BlockSpec((rows, HW), lambda i, cb=cb: (rev(i), cb))
    slot = jax.ShapeDtypeStruct((SLOT, D_MODEL), F32)
    return _call_behind(
        deps, body, name="hgrn_bwd", grid=(n_steps,),
        out_shape=[jax.ShapeDtypeStruct((s, IN_WIDTH), F32), slot, slot],
        in_specs=[zspec(0), zspec(1), zspec(2), zspec(3), pl.BlockSpec((rows, HW), lambda i: (rev(i), 0)),
                  pl.BlockSpec((rows, D_MODEL), lambda i: (rev(i), 0)), VMEM_WHOLE,
                  pl.BlockSpec((CHUNKS_PER_STEP, HEADS, HD, HD), lambda i: (rev(i), 0, 0, 0)), _full((2, HW)),
                  _full((HEADS, HD)), ANY_SPACE],
        out_specs=[pl.BlockSpec((rows, 4 * HW), lambda i: (rev(i), 0)), _full((SLOT, D_MODEL)), _full((SLOT, D_MODEL))],
        scratch_shapes=[pltpu.VMEM((HEADS, HD, HD), F32), pltpu.VMEM((rows, HW), F32), pltpu.VMEM((1, HW), F32),
                        pltpu.VMEM((rows, HW), F32)],
        input_output_aliases={10: 0},
        compiler_params=_params(("arbitrary",)),
    )(z, z, z, z, o, dx1, w_out, states, lb_logits, gn, dz_in)


def _in_bwd(dz, w_t, x0, g, dx1, deps):
    s = x0.shape[0]
    tm = min(ROW_TILE, s)

    def body(dz_ref, w_ref, x_ref, g_ref, d_ref, dx_ref, slot_ref):
        _zero_slot(slot_ref)
        dx, dg = _rms_bwd(x_ref[...], g_ref[...], _mm(dz_ref[...], w_ref[...]))
        dx_ref[...] = d_ref[...] + dx
        slot_ref[0:1, :] += dg

    row_f32 = pl.BlockSpec((tm, D_MODEL), lambda i: (i, 0))
    return _call_behind(
        deps, body, name="in_bwd", grid=(s // tm,),
        out_shape=[jax.ShapeDtypeStruct((s, D_MODEL), F32), jax.ShapeDtypeStruct((SLOT, D_MODEL), F32)],
        in_specs=[pl.BlockSpec((tm, IN_WIDTH), lambda i: (i, 0)), VMEM_WHOLE, row_f32, _full((1, D_MODEL)), row_f32],
        out_specs=[row_f32, _full((SLOT, D_MODEL))],
        compiler_params=_params(("arbitrary",)),
    )(dz, w_t, x0, g, dx1)


def kernel(x, mem, norm_mix_g, w_in, lb_logits, hgrn_norm_g, w_pool, pool_scale, w_out, norm_x_g, norm_mem_g, w_xq, w_xk, w_xv, w_xo, norm_ffn_g, w_ff1, w_ff2, final_norm_g, loss_target, m_norm_mix_g, m_w_in, m_lb_logits, m_hgrn_norm_g, m_w_pool, m_pool_scale, m_w_out, m_norm_x_g, m_norm_mem_g, m_w_xq, m_w_xk, m_w_xv, m_w_xo, m_norm_ffn_g, m_w_ff1, m_w_ff2, m_final_norm_g, v_norm_mix_g, v_w_in, v_lb_logits, v_hgrn_norm_g, v_w_pool, v_pool_scale, v_w_out, v_norm_x_g, v_norm_mem_g, v_w_xq, v_w_xk, v_w_xv, v_w_xo, v_norm_ffn_g, v_w_ff1, v_w_ff2, v_final_norm_g):
    x0 = x[0]
    mem0 = mem[0]
    tgt = loss_target[0]
    gn = hgrn_norm_g[0]
    gfin = final_norm_g.reshape(1, D_MODEL)
    wp = w_pool[0]
    heads_2d = lambda w: w.reshape(D_MODEL // N_DEV, D_MODEL)
    xo_2d = lambda w: w.reshape(D_MODEL, D_MODEL // N_DEV)

    first = _all_gather_weights([w_in[0].T], [w_out[0], heads_2d(w_xq), heads_2d(w_xk), heads_2d(w_xv), xo_2d(w_xo).T,
                                              w_ff1[0], w_ff2[0]])
    win_t = first[0].reshape(IN_WIDTH, D_MODEL)
    ga_attn, ga_mlp = _gather_first_start([first[1:6], first[6:8]], "gather_first_start")

    z, h = _in_proj(x0, norm_mix_g, win_t, deps=[ga_attn[3]])
    mixed_a, o_pre, states = _hgrn_fwd(z, lb_logits, gn)
    lands = _split_wait(_gather_first_copies, ga_attn, o_pre, "gather_attn_first_wait")
    gb_attn = _gather_forward_start(lands, "gather_attn_forward_start")
    mixed, pooled = _pool_fwd(z, wp, pool_scale, mixed_a, deps=[gb_attn[3]])
    lands = _split_wait(_gather_forward_copies, gb_attn, pooled, "gather_attn_forward_wait")
    wout_f, wq_f, wk_f, wv_f, wo_t = (t.reshape(D_MODEL, D_MODEL) for t in lands)
    x1 = _out_proj(x0, mixed, wout_f)
    hm, xk, xv = _mem_kv(mem0, norm_mem_g, wk_f, wv_f, deps=[x1])
    lands = _split_wait(_gather_first_copies, ga_mlp, xk, "gather_mlp_first_wait")
    gb_mlp = _gather_forward_start(lands, "gather_mlp_forward_start")
    x2, hq, xq, att = _xattn_fwd(x1, norm_x_g, wq_f, xk, xv, wo_t, deps=[gb_mlp[3]])
    w1_b, w2_b = _split_wait(_gather_forward_copies, gb_mlp, x2, "gather_mlp_forward_wait")
    dx3, u, hf, slot_fin = _mlp_fwd_loss(x2, norm_ffn_g, w1_b, w2_b.reshape(D_FF, D_MODEL), gfin, tgt)

    rows = lambda t, r: t.reshape(N_DEV, r, D_MODEL)
    dw2 = _wgrad(u, dx3, "wgrad_ff2")
    ex_ff2 = _all_to_all_start([rows(dw2, FF_BLK)], [], "exchange_ff2_start")
    da, dx2, slot_ffn = _mlp_bwd(dx3, u, x2, norm_ffn_g, w1_b, w2_b, deps=[ex_ff2[3]])
    dw1 = _wgrad(hf, da, "wgrad_ff1", col_blocks=True)
    ex_ff1 = _all_to_all_start([dw1], [], "exchange_ff1_start")
    dx1, dxq, dxk, dxv, slot_x = _xattn_bwd(dx2, x1, norm_x_g, xq, xk, xv, wq_f, wo_t, deps=[ex_ff1[3]])
    dwo_t = _wgrad(dx2, att, "wgrad_xo")
    dwq = _wgrad(hq, dxq, "wgrad_xq")
    dwk, dwv, slot_mem = _mem_bwd(mem0, norm_mem_g, hm, dxk, dxv, wk_f, wv_f)
    ex_attn = _all_to_all_start([rows(dwq, 128), rows(dwk, 128), rows(dwv, 128), rows(dwo_t, 128)], [],
                                "exchange_attn_start")
    dwout = _wgrad(mixed, dx1, "wgrad_out")
    dz_pool, d_wpool, slot_ps = _pool_bwd(dx1, wout_f, pooled, wp, pool_scale, deps=[ex_attn[3]])
    small0 = jnp.concatenate([slot_x, slot_mem, slot_ffn, slot_fin, slot_ps], axis=0)
    ex_out = _all_to_all_start([rows(dwout, 128)], [small0, d_wpool], "exchange_out_start")
    dz, slot_lb, slot_gn = _hgrn_bwd(z, o_pre, dx1, wout_f, states, lb_logits, gn, dz_pool, deps=[ex_out[3]])
    dwin_t = _wgrad(dz, h, "wgrad_in")
    small1 = jnp.concatenate([slot_lb, slot_gn], axis=0)
    ex_in = _all_to_all_start([rows(dwin_t, 320)], [small1], "exchange_in_start")
    grad_x, slot_mix = _in_bwd(dz, win_t, x0, norm_mix_g, dx1, deps=[ex_in[3]])
    ex_mix = _all_to_all_start([], [slot_mix], "exchange_mix_start")

    out = {}
    (r_2,) = _split_wait(_all_to_all_copies(1), ex_ff2, ex_mix[3], "exchange_ff2_wait")
    out["w_ff2"] = _sum_adamw(r_2, w_ff2[0], m_w_ff2[0], v_w_ff2[0], "adamw_ff2")
    (r_1,) = _split_wait(_all_to_all_copies(1), ex_ff1, out["w_ff2"][1], "exchange_ff1_wait")
    out["w_ff1"] = _sum_adamw(r_1, w_ff1[0], m_w_ff1[0], v_w_ff1[0], "adamw_ff1")
    r_q, r_k, r_v, r_o = _split_wait(_all_to_all_copies(4), ex_attn, out["w_ff1"][1], "exchange_attn_wait")
    for n, r, (w, m, v) in (("w_xq", r_q, (w_xq, m_w_xq, v_w_xq)), ("w_xk", r_k, (w_xk, m_w_xk, v_w_xk)),
                            ("w_xv", r_v, (w_xv, m_w_xv, v_w_xv))):
        g = _sum_sources(r, "sum_grad_" + n).reshape(w.shape)
        out[n] = (g, *_adamw_whole(g, w, m, v, "adamw_" + n))
    g_xo = _sum_sources(r_o, "sum_grad_xo").T
    out["w_xo"] = (g_xo, *_adamw(g_xo, xo_2d(w_xo), xo_2d(m_w_xo), xo_2d(v_w_xo), "adamw_xo"))
    r_out, r_small0, r_wpool = _split_wait(_all_to_all_copies(1), ex_out, out["w_xo"][1], "exchange_out_wait")
    out["w_out"] = _sum_adamw(r_out, w_out[0], m_w_out[0], v_w_out[0], "adamw_out")
    r_in, r_small1 = _split_wait(_all_to_all_copies(1), ex_in, out["w_out"][1], "exchange_in_wait")
    g_in = _sum_sources(r_in, "sum_grad_in").T
    out["w_in"] = (g_in, *_adamw(g_in, w_in[0], m_w_in[0], v_w_in[0], "adamw_in"))
    (r_small2,) = _split_wait(_all_to_all_copies(0), ex_mix, out["w_in"][1], "exchange_mix_wait")
    row = lambda t: t.reshape(1, -1)
    small_params = {
        "norm_mix_g": (norm_mix_g, m_norm_mix_g, v_norm_mix_g),
        "lb_logits": (lb_logits, m_lb_logits, v_lb_logits),
        "hgrn_norm_g": (hgrn_norm_g[0], m_hgrn_norm_g[0], v_hgrn_norm_g[0]),
        "pool_scale": (pool_scale, m_pool_scale, v_pool_scale),
        "norm_x_g": (norm_x_g, m_norm_x_g, v_norm_x_g),
        "norm_mem_g": (norm_mem_g, m_norm_mem_g, v_norm_mem_g),
        "norm_ffn_g": (norm_ffn_g, m_norm_ffn_g, v_norm_ffn_g),
        "final_norm_g": (row(final_norm_g), row(m_final_norm_g), row(v_final_norm_g)),
        "w_pool": (wp, m_w_pool[0], v_w_pool[0]),
    }
    loss, small_out = _small_update([r_small0, r_small1, r_small2], r_wpool, small_params)
    out.update(small_out)

    shapes = dict(norm_mix_g=norm_mix_g, w_in=w_in, lb_logits=lb_logits, hgrn_norm_g=hgrn_norm_g, w_pool=w_pool,
                  pool_scale=pool_scale, w_out=w_out, norm_x_g=norm_x_g, norm_mem_g=norm_mem_g, w_xq=w_xq, w_xk=w_xk,
                  w_xv=w_xv, w_xo=w_xo, norm_ffn_g=norm_ffn_g, w_ff1=w_ff1, w_ff2=w_ff2, final_norm_g=final_norm_g)
    order = list(shapes)
    group = lambda k: [out[n][k].reshape(shapes[n].shape) for n in order]
    return (loss.reshape(()), grad_x.reshape(x.shape), *group(0), *group(1), *group(2), *group(3))
```

```python
import jax
import jax.numpy as jnp
from jax import lax
from jax.experimental import pallas as pl
from jax.experimental.pallas import tpu as pltpu

F32 = jnp.float32
BF16 = jnp.bfloat16

D_MODEL = 1024
N_DEV = 8
HEADS = 4
HD = 128
HW = HEADS * HD
IN_WIDTH = 5 * HW
XHD = 256
MEM_LEN = 256
D_FF = 4096
FF_BLK = D_FF // N_DEV
POOL_WINDOWS = (2, 4, 8, 16)
POOL_HALO = 16
CHUNK = 64
CHUNKS_PER_STEP = 4
SUB = 16
N_SUB = CHUNK // SUB
EXP_CAP = 80.0
EPS = 1e-6
TINY = 1e-30
ROW_TILE = 512
WIDE_ROW_TILE = 1024
SLOT = 8
V7X_VMEM_LIMIT = 56 * 1024 * 1024

ADAM_LR = 0.001
ADAM_B1 = 0.9
ADAM_B2 = 0.999
ADAM_EPS = 1e-08
ADAM_WD = 0.01
ADAM_STEP = 10

MESH_ID = pl.DeviceIdType.MESH


def _params(sem=None, vmem=V7X_VMEM_LIMIT):
    return pltpu.CompilerParams(dimension_semantics=sem, vmem_limit_bytes=vmem)


def _mm(a, b):
    return lax.dot_general(a.astype(BF16), b.astype(BF16), (((1,), (0,)), ((), ())), preferred_element_type=F32)


def _mm_nt(a, b):
    return lax.dot_general(a.astype(BF16), b.astype(BF16), (((1,), (1,)), ((), ())), preferred_element_type=F32)


def _mm_tn(a, b):
    return lax.dot_general(a.astype(BF16), b.astype(BF16), (((0,), (0,)), ((), ())), preferred_element_type=F32)


def _sigmoid(x):
    return 1.0 / (1.0 + jnp.exp(-x))


def _rms(x):
    return lax.rsqrt(jnp.mean(x * x, axis=-1, keepdims=True) + EPS)


def _rms_bwd(x, g, dh):
    r = _rms(x)
    n = x * r
    dn = dh * g
    dx = r * (dn - n * jnp.mean(dn * n, axis=-1, keepdims=True))
    return dx, jnp.sum(dh * n, axis=0, keepdims=True)


def _tri_dot(tri, x, passes):
    acc = None
    rest = x
    for _ in range(passes):
        piece = rest.astype(BF16)
        part = lax.dot_general(tri, piece, (((1,), (0,)), ((), ())), preferred_element_type=F32)
        acc = part if acc is None else acc + part
        rest = rest - piece.astype(F32)
    return acc


def _adam_update(g, w, m, v):
    nm = ADAM_B1 * m + (1.0 - ADAM_B1) * g
    nv = ADAM_B2 * v + (1.0 - ADAM_B2) * (g * g)
    m_hat = nm / (1.0 - ADAM_B1 ** ADAM_STEP)
    v_hat = nv / (1.0 - ADAM_B2 ** ADAM_STEP)
    return -ADAM_LR * (m_hat / (jnp.sqrt(v_hat) + ADAM_EPS) + ADAM_WD * w), nm, nv


def _full(shape):
    return pl.BlockSpec(shape, lambda *_: (0,) * len(shape))


VMEM_WHOLE = pl.BlockSpec(memory_space=pltpu.VMEM)
ANY_SPACE = pl.BlockSpec(memory_space=pl.ANY)


def _mesh_pos():
    return lax.axis_index("x"), lax.axis_index("y"), lax.axis_index("c")


def _flat(px, py, pc):
    return 4 * px + 2 * py + pc


def _all_gather_weights(shards, cast_only):
    n, nc = len(shards), len(cast_only)
    step = 64

    def body(*refs):
        x_refs, c_refs = refs[:n], refs[n:n + nc]
        out_refs, cast_refs = refs[n + nc:2 * n + nc], refs[2 * n + nc:2 * n + 2 * nc]
        bufs = refs[2 * n + 2 * nc:3 * n + 2 * nc]
        send_sems, recv_sems, local_sems = refs[3 * n + 2 * nc:]
        x, y, c = _mesh_pos()
        me, sibling = (x, y, c), (x, y, 1 - c)
        chips = [(1 - x, y), (x, 1 - y), (1 - x, 1 - y)]

        def copy(a, k, blk, to, src=None):
            rows = out_refs[a].at[_flat(*blk)]
            return pltpu.make_async_remote_copy(
                src_ref=rows if src is None else src, dst_ref=rows,
                send_sem=send_sems.at[7 * a + k], recv_sem=recv_sems.at[7 * a + k], device_id=to, device_id_type=MESH_ID)

        def cast_rows(src, dst, rows):
            def cast(i, carry):
                r0 = pl.multiple_of(i * step, step)
                dst[pl.ds(r0, step), :] = src[pl.ds(r0, step), :].astype(BF16)
                return carry
            lax.fori_loop(0, rows // step, cast, 0)

        first, mine = [], []
        for a in range(n):
            cast_rows(x_refs[a], bufs[a], shards[a].shape[0])
            mine.append(pltpu.make_async_copy(bufs[a], out_refs[a].at[_flat(*me)], local_sems.at[a]))
            first.append(copy(a, 0, me, sibling, src=bufs[a]))
            first += [copy(a, 1 + j, me, (*chip, c), src=bufs[a]) for j, chip in enumerate(chips)]
            for cp in [mine[-1]] + first[-4:]:
                cp.start()
        for a in range(nc):
            cast_rows(c_refs[a], cast_refs[a], cast_only[a].shape[0])
        passed = []
        for j, chip in enumerate(chips):
            for a in range(n):
                copy(a, 1 + j, (*chip, c), me).wait_recv()
                passed.append(copy(a, 4 + j, (*chip, c), sibling))
                passed[-1].start()
        for a in range(n):
            copy(a, 0, sibling, me).wait_recv()
            for j, chip in enumerate(chips):
                copy(a, 4 + j, (*chip, 1 - c), me).wait_recv()
        for cp in first + passed:
            cp.wait_send()
        for cp in mine:
            cp.wait()

    return pl.pallas_call(
        body, name="all_gather_w_in",
        out_shape=[jax.ShapeDtypeStruct((N_DEV,) + s.shape, BF16) for s in shards]
        + [jax.ShapeDtypeStruct(s.shape, BF16) for s in cast_only],
        in_specs=[VMEM_WHOLE] * (n + nc), out_specs=[ANY_SPACE] * n + [VMEM_WHOLE] * nc,
        scratch_shapes=[pltpu.VMEM(s.shape, BF16) for s in shards]
        + [pltpu.SemaphoreType.DMA((7 * n,)), pltpu.SemaphoreType.DMA((7 * n,)), pltpu.SemaphoreType.DMA((n,))],
        compiler_params=_params(),
    )(*shards, *cast_only)


HBM_SPEC = pl.BlockSpec(memory_space=pltpu.HBM)
SEM_SPEC = pl.BlockSpec(memory_space=pltpu.SEMAPHORE)
EFFECT = pltpu.SideEffectType.DATAFLOW_SIDE_EFFECTING
TOKEN = jax.ShapeDtypeStruct((8, 128), F32)


def _in_hbm(a):
    return pltpu.with_memory_space_constraint(a, pltpu.HBM)


def _split_start(copies_of, srcs, lands, n_sems, name):
    ns, nl, k = len(srcs), len(lands), len(n_sems)

    def body(*refs):
        src_refs, land_refs = refs[:ns], refs[ns:ns + nl]
        sems = refs[ns + nl:ns + nl + k]
        token = refs[-1]
        for cp in copies_of(src_refs, land_refs, sems):
            cp.start()
        token[...] = jnp.zeros_like(token)

    outs = pl.pallas_call(
        body, name=name,
        out_shape=[pltpu.SemaphoreType.DMA((q,)) for q in n_sems]
        + [pltpu.HBM(a.shape, a.dtype) for a in list(srcs) + list(lands)] + [TOKEN],
        in_specs=[HBM_SPEC] * (ns + nl),
        out_specs=[SEM_SPEC] * k + [HBM_SPEC] * (ns + nl) + [VMEM_WHOLE],
        input_output_aliases={i: k + i for i in range(ns + nl)},
        compiler_params=pltpu.CompilerParams(has_side_effects=EFFECT),
    )(*[_in_hbm(a) for a in list(srcs) + list(lands)])
    return outs[:k], outs[k:k + ns], outs[k + ns:k + ns + nl], outs[-1]


def _split_wait(copies_of, handle, after, name):
    sems, srcs, lands, _ = handle
    ns, nl, k = len(srcs), len(lands), len(sems)

    def body(*refs):
        src_refs, land_refs = refs[:ns], refs[ns:ns + nl]
        sem_refs = refs[ns + nl:ns + nl + k]
        for cp in copies_of(src_refs, land_refs, sem_refs):
            cp.wait()

    outs = pl.pallas_call(
        body, name=name,
        out_shape=[pltpu.HBM(a.shape, a.dtype) for a in list(srcs) + list(lands)],
        in_specs=[HBM_SPEC] * (ns + nl) + [SEM_SPEC] * k + [ANY_SPACE],
        out_specs=[HBM_SPEC] * (ns + nl),
        input_output_aliases={i: i for i in range(ns + nl)},
        compiler_params=pltpu.CompilerParams(has_side_effects=EFFECT),
    )(*srcs, *lands, *sems, after)
    return outs[ns:]


def _gather_first_copies(shard_refs, land_refs, sems):
    send_sems, recv_sems, local_sems = sems
    x, y, c = _mesh_pos()
    me = _flat(x, y, c)
    peers = [(x, y, 1 - c), (1 - x, y, c), (x, 1 - y, c), (1 - x, 1 - y, c)]
    copies = []
    for a, (shard, land) in enumerate(zip(shard_refs, land_refs)):
        copies.append(pltpu.make_async_copy(shard, land.at[me], local_sems.at[a]))
        for k, peer in enumerate(peers):
            copies.append(pltpu.make_async_remote_copy(
                src_ref=shard, dst_ref=land.at[me], send_sem=send_sems.at[4 * a + k], recv_sem=recv_sems.at[4 * a + k],
                device_id=peer, device_id_type=MESH_ID))
    return copies


def _gather_forward_copies(src_refs, land_refs, sems):
    del src_refs
    send_sems, recv_sems = sems
    x, y, c = _mesh_pos()
    chips = [(1 - x, y), (x, 1 - y), (1 - x, 1 - y)]
    copies = []
    for a, land in enumerate(land_refs):
        for j, chip in enumerate(chips):
            rows = land.at[_flat(*chip, c)]
            copies.append(pltpu.make_async_remote_copy(
                src_ref=rows, dst_ref=rows, send_sem=send_sems.at[3 * a + j], recv_sem=recv_sems.at[3 * a + j],
                device_id=(x, y, 1 - c), device_id_type=MESH_ID))
    return copies


def _gather_first_start(groups, name):
    shards = [s for g in groups for s in g]
    lands = [lax.empty((N_DEV,) + s.shape, s.dtype) for s in shards]
    bounds = [sum(len(g) for g in groups[:i]) for i in range(len(groups) + 1)]

    def copies_of(src_refs, land_refs, sems):
        copies = []
        for i in range(len(groups)):
            lo, hi = bounds[i], bounds[i + 1]
            copies += _gather_first_copies(src_refs[lo:hi], land_refs[lo:hi], sems[3 * i:3 * i + 3])
        return copies

    n_sems = tuple(q for g in groups for q in (4 * len(g), 4 * len(g), len(g)))
    sems, srcs, lands, token = _split_start(copies_of, shards, lands, n_sems, name)
    return [(sems[3 * i:3 * i + 3], srcs[bounds[i]:bounds[i + 1]], lands[bounds[i]:bounds[i + 1]], token)
            for i in range(len(groups))]


def _gather_forward_start(lands, name):
    n = len(lands)
    return _split_start(_gather_forward_copies, [], lands, (3 * n, 3 * n), name)


def _all_to_all_copies(n_scattered):
    def copies_of(src_refs, land_refs, sems):
        send_sems, recv_sems, local_sems = sems
        x, y, c = _mesh_pos()
        me = _flat(x, y, c)
        copies = []
        for a, (src, land) in enumerate(zip(src_refs, land_refs)):
            scattered = a < n_scattered
            copies.append(pltpu.make_async_copy(src.at[me] if scattered else src, land.at[me], local_sems.at[a]))
            for k in range(1, N_DEV):
                peer = (1 - x if k & 4 else x, 1 - y if k & 2 else y, 1 - c if k & 1 else c)
                copies.append(pltpu.make_async_remote_copy(
                    src_ref=src.at[_flat(*peer)] if scattered else src, dst_ref=land.at[me],
                    send_sem=send_sems.at[7 * a + k - 1], recv_sem=recv_sems.at[7 * a + k - 1],
                    device_id=peer, device_id_type=MESH_ID))
        return copies
    return copies_of


def _all_to_all_start(scattered, broadcast, name):
    srcs = list(scattered) + list(broadcast)
    lands = [lax.empty(a.shape, a.dtype) for a in scattered] + [lax.empty((N_DEV,) + a.shape, a.dtype) for a in broadcast]
    n = len(srcs)
    return _split_start(_all_to_all_copies(len(scattered)), srcs, lands, (7 * n, 7 * n, n), name)


def _call_behind(deps, body, *, in_specs, **kwargs):
    n_in, n_dep = len(in_specs), len(deps)

    def body_without_deps(*refs):
        return body(*refs[:n_in], *refs[n_in + n_dep:])

    call = pl.pallas_call(body_without_deps, in_specs=list(in_specs) + [ANY_SPACE] * n_dep, **kwargs)
    return lambda *operands: call(*operands, *deps)


def _row_tile(rows):
    for cand in (256, 128, 64, 32, 16):
        if rows % cand == 0:
            return cand
    return rows


def _sum_sources(recv, name):
    _, rows, cols = recv.shape
    tile = _row_tile(rows)

    def body(r_ref, o_ref):
        acc = r_ref[0].astype(F32)
        for d in range(1, N_DEV):
            acc = acc + r_ref[d].astype(F32)
        o_ref[...] = acc

    return pl.pallas_call(
        body, name=name, grid=(rows // tile,),
        out_shape=jax.ShapeDtypeStruct((rows, cols), F32),
        in_specs=[pl.BlockSpec((N_DEV, tile, cols), lambda i: (0, i, 0))],
        out_specs=pl.BlockSpec((tile, cols), lambda i: (i, 0)),
        compiler_params=_params(("parallel",)),
    )(recv)


def _adamw(g, w, m, v, name):
    rows, cols = g.shape
    tile = _row_tile(rows)

    def body(g_ref, w_ref, m_ref, v_ref, d_ref, nm_ref, nv_ref):
        d_ref[...], nm_ref[...], nv_ref[...] = _adam_update(g_ref[...], w_ref[...], m_ref[...], v_ref[...])

    spec = pl.BlockSpec((tile, cols), lambda i: (i, 0))
    shp = jax.ShapeDtypeStruct((rows, cols), F32)
    return pl.pallas_call(
        body, name=name, grid=(rows // tile,), out_shape=[shp, shp, shp],
        in_specs=[spec] * 4, out_specs=[spec] * 3,
        compiler_params=_params(("parallel",)),
    )(g, w, m, v)


def _adamw_whole(g, w, m, v, name):
    def body(g_ref, w_ref, m_ref, v_ref, d_ref, nm_ref, nv_ref):
        d_ref[...], nm_ref[...], nv_ref[...] = _adam_update(g_ref[...], w_ref[...], m_ref[...], v_ref[...])

    shp = jax.ShapeDtypeStruct(g.shape, F32)
    return pl.pallas_call(
        body, name=name, out_shape=[shp, shp, shp], in_specs=[VMEM_WHOLE] * 4, out_specs=[VMEM_WHOLE] * 3,
        compiler_params=_params(),
    )(g, w, m, v)


def _sum_adamw(recv, w, m, v, name):
    _, rows, cols = recv.shape
    tile = _row_tile(rows)

    def body(r_ref, w_ref, m_ref, v_ref, g_ref, d_ref, nm_ref, nv_ref):
        acc = r_ref[0].astype(F32)
        for d in range(1, N_DEV):
            acc = acc + r_ref[d].astype(F32)
        g_ref[...] = acc
        d_ref[...], nm_ref[...], nv_ref[...] = _adam_update(acc, w_ref[...], m_ref[...], v_ref[...])

    spec = pl.BlockSpec((tile, cols), lambda i: (i, 0))
    shp = jax.ShapeDtypeStruct((rows, cols), F32)
    return pl.pallas_call(
        body, name=name, grid=(rows // tile,), out_shape=[shp] * 4,
        in_specs=[pl.BlockSpec((N_DEV, tile, cols), lambda i: (0, i, 0)), spec, spec, spec], out_specs=[spec] * 4,
        compiler_params=_params(("parallel",)),
    )(recv, w, m, v)


SMALL_SLOTS = {"norm_x_g": (0, 0, 1, D_MODEL), "norm_mem_g": (0, 8, 1, D_MODEL), "norm_ffn_g": (0, 16, 1, D_MODEL),
               "final_norm_g": (0, 24, 1, D_MODEL), "pool_scale": (0, 32, 1, HW),
               "lb_logits": (1, 0, 2, HW), "hgrn_norm_g": (1, 8, HEADS, HD), "norm_mix_g": (2, 0, 1, D_MODEL)}
LOSS_ROW = 25
SMALL_ORDER = ("norm_mix_g", "lb_logits", "hgrn_norm_g", "pool_scale", "norm_x_g", "norm_mem_g", "norm_ffn_g",
               "final_norm_g", "w_pool")


def _small_update(srecvs, wprecv, params):
    flat = [t for n in SMALL_ORDER for t in params[n]]
    nb = len(srecvs)
    n_in = nb + 1 + len(flat)

    def body(*refs):
        s_refs, wp_ref = refs[0:nb], refs[nb]
        in_refs = refs[nb + 1:n_in]
        loss_ref = refs[n_in]
        out_refs = refs[n_in + 1:-nb]
        accs = refs[-nb:]
        for s_ref, acc in zip(s_refs, accs):
            total = s_ref[0]
            for d in range(1, N_DEV):
                total = total + s_ref[d]
            acc[...] = total
        loss_ref[...] = accs[0][LOSS_ROW:LOSS_ROW + 1, 0:1]
        for i, name in enumerate(SMALL_ORDER):
            w_ref, m_ref, v_ref = in_refs[3 * i:3 * i + 3]
            g_ref, d_ref, nm_ref, nv_ref = out_refs[4 * i:4 * i + 4]
            if name == "w_pool":
                g = wp_ref[0]
                for d in range(1, N_DEV):
                    g = g + wp_ref[d]
            else:
                buf, r0, nr, nc = SMALL_SLOTS[name]
                g = accs[buf][r0:r0 + nr, 0:nc]
            g_ref[...] = g
            d_ref[...], nm_ref[...], nv_ref[...] = _adam_update(g, w_ref[...], m_ref[...], v_ref[...])

    out_shape = [jax.ShapeDtypeStruct((1, 1), F32)]
    for n in SMALL_ORDER:
        out_shape += [jax.ShapeDtypeStruct(params[n][0].shape, F32)] * 4
    outs = pl.pallas_call(
        body, name="small_update", out_shape=out_shape,
        in_specs=[VMEM_WHOLE] * n_in, out_specs=[VMEM_WHOLE] * len(out_shape),
        scratch_shapes=[pltpu.VMEM(r.shape[1:], F32) for r in srecvs],
        compiler_params=_params(),
    )(*srecvs, wprecv, *flat)
    return outs[0], {n: outs[1 + 4 * i:5 + 4 * i] for i, n in enumerate(SMALL_ORDER)}


def _in_proj(x, g, w_t, deps):
    s = x.shape[0]
    tm = min(WIDE_ROW_TILE, s)

    def body(x_ref, g_ref, w_ref, z_ref, h_ref):
        xv = x_ref[...]
        h = (xv * _rms(xv) * g_ref[...]).astype(BF16)
        h_ref[...] = h
        z_ref[...] = _mm_nt(h, w_ref[...])

    return _call_behind(
        deps, body, name="in_proj", grid=(s // tm,),
        out_shape=[jax.ShapeDtypeStruct((s, IN_WIDTH), F32), jax.ShapeDtypeStruct((s, D_MODEL), BF16)],
        in_specs=[pl.BlockSpec((tm, D_MODEL), lambda i: (i, 0)), _full((1, D_MODEL)), VMEM_WHOLE],
        out_specs=[pl.BlockSpec((tm, IN_WIDTH), lambda i: (i, 0)), pl.BlockSpec((tm, D_MODEL), lambda i: (i, 0))],
        compiler_params=_params(("parallel",)),
    )(x, g, w_t)


def _chunk_masks():
    row = lax.broadcasted_iota(jnp.int32, (CHUNK, CHUNK), 0)
    col = lax.broadcasted_iota(jnp.int32, (CHUNK, CHUNK), 1)
    return row, col


def _ones_where(mask):
    return jnp.where(mask, 1.0, 0.0).astype(BF16)


def _hgrn_gates(zq, zf, lb):
    sq = _sigmoid(zq)
    sig = _sigmoid(zf)
    f = lb + (1.0 - lb) * sig
    return zq * sq, sq, sig, f


def _sub_chunk_masks(width):
    trow = lax.broadcasted_iota(jnp.int32, (CHUNK, width), 0)
    return [(trow >= SUB * j) & (trow < SUB * (j + 1)) for j in range(N_SUB)]


def _head(a, h):
    return a[:, HD * h:HD * (h + 1)]


def _lanes(parts):
    return jnp.concatenate(parts, axis=1)


def _hgrn_decay_factors(b_scr, r0, b, in_sub):
    bases = [jnp.zeros((1, HW), F32)] + [b_scr[r0 + SUB * j - 1:r0 + SUB * j, :] for j in range(1, N_SUB)]
    own_base = bases[N_SUB - 1]
    for j in range(N_SUB - 2, -1, -1):
        own_base = jnp.where(in_sub[j], bases[j], own_base)
    eq = jnp.exp(b - own_base)
    ek = []
    for j in range(N_SUB):
        upto = SUB * (j + 1)
        e = jnp.exp(jnp.minimum(bases[j] - b[0:upto], EXP_CAP))
        ek.append(e if upto == CHUNK else jnp.concatenate([e, jnp.zeros((CHUNK - upto, HW), F32)], axis=0))
    return eq, ek


def _per_sub_chunk(x, in_sub):
    return _lanes([jnp.where(in_sub[j], x, 0.0) for j in range(N_SUB)])


def _own_lane_block(a, in_sub):
    out = a[:, HD * (N_SUB - 1):HD * N_SUB]
    for j in range(N_SUB - 2, -1, -1):
        out = jnp.where(in_sub[j], a[:, HD * j:HD * (j + 1)], out)
    return out


def _head_rms(o):
    return _lanes([jnp.broadcast_to(_rms(_head(o, h)), (CHUNK, HD)) for h in range(HEADS)])


def _head_mean(a):
    return _lanes([jnp.broadcast_to(jnp.mean(_head(a, h), axis=-1, keepdims=True), (CHUNK, HD)) for h in range(HEADS)])


def _hgrn_fwd(z, lb_logits, gn):
    s = z.shape[0]
    n_chunks = s // CHUNK

    def body(zq_ref, zf_ref, zi_ref, zg_ref, lbl_ref, gn_ref, oa_ref, o_ref, st_ref, state, b_scr):
        @pl.when(pl.program_id(0) == 0)
        def _():
            state[...] = jnp.zeros_like(state)

        lb = _sigmoid(lbl_ref[0:1, :] - lbl_ref[1:2, :])
        row, col = _chunk_masks()
        causal = col <= row
        tri = _ones_where(causal)
        in_sub, in_sub_head = _sub_chunk_masks(HW), _sub_chunk_masks(HD)
        gn_row = _lanes([gn_ref[h:h + 1, :] for h in range(HEADS)])
        for c in range(CHUNKS_PER_STEP):
            r0 = CHUNK * c
            rs = slice(r0, r0 + CHUNK)
            st_ref[c] = state[...]
            q, _, _, f = _hgrn_gates(zq_ref[rs, :], zf_ref[rs, :], lb)
            kk = 1.0 - f
            b = _tri_dot(tri, jnp.log(f), 3)
            b_scr[rs, :] = b
            eq, ek = _hgrn_decay_factors(b_scr, r0, b, in_sub)
            b_last = b_scr[r0 + CHUNK - 1:r0 + CHUNK, :]
            qe, qg = q * eq, q * jnp.exp(b)
            ke = [kk * e for e in ek]
            kd = kk * jnp.exp(b_last - b)
            lam_last = jnp.exp(b_last)
            v = zi_ref[rs, :]
            o_heads = []
            for h in range(HEADS):
                vh, st = _head(v, h), state[h]
                a = jnp.where(causal, _mm_nt(_per_sub_chunk(_head(qe, h), in_sub_head),
                                             _lanes([_head(ke[j], h) for j in range(N_SUB)])), 0.0)
                o_heads.append(_mm(a, vh) + _mm_nt(_head(qg, h), st))
                state[h] = st * _head(lam_last, h) + _mm_tn(vh, _head(kd, h))
            o = _lanes(o_heads)
            o_ref[rs, :] = o
            zg = zg_ref[rs, :]
            oa_ref[rs, :] = (o * _head_rms(o) * gn_row * zg * _sigmoid(zg)).astype(BF16)

    rows = CHUNK * CHUNKS_PER_STEP
    zspec = lambda cb: pl.BlockSpec((rows, HW), lambda i, cb=cb: (i, cb))
    return pl.pallas_call(
        body, name="hgrn_fwd", grid=(s // rows,),
        out_shape=[jax.ShapeDtypeStruct((s, 2 * HW), BF16), jax.ShapeDtypeStruct((s, HW), F32),
                   jax.ShapeDtypeStruct((n_chunks, HEADS, HD, HD), F32)],
        in_specs=[zspec(0), zspec(1), zspec(2), zspec(3), _full((2, HW)), _full((HEADS, HD))],
        out_specs=[pl.BlockSpec((rows, HW), lambda i: (i, 0)), pl.BlockSpec((rows, HW), lambda i: (i, 0)),
                   pl.BlockSpec((CHUNKS_PER_STEP, HEADS, HD, HD), lambda i: (i, 0, 0, 0))],
        scratch_shapes=[pltpu.VMEM((HEADS, HD, HD), F32), pltpu.VMEM((rows, HW), F32)],
        compiler_params=_params(("arbitrary",)),
    )(z, z, z, z, lb_logits, gn)


def _pool_counts(tile_idx, tm):
    t = tile_idx * tm + lax.broadcasted_iota(jnp.int32, (tm, 1), 0)
    return [1.0 / jnp.minimum(t + 1, w).astype(F32) for w in POOL_WINDOWS]


def _pool_fwd(z, w_pool, scale, mixed_in, deps):
    s = z.shape[0]
    tm = min(ROW_TILE, s)

    def body(p_ref, w_ref, sc_ref, mixin_ref, ob_ref, pooled_ref, ext):
        i = pl.program_id(0)

        @pl.when(i == 0)
        def _():
            ext[0:POOL_HALO, :] = jnp.zeros((POOL_HALO, HW), F32)

        @pl.when(i > 0)
        def _():
            ext[0:POOL_HALO, :] = ext[tm:tm + POOL_HALO, :]

        ext[POOL_HALO:POOL_HALO + tm, :] = p_ref[...]
        inv = _pool_counts(i, tm)
        for g, w in enumerate(POOL_WINDOWS):
            sl = slice(HD * g, HD * (g + 1))
            p = ext[POOL_HALO:POOL_HALO + tm, sl]
            win = p
            for d in range(1, w):
                win = win + ext[POOL_HALO - d:POOL_HALO - d + tm, sl]
            pooled = (win * inv[g] - p).astype(BF16)
            pooled_ref[:, sl] = pooled
            ob_ref[:, sl] = (_mm(pooled, w_ref[g]) * sc_ref[:, sl]).astype(BF16)

    return _call_behind(
        deps, body, name="pool_fwd", grid=(s // tm,),
        out_shape=[jax.ShapeDtypeStruct((s, 2 * HW), BF16), jax.ShapeDtypeStruct((s, HW), BF16)],
        in_specs=[pl.BlockSpec((tm, HW), lambda i: (i, 4)), _full((HEADS, HD, HD)), _full((1, HW)), ANY_SPACE],
        out_specs=[pl.BlockSpec((tm, HW), lambda i: (i, 1)), pl.BlockSpec((tm, HW), lambda i: (i, 0))],
        scratch_shapes=[pltpu.VMEM((tm + POOL_HALO, HW), F32)],
        input_output_aliases={3: 0},
        compiler_params=_params(("arbitrary",)),
    )(z, w_pool, scale, mixed_in)


def _out_proj(x, mixed, w_out):
    s = x.shape[0]
    tm = min(WIDE_ROW_TILE, s)

    def body(x_ref, a_ref, w_ref, o_ref):
        o_ref[...] = x_ref[...] + _mm(a_ref[...], w_ref[...])

    return pl.pallas_call(
        body, name="out_proj", grid=(s // tm,),
        out_shape=jax.ShapeDtypeStruct((s, D_MODEL), F32),
        in_specs=[pl.BlockSpec((tm, D_MODEL), lambda i: (i, 0)), pl.BlockSpec((tm, D_MODEL), lambda i: (i, 0)), VMEM_WHOLE],
        out_specs=pl.BlockSpec((tm, D_MODEL), lambda i: (i, 0)),
        compiler_params=_params(("parallel",)),
    )(x, mixed, w_out)


def _mem_kv(mem, g, wk, wv, deps):
    def body(m_ref, g_ref, wk_ref, wv_ref, hm_ref, k_ref, v_ref):
        m = m_ref[...]
        hm = (m * _rms(m) * g_ref[...]).astype(BF16)
        hm_ref[...] = hm
        k_ref[...] = _mm(hm, wk_ref[...]).astype(BF16)
        v_ref[...] = _mm(hm, wv_ref[...]).astype(BF16)

    shp = jax.ShapeDtypeStruct((MEM_LEN, D_MODEL), BF16)
    return _call_behind(
        deps, body, name="mem_kv", out_shape=[shp, shp, shp],
        in_specs=[VMEM_WHOLE] * 4, out_specs=[VMEM_WHOLE] * 3,
        compiler_params=_params(),
    )(mem, g, wk, wv)


def _softmax_rows(sc):
    e = jnp.exp(sc - jnp.max(sc, axis=-1, keepdims=True))
    return e / jnp.sum(e, axis=-1, keepdims=True)


def _xattn_fwd(x, g, wq, xk, xv, wo_t, deps):
    s = x.shape[0]
    tm = min(ROW_TILE, s)
    scale = XHD ** -0.5

    def body(x_ref, g_ref, wq_ref, k_ref, v_ref, wo_ref, o_ref, hq_ref, q_ref, att_ref):
        xv_ = x_ref[...]
        hq = (xv_ * _rms(xv_) * g_ref[...]).astype(BF16)
        hq_ref[...] = hq
        q_ref[...] = (_mm(hq, wq_ref[...]) * scale).astype(BF16)
        for h in range(HEADS):
            sl = slice(XHD * h, XHD * (h + 1))
            p = _softmax_rows(_mm_nt(q_ref[:, sl], k_ref[:, sl]))
            att_ref[:, sl] = _mm(p, v_ref[:, sl]).astype(BF16)
        o_ref[...] = xv_ + _mm_nt(att_ref[...], wo_ref[...])

    row_f32 = pl.BlockSpec((tm, D_MODEL), lambda i: (i, 0))
    bshape = jax.ShapeDtypeStruct((s, D_MODEL), BF16)
    return _call_behind(
        deps, body, name="xattn_fwd", grid=(s // tm,),
        out_shape=[jax.ShapeDtypeStruct((s, D_MODEL), F32), bshape, bshape, bshape],
        in_specs=[row_f32, _full((1, D_MODEL)), VMEM_WHOLE, VMEM_WHOLE, VMEM_WHOLE, VMEM_WHOLE],
        out_specs=[row_f32] * 4,
        compiler_params=_params(("parallel",)),
    )(x, g, wq, xk, xv, wo_t)


def _mlp_fwd_loss(x, g, w1, w2, gf, target):
    s = x.shape[0]
    tm = min(ROW_TILE, s)

    def body(x_ref, g_ref, w1_ref, w2_ref, gf_ref, t_ref, dx_ref, u_ref, hf_ref, slot_ref):
        @pl.when(pl.program_id(0) == 0)
        def _():
            slot_ref[...] = jnp.zeros_like(slot_ref)

        xv = x_ref[...]
        hf = (xv * _rms(xv) * g_ref[...]).astype(BF16)
        hf_ref[...] = hf
        for j in range(N_DEV):
            a = jnp.maximum(_mm(hf, w1_ref[j]), 0.0)
            u_ref[:, FF_BLK * j:FF_BLK * (j + 1)] = (a * a).astype(BF16)
        acc = xv + _mm(u_ref[...], w2_ref[...])
        gfv = gf_ref[...]
        r = _rms(acc)
        n = acc * r
        err = n * gfv - t_ref[...]
        slot_ref[1:2, :] += jnp.sum(jnp.mean(err * err, axis=-1, keepdims=True), axis=0, keepdims=True) * 0.5
        dy = err * (1.0 / D_MODEL)
        slot_ref[0:1, :] += jnp.sum(dy * n, axis=0, keepdims=True)
        dn = dy * gfv
        dx_ref[...] = r * (dn - n * jnp.mean(dn * n, axis=-1, keepdims=True))

    row_f32 = pl.BlockSpec((tm, D_MODEL), lambda i: (i, 0))
    return pl.pallas_call(
        body, name="mlp_fwd_loss", grid=(s // tm,),
        out_shape=[jax.ShapeDtypeStruct((s, D_MODEL), F32), jax.ShapeDtypeStruct((s, D_FF), BF16),
                   jax.ShapeDtypeStruct((s, D_MODEL), BF16), jax.ShapeDtypeStruct((SLOT, D_MODEL), F32)],
        in_specs=[row_f32, _full((1, D_MODEL)), VMEM_WHOLE, VMEM_WHOLE, _full((1, D_MODEL)), row_f32],
        out_specs=[row_f32, pl.BlockSpec((tm, D_FF), lambda i: (i, 0)), row_f32, _full((SLOT, D_MODEL))],
        compiler_params=_params(("arbitrary",)),
    )(x, g, w1, w2, gf, target)


def _zero_slot(slot_ref):
    @pl.when(pl.program_id(0) == 0)
    def _():
        slot_ref[...] = jnp.zeros_like(slot_ref)


def _mlp_bwd(dx3, u, x2, g, w1, w2, deps):
    s = x2.shape[0]
    tm = min(ROW_TILE, s)

    def body(d_ref, u_ref, x_ref, g_ref, w1_ref, w2_ref, da_ref, dx_ref, slot_ref):
        _zero_slot(slot_ref)
        d = d_ref[...]
        d16 = d.astype(BF16)
        dhf = jnp.zeros((tm, D_MODEL), F32)
        for j in range(N_DEV):
            sl = slice(FF_BLK * j, FF_BLK * (j + 1))
            u = u_ref[:, sl].astype(F32)
            da = (_mm_nt(d16, w2_ref[j]) * (2.0 * u * lax.rsqrt(jnp.maximum(u, TINY)))).astype(BF16)
            da_ref[:, sl] = da
            dhf = dhf + _mm_nt(da, w1_ref[j])
        dx, dg = _rms_bwd(x_ref[...], g_ref[...], dhf)
        dx_ref[...] = d + dx
        slot_ref[0:1, :] += dg

    row_f32 = pl.BlockSpec((tm, D_MODEL), lambda i: (i, 0))
    return _call_behind(
        deps, body, name="mlp_bwd", grid=(s // tm,),
        out_shape=[jax.ShapeDtypeStruct((s, D_FF), BF16), jax.ShapeDtypeStruct((s, D_MODEL), F32),
                   jax.ShapeDtypeStruct((SLOT, D_MODEL), F32)],
        in_specs=[row_f32, pl.BlockSpec((tm, D_FF), lambda i: (i, 0)), row_f32, _full((1, D_MODEL)),
                  VMEM_WHOLE, VMEM_WHOLE],
        out_specs=[pl.BlockSpec((tm, D_FF), lambda i: (i, 0)), row_f32, _full((SLOT, D_MODEL))],
        compiler_params=_params(("arbitrary",)),
    )(dx3, u, x2, g, w1, w2)


def _wgrad(a, b, name, col_blocks=False):
    s, m = a.shape
    n = b.shape[1]
    tm = 1280 if m % 1280 == 0 else min(1024, m)
    tn = min(1024, n)
    blk = n // N_DEV
    per_step = tn // blk if col_blocks else 1
    ts = min(2 * ROW_TILE, s)
    n_s = s // ts

    def body(a_ref, b_ref, o_ref, acc):
        k = pl.program_id(2)

        @pl.when(k == 0)
        def _():
            acc[...] = jnp.zeros_like(acc)

        acc[...] += _mm_tn(a_ref[...], b_ref[...])

        @pl.when(k == n_s - 1)
        def _():
            if col_blocks:
                for p in range(per_step):
                    o_ref[p] = acc[:, blk * p:blk * (p + 1)].astype(BF16)
            else:
                o_ref[...] = acc[...].astype(BF16)

    if col_blocks:
        out_shape = jax.ShapeDtypeStruct((N_DEV, m, blk), BF16)
        out_spec = pl.BlockSpec((per_step, tm, blk), lambda i, j, k: (j, i, 0))
    else:
        out_shape = jax.ShapeDtypeStruct((m, n), BF16)
        out_spec = pl.BlockSpec((tm, tn), lambda i, j, k: (i, j))
    return pl.pallas_call(
        body, name=name, grid=(m // tm, n // tn, n_s), out_shape=out_shape,
        in_specs=[pl.BlockSpec((ts, tm), lambda i, j, k: (k, i)), pl.BlockSpec((ts, tn), lambda i, j, k: (k, j))],
        out_specs=out_spec,
        scratch_shapes=[pltpu.VMEM((tm, tn), F32)],
        compiler_params=_params(("parallel", "parallel", "arbitrary")),
    )(a, b)


def _xattn_bwd(dx2, x1, g, q, xk, xv, wq, wo_t, deps):
    s = x1.shape[0]
    tm = min(ROW_TILE, s)
    scale = XHD ** -0.5

    def body(d_ref, x_ref, g_ref, q_ref, k_ref, v_ref, wq_ref, wo_ref, dx_ref, dq_ref, dk_ref, dv_ref, slot_ref, datt):
        _zero_slot(slot_ref)

        @pl.when(pl.program_id(0) == 0)
        def _():
            dk_ref[...] = jnp.zeros_like(dk_ref)
            dv_ref[...] = jnp.zeros_like(dv_ref)

        d = d_ref[...]
        datt[...] = _mm(d, wo_ref[...]).astype(BF16)
        for h in range(HEADS):
            sl = slice(XHD * h, XHD * (h + 1))
            qh, kh, vh, dah = q_ref[:, sl], k_ref[:, sl], v_ref[:, sl], datt[:, sl]
            p = _softmax_rows(_mm_nt(qh, kh))
            dp = _mm_nt(dah, vh)
            ds = (p * (dp - jnp.sum(dp * p, axis=-1, keepdims=True))).astype(BF16)
            dq_ref[:, sl] = (_mm(ds, kh) * scale).astype(BF16)
            dk_ref[:, sl] += _mm_tn(ds, qh)
            dv_ref[:, sl] += _mm_tn(p, dah)
        dx, dg = _rms_bwd(x_ref[...], g_ref[...], _mm_nt(dq_ref[...], wq_ref[...]))
        dx_ref[...] = d + dx
        slot_ref[0:1, :] += dg

    row_f32 = pl.BlockSpec((tm, D_MODEL), lambda i: (i, 0))
    kv = jax.ShapeDtypeStruct((MEM_LEN, D_MODEL), F32)
    return _call_behind(
        deps, body, name="xattn_bwd", grid=(s // tm,),
        out_shape=[jax.ShapeDtypeStruct((s, D_MODEL), F32), jax.ShapeDtypeStruct((s, D_MODEL), BF16), kv, kv,
                   jax.ShapeDtypeStruct((SLOT, D_MODEL), F32)],
        in_specs=[row_f32, row_f32, _full((1, D_MODEL)), row_f32, VMEM_WHOLE, VMEM_WHOLE, VMEM_WHOLE, VMEM_WHOLE],
        out_specs=[row_f32, row_f32, _full((MEM_LEN, D_MODEL)), _full((MEM_LEN, D_MODEL)), _full((SLOT, D_MODEL))],
        scratch_shapes=[pltpu.VMEM((tm, D_MODEL), BF16)],
        compiler_params=_params(("arbitrary",)),
    )(dx2, x1, g, q, xk, xv, wq, wo_t)


def _mem_bwd(mem, g, hm, dxk, dxv, wk, wv):
    def body(m_ref, g_ref, hm_ref, dk_ref, dv_ref, wk_ref, wv_ref, dwk_ref, dwv_ref, slot_ref):
        dk, dv = dk_ref[...], dv_ref[...]
        hm_ = hm_ref[...]
        dwk_ref[...] = _mm_tn(hm_, dk).astype(BF16)
        dwv_ref[...] = _mm_tn(hm_, dv).astype(BF16)
        _, dg = _rms_bwd(m_ref[...], g_ref[...], _mm_nt(dk, wk_ref[...]) + _mm_nt(dv, wv_ref[...]))
        slot_ref[...] = jnp.zeros_like(slot_ref)
        slot_ref[0:1, :] = dg

    wshape = jax.ShapeDtypeStruct((D_MODEL, D_MODEL), BF16)
    return pl.pallas_call(
        body, name="mem_bwd", out_shape=[wshape, wshape, jax.ShapeDtypeStruct((SLOT, D_MODEL), F32)],
        in_specs=[VMEM_WHOLE] * 7, out_specs=[VMEM_WHOLE] * 3,
        compiler_params=_params(),
    )(mem, g, hm, dxk, dxv, wk, wv)


def _pool_bwd(dx1, w_out, pooled, w_pool, scale, deps):
    s = dx1.shape[0]
    tm = min(ROW_TILE, s)
    n_t = s // tm

    def body(dx_ref, wo_ref, pl_ref, w_ref, sc_ref, dz_ref, dw_ref, slot_ref, ext, do_ref):
        i = pl.program_id(0)
        tile = n_t - 1 - i
        _zero_slot(slot_ref)
        do_ref[...] = _mm_nt(dx_ref[...], wo_ref[HW:2 * HW, :])

        @pl.when(i == 0)
        def _():
            dw_ref[...] = jnp.zeros_like(dw_ref)
            ext[tm:tm + POOL_HALO, :] = jnp.zeros((POOL_HALO, HW), F32)

        @pl.when(i > 0)
        def _():
            ext[tm:tm + POOL_HALO, :] = ext[0:POOL_HALO, :]

        inv = _pool_counts(tile, tm)
        dpooled = []
        for g in range(HEADS):
            sl = slice(HD * g, HD * (g + 1))
            pooled_g = pl_ref[:, sl]
            do = do_ref[:, sl]
            slot_ref[0:1, sl] += jnp.sum(_mm(pooled_g, w_ref[g]) * do, axis=0, keepdims=True)
            dy = (do * sc_ref[:, sl]).astype(BF16)
            dw_ref[g] += _mm_tn(pooled_g, dy)
            dpo = _mm_nt(dy, w_ref[g])
            dpooled.append(dpo)
            ext[0:tm, sl] = dpo * inv[g]
        for g, w in enumerate(POOL_WINDOWS):
            sl = slice(HD * g, HD * (g + 1))
            win = ext[0:tm, sl]
            for d in range(1, w):
                win = win + ext[d:d + tm, sl]
            dz_ref[:, sl] = win - dpooled[g]

    return _call_behind(
        deps, body, name="pool_bwd", grid=(n_t,),
        out_shape=[jax.ShapeDtypeStruct((s, IN_WIDTH), F32), jax.ShapeDtypeStruct((HEADS, HD, HD), F32),
                   jax.ShapeDtypeStruct((SLOT, D_MODEL), F32)],
        in_specs=[pl.BlockSpec((tm, D_MODEL), lambda i: (n_t - 1 - i, 0)), VMEM_WHOLE,
                  pl.BlockSpec((tm, HW), lambda i: (n_t - 1 - i, 0)), _full((HEADS, HD, HD)), _full((1, HW))],
        out_specs=[pl.BlockSpec((tm, HW), lambda i: (n_t - 1 - i, 4)), _full((HEADS, HD, HD)), _full((SLOT, D_MODEL))],
        scratch_shapes=[pltpu.VMEM((tm + POOL_HALO, HW), F32), pltpu.VMEM((tm, HW), F32)],
        compiler_params=_params(("arbitrary",)),
    )(dx1, w_out, pooled, w_pool, scale)


def _hgrn_bwd(z, o, dx1, w_out, states, lb_logits, gn, dz_in, deps):
    s = z.shape[0]
    n_chunks = s // CHUNK

    def body(zq_ref, zf_ref, zi_ref, zg_ref, o_ref, dx_ref, wo_ref, st_ref, lbl_ref, gn_ref, dzin_ref,
             dz_ref, dlb_ref, dgn_ref, dstate, b_scr, dlb_acc, do_ref):
        i = pl.program_id(0)

        @pl.when(i == 0)
        def _():
            dstate[...] = jnp.zeros_like(dstate)
            dlb_acc[...] = jnp.zeros_like(dlb_acc)
            dgn_ref[...] = jnp.zeros_like(dgn_ref)
            dlb_ref[...] = jnp.zeros_like(dlb_ref)

        do_ref[...] = _mm_nt(dx_ref[...], wo_ref[0:HW, :])
        lb = _sigmoid(lbl_ref[0:1, :] - lbl_ref[1:2, :])
        row, col = _chunk_masks()
        causal = col <= row
        tri = _ones_where(causal)
        upper = _ones_where(col >= row)
        strict_lower = _ones_where(col < row)
        in_sub, in_sub_head = _sub_chunk_masks(HW), _sub_chunk_masks(HD)
        gn_row = _lanes([gn_ref[h:h + 1, :] for h in range(HEADS)])
        dlb_sum, dgn_sum = 0.0, 0.0
        for c in reversed(range(CHUNKS_PER_STEP)):
            r0 = CHUNK * c
            rs = slice(r0, r0 + CHUNK)
            zq = zq_ref[rs, :]
            q, sq, sig, f = _hgrn_gates(zq, zf_ref[rs, :], lb)
            kk = 1.0 - f
            b = _tri_dot(tri, jnp.log(f), 3)
            b_scr[rs, :] = b
            v = zi_ref[rs, :]
            o, zg, doa = o_ref[rs, :], zg_ref[rs, :], do_ref[rs, :]
            sg = _sigmoid(zg)
            n = o * _head_rms(o)
            don = doa * (zg * sg)
            dgn_sum = dgn_sum + jnp.sum(don * n, axis=0, keepdims=True)
            dn = don * gn_row
            d_o = _head_rms(o) * (dn - n * _head_mean(dn * n))
            dz_ref[rs, 3 * HW:4 * HW] = doa * (n * gn_row) * (sg * (1.0 + zg * (1.0 - sg)))
            eq, ek = _hgrn_decay_factors(b_scr, r0, b, in_sub)
            b_last = b_scr[r0 + CHUNK - 1:r0 + CHUNK, :]
            lam, e_last, lam_last = jnp.exp(b), jnp.exp(b_last - b), jnp.exp(b_last)
            qe, qg, kd = q * eq, q * lam, kk * e_last
            ke = [kk * e for e in ek]
            dv_h, gq_h, gk_h, dqi_h, dkd_h, st_h = [], [], [], [], [], []
            for h in range(HEADS):
                vh, doh = _head(v, h), _head(d_o, h)
                st0, ds1 = st_ref[c, h], dstate[h]
                q16 = _per_sub_chunk(_head(qe, h), in_sub_head).astype(BF16)
                ke16 = _lanes([_head(ke[j], h) for j in range(N_SUB)]).astype(BF16)
                a = jnp.where(causal, _mm_nt(q16, ke16), 0.0)
                da = jnp.where(causal, _mm_nt(doh, vh), 0.0)
                dv_h.append(_mm_tn(a, doh) + _mm_nt(_head(kd, h), ds1))
                gq_h.append(_own_lane_block(_mm(da, ke16), in_sub_head))
                gk_h.append(_mm_tn(da, q16))
                dqi_h.append(_mm(doh, st0))
                dkd_h.append(_mm(vh, ds1))
                st_h.append(jnp.sum(st0 * ds1, axis=0, keepdims=True))
                dstate[h] = ds1 * _head(lam_last, h) + _mm_tn(doh, _head(qg, h))
            dz_ref[rs, 2 * HW:3 * HW] = _lanes(dv_h)
            gq = _lanes(gq_h)
            gk = [_lanes([gk_h[h][:, HD * j:HD * (j + 1)] for h in range(HEADS)]) for j in range(N_SUB)]
            dq_inter = lam * _lanes(dqi_h)
            dq = eq * gq + dq_inter
            dk_intra = sum(ek[j] * gk[j] for j in range(N_SUB))
            dk_state = _lanes(dkd_h) * e_last
            db_intra = (qe.astype(BF16).astype(F32) * gq
                        - sum(ke[j].astype(BF16).astype(F32) * gk[j] for j in range(N_SUB)))
            dlf = (_tri_dot(upper, db_intra + q * dq_inter, 2) + _tri_dot(strict_lower, kk * dk_state, 2)
                   + lam_last * _lanes(st_h))
            df = dlf / f - (dk_intra + dk_state)
            dlb_sum = dlb_sum + jnp.sum(df * (1.0 - sig), axis=0, keepdims=True)
            dz_ref[rs, HW:2 * HW] = df * (1.0 - lb) * sig * (1.0 - sig)
            dz_ref[rs, 0:HW] = dq * (sq * (1.0 + zq * (1.0 - sq)))
        dlb_acc[...] += dlb_sum
        for h in range(HEADS):
            dgn_ref[h:h + 1, 0:HD] += _head(dgn_sum, h)

        @pl.when(i == n_steps - 1)
        def _():
            dl0 = dlb_acc[...] * lb * (1.0 - lb)
            dlb_ref[0:1, 0:HW] = dl0
            dlb_ref[1:2, 0:HW] = -dl0

    rows = CHUNK * CHUNKS_PER_STEP
    n_steps = s // rows
    rev = lambda i: n_steps - 1 - i
    zspec = lambda cb: pl.BlockSpec((rows, HW), lambda i, cb=cb: (rev(i), cb))
    slot = jax.ShapeDtypeStruct((SLOT, D_MODEL), F32)
    return _call_behind(
        deps, body, name="hgrn_bwd", grid=(n_steps,),
        out_shape=[jax.ShapeDtypeStruct((s, IN_WIDTH), F32), slot, slot],
        in_specs=[zspec(0), zspec(1), zspec(2), zspec(3), pl.BlockSpec((rows, HW), lambda i: (rev(i), 0)),
                  pl.BlockSpec((rows, D_MODEL), lambda i: (rev(i), 0)), VMEM_WHOLE,
                  pl.BlockSpec((CHUNKS_PER_STEP, HEADS, HD, HD), lambda i: (rev(i), 0, 0, 0)), _full((2, HW)),
                  _full((HEADS, HD)), ANY_SPACE],
        out_specs=[pl.BlockSpec((rows, 4 * HW), lambda i: (rev(i), 0)), _full((SLOT, D_MODEL)), _full((SLOT, D_MODEL))],
        scratch_shapes=[pltpu.VMEM((HEADS, HD, HD), F32), pltpu.VMEM((rows, HW), F32), pltpu.VMEM((1, HW), F32),
                        pltpu.VMEM((rows, HW), F32)],
        input_output_aliases={10: 0},
        compiler_params=_params(("arbitrary",)),
    )(z, z, z, z, o, dx1, w_out, states, lb_logits, gn, dz_in)


def _in_bwd(dz, w_t, x0, g, dx1, deps):
    s = x0.shape[0]
    tm = min(ROW_TILE, s)

    def body(dz_ref, w_ref, x_ref, g_ref, d_ref, dx_ref, slot_ref):
        _zero_slot(slot_ref)
        dx, dg = _rms_bwd(x_ref[...], g_ref[...], _mm(dz_ref[...], w_ref[...]))
        dx_ref[...] = d_ref[...] + dx
        slot_ref[0:1, :] += dg

    row_f32 = pl.BlockSpec((tm, D_MODEL), lambda i: (i, 0))
    return _call_behind(
        deps, body, name="in_bwd", grid=(s // tm,),
        out_shape=[jax.ShapeDtypeStruct((s, D_MODEL), F32), jax.ShapeDtypeStruct((SLOT, D_MODEL), F32)],
        in_specs=[pl.BlockSpec((tm, IN_WIDTH), lambda i: (i, 0)), VMEM_WHOLE, row_f32, _full((1, D_MODEL)), row_f32],
        out_specs=[row_f32, _full((SLOT, D_MODEL))],
        compiler_params=_params(("arbitrary",)),
    )(dz, w_t, x0, g, dx1)


def kernel(x, mem, norm_mix_g, w_in, lb_logits, hgrn_norm_g, w_pool, pool_scale, w_out, norm_x_g, norm_mem_g, w_xq, w_xk, w_xv, w_xo, norm_ffn_g, w_ff1, w_ff2, final_norm_g, loss_target, m_norm_mix_g, m_w_in, m_lb_logits, m_hgrn_norm_g, m_w_pool, m_pool_scale, m_w_out, m_norm_x_g, m_norm_mem_g, m_w_xq, m_w_xk, m_w_xv, m_w_xo, m_norm_ffn_g, m_w_ff1, m_w_ff2, m_final_norm_g, v_norm_mix_g, v_w_in, v_lb_logits, v_hgrn_norm_g, v_w_pool, v_pool_scale, v_w_out, v_norm_x_g, v_norm_mem_g, v_w_xq, v_w_xk, v_w_xv, v_w_xo, v_norm_ffn_g, v_w_ff1, v_w_ff2, v_final_norm_g):
    x0 = x[0]
    mem0 = mem[0]
    tgt = loss_target[0]
    gn = hgrn_norm_g[0]
    gfin = final_norm_g.reshape(1, D_MODEL)
    wp = w_pool[0]
    heads_2d = lambda w: w.reshape(D_MODEL // N_DEV, D_MODEL)
    xo_2d = lambda w: w.reshape(D_MODEL, D_MODEL // N_DEV)

    first = _all_gather_weights([w_in[0].T], [w_out[0], heads_2d(w_xq), heads_2d(w_xk), heads_2d(w_xv), xo_2d(w_xo).T,
                                              w_ff1[0], w_ff2[0]])
    win_t = first[0].reshape(IN_WIDTH, D_MODEL)
    ga_attn, ga_mlp = _gather_first_start([first[1:6], first[6:8]], "gather_first_start")

    z, h = _in_proj(x0, norm_mix_g, win_t, deps=[ga_attn[3]])
    mixed_a, o_pre, states = _hgrn_fwd(z, lb_logits, gn)
    lands = _split_wait(_gather_first_copies, ga_attn, o_pre, "gather_attn_first_wait")
    gb_attn = _gather_forward_start(lands, "gather_attn_forward_start")
    mixed, pooled = _pool_fwd(z, wp, pool_scale, mixed_a, deps=[gb_attn[3]])
    lands = _split_wait(_gather_forward_copies, gb_attn, pooled, "gather_attn_forward_wait")
    wout_f, wq_f, wk_f, wv_f, wo_t = (t.reshape(D_MODEL, D_MODEL) for t in lands)
    x1 = _out_proj(x0, mixed, wout_f)
    hm, xk, xv = _mem_kv(mem0, norm_mem_g, wk_f, wv_f, deps=[x1])
    lands = _split_wait(_gather_first_copies, ga_mlp, xk, "gather_mlp_first_wait")
    gb_mlp = _gather_forward_start(lands, "gather_mlp_forward_start")
    x2, hq, xq, att = _xattn_fwd(x1, norm_x_g, wq_f, xk, xv, wo_t, deps=[gb_mlp[3]])
    w1_b, w2_b = _split_wait(_gather_forward_copies, gb_mlp, x2, "gather_mlp_forward_wait")
    dx3, u, hf, slot_fin = _mlp_fwd_loss(x2, norm_ffn_g, w1_b, w2_b.reshape(D_FF, D_MODEL), gfin, tgt)

    rows = lambda t, r: t.reshape(N_DEV, r, D_MODEL)
    dw2 = _wgrad(u, dx3, "wgrad_ff2")
    ex_ff2 = _all_to_all_start([rows(dw2, FF_BLK)], [], "exchange_ff2_start")
    da, dx2, slot_ffn = _mlp_bwd(dx3, u, x2, norm_ffn_g, w1_b, w2_b, deps=[ex_ff2[3]])
    dw1 = _wgrad(hf, da, "wgrad_ff1", col_blocks=True)
    ex_ff1 = _all_to_all_start([dw1], [], "exchange_ff1_start")
    dx1, dxq, dxk, dxv, slot_x = _xattn_bwd(dx2, x1, norm_x_g, xq, xk, xv, wq_f, wo_t, deps=[ex_ff1[3]])
    dwo_t = _wgrad(dx2, att, "wgrad_xo")
    dwq = _wgrad(hq, dxq, "wgrad_xq")
    dwk, dwv, slot_mem = _mem_bwd(mem0, norm_mem_g, hm, dxk, dxv, wk_f, wv_f)
    ex_attn = _all_to_all_start([rows(dwq, 128), rows(dwk, 128), rows(dwv, 128), rows(dwo_t, 128)], [],
                                "exchange_attn_start")
    dwout = _wgrad(mixed, dx1, "wgrad_out")
    dz_pool, d_wpool, slot_ps = _pool_bwd(dx1, wout_f, pooled, wp, pool_scale, deps=[ex_attn[3]])
    small0 = jnp.concatenate([slot_x, slot_mem, slot_ffn, slot_fin, slot_ps], axis=0)
    ex_out = _all_to_all_start([rows(dwout, 128)], [small0, d_wpool], "exchange_out_start")
    dz, slot_lb, slot_gn = _hgrn_bwd(z, o_pre, dx1, wout_f, states, lb_logits, gn, dz_pool, deps=[ex_out[3]])
    dwin_t = _wgrad(dz, h, "wgrad_in")
    small1 = jnp.concatenate([slot_lb, slot_gn], axis=0)
    ex_in = _all_to_all_start([rows(dwin_t, 320)], [small1], "exchange_in_start")
    grad_x, slot_mix = _in_bwd(dz, win_t, x0, norm_mix_g, dx1, deps=[ex_in[3]])
    ex_mix = _all_to_all_start([], [slot_mix], "exchange_mix_start")

    out = {}
    (r_2,) = _split_wait(_all_to_all_copies(1), ex_ff2, ex_mix[3], "exchange_ff2_wait")
    out["w_ff2"] = _sum_adamw(r_2, w_ff2[0], m_w_ff2[0], v_w_ff2[0], "adamw_ff2")
    (r_1,) = _split_wait(_all_to_all_copies(1), ex_ff1, out["w_ff2"][1], "exchange_ff1_wait")
    out["w_ff1"] = _sum_adamw(r_1, w_ff1[0], m_w_ff1[0], v_w_ff1[0], "adamw_ff1")
    r_q, r_k, r_v, r_o = _split_wait(_all_to_all_copies(4), ex_attn, out["w_ff1"][1], "exchange_attn_wait")
    for n, r, (w, m, v) in (("w_xq", r_q, (w_xq, m_w_xq, v_w_xq)), ("w_xk", r_k, (w_xk, m_w_xk, v_w_xk)),
                            ("w_xv", r_v, (w_xv, m_w_xv, v_w_xv))):
        g = _sum_sources(r, "sum_grad_" + n).reshape(w.shape)
        out[n] = (g, *_adamw_whole(g, w, m, v, "adamw_" + n))
    g_xo = _sum_sources(r_o, "sum_grad_xo").T
    out["w_xo"] = (g_xo, *_adamw(g_xo, xo_2d(w_xo), xo_2d(m_w_xo), xo_2d(v_w_xo), "adamw_xo"))
    r_out, r_small0, r_wpool = _split_wait(_all_to_all_copies(1), ex_out, out["w_xo"][1], "exchange_out_wait")
    out["w_out"] = _sum_adamw(r_out, w_out[0], m_w_out[0], v_w_out[0], "adamw_out")
    r_in, r_small1 = _split_wait(_all_to_all_copies(1), ex_in, out["w_out"][1], "exchange_in_wait")
    g_in = _sum_sources(r_in, "sum_grad_in").T
    out["w_in"] = (g_in, *_adamw(g_in, w_in[0], m_w_in[0], v_w_in[0], "adamw_in"))
    (r_small2,) = _split_wait(_all_to_all_copies(0), ex_mix, out["w_in"][1], "exchange_mix_wait")
    row = lambda t: t.reshape(1, -1)
    small_params = {
        "norm_mix_g": (norm_mix_g, m_norm_mix_g, v_norm_mix_g),
        "lb_logits": (lb_logits, m_lb_logits, v_lb_logits),
        "hgrn_norm_g": (hgrn_norm_g[0], m_hgrn_norm_g[0], v_hgrn_norm_g[0]),
        "pool_scale": (pool_scale, m_pool_scale, v_pool_scale),
        "norm_x_g": (norm_x_g, m_norm_x_g, v_norm_x_g),
        "norm_mem_g": (norm_mem_g, m_norm_mem_g, v_norm_mem_g),
        "norm_ffn_g": (norm_ffn_g, m_norm_ffn_g, v_norm_ffn_g),
        "final_norm_g": (row(final_norm_g), row(m_final_norm_g), row(v_final_norm_g)),
        "w_pool": (wp, m_w_pool[0], v_w_pool[0]),
    }
    loss, small_out = _small_update([r_small0, r_small1, r_small2], r_wpool, small_params)
    out.update(small_out)

    shapes = dict(norm_mix_g=norm_mix_g, w_in=w_in, lb_logits=lb_logits, hgrn_norm_g=hgrn_norm_g, w_pool=w_pool,
                  pool_scale=pool_scale, w_out=w_out, norm_x_g=norm_x_g, norm_mem_g=norm_mem_g, w_xq=w_xq, w_xk=w_xk,
                  w_xv=w_xv, w_xo=w_xo, norm_ffn_g=norm_ffn_g, w_ff1=w_ff1, w_ff2=w_ff2, final_norm_g=final_norm_g)
    order = list(shapes)
    group = lambda k: [out[n][k].reshape(shapes[n].shape) for n in order]
    return (loss.reshape(()), grad_x.reshape(x.shape), *group(0), *group(1), *group(2), *group(3))
```

```python
import jax
import jax.numpy as jnp
from jax import lax
from jax.experimental import pallas as pl
from jax.experimental.pallas import tpu as pltpu

F32 = jnp.float32
BF16 = jnp.bfloat16

D_MODEL = 1024
N_DEV = 8
HEADS = 4
HD = 128
HW = HEADS * HD
IN_WIDTH = 5 * HW
XHD = 256
MEM_LEN = 256
D_FF = 4096
FF_BLK = D_FF // N_DEV
POOL_WINDOWS = (2, 4, 8, 16)
POOL_HALO = 16
CHUNK = 64
CHUNKS_PER_STEP = 4
SUB = 16
N_SUB = CHUNK // SUB
EXP_CAP = 80.0
EPS = 1e-6
TINY = 1e-30
ROW_TILE = 512
WIDE_ROW_TILE = 1024
SLOT = 8
V7X_VMEM_LIMIT = 56 * 1024 * 1024

ADAM_LR = 0.001
ADAM_B1 = 0.9
ADAM_B2 = 0.999
ADAM_EPS = 1e-08
ADAM_WD = 0.01
ADAM_STEP = 10

MESH_ID = pl.DeviceIdType.MESH


def _params(sem=None, vmem=V7X_VMEM_LIMIT):
    return pltpu.CompilerParams(dimension_semantics=sem, vmem_limit_bytes=vmem)


def _mm(a, b):
    return lax.dot_general(a.astype(BF16), b.astype(BF16), (((1,), (0,)), ((), ())), preferred_element_type=F32)


def _mm_nt(a, b):
    return lax.dot_general(a.astype(BF16), b.astype(BF16), (((1,), (1,)), ((), ())), preferred_element_type=F32)


def _mm_tn(a, b):
    return lax.dot_general(a.astype(BF16), b.astype(BF16), (((0,), (0,)), ((), ())), preferred_element_type=F32)


def _sigmoid(x):
    return 1.0 / (1.0 + jnp.exp(-x))


def _rms(x):
    return lax.rsqrt(jnp.mean(x * x, axis=-1, keepdims=True) + EPS)


def _rms_bwd(x, g, dh):
    r = _rms(x)
    n = x * r
    dn = dh * g
    dx = r * (dn - n * jnp.mean(dn * n, axis=-1, keepdims=True))
    return dx, jnp.sum(dh * n, axis=0, keepdims=True)


def _tri_dot(tri, x, passes):
    acc = None
    rest = x
    for _ in range(passes):
        piece = rest.astype(BF16)
        part = lax.dot_general(tri, piece, (((1,), (0,)), ((), ())), preferred_element_type=F32)
        acc = part if acc is None else acc + part
        rest = rest - piece.astype(F32)
    return acc


def _adam_update(g, w, m, v):
    nm = ADAM_B1 * m + (1.0 - ADAM_B1) * g
    nv = ADAM_B2 * v + (1.0 - ADAM_B2) * (g * g)
    m_hat = nm / (1.0 - ADAM_B1 ** ADAM_STEP)
    v_hat = nv / (1.0 - ADAM_B2 ** ADAM_STEP)
    return -ADAM_LR * (m_hat / (jnp.sqrt(v_hat) + ADAM_EPS) + ADAM_WD * w), nm, nv


def _full(shape):
    return pl.BlockSpec(shape, lambda *_: (0,) * len(shape))


VMEM_WHOLE = pl.BlockSpec(memory_space=pltpu.VMEM)
ANY_SPACE = pl.BlockSpec(memory_space=pl.ANY)


def _mesh_pos():
    return lax.axis_index("x"), lax.axis_index("y"), lax.axis_index("c")


def _flat(px, py, pc):
    return 4 * px + 2 * py + pc


def _all_gather_weights(shards, cast_only):
    n, nc = len(shards), len(cast_only)
    step = 64

    def body(*refs):
        x_refs, c_refs = refs[:n], refs[n:n + nc]
        out_refs, cast_refs = refs[n + nc:2 * n + nc], refs[2 * n + nc:2 * n + 2 * nc]
        bufs = refs[2 * n + 2 * nc:3 * n + 2 * nc]
        send_sems, recv_sems, local_sems = refs[3 * n + 2 * nc:]
        x, y, c = _mesh_pos()
        me, sibling = (x, y, c), (x, y, 1 - c)
        chips = [(1 - x, y), (x, 1 - y), (1 - x, 1 - y)]

        def copy(a, k, blk, to, src=None):
            rows = out_refs[a].at[_flat(*blk)]
            return pltpu.make_async_remote_copy(
                src_ref=rows if src is None else src, dst_ref=rows,
                send_sem=send_sems.at[7 * a + k], recv_sem=recv_sems.at[7 * a + k], device_id=to, device_id_type=MESH_ID)

        def cast_rows(src, dst, rows):
            def cast(i, carry):
                r0 = pl.multiple_of(i * step, step)
                dst[pl.ds(r0, step), :] = src[pl.ds(r0, step), :].astype(BF16)
                return carry
            lax.fori_loop(0, rows // step, cast, 0)

        first, mine = [], []
        for a in range(n):
            cast_rows(x_refs[a], bufs[a], shards[a].shape[0])
            mine.append(pltpu.make_async_copy(bufs[a], out_refs[a].at[_flat(*me)], local_sems.at[a]))
            first.append(copy(a, 0, me, sibling, src=bufs[a]))
            first += [copy(a, 1 + j, me, (*chip, c), src=bufs[a]) for j, chip in enumerate(chips)]
            for cp in [mine[-1]] + first[-4:]:
                cp.start()
        for a in range(nc):
            cast_rows(c_refs[a], cast_refs[a], cast_only[a].shape[0])
        passed = []
        for j, chip in enumerate(chips):
            for a in range(n):
                copy(a, 1 + j, (*chip, c), me).wait_recv()
                passed.append(copy(a, 4 + j, (*chip, c), sibling))
                passed[-1].start()
        for a in range(n):
            copy(a, 0, sibling, me).wait_recv()
            for j, chip in enumerate(chips):
                copy(a, 4 + j, (*chip, 1 - c), me).wait_recv()
        for cp in first + passed:
            cp.wait_send()
        for cp in mine:
            cp.wait()

    return pl.pallas_call(
        body, name="all_gather_w_in",
        out_shape=[jax.ShapeDtypeStruct((N_DEV,) + s.shape, BF16) for s in shards]
        + [jax.ShapeDtypeStruct(s.shape, BF16) for s in cast_only],
        in_specs=[VMEM_WHOLE] * (n + nc), out_specs=[ANY_SPACE] * n + [VMEM_WHOLE] * nc,
        scratch_shapes=[pltpu.VMEM(s.shape, BF16) for s in shards]
        + [pltpu.SemaphoreType.DMA((7 * n,)), pltpu.SemaphoreType.DMA((7 * n,)), pltpu.SemaphoreType.DMA((n,))],
        compiler_params=_params(),
    )(*shards, *cast_only)


HBM_SPEC = pl.BlockSpec(memory_space=pltpu.HBM)
SEM_SPEC = pl.BlockSpec(memory_space=pltpu.SEMAPHORE)
EFFECT = pltpu.SideEffectType.DATAFLOW_SIDE_EFFECTING
TOKEN = jax.ShapeDtypeStruct((8, 128), F32)


def _in_hbm(a):
    return pltpu.with_memory_space_constraint(a, pltpu.HBM)


def _split_start(copies_of, srcs, lands, n_sems, name):
    ns, nl, k = len(srcs), len(lands), len(n_sems)

    def body(*refs):
        src_refs, land_refs = refs[:ns], refs[ns:ns + nl]
        sems = refs[ns + nl:ns + nl + k]
        token = refs[-1]
        for cp in copies_of(src_refs, land_refs, sems):
            cp.start()
        token[...] = jnp.zeros_like(token)

    outs = pl.pallas_call(
        body, name=name,
        out_shape=[pltpu.SemaphoreType.DMA((q,)) for q in n_sems]
        + [pltpu.HBM(a.shape, a.dtype) for a in list(srcs) + list(lands)] + [TOKEN],
        in_specs=[HBM_SPEC] * (ns + nl),
        out_specs=[SEM_SPEC] * k + [HBM_SPEC] * (ns + nl) + [VMEM_WHOLE],
        input_output_aliases={i: k + i for i in range(ns + nl)},
        compiler_params=pltpu.CompilerParams(has_side_effects=EFFECT),
    )(*[_in_hbm(a) for a in list(srcs) + list(lands)])
    return outs[:k], outs[k:k + ns], outs[k + ns:k + ns + nl], outs[-1]


def _split_wait(copies_of, handle, after, name):
    sems, srcs, lands, _ = handle
    ns, nl, k = len(srcs), len(lands), len(sems)

    def body(*refs):
        src_refs, land_refs = refs[:ns], refs[ns:ns + nl]
        sem_refs = refs[ns + nl:ns + nl + k]
        for cp in copies_of(src_refs, land_refs, sem_refs):
            cp.wait()

    outs = pl.pallas_call(
        body, name=name,
        out_shape=[pltpu.HBM(a.shape, a.dtype) for a in list(srcs) + list(lands)],
        in_specs=[HBM_SPEC] * (ns + nl) + [SEM_SPEC] * k + [ANY_SPACE],
        out_specs=[HBM_SPEC] * (ns + nl),
        input_output_aliases={i: i for i in range(ns + nl)},
        compiler_params=pltpu.CompilerParams(has_side_effects=EFFECT),
    )(*srcs, *lands, *sems, after)
    return outs[ns:]


def _gather_first_copies(shard_refs, land_refs, sems):
    send_sems, recv_sems, local_sems = sems
    x, y, c = _mesh_pos()
    me = _flat(x, y, c)
    peers = [(x, y, 1 - c), (1 - x, y, c), (x, 1 - y, c), (1 - x, 1 - y, c)]
    copies = []
    for a, (shard, land) in enumerate(zip(shard_refs, land_refs)):
        copies.append(pltpu.make_async_copy(shard, land.at[me], local_sems.at[a]))
        for k, peer in enumerate(peers):
            copies.append(pltpu.make_async_remote_copy(
                src_ref=shard, dst_ref=land.at[me], send_sem=send_sems.at[4 * a + k], recv_sem=recv_sems.at[4 * a + k],
                device_id=peer, device_id_type=MESH_ID))
    return copies


def _gather_forward_copies(src_refs, land_refs, sems):
    del src_refs
    send_sems, recv_sems = sems
    x, y, c = _mesh_pos()
    chips = [(1 - x, y), (x, 1 - y), (1 - x, 1 - y)]
    copies = []
    for a, land in enumerate(land_refs):
        for j, chip in enumerate(chips):
            rows = land.at[_flat(*chip, c)]
            copies.append(pltpu.make_async_remote_copy(
                src_ref=rows, dst_ref=rows, send_sem=send_sems.at[3 * a + j], recv_sem=recv_sems.at[3 * a + j],
                device_id=(x, y, 1 - c), device_id_type=MESH_ID))
    return copies


def _gather_first_start(groups, name):
    shards = [s for g in groups for s in g]
    lands = [lax.empty((N_DEV,) + s.shape, s.dtype) for s in shards]
    bounds = [sum(len(g) for g in groups[:i]) for i in range(len(groups) + 1)]

    def copies_of(src_refs, land_refs, sems):
        copies = []
        for i in range(len(groups)):
            lo, hi = bounds[i], bounds[i + 1]
            copies += _gather_first_copies(src_refs[lo:hi], land_refs[lo:hi], sems[3 * i:3 * i + 3])
        return copies

    n_sems = tuple(q for g in groups for q in (4 * len(g), 4 * len(g), len(g)))
    sems, srcs, lands, token = _split_start(copies_of, shards, lands, n_sems, name)
    return [(sems[3 * i:3 * i + 3], srcs[bounds[i]:bounds[i + 1]], lands[bounds[i]:bounds[i + 1]], token)
            for i in range(len(groups))]


def _gather_forward_start(lands, name):
    n = len(lands)
    return _split_start(_gather_forward_copies, [], lands, (3 * n, 3 * n), name)


def _all_to_all_copies(n_scattered):
    def copies_of(src_refs, land_refs, sems):
        send_sems, recv_sems, local_sems = sems
        x, y, c = _mesh_pos()
        me = _flat(x, y, c)
        copies = []
        for a, (src, land) in enumerate(zip(src_refs, land_refs)):
            scattered = a < n_scattered
            copies.append(pltpu.make_async_copy(src.at[me] if scattered else src, land.at[me], local_sems.at[a]))
            for k in range(1, N_DEV):
                peer = (1 - x if k & 4 else x, 1 - y if k & 2 else y, 1 - c if k & 1 else c)
                copies.append(pltpu.make_async_remote_copy(
                    src_ref=src.at[_flat(*peer)] if scattered else src, dst_ref=land.at[me],
                    send_sem=send_sems.at[7 * a + k - 1], recv_sem=recv_sems.at[7 * a + k - 1],
                    device_id=peer, device_id_type=MESH_ID))
        return copies
    return copies_of


def _all_to_all_start(scattered, broadcast, name):
    srcs = list(scattered) + list(broadcast)
    lands = [lax.empty(a.shape, a.dtype) for a in scattered] + [lax.empty((N_DEV,) + a.shape, a.dtype) for a in broadcast]
    n = len(srcs)
    return _split_start(_all_to_all_copies(len(scattered)), srcs, lands, (7 * n, 7 * n, n), name)


def _call_behind(deps, body, *, in_specs, **kwargs):
    n_in, n_dep = len(in_specs), len(deps)

    def body_without_deps(*refs):
        return body(*refs[:n_in], *refs[n_in + n_dep:])

    call = pl.pallas_call(body_without_deps, in_specs=list(in_specs) + [ANY_SPACE] * n_dep, **kwargs)
    return lambda *operands: call(*operands, *deps)


def _row_tile(rows):
    for cand in (256, 128, 64, 32, 16):
        if rows % cand == 0:
            return cand
    return rows


def _sum_sources(recv, name):
    _, rows, cols = recv.shape
    tile = _row_tile(rows)

    def body(r_ref, o_ref):
        acc = r_ref[0].astype(F32)
        for d in range(1, N_DEV):
            acc = acc + r_ref[d].astype(F32)
        o_ref[...] = acc

    return pl.pallas_call(
        body, name=name, grid=(rows // tile,),
        out_shape=jax.ShapeDtypeStruct((rows, cols), F32),
        in_specs=[pl.BlockSpec((N_DEV, tile, cols), lambda i: (0, i, 0))],
        out_specs=pl.BlockSpec((tile, cols), lambda i: (i, 0)),
        compiler_params=_params(("parallel",)),
    )(recv)


def _adamw(g, w, m, v, name):
    rows, cols = g.shape
    tile = _row_tile(rows)

    def body(g_ref, w_ref, m_ref, v_ref, d_ref, nm_ref, nv_ref):
        d_ref[...], nm_ref[...], nv_ref[...] = _adam_update(g_ref[...], w_ref[...], m_ref[...], v_ref[...])

    spec = pl.BlockSpec((tile, cols), lambda i: (i, 0))
    shp = jax.ShapeDtypeStruct((rows, cols), F32)
    return pl.pallas_call(
        body, name=name, grid=(rows // tile,), out_shape=[shp, shp, shp],
        in_specs=[spec] * 4, out_specs=[spec] * 3,
        compiler_params=_params(("parallel",)),
    )(g, w, m, v)


def _adamw_whole(g, w, m, v, name):
    def body(g_ref, w_ref, m_ref, v_ref, d_ref, nm_ref, nv_ref):
        d_ref[...], nm_ref[...], nv_ref[...] = _adam_update(g_ref[...], w_ref[...], m_ref[...], v_ref[...])

    shp = jax.ShapeDtypeStruct(g.shape, F32)
    return pl.pallas_call(
        body, name=name, out_shape=[shp, shp, shp], in_specs=[VMEM_WHOLE] * 4, out_specs=[VMEM_WHOLE] * 3,
        compiler_params=_params(),
    )(g, w, m, v)


def _sum_adamw(recv, w, m, v, name):
    _, rows, cols = recv.shape
    tile = _row_tile(rows)

    def body(r_ref, w_ref, m_ref, v_ref, g_ref, d_ref, nm_ref, nv_ref):
        acc = r_ref[0].astype(F32)
        for d in range(1, N_DEV):
            acc = acc + r_ref[d].astype(F32)
        g_ref[...] = acc
        d_ref[...], nm_ref[...], nv_ref[...] = _adam_update(acc, w_ref[...], m_ref[...], v_ref[...])

    spec = pl.BlockSpec((tile, cols), lambda i: (i, 0))
    shp = jax.ShapeDtypeStruct((rows, cols), F32)
    return pl.pallas_call(
        body, name=name, grid=(rows // tile,), out_shape=[shp] * 4,
        in_specs=[pl.BlockSpec((N_DEV, tile, cols), lambda i: (0, i, 0)), spec, spec, spec], out_specs=[spec] * 4,
        compiler_params=_params(("parallel",)),
    )(recv, w, m, v)


SMALL_SLOTS = {"norm_x_g": (0, 0, 1, D_MODEL), "norm_mem_g": (0, 8, 1, D_MODEL), "norm_ffn_g": (0, 16, 1, D_MODEL),
               "final_norm_g": (0, 24, 1, D_MODEL), "pool_scale": (0, 32, 1, HW),
               "lb_logits": (1, 0, 2, HW), "hgrn_norm_g": (1, 8, HEADS, HD), "norm_mix_g": (2, 0, 1, D_MODEL)}
LOSS_ROW = 25
SMALL_ORDER = ("norm_mix_g", "lb_logits", "hgrn_norm_g", "pool_scale", "norm_x_g", "norm_mem_g", "norm_ffn_g",
               "final_norm_g", "w_pool")


def _small_update(srecvs, wprecv, params):
    flat = [t for n in SMALL_ORDER for t in params[n]]
    nb = len(srecvs)
    n_in = nb + 1 + len(flat)

    def body(*refs):
        s_refs, wp_ref = refs[0:nb], refs[nb]
        in_refs = refs[nb + 1:n_in]
        loss_ref = refs[n_in]
        out_refs = refs[n_in + 1:-nb]
        accs = refs[-nb:]
        for s_ref, acc in zip(s_refs, accs):
            total = s_ref[0]
            for d in range(1, N_DEV):
                total = total + s_ref[d]
            acc[...] = total
        loss_ref[...] = accs[0][LOSS_ROW:LOSS_ROW + 1, 0:1]
        for i, name in enumerate(SMALL_ORDER):
            w_ref, m_ref, v_ref = in_refs[3 * i:3 * i + 3]
            g_ref, d_ref, nm_ref, nv_ref = out_refs[4 * i:4 * i + 4]
            if name == "w_pool":
                g = wp_ref[0]
                for d in range(1, N_DEV):
                    g = g + wp_ref[d]
            else:
                buf, r0, nr, nc = SMALL_SLOTS[name]
                g = accs[buf][r0:r0 + nr, 0:nc]
            g_ref[...] = g
            d_ref[...], nm_ref[...], nv_ref[...] = _adam_update(g, w_ref[...], m_ref[...], v_ref[...])

    out_shape = [jax.ShapeDtypeStruct((1, 1), F32)]
    for n in SMALL_ORDER:
        out_shape += [jax.ShapeDtypeStruct(params[n][0].shape, F32)] * 4
    outs = pl.pallas_call(
        body, name="small_update", out_shape=out_shape,
        in_specs=[VMEM_WHOLE] * n_in, out_specs=[VMEM_WHOLE] * len(out_shape),
        scratch_shapes=[pltpu.VMEM(r.shape[1:], F32) for r in srecvs],
        compiler_params=_params(),
    )(*srecvs, wprecv, *flat)
    return outs[0], {n: outs[1 + 4 * i:5 + 4 * i] for i, n in enumerate(SMALL_ORDER)}


def _in_proj(x, g, w_t, deps):
    s = x.shape[0]
    tm = min(WIDE_ROW_TILE, s)

    def body(x_ref, g_ref, w_ref, z_ref, h_ref):
        xv = x_ref[...]
        h = (xv * _rms(xv) * g_ref[...]).astype(BF16)
        h_ref[...] = h
        z_ref[...] = _mm_nt(h, w_ref[...])

    return _call_behind(
        deps, body, name="in_proj", grid=(s // tm,),
        out_shape=[jax.ShapeDtypeStruct((s, IN_WIDTH), F32), jax.ShapeDtypeStruct((s, D_MODEL), BF16)],
        in_specs=[pl.BlockSpec((tm, D_MODEL), lambda i: (i, 0)), _full((1, D_MODEL)), VMEM_WHOLE],
        out_specs=[pl.BlockSpec((tm, IN_WIDTH), lambda i: (i, 0)), pl.BlockSpec((tm, D_MODEL), lambda i: (i, 0))],
        compiler_params=_params(("parallel",)),
    )(x, g, w_t)


def _chunk_masks():
    row = lax.broadcasted_iota(jnp.int32, (CHUNK, CHUNK), 0)
    col = lax.broadcasted_iota(jnp.int32, (CHUNK, CHUNK), 1)
    return row, col


def _ones_where(mask):
    return jnp.where(mask, 1.0, 0.0).astype(BF16)


def _hgrn_gates(zq, zf, lb):
    sq = _sigmoid(zq)
    sig = _sigmoid(zf)
    f = lb + (1.0 - lb) * sig
    return zq * sq, sq, sig, f


def _sub_chunk_masks(width):
    trow = lax.broadcasted_iota(jnp.int32, (CHUNK, width), 0)
    return [(trow >= SUB * j) & (trow < SUB * (j + 1)) for j in range(N_SUB)]


def _head(a, h):
    return a[:, HD * h:HD * (h + 1)]


def _lanes(parts):
    return jnp.concatenate(parts, axis=1)


def _hgrn_decay_factors(b_scr, r0, b, in_sub):
    bases = [jnp.zeros((1, HW), F32)] + [b_scr[r0 + SUB * j - 1:r0 + SUB * j, :] for j in range(1, N_SUB)]
    own_base = bases[N_SUB - 1]
    for j in range(N_SUB - 2, -1, -1):
        own_base = jnp.where(in_sub[j], bases[j], own_base)
    eq = jnp.exp(b - own_base)
    ek = []
    for j in range(N_SUB):
        upto = SUB * (j + 1)
        e = jnp.exp(jnp.minimum(bases[j] - b[0:upto], EXP_CAP))
        ek.append(e if upto == CHUNK else jnp.concatenate([e, jnp.zeros((CHUNK - upto, HW), F32)], axis=0))
    return eq, ek


def _per_sub_chunk(x, in_sub):
    return _lanes([jnp.where(in_sub[j], x, 0.0) for j in range(N_SUB)])


def _own_lane_block(a, in_sub):
    out = a[:, HD * (N_SUB - 1):HD * N_SUB]
    for j in range(N_SUB - 2, -1, -1):
        out = jnp.where(in_sub[j], a[:, HD * j:HD * (j + 1)], out)
    return out


def _head_rms(o):
    return _lanes([jnp.broadcast_to(_rms(_head(o, h)), (CHUNK, HD)) for h in range(HEADS)])


def _head_mean(a):
    return _lanes([jnp.broadcast_to(jnp.mean(_head(a, h), axis=-1, keepdims=True), (CHUNK, HD)) for h in range(HEADS)])


def _hgrn_fwd(z, lb_logits, gn):
    s = z.shape[0]
    n_chunks = s // CHUNK

    def body(zq_ref, zf_ref, zi_ref, zg_ref, lbl_ref, gn_ref, oa_ref, o_ref, st_ref, state, b_scr):
        @pl.when(pl.program_id(0) == 0)
        def _():
            state[...] = jnp.zeros_like(state)

        lb = _sigmoid(lbl_ref[0:1, :] - lbl_ref[1:2, :])
        row, col = _chunk_masks()
        causal = col <= row
        tri = _ones_where(causal)
        in_sub, in_sub_head = _sub_chunk_masks(HW), _sub_chunk_masks(HD)
        gn_row = _lanes([gn_ref[h:h + 1, :] for h in range(HEADS)])
        for c in range(CHUNKS_PER_STEP):
            r0 = CHUNK * c
            rs = slice(r0, r0 + CHUNK)
            st_ref[c] = state[...]
            q, _, _, f = _hgrn_gates(zq_ref[rs, :], zf_ref[rs, :], lb)
            kk = 1.0 - f
            b = _tri_dot(tri, jnp.log(f), 3)
            b_scr[rs, :] = b
            eq, ek = _hgrn_decay_factors(b_scr, r0, b, in_sub)
            b_last = b_scr[r0 + CHUNK - 1:r0 + CHUNK, :]
            qe, qg = q * eq, q * jnp.exp(b)
            ke = [kk * e for e in ek]
            kd = kk * jnp.exp(b_last - b)
            lam_last = jnp.exp(b_last)
            v = zi_ref[rs, :]
            o_heads = []
            for h in range(HEADS):
                vh, st = _head(v, h), state[h]
                a = jnp.where(causal, _mm_nt(_per_sub_chunk(_head(qe, h), in_sub_head),
                                             _lanes([_head(ke[j], h) for j in range(N_SUB)])), 0.0)
                o_heads.append(_mm(a, vh) + _mm_nt(_head(qg, h), st))
                state[h] = st * _head(lam_last, h) + _mm_tn(vh, _head(kd, h))
            o = _lanes(o_heads)
            o_ref[rs, :] = o
            zg = zg_ref[rs, :]
            oa_ref[rs, :] = (o * _head_rms(o) * gn_row * zg * _sigmoid(zg)).astype(BF16)

    rows = CHUNK * CHUNKS_PER_STEP
    zspec = lambda cb: pl.BlockSpec((rows, HW), lambda i, cb=cb: (i, cb))
    return pl.pallas_call(
        body, name="hgrn_fwd", grid=(s // rows,),
        out_shape=[jax.ShapeDtypeStruct((s, 2 * HW), BF16), jax.ShapeDtypeStruct((s, HW), F32),
                   jax.ShapeDtypeStruct((n_chunks, HEADS, HD, HD), F32)],
        in_specs=[zspec(0), zspec(1), zspec(2), zspec(3), _full((2, HW)), _full((HEADS, HD))],
        out_specs=[pl.BlockSpec((rows, HW), lambda i: (i, 0)), pl.BlockSpec((rows, HW), lambda i: (i, 0)),
                   pl.BlockSpec((CHUNKS_PER_STEP, HEADS, HD, HD), lambda i: (i, 0, 0, 0))],
        scratch_shapes=[pltpu.VMEM((HEADS, HD, HD), F32), pltpu.VMEM((rows, HW), F32)],
        compiler_params=_params(("arbitrary",)),
    )(z, z, z, z, lb_logits, gn)


def _pool_counts(tile_idx, tm):
    t = tile_idx * tm + lax.broadcasted_iota(jnp.int32, (tm, 1), 0)
    return [1.0 / jnp.minimum(t + 1, w).astype(F32) for w in POOL_WINDOWS]


def _pool_fwd(z, w_pool, scale, mixed_in, deps):
    s = z.shape[0]
    tm = min(ROW_TILE, s)

    def body(p_ref, w_ref, sc_ref, mixin_ref, ob_ref, pooled_ref, ext):
        i = pl.program_id(0)

        @pl.when(i == 0)
        def _():
            ext[0:POOL_HALO, :] = jnp.zeros((POOL_HALO, HW), F32)

        @pl.when(i > 0)
        def _():
            ext[0:POOL_HALO, :] = ext[tm:tm + POOL_HALO, :]

        ext[POOL_HALO:POOL_HALO + tm, :] = p_ref[...]
        inv = _pool_counts(i, tm)
        for g, w in enumerate(POOL_WINDOWS):
            sl = slice(HD * g, HD * (g + 1))
            p = ext[POOL_HALO:POOL_HALO + tm, sl]
            win = p
            for d in range(1, w):
                win = win + ext[POOL_HALO - d:POOL_HALO - d + tm, sl]
            pooled = (win * inv[g] - p).astype(BF16)
            pooled_ref[:, sl] = pooled
            ob_ref[:, sl] = (_mm(pooled, w_ref[g]) * sc_ref[:, sl]).astype(BF16)

    return _call_behind(
        deps, body, name="pool_fwd", grid=(s // tm,),
        out_shape=[jax.ShapeDtypeStruct((s, 2 * HW), BF16), jax.ShapeDtypeStruct((s, HW), BF16)],
        in_specs=[pl.BlockSpec((tm, HW), lambda i: (i, 4)), _full((HEADS, HD, HD)), _full((1, HW)), ANY_SPACE],
        out_specs=[pl.BlockSpec((tm, HW), lambda i: (i, 1)), pl.BlockSpec((tm, HW), lambda i: (i, 0))],
        scratch_shapes=[pltpu.VMEM((tm + POOL_HALO, HW), F32)],
        input_output_aliases={3: 0},
        compiler_params=_params(("arbitrary",)),
    )(z, w_pool, scale, mixed_in)


def _out_proj(x, mixed, w_out):
    s = x.shape[0]
    tm = min(WIDE_ROW_TILE, s)

    def body(x_ref, a_ref, w_ref, o_ref):
        o_ref[...] = x_ref[...] + _mm(a_ref[...], w_ref[...])

    return pl.pallas_call(
        body, name="out_proj", grid=(s // tm,),
        out_shape=jax.ShapeDtypeStruct((s, D_MODEL), F32),
        in_specs=[pl.BlockSpec((tm, D_MODEL), lambda i: (i, 0)), pl.BlockSpec((tm, D_MODEL), lambda i: (i, 0)), VMEM_WHOLE],
        out_specs=pl.BlockSpec((tm, D_MODEL), lambda i: (i, 0)),
        compiler_params=_params(("parallel",)),
    )(x, mixed, w_out)


def _mem_kv(mem, g, wk, wv, deps):
    def body(m_ref, g_ref, wk_ref, wv_ref, hm_ref, k_ref, v_ref):
        m = m_ref[...]
        hm = (m * _rms(m) * g_ref[...]).astype(BF16)
        hm_ref[...] = hm
        k_ref[...] = _mm(hm, wk_ref[...]).astype(BF16)
        v_ref[...] = _mm(hm, wv_ref[...]).astype(BF16)

    shp = jax.ShapeDtypeStruct((MEM_LEN, D_MODEL), BF16)
    return _call_behind(
        deps, body, name="mem_kv", out_shape=[shp, shp, shp],
        in_specs=[VMEM_WHOLE] * 4, out_specs=[VMEM_WHOLE] * 3,
        compiler_params=_params(),
    )(mem, g, wk, wv)


def _softmax_rows(sc):
    e = jnp.exp(sc - jnp.max(sc, axis=-1, keepdims=True))
    return e / jnp.sum(e, axis=-1, keepdims=True)


def _xattn_fwd(x, g, wq, xk, xv, wo_t, deps):
    s = x.shape[0]
    tm = min(ROW_TILE, s)
    scale = XHD ** -0.5

    def body(x_ref, g_ref, wq_ref, k_ref, v_ref, wo_ref, o_ref, hq_ref, q_ref, att_ref):
        xv_ = x_ref[...]
        hq = (xv_ * _rms(xv_) * g_ref[...]).astype(BF16)
        hq_ref[...] = hq
        q_ref[...] = (_mm(hq, wq_ref[...]) * scale).astype(BF16)
        for h in range(HEADS):
            sl = slice(XHD * h, XHD * (h + 1))
            p = _softmax_rows(_mm_nt(q_ref[:, sl], k_ref[:, sl]))
            att_ref[:, sl] = _mm(p, v_ref[:, sl]).astype(BF16)
        o_ref[...] = xv_ + _mm_nt(att_ref[...], wo_ref[...])

    row_f32 = pl.BlockSpec((tm, D_MODEL), lambda i: (i, 0))
    bshape = jax.ShapeDtypeStruct((s, D_MODEL), BF16)
    return _call_behind(
        deps, body, name="xattn_fwd", grid=(s // tm,),
        out_shape=[jax.ShapeDtypeStruct((s, D_MODEL), F32), bshape, bshape, bshape],
        in_specs=[row_f32, _full((1, D_MODEL)), VMEM_WHOLE, VMEM_WHOLE, VMEM_WHOLE, VMEM_WHOLE],
        out_specs=[row_f32] * 4,
        compiler_params=_params(("parallel",)),
    )(x, g, wq, xk, xv, wo_t)


def _mlp_fwd_loss(x, g, w1, w2, gf, target):
    s = x.shape[0]
    tm = min(ROW_TILE, s)

    def body(x_ref, g_ref, w1_ref, w2_ref, gf_ref, t_ref, dx_ref, dx16_ref, u_ref, hf_ref, slot_ref):
        @pl.when(pl.program_id(0) == 0)
        def _():
            slot_ref[...] = jnp.zeros_like(slot_ref)

        xv = x_ref[...]
        hf = (xv * _rms(xv) * g_ref[...]).astype(BF16)
        hf_ref[...] = hf
        for j in range(N_DEV):
            a = jnp.maximum(_mm(hf, w1_ref[j]), 0.0)
            u_ref[:, FF_BLK * j:FF_BLK * (j + 1)] = (a * a).astype(BF16)
        acc = xv + _mm(u_ref[...], w2_ref[...])
        gfv = gf_ref[...]
        r = _rms(acc)
        n = acc * r
        err = n * gfv - t_ref[...]
        slot_ref[1:2, :] += jnp.sum(jnp.mean(err * err, axis=-1, keepdims=True), axis=0, keepdims=True) * 0.5
        dy = err * (1.0 / D_MODEL)
        slot_ref[0:1, :] += jnp.sum(dy * n, axis=0, keepdims=True)
        dn = dy * gfv
        dx = r * (dn - n * jnp.mean(dn * n, axis=-1, keepdims=True))
        dx_ref[...] = dx
        dx16_ref[...] = dx.astype(BF16)

    row_f32 = pl.BlockSpec((tm, D_MODEL), lambda i: (i, 0))
    return pl.pallas_call(
        body, name="mlp_fwd_loss", grid=(s // tm,),
        out_shape=[jax.ShapeDtypeStruct((s, D_MODEL), F32), jax.ShapeDtypeStruct((s, D_MODEL), BF16),
                   jax.ShapeDtypeStruct((s, D_FF), BF16), jax.ShapeDtypeStruct((s, D_MODEL), BF16),
                   jax.ShapeDtypeStruct((SLOT, D_MODEL), F32)],
        in_specs=[row_f32, _full((1, D_MODEL)), VMEM_WHOLE, VMEM_WHOLE, _full((1, D_MODEL)), row_f32],
        out_specs=[row_f32, row_f32, pl.BlockSpec((tm, D_FF), lambda i: (i, 0)), row_f32, _full((SLOT, D_MODEL))],
        compiler_params=_params(("arbitrary",)),
    )(x, g, w1, w2, gf, target)


def _zero_slot(slot_ref):
    @pl.when(pl.program_id(0) == 0)
    def _():
        slot_ref[...] = jnp.zeros_like(slot_ref)


def _mlp_bwd(dx3, u, x2, g, w1, w2, deps):
    s = x2.shape[0]
    tm = min(ROW_TILE, s)

    def body(d_ref, u_ref, x_ref, g_ref, w1_ref, w2_ref, da_ref, dx_ref, slot_ref):
        _zero_slot(slot_ref)
        d = d_ref[...]
        d16 = d.astype(BF16)
        dhf = jnp.zeros((tm, D_MODEL), F32)
        for j in range(N_DEV):
            sl = slice(FF_BLK * j, FF_BLK * (j + 1))
            u = u_ref[:, sl].astype(F32)
            da = (_mm_nt(d16, w2_ref[j]) * (2.0 * u * lax.rsqrt(jnp.maximum(u, TINY)))).astype(BF16)
            da_ref[:, sl] = da
            dhf = dhf + _mm_nt(da, w1_ref[j])
        dx, dg = _rms_bwd(x_ref[...], g_ref[...], dhf)
        dx_ref[...] = d + dx
        slot_ref[0:1, :] += dg

    row_f32 = pl.BlockSpec((tm, D_MODEL), lambda i: (i, 0))
    return _call_behind(
        deps, body, name="mlp_bwd", grid=(s // tm,),
        out_shape=[jax.ShapeDtypeStruct((s, D_FF), BF16), jax.ShapeDtypeStruct((s, D_MODEL), F32),
                   jax.ShapeDtypeStruct((SLOT, D_MODEL), F32)],
        in_specs=[row_f32, pl.BlockSpec((tm, D_FF), lambda i: (i, 0)), row_f32, _full((1, D_MODEL)),
                  VMEM_WHOLE, VMEM_WHOLE],
        out_specs=[pl.BlockSpec((tm, D_FF), lambda i: (i, 0)), row_f32, _full((SLOT, D_MODEL))],
        compiler_params=_params(("arbitrary",)),
    )(dx3, u, x2, g, w1, w2)


def _wgrad(a, b, name, col_blocks=False):
    s, m = a.shape
    n = b.shape[1]
    tm = 1280 if m % 1280 == 0 else min(1024, m)
    tn = min(1024, n)
    blk = n // N_DEV
    per_step = tn // blk if col_blocks else 1
    ts = min(2 * ROW_TILE, s)
    n_s = s // ts

    def body(a_ref, b_ref, o_ref, acc):
        k = pl.program_id(2)

        @pl.when(k == 0)
        def _():
            acc[...] = jnp.zeros_like(acc)

        acc[...] += _mm_tn(a_ref[...], b_ref[...])

        @pl.when(k == n_s - 1)
        def _():
            if col_blocks:
                for p in range(per_step):
                    o_ref[p] = acc[:, blk * p:blk * (p + 1)].astype(BF16)
            else:
                o_ref[...] = acc[...].astype(BF16)

    if col_blocks:
        out_shape = jax.ShapeDtypeStruct((N_DEV, m, blk), BF16)
        out_spec = pl.BlockSpec((per_step, tm, blk), lambda i, j, k: (j, i, 0))
    else:
        out_shape = jax.ShapeDtypeStruct((m, n), BF16)
        out_spec = pl.BlockSpec((tm, tn), lambda i, j, k: (i, j))
    return pl.pallas_call(
        body, name=name, grid=(m // tm, n // tn, n_s), out_shape=out_shape,
        in_specs=[pl.BlockSpec((ts, tm), lambda i, j, k: (k, i)), pl.BlockSpec((ts, tn), lambda i, j, k: (k, j))],
        out_specs=out_spec,
        scratch_shapes=[pltpu.VMEM((tm, tn), F32)],
        compiler_params=_params(("parallel", "parallel", "arbitrary")),
    )(a, b)


def _xattn_bwd(dx2, x1, g, q, xk, xv, wq, wo_t, deps):
    s = x1.shape[0]
    tm = min(ROW_TILE, s)
    scale = XHD ** -0.5

    def body(d_ref, x_ref, g_ref, q_ref, k_ref, v_ref, wq_ref, wo_ref, dx_ref, dx16_ref, dq_ref, dk_ref, dv_ref, slot_ref,
             datt):
        _zero_slot(slot_ref)

        @pl.when(pl.program_id(0) == 0)
        def _():
            dk_ref[...] = jnp.zeros_like(dk_ref)
            dv_ref[...] = jnp.zeros_like(dv_ref)

        d = d_ref[...]
        datt[...] = _mm(d, wo_ref[...]).astype(BF16)
        for h in range(HEADS):
            sl = slice(XHD * h, XHD * (h + 1))
            qh, kh, vh, dah = q_ref[:, sl], k_ref[:, sl], v_ref[:, sl], datt[:, sl]
            p = _softmax_rows(_mm_nt(qh, kh))
            dp = _mm_nt(dah, vh)
            ds = (p * (dp - jnp.sum(dp * p, axis=-1, keepdims=True))).astype(BF16)
            dq_ref[:, sl] = (_mm(ds, kh) * scale).astype(BF16)
            dk_ref[:, sl] += _mm_tn(ds, qh)
            dv_ref[:, sl] += _mm_tn(p, dah)
        dx, dg = _rms_bwd(x_ref[...], g_ref[...], _mm_nt(dq_ref[...], wq_ref[...]))
        dx_ref[...] = d + dx
        dx16_ref[...] = (d + dx).astype(BF16)
        slot_ref[0:1, :] += dg

    row_f32 = pl.BlockSpec((tm, D_MODEL), lambda i: (i, 0))
    kv = jax.ShapeDtypeStruct((MEM_LEN, D_MODEL), F32)
    tokens16 = jax.ShapeDtypeStruct((s, D_MODEL), BF16)
    return _call_behind(
        deps, body, name="xattn_bwd", grid=(s // tm,),
        out_shape=[jax.ShapeDtypeStruct((s, D_MODEL), F32), tokens16, tokens16, kv, kv,
                   jax.ShapeDtypeStruct((SLOT, D_MODEL), F32)],
        in_specs=[row_f32, row_f32, _full((1, D_MODEL)), row_f32, VMEM_WHOLE, VMEM_WHOLE, VMEM_WHOLE, VMEM_WHOLE],
        out_specs=[row_f32, row_f32, row_f32, _full((MEM_LEN, D_MODEL)), _full((MEM_LEN, D_MODEL)),
                   _full((SLOT, D_MODEL))],
        scratch_shapes=[pltpu.VMEM((tm, D_MODEL), BF16)],
        compiler_params=_params(("arbitrary",)),
    )(dx2, x1, g, q, xk, xv, wq, wo_t)


def _mem_bwd(mem, g, hm, dxk, dxv, wk, wv):
    def body(m_ref, g_ref, hm_ref, dk_ref, dv_ref, wk_ref, wv_ref, dwk_ref, dwv_ref, slot_ref):
        dk, dv = dk_ref[...], dv_ref[...]
        hm_ = hm_ref[...]
        dwk_ref[...] = _mm_tn(hm_, dk).astype(BF16)
        dwv_ref[...] = _mm_tn(hm_, dv).astype(BF16)
        _, dg = _rms_bwd(m_ref[...], g_ref[...], _mm_nt(dk, wk_ref[...]) + _mm_nt(dv, wv_ref[...]))
        slot_ref[...] = jnp.zeros_like(slot_ref)
        slot_ref[0:1, :] = dg

    wshape = jax.ShapeDtypeStruct((D_MODEL, D_MODEL), BF16)
    return pl.pallas_call(
        body, name="mem_bwd", out_shape=[wshape, wshape, jax.ShapeDtypeStruct((SLOT, D_MODEL), F32)],
        in_specs=[VMEM_WHOLE] * 7, out_specs=[VMEM_WHOLE] * 3,
        compiler_params=_params(),
    )(mem, g, hm, dxk, dxv, wk, wv)


def _pool_bwd(dx1, w_out, pooled, w_pool, scale, deps):
    s = dx1.shape[0]
    tm = min(ROW_TILE, s)
    n_t = s // tm

    def body(dx_ref, wo_ref, pl_ref, w_ref, sc_ref, dz_ref, dw_ref, slot_ref, ext, do_ref):
        i = pl.program_id(0)
        tile = n_t - 1 - i
        _zero_slot(slot_ref)
        do_ref[...] = _mm_nt(dx_ref[...], wo_ref[HW:2 * HW, :])

        @pl.when(i == 0)
        def _():
            dw_ref[...] = jnp.zeros_like(dw_ref)
            ext[tm:tm + POOL_HALO, :] = jnp.zeros((POOL_HALO, HW), F32)

        @pl.when(i > 0)
        def _():
            ext[tm:tm + POOL_HALO, :] = ext[0:POOL_HALO, :]

        inv = _pool_counts(tile, tm)
        dpooled = []
        for g in range(HEADS):
            sl = slice(HD * g, HD * (g + 1))
            pooled_g = pl_ref[:, sl]
            do = do_ref[:, sl]
            slot_ref[0:1, sl] += jnp.sum(_mm(pooled_g, w_ref[g]) * do, axis=0, keepdims=True)
            dy = (do * sc_ref[:, sl]).astype(BF16)
            dw_ref[g] += _mm_tn(pooled_g, dy)
            dpo = _mm_nt(dy, w_ref[g])
            dpooled.append(dpo)
            ext[0:tm, sl] = dpo * inv[g]
        for g, w in enumerate(POOL_WINDOWS):
            sl = slice(HD * g, HD * (g + 1))
            win = ext[0:tm, sl]
            for d in range(1, w):
                win = win + ext[d:d + tm, sl]
            dz_ref[:, sl] = (win - dpooled[g]).astype(BF16)

    return _call_behind(
        deps, body, name="pool_bwd", grid=(n_t,),
        out_shape=[jax.ShapeDtypeStruct((s, IN_WIDTH), BF16), jax.ShapeDtypeStruct((HEADS, HD, HD), F32),
                   jax.ShapeDtypeStruct((SLOT, D_MODEL), F32)],
        in_specs=[pl.BlockSpec((tm, D_MODEL), lambda i: (n_t - 1 - i, 0)), VMEM_WHOLE,
                  pl.BlockSpec((tm, HW), lambda i: (n_t - 1 - i, 0)), _full((HEADS, HD, HD)), _full((1, HW))],
        out_specs=[pl.BlockSpec((tm, HW), lambda i: (n_t - 1 - i, 4)), _full((HEADS, HD, HD)), _full((SLOT, D_MODEL))],
        scratch_shapes=[pltpu.VMEM((tm + POOL_HALO, HW), F32), pltpu.VMEM((tm, HW), F32)],
        compiler_params=_params(("arbitrary",)),
    )(dx1, w_out, pooled, w_pool, scale)


def _hgrn_bwd(z, o, dx1, w_out, states, lb_logits, gn, dz_in, deps):
    s = z.shape[0]
    n_chunks = s // CHUNK

    def body(zq_ref, zf_ref, zi_ref, zg_ref, o_ref, dx_ref, wo_ref, st_ref, lbl_ref, gn_ref, dzin_ref,
             dz_ref, dlb_ref, dgn_ref, dstate, b_scr, dlb_acc, do_ref):
        i = pl.program_id(0)

        @pl.when(i == 0)
        def _():
            dstate[...] = jnp.zeros_like(dstate)
            dlb_acc[...] = jnp.zeros_like(dlb_acc)
            dgn_ref[...] = jnp.zeros_like(dgn_ref)
            dlb_ref[...] = jnp.zeros_like(dlb_ref)

        do_ref[...] = _mm_nt(dx_ref[...], wo_ref[0:HW, :])
        lb = _sigmoid(lbl_ref[0:1, :] - lbl_ref[1:2, :])
        row, col = _chunk_masks()
        causal = col <= row
        tri = _ones_where(causal)
        upper = _ones_where(col >= row)
        strict_lower = _ones_where(col < row)
        in_sub, in_sub_head = _sub_chunk_masks(HW), _sub_chunk_masks(HD)
        gn_row = _lanes([gn_ref[h:h + 1, :] for h in range(HEADS)])
        dlb_sum, dgn_sum = 0.0, 0.0
        for c in reversed(range(CHUNKS_PER_STEP)):
            r0 = CHUNK * c
            rs = slice(r0, r0 + CHUNK)
            zq = zq_ref[rs, :]
            q, sq, sig, f = _hgrn_gates(zq, zf_ref[rs, :], lb)
            kk = 1.0 - f
            b = _tri_dot(tri, jnp.log(f), 3)
            b_scr[rs, :] = b
            v = zi_ref[rs, :]
            o, zg, doa = o_ref[rs, :], zg_ref[rs, :], do_ref[rs, :]
            sg = _sigmoid(zg)
            n = o * _head_rms(o)
            don = doa * (zg * sg)
            dgn_sum = dgn_sum + jnp.sum(don * n, axis=0, keepdims=True)
            dn = don * gn_row
            d_o = _head_rms(o) * (dn - n * _head_mean(dn * n))
            dz_ref[rs, 3 * HW:4 * HW] = (doa * (n * gn_row) * (sg * (1.0 + zg * (1.0 - sg)))).astype(BF16)
            eq, ek = _hgrn_decay_factors(b_scr, r0, b, in_sub)
            b_last = b_scr[r0 + CHUNK - 1:r0 + CHUNK, :]
            lam, e_last, lam_last = jnp.exp(b), jnp.exp(b_last - b), jnp.exp(b_last)
            qe, qg, kd = q * eq, q * lam, kk * e_last
            ke = [kk * e for e in ek]
            dv_h, gq_h, gk_h, dqi_h, dkd_h, st_h = [], [], [], [], [], []
            for h in range(HEADS):
                vh, doh = _head(v, h), _head(d_o, h)
                st0, ds1 = st_ref[c, h], dstate[h]
                q16 = _per_sub_chunk(_head(qe, h), in_sub_head).astype(BF16)
                ke16 = _lanes([_head(ke[j], h) for j in range(N_SUB)]).astype(BF16)
                a = jnp.where(causal, _mm_nt(q16, ke16), 0.0)
                da = jnp.where(causal, _mm_nt(doh, vh), 0.0)
                dv_h.append(_mm_tn(a, doh) + _mm_nt(_head(kd, h), ds1))
                gq_h.append(_own_lane_block(_mm(da, ke16), in_sub_head))
                gk_h.append(_mm_tn(da, q16))
                dqi_h.append(_mm(doh, st0))
                dkd_h.append(_mm(vh, ds1))
                st_h.append(jnp.sum(st0 * ds1, axis=0, keepdims=True))
                dstate[h] = ds1 * _head(lam_last, h) + _mm_tn(doh, _head(qg, h))
            dz_ref[rs, 2 * HW:3 * HW] = _lanes(dv_h).astype(BF16)
            gq = _lanes(gq_h)
            gk = [_lanes([gk_h[h][:, HD * j:HD * (j + 1)] for h in range(HEADS)]) for j in range(N_SUB)]
            dq_inter = lam * _lanes(dqi_h)
            dq = eq * gq + dq_inter
            dk_intra = sum(ek[j] * gk[j] for j in range(N_SUB))
            dk_state = _lanes(dkd_h) * e_last
            db_intra = (qe.astype(BF16).astype(F32) * gq
                        - sum(ke[j].astype(BF16).astype(F32) * gk[j] for j in range(N_SUB)))
            dlf = (_tri_dot(upper, db_intra + q * dq_inter, 2) + _tri_dot(strict_lower, kk * dk_state, 2)
                   + lam_last * _lanes(st_h))
            df = dlf / f - (dk_intra + dk_state)
            dlb_sum = dlb_sum + jnp.sum(df * (1.0 - sig), axis=0, keepdims=True)
            dz_ref[rs, HW:2 * HW] = (df * (1.0 - lb) * sig * (1.0 - sig)).astype(BF16)
            dz_ref[rs, 0:HW] = (dq * (sq * (1.0 + zq * (1.0 - sq)))).astype(BF16)
        dlb_acc[...] += dlb_sum
        for h in range(HEADS):
            dgn_ref[h:h + 1, 0:HD] += _head(dgn_sum, h)

        @pl.when(i == n_steps - 1)
        def _():
            dl0 = dlb_acc[...] * lb * (1.0 - lb)
            dlb_ref[0:1, 0:HW] = dl0
            dlb_ref[1:2, 0:HW] = -dl0

    rows = CHUNK * CHUNKS_PER_STEP
    n_steps = s // rows
    rev = lambda i: n_steps - 1 - i
    zspec = lambda cb: pl.BlockSpec((rows, HW), lambda i, cb=cb: (rev(i), cb))
    slot = jax.ShapeDtypeStruct((SLOT, D_MODEL), F32)
    return _call_behind(
        deps, body, name="hgrn_bwd", grid=(n_steps,),
        out_shape=[jax.ShapeDtypeStruct((s, IN_WIDTH), BF16), slot, slot],
        in_specs=[zspec(0), zspec(1), zspec(2), zspec(3), pl.BlockSpec((rows, HW), lambda i: (rev(i), 0)),
                  pl.BlockSpec((rows, D_MODEL), lambda i: (rev(i), 0)), VMEM_WHOLE,
                  pl.BlockSpec((CHUNKS_PER_STEP, HEADS, HD, HD), lambda i: (rev(i), 0, 0, 0)), _full((2, HW)),
                  _full((HEADS, HD)), ANY_SPACE],
        out_specs=[pl.BlockSpec((rows, 4 * HW), lambda i: (rev(i), 0)), _full((SLOT, D_MODEL)), _full((SLOT, D_MODEL))],
        scratch_shapes=[pltpu.VMEM((HEADS, HD, HD), F32), pltpu.VMEM((rows, HW), F32), pltpu.VMEM((1, HW), F32),
                        pltpu.VMEM((rows, HW), F32)],
        input_output_aliases={10: 0},
        compiler_params=_params(("arbitrary",)),
    )(z, z, z, z, o, dx1, w_out, states, lb_logits, gn, dz_in)


def _in_bwd(dz, w_t, x0, g, dx1, deps):
    s = x0.shape[0]
    tm = min(WIDE_ROW_TILE, s)

    def body(dz_ref, w_ref, x_ref, g_ref, d_ref, dx_ref, slot_ref):
        _zero_slot(slot_ref)
        dx, dg = _rms_bwd(x_ref[...], g_ref[...], _mm(dz_ref[...], w_ref[...]))
        dx_ref[...] = d_ref[...] + dx
        slot_ref[0:1, :] += dg

    row_f32 = pl.BlockSpec((tm, D_MODEL), lambda i: (i, 0))
    return _call_behind(
        deps, body, name="in_bwd", grid=(s // tm,),
        out_shape=[jax.ShapeDtypeStruct((s, D_MODEL), F32), jax.ShapeDtypeStruct((SLOT, D_MODEL), F32)],
        in_specs=[pl.BlockSpec((tm, IN_WIDTH), lambda i: (i, 0)), VMEM_WHOLE, row_f32, _full((1, D_MODEL)), row_f32],
        out_specs=[row_f32, _full((SLOT, D_MODEL))],
        compiler_params=_params(("arbitrary",)),
    )(dz, w_t, x0, g, dx1)


def kernel(x, mem, norm_mix_g, w_in, lb_logits, hgrn_norm_g, w_pool, pool_scale, w_out, norm_x_g, norm_mem_g, w_xq, w_xk, w_xv, w_xo, norm_ffn_g, w_ff1, w_ff2, final_norm_g, loss_target, m_norm_mix_g, m_w_in, m_lb_logits, m_hgrn_norm_g, m_w_pool, m_pool_scale, m_w_out, m_norm_x_g, m_norm_mem_g, m_w_xq, m_w_xk, m_w_xv, m_w_xo, m_norm_ffn_g, m_w_ff1, m_w_ff2, m_final_norm_g, v_norm_mix_g, v_w_in, v_lb_logits, v_hgrn_norm_g, v_w_pool, v_pool_scale, v_w_out, v_norm_x_g, v_norm_mem_g, v_w_xq, v_w_xk, v_w_xv, v_w_xo, v_norm_ffn_g, v_w_ff1, v_w_ff2, v_final_norm_g):
    x0 = x[0]
    mem0 = mem[0]
    tgt = loss_target[0]
    gn = hgrn_norm_g[0]
    gfin = final_norm_g.reshape(1, D_MODEL)
    wp = w_pool[0]
    heads_2d = lambda w: w.reshape(D_MODEL // N_DEV, D_MODEL)
    xo_2d = lambda w: w.reshape(D_MODEL, D_MODEL // N_DEV)

    first = _all_gather_weights([w_in[0].T], [w_out[0], heads_2d(w_xq), heads_2d(w_xk), heads_2d(w_xv), xo_2d(w_xo).T,
                                              w_ff1[0], w_ff2[0]])
    win_t = first[0].reshape(IN_WIDTH, D_MODEL)
    ga_attn, ga_mlp = _gather_first_start([first[1:6], first[6:8]], "gather_first_start")

    z, h = _in_proj(x0, norm_mix_g, win_t, deps=[ga_attn[3]])
    mixed_a, o_pre, states = _hgrn_fwd(z, lb_logits, gn)
    lands = _split_wait(_gather_first_copies, ga_attn, o_pre, "gather_attn_first_wait")
    gb_attn = _gather_forward_start(lands, "gather_attn_forward_start")
    mixed, pooled = _pool_fwd(z, wp, pool_scale, mixed_a, deps=[gb_attn[3]])
    lands = _split_wait(_gather_forward_copies, gb_attn, pooled, "gather_attn_forward_wait")
    wout_f, wq_f, wk_f, wv_f, wo_t = (t.reshape(D_MODEL, D_MODEL) for t in lands)
    x1 = _out_proj(x0, mixed, wout_f)
    hm, xk, xv = _mem_kv(mem0, norm_mem_g, wk_f, wv_f, deps=[x1])
    lands = _split_wait(_gather_first_copies, ga_mlp, xk, "gather_mlp_first_wait")
    gb_mlp = _gather_forward_start(lands, "gather_mlp_forward_start")
    x2, hq, xq, att = _xattn_fwd(x1, norm_x_g, wq_f, xk, xv, wo_t, deps=[gb_mlp[3]])
    w1_b, w2_b = _split_wait(_gather_forward_copies, gb_mlp, x2, "gather_mlp_forward_wait")
    dx3, dx3_16, u, hf, slot_fin = _mlp_fwd_loss(x2, norm_ffn_g, w1_b, w2_b.reshape(D_FF, D_MODEL), gfin, tgt)

    rows = lambda t, r: t.reshape(N_DEV, r, D_MODEL)
    dw2 = _wgrad(u, dx3_16, "wgrad_ff2")
    ex_ff2 = _all_to_all_start([rows(dw2, FF_BLK)], [], "exchange_ff2_start")
    da, dx2, slot_ffn = _mlp_bwd(dx3, u, x2, norm_ffn_g, w1_b, w2_b, deps=[ex_ff2[3]])
    dw1 = _wgrad(hf, da, "wgrad_ff1", col_blocks=True)
    ex_ff1 = _all_to_all_start([dw1], [], "exchange_ff1_start")
    dx1, dx1_16, dxq, dxk, dxv, slot_x = _xattn_bwd(dx2, x1, norm_x_g, xq, xk, xv, wq_f, wo_t, deps=[ex_ff1[3]])
    dwo_t = _wgrad(dx2, att, "wgrad_xo")
    dwq = _wgrad(hq, dxq, "wgrad_xq")
    dwk, dwv, slot_mem = _mem_bwd(mem0, norm_mem_g, hm, dxk, dxv, wk_f, wv_f)
    ex_attn = _all_to_all_start([rows(dwq, 128), rows(dwk, 128), rows(dwv, 128), rows(dwo_t, 128)], [],
                                "exchange_attn_start")
    dwout = _wgrad(mixed, dx1_16, "wgrad_out")
    dz_pool, d_wpool, slot_ps = _pool_bwd(dx1_16, wout_f, pooled, wp, pool_scale, deps=[ex_attn[3]])
    small0 = jnp.concatenate([slot_x, slot_mem, slot_ffn, slot_fin, slot_ps], axis=0)
    ex_out = _all_to_all_start([rows(dwout, 128)], [small0, d_wpool], "exchange_out_start")
    dz, slot_lb, slot_gn = _hgrn_bwd(z, o_pre, dx1_16, wout_f, states, lb_logits, gn, dz_pool, deps=[ex_out[3]])
    dwin_t = _wgrad(dz, h, "wgrad_in")
    small1 = jnp.concatenate([slot_lb, slot_gn], axis=0)
    ex_in = _all_to_all_start([rows(dwin_t, 320)], [small1], "exchange_in_start")
    grad_x, slot_mix = _in_bwd(dz, win_t, x0, norm_mix_g, dx1, deps=[ex_in[3]])
    ex_mix = _all_to_all_start([], [slot_mix], "exchange_mix_start")

    out = {}
    (r_2,) = _split_wait(_all_to_all_copies(1), ex_ff2, ex_mix[3], "exchange_ff2_wait")
    out["w_ff2"] = _sum_adamw(r_2, w_ff2[0], m_w_ff2[0], v_w_ff2[0], "adamw_ff2")
    (r_1,) = _split_wait(_all_to_all_copies(1), ex_ff1, out["w_ff2"][1], "exchange_ff1_wait")
    out["w_ff1"] = _sum_adamw(r_1, w_ff1[0], m_w_ff1[0], v_w_ff1[0], "adamw_ff1")
    r_q, r_k, r_v, r_o = _split_wait(_all_to_all_copies(4), ex_attn, out["w_ff1"][1], "exchange_attn_wait")
    for n, r, (w, m, v) in (("w_xq", r_q, (w_xq, m_w_xq, v_w_xq)), ("w_xk", r_k, (w_xk, m_w_xk, v_w_xk)),
                            ("w_xv", r_v, (w_xv, m_w_xv, v_w_xv))):
        g = _sum_sources(r, "sum_grad_" + n).reshape(w.shape)
        out[n] = (g, *_adamw_whole(g, w, m, v, "adamw_" + n))
    g_xo = _sum_sources(r_o, "sum_grad_xo").T
    out["w_xo"] = (g_xo, *_adamw(g_xo, xo_2d(w_xo), xo_2d(m_w_xo), xo_2d(v_w_xo), "adamw_xo"))
    r_out, r_small0, r_wpool = _split_wait(_all_to_all_copies(1), ex_out, out["w_xo"][1], "exchange_out_wait")
    out["w_out"] = _sum_adamw(r_out, w_out[0], m_w_out[0], v_w_out[0], "adamw_out")
    r_in, r_small1 = _split_wait(_all_to_all_copies(1), ex_in, out["w_out"][1], "exchange_in_wait")
    g_in = _sum_sources(r_in, "sum_grad_in").T
    out["w_in"] = (g_in, *_adamw(g_in, w_in[0], m_w_in[0], v_w_in[0], "adamw_in"))
    (r_small2,) = _split_wait(_all_to_all_copies(0), ex_mix, out["w_in"][1], "exchange_mix_wait")
    row = lambda t: t.reshape(1, -1)
    small_params = {
        "norm_mix_g": (norm_mix_g, m_norm_mix_g, v_norm_mix_g),
        "lb_logits": (lb_logits, m_lb_logits, v_lb_logits),
        "hgrn_norm_g": (hgrn_norm_g[0], m_hgrn_norm_g[0], v_hgrn_norm_g[0]),
        "pool_scale": (pool_scale, m_pool_scale, v_pool_scale),
        "norm_x_g": (norm_x_g, m_norm_x_g, v_norm_x_g),
        "norm_mem_g": (norm_mem_g, m_norm_mem_g, v_norm_mem_g),
        "norm_ffn_g": (norm_ffn_g, m_norm_ffn_g, v_norm_ffn_g),
        "final_norm_g": (row(final_norm_g), row(m_final_norm_g), row(v_final_norm_g)),
        "w_pool": (wp, m_w_pool[0], v_w_pool[0]),
    }
    loss, small_out = _small_update([r_small0, r_small1, r_small2], r_wpool, small_params)
    out.update(small_out)

    shapes = dict(norm_mix_g=norm_mix_g, w_in=w_in, lb_logits=lb_logits, hgrn_norm_g=hgrn_norm_g, w_pool=w_pool,
                  pool_scale=pool_scale, w_out=w_out, norm_x_g=norm_x_g, norm_mem_g=norm_mem_g, w_xq=w_xq, w_xk=w_xk,
                  w_xv=w_xv, w_xo=w_xo, norm_ffn_g=norm_ffn_g, w_ff1=w_ff1, w_ff2=w_ff2, final_norm_g=final_norm_g)
    order = list(shapes)
    group = lambda k: [out[n][k].reshape(shapes[n].shape) for n in order]
    return (loss.reshape(()), grad_x.reshape(x.shape), *group(0), *group(1), *group(2), *group(3))
```

```python
import jax
import jax.numpy as jnp
from jax import lax
from jax.experimental import pallas as pl
from jax.experimental.pallas import tpu as pltpu

F32 = jnp.float32
BF16 = jnp.bfloat16

D_MODEL = 1024
N_DEV = 8
HEADS = 4
HD = 128
HW = HEADS * HD
IN_WIDTH = 5 * HW
XHD = 256
MEM_LEN = 256
D_FF = 4096
FF_BLK = D_FF // N_DEV
POOL_WINDOWS = (2, 4, 8, 16)
POOL_HALO = 16
CHUNK = 64
CHUNKS_PER_STEP = 8
SUB = 16
N_SUB = CHUNK // SUB
EXP_CAP = 80.0
EPS = 1e-6
TINY = 1e-30
ROW_TILE = 512
WIDE_ROW_TILE = 1024
SLOT = 8
V7X_VMEM_LIMIT = 56 * 1024 * 1024

ADAM_LR = 0.001
ADAM_B1 = 0.9
ADAM_B2 = 0.999
ADAM_EPS = 1e-08
ADAM_WD = 0.01
ADAM_STEP = 10

MESH_ID = pl.DeviceIdType.MESH


def _params(sem=None, vmem=V7X_VMEM_LIMIT):
    return pltpu.CompilerParams(dimension_semantics=sem, vmem_limit_bytes=vmem)


def _mm(a, b):
    return lax.dot_general(a.astype(BF16), b.astype(BF16), (((1,), (0,)), ((), ())), preferred_element_type=F32)


def _mm_nt(a, b):
    return lax.dot_general(a.astype(BF16), b.astype(BF16), (((1,), (1,)), ((), ())), preferred_element_type=F32)


def _mm_tn(a, b):
    return lax.dot_general(a.astype(BF16), b.astype(BF16), (((0,), (0,)), ((), ())), preferred_element_type=F32)


def _sigmoid(x):
    return 1.0 / (1.0 + jnp.exp(-x))


def _rms(x):
    return lax.rsqrt(jnp.mean(x * x, axis=-1, keepdims=True) + EPS)


def _rms_bwd(x, g, dh):
    r = _rms(x)
    n = x * r
    dn = dh * g
    dx = r * (dn - n * jnp.mean(dn * n, axis=-1, keepdims=True))
    return dx, jnp.sum(dh * n, axis=0, keepdims=True)


def _tri_dot(tri, x, passes):
    acc = None
    rest = x
    for _ in range(passes):
        piece = rest.astype(BF16)
        part = lax.dot_general(tri, piece, (((1,), (0,)), ((), ())), preferred_element_type=F32)
        acc = part if acc is None else acc + part
        rest = rest - piece.astype(F32)
    return acc


def _adam_update(g, w, m, v):
    nm = ADAM_B1 * m + (1.0 - ADAM_B1) * g
    nv = ADAM_B2 * v + (1.0 - ADAM_B2) * (g * g)
    m_hat = nm / (1.0 - ADAM_B1 ** ADAM_STEP)
    v_hat = nv / (1.0 - ADAM_B2 ** ADAM_STEP)
    return -ADAM_LR * (m_hat / (jnp.sqrt(v_hat) + ADAM_EPS) + ADAM_WD * w), nm, nv


def _full(shape):
    return pl.BlockSpec(shape, lambda *_: (0,) * len(shape))


VMEM_WHOLE = pl.BlockSpec(memory_space=pltpu.VMEM)
ANY_SPACE = pl.BlockSpec(memory_space=pl.ANY)


def _mesh_pos():
    return lax.axis_index("x"), lax.axis_index("y"), lax.axis_index("c")


def _flat(px, py, pc):
    return 4 * px + 2 * py + pc


def _all_gather_weights(shards, cast_only):
    n, nc = len(shards), len(cast_only)
    step = 64

    def body(*refs):
        x_refs, c_refs = refs[:n], refs[n:n + nc]
        out_refs, cast_refs = refs[n + nc:2 * n + nc], refs[2 * n + nc:2 * n + 2 * nc]
        bufs = refs[2 * n + 2 * nc:3 * n + 2 * nc]
        send_sems, recv_sems, local_sems = refs[3 * n + 2 * nc:]
        x, y, c = _mesh_pos()
        me, sibling = (x, y, c), (x, y, 1 - c)
        chips = [(1 - x, y), (x, 1 - y), (1 - x, 1 - y)]

        def copy(a, k, blk, to, src=None):
            rows = out_refs[a].at[_flat(*blk)]
            return pltpu.make_async_remote_copy(
                src_ref=rows if src is None else src, dst_ref=rows,
                send_sem=send_sems.at[7 * a + k], recv_sem=recv_sems.at[7 * a + k], device_id=to, device_id_type=MESH_ID)

        def cast_rows(src, dst, rows):
            def cast(i, carry):
                r0 = pl.multiple_of(i * step, step)
                dst[pl.ds(r0, step), :] = src[pl.ds(r0, step), :].astype(BF16)
                return carry
            lax.fori_loop(0, rows // step, cast, 0)

        first, mine = [], []
        for a in range(n):
            cast_rows(x_refs[a], bufs[a], shards[a].shape[0])
            mine.append(pltpu.make_async_copy(bufs[a], out_refs[a].at[_flat(*me)], local_sems.at[a]))
            first.append(copy(a, 0, me, sibling, src=bufs[a]))
            first += [copy(a, 1 + j, me, (*chip, c), src=bufs[a]) for j, chip in enumerate(chips)]
            for cp in [mine[-1]] + first[-4:]:
                cp.start()
        for a in range(nc):
            cast_rows(c_refs[a], cast_refs[a], cast_only[a].shape[0])
        passed = []
        for j, chip in enumerate(chips):
            for a in range(n):
                copy(a, 1 + j, (*chip, c), me).wait_recv()
                passed.append(copy(a, 4 + j, (*chip, c), sibling))
                passed[-1].start()
        for a in range(n):
            copy(a, 0, sibling, me).wait_recv()
            for j, chip in enumerate(chips):
                copy(a, 4 + j, (*chip, 1 - c), me).wait_recv()
        for cp in first + passed:
            cp.wait_send()
        for cp in mine:
            cp.wait()

    return pl.pallas_call(
        body, name="all_gather_w_in",
        out_shape=[jax.ShapeDtypeStruct((N_DEV,) + s.shape, BF16) for s in shards]
        + [jax.ShapeDtypeStruct(s.shape, BF16) for s in cast_only],
        in_specs=[VMEM_WHOLE] * (n + nc), out_specs=[ANY_SPACE] * n + [VMEM_WHOLE] * nc,
        scratch_shapes=[pltpu.VMEM(s.shape, BF16) for s in shards]
        + [pltpu.SemaphoreType.DMA((7 * n,)), pltpu.SemaphoreType.DMA((7 * n,)), pltpu.SemaphoreType.DMA((n,))],
        compiler_params=_params(),
    )(*shards, *cast_only)


HBM_SPEC = pl.BlockSpec(memory_space=pltpu.HBM)
SEM_SPEC = pl.BlockSpec(memory_space=pltpu.SEMAPHORE)
EFFECT = pltpu.SideEffectType.DATAFLOW_SIDE_EFFECTING
TOKEN = jax.ShapeDtypeStruct((8, 128), F32)


def _in_hbm(a):
    return pltpu.with_memory_space_constraint(a, pltpu.HBM)


def _split_start(copies_of, srcs, lands, n_sems, name):
    ns, nl, k = len(srcs), len(lands), len(n_sems)

    def body(*refs):
        src_refs, land_refs = refs[:ns], refs[ns:ns + nl]
        sems = refs[ns + nl:ns + nl + k]
        token = refs[-1]
        for cp in copies_of(src_refs, land_refs, sems):
            cp.start()
        token[...] = jnp.zeros_like(token)

    outs = pl.pallas_call(
        body, name=name,
        out_shape=[pltpu.SemaphoreType.DMA((q,)) for q in n_sems]
        + [pltpu.HBM(a.shape, a.dtype) for a in list(srcs) + list(lands)] + [TOKEN],
        in_specs=[HBM_SPEC] * (ns + nl),
        out_specs=[SEM_SPEC] * k + [HBM_SPEC] * (ns + nl) + [VMEM_WHOLE],
        input_output_aliases={i: k + i for i in range(ns + nl)},
        compiler_params=pltpu.CompilerParams(has_side_effects=EFFECT),
    )(*[_in_hbm(a) for a in list(srcs) + list(lands)])
    return outs[:k], outs[k:k + ns], outs[k + ns:k + ns + nl], outs[-1]


def _split_wait(copies_of, handle, after, name):
    sems, srcs, lands, _ = handle
    ns, nl, k = len(srcs), len(lands), len(sems)

    def body(*refs):
        src_refs, land_refs = refs[:ns], refs[ns:ns + nl]
        sem_refs = refs[ns + nl:ns + nl + k]
        for cp in copies_of(src_refs, land_refs, sem_refs):
            cp.wait()

    outs = pl.pallas_call(
        body, name=name,
        out_shape=[pltpu.HBM(a.shape, a.dtype) for a in list(srcs) + list(lands)],
        in_specs=[HBM_SPEC] * (ns + nl) + [SEM_SPEC] * k + [ANY_SPACE],
        out_specs=[HBM_SPEC] * (ns + nl),
        input_output_aliases={i: i for i in range(ns + nl)},
        compiler_params=pltpu.CompilerParams(has_side_effects=EFFECT),
    )(*srcs, *lands, *sems, after)
    return outs[ns:]


def _gather_first_copies(shard_refs, land_refs, sems):
    send_sems, recv_sems, local_sems = sems
    x, y, c = _mesh_pos()
    me = _flat(x, y, c)
    peers = [(x, y, 1 - c), (1 - x, y, c), (x, 1 - y, c), (1 - x, 1 - y, c)]
    copies = []
    for a, (shard, land) in enumerate(zip(shard_refs, land_refs)):
        copies.append(pltpu.make_async_copy(shard, land.at[me], local_sems.at[a]))
        for k, peer in enumerate(peers):
            copies.append(pltpu.make_async_remote_copy(
                src_ref=shard, dst_ref=land.at[me], send_sem=send_sems.at[4 * a + k], recv_sem=recv_sems.at[4 * a + k],
                device_id=peer, device_id_type=MESH_ID))
    return copies


def _gather_forward_copies(src_refs, land_refs, sems):
    del src_refs
    send_sems, recv_sems = sems
    x, y, c = _mesh_pos()
    chips = [(1 - x, y), (x, 1 - y), (1 - x, 1 - y)]
    copies = []
    for a, land in enumerate(land_refs):
        for j, chip in enumerate(chips):
            rows = land.at[_flat(*chip, c)]
            copies.append(pltpu.make_async_remote_copy(
                src_ref=rows, dst_ref=rows, send_sem=send_sems.at[3 * a + j], recv_sem=recv_sems.at[3 * a + j],
                device_id=(x, y, 1 - c), device_id_type=MESH_ID))
    return copies


def _gather_first_start(groups, name):
    shards = [s for g in groups for s in g]
    lands = [lax.empty((N_DEV,) + s.shape, s.dtype) for s in shards]
    bounds = [sum(len(g) for g in groups[:i]) for i in range(len(groups) + 1)]

    def copies_of(src_refs, land_refs, sems):
        copies = []
        for i in range(len(groups)):
            lo, hi = bounds[i], bounds[i + 1]
            copies += _gather_first_copies(src_refs[lo:hi], land_refs[lo:hi], sems[3 * i:3 * i + 3])
        return copies

    n_sems = tuple(q for g in groups for q in (4 * len(g), 4 * len(g), len(g)))
    sems, srcs, lands, token = _split_start(copies_of, shards, lands, n_sems, name)
    return [(sems[3 * i:3 * i + 3], srcs[bounds[i]:bounds[i + 1]], lands[bounds[i]:bounds[i + 1]], token)
            for i in range(len(groups))]


def _gather_forward_start(lands, name):
    n = len(lands)
    return _split_start(_gather_forward_copies, [], lands, (3 * n, 3 * n), name)


def _all_to_all_copies(n_scattered):
    def copies_of(src_refs, land_refs, sems):
        send_sems, recv_sems, local_sems = sems
        x, y, c = _mesh_pos()
        me = _flat(x, y, c)
        copies = []
        for a, (src, land) in enumerate(zip(src_refs, land_refs)):
            scattered = a < n_scattered
            copies.append(pltpu.make_async_copy(src.at[me] if scattered else src, land.at[me], local_sems.at[a]))
            for k in range(1, N_DEV):
                peer = (1 - x if k & 4 else x, 1 - y if k & 2 else y, 1 - c if k & 1 else c)
                copies.append(pltpu.make_async_remote_copy(
                    src_ref=src.at[_flat(*peer)] if scattered else src, dst_ref=land.at[me],
                    send_sem=send_sems.at[7 * a + k - 1], recv_sem=recv_sems.at[7 * a + k - 1],
                    device_id=peer, device_id_type=MESH_ID))
        return copies
    return copies_of


def _all_to_all_start(scattered, broadcast, name):
    srcs = list(scattered) + list(broadcast)
    lands = [lax.empty(a.shape, a.dtype) for a in scattered] + [lax.empty((N_DEV,) + a.shape, a.dtype) for a in broadcast]
    n = len(srcs)
    return _split_start(_all_to_all_copies(len(scattered)), srcs, lands, (7 * n, 7 * n, n), name)


def _call_behind(deps, body, *, in_specs, **kwargs):
    n_in, n_dep = len(in_specs), len(deps)

    def body_without_deps(*refs):
        return body(*refs[:n_in], *refs[n_in + n_dep:])

    call = pl.pallas_call(body_without_deps, in_specs=list(in_specs) + [ANY_SPACE] * n_dep, **kwargs)
    return lambda *operands: call(*operands, *deps)


def _row_tile(rows):
    for cand in (512, 256, 128, 64, 32, 16):
        if rows % cand == 0:
            return cand
    return rows


def _sum_sources(recv, name):
    _, rows, cols = recv.shape
    tile = _row_tile(rows)

    def body(r_ref, o_ref):
        acc = r_ref[0].astype(F32)
        for d in range(1, N_DEV):
            acc = acc + r_ref[d].astype(F32)
        o_ref[...] = acc

    return pl.pallas_call(
        body, name=name, grid=(rows // tile,),
        out_shape=jax.ShapeDtypeStruct((rows, cols), F32),
        in_specs=[pl.BlockSpec((N_DEV, tile, cols), lambda i: (0, i, 0))],
        out_specs=pl.BlockSpec((tile, cols), lambda i: (i, 0)),
        compiler_params=_params(("parallel",)),
    )(recv)


def _adamw(g, w, m, v, name):
    rows, cols = g.shape
    tile = _row_tile(rows)

    def body(g_ref, w_ref, m_ref, v_ref, d_ref, nm_ref, nv_ref):
        d_ref[...], nm_ref[...], nv_ref[...] = _adam_update(g_ref[...], w_ref[...], m_ref[...], v_ref[...])

    spec = pl.BlockSpec((tile, cols), lambda i: (i, 0))
    shp = jax.ShapeDtypeStruct((rows, cols), F32)
    return pl.pallas_call(
        body, name=name, grid=(rows // tile,), out_shape=[shp, shp, shp],
        in_specs=[spec] * 4, out_specs=[spec] * 3,
        compiler_params=_params(("parallel",)),
    )(g, w, m, v)


def _adamw_whole(g, w, m, v, name):
    def body(g_ref, w_ref, m_ref, v_ref, d_ref, nm_ref, nv_ref):
        d_ref[...], nm_ref[...], nv_ref[...] = _adam_update(g_ref[...], w_ref[...], m_ref[...], v_ref[...])

    shp = jax.ShapeDtypeStruct(g.shape, F32)
    return pl.pallas_call(
        body, name=name, out_shape=[shp, shp, shp], in_specs=[VMEM_WHOLE] * 4, out_specs=[VMEM_WHOLE] * 3,
        compiler_params=_params(),
    )(g, w, m, v)


def _sum_adamw(recv, w, m, v, name):
    _, rows, cols = recv.shape
    tile = _row_tile(rows)

    def body(r_ref, w_ref, m_ref, v_ref, g_ref, d_ref, nm_ref, nv_ref):
        acc = r_ref[0].astype(F32)
        for d in range(1, N_DEV):
            acc = acc + r_ref[d].astype(F32)
        g_ref[...] = acc
        d_ref[...], nm_ref[...], nv_ref[...] = _adam_update(acc, w_ref[...], m_ref[...], v_ref[...])

    spec = pl.BlockSpec((tile, cols), lambda i: (i, 0))
    shp = jax.ShapeDtypeStruct((rows, cols), F32)
    return pl.pallas_call(
        body, name=name, grid=(rows // tile,), out_shape=[shp] * 4,
        in_specs=[pl.BlockSpec((N_DEV, tile, cols), lambda i: (0, i, 0)), spec, spec, spec], out_specs=[spec] * 4,
        compiler_params=_params(("parallel",)),
    )(recv, w, m, v)


SMALL_SLOTS = {"norm_x_g": (0, 0, 1, D_MODEL), "norm_mem_g": (0, 8, 1, D_MODEL), "norm_ffn_g": (0, 16, 1, D_MODEL),
               "final_norm_g": (0, 24, 1, D_MODEL), "pool_scale": (0, 32, 1, HW),
               "lb_logits": (1, 0, 2, HW), "hgrn_norm_g": (1, 8, HEADS, HD), "norm_mix_g": (2, 0, 1, D_MODEL)}
LOSS_ROW = 25
SMALL_ORDER = ("norm_mix_g", "lb_logits", "hgrn_norm_g", "pool_scale", "norm_x_g", "norm_mem_g", "norm_ffn_g",
               "final_norm_g", "w_pool")


def _small_update(srecvs, wprecv, params):
    flat = [t for n in SMALL_ORDER for t in params[n]]
    nb = len(srecvs)
    n_in = nb + 1 + len(flat)

    def body(*refs):
        s_refs, wp_ref = refs[0:nb], refs[nb]
        in_refs = refs[nb + 1:n_in]
        loss_ref = refs[n_in]
        out_refs = refs[n_in + 1:-nb]
        accs = refs[-nb:]
        for s_ref, acc in zip(s_refs, accs):
            total = s_ref[0]
            for d in range(1, N_DEV):
                total = total + s_ref[d]
            acc[...] = total
        loss_ref[...] = accs[0][LOSS_ROW:LOSS_ROW + 1, 0:1]
        for i, name in enumerate(SMALL_ORDER):
            w_ref, m_ref, v_ref = in_refs[3 * i:3 * i + 3]
            g_ref, d_ref, nm_ref, nv_ref = out_refs[4 * i:4 * i + 4]
            if name == "w_pool":
                g = wp_ref[0]
                for d in range(1, N_DEV):
                    g = g + wp_ref[d]
            else:
                buf, r0, nr, nc = SMALL_SLOTS[name]
                g = accs[buf][r0:r0 + nr, 0:nc]
            g_ref[...] = g
            d_ref[...], nm_ref[...], nv_ref[...] = _adam_update(g, w_ref[...], m_ref[...], v_ref[...])

    out_shape = [jax.ShapeDtypeStruct((1, 1), F32)]
    for n in SMALL_ORDER:
        out_shape += [jax.ShapeDtypeStruct(params[n][0].shape, F32)] * 4
    outs = pl.pallas_call(
        body, name="small_update", out_shape=out_shape,
        in_specs=[VMEM_WHOLE] * n_in, out_specs=[VMEM_WHOLE] * len(out_shape),
        scratch_shapes=[pltpu.VMEM(r.shape[1:], F32) for r in srecvs],
        compiler_params=_params(),
    )(*srecvs, wprecv, *flat)
    return outs[0], {n: outs[1 + 4 * i:5 + 4 * i] for i, n in enumerate(SMALL_ORDER)}


def _in_proj(x, g, w_t, deps):
    s = x.shape[0]
    tm = min(WIDE_ROW_TILE, s)

    def body(x_ref, g_ref, w_ref, z_ref, h_ref):
        xv = x_ref[...]
        h = (xv * _rms(xv) * g_ref[...]).astype(BF16)
        h_ref[...] = h
        z_ref[...] = _mm_nt(h, w_ref[...])

    return _call_behind(
        deps, body, name="in_proj", grid=(s // tm,),
        out_shape=[jax.ShapeDtypeStruct((s, IN_WIDTH), F32), jax.ShapeDtypeStruct((s, D_MODEL), BF16)],
        in_specs=[pl.BlockSpec((tm, D_MODEL), lambda i: (i, 0)), _full((1, D_MODEL)), VMEM_WHOLE],
        out_specs=[pl.BlockSpec((tm, IN_WIDTH), lambda i: (i, 0)), pl.BlockSpec((tm, D_MODEL), lambda i: (i, 0))],
        compiler_params=_params(("parallel",)),
    )(x, g, w_t)


def _chunk_masks():
    row = lax.broadcasted_iota(jnp.int32, (CHUNK, CHUNK), 0)
    col = lax.broadcasted_iota(jnp.int32, (CHUNK, CHUNK), 1)
    return row, col


def _ones_where(mask):
    return jnp.where(mask, 1.0, 0.0).astype(BF16)


def _hgrn_gates(zq, zf, lb):
    sq = _sigmoid(zq)
    sig = _sigmoid(zf)
    f = lb + (1.0 - lb) * sig
    return zq * sq, sq, sig, f


def _sub_chunk_masks(width):
    trow = lax.broadcasted_iota(jnp.int32, (CHUNK, width), 0)
    return [(trow >= SUB * j) & (trow < SUB * (j + 1)) for j in range(N_SUB)]


def _head(a, h):
    return a[:, HD * h:HD * (h + 1)]


def _lanes(parts):
    return jnp.concatenate(parts, axis=1)


def _hgrn_decay_factors(b_scr, r0, b, in_sub):
    bases = [jnp.zeros((1, HW), F32)] + [b_scr[r0 + SUB * j - 1:r0 + SUB * j, :] for j in range(1, N_SUB)]
    own_base = bases[N_SUB - 1]
    for j in range(N_SUB - 2, -1, -1):
        own_base = jnp.where(in_sub[j], bases[j], own_base)
    eq = jnp.exp(b - own_base)
    ek = []
    for j in range(N_SUB):
        upto = SUB * (j + 1)
        e = jnp.exp(jnp.minimum(bases[j] - b[0:upto], EXP_CAP))
        ek.append(e if upto == CHUNK else jnp.concatenate([e, jnp.zeros((CHUNK - upto, HW), F32)], axis=0))
    return eq, ek


def _per_sub_chunk(x, in_sub):
    return _lanes([jnp.where(in_sub[j], x, 0.0) for j in range(N_SUB)])


def _own_lane_block(a, in_sub):
    out = a[:, HD * (N_SUB - 1):HD * N_SUB]
    for j in range(N_SUB - 2, -1, -1):
        out = jnp.where(in_sub[j], a[:, HD * j:HD * (j + 1)], out)
    return out


def _head_rms(o):
    return _lanes([jnp.broadcast_to(_rms(_head(o, h)), (CHUNK, HD)) for h in range(HEADS)])


def _head_mean(a):
    return _lanes([jnp.broadcast_to(jnp.mean(_head(a, h), axis=-1, keepdims=True), (CHUNK, HD)) for h in range(HEADS)])


def _hgrn_fwd(z, lb_logits, gn):
    s = z.shape[0]
    n_chunks = s // CHUNK

    def body(zq_ref, zf_ref, zi_ref, zg_ref, lbl_ref, gn_ref, oa_ref, o_ref, st_ref, state, b_scr):
        @pl.when(pl.program_id(0) == 0)
        def _():
            state[...] = jnp.zeros_like(state)

        lb = _sigmoid(lbl_ref[0:1, :] - lbl_ref[1:2, :])
        row, col = _chunk_masks()
        causal = col <= row
        tri = _ones_where(causal)
        in_sub, in_sub_head = _sub_chunk_masks(HW), _sub_chunk_masks(HD)
        gn_row = _lanes([gn_ref[h:h + 1, :] for h in range(HEADS)])
        for c in range(CHUNKS_PER_STEP):
            r0 = CHUNK * c
            rs = slice(r0, r0 + CHUNK)
            st_ref[c] = state[...]
            q, _, _, f = _hgrn_gates(zq_ref[rs, :], zf_ref[rs, :], lb)
            kk = 1.0 - f
            b = _tri_dot(tri, jnp.log(f), 3)
            b_scr[rs, :] = b
            eq, ek = _hgrn_decay_factors(b_scr, r0, b, in_sub)
            b_last = b_scr[r0 + CHUNK - 1:r0 + CHUNK, :]
            qe, qg = q * eq, q * jnp.exp(b)
            ke = [kk * e for e in ek]
            kd = kk * jnp.exp(b_last - b)
            lam_last = jnp.exp(b_last)
            v = zi_ref[rs, :]
            o_heads = []
            for h in range(HEADS):
                vh, st = _head(v, h), state[h]
                a = jnp.where(causal, _mm_nt(_per_sub_chunk(_head(qe, h), in_sub_head),
                                             _lanes([_head(ke[j], h) for j in range(N_SUB)])), 0.0)
                o_heads.append(_mm(a, vh) + _mm_nt(_head(qg, h), st))
                state[h] = st * _head(lam_last, h) + _mm_tn(vh, _head(kd, h))
            o = _lanes(o_heads)
            o_ref[rs, :] = o
            zg = zg_ref[rs, :]
            oa_ref[rs, :] = (o * _head_rms(o) * gn_row * zg * _sigmoid(zg)).astype(BF16)

    rows = CHUNK * CHUNKS_PER_STEP
    zspec = lambda cb: pl.BlockSpec((rows, HW), lambda i, cb=cb: (i, cb))
    return pl.pallas_call(
        body, name="hgrn_fwd", grid=(s // rows,),
        out_shape=[jax.ShapeDtypeStruct((s, 2 * HW), BF16), jax.ShapeDtypeStruct((s, HW), F32),
                   jax.ShapeDtypeStruct((n_chunks, HEADS, HD, HD), F32)],
        in_specs=[zspec(0), zspec(1), zspec(2), zspec(3), _full((2, HW)), _full((HEADS, HD))],
        out_specs=[pl.BlockSpec((rows, HW), lambda i: (i, 0)), pl.BlockSpec((rows, HW), lambda i: (i, 0)),
                   pl.BlockSpec((CHUNKS_PER_STEP, HEADS, HD, HD), lambda i: (i, 0, 0, 0))],
        scratch_shapes=[pltpu.VMEM((HEADS, HD, HD), F32), pltpu.VMEM((rows, HW), F32)],
        compiler_params=_params(("arbitrary",)),
    )(z, z, z, z, lb_logits, gn)


def _pool_counts(tile_idx, tm):
    t = tile_idx * tm + lax.broadcasted_iota(jnp.int32, (tm, 1), 0)
    return [1.0 / jnp.minimum(t + 1, w).astype(F32) for w in POOL_WINDOWS]


def _pool_fwd(z, w_pool, scale, mixed_in, deps):
    s = z.shape[0]
    tm = min(ROW_TILE, s)

    def body(p_ref, w_ref, sc_ref, mixin_ref, ob_ref, pooled_ref, ext):
        i = pl.program_id(0)

        @pl.when(i == 0)
        def _():
            ext[0:POOL_HALO, :] = jnp.zeros((POOL_HALO, HW), F32)

        @pl.when(i > 0)
        def _():
            ext[0:POOL_HALO, :] = ext[tm:tm + POOL_HALO, :]

        ext[POOL_HALO:POOL_HALO + tm, :] = p_ref[...]
        inv = _pool_counts(i, tm)
        for g, w in enumerate(POOL_WINDOWS):
            sl = slice(HD * g, HD * (g + 1))
            p = ext[POOL_HALO:POOL_HALO + tm, sl]
            win = p
            for d in range(1, w):
                win = win + ext[POOL_HALO - d:POOL_HALO - d + tm, sl]
            pooled = (win * inv[g] - p).astype(BF16)
            pooled_ref[:, sl] = pooled
            ob_ref[:, sl] = (_mm(pooled, w_ref[g]) * sc_ref[:, sl]).astype(BF16)

    return _call_behind(
        deps, body, name="pool_fwd", grid=(s // tm,),
        out_shape=[jax.ShapeDtypeStruct((s, 2 * HW), BF16), jax.ShapeDtypeStruct((s, HW), BF16)],
        in_specs=[pl.BlockSpec((tm, HW), lambda i: (i, 4)), _full((HEADS, HD, HD)), _full((1, HW)), ANY_SPACE],
        out_specs=[pl.BlockSpec((tm, HW), lambda i: (i, 1)), pl.BlockSpec((tm, HW), lambda i: (i, 0))],
        scratch_shapes=[pltpu.VMEM((tm + POOL_HALO, HW), F32)],
        input_output_aliases={3: 0},
        compiler_params=_params(("arbitrary",)),
    )(z, w_pool, scale, mixed_in)


def _out_proj(x, mixed, w_out):
    s = x.shape[0]
    tm = min(WIDE_ROW_TILE, s)

    def body(x_ref, a_ref, w_ref, o_ref):
        o_ref[...] = x_ref[...] + _mm(a_ref[...], w_ref[...])

    return pl.pallas_call(
        body, name="out_proj", grid=(s // tm,),
        out_shape=jax.ShapeDtypeStruct((s, D_MODEL), F32),
        in_specs=[pl.BlockSpec((tm, D_MODEL), lambda i: (i, 0)), pl.BlockSpec((tm, D_MODEL), lambda i: (i, 0)), VMEM_WHOLE],
        out_specs=pl.BlockSpec((tm, D_MODEL), lambda i: (i, 0)),
        compiler_params=_params(("parallel",)),
    )(x, mixed, w_out)


def _mem_kv(mem, g, wk, wv, deps):
    def body(m_ref, g_ref, wk_ref, wv_ref, hm_ref, k_ref, v_ref):
        m = m_ref[...]
        hm = (m * _rms(m) * g_ref[...]).astype(BF16)
        hm_ref[...] = hm
        k_ref[...] = _mm(hm, wk_ref[...]).astype(BF16)
        v_ref[...] = _mm(hm, wv_ref[...]).astype(BF16)

    shp = jax.ShapeDtypeStruct((MEM_LEN, D_MODEL), BF16)
    return _call_behind(
        deps, body, name="mem_kv", out_shape=[shp, shp, shp],
        in_specs=[VMEM_WHOLE] * 4, out_specs=[VMEM_WHOLE] * 3,
        compiler_params=_params(),
    )(mem, g, wk, wv)


def _softmax_rows(sc):
    e = jnp.exp(sc - jnp.max(sc, axis=-1, keepdims=True))
    return e / jnp.sum(e, axis=-1, keepdims=True)


def _xattn_fwd(x, g, wq, xk, xv, wo_t, deps):
    s = x.shape[0]
    tm = min(WIDE_ROW_TILE, s)
    scale = XHD ** -0.5

    def body(x_ref, g_ref, wq_ref, k_ref, v_ref, wo_ref, o_ref, hq_ref, q_ref, att_ref):
        xv_ = x_ref[...]
        hq = (xv_ * _rms(xv_) * g_ref[...]).astype(BF16)
        hq_ref[...] = hq
        q_ref[...] = (_mm(hq, wq_ref[...]) * scale).astype(BF16)
        for h in range(HEADS):
            sl = slice(XHD * h, XHD * (h + 1))
            p = _softmax_rows(_mm_nt(q_ref[:, sl], k_ref[:, sl]))
            att_ref[:, sl] = _mm(p, v_ref[:, sl]).astype(BF16)
        o_ref[...] = xv_ + _mm_nt(att_ref[...], wo_ref[...])

    row_f32 = pl.BlockSpec((tm, D_MODEL), lambda i: (i, 0))
    bshape = jax.ShapeDtypeStruct((s, D_MODEL), BF16)
    return _call_behind(
        deps, body, name="xattn_fwd", grid=(s // tm,),
        out_shape=[jax.ShapeDtypeStruct((s, D_MODEL), F32), bshape, bshape, bshape],
        in_specs=[row_f32, _full((1, D_MODEL)), VMEM_WHOLE, VMEM_WHOLE, VMEM_WHOLE, VMEM_WHOLE],
        out_specs=[row_f32] * 4,
        compiler_params=_params(("parallel",)),
    )(x, g, wq, xk, xv, wo_t)


def _mlp_fwd_loss(x, g, w1, w2, gf, target):
    s = x.shape[0]
    tm = min(ROW_TILE, s)

    def body(x_ref, g_ref, w1_ref, w2_ref, gf_ref, t_ref, dx_ref, dx16_ref, u_ref, hf_ref, slot_ref):
        @pl.when(pl.program_id(0) == 0)
        def _():
            slot_ref[...] = jnp.zeros_like(slot_ref)

        xv = x_ref[...]
        hf = (xv * _rms(xv) * g_ref[...]).astype(BF16)
        hf_ref[...] = hf
        for j in range(N_DEV):
            a = jnp.maximum(_mm(hf, w1_ref[j]), 0.0)
            u_ref[:, FF_BLK * j:FF_BLK * (j + 1)] = (a * a).astype(BF16)
        acc = xv + _mm(u_ref[...], w2_ref[...])
        gfv = gf_ref[...]
        r = _rms(acc)
        n = acc * r
        err = n * gfv - t_ref[...]
        slot_ref[1:2, :] += jnp.sum(jnp.mean(err * err, axis=-1, keepdims=True), axis=0, keepdims=True) * 0.5
        dy = err * (1.0 / D_MODEL)
        slot_ref[0:1, :] += jnp.sum(dy * n, axis=0, keepdims=True)
        dn = dy * gfv
        dx = r * (dn - n * jnp.mean(dn * n, axis=-1, keepdims=True))
        dx_ref[...] = dx
        dx16_ref[...] = dx.astype(BF16)

    row_f32 = pl.BlockSpec((tm, D_MODEL), lambda i: (i, 0))
    return pl.pallas_call(
        body, name="mlp_fwd_loss", grid=(s // tm,),
        out_shape=[jax.ShapeDtypeStruct((s, D_MODEL), F32), jax.ShapeDtypeStruct((s, D_MODEL), BF16),
                   jax.ShapeDtypeStruct((s, D_FF), BF16), jax.ShapeDtypeStruct((s, D_MODEL), BF16),
                   jax.ShapeDtypeStruct((SLOT, D_MODEL), F32)],
        in_specs=[row_f32, _full((1, D_MODEL)), VMEM_WHOLE, VMEM_WHOLE, _full((1, D_MODEL)), row_f32],
        out_specs=[row_f32, row_f32, pl.BlockSpec((tm, D_FF), lambda i: (i, 0)), row_f32, _full((SLOT, D_MODEL))],
        compiler_params=_params(("arbitrary",)),
    )(x, g, w1, w2, gf, target)


def _zero_slot(slot_ref):
    @pl.when(pl.program_id(0) == 0)
    def _():
        slot_ref[...] = jnp.zeros_like(slot_ref)


def _mlp_bwd(dx3, u, x2, g, w1, w2, deps):
    s = x2.shape[0]
    tm = min(ROW_TILE, s)

    def body(d_ref, u_ref, x_ref, g_ref, w1_ref, w2_ref, da_ref, dx_ref, slot_ref):
        _zero_slot(slot_ref)
        d = d_ref[...]
        d16 = d.astype(BF16)
        dhf = jnp.zeros((tm, D_MODEL), F32)
        for j in range(N_DEV):
            sl = slice(FF_BLK * j, FF_BLK * (j + 1))
            u = u_ref[:, sl].astype(F32)
            da = (_mm_nt(d16, w2_ref[j]) * (2.0 * u * lax.rsqrt(jnp.maximum(u, TINY)))).astype(BF16)
            da_ref[:, sl] = da
            dhf = dhf + _mm_nt(da, w1_ref[j])
        dx, dg = _rms_bwd(x_ref[...], g_ref[...], dhf)
        dx_ref[...] = d + dx
        slot_ref[0:1, :] += dg

    row_f32 = pl.BlockSpec((tm, D_MODEL), lambda i: (i, 0))
    return _call_behind(
        deps, body, name="mlp_bwd", grid=(s // tm,),
        out_shape=[jax.ShapeDtypeStruct((s, D_FF), BF16), jax.ShapeDtypeStruct((s, D_MODEL), F32),
                   jax.ShapeDtypeStruct((SLOT, D_MODEL), F32)],
        in_specs=[row_f32, pl.BlockSpec((tm, D_FF), lambda i: (i, 0)), row_f32, _full((1, D_MODEL)),
                  VMEM_WHOLE, VMEM_WHOLE],
        out_specs=[pl.BlockSpec((tm, D_FF), lambda i: (i, 0)), row_f32, _full((SLOT, D_MODEL))],
        compiler_params=_params(("arbitrary",)),
    )(dx3, u, x2, g, w1, w2)


def _wgrad(a, b, name, col_blocks=False):
    s, m = a.shape
    n = b.shape[1]
    tm = 1280 if m % 1280 == 0 else min(1024, m)
    tn = min(1024, n)
    blk = n // N_DEV
    per_step = tn // blk if col_blocks else 1
    ts = min(2 * ROW_TILE, s)
    n_s = s // ts

    def body(a_ref, b_ref, o_ref, acc):
        k = pl.program_id(2)

        @pl.when(k == 0)
        def _():
            acc[...] = jnp.zeros_like(acc)

        acc[...] += _mm_tn(a_ref[...], b_ref[...])

        @pl.when(k == n_s - 1)
        def _():
            if col_blocks:
                for p in range(per_step):
                    o_ref[p] = acc[:, blk * p:blk * (p + 1)].astype(BF16)
            else:
                o_ref[...] = acc[...].astype(BF16)

    if col_blocks:
        out_shape = jax.ShapeDtypeStruct((N_DEV, m, blk), BF16)
        out_spec = pl.BlockSpec((per_step, tm, blk), lambda i, j, k: (j, i, 0))
    else:
        out_shape = jax.ShapeDtypeStruct((m, n), BF16)
        out_spec = pl.BlockSpec((tm, tn), lambda i, j, k: (i, j))
    return pl.pallas_call(
        body, name=name, grid=(m // tm, n // tn, n_s), out_shape=out_shape,
        in_specs=[pl.BlockSpec((ts, tm), lambda i, j, k: (k, i)), pl.BlockSpec((ts, tn), lambda i, j, k: (k, j))],
        out_specs=out_spec,
        scratch_shapes=[pltpu.VMEM((tm, tn), F32)],
        compiler_params=_params(("parallel", "parallel", "arbitrary")),
    )(a, b)


def _xattn_bwd(dx2, x1, g, q, xk, xv, wq, wo_t, deps):
    s = x1.shape[0]
    tm = min(ROW_TILE, s)
    scale = XHD ** -0.5

    def body(d_ref, x_ref, g_ref, q_ref, k_ref, v_ref, wq_ref, wo_ref, dx_ref, dx16_ref, dq_ref, dk_ref, dv_ref, slot_ref,
             datt):
        _zero_slot(slot_ref)

        @pl.when(pl.program_id(0) == 0)
        def _():
            dk_ref[...] = jnp.zeros_like(dk_ref)
            dv_ref[...] = jnp.zeros_like(dv_ref)

        d = d_ref[...]
        datt[...] = _mm(d, wo_ref[...]).astype(BF16)
        for h in range(HEADS):
            sl = slice(XHD * h, XHD * (h + 1))
            qh, kh, vh, dah = q_ref[:, sl], k_ref[:, sl], v_ref[:, sl], datt[:, sl]
            p = _softmax_rows(_mm_nt(qh, kh))
            dp = _mm_nt(dah, vh)
            ds = (p * (dp - jnp.sum(dp * p, axis=-1, keepdims=True))).astype(BF16)
            dq_ref[:, sl] = (_mm(ds, kh) * scale).astype(BF16)
            dk_ref[:, sl] += _mm_tn(ds, qh)
            dv_ref[:, sl] += _mm_tn(p, dah)
        dx, dg = _rms_bwd(x_ref[...], g_ref[...], _mm_nt(dq_ref[...], wq_ref[...]))
        dx_ref[...] = d + dx
        dx16_ref[...] = (d + dx).astype(BF16)
        slot_ref[0:1, :] += dg

    row_f32 = pl.BlockSpec((tm, D_MODEL), lambda i: (i, 0))
    kv = jax.ShapeDtypeStruct((MEM_LEN, D_MODEL), F32)
    tokens16 = jax.ShapeDtypeStruct((s, D_MODEL), BF16)
    return _call_behind(
        deps, body, name="xattn_bwd", grid=(s // tm,),
        out_shape=[jax.ShapeDtypeStruct((s, D_MODEL), F32), tokens16, tokens16, kv, kv,
                   jax.ShapeDtypeStruct((SLOT, D_MODEL), F32)],
        in_specs=[row_f32, row_f32, _full((1, D_MODEL)), row_f32, VMEM_WHOLE, VMEM_WHOLE, VMEM_WHOLE, VMEM_WHOLE],
        out_specs=[row_f32, row_f32, row_f32, _full((MEM_LEN, D_MODEL)), _full((MEM_LEN, D_MODEL)),
                   _full((SLOT, D_MODEL))],
        scratch_shapes=[pltpu.VMEM((tm, D_MODEL), BF16)],
        compiler_params=_params(("arbitrary",)),
    )(dx2, x1, g, q, xk, xv, wq, wo_t)


def _mem_bwd(mem, g, hm, dxk, dxv, wk, wv):
    def body(m_ref, g_ref, hm_ref, dk_ref, dv_ref, wk_ref, wv_ref, dwk_ref, dwv_ref, slot_ref):
        dk, dv = dk_ref[...], dv_ref[...]
        hm_ = hm_ref[...]
        dwk_ref[...] = _mm_tn(hm_, dk).astype(BF16)
        dwv_ref[...] = _mm_tn(hm_, dv).astype(BF16)
        _, dg = _rms_bwd(m_ref[...], g_ref[...], _mm_nt(dk, wk_ref[...]) + _mm_nt(dv, wv_ref[...]))
        slot_ref[...] = jnp.zeros_like(slot_ref)
        slot_ref[0:1, :] = dg

    wshape = jax.ShapeDtypeStruct((D_MODEL, D_MODEL), BF16)
    return pl.pallas_call(
        body, name="mem_bwd", out_shape=[wshape, wshape, jax.ShapeDtypeStruct((SLOT, D_MODEL), F32)],
        in_specs=[VMEM_WHOLE] * 7, out_specs=[VMEM_WHOLE] * 3,
        compiler_params=_params(),
    )(mem, g, hm, dxk, dxv, wk, wv)


def _pool_bwd(dx1, w_out, pooled, w_pool, scale, deps):
    s = dx1.shape[0]
    tm = min(ROW_TILE, s)
    n_t = s // tm

    def body(dx_ref, wo_ref, pl_ref, w_ref, sc_ref, dz_ref, dw_ref, slot_ref, ext, do_ref):
        i = pl.program_id(0)
        tile = n_t - 1 - i
        _zero_slot(slot_ref)
        do_ref[...] = _mm_nt(dx_ref[...], wo_ref[HW:2 * HW, :])

        @pl.when(i == 0)
        def _():
            dw_ref[...] = jnp.zeros_like(dw_ref)
            ext[tm:tm + POOL_HALO, :] = jnp.zeros((POOL_HALO, HW), F32)

        @pl.when(i > 0)
        def _():
            ext[tm:tm + POOL_HALO, :] = ext[0:POOL_HALO, :]

        inv = _pool_counts(tile, tm)
        dpooled = []
        for g in range(HEADS):
            sl = slice(HD * g, HD * (g + 1))
            pooled_g = pl_ref[:, sl]
            do = do_ref[:, sl]
            slot_ref[0:1, sl] += jnp.sum(_mm(pooled_g, w_ref[g]) * do, axis=0, keepdims=True)
            dy = (do * sc_ref[:, sl]).astype(BF16)
            dw_ref[g] += _mm_tn(pooled_g, dy)
            dpo = _mm_nt(dy, w_ref[g])
            dpooled.append(dpo)
            ext[0:tm, sl] = dpo * inv[g]
        for g, w in enumerate(POOL_WINDOWS):
            sl = slice(HD * g, HD * (g + 1))
            win = ext[0:tm, sl]
            for d in range(1, w):
                win = win + ext[d:d + tm, sl]
            dz_ref[:, sl] = (win - dpooled[g]).astype(BF16)

    return _call_behind(
        deps, body, name="pool_bwd", grid=(n_t,),
        out_shape=[jax.ShapeDtypeStruct((s, IN_WIDTH), BF16), jax.ShapeDtypeStruct((HEADS, HD, HD), F32),
                   jax.ShapeDtypeStruct((SLOT, D_MODEL), F32)],
        in_specs=[pl.BlockSpec((tm, D_MODEL), lambda i: (n_t - 1 - i, 0)), VMEM_WHOLE,
                  pl.BlockSpec((tm, HW), lambda i: (n_t - 1 - i, 0)), _full((HEADS, HD, HD)), _full((1, HW))],
        out_specs=[pl.BlockSpec((tm, HW), lambda i: (n_t - 1 - i, 4)), _full((HEADS, HD, HD)), _full((SLOT, D_MODEL))],
        scratch_shapes=[pltpu.VMEM((tm + POOL_HALO, HW), F32), pltpu.VMEM((tm, HW), F32)],
        compiler_params=_params(("arbitrary",)),
    )(dx1, w_out, pooled, w_pool, scale)


def _hgrn_bwd(z, o, dx1, w_out, states, lb_logits, gn, dz_in, deps):
    s = z.shape[0]
    n_chunks = s // CHUNK

    def body(zq_ref, zf_ref, zi_ref, zg_ref, o_ref, dx_ref, wo_ref, st_ref, lbl_ref, gn_ref, dzin_ref,
             dz_ref, dlb_ref, dgn_ref, dstate, b_scr, dlb_acc, do_ref):
        i = pl.program_id(0)

        @pl.when(i == 0)
        def _():
            dstate[...] = jnp.zeros_like(dstate)
            dlb_acc[...] = jnp.zeros_like(dlb_acc)
            dgn_ref[...] = jnp.zeros_like(dgn_ref)
            dlb_ref[...] = jnp.zeros_like(dlb_ref)

        do_ref[...] = _mm_nt(dx_ref[...], wo_ref[0:HW, :])
        lb = _sigmoid(lbl_ref[0:1, :] - lbl_ref[1:2, :])
        row, col = _chunk_masks()
        causal = col <= row
        tri = _ones_where(causal)
        upper = _ones_where(col >= row)
        strict_lower = _ones_where(col < row)
        in_sub, in_sub_head = _sub_chunk_masks(HW), _sub_chunk_masks(HD)
        gn_row = _lanes([gn_ref[h:h + 1, :] for h in range(HEADS)])
        dlb_sum, dgn_sum = 0.0, 0.0
        for c in reversed(range(CHUNKS_PER_STEP)):
            r0 = CHUNK * c
            rs = slice(r0, r0 + CHUNK)
            zq = zq_ref[rs, :]
            q, sq, sig, f = _hgrn_gates(zq, zf_ref[rs, :], lb)
            kk = 1.0 - f
            b = _tri_dot(tri, jnp.log(f), 3)
            b_scr[rs, :] = b
            v = zi_ref[rs, :]
            o, zg, doa = o_ref[rs, :], zg_ref[rs, :], do_ref[rs, :]
            sg = _sigmoid(zg)
            n = o * _head_rms(o)
            don = doa * (zg * sg)
            dgn_sum = dgn_sum + jnp.sum(don * n, axis=0, keepdims=True)
            dn = don * gn_row
            d_o = _head_rms(o) * (dn - n * _head_mean(dn * n))
            dz_ref[rs, 3 * HW:4 * HW] = (doa * (n * gn_row) * (sg * (1.0 + zg * (1.0 - sg)))).astype(BF16)
            eq, ek = _hgrn_decay_factors(b_scr, r0, b, in_sub)
            b_last = b_scr[r0 + CHUNK - 1:r0 + CHUNK, :]
            lam, e_last, lam_last = jnp.exp(b), jnp.exp(b_last - b), jnp.exp(b_last)
            qe, qg, kd = q * eq, q * lam, kk * e_last
            ke = [kk * e for e in ek]
            dv_h, gq_h, gk_h, dqi_h, dkd_h, st_h = [], [], [], [], [], []
            for h in range(HEADS):
                vh, doh = _head(v, h), _head(d_o, h)
                st0, ds1 = st_ref[c, h], dstate[h]
                q16 = _per_sub_chunk(_head(qe, h), in_sub_head).astype(BF16)
                ke16 = _lanes([_head(ke[j], h) for j in range(N_SUB)]).astype(BF16)
                a = jnp.where(causal, _mm_nt(q16, ke16), 0.0)
                da = jnp.where(causal, _mm_nt(doh, vh), 0.0)
                dv_h.append(_mm_tn(a, doh) + _mm_nt(_head(kd, h), ds1))
                gq_h.append(_own_lane_block(_mm(da, ke16), in_sub_head))
                gk_h.append(_mm_tn(da, q16))
                dqi_h.append(_mm(doh, st0))
                dkd_h.append(_mm(vh, ds1))
                st_h.append(jnp.sum(st0 * ds1, axis=0, keepdims=True))
                dstate[h] = ds1 * _head(lam_last, h) + _mm_tn(doh, _head(qg, h))
            dz_ref[rs, 2 * HW:3 * HW] = _lanes(dv_h).astype(BF16)
            gq = _lanes(gq_h)
            gk = [_lanes([gk_h[h][:, HD * j:HD * (j + 1)] for h in range(HEADS)]) for j in range(N_SUB)]
            dq_inter = lam * _lanes(dqi_h)
            dq = eq * gq + dq_inter
            dk_intra = sum(ek[j] * gk[j] for j in range(N_SUB))
            dk_state = _lanes(dkd_h) * e_last
            db_intra = (qe.astype(BF16).astype(F32) * gq
                        - sum(ke[j].astype(BF16).astype(F32) * gk[j] for j in range(N_SUB)))
            dlf = (_tri_dot(upper, db_intra + q * dq_inter, 2) + _tri_dot(strict_lower, kk * dk_state, 2)
                   + lam_last * _lanes(st_h))
            df = dlf / f - (dk_intra + dk_state)
            dlb_sum = dlb_sum + jnp.sum(df * (1.0 - sig), axis=0, keepdims=True)
            dz_ref[rs, HW:2 * HW] = (df * (1.0 - lb) * sig * (1.0 - sig)).astype(BF16)
            dz_ref[rs, 0:HW] = (dq * (sq * (1.0 + zq * (1.0 - sq)))).astype(BF16)
        dlb_acc[...] += dlb_sum
        for h in range(HEADS):
            dgn_ref[h:h + 1, 0:HD] += _head(dgn_sum, h)

        @pl.when(i == n_steps - 1)
        def _():
            dl0 = dlb_acc[...] * lb * (1.0 - lb)
            dlb_ref[0:1, 0:HW] = dl0
            dlb_ref[1:2, 0:HW] = -dl0

    rows = CHUNK * CHUNKS_PER_STEP
    n_steps = s // rows
    rev = lambda i: n_steps - 1 - i
    zspec = lambda cb: pl.BlockSpec((rows, HW), lambda i, cb=cb: (rev(i), cb))
    slot = jax.ShapeDtypeStruct((SLOT, D_MODEL), F32)
    return _call_behind(
        deps, body, name="hgrn_bwd", grid=(n_steps,),
        out_shape=[jax.ShapeDtypeStruct((s, IN_WIDTH), BF16), slot, slot],
        in_specs=[zspec(0), zspec(1), zspec(2), zspec(3), pl.BlockSpec((rows, HW), lambda i: (rev(i), 0)),
                  pl.BlockSpec((rows, D_MODEL), lambda i: (rev(i), 0)), VMEM_WHOLE,
                  pl.BlockSpec((CHUNKS_PER_STEP, HEADS, HD, HD), lambda i: (rev(i), 0, 0, 0)), _full((2, HW)),
                  _full((HEADS, HD)), ANY_SPACE],
        out_specs=[pl.BlockSpec((rows, 4 * HW), lambda i: (rev(i), 0)), _full((SLOT, D_MODEL)), _full((SLOT, D_MODEL))],
        scratch_shapes=[pltpu.VMEM((HEADS, HD, HD), F32), pltpu.VMEM((rows, HW), F32), pltpu.VMEM((1, HW), F32),
                        pltpu.VMEM((rows, HW), F32)],
        input_output_aliases={10: 0},
        compiler_params=_params(("arbitrary",)),
    )(z, z, z, z, o, dx1, w_out, states, lb_logits, gn, dz_in)


def _in_bwd(dz, w_t, x0, g, dx1, deps):
    s = x0.shape[0]
    tm = min(WIDE_ROW_TILE, s)

    def body(dz_ref, w_ref, x_ref, g_ref, d_ref, dx_ref, slot_ref):
        _zero_slot(slot_ref)
        dx, dg = _rms_bwd(x_ref[...], g_ref[...], _mm(dz_ref[...], w_ref[...]))
        dx_ref[...] = d_ref[...] + dx
        slot_ref[0:1, :] += dg

    row_f32 = pl.BlockSpec((tm, D_MODEL), lambda i: (i, 0))
    return _call_behind(
        deps, body, name="in_bwd", grid=(s // tm,),
        out_shape=[jax.ShapeDtypeStruct((s, D_MODEL), F32), jax.ShapeDtypeStruct((SLOT, D_MODEL), F32)],
        in_specs=[pl.BlockSpec((tm, IN_WIDTH), lambda i: (i, 0)), VMEM_WHOLE, row_f32, _full((1, D_MODEL)), row_f32],
        out_specs=[row_f32, _full((SLOT, D_MODEL))],
        compiler_params=_params(("arbitrary",)),
    )(dz, w_t, x0, g, dx1)


def kernel(x, mem, norm_mix_g, w_in, lb_logits, hgrn_norm_g, w_pool, pool_scale, w_out, norm_x_g, norm_mem_g, w_xq, w_xk, w_xv, w_xo, norm_ffn_g, w_ff1, w_ff2, final_norm_g, loss_target, m_norm_mix_g, m_w_in, m_lb_logits, m_hgrn_norm_g, m_w_pool, m_pool_scale, m_w_out, m_norm_x_g, m_norm_mem_g, m_w_xq, m_w_xk, m_w_xv, m_w_xo, m_norm_ffn_g, m_w_ff1, m_w_ff2, m_final_norm_g, v_norm_mix_g, v_w_in, v_lb_logits, v_hgrn_norm_g, v_w_pool, v_pool_scale, v_w_out, v_norm_x_g, v_norm_mem_g, v_w_xq, v_w_xk, v_w_xv, v_w_xo, v_norm_ffn_g, v_w_ff1, v_w_ff2, v_final_norm_g):
    x0 = x[0]
    mem0 = mem[0]
    tgt = loss_target[0]
    gn = hgrn_norm_g[0]
    gfin = final_norm_g.reshape(1, D_MODEL)
    wp = w_pool[0]
    heads_2d = lambda w: w.reshape(D_MODEL // N_DEV, D_MODEL)
    xo_2d = lambda w: w.reshape(D_MODEL, D_MODEL // N_DEV)

    first = _all_gather_weights([w_in[0].T], [w_out[0], heads_2d(w_xq), heads_2d(w_xk), heads_2d(w_xv), xo_2d(w_xo).T,
                                              w_ff1[0], w_ff2[0]])
    win_t = first[0].reshape(IN_WIDTH, D_MODEL)
    ga_attn, ga_mlp = _gather_first_start([first[1:6], first[6:8]], "gather_first_start")

    z, h = _in_proj(x0, norm_mix_g, win_t, deps=[ga_attn[3]])
    mixed_a, o_pre, states = _hgrn_fwd(z, lb_logits, gn)
    lands = _split_wait(_gather_first_copies, ga_attn, o_pre, "gather_attn_first_wait")
    gb_attn = _gather_forward_start(lands, "gather_attn_forward_start")
    mixed, pooled = _pool_fwd(z, wp, pool_scale, mixed_a, deps=[gb_attn[3]])
    lands = _split_wait(_gather_forward_copies, gb_attn, pooled, "gather_attn_forward_wait")
    wout_f, wq_f, wk_f, wv_f, wo_t = (t.reshape(D_MODEL, D_MODEL) for t in lands)
    x1 = _out_proj(x0, mixed, wout_f)
    hm, xk, xv = _mem_kv(mem0, norm_mem_g, wk_f, wv_f, deps=[x1])
    lands = _split_wait(_gather_first_copies, ga_mlp, xk, "gather_mlp_first_wait")
    gb_mlp = _gather_forward_start(lands, "gather_mlp_forward_start")
    x2, hq, xq, att = _xattn_fwd(x1, norm_x_g, wq_f, xk, xv, wo_t, deps=[gb_mlp[3]])
    w1_b, w2_b = _split_wait(_gather_forward_copies, gb_mlp, x2, "gather_mlp_forward_wait")
    dx3, dx3_16, u, hf, slot_fin = _mlp_fwd_loss(x2, norm_ffn_g, w1_b, w2_b.reshape(D_FF, D_MODEL), gfin, tgt)

    rows = lambda t, r: t.reshape(N_DEV, r, D_MODEL)
    dw2 = _wgrad(u, dx3_16, "wgrad_ff2")
    ex_ff2 = _all_to_all_start([rows(dw2, FF_BLK)], [], "exchange_ff2_start")
    da, dx2, slot_ffn = _mlp_bwd(dx3, u, x2, norm_ffn_g, w1_b, w2_b, deps=[ex_ff2[3]])
    dw1 = _wgrad(hf, da, "wgrad_ff1", col_blocks=True)
    ex_ff1 = _all_to_all_start([dw1], [], "exchange_ff1_start")
    dx1, dx1_16, dxq, dxk, dxv, slot_x = _xattn_bwd(dx2, x1, norm_x_g, xq, xk, xv, wq_f, wo_t, deps=[ex_ff1[3]])
    dwo_t = _wgrad(dx2, att, "wgrad_xo")
    dwq = _wgrad(hq, dxq, "wgrad_xq")
    dwk, dwv, slot_mem = _mem_bwd(mem0, norm_mem_g, hm, dxk, dxv, wk_f, wv_f)
    ex_attn = _all_to_all_start([rows(dwq, 128), rows(dwk, 128), rows(dwv, 128), rows(dwo_t, 128)], [],
                                "exchange_attn_start")
    dwout = _wgrad(mixed, dx1_16, "wgrad_out")
    dz_pool, d_wpool, slot_ps = _pool_bwd(dx1_16, wout_f, pooled, wp, pool_scale, deps=[ex_attn[3]])
    small0 = jnp.concatenate([slot_x, slot_mem, slot_ffn, slot_fin, slot_ps], axis=0)
    ex_out = _all_to_all_start([rows(dwout, 128)], [small0, d_wpool], "exchange_out_start")
    dz, slot_lb, slot_gn = _hgrn_bwd(z, o_pre, dx1_16, wout_f, states, lb_logits, gn, dz_pool, deps=[ex_out[3]])
    dwin_t = _wgrad(dz, h, "wgrad_in")
    small1 = jnp.concatenate([slot_lb, slot_gn], axis=0)
    ex_in = _all_to_all_start([rows(dwin_t, 320)], [small1], "exchange_in_start")
    grad_x, slot_mix = _in_bwd(dz, win_t, x0, norm_mix_g, dx1, deps=[ex_in[3]])
    ex_mix = _all_to_all_start([], [slot_mix], "exchange_mix_start")

    out = {}
    (r_2,) = _split_wait(_all_to_all_copies(1), ex_ff2, ex_mix[3], "exchange_ff2_wait")
    out["w_ff2"] = _sum_adamw(r_2, w_ff2[0], m_w_ff2[0], v_w_ff2[0], "adamw_ff2")
    (r_1,) = _split_wait(_all_to_all_copies(1), ex_ff1, out["w_ff2"][1], "exchange_ff1_wait")
    out["w_ff1"] = _sum_adamw(r_1, w_ff1[0], m_w_ff1[0], v_w_ff1[0], "adamw_ff1")
    r_q, r_k, r_v, r_o = _split_wait(_all_to_all_copies(4), ex_attn, out["w_ff1"][1], "exchange_attn_wait")
    for n, r, (w, m, v) in (("w_xq", r_q, (w_xq, m_w_xq, v_w_xq)), ("w_xk", r_k, (w_xk, m_w_xk, v_w_xk)),
                            ("w_xv", r_v, (w_xv, m_w_xv, v_w_xv))):
        g = _sum_sources(r, "sum_grad_" + n).reshape(w.shape)
        out[n] = (g, *_adamw_whole(g, w, m, v, "adamw_" + n))
    g_xo = _sum_sources(r_o, "sum_grad_xo").T
    out["w_xo"] = (g_xo, *_adamw(g_xo, xo_2d(w_xo), xo_2d(m_w_xo), xo_2d(v_w_xo), "adamw_xo"))
    r_out, r_small0, r_wpool = _split_wait(_all_to_all_copies(1), ex_out, out["w_xo"][1], "exchange_out_wait")
    out["w_out"] = _sum_adamw(r_out, w_out[0], m_w_out[0], v_w_out[0], "adamw_out")
    r_in, r_small1 = _split_wait(_all_to_all_copies(1), ex_in, out["w_out"][1], "exchange_in_wait")
    g_in = _sum_sources(r_in, "sum_grad_in").T
    out["w_in"] = (g_in, *_adamw(g_in, w_in[0], m_w_in[0], v_w_in[0], "adamw_in"))
    (r_small2,) = _split_wait(_all_to_all_copies(0), ex_mix, out["w_in"][1], "exchange_mix_wait")
    row = lambda t: t.reshape(1, -1)
    small_params = {
        "norm_mix_g": (norm_mix_g, m_norm_mix_g, v_norm_mix_g),
        "lb_logits": (lb_logits, m_lb_logits, v_lb_logits),
        "hgrn_norm_g": (hgrn_norm_g[0], m_hgrn_norm_g[0], v_hgrn_norm_g[0]),
        "pool_scale": (pool_scale, m_pool_scale, v_pool_scale),
        "norm_x_g": (norm_x_g, m_norm_x_g, v_norm_x_g),
        "norm_mem_g": (norm_mem_g, m_norm_mem_g, v_norm_mem_g),
        "norm_ffn_g": (norm_ffn_g, m_norm_ffn_g, v_norm_ffn_g),
        "final_norm_g": (row(final_norm_g), row(m_final_norm_g), row(v_final_norm_g)),
        "w_pool": (wp, m_w_pool[0], v_w_pool[0]),
    }
    loss, small_out = _small_update([r_small0, r_small1, r_small2], r_wpool, small_params)
    out.update(small_out)

    shapes = dict(norm_mix_g=norm_mix_g, w_in=w_in, lb_logits=lb_logits, hgrn_norm_g=hgrn_norm_g, w_pool=w_pool,
                  pool_scale=pool_scale, w_out=w_out, norm_x_g=norm_x_g, norm_mem_g=norm_mem_g, w_xq=w_xq, w_xk=w_xk,
                  w_xv=w_xv, w_xo=w_xo, norm_ffn_g=norm_ffn_g, w_ff1=w_ff1, w_ff2=w_ff2, final_norm_g=final_norm_g)
    order = list(shapes)
    group = lambda k: [out[n][k].reshape(shapes[n].shape) for n in order]
    return (loss.reshape(()), grad_x.reshape(x.shape), *group(0), *group(1), *group(2), *group(3))
```

```python
import jax
import jax.numpy as jnp
from jax import lax
from jax.experimental import pallas as pl
from jax.experimental.pallas import tpu as pltpu

F32 = jnp.float32
BF16 = jnp.bfloat16

D_MODEL = 1024
N_DEV = 8
HEADS = 4
HD = 128
HW = HEADS * HD
IN_WIDTH = 5 * HW
XHD = 256
MEM_LEN = 256
D_FF = 4096
FF_BLK = D_FF // N_DEV
POOL_WINDOWS = (2, 4, 8, 16)
POOL_HALO = 16
CHUNK = 64
CHUNKS_PER_STEP = 8
SUB = 16
N_SUB = CHUNK // SUB
EXP_CAP = 80.0
EPS = 1e-6
TINY = 1e-30
ROW_TILE = 512
WIDE_ROW_TILE = 1024
SLOT = 8
V7X_VMEM_LIMIT = 56 * 1024 * 1024

ADAM_LR = 0.001
ADAM_B1 = 0.9
ADAM_B2 = 0.999
ADAM_EPS = 1e-08
ADAM_WD = 0.01
ADAM_STEP = 10

MESH_ID = pl.DeviceIdType.MESH


def _params(sem=None, vmem=V7X_VMEM_LIMIT):
    return pltpu.CompilerParams(dimension_semantics=sem, vmem_limit_bytes=vmem)


def _mm(a, b):
    return lax.dot_general(a.astype(BF16), b.astype(BF16), (((1,), (0,)), ((), ())), preferred_element_type=F32)


def _mm_nt(a, b):
    return lax.dot_general(a.astype(BF16), b.astype(BF16), (((1,), (1,)), ((), ())), preferred_element_type=F32)


def _mm_tn(a, b):
    return lax.dot_general(a.astype(BF16), b.astype(BF16), (((0,), (0,)), ((), ())), preferred_element_type=F32)


def _sigmoid(x):
    return 1.0 / (1.0 + jnp.exp(-x))


def _rms(x):
    return lax.rsqrt(jnp.mean(x * x, axis=-1, keepdims=True) + EPS)


def _rms_bwd(x, g, dh):
    r = _rms(x)
    n = x * r
    dn = dh * g
    dx = r * (dn - n * jnp.mean(dn * n, axis=-1, keepdims=True))
    return dx, jnp.sum(dh * n, axis=0, keepdims=True)


def _tri_dot(tri, x, passes):
    acc = None
    rest = x
    for _ in range(passes):
        piece = rest.astype(BF16)
        part = lax.dot_general(tri, piece, (((1,), (0,)), ((), ())), preferred_element_type=F32)
        acc = part if acc is None else acc + part
        rest = rest - piece.astype(F32)
    return acc


def _adam_update(g, w, m, v):
    nm = ADAM_B1 * m + (1.0 - ADAM_B1) * g
    nv = ADAM_B2 * v + (1.0 - ADAM_B2) * (g * g)
    m_hat = nm / (1.0 - ADAM_B1 ** ADAM_STEP)
    v_hat = nv / (1.0 - ADAM_B2 ** ADAM_STEP)
    return -ADAM_LR * (m_hat / (jnp.sqrt(v_hat) + ADAM_EPS) + ADAM_WD * w), nm, nv


def _full(shape):
    return pl.BlockSpec(shape, lambda *_: (0,) * len(shape))


VMEM_WHOLE = pl.BlockSpec(memory_space=pltpu.VMEM)
ANY_SPACE = pl.BlockSpec(memory_space=pl.ANY)


def _mesh_pos():
    return lax.axis_index("x"), lax.axis_index("y"), lax.axis_index("c")


def _flat(px, py, pc):
    return 4 * px + 2 * py + pc


def _all_gather_weights(shards, cast_only):
    n, nc = len(shards), len(cast_only)
    step = 64

    def body(*refs):
        x_refs, c_refs = refs[:n], refs[n:n + nc]
        out_refs, cast_refs = refs[n + nc:2 * n + nc], refs[2 * n + nc:2 * n + 2 * nc]
        bufs = refs[2 * n + 2 * nc:3 * n + 2 * nc]
        send_sems, recv_sems, local_sems = refs[3 * n + 2 * nc:]
        x, y, c = _mesh_pos()
        me, sibling = (x, y, c), (x, y, 1 - c)
        chips = [(1 - x, y), (x, 1 - y), (1 - x, 1 - y)]

        def copy(a, k, blk, to, src=None):
            rows = out_refs[a].at[_flat(*blk)]
            return pltpu.make_async_remote_copy(
                src_ref=rows if src is None else src, dst_ref=rows,
                send_sem=send_sems.at[7 * a + k], recv_sem=recv_sems.at[7 * a + k], device_id=to, device_id_type=MESH_ID)

        def cast_rows(src, dst, rows):
            def cast(i, carry):
                r0 = pl.multiple_of(i * step, step)
                dst[pl.ds(r0, step), :] = src[pl.ds(r0, step), :].astype(BF16)
                return carry
            lax.fori_loop(0, rows // step, cast, 0)

        first, mine = [], []
        for a in range(n):
            cast_rows(x_refs[a], bufs[a], shards[a].shape[0])
            mine.append(pltpu.make_async_copy(bufs[a], out_refs[a].at[_flat(*me)], local_sems.at[a]))
            first.append(copy(a, 0, me, sibling, src=bufs[a]))
            first += [copy(a, 1 + j, me, (*chip, c), src=bufs[a]) for j, chip in enumerate(chips)]
            for cp in [mine[-1]] + first[-4:]:
                cp.start()
        for a in range(nc):
            cast_rows(c_refs[a], cast_refs[a], cast_only[a].shape[0])
        passed = []
        for j, chip in enumerate(chips):
            for a in range(n):
                copy(a, 1 + j, (*chip, c), me).wait_recv()
                passed.append(copy(a, 4 + j, (*chip, c), sibling))
                passed[-1].start()
        for a in range(n):
            copy(a, 0, sibling, me).wait_recv()
            for j, chip in enumerate(chips):
                copy(a, 4 + j, (*chip, 1 - c), me).wait_recv()
        for cp in first + passed:
            cp.wait_send()
        for cp in mine:
            cp.wait()

    return pl.pallas_call(
        body, name="all_gather_w_in",
        out_shape=[jax.ShapeDtypeStruct((N_DEV,) + s.shape, BF16) for s in shards]
        + [jax.ShapeDtypeStruct(s.shape, BF16) for s in cast_only],
        in_specs=[VMEM_WHOLE] * (n + nc), out_specs=[ANY_SPACE] * n + [VMEM_WHOLE] * nc,
        scratch_shapes=[pltpu.VMEM(s.shape, BF16) for s in shards]
        + [pltpu.SemaphoreType.DMA((7 * n,)), pltpu.SemaphoreType.DMA((7 * n,)), pltpu.SemaphoreType.DMA((n,))],
        compiler_params=_params(),
    )(*shards, *cast_only)


HBM_SPEC = pl.BlockSpec(memory_space=pltpu.HBM)
SEM_SPEC = pl.BlockSpec(memory_space=pltpu.SEMAPHORE)
EFFECT = pltpu.SideEffectType.DATAFLOW_SIDE_EFFECTING
TOKEN = jax.ShapeDtypeStruct((8, 128), F32)


def _in_hbm(a):
    return pltpu.with_memory_space_constraint(a, pltpu.HBM)


def _split_start(copies_of, srcs, lands, n_sems, name):
    ns, nl, k = len(srcs), len(lands), len(n_sems)

    def body(*refs):
        src_refs, land_refs = refs[:ns], refs[ns:ns + nl]
        sems = refs[ns + nl:ns + nl + k]
        token = refs[-1]
        for cp in copies_of(src_refs, land_refs, sems):
            cp.start()
        token[...] = jnp.zeros_like(token)

    outs = pl.pallas_call(
        body, name=name,
        out_shape=[pltpu.SemaphoreType.DMA((q,)) for q in n_sems]
        + [pltpu.HBM(a.shape, a.dtype) for a in list(srcs) + list(lands)] + [TOKEN],
        in_specs=[HBM_SPEC] * (ns + nl),
        out_specs=[SEM_SPEC] * k + [HBM_SPEC] * (ns + nl) + [VMEM_WHOLE],
        input_output_aliases={i: k + i for i in range(ns + nl)},
        compiler_params=pltpu.CompilerParams(has_side_effects=EFFECT),
    )(*[_in_hbm(a) for a in list(srcs) + list(lands)])
    return outs[:k], outs[k:k + ns], outs[k + ns:k + ns + nl], outs[-1]


def _split_wait(copies_of, handle, after, name):
    sems, srcs, lands, _ = handle
    ns, nl, k = len(srcs), len(lands), len(sems)

    def body(*refs):
        src_refs, land_refs = refs[:ns], refs[ns:ns + nl]
        sem_refs = refs[ns + nl:ns + nl + k]
        for cp in copies_of(src_refs, land_refs, sem_refs):
            cp.wait()

    outs = pl.pallas_call(
        body, name=name,
        out_shape=[pltpu.HBM(a.shape, a.dtype) for a in list(srcs) + list(lands)],
        in_specs=[HBM_SPEC] * (ns + nl) + [SEM_SPEC] * k + [ANY_SPACE],
        out_specs=[HBM_SPEC] * (ns + nl),
        input_output_aliases={i: i for i in range(ns + nl)},
        compiler_params=pltpu.CompilerParams(has_side_effects=EFFECT),
    )(*srcs, *lands, *sems, after)
    return outs[ns:]


def _gather_first_copies(shard_refs, land_refs, sems):
    send_sems, recv_sems, local_sems = sems
    x, y, c = _mesh_pos()
    me = _flat(x, y, c)
    peers = [(x, y, 1 - c), (1 - x, y, c), (x, 1 - y, c), (1 - x, 1 - y, c)]
    copies = []
    for a, (shard, land) in enumerate(zip(shard_refs, land_refs)):
        copies.append(pltpu.make_async_copy(shard, land.at[me], local_sems.at[a]))
        for k, peer in enumerate(peers):
            copies.append(pltpu.make_async_remote_copy(
                src_ref=shard, dst_ref=land.at[me], send_sem=send_sems.at[4 * a + k], recv_sem=recv_sems.at[4 * a + k],
                device_id=peer, device_id_type=MESH_ID))
    return copies


def _gather_forward_copies(src_refs, land_refs, sems):
    del src_refs
    send_sems, recv_sems = sems
    x, y, c = _mesh_pos()
    chips = [(1 - x, y), (x, 1 - y), (1 - x, 1 - y)]
    copies = []
    for a, land in enumerate(land_refs):
        for j, chip in enumerate(chips):
            rows = land.at[_flat(*chip, c)]
            copies.append(pltpu.make_async_remote_copy(
                src_ref=rows, dst_ref=rows, send_sem=send_sems.at[3 * a + j], recv_sem=recv_sems.at[3 * a + j],
                device_id=(x, y, 1 - c), device_id_type=MESH_ID))
    return copies


def _gather_first_start(groups, name):
    shards = [s for g in groups for s in g]
    lands = [lax.empty((N_DEV,) + s.shape, s.dtype) for s in shards]
    bounds = [sum(len(g) for g in groups[:i]) for i in range(len(groups) + 1)]

    def copies_of(src_refs, land_refs, sems):
        copies = []
        for i in range(len(groups)):
            lo, hi = bounds[i], bounds[i + 1]
            copies += _gather_first_copies(src_refs[lo:hi], land_refs[lo:hi], sems[3 * i:3 * i + 3])
        return copies

    n_sems = tuple(q for g in groups for q in (4 * len(g), 4 * len(g), len(g)))
    sems, srcs, lands, token = _split_start(copies_of, shards, lands, n_sems, name)
    return [(sems[3 * i:3 * i + 3], srcs[bounds[i]:bounds[i + 1]], lands[bounds[i]:bounds[i + 1]], token)
            for i in range(len(groups))]


def _gather_forward_start(lands, name):
    n = len(lands)
    return _split_start(_gather_forward_copies, [], lands, (3 * n, 3 * n), name)


def _all_to_all_copies(n_scattered):
    def copies_of(src_refs, land_refs, sems):
        send_sems, recv_sems, local_sems = sems
        x, y, c = _mesh_pos()
        me = _flat(x, y, c)
        copies = []
        for a, (src, land) in enumerate(zip(src_refs, land_refs)):
            scattered = a < n_scattered
            copies.append(pltpu.make_async_copy(src.at[me] if scattered else src, land.at[me], local_sems.at[a]))
            for k in range(1, N_DEV):
                peer = (1 - x if k & 4 else x, 1 - y if k & 2 else y, 1 - c if k & 1 else c)
                copies.append(pltpu.make_async_remote_copy(
                    src_ref=src.at[_flat(*peer)] if scattered else src, dst_ref=land.at[me],
                    send_sem=send_sems.at[7 * a + k - 1], recv_sem=recv_sems.at[7 * a + k - 1],
                    device_id=peer, device_id_type=MESH_ID))
        return copies
    return copies_of


def _all_to_all_start(scattered, broadcast, name):
    srcs = list(scattered) + list(broadcast)
    lands = [lax.empty(a.shape, a.dtype) for a in scattered] + [lax.empty((N_DEV,) + a.shape, a.dtype) for a in broadcast]
    n = len(srcs)
    return _split_start(_all_to_all_copies(len(scattered)), srcs, lands, (7 * n, 7 * n, n), name)


def _call_behind(deps, body, *, in_specs, **kwargs):
    n_in, n_dep = len(in_specs), len(deps)

    def body_without_deps(*refs):
        return body(*refs[:n_in], *refs[n_in + n_dep:])

    call = pl.pallas_call(body_without_deps, in_specs=list(in_specs) + [ANY_SPACE] * n_dep, **kwargs)
    return lambda *operands: call(*operands, *deps)


def _row_tile(rows):
    for cand in (256, 128, 64, 32, 16):
        if rows % cand == 0:
            return cand
    return rows


def _sum_sources(recv, name):
    _, rows, cols = recv.shape
    tile = _row_tile(rows)

    def body(r_ref, o_ref):
        acc = r_ref[0].astype(F32)
        for d in range(1, N_DEV):
            acc = acc + r_ref[d].astype(F32)
        o_ref[...] = acc

    return pl.pallas_call(
        body, name=name, grid=(rows // tile,),
        out_shape=jax.ShapeDtypeStruct((rows, cols), F32),
        in_specs=[pl.BlockSpec((N_DEV, tile, cols), lambda i: (0, i, 0))],
        out_specs=pl.BlockSpec((tile, cols), lambda i: (i, 0)),
        compiler_params=_params(("parallel",)),
    )(recv)


def _adamw(g, w, m, v, name):
    rows, cols = g.shape
    tile = _row_tile(rows)

    def body(g_ref, w_ref, m_ref, v_ref, d_ref, nm_ref, nv_ref):
        d_ref[...], nm_ref[...], nv_ref[...] = _adam_update(g_ref[...], w_ref[...], m_ref[...], v_ref[...])

    spec = pl.BlockSpec((tile, cols), lambda i: (i, 0))
    shp = jax.ShapeDtypeStruct((rows, cols), F32)
    return pl.pallas_call(
        body, name=name, grid=(rows // tile,), out_shape=[shp, shp, shp],
        in_specs=[spec] * 4, out_specs=[spec] * 3,
        compiler_params=_params(("parallel",)),
    )(g, w, m, v)


def _adamw_whole(groups, name):
    n = len(groups)

    def body(*refs):
        for i in range(n):
            g_ref, w_ref, m_ref, v_ref = refs[4 * i:4 * i + 4]
            d_ref, nm_ref, nv_ref = refs[4 * n + 3 * i:4 * n + 3 * i + 3]
            d_ref[...], nm_ref[...], nv_ref[...] = _adam_update(g_ref[...], w_ref[...], m_ref[...], v_ref[...])

    outs = pl.pallas_call(
        body, name=name, out_shape=[jax.ShapeDtypeStruct(grp[0].shape, F32) for grp in groups for _ in range(3)],
        in_specs=[VMEM_WHOLE] * (4 * n), out_specs=[VMEM_WHOLE] * (3 * n),
        compiler_params=_params(),
    )(*[t for grp in groups for t in grp])
    return [outs[3 * i:3 * i + 3] for i in range(n)]


def _sum_sources_whole(recvs, name):
    n = len(recvs)

    def body(*refs):
        for r_ref, o_ref in zip(refs[:n], refs[n:]):
            acc = r_ref[0].astype(F32)
            for d in range(1, N_DEV):
                acc = acc + r_ref[d].astype(F32)
            o_ref[...] = acc

    return pl.pallas_call(
        body, name=name, out_shape=[jax.ShapeDtypeStruct(r.shape[1:], F32) for r in recvs],
        in_specs=[VMEM_WHOLE] * n, out_specs=[VMEM_WHOLE] * n,
        compiler_params=_params(),
    )(*recvs)


def _sum_adamw(recv, w, m, v, name):
    _, rows, cols = recv.shape
    tile = _row_tile(rows)

    def body(r_ref, w_ref, m_ref, v_ref, g_ref, d_ref, nm_ref, nv_ref):
        acc = r_ref[0].astype(F32)
        for d in range(1, N_DEV):
            acc = acc + r_ref[d].astype(F32)
        g_ref[...] = acc
        d_ref[...], nm_ref[...], nv_ref[...] = _adam_update(acc, w_ref[...], m_ref[...], v_ref[...])

    spec = pl.BlockSpec((tile, cols), lambda i: (i, 0))
    shp = jax.ShapeDtypeStruct((rows, cols), F32)
    return pl.pallas_call(
        body, name=name, grid=(rows // tile,), out_shape=[shp] * 4,
        in_specs=[pl.BlockSpec((N_DEV, tile, cols), lambda i: (0, i, 0)), spec, spec, spec], out_specs=[spec] * 4,
        compiler_params=_params(("parallel",)),
    )(recv, w, m, v)


SMALL_SLOTS = {"norm_x_g": (0, 0, 1, D_MODEL), "norm_mem_g": (0, 8, 1, D_MODEL), "norm_ffn_g": (0, 16, 1, D_MODEL),
               "final_norm_g": (0, 24, 1, D_MODEL), "pool_scale": (0, 32, 1, HW),
               "lb_logits": (1, 0, 2, HW), "hgrn_norm_g": (1, 8, HEADS, HD), "norm_mix_g": (2, 0, 1, D_MODEL)}
LOSS_ROW = 25
SMALL_ORDER = ("norm_mix_g", "lb_logits", "hgrn_norm_g", "pool_scale", "norm_x_g", "norm_mem_g", "norm_ffn_g",
               "final_norm_g", "w_pool")


def _small_update(srecvs, wprecv, params):
    flat = [t for n in SMALL_ORDER for t in params[n]]
    nb = len(srecvs)
    n_in = nb + 1 + len(flat)

    def body(*refs):
        s_refs, wp_ref = refs[0:nb], refs[nb]
        in_refs = refs[nb + 1:n_in]
        loss_ref = refs[n_in]
        out_refs = refs[n_in + 1:-nb]
        accs = refs[-nb:]
        for s_ref, acc in zip(s_refs, accs):
            total = s_ref[0]
            for d in range(1, N_DEV):
                total = total + s_ref[d]
            acc[...] = total
        loss_ref[...] = accs[0][LOSS_ROW:LOSS_ROW + 1, 0:1]
        for i, name in enumerate(SMALL_ORDER):
            w_ref, m_ref, v_ref = in_refs[3 * i:3 * i + 3]
            g_ref, d_ref, nm_ref, nv_ref = out_refs[4 * i:4 * i + 4]
            if name == "w_pool":
                g = wp_ref[0]
                for d in range(1, N_DEV):
                    g = g + wp_ref[d]
            else:
                buf, r0, nr, nc = SMALL_SLOTS[name]
                g = accs[buf][r0:r0 + nr, 0:nc]
            g_ref[...] = g
            d_ref[...], nm_ref[...], nv_ref[...] = _adam_update(g, w_ref[...], m_ref[...], v_ref[...])

    out_shape = [jax.ShapeDtypeStruct((1, 1), F32)]
    for n in SMALL_ORDER:
        out_shape += [jax.ShapeDtypeStruct(params[n][0].shape, F32)] * 4
    outs = pl.pallas_call(
        body, name="small_update", out_shape=out_shape,
        in_specs=[VMEM_WHOLE] * n_in, out_specs=[VMEM_WHOLE] * len(out_shape),
        scratch_shapes=[pltpu.VMEM(r.shape[1:], F32) for r in srecvs],
        compiler_params=_params(),
    )(*srecvs, wprecv, *flat)
    return outs[0], {n: outs[1 + 4 * i:5 + 4 * i] for i, n in enumerate(SMALL_ORDER)}


def _in_proj(x, g, w_t, deps):
    s = x.shape[0]
    tm = min(WIDE_ROW_TILE, s)

    def body(x_ref, g_ref, w_ref, z_ref, h_ref):
        xv = x_ref[...]
        h = (xv * _rms(xv) * g_ref[...]).astype(BF16)
        h_ref[...] = h
        z_ref[...] = _mm_nt(h, w_ref[...])

    return _call_behind(
        deps, body, name="in_proj", grid=(s // tm,),
        out_shape=[jax.ShapeDtypeStruct((s, IN_WIDTH), F32), jax.ShapeDtypeStruct((s, D_MODEL), BF16)],
        in_specs=[pl.BlockSpec((tm, D_MODEL), lambda i: (i, 0)), _full((1, D_MODEL)), VMEM_WHOLE],
        out_specs=[pl.BlockSpec((tm, IN_WIDTH), lambda i: (i, 0)), pl.BlockSpec((tm, D_MODEL), lambda i: (i, 0))],
        compiler_params=_params(("parallel",)),
    )(x, g, w_t)


def _chunk_masks():
    row = lax.broadcasted_iota(jnp.int32, (CHUNK, CHUNK), 0)
    col = lax.broadcasted_iota(jnp.int32, (CHUNK, CHUNK), 1)
    return row, col


def _ones_where(mask):
    return jnp.where(mask, 1.0, 0.0).astype(BF16)


def _hgrn_gates(zq, zf, lb):
    sq = _sigmoid(zq)
    sig = _sigmoid(zf)
    f = lb + (1.0 - lb) * sig
    return zq * sq, sq, sig, f


def _sub_chunk_masks(width):
    trow = lax.broadcasted_iota(jnp.int32, (CHUNK, width), 0)
    return [(trow >= SUB * j) & (trow < SUB * (j + 1)) for j in range(N_SUB)]


def _head(a, h):
    return a[:, HD * h:HD * (h + 1)]


def _lanes(parts):
    return jnp.concatenate(parts, axis=1)


def _hgrn_decay_factors(b_scr, r0, b, in_sub):
    bases = [jnp.zeros((1, HW), F32)] + [b_scr[r0 + SUB * j - 1:r0 + SUB * j, :] for j in range(1, N_SUB)]
    own_base = bases[N_SUB - 1]
    for j in range(N_SUB - 2, -1, -1):
        own_base = jnp.where(in_sub[j], bases[j], own_base)
    eq = jnp.exp(b - own_base)
    ek = []
    for j in range(N_SUB):
        upto = SUB * (j + 1)
        e = jnp.exp(jnp.minimum(bases[j] - b[0:upto], EXP_CAP))
        ek.append(e if upto == CHUNK else jnp.concatenate([e, jnp.zeros((CHUNK - upto, HW), F32)], axis=0))
    return eq, ek


def _per_sub_chunk(x, in_sub):
    return _lanes([jnp.where(in_sub[j], x, 0.0) for j in range(N_SUB)])


def _own_lane_block(a, in_sub):
    out = a[:, HD * (N_SUB - 1):HD * N_SUB]
    for j in range(N_SUB - 2, -1, -1):
        out = jnp.where(in_sub[j], a[:, HD * j:HD * (j + 1)], out)
    return out


def _head_rms(o):
    return _lanes([jnp.broadcast_to(_rms(_head(o, h)), (CHUNK, HD)) for h in range(HEADS)])


def _head_mean(a):
    return _lanes([jnp.broadcast_to(jnp.mean(_head(a, h), axis=-1, keepdims=True), (CHUNK, HD)) for h in range(HEADS)])


def _hgrn_fwd(z, lb_logits, gn):
    s = z.shape[0]
    n_chunks = s // CHUNK

    def body(zq_ref, zf_ref, zi_ref, zg_ref, lbl_ref, gn_ref, oa_ref, o_ref, st_ref, state, b_scr):
        @pl.when(pl.program_id(0) == 0)
        def _():
            state[...] = jnp.zeros_like(state)

        lb = _sigmoid(lbl_ref[0:1, :] - lbl_ref[1:2, :])
        row, col = _chunk_masks()
        causal = col <= row
        tri = _ones_where(causal)
        in_sub, in_sub_head = _sub_chunk_masks(HW), _sub_chunk_masks(HD)
        gn_row = _lanes([gn_ref[h:h + 1, :] for h in range(HEADS)])
        for c in range(CHUNKS_PER_STEP):
            r0 = CHUNK * c
            rs = slice(r0, r0 + CHUNK)
            st_ref[c] = state[...]
            q, _, _, f = _hgrn_gates(zq_ref[rs, :], zf_ref[rs, :], lb)
            kk = 1.0 - f
            b = _tri_dot(tri, jnp.log(f), 3)
            b_scr[rs, :] = b
            eq, ek = _hgrn_decay_factors(b_scr, r0, b, in_sub)
            b_last = b_scr[r0 + CHUNK - 1:r0 + CHUNK, :]
            qe, qg = q * eq, q * jnp.exp(b)
            ke = [kk * e for e in ek]
            kd = kk * jnp.exp(b_last - b)
            lam_last = jnp.exp(b_last)
            v = zi_ref[rs, :]
            o_heads = []
            for h in range(HEADS):
                vh, st = _head(v, h), state[h]
                a = jnp.where(causal, _mm_nt(_per_sub_chunk(_head(qe, h), in_sub_head),
                                             _lanes([_head(ke[j], h) for j in range(N_SUB)])), 0.0)
                o_heads.append(_mm(a, vh) + _mm_nt(_head(qg, h), st))
                state[h] = st * _head(lam_last, h) + _mm_tn(vh, _head(kd, h))
            o = _lanes(o_heads)
            o_ref[rs, :] = o
            zg = zg_ref[rs, :]
            oa_ref[rs, :] = (o * _head_rms(o) * gn_row * zg * _sigmoid(zg)).astype(BF16)

    rows = CHUNK * CHUNKS_PER_STEP
    zspec = lambda cb: pl.BlockSpec((rows, HW), lambda i, cb=cb: (i, cb))
    return pl.pallas_call(
        body, name="hgrn_fwd", grid=(s // rows,),
        out_shape=[jax.ShapeDtypeStruct((s, 2 * HW), BF16), jax.ShapeDtypeStruct((s, HW), F32),
                   jax.ShapeDtypeStruct((n_chunks, HEADS, HD, HD), F32)],
        in_specs=[zspec(0), zspec(1), zspec(2), zspec(3), _full((2, HW)), _full((HEADS, HD))],
        out_specs=[pl.BlockSpec((rows, HW), lambda i: (i, 0)), pl.BlockSpec((rows, HW), lambda i: (i, 0)),
                   pl.BlockSpec((CHUNKS_PER_STEP, HEADS, HD, HD), lambda i: (i, 0, 0, 0))],
        scratch_shapes=[pltpu.VMEM((HEADS, HD, HD), F32), pltpu.VMEM((rows, HW), F32)],
        compiler_params=_params(("arbitrary",)),
    )(z, z, z, z, lb_logits, gn)


def _pool_counts(tile_idx, tm):
    t = tile_idx * tm + lax.broadcasted_iota(jnp.int32, (tm, 1), 0)
    return [1.0 / jnp.minimum(t + 1, w).astype(F32) for w in POOL_WINDOWS]


def _pool_fwd(z, w_pool, scale, mixed_in, deps):
    s = z.shape[0]
    tm = min(ROW_TILE, s)

    def body(p_ref, w_ref, sc_ref, mixin_ref, ob_ref, pooled_ref, ext):
        i = pl.program_id(0)

        @pl.when(i == 0)
        def _():
            ext[0:POOL_HALO, :] = jnp.zeros((POOL_HALO, HW), F32)

        @pl.when(i > 0)
        def _():
            ext[0:POOL_HALO, :] = ext[tm:tm + POOL_HALO, :]

        ext[POOL_HALO:POOL_HALO + tm, :] = p_ref[...]
        inv = _pool_counts(i, tm)
        for g, w in enumerate(POOL_WINDOWS):
            sl = slice(HD * g, HD * (g + 1))
            p = ext[POOL_HALO:POOL_HALO + tm, sl]
            win = p
            for d in range(1, w):
                win = win + ext[POOL_HALO - d:POOL_HALO - d + tm, sl]
            pooled = (win * inv[g] - p).astype(BF16)
            pooled_ref[:, sl] = pooled
            ob_ref[:, sl] = (_mm(pooled, w_ref[g]) * sc_ref[:, sl]).astype(BF16)

    return _call_behind(
        deps, body, name="pool_fwd", grid=(s // tm,),
        out_shape=[jax.ShapeDtypeStruct((s, 2 * HW), BF16), jax.ShapeDtypeStruct((s, HW), BF16)],
        in_specs=[pl.BlockSpec((tm, HW), lambda i: (i, 4)), _full((HEADS, HD, HD)), _full((1, HW)), ANY_SPACE],
        out_specs=[pl.BlockSpec((tm, HW), lambda i: (i, 1)), pl.BlockSpec((tm, HW), lambda i: (i, 0))],
        scratch_shapes=[pltpu.VMEM((tm + POOL_HALO, HW), F32)],
        input_output_aliases={3: 0},
        compiler_params=_params(("arbitrary",)),
    )(z, w_pool, scale, mixed_in)


def _mem_kv(mem, g, wk, wv, deps):
    def body(m_ref, g_ref, wk_ref, wv_ref, hm_ref, k_ref, v_ref):
        m = m_ref[...]
        hm = (m * _rms(m) * g_ref[...]).astype(BF16)
        hm_ref[...] = hm
        k_ref[...] = _mm(hm, wk_ref[...]).astype(BF16)
        v_ref[...] = _mm(hm, wv_ref[...]).astype(BF16)

    shp = jax.ShapeDtypeStruct((MEM_LEN, D_MODEL), BF16)
    return _call_behind(
        deps, body, name="mem_kv", out_shape=[shp, shp, shp],
        in_specs=[VMEM_WHOLE] * 4, out_specs=[VMEM_WHOLE] * 3,
        compiler_params=_params(),
    )(mem, g, wk, wv)


def _softmax_rows(sc):
    e = jnp.exp(sc - jnp.max(sc, axis=-1, keepdims=True))
    return e / jnp.sum(e, axis=-1, keepdims=True)


def _mix_xattn_fwd(x0, mixed, w_out, g, wq, xk, xv, wo_t, deps):
    s = x0.shape[0]
    tm = min(ROW_TILE, s)
    scale = XHD ** -0.5

    def body(x_ref, mix_ref, wout_ref, g_ref, wq_ref, k_ref, v_ref, wo_ref, x1_ref, o_ref, hq_ref, q_ref, att_ref):
        xv_ = x_ref[...] + _mm(mix_ref[...], wout_ref[...])
        x1_ref[...] = xv_
        hq = (xv_ * _rms(xv_) * g_ref[...]).astype(BF16)
        hq_ref[...] = hq
        q_ref[...] = (_mm(hq, wq_ref[...]) * scale).astype(BF16)
        for h in range(HEADS):
            sl = slice(XHD * h, XHD * (h + 1))
            p = _softmax_rows(_mm_nt(q_ref[:, sl], k_ref[:, sl]))
            att_ref[:, sl] = _mm(p, v_ref[:, sl]).astype(BF16)
        o_ref[...] = xv_ + _mm_nt(att_ref[...], wo_ref[...])

    row_f32 = pl.BlockSpec((tm, D_MODEL), lambda i: (i, 0))
    bshape = jax.ShapeDtypeStruct((s, D_MODEL), BF16)
    fshape = jax.ShapeDtypeStruct((s, D_MODEL), F32)
    return _call_behind(
        deps, body, name="mix_xattn_fwd", grid=(s // tm,),
        out_shape=[fshape, fshape, bshape, bshape, bshape],
        in_specs=[row_f32, row_f32, VMEM_WHOLE, _full((1, D_MODEL)), VMEM_WHOLE, VMEM_WHOLE, VMEM_WHOLE, VMEM_WHOLE],
        out_specs=[row_f32] * 5,
        compiler_params=_params(("parallel",)),
    )(x0, mixed, w_out, g, wq, xk, xv, wo_t)


def _mlp_fwd_loss(x, g, w1, w2, gf, target):
    s = x.shape[0]
    tm = min(ROW_TILE, s)

    def body(x_ref, g_ref, w1_ref, w2_ref, gf_ref, t_ref, dx_ref, dx16_ref, u_ref, hf_ref, slot_ref):
        @pl.when(pl.program_id(0) == 0)
        def _():
            slot_ref[...] = jnp.zeros_like(slot_ref)

        xv = x_ref[...]
        hf = (xv * _rms(xv) * g_ref[...]).astype(BF16)
        hf_ref[...] = hf
        for j in range(N_DEV):
            a = jnp.maximum(_mm(hf, w1_ref[j]), 0.0)
            u_ref[:, FF_BLK * j:FF_BLK * (j + 1)] = (a * a).astype(BF16)
        acc = xv + _mm(u_ref[...], w2_ref[...])
        gfv = gf_ref[...]
        r = _rms(acc)
        n = acc * r
        err = n * gfv - t_ref[...]
        slot_ref[1:2, :] += jnp.sum(jnp.mean(err * err, axis=-1, keepdims=True), axis=0, keepdims=True) * 0.5
        dy = err * (1.0 / D_MODEL)
        slot_ref[0:1, :] += jnp.sum(dy * n, axis=0, keepdims=True)
        dn = dy * gfv
        dx = r * (dn - n * jnp.mean(dn * n, axis=-1, keepdims=True))
        dx_ref[...] = dx
        dx16_ref[...] = dx.astype(BF16)

    row_f32 = pl.BlockSpec((tm, D_MODEL), lambda i: (i, 0))
    return pl.pallas_call(
        body, name="mlp_fwd_loss", grid=(s // tm,),
        out_shape=[jax.ShapeDtypeStruct((s, D_MODEL), F32), jax.ShapeDtypeStruct((s, D_MODEL), BF16),
                   jax.ShapeDtypeStruct((s, D_FF), BF16), jax.ShapeDtypeStruct((s, D_MODEL), BF16),
                   jax.ShapeDtypeStruct((SLOT, D_MODEL), F32)],
        in_specs=[row_f32, _full((1, D_MODEL)), VMEM_WHOLE, VMEM_WHOLE, _full((1, D_MODEL)), row_f32],
        out_specs=[row_f32, row_f32, pl.BlockSpec((tm, D_FF), lambda i: (i, 0)), row_f32, _full((SLOT, D_MODEL))],
        compiler_params=_params(("arbitrary",)),
    )(x, g, w1, w2, gf, target)


def _zero_slot(slot_ref):
    @pl.when(pl.program_id(0) == 0)
    def _():
        slot_ref[...] = jnp.zeros_like(slot_ref)


def _mlp_bwd(dx3, u, x2, g, w1, w2, deps):
    s = x2.shape[0]
    tm = min(ROW_TILE, s)

    def body(d_ref, u_ref, x_ref, g_ref, w1_ref, w2_ref, da_ref, dx_ref, slot_ref):
        _zero_slot(slot_ref)
        d = d_ref[...]
        d16 = d.astype(BF16)
        dhf = jnp.zeros((tm, D_MODEL), F32)
        for j in range(N_DEV):
            sl = slice(FF_BLK * j, FF_BLK * (j + 1))
            u = u_ref[:, sl].astype(F32)
            da = (_mm_nt(d16, w2_ref[j]) * (2.0 * u * lax.rsqrt(jnp.maximum(u, TINY)))).astype(BF16)
            da_ref[:, sl] = da
            dhf = dhf + _mm_nt(da, w1_ref[j])
        dx, dg = _rms_bwd(x_ref[...], g_ref[...], dhf)
        dx_ref[...] = d + dx
        slot_ref[0:1, :] += dg

    row_f32 = pl.BlockSpec((tm, D_MODEL), lambda i: (i, 0))
    return _call_behind(
        deps, body, name="mlp_bwd", grid=(s // tm,),
        out_shape=[jax.ShapeDtypeStruct((s, D_FF), BF16), jax.ShapeDtypeStruct((s, D_MODEL), F32),
                   jax.ShapeDtypeStruct((SLOT, D_MODEL), F32)],
        in_specs=[row_f32, pl.BlockSpec((tm, D_FF), lambda i: (i, 0)), row_f32, _full((1, D_MODEL)),
                  VMEM_WHOLE, VMEM_WHOLE],
        out_specs=[pl.BlockSpec((tm, D_FF), lambda i: (i, 0)), row_f32, _full((SLOT, D_MODEL))],
        compiler_params=_params(("arbitrary",)),
    )(dx3, u, x2, g, w1, w2)


def _wgrad(a, b, name, col_blocks=False):
    s, m = a.shape
    n = b.shape[1]
    tm = 1280 if m % 1280 == 0 else min(1024, m)
    tn = min(1024, n)
    blk = n // N_DEV
    per_step = tn // blk if col_blocks else 1
    ts = min(2 * ROW_TILE, s)
    n_s = s // ts

    def body(a_ref, b_ref, o_ref, acc):
        k = pl.program_id(2)

        @pl.when(k == 0)
        def _():
            acc[...] = jnp.zeros_like(acc)

        acc[...] += _mm_tn(a_ref[...], b_ref[...])

        @pl.when(k == n_s - 1)
        def _():
            if col_blocks:
                for p in range(per_step):
                    o_ref[p] = acc[:, blk * p:blk * (p + 1)].astype(BF16)
            else:
                o_ref[...] = acc[...].astype(BF16)

    if col_blocks:
        out_shape = jax.ShapeDtypeStruct((N_DEV, m, blk), BF16)
        out_spec = pl.BlockSpec((per_step, tm, blk), lambda i, j, k: (j, i, 0))
    else:
        out_shape = jax.ShapeDtypeStruct((m, n), BF16)
        out_spec = pl.BlockSpec((tm, tn), lambda i, j, k: (i, j))
    return pl.pallas_call(
        body, name=name, grid=(m // tm, n // tn, n_s), out_shape=out_shape,
        in_specs=[pl.BlockSpec((ts, tm), lambda i, j, k: (k, i)), pl.BlockSpec((ts, tn), lambda i, j, k: (k, j))],
        out_specs=out_spec,
        scratch_shapes=[pltpu.VMEM((tm, tn), F32)],
        compiler_params=_params(("parallel", "parallel", "arbitrary")),
    )(a, b)


def _xattn_bwd(dx2, x1, g, q, xk, xv, wq, wo_t, deps):
    s = x1.shape[0]
    tm = min(ROW_TILE, s)
    scale = XHD ** -0.5

    def body(d_ref, x_ref, g_ref, q_ref, k_ref, v_ref, wq_ref, wo_ref, dx_ref, dx16_ref, dq_ref, dk_ref, dv_ref, slot_ref,
             datt):
        _zero_slot(slot_ref)

        @pl.when(pl.program_id(0) == 0)
        def _():
            dk_ref[...] = jnp.zeros_like(dk_ref)
            dv_ref[...] = jnp.zeros_like(dv_ref)

        d = d_ref[...]
        datt[...] = _mm(d, wo_ref[...]).astype(BF16)
        for h in range(HEADS):
            sl = slice(XHD * h, XHD * (h + 1))
            qh, kh, vh, dah = q_ref[:, sl], k_ref[:, sl], v_ref[:, sl], datt[:, sl]
            p = _softmax_rows(_mm_nt(qh, kh))
            dp = _mm_nt(dah, vh)
            ds = (p * (dp - jnp.sum(dp * p, axis=-1, keepdims=True))).astype(BF16)
            dq_ref[:, sl] = (_mm(ds, kh) * scale).astype(BF16)
            dk_ref[:, sl] += _mm_tn(ds, qh)
            dv_ref[:, sl] += _mm_tn(p, dah)
        dx, dg = _rms_bwd(x_ref[...], g_ref[...], _mm_nt(dq_ref[...], wq_ref[...]))
        dx_ref[...] = d + dx
        dx16_ref[...] = (d + dx).astype(BF16)
        slot_ref[0:1, :] += dg

    row_f32 = pl.BlockSpec((tm, D_MODEL), lambda i: (i, 0))
    kv = jax.ShapeDtypeStruct((MEM_LEN, D_MODEL), F32)
    tokens16 = jax.ShapeDtypeStruct((s, D_MODEL), BF16)
    return _call_behind(
        deps, body, name="xattn_bwd", grid=(s // tm,),
        out_shape=[jax.ShapeDtypeStruct((s, D_MODEL), F32), tokens16, tokens16, kv, kv,
                   jax.ShapeDtypeStruct((SLOT, D_MODEL), F32)],
        in_specs=[row_f32, row_f32, _full((1, D_MODEL)), row_f32, VMEM_WHOLE, VMEM_WHOLE, VMEM_WHOLE, VMEM_WHOLE],
        out_specs=[row_f32, row_f32, row_f32, _full((MEM_LEN, D_MODEL)), _full((MEM_LEN, D_MODEL)),
                   _full((SLOT, D_MODEL))],
        scratch_shapes=[pltpu.VMEM((tm, D_MODEL), BF16)],
        compiler_params=_params(("arbitrary",)),
    )(dx2, x1, g, q, xk, xv, wq, wo_t)


def _mem_bwd(mem, g, hm, dxk, dxv, wk, wv):
    def body(m_ref, g_ref, hm_ref, dk_ref, dv_ref, wk_ref, wv_ref, dwk_ref, dwv_ref, slot_ref):
        dk, dv = dk_ref[...], dv_ref[...]
        hm_ = hm_ref[...]
        dwk_ref[...] = _mm_tn(hm_, dk).astype(BF16)
        dwv_ref[...] = _mm_tn(hm_, dv).astype(BF16)
        _, dg = _rms_bwd(m_ref[...], g_ref[...], _mm_nt(dk, wk_ref[...]) + _mm_nt(dv, wv_ref[...]))
        slot_ref[...] = jnp.zeros_like(slot_ref)
        slot_ref[0:1, :] = dg

    wshape = jax.ShapeDtypeStruct((D_MODEL, D_MODEL), BF16)
    return pl.pallas_call(
        body, name="mem_bwd", out_shape=[wshape, wshape, jax.ShapeDtypeStruct((SLOT, D_MODEL), F32)],
        in_specs=[VMEM_WHOLE] * 7, out_specs=[VMEM_WHOLE] * 3,
        compiler_params=_params(),
    )(mem, g, hm, dxk, dxv, wk, wv)


def _pool_bwd(dx1, w_out, pooled, w_pool, scale, deps):
    s = dx1.shape[0]
    tm = min(ROW_TILE, s)
    n_t = s // tm

    def body(dx_ref, wo_ref, pl_ref, w_ref, sc_ref, dz_ref, dw_ref, slot_ref, ext, do_ref):
        i = pl.program_id(0)
        tile = n_t - 1 - i
        _zero_slot(slot_ref)
        do_ref[...] = _mm_nt(dx_ref[...], wo_ref[HW:2 * HW, :])

        @pl.when(i == 0)
        def _():
            dw_ref[...] = jnp.zeros_like(dw_ref)
            ext[tm:tm + POOL_HALO, :] = jnp.zeros((POOL_HALO, HW), F32)

        @pl.when(i > 0)
        def _():
            ext[tm:tm + POOL_HALO, :] = ext[0:POOL_HALO, :]

        inv = _pool_counts(tile, tm)
        dpooled = []
        for g in range(HEADS):
            sl = slice(HD * g, HD * (g + 1))
            pooled_g = pl_ref[:, sl]
            do = do_ref[:, sl]
            slot_ref[0:1, sl] += jnp.sum(_mm(pooled_g, w_ref[g]) * do, axis=0, keepdims=True)
            dy = (do * sc_ref[:, sl]).astype(BF16)
            dw_ref[g] += _mm_tn(pooled_g, dy)
            dpo = _mm_nt(dy, w_ref[g])
            dpooled.append(dpo)
            ext[0:tm, sl] = dpo * inv[g]
        for g, w in enumerate(POOL_WINDOWS):
            sl = slice(HD * g, HD * (g + 1))
            win = ext[0:tm, sl]
            for d in range(1, w):
                win = win + ext[d:d + tm, sl]
            dz_ref[:, sl] = (win - dpooled[g]).astype(BF16)

    return _call_behind(
        deps, body, name="pool_bwd", grid=(n_t,),
        out_shape=[jax.ShapeDtypeStruct((s, IN_WIDTH), BF16), jax.ShapeDtypeStruct((HEADS, HD, HD), F32),
                   jax.ShapeDtypeStruct((SLOT, D_MODEL), F32)],
        in_specs=[pl.BlockSpec((tm, D_MODEL), lambda i: (n_t - 1 - i, 0)), VMEM_WHOLE,
                  pl.BlockSpec((tm, HW), lambda i: (n_t - 1 - i, 0)), _full((HEADS, HD, HD)), _full((1, HW))],
        out_specs=[pl.BlockSpec((tm, HW), lambda i: (n_t - 1 - i, 4)), _full((HEADS, HD, HD)), _full((SLOT, D_MODEL))],
        scratch_shapes=[pltpu.VMEM((tm + POOL_HALO, HW), F32), pltpu.VMEM((tm, HW), F32)],
        compiler_params=_params(("arbitrary",)),
    )(dx1, w_out, pooled, w_pool, scale)


def _hgrn_bwd(z, o, dx1, w_out, states, lb_logits, gn, dz_in, deps):
    s = z.shape[0]
    n_chunks = s // CHUNK

    def body(zq_ref, zf_ref, zi_ref, zg_ref, o_ref, dx_ref, wo_ref, st_ref, lbl_ref, gn_ref, dzin_ref,
             dz_ref, dlb_ref, dgn_ref, dstate, b_scr, dlb_acc, do_ref):
        i = pl.program_id(0)

        @pl.when(i == 0)
        def _():
            dstate[...] = jnp.zeros_like(dstate)
            dlb_acc[...] = jnp.zeros_like(dlb_acc)
            dgn_ref[...] = jnp.zeros_like(dgn_ref)
            dlb_ref[...] = jnp.zeros_like(dlb_ref)

        do_ref[...] = _mm_nt(dx_ref[...], wo_ref[0:HW, :])
        lb = _sigmoid(lbl_ref[0:1, :] - lbl_ref[1:2, :])
        row, col = _chunk_masks()
        causal = col <= row
        tri = _ones_where(causal)
        upper = _ones_where(col >= row)
        strict_lower = _ones_where(col < row)
        in_sub, in_sub_head = _sub_chunk_masks(HW), _sub_chunk_masks(HD)
        gn_row = _lanes([gn_ref[h:h + 1, :] for h in range(HEADS)])
        dlb_sum, dgn_sum = 0.0, 0.0
        for c in reversed(range(CHUNKS_PER_STEP)):
            r0 = CHUNK * c
            rs = slice(r0, r0 + CHUNK)
            zq = zq_ref[rs, :]
            q, sq, sig, f = _hgrn_gates(zq, zf_ref[rs, :], lb)
            kk = 1.0 - f
            b = _tri_dot(tri, jnp.log(f), 3)
            b_scr[rs, :] = b
            v = zi_ref[rs, :]
            o, zg, doa = o_ref[rs, :], zg_ref[rs, :], do_ref[rs, :]
            sg = _sigmoid(zg)
            n = o * _head_rms(o)
            don = doa * (zg * sg)
            dgn_sum = dgn_sum + jnp.sum(don * n, axis=0, keepdims=True)
            dn = don * gn_row
            d_o = _head_rms(o) * (dn - n * _head_mean(dn * n))
            dz_ref[rs, 3 * HW:4 * HW] = (doa * (n * gn_row) * (sg * (1.0 + zg * (1.0 - sg)))).astype(BF16)
            eq, ek = _hgrn_decay_factors(b_scr, r0, b, in_sub)
            b_last = b_scr[r0 + CHUNK - 1:r0 + CHUNK, :]
            lam, e_last, lam_last = jnp.exp(b), jnp.exp(b_last - b), jnp.exp(b_last)
            qe, qg, kd = q * eq, q * lam, kk * e_last
            ke = [kk * e for e in ek]
            dv_h, gq_h, gk_h, dqi_h, dkd_h, st_h = [], [], [], [], [], []
            for h in range(HEADS):
                vh, doh = _head(v, h), _head(d_o, h)
                st0, ds1 = st_ref[c, h], dstate[h]
                q16 = _per_sub_chunk(_head(qe, h), in_sub_head).astype(BF16)
                ke16 = _lanes([_head(ke[j], h) for j in range(N_SUB)]).astype(BF16)
                a = jnp.where(causal, _mm_nt(q16, ke16), 0.0)
                da = jnp.where(causal, _mm_nt(doh, vh), 0.0)
                dv_h.append(_mm_tn(a, doh) + _mm_nt(_head(kd, h), ds1))
                gq_h.append(_own_lane_block(_mm(da, ke16), in_sub_head))
                gk_h.append(_mm_tn(da, q16))
                dqi_h.append(_mm(doh, st0))
                dkd_h.append(_mm(vh, ds1))
                st_h.append(jnp.sum(st0 * ds1, axis=0, keepdims=True))
                dstate[h] = ds1 * _head(lam_last, h) + _mm_tn(doh, _head(qg, h))
            dz_ref[rs, 2 * HW:3 * HW] = _lanes(dv_h).astype(BF16)
            gq = _lanes(gq_h)
            gk = [_lanes([gk_h[h][:, HD * j:HD * (j + 1)] for h in range(HEADS)]) for j in range(N_SUB)]
            dq_inter = lam * _lanes(dqi_h)
            dq = eq * gq + dq_inter
            dk_intra = sum(ek[j] * gk[j] for j in range(N_SUB))
            dk_state = _lanes(dkd_h) * e_last
            db_intra = (qe.astype(BF16).astype(F32) * gq
                        - sum(ke[j].astype(BF16).astype(F32) * gk[j] for j in range(N_SUB)))
            dlf = (_tri_dot(upper, db_intra + q * dq_inter, 2) + _tri_dot(strict_lower, kk * dk_state, 2)
                   + lam_last * _lanes(st_h))
            df = dlf / f - (dk_intra + dk_state)
            dlb_sum = dlb_sum + jnp.sum(df * (1.0 - sig), axis=0, keepdims=True)
            dz_ref[rs, HW:2 * HW] = (df * (1.0 - lb) * sig * (1.0 - sig)).astype(BF16)
            dz_ref[rs, 0:HW] = (dq * (sq * (1.0 + zq * (1.0 - sq)))).astype(BF16)
        dlb_acc[...] += dlb_sum
        for h in range(HEADS):
            dgn_ref[h:h + 1, 0:HD] += _head(dgn_sum, h)

        @pl.when(i == n_steps - 1)
        def _():
            dl0 = dlb_acc[...] * lb * (1.0 - lb)
            dlb_ref[0:1, 0:HW] = dl0
            dlb_ref[1:2, 0:HW] = -dl0

    rows = CHUNK * CHUNKS_PER_STEP
    n_steps = s // rows
    rev = lambda i: n_steps - 1 - i
    zspec = lambda cb: pl.BlockSpec((rows, HW), lambda i, cb=cb: (rev(i), cb))
    slot = jax.ShapeDtypeStruct((SLOT, D_MODEL), F32)
    return _call_behind(
        deps, body, name="hgrn_bwd", grid=(n_steps,),
        out_shape=[jax.ShapeDtypeStruct((s, IN_WIDTH), BF16), slot, slot],
        in_specs=[zspec(0), zspec(1), zspec(2), zspec(3), pl.BlockSpec((rows, HW), lambda i: (rev(i), 0)),
                  pl.BlockSpec((rows, D_MODEL), lambda i: (rev(i), 0)), VMEM_WHOLE,
                  pl.BlockSpec((CHUNKS_PER_STEP, HEADS, HD, HD), lambda i: (rev(i), 0, 0, 0)), _full((2, HW)),
                  _full((HEADS, HD)), ANY_SPACE],
        out_specs=[pl.BlockSpec((rows, 4 * HW), lambda i: (rev(i), 0)), _full((SLOT, D_MODEL)), _full((SLOT, D_MODEL))],
        scratch_shapes=[pltpu.VMEM((HEADS, HD, HD), F32), pltpu.VMEM((rows, HW), F32), pltpu.VMEM((1, HW), F32),
                        pltpu.VMEM((rows, HW), F32)],
        input_output_aliases={10: 0},
        compiler_params=_params(("arbitrary",)),
    )(z, z, z, z, o, dx1, w_out, states, lb_logits, gn, dz_in)


def _in_bwd(dz, w_t, x0, g, dx1, deps):
    s = x0.shape[0]
    tm = min(WIDE_ROW_TILE, s)

    def body(dz_ref, w_ref, x_ref, g_ref, d_ref, dx_ref, slot_ref):
        _zero_slot(slot_ref)
        dx, dg = _rms_bwd(x_ref[...], g_ref[...], _mm(dz_ref[...], w_ref[...]))
        dx_ref[...] = d_ref[...] + dx
        slot_ref[0:1, :] += dg

    row_f32 = pl.BlockSpec((tm, D_MODEL), lambda i: (i, 0))
    return _call_behind(
        deps, body, name="in_bwd", grid=(s // tm,),
        out_shape=[jax.ShapeDtypeStruct((s, D_MODEL), F32), jax.ShapeDtypeStruct((SLOT, D_MODEL), F32)],
        in_specs=[pl.BlockSpec((tm, IN_WIDTH), lambda i: (i, 0)), VMEM_WHOLE, row_f32, _full((1, D_MODEL)), row_f32],
        out_specs=[row_f32, _full((SLOT, D_MODEL))],
        compiler_params=_params(("arbitrary",)),
    )(dz, w_t, x0, g, dx1)


def kernel(x, mem, norm_mix_g, w_in, lb_logits, hgrn_norm_g, w_pool, pool_scale, w_out, norm_x_g, norm_mem_g, w_xq, w_xk, w_xv, w_xo, norm_ffn_g, w_ff1, w_ff2, final_norm_g, loss_target, m_norm_mix_g, m_w_in, m_lb_logits, m_hgrn_norm_g, m_w_pool, m_pool_scale, m_w_out, m_norm_x_g, m_norm_mem_g, m_w_xq, m_w_xk, m_w_xv, m_w_xo, m_norm_ffn_g, m_w_ff1, m_w_ff2, m_final_norm_g, v_norm_mix_g, v_w_in, v_lb_logits, v_hgrn_norm_g, v_w_pool, v_pool_scale, v_w_out, v_norm_x_g, v_norm_mem_g, v_w_xq, v_w_xk, v_w_xv, v_w_xo, v_norm_ffn_g, v_w_ff1, v_w_ff2, v_final_norm_g):
    x0 = x[0]
    mem0 = mem[0]
    tgt = loss_target[0]
    gn = hgrn_norm_g[0]
    gfin = final_norm_g.reshape(1, D_MODEL)
    wp = w_pool[0]
    heads_2d = lambda w: w.reshape(D_MODEL // N_DEV, D_MODEL)
    xo_2d = lambda w: w.reshape(D_MODEL, D_MODEL // N_DEV)

    first = _all_gather_weights([w_in[0].T], [w_out[0], heads_2d(w_xq), heads_2d(w_xk), heads_2d(w_xv), xo_2d(w_xo).T,
                                              w_ff1[0], w_ff2[0]])
    win_t = first[0].reshape(IN_WIDTH, D_MODEL)
    ga_attn, ga_mlp = _gather_first_start([first[1:6], first[6:8]], "gather_first_start")

    z, h = _in_proj(x0, norm_mix_g, win_t, deps=[ga_attn[3]])
    mixed_a, o_pre, states = _hgrn_fwd(z, lb_logits, gn)
    lands = _split_wait(_gather_first_copies, ga_attn, o_pre, "gather_attn_first_wait")
    gb_attn = _gather_forward_start(lands, "gather_attn_forward_start")
    mixed, pooled = _pool_fwd(z, wp, pool_scale, mixed_a, deps=[gb_attn[3]])
    lands = _split_wait(_gather_forward_copies, gb_attn, pooled, "gather_attn_forward_wait")
    wout_f, wq_f, wk_f, wv_f, wo_t = (t.reshape(D_MODEL, D_MODEL) for t in lands)
    hm, xk, xv = _mem_kv(mem0, norm_mem_g, wk_f, wv_f, deps=[])
    x1, x2, hq, xq, att = _mix_xattn_fwd(x0, mixed, wout_f, norm_x_g, wq_f, xk, xv, wo_t, deps=[])
    lands = _split_wait(_gather_first_copies, ga_mlp, x2, "gather_mlp_first_wait")
    gb_mlp = _gather_forward_start(lands, "gather_mlp_forward_start")
    w1_b, w2_b = _split_wait(_gather_forward_copies, gb_mlp, gb_mlp[3], "gather_mlp_forward_wait")
    dx3, dx3_16, u, hf, slot_fin = _mlp_fwd_loss(x2, norm_ffn_g, w1_b, w2_b.reshape(D_FF, D_MODEL), gfin, tgt)

    rows = lambda t, r: t.reshape(N_DEV, r, D_MODEL)
    dw2 = _wgrad(u, dx3_16, "wgrad_ff2")
    ex_ff2 = _all_to_all_start([rows(dw2, FF_BLK)], [], "exchange_ff2_start")
    da, dx2, slot_ffn = _mlp_bwd(dx3, u, x2, norm_ffn_g, w1_b, w2_b, deps=[ex_ff2[3]])
    dw1 = _wgrad(hf, da, "wgrad_ff1", col_blocks=True)
    ex_ff1 = _all_to_all_start([dw1], [], "exchange_ff1_start")
    dx1, dx1_16, dxq, dxk, dxv, slot_x = _xattn_bwd(dx2, x1, norm_x_g, xq, xk, xv, wq_f, wo_t, deps=[ex_ff1[3]])
    dwo_t = _wgrad(dx2, att, "wgrad_xo")
    dwq = _wgrad(hq, dxq, "wgrad_xq")
    dwk, dwv, slot_mem = _mem_bwd(mem0, norm_mem_g, hm, dxk, dxv, wk_f, wv_f)
    ex_attn = _all_to_all_start([rows(dwq, 128), rows(dwk, 128), rows(dwv, 128), rows(dwo_t, 128)], [],
                                "exchange_attn_start")
    dwout = _wgrad(mixed, dx1_16, "wgrad_out")
    dz_pool, d_wpool, slot_ps = _pool_bwd(dx1_16, wout_f, pooled, wp, pool_scale, deps=[ex_attn[3]])
    small0 = jnp.concatenate([slot_x, slot_mem, slot_ffn, slot_fin, slot_ps], axis=0)
    ex_out = _all_to_all_start([rows(dwout, 128)], [small0, d_wpool], "exchange_out_start")
    dz, slot_lb, slot_gn = _hgrn_bwd(z, o_pre, dx1_16, wout_f, states, lb_logits, gn, dz_pool, deps=[ex_out[3]])
    dwin_t = _wgrad(dz, h, "wgrad_in")
    small1 = jnp.concatenate([slot_lb, slot_gn], axis=0)
    ex_in = _all_to_all_start([rows(dwin_t, 320)], [small1], "exchange_in_start")
    grad_x, slot_mix = _in_bwd(dz, win_t, x0, norm_mix_g, dx1, deps=[ex_in[3]])
    ex_mix = _all_to_all_start([], [slot_mix], "exchange_mix_start")

    out = {}
    (r_2,) = _split_wait(_all_to_all_copies(1), ex_ff2, ex_mix[3], "exchange_ff2_wait")
    out["w_ff2"] = _sum_adamw(r_2, w_ff2[0], m_w_ff2[0], v_w_ff2[0], "adamw_ff2")
    (r_1,) = _split_wait(_all_to_all_copies(1), ex_ff1, out["w_ff2"][1], "exchange_ff1_wait")
    out["w_ff1"] = _sum_adamw(r_1, w_ff1[0], m_w_ff1[0], v_w_ff1[0], "adamw_ff1")
    r_q, r_k, r_v, r_o = _split_wait(_all_to_all_copies(4), ex_attn, out["w_ff1"][1], "exchange_attn_wait")
    g_qkv = [g.reshape(w_xq.shape) for g in _sum_sources_whole([r_q, r_k, r_v], "sum_grad_qkv")]
    qkv = _adamw_whole([(g_qkv[0], w_xq, m_w_xq, v_w_xq), (g_qkv[1], w_xk, m_w_xk, v_w_xk),
                        (g_qkv[2], w_xv, m_w_xv, v_w_xv)], "adamw_qkv")
    for n, g, res in zip(("w_xq", "w_xk", "w_xv"), g_qkv, qkv):
        out[n] = (g, *res)
    g_xo = _sum_sources(r_o, "sum_grad_xo").T
    out["w_xo"] = (g_xo, *_adamw(g_xo, xo_2d(w_xo), xo_2d(m_w_xo), xo_2d(v_w_xo), "adamw_xo"))
    r_out, r_small0, r_wpool = _split_wait(_all_to_all_copies(1), ex_out, out["w_xo"][1], "exchange_out_wait")
    out["w_out"] = _sum_adamw(r_out, w_out[0], m_w_out[0], v_w_out[0], "adamw_out")
    r_in, r_small1 = _split_wait(_all_to_all_copies(1), ex_in, out["w_out"][1], "exchange_in_wait")
    g_in = _sum_sources(r_in, "sum_grad_in").T
    out["w_in"] = (g_in, *_adamw(g_in, w_in[0], m_w_in[0], v_w_in[0], "adamw_in"))
    (r_small2,) = _split_wait(_all_to_all_copies(0), ex_mix, out["w_in"][1], "exchange_mix_wait")
    row = lambda t: t.reshape(1, -1)
    small_params = {
        "norm_mix_g": (norm_mix_g, m_norm_mix_g, v_norm_mix_g),
        "lb_logits": (lb_logits, m_lb_logits, v_lb_logits),
        "hgrn_norm_g": (hgrn_norm_g[0], m_hgrn_norm_g[0], v_hgrn_norm_g[0]),
        "pool_scale": (pool_scale, m_pool_scale, v_pool_scale),
        "norm_x_g": (norm_x_g, m_norm_x_g, v_norm_x_g),
        "norm_mem_g": (norm_mem_g, m_norm_mem_g, v_norm_mem_g),
        "norm_ffn_g": (norm_ffn_g, m_norm_ffn_g, v_norm_ffn_g),
        "final_norm_g": (row(final_norm_g), row(m_final_norm_g), row(v_final_norm_g)),
        "w_pool": (wp, m_w_pool[0], v_w_pool[0]),
    }
    loss, small_out = _small_update([r_small0, r_small1, r_small2], r_wpool, small_params)
    out.update(small_out)

    shapes = dict(norm_mix_g=norm_mix_g, w_in=w_in, lb_logits=lb_logits, hgrn_norm_g=hgrn_norm_g, w_pool=w_pool,
                  pool_scale=pool_scale, w_out=w_out, norm_x_g=norm_x_g, norm_mem_g=norm_mem_g, w_xq=w_xq, w_xk=w_xk,
                  w_xv=w_xv, w_xo=w_xo, norm_ffn_g=norm_ffn_g, w_ff1=w_ff1, w_ff2=w_ff2, final_norm_g=final_norm_g)
    order = list(shapes)
    group = lambda k: [out[n][k].reshape(shapes[n].shape) for n in order]
    return (loss.reshape(()), grad_x.reshape(x.shape), *group(0), *group(1), *group(2), *group(3))
```

```python
import jax
import jax.numpy as jnp
from jax import lax
from jax.experimental import pallas as pl
from jax.experimental.pallas import tpu as pltpu

F32 = jnp.float32
BF16 = jnp.bfloat16

D_MODEL = 1024
N_DEV = 8
HEADS = 4
HD = 128
HW = HEADS * HD
IN_WIDTH = 5 * HW
XHD = 256
MEM_LEN = 256
D_FF = 4096
FF_BLK = D_FF // N_DEV
POOL_WINDOWS = (2, 4, 8, 16)
POOL_HALO = 16
CHUNK = 64
CHUNKS_PER_STEP = 8
SUB = 16
N_SUB = CHUNK // SUB
EXP_CAP = 80.0
EPS = 1e-6
TINY = 1e-30
ROW_TILE = 512
WIDE_ROW_TILE = 1024
SLOT = 8
V7X_VMEM_LIMIT = 56 * 1024 * 1024

ADAM_LR = 0.001
ADAM_B1 = 0.9
ADAM_B2 = 0.999
ADAM_EPS = 1e-08
ADAM_WD = 0.01
ADAM_STEP = 10

MESH_ID = pl.DeviceIdType.MESH


def _params(sem=None, vmem=V7X_VMEM_LIMIT):
    return pltpu.CompilerParams(dimension_semantics=sem, vmem_limit_bytes=vmem)


def _mm(a, b):
    return lax.dot_general(a.astype(BF16), b.astype(BF16), (((1,), (0,)), ((), ())), preferred_element_type=F32)


def _mm_nt(a, b):
    return lax.dot_general(a.astype(BF16), b.astype(BF16), (((1,), (1,)), ((), ())), preferred_element_type=F32)


def _mm_tn(a, b):
    return lax.dot_general(a.astype(BF16), b.astype(BF16), (((0,), (0,)), ((), ())), preferred_element_type=F32)


def _sigmoid(x):
    return 1.0 / (1.0 + jnp.exp(-x))


def _rms(x):
    return lax.rsqrt(jnp.mean(x * x, axis=-1, keepdims=True) + EPS)


def _rms_bwd(x, g, dh):
    r = _rms(x)
    n = x * r
    dn = dh * g
    dx = r * (dn - n * jnp.mean(dn * n, axis=-1, keepdims=True))
    return dx, jnp.sum(dh * n, axis=0, keepdims=True)


def _tri_dot(tri, x, passes):
    acc = None
    rest = x
    for _ in range(passes):
        piece = rest.astype(BF16)
        part = lax.dot_general(tri, piece, (((1,), (0,)), ((), ())), preferred_element_type=F32)
        acc = part if acc is None else acc + part
        rest = rest - piece.astype(F32)
    return acc


def _adam_update(g, w, m, v):
    nm = ADAM_B1 * m + (1.0 - ADAM_B1) * g
    nv = ADAM_B2 * v + (1.0 - ADAM_B2) * (g * g)
    m_hat = nm / (1.0 - ADAM_B1 ** ADAM_STEP)
    v_hat = nv / (1.0 - ADAM_B2 ** ADAM_STEP)
    return -ADAM_LR * (m_hat / (jnp.sqrt(v_hat) + ADAM_EPS) + ADAM_WD * w), nm, nv


def _full(shape):
    return pl.BlockSpec(shape, lambda *_: (0,) * len(shape))


VMEM_WHOLE = pl.BlockSpec(memory_space=pltpu.VMEM)
ANY_SPACE = pl.BlockSpec(memory_space=pl.ANY)


def _mesh_pos():
    return lax.axis_index("x"), lax.axis_index("y"), lax.axis_index("c")


def _flat(px, py, pc):
    return 4 * px + 2 * py + pc


def _all_gather_weights(shards, cast_only):
    n, nc = len(shards), len(cast_only)
    step = 64

    def body(*refs):
        x_refs, c_refs = refs[:n], refs[n:n + nc]
        out_refs, cast_refs = refs[n + nc:2 * n + nc], refs[2 * n + nc:2 * n + 2 * nc]
        bufs = refs[2 * n + 2 * nc:3 * n + 2 * nc]
        send_sems, recv_sems, local_sems = refs[3 * n + 2 * nc:]
        x, y, c = _mesh_pos()
        me, sibling = (x, y, c), (x, y, 1 - c)
        chips = [(1 - x, y), (x, 1 - y), (1 - x, 1 - y)]

        def copy(a, k, blk, to, src=None):
            rows = out_refs[a].at[_flat(*blk)]
            return pltpu.make_async_remote_copy(
                src_ref=rows if src is None else src, dst_ref=rows,
                send_sem=send_sems.at[7 * a + k], recv_sem=recv_sems.at[7 * a + k], device_id=to, device_id_type=MESH_ID)

        def cast_rows(src, dst, rows):
            def cast(i, carry):
                r0 = pl.multiple_of(i * step, step)
                dst[pl.ds(r0, step), :] = src[pl.ds(r0, step), :].astype(BF16)
                return carry
            lax.fori_loop(0, rows // step, cast, 0)

        first, mine = [], []
        for a in range(n):
            cast_rows(x_refs[a], bufs[a], shards[a].shape[0])
            mine.append(pltpu.make_async_copy(bufs[a], out_refs[a].at[_flat(*me)], local_sems.at[a]))
            first.append(copy(a, 0, me, sibling, src=bufs[a]))
            first += [copy(a, 1 + j, me, (*chip, c), src=bufs[a]) for j, chip in enumerate(chips)]
            for cp in [mine[-1]] + first[-4:]:
                cp.start()
        for a in range(nc):
            cast_rows(c_refs[a], cast_refs[a], cast_only[a].shape[0])
        passed = []
        for j, chip in enumerate(chips):
            for a in range(n):
                copy(a, 1 + j, (*chip, c), me).wait_recv()
                passed.append(copy(a, 4 + j, (*chip, c), sibling))
                passed[-1].start()
        for a in range(n):
            copy(a, 0, sibling, me).wait_recv()
            for j, chip in enumerate(chips):
                copy(a, 4 + j, (*chip, 1 - c), me).wait_recv()
        for cp in first + passed:
            cp.wait_send()
        for cp in mine:
            cp.wait()

    return pl.pallas_call(
        body, name="all_gather_w_in",
        out_shape=[jax.ShapeDtypeStruct((N_DEV,) + s.shape, BF16) for s in shards]
        + [jax.ShapeDtypeStruct(s.shape, BF16) for s in cast_only],
        in_specs=[VMEM_WHOLE] * (n + nc), out_specs=[ANY_SPACE] * n + [VMEM_WHOLE] * nc,
        scratch_shapes=[pltpu.VMEM(s.shape, BF16) for s in shards]
        + [pltpu.SemaphoreType.DMA((7 * n,)), pltpu.SemaphoreType.DMA((7 * n,)), pltpu.SemaphoreType.DMA((n,))],
        compiler_params=_params(),
    )(*shards, *cast_only)


HBM_SPEC = pl.BlockSpec(memory_space=pltpu.HBM)
SEM_SPEC = pl.BlockSpec(memory_space=pltpu.SEMAPHORE)
EFFECT = pltpu.SideEffectType.DATAFLOW_SIDE_EFFECTING
TOKEN = jax.ShapeDtypeStruct((8, 128), F32)


def _in_hbm(a):
    return pltpu.with_memory_space_constraint(a, pltpu.HBM)


def _split_start(copies_of, srcs, lands, n_sems, name):
    ns, nl, k = len(srcs), len(lands), len(n_sems)

    def body(*refs):
        src_refs, land_refs = refs[:ns], refs[ns:ns + nl]
        sems = refs[ns + nl:ns + nl + k]
        token = refs[-1]
        for cp in copies_of(src_refs, land_refs, sems):
            cp.start()
        token[...] = jnp.zeros_like(token)

    outs = pl.pallas_call(
        body, name=name,
        out_shape=[pltpu.SemaphoreType.DMA((q,)) for q in n_sems]
        + [pltpu.HBM(a.shape, a.dtype) for a in list(srcs) + list(lands)] + [TOKEN],
        in_specs=[HBM_SPEC] * (ns + nl),
        out_specs=[SEM_SPEC] * k + [HBM_SPEC] * (ns + nl) + [VMEM_WHOLE],
        input_output_aliases={i: k + i for i in range(ns + nl)},
        compiler_params=pltpu.CompilerParams(has_side_effects=EFFECT),
    )(*[_in_hbm(a) for a in list(srcs) + list(lands)])
    return outs[:k], outs[k:k + ns], outs[k + ns:k + ns + nl], outs[-1]


def _split_wait(copies_of, handle, after, name):
    sems, srcs, lands, _ = handle
    ns, nl, k = len(srcs), len(lands), len(sems)

    def body(*refs):
        src_refs, land_refs = refs[:ns], refs[ns:ns + nl]
        sem_refs = refs[ns + nl:ns + nl + k]
        for cp in copies_of(src_refs, land_refs, sem_refs):
            cp.wait()

    outs = pl.pallas_call(
        body, name=name,
        out_shape=[pltpu.HBM(a.shape, a.dtype) for a in list(srcs) + list(lands)],
        in_specs=[HBM_SPEC] * (ns + nl) + [SEM_SPEC] * k + [ANY_SPACE],
        out_specs=[HBM_SPEC] * (ns + nl),
        input_output_aliases={i: i for i in range(ns + nl)},
        compiler_params=pltpu.CompilerParams(has_side_effects=EFFECT),
    )(*srcs, *lands, *sems, after)
    return outs[ns:]


def _gather_first_copies(shard_refs, land_refs, sems):
    send_sems, recv_sems, local_sems = sems
    x, y, c = _mesh_pos()
    me = _flat(x, y, c)
    peers = [(x, y, 1 - c), (1 - x, y, c), (x, 1 - y, c), (1 - x, 1 - y, c)]
    copies = []
    for a, (shard, land) in enumerate(zip(shard_refs, land_refs)):
        copies.append(pltpu.make_async_copy(shard, land.at[me], local_sems.at[a]))
        for k, peer in enumerate(peers):
            copies.append(pltpu.make_async_remote_copy(
                src_ref=shard, dst_ref=land.at[me], send_sem=send_sems.at[4 * a + k], recv_sem=recv_sems.at[4 * a + k],
                device_id=peer, device_id_type=MESH_ID))
    return copies


def _gather_forward_copies(src_refs, land_refs, sems):
    del src_refs
    send_sems, recv_sems = sems
    x, y, c = _mesh_pos()
    chips = [(1 - x, y), (x, 1 - y), (1 - x, 1 - y)]
    copies = []
    for a, land in enumerate(land_refs):
        for j, chip in enumerate(chips):
            rows = land.at[_flat(*chip, c)]
            copies.append(pltpu.make_async_remote_copy(
                src_ref=rows, dst_ref=rows, send_sem=send_sems.at[3 * a + j], recv_sem=recv_sems.at[3 * a + j],
                device_id=(x, y, 1 - c), device_id_type=MESH_ID))
    return copies


def _gather_first_start(groups, name):
    shards = [s for g in groups for s in g]
    lands = [lax.empty((N_DEV,) + s.shape, s.dtype) for s in shards]
    bounds = [sum(len(g) for g in groups[:i]) for i in range(len(groups) + 1)]

    def copies_of(src_refs, land_refs, sems):
        copies = []
        for i in range(len(groups)):
            lo, hi = bounds[i], bounds[i + 1]
            copies += _gather_first_copies(src_refs[lo:hi], land_refs[lo:hi], sems[3 * i:3 * i + 3])
        return copies

    n_sems = tuple(q for g in groups for q in (4 * len(g), 4 * len(g), len(g)))
    sems, srcs, lands, token = _split_start(copies_of, shards, lands, n_sems, name)
    return [(sems[3 * i:3 * i + 3], srcs[bounds[i]:bounds[i + 1]], lands[bounds[i]:bounds[i + 1]], token)
            for i in range(len(groups))]


def _gather_forward_start(lands, name):
    n = len(lands)
    return _split_start(_gather_forward_copies, [], lands, (3 * n, 3 * n), name)


def _all_to_all_copies(n_scattered):
    def copies_of(src_refs, land_refs, sems):
        send_sems, recv_sems, local_sems = sems
        x, y, c = _mesh_pos()
        me = _flat(x, y, c)
        copies = []
        for a, (src, land) in enumerate(zip(src_refs, land_refs)):
            scattered = a < n_scattered
            copies.append(pltpu.make_async_copy(src.at[me] if scattered else src, land.at[me], local_sems.at[a]))
            for k in range(1, N_DEV):
                peer = (1 - x if k & 4 else x, 1 - y if k & 2 else y, 1 - c if k & 1 else c)
                copies.append(pltpu.make_async_remote_copy(
                    src_ref=src.at[_flat(*peer)] if scattered else src, dst_ref=land.at[me],
                    send_sem=send_sems.at[7 * a + k - 1], recv_sem=recv_sems.at[7 * a + k - 1],
                    device_id=peer, device_id_type=MESH_ID))
        return copies
    return copies_of


def _all_to_all_start(scattered, broadcast, name):
    srcs = list(scattered) + list(broadcast)
    lands = [lax.empty(a.shape, a.dtype) for a in scattered] + [lax.empty((N_DEV,) + a.shape, a.dtype) for a in broadcast]
    n = len(srcs)
    return _split_start(_all_to_all_copies(len(scattered)), srcs, lands, (7 * n, 7 * n, n), name)


def _call_behind(deps, body, *, in_specs, **kwargs):
    n_in, n_dep = len(in_specs), len(deps)

    def body_without_deps(*refs):
        return body(*refs[:n_in], *refs[n_in + n_dep:])

    call = pl.pallas_call(body_without_deps, in_specs=list(in_specs) + [ANY_SPACE] * n_dep, **kwargs)
    return lambda *operands: call(*operands, *deps)


def _row_tile(rows):
    for cand in (256, 128, 64, 32, 16):
        if rows % cand == 0:
            return cand
    return rows


def _adamw_whole(groups, name):
    n = len(groups)

    def body(*refs):
        for i in range(n):
            g_ref, w_ref, m_ref, v_ref = refs[4 * i:4 * i + 4]
            d_ref, nm_ref, nv_ref = refs[4 * n + 3 * i:4 * n + 3 * i + 3]
            d_ref[...], nm_ref[...], nv_ref[...] = _adam_update(g_ref[...], w_ref[...], m_ref[...], v_ref[...])

    outs = pl.pallas_call(
        body, name=name, out_shape=[jax.ShapeDtypeStruct(grp[0].shape, F32) for grp in groups for _ in range(3)],
        in_specs=[VMEM_WHOLE] * (4 * n), out_specs=[VMEM_WHOLE] * (3 * n),
        compiler_params=_params(),
    )(*[t for grp in groups for t in grp])
    return [outs[3 * i:3 * i + 3] for i in range(n)]


def _sum_sources_whole(recvs, name):
    n = len(recvs)

    def body(*refs):
        for r_ref, o_ref in zip(refs[:n], refs[n:]):
            acc = r_ref[0].astype(F32)
            for d in range(1, N_DEV):
                acc = acc + r_ref[d].astype(F32)
            o_ref[...] = acc

    return pl.pallas_call(
        body, name=name, out_shape=[jax.ShapeDtypeStruct(r.shape[1:], F32) for r in recvs],
        in_specs=[VMEM_WHOLE] * n, out_specs=[VMEM_WHOLE] * n,
        compiler_params=_params(),
    )(*recvs)


def _sum_adamw(recv, w, m, v, name):
    _, rows, cols = recv.shape
    tile = _row_tile(rows)

    def body(r_ref, w_ref, m_ref, v_ref, g_ref, d_ref, nm_ref, nv_ref):
        acc = r_ref[0].astype(F32)
        for d in range(1, N_DEV):
            acc = acc + r_ref[d].astype(F32)
        g_ref[...] = acc
        d_ref[...], nm_ref[...], nv_ref[...] = _adam_update(acc, w_ref[...], m_ref[...], v_ref[...])

    spec = pl.BlockSpec((tile, cols), lambda i: (i, 0))
    shp = jax.ShapeDtypeStruct((rows, cols), F32)
    return pl.pallas_call(
        body, name=name, grid=(rows // tile,), out_shape=[shp] * 4,
        in_specs=[pl.BlockSpec((N_DEV, tile, cols), lambda i: (0, i, 0)), spec, spec, spec], out_specs=[spec] * 4,
        compiler_params=_params(("parallel",)),
    )(recv, w, m, v)


SMALL_SLOTS = {"norm_x_g": (0, 0, 1, D_MODEL), "norm_mem_g": (0, 8, 1, D_MODEL), "norm_ffn_g": (0, 16, 1, D_MODEL),
               "final_norm_g": (0, 24, 1, D_MODEL), "pool_scale": (0, 32, 1, HW),
               "lb_logits": (1, 0, 2, HW), "hgrn_norm_g": (1, 8, HEADS, HD), "norm_mix_g": (2, 0, 1, D_MODEL)}
LOSS_ROW = 25
SMALL_ORDER = ("norm_mix_g", "lb_logits", "hgrn_norm_g", "pool_scale", "norm_x_g", "norm_mem_g", "norm_ffn_g",
               "final_norm_g", "w_pool")


def _small_update(srecvs, wprecv, params):
    flat = [t for n in SMALL_ORDER for t in params[n]]
    nb = len(srecvs)
    n_in = nb + 1 + len(flat)

    def body(*refs):
        s_refs, wp_ref = refs[0:nb], refs[nb]
        in_refs = refs[nb + 1:n_in]
        loss_ref = refs[n_in]
        out_refs = refs[n_in + 1:-nb]
        accs = refs[-nb:]
        for s_ref, acc in zip(s_refs, accs):
            total = s_ref[0]
            for d in range(1, N_DEV):
                total = total + s_ref[d]
            acc[...] = total
        loss_ref[...] = accs[0][LOSS_ROW:LOSS_ROW + 1, 0:1]
        for i, name in enumerate(SMALL_ORDER):
            w_ref, m_ref, v_ref = in_refs[3 * i:3 * i + 3]
            g_ref, d_ref, nm_ref, nv_ref = out_refs[4 * i:4 * i + 4]
            if name == "w_pool":
                g = wp_ref[0]
                for d in range(1, N_DEV):
                    g = g + wp_ref[d]
            else:
                buf, r0, nr, nc = SMALL_SLOTS[name]
                g = accs[buf][r0:r0 + nr, 0:nc]
            g_ref[...] = g
            d_ref[...], nm_ref[...], nv_ref[...] = _adam_update(g, w_ref[...], m_ref[...], v_ref[...])

    out_shape = [jax.ShapeDtypeStruct((1, 1), F32)]
    for n in SMALL_ORDER:
        out_shape += [jax.ShapeDtypeStruct(params[n][0].shape, F32)] * 4
    outs = pl.pallas_call(
        body, name="small_update", out_shape=out_shape,
        in_specs=[VMEM_WHOLE] * n_in, out_specs=[VMEM_WHOLE] * len(out_shape),
        scratch_shapes=[pltpu.VMEM(r.shape[1:], F32) for r in srecvs],
        compiler_params=_params(),
    )(*srecvs, wprecv, *flat)
    return outs[0], {n: outs[1 + 4 * i:5 + 4 * i] for i, n in enumerate(SMALL_ORDER)}


def _in_proj(x, g, w_t, deps):
    s = x.shape[0]
    tm = min(ROW_TILE, s)

    def body(x_ref, g_ref, w_ref, z_ref, h_ref):
        xv = x_ref[...]
        h = (xv * _rms(xv) * g_ref[...]).astype(BF16)
        h_ref[...] = h
        z_ref[...] = _mm_nt(h, w_ref[...])

    return _call_behind(
        deps, body, name="in_proj", grid=(s // tm,),
        out_shape=[jax.ShapeDtypeStruct((s, IN_WIDTH), F32), jax.ShapeDtypeStruct((s, D_MODEL), BF16)],
        in_specs=[pl.BlockSpec((tm, D_MODEL), lambda i: (i, 0)), _full((1, D_MODEL)), VMEM_WHOLE],
        out_specs=[pl.BlockSpec((tm, IN_WIDTH), lambda i: (i, 0)), pl.BlockSpec((tm, D_MODEL), lambda i: (i, 0))],
        compiler_params=_params(("parallel",)),
    )(x, g, w_t)


def _chunk_masks():
    row = lax.broadcasted_iota(jnp.int32, (CHUNK, CHUNK), 0)
    col = lax.broadcasted_iota(jnp.int32, (CHUNK, CHUNK), 1)
    return row, col


def _ones_where(mask):
    return jnp.where(mask, 1.0, 0.0).astype(BF16)


def _hgrn_gates(zq, zf, lb):
    sq = _sigmoid(zq)
    sig = _sigmoid(zf)
    f = lb + (1.0 - lb) * sig
    return zq * sq, sq, sig, f


def _sub_chunk_masks(width):
    trow = lax.broadcasted_iota(jnp.int32, (CHUNK, width), 0)
    return [(trow >= SUB * j) & (trow < SUB * (j + 1)) for j in range(N_SUB)]


def _head(a, h):
    return a[:, HD * h:HD * (h + 1)]


def _lanes(parts):
    return jnp.concatenate(parts, axis=1)


def _hgrn_decay_factors(b_scr, r0, b, in_sub):
    bases = [jnp.zeros((1, HW), F32)] + [b_scr[r0 + SUB * j - 1:r0 + SUB * j, :] for j in range(1, N_SUB)]
    own_base = bases[N_SUB - 1]
    for j in range(N_SUB - 2, -1, -1):
        own_base = jnp.where(in_sub[j], bases[j], own_base)
    eq = jnp.exp(b - own_base)
    ek = []
    for j in range(N_SUB):
        upto = SUB * (j + 1)
        e = jnp.exp(jnp.minimum(bases[j] - b[0:upto], EXP_CAP))
        ek.append(e if upto == CHUNK else jnp.concatenate([e, jnp.zeros((CHUNK - upto, HW), F32)], axis=0))
    return eq, ek


def _per_sub_chunk(x, in_sub):
    return _lanes([jnp.where(in_sub[j], x, 0.0) for j in range(N_SUB)])


def _own_lane_block(a, in_sub):
    out = a[:, HD * (N_SUB - 1):HD * N_SUB]
    for j in range(N_SUB - 2, -1, -1):
        out = jnp.where(in_sub[j], a[:, HD * j:HD * (j + 1)], out)
    return out


def _head_rms(o):
    return _lanes([jnp.broadcast_to(_rms(_head(o, h)), (CHUNK, HD)) for h in range(HEADS)])


def _head_mean(a):
    return _lanes([jnp.broadcast_to(jnp.mean(_head(a, h), axis=-1, keepdims=True), (CHUNK, HD)) for h in range(HEADS)])


def _hgrn_fwd(z, lb_logits, gn):
    s = z.shape[0]
    n_chunks = s // CHUNK

    def body(zq_ref, zf_ref, zi_ref, zg_ref, lbl_ref, gn_ref, oa_ref, o_ref, st_ref, state, b_scr):
        @pl.when(pl.program_id(0) == 0)
        def _():
            state[...] = jnp.zeros_like(state)

        lb = _sigmoid(lbl_ref[0:1, :] - lbl_ref[1:2, :])
        row, col = _chunk_masks()
        causal = col <= row
        tri = _ones_where(causal)
        in_sub, in_sub_head = _sub_chunk_masks(HW), _sub_chunk_masks(HD)
        gn_row = _lanes([gn_ref[h:h + 1, :] for h in range(HEADS)])
        for c in range(CHUNKS_PER_STEP):
            r0 = CHUNK * c
            rs = slice(r0, r0 + CHUNK)
            st_ref[c] = state[...]
            q, _, _, f = _hgrn_gates(zq_ref[rs, :], zf_ref[rs, :], lb)
            kk = 1.0 - f
            b = _tri_dot(tri, jnp.log(f), 3)
            b_scr[rs, :] = b
            eq, ek = _hgrn_decay_factors(b_scr, r0, b, in_sub)
            b_last = b_scr[r0 + CHUNK - 1:r0 + CHUNK, :]
            qe, qg = q * eq, q * jnp.exp(b)
            ke = [kk * e for e in ek]
            kd = kk * jnp.exp(b_last - b)
            lam_last = jnp.exp(b_last)
            v = zi_ref[rs, :]
            o_heads = []
            for h in range(HEADS):
                vh, st = _head(v, h), state[h]
                a = jnp.where(causal, _mm_nt(_per_sub_chunk(_head(qe, h), in_sub_head),
                                             _lanes([_head(ke[j], h) for j in range(N_SUB)])), 0.0)
                o_heads.append(_mm(a, vh) + _mm_nt(_head(qg, h), st))
                state[h] = st * _head(lam_last, h) + _mm_tn(vh, _head(kd, h))
            o = _lanes(o_heads)
            o_ref[rs, :] = o
            zg = zg_ref[rs, :]
            oa_ref[rs, :] = (o * _head_rms(o) * gn_row * zg * _sigmoid(zg)).astype(BF16)

    rows = CHUNK * CHUNKS_PER_STEP
    zspec = lambda cb: pl.BlockSpec((rows, HW), lambda i, cb=cb: (i, cb))
    return pl.pallas_call(
        body, name="hgrn_fwd", grid=(s // rows,),
        out_shape=[jax.ShapeDtypeStruct((s, 2 * HW), BF16), jax.ShapeDtypeStruct((s, HW), F32),
                   jax.ShapeDtypeStruct((n_chunks, HEADS, HD, HD), F32)],
        in_specs=[zspec(0), zspec(1), zspec(2), zspec(3), _full((2, HW)), _full((HEADS, HD))],
        out_specs=[pl.BlockSpec((rows, HW), lambda i: (i, 0)), pl.BlockSpec((rows, HW), lambda i: (i, 0)),
                   pl.BlockSpec((CHUNKS_PER_STEP, HEADS, HD, HD), lambda i: (i, 0, 0, 0))],
        scratch_shapes=[pltpu.VMEM((HEADS, HD, HD), F32), pltpu.VMEM((rows, HW), F32)],
        compiler_params=_params(("arbitrary",)),
    )(z, z, z, z, lb_logits, gn)


def _pool_counts(tile_idx, tm):
    t = tile_idx * tm + lax.broadcasted_iota(jnp.int32, (tm, 1), 0)
    return [1.0 / jnp.minimum(t + 1, w).astype(F32) for w in POOL_WINDOWS]


def _pool_fwd(z, w_pool, scale, mixed_in, deps):
    s = z.shape[0]
    tm = min(ROW_TILE, s)

    def body(p_ref, w_ref, sc_ref, mixin_ref, ob_ref, pooled_ref, ext):
        i = pl.program_id(0)

        @pl.when(i == 0)
        def _():
            ext[0:POOL_HALO, :] = jnp.zeros((POOL_HALO, HW), F32)

        @pl.when(i > 0)
        def _():
            ext[0:POOL_HALO, :] = ext[tm:tm + POOL_HALO, :]

        ext[POOL_HALO:POOL_HALO + tm, :] = p_ref[...]
        inv = _pool_counts(i, tm)
        for g, w in enumerate(POOL_WINDOWS):
            sl = slice(HD * g, HD * (g + 1))
            p = ext[POOL_HALO:POOL_HALO + tm, sl]
            win = p
            for d in range(1, w):
                win = win + ext[POOL_HALO - d:POOL_HALO - d + tm, sl]
            pooled = (win * inv[g] - p).astype(BF16)
            pooled_ref[:, sl] = pooled
            ob_ref[:, sl] = (_mm(pooled, w_ref[g]) * sc_ref[:, sl]).astype(BF16)

    return _call_behind(
        deps, body, name="pool_fwd", grid=(s // tm,),
        out_shape=[jax.ShapeDtypeStruct((s, 2 * HW), BF16), jax.ShapeDtypeStruct((s, HW), BF16)],
        in_specs=[pl.BlockSpec((tm, HW), lambda i: (i, 4)), _full((HEADS, HD, HD)), _full((1, HW)), ANY_SPACE],
        out_specs=[pl.BlockSpec((tm, HW), lambda i: (i, 1)), pl.BlockSpec((tm, HW), lambda i: (i, 0))],
        scratch_shapes=[pltpu.VMEM((tm + POOL_HALO, HW), F32)],
        input_output_aliases={3: 0},
        compiler_params=_params(("arbitrary",)),
    )(z, w_pool, scale, mixed_in)


def _mem_kv(mem, g, wk, wv, deps):
    def body(m_ref, g_ref, wk_ref, wv_ref, hm_ref, k_ref, v_ref):
        m = m_ref[...]
        hm = (m * _rms(m) * g_ref[...]).astype(BF16)
        hm_ref[...] = hm
        k_ref[...] = _mm(hm, wk_ref[...]).astype(BF16)
        v_ref[...] = _mm(hm, wv_ref[...]).astype(BF16)

    shp = jax.ShapeDtypeStruct((MEM_LEN, D_MODEL), BF16)
    return _call_behind(
        deps, body, name="mem_kv", out_shape=[shp, shp, shp],
        in_specs=[VMEM_WHOLE] * 4, out_specs=[VMEM_WHOLE] * 3,
        compiler_params=_params(),
    )(mem, g, wk, wv)


def _softmax_rows(sc):
    e = jnp.exp(sc - jnp.max(sc, axis=-1, keepdims=True))
    return e / jnp.sum(e, axis=-1, keepdims=True)


def _mix_xattn_fwd(x0, mixed, w_out, g, wq, xk, xv, wo_t, deps):
    s = x0.shape[0]
    tm = min(ROW_TILE, s)
    scale = XHD ** -0.5

    def body(x_ref, mix_ref, wout_ref, g_ref, wq_ref, k_ref, v_ref, wo_ref, x1_ref, o_ref, hq_ref, q_ref, att_ref):
        xv_ = x_ref[...] + _mm(mix_ref[...], wout_ref[...])
        x1_ref[...] = xv_
        hq = (xv_ * _rms(xv_) * g_ref[...]).astype(BF16)
        hq_ref[...] = hq
        q_ref[...] = (_mm(hq, wq_ref[...]) * scale).astype(BF16)
        for h in range(HEADS):
            sl = slice(XHD * h, XHD * (h + 1))
            p = _softmax_rows(_mm_nt(q_ref[:, sl], k_ref[:, sl]))
            att_ref[:, sl] = _mm(p, v_ref[:, sl]).astype(BF16)
        o_ref[...] = xv_ + _mm_nt(att_ref[...], wo_ref[...])

    row_f32 = pl.BlockSpec((tm, D_MODEL), lambda i: (i, 0))
    bshape = jax.ShapeDtypeStruct((s, D_MODEL), BF16)
    fshape = jax.ShapeDtypeStruct((s, D_MODEL), F32)
    return _call_behind(
        deps, body, name="mix_xattn_fwd", grid=(s // tm,),
        out_shape=[fshape, fshape, bshape, bshape, bshape],
        in_specs=[row_f32, row_f32, VMEM_WHOLE, _full((1, D_MODEL)), VMEM_WHOLE, VMEM_WHOLE, VMEM_WHOLE, VMEM_WHOLE],
        out_specs=[row_f32] * 5,
        compiler_params=_params(("parallel",)),
    )(x0, mixed, w_out, g, wq, xk, xv, wo_t)


def _mlp_fwd_loss(x, g, w1, w2, gf, target):
    s = x.shape[0]
    tm = min(ROW_TILE, s)

    def body(x_ref, g_ref, w1_ref, w2_ref, gf_ref, t_ref, dx_ref, dx16_ref, u_ref, hf_ref, slot_ref):
        @pl.when(pl.program_id(0) == 0)
        def _():
            slot_ref[...] = jnp.zeros_like(slot_ref)

        xv = x_ref[...]
        hf = (xv * _rms(xv) * g_ref[...]).astype(BF16)
        hf_ref[...] = hf
        for j in range(N_DEV):
            a = jnp.maximum(_mm(hf, w1_ref[j]), 0.0)
            u_ref[:, FF_BLK * j:FF_BLK * (j + 1)] = (a * a).astype(BF16)
        acc = xv + _mm(u_ref[...], w2_ref[...])
        gfv = gf_ref[...]
        r = _rms(acc)
        n = acc * r
        err = n * gfv - t_ref[...]
        slot_ref[1:2, :] += jnp.sum(jnp.mean(err * err, axis=-1, keepdims=True), axis=0, keepdims=True) * 0.5
        dy = err * (1.0 / D_MODEL)
        slot_ref[0:1, :] += jnp.sum(dy * n, axis=0, keepdims=True)
        dn = dy * gfv
        dx = r * (dn - n * jnp.mean(dn * n, axis=-1, keepdims=True))
        dx_ref[...] = dx
        dx16_ref[...] = dx.astype(BF16)

    row_f32 = pl.BlockSpec((tm, D_MODEL), lambda i: (i, 0))
    return pl.pallas_call(
        body, name="mlp_fwd_loss", grid=(s // tm,),
        out_shape=[jax.ShapeDtypeStruct((s, D_MODEL), F32), jax.ShapeDtypeStruct((s, D_MODEL), BF16),
                   jax.ShapeDtypeStruct((s, D_FF), BF16), jax.ShapeDtypeStruct((s, D_MODEL), BF16),
                   jax.ShapeDtypeStruct((SLOT, D_MODEL), F32)],
        in_specs=[row_f32, _full((1, D_MODEL)), VMEM_WHOLE, VMEM_WHOLE, _full((1, D_MODEL)), row_f32],
        out_specs=[row_f32, row_f32, pl.BlockSpec((tm, D_FF), lambda i: (i, 0)), row_f32, _full((SLOT, D_MODEL))],
        compiler_params=_params(("arbitrary",)),
    )(x, g, w1, w2, gf, target)


def _zero_slot(slot_ref):
    @pl.when(pl.program_id(0) == 0)
    def _():
        slot_ref[...] = jnp.zeros_like(slot_ref)


def _mlp_bwd(dx3, u, x2, g, w1, w2, deps):
    s = x2.shape[0]
    tm = min(ROW_TILE, s)

    def body(d_ref, u_ref, x_ref, g_ref, w1_ref, w2_ref, da_ref, dx_ref, slot_ref):
        _zero_slot(slot_ref)
        d = d_ref[...]
        d16 = d.astype(BF16)
        dhf = jnp.zeros((tm, D_MODEL), F32)
        for j in range(N_DEV):
            sl = slice(FF_BLK * j, FF_BLK * (j + 1))
            u = u_ref[:, sl].astype(F32)
            da = (_mm_nt(d16, w2_ref[j]) * (2.0 * u * lax.rsqrt(jnp.maximum(u, TINY)))).astype(BF16)
            da_ref[:, sl] = da
            dhf = dhf + _mm_nt(da, w1_ref[j])
        dx, dg = _rms_bwd(x_ref[...], g_ref[...], dhf)
        dx_ref[...] = d + dx
        slot_ref[0:1, :] += dg

    row_f32 = pl.BlockSpec((tm, D_MODEL), lambda i: (i, 0))
    return _call_behind(
        deps, body, name="mlp_bwd", grid=(s // tm,),
        out_shape=[jax.ShapeDtypeStruct((s, D_FF), BF16), jax.ShapeDtypeStruct((s, D_MODEL), F32),
                   jax.ShapeDtypeStruct((SLOT, D_MODEL), F32)],
        in_specs=[row_f32, pl.BlockSpec((tm, D_FF), lambda i: (i, 0)), row_f32, _full((1, D_MODEL)),
                  VMEM_WHOLE, VMEM_WHOLE],
        out_specs=[pl.BlockSpec((tm, D_FF), lambda i: (i, 0)), row_f32, _full((SLOT, D_MODEL))],
        compiler_params=_params(("arbitrary",)),
    )(dx3, u, x2, g, w1, w2)


def _wgrad(a, b, name, col_blocks=False):
    s, m = a.shape
    n = b.shape[1]
    tm = 1280 if m % 1280 == 0 else min(1024, m)
    tn = min(1024, n)
    blk = n // N_DEV
    per_step = tn // blk if col_blocks else 1
    ts = min(2 * ROW_TILE, s)
    n_s = s // ts

    def body(a_ref, b_ref, o_ref, acc):
        k = pl.program_id(2)

        @pl.when(k == 0)
        def _():
            acc[...] = jnp.zeros_like(acc)

        acc[...] += _mm_tn(a_ref[...], b_ref[...])

        @pl.when(k == n_s - 1)
        def _():
            if col_blocks:
                for p in range(per_step):
                    o_ref[p] = acc[:, blk * p:blk * (p + 1)].astype(BF16)
            else:
                o_ref[...] = acc[...].astype(BF16)

    if col_blocks:
        out_shape = jax.ShapeDtypeStruct((N_DEV, m, blk), BF16)
        out_spec = pl.BlockSpec((per_step, tm, blk), lambda i, j, k: (j, i, 0))
    else:
        out_shape = jax.ShapeDtypeStruct((m, n), BF16)
        out_spec = pl.BlockSpec((tm, tn), lambda i, j, k: (i, j))
    return pl.pallas_call(
        body, name=name, grid=(m // tm, n // tn, n_s), out_shape=out_shape,
        in_specs=[pl.BlockSpec((ts, tm), lambda i, j, k: (k, i)), pl.BlockSpec((ts, tn), lambda i, j, k: (k, j))],
        out_specs=out_spec,
        scratch_shapes=[pltpu.VMEM((tm, tn), F32)],
        compiler_params=_params(("parallel", "parallel", "arbitrary")),
    )(a, b)


def _xattn_bwd(dx2, x1, g, q, xk, xv, wq, wo_t, deps):
    s = x1.shape[0]
    tm = min(ROW_TILE, s)
    scale = XHD ** -0.5

    def body(d_ref, x_ref, g_ref, q_ref, k_ref, v_ref, wq_ref, wo_ref, dx_ref, dx16_ref, dq_ref, dk_ref, dv_ref, slot_ref,
             datt):
        _zero_slot(slot_ref)

        @pl.when(pl.program_id(0) == 0)
        def _():
            dk_ref[...] = jnp.zeros_like(dk_ref)
            dv_ref[...] = jnp.zeros_like(dv_ref)

        d = d_ref[...]
        datt[...] = _mm(d, wo_ref[...]).astype(BF16)
        for h in range(HEADS):
            sl = slice(XHD * h, XHD * (h + 1))
            qh, kh, vh, dah = q_ref[:, sl], k_ref[:, sl], v_ref[:, sl], datt[:, sl]
            p = _softmax_rows(_mm_nt(qh, kh))
            dp = _mm_nt(dah, vh)
            ds = (p * (dp - jnp.sum(dp * p, axis=-1, keepdims=True))).astype(BF16)
            dq_ref[:, sl] = (_mm(ds, kh) * scale).astype(BF16)
            dk_ref[:, sl] += _mm_tn(ds, qh)
            dv_ref[:, sl] += _mm_tn(p, dah)
        dx, dg = _rms_bwd(x_ref[...], g_ref[...], _mm_nt(dq_ref[...], wq_ref[...]))
        dx_ref[...] = d + dx
        dx16_ref[...] = (d + dx).astype(BF16)
        slot_ref[0:1, :] += dg

    row_f32 = pl.BlockSpec((tm, D_MODEL), lambda i: (i, 0))
    kv = jax.ShapeDtypeStruct((MEM_LEN, D_MODEL), F32)
    tokens16 = jax.ShapeDtypeStruct((s, D_MODEL), BF16)
    return _call_behind(
        deps, body, name="xattn_bwd", grid=(s // tm,),
        out_shape=[jax.ShapeDtypeStruct((s, D_MODEL), F32), tokens16, tokens16, kv, kv,
                   jax.ShapeDtypeStruct((SLOT, D_MODEL), F32)],
        in_specs=[row_f32, row_f32, _full((1, D_MODEL)), row_f32, VMEM_WHOLE, VMEM_WHOLE, VMEM_WHOLE, VMEM_WHOLE],
        out_specs=[row_f32, row_f32, row_f32, _full((MEM_LEN, D_MODEL)), _full((MEM_LEN, D_MODEL)),
                   _full((SLOT, D_MODEL))],
        scratch_shapes=[pltpu.VMEM((tm, D_MODEL), BF16)],
        compiler_params=_params(("arbitrary",)),
    )(dx2, x1, g, q, xk, xv, wq, wo_t)


def _mem_bwd(mem, g, hm, dxk, dxv, wk, wv):
    def body(m_ref, g_ref, hm_ref, dk_ref, dv_ref, wk_ref, wv_ref, dwk_ref, dwv_ref, slot_ref):
        dk, dv = dk_ref[...], dv_ref[...]
        hm_ = hm_ref[...]
        dwk_ref[...] = _mm_tn(hm_, dk).astype(BF16)
        dwv_ref[...] = _mm_tn(hm_, dv).astype(BF16)
        _, dg = _rms_bwd(m_ref[...], g_ref[...], _mm_nt(dk, wk_ref[...]) + _mm_nt(dv, wv_ref[...]))
        slot_ref[...] = jnp.zeros_like(slot_ref)
        slot_ref[0:1, :] = dg

    wshape = jax.ShapeDtypeStruct((D_MODEL, D_MODEL), BF16)
    return pl.pallas_call(
        body, name="mem_bwd", out_shape=[wshape, wshape, jax.ShapeDtypeStruct((SLOT, D_MODEL), F32)],
        in_specs=[VMEM_WHOLE] * 7, out_specs=[VMEM_WHOLE] * 3,
        compiler_params=_params(),
    )(mem, g, hm, dxk, dxv, wk, wv)


def _pool_bwd(dx1, w_out, pooled, w_pool, scale, deps):
    s = dx1.shape[0]
    tm = min(ROW_TILE, s)
    n_t = s // tm

    def body(dx_ref, wo_ref, pl_ref, w_ref, sc_ref, dz_ref, dw_ref, slot_ref, ext, do_ref):
        i = pl.program_id(0)
        tile = n_t - 1 - i
        _zero_slot(slot_ref)
        do_ref[...] = _mm_nt(dx_ref[...], wo_ref[HW:2 * HW, :])

        @pl.when(i == 0)
        def _():
            dw_ref[...] = jnp.zeros_like(dw_ref)
            ext[tm:tm + POOL_HALO, :] = jnp.zeros((POOL_HALO, HW), F32)

        @pl.when(i > 0)
        def _():
            ext[tm:tm + POOL_HALO, :] = ext[0:POOL_HALO, :]

        inv = _pool_counts(tile, tm)
        dpooled = []
        for g in range(HEADS):
            sl = slice(HD * g, HD * (g + 1))
            pooled_g = pl_ref[:, sl]
            do = do_ref[:, sl]
            slot_ref[0:1, sl] += jnp.sum(_mm(pooled_g, w_ref[g]) * do, axis=0, keepdims=True)
            dy = (do * sc_ref[:, sl]).astype(BF16)
            dw_ref[g] += _mm_tn(pooled_g, dy)
            dpo = _mm_nt(dy, w_ref[g])
            dpooled.append(dpo)
            ext[0:tm, sl] = dpo * inv[g]
        for g, w in enumerate(POOL_WINDOWS):
            sl = slice(HD * g, HD * (g + 1))
            win = ext[0:tm, sl]
            for d in range(1, w):
                win = win + ext[d:d + tm, sl]
            dz_ref[:, sl] = (win - dpooled[g]).astype(BF16)

    return _call_behind(
        deps, body, name="pool_bwd", grid=(n_t,),
        out_shape=[jax.ShapeDtypeStruct((s, IN_WIDTH), BF16), jax.ShapeDtypeStruct((HEADS, HD, HD), F32),
                   jax.ShapeDtypeStruct((SLOT, D_MODEL), F32)],
        in_specs=[pl.BlockSpec((tm, D_MODEL), lambda i: (n_t - 1 - i, 0)), VMEM_WHOLE,
                  pl.BlockSpec((tm, HW), lambda i: (n_t - 1 - i, 0)), _full((HEADS, HD, HD)), _full((1, HW))],
        out_specs=[pl.BlockSpec((tm, HW), lambda i: (n_t - 1 - i, 4)), _full((HEADS, HD, HD)), _full((SLOT, D_MODEL))],
        scratch_shapes=[pltpu.VMEM((tm + POOL_HALO, HW), F32), pltpu.VMEM((tm, HW), F32)],
        compiler_params=_params(("arbitrary",)),
    )(dx1, w_out, pooled, w_pool, scale)


def _hgrn_bwd(z, o, dx1, w_out, states, lb_logits, gn, dz_in, deps):
    s = z.shape[0]
    n_chunks = s // CHUNK

    def body(zq_ref, zf_ref, zi_ref, zg_ref, o_ref, dx_ref, wo_ref, st_ref, lbl_ref, gn_ref, dzin_ref,
             dz_ref, dlb_ref, dgn_ref, dstate, b_scr, dlb_acc, do_ref):
        i = pl.program_id(0)

        @pl.when(i == 0)
        def _():
            dstate[...] = jnp.zeros_like(dstate)
            dlb_acc[...] = jnp.zeros_like(dlb_acc)
            dgn_ref[...] = jnp.zeros_like(dgn_ref)
            dlb_ref[...] = jnp.zeros_like(dlb_ref)

        do_ref[...] = _mm_nt(dx_ref[...], wo_ref[0:HW, :])
        lb = _sigmoid(lbl_ref[0:1, :] - lbl_ref[1:2, :])
        row, col = _chunk_masks()
        causal = col <= row
        tri = _ones_where(causal)
        upper = _ones_where(col >= row)
        strict_lower = _ones_where(col < row)
        in_sub, in_sub_head = _sub_chunk_masks(HW), _sub_chunk_masks(HD)
        gn_row = _lanes([gn_ref[h:h + 1, :] for h in range(HEADS)])
        dlb_sum, dgn_sum = 0.0, 0.0
        for c in reversed(range(CHUNKS_PER_STEP)):
            r0 = CHUNK * c
            rs = slice(r0, r0 + CHUNK)
            zq = zq_ref[rs, :]
            q, sq, sig, f = _hgrn_gates(zq, zf_ref[rs, :], lb)
            kk = 1.0 - f
            b = _tri_dot(tri, jnp.log(f), 3)
            b_scr[rs, :] = b
            v = zi_ref[rs, :]
            o, zg, doa = o_ref[rs, :], zg_ref[rs, :], do_ref[rs, :]
            sg = _sigmoid(zg)
            n = o * _head_rms(o)
            don = doa * (zg * sg)
            dgn_sum = dgn_sum + jnp.sum(don * n, axis=0, keepdims=True)
            dn = don * gn_row
            d_o = _head_rms(o) * (dn - n * _head_mean(dn * n))
            dz_ref[rs, 3 * HW:4 * HW] = (doa * (n * gn_row) * (sg * (1.0 + zg * (1.0 - sg)))).astype(BF16)
            eq, ek = _hgrn_decay_factors(b_scr, r0, b, in_sub)
            b_last = b_scr[r0 + CHUNK - 1:r0 + CHUNK, :]
            lam, e_last, lam_last = jnp.exp(b), jnp.exp(b_last - b), jnp.exp(b_last)
            qe, qg, kd = q * eq, q * lam, kk * e_last
            ke = [kk * e for e in ek]
            dv_h, gq_h, gk_h, dqi_h, dkd_h, st_h = [], [], [], [], [], []
            for h in range(HEADS):
                vh, doh = _head(v, h), _head(d_o, h)
                st0, ds1 = st_ref[c, h], dstate[h]
                q16 = _per_sub_chunk(_head(qe, h), in_sub_head).astype(BF16)
                ke16 = _lanes([_head(ke[j], h) for j in range(N_SUB)]).astype(BF16)
                a = jnp.where(causal, _mm_nt(q16, ke16), 0.0)
                da = jnp.where(causal, _mm_nt(doh, vh), 0.0)
                dv_h.append(_mm_tn(a, doh) + _mm_nt(_head(kd, h), ds1))
                gq_h.append(_own_lane_block(_mm(da, ke16), in_sub_head))
                gk_h.append(_mm_tn(da, q16))
                dqi_h.append(_mm(doh, st0))
                dkd_h.append(_mm(vh, ds1))
                st_h.append(jnp.sum(st0 * ds1, axis=0, keepdims=True))
                dstate[h] = ds1 * _head(lam_last, h) + _mm_tn(doh, _head(qg, h))
            dz_ref[rs, 2 * HW:3 * HW] = _lanes(dv_h).astype(BF16)
            gq = _lanes(gq_h)
            gk = [_lanes([gk_h[h][:, HD * j:HD * (j + 1)] for h in range(HEADS)]) for j in range(N_SUB)]
            dq_inter = lam * _lanes(dqi_h)
            dq = eq * gq + dq_inter
            dk_intra = sum(ek[j] * gk[j] for j in range(N_SUB))
            dk_state = _lanes(dkd_h) * e_last
            db_intra = (qe.astype(BF16).astype(F32) * gq
                        - sum(ke[j].astype(BF16).astype(F32) * gk[j] for j in range(N_SUB)))
            dlf = (_tri_dot(upper, db_intra + q * dq_inter, 2) + _tri_dot(strict_lower, kk * dk_state, 2)
                   + lam_last * _lanes(st_h))
            df = dlf / f - (dk_intra + dk_state)
            dlb_sum = dlb_sum + jnp.sum(df * (1.0 - sig), axis=0, keepdims=True)
            dz_ref[rs, HW:2 * HW] = (df * (1.0 - lb) * sig * (1.0 - sig)).astype(BF16)
            dz_ref[rs, 0:HW] = (dq * (sq * (1.0 + zq * (1.0 - sq)))).astype(BF16)
        dlb_acc[...] += dlb_sum
        for h in range(HEADS):
            dgn_ref[h:h + 1, 0:HD] += _head(dgn_sum, h)

        @pl.when(i == n_steps - 1)
        def _():
            dl0 = dlb_acc[...] * lb * (1.0 - lb)
            dlb_ref[0:1, 0:HW] = dl0
            dlb_ref[1:2, 0:HW] = -dl0

    rows = CHUNK * CHUNKS_PER_STEP
    n_steps = s // rows
    rev = lambda i: n_steps - 1 - i
    zspec = lambda cb: pl.BlockSpec((rows, HW), lambda i, cb=cb: (rev(i), cb))
    slot = jax.ShapeDtypeStruct((SLOT, D_MODEL), F32)
    return _call_behind(
        deps, body, name="hgrn_bwd", grid=(n_steps,),
        out_shape=[jax.ShapeDtypeStruct((s, IN_WIDTH), BF16), slot, slot],
        in_specs=[zspec(0), zspec(1), zspec(2), zspec(3), pl.BlockSpec((rows, HW), lambda i: (rev(i), 0)),
                  pl.BlockSpec((rows, D_MODEL), lambda i: (rev(i), 0)), VMEM_WHOLE,
                  pl.BlockSpec((CHUNKS_PER_STEP, HEADS, HD, HD), lambda i: (rev(i), 0, 0, 0)), _full((2, HW)),
                  _full((HEADS, HD)), ANY_SPACE],
        out_specs=[pl.BlockSpec((rows, 4 * HW), lambda i: (rev(i), 0)), _full((SLOT, D_MODEL)), _full((SLOT, D_MODEL))],
        scratch_shapes=[pltpu.VMEM((HEADS, HD, HD), F32), pltpu.VMEM((rows, HW), F32), pltpu.VMEM((1, HW), F32),
                        pltpu.VMEM((rows, HW), F32)],
        input_output_aliases={10: 0},
        compiler_params=_params(("arbitrary",)),
    )(z, z, z, z, o, dx1, w_out, states, lb_logits, gn, dz_in)


def _in_bwd(dz, w_t, x0, g, dx1, deps):
    s = x0.shape[0]
    tm = min(WIDE_ROW_TILE, s)

    def body(dz_ref, w_ref, x_ref, g_ref, d_ref, dx_ref, slot_ref):
        _zero_slot(slot_ref)
        dx, dg = _rms_bwd(x_ref[...], g_ref[...], _mm(dz_ref[...], w_ref[...]))
        dx_ref[...] = d_ref[...] + dx
        slot_ref[0:1, :] += dg

    row_f32 = pl.BlockSpec((tm, D_MODEL), lambda i: (i, 0))
    return _call_behind(
        deps, body, name="in_bwd", grid=(s // tm,),
        out_shape=[jax.ShapeDtypeStruct((s, D_MODEL), F32), jax.ShapeDtypeStruct((SLOT, D_MODEL), F32)],
        in_specs=[pl.BlockSpec((tm, IN_WIDTH), lambda i: (i, 0)), VMEM_WHOLE, row_f32, _full((1, D_MODEL)), row_f32],
        out_specs=[row_f32, _full((SLOT, D_MODEL))],
        compiler_params=_params(("arbitrary",)),
    )(dz, w_t, x0, g, dx1)


def kernel(x, mem, norm_mix_g, w_in, lb_logits, hgrn_norm_g, w_pool, pool_scale, w_out, norm_x_g, norm_mem_g, w_xq, w_xk, w_xv, w_xo, norm_ffn_g, w_ff1, w_ff2, final_norm_g, loss_target, m_norm_mix_g, m_w_in, m_lb_logits, m_hgrn_norm_g, m_w_pool, m_pool_scale, m_w_out, m_norm_x_g, m_norm_mem_g, m_w_xq, m_w_xk, m_w_xv, m_w_xo, m_norm_ffn_g, m_w_ff1, m_w_ff2, m_final_norm_g, v_norm_mix_g, v_w_in, v_lb_logits, v_hgrn_norm_g, v_w_pool, v_pool_scale, v_w_out, v_norm_x_g, v_norm_mem_g, v_w_xq, v_w_xk, v_w_xv, v_w_xo, v_norm_ffn_g, v_w_ff1, v_w_ff2, v_final_norm_g):
    x0 = x[0]
    mem0 = mem[0]
    tgt = loss_target[0]
    gn = hgrn_norm_g[0]
    gfin = final_norm_g.reshape(1, D_MODEL)
    wp = w_pool[0]
    heads_2d = lambda w: w.reshape(D_MODEL // N_DEV, D_MODEL)
    xo_2d = lambda w: w.reshape(D_MODEL, D_MODEL // N_DEV)

    first = _all_gather_weights([w_in[0].T], [w_out[0], heads_2d(w_xq), heads_2d(w_xk), heads_2d(w_xv), xo_2d(w_xo).T,
                                              w_ff1[0], w_ff2[0]])
    win_t = first[0].reshape(IN_WIDTH, D_MODEL)
    ga_attn, ga_mlp = _gather_first_start([first[1:6], first[6:8]], "gather_first_start")

    z, h = _in_proj(x0, norm_mix_g, win_t, deps=[ga_attn[3]])
    mixed_a, o_pre, states = _hgrn_fwd(z, lb_logits, gn)
    lands = _split_wait(_gather_first_copies, ga_attn, o_pre, "gather_attn_first_wait")
    gb_attn = _gather_forward_start(lands, "gather_attn_forward_start")
    mixed, pooled = _pool_fwd(z, wp, pool_scale, mixed_a, deps=[gb_attn[3]])
    lands = _split_wait(_gather_forward_copies, gb_attn, pooled, "gather_attn_forward_wait")
    wout_f, wq_f, wk_f, wv_f, wo_t = (t.reshape(D_MODEL, D_MODEL) for t in lands)
    hm, xk, xv = _mem_kv(mem0, norm_mem_g, wk_f, wv_f, deps=[])
    x1, x2, hq, xq, att = _mix_xattn_fwd(x0, mixed, wout_f, norm_x_g, wq_f, xk, xv, wo_t, deps=[])
    lands = _split_wait(_gather_first_copies, ga_mlp, x2, "gather_mlp_first_wait")
    gb_mlp = _gather_forward_start(lands, "gather_mlp_forward_start")
    w1_b, w2_b = _split_wait(_gather_forward_copies, gb_mlp, gb_mlp[3], "gather_mlp_forward_wait")
    dx3, dx3_16, u, hf, slot_fin = _mlp_fwd_loss(x2, norm_ffn_g, w1_b, w2_b.reshape(D_FF, D_MODEL), gfin, tgt)

    rows = lambda t, r: t.reshape(N_DEV, r, D_MODEL)
    dw2 = _wgrad(u, dx3_16, "wgrad_ff2")
    ex_ff2 = _all_to_all_start([rows(dw2, FF_BLK)], [], "exchange_ff2_start")
    da, dx2, slot_ffn = _mlp_bwd(dx3, u, x2, norm_ffn_g, w1_b, w2_b, deps=[ex_ff2[3]])
    dw1 = _wgrad(hf, da, "wgrad_ff1", col_blocks=True)
    ex_ff1 = _all_to_all_start([dw1], [], "exchange_ff1_start")
    dx1, dx1_16, dxq, dxk, dxv, slot_x = _xattn_bwd(dx2, x1, norm_x_g, xq, xk, xv, wq_f, wo_t, deps=[ex_ff1[3]])
    dwo_t = _wgrad(dx2, att, "wgrad_xo")
    dwq = _wgrad(hq, dxq, "wgrad_xq")
    dwk, dwv, slot_mem = _mem_bwd(mem0, norm_mem_g, hm, dxk, dxv, wk_f, wv_f)
    ex_attn = _all_to_all_start([rows(dwq, 128), rows(dwk, 128), rows(dwv, 128), rows(dwo_t, 128)], [],
                                "exchange_attn_start")
    dwout = _wgrad(mixed, dx1_16, "wgrad_out")
    dz_pool, d_wpool, slot_ps = _pool_bwd(dx1_16, wout_f, pooled, wp, pool_scale, deps=[ex_attn[3]])
    small0 = jnp.concatenate([slot_x, slot_mem, slot_ffn, slot_fin, slot_ps], axis=0)
    ex_out = _all_to_all_start([rows(dwout, 128)], [small0, d_wpool], "exchange_out_start")
    dz, slot_lb, slot_gn = _hgrn_bwd(z, o_pre, dx1_16, wout_f, states, lb_logits, gn, dz_pool, deps=[ex_out[3]])
    dwin_t = _wgrad(dz, h, "wgrad_in")
    small1 = jnp.concatenate([slot_lb, slot_gn], axis=0)
    ex_in = _all_to_all_start([rows(dwin_t, 320)], [small1], "exchange_in_start")
    grad_x, slot_mix = _in_bwd(dz, win_t, x0, norm_mix_g, dx1, deps=[ex_in[3]])
    ex_mix = _all_to_all_start([], [slot_mix], "exchange_mix_start")

    out = {}
    (r_2,) = _split_wait(_all_to_all_copies(1), ex_ff2, ex_mix[3], "exchange_ff2_wait")
    out["w_ff2"] = _sum_adamw(r_2, w_ff2[0], m_w_ff2[0], v_w_ff2[0], "adamw_ff2")
    (r_1,) = _split_wait(_all_to_all_copies(1), ex_ff1, out["w_ff2"][1], "exchange_ff1_wait")
    out["w_ff1"] = _sum_adamw(r_1, w_ff1[0], m_w_ff1[0], v_w_ff1[0], "adamw_ff1")
    r_q, r_k, r_v, r_o = _split_wait(_all_to_all_copies(4), ex_attn, out["w_ff1"][1], "exchange_attn_wait")
    sums = _sum_sources_whole([r_q, r_k, r_v, r_o], "sum_grad_attn")
    g_attn = [g.reshape(w_xq.shape) for g in sums[:3]] + [sums[3].T]
    attn = _adamw_whole([(g_attn[0], w_xq, m_w_xq, v_w_xq), (g_attn[1], w_xk, m_w_xk, v_w_xk),
                         (g_attn[2], w_xv, m_w_xv, v_w_xv),
                         (g_attn[3], xo_2d(w_xo), xo_2d(m_w_xo), xo_2d(v_w_xo))], "adamw_attn")
    for n, g, res in zip(("w_xq", "w_xk", "w_xv", "w_xo"), g_attn, attn):
        out[n] = (g, *res)
    r_out, r_small0, r_wpool = _split_wait(_all_to_all_copies(1), ex_out, attn[3][0], "exchange_out_wait")
    out["w_out"] = _sum_adamw(r_out, w_out[0], m_w_out[0], v_w_out[0], "adamw_out")
    r_in, r_small1 = _split_wait(_all_to_all_copies(1), ex_in, out["w_out"][1], "exchange_in_wait")
    in_t = _sum_adamw(r_in, w_in[0].T, m_w_in[0].T, v_w_in[0].T, "adamw_in")
    out["w_in"] = tuple(t.T for t in in_t)
    (r_small2,) = _split_wait(_all_to_all_copies(0), ex_mix, in_t[1], "exchange_mix_wait")
    row = lambda t: t.reshape(1, -1)
    small_params = {
        "norm_mix_g": (norm_mix_g, m_norm_mix_g, v_norm_mix_g),
        "lb_logits": (lb_logits, m_lb_logits, v_lb_logits),
        "hgrn_norm_g": (hgrn_norm_g[0], m_hgrn_norm_g[0], v_hgrn_norm_g[0]),
        "pool_scale": (pool_scale, m_pool_scale, v_pool_scale),
        "norm_x_g": (norm_x_g, m_norm_x_g, v_norm_x_g),
        "norm_mem_g": (norm_mem_g, m_norm_mem_g, v_norm_mem_g),
        "norm_ffn_g": (norm_ffn_g, m_norm_ffn_g, v_norm_ffn_g),
        "final_norm_g": (row(final_norm_g), row(m_final_norm_g), row(v_final_norm_g)),
        "w_pool": (wp, m_w_pool[0], v_w_pool[0]),
    }
    loss, small_out = _small_update([r_small0, r_small1, r_small2], r_wpool, small_params)
    out.update(small_out)

    shapes = dict(norm_mix_g=norm_mix_g, w_in=w_in, lb_logits=lb_logits, hgrn_norm_g=hgrn_norm_g, w_pool=w_pool,
                  pool_scale=pool_scale, w_out=w_out, norm_x_g=norm_x_g, norm_mem_g=norm_mem_g, w_xq=w_xq, w_xk=w_xk,
                  w_xv=w_xv, w_xo=w_xo, norm_ffn_g=norm_ffn_g, w_ff1=w_ff1, w_ff2=w_ff2, final_norm_g=final_norm_g)
    order = list(shapes)
    group = lambda k: [out[n][k].reshape(shapes[n].shape) for n in order]
    return (loss.reshape(()), grad_x.reshape(x.shape), *group(0), *group(1), *group(2), *group(3))
```

```python
import jax
import jax.numpy as jnp
from jax import lax
from jax.experimental import pallas as pl
from jax.experimental.pallas import tpu as pltpu

F32 = jnp.float32
BF16 = jnp.bfloat16

D_MODEL = 1024
N_DEV = 8
HEADS = 4
HD = 128
HW = HEADS * HD
IN_WIDTH = 5 * HW
XHD = 256
MEM_LEN = 256
D_FF = 4096
FF_BLK = D_FF // N_DEV
POOL_WINDOWS = (2, 4, 8, 16)
POOL_HALO = 16
CHUNK = 64
CHUNKS_PER_STEP = 8
SUB = 16
N_SUB = CHUNK // SUB
EXP_CAP = 80.0
EPS = 1e-6
TINY = 1e-30
ROW_TILE = 512
WIDE_ROW_TILE = 1024
SLOT = 8
V7X_VMEM_LIMIT = 56 * 1024 * 1024

ADAM_LR = 0.001
ADAM_B1 = 0.9
ADAM_B2 = 0.999
ADAM_EPS = 1e-08
ADAM_WD = 0.01
ADAM_STEP = 10

MESH_ID = pl.DeviceIdType.MESH


def _params(sem=None, vmem=V7X_VMEM_LIMIT):
    return pltpu.CompilerParams(dimension_semantics=sem, vmem_limit_bytes=vmem)


def _mm(a, b):
    return lax.dot_general(a.astype(BF16), b.astype(BF16), (((1,), (0,)), ((), ())), preferred_element_type=F32)


def _mm_nt(a, b):
    return lax.dot_general(a.astype(BF16), b.astype(BF16), (((1,), (1,)), ((), ())), preferred_element_type=F32)


def _mm_tn(a, b):
    return lax.dot_general(a.astype(BF16), b.astype(BF16), (((0,), (0,)), ((), ())), preferred_element_type=F32)


def _sigmoid(x):
    return 1.0 / (1.0 + jnp.exp(-x))


def _rms(x):
    return lax.rsqrt(jnp.mean(x * x, axis=-1, keepdims=True) + EPS)


def _rms_bwd(x, g, dh):
    r = _rms(x)
    n = x * r
    dn = dh * g
    dx = r * (dn - n * jnp.mean(dn * n, axis=-1, keepdims=True))
    return dx, jnp.sum(dh * n, axis=0, keepdims=True)


def _tri_dot(tri, x, passes):
    acc = None
    rest = x
    for _ in range(passes):
        piece = rest.astype(BF16)
        part = lax.dot_general(tri, piece, (((1,), (0,)), ((), ())), preferred_element_type=F32)
        acc = part if acc is None else acc + part
        rest = rest - piece.astype(F32)
    return acc


def _adam_update(g, w, m, v):
    nm = ADAM_B1 * m + (1.0 - ADAM_B1) * g
    nv = ADAM_B2 * v + (1.0 - ADAM_B2) * (g * g)
    m_hat = nm / (1.0 - ADAM_B1 ** ADAM_STEP)
    v_hat = nv / (1.0 - ADAM_B2 ** ADAM_STEP)
    return -ADAM_LR * (m_hat / (jnp.sqrt(v_hat) + ADAM_EPS) + ADAM_WD * w), nm, nv


def _full(shape):
    return pl.BlockSpec(shape, lambda *_: (0,) * len(shape))


VMEM_WHOLE = pl.BlockSpec(memory_space=pltpu.VMEM)
ANY_SPACE = pl.BlockSpec(memory_space=pl.ANY)


def _mesh_pos():
    return lax.axis_index("x"), lax.axis_index("y"), lax.axis_index("c")


def _flat(px, py, pc):
    return 4 * px + 2 * py + pc


def _all_gather_weights(shards, cast_only):
    n, nc = len(shards), len(cast_only)
    step = 64

    def body(*refs):
        x_refs, c_refs = refs[:n], refs[n:n + nc]
        out_refs, cast_refs = refs[n + nc:2 * n + nc], refs[2 * n + nc:2 * n + 2 * nc]
        bufs = refs[2 * n + 2 * nc:3 * n + 2 * nc]
        send_sems, recv_sems, local_sems = refs[3 * n + 2 * nc:]
        x, y, c = _mesh_pos()
        me, sibling = (x, y, c), (x, y, 1 - c)
        chips = [(1 - x, y), (x, 1 - y), (1 - x, 1 - y)]

        def copy(a, k, blk, to, src=None):
            rows = out_refs[a].at[_flat(*blk)]
            return pltpu.make_async_remote_copy(
                src_ref=rows if src is None else src, dst_ref=rows,
                send_sem=send_sems.at[7 * a + k], recv_sem=recv_sems.at[7 * a + k], device_id=to, device_id_type=MESH_ID)

        def cast_rows(src, dst, rows):
            def cast(i, carry):
                r0 = pl.multiple_of(i * step, step)
                dst[pl.ds(r0, step), :] = src[pl.ds(r0, step), :].astype(BF16)
                return carry
            lax.fori_loop(0, rows // step, cast, 0)

        first, mine = [], []
        for a in range(n):
            cast_rows(x_refs[a], bufs[a], shards[a].shape[0])
            mine.append(pltpu.make_async_copy(bufs[a], out_refs[a].at[_flat(*me)], local_sems.at[a]))
            first.append(copy(a, 0, me, sibling, src=bufs[a]))
            first += [copy(a, 1 + j, me, (*chip, c), src=bufs[a]) for j, chip in enumerate(chips)]
            for cp in [mine[-1]] + first[-4:]:
                cp.start()
        for a in range(nc):
            cast_rows(c_refs[a], cast_refs[a], cast_only[a].shape[0])
        passed = []
        for j, chip in enumerate(chips):
            for a in range(n):
                copy(a, 1 + j, (*chip, c), me).wait_recv()
                passed.append(copy(a, 4 + j, (*chip, c), sibling))
                passed[-1].start()
        for a in range(n):
            copy(a, 0, sibling, me).wait_recv()
            for j, chip in enumerate(chips):
                copy(a, 4 + j, (*chip, 1 - c), me).wait_recv()
        for cp in first + passed:
            cp.wait_send()
        for cp in mine:
            cp.wait()

    return pl.pallas_call(
        body, name="all_gather_w_in",
        out_shape=[jax.ShapeDtypeStruct((N_DEV,) + s.shape, BF16) for s in shards]
        + [jax.ShapeDtypeStruct(s.shape, BF16) for s in cast_only],
        in_specs=[VMEM_WHOLE] * (n + nc), out_specs=[ANY_SPACE] * n + [VMEM_WHOLE] * nc,
        scratch_shapes=[pltpu.VMEM(s.shape, BF16) for s in shards]
        + [pltpu.SemaphoreType.DMA((7 * n,)), pltpu.SemaphoreType.DMA((7 * n,)), pltpu.SemaphoreType.DMA((n,))],
        compiler_params=_params(),
    )(*shards, *cast_only)


HBM_SPEC = pl.BlockSpec(memory_space=pltpu.HBM)
SEM_SPEC = pl.BlockSpec(memory_space=pltpu.SEMAPHORE)
EFFECT = pltpu.SideEffectType.DATAFLOW_SIDE_EFFECTING
TOKEN = jax.ShapeDtypeStruct((8, 128), F32)


def _in_hbm(a):
    return pltpu.with_memory_space_constraint(a, pltpu.HBM)


def _split_start(copies_of, srcs, lands, n_sems, name):
    ns, nl, k = len(srcs), len(lands), len(n_sems)

    def body(*refs):
        src_refs, land_refs = refs[:ns], refs[ns:ns + nl]
        sems = refs[ns + nl:ns + nl + k]
        token = refs[-1]
        for cp in copies_of(src_refs, land_refs, sems):
            cp.start()
        token[...] = jnp.zeros_like(token)

    outs = pl.pallas_call(
        body, name=name,
        out_shape=[pltpu.SemaphoreType.DMA((q,)) for q in n_sems]
        + [pltpu.HBM(a.shape, a.dtype) for a in list(srcs) + list(lands)] + [TOKEN],
        in_specs=[HBM_SPEC] * (ns + nl),
        out_specs=[SEM_SPEC] * k + [HBM_SPEC] * (ns + nl) + [VMEM_WHOLE],
        input_output_aliases={i: k + i for i in range(ns + nl)},
        compiler_params=pltpu.CompilerParams(has_side_effects=EFFECT),
    )(*[_in_hbm(a) for a in list(srcs) + list(lands)])
    return outs[:k], outs[k:k + ns], outs[k + ns:k + ns + nl], outs[-1]


def _split_wait(copies_of, handle, after, name):
    sems, srcs, lands, _ = handle
    ns, nl, k = len(srcs), len(lands), len(sems)

    def body(*refs):
        src_refs, land_refs = refs[:ns], refs[ns:ns + nl]
        sem_refs = refs[ns + nl:ns + nl + k]
        for cp in copies_of(src_refs, land_refs, sem_refs):
            cp.wait()

    outs = pl.pallas_call(
        body, name=name,
        out_shape=[pltpu.HBM(a.shape, a.dtype) for a in list(srcs) + list(lands)],
        in_specs=[HBM_SPEC] * (ns + nl) + [SEM_SPEC] * k + [ANY_SPACE],
        out_specs=[HBM_SPEC] * (ns + nl),
        input_output_aliases={i: i for i in range(ns + nl)},
        compiler_params=pltpu.CompilerParams(has_side_effects=EFFECT),
    )(*srcs, *lands, *sems, after)
    return outs[ns:]


def _gather_first_copies(shard_refs, land_refs, sems):
    send_sems, recv_sems, local_sems = sems
    x, y, c = _mesh_pos()
    me = _flat(x, y, c)
    peers = [(x, y, 1 - c), (1 - x, y, c), (x, 1 - y, c), (1 - x, 1 - y, c)]
    copies = []
    for a, (shard, land) in enumerate(zip(shard_refs, land_refs)):
        copies.append(pltpu.make_async_copy(shard, land.at[me], local_sems.at[a]))
        for k, peer in enumerate(peers):
            copies.append(pltpu.make_async_remote_copy(
                src_ref=shard, dst_ref=land.at[me], send_sem=send_sems.at[4 * a + k], recv_sem=recv_sems.at[4 * a + k],
                device_id=peer, device_id_type=MESH_ID))
    return copies


def _gather_forward_copies(src_refs, land_refs, sems):
    del src_refs
    send_sems, recv_sems = sems
    x, y, c = _mesh_pos()
    chips = [(1 - x, y), (x, 1 - y), (1 - x, 1 - y)]
    copies = []
    for a, land in enumerate(land_refs):
        for j, chip in enumerate(chips):
            rows = land.at[_flat(*chip, c)]
            copies.append(pltpu.make_async_remote_copy(
                src_ref=rows, dst_ref=rows, send_sem=send_sems.at[3 * a + j], recv_sem=recv_sems.at[3 * a + j],
                device_id=(x, y, 1 - c), device_id_type=MESH_ID))
    return copies


def _gather_first_start(groups, name):
    shards = [s for g in groups for s in g]
    lands = [lax.empty((N_DEV,) + s.shape, s.dtype) for s in shards]
    bounds = [sum(len(g) for g in groups[:i]) for i in range(len(groups) + 1)]

    def copies_of(src_refs, land_refs, sems):
        copies = []
        for i in range(len(groups)):
            lo, hi = bounds[i], bounds[i + 1]
            copies += _gather_first_copies(src_refs[lo:hi], land_refs[lo:hi], sems[3 * i:3 * i + 3])
        return copies

    n_sems = tuple(q for g in groups for q in (4 * len(g), 4 * len(g), len(g)))
    sems, srcs, lands, token = _split_start(copies_of, shards, lands, n_sems, name)
    return [(sems[3 * i:3 * i + 3], srcs[bounds[i]:bounds[i + 1]], lands[bounds[i]:bounds[i + 1]], token)
            for i in range(len(groups))]


def _gather_forward_start(lands, name):
    n = len(lands)
    return _split_start(_gather_forward_copies, [], lands, (3 * n, 3 * n), name)


def _all_to_all_copies(n_scattered):
    def copies_of(src_refs, land_refs, sems):
        send_sems, recv_sems, local_sems = sems
        x, y, c = _mesh_pos()
        me = _flat(x, y, c)
        copies = []
        for a, (src, land) in enumerate(zip(src_refs, land_refs)):
            scattered = a < n_scattered
            copies.append(pltpu.make_async_copy(src.at[me] if scattered else src, land.at[me], local_sems.at[a]))
            for k in range(1, N_DEV):
                peer = (1 - x if k & 4 else x, 1 - y if k & 2 else y, 1 - c if k & 1 else c)
                copies.append(pltpu.make_async_remote_copy(
                    src_ref=src.at[_flat(*peer)] if scattered else src, dst_ref=land.at[me],
                    send_sem=send_sems.at[7 * a + k - 1], recv_sem=recv_sems.at[7 * a + k - 1],
                    device_id=peer, device_id_type=MESH_ID))
        return copies
    return copies_of


def _all_to_all_start(scattered, broadcast, name):
    srcs = list(scattered) + list(broadcast)
    lands = [lax.empty(a.shape, a.dtype) for a in scattered] + [lax.empty((N_DEV,) + a.shape, a.dtype) for a in broadcast]
    n = len(srcs)
    return _split_start(_all_to_all_copies(len(scattered)), srcs, lands, (7 * n, 7 * n, n), name)


def _call_behind(deps, body, *, in_specs, **kwargs):
    n_in, n_dep = len(in_specs), len(deps)

    def body_without_deps(*refs):
        return body(*refs[:n_in], *refs[n_in + n_dep:])

    call = pl.pallas_call(body_without_deps, in_specs=list(in_specs) + [ANY_SPACE] * n_dep, **kwargs)
    return lambda *operands: call(*operands, *deps)


def _row_tile(rows):
    for cand in (256, 128, 64, 32, 16):
        if rows % cand == 0:
            return cand
    return rows


def _adamw_whole(groups, name):
    n = len(groups)

    def body(*refs):
        for i in range(n):
            g_ref, w_ref, m_ref, v_ref = refs[4 * i:4 * i + 4]
            d_ref, nm_ref, nv_ref = refs[4 * n + 3 * i:4 * n + 3 * i + 3]
            d_ref[...], nm_ref[...], nv_ref[...] = _adam_update(g_ref[...], w_ref[...], m_ref[...], v_ref[...])

    outs = pl.pallas_call(
        body, name=name, out_shape=[jax.ShapeDtypeStruct(grp[0].shape, F32) for grp in groups for _ in range(3)],
        in_specs=[VMEM_WHOLE] * (4 * n), out_specs=[VMEM_WHOLE] * (3 * n),
        compiler_params=_params(),
    )(*[t for grp in groups for t in grp])
    return [outs[3 * i:3 * i + 3] for i in range(n)]


def _sum_sources_whole(recvs, name):
    n = len(recvs)

    def body(*refs):
        for r_ref, o_ref in zip(refs[:n], refs[n:]):
            acc = r_ref[0].astype(F32)
            for d in range(1, N_DEV):
                acc = acc + r_ref[d].astype(F32)
            o_ref[...] = acc

    return pl.pallas_call(
        body, name=name, out_shape=[jax.ShapeDtypeStruct(r.shape[1:], F32) for r in recvs],
        in_specs=[VMEM_WHOLE] * n, out_specs=[VMEM_WHOLE] * n,
        compiler_params=_params(),
    )(*recvs)


def _sum_adamw(recv, w, m, v, name):
    _, rows, cols = recv.shape
    tile = _row_tile(rows)

    def body(r_ref, w_ref, m_ref, v_ref, g_ref, d_ref, nm_ref, nv_ref):
        acc = r_ref[0].astype(F32)
        for d in range(1, N_DEV):
            acc = acc + r_ref[d].astype(F32)
        g_ref[...] = acc
        d_ref[...], nm_ref[...], nv_ref[...] = _adam_update(acc, w_ref[...], m_ref[...], v_ref[...])

    spec = pl.BlockSpec((tile, cols), lambda i: (i, 0))
    shp = jax.ShapeDtypeStruct((rows, cols), F32)
    return pl.pallas_call(
        body, name=name, grid=(rows // tile,), out_shape=[shp] * 4,
        in_specs=[pl.BlockSpec((N_DEV, tile, cols), lambda i: (0, i, 0)), spec, spec, spec], out_specs=[spec] * 4,
        compiler_params=_params(("parallel",)),
    )(recv, w, m, v)


SMALL_SLOTS = {"norm_x_g": (0, 0, 1, D_MODEL), "norm_mem_g": (0, 8, 1, D_MODEL), "norm_ffn_g": (0, 16, 1, D_MODEL),
               "final_norm_g": (0, 24, 1, D_MODEL), "pool_scale": (0, 32, 1, HW),
               "lb_logits": (1, 0, 2, HW), "hgrn_norm_g": (1, 8, HEADS, HD), "norm_mix_g": (2, 0, 1, D_MODEL)}
LOSS_ROW = 25
SMALL_ORDER = ("norm_mix_g", "lb_logits", "hgrn_norm_g", "pool_scale", "norm_x_g", "norm_mem_g", "norm_ffn_g",
               "final_norm_g", "w_pool")


def _small_update(srecvs, wprecv, params):
    flat = [t for n in SMALL_ORDER for t in params[n]]
    nb = len(srecvs)
    n_in = nb + 1 + len(flat)

    def body(*refs):
        s_refs, wp_ref = refs[0:nb], refs[nb]
        in_refs = refs[nb + 1:n_in]
        loss_ref = refs[n_in]
        out_refs = refs[n_in + 1:-nb]
        accs = refs[-nb:]
        for s_ref, acc in zip(s_refs, accs):
            total = s_ref[0]
            for d in range(1, N_DEV):
                total = total + s_ref[d]
            acc[...] = total
        loss_ref[...] = accs[0][LOSS_ROW:LOSS_ROW + 1, 0:1]
        for i, name in enumerate(SMALL_ORDER):
            w_ref, m_ref, v_ref = in_refs[3 * i:3 * i + 3]
            g_ref, d_ref, nm_ref, nv_ref = out_refs[4 * i:4 * i + 4]
            if name == "w_pool":
                g = wp_ref[0]
                for d in range(1, N_DEV):
                    g = g + wp_ref[d]
            else:
                buf, r0, nr, nc = SMALL_SLOTS[name]
                g = accs[buf][r0:r0 + nr, 0:nc]
            g_ref[...] = g
            d_ref[...], nm_ref[...], nv_ref[...] = _adam_update(g, w_ref[...], m_ref[...], v_ref[...])

    out_shape = [jax.ShapeDtypeStruct((1, 1), F32)]
    for n in SMALL_ORDER:
        out_shape += [jax.ShapeDtypeStruct(params[n][0].shape, F32)] * 4
    outs = pl.pallas_call(
        body, name="small_update", out_shape=out_shape,
        in_specs=[VMEM_WHOLE] * n_in, out_specs=[VMEM_WHOLE] * len(out_shape),
        scratch_shapes=[pltpu.VMEM(r.shape[1:], F32) for r in srecvs],
        compiler_params=_params(),
    )(*srecvs, wprecv, *flat)
    return outs[0], {n: outs[1 + 4 * i:5 + 4 * i] for i, n in enumerate(SMALL_ORDER)}


def _in_proj(x, g, w_t, deps):
    s = x.shape[0]
    tm = min(ROW_TILE, s)

    def body(x_ref, g_ref, w_ref, z_ref, h_ref):
        xv = x_ref[...]
        h = (xv * _rms(xv) * g_ref[...]).astype(BF16)
        h_ref[...] = h
        z_ref[...] = _mm_nt(h, w_ref[...])

    return _call_behind(
        deps, body, name="in_proj", grid=(s // tm,),
        out_shape=[jax.ShapeDtypeStruct((s, IN_WIDTH), F32), jax.ShapeDtypeStruct((s, D_MODEL), BF16)],
        in_specs=[pl.BlockSpec((tm, D_MODEL), lambda i: (i, 0)), _full((1, D_MODEL)), VMEM_WHOLE],
        out_specs=[pl.BlockSpec((tm, IN_WIDTH), lambda i: (i, 0)), pl.BlockSpec((tm, D_MODEL), lambda i: (i, 0))],
        compiler_params=_params(("parallel",)),
    )(x, g, w_t)


def _chunk_masks():
    row = lax.broadcasted_iota(jnp.int32, (CHUNK, CHUNK), 0)
    col = lax.broadcasted_iota(jnp.int32, (CHUNK, CHUNK), 1)
    return row, col


def _ones_where(mask):
    return jnp.where(mask, 1.0, 0.0).astype(BF16)


def _hgrn_gates(zq, zf, lb):
    sq = _sigmoid(zq)
    sig = _sigmoid(zf)
    f = lb + (1.0 - lb) * sig
    return zq * sq, sq, sig, f


def _sub_chunk_masks(width):
    trow = lax.broadcasted_iota(jnp.int32, (CHUNK, width), 0)
    return [(trow >= SUB * j) & (trow < SUB * (j + 1)) for j in range(N_SUB)]


def _head(a, h):
    return a[:, HD * h:HD * (h + 1)]


def _lanes(parts):
    return jnp.concatenate(parts, axis=1)


def _hgrn_decay_factors(b_scr, r0, b, in_sub):
    bases = [jnp.zeros((1, HW), F32)] + [b_scr[r0 + SUB * j - 1:r0 + SUB * j, :] for j in range(1, N_SUB)]
    own_base = bases[N_SUB - 1]
    for j in range(N_SUB - 2, -1, -1):
        own_base = jnp.where(in_sub[j], bases[j], own_base)
    eq = jnp.exp(b - own_base)
    ek = []
    for j in range(N_SUB):
        upto = SUB * (j + 1)
        e = jnp.exp(jnp.minimum(bases[j] - b[0:upto], EXP_CAP))
        ek.append(e if upto == CHUNK else jnp.concatenate([e, jnp.zeros((CHUNK - upto, HW), F32)], axis=0))
    return eq, ek


def _per_sub_chunk(x, in_sub):
    return _lanes([jnp.where(in_sub[j], x, 0.0) for j in range(N_SUB)])


def _own_lane_block(a, in_sub):
    out = a[:, HD * (N_SUB - 1):HD * N_SUB]
    for j in range(N_SUB - 2, -1, -1):
        out = jnp.where(in_sub[j], a[:, HD * j:HD * (j + 1)], out)
    return out


def _head_rms(o):
    return _lanes([jnp.broadcast_to(_rms(_head(o, h)), (CHUNK, HD)) for h in range(HEADS)])


def _head_mean(a):
    return _lanes([jnp.broadcast_to(jnp.mean(_head(a, h), axis=-1, keepdims=True), (CHUNK, HD)) for h in range(HEADS)])


def _hgrn_fwd(z, lb_logits, gn):
    s = z.shape[0]
    n_chunks = s // CHUNK

    def body(zq_ref, zf_ref, zi_ref, zg_ref, lbl_ref, gn_ref, oa_ref, o_ref, st_ref, state, b_scr):
        @pl.when(pl.program_id(0) == 0)
        def _():
            state[...] = jnp.zeros_like(state)

        lb = _sigmoid(lbl_ref[0:1, :] - lbl_ref[1:2, :])
        row, col = _chunk_masks()
        causal = col <= row
        tri = _ones_where(causal)
        in_sub, in_sub_head = _sub_chunk_masks(HW), _sub_chunk_masks(HD)
        gn_row = _lanes([gn_ref[h:h + 1, :] for h in range(HEADS)])
        def front(c):
            r0 = CHUNK * c
            rs = slice(r0, r0 + CHUNK)
            q, _, _, f = _hgrn_gates(zq_ref[rs, :], zf_ref[rs, :], lb)
            kk = 1.0 - f
            b = _tri_dot(tri, jnp.log(f), 3)
            b_scr[rs, :] = b
            eq, ek = _hgrn_decay_factors(b_scr, r0, b, in_sub)
            b_last = b_scr[r0 + CHUNK - 1:r0 + CHUNK, :]
            qe = q * eq
            return {"rs": rs, "v": zi_ref[rs, :], "qg": q * jnp.exp(b), "kd": kk * jnp.exp(b_last - b),
                    "lam_last": jnp.exp(b_last),
                    "q16": [_per_sub_chunk(_head(qe, h), in_sub_head).astype(BF16) for h in range(HEADS)],
                    "ke16": [_lanes([_head(kk * e, h) for e in ek]).astype(BF16) for h in range(HEADS)]}

        def recurrence(c, p):
            st_ref[c] = state[...]
            a, o_inter = [], []
            for h in range(HEADS):
                vh, st = _head(p["v"], h), state[h]
                a.append(jnp.where(causal, _mm_nt(p["q16"][h], p["ke16"][h]), 0.0))
                o_inter.append(_mm_nt(_head(p["qg"], h), st))
                state[h] = st * _head(p["lam_last"], h) + _mm_tn(vh, _head(p["kd"], h))
            return _lanes([_mm(a[h], _head(p["v"], h)) + o_inter[h] for h in range(HEADS)])

        def back(p, o):
            rs = p["rs"]
            o_ref[rs, :] = o
            zg = zg_ref[rs, :]
            oa_ref[rs, :] = (o * _head_rms(o) * gn_row * zg * _sigmoid(zg)).astype(BF16)

        p = front(0)
        for c in range(CHUNKS_PER_STEP):
            o = recurrence(c, p)
            p_next = front(c + 1) if c + 1 < CHUNKS_PER_STEP else None
            back(p, o)
            p = p_next

    rows = CHUNK * CHUNKS_PER_STEP
    zspec = lambda cb: pl.BlockSpec((rows, HW), lambda i, cb=cb: (i, cb))
    return pl.pallas_call(
        body, name="hgrn_fwd", grid=(s // rows,),
        out_shape=[jax.ShapeDtypeStruct((s, 2 * HW), BF16), jax.ShapeDtypeStruct((s, HW), F32),
                   jax.ShapeDtypeStruct((n_chunks, HEADS, HD, HD), F32)],
        in_specs=[zspec(0), zspec(1), zspec(2), zspec(3), _full((2, HW)), _full((HEADS, HD))],
        out_specs=[pl.BlockSpec((rows, HW), lambda i: (i, 0)), pl.BlockSpec((rows, HW), lambda i: (i, 0)),
                   pl.BlockSpec((CHUNKS_PER_STEP, HEADS, HD, HD), lambda i: (i, 0, 0, 0))],
        scratch_shapes=[pltpu.VMEM((HEADS, HD, HD), F32), pltpu.VMEM((rows, HW), F32)],
        compiler_params=_params(("arbitrary",)),
    )(z, z, z, z, lb_logits, gn)


def _pool_counts(tile_idx, tm):
    t = tile_idx * tm + lax.broadcasted_iota(jnp.int32, (tm, 1), 0)
    return [1.0 / jnp.minimum(t + 1, w).astype(F32) for w in POOL_WINDOWS]


def _pool_fwd(z, w_pool, scale, mixed_in, deps):
    s = z.shape[0]
    tm = min(ROW_TILE, s)

    def body(p_ref, w_ref, sc_ref, mixin_ref, ob_ref, pooled_ref, ext):
        i = pl.program_id(0)

        @pl.when(i == 0)
        def _():
            ext[0:POOL_HALO, :] = jnp.zeros((POOL_HALO, HW), F32)

        @pl.when(i > 0)
        def _():
            ext[0:POOL_HALO, :] = ext[tm:tm + POOL_HALO, :]

        ext[POOL_HALO:POOL_HALO + tm, :] = p_ref[...]
        inv = _pool_counts(i, tm)
        for g, w in enumerate(POOL_WINDOWS):
            sl = slice(HD * g, HD * (g + 1))
            p = ext[POOL_HALO:POOL_HALO + tm, sl]
            win = p
            for d in range(1, w):
                win = win + ext[POOL_HALO - d:POOL_HALO - d + tm, sl]
            pooled = (win * inv[g] - p).astype(BF16)
            pooled_ref[:, sl] = pooled
            ob_ref[:, sl] = (_mm(pooled, w_ref[g]) * sc_ref[:, sl]).astype(BF16)

    return _call_behind(
        deps, body, name="pool_fwd", grid=(s // tm,),
        out_shape=[jax.ShapeDtypeStruct((s, 2 * HW), BF16), jax.ShapeDtypeStruct((s, HW), BF16)],
        in_specs=[pl.BlockSpec((tm, HW), lambda i: (i, 4)), _full((HEADS, HD, HD)), _full((1, HW)), ANY_SPACE],
        out_specs=[pl.BlockSpec((tm, HW), lambda i: (i, 1)), pl.BlockSpec((tm, HW), lambda i: (i, 0))],
        scratch_shapes=[pltpu.VMEM((tm + POOL_HALO, HW), F32)],
        input_output_aliases={3: 0},
        compiler_params=_params(("arbitrary",)),
    )(z, w_pool, scale, mixed_in)


def _mem_kv(mem, g, wk, wv, deps):
    def body(m_ref, g_ref, wk_ref, wv_ref, hm_ref, k_ref, v_ref):
        m = m_ref[...]
        hm = (m * _rms(m) * g_ref[...]).astype(BF16)
        hm_ref[...] = hm
        k_ref[...] = _mm(hm, wk_ref[...]).astype(BF16)
        v_ref[...] = _mm(hm, wv_ref[...]).astype(BF16)

    shp = jax.ShapeDtypeStruct((MEM_LEN, D_MODEL), BF16)
    return _call_behind(
        deps, body, name="mem_kv", out_shape=[shp, shp, shp],
        in_specs=[VMEM_WHOLE] * 4, out_specs=[VMEM_WHOLE] * 3,
        compiler_params=_params(),
    )(mem, g, wk, wv)


def _softmax_rows(sc):
    e = jnp.exp(sc - jnp.max(sc, axis=-1, keepdims=True))
    return e / jnp.sum(e, axis=-1, keepdims=True)


def _mix_xattn_fwd(x0, mixed, w_out, g, wq, xk, xv, wo_t, deps):
    s = x0.shape[0]
    tm = min(ROW_TILE, s)
    scale = XHD ** -0.5

    def body(x_ref, mix_ref, wout_ref, g_ref, wq_ref, k_ref, v_ref, wo_ref, x1_ref, o_ref, hq_ref, q_ref, att_ref):
        xv_ = x_ref[...] + _mm(mix_ref[...], wout_ref[...])
        x1_ref[...] = xv_
        hq = (xv_ * _rms(xv_) * g_ref[...]).astype(BF16)
        hq_ref[...] = hq
        q_ref[...] = (_mm(hq, wq_ref[...]) * scale).astype(BF16)
        heads = [slice(XHD * h, XHD * (h + 1)) for h in range(HEADS)]
        scores = [_mm_nt(q_ref[:, sl], k_ref[:, sl]) for sl in heads]
        probs = [_softmax_rows(sc) for sc in scores]
        for sl, p in zip(heads, probs):
            att_ref[:, sl] = _mm(p, v_ref[:, sl]).astype(BF16)
        o_ref[...] = xv_ + _mm_nt(att_ref[...], wo_ref[...])

    row_f32 = pl.BlockSpec((tm, D_MODEL), lambda i: (i, 0))
    bshape = jax.ShapeDtypeStruct((s, D_MODEL), BF16)
    fshape = jax.ShapeDtypeStruct((s, D_MODEL), F32)
    return _call_behind(
        deps, body, name="mix_xattn_fwd", grid=(s // tm,),
        out_shape=[fshape, fshape, bshape, bshape, bshape],
        in_specs=[row_f32, row_f32, VMEM_WHOLE, _full((1, D_MODEL)), VMEM_WHOLE, VMEM_WHOLE, VMEM_WHOLE, VMEM_WHOLE],
        out_specs=[row_f32] * 5,
        compiler_params=_params(("parallel",)),
    )(x0, mixed, w_out, g, wq, xk, xv, wo_t)


def _mlp_fwd_loss(x, g, w1, w2, gf, target):
    s = x.shape[0]
    tm = min(ROW_TILE, s)

    def body(x_ref, g_ref, w1_ref, w2_ref, gf_ref, t_ref, dx_ref, dx16_ref, u_ref, hf_ref, slot_ref):
        @pl.when(pl.program_id(0) == 0)
        def _():
            slot_ref[...] = jnp.zeros_like(slot_ref)

        xv = x_ref[...]
        hf = (xv * _rms(xv) * g_ref[...]).astype(BF16)
        hf_ref[...] = hf
        a_next = _mm(hf, w1_ref[0])
        for j in range(N_DEV):
            a = jnp.maximum(a_next, 0.0)
            if j + 1 < N_DEV:
                a_next = _mm(hf, w1_ref[j + 1])
            u_ref[:, FF_BLK * j:FF_BLK * (j + 1)] = (a * a).astype(BF16)
        acc = xv + _mm(u_ref[...], w2_ref[...])
        gfv = gf_ref[...]
        r = _rms(acc)
        n = acc * r
        err = n * gfv - t_ref[...]
        slot_ref[1:2, :] += jnp.sum(jnp.mean(err * err, axis=-1, keepdims=True), axis=0, keepdims=True) * 0.5
        dy = err * (1.0 / D_MODEL)
        slot_ref[0:1, :] += jnp.sum(dy * n, axis=0, keepdims=True)
        dn = dy * gfv
        dx = r * (dn - n * jnp.mean(dn * n, axis=-1, keepdims=True))
        dx_ref[...] = dx
        dx16_ref[...] = dx.astype(BF16)

    row_f32 = pl.BlockSpec((tm, D_MODEL), lambda i: (i, 0))
    return pl.pallas_call(
        body, name="mlp_fwd_loss", grid=(s // tm,),
        out_shape=[jax.ShapeDtypeStruct((s, D_MODEL), F32), jax.ShapeDtypeStruct((s, D_MODEL), BF16),
                   jax.ShapeDtypeStruct((s, D_FF), BF16), jax.ShapeDtypeStruct((s, D_MODEL), BF16),
                   jax.ShapeDtypeStruct((SLOT, D_MODEL), F32)],
        in_specs=[row_f32, _full((1, D_MODEL)), VMEM_WHOLE, VMEM_WHOLE, _full((1, D_MODEL)), row_f32],
        out_specs=[row_f32, row_f32, pl.BlockSpec((tm, D_FF), lambda i: (i, 0)), row_f32, _full((SLOT, D_MODEL))],
        compiler_params=_params(("arbitrary",)),
    )(x, g, w1, w2, gf, target)


def _zero_slot(slot_ref):
    @pl.when(pl.program_id(0) == 0)
    def _():
        slot_ref[...] = jnp.zeros_like(slot_ref)


def _mlp_bwd(dx3, u, x2, g, w1, w2, deps):
    s = x2.shape[0]
    tm = min(ROW_TILE, s)

    def body(d_ref, u_ref, x_ref, g_ref, w1_ref, w2_ref, da_ref, dx_ref, slot_ref):
        _zero_slot(slot_ref)
        d = d_ref[...]
        d16 = d.astype(BF16)
        du_next = _mm_nt(d16, w2_ref[0])
        dhf = jnp.zeros((tm, D_MODEL), F32)
        for j in range(N_DEV):
            sl = slice(FF_BLK * j, FF_BLK * (j + 1))
            du = du_next
            if j + 1 < N_DEV:
                du_next = _mm_nt(d16, w2_ref[j + 1])
            u = u_ref[:, sl].astype(F32)
            da = (du * (2.0 * u * lax.rsqrt(jnp.maximum(u, TINY)))).astype(BF16)
            da_ref[:, sl] = da
            dhf = dhf + _mm_nt(da, w1_ref[j])
        dx, dg = _rms_bwd(x_ref[...], g_ref[...], dhf)
        dx_ref[...] = d + dx
        slot_ref[0:1, :] += dg

    row_f32 = pl.BlockSpec((tm, D_MODEL), lambda i: (i, 0))
    return _call_behind(
        deps, body, name="mlp_bwd", grid=(s // tm,),
        out_shape=[jax.ShapeDtypeStruct((s, D_FF), BF16), jax.ShapeDtypeStruct((s, D_MODEL), F32),
                   jax.ShapeDtypeStruct((SLOT, D_MODEL), F32)],
        in_specs=[row_f32, pl.BlockSpec((tm, D_FF), lambda i: (i, 0)), row_f32, _full((1, D_MODEL)),
                  VMEM_WHOLE, VMEM_WHOLE],
        out_specs=[pl.BlockSpec((tm, D_FF), lambda i: (i, 0)), row_f32, _full((SLOT, D_MODEL))],
        compiler_params=_params(("arbitrary",)),
    )(dx3, u, x2, g, w1, w2)


def _wgrad(a, b, name, col_blocks=False):
    s, m = a.shape
    n = b.shape[1]
    tm = 1280 if m % 1280 == 0 else min(1024, m)
    tn = min(1024, n)
    blk = n // N_DEV
    per_step = tn // blk if col_blocks else 1
    ts = min(2 * ROW_TILE, s)
    n_s = s // ts

    def body(a_ref, b_ref, o_ref, acc):
        k = pl.program_id(2)

        @pl.when(k == 0)
        def _():
            acc[...] = jnp.zeros_like(acc)

        acc[...] += _mm_tn(a_ref[...], b_ref[...])

        @pl.when(k == n_s - 1)
        def _():
            if col_blocks:
                for p in range(per_step):
                    o_ref[p] = acc[:, blk * p:blk * (p + 1)].astype(BF16)
            else:
                o_ref[...] = acc[...].astype(BF16)

    if col_blocks:
        out_shape = jax.ShapeDtypeStruct((N_DEV, m, blk), BF16)
        out_spec = pl.BlockSpec((per_step, tm, blk), lambda i, j, k: (j, i, 0))
    else:
        out_shape = jax.ShapeDtypeStruct((m, n), BF16)
        out_spec = pl.BlockSpec((tm, tn), lambda i, j, k: (i, j))
    return pl.pallas_call(
        body, name=name, grid=(m // tm, n // tn, n_s), out_shape=out_shape,
        in_specs=[pl.BlockSpec((ts, tm), lambda i, j, k: (k, i)), pl.BlockSpec((ts, tn), lambda i, j, k: (k, j))],
        out_specs=out_spec,
        scratch_shapes=[pltpu.VMEM((tm, tn), F32)],
        compiler_params=_params(("parallel", "parallel", "arbitrary")),
    )(a, b)


def _xattn_bwd(dx2, x1, g, q, xk, xv, wq, wo_t, deps):
    s = x1.shape[0]
    tm = min(ROW_TILE, s)
    scale = XHD ** -0.5

    def body(d_ref, x_ref, g_ref, q_ref, k_ref, v_ref, wq_ref, wo_ref, dx_ref, dx16_ref, dq_ref, dk_ref, dv_ref, slot_ref,
             datt):
        _zero_slot(slot_ref)

        @pl.when(pl.program_id(0) == 0)
        def _():
            dk_ref[...] = jnp.zeros_like(dk_ref)
            dv_ref[...] = jnp.zeros_like(dv_ref)

        d = d_ref[...]
        datt[...] = _mm(d, wo_ref[...]).astype(BF16)
        heads = [slice(XHD * h, XHD * (h + 1)) for h in range(HEADS)]
        scores = [_mm_nt(q_ref[:, sl], k_ref[:, sl]) for sl in heads]
        dps = [_mm_nt(datt[:, sl], v_ref[:, sl]) for sl in heads]
        probs = [_softmax_rows(sc) for sc in scores]
        dss = [(p * (dp - jnp.sum(dp * p, axis=-1, keepdims=True))).astype(BF16) for p, dp in zip(probs, dps)]
        for sl, p, ds in zip(heads, probs, dss):
            dq_ref[:, sl] = (_mm(ds, k_ref[:, sl]) * scale).astype(BF16)
            dk_ref[:, sl] += _mm_tn(ds, q_ref[:, sl])
            dv_ref[:, sl] += _mm_tn(p, datt[:, sl])
        dx, dg = _rms_bwd(x_ref[...], g_ref[...], _mm_nt(dq_ref[...], wq_ref[...]))
        dx_ref[...] = d + dx
        dx16_ref[...] = (d + dx).astype(BF16)
        slot_ref[0:1, :] += dg

    row_f32 = pl.BlockSpec((tm, D_MODEL), lambda i: (i, 0))
    kv = jax.ShapeDtypeStruct((MEM_LEN, D_MODEL), F32)
    tokens16 = jax.ShapeDtypeStruct((s, D_MODEL), BF16)
    return _call_behind(
        deps, body, name="xattn_bwd", grid=(s // tm,),
        out_shape=[jax.ShapeDtypeStruct((s, D_MODEL), F32), tokens16, tokens16, kv, kv,
                   jax.ShapeDtypeStruct((SLOT, D_MODEL), F32)],
        in_specs=[row_f32, row_f32, _full((1, D_MODEL)), row_f32, VMEM_WHOLE, VMEM_WHOLE, VMEM_WHOLE, VMEM_WHOLE],
        out_specs=[row_f32, row_f32, row_f32, _full((MEM_LEN, D_MODEL)), _full((MEM_LEN, D_MODEL)),
                   _full((SLOT, D_MODEL))],
        scratch_shapes=[pltpu.VMEM((tm, D_MODEL), BF16)],
        compiler_params=_params(("arbitrary",)),
    )(dx2, x1, g, q, xk, xv, wq, wo_t)


def _mem_bwd(mem, g, hm, dxk, dxv, wk, wv):
    def body(m_ref, g_ref, hm_ref, dk_ref, dv_ref, wk_ref, wv_ref, dwk_ref, dwv_ref, slot_ref):
        dk, dv = dk_ref[...], dv_ref[...]
        hm_ = hm_ref[...]
        dwk_ref[...] = _mm_tn(hm_, dk).astype(BF16)
        dwv_ref[...] = _mm_tn(hm_, dv).astype(BF16)
        _, dg = _rms_bwd(m_ref[...], g_ref[...], _mm_nt(dk, wk_ref[...]) + _mm_nt(dv, wv_ref[...]))
        slot_ref[...] = jnp.zeros_like(slot_ref)
        slot_ref[0:1, :] = dg

    wshape = jax.ShapeDtypeStruct((D_MODEL, D_MODEL), BF16)
    return pl.pallas_call(
        body, name="mem_bwd", out_shape=[wshape, wshape, jax.ShapeDtypeStruct((SLOT, D_MODEL), F32)],
        in_specs=[VMEM_WHOLE] * 7, out_specs=[VMEM_WHOLE] * 3,
        compiler_params=_params(),
    )(mem, g, hm, dxk, dxv, wk, wv)


def _pool_bwd(dx1, w_out, pooled, w_pool, scale, deps):
    s = dx1.shape[0]
    tm = min(ROW_TILE, s)
    n_t = s // tm

    def body(dx_ref, wo_ref, pl_ref, w_ref, sc_ref, dz_ref, dw_ref, slot_ref, ext, do_ref):
        i = pl.program_id(0)
        tile = n_t - 1 - i
        _zero_slot(slot_ref)
        do_ref[...] = _mm_nt(dx_ref[...], wo_ref[HW:2 * HW, :])

        @pl.when(i == 0)
        def _():
            dw_ref[...] = jnp.zeros_like(dw_ref)
            ext[tm:tm + POOL_HALO, :] = jnp.zeros((POOL_HALO, HW), F32)

        @pl.when(i > 0)
        def _():
            ext[tm:tm + POOL_HALO, :] = ext[0:POOL_HALO, :]

        inv = _pool_counts(tile, tm)
        dpooled = []
        for g in range(HEADS):
            sl = slice(HD * g, HD * (g + 1))
            pooled_g = pl_ref[:, sl]
            do = do_ref[:, sl]
            slot_ref[0:1, sl] += jnp.sum(_mm(pooled_g, w_ref[g]) * do, axis=0, keepdims=True)
            dy = (do * sc_ref[:, sl]).astype(BF16)
            dw_ref[g] += _mm_tn(pooled_g, dy)
            dpo = _mm_nt(dy, w_ref[g])
            dpooled.append(dpo)
            ext[0:tm, sl] = dpo * inv[g]
        for g, w in enumerate(POOL_WINDOWS):
            sl = slice(HD * g, HD * (g + 1))
            win = ext[0:tm, sl]
            for d in range(1, w):
                win = win + ext[d:d + tm, sl]
            dz_ref[:, sl] = (win - dpooled[g]).astype(BF16)

    return _call_behind(
        deps, body, name="pool_bwd", grid=(n_t,),
        out_shape=[jax.ShapeDtypeStruct((s, IN_WIDTH), BF16), jax.ShapeDtypeStruct((HEADS, HD, HD), F32),
                   jax.ShapeDtypeStruct((SLOT, D_MODEL), F32)],
        in_specs=[pl.BlockSpec((tm, D_MODEL), lambda i: (n_t - 1 - i, 0)), VMEM_WHOLE,
                  pl.BlockSpec((tm, HW), lambda i: (n_t - 1 - i, 0)), _full((HEADS, HD, HD)), _full((1, HW))],
        out_specs=[pl.BlockSpec((tm, HW), lambda i: (n_t - 1 - i, 4)), _full((HEADS, HD, HD)), _full((SLOT, D_MODEL))],
        scratch_shapes=[pltpu.VMEM((tm + POOL_HALO, HW), F32), pltpu.VMEM((tm, HW), F32)],
        compiler_params=_params(("arbitrary",)),
    )(dx1, w_out, pooled, w_pool, scale)


def _hgrn_bwd(z, o, dx1, w_out, states, lb_logits, gn, dz_in, deps):
    s = z.shape[0]
    n_chunks = s // CHUNK

    def body(zq_ref, zf_ref, zi_ref, zg_ref, o_ref, dx_ref, wo_ref, st_ref, lbl_ref, gn_ref, dzin_ref,
             dz_ref, dlb_ref, dgn_ref, dstate, b_scr, dlb_acc, do_ref):
        i = pl.program_id(0)

        @pl.when(i == 0)
        def _():
            dstate[...] = jnp.zeros_like(dstate)
            dlb_acc[...] = jnp.zeros_like(dlb_acc)
            dgn_ref[...] = jnp.zeros_like(dgn_ref)
            dlb_ref[...] = jnp.zeros_like(dlb_ref)

        do_ref[...] = _mm_nt(dx_ref[...], wo_ref[0:HW, :])
        lb = _sigmoid(lbl_ref[0:1, :] - lbl_ref[1:2, :])
        row, col = _chunk_masks()
        causal = col <= row
        tri = _ones_where(causal)
        upper = _ones_where(col >= row)
        strict_lower = _ones_where(col < row)
        in_sub, in_sub_head = _sub_chunk_masks(HW), _sub_chunk_masks(HD)
        gn_row = _lanes([gn_ref[h:h + 1, :] for h in range(HEADS)])
        sums = {"dlb": 0.0, "dgn": 0.0}

        def front(c):
            r0 = CHUNK * c
            rs = slice(r0, r0 + CHUNK)
            p = {"rs": rs}
            p["zq"] = zq_ref[rs, :]
            p["q"], p["sq"], p["sig"], p["f"] = _hgrn_gates(p["zq"], zf_ref[rs, :], lb)
            p["kk"] = 1.0 - p["f"]
            b = _tri_dot(tri, jnp.log(p["f"]), 3)
            b_scr[rs, :] = b
            p["v"] = zi_ref[rs, :]
            o, zg, doa = o_ref[rs, :], zg_ref[rs, :], do_ref[rs, :]
            sg = _sigmoid(zg)
            rms = _head_rms(o)
            n = o * rms
            don = doa * (zg * sg)
            sums["dgn"] = sums["dgn"] + jnp.sum(don * n, axis=0, keepdims=True)
            dn = don * gn_row
            p["d_o"] = rms * (dn - n * _head_mean(dn * n))
            dz_ref[rs, 3 * HW:4 * HW] = (doa * (n * gn_row) * (sg * (1.0 + zg * (1.0 - sg)))).astype(BF16)
            p["eq"], p["ek"] = _hgrn_decay_factors(b_scr, r0, b, in_sub)
            b_last = b_scr[r0 + CHUNK - 1:r0 + CHUNK, :]
            p["lam"], p["e_last"], p["lam_last"] = jnp.exp(b), jnp.exp(b_last - b), jnp.exp(b_last)
            p["qe"], p["qg"], p["kd"] = p["q"] * p["eq"], p["q"] * p["lam"], p["kk"] * p["e_last"]
            p["ke"] = [p["kk"] * e for e in p["ek"]]
            p["q16"] = [_per_sub_chunk(_head(p["qe"], h), in_sub_head).astype(BF16) for h in range(HEADS)]
            p["ke16"] = [_lanes([_head(p["ke"][j], h) for j in range(N_SUB)]).astype(BF16) for h in range(HEADS)]
            return p

        def recurrence(c, p):
            m = {k: [] for k in ("dv", "gq", "gk", "dqi", "dkd", "st")}
            a, da, dv_state = [], [], []
            for h in range(HEADS):
                vh, doh = _head(p["v"], h), _head(p["d_o"], h)
                st0, ds1 = st_ref[c, h], dstate[h]
                a.append(jnp.where(causal, _mm_nt(p["q16"][h], p["ke16"][h]), 0.0))
                da.append(jnp.where(causal, _mm_nt(doh, vh), 0.0))
                dv_state.append(_mm_nt(_head(p["kd"], h), ds1))
                m["dqi"].append(_mm(doh, st0))
                m["dkd"].append(_mm(vh, ds1))
                m["st"].append(jnp.sum(st0 * ds1, axis=0, keepdims=True))
                dstate[h] = ds1 * _head(p["lam_last"], h) + _mm_tn(doh, _head(p["qg"], h))
            for h in range(HEADS):
                m["dv"].append(_mm_tn(a[h], _head(p["d_o"], h)) + dv_state[h])
                m["gq"].append(_own_lane_block(_mm(da[h], p["ke16"][h]), in_sub_head))
                m["gk"].append(_mm_tn(da[h], p["q16"][h]))
            return m

        def back(p, m):
            rs = p["rs"]
            dz_ref[rs, 2 * HW:3 * HW] = _lanes(m["dv"]).astype(BF16)
            gq = _lanes(m["gq"])
            gk = [_lanes([m["gk"][h][:, HD * j:HD * (j + 1)] for h in range(HEADS)]) for j in range(N_SUB)]
            dq_inter = p["lam"] * _lanes(m["dqi"])
            dq = p["eq"] * gq + dq_inter
            dk_intra = sum(p["ek"][j] * gk[j] for j in range(N_SUB))
            dk_state = _lanes(m["dkd"]) * p["e_last"]
            db_intra = (p["qe"].astype(BF16).astype(F32) * gq
                        - sum(p["ke"][j].astype(BF16).astype(F32) * gk[j] for j in range(N_SUB)))
            dlf = (_tri_dot(upper, db_intra + p["q"] * dq_inter, 2) + _tri_dot(strict_lower, p["kk"] * dk_state, 2)
                   + p["lam_last"] * _lanes(m["st"]))
            sig, sq, zq = p["sig"], p["sq"], p["zq"]
            df = dlf / p["f"] - (dk_intra + dk_state)
            sums["dlb"] = sums["dlb"] + jnp.sum(df * (1.0 - sig), axis=0, keepdims=True)
            dz_ref[rs, HW:2 * HW] = (df * (1.0 - lb) * sig * (1.0 - sig)).astype(BF16)
            dz_ref[rs, 0:HW] = (dq * (sq * (1.0 + zq * (1.0 - sq)))).astype(BF16)

        p = front(CHUNKS_PER_STEP - 1)
        for c in reversed(range(CHUNKS_PER_STEP)):
            m = recurrence(c, p)
            p_next = front(c - 1) if c > 0 else None
            back(p, m)
            p = p_next
        dlb_acc[...] += sums["dlb"]
        for h in range(HEADS):
            dgn_ref[h:h + 1, 0:HD] += _head(sums["dgn"], h)

        @pl.when(i == n_steps - 1)
        def _():
            dl0 = dlb_acc[...] * lb * (1.0 - lb)
            dlb_ref[0:1, 0:HW] = dl0
            dlb_ref[1:2, 0:HW] = -dl0

    rows = CHUNK * CHUNKS_PER_STEP
    n_steps = s // rows
    rev = lambda i: n_steps - 1 - i
    zspec = lambda cb: pl.BlockSpec((rows, HW), lambda i, cb=cb: (rev(i), cb))
    slot = jax.ShapeDtypeStruct((SLOT, D_MODEL), F32)
    return _call_behind(
        deps, body, name="hgrn_bwd", grid=(n_steps,),
        out_shape=[jax.ShapeDtypeStruct((s, IN_WIDTH), BF16), slot, slot],
        in_specs=[zspec(0), zspec(1), zspec(2), zspec(3), pl.BlockSpec((rows, HW), lambda i: (rev(i), 0)),
                  pl.BlockSpec((rows, D_MODEL), lambda i: (rev(i), 0)), VMEM_WHOLE,
                  pl.BlockSpec((CHUNKS_PER_STEP, HEADS, HD, HD), lambda i: (rev(i), 0, 0, 0)), _full((2, HW)),
                  _full((HEADS, HD)), ANY_SPACE],
        out_specs=[pl.BlockSpec((rows, 4 * HW), lambda i: (rev(i), 0)), _full((SLOT, D_MODEL)), _full((SLOT, D_MODEL))],
        scratch_shapes=[pltpu.VMEM((HEADS, HD, HD), F32), pltpu.VMEM((rows, HW), F32), pltpu.VMEM((1, HW), F32),
                        pltpu.VMEM((rows, HW), F32)],
        input_output_aliases={10: 0},
        compiler_params=_params(("arbitrary",)),
    )(z, z, z, z, o, dx1, w_out, states, lb_logits, gn, dz_in)


def _in_bwd(dz, w_t, x0, g, dx1, deps):
    s = x0.shape[0]
    tm = min(WIDE_ROW_TILE, s)

    def body(dz_ref, w_ref, x_ref, g_ref, d_ref, dx_ref, slot_ref):
        _zero_slot(slot_ref)
        dx, dg = _rms_bwd(x_ref[...], g_ref[...], _mm(dz_ref[...], w_ref[...]))
        dx_ref[...] = d_ref[...] + dx
        slot_ref[0:1, :] += dg

    row_f32 = pl.BlockSpec((tm, D_MODEL), lambda i: (i, 0))
    return _call_behind(
        deps, body, name="in_bwd", grid=(s // tm,),
        out_shape=[jax.ShapeDtypeStruct((s, D_MODEL), F32), jax.ShapeDtypeStruct((SLOT, D_MODEL), F32)],
        in_specs=[pl.BlockSpec((tm, IN_WIDTH), lambda i: (i, 0)), VMEM_WHOLE, row_f32, _full((1, D_MODEL)), row_f32],
        out_specs=[row_f32, _full((SLOT, D_MODEL))],
        compiler_params=_params(("arbitrary",)),
    )(dz, w_t, x0, g, dx1)


def kernel(x, mem, norm_mix_g, w_in, lb_logits, hgrn_norm_g, w_pool, pool_scale, w_out, norm_x_g, norm_mem_g, w_xq, w_xk, w_xv, w_xo, norm_ffn_g, w_ff1, w_ff2, final_norm_g, loss_target, m_norm_mix_g, m_w_in, m_lb_logits, m_hgrn_norm_g, m_w_pool, m_pool_scale, m_w_out, m_norm_x_g, m_norm_mem_g, m_w_xq, m_w_xk, m_w_xv, m_w_xo, m_norm_ffn_g, m_w_ff1, m_w_ff2, m_final_norm_g, v_norm_mix_g, v_w_in, v_lb_logits, v_hgrn_norm_g, v_w_pool, v_pool_scale, v_w_out, v_norm_x_g, v_norm_mem_g, v_w_xq, v_w_xk, v_w_xv, v_w_xo, v_norm_ffn_g, v_w_ff1, v_w_ff2, v_final_norm_g):
    x0 = x[0]
    mem0 = mem[0]
    tgt = loss_target[0]
    gn = hgrn_norm_g[0]
    gfin = final_norm_g.reshape(1, D_MODEL)
    wp = w_pool[0]
    heads_2d = lambda w: w.reshape(D_MODEL // N_DEV, D_MODEL)
    xo_2d = lambda w: w.reshape(D_MODEL, D_MODEL // N_DEV)

    first = _all_gather_weights([w_in[0].T], [w_out[0], heads_2d(w_xq), heads_2d(w_xk), heads_2d(w_xv), xo_2d(w_xo).T,
                                              w_ff1[0], w_ff2[0]])
    win_t = first[0].reshape(IN_WIDTH, D_MODEL)
    ga_attn, ga_mlp = _gather_first_start([first[1:6], first[6:8]], "gather_first_start")

    z, h = _in_proj(x0, norm_mix_g, win_t, deps=[ga_attn[3]])
    mixed_a, o_pre, states = _hgrn_fwd(z, lb_logits, gn)
    lands = _split_wait(_gather_first_copies, ga_attn, o_pre, "gather_attn_first_wait")
    gb_attn = _gather_forward_start(lands, "gather_attn_forward_start")
    mixed, pooled = _pool_fwd(z, wp, pool_scale, mixed_a, deps=[gb_attn[3]])
    lands = _split_wait(_gather_forward_copies, gb_attn, pooled, "gather_attn_forward_wait")
    wout_f, wq_f, wk_f, wv_f, wo_t = (t.reshape(D_MODEL, D_MODEL) for t in lands)
    hm, xk, xv = _mem_kv(mem0, norm_mem_g, wk_f, wv_f, deps=[])
    x1, x2, hq, xq, att = _mix_xattn_fwd(x0, mixed, wout_f, norm_x_g, wq_f, xk, xv, wo_t, deps=[])
    lands = _split_wait(_gather_first_copies, ga_mlp, x2, "gather_mlp_first_wait")
    gb_mlp = _gather_forward_start(lands, "gather_mlp_forward_start")
    w1_b, w2_b = _split_wait(_gather_forward_copies, gb_mlp, gb_mlp[3], "gather_mlp_forward_wait")
    dx3, dx3_16, u, hf, slot_fin = _mlp_fwd_loss(x2, norm_ffn_g, w1_b, w2_b.reshape(D_FF, D_MODEL), gfin, tgt)

    rows = lambda t, r: t.reshape(N_DEV, r, D_MODEL)
    dw2 = _wgrad(u, dx3_16, "wgrad_ff2")
    ex_ff2 = _all_to_all_start([rows(dw2, FF_BLK)], [], "exchange_ff2_start")
    da, dx2, slot_ffn = _mlp_bwd(dx3, u, x2, norm_ffn_g, w1_b, w2_b, deps=[ex_ff2[3]])
    dw1 = _wgrad(hf, da, "wgrad_ff1", col_blocks=True)
    ex_ff1 = _all_to_all_start([dw1], [], "exchange_ff1_start")
    dx1, dx1_16, dxq, dxk, dxv, slot_x = _xattn_bwd(dx2, x1, norm_x_g, xq, xk, xv, wq_f, wo_t, deps=[ex_ff1[3]])
    dwo_t = _wgrad(dx2, att, "wgrad_xo")
    dwq = _wgrad(hq, dxq, "wgrad_xq")
    dwk, dwv, slot_mem = _mem_bwd(mem0, norm_mem_g, hm, dxk, dxv, wk_f, wv_f)
    ex_attn = _all_to_all_start([rows(dwq, 128), rows(dwk, 128), rows(dwv, 128), rows(dwo_t, 128)], [],
                                "exchange_attn_start")
    dwout = _wgrad(mixed, dx1_16, "wgrad_out")
    dz_pool, d_wpool, slot_ps = _pool_bwd(dx1_16, wout_f, pooled, wp, pool_scale, deps=[ex_attn[3]])
    small0 = jnp.concatenate([slot_x, slot_mem, slot_ffn, slot_fin, slot_ps], axis=0)
    ex_out = _all_to_all_start([rows(dwout, 128)], [small0, d_wpool], "exchange_out_start")
    dz, slot_lb, slot_gn = _hgrn_bwd(z, o_pre, dx1_16, wout_f, states, lb_logits, gn, dz_pool, deps=[ex_out[3]])
    dwin_t = _wgrad(dz, h, "wgrad_in")
    small1 = jnp.concatenate([slot_lb, slot_gn], axis=0)
    ex_in = _all_to_all_start([rows(dwin_t, 320)], [small1], "exchange_in_start")
    grad_x, slot_mix = _in_bwd(dz, win_t, x0, norm_mix_g, dx1, deps=[ex_in[3]])
    ex_mix = _all_to_all_start([], [slot_mix], "exchange_mix_start")

    out = {}
    (r_2,) = _split_wait(_all_to_all_copies(1), ex_ff2, ex_mix[3], "exchange_ff2_wait")
    out["w_ff2"] = _sum_adamw(r_2, w_ff2[0], m_w_ff2[0], v_w_ff2[0], "adamw_ff2")
    (r_1,) = _split_wait(_all_to_all_copies(1), ex_ff1, out["w_ff2"][1], "exchange_ff1_wait")
    out["w_ff1"] = _sum_adamw(r_1, w_ff1[0], m_w_ff1[0], v_w_ff1[0], "adamw_ff1")
    r_q, r_k, r_v, r_o = _split_wait(_all_to_all_copies(4), ex_attn, out["w_ff1"][1], "exchange_attn_wait")
    sums = _sum_sources_whole([r_q, r_k, r_v, r_o], "sum_grad_attn")
    g_attn = [g.reshape(w_xq.shape) for g in sums[:3]] + [sums[3].T]
    attn = _adamw_whole([(g_attn[0], w_xq, m_w_xq, v_w_xq), (g_attn[1], w_xk, m_w_xk, v_w_xk),
                         (g_attn[2], w_xv, m_w_xv, v_w_xv),
                         (g_attn[3], xo_2d(w_xo), xo_2d(m_w_xo), xo_2d(v_w_xo))], "adamw_attn")
    for n, g, res in zip(("w_xq", "w_xk", "w_xv", "w_xo"), g_attn, attn):
        out[n] = (g, *res)
    r_out, r_small0, r_wpool = _split_wait(_all_to_all_copies(1), ex_out, attn[3][0], "exchange_out_wait")
    out["w_out"] = _sum_adamw(r_out, w_out[0], m_w_out[0], v_w_out[0], "adamw_out")
    r_in, r_small1 = _split_wait(_all_to_all_copies(1), ex_in, out["w_out"][1], "exchange_in_wait")
    in_t = _sum_adamw(r_in, w_in[0].T, m_w_in[0].T, v_w_in[0].T, "adamw_in")
    out["w_in"] = tuple(t.T for t in in_t)
    (r_small2,) = _split_wait(_all_to_all_copies(0), ex_mix, in_t[1], "exchange_mix_wait")
    row = lambda t: t.reshape(1, -1)
    small_params = {
        "norm_mix_g": (norm_mix_g, m_norm_mix_g, v_norm_mix_g),
        "lb_logits": (lb_logits, m_lb_logits, v_lb_logits),
        "hgrn_norm_g": (hgrn_norm_g[0], m_hgrn_norm_g[0], v_hgrn_norm_g[0]),
        "pool_scale": (pool_scale, m_pool_scale, v_pool_scale),
        "norm_x_g": (norm_x_g, m_norm_x_g, v_norm_x_g),
        "norm_mem_g": (norm_mem_g, m_norm_mem_g, v_norm_mem_g),
        "norm_ffn_g": (norm_ffn_g, m_norm_ffn_g, v_norm_ffn_g),
        "final_norm_g": (row(final_norm_g), row(m_final_norm_g), row(v_final_norm_g)),
        "w_pool": (wp, m_w_pool[0], v_w_pool[0]),
    }
    loss, small_out = _small_update([r_small0, r_small1, r_small2], r_wpool, small_params)
    out.update(small_out)

    shapes = dict(norm_mix_g=norm_mix_g, w_in=w_in, lb_logits=lb_logits, hgrn_norm_g=hgrn_norm_g, w_pool=w_pool,
                  pool_scale=pool_scale, w_out=w_out, norm_x_g=norm_x_g, norm_mem_g=norm_mem_g, w_xq=w_xq, w_xk=w_xk,
                  w_xv=w_xv, w_xo=w_xo, norm_ffn_g=norm_ffn_g, w_ff1=w_ff1, w_ff2=w_ff2, final_norm_g=final_norm_g)
    order = list(shapes)
    group = lambda k: [out[n][k].reshape(shapes[n].shape) for n in order]
    return (loss.reshape(()), grad_x.reshape(x.shape), *group(0), *group(1), *group(2), *group(3))
```

```python
import jax
import jax.numpy as jnp
from jax import lax
from jax.experimental import pallas as pl
from jax.experimental.pallas import tpu as pltpu

F32 = jnp.float32
BF16 = jnp.bfloat16

D_MODEL = 1024
N_DEV = 8
HEADS = 4
HD = 128
HW = HEADS * HD
IN_WIDTH = 5 * HW
XHD = 256
MEM_LEN = 256
D_FF = 4096
FF_BLK = D_FF // N_DEV
POOL_WINDOWS = (2, 4, 8, 16)
POOL_HALO = 16
CHUNK = 64
CHUNKS_PER_STEP = 8
SUB = 16
N_SUB = CHUNK // SUB
EXP_CAP = 80.0
EPS = 1e-6
TINY = 1e-30
ROW_TILE = 512
WIDE_ROW_TILE = 1024
SLOT = 8
V7X_VMEM_LIMIT = 56 * 1024 * 1024

ADAM_LR = 0.001
ADAM_B1 = 0.9
ADAM_B2 = 0.999
ADAM_EPS = 1e-08
ADAM_WD = 0.01
ADAM_STEP = 10

MESH_ID = pl.DeviceIdType.MESH


def _params(sem=None, vmem=V7X_VMEM_LIMIT):
    return pltpu.CompilerParams(dimension_semantics=sem, vmem_limit_bytes=vmem)


def _mm(a, b):
    return lax.dot_general(a.astype(BF16), b.astype(BF16), (((1,), (0,)), ((), ())), preferred_element_type=F32)


def _mm_nt(a, b):
    return lax.dot_general(a.astype(BF16), b.astype(BF16), (((1,), (1,)), ((), ())), preferred_element_type=F32)


def _mm_tn(a, b):
    return lax.dot_general(a.astype(BF16), b.astype(BF16), (((0,), (0,)), ((), ())), preferred_element_type=F32)


def _sigmoid(x):
    return 1.0 / (1.0 + jnp.exp(-x))


def _rms(x):
    return lax.rsqrt(jnp.mean(x * x, axis=-1, keepdims=True) + EPS)


def _rms_bwd(x, g, dh):
    r = _rms(x)
    n = x * r
    dn = dh * g
    dx = r * (dn - n * jnp.mean(dn * n, axis=-1, keepdims=True))
    return dx, jnp.sum(dh * n, axis=0, keepdims=True)


def _tri_dot(tri, x, passes):
    acc = None
    rest = x
    for _ in range(passes):
        piece = rest.astype(BF16)
        part = lax.dot_general(tri, piece, (((1,), (0,)), ((), ())), preferred_element_type=F32)
        acc = part if acc is None else acc + part
        rest = rest - piece.astype(F32)
    return acc


def _adam_update(g, w, m, v):
    nm = ADAM_B1 * m + (1.0 - ADAM_B1) * g
    nv = ADAM_B2 * v + (1.0 - ADAM_B2) * (g * g)
    m_hat = nm / (1.0 - ADAM_B1 ** ADAM_STEP)
    v_hat = nv / (1.0 - ADAM_B2 ** ADAM_STEP)
    return -ADAM_LR * (m_hat / (jnp.sqrt(v_hat) + ADAM_EPS) + ADAM_WD * w), nm, nv


def _full(shape):
    return pl.BlockSpec(shape, lambda *_: (0,) * len(shape))


VMEM_WHOLE = pl.BlockSpec(memory_space=pltpu.VMEM)
ANY_SPACE = pl.BlockSpec(memory_space=pl.ANY)


def _mesh_pos():
    return lax.axis_index("x"), lax.axis_index("y"), lax.axis_index("c")


def _flat(px, py, pc):
    return 4 * px + 2 * py + pc


def _all_gather_weights(shards, cast_only):
    n, nc = len(shards), len(cast_only)
    step = 64

    def body(*refs):
        x_refs, c_refs = refs[:n], refs[n:n + nc]
        out_refs, cast_refs = refs[n + nc:2 * n + nc], refs[2 * n + nc:2 * n + 2 * nc]
        bufs = refs[2 * n + 2 * nc:3 * n + 2 * nc]
        send_sems, recv_sems, local_sems = refs[3 * n + 2 * nc:]
        x, y, c = _mesh_pos()
        me, sibling = (x, y, c), (x, y, 1 - c)
        chips = [(1 - x, y), (x, 1 - y), (1 - x, 1 - y)]

        def copy(a, k, blk, to, src=None):
            rows = out_refs[a].at[_flat(*blk)]
            return pltpu.make_async_remote_copy(
                src_ref=rows if src is None else src, dst_ref=rows,
                send_sem=send_sems.at[7 * a + k], recv_sem=recv_sems.at[7 * a + k], device_id=to, device_id_type=MESH_ID)

        def cast_rows(src, dst, rows):
            def cast(i, carry):
                r0 = pl.multiple_of(i * step, step)
                dst[pl.ds(r0, step), :] = src[pl.ds(r0, step), :].astype(BF16)
                return carry
            lax.fori_loop(0, rows // step, cast, 0)

        first, mine = [], []
        for a in range(n):
            cast_rows(x_refs[a], bufs[a], shards[a].shape[0])
            mine.append(pltpu.make_async_copy(bufs[a], out_refs[a].at[_flat(*me)], local_sems.at[a]))
            first.append(copy(a, 0, me, sibling, src=bufs[a]))
            first += [copy(a, 1 + j, me, (*chip, c), src=bufs[a]) for j, chip in enumerate(chips)]
            for cp in [mine[-1]] + first[-4:]:
                cp.start()
        for a in range(nc):
            cast_rows(c_refs[a], cast_refs[a], cast_only[a].shape[0])
        passed = []
        for j, chip in enumerate(chips):
            for a in range(n):
                copy(a, 1 + j, (*chip, c), me).wait_recv()
                passed.append(copy(a, 4 + j, (*chip, c), sibling))
                passed[-1].start()
        for a in range(n):
            copy(a, 0, sibling, me).wait_recv()
            for j, chip in enumerate(chips):
                copy(a, 4 + j, (*chip, 1 - c), me).wait_recv()
        for cp in first + passed:
            cp.wait_send()
        for cp in mine:
            cp.wait()

    return pl.pallas_call(
        body, name="all_gather_w_in",
        out_shape=[jax.ShapeDtypeStruct((N_DEV,) + s.shape, BF16) for s in shards]
        + [jax.ShapeDtypeStruct(s.shape, BF16) for s in cast_only],
        in_specs=[VMEM_WHOLE] * (n + nc), out_specs=[ANY_SPACE] * n + [VMEM_WHOLE] * nc,
        scratch_shapes=[pltpu.VMEM(s.shape, BF16) for s in shards]
        + [pltpu.SemaphoreType.DMA((7 * n,)), pltpu.SemaphoreType.DMA((7 * n,)), pltpu.SemaphoreType.DMA((n,))],
        compiler_params=_params(),
    )(*shards, *cast_only)


HBM_SPEC = pl.BlockSpec(memory_space=pltpu.HBM)
SEM_SPEC = pl.BlockSpec(memory_space=pltpu.SEMAPHORE)
EFFECT = pltpu.SideEffectType.DATAFLOW_SIDE_EFFECTING
TOKEN = jax.ShapeDtypeStruct((8, 128), F32)


def _in_hbm(a):
    return pltpu.with_memory_space_constraint(a, pltpu.HBM)


def _split_start(copies_of, srcs, lands, n_sems, name):
    ns, nl, k = len(srcs), len(lands), len(n_sems)

    def body(*refs):
        src_refs, land_refs = refs[:ns], refs[ns:ns + nl]
        sems = refs[ns + nl:ns + nl + k]
        token = refs[-1]
        for cp in copies_of(src_refs, land_refs, sems):
            cp.start()
        token[...] = jnp.zeros_like(token)

    outs = pl.pallas_call(
        body, name=name,
        out_shape=[pltpu.SemaphoreType.DMA((q,)) for q in n_sems]
        + [pltpu.HBM(a.shape, a.dtype) for a in list(srcs) + list(lands)] + [TOKEN],
        in_specs=[HBM_SPEC] * (ns + nl),
        out_specs=[SEM_SPEC] * k + [HBM_SPEC] * (ns + nl) + [VMEM_WHOLE],
        input_output_aliases={i: k + i for i in range(ns + nl)},
        compiler_params=pltpu.CompilerParams(has_side_effects=EFFECT),
    )(*[_in_hbm(a) for a in list(srcs) + list(lands)])
    return outs[:k], outs[k:k + ns], outs[k + ns:k + ns + nl], outs[-1]


def _split_wait(copies_of, handle, after, name):
    sems, srcs, lands, _ = handle
    ns, nl, k = len(srcs), len(lands), len(sems)

    def body(*refs):
        src_refs, land_refs = refs[:ns], refs[ns:ns + nl]
        sem_refs = refs[ns + nl:ns + nl + k]
        for cp in copies_of(src_refs, land_refs, sem_refs):
            cp.wait()

    outs = pl.pallas_call(
        body, name=name,
        out_shape=[pltpu.HBM(a.shape, a.dtype) for a in list(srcs) + list(lands)],
        in_specs=[HBM_SPEC] * (ns + nl) + [SEM_SPEC] * k + [ANY_SPACE],
        out_specs=[HBM_SPEC] * (ns + nl),
        input_output_aliases={i: i for i in range(ns + nl)},
        compiler_params=pltpu.CompilerParams(has_side_effects=EFFECT),
    )(*srcs, *lands, *sems, after)
    return outs[ns:]


def _gather_first_copies(shard_refs, land_refs, sems):
    send_sems, recv_sems, local_sems = sems
    x, y, c = _mesh_pos()
    me = _flat(x, y, c)
    peers = [(x, y, 1 - c), (1 - x, y, c), (x, 1 - y, c), (1 - x, 1 - y, c)]
    copies = []
    for a, (shard, land) in enumerate(zip(shard_refs, land_refs)):
        copies.append(pltpu.make_async_copy(shard, land.at[me], local_sems.at[a]))
        for k, peer in enumerate(peers):
            copies.append(pltpu.make_async_remote_copy(
                src_ref=shard, dst_ref=land.at[me], send_sem=send_sems.at[4 * a + k], recv_sem=recv_sems.at[4 * a + k],
                device_id=peer, device_id_type=MESH_ID))
    return copies


def _gather_forward_copies(src_refs, land_refs, sems):
    del src_refs
    send_sems, recv_sems = sems
    x, y, c = _mesh_pos()
    chips = [(1 - x, y), (x, 1 - y), (1 - x, 1 - y)]
    copies = []
    for a, land in enumerate(land_refs):
        for j, chip in enumerate(chips):
            rows = land.at[_flat(*chip, c)]
            copies.append(pltpu.make_async_remote_copy(
                src_ref=rows, dst_ref=rows, send_sem=send_sems.at[3 * a + j], recv_sem=recv_sems.at[3 * a + j],
                device_id=(x, y, 1 - c), device_id_type=MESH_ID))
    return copies


def _gather_first_start(groups, name):
    shards = [s for g in groups for s in g]
    lands = [lax.empty((N_DEV,) + s.shape, s.dtype) for s in shards]
    bounds = [sum(len(g) for g in groups[:i]) for i in range(len(groups) + 1)]

    def copies_of(src_refs, land_refs, sems):
        copies = []
        for i in range(len(groups)):
            lo, hi = bounds[i], bounds[i + 1]
            copies += _gather_first_copies(src_refs[lo:hi], land_refs[lo:hi], sems[3 * i:3 * i + 3])
        return copies

    n_sems = tuple(q for g in groups for q in (4 * len(g), 4 * len(g), len(g)))
    sems, srcs, lands, token = _split_start(copies_of, shards, lands, n_sems, name)
    return [(sems[3 * i:3 * i + 3], srcs[bounds[i]:bounds[i + 1]], lands[bounds[i]:bounds[i + 1]], token)
            for i in range(len(groups))]


def _gather_forward_start(lands, name):
    n = len(lands)
    return _split_start(_gather_forward_copies, [], lands, (3 * n, 3 * n), name)


def _all_to_all_copies(n_scattered):
    def copies_of(src_refs, land_refs, sems):
        send_sems, recv_sems, local_sems = sems
        x, y, c = _mesh_pos()
        me = _flat(x, y, c)
        copies = []
        for a, (src, land) in enumerate(zip(src_refs, land_refs)):
            scattered = a < n_scattered
            copies.append(pltpu.make_async_copy(src.at[me] if scattered else src, land.at[me], local_sems.at[a]))
            for k in range(1, N_DEV):
                peer = (1 - x if k & 4 else x, 1 - y if k & 2 else y, 1 - c if k & 1 else c)
                copies.append(pltpu.make_async_remote_copy(
                    src_ref=src.at[_flat(*peer)] if scattered else src, dst_ref=land.at[me],
                    send_sem=send_sems.at[7 * a + k - 1], recv_sem=recv_sems.at[7 * a + k - 1],
                    device_id=peer, device_id_type=MESH_ID))
        return copies
    return copies_of


def _all_to_all_start(scattered, broadcast, name):
    srcs = list(scattered) + list(broadcast)
    lands = [lax.empty(a.shape, a.dtype) for a in scattered] + [lax.empty((N_DEV,) + a.shape, a.dtype) for a in broadcast]
    n = len(srcs)
    return _split_start(_all_to_all_copies(len(scattered)), srcs, lands, (7 * n, 7 * n, n), name)


def _call_behind(deps, body, *, in_specs, **kwargs):
    n_in, n_dep = len(in_specs), len(deps)

    def body_without_deps(*refs):
        return body(*refs[:n_in], *refs[n_in + n_dep:])

    call = pl.pallas_call(body_without_deps, in_specs=list(in_specs) + [ANY_SPACE] * n_dep, **kwargs)
    return lambda *operands: call(*operands, *deps)


def _row_tile(rows):
    for cand in (256, 128, 64, 32, 16):
        if rows % cand == 0:
            return cand
    return rows


def _adamw_whole(groups, name):
    n = len(groups)

    def body(*refs):
        for i in range(n):
            g_ref, w_ref, m_ref, v_ref = refs[4 * i:4 * i + 4]
            d_ref, nm_ref, nv_ref = refs[4 * n + 3 * i:4 * n + 3 * i + 3]
            d_ref[...], nm_ref[...], nv_ref[...] = _adam_update(g_ref[...], w_ref[...], m_ref[...], v_ref[...])

    outs = pl.pallas_call(
        body, name=name, out_shape=[jax.ShapeDtypeStruct(grp[0].shape, F32) for grp in groups for _ in range(3)],
        in_specs=[VMEM_WHOLE] * (4 * n), out_specs=[VMEM_WHOLE] * (3 * n),
        compiler_params=_params(),
    )(*[t for grp in groups for t in grp])
    return [outs[3 * i:3 * i + 3] for i in range(n)]


def _sum_sources_whole(recvs, name):
    n = len(recvs)

    def body(*refs):
        for r_ref, o_ref in zip(refs[:n], refs[n:]):
            acc = r_ref[0].astype(F32)
            for d in range(1, N_DEV):
                acc = acc + r_ref[d].astype(F32)
            o_ref[...] = acc

    return pl.pallas_call(
        body, name=name, out_shape=[jax.ShapeDtypeStruct(r.shape[1:], F32) for r in recvs],
        in_specs=[VMEM_WHOLE] * n, out_specs=[VMEM_WHOLE] * n,
        compiler_params=_params(),
    )(*recvs)


def _sum_adamw(recv, w, m, v, name):
    _, rows, cols = recv.shape
    tile = _row_tile(rows)

    def body(r_ref, w_ref, m_ref, v_ref, g_ref, d_ref, nm_ref, nv_ref):
        acc = r_ref[0].astype(F32)
        for d in range(1, N_DEV):
            acc = acc + r_ref[d].astype(F32)
        g_ref[...] = acc
        d_ref[...], nm_ref[...], nv_ref[...] = _adam_update(acc, w_ref[...], m_ref[...], v_ref[...])

    spec = pl.BlockSpec((tile, cols), lambda i: (i, 0))
    shp = jax.ShapeDtypeStruct((rows, cols), F32)
    return pl.pallas_call(
        body, name=name, grid=(rows // tile,), out_shape=[shp] * 4,
        in_specs=[pl.BlockSpec((N_DEV, tile, cols), lambda i: (0, i, 0)), spec, spec, spec], out_specs=[spec] * 4,
        compiler_params=_params(("parallel",)),
    )(recv, w, m, v)


SMALL_SLOTS = {"norm_x_g": (0, 0, 1, D_MODEL), "norm_mem_g": (0, 8, 1, D_MODEL), "norm_ffn_g": (0, 16, 1, D_MODEL),
               "final_norm_g": (0, 24, 1, D_MODEL), "pool_scale": (0, 32, 1, HW),
               "lb_logits": (1, 0, 2, HW), "hgrn_norm_g": (1, 8, HEADS, HD), "norm_mix_g": (2, 0, 1, D_MODEL)}
LOSS_ROW = 25
SMALL_ORDER = ("norm_mix_g", "lb_logits", "hgrn_norm_g", "pool_scale", "norm_x_g", "norm_mem_g", "norm_ffn_g",
               "final_norm_g", "w_pool")


def _small_update(srecvs, wprecv, params):
    flat = [t for n in SMALL_ORDER for t in params[n]]
    nb = len(srecvs)
    n_in = nb + 1 + len(flat)

    def body(*refs):
        s_refs, wp_ref = refs[0:nb], refs[nb]
        in_refs = refs[nb + 1:n_in]
        loss_ref = refs[n_in]
        out_refs = refs[n_in + 1:-nb]
        accs = refs[-nb:]
        for s_ref, acc in zip(s_refs, accs):
            total = s_ref[0]
            for d in range(1, N_DEV):
                total = total + s_ref[d]
            acc[...] = total
        loss_ref[...] = accs[0][LOSS_ROW:LOSS_ROW + 1, 0:1]
        for i, name in enumerate(SMALL_ORDER):
            w_ref, m_ref, v_ref = in_refs[3 * i:3 * i + 3]
            g_ref, d_ref, nm_ref, nv_ref = out_refs[4 * i:4 * i + 4]
            if name == "w_pool":
                g = wp_ref[0]
                for d in range(1, N_DEV):
                    g = g + wp_ref[d]
            else:
                buf, r0, nr, nc = SMALL_SLOTS[name]
                g = accs[buf][r0:r0 + nr, 0:nc]
            g_ref[...] = g
            d_ref[...], nm_ref[...], nv_ref[...] = _adam_update(g, w_ref[...], m_ref[...], v_ref[...])

    out_shape = [jax.ShapeDtypeStruct((1, 1), F32)]
    for n in SMALL_ORDER:
        out_shape += [jax.ShapeDtypeStruct(params[n][0].shape, F32)] * 4
    outs = pl.pallas_call(
        body, name="small_update", out_shape=out_shape,
        in_specs=[VMEM_WHOLE] * n_in, out_specs=[VMEM_WHOLE] * len(out_shape),
        scratch_shapes=[pltpu.VMEM(r.shape[1:], F32) for r in srecvs],
        compiler_params=_params(),
    )(*srecvs, wprecv, *flat)
    return outs[0], {n: outs[1 + 4 * i:5 + 4 * i] for i, n in enumerate(SMALL_ORDER)}


def _in_proj(x, g, w_t, deps):
    s = x.shape[0]
    tm = min(ROW_TILE, s)

    def body(x_ref, g_ref, w_ref, z_ref, h_ref):
        xv = x_ref[...]
        h = (xv * _rms(xv) * g_ref[...]).astype(BF16)
        h_ref[...] = h
        z_ref[...] = _mm_nt(h, w_ref[...])

    return _call_behind(
        deps, body, name="in_proj", grid=(s // tm,),
        out_shape=[jax.ShapeDtypeStruct((s, IN_WIDTH), F32), jax.ShapeDtypeStruct((s, D_MODEL), BF16)],
        in_specs=[pl.BlockSpec((tm, D_MODEL), lambda i: (i, 0)), _full((1, D_MODEL)), VMEM_WHOLE],
        out_specs=[pl.BlockSpec((tm, IN_WIDTH), lambda i: (i, 0)), pl.BlockSpec((tm, D_MODEL), lambda i: (i, 0))],
        compiler_params=_params(("parallel",)),
    )(x, g, w_t)


def _chunk_masks():
    row = lax.broadcasted_iota(jnp.int32, (CHUNK, CHUNK), 0)
    col = lax.broadcasted_iota(jnp.int32, (CHUNK, CHUNK), 1)
    return row, col


def _ones_where(mask):
    return jnp.where(mask, 1.0, 0.0).astype(BF16)


def _hgrn_gates(zq, zf, lb):
    sq = _sigmoid(zq)
    sig = _sigmoid(zf)
    f = lb + (1.0 - lb) * sig
    return zq * sq, sq, sig, f


def _sub_chunk_masks(width):
    trow = lax.broadcasted_iota(jnp.int32, (CHUNK, width), 0)
    return [(trow >= SUB * j) & (trow < SUB * (j + 1)) for j in range(N_SUB)]


def _head(a, h):
    return a[:, HD * h:HD * (h + 1)]


def _lanes(parts):
    return jnp.concatenate(parts, axis=1)


def _hgrn_decay_factors(b_scr, r0, b, in_sub):
    bases = [jnp.zeros((1, HW), F32)] + [b_scr[r0 + SUB * j - 1:r0 + SUB * j, :] for j in range(1, N_SUB)]
    own_base = bases[N_SUB - 1]
    for j in range(N_SUB - 2, -1, -1):
        own_base = jnp.where(in_sub[j], bases[j], own_base)
    eq = jnp.exp(b - own_base)
    ek = []
    for j in range(N_SUB):
        upto = SUB * (j + 1)
        e = jnp.exp(jnp.minimum(bases[j] - b[0:upto], EXP_CAP))
        ek.append(e if upto == CHUNK else jnp.concatenate([e, jnp.zeros((CHUNK - upto, HW), F32)], axis=0))
    return eq, ek


def _per_sub_chunk(x, in_sub):
    return _lanes([jnp.where(in_sub[j], x, 0.0) for j in range(N_SUB)])


def _own_lane_block(a, in_sub):
    out = a[:, HD * (N_SUB - 1):HD * N_SUB]
    for j in range(N_SUB - 2, -1, -1):
        out = jnp.where(in_sub[j], a[:, HD * j:HD * (j + 1)], out)
    return out


def _head_rms(o):
    return _lanes([jnp.broadcast_to(_rms(_head(o, h)), (CHUNK, HD)) for h in range(HEADS)])


def _head_mean(a):
    return _lanes([jnp.broadcast_to(jnp.mean(_head(a, h), axis=-1, keepdims=True), (CHUNK, HD)) for h in range(HEADS)])


def _hgrn_fwd(z, lb_logits, gn):
    s = z.shape[0]
    n_chunks = s // CHUNK

    def body(zq_ref, zf_ref, zi_ref, zg_ref, lbl_ref, gn_ref, oa_ref, o_ref, st_ref, state, b_scr):
        @pl.when(pl.program_id(0) == 0)
        def _():
            state[...] = jnp.zeros_like(state)

        lb = _sigmoid(lbl_ref[0:1, :] - lbl_ref[1:2, :])
        row, col = _chunk_masks()
        causal = col <= row
        tri = _ones_where(causal)
        in_sub, in_sub_head = _sub_chunk_masks(HW), _sub_chunk_masks(HD)
        gn_row = _lanes([gn_ref[h:h + 1, :] for h in range(HEADS)])
        def front(c):
            r0 = CHUNK * c
            rs = slice(r0, r0 + CHUNK)
            q, _, _, f = _hgrn_gates(zq_ref[rs, :], zf_ref[rs, :], lb)
            kk = 1.0 - f
            b = _tri_dot(tri, jnp.log(f), 3)
            b_scr[rs, :] = b
            eq, ek = _hgrn_decay_factors(b_scr, r0, b, in_sub)
            b_last = b_scr[r0 + CHUNK - 1:r0 + CHUNK, :]
            qe = q * eq
            return {"rs": rs, "v": zi_ref[rs, :], "qg": q * jnp.exp(b), "kd": kk * jnp.exp(b_last - b),
                    "lam_last": jnp.exp(b_last),
                    "q16": [_per_sub_chunk(_head(qe, h), in_sub_head).astype(BF16) for h in range(HEADS)],
                    "ke16": [_lanes([_head(kk * e, h) for e in ek]).astype(BF16) for h in range(HEADS)]}

        def recurrence(c, p):
            st_ref[c] = state[...]
            a, o_inter = [], []
            for h in range(HEADS):
                vh, st = _head(p["v"], h), state[h]
                a.append(jnp.where(causal, _mm_nt(p["q16"][h], p["ke16"][h]), 0.0))
                o_inter.append(_mm_nt(_head(p["qg"], h), st))
                state[h] = st * _head(p["lam_last"], h) + _mm_tn(vh, _head(p["kd"], h))
            return _lanes([_mm(a[h], _head(p["v"], h)) + o_inter[h] for h in range(HEADS)])

        def back(p, o):
            rs = p["rs"]
            o_ref[rs, :] = o
            zg = zg_ref[rs, :]
            oa_ref[rs, :] = (o * _head_rms(o) * gn_row * zg * _sigmoid(zg)).astype(BF16)

        p = front(0)
        for c in range(CHUNKS_PER_STEP):
            o = recurrence(c, p)
            p_next = front(c + 1) if c + 1 < CHUNKS_PER_STEP else None
            back(p, o)
            p = p_next

    rows = CHUNK * CHUNKS_PER_STEP
    zspec = lambda cb: pl.BlockSpec((rows, HW), lambda i, cb=cb: (i, cb))
    return pl.pallas_call(
        body, name="hgrn_fwd", grid=(s // rows,),
        out_shape=[jax.ShapeDtypeStruct((s, 2 * HW), BF16), jax.ShapeDtypeStruct((s, HW), F32),
                   jax.ShapeDtypeStruct((n_chunks, HEADS, HD, HD), F32)],
        in_specs=[zspec(0), zspec(1), zspec(2), zspec(3), _full((2, HW)), _full((HEADS, HD))],
        out_specs=[pl.BlockSpec((rows, HW), lambda i: (i, 0)), pl.BlockSpec((rows, HW), lambda i: (i, 0)),
                   pl.BlockSpec((CHUNKS_PER_STEP, HEADS, HD, HD), lambda i: (i, 0, 0, 0))],
        scratch_shapes=[pltpu.VMEM((HEADS, HD, HD), F32), pltpu.VMEM((rows, HW), F32)],
        compiler_params=_params(("arbitrary",)),
    )(z, z, z, z, lb_logits, gn)


def _pool_counts(tile_idx, tm):
    t = tile_idx * tm + lax.broadcasted_iota(jnp.int32, (tm, 1), 0)
    return [1.0 / jnp.minimum(t + 1, w).astype(F32) for w in POOL_WINDOWS]


def _pool_fwd(z, w_pool, scale, mixed_in, deps):
    s = z.shape[0]
    tm = min(ROW_TILE, s)

    def body(p_ref, w_ref, sc_ref, mixin_ref, ob_ref, pooled_ref, ext):
        i = pl.program_id(0)

        @pl.when(i == 0)
        def _():
            ext[0:POOL_HALO, :] = jnp.zeros((POOL_HALO, HW), F32)

        @pl.when(i > 0)
        def _():
            ext[0:POOL_HALO, :] = ext[tm:tm + POOL_HALO, :]

        ext[POOL_HALO:POOL_HALO + tm, :] = p_ref[...]
        inv = _pool_counts(i, tm)
        for g, w in enumerate(POOL_WINDOWS):
            sl = slice(HD * g, HD * (g + 1))
            p = ext[POOL_HALO:POOL_HALO + tm, sl]
            win = p
            for d in range(1, w):
                win = win + ext[POOL_HALO - d:POOL_HALO - d + tm, sl]
            pooled = (win * inv[g] - p).astype(BF16)
            pooled_ref[:, sl] = pooled
            ob_ref[:, sl] = (_mm(pooled, w_ref[g]) * sc_ref[:, sl]).astype(BF16)

    return _call_behind(
        deps, body, name="pool_fwd", grid=(s // tm,),
        out_shape=[jax.ShapeDtypeStruct((s, 2 * HW), BF16), jax.ShapeDtypeStruct((s, HW), BF16)],
        in_specs=[pl.BlockSpec((tm, HW), lambda i: (i, 4)), _full((HEADS, HD, HD)), _full((1, HW)), ANY_SPACE],
        out_specs=[pl.BlockSpec((tm, HW), lambda i: (i, 1)), pl.BlockSpec((tm, HW), lambda i: (i, 0))],
        scratch_shapes=[pltpu.VMEM((tm + POOL_HALO, HW), F32)],
        input_output_aliases={3: 0},
        compiler_params=_params(("arbitrary",)),
    )(z, w_pool, scale, mixed_in)


def _mem_kv(mem, g, wk, wv, deps):
    def body(m_ref, g_ref, wk_ref, wv_ref, hm_ref, k_ref, v_ref):
        m = m_ref[...]
        hm = (m * _rms(m) * g_ref[...]).astype(BF16)
        hm_ref[...] = hm
        k_ref[...] = _mm(hm, wk_ref[...]).astype(BF16)
        v_ref[...] = _mm(hm, wv_ref[...]).astype(BF16)

    shp = jax.ShapeDtypeStruct((MEM_LEN, D_MODEL), BF16)
    return _call_behind(
        deps, body, name="mem_kv", out_shape=[shp, shp, shp],
        in_specs=[VMEM_WHOLE] * 4, out_specs=[VMEM_WHOLE] * 3,
        compiler_params=_params(),
    )(mem, g, wk, wv)


def _softmax_rows(sc):
    e = jnp.exp(sc - jnp.max(sc, axis=-1, keepdims=True))
    return e / jnp.sum(e, axis=-1, keepdims=True)


def _mix_xattn_fwd(x0, mixed, w_out, g, wq, xk, xv, wo_t, deps):
    s = x0.shape[0]
    tm = min(ROW_TILE, s)
    scale = XHD ** -0.5

    def body(x_ref, mix_ref, wout_ref, g_ref, wq_ref, k_ref, v_ref, wo_ref, x1_ref, o_ref, hq_ref, q_ref, att_ref):
        xv_ = x_ref[...] + _mm(mix_ref[...], wout_ref[...])
        x1_ref[...] = xv_
        hq = (xv_ * _rms(xv_) * g_ref[...]).astype(BF16)
        hq_ref[...] = hq
        q_ref[...] = (_mm(hq, wq_ref[...]) * scale).astype(BF16)
        heads = [slice(XHD * h, XHD * (h + 1)) for h in range(HEADS)]
        scores = [_mm_nt(q_ref[:, sl], k_ref[:, sl]) for sl in heads]
        probs = [_softmax_rows(sc) for sc in scores]
        for sl, p in zip(heads, probs):
            att_ref[:, sl] = _mm(p, v_ref[:, sl]).astype(BF16)
        o_ref[...] = xv_ + _mm_nt(att_ref[...], wo_ref[...])

    row_f32 = pl.BlockSpec((tm, D_MODEL), lambda i: (i, 0))
    bshape = jax.ShapeDtypeStruct((s, D_MODEL), BF16)
    fshape = jax.ShapeDtypeStruct((s, D_MODEL), F32)
    return _call_behind(
        deps, body, name="mix_xattn_fwd", grid=(s // tm,),
        out_shape=[fshape, fshape, bshape, bshape, bshape],
        in_specs=[row_f32, row_f32, VMEM_WHOLE, _full((1, D_MODEL)), VMEM_WHOLE, VMEM_WHOLE, VMEM_WHOLE, VMEM_WHOLE],
        out_specs=[row_f32] * 5,
        compiler_params=_params(("parallel",)),
    )(x0, mixed, w_out, g, wq, xk, xv, wo_t)


def _mlp_fwd_loss(x, g, w1, w2, gf, target):
    s = x.shape[0]
    tm = min(ROW_TILE, s)

    def body(x_ref, g_ref, w1_hbm, w2_hbm, gf_ref, t_ref, dx_ref, dx16_ref, u_ref, hf_ref, slot_ref, w1_ref, w2_ref, wsem):
        def w1_copy(j):
            return pltpu.make_async_copy(w1_hbm.at[j], w1_ref.at[j], wsem.at[j])

        def w2_copy(j):
            rows = pl.ds(FF_BLK * j, FF_BLK)
            return pltpu.make_async_copy(w2_hbm.at[rows], w2_ref.at[rows], wsem.at[N_DEV + j])

        def tile(first):
            xv = x_ref[...]
            hf = (xv * _rms(xv) * g_ref[...]).astype(BF16)
            hf_ref[...] = hf
            if first:
                w1_copy(0).wait()
            a_next = _mm(hf, w1_ref[0])
            for j in range(N_DEV):
                a = jnp.maximum(a_next, 0.0)
                if j + 1 < N_DEV:
                    if first:
                        w1_copy(j + 1).wait()
                    a_next = _mm(hf, w1_ref[j + 1])
                u_ref[:, FF_BLK * j:FF_BLK * (j + 1)] = (a * a).astype(BF16)
            if first:
                for j in range(N_DEV):
                    w2_copy(j).wait()
            acc = xv + _mm(u_ref[...], w2_ref[...])
            gfv = gf_ref[...]
            r = _rms(acc)
            n = acc * r
            err = n * gfv - t_ref[...]
            slot_ref[1:2, :] += jnp.sum(jnp.mean(err * err, axis=-1, keepdims=True), axis=0, keepdims=True) * 0.5
            dy = err * (1.0 / D_MODEL)
            slot_ref[0:1, :] += jnp.sum(dy * n, axis=0, keepdims=True)
            dn = dy * gfv
            dx = r * (dn - n * jnp.mean(dn * n, axis=-1, keepdims=True))
            dx_ref[...] = dx
            dx16_ref[...] = dx.astype(BF16)

        @pl.when(pl.program_id(0) == 0)
        def _():
            slot_ref[...] = jnp.zeros_like(slot_ref)
            for j in range(N_DEV):
                w1_copy(j).start()
            for j in range(N_DEV):
                w2_copy(j).start()
            tile(True)

        @pl.when(pl.program_id(0) > 0)
        def _():
            tile(False)

    row_f32 = pl.BlockSpec((tm, D_MODEL), lambda i: (i, 0))
    return pl.pallas_call(
        body, name="mlp_fwd_loss", grid=(s // tm,),
        out_shape=[jax.ShapeDtypeStruct((s, D_MODEL), F32), jax.ShapeDtypeStruct((s, D_MODEL), BF16),
                   jax.ShapeDtypeStruct((s, D_FF), BF16), jax.ShapeDtypeStruct((s, D_MODEL), BF16),
                   jax.ShapeDtypeStruct((SLOT, D_MODEL), F32)],
        in_specs=[row_f32, _full((1, D_MODEL)), ANY_SPACE, ANY_SPACE, _full((1, D_MODEL)), row_f32],
        out_specs=[row_f32, row_f32, pl.BlockSpec((tm, D_FF), lambda i: (i, 0)), row_f32, _full((SLOT, D_MODEL))],
        scratch_shapes=[pltpu.VMEM(w1.shape, BF16), pltpu.VMEM(w2.shape, BF16), pltpu.SemaphoreType.DMA((2 * N_DEV,))],
        compiler_params=_params(("arbitrary",)),
    )(x, g, w1, w2, gf, target)


def _zero_slot(slot_ref):
    @pl.when(pl.program_id(0) == 0)
    def _():
        slot_ref[...] = jnp.zeros_like(slot_ref)


def _mlp_bwd(dx3, u, x2, g, w1, w2, deps):
    s = x2.shape[0]
    tm = min(ROW_TILE, s)

    def body(d_ref, u_ref, x_ref, g_ref, w1_hbm, w2_hbm, da_ref, dx_ref, slot_ref, w1_ref, w2_ref, wsem):
        def w1_copy(j):
            return pltpu.make_async_copy(w1_hbm.at[j], w1_ref.at[j], wsem.at[j])

        def w2_copy(j):
            return pltpu.make_async_copy(w2_hbm.at[j], w2_ref.at[j], wsem.at[N_DEV + j])

        def tile(first):
            d = d_ref[...]
            d16 = d.astype(BF16)
            if first:
                w2_copy(0).wait()
            du_next = _mm_nt(d16, w2_ref[0])
            dhf = jnp.zeros((tm, D_MODEL), F32)
            for j in range(N_DEV):
                sl = slice(FF_BLK * j, FF_BLK * (j + 1))
                du = du_next
                if j + 1 < N_DEV:
                    if first:
                        w2_copy(j + 1).wait()
                    du_next = _mm_nt(d16, w2_ref[j + 1])
                u = u_ref[:, sl].astype(F32)
                da = (du * (2.0 * u * lax.rsqrt(jnp.maximum(u, TINY)))).astype(BF16)
                da_ref[:, sl] = da
                if first:
                    w1_copy(j).wait()
                dhf = dhf + _mm_nt(da, w1_ref[j])
            dx, dg = _rms_bwd(x_ref[...], g_ref[...], dhf)
            dx_ref[...] = d + dx
            slot_ref[0:1, :] += dg

        @pl.when(pl.program_id(0) == 0)
        def _():
            slot_ref[...] = jnp.zeros_like(slot_ref)
            for j in range(N_DEV):
                w2_copy(j).start()
                w1_copy(j).start()
            tile(True)

        @pl.when(pl.program_id(0) > 0)
        def _():
            tile(False)

    row_f32 = pl.BlockSpec((tm, D_MODEL), lambda i: (i, 0))
    return _call_behind(
        deps, body, name="mlp_bwd", grid=(s // tm,),
        out_shape=[jax.ShapeDtypeStruct((s, D_FF), BF16), jax.ShapeDtypeStruct((s, D_MODEL), F32),
                   jax.ShapeDtypeStruct((SLOT, D_MODEL), F32)],
        in_specs=[row_f32, pl.BlockSpec((tm, D_FF), lambda i: (i, 0)), row_f32, _full((1, D_MODEL)),
                  ANY_SPACE, ANY_SPACE],
        out_specs=[pl.BlockSpec((tm, D_FF), lambda i: (i, 0)), row_f32, _full((SLOT, D_MODEL))],
        scratch_shapes=[pltpu.VMEM(w1.shape, BF16), pltpu.VMEM(w2.shape, BF16), pltpu.SemaphoreType.DMA((2 * N_DEV,))],
        compiler_params=_params(("arbitrary",)),
    )(dx3, u, x2, g, w1, w2)


def _wgrad(a, b, name, col_blocks=False):
    s, m = a.shape
    n = b.shape[1]
    tm = 1280 if m % 1280 == 0 else min(1024, m)
    tn = min(1024, n)
    blk = n // N_DEV
    per_step = tn // blk if col_blocks else 1
    ts = min(4 * ROW_TILE, s)
    n_s = s // ts

    def body(a_ref, b_ref, o_ref, acc):
        k = pl.program_id(2)

        @pl.when(k == 0)
        def _():
            acc[...] = jnp.zeros_like(acc)

        acc[...] += _mm_tn(a_ref[...], b_ref[...])

        @pl.when(k == n_s - 1)
        def _():
            if col_blocks:
                for p in range(per_step):
                    o_ref[p] = acc[:, blk * p:blk * (p + 1)].astype(BF16)
            else:
                o_ref[...] = acc[...].astype(BF16)

    if col_blocks:
        out_shape = jax.ShapeDtypeStruct((N_DEV, m, blk), BF16)
        out_spec = pl.BlockSpec((per_step, tm, blk), lambda i, j, k: (j, i, 0))
    else:
        out_shape = jax.ShapeDtypeStruct((m, n), BF16)
        out_spec = pl.BlockSpec((tm, tn), lambda i, j, k: (i, j))
    return pl.pallas_call(
        body, name=name, grid=(m // tm, n // tn, n_s), out_shape=out_shape,
        in_specs=[pl.BlockSpec((ts, tm), lambda i, j, k: (k, i)), pl.BlockSpec((ts, tn), lambda i, j, k: (k, j))],
        out_specs=out_spec,
        scratch_shapes=[pltpu.VMEM((tm, tn), F32)],
        compiler_params=_params(("parallel", "parallel", "arbitrary")),
    )(a, b)


def _xattn_bwd(dx2, x1, g, q, xk, xv, wq, wo_t, deps):
    s = x1.shape[0]
    tm = min(ROW_TILE, s)
    scale = XHD ** -0.5

    def body(d_ref, x_ref, g_ref, q_ref, k_ref, v_ref, wq_ref, wo_ref, dx_ref, dx16_ref, dq_ref, dk_ref, dv_ref, slot_ref,
             datt):
        _zero_slot(slot_ref)

        @pl.when(pl.program_id(0) == 0)
        def _():
            dk_ref[...] = jnp.zeros_like(dk_ref)
            dv_ref[...] = jnp.zeros_like(dv_ref)

        d = d_ref[...]
        datt[...] = _mm(d, wo_ref[...]).astype(BF16)
        heads = [slice(XHD * h, XHD * (h + 1)) for h in range(HEADS)]
        scores = [_mm_nt(q_ref[:, sl], k_ref[:, sl]) for sl in heads]
        dps = [_mm_nt(datt[:, sl], v_ref[:, sl]) for sl in heads]
        probs = [_softmax_rows(sc) for sc in scores]
        dss = [(p * (dp - jnp.sum(dp * p, axis=-1, keepdims=True))).astype(BF16) for p, dp in zip(probs, dps)]
        for sl, p, ds in zip(heads, probs, dss):
            dq_ref[:, sl] = (_mm(ds, k_ref[:, sl]) * scale).astype(BF16)
            dk_ref[:, sl] += _mm_tn(ds, q_ref[:, sl])
            dv_ref[:, sl] += _mm_tn(p, datt[:, sl])
        dx, dg = _rms_bwd(x_ref[...], g_ref[...], _mm_nt(dq_ref[...], wq_ref[...]))
        dx_ref[...] = d + dx
        dx16_ref[...] = (d + dx).astype(BF16)
        slot_ref[0:1, :] += dg

    row_f32 = pl.BlockSpec((tm, D_MODEL), lambda i: (i, 0))
    kv = jax.ShapeDtypeStruct((MEM_LEN, D_MODEL), F32)
    tokens16 = jax.ShapeDtypeStruct((s, D_MODEL), BF16)
    return _call_behind(
        deps, body, name="xattn_bwd", grid=(s // tm,),
        out_shape=[jax.ShapeDtypeStruct((s, D_MODEL), F32), tokens16, tokens16, kv, kv,
                   jax.ShapeDtypeStruct((SLOT, D_MODEL), F32)],
        in_specs=[row_f32, row_f32, _full((1, D_MODEL)), row_f32, VMEM_WHOLE, VMEM_WHOLE, VMEM_WHOLE, VMEM_WHOLE],
        out_specs=[row_f32, row_f32, row_f32, _full((MEM_LEN, D_MODEL)), _full((MEM_LEN, D_MODEL)),
                   _full((SLOT, D_MODEL))],
        scratch_shapes=[pltpu.VMEM((tm, D_MODEL), BF16)],
        compiler_params=_params(("arbitrary",)),
    )(dx2, x1, g, q, xk, xv, wq, wo_t)


def _mem_bwd(mem, g, hm, dxk, dxv, wk, wv):
    def body(m_ref, g_ref, hm_ref, dk_ref, dv_ref, wk_ref, wv_ref, dwk_ref, dwv_ref, slot_ref):
        dk, dv = dk_ref[...], dv_ref[...]
        hm_ = hm_ref[...]
        dwk_ref[...] = _mm_tn(hm_, dk).astype(BF16)
        dwv_ref[...] = _mm_tn(hm_, dv).astype(BF16)
        _, dg = _rms_bwd(m_ref[...], g_ref[...], _mm_nt(dk, wk_ref[...]) + _mm_nt(dv, wv_ref[...]))
        slot_ref[...] = jnp.zeros_like(slot_ref)
        slot_ref[0:1, :] = dg

    wshape = jax.ShapeDtypeStruct((D_MODEL, D_MODEL), BF16)
    return pl.pallas_call(
        body, name="mem_bwd", out_shape=[wshape, wshape, jax.ShapeDtypeStruct((SLOT, D_MODEL), F32)],
        in_specs=[VMEM_WHOLE] * 7, out_specs=[VMEM_WHOLE] * 3,
        compiler_params=_params(),
    )(mem, g, hm, dxk, dxv, wk, wv)


def _pool_bwd(dx1, w_out, pooled, w_pool, scale, deps):
    s = dx1.shape[0]
    tm = min(ROW_TILE, s)
    n_t = s // tm

    def body(dx_ref, wo_ref, pl_ref, w_ref, sc_ref, dz_ref, dw_ref, slot_ref, ext, do_ref):
        i = pl.program_id(0)
        tile = n_t - 1 - i
        _zero_slot(slot_ref)
        do_ref[...] = _mm_nt(dx_ref[...], wo_ref[HW:2 * HW, :])

        @pl.when(i == 0)
        def _():
            dw_ref[...] = jnp.zeros_like(dw_ref)
            ext[tm:tm + POOL_HALO, :] = jnp.zeros((POOL_HALO, HW), F32)

        @pl.when(i > 0)
        def _():
            ext[tm:tm + POOL_HALO, :] = ext[0:POOL_HALO, :]

        inv = _pool_counts(tile, tm)
        dpooled = []
        for g in range(HEADS):
            sl = slice(HD * g, HD * (g + 1))
            pooled_g = pl_ref[:, sl]
            do = do_ref[:, sl]
            slot_ref[0:1, sl] += jnp.sum(_mm(pooled_g, w_ref[g]) * do, axis=0, keepdims=True)
            dy = (do * sc_ref[:, sl]).astype(BF16)
            dw_ref[g] += _mm_tn(pooled_g, dy)
            dpo = _mm_nt(dy, w_ref[g])
            dpooled.append(dpo)
            ext[0:tm, sl] = dpo * inv[g]
        for g, w in enumerate(POOL_WINDOWS):
            sl = slice(HD * g, HD * (g + 1))
            win = ext[0:tm, sl]
            for d in range(1, w):
                win = win + ext[d:d + tm, sl]
            dz_ref[:, sl] = (win - dpooled[g]).astype(BF16)

    return _call_behind(
        deps, body, name="pool_bwd", grid=(n_t,),
        out_shape=[jax.ShapeDtypeStruct((s, IN_WIDTH), BF16), jax.ShapeDtypeStruct((HEADS, HD, HD), F32),
                   jax.ShapeDtypeStruct((SLOT, D_MODEL), F32)],
        in_specs=[pl.BlockSpec((tm, D_MODEL), lambda i: (n_t - 1 - i, 0)), VMEM_WHOLE,
                  pl.BlockSpec((tm, HW), lambda i: (n_t - 1 - i, 0)), _full((HEADS, HD, HD)), _full((1, HW))],
        out_specs=[pl.BlockSpec((tm, HW), lambda i: (n_t - 1 - i, 4)), _full((HEADS, HD, HD)), _full((SLOT, D_MODEL))],
        scratch_shapes=[pltpu.VMEM((tm + POOL_HALO, HW), F32), pltpu.VMEM((tm, HW), F32)],
        compiler_params=_params(("arbitrary",)),
    )(dx1, w_out, pooled, w_pool, scale)


def _hgrn_bwd(z, o, dx1, w_out, states, lb_logits, gn, dz_in, deps):
    s = z.shape[0]
    n_chunks = s // CHUNK

    def body(zq_ref, zf_ref, zi_ref, zg_ref, o_ref, dx_ref, wo_ref, st_ref, lbl_ref, gn_ref, dzin_ref,
             dz_ref, dlb_ref, dgn_ref, dstate, b_scr, dlb_acc, do_ref):
        i = pl.program_id(0)

        @pl.when(i == 0)
        def _():
            dstate[...] = jnp.zeros_like(dstate)
            dlb_acc[...] = jnp.zeros_like(dlb_acc)
            dgn_ref[...] = jnp.zeros_like(dgn_ref)
            dlb_ref[...] = jnp.zeros_like(dlb_ref)

        do_ref[...] = _mm_nt(dx_ref[...], wo_ref[0:HW, :])
        lb = _sigmoid(lbl_ref[0:1, :] - lbl_ref[1:2, :])
        row, col = _chunk_masks()
        causal = col <= row
        tri = _ones_where(causal)
        upper = _ones_where(col >= row)
        strict_lower = _ones_where(col < row)
        in_sub, in_sub_head = _sub_chunk_masks(HW), _sub_chunk_masks(HD)
        gn_row = _lanes([gn_ref[h:h + 1, :] for h in range(HEADS)])
        sums = {"dlb": 0.0, "dgn": 0.0}

        def front(c):
            r0 = CHUNK * c
            rs = slice(r0, r0 + CHUNK)
            p = {"rs": rs}
            p["zq"] = zq_ref[rs, :]
            p["q"], p["sq"], p["sig"], p["f"] = _hgrn_gates(p["zq"], zf_ref[rs, :], lb)
            p["kk"] = 1.0 - p["f"]
            b = _tri_dot(tri, jnp.log(p["f"]), 3)
            b_scr[rs, :] = b
            p["v"] = zi_ref[rs, :]
            o, zg, doa = o_ref[rs, :], zg_ref[rs, :], do_ref[rs, :]
            sg = _sigmoid(zg)
            rms = _head_rms(o)
            n = o * rms
            don = doa * (zg * sg)
            sums["dgn"] = sums["dgn"] + jnp.sum(don * n, axis=0, keepdims=True)
            dn = don * gn_row
            p["d_o"] = rms * (dn - n * _head_mean(dn * n))
            dz_ref[rs, 3 * HW:4 * HW] = (doa * (n * gn_row) * (sg * (1.0 + zg * (1.0 - sg)))).astype(BF16)
            p["eq"], p["ek"] = _hgrn_decay_factors(b_scr, r0, b, in_sub)
            b_last = b_scr[r0 + CHUNK - 1:r0 + CHUNK, :]
            p["lam"], p["e_last"], p["lam_last"] = jnp.exp(b), jnp.exp(b_last - b), jnp.exp(b_last)
            p["qe"], p["qg"], p["kd"] = p["q"] * p["eq"], p["q"] * p["lam"], p["kk"] * p["e_last"]
            p["ke"] = [p["kk"] * e for e in p["ek"]]
            p["q16"] = [_per_sub_chunk(_head(p["qe"], h), in_sub_head).astype(BF16) for h in range(HEADS)]
            p["ke16"] = [_lanes([_head(p["ke"][j], h) for j in range(N_SUB)]).astype(BF16) for h in range(HEADS)]
            return p

        def recurrence(c, p):
            m = {k: [] for k in ("dv", "gq", "gk", "dqi", "dkd", "st")}
            a, da, dv_state = [], [], []
            for h in range(HEADS):
                vh, doh = _head(p["v"], h), _head(p["d_o"], h)
                st0, ds1 = st_ref[c, h], dstate[h]
                a.append(jnp.where(causal, _mm_nt(p["q16"][h], p["ke16"][h]), 0.0))
                da.append(jnp.where(causal, _mm_nt(doh, vh), 0.0))
                dv_state.append(_mm_nt(_head(p["kd"], h), ds1))
                m["dqi"].append(_mm(doh, st0))
                m["dkd"].append(_mm(vh, ds1))
                m["st"].append(jnp.sum(st0 * ds1, axis=0, keepdims=True))
                dstate[h] = ds1 * _head(p["lam_last"], h) + _mm_tn(doh, _head(p["qg"], h))
            for h in range(HEADS):
                m["dv"].append(_mm_tn(a[h], _head(p["d_o"], h)) + dv_state[h])
                m["gq"].append(_own_lane_block(_mm(da[h], p["ke16"][h]), in_sub_head))
                m["gk"].append(_mm_tn(da[h], p["q16"][h]))
            return m

        def back(p, m):
            rs = p["rs"]
            dz_ref[rs, 2 * HW:3 * HW] = _lanes(m["dv"]).astype(BF16)
            gq = _lanes(m["gq"])
            gk = [_lanes([m["gk"][h][:, HD * j:HD * (j + 1)] for h in range(HEADS)]) for j in range(N_SUB)]
            dq_inter = p["lam"] * _lanes(m["dqi"])
            dq = p["eq"] * gq + dq_inter
            dk_intra = sum(p["ek"][j] * gk[j] for j in range(N_SUB))
            dk_state = _lanes(m["dkd"]) * p["e_last"]
            db_intra = (p["qe"].astype(BF16).astype(F32) * gq
                        - sum(p["ke"][j].astype(BF16).astype(F32) * gk[j] for j in range(N_SUB)))
            dlf = (_tri_dot(upper, db_intra + p["q"] * dq_inter, 2) + _tri_dot(strict_lower, p["kk"] * dk_state, 2)
                   + p["lam_last"] * _lanes(m["st"]))
            sig, sq, zq = p["sig"], p["sq"], p["zq"]
            df = dlf / p["f"] - (dk_intra + dk_state)
            sums["dlb"] = sums["dlb"] + jnp.sum(df * (1.0 - sig), axis=0, keepdims=True)
            dz_ref[rs, HW:2 * HW] = (df * (1.0 - lb) * sig * (1.0 - sig)).astype(BF16)
            dz_ref[rs, 0:HW] = (dq * (sq * (1.0 + zq * (1.0 - sq)))).astype(BF16)

        p = front(CHUNKS_PER_STEP - 1)
        for c in reversed(range(CHUNKS_PER_STEP)):
            m = recurrence(c, p)
            p_next = front(c - 1) if c > 0 else None
            back(p, m)
            p = p_next
        dlb_acc[...] += sums["dlb"]
        for h in range(HEADS):
            dgn_ref[h:h + 1, 0:HD] += _head(sums["dgn"], h)

        @pl.when(i == n_steps - 1)
        def _():
            dl0 = dlb_acc[...] * lb * (1.0 - lb)
            dlb_ref[0:1, 0:HW] = dl0
            dlb_ref[1:2, 0:HW] = -dl0

    rows = CHUNK * CHUNKS_PER_STEP
    n_steps = s // rows
    rev = lambda i: n_steps - 1 - i
    zspec = lambda cb: pl.BlockSpec((rows, HW), lambda i, cb=cb: (rev(i), cb))
    slot = jax.ShapeDtypeStruct((SLOT, D_MODEL), F32)
    return _call_behind(
        deps, body, name="hgrn_bwd", grid=(n_steps,),
        out_shape=[jax.ShapeDtypeStruct((s, IN_WIDTH), BF16), slot, slot],
        in_specs=[zspec(0), zspec(1), zspec(2), zspec(3), pl.BlockSpec((rows, HW), lambda i: (rev(i), 0)),
                  pl.BlockSpec((rows, D_MODEL), lambda i: (rev(i), 0)), VMEM_WHOLE,
                  pl.BlockSpec((CHUNKS_PER_STEP, HEADS, HD, HD), lambda i: (rev(i), 0, 0, 0)), _full((2, HW)),
                  _full((HEADS, HD)), ANY_SPACE],
        out_specs=[pl.BlockSpec((rows, 4 * HW), lambda i: (rev(i), 0)), _full((SLOT, D_MODEL)), _full((SLOT, D_MODEL))],
        scratch_shapes=[pltpu.VMEM((HEADS, HD, HD), F32), pltpu.VMEM((rows, HW), F32), pltpu.VMEM((1, HW), F32),
                        pltpu.VMEM((rows, HW), F32)],
        input_output_aliases={10: 0},
        compiler_params=_params(("arbitrary",)),
    )(z, z, z, z, o, dx1, w_out, states, lb_logits, gn, dz_in)


def _in_bwd(dz, w_t, x0, g, dx1, deps):
    s = x0.shape[0]
    tm = min(WIDE_ROW_TILE, s)

    def body(dz_ref, w_ref, x_ref, g_ref, d_ref, dx_ref, slot_ref):
        _zero_slot(slot_ref)
        dx, dg = _rms_bwd(x_ref[...], g_ref[...], _mm(dz_ref[...], w_ref[...]))
        dx_ref[...] = d_ref[...] + dx
        slot_ref[0:1, :] += dg

    row_f32 = pl.BlockSpec((tm, D_MODEL), lambda i: (i, 0))
    return _call_behind(
        deps, body, name="in_bwd", grid=(s // tm,),
        out_shape=[jax.ShapeDtypeStruct((s, D_MODEL), F32), jax.ShapeDtypeStruct((SLOT, D_MODEL), F32)],
        in_specs=[pl.BlockSpec((tm, IN_WIDTH), lambda i: (i, 0)), VMEM_WHOLE, row_f32, _full((1, D_MODEL)), row_f32],
        out_specs=[row_f32, _full((SLOT, D_MODEL))],
        compiler_params=_params(("arbitrary",)),
    )(dz, w_t, x0, g, dx1)


def kernel(x, mem, norm_mix_g, w_in, lb_logits, hgrn_norm_g, w_pool, pool_scale, w_out, norm_x_g, norm_mem_g, w_xq, w_xk, w_xv, w_xo, norm_ffn_g, w_ff1, w_ff2, final_norm_g, loss_target, m_norm_mix_g, m_w_in, m_lb_logits, m_hgrn_norm_g, m_w_pool, m_pool_scale, m_w_out, m_norm_x_g, m_norm_mem_g, m_w_xq, m_w_xk, m_w_xv, m_w_xo, m_norm_ffn_g, m_w_ff1, m_w_ff2, m_final_norm_g, v_norm_mix_g, v_w_in, v_lb_logits, v_hgrn_norm_g, v_w_pool, v_pool_scale, v_w_out, v_norm_x_g, v_norm_mem_g, v_w_xq, v_w_xk, v_w_xv, v_w_xo, v_norm_ffn_g, v_w_ff1, v_w_ff2, v_final_norm_g):
    x0 = x[0]
    mem0 = mem[0]
    tgt = loss_target[0]
    gn = hgrn_norm_g[0]
    gfin = final_norm_g.reshape(1, D_MODEL)
    wp = w_pool[0]
    heads_2d = lambda w: w.reshape(D_MODEL // N_DEV, D_MODEL)
    xo_2d = lambda w: w.reshape(D_MODEL, D_MODEL // N_DEV)

    first = _all_gather_weights([w_in[0].T], [w_out[0], heads_2d(w_xq), heads_2d(w_xk), heads_2d(w_xv), xo_2d(w_xo).T,
                                              w_ff1[0], w_ff2[0]])
    win_t = first[0].reshape(IN_WIDTH, D_MODEL)
    ga_attn, ga_mlp = _gather_first_start([first[1:6], first[6:8]], "gather_first_start")

    z, h = _in_proj(x0, norm_mix_g, win_t, deps=[ga_attn[3]])
    mixed_a, o_pre, states = _hgrn_fwd(z, lb_logits, gn)
    lands = _split_wait(_gather_first_copies, ga_attn, o_pre, "gather_attn_first_wait")
    gb_attn = _gather_forward_start(lands, "gather_attn_forward_start")
    mixed, pooled = _pool_fwd(z, wp, pool_scale, mixed_a, deps=[gb_attn[3]])
    lands = _split_wait(_gather_forward_copies, gb_attn, pooled, "gather_attn_forward_wait")
    wout_f, wq_f, wk_f, wv_f, wo_t = (t.reshape(D_MODEL, D_MODEL) for t in lands)
    hm, xk, xv = _mem_kv(mem0, norm_mem_g, wk_f, wv_f, deps=[])
    x1, x2, hq, xq, att = _mix_xattn_fwd(x0, mixed, wout_f, norm_x_g, wq_f, xk, xv, wo_t, deps=[])
    lands = _split_wait(_gather_first_copies, ga_mlp, x2, "gather_mlp_first_wait")
    gb_mlp = _gather_forward_start(lands, "gather_mlp_forward_start")
    w1_b, w2_b = _split_wait(_gather_forward_copies, gb_mlp, gb_mlp[3], "gather_mlp_forward_wait")
    dx3, dx3_16, u, hf, slot_fin = _mlp_fwd_loss(x2, norm_ffn_g, w1_b, w2_b.reshape(D_FF, D_MODEL), gfin, tgt)

    rows = lambda t, r: t.reshape(N_DEV, r, D_MODEL)
    dw2 = _wgrad(u, dx3_16, "wgrad_ff2")
    ex_ff2 = _all_to_all_start([rows(dw2, FF_BLK)], [], "exchange_ff2_start")
    da, dx2, slot_ffn = _mlp_bwd(dx3, u, x2, norm_ffn_g, w1_b, w2_b, deps=[ex_ff2[3]])
    dw1 = _wgrad(hf, da, "wgrad_ff1", col_blocks=True)
    ex_ff1 = _all_to_all_start([dw1], [], "exchange_ff1_start")
    dx1, dx1_16, dxq, dxk, dxv, slot_x = _xattn_bwd(dx2, x1, norm_x_g, xq, xk, xv, wq_f, wo_t, deps=[ex_ff1[3]])
    dwo_t = _wgrad(dx2, att, "wgrad_xo")
    dwq = _wgrad(hq, dxq, "wgrad_xq")
    dwk, dwv, slot_mem = _mem_bwd(mem0, norm_mem_g, hm, dxk, dxv, wk_f, wv_f)
    ex_attn = _all_to_all_start([rows(dwq, 128), rows(dwk, 128), rows(dwv, 128), rows(dwo_t, 128)], [],
                                "exchange_attn_start")
    dwout = _wgrad(mixed, dx1_16, "wgrad_out")
    dz_pool, d_wpool, slot_ps = _pool_bwd(dx1_16, wout_f, pooled, wp, pool_scale, deps=[ex_attn[3]])
    small0 = jnp.concatenate([slot_x, slot_mem, slot_ffn, slot_fin, slot_ps], axis=0)
    ex_out = _all_to_all_start([rows(dwout, 128)], [small0, d_wpool], "exchange_out_start")
    dz, slot_lb, slot_gn = _hgrn_bwd(z, o_pre, dx1_16, wout_f, states, lb_logits, gn, dz_pool, deps=[ex_out[3]])
    dwin_t = _wgrad(dz, h, "wgrad_in")
    small1 = jnp.concatenate([slot_lb, slot_gn], axis=0)
    ex_in = _all_to_all_start([rows(dwin_t, 320)], [small1], "exchange_in_start")
    grad_x, slot_mix = _in_bwd(dz, win_t, x0, norm_mix_g, dx1, deps=[ex_in[3]])
    ex_mix = _all_to_all_start([], [slot_mix], "exchange_mix_start")

    out = {}
    (r_2,) = _split_wait(_all_to_all_copies(1), ex_ff2, ex_mix[3], "exchange_ff2_wait")
    out["w_ff2"] = _sum_adamw(r_2, w_ff2[0], m_w_ff2[0], v_w_ff2[0], "adamw_ff2")
    (r_1,) = _split_wait(_all_to_all_copies(1), ex_ff1, out["w_ff2"][1], "exchange_ff1_wait")
    out["w_ff1"] = _sum_adamw(r_1, w_ff1[0], m_w_ff1[0], v_w_ff1[0], "adamw_ff1")
    r_q, r_k, r_v, r_o = _split_wait(_all_to_all_copies(4), ex_attn, out["w_ff1"][1], "exchange_attn_wait")
    sums = _sum_sources_whole([r_q, r_k, r_v, r_o], "sum_grad_attn")
    g_attn = [g.reshape(w_xq.shape) for g in sums[:3]] + [sums[3].T]
    attn = _adamw_whole([(g_attn[0], w_xq, m_w_xq, v_w_xq), (g_attn[1], w_xk, m_w_xk, v_w_xk),
                         (g_attn[2], w_xv, m_w_xv, v_w_xv),
                         (g_attn[3], xo_2d(w_xo), xo_2d(m_w_xo), xo_2d(v_w_xo))], "adamw_attn")
    for n, g, res in zip(("w_xq", "w_xk", "w_xv", "w_xo"), g_attn, attn):
        out[n] = (g, *res)
    r_out, r_small0, r_wpool = _split_wait(_all_to_all_copies(1), ex_out, attn[3][0], "exchange_out_wait")
    out["w_out"] = _sum_adamw(r_out, w_out[0], m_w_out[0], v_w_out[0], "adamw_out")
    r_in, r_small1 = _split_wait(_all_to_all_copies(1), ex_in, out["w_out"][1], "exchange_in_wait")
    in_t = _sum_adamw(r_in, w_in[0].T, m_w_in[0].T, v_w_in[0].T, "adamw_in")
    out["w_in"] = tuple(t.T for t in in_t)
    (r_small2,) = _split_wait(_all_to_all_copies(0), ex_mix, in_t[1], "exchange_mix_wait")
    row = lambda t: t.reshape(1, -1)
    small_params = {
        "norm_mix_g": (norm_mix_g, m_norm_mix_g, v_norm_mix_g),
        "lb_logits": (lb_logits, m_lb_logits, v_lb_logits),
        "hgrn_norm_g": (hgrn_norm_g[0], m_hgrn_norm_g[0], v_hgrn_norm_g[0]),
        "pool_scale": (pool_scale, m_pool_scale, v_pool_scale),
        "norm_x_g": (norm_x_g, m_norm_x_g, v_norm_x_g),
        "norm_mem_g": (norm_mem_g, m_norm_mem_g, v_norm_mem_g),
        "norm_ffn_g": (norm_ffn_g, m_norm_ffn_g, v_norm_ffn_g),
        "final_norm_g": (row(final_norm_g), row(m_final_norm_g), row(v_final_norm_g)),
        "w_pool": (wp, m_w_pool[0], v_w_pool[0]),
    }
    loss, small_out = _small_update([r_small0, r_small1, r_small2], r_wpool, small_params)
    out.update(small_out)

    shapes = dict(norm_mix_g=norm_mix_g, w_in=w_in, lb_logits=lb_logits, hgrn_norm_g=hgrn_norm_g, w_pool=w_pool,
                  pool_scale=pool_scale, w_out=w_out, norm_x_g=norm_x_g, norm_mem_g=norm_mem_g, w_xq=w_xq, w_xk=w_xk,
                  w_xv=w_xv, w_xo=w_xo, norm_ffn_g=norm_ffn_g, w_ff1=w_ff1, w_ff2=w_ff2, final_norm_g=final_norm_g)
    order = list(shapes)
    group = lambda k: [out[n][k].reshape(shapes[n].shape) for n in order]
    return (loss.reshape(()), grad_x.reshape(x.shape), *group(0), *group(1), *group(2), *group(3))
```

```python
import jax
import jax.numpy as jnp
from jax import lax
from jax.experimental import pallas as pl
from jax.experimental.pallas import tpu as pltpu

F32 = jnp.float32
BF16 = jnp.bfloat16

D_MODEL = 1024
N_DEV = 8
HEADS = 4
HD = 128
HW = HEADS * HD
IN_WIDTH = 5 * HW
XHD = 256
MEM_LEN = 256
D_FF = 4096
FF_BLK = D_FF // N_DEV
POOL_WINDOWS = (2, 4, 8, 16)
POOL_HALO = 16
CHUNK = 64
CHUNKS_PER_STEP = 8
SUB = 16
N_SUB = CHUNK // SUB
EXP_CAP = 80.0
EPS = 1e-6
TINY = 1e-30
ROW_TILE = 512
WIDE_ROW_TILE = 1024
SLOT = 8
V7X_VMEM_LIMIT = 56 * 1024 * 1024

ADAM_LR = 0.001
ADAM_B1 = 0.9
ADAM_B2 = 0.999
ADAM_EPS = 1e-08
ADAM_WD = 0.01
ADAM_STEP = 10

MESH_ID = pl.DeviceIdType.MESH


def _params(sem=None, vmem=V7X_VMEM_LIMIT):
    return pltpu.CompilerParams(dimension_semantics=sem, vmem_limit_bytes=vmem)


def _mm(a, b):
    return lax.dot_general(a.astype(BF16), b.astype(BF16), (((1,), (0,)), ((), ())), preferred_element_type=F32)


def _mm_nt(a, b):
    return lax.dot_general(a.astype(BF16), b.astype(BF16), (((1,), (1,)), ((), ())), preferred_element_type=F32)


def _mm_tn(a, b):
    return lax.dot_general(a.astype(BF16), b.astype(BF16), (((0,), (0,)), ((), ())), preferred_element_type=F32)


def _sigmoid(x):
    return 1.0 / (1.0 + jnp.exp(-x))


def _rms(x):
    return lax.rsqrt(jnp.mean(x * x, axis=-1, keepdims=True) + EPS)


def _rms_bwd(x, g, dh):
    r = _rms(x)
    n = x * r
    dn = dh * g
    dx = r * (dn - n * jnp.mean(dn * n, axis=-1, keepdims=True))
    return dx, jnp.sum(dh * n, axis=0, keepdims=True)


def _tri_dot(tri, x, passes):
    acc = None
    rest = x
    for _ in range(passes):
        piece = rest.astype(BF16)
        part = lax.dot_general(tri, piece, (((1,), (0,)), ((), ())), preferred_element_type=F32)
        acc = part if acc is None else acc + part
        rest = rest - piece.astype(F32)
    return acc


def _adam_update(g, w, m, v):
    nm = ADAM_B1 * m + (1.0 - ADAM_B1) * g
    nv = ADAM_B2 * v + (1.0 - ADAM_B2) * (g * g)
    m_hat = nm / (1.0 - ADAM_B1 ** ADAM_STEP)
    v_hat = nv / (1.0 - ADAM_B2 ** ADAM_STEP)
    return -ADAM_LR * (m_hat / (jnp.sqrt(v_hat) + ADAM_EPS) + ADAM_WD * w), nm, nv


def _full(shape):
    return pl.BlockSpec(shape, lambda *_: (0,) * len(shape))


VMEM_WHOLE = pl.BlockSpec(memory_space=pltpu.VMEM)
ANY_SPACE = pl.BlockSpec(memory_space=pl.ANY)


def _mesh_pos():
    return lax.axis_index("x"), lax.axis_index("y"), lax.axis_index("c")


def _flat(px, py, pc):
    return 4 * px + 2 * py + pc


def _all_gather_weights(shards, cast_only):
    n, nc = len(shards), len(cast_only)
    step = 64

    def body(*refs):
        x_refs, c_refs = refs[:n], refs[n:n + nc]
        out_refs, cast_refs = refs[n + nc:2 * n + nc], refs[2 * n + nc:2 * n + 2 * nc]
        bufs = refs[2 * n + 2 * nc:3 * n + 2 * nc]
        send_sems, recv_sems, local_sems = refs[3 * n + 2 * nc:]
        x, y, c = _mesh_pos()
        me, sibling = (x, y, c), (x, y, 1 - c)
        chips = [(1 - x, y), (x, 1 - y), (1 - x, 1 - y)]

        def copy(a, k, blk, to, src=None):
            rows = out_refs[a].at[_flat(*blk)]
            return pltpu.make_async_remote_copy(
                src_ref=rows if src is None else src, dst_ref=rows,
                send_sem=send_sems.at[7 * a + k], recv_sem=recv_sems.at[7 * a + k], device_id=to, device_id_type=MESH_ID)

        def cast_rows(src, dst, rows):
            def cast(i, carry):
                r0 = pl.multiple_of(i * step, step)
                dst[pl.ds(r0, step), :] = src[pl.ds(r0, step), :].astype(BF16)
                return carry
            lax.fori_loop(0, rows // step, cast, 0)

        first, mine = [], []
        for a in range(n):
            cast_rows(x_refs[a], bufs[a], shards[a].shape[0])
            mine.append(pltpu.make_async_copy(bufs[a], out_refs[a].at[_flat(*me)], local_sems.at[a]))
            first.append(copy(a, 0, me, sibling, src=bufs[a]))
            first += [copy(a, 1 + j, me, (*chip, c), src=bufs[a]) for j, chip in enumerate(chips)]
            for cp in [mine[-1]] + first[-4:]:
                cp.start()
        for a in range(nc):
            cast_rows(c_refs[a], cast_refs[a], cast_only[a].shape[0])
        passed = []
        for j, chip in enumerate(chips):
            for a in range(n):
                copy(a, 1 + j, (*chip, c), me).wait_recv()
                passed.append(copy(a, 4 + j, (*chip, c), sibling))
                passed[-1].start()
        for a in range(n):
            copy(a, 0, sibling, me).wait_recv()
            for j, chip in enumerate(chips):
                copy(a, 4 + j, (*chip, 1 - c), me).wait_recv()
        for cp in first + passed:
            cp.wait_send()
        for cp in mine:
            cp.wait()

    return pl.pallas_call(
        body, name="all_gather_w_in",
        out_shape=[jax.ShapeDtypeStruct((N_DEV,) + s.shape, BF16) for s in shards]
        + [jax.ShapeDtypeStruct(s.shape, BF16) for s in cast_only],
        in_specs=[VMEM_WHOLE] * (n + nc), out_specs=[ANY_SPACE] * n + [VMEM_WHOLE] * nc,
        scratch_shapes=[pltpu.VMEM(s.shape, BF16) for s in shards]
        + [pltpu.SemaphoreType.DMA((7 * n,)), pltpu.SemaphoreType.DMA((7 * n,)), pltpu.SemaphoreType.DMA((n,))],
        compiler_params=_params(),
    )(*shards, *cast_only)


HBM_SPEC = pl.BlockSpec(memory_space=pltpu.HBM)
SEM_SPEC = pl.BlockSpec(memory_space=pltpu.SEMAPHORE)
EFFECT = pltpu.SideEffectType.DATAFLOW_SIDE_EFFECTING
TOKEN = jax.ShapeDtypeStruct((8, 128), F32)


def _in_hbm(a):
    return pltpu.with_memory_space_constraint(a, pltpu.HBM)


def _split_start(copies_of, srcs, lands, n_sems, name):
    ns, nl, k = len(srcs), len(lands), len(n_sems)

    def body(*refs):
        src_refs, land_refs = refs[:ns], refs[ns:ns + nl]
        sems = refs[ns + nl:ns + nl + k]
        token = refs[-1]
        for cp in copies_of(src_refs, land_refs, sems):
            cp.start()
        token[...] = jnp.zeros_like(token)

    outs = pl.pallas_call(
        body, name=name,
        out_shape=[pltpu.SemaphoreType.DMA((q,)) for q in n_sems]
        + [pltpu.HBM(a.shape, a.dtype) for a in list(srcs) + list(lands)] + [TOKEN],
        in_specs=[HBM_SPEC] * (ns + nl),
        out_specs=[SEM_SPEC] * k + [HBM_SPEC] * (ns + nl) + [VMEM_WHOLE],
        input_output_aliases={i: k + i for i in range(ns + nl)},
        compiler_params=pltpu.CompilerParams(has_side_effects=EFFECT),
    )(*[_in_hbm(a) for a in list(srcs) + list(lands)])
    return outs[:k], outs[k:k + ns], outs[k + ns:k + ns + nl], outs[-1]


def _split_wait(copies_of, handle, after, name):
    sems, srcs, lands, _ = handle
    ns, nl, k = len(srcs), len(lands), len(sems)

    def body(*refs):
        src_refs, land_refs = refs[:ns], refs[ns:ns + nl]
        sem_refs = refs[ns + nl:ns + nl + k]
        for cp in copies_of(src_refs, land_refs, sem_refs):
            cp.wait()

    outs = pl.pallas_call(
        body, name=name,
        out_shape=[pltpu.HBM(a.shape, a.dtype) for a in list(srcs) + list(lands)],
        in_specs=[HBM_SPEC] * (ns + nl) + [SEM_SPEC] * k + [ANY_SPACE],
        out_specs=[HBM_SPEC] * (ns + nl),
        input_output_aliases={i: i for i in range(ns + nl)},
        compiler_params=pltpu.CompilerParams(has_side_effects=EFFECT),
    )(*srcs, *lands, *sems, after)
    return outs[ns:]


def _gather_first_copies(shard_refs, land_refs, sems):
    send_sems, recv_sems, local_sems = sems
    x, y, c = _mesh_pos()
    me = _flat(x, y, c)
    peers = [(x, y, 1 - c), (1 - x, y, c), (x, 1 - y, c), (1 - x, 1 - y, c)]
    copies = []
    for a, (shard, land) in enumerate(zip(shard_refs, land_refs)):
        copies.append(pltpu.make_async_copy(shard, land.at[me], local_sems.at[a]))
        for k, peer in enumerate(peers):
            copies.append(pltpu.make_async_remote_copy(
                src_ref=shard, dst_ref=land.at[me], send_sem=send_sems.at[4 * a + k], recv_sem=recv_sems.at[4 * a + k],
                device_id=peer, device_id_type=MESH_ID))
    return copies


def _gather_forward_copies(src_refs, land_refs, sems):
    del src_refs
    send_sems, recv_sems = sems
    x, y, c = _mesh_pos()
    chips = [(1 - x, y), (x, 1 - y), (1 - x, 1 - y)]
    copies = []
    for a, land in enumerate(land_refs):
        for j, chip in enumerate(chips):
            rows = land.at[_flat(*chip, c)]
            copies.append(pltpu.make_async_remote_copy(
                src_ref=rows, dst_ref=rows, send_sem=send_sems.at[3 * a + j], recv_sem=recv_sems.at[3 * a + j],
                device_id=(x, y, 1 - c), device_id_type=MESH_ID))
    return copies


def _gather_first_start(groups, name):
    shards = [s for g in groups for s in g]
    lands = [lax.empty((N_DEV,) + s.shape, s.dtype) for s in shards]
    bounds = [sum(len(g) for g in groups[:i]) for i in range(len(groups) + 1)]

    def copies_of(src_refs, land_refs, sems):
        copies = []
        for i in range(len(groups)):
            lo, hi = bounds[i], bounds[i + 1]
            copies += _gather_first_copies(src_refs[lo:hi], land_refs[lo:hi], sems[3 * i:3 * i + 3])
        return copies

    n_sems = tuple(q for g in groups for q in (4 * len(g), 4 * len(g), len(g)))
    sems, srcs, lands, token = _split_start(copies_of, shards, lands, n_sems, name)
    return [(sems[3 * i:3 * i + 3], srcs[bounds[i]:bounds[i + 1]], lands[bounds[i]:bounds[i + 1]], token)
            for i in range(len(groups))]


def _gather_forward_start(lands, name):
    n = len(lands)
    return _split_start(_gather_forward_copies, [], lands, (3 * n, 3 * n), name)


def _all_to_all_copies(n_scattered):
    def copies_of(src_refs, land_refs, sems):
        send_sems, recv_sems, local_sems = sems
        x, y, c = _mesh_pos()
        me = _flat(x, y, c)
        copies = []
        for a, (src, land) in enumerate(zip(src_refs, land_refs)):
            scattered = a < n_scattered
            copies.append(pltpu.make_async_copy(src.at[me] if scattered else src, land.at[me], local_sems.at[a]))
            for k in range(1, N_DEV):
                peer = (1 - x if k & 4 else x, 1 - y if k & 2 else y, 1 - c if k & 1 else c)
                copies.append(pltpu.make_async_remote_copy(
                    src_ref=src.at[_flat(*peer)] if scattered else src, dst_ref=land.at[me],
                    send_sem=send_sems.at[7 * a + k - 1], recv_sem=recv_sems.at[7 * a + k - 1],
                    device_id=peer, device_id_type=MESH_ID))
        return copies
    return copies_of


def _all_to_all_start(scattered, broadcast, name):
    srcs = list(scattered) + list(broadcast)
    lands = [lax.empty(a.shape, a.dtype) for a in scattered] + [lax.empty((N_DEV,) + a.shape, a.dtype) for a in broadcast]
    n = len(srcs)
    return _split_start(_all_to_all_copies(len(scattered)), srcs, lands, (7 * n, 7 * n, n), name)


def _call_behind(deps, body, *, in_specs, **kwargs):
    n_in, n_dep = len(in_specs), len(deps)

    def body_without_deps(*refs):
        return body(*refs[:n_in], *refs[n_in + n_dep:])

    call = pl.pallas_call(body_without_deps, in_specs=list(in_specs) + [ANY_SPACE] * n_dep, **kwargs)
    return lambda *operands: call(*operands, *deps)


def _row_tile(rows):
    for cand in (256, 128, 64, 32, 16):
        if rows % cand == 0:
            return cand
    return rows


def _adamw_whole(groups, name):
    n = len(groups)

    def body(*refs):
        for i in range(n):
            g_ref, w_ref, m_ref, v_ref = refs[4 * i:4 * i + 4]
            d_ref, nm_ref, nv_ref = refs[4 * n + 3 * i:4 * n + 3 * i + 3]
            d_ref[...], nm_ref[...], nv_ref[...] = _adam_update(g_ref[...], w_ref[...], m_ref[...], v_ref[...])

    outs = pl.pallas_call(
        body, name=name, out_shape=[jax.ShapeDtypeStruct(grp[0].shape, F32) for grp in groups for _ in range(3)],
        in_specs=[VMEM_WHOLE] * (4 * n), out_specs=[VMEM_WHOLE] * (3 * n),
        compiler_params=_params(),
    )(*[t for grp in groups for t in grp])
    return [outs[3 * i:3 * i + 3] for i in range(n)]


def _sum_sources_whole(recvs, name):
    n = len(recvs)

    def body(*refs):
        for r_ref, o_ref in zip(refs[:n], refs[n:]):
            acc = r_ref[0].astype(F32)
            for d in range(1, N_DEV):
                acc = acc + r_ref[d].astype(F32)
            o_ref[...] = acc

    return pl.pallas_call(
        body, name=name, out_shape=[jax.ShapeDtypeStruct(r.shape[1:], F32) for r in recvs],
        in_specs=[VMEM_WHOLE] * n, out_specs=[VMEM_WHOLE] * n,
        compiler_params=_params(),
    )(*recvs)


def _sum_adamw(recv, w, m, v, name):
    _, rows, cols = recv.shape
    tile = _row_tile(rows)

    def body(r_ref, w_ref, m_ref, v_ref, g_ref, d_ref, nm_ref, nv_ref):
        acc = r_ref[0].astype(F32)
        for d in range(1, N_DEV):
            acc = acc + r_ref[d].astype(F32)
        g_ref[...] = acc
        d_ref[...], nm_ref[...], nv_ref[...] = _adam_update(acc, w_ref[...], m_ref[...], v_ref[...])

    spec = pl.BlockSpec((tile, cols), lambda i: (i, 0))
    shp = jax.ShapeDtypeStruct((rows, cols), F32)
    return pl.pallas_call(
        body, name=name, grid=(rows // tile,), out_shape=[shp] * 4,
        in_specs=[pl.BlockSpec((N_DEV, tile, cols), lambda i: (0, i, 0)), spec, spec, spec], out_specs=[spec] * 4,
        compiler_params=_params(("parallel",)),
    )(recv, w, m, v)


SMALL_SLOTS = {"norm_x_g": (0, 0, 1, D_MODEL), "norm_mem_g": (0, 8, 1, D_MODEL), "norm_ffn_g": (0, 16, 1, D_MODEL),
               "final_norm_g": (0, 24, 1, D_MODEL), "pool_scale": (0, 32, 1, HW),
               "lb_logits": (1, 0, 2, HW), "hgrn_norm_g": (1, 8, HEADS, HD), "norm_mix_g": (2, 0, 1, D_MODEL)}
LOSS_ROW = 25
SMALL_ORDER = ("norm_mix_g", "lb_logits", "hgrn_norm_g", "pool_scale", "norm_x_g", "norm_mem_g", "norm_ffn_g",
               "final_norm_g", "w_pool")


def _small_update(srecvs, wprecv, params):
    flat = [t for n in SMALL_ORDER for t in params[n]]
    nb = len(srecvs)
    n_in = nb + 1 + len(flat)

    def body(*refs):
        s_refs, wp_ref = refs[0:nb], refs[nb]
        in_refs = refs[nb + 1:n_in]
        loss_ref = refs[n_in]
        out_refs = refs[n_in + 1:-nb]
        accs = refs[-nb:]
        for s_ref, acc in zip(s_refs, accs):
            total = s_ref[0]
            for d in range(1, N_DEV):
                total = total + s_ref[d]
            acc[...] = total
        loss_ref[...] = accs[0][LOSS_ROW:LOSS_ROW + 1, 0:1]
        for i, name in enumerate(SMALL_ORDER):
            w_ref, m_ref, v_ref = in_refs[3 * i:3 * i + 3]
            g_ref, d_ref, nm_ref, nv_ref = out_refs[4 * i:4 * i + 4]
            if name == "w_pool":
                g = wp_ref[0]
                for d in range(1, N_DEV):
                    g = g + wp_ref[d]
            else:
                buf, r0, nr, nc = SMALL_SLOTS[name]
                g = accs[buf][r0:r0 + nr, 0:nc]
            g_ref[...] = g
            d_ref[...], nm_ref[...], nv_ref[...] = _adam_update(g, w_ref[...], m_ref[...], v_ref[...])

    out_shape = [jax.ShapeDtypeStruct((1, 1), F32)]
    for n in SMALL_ORDER:
        out_shape += [jax.ShapeDtypeStruct(params[n][0].shape, F32)] * 4
    outs = pl.pallas_call(
        body, name="small_update", out_shape=out_shape,
        in_specs=[VMEM_WHOLE] * n_in, out_specs=[VMEM_WHOLE] * len(out_shape),
        scratch_shapes=[pltpu.VMEM(r.shape[1:], F32) for r in srecvs],
        compiler_params=_params(),
    )(*srecvs, wprecv, *flat)
    return outs[0], {n: outs[1 + 4 * i:5 + 4 * i] for i, n in enumerate(SMALL_ORDER)}


def _in_proj(x, g, w_t, deps):
    s = x.shape[0]
    tm = min(ROW_TILE, s)

    def body(x_ref, g_ref, w_ref, z_ref, h_ref):
        xv = x_ref[...]
        h = (xv * _rms(xv) * g_ref[...]).astype(BF16)
        h_ref[...] = h
        z_ref[...] = _mm_nt(h, w_ref[...])

    return _call_behind(
        deps, body, name="in_proj", grid=(s // tm,),
        out_shape=[jax.ShapeDtypeStruct((s, IN_WIDTH), F32), jax.ShapeDtypeStruct((s, D_MODEL), BF16)],
        in_specs=[pl.BlockSpec((tm, D_MODEL), lambda i: (i, 0)), _full((1, D_MODEL)), VMEM_WHOLE],
        out_specs=[pl.BlockSpec((tm, IN_WIDTH), lambda i: (i, 0)), pl.BlockSpec((tm, D_MODEL), lambda i: (i, 0))],
        compiler_params=_params(("parallel",)),
    )(x, g, w_t)


def _chunk_masks():
    row = lax.broadcasted_iota(jnp.int32, (CHUNK, CHUNK), 0)
    col = lax.broadcasted_iota(jnp.int32, (CHUNK, CHUNK), 1)
    return row, col


def _ones_where(mask):
    return jnp.where(mask, 1.0, 0.0).astype(BF16)


def _hgrn_gates(zq, zf, lb):
    sq = _sigmoid(zq)
    sig = _sigmoid(zf)
    f = lb + (1.0 - lb) * sig
    return zq * sq, sq, sig, f


def _sub_chunk_masks(width):
    trow = lax.broadcasted_iota(jnp.int32, (CHUNK, width), 0)
    return [(trow >= SUB * j) & (trow < SUB * (j + 1)) for j in range(N_SUB)]


def _head(a, h):
    return a[:, HD * h:HD * (h + 1)]


def _lanes(parts):
    return jnp.concatenate(parts, axis=1)


def _hgrn_decay_factors(b_scr, r0, b, in_sub):
    bases = [jnp.zeros((1, HW), F32)] + [b_scr[r0 + SUB * j - 1:r0 + SUB * j, :] for j in range(1, N_SUB)]
    own_base = bases[N_SUB - 1]
    for j in range(N_SUB - 2, -1, -1):
        own_base = jnp.where(in_sub[j], bases[j], own_base)
    eq = jnp.exp(b - own_base)
    ek = []
    for j in range(N_SUB):
        upto = SUB * (j + 1)
        e = jnp.exp(jnp.minimum(bases[j] - b[0:upto], EXP_CAP))
        ek.append(e if upto == CHUNK else jnp.concatenate([e, jnp.zeros((CHUNK - upto, HW), F32)], axis=0))
    return eq, ek


def _per_sub_chunk(x, in_sub):
    return _lanes([jnp.where(in_sub[j], x, 0.0) for j in range(N_SUB)])


def _own_lane_block(a, in_sub):
    out = a[:, HD * (N_SUB - 1):HD * N_SUB]
    for j in range(N_SUB - 2, -1, -1):
        out = jnp.where(in_sub[j], a[:, HD * j:HD * (j + 1)], out)
    return out


def _head_rms(o):
    return _lanes([jnp.broadcast_to(_rms(_head(o, h)), (CHUNK, HD)) for h in range(HEADS)])


def _head_mean(a):
    return _lanes([jnp.broadcast_to(jnp.mean(_head(a, h), axis=-1, keepdims=True), (CHUNK, HD)) for h in range(HEADS)])


def _hgrn_fwd(z, lb_logits, gn):
    s = z.shape[0]
    n_chunks = s // CHUNK

    def body(zq_ref, zf_ref, zi_ref, zg_ref, lbl_ref, gn_ref, oa_ref, o_ref, st_ref, state, b_scr):
        @pl.when(pl.program_id(0) == 0)
        def _():
            state[...] = jnp.zeros_like(state)

        lb = _sigmoid(lbl_ref[0:1, :] - lbl_ref[1:2, :])
        row, col = _chunk_masks()
        causal = col <= row
        tri = _ones_where(causal)
        in_sub, in_sub_head = _sub_chunk_masks(HW), _sub_chunk_masks(HD)
        gn_row = _lanes([gn_ref[h:h + 1, :] for h in range(HEADS)])
        def front(c):
            r0 = CHUNK * c
            rs = slice(r0, r0 + CHUNK)
            q, _, _, f = _hgrn_gates(zq_ref[rs, :], zf_ref[rs, :], lb)
            kk = 1.0 - f
            b = _tri_dot(tri, jnp.log(f), 3)
            b_scr[rs, :] = b
            eq, ek = _hgrn_decay_factors(b_scr, r0, b, in_sub)
            b_last = b_scr[r0 + CHUNK - 1:r0 + CHUNK, :]
            qe = q * eq
            return {"rs": rs, "v": zi_ref[rs, :], "qg": q * jnp.exp(b), "kd": kk * jnp.exp(b_last - b),
                    "lam_last": jnp.exp(b_last),
                    "q16": [_per_sub_chunk(_head(qe, h), in_sub_head).astype(BF16) for h in range(HEADS)],
                    "ke16": [_lanes([_head(kk * e, h) for e in ek]).astype(BF16) for h in range(HEADS)]}

        def recurrence(c, p):
            st_ref[c] = state[...]
            a, o_inter = [], []
            for h in range(HEADS):
                vh, st = _head(p["v"], h), state[h]
                a.append(jnp.where(causal, _mm_nt(p["q16"][h], p["ke16"][h]), 0.0))
                o_inter.append(_mm_nt(_head(p["qg"], h), st))
                state[h] = st * _head(p["lam_last"], h) + _mm_tn(vh, _head(p["kd"], h))
            return _lanes([_mm(a[h], _head(p["v"], h)) + o_inter[h] for h in range(HEADS)])

        def back(p, o):
            rs = p["rs"]
            o_ref[rs, :] = o
            zg = zg_ref[rs, :]
            oa_ref[rs, :] = (o * _head_rms(o) * gn_row * zg * _sigmoid(zg)).astype(BF16)

        p = front(0)
        for c in range(CHUNKS_PER_STEP):
            o = recurrence(c, p)
            p_next = front(c + 1) if c + 1 < CHUNKS_PER_STEP else None
            back(p, o)
            p = p_next

    rows = CHUNK * CHUNKS_PER_STEP
    zspec = lambda cb: pl.BlockSpec((rows, HW), lambda i, cb=cb: (i, cb))
    return pl.pallas_call(
        body, name="hgrn_fwd", grid=(s // rows,),
        out_shape=[jax.ShapeDtypeStruct((s, 2 * HW), BF16), jax.ShapeDtypeStruct((s, HW), F32),
                   jax.ShapeDtypeStruct((n_chunks, HEADS, HD, HD), F32)],
        in_specs=[zspec(0), zspec(1), zspec(2), zspec(3), _full((2, HW)), _full((HEADS, HD))],
        out_specs=[pl.BlockSpec((rows, HW), lambda i: (i, 0)), pl.BlockSpec((rows, HW), lambda i: (i, 0)),
                   pl.BlockSpec((CHUNKS_PER_STEP, HEADS, HD, HD), lambda i: (i, 0, 0, 0))],
        scratch_shapes=[pltpu.VMEM((HEADS, HD, HD), F32), pltpu.VMEM((rows, HW), F32)],
        compiler_params=_params(("arbitrary",)),
    )(z, z, z, z, lb_logits, gn)


def _pool_counts(tile_idx, tm):
    t = tile_idx * tm + lax.broadcasted_iota(jnp.int32, (tm, 1), 0)
    return [1.0 / jnp.minimum(t + 1, w).astype(F32) for w in POOL_WINDOWS]


def _pool_fwd(z, w_pool, scale, mixed_in, deps):
    s = z.shape[0]
    tm = min(ROW_TILE, s)

    def body(p_ref, w_ref, sc_ref, mixin_ref, ob_ref, pooled_ref, ext):
        i = pl.program_id(0)

        @pl.when(i == 0)
        def _():
            ext[0:POOL_HALO, :] = jnp.zeros((POOL_HALO, HW), F32)

        @pl.when(i > 0)
        def _():
            ext[0:POOL_HALO, :] = ext[tm:tm + POOL_HALO, :]

        ext[POOL_HALO:POOL_HALO + tm, :] = p_ref[...]
        inv = _pool_counts(i, tm)
        for g, w in enumerate(POOL_WINDOWS):
            sl = slice(HD * g, HD * (g + 1))
            p = ext[POOL_HALO:POOL_HALO + tm, sl]
            win = p
            for d in range(1, w):
                win = win + ext[POOL_HALO - d:POOL_HALO - d + tm, sl]
            pooled = (win * inv[g] - p).astype(BF16)
            pooled_ref[:, sl] = pooled
            ob_ref[:, sl] = (_mm(pooled, w_ref[g]) * sc_ref[:, sl]).astype(BF16)

    return _call_behind(
        deps, body, name="pool_fwd", grid=(s // tm,),
        out_shape=[jax.ShapeDtypeStruct((s, 2 * HW), BF16), jax.ShapeDtypeStruct((s, HW), BF16)],
        in_specs=[pl.BlockSpec((tm, HW), lambda i: (i, 4)), _full((HEADS, HD, HD)), _full((1, HW)), ANY_SPACE],
        out_specs=[pl.BlockSpec((tm, HW), lambda i: (i, 1)), pl.BlockSpec((tm, HW), lambda i: (i, 0))],
        scratch_shapes=[pltpu.VMEM((tm + POOL_HALO, HW), F32)],
        input_output_aliases={3: 0},
        compiler_params=_params(("arbitrary",)),
    )(z, w_pool, scale, mixed_in)


def _mem_kv(mem, g, wk, wv, deps):
    def body(m_ref, g_ref, wk_ref, wv_ref, hm_ref, k_ref, v_ref):
        m = m_ref[...]
        hm = (m * _rms(m) * g_ref[...]).astype(BF16)
        hm_ref[...] = hm
        k_ref[...] = _mm(hm, wk_ref[...]).astype(BF16)
        v_ref[...] = _mm(hm, wv_ref[...]).astype(BF16)

    shp = jax.ShapeDtypeStruct((MEM_LEN, D_MODEL), BF16)
    return _call_behind(
        deps, body, name="mem_kv", out_shape=[shp, shp, shp],
        in_specs=[VMEM_WHOLE] * 4, out_specs=[VMEM_WHOLE] * 3,
        compiler_params=_params(),
    )(mem, g, wk, wv)


def _softmax_rows(sc):
    e = jnp.exp(sc - jnp.max(sc, axis=-1, keepdims=True))
    return e / jnp.sum(e, axis=-1, keepdims=True)


def _mix_xattn_fwd(x0, mixed, w_out, g, wq, xk, xv, wo_t, deps):
    s = x0.shape[0]
    tm = min(ROW_TILE, s)
    scale = XHD ** -0.5

    def body(x_ref, mix_ref, wout_ref, g_ref, wq_ref, k_ref, v_ref, wo_ref, x1_ref, o_ref, hq_ref, q_ref, att_ref):
        xv_ = x_ref[...] + _mm(mix_ref[...], wout_ref[...])
        x1_ref[...] = xv_
        hq = (xv_ * _rms(xv_) * g_ref[...]).astype(BF16)
        hq_ref[...] = hq
        q_ref[...] = (_mm(hq, wq_ref[...]) * scale).astype(BF16)
        heads = [slice(XHD * h, XHD * (h + 1)) for h in range(HEADS)]
        scores = [_mm_nt(q_ref[:, sl], k_ref[:, sl]) for sl in heads]
        probs = [_softmax_rows(sc) for sc in scores]
        for sl, p in zip(heads, probs):
            att_ref[:, sl] = _mm(p, v_ref[:, sl]).astype(BF16)
        o_ref[...] = xv_ + _mm_nt(att_ref[...], wo_ref[...])

    row_f32 = pl.BlockSpec((tm, D_MODEL), lambda i: (i, 0))
    bshape = jax.ShapeDtypeStruct((s, D_MODEL), BF16)
    fshape = jax.ShapeDtypeStruct((s, D_MODEL), F32)
    return _call_behind(
        deps, body, name="mix_xattn_fwd", grid=(s // tm,),
        out_shape=[fshape, fshape, bshape, bshape, bshape],
        in_specs=[row_f32, row_f32, VMEM_WHOLE, _full((1, D_MODEL)), VMEM_WHOLE, VMEM_WHOLE, VMEM_WHOLE, VMEM_WHOLE],
        out_specs=[row_f32] * 5,
        compiler_params=_params(("parallel",)),
    )(x0, mixed, w_out, g, wq, xk, xv, wo_t)


def _mlp_fwd_loss(x, g, w1, w2, gf, target):
    s = x.shape[0]
    tm = min(ROW_TILE, s)

    def body(x_ref, g_ref, w1_ref, w2_ref, gf_ref, t_ref, dx_ref, dx16_ref, u_ref, hf_ref, slot_ref):
        @pl.when(pl.program_id(0) == 0)
        def _():
            slot_ref[...] = jnp.zeros_like(slot_ref)

        xv = x_ref[...]
        hf = (xv * _rms(xv) * g_ref[...]).astype(BF16)
        hf_ref[...] = hf
        a_next = _mm(hf, w1_ref[0])
        for j in range(N_DEV):
            a = jnp.maximum(a_next, 0.0)
            if j + 1 < N_DEV:
                a_next = _mm(hf, w1_ref[j + 1])
            u_ref[:, FF_BLK * j:FF_BLK * (j + 1)] = (a * a).astype(BF16)
        acc = xv + _mm(u_ref[...], w2_ref[...])
        gfv = gf_ref[...]
        r = _rms(acc)
        n = acc * r
        err = n * gfv - t_ref[...]
        slot_ref[1:2, :] += jnp.sum(jnp.mean(err * err, axis=-1, keepdims=True), axis=0, keepdims=True) * 0.5
        dy = err * (1.0 / D_MODEL)
        slot_ref[0:1, :] += jnp.sum(dy * n, axis=0, keepdims=True)
        dn = dy * gfv
        dx = r * (dn - n * jnp.mean(dn * n, axis=-1, keepdims=True))
        dx_ref[...] = dx
        dx16_ref[...] = dx.astype(BF16)

    row_f32 = pl.BlockSpec((tm, D_MODEL), lambda i: (i, 0))
    return pl.pallas_call(
        body, name="mlp_fwd_loss", grid=(s // tm,),
        out_shape=[jax.ShapeDtypeStruct((s, D_MODEL), F32), jax.ShapeDtypeStruct((s, D_MODEL), BF16),
                   jax.ShapeDtypeStruct((s, D_FF), BF16), jax.ShapeDtypeStruct((s, D_MODEL), BF16),
                   jax.ShapeDtypeStruct((SLOT, D_MODEL), F32)],
        in_specs=[row_f32, _full((1, D_MODEL)), VMEM_WHOLE, VMEM_WHOLE, _full((1, D_MODEL)), row_f32],
        out_specs=[row_f32, row_f32, pl.BlockSpec((tm, D_FF), lambda i: (i, 0)), row_f32, _full((SLOT, D_MODEL))],
        compiler_params=_params(("arbitrary",)),
    )(x, g, w1, w2, gf, target)


def _zero_slot(slot_ref):
    @pl.when(pl.program_id(0) == 0)
    def _():
        slot_ref[...] = jnp.zeros_like(slot_ref)


def _mlp_bwd(dx3, u, x2, g, w1, w2, deps):
    s = x2.shape[0]
    tm = min(ROW_TILE // 2, s)

    def body(d_ref, u_ref, x_ref, g_ref, w1_ref, w2_ref, da_ref, dx_ref, slot_ref):
        _zero_slot(slot_ref)
        d = d_ref[...]
        d16 = d.astype(BF16)
        du_next = _mm_nt(d16, w2_ref[0])
        dhf = jnp.zeros((tm, D_MODEL), F32)
        for j in range(N_DEV):
            sl = slice(FF_BLK * j, FF_BLK * (j + 1))
            du = du_next
            if j + 1 < N_DEV:
                du_next = _mm_nt(d16, w2_ref[j + 1])
            u = u_ref[:, sl].astype(F32)
            da = (du * (2.0 * u * lax.rsqrt(jnp.maximum(u, TINY)))).astype(BF16)
            da_ref[:, sl] = da
            dhf = dhf + _mm_nt(da, w1_ref[j])
        dx, dg = _rms_bwd(x_ref[...], g_ref[...], dhf)
        dx_ref[...] = d + dx
        slot_ref[0:1, :] += dg

    row_f32 = pl.BlockSpec((tm, D_MODEL), lambda i: (i, 0))
    return _call_behind(
        deps, body, name="mlp_bwd", grid=(s // tm,),
        out_shape=[jax.ShapeDtypeStruct((s, D_FF), BF16), jax.ShapeDtypeStruct((s, D_MODEL), F32),
                   jax.ShapeDtypeStruct((SLOT, D_MODEL), F32)],
        in_specs=[row_f32, pl.BlockSpec((tm, D_FF), lambda i: (i, 0)), row_f32, _full((1, D_MODEL)),
                  VMEM_WHOLE, VMEM_WHOLE],
        out_specs=[pl.BlockSpec((tm, D_FF), lambda i: (i, 0)), row_f32, _full((SLOT, D_MODEL))],
        compiler_params=_params(("arbitrary",)),
    )(dx3, u, x2, g, w1, w2)


def _wgrad(a, b, name, col_blocks=False):
    s, m = a.shape
    n = b.shape[1]
    tm = 1280 if m % 1280 == 0 else min(1024, m)
    tn = min(1024, n)
    blk = n // N_DEV
    per_step = tn // blk if col_blocks else 1
    ts = min((4 if m * n >= D_MODEL * D_FF else 2) * ROW_TILE, s)
    n_s = s // ts

    def body(a_ref, b_ref, o_ref, acc):
        k = pl.program_id(2)

        @pl.when(k == 0)
        def _():
            acc[...] = jnp.zeros_like(acc)

        acc[...] += _mm_tn(a_ref[...], b_ref[...])

        @pl.when(k == n_s - 1)
        def _():
            if col_blocks:
                for p in range(per_step):
                    o_ref[p] = acc[:, blk * p:blk * (p + 1)].astype(BF16)
            else:
                o_ref[...] = acc[...].astype(BF16)

    if col_blocks:
        out_shape = jax.ShapeDtypeStruct((N_DEV, m, blk), BF16)
        out_spec = pl.BlockSpec((per_step, tm, blk), lambda i, j, k: (j, i, 0))
    else:
        out_shape = jax.ShapeDtypeStruct((m, n), BF16)
        out_spec = pl.BlockSpec((tm, tn), lambda i, j, k: (i, j))
    return pl.pallas_call(
        body, name=name, grid=(m // tm, n // tn, n_s), out_shape=out_shape,
        in_specs=[pl.BlockSpec((ts, tm), lambda i, j, k: (k, i)), pl.BlockSpec((ts, tn), lambda i, j, k: (k, j))],
        out_specs=out_spec,
        scratch_shapes=[pltpu.VMEM((tm, tn), F32)],
        compiler_params=_params(("parallel", "parallel", "arbitrary")),
    )(a, b)


def _xattn_bwd(dx2, x1, g, q, xk, xv, wq, wo_t, deps):
    s = x1.shape[0]
    tm = min(ROW_TILE, s)
    scale = XHD ** -0.5

    def body(d_ref, x_ref, g_ref, q_ref, k_ref, v_ref, wq_ref, wo_ref, dx_ref, dx16_ref, dq_ref, dk_ref, dv_ref, slot_ref,
             datt):
        _zero_slot(slot_ref)

        @pl.when(pl.program_id(0) == 0)
        def _():
            dk_ref[...] = jnp.zeros_like(dk_ref)
            dv_ref[...] = jnp.zeros_like(dv_ref)

        d = d_ref[...]
        datt[...] = _mm(d, wo_ref[...]).astype(BF16)
        heads = [slice(XHD * h, XHD * (h + 1)) for h in range(HEADS)]
        scores = [_mm_nt(q_ref[:, sl], k_ref[:, sl]) for sl in heads]
        dps = [_mm_nt(datt[:, sl], v_ref[:, sl]) for sl in heads]
        probs = [_softmax_rows(sc) for sc in scores]
        dss = [(p * (dp - jnp.sum(dp * p, axis=-1, keepdims=True))).astype(BF16) for p, dp in zip(probs, dps)]
        for sl, p, ds in zip(heads, probs, dss):
            dq_ref[:, sl] = (_mm(ds, k_ref[:, sl]) * scale).astype(BF16)
            dk_ref[:, sl] += _mm_tn(ds, q_ref[:, sl])
            dv_ref[:, sl] += _mm_tn(p, datt[:, sl])
        dx, dg = _rms_bwd(x_ref[...], g_ref[...], _mm_nt(dq_ref[...], wq_ref[...]))
        dx_ref[...] = d + dx
        dx16_ref[...] = (d + dx).astype(BF16)
        slot_ref[0:1, :] += dg

    row_f32 = pl.BlockSpec((tm, D_MODEL), lambda i: (i, 0))
    kv = jax.ShapeDtypeStruct((MEM_LEN, D_MODEL), F32)
    tokens16 = jax.ShapeDtypeStruct((s, D_MODEL), BF16)
    return _call_behind(
        deps, body, name="xattn_bwd", grid=(s // tm,),
        out_shape=[jax.ShapeDtypeStruct((s, D_MODEL), F32), tokens16, tokens16, kv, kv,
                   jax.ShapeDtypeStruct((SLOT, D_MODEL), F32)],
        in_specs=[row_f32, row_f32, _full((1, D_MODEL)), row_f32, VMEM_WHOLE, VMEM_WHOLE, VMEM_WHOLE, VMEM_WHOLE],
        out_specs=[row_f32, row_f32, row_f32, _full((MEM_LEN, D_MODEL)), _full((MEM_LEN, D_MODEL)),
                   _full((SLOT, D_MODEL))],
        scratch_shapes=[pltpu.VMEM((tm, D_MODEL), BF16)],
        compiler_params=_params(("arbitrary",)),
    )(dx2, x1, g, q, xk, xv, wq, wo_t)


def _mem_bwd(mem, g, hm, dxk, dxv, wk, wv):
    def body(m_ref, g_ref, hm_ref, dk_ref, dv_ref, wk_ref, wv_ref, dwk_ref, dwv_ref, slot_ref):
        dk, dv = dk_ref[...], dv_ref[...]
        hm_ = hm_ref[...]
        dwk_ref[...] = _mm_tn(hm_, dk).astype(BF16)
        dwv_ref[...] = _mm_tn(hm_, dv).astype(BF16)
        _, dg = _rms_bwd(m_ref[...], g_ref[...], _mm_nt(dk, wk_ref[...]) + _mm_nt(dv, wv_ref[...]))
        slot_ref[...] = jnp.zeros_like(slot_ref)
        slot_ref[0:1, :] = dg

    wshape = jax.ShapeDtypeStruct((D_MODEL, D_MODEL), BF16)
    return pl.pallas_call(
        body, name="mem_bwd", out_shape=[wshape, wshape, jax.ShapeDtypeStruct((SLOT, D_MODEL), F32)],
        in_specs=[VMEM_WHOLE] * 7, out_specs=[VMEM_WHOLE] * 3,
        compiler_params=_params(),
    )(mem, g, hm, dxk, dxv, wk, wv)


def _pool_bwd(dx1, w_out, pooled, w_pool, scale, deps):
    s = dx1.shape[0]
    tm = min(ROW_TILE, s)
    n_t = s // tm

    def body(dx_ref, wo_ref, pl_ref, w_ref, sc_ref, dz_ref, dw_ref, slot_ref, ext, do_ref):
        i = pl.program_id(0)
        tile = n_t - 1 - i
        _zero_slot(slot_ref)
        do_ref[...] = _mm_nt(dx_ref[...], wo_ref[HW:2 * HW, :])

        @pl.when(i == 0)
        def _():
            dw_ref[...] = jnp.zeros_like(dw_ref)
            ext[tm:tm + POOL_HALO, :] = jnp.zeros((POOL_HALO, HW), F32)

        @pl.when(i > 0)
        def _():
            ext[tm:tm + POOL_HALO, :] = ext[0:POOL_HALO, :]

        inv = _pool_counts(tile, tm)
        dpooled = []
        for g in range(HEADS):
            sl = slice(HD * g, HD * (g + 1))
            pooled_g = pl_ref[:, sl]
            do = do_ref[:, sl]
            slot_ref[0:1, sl] += jnp.sum(_mm(pooled_g, w_ref[g]) * do, axis=0, keepdims=True)
            dy = (do * sc_ref[:, sl]).astype(BF16)
            dw_ref[g] += _mm_tn(pooled_g, dy)
            dpo = _mm_nt(dy, w_ref[g])
            dpooled.append(dpo)
            ext[0:tm, sl] = dpo * inv[g]
        for g, w in enumerate(POOL_WINDOWS):
            sl = slice(HD * g, HD * (g + 1))
            win = ext[0:tm, sl]
            for d in range(1, w):
                win = win + ext[d:d + tm, sl]
            dz_ref[:, sl] = (win - dpooled[g]).astype(BF16)

    return _call_behind(
        deps, body, name="pool_bwd", grid=(n_t,),
        out_shape=[jax.ShapeDtypeStruct((s, IN_WIDTH), BF16), jax.ShapeDtypeStruct((HEADS, HD, HD), F32),
                   jax.ShapeDtypeStruct((SLOT, D_MODEL), F32)],
        in_specs=[pl.BlockSpec((tm, D_MODEL), lambda i: (n_t - 1 - i, 0)), VMEM_WHOLE,
                  pl.BlockSpec((tm, HW), lambda i: (n_t - 1 - i, 0)), _full((HEADS, HD, HD)), _full((1, HW))],
        out_specs=[pl.BlockSpec((tm, HW), lambda i: (n_t - 1 - i, 4)), _full((HEADS, HD, HD)), _full((SLOT, D_MODEL))],
        scratch_shapes=[pltpu.VMEM((tm + POOL_HALO, HW), F32), pltpu.VMEM((tm, HW), F32)],
        compiler_params=_params(("arbitrary",)),
    )(dx1, w_out, pooled, w_pool, scale)


def _hgrn_bwd(z, o, dx1, w_out, states, lb_logits, gn, dz_in, deps):
    s = z.shape[0]
    n_chunks = s // CHUNK

    def body(zq_ref, zf_ref, zi_ref, zg_ref, o_ref, dx_ref, wo_ref, st_ref, lbl_ref, gn_ref, dzin_ref,
             dz_ref, dlb_ref, dgn_ref, dstate, b_scr, dlb_acc, do_ref):
        i = pl.program_id(0)

        @pl.when(i == 0)
        def _():
            dstate[...] = jnp.zeros_like(dstate)
            dlb_acc[...] = jnp.zeros_like(dlb_acc)
            dgn_ref[...] = jnp.zeros_like(dgn_ref)
            dlb_ref[...] = jnp.zeros_like(dlb_ref)

        do_ref[...] = _mm_nt(dx_ref[...], wo_ref[0:HW, :])
        lb = _sigmoid(lbl_ref[0:1, :] - lbl_ref[1:2, :])
        row, col = _chunk_masks()
        causal = col <= row
        tri = _ones_where(causal)
        upper = _ones_where(col >= row)
        strict_lower = _ones_where(col < row)
        in_sub, in_sub_head = _sub_chunk_masks(HW), _sub_chunk_masks(HD)
        gn_row = _lanes([gn_ref[h:h + 1, :] for h in range(HEADS)])
        sums = {"dlb": 0.0, "dgn": 0.0}

        def front(c):
            r0 = CHUNK * c
            rs = slice(r0, r0 + CHUNK)
            p = {"rs": rs}
            p["zq"] = zq_ref[rs, :]
            p["q"], p["sq"], p["sig"], p["f"] = _hgrn_gates(p["zq"], zf_ref[rs, :], lb)
            p["kk"] = 1.0 - p["f"]
            b = _tri_dot(tri, jnp.log(p["f"]), 3)
            b_scr[rs, :] = b
            p["v"] = zi_ref[rs, :]
            o, zg, doa = o_ref[rs, :], zg_ref[rs, :], do_ref[rs, :]
            sg = _sigmoid(zg)
            rms = _head_rms(o)
            n = o * rms
            don = doa * (zg * sg)
            sums["dgn"] = sums["dgn"] + jnp.sum(don * n, axis=0, keepdims=True)
            dn = don * gn_row
            p["d_o"] = rms * (dn - n * _head_mean(dn * n))
            dz_ref[rs, 3 * HW:4 * HW] = (doa * (n * gn_row) * (sg * (1.0 + zg * (1.0 - sg)))).astype(BF16)
            p["eq"], p["ek"] = _hgrn_decay_factors(b_scr, r0, b, in_sub)
            b_last = b_scr[r0 + CHUNK - 1:r0 + CHUNK, :]
            p["lam"], p["e_last"], p["lam_last"] = jnp.exp(b), jnp.exp(b_last - b), jnp.exp(b_last)
            p["qe"], p["qg"], p["kd"] = p["q"] * p["eq"], p["q"] * p["lam"], p["kk"] * p["e_last"]
            p["ke"] = [p["kk"] * e for e in p["ek"]]
            p["q16"] = [_per_sub_chunk(_head(p["qe"], h), in_sub_head).astype(BF16) for h in range(HEADS)]
            p["ke16"] = [_lanes([_head(p["ke"][j], h) for j in range(N_SUB)]).astype(BF16) for h in range(HEADS)]
            return p

        def recurrence(c, p):
            m = {k: [] for k in ("dv", "gq", "gk", "dqi", "dkd", "st")}
            a, da, dv_state = [], [], []
            for h in range(HEADS):
                vh, doh = _head(p["v"], h), _head(p["d_o"], h)
                st0, ds1 = st_ref[c, h], dstate[h]
                a.append(jnp.where(causal, _mm_nt(p["q16"][h], p["ke16"][h]), 0.0))
                da.append(jnp.where(causal, _mm_nt(doh, vh), 0.0))
                dv_state.append(_mm_nt(_head(p["kd"], h), ds1))
                m["dqi"].append(_mm(doh, st0))
                m["dkd"].append(_mm(vh, ds1))
                m["st"].append(jnp.sum(st0 * ds1, axis=0, keepdims=True))
                dstate[h] = ds1 * _head(p["lam_last"], h) + _mm_tn(doh, _head(p["qg"], h))
            for h in range(HEADS):
                m["dv"].append(_mm_tn(a[h], _head(p["d_o"], h)) + dv_state[h])
                m["gq"].append(_own_lane_block(_mm(da[h], p["ke16"][h]), in_sub_head))
                m["gk"].append(_mm_tn(da[h], p["q16"][h]))
            return m

        def back(p, m):
            rs = p["rs"]
            dz_ref[rs, 2 * HW:3 * HW] = _lanes(m["dv"]).astype(BF16)
            gq = _lanes(m["gq"])
            gk = [_lanes([m["gk"][h][:, HD * j:HD * (j + 1)] for h in range(HEADS)]) for j in range(N_SUB)]
            dq_inter = p["lam"] * _lanes(m["dqi"])
            dq = p["eq"] * gq + dq_inter
            dk_intra = sum(p["ek"][j] * gk[j] for j in range(N_SUB))
            dk_state = _lanes(m["dkd"]) * p["e_last"]
            db_intra = (p["qe"].astype(BF16).astype(F32) * gq
                        - sum(p["ke"][j].astype(BF16).astype(F32) * gk[j] for j in range(N_SUB)))
            dlf = (_tri_dot(upper, db_intra + p["q"] * dq_inter, 2) + _tri_dot(strict_lower, p["kk"] * dk_state, 2)
                   + p["lam_last"] * _lanes(m["st"]))
            sig, sq, zq = p["sig"], p["sq"], p["zq"]
            df = dlf / p["f"] - (dk_intra + dk_state)
            sums["dlb"] = sums["dlb"] + jnp.sum(df * (1.0 - sig), axis=0, keepdims=True)
            dz_ref[rs, HW:2 * HW] = (df * (1.0 - lb) * sig * (1.0 - sig)).astype(BF16)
            dz_ref[rs, 0:HW] = (dq * (sq * (1.0 + zq * (1.0 - sq)))).astype(BF16)

        p = front(CHUNKS_PER_STEP - 1)
        for c in reversed(range(CHUNKS_PER_STEP)):
            m = recurrence(c, p)
            p_next = front(c - 1) if c > 0 else None
            back(p, m)
            p = p_next
        dlb_acc[...] += sums["dlb"]
        for h in range(HEADS):
            dgn_ref[h:h + 1, 0:HD] += _head(sums["dgn"], h)

        @pl.when(i == n_steps - 1)
        def _():
            dl0 = dlb_acc[...] * lb * (1.0 - lb)
            dlb_ref[0:1, 0:HW] = dl0
            dlb_ref[1:2, 0:HW] = -dl0

    rows = CHUNK * CHUNKS_PER_STEP
    n_steps = s // rows
    rev = lambda i: n_steps - 1 - i
    zspec = lambda cb: pl.BlockSpec((rows, HW), lambda i, cb=cb: (rev(i), cb))
    slot = jax.ShapeDtypeStruct((SLOT, D_MODEL), F32)
    return _call_behind(
        deps, body, name="hgrn_bwd", grid=(n_steps,),
        out_shape=[jax.ShapeDtypeStruct((s, IN_WIDTH), BF16), slot, slot],
        in_specs=[zspec(0), zspec(1), zspec(2), zspec(3), pl.BlockSpec((rows, HW), lambda i: (rev(i), 0)),
                  pl.BlockSpec((rows, D_MODEL), lambda i: (rev(i), 0)), VMEM_WHOLE,
                  pl.BlockSpec((CHUNKS_PER_STEP, HEADS, HD, HD), lambda i: (rev(i), 0, 0, 0)), _full((2, HW)),
                  _full((HEADS, HD)), ANY_SPACE],
        out_specs=[pl.BlockSpec((rows, 4 * HW), lambda i: (rev(i), 0)), _full((SLOT, D_MODEL)), _full((SLOT, D_MODEL))],
        scratch_shapes=[pltpu.VMEM((HEADS, HD, HD), F32), pltpu.VMEM((rows, HW), F32), pltpu.VMEM((1, HW), F32),
                        pltpu.VMEM((rows, HW), F32)],
        input_output_aliases={10: 0},
        compiler_params=_params(("arbitrary",)),
    )(z, z, z, z, o, dx1, w_out, states, lb_logits, gn, dz_in)


def _in_bwd(dz, w_t, x0, g, dx1, deps):
    s = x0.shape[0]
    tm = min(WIDE_ROW_TILE, s)

    def body(dz_ref, w_ref, x_ref, g_ref, d_ref, dx_ref, slot_ref):
        _zero_slot(slot_ref)
        dx, dg = _rms_bwd(x_ref[...], g_ref[...], _mm(dz_ref[...], w_ref[...]))
        dx_ref[...] = d_ref[...] + dx
        slot_ref[0:1, :] += dg

    row_f32 = pl.BlockSpec((tm, D_MODEL), lambda i: (i, 0))
    return _call_behind(
        deps, body, name="in_bwd", grid=(s // tm,),
        out_shape=[jax.ShapeDtypeStruct((s, D_MODEL), F32), jax.ShapeDtypeStruct((SLOT, D_MODEL), F32)],
        in_specs=[pl.BlockSpec((tm, IN_WIDTH), lambda i: (i, 0)), VMEM_WHOLE, row_f32, _full((1, D_MODEL)), row_f32],
        out_specs=[row_f32, _full((SLOT, D_MODEL))],
        compiler_params=_params(("arbitrary",)),
    )(dz, w_t, x0, g, dx1)


def kernel(x, mem, norm_mix_g, w_in, lb_logits, hgrn_norm_g, w_pool, pool_scale, w_out, norm_x_g, norm_mem_g, w_xq, w_xk, w_xv, w_xo, norm_ffn_g, w_ff1, w_ff2, final_norm_g, loss_target, m_norm_mix_g, m_w_in, m_lb_logits, m_hgrn_norm_g, m_w_pool, m_pool_scale, m_w_out, m_norm_x_g, m_norm_mem_g, m_w_xq, m_w_xk, m_w_xv, m_w_xo, m_norm_ffn_g, m_w_ff1, m_w_ff2, m_final_norm_g, v_norm_mix_g, v_w_in, v_lb_logits, v_hgrn_norm_g, v_w_pool, v_pool_scale, v_w_out, v_norm_x_g, v_norm_mem_g, v_w_xq, v_w_xk, v_w_xv, v_w_xo, v_norm_ffn_g, v_w_ff1, v_w_ff2, v_final_norm_g):
    x0 = x[0]
    mem0 = mem[0]
    tgt = loss_target[0]
    gn = hgrn_norm_g[0]
    gfin = final_norm_g.reshape(1, D_MODEL)
    wp = w_pool[0]
    heads_2d = lambda w: w.reshape(D_MODEL // N_DEV, D_MODEL)
    xo_2d = lambda w: w.reshape(D_MODEL, D_MODEL // N_DEV)

    first = _all_gather_weights([w_in[0].T], [w_out[0], heads_2d(w_xq), heads_2d(w_xk), heads_2d(w_xv), xo_2d(w_xo).T,
                                              w_ff1[0], w_ff2[0]])
    win_t = first[0].reshape(IN_WIDTH, D_MODEL)
    ga_attn, ga_mlp = _gather_first_start([first[1:6], first[6:8]], "gather_first_start")

    z, h = _in_proj(x0, norm_mix_g, win_t, deps=[ga_attn[3]])
    mixed_a, o_pre, states = _hgrn_fwd(z, lb_logits, gn)
    lands = _split_wait(_gather_first_copies, ga_attn, o_pre, "gather_attn_first_wait")
    gb_attn = _gather_forward_start(lands, "gather_attn_forward_start")
    mixed, pooled = _pool_fwd(z, wp, pool_scale, mixed_a, deps=[gb_attn[3]])
    lands = _split_wait(_gather_forward_copies, gb_attn, pooled, "gather_attn_forward_wait")
    wout_f, wq_f, wk_f, wv_f, wo_t = (t.reshape(D_MODEL, D_MODEL) for t in lands)
    hm, xk, xv = _mem_kv(mem0, norm_mem_g, wk_f, wv_f, deps=[])
    x1, x2, hq, xq, att = _mix_xattn_fwd(x0, mixed, wout_f, norm_x_g, wq_f, xk, xv, wo_t, deps=[])
    lands = _split_wait(_gather_first_copies, ga_mlp, x2, "gather_mlp_first_wait")
    gb_mlp = _gather_forward_start(lands, "gather_mlp_forward_start")
    w1_b, w2_b = _split_wait(_gather_forward_copies, gb_mlp, gb_mlp[3], "gather_mlp_forward_wait")
    dx3, dx3_16, u, hf, slot_fin = _mlp_fwd_loss(x2, norm_ffn_g, w1_b, w2_b.reshape(D_FF, D_MODEL), gfin, tgt)

    rows = lambda t, r: t.reshape(N_DEV, r, D_MODEL)
    dw2 = _wgrad(u, dx3_16, "wgrad_ff2")
    ex_ff2 = _all_to_all_start([rows(dw2, FF_BLK)], [], "exchange_ff2_start")
    da, dx2, slot_ffn = _mlp_bwd(dx3, u, x2, norm_ffn_g, w1_b, w2_b, deps=[ex_ff2[3]])
    dw1 = _wgrad(hf, da, "wgrad_ff1", col_blocks=True)
    ex_ff1 = _all_to_all_start([dw1], [], "exchange_ff1_start")
    dx1, dx1_16, dxq, dxk, dxv, slot_x = _xattn_bwd(dx2, x1, norm_x_g, xq, xk, xv, wq_f, wo_t, deps=[ex_ff1[3]])
    dwo_t = _wgrad(dx2, att, "wgrad_xo")
    dwq = _wgrad(hq, dxq, "wgrad_xq")
    dwk, dwv, slot_mem = _mem_bwd(mem0, norm_mem_g, hm, dxk, dxv, wk_f, wv_f)
    ex_attn = _all_to_all_start([rows(dwq, 128), rows(dwk, 128), rows(dwv, 128), rows(dwo_t, 128)], [],
                                "exchange_attn_start")
    dwout = _wgrad(mixed, dx1_16, "wgrad_out")
    dz_pool, d_wpool, slot_ps = _pool_bwd(dx1_16, wout_f, pooled, wp, pool_scale, deps=[ex_attn[3]])
    small0 = jnp.concatenate([slot_x, slot_mem, slot_ffn, slot_fin, slot_ps], axis=0)
    ex_out = _all_to_all_start([rows(dwout, 128)], [small0, d_wpool], "exchange_out_start")
    dz, slot_lb, slot_gn = _hgrn_bwd(z, o_pre, dx1_16, wout_f, states, lb_logits, gn, dz_pool, deps=[ex_out[3]])
    dwin_t = _wgrad(dz, h, "wgrad_in")
    small1 = jnp.concatenate([slot_lb, slot_gn], axis=0)
    ex_in = _all_to_all_start([rows(dwin_t, 320)], [small1], "exchange_in_start")
    grad_x, slot_mix = _in_bwd(dz, win_t, x0, norm_mix_g, dx1, deps=[ex_in[3]])
    ex_mix = _all_to_all_start([], [slot_mix], "exchange_mix_start")

    out = {}
    (r_2,) = _split_wait(_all_to_all_copies(1), ex_ff2, ex_mix[3], "exchange_ff2_wait")
    out["w_ff2"] = _sum_adamw(r_2, w_ff2[0], m_w_ff2[0], v_w_ff2[0], "adamw_ff2")
    (r_1,) = _split_wait(_all_to_all_copies(1), ex_ff1, out["w_ff2"][1], "exchange_ff1_wait")
    out["w_ff1"] = _sum_adamw(r_1, w_ff1[0], m_w_ff1[0], v_w_ff1[0], "adamw_ff1")
    r_q, r_k, r_v, r_o = _split_wait(_all_to_all_copies(4), ex_attn, out["w_ff1"][1], "exchange_attn_wait")
    sums = _sum_sources_whole([r_q, r_k, r_v, r_o], "sum_grad_attn")
    g_attn = [g.reshape(w_xq.shape) for g in sums[:3]] + [sums[3].T]
    attn = _adamw_whole([(g_attn[0], w_xq, m_w_xq, v_w_xq), (g_attn[1], w_xk, m_w_xk, v_w_xk),
                         (g_attn[2], w_xv, m_w_xv, v_w_xv),
                         (g_attn[3], xo_2d(w_xo), xo_2d(m_w_xo), xo_2d(v_w_xo))], "adamw_attn")
    for n, g, res in zip(("w_xq", "w_xk", "w_xv", "w_xo"), g_attn, attn):
        out[n] = (g, *res)
    r_out, r_small0, r_wpool = _split_wait(_all_to_all_copies(1), ex_out, attn[3][0], "exchange_out_wait")
    out["w_out"] = _sum_adamw(r_out, w_out[0], m_w_out[0], v_w_out[0], "adamw_out")
    r_in, r_small1 = _split_wait(_all_to_all_copies(1), ex_in, out["w_out"][1], "exchange_in_wait")
    in_t = _sum_adamw(r_in, w_in[0].T, m_w_in[0].T, v_w_in[0].T, "adamw_in")
    out["w_in"] = tuple(t.T for t in in_t)
    (r_small2,) = _split_wait(_all_to_all_copies(0), ex_mix, in_t[1], "exchange_mix_wait")
    row = lambda t: t.reshape(1, -1)
    small_params = {
        "norm_mix_g": (norm_mix_g, m_norm_mix_g, v_norm_mix_g),
        "lb_logits": (lb_logits, m_lb_logits, v_lb_logits),
        "hgrn_norm_g": (hgrn_norm_g[0], m_hgrn_norm_g[0], v_hgrn_norm_g[0]),
        "pool_scale": (pool_scale, m_pool_scale, v_pool_scale),
        "norm_x_g": (norm_x_g, m_norm_x_g, v_norm_x_g),
        "norm_mem_g": (norm_mem_g, m_norm_mem_g, v_norm_mem_g),
        "norm_ffn_g": (norm_ffn_g, m_norm_ffn_g, v_norm_ffn_g),
        "final_norm_g": (row(final_norm_g), row(m_final_norm_g), row(v_final_norm_g)),
        "w_pool": (wp, m_w_pool[0], v_w_pool[0]),
    }
    loss, small_out = _small_update([r_small0, r_small1, r_small2], r_wpool, small_params)
    out.update(small_out)

    shapes = dict(norm_mix_g=norm_mix_g, w_in=w_in, lb_logits=lb_logits, hgrn_norm_g=hgrn_norm_g, w_pool=w_pool,
                  pool_scale=pool_scale, w_out=w_out, norm_x_g=norm_x_g, norm_mem_g=norm_mem_g, w_xq=w_xq, w_xk=w_xk,
                  w_xv=w_xv, w_xo=w_xo, norm_ffn_g=norm_ffn_g, w_ff1=w_ff1, w_ff2=w_ff2, final_norm_g=final_norm_g)
    order = list(shapes)
    group = lambda k: [out[n][k].reshape(shapes[n].shape) for n in order]
    return (loss.reshape(()), grad_x.reshape(x.shape), *group(0), *group(1), *group(2), *group(3))
```

```python
import jax
import jax.numpy as jnp
from jax import lax
from jax.experimental import pallas as pl
from jax.experimental.pallas import tpu as pltpu

F32 = jnp.float32
BF16 = jnp.bfloat16

D_MODEL = 1024
N_DEV = 8
HEADS = 4
HD = 128
HW = HEADS * HD
IN_WIDTH = 5 * HW
XHD = 256
MEM_LEN = 256
D_FF = 4096
FF_BLK = D_FF // N_DEV
POOL_WINDOWS = (2, 4, 8, 16)
POOL_HALO = 16
CHUNK = 64
CHUNKS_PER_STEP = 8
SUB = 16
N_SUB = CHUNK // SUB
EXP_CAP = 80.0
EPS = 1e-6
TINY = 1e-30
ROW_TILE = 512
WIDE_ROW_TILE = 1024
SLOT = 8
V7X_VMEM_LIMIT = 56 * 1024 * 1024

ADAM_LR = 0.001
ADAM_B1 = 0.9
ADAM_B2 = 0.999
ADAM_EPS = 1e-08
ADAM_WD = 0.01
ADAM_STEP = 10

MESH_ID = pl.DeviceIdType.MESH


def _params(sem=None, vmem=V7X_VMEM_LIMIT):
    return pltpu.CompilerParams(dimension_semantics=sem, vmem_limit_bytes=vmem)


def _mm(a, b):
    return lax.dot_general(a.astype(BF16), b.astype(BF16), (((1,), (0,)), ((), ())), preferred_element_type=F32)


def _mm_nt(a, b):
    return lax.dot_general(a.astype(BF16), b.astype(BF16), (((1,), (1,)), ((), ())), preferred_element_type=F32)


def _mm_tn(a, b):
    return lax.dot_general(a.astype(BF16), b.astype(BF16), (((0,), (0,)), ((), ())), preferred_element_type=F32)


def _sigmoid(x):
    return 1.0 / (1.0 + jnp.exp(-x))


def _rms(x):
    return lax.rsqrt(jnp.mean(x * x, axis=-1, keepdims=True) + EPS)


def _rms_bwd(x, g, dh):
    r = _rms(x)
    n = x * r
    dn = dh * g
    dx = r * (dn - n * jnp.mean(dn * n, axis=-1, keepdims=True))
    return dx, jnp.sum(dh * n, axis=0, keepdims=True)


def _tri_dot(tri, x, passes):
    acc = None
    rest = x
    for _ in range(passes):
        piece = rest.astype(BF16)
        part = lax.dot_general(tri, piece, (((1,), (0,)), ((), ())), preferred_element_type=F32)
        acc = part if acc is None else acc + part
        rest = rest - piece.astype(F32)
    return acc


def _adam_update(g, w, m, v):
    nm = ADAM_B1 * m + (1.0 - ADAM_B1) * g
    nv = ADAM_B2 * v + (1.0 - ADAM_B2) * (g * g)
    m_hat = nm / (1.0 - ADAM_B1 ** ADAM_STEP)
    v_hat = nv / (1.0 - ADAM_B2 ** ADAM_STEP)
    return -ADAM_LR * (m_hat / (jnp.sqrt(v_hat) + ADAM_EPS) + ADAM_WD * w), nm, nv


def _full(shape):
    return pl.BlockSpec(shape, lambda *_: (0,) * len(shape))


VMEM_WHOLE = pl.BlockSpec(memory_space=pltpu.VMEM)
ANY_SPACE = pl.BlockSpec(memory_space=pl.ANY)


def _mesh_pos():
    return lax.axis_index("x"), lax.axis_index("y"), lax.axis_index("c")


def _flat(px, py, pc):
    return 4 * px + 2 * py + pc


def _all_gather_weights(shards, cast_only):
    n, nc = len(shards), len(cast_only)
    step = 64

    def body(*refs):
        x_refs, c_refs = refs[:n], refs[n:n + nc]
        out_refs, cast_refs = refs[n + nc:2 * n + nc], refs[2 * n + nc:2 * n + 2 * nc]
        bufs = refs[2 * n + 2 * nc:3 * n + 2 * nc]
        send_sems, recv_sems, local_sems = refs[3 * n + 2 * nc:]
        x, y, c = _mesh_pos()
        me, sibling = (x, y, c), (x, y, 1 - c)
        chips = [(1 - x, y), (x, 1 - y), (1 - x, 1 - y)]

        def copy(a, k, blk, to, src=None):
            rows = out_refs[a].at[_flat(*blk)]
            return pltpu.make_async_remote_copy(
                src_ref=rows if src is None else src, dst_ref=rows,
                send_sem=send_sems.at[7 * a + k], recv_sem=recv_sems.at[7 * a + k], device_id=to, device_id_type=MESH_ID)

        def cast_rows(src, dst, rows):
            def cast(i, carry):
                r0 = pl.multiple_of(i * step, step)
                dst[pl.ds(r0, step), :] = src[pl.ds(r0, step), :].astype(BF16)
                return carry
            lax.fori_loop(0, rows // step, cast, 0)

        first, mine = [], []
        for a in range(n):
            cast_rows(x_refs[a], bufs[a], shards[a].shape[0])
            mine.append(pltpu.make_async_copy(bufs[a], out_refs[a].at[_flat(*me)], local_sems.at[a]))
            first.append(copy(a, 0, me, sibling, src=bufs[a]))
            first += [copy(a, 1 + j, me, (*chip, c), src=bufs[a]) for j, chip in enumerate(chips)]
            for cp in [mine[-1]] + first[-4:]:
                cp.start()
        for a in range(nc):
            cast_rows(c_refs[a], cast_refs[a], cast_only[a].shape[0])
        passed = []
        for j, chip in enumerate(chips):
            for a in range(n):
                copy(a, 1 + j, (*chip, c), me).wait_recv()
                passed.append(copy(a, 4 + j, (*chip, c), sibling))
                passed[-1].start()
        for a in range(n):
            copy(a, 0, sibling, me).wait_recv()
            for j, chip in enumerate(chips):
                copy(a, 4 + j, (*chip, 1 - c), me).wait_recv()
        for cp in first + passed:
            cp.wait_send()
        for cp in mine:
            cp.wait()

    return pl.pallas_call(
        body, name="all_gather_w_in",
        out_shape=[jax.ShapeDtypeStruct((N_DEV,) + s.shape, BF16) for s in shards]
        + [jax.ShapeDtypeStruct(s.shape, BF16) for s in cast_only],
        in_specs=[VMEM_WHOLE] * (n + nc), out_specs=[ANY_SPACE] * n + [VMEM_WHOLE] * nc,
        scratch_shapes=[pltpu.VMEM(s.shape, BF16) for s in shards]
        + [pltpu.SemaphoreType.DMA((7 * n,)), pltpu.SemaphoreType.DMA((7 * n,)), pltpu.SemaphoreType.DMA((n,))],
        compiler_params=_params(),
    )(*shards, *cast_only)


HBM_SPEC = pl.BlockSpec(memory_space=pltpu.HBM)
SEM_SPEC = pl.BlockSpec(memory_space=pltpu.SEMAPHORE)
EFFECT = pltpu.SideEffectType.DATAFLOW_SIDE_EFFECTING
TOKEN = jax.ShapeDtypeStruct((8, 128), F32)


def _in_hbm(a):
    return pltpu.with_memory_space_constraint(a, pltpu.HBM)


def _split_start(copies_of, srcs, lands, n_sems, name):
    ns, nl, k = len(srcs), len(lands), len(n_sems)

    def body(*refs):
        src_refs, land_refs = refs[:ns], refs[ns:ns + nl]
        sems = refs[ns + nl:ns + nl + k]
        token = refs[-1]
        for cp in copies_of(src_refs, land_refs, sems):
            cp.start()
        token[...] = jnp.zeros_like(token)

    outs = pl.pallas_call(
        body, name=name,
        out_shape=[pltpu.SemaphoreType.DMA((q,)) for q in n_sems]
        + [pltpu.HBM(a.shape, a.dtype) for a in list(srcs) + list(lands)] + [TOKEN],
        in_specs=[HBM_SPEC] * (ns + nl),
        out_specs=[SEM_SPEC] * k + [HBM_SPEC] * (ns + nl) + [VMEM_WHOLE],
        input_output_aliases={i: k + i for i in range(ns + nl)},
        compiler_params=pltpu.CompilerParams(has_side_effects=EFFECT),
    )(*[_in_hbm(a) for a in list(srcs) + list(lands)])
    return outs[:k], outs[k:k + ns], outs[k + ns:k + ns + nl], outs[-1]


def _split_wait(copies_of, handle, after, name):
    sems, srcs, lands, _ = handle
    ns, nl, k = len(srcs), len(lands), len(sems)

    def body(*refs):
        src_refs, land_refs = refs[:ns], refs[ns:ns + nl]
        sem_refs = refs[ns + nl:ns + nl + k]
        for cp in copies_of(src_refs, land_refs, sem_refs):
            cp.wait()

    outs = pl.pallas_call(
        body, name=name,
        out_shape=[pltpu.HBM(a.shape, a.dtype) for a in list(srcs) + list(lands)],
        in_specs=[HBM_SPEC] * (ns + nl) + [SEM_SPEC] * k + [ANY_SPACE],
        out_specs=[HBM_SPEC] * (ns + nl),
        input_output_aliases={i: i for i in range(ns + nl)},
        compiler_params=pltpu.CompilerParams(has_side_effects=EFFECT),
    )(*srcs, *lands, *sems, after)
    return outs[ns:]


def _gather_first_copies(shard_refs, land_refs, sems):
    send_sems, recv_sems, local_sems = sems
    x, y, c = _mesh_pos()
    me = _flat(x, y, c)
    peers = [(x, y, 1 - c), (1 - x, y, c), (x, 1 - y, c), (1 - x, 1 - y, c)]
    copies = []
    for a, (shard, land) in enumerate(zip(shard_refs, land_refs)):
        copies.append(pltpu.make_async_copy(shard, land.at[me], local_sems.at[a]))
        for k, peer in enumerate(peers):
            copies.append(pltpu.make_async_remote_copy(
                src_ref=shard, dst_ref=land.at[me], send_sem=send_sems.at[4 * a + k], recv_sem=recv_sems.at[4 * a + k],
                device_id=peer, device_id_type=MESH_ID))
    return copies


def _gather_forward_copies(src_refs, land_refs, sems):
    del src_refs
    send_sems, recv_sems = sems
    x, y, c = _mesh_pos()
    chips = [(1 - x, y), (x, 1 - y), (1 - x, 1 - y)]
    copies = []
    for a, land in enumerate(land_refs):
        for j, chip in enumerate(chips):
            rows = land.at[_flat(*chip, c)]
            copies.append(pltpu.make_async_remote_copy(
                src_ref=rows, dst_ref=rows, send_sem=send_sems.at[3 * a + j], recv_sem=recv_sems.at[3 * a + j],
                device_id=(x, y, 1 - c), device_id_type=MESH_ID))
    return copies


def _gather_first_start(groups, name):
    shards = [s for g in groups for s in g]
    lands = [lax.empty((N_DEV,) + s.shape, s.dtype) for s in shards]
    bounds = [sum(len(g) for g in groups[:i]) for i in range(len(groups) + 1)]

    def copies_of(src_refs, land_refs, sems):
        copies = []
        for i in range(len(groups)):
            lo, hi = bounds[i], bounds[i + 1]
            copies += _gather_first_copies(src_refs[lo:hi], land_refs[lo:hi], sems[3 * i:3 * i + 3])
        return copies

    n_sems = tuple(q for g in groups for q in (4 * len(g), 4 * len(g), len(g)))
    sems, srcs, lands, token = _split_start(copies_of, shards, lands, n_sems, name)
    return [(sems[3 * i:3 * i + 3], srcs[bounds[i]:bounds[i + 1]], lands[bounds[i]:bounds[i + 1]], token)
            for i in range(len(groups))]


def _gather_forward_start(lands, name):
    n = len(lands)
    return _split_start(_gather_forward_copies, [], lands, (3 * n, 3 * n), name)


def _all_to_all_copies(n_scattered):
    def copies_of(src_refs, land_refs, sems):
        send_sems, recv_sems, local_sems = sems
        x, y, c = _mesh_pos()
        me = _flat(x, y, c)
        copies = []
        for a, (src, land) in enumerate(zip(src_refs, land_refs)):
            scattered = a < n_scattered
            copies.append(pltpu.make_async_copy(src.at[me] if scattered else src, land.at[me], local_sems.at[a]))
            for k in range(1, N_DEV):
                peer = (1 - x if k & 4 else x, 1 - y if k & 2 else y, 1 - c if k & 1 else c)
                copies.append(pltpu.make_async_remote_copy(
                    src_ref=src.at[_flat(*peer)] if scattered else src, dst_ref=land.at[me],
                    send_sem=send_sems.at[7 * a + k - 1], recv_sem=recv_sems.at[7 * a + k - 1],
                    device_id=peer, device_id_type=MESH_ID))
        return copies
    return copies_of


def _all_to_all_start(scattered, broadcast, name):
    srcs = list(scattered) + list(broadcast)
    lands = [lax.empty(a.shape, a.dtype) for a in scattered] + [lax.empty((N_DEV,) + a.shape, a.dtype) for a in broadcast]
    n = len(srcs)
    return _split_start(_all_to_all_copies(len(scattered)), srcs, lands, (7 * n, 7 * n, n), name)


def _call_behind(deps, body, *, in_specs, **kwargs):
    n_in, n_dep = len(in_specs), len(deps)

    def body_without_deps(*refs):
        return body(*refs[:n_in], *refs[n_in + n_dep:])

    call = pl.pallas_call(body_without_deps, in_specs=list(in_specs) + [ANY_SPACE] * n_dep, **kwargs)
    return lambda *operands: call(*operands, *deps)


def _row_tile(rows):
    for cand in (256, 128, 64, 32, 16):
        if rows % cand == 0:
            return cand
    return rows


def _adamw_whole(groups, name):
    n = len(groups)

    def body(*refs):
        for i in range(n):
            g_ref, w_ref, m_ref, v_ref = refs[4 * i:4 * i + 4]
            d_ref, nm_ref, nv_ref = refs[4 * n + 3 * i:4 * n + 3 * i + 3]
            d_ref[...], nm_ref[...], nv_ref[...] = _adam_update(g_ref[...], w_ref[...], m_ref[...], v_ref[...])

    outs = pl.pallas_call(
        body, name=name, out_shape=[jax.ShapeDtypeStruct(grp[0].shape, F32) for grp in groups for _ in range(3)],
        in_specs=[VMEM_WHOLE] * (4 * n), out_specs=[VMEM_WHOLE] * (3 * n),
        compiler_params=_params(),
    )(*[t for grp in groups for t in grp])
    return [outs[3 * i:3 * i + 3] for i in range(n)]


def _sum_sources_whole(recvs, name):
    n = len(recvs)

    def body(*refs):
        for r_ref, o_ref in zip(refs[:n], refs[n:]):
            acc = r_ref[0].astype(F32)
            for d in range(1, N_DEV):
                acc = acc + r_ref[d].astype(F32)
            o_ref[...] = acc

    return pl.pallas_call(
        body, name=name, out_shape=[jax.ShapeDtypeStruct(r.shape[1:], F32) for r in recvs],
        in_specs=[VMEM_WHOLE] * n, out_specs=[VMEM_WHOLE] * n,
        compiler_params=_params(),
    )(*recvs)


def _sum_adamw(recv, w, m, v, name):
    _, rows, cols = recv.shape
    tile = _row_tile(rows)

    def body(r_ref, w_ref, m_ref, v_ref, g_ref, d_ref, nm_ref, nv_ref):
        acc = r_ref[0].astype(F32)
        for d in range(1, N_DEV):
            acc = acc + r_ref[d].astype(F32)
        g_ref[...] = acc
        d_ref[...], nm_ref[...], nv_ref[...] = _adam_update(acc, w_ref[...], m_ref[...], v_ref[...])

    spec = pl.BlockSpec((tile, cols), lambda i: (i, 0))
    shp = jax.ShapeDtypeStruct((rows, cols), F32)
    return pl.pallas_call(
        body, name=name, grid=(rows // tile,), out_shape=[shp] * 4,
        in_specs=[pl.BlockSpec((N_DEV, tile, cols), lambda i: (0, i, 0)), spec, spec, spec], out_specs=[spec] * 4,
        compiler_params=_params(("parallel",)),
    )(recv, w, m, v)


SMALL_SLOTS = {"norm_x_g": (0, 0, 1, D_MODEL), "norm_mem_g": (0, 8, 1, D_MODEL), "norm_ffn_g": (0, 16, 1, D_MODEL),
               "final_norm_g": (0, 24, 1, D_MODEL), "pool_scale": (0, 32, 1, HW),
               "lb_logits": (1, 0, 2, HW), "hgrn_norm_g": (1, 8, HEADS, HD), "norm_mix_g": (2, 0, 1, D_MODEL)}
LOSS_ROW = 25
SMALL_ORDER = ("norm_mix_g", "lb_logits", "hgrn_norm_g", "pool_scale", "norm_x_g", "norm_mem_g", "norm_ffn_g",
               "final_norm_g", "w_pool")


def _small_update(srecvs, wprecv, params):
    flat = [t for n in SMALL_ORDER for t in params[n]]
    nb = len(srecvs)
    n_in = nb + 1 + len(flat)

    def body(*refs):
        s_refs, wp_ref = refs[0:nb], refs[nb]
        in_refs = refs[nb + 1:n_in]
        loss_ref = refs[n_in]
        out_refs = refs[n_in + 1:-nb]
        accs = refs[-nb:]
        for s_ref, acc in zip(s_refs, accs):
            total = s_ref[0]
            for d in range(1, N_DEV):
                total = total + s_ref[d]
            acc[...] = total
        loss_ref[...] = accs[0][LOSS_ROW:LOSS_ROW + 1, 0:1]
        for i, name in enumerate(SMALL_ORDER):
            w_ref, m_ref, v_ref = in_refs[3 * i:3 * i + 3]
            g_ref, d_ref, nm_ref, nv_ref = out_refs[4 * i:4 * i + 4]
            if name == "w_pool":
                g = wp_ref[0]
                for d in range(1, N_DEV):
                    g = g + wp_ref[d]
            else:
                buf, r0, nr, nc = SMALL_SLOTS[name]
                g = accs[buf][r0:r0 + nr, 0:nc]
            g_ref[...] = g
            d_ref[...], nm_ref[...], nv_ref[...] = _adam_update(g, w_ref[...], m_ref[...], v_ref[...])

    out_shape = [jax.ShapeDtypeStruct((1, 1), F32)]
    for n in SMALL_ORDER:
        out_shape += [jax.ShapeDtypeStruct(params[n][0].shape, F32)] * 4
    outs = pl.pallas_call(
        body, name="small_update", out_shape=out_shape,
        in_specs=[VMEM_WHOLE] * n_in, out_specs=[VMEM_WHOLE] * len(out_shape),
        scratch_shapes=[pltpu.VMEM(r.shape[1:], F32) for r in srecvs],
        compiler_params=_params(),
    )(*srecvs, wprecv, *flat)
    return outs[0], {n: outs[1 + 4 * i:5 + 4 * i] for i, n in enumerate(SMALL_ORDER)}


def _in_proj(x, g, w_t, deps):
    s = x.shape[0]
    tm = min(ROW_TILE, s)

    def body(x_ref, g_ref, w_ref, z_ref, h_ref):
        xv = x_ref[...]
        h = (xv * _rms(xv) * g_ref[...]).astype(BF16)
        h_ref[...] = h
        z_ref[...] = _mm_nt(h, w_ref[...])

    return _call_behind(
        deps, body, name="in_proj", grid=(s // tm,),
        out_shape=[jax.ShapeDtypeStruct((s, IN_WIDTH), F32), jax.ShapeDtypeStruct((s, D_MODEL), BF16)],
        in_specs=[pl.BlockSpec((tm, D_MODEL), lambda i: (i, 0)), _full((1, D_MODEL)), VMEM_WHOLE],
        out_specs=[pl.BlockSpec((tm, IN_WIDTH), lambda i: (i, 0)), pl.BlockSpec((tm, D_MODEL), lambda i: (i, 0))],
        compiler_params=_params(("parallel",)),
    )(x, g, w_t)


def _chunk_masks():
    row = lax.broadcasted_iota(jnp.int32, (CHUNK, CHUNK), 0)
    col = lax.broadcasted_iota(jnp.int32, (CHUNK, CHUNK), 1)
    return row, col


def _ones_where(mask):
    return jnp.where(mask, 1.0, 0.0).astype(BF16)


def _hgrn_gates(zq, zf, lb):
    sq = _sigmoid(zq)
    sig = _sigmoid(zf)
    f = lb + (1.0 - lb) * sig
    return zq * sq, sq, sig, f


def _sub_chunk_masks(width):
    trow = lax.broadcasted_iota(jnp.int32, (CHUNK, width), 0)
    return [(trow >= SUB * j) & (trow < SUB * (j + 1)) for j in range(N_SUB)]


def _head(a, h):
    return a[:, HD * h:HD * (h + 1)]


def _lanes(parts):
    return jnp.concatenate(parts, axis=1)


def _hgrn_decay_factors(b_scr, r0, b, in_sub):
    bases = [jnp.zeros((1, HW), F32)] + [b_scr[r0 + SUB * j - 1:r0 + SUB * j, :] for j in range(1, N_SUB)]
    own_base = bases[N_SUB - 1]
    for j in range(N_SUB - 2, -1, -1):
        own_base = jnp.where(in_sub[j], bases[j], own_base)
    eq = jnp.exp(b - own_base)
    ek = []
    for j in range(N_SUB):
        upto = SUB * (j + 1)
        e = jnp.exp(jnp.minimum(bases[j] - b[0:upto], EXP_CAP))
        ek.append(e if upto == CHUNK else jnp.concatenate([e, jnp.zeros((CHUNK - upto, HW), F32)], axis=0))
    return eq, ek


def _per_sub_chunk(x, in_sub):
    return _lanes([jnp.where(in_sub[j], x, 0.0) for j in range(N_SUB)])


def _own_lane_block(a, in_sub):
    out = a[:, HD * (N_SUB - 1):HD * N_SUB]
    for j in range(N_SUB - 2, -1, -1):
        out = jnp.where(in_sub[j], a[:, HD * j:HD * (j + 1)], out)
    return out


def _head_rms(o):
    return _lanes([jnp.broadcast_to(_rms(_head(o, h)), (CHUNK, HD)) for h in range(HEADS)])


def _head_mean(a):
    return _lanes([jnp.broadcast_to(jnp.mean(_head(a, h), axis=-1, keepdims=True), (CHUNK, HD)) for h in range(HEADS)])


def _hgrn_fwd(z, lb_logits, gn):
    s = z.shape[0]
    n_chunks = s // CHUNK

    def body(zq_ref, zf_ref, zi_ref, zg_ref, lbl_ref, gn_ref, oa_ref, o_ref, st_ref, state, b_scr):
        @pl.when(pl.program_id(0) == 0)
        def _():
            state[...] = jnp.zeros_like(state)

        lb = _sigmoid(lbl_ref[0:1, :] - lbl_ref[1:2, :])
        row, col = _chunk_masks()
        causal = col <= row
        tri = _ones_where(causal)
        in_sub, in_sub_head = _sub_chunk_masks(HW), _sub_chunk_masks(HD)
        gn_row = _lanes([gn_ref[h:h + 1, :] for h in range(HEADS)])
        def front(c):
            r0 = CHUNK * c
            rs = slice(r0, r0 + CHUNK)
            q, _, _, f = _hgrn_gates(zq_ref[rs, :], zf_ref[rs, :], lb)
            kk = 1.0 - f
            b = _tri_dot(tri, jnp.log(f), 3)
            b_scr[rs, :] = b
            eq, ek = _hgrn_decay_factors(b_scr, r0, b, in_sub)
            b_last = b_scr[r0 + CHUNK - 1:r0 + CHUNK, :]
            qe = q * eq
            return {"rs": rs, "v": zi_ref[rs, :], "qg": q * jnp.exp(b), "kd": kk * jnp.exp(b_last - b),
                    "lam_last": jnp.exp(b_last),
                    "q16": [_per_sub_chunk(_head(qe, h), in_sub_head).astype(BF16) for h in range(HEADS)],
                    "ke16": [_lanes([_head(kk * e, h) for e in ek]).astype(BF16) for h in range(HEADS)]}

        def recurrence(c, p):
            st_ref[c] = state[...]
            a, o_inter = [], []
            for h in range(HEADS):
                vh, st = _head(p["v"], h), state[h]
                a.append(jnp.where(causal, _mm_nt(p["q16"][h], p["ke16"][h]), 0.0))
                o_inter.append(_mm_nt(_head(p["qg"], h), st))
                state[h] = st * _head(p["lam_last"], h) + _mm_tn(vh, _head(p["kd"], h))
            return _lanes([_mm(a[h], _head(p["v"], h)) + o_inter[h] for h in range(HEADS)])

        def back(p, o):
            rs = p["rs"]
            o_ref[rs, :] = o
            zg = zg_ref[rs, :]
            oa_ref[rs, :] = (o * _head_rms(o) * gn_row * zg * _sigmoid(zg)).astype(BF16)

        p = front(0)
        for c in range(CHUNKS_PER_STEP):
            o = recurrence(c, p)
            p_next = front(c + 1) if c + 1 < CHUNKS_PER_STEP else None
            back(p, o)
            p = p_next

    rows = CHUNK * CHUNKS_PER_STEP
    zspec = lambda cb: pl.BlockSpec((rows, HW), lambda i, cb=cb: (i, cb))
    return pl.pallas_call(
        body, name="hgrn_fwd", grid=(s // rows,),
        out_shape=[jax.ShapeDtypeStruct((s, 2 * HW), BF16), jax.ShapeDtypeStruct((s, HW), F32),
                   jax.ShapeDtypeStruct((n_chunks, HEADS, HD, HD), F32)],
        in_specs=[zspec(0), zspec(1), zspec(2), zspec(3), _full((2, HW)), _full((HEADS, HD))],
        out_specs=[pl.BlockSpec((rows, HW), lambda i: (i, 0)), pl.BlockSpec((rows, HW), lambda i: (i, 0)),
                   pl.BlockSpec((CHUNKS_PER_STEP, HEADS, HD, HD), lambda i: (i, 0, 0, 0))],
        scratch_shapes=[pltpu.VMEM((HEADS, HD, HD), F32), pltpu.VMEM((rows, HW), F32)],
        compiler_params=_params(("arbitrary",)),
    )(z, z, z, z, lb_logits, gn)


def _pool_counts(tile_idx, tm):
    t = tile_idx * tm + lax.broadcasted_iota(jnp.int32, (tm, 1), 0)
    return [1.0 / jnp.minimum(t + 1, w).astype(F32) for w in POOL_WINDOWS]


def _pool_fwd(z, w_pool, scale, mixed_in, deps):
    s = z.shape[0]
    tm = min(ROW_TILE, s)

    def body(p_ref, w_ref, sc_ref, mixin_ref, ob_ref, pooled_ref, ext):
        i = pl.program_id(0)

        @pl.when(i == 0)
        def _():
            ext[0:POOL_HALO, :] = jnp.zeros((POOL_HALO, HW), F32)

        @pl.when(i > 0)
        def _():
            ext[0:POOL_HALO, :] = ext[tm:tm + POOL_HALO, :]

        ext[POOL_HALO:POOL_HALO + tm, :] = p_ref[...]
        inv = _pool_counts(i, tm)
        for g, w in enumerate(POOL_WINDOWS):
            sl = slice(HD * g, HD * (g + 1))
            p = ext[POOL_HALO:POOL_HALO + tm, sl]
            win = p
            for d in range(1, w):
                win = win + ext[POOL_HALO - d:POOL_HALO - d + tm, sl]
            pooled = (win * inv[g] - p).astype(BF16)
            pooled_ref[:, sl] = pooled
            ob_ref[:, sl] = (_mm(pooled, w_ref[g]) * sc_ref[:, sl]).astype(BF16)

    return _call_behind(
        deps, body, name="pool_fwd", grid=(s // tm,),
        out_shape=[jax.ShapeDtypeStruct((s, 2 * HW), BF16), jax.ShapeDtypeStruct((s, HW), BF16)],
        in_specs=[pl.BlockSpec((tm, HW), lambda i: (i, 4)), _full((HEADS, HD, HD)), _full((1, HW)), ANY_SPACE],
        out_specs=[pl.BlockSpec((tm, HW), lambda i: (i, 1)), pl.BlockSpec((tm, HW), lambda i: (i, 0))],
        scratch_shapes=[pltpu.VMEM((tm + POOL_HALO, HW), F32)],
        input_output_aliases={3: 0},
        compiler_params=_params(("arbitrary",)),
    )(z, w_pool, scale, mixed_in)


def _mem_kv(mem, g, wk, wv, deps):
    def body(m_ref, g_ref, wk_ref, wv_ref, hm_ref, k_ref, v_ref):
        m = m_ref[...]
        hm = (m * _rms(m) * g_ref[...]).astype(BF16)
        hm_ref[...] = hm
        k_ref[...] = _mm(hm, wk_ref[...]).astype(BF16)
        v_ref[...] = _mm(hm, wv_ref[...]).astype(BF16)

    shp = jax.ShapeDtypeStruct((MEM_LEN, D_MODEL), BF16)
    return _call_behind(
        deps, body, name="mem_kv", out_shape=[shp, shp, shp],
        in_specs=[VMEM_WHOLE] * 4, out_specs=[VMEM_WHOLE] * 3,
        compiler_params=_params(),
    )(mem, g, wk, wv)


def _softmax_rows(sc):
    e = jnp.exp(sc - jnp.max(sc, axis=-1, keepdims=True))
    return e / jnp.sum(e, axis=-1, keepdims=True)


def _mix_xattn_fwd(x0, mixed, w_out, g, wq, xk, xv, wo_t, deps):
    s = x0.shape[0]
    tm = min(ROW_TILE, s)
    scale = XHD ** -0.5

    def body(x_ref, mix_ref, wout_ref, g_ref, wq_ref, k_ref, v_ref, wo_ref, x1_ref, o_ref, hq_ref, q_ref, att_ref):
        xv_ = x_ref[...] + _mm(mix_ref[...], wout_ref[...])
        x1_ref[...] = xv_
        hq = (xv_ * _rms(xv_) * g_ref[...]).astype(BF16)
        hq_ref[...] = hq
        q_ref[...] = (_mm(hq, wq_ref[...]) * scale).astype(BF16)
        heads = [slice(XHD * h, XHD * (h + 1)) for h in range(HEADS)]
        scores = [_mm_nt(q_ref[:, sl], k_ref[:, sl]) for sl in heads]
        probs = [_softmax_rows(sc) for sc in scores]
        for sl, p in zip(heads, probs):
            att_ref[:, sl] = _mm(p, v_ref[:, sl]).astype(BF16)
        o_ref[...] = xv_ + _mm_nt(att_ref[...], wo_ref[...])

    row_f32 = pl.BlockSpec((tm, D_MODEL), lambda i: (i, 0))
    bshape = jax.ShapeDtypeStruct((s, D_MODEL), BF16)
    fshape = jax.ShapeDtypeStruct((s, D_MODEL), F32)
    return _call_behind(
        deps, body, name="mix_xattn_fwd", grid=(s // tm,),
        out_shape=[fshape, fshape, bshape, bshape, bshape],
        in_specs=[row_f32, row_f32, VMEM_WHOLE, _full((1, D_MODEL)), VMEM_WHOLE, VMEM_WHOLE, VMEM_WHOLE, VMEM_WHOLE],
        out_specs=[row_f32] * 5,
        compiler_params=_params(("parallel",)),
    )(x0, mixed, w_out, g, wq, xk, xv, wo_t)


def _mlp_fwd_loss(x, g, w1, w2, gf, target):
    s = x.shape[0]
    tm = min(ROW_TILE, s)

    def body(x_ref, g_ref, w1_ref, w2_ref, gf_ref, t_ref, dx_ref, dx16_ref, u_ref, hf_ref, slot_ref):
        @pl.when(pl.program_id(0) == 0)
        def _():
            slot_ref[...] = jnp.zeros_like(slot_ref)

        xv = x_ref[...]
        hf = (xv * _rms(xv) * g_ref[...]).astype(BF16)
        hf_ref[...] = hf
        a_next = _mm(hf, w1_ref[0])
        for j in range(N_DEV):
            a = jnp.maximum(a_next, 0.0)
            if j + 1 < N_DEV:
                a_next = _mm(hf, w1_ref[j + 1])
            u_ref[:, FF_BLK * j:FF_BLK * (j + 1)] = (a * a).astype(BF16)
        acc = xv + _mm(u_ref[...], w2_ref[...])
        gfv = gf_ref[...]
        r = _rms(acc)
        n = acc * r
        err = n * gfv - t_ref[...]
        slot_ref[1:2, :] += jnp.sum(jnp.mean(err * err, axis=-1, keepdims=True), axis=0, keepdims=True) * 0.5
        dy = err * (1.0 / D_MODEL)
        slot_ref[0:1, :] += jnp.sum(dy * n, axis=0, keepdims=True)
        dn = dy * gfv
        dx = r * (dn - n * jnp.mean(dn * n, axis=-1, keepdims=True))
        dx_ref[...] = dx
        dx16_ref[...] = dx.astype(BF16)

    row_f32 = pl.BlockSpec((tm, D_MODEL), lambda i: (i, 0))
    return pl.pallas_call(
        body, name="mlp_fwd_loss", grid=(s // tm,),
        out_shape=[jax.ShapeDtypeStruct((s, D_MODEL), F32), jax.ShapeDtypeStruct((s, D_MODEL), BF16),
                   jax.ShapeDtypeStruct((s, D_FF), BF16), jax.ShapeDtypeStruct((s, D_MODEL), BF16),
                   jax.ShapeDtypeStruct((SLOT, D_MODEL), F32)],
        in_specs=[row_f32, _full((1, D_MODEL)), VMEM_WHOLE, VMEM_WHOLE, _full((1, D_MODEL)), row_f32],
        out_specs=[row_f32, row_f32, pl.BlockSpec((tm, D_FF), lambda i: (i, 0)), row_f32, _full((SLOT, D_MODEL))],
        compiler_params=_params(("arbitrary",)),
    )(x, g, w1, w2, gf, target)


def _zero_slot(slot_ref):
    @pl.when(pl.program_id(0) == 0)
    def _():
        slot_ref[...] = jnp.zeros_like(slot_ref)


def _mlp_bwd(dx3, u, x2, g, w1, w2, deps):
    s = x2.shape[0]
    tm = min(ROW_TILE // 2, s)

    def body(d_ref, u_ref, x_ref, g_ref, w1_ref, w2_ref, da_ref, dx_ref, slot_ref):
        _zero_slot(slot_ref)
        d = d_ref[...]
        d16 = d.astype(BF16)
        du_next = _mm_nt(d16, w2_ref[0])
        dhf = jnp.zeros((tm, D_MODEL), F32)
        for j in range(N_DEV):
            sl = slice(FF_BLK * j, FF_BLK * (j + 1))
            du = du_next
            if j + 1 < N_DEV:
                du_next = _mm_nt(d16, w2_ref[j + 1])
            u = u_ref[:, sl].astype(F32)
            da = (du * (2.0 * u * lax.rsqrt(jnp.maximum(u, TINY)))).astype(BF16)
            da_ref[:, sl] = da
            dhf = dhf + _mm_nt(da, w1_ref[j])
        dx, dg = _rms_bwd(x_ref[...], g_ref[...], dhf)
        dx_ref[...] = d + dx
        slot_ref[0:1, :] += dg

    row_f32 = pl.BlockSpec((tm, D_MODEL), lambda i: (i, 0))
    return _call_behind(
        deps, body, name="mlp_bwd", grid=(s // tm,),
        out_shape=[jax.ShapeDtypeStruct((s, D_FF), BF16), jax.ShapeDtypeStruct((s, D_MODEL), F32),
                   jax.ShapeDtypeStruct((SLOT, D_MODEL), F32)],
        in_specs=[row_f32, pl.BlockSpec((tm, D_FF), lambda i: (i, 0)), row_f32, _full((1, D_MODEL)),
                  VMEM_WHOLE, VMEM_WHOLE],
        out_specs=[pl.BlockSpec((tm, D_FF), lambda i: (i, 0)), row_f32, _full((SLOT, D_MODEL))],
        compiler_params=_params(("arbitrary",)),
    )(dx3, u, x2, g, w1, w2)


def _wgrad(a, b, name, col_blocks=False, update=None):
    s, m = a.shape
    n = b.shape[1]
    tm = 1280 if m % 1280 == 0 else min(1024, m)
    tn = min(1024, n)
    blk = n // N_DEV
    per_step = tn // blk if col_blocks else 1
    ts = min((4 if m * n >= D_MODEL * D_FF else 2) * ROW_TILE, s)
    n_s = s // ts
    grid = (m // tm, n // tn, n_s)

    def body(a_ref, b_ref, *rest):
        o_ref, acc = rest[-2], rest[-1]
        k = pl.program_id(2)

        @pl.when(k == 0)
        def _():
            acc[...] = jnp.zeros_like(acc)

        acc[...] += _mm_tn(a_ref[...], b_ref[...])
        if update is not None:
            r_ref, w_ref, m_ref, v_ref, g_ref, d_ref, nm_ref, nv_ref = rest[:8]
            g = r_ref[0].astype(F32)
            for d in range(1, N_DEV):
                g = g + r_ref[d].astype(F32)
            g_ref[...] = g
            d_ref[...], nm_ref[...], nv_ref[...] = _adam_update(g, w_ref[...], m_ref[...], v_ref[...])

        @pl.when(k == n_s - 1)
        def _():
            if col_blocks:
                for p in range(per_step):
                    o_ref[p] = acc[:, blk * p:blk * (p + 1)].astype(BF16)
            else:
                o_ref[...] = acc[...].astype(BF16)

    if col_blocks:
        out_shape = jax.ShapeDtypeStruct((N_DEV, m, blk), BF16)
        out_spec = pl.BlockSpec((per_step, tm, blk), lambda i, j, k: (j, i, 0))
    else:
        out_shape = jax.ShapeDtypeStruct((m, n), BF16)
        out_spec = pl.BlockSpec((tm, tn), lambda i, j, k: (i, j))
    in_specs = [pl.BlockSpec((ts, tm), lambda i, j, k: (k, i)), pl.BlockSpec((ts, tn), lambda i, j, k: (k, j))]
    out_shapes, out_specs, operands = [out_shape], [out_spec], [a, b]
    if update is not None:
        rows, cols = update[1].shape
        steps = grid[0] * grid[1] * grid[2]
        tr = rows // steps
        step = lambda i, j, k: (i * grid[1] + j) * grid[2] + k
        piece = pl.BlockSpec((tr, cols), lambda i, j, k: (step(i, j, k), 0))
        in_specs += [pl.BlockSpec((N_DEV, tr, cols), lambda i, j, k: (0, step(i, j, k), 0)), piece, piece, piece]
        out_shapes = [jax.ShapeDtypeStruct((rows, cols), F32)] * 4 + out_shapes
        out_specs = [piece] * 4 + out_specs
        operands += list(update)
    outs = pl.pallas_call(
        body, name=name, grid=grid, out_shape=out_shapes, in_specs=in_specs, out_specs=out_specs,
        scratch_shapes=[pltpu.VMEM((tm, tn), F32)],
        compiler_params=_params(("parallel", "parallel", "arbitrary")),
    )(*operands)
    return outs[0] if update is None else (outs[4], tuple(outs[:4]))


def _xattn_bwd(dx2, x1, g, q, xk, xv, wq, wo_t, deps):
    s = x1.shape[0]
    tm = min(ROW_TILE, s)
    scale = XHD ** -0.5

    def body(d_ref, x_ref, g_ref, q_ref, k_ref, v_ref, wq_ref, wo_ref, dx_ref, dx16_ref, dq_ref, dk_ref, dv_ref, slot_ref,
             datt):
        _zero_slot(slot_ref)

        @pl.when(pl.program_id(0) == 0)
        def _():
            dk_ref[...] = jnp.zeros_like(dk_ref)
            dv_ref[...] = jnp.zeros_like(dv_ref)

        d = d_ref[...]
        datt[...] = _mm(d, wo_ref[...]).astype(BF16)
        heads = [slice(XHD * h, XHD * (h + 1)) for h in range(HEADS)]
        scores = [_mm_nt(q_ref[:, sl], k_ref[:, sl]) for sl in heads]
        dps = [_mm_nt(datt[:, sl], v_ref[:, sl]) for sl in heads]
        probs = [_softmax_rows(sc) for sc in scores]
        dss = [(p * (dp - jnp.sum(dp * p, axis=-1, keepdims=True))).astype(BF16) for p, dp in zip(probs, dps)]
        for sl, p, ds in zip(heads, probs, dss):
            dq_ref[:, sl] = (_mm(ds, k_ref[:, sl]) * scale).astype(BF16)
            dk_ref[:, sl] += _mm_tn(ds, q_ref[:, sl])
            dv_ref[:, sl] += _mm_tn(p, datt[:, sl])
        dx, dg = _rms_bwd(x_ref[...], g_ref[...], _mm_nt(dq_ref[...], wq_ref[...]))
        dx_ref[...] = d + dx
        dx16_ref[...] = (d + dx).astype(BF16)
        slot_ref[0:1, :] += dg

    row_f32 = pl.BlockSpec((tm, D_MODEL), lambda i: (i, 0))
    kv = jax.ShapeDtypeStruct((MEM_LEN, D_MODEL), F32)
    tokens16 = jax.ShapeDtypeStruct((s, D_MODEL), BF16)
    return _call_behind(
        deps, body, name="xattn_bwd", grid=(s // tm,),
        out_shape=[jax.ShapeDtypeStruct((s, D_MODEL), F32), tokens16, tokens16, kv, kv,
                   jax.ShapeDtypeStruct((SLOT, D_MODEL), F32)],
        in_specs=[row_f32, row_f32, _full((1, D_MODEL)), row_f32, VMEM_WHOLE, VMEM_WHOLE, VMEM_WHOLE, VMEM_WHOLE],
        out_specs=[row_f32, row_f32, row_f32, _full((MEM_LEN, D_MODEL)), _full((MEM_LEN, D_MODEL)),
                   _full((SLOT, D_MODEL))],
        scratch_shapes=[pltpu.VMEM((tm, D_MODEL), BF16)],
        compiler_params=_params(("arbitrary",)),
    )(dx2, x1, g, q, xk, xv, wq, wo_t)


def _mem_bwd(mem, g, hm, dxk, dxv, wk, wv):
    def body(m_ref, g_ref, hm_ref, dk_ref, dv_ref, wk_ref, wv_ref, dwk_ref, dwv_ref, slot_ref):
        dk, dv = dk_ref[...], dv_ref[...]
        hm_ = hm_ref[...]
        dwk_ref[...] = _mm_tn(hm_, dk).astype(BF16)
        dwv_ref[...] = _mm_tn(hm_, dv).astype(BF16)
        _, dg = _rms_bwd(m_ref[...], g_ref[...], _mm_nt(dk, wk_ref[...]) + _mm_nt(dv, wv_ref[...]))
        slot_ref[...] = jnp.zeros_like(slot_ref)
        slot_ref[0:1, :] = dg

    wshape = jax.ShapeDtypeStruct((D_MODEL, D_MODEL), BF16)
    return pl.pallas_call(
        body, name="mem_bwd", out_shape=[wshape, wshape, jax.ShapeDtypeStruct((SLOT, D_MODEL), F32)],
        in_specs=[VMEM_WHOLE] * 7, out_specs=[VMEM_WHOLE] * 3,
        compiler_params=_params(),
    )(mem, g, hm, dxk, dxv, wk, wv)


def _pool_bwd(dx1, w_out, pooled, w_pool, scale, deps):
    s = dx1.shape[0]
    tm = min(ROW_TILE, s)
    n_t = s // tm

    def body(dx_ref, wo_ref, pl_ref, w_ref, sc_ref, dz_ref, dw_ref, slot_ref, ext, do_ref):
        i = pl.program_id(0)
        tile = n_t - 1 - i
        _zero_slot(slot_ref)
        do_ref[...] = _mm_nt(dx_ref[...], wo_ref[HW:2 * HW, :])

        @pl.when(i == 0)
        def _():
            dw_ref[...] = jnp.zeros_like(dw_ref)
            ext[tm:tm + POOL_HALO, :] = jnp.zeros((POOL_HALO, HW), F32)

        @pl.when(i > 0)
        def _():
            ext[tm:tm + POOL_HALO, :] = ext[0:POOL_HALO, :]

        inv = _pool_counts(tile, tm)
        dpooled = []
        for g in range(HEADS):
            sl = slice(HD * g, HD * (g + 1))
            pooled_g = pl_ref[:, sl]
            do = do_ref[:, sl]
            slot_ref[0:1, sl] += jnp.sum(_mm(pooled_g, w_ref[g]) * do, axis=0, keepdims=True)
            dy = (do * sc_ref[:, sl]).astype(BF16)
            dw_ref[g] += _mm_tn(pooled_g, dy)
            dpo = _mm_nt(dy, w_ref[g])
            dpooled.append(dpo)
            ext[0:tm, sl] = dpo * inv[g]
        for g, w in enumerate(POOL_WINDOWS):
            sl = slice(HD * g, HD * (g + 1))
            win = ext[0:tm, sl]
            for d in range(1, w):
                win = win + ext[d:d + tm, sl]
            dz_ref[:, sl] = (win - dpooled[g]).astype(BF16)

    return _call_behind(
        deps, body, name="pool_bwd", grid=(n_t,),
        out_shape=[jax.ShapeDtypeStruct((s, IN_WIDTH), BF16), jax.ShapeDtypeStruct((HEADS, HD, HD), F32),
                   jax.ShapeDtypeStruct((SLOT, D_MODEL), F32)],
        in_specs=[pl.BlockSpec((tm, D_MODEL), lambda i: (n_t - 1 - i, 0)), VMEM_WHOLE,
                  pl.BlockSpec((tm, HW), lambda i: (n_t - 1 - i, 0)), _full((HEADS, HD, HD)), _full((1, HW))],
        out_specs=[pl.BlockSpec((tm, HW), lambda i: (n_t - 1 - i, 4)), _full((HEADS, HD, HD)), _full((SLOT, D_MODEL))],
        scratch_shapes=[pltpu.VMEM((tm + POOL_HALO, HW), F32), pltpu.VMEM((tm, HW), F32)],
        compiler_params=_params(("arbitrary",)),
    )(dx1, w_out, pooled, w_pool, scale)


def _hgrn_bwd(z, o, dx1, w_out, states, lb_logits, gn, dz_in, deps):
    s = z.shape[0]
    n_chunks = s // CHUNK

    def body(zq_ref, zf_ref, zi_ref, zg_ref, o_ref, dx_ref, wo_ref, st_ref, lbl_ref, gn_ref, dzin_ref,
             dz_ref, dlb_ref, dgn_ref, dstate, b_scr, dlb_acc, do_ref):
        i = pl.program_id(0)

        @pl.when(i == 0)
        def _():
            dstate[...] = jnp.zeros_like(dstate)
            dlb_acc[...] = jnp.zeros_like(dlb_acc)
            dgn_ref[...] = jnp.zeros_like(dgn_ref)
            dlb_ref[...] = jnp.zeros_like(dlb_ref)

        do_ref[...] = _mm_nt(dx_ref[...], wo_ref[0:HW, :])
        lb = _sigmoid(lbl_ref[0:1, :] - lbl_ref[1:2, :])
        row, col = _chunk_masks()
        causal = col <= row
        tri = _ones_where(causal)
        upper = _ones_where(col >= row)
        strict_lower = _ones_where(col < row)
        in_sub, in_sub_head = _sub_chunk_masks(HW), _sub_chunk_masks(HD)
        gn_row = _lanes([gn_ref[h:h + 1, :] for h in range(HEADS)])
        sums = {"dlb": 0.0, "dgn": 0.0}

        def front(c):
            r0 = CHUNK * c
            rs = slice(r0, r0 + CHUNK)
            p = {"rs": rs}
            p["zq"] = zq_ref[rs, :]
            p["q"], p["sq"], p["sig"], p["f"] = _hgrn_gates(p["zq"], zf_ref[rs, :], lb)
            p["kk"] = 1.0 - p["f"]
            b = _tri_dot(tri, jnp.log(p["f"]), 3)
            b_scr[rs, :] = b
            p["v"] = zi_ref[rs, :]
            o, zg, doa = o_ref[rs, :], zg_ref[rs, :], do_ref[rs, :]
            sg = _sigmoid(zg)
            rms = _head_rms(o)
            n = o * rms
            don = doa * (zg * sg)
            sums["dgn"] = sums["dgn"] + jnp.sum(don * n, axis=0, keepdims=True)
            dn = don * gn_row
            p["d_o"] = rms * (dn - n * _head_mean(dn * n))
            dz_ref[rs, 3 * HW:4 * HW] = (doa * (n * gn_row) * (sg * (1.0 + zg * (1.0 - sg)))).astype(BF16)
            p["eq"], p["ek"] = _hgrn_decay_factors(b_scr, r0, b, in_sub)
            b_last = b_scr[r0 + CHUNK - 1:r0 + CHUNK, :]
            p["lam"], p["e_last"], p["lam_last"] = jnp.exp(b), jnp.exp(b_last - b), jnp.exp(b_last)
            p["qe"], p["qg"], p["kd"] = p["q"] * p["eq"], p["q"] * p["lam"], p["kk"] * p["e_last"]
            p["ke"] = [p["kk"] * e for e in p["ek"]]
            p["q16"] = [_per_sub_chunk(_head(p["qe"], h), in_sub_head).astype(BF16) for h in range(HEADS)]
            p["ke16"] = [_lanes([_head(p["ke"][j], h) for j in range(N_SUB)]).astype(BF16) for h in range(HEADS)]
            return p

        def recurrence(c, p):
            m = {k: [] for k in ("dv", "gq", "gk", "dqi", "dkd", "st")}
            a, da, dv_state = [], [], []
            for h in range(HEADS):
                vh, doh = _head(p["v"], h), _head(p["d_o"], h)
                st0, ds1 = st_ref[c, h], dstate[h]
                a.append(jnp.where(causal, _mm_nt(p["q16"][h], p["ke16"][h]), 0.0))
                da.append(jnp.where(causal, _mm_nt(doh, vh), 0.0))
                dv_state.append(_mm_nt(_head(p["kd"], h), ds1))
                m["dqi"].append(_mm(doh, st0))
                m["dkd"].append(_mm(vh, ds1))
                m["st"].append(jnp.sum(st0 * ds1, axis=0, keepdims=True))
                dstate[h] = ds1 * _head(p["lam_last"], h) + _mm_tn(doh, _head(p["qg"], h))
            for h in range(HEADS):
                m["dv"].append(_mm_tn(a[h], _head(p["d_o"], h)) + dv_state[h])
                m["gq"].append(_own_lane_block(_mm(da[h], p["ke16"][h]), in_sub_head))
                m["gk"].append(_mm_tn(da[h], p["q16"][h]))
            return m

        def back(p, m):
            rs = p["rs"]
            dz_ref[rs, 2 * HW:3 * HW] = _lanes(m["dv"]).astype(BF16)
            gq = _lanes(m["gq"])
            gk = [_lanes([m["gk"][h][:, HD * j:HD * (j + 1)] for h in range(HEADS)]) for j in range(N_SUB)]
            dq_inter = p["lam"] * _lanes(m["dqi"])
            dq = p["eq"] * gq + dq_inter
            dk_intra = sum(p["ek"][j] * gk[j] for j in range(N_SUB))
            dk_state = _lanes(m["dkd"]) * p["e_last"]
            db_intra = (p["qe"].astype(BF16).astype(F32) * gq
                        - sum(p["ke"][j].astype(BF16).astype(F32) * gk[j] for j in range(N_SUB)))
            dlf = (_tri_dot(upper, db_intra + p["q"] * dq_inter, 2) + _tri_dot(strict_lower, p["kk"] * dk_state, 2)
                   + p["lam_last"] * _lanes(m["st"]))
            sig, sq, zq = p["sig"], p["sq"], p["zq"]
            df = dlf / p["f"] - (dk_intra + dk_state)
            sums["dlb"] = sums["dlb"] + jnp.sum(df * (1.0 - sig), axis=0, keepdims=True)
            dz_ref[rs, HW:2 * HW] = (df * (1.0 - lb) * sig * (1.0 - sig)).astype(BF16)
            dz_ref[rs, 0:HW] = (dq * (sq * (1.0 + zq * (1.0 - sq)))).astype(BF16)

        p = front(CHUNKS_PER_STEP - 1)
        for c in reversed(range(CHUNKS_PER_STEP)):
            m = recurrence(c, p)
            p_next = front(c - 1) if c > 0 else None
            back(p, m)
            p = p_next
        dlb_acc[...] += sums["dlb"]
        for h in range(HEADS):
            dgn_ref[h:h + 1, 0:HD] += _head(sums["dgn"], h)

        @pl.when(i == n_steps - 1)
        def _():
            dl0 = dlb_acc[...] * lb * (1.0 - lb)
            dlb_ref[0:1, 0:HW] = dl0
            dlb_ref[1:2, 0:HW] = -dl0

    rows = CHUNK * CHUNKS_PER_STEP
    n_steps = s // rows
    rev = lambda i: n_steps - 1 - i
    zspec = lambda cb: pl.BlockSpec((rows, HW), lambda i, cb=cb: (rev(i), cb))
    slot = jax.ShapeDtypeStruct((SLOT, D_MODEL), F32)
    return _call_behind(
        deps, body, name="hgrn_bwd", grid=(n_steps,),
        out_shape=[jax.ShapeDtypeStruct((s, IN_WIDTH), BF16), slot, slot],
        in_specs=[zspec(0), zspec(1), zspec(2), zspec(3), pl.BlockSpec((rows, HW), lambda i: (rev(i), 0)),
                  pl.BlockSpec((rows, D_MODEL), lambda i: (rev(i), 0)), VMEM_WHOLE,
                  pl.BlockSpec((CHUNKS_PER_STEP, HEADS, HD, HD), lambda i: (rev(i), 0, 0, 0)), _full((2, HW)),
                  _full((HEADS, HD)), ANY_SPACE],
        out_specs=[pl.BlockSpec((rows, 4 * HW), lambda i: (rev(i), 0)), _full((SLOT, D_MODEL)), _full((SLOT, D_MODEL))],
        scratch_shapes=[pltpu.VMEM((HEADS, HD, HD), F32), pltpu.VMEM((rows, HW), F32), pltpu.VMEM((1, HW), F32),
                        pltpu.VMEM((rows, HW), F32)],
        input_output_aliases={10: 0},
        compiler_params=_params(("arbitrary",)),
    )(z, z, z, z, o, dx1, w_out, states, lb_logits, gn, dz_in)


def _in_bwd(dz, w_t, x0, g, dx1, deps):
    s = x0.shape[0]
    tm = min(WIDE_ROW_TILE, s)

    def body(dz_ref, w_ref, x_ref, g_ref, d_ref, dx_ref, slot_ref):
        _zero_slot(slot_ref)
        dx, dg = _rms_bwd(x_ref[...], g_ref[...], _mm(dz_ref[...], w_ref[...]))
        dx_ref[...] = d_ref[...] + dx
        slot_ref[0:1, :] += dg

    row_f32 = pl.BlockSpec((tm, D_MODEL), lambda i: (i, 0))
    return _call_behind(
        deps, body, name="in_bwd", grid=(s // tm,),
        out_shape=[jax.ShapeDtypeStruct((s, D_MODEL), F32), jax.ShapeDtypeStruct((SLOT, D_MODEL), F32)],
        in_specs=[pl.BlockSpec((tm, IN_WIDTH), lambda i: (i, 0)), VMEM_WHOLE, row_f32, _full((1, D_MODEL)), row_f32],
        out_specs=[row_f32, _full((SLOT, D_MODEL))],
        compiler_params=_params(("arbitrary",)),
    )(dz, w_t, x0, g, dx1)


def kernel(x, mem, norm_mix_g, w_in, lb_logits, hgrn_norm_g, w_pool, pool_scale, w_out, norm_x_g, norm_mem_g, w_xq, w_xk, w_xv, w_xo, norm_ffn_g, w_ff1, w_ff2, final_norm_g, loss_target, m_norm_mix_g, m_w_in, m_lb_logits, m_hgrn_norm_g, m_w_pool, m_pool_scale, m_w_out, m_norm_x_g, m_norm_mem_g, m_w_xq, m_w_xk, m_w_xv, m_w_xo, m_norm_ffn_g, m_w_ff1, m_w_ff2, m_final_norm_g, v_norm_mix_g, v_w_in, v_lb_logits, v_hgrn_norm_g, v_w_pool, v_pool_scale, v_w_out, v_norm_x_g, v_norm_mem_g, v_w_xq, v_w_xk, v_w_xv, v_w_xo, v_norm_ffn_g, v_w_ff1, v_w_ff2, v_final_norm_g):
    x0 = x[0]
    mem0 = mem[0]
    tgt = loss_target[0]
    gn = hgrn_norm_g[0]
    gfin = final_norm_g.reshape(1, D_MODEL)
    wp = w_pool[0]
    heads_2d = lambda w: w.reshape(D_MODEL // N_DEV, D_MODEL)
    xo_2d = lambda w: w.reshape(D_MODEL, D_MODEL // N_DEV)

    first = _all_gather_weights([w_in[0].T], [w_out[0], heads_2d(w_xq), heads_2d(w_xk), heads_2d(w_xv), xo_2d(w_xo).T,
                                              w_ff1[0], w_ff2[0]])
    win_t = first[0].reshape(IN_WIDTH, D_MODEL)
    ga_attn, ga_mlp = _gather_first_start([first[1:6], first[6:8]], "gather_first_start")

    z, h = _in_proj(x0, norm_mix_g, win_t, deps=[ga_attn[3]])
    mixed_a, o_pre, states = _hgrn_fwd(z, lb_logits, gn)
    lands = _split_wait(_gather_first_copies, ga_attn, o_pre, "gather_attn_first_wait")
    gb_attn = _gather_forward_start(lands, "gather_attn_forward_start")
    mixed, pooled = _pool_fwd(z, wp, pool_scale, mixed_a, deps=[gb_attn[3]])
    lands = _split_wait(_gather_forward_copies, gb_attn, pooled, "gather_attn_forward_wait")
    wout_f, wq_f, wk_f, wv_f, wo_t = (t.reshape(D_MODEL, D_MODEL) for t in lands)
    hm, xk, xv = _mem_kv(mem0, norm_mem_g, wk_f, wv_f, deps=[])
    x1, x2, hq, xq, att = _mix_xattn_fwd(x0, mixed, wout_f, norm_x_g, wq_f, xk, xv, wo_t, deps=[])
    lands = _split_wait(_gather_first_copies, ga_mlp, x2, "gather_mlp_first_wait")
    gb_mlp = _gather_forward_start(lands, "gather_mlp_forward_start")
    w1_b, w2_b = _split_wait(_gather_forward_copies, gb_mlp, gb_mlp[3], "gather_mlp_forward_wait")
    dx3, dx3_16, u, hf, slot_fin = _mlp_fwd_loss(x2, norm_ffn_g, w1_b, w2_b.reshape(D_FF, D_MODEL), gfin, tgt)

    rows = lambda t, r: t.reshape(N_DEV, r, D_MODEL)
    dw2 = _wgrad(u, dx3_16, "wgrad_ff2")
    ex_ff2 = _all_to_all_start([rows(dw2, FF_BLK)], [], "exchange_ff2_start")
    da, dx2, slot_ffn = _mlp_bwd(dx3, u, x2, norm_ffn_g, w1_b, w2_b, deps=[ex_ff2[3]])
    dw1 = _wgrad(hf, da, "wgrad_ff1", col_blocks=True)
    ex_ff1 = _all_to_all_start([dw1], [], "exchange_ff1_start")
    dx1, dx1_16, dxq, dxk, dxv, slot_x = _xattn_bwd(dx2, x1, norm_x_g, xq, xk, xv, wq_f, wo_t, deps=[ex_ff1[3]])
    dwo_t = _wgrad(dx2, att, "wgrad_xo")
    dwq = _wgrad(hq, dxq, "wgrad_xq")
    dwk, dwv, slot_mem = _mem_bwd(mem0, norm_mem_g, hm, dxk, dxv, wk_f, wv_f)
    ex_attn = _all_to_all_start([rows(dwq, 128), rows(dwk, 128), rows(dwv, 128), rows(dwo_t, 128)], [],
                                "exchange_attn_start")
    dwout = _wgrad(mixed, dx1_16, "wgrad_out")
    dz_pool, d_wpool, slot_ps = _pool_bwd(dx1_16, wout_f, pooled, wp, pool_scale, deps=[ex_attn[3]])
    small0 = jnp.concatenate([slot_x, slot_mem, slot_ffn, slot_fin, slot_ps], axis=0)
    ex_out = _all_to_all_start([rows(dwout, 128)], [small0, d_wpool], "exchange_out_start")
    dz, slot_lb, slot_gn = _hgrn_bwd(z, o_pre, dx1_16, wout_f, states, lb_logits, gn, dz_pool, deps=[ex_out[3]])
    (r_2,) = _split_wait(_all_to_all_copies(1), ex_ff2, dz, "exchange_ff2_wait")
    dwin_t, ff2_update = _wgrad(dz, h, "wgrad_in", update=(r_2, w_ff2[0], m_w_ff2[0], v_w_ff2[0]))
    small1 = jnp.concatenate([slot_lb, slot_gn], axis=0)
    ex_in = _all_to_all_start([rows(dwin_t, 320)], [small1], "exchange_in_start")
    grad_x, slot_mix = _in_bwd(dz, win_t, x0, norm_mix_g, dx1, deps=[ex_in[3]])
    ex_mix = _all_to_all_start([], [slot_mix], "exchange_mix_start")

    out = {}
    out["w_ff2"] = ff2_update
    (r_1,) = _split_wait(_all_to_all_copies(1), ex_ff1, ex_mix[3], "exchange_ff1_wait")
    out["w_ff1"] = _sum_adamw(r_1, w_ff1[0], m_w_ff1[0], v_w_ff1[0], "adamw_ff1")
    r_q, r_k, r_v, r_o = _split_wait(_all_to_all_copies(4), ex_attn, out["w_ff1"][1], "exchange_attn_wait")
    sums = _sum_sources_whole([r_q, r_k, r_v, r_o], "sum_grad_attn")
    g_attn = [g.reshape(w_xq.shape) for g in sums[:3]] + [sums[3].T]
    attn = _adamw_whole([(g_attn[0], w_xq, m_w_xq, v_w_xq), (g_attn[1], w_xk, m_w_xk, v_w_xk),
                         (g_attn[2], w_xv, m_w_xv, v_w_xv),
                         (g_attn[3], xo_2d(w_xo), xo_2d(m_w_xo), xo_2d(v_w_xo))], "adamw_attn")
    for n, g, res in zip(("w_xq", "w_xk", "w_xv", "w_xo"), g_attn, attn):
        out[n] = (g, *res)
    r_out, r_small0, r_wpool = _split_wait(_all_to_all_copies(1), ex_out, attn[3][0], "exchange_out_wait")
    out["w_out"] = _sum_adamw(r_out, w_out[0], m_w_out[0], v_w_out[0], "adamw_out")
    r_in, r_small1 = _split_wait(_all_to_all_copies(1), ex_in, out["w_out"][1], "exchange_in_wait")
    in_t = _sum_adamw(r_in, w_in[0].T, m_w_in[0].T, v_w_in[0].T, "adamw_in")
    out["w_in"] = tuple(t.T for t in in_t)
    (r_small2,) = _split_wait(_all_to_all_copies(0), ex_mix, in_t[1], "exchange_mix_wait")
    row = lambda t: t.reshape(1, -1)
    small_params = {
        "norm_mix_g": (norm_mix_g, m_norm_mix_g, v_norm_mix_g),
        "lb_logits": (lb_logits, m_lb_logits, v_lb_logits),
        "hgrn_norm_g": (hgrn_norm_g[0], m_hgrn_norm_g[0], v_hgrn_norm_g[0]),
        "pool_scale": (pool_scale, m_pool_scale, v_pool_scale),
        "norm_x_g": (norm_x_g, m_norm_x_g, v_norm_x_g),
        "norm_mem_g": (norm_mem_g, m_norm_mem_g, v_norm_mem_g),
        "norm_ffn_g": (norm_ffn_g, m_norm_ffn_g, v_norm_ffn_g),
        "final_norm_g": (row(final_norm_g), row(m_final_norm_g), row(v_final_norm_g)),
        "w_pool": (wp, m_w_pool[0], v_w_pool[0]),
    }
    loss, small_out = _small_update([r_small0, r_small1, r_small2], r_wpool, small_params)
    out.update(small_out)

    shapes = dict(norm_mix_g=norm_mix_g, w_in=w_in, lb_logits=lb_logits, hgrn_norm_g=hgrn_norm_g, w_pool=w_pool,
                  pool_scale=pool_scale, w_out=w_out, norm_x_g=norm_x_g, norm_mem_g=norm_mem_g, w_xq=w_xq, w_xk=w_xk,
                  w_xv=w_xv, w_xo=w_xo, norm_ffn_g=norm_ffn_g, w_ff1=w_ff1, w_ff2=w_ff2, final_norm_g=final_norm_g)
    order = list(shapes)
    group = lambda k: [out[n][k].reshape(shapes[n].shape) for n in order]
    return (loss.reshape(()), grad_x.reshape(x.shape), *group(0), *group(1), *group(2), *group(3))
```

```python
import jax
import jax.numpy as jnp
from jax import lax
from jax.experimental import pallas as pl
from jax.experimental.pallas import tpu as pltpu

F32 = jnp.float32
BF16 = jnp.bfloat16

D_MODEL = 1024
N_DEV = 8
HEADS = 4
HD = 128
HW = HEADS * HD
IN_WIDTH = 5 * HW
XHD = 256
MEM_LEN = 256
D_FF = 4096
FF_BLK = D_FF // N_DEV
POOL_WINDOWS = (2, 4, 8, 16)
POOL_HALO = 16
CHUNK = 64
CHUNKS_PER_STEP = 8
SUB = 16
N_SUB = CHUNK // SUB
EXP_CAP = 80.0
EPS = 1e-6
TINY = 1e-30
ROW_TILE = 512
WIDE_ROW_TILE = 1024
SLOT = 8
V7X_VMEM_LIMIT = 56 * 1024 * 1024

ADAM_LR = 0.001
ADAM_B1 = 0.9
ADAM_B2 = 0.999
ADAM_EPS = 1e-08
ADAM_WD = 0.01
ADAM_STEP = 10

MESH_ID = pl.DeviceIdType.MESH


def _params(sem=None, vmem=V7X_VMEM_LIMIT):
    return pltpu.CompilerParams(dimension_semantics=sem, vmem_limit_bytes=vmem)


def _mm(a, b):
    return lax.dot_general(a.astype(BF16), b.astype(BF16), (((1,), (0,)), ((), ())), preferred_element_type=F32)


def _mm_nt(a, b):
    return lax.dot_general(a.astype(BF16), b.astype(BF16), (((1,), (1,)), ((), ())), preferred_element_type=F32)


def _mm_tn(a, b):
    return lax.dot_general(a.astype(BF16), b.astype(BF16), (((0,), (0,)), ((), ())), preferred_element_type=F32)


def _sigmoid(x):
    return 1.0 / (1.0 + jnp.exp(-x))


def _rms(x):
    return lax.rsqrt(jnp.mean(x * x, axis=-1, keepdims=True) + EPS)


def _rms_bwd(x, g, dh):
    r = _rms(x)
    n = x * r
    dn = dh * g
    dx = r * (dn - n * jnp.mean(dn * n, axis=-1, keepdims=True))
    return dx, jnp.sum(dh * n, axis=0, keepdims=True)


def _tri_dot(tri, x, passes):
    acc = None
    rest = x
    for _ in range(passes):
        piece = rest.astype(BF16)
        part = lax.dot_general(tri, piece, (((1,), (0,)), ((), ())), preferred_element_type=F32)
        acc = part if acc is None else acc + part
        rest = rest - piece.astype(F32)
    return acc


def _adam_update(g, w, m, v):
    nm = ADAM_B1 * m + (1.0 - ADAM_B1) * g
    nv = ADAM_B2 * v + (1.0 - ADAM_B2) * (g * g)
    m_hat = nm / (1.0 - ADAM_B1 ** ADAM_STEP)
    v_hat = nv / (1.0 - ADAM_B2 ** ADAM_STEP)
    return -ADAM_LR * (m_hat / (jnp.sqrt(v_hat) + ADAM_EPS) + ADAM_WD * w), nm, nv


def _full(shape):
    return pl.BlockSpec(shape, lambda *_: (0,) * len(shape))


VMEM_WHOLE = pl.BlockSpec(memory_space=pltpu.VMEM)
ANY_SPACE = pl.BlockSpec(memory_space=pl.ANY)


def _mesh_pos():
    return lax.axis_index("x"), lax.axis_index("y"), lax.axis_index("c")


def _flat(px, py, pc):
    return 4 * px + 2 * py + pc


def _all_gather_weights(shards, cast_only):
    n, nc = len(shards), len(cast_only)
    step = 64

    def body(*refs):
        x_refs, c_refs = refs[:n], refs[n:n + nc]
        out_refs, cast_refs = refs[n + nc:2 * n + nc], refs[2 * n + nc:2 * n + 2 * nc]
        bufs = refs[2 * n + 2 * nc:3 * n + 2 * nc]
        send_sems, recv_sems, local_sems = refs[3 * n + 2 * nc:]
        x, y, c = _mesh_pos()
        me, sibling = (x, y, c), (x, y, 1 - c)
        chips = [(1 - x, y), (x, 1 - y), (1 - x, 1 - y)]

        def copy(a, k, blk, to, src=None):
            rows = out_refs[a].at[_flat(*blk)]
            return pltpu.make_async_remote_copy(
                src_ref=rows if src is None else src, dst_ref=rows,
                send_sem=send_sems.at[7 * a + k], recv_sem=recv_sems.at[7 * a + k], device_id=to, device_id_type=MESH_ID)

        def cast_rows(src, dst, rows):
            def cast(i, carry):
                r0 = pl.multiple_of(i * step, step)
                dst[pl.ds(r0, step), :] = src[pl.ds(r0, step), :].astype(BF16)
                return carry
            lax.fori_loop(0, rows // step, cast, 0)

        first, mine = [], []
        for a in range(n):
            cast_rows(x_refs[a], bufs[a], shards[a].shape[0])
            mine.append(pltpu.make_async_copy(bufs[a], out_refs[a].at[_flat(*me)], local_sems.at[a]))
            first.append(copy(a, 0, me, sibling, src=bufs[a]))
            first += [copy(a, 1 + j, me, (*chip, c), src=bufs[a]) for j, chip in enumerate(chips)]
            for cp in [mine[-1]] + first[-4:]:
                cp.start()
        for a in range(nc):
            cast_rows(c_refs[a], cast_refs[a], cast_only[a].shape[0])
        passed = []
        for j, chip in enumerate(chips):
            for a in range(n):
                copy(a, 1 + j, (*chip, c), me).wait_recv()
                passed.append(copy(a, 4 + j, (*chip, c), sibling))
                passed[-1].start()
        for a in range(n):
            copy(a, 0, sibling, me).wait_recv()
            for j, chip in enumerate(chips):
                copy(a, 4 + j, (*chip, 1 - c), me).wait_recv()
        for cp in first + passed:
            cp.wait_send()
        for cp in mine:
            cp.wait()

    return pl.pallas_call(
        body, name="all_gather_w_in",
        out_shape=[jax.ShapeDtypeStruct((N_DEV,) + s.shape, BF16) for s in shards]
        + [jax.ShapeDtypeStruct(s.shape, BF16) for s in cast_only],
        in_specs=[VMEM_WHOLE] * (n + nc), out_specs=[ANY_SPACE] * n + [VMEM_WHOLE] * nc,
        scratch_shapes=[pltpu.VMEM(s.shape, BF16) for s in shards]
        + [pltpu.SemaphoreType.DMA((7 * n,)), pltpu.SemaphoreType.DMA((7 * n,)), pltpu.SemaphoreType.DMA((n,))],
        compiler_params=_params(),
    )(*shards, *cast_only)


HBM_SPEC = pl.BlockSpec(memory_space=pltpu.HBM)
SEM_SPEC = pl.BlockSpec(memory_space=pltpu.SEMAPHORE)
EFFECT = pltpu.SideEffectType.DATAFLOW_SIDE_EFFECTING
TOKEN = jax.ShapeDtypeStruct((8, 128), F32)


def _in_hbm(a):
    return pltpu.with_memory_space_constraint(a, pltpu.HBM)


START_IDS = {name: i for i, name in enumerate((
    "gather_first_start", "gather_attn_forward_start", "gather_mlp_forward_start", "exchange_ff2_start",
    "exchange_ff1_start", "exchange_attn_start", "exchange_out_start", "exchange_in_start", "exchange_mix_start"))}


def _peers_all():
    x, y, c = _mesh_pos()
    return [(1 - x if k & 4 else x, 1 - y if k & 2 else y, 1 - c if k & 1 else c) for k in range(1, N_DEV)]


def _peers_first_level():
    x, y, c = _mesh_pos()
    return [(x, y, 1 - c), (1 - x, y, c), (x, 1 - y, c), (1 - x, 1 - y, c)]


def _peers_sibling():
    x, y, c = _mesh_pos()
    return [(x, y, 1 - c)]


def _split_start(copies_of, srcs, lands, n_sems, name, peers_of, collective_id):
    ns, nl, k = len(srcs), len(lands), len(n_sems)

    def body(*refs):
        barrier = pltpu.get_barrier_semaphore()
        peers = peers_of()
        for peer in peers:
            pl.semaphore_signal(barrier, inc=1, device_id=peer, device_id_type=MESH_ID)
        pl.semaphore_wait(barrier, len(peers))
        src_refs, land_refs = refs[:ns], refs[ns:ns + nl]
        sems = refs[ns + nl:ns + nl + k]
        token = refs[-1]
        for cp in copies_of(src_refs, land_refs, sems):
            cp.start()
        token[...] = jnp.zeros_like(token)

    outs = pl.pallas_call(
        body, name=name,
        out_shape=[pltpu.SemaphoreType.DMA((q,)) for q in n_sems]
        + [pltpu.HBM(a.shape, a.dtype) for a in list(srcs) + list(lands)] + [TOKEN],
        in_specs=[HBM_SPEC] * (ns + nl),
        out_specs=[SEM_SPEC] * k + [HBM_SPEC] * (ns + nl) + [VMEM_WHOLE],
        input_output_aliases={i: k + i for i in range(ns + nl)},
        compiler_params=pltpu.CompilerParams(has_side_effects=EFFECT, collective_id=collective_id),
    )(*[_in_hbm(a) for a in list(srcs) + list(lands)])
    return outs[:k], outs[k:k + ns], outs[k + ns:k + ns + nl], outs[-1]


def _split_wait(copies_of, handle, after, name):
    sems, srcs, lands, _ = handle
    ns, nl, k = len(srcs), len(lands), len(sems)

    def body(*refs):
        src_refs, land_refs = refs[:ns], refs[ns:ns + nl]
        sem_refs = refs[ns + nl:ns + nl + k]
        for cp in copies_of(src_refs, land_refs, sem_refs):
            cp.wait()

    outs = pl.pallas_call(
        body, name=name,
        out_shape=[pltpu.HBM(a.shape, a.dtype) for a in list(srcs) + list(lands)],
        in_specs=[HBM_SPEC] * (ns + nl) + [SEM_SPEC] * k + [ANY_SPACE],
        out_specs=[HBM_SPEC] * (ns + nl),
        input_output_aliases={i: i for i in range(ns + nl)},
        compiler_params=pltpu.CompilerParams(has_side_effects=EFFECT),
    )(*srcs, *lands, *sems, after)
    return outs[ns:]


def _gather_first_copies(shard_refs, land_refs, sems):
    send_sems, recv_sems, local_sems = sems
    x, y, c = _mesh_pos()
    me = _flat(x, y, c)
    peers = [(x, y, 1 - c), (1 - x, y, c), (x, 1 - y, c), (1 - x, 1 - y, c)]
    copies = []
    for a, (shard, land) in enumerate(zip(shard_refs, land_refs)):
        copies.append(pltpu.make_async_copy(shard, land.at[me], local_sems.at[a]))
        for k, peer in enumerate(peers):
            copies.append(pltpu.make_async_remote_copy(
                src_ref=shard, dst_ref=land.at[me], send_sem=send_sems.at[4 * a + k], recv_sem=recv_sems.at[4 * a + k],
                device_id=peer, device_id_type=MESH_ID))
    return copies


def _gather_forward_copies(src_refs, land_refs, sems):
    del src_refs
    send_sems, recv_sems = sems
    x, y, c = _mesh_pos()
    chips = [(1 - x, y), (x, 1 - y), (1 - x, 1 - y)]
    copies = []
    for a, land in enumerate(land_refs):
        for j, chip in enumerate(chips):
            rows = land.at[_flat(*chip, c)]
            copies.append(pltpu.make_async_remote_copy(
                src_ref=rows, dst_ref=rows, send_sem=send_sems.at[3 * a + j], recv_sem=recv_sems.at[3 * a + j],
                device_id=(x, y, 1 - c), device_id_type=MESH_ID))
    return copies


def _gather_first_start(groups, name):
    shards = [s for g in groups for s in g]
    lands = [lax.empty((N_DEV,) + s.shape, s.dtype) for s in shards]
    bounds = [sum(len(g) for g in groups[:i]) for i in range(len(groups) + 1)]

    def copies_of(src_refs, land_refs, sems):
        copies = []
        for i in range(len(groups)):
            lo, hi = bounds[i], bounds[i + 1]
            copies += _gather_first_copies(src_refs[lo:hi], land_refs[lo:hi], sems[3 * i:3 * i + 3])
        return copies

    n_sems = tuple(q for g in groups for q in (4 * len(g), 4 * len(g), len(g)))
    sems, srcs, lands, token = _split_start(copies_of, shards, lands, n_sems, name, _peers_first_level, START_IDS[name])
    return [(sems[3 * i:3 * i + 3], srcs[bounds[i]:bounds[i + 1]], lands[bounds[i]:bounds[i + 1]], token)
            for i in range(len(groups))]


def _gather_forward_start(lands, name):
    n = len(lands)
    return _split_start(_gather_forward_copies, [], lands, (3 * n, 3 * n), name, _peers_sibling, START_IDS[name])


def _all_to_all_copies(n_scattered):
    def copies_of(src_refs, land_refs, sems):
        send_sems, recv_sems, local_sems = sems
        x, y, c = _mesh_pos()
        me = _flat(x, y, c)
        copies = []
        for a, (src, land) in enumerate(zip(src_refs, land_refs)):
            scattered = a < n_scattered
            copies.append(pltpu.make_async_copy(src.at[me] if scattered else src, land.at[me], local_sems.at[a]))
            for k in range(1, N_DEV):
                peer = (1 - x if k & 4 else x, 1 - y if k & 2 else y, 1 - c if k & 1 else c)
                copies.append(pltpu.make_async_remote_copy(
                    src_ref=src.at[_flat(*peer)] if scattered else src, dst_ref=land.at[me],
                    send_sem=send_sems.at[7 * a + k - 1], recv_sem=recv_sems.at[7 * a + k - 1],
                    device_id=peer, device_id_type=MESH_ID))
        return copies
    return copies_of


def _all_to_all_start(scattered, broadcast, name):
    srcs = list(scattered) + list(broadcast)
    lands = [lax.empty(a.shape, a.dtype) for a in scattered] + [lax.empty((N_DEV,) + a.shape, a.dtype) for a in broadcast]
    n = len(srcs)
    return _split_start(_all_to_all_copies(len(scattered)), srcs, lands, (7 * n, 7 * n, n), name, _peers_all,
                        START_IDS[name])


def _call_behind(deps, body, *, in_specs, **kwargs):
    n_in, n_dep = len(in_specs), len(deps)

    def body_without_deps(*refs):
        return body(*refs[:n_in], *refs[n_in + n_dep:])

    call = pl.pallas_call(body_without_deps, in_specs=list(in_specs) + [ANY_SPACE] * n_dep, **kwargs)
    return lambda *operands: call(*operands, *deps)


def _row_tile(rows):
    for cand in (256, 128, 64, 32, 16):
        if rows % cand == 0:
            return cand
    return rows


def _adamw_whole(groups, name):
    n = len(groups)

    def body(*refs):
        for i in range(n):
            g_ref, w_ref, m_ref, v_ref = refs[4 * i:4 * i + 4]
            d_ref, nm_ref, nv_ref = refs[4 * n + 3 * i:4 * n + 3 * i + 3]
            d_ref[...], nm_ref[...], nv_ref[...] = _adam_update(g_ref[...], w_ref[...], m_ref[...], v_ref[...])

    outs = pl.pallas_call(
        body, name=name, out_shape=[jax.ShapeDtypeStruct(grp[0].shape, F32) for grp in groups for _ in range(3)],
        in_specs=[VMEM_WHOLE] * (4 * n), out_specs=[VMEM_WHOLE] * (3 * n),
        compiler_params=_params(),
    )(*[t for grp in groups for t in grp])
    return [outs[3 * i:3 * i + 3] for i in range(n)]


def _sum_sources_whole(recvs, name):
    n = len(recvs)

    def body(*refs):
        for r_ref, o_ref in zip(refs[:n], refs[n:]):
            acc = r_ref[0].astype(F32)
            for d in range(1, N_DEV):
                acc = acc + r_ref[d].astype(F32)
            o_ref[...] = acc

    return pl.pallas_call(
        body, name=name, out_shape=[jax.ShapeDtypeStruct(r.shape[1:], F32) for r in recvs],
        in_specs=[VMEM_WHOLE] * n, out_specs=[VMEM_WHOLE] * n,
        compiler_params=_params(),
    )(*recvs)


def _sum_adamw(recv, w, m, v, name):
    _, rows, cols = recv.shape
    tile = _row_tile(rows)

    def body(r_ref, w_ref, m_ref, v_ref, g_ref, d_ref, nm_ref, nv_ref):
        acc = r_ref[0].astype(F32)
        for d in range(1, N_DEV):
            acc = acc + r_ref[d].astype(F32)
        g_ref[...] = acc
        d_ref[...], nm_ref[...], nv_ref[...] = _adam_update(acc, w_ref[...], m_ref[...], v_ref[...])

    spec = pl.BlockSpec((tile, cols), lambda i: (i, 0))
    shp = jax.ShapeDtypeStruct((rows, cols), F32)
    return pl.pallas_call(
        body, name=name, grid=(rows // tile,), out_shape=[shp] * 4,
        in_specs=[pl.BlockSpec((N_DEV, tile, cols), lambda i: (0, i, 0)), spec, spec, spec], out_specs=[spec] * 4,
        compiler_params=_params(("parallel",)),
    )(recv, w, m, v)


SMALL_SLOTS = {"norm_x_g": (0, 0, 1, D_MODEL), "norm_mem_g": (0, 8, 1, D_MODEL), "norm_ffn_g": (0, 16, 1, D_MODEL),
               "final_norm_g": (0, 24, 1, D_MODEL), "pool_scale": (0, 32, 1, HW),
               "lb_logits": (1, 0, 2, HW), "hgrn_norm_g": (1, 8, HEADS, HD), "norm_mix_g": (2, 0, 1, D_MODEL)}
LOSS_ROW = 25
SMALL_ORDER = ("norm_mix_g", "lb_logits", "hgrn_norm_g", "pool_scale", "norm_x_g", "norm_mem_g", "norm_ffn_g",
               "final_norm_g", "w_pool")


def _small_update(srecvs, wprecv, params):
    flat = [t for n in SMALL_ORDER for t in params[n]]
    nb = len(srecvs)
    n_in = nb + 1 + len(flat)

    def body(*refs):
        s_refs, wp_ref = refs[0:nb], refs[nb]
        in_refs = refs[nb + 1:n_in]
        loss_ref = refs[n_in]
        out_refs = refs[n_in + 1:-nb]
        accs = refs[-nb:]
        for s_ref, acc in zip(s_refs, accs):
            total = s_ref[0]
            for d in range(1, N_DEV):
                total = total + s_ref[d]
            acc[...] = total
        loss_ref[...] = accs[0][LOSS_ROW:LOSS_ROW + 1, 0:1]
        for i, name in enumerate(SMALL_ORDER):
            w_ref, m_ref, v_ref = in_refs[3 * i:3 * i + 3]
            g_ref, d_ref, nm_ref, nv_ref = out_refs[4 * i:4 * i + 4]
            if name == "w_pool":
                g = wp_ref[0]
                for d in range(1, N_DEV):
                    g = g + wp_ref[d]
            else:
                buf, r0, nr, nc = SMALL_SLOTS[name]
                g = accs[buf][r0:r0 + nr, 0:nc]
            g_ref[...] = g
            d_ref[...], nm_ref[...], nv_ref[...] = _adam_update(g, w_ref[...], m_ref[...], v_ref[...])

    out_shape = [jax.ShapeDtypeStruct((1, 1), F32)]
    for n in SMALL_ORDER:
        out_shape += [jax.ShapeDtypeStruct(params[n][0].shape, F32)] * 4
    outs = pl.pallas_call(
        body, name="small_update", out_shape=out_shape,
        in_specs=[VMEM_WHOLE] * n_in, out_specs=[VMEM_WHOLE] * len(out_shape),
        scratch_shapes=[pltpu.VMEM(r.shape[1:], F32) for r in srecvs],
        compiler_params=_params(),
    )(*srecvs, wprecv, *flat)
    return outs[0], {n: outs[1 + 4 * i:5 + 4 * i] for i, n in enumerate(SMALL_ORDER)}


def _in_proj(x, g, w_t, deps):
    s = x.shape[0]
    tm = min(ROW_TILE, s)

    def body(x_ref, g_ref, w_ref, z_ref, h_ref):
        xv = x_ref[...]
        h = (xv * _rms(xv) * g_ref[...]).astype(BF16)
        h_ref[...] = h
        z_ref[...] = _mm_nt(h, w_ref[...])

    return _call_behind(
        deps, body, name="in_proj", grid=(s // tm,),
        out_shape=[jax.ShapeDtypeStruct((s, IN_WIDTH), F32), jax.ShapeDtypeStruct((s, D_MODEL), BF16)],
        in_specs=[pl.BlockSpec((tm, D_MODEL), lambda i: (i, 0)), _full((1, D_MODEL)), VMEM_WHOLE],
        out_specs=[pl.BlockSpec((tm, IN_WIDTH), lambda i: (i, 0)), pl.BlockSpec((tm, D_MODEL), lambda i: (i, 0))],
        compiler_params=_params(("parallel",)),
    )(x, g, w_t)


def _chunk_masks():
    row = lax.broadcasted_iota(jnp.int32, (CHUNK, CHUNK), 0)
    col = lax.broadcasted_iota(jnp.int32, (CHUNK, CHUNK), 1)
    return row, col


def _ones_where(mask):
    return jnp.where(mask, 1.0, 0.0).astype(BF16)


def _hgrn_gates(zq, zf, lb):
    sq = _sigmoid(zq)
    sig = _sigmoid(zf)
    f = lb + (1.0 - lb) * sig
    return zq * sq, sq, sig, f


def _sub_chunk_masks(width):
    trow = lax.broadcasted_iota(jnp.int32, (CHUNK, width), 0)
    return [(trow >= SUB * j) & (trow < SUB * (j + 1)) for j in range(N_SUB)]


def _head(a, h):
    return a[:, HD * h:HD * (h + 1)]


def _lanes(parts):
    return jnp.concatenate(parts, axis=1)


def _hgrn_decay_factors(b_scr, r0, b, in_sub):
    bases = [jnp.zeros((1, HW), F32)] + [b_scr[r0 + SUB * j - 1:r0 + SUB * j, :] for j in range(1, N_SUB)]
    own_base = bases[N_SUB - 1]
    for j in range(N_SUB - 2, -1, -1):
        own_base = jnp.where(in_sub[j], bases[j], own_base)
    eq = jnp.exp(b - own_base)
    ek = []
    for j in range(N_SUB):
        upto = SUB * (j + 1)
        e = jnp.exp(jnp.minimum(bases[j] - b[0:upto], EXP_CAP))
        ek.append(e if upto == CHUNK else jnp.concatenate([e, jnp.zeros((CHUNK - upto, HW), F32)], axis=0))
    return eq, ek


def _per_sub_chunk(x, in_sub):
    return _lanes([jnp.where(in_sub[j], x, 0.0) for j in range(N_SUB)])


def _own_lane_block(a, in_sub):
    out = a[:, HD * (N_SUB - 1):HD * N_SUB]
    for j in range(N_SUB - 2, -1, -1):
        out = jnp.where(in_sub[j], a[:, HD * j:HD * (j + 1)], out)
    return out


def _head_rms(o):
    return _lanes([jnp.broadcast_to(_rms(_head(o, h)), (CHUNK, HD)) for h in range(HEADS)])


def _head_mean(a):
    return _lanes([jnp.broadcast_to(jnp.mean(_head(a, h), axis=-1, keepdims=True), (CHUNK, HD)) for h in range(HEADS)])


def _hgrn_fwd(z, lb_logits, gn):
    s = z.shape[0]
    n_chunks = s // CHUNK

    def body(zq_ref, zf_ref, zi_ref, zg_ref, lbl_ref, gn_ref, oa_ref, o_ref, st_ref, state, b_scr):
        @pl.when(pl.program_id(0) == 0)
        def _():
            state[...] = jnp.zeros_like(state)

        lb = _sigmoid(lbl_ref[0:1, :] - lbl_ref[1:2, :])
        row, col = _chunk_masks()
        causal = col <= row
        tri = _ones_where(causal)
        in_sub, in_sub_head = _sub_chunk_masks(HW), _sub_chunk_masks(HD)
        gn_row = _lanes([gn_ref[h:h + 1, :] for h in range(HEADS)])
        def front(c):
            r0 = CHUNK * c
            rs = slice(r0, r0 + CHUNK)
            q, _, _, f = _hgrn_gates(zq_ref[rs, :], zf_ref[rs, :], lb)
            kk = 1.0 - f
            b = _tri_dot(tri, jnp.log(f), 3)
            b_scr[rs, :] = b
            eq, ek = _hgrn_decay_factors(b_scr, r0, b, in_sub)
            b_last = b_scr[r0 + CHUNK - 1:r0 + CHUNK, :]
            qe = q * eq
            return {"rs": rs, "v": zi_ref[rs, :], "qg": q * jnp.exp(b), "kd": kk * jnp.exp(b_last - b),
                    "lam_last": jnp.exp(b_last),
                    "q16": [_per_sub_chunk(_head(qe, h), in_sub_head).astype(BF16) for h in range(HEADS)],
                    "ke16": [_lanes([_head(kk * e, h) for e in ek]).astype(BF16) for h in range(HEADS)]}

        def recurrence(c, p):
            st_ref[c] = state[...]
            a, o_inter = [], []
            for h in range(HEADS):
                vh, st = _head(p["v"], h), state[h]
                a.append(jnp.where(causal, _mm_nt(p["q16"][h], p["ke16"][h]), 0.0))
                o_inter.append(_mm_nt(_head(p["qg"], h), st))
                state[h] = st * _head(p["lam_last"], h) + _mm_tn(vh, _head(p["kd"], h))
            return _lanes([_mm(a[h], _head(p["v"], h)) + o_inter[h] for h in range(HEADS)])

        def back(p, o):
            rs = p["rs"]
            o_ref[rs, :] = o
            zg = zg_ref[rs, :]
            oa_ref[rs, :] = (o * _head_rms(o) * gn_row * zg * _sigmoid(zg)).astype(BF16)

        p = front(0)
        for c in range(CHUNKS_PER_STEP):
            o = recurrence(c, p)
            p_next = front(c + 1) if c + 1 < CHUNKS_PER_STEP else None
            back(p, o)
            p = p_next

    rows = CHUNK * CHUNKS_PER_STEP
    zspec = lambda cb: pl.BlockSpec((rows, HW), lambda i, cb=cb: (i, cb))
    return pl.pallas_call(
        body, name="hgrn_fwd", grid=(s // rows,),
        out_shape=[jax.ShapeDtypeStruct((s, 2 * HW), BF16), jax.ShapeDtypeStruct((s, HW), F32),
                   jax.ShapeDtypeStruct((n_chunks, HEADS, HD, HD), F32)],
        in_specs=[zspec(0), zspec(1), zspec(2), zspec(3), _full((2, HW)), _full((HEADS, HD))],
        out_specs=[pl.BlockSpec((rows, HW), lambda i: (i, 0)), pl.BlockSpec((rows, HW), lambda i: (i, 0)),
                   pl.BlockSpec((CHUNKS_PER_STEP, HEADS, HD, HD), lambda i: (i, 0, 0, 0))],
        scratch_shapes=[pltpu.VMEM((HEADS, HD, HD), F32), pltpu.VMEM((rows, HW), F32)],
        compiler_params=_params(("arbitrary",)),
    )(z, z, z, z, lb_logits, gn)


def _pool_counts(tile_idx, tm):
    t = tile_idx * tm + lax.broadcasted_iota(jnp.int32, (tm, 1), 0)
    return [1.0 / jnp.minimum(t + 1, w).astype(F32) for w in POOL_WINDOWS]


def _pool_fwd(z, w_pool, scale, mixed_in, deps):
    s = z.shape[0]
    tm = min(ROW_TILE, s)

    def body(p_ref, w_ref, sc_ref, mixin_ref, ob_ref, pooled_ref, ext):
        i = pl.program_id(0)

        @pl.when(i == 0)
        def _():
            ext[0:POOL_HALO, :] = jnp.zeros((POOL_HALO, HW), F32)

        @pl.when(i > 0)
        def _():
            ext[0:POOL_HALO, :] = ext[tm:tm + POOL_HALO, :]

        ext[POOL_HALO:POOL_HALO + tm, :] = p_ref[...]
        inv = _pool_counts(i, tm)
        for g, w in enumerate(POOL_WINDOWS):
            sl = slice(HD * g, HD * (g + 1))
            p = ext[POOL_HALO:POOL_HALO + tm, sl]
            win = p
            for d in range(1, w):
                win = win + ext[POOL_HALO - d:POOL_HALO - d + tm, sl]
            pooled = (win * inv[g] - p).astype(BF16)
            pooled_ref[:, sl] = pooled
            ob_ref[:, sl] = (_mm(pooled, w_ref[g]) * sc_ref[:, sl]).astype(BF16)

    return _call_behind(
        deps, body, name="pool_fwd", grid=(s // tm,),
        out_shape=[jax.ShapeDtypeStruct((s, 2 * HW), BF16), jax.ShapeDtypeStruct((s, HW), BF16)],
        in_specs=[pl.BlockSpec((tm, HW), lambda i: (i, 4)), _full((HEADS, HD, HD)), _full((1, HW)), ANY_SPACE],
        out_specs=[pl.BlockSpec((tm, HW), lambda i: (i, 1)), pl.BlockSpec((tm, HW), lambda i: (i, 0))],
        scratch_shapes=[pltpu.VMEM((tm + POOL_HALO, HW), F32)],
        input_output_aliases={3: 0},
        compiler_params=_params(("arbitrary",)),
    )(z, w_pool, scale, mixed_in)


def _mem_kv(mem, g, wk, wv, deps):
    def body(m_ref, g_ref, wk_ref, wv_ref, hm_ref, k_ref, v_ref):
        m = m_ref[...]
        hm = (m * _rms(m) * g_ref[...]).astype(BF16)
        hm_ref[...] = hm
        k_ref[...] = _mm(hm, wk_ref[...]).astype(BF16)
        v_ref[...] = _mm(hm, wv_ref[...]).astype(BF16)

    shp = jax.ShapeDtypeStruct((MEM_LEN, D_MODEL), BF16)
    return _call_behind(
        deps, body, name="mem_kv", out_shape=[shp, shp, shp],
        in_specs=[VMEM_WHOLE] * 4, out_specs=[VMEM_WHOLE] * 3,
        compiler_params=_params(),
    )(mem, g, wk, wv)


def _softmax_rows(sc):
    e = jnp.exp(sc - jnp.max(sc, axis=-1, keepdims=True))
    return e / jnp.sum(e, axis=-1, keepdims=True)


def _mix_xattn_fwd(x0, mixed, w_out, g, wq, xk, xv, wo_t, deps):
    s = x0.shape[0]
    tm = min(ROW_TILE, s)
    scale = XHD ** -0.5

    def body(x_ref, mix_ref, wout_ref, g_ref, wq_ref, k_ref, v_ref, wo_ref, x1_ref, o_ref, hq_ref, q_ref, att_ref):
        xv_ = x_ref[...] + _mm(mix_ref[...], wout_ref[...])
        x1_ref[...] = xv_
        hq = (xv_ * _rms(xv_) * g_ref[...]).astype(BF16)
        hq_ref[...] = hq
        q_ref[...] = (_mm(hq, wq_ref[...]) * scale).astype(BF16)
        heads = [slice(XHD * h, XHD * (h + 1)) for h in range(HEADS)]
        scores = [_mm_nt(q_ref[:, sl], k_ref[:, sl]) for sl in heads]
        probs = [_softmax_rows(sc) for sc in scores]
        for sl, p in zip(heads, probs):
            att_ref[:, sl] = _mm(p, v_ref[:, sl]).astype(BF16)
        o_ref[...] = xv_ + _mm_nt(att_ref[...], wo_ref[...])

    row_f32 = pl.BlockSpec((tm, D_MODEL), lambda i: (i, 0))
    bshape = jax.ShapeDtypeStruct((s, D_MODEL), BF16)
    fshape = jax.ShapeDtypeStruct((s, D_MODEL), F32)
    return _call_behind(
        deps, body, name="mix_xattn_fwd", grid=(s // tm,),
        out_shape=[fshape, fshape, bshape, bshape, bshape],
        in_specs=[row_f32, row_f32, VMEM_WHOLE, _full((1, D_MODEL)), VMEM_WHOLE, VMEM_WHOLE, VMEM_WHOLE, VMEM_WHOLE],
        out_specs=[row_f32] * 5,
        compiler_params=_params(("parallel",)),
    )(x0, mixed, w_out, g, wq, xk, xv, wo_t)


def _mlp_fwd_loss(x, g, w1, w2, gf, target):
    s = x.shape[0]
    tm = min(ROW_TILE, s)

    def body(x_ref, g_ref, w1_ref, w2_ref, gf_ref, t_ref, dx_ref, dx16_ref, u_ref, hf_ref, slot_ref):
        @pl.when(pl.program_id(0) == 0)
        def _():
            slot_ref[...] = jnp.zeros_like(slot_ref)

        xv = x_ref[...]
        hf = (xv * _rms(xv) * g_ref[...]).astype(BF16)
        hf_ref[...] = hf
        a_next = _mm(hf, w1_ref[0])
        for j in range(N_DEV):
            a = jnp.maximum(a_next, 0.0)
            if j + 1 < N_DEV:
                a_next = _mm(hf, w1_ref[j + 1])
            u_ref[:, FF_BLK * j:FF_BLK * (j + 1)] = (a * a).astype(BF16)
        acc = xv + _mm(u_ref[...], w2_ref[...])
        gfv = gf_ref[...]
        r = _rms(acc)
        n = acc * r
        err = n * gfv - t_ref[...]
        slot_ref[1:2, :] += jnp.sum(jnp.mean(err * err, axis=-1, keepdims=True), axis=0, keepdims=True) * 0.5
        dy = err * (1.0 / D_MODEL)
        slot_ref[0:1, :] += jnp.sum(dy * n, axis=0, keepdims=True)
        dn = dy * gfv
        dx = r * (dn - n * jnp.mean(dn * n, axis=-1, keepdims=True))
        dx_ref[...] = dx
        dx16_ref[...] = dx.astype(BF16)

    row_f32 = pl.BlockSpec((tm, D_MODEL), lambda i: (i, 0))
    return pl.pallas_call(
        body, name="mlp_fwd_loss", grid=(s // tm,),
        out_shape=[jax.ShapeDtypeStruct((s, D_MODEL), F32), jax.ShapeDtypeStruct((s, D_MODEL), BF16),
                   jax.ShapeDtypeStruct((s, D_FF), BF16), jax.ShapeDtypeStruct((s, D_MODEL), BF16),
                   jax.ShapeDtypeStruct((SLOT, D_MODEL), F32)],
        in_specs=[row_f32, _full((1, D_MODEL)), VMEM_WHOLE, VMEM_WHOLE, _full((1, D_MODEL)), row_f32],
        out_specs=[row_f32, row_f32, pl.BlockSpec((tm, D_FF), lambda i: (i, 0)), row_f32, _full((SLOT, D_MODEL))],
        compiler_params=_params(("arbitrary",)),
    )(x, g, w1, w2, gf, target)


def _zero_slot(slot_ref):
    @pl.when(pl.program_id(0) == 0)
    def _():
        slot_ref[...] = jnp.zeros_like(slot_ref)


def _mlp_bwd(dx3, u, x2, g, w1, w2, deps):
    s = x2.shape[0]
    tm = min(ROW_TILE // 2, s)

    def body(d_ref, u_ref, x_ref, g_ref, w1_ref, w2_ref, da_ref, dx_ref, slot_ref):
        _zero_slot(slot_ref)
        d = d_ref[...]
        d16 = d.astype(BF16)
        du_next = _mm_nt(d16, w2_ref[0])
        dhf = jnp.zeros((tm, D_MODEL), F32)
        for j in range(N_DEV):
            sl = slice(FF_BLK * j, FF_BLK * (j + 1))
            du = du_next
            if j + 1 < N_DEV:
                du_next = _mm_nt(d16, w2_ref[j + 1])
            u = u_ref[:, sl].astype(F32)
            da = (du * (2.0 * u * lax.rsqrt(jnp.maximum(u, TINY)))).astype(BF16)
            da_ref[:, sl] = da
            dhf = dhf + _mm_nt(da, w1_ref[j])
        dx, dg = _rms_bwd(x_ref[...], g_ref[...], dhf)
        dx_ref[...] = d + dx
        slot_ref[0:1, :] += dg

    row_f32 = pl.BlockSpec((tm, D_MODEL), lambda i: (i, 0))
    return _call_behind(
        deps, body, name="mlp_bwd", grid=(s // tm,),
        out_shape=[jax.ShapeDtypeStruct((s, D_FF), BF16), jax.ShapeDtypeStruct((s, D_MODEL), F32),
                   jax.ShapeDtypeStruct((SLOT, D_MODEL), F32)],
        in_specs=[row_f32, pl.BlockSpec((tm, D_FF), lambda i: (i, 0)), row_f32, _full((1, D_MODEL)),
                  VMEM_WHOLE, VMEM_WHOLE],
        out_specs=[pl.BlockSpec((tm, D_FF), lambda i: (i, 0)), row_f32, _full((SLOT, D_MODEL))],
        compiler_params=_params(("arbitrary",)),
    )(dx3, u, x2, g, w1, w2)


def _wgrad(a, b, name, col_blocks=False, update=None):
    s, m = a.shape
    n = b.shape[1]
    tm = 1280 if m % 1280 == 0 else min(1024, m)
    tn = min(1024, n)
    blk = n // N_DEV
    per_step = tn // blk if col_blocks else 1
    ts = min((4 if m * n >= D_MODEL * D_FF else 2) * ROW_TILE, s)
    n_s = s // ts
    grid = (m // tm, n // tn, n_s)

    def body(a_ref, b_ref, *rest):
        o_ref, acc = rest[-2], rest[-1]
        k = pl.program_id(2)

        @pl.when(k == 0)
        def _():
            acc[...] = jnp.zeros_like(acc)

        acc[...] += _mm_tn(a_ref[...], b_ref[...])
        if update is not None:
            r_ref, w_ref, m_ref, v_ref, g_ref, d_ref, nm_ref, nv_ref = rest[:8]
            g = r_ref[0].astype(F32)
            for d in range(1, N_DEV):
                g = g + r_ref[d].astype(F32)
            g_ref[...] = g
            d_ref[...], nm_ref[...], nv_ref[...] = _adam_update(g, w_ref[...], m_ref[...], v_ref[...])

        @pl.when(k == n_s - 1)
        def _():
            if col_blocks:
                for p in range(per_step):
                    o_ref[p] = acc[:, blk * p:blk * (p + 1)].astype(BF16)
            else:
                o_ref[...] = acc[...].astype(BF16)

    if col_blocks:
        out_shape = jax.ShapeDtypeStruct((N_DEV, m, blk), BF16)
        out_spec = pl.BlockSpec((per_step, tm, blk), lambda i, j, k: (j, i, 0))
    else:
        out_shape = jax.ShapeDtypeStruct((m, n), BF16)
        out_spec = pl.BlockSpec((tm, tn), lambda i, j, k: (i, j))
    in_specs = [pl.BlockSpec((ts, tm), lambda i, j, k: (k, i)), pl.BlockSpec((ts, tn), lambda i, j, k: (k, j))]
    out_shapes, out_specs, operands = [out_shape], [out_spec], [a, b]
    if update is not None:
        rows, cols = update[1].shape
        steps = grid[0] * grid[1] * grid[2]
        tr = rows // steps
        step = lambda i, j, k: (i * grid[1] + j) * grid[2] + k
        piece = pl.BlockSpec((tr, cols), lambda i, j, k: (step(i, j, k), 0))
        in_specs += [pl.BlockSpec((N_DEV, tr, cols), lambda i, j, k: (0, step(i, j, k), 0)), piece, piece, piece]
        out_shapes = [jax.ShapeDtypeStruct((rows, cols), F32)] * 4 + out_shapes
        out_specs = [piece] * 4 + out_specs
        operands += list(update)
    outs = pl.pallas_call(
        body, name=name, grid=grid, out_shape=out_shapes, in_specs=in_specs, out_specs=out_specs,
        scratch_shapes=[pltpu.VMEM((tm, tn), F32)],
        compiler_params=_params(("parallel", "parallel", "arbitrary")),
    )(*operands)
    return outs[0] if update is None else (outs[4], tuple(outs[:4]))


def _xattn_bwd(dx2, x1, g, q, xk, xv, wq, wo_t, deps):
    s = x1.shape[0]
    tm = min(ROW_TILE, s)
    scale = XHD ** -0.5

    def body(d_ref, x_ref, g_ref, q_ref, k_ref, v_ref, wq_ref, wo_ref, dx_ref, dx16_ref, dq_ref, dk_ref, dv_ref, slot_ref,
             datt):
        _zero_slot(slot_ref)

        @pl.when(pl.program_id(0) == 0)
        def _():
            dk_ref[...] = jnp.zeros_like(dk_ref)
            dv_ref[...] = jnp.zeros_like(dv_ref)

        d = d_ref[...]
        datt[...] = _mm(d, wo_ref[...]).astype(BF16)
        heads = [slice(XHD * h, XHD * (h + 1)) for h in range(HEADS)]
        scores = [_mm_nt(q_ref[:, sl], k_ref[:, sl]) for sl in heads]
        dps = [_mm_nt(datt[:, sl], v_ref[:, sl]) for sl in heads]
        probs = [_softmax_rows(sc) for sc in scores]
        dss = [(p * (dp - jnp.sum(dp * p, axis=-1, keepdims=True))).astype(BF16) for p, dp in zip(probs, dps)]
        for sl, p, ds in zip(heads, probs, dss):
            dq_ref[:, sl] = (_mm(ds, k_ref[:, sl]) * scale).astype(BF16)
            dk_ref[:, sl] += _mm_tn(ds, q_ref[:, sl])
            dv_ref[:, sl] += _mm_tn(p, datt[:, sl])
        dx, dg = _rms_bwd(x_ref[...], g_ref[...], _mm_nt(dq_ref[...], wq_ref[...]))
        dx_ref[...] = d + dx
        dx16_ref[...] = (d + dx).astype(BF16)
        slot_ref[0:1, :] += dg

    row_f32 = pl.BlockSpec((tm, D_MODEL), lambda i: (i, 0))
    kv = jax.ShapeDtypeStruct((MEM_LEN, D_MODEL), F32)
    tokens16 = jax.ShapeDtypeStruct((s, D_MODEL), BF16)
    return _call_behind(
        deps, body, name="xattn_bwd", grid=(s // tm,),
        out_shape=[jax.ShapeDtypeStruct((s, D_MODEL), F32), tokens16, tokens16, kv, kv,
                   jax.ShapeDtypeStruct((SLOT, D_MODEL), F32)],
        in_specs=[row_f32, row_f32, _full((1, D_MODEL)), row_f32, VMEM_WHOLE, VMEM_WHOLE, VMEM_WHOLE, VMEM_WHOLE],
        out_specs=[row_f32, row_f32, row_f32, _full((MEM_LEN, D_MODEL)), _full((MEM_LEN, D_MODEL)),
                   _full((SLOT, D_MODEL))],
        scratch_shapes=[pltpu.VMEM((tm, D_MODEL), BF16)],
        compiler_params=_params(("arbitrary",)),
    )(dx2, x1, g, q, xk, xv, wq, wo_t)


def _mem_bwd(mem, g, hm, dxk, dxv, wk, wv):
    def body(m_ref, g_ref, hm_ref, dk_ref, dv_ref, wk_ref, wv_ref, dwk_ref, dwv_ref, slot_ref):
        dk, dv = dk_ref[...], dv_ref[...]
        hm_ = hm_ref[...]
        dwk_ref[...] = _mm_tn(hm_, dk).astype(BF16)
        dwv_ref[...] = _mm_tn(hm_, dv).astype(BF16)
        _, dg = _rms_bwd(m_ref[...], g_ref[...], _mm_nt(dk, wk_ref[...]) + _mm_nt(dv, wv_ref[...]))
        slot_ref[...] = jnp.zeros_like(slot_ref)
        slot_ref[0:1, :] = dg

    wshape = jax.ShapeDtypeStruct((D_MODEL, D_MODEL), BF16)
    return pl.pallas_call(
        body, name="mem_bwd", out_shape=[wshape, wshape, jax.ShapeDtypeStruct((SLOT, D_MODEL), F32)],
        in_specs=[VMEM_WHOLE] * 7, out_specs=[VMEM_WHOLE] * 3,
        compiler_params=_params(),
    )(mem, g, hm, dxk, dxv, wk, wv)


def _pool_bwd(dx1, w_out, pooled, w_pool, scale, deps):
    s = dx1.shape[0]
    tm = min(ROW_TILE, s)
    n_t = s // tm

    def body(dx_ref, wo_ref, pl_ref, w_ref, sc_ref, dz_ref, dw_ref, slot_ref, ext, do_ref):
        i = pl.program_id(0)
        tile = n_t - 1 - i
        _zero_slot(slot_ref)
        do_ref[...] = _mm_nt(dx_ref[...], wo_ref[HW:2 * HW, :])

        @pl.when(i == 0)
        def _():
            dw_ref[...] = jnp.zeros_like(dw_ref)
            ext[tm:tm + POOL_HALO, :] = jnp.zeros((POOL_HALO, HW), F32)

        @pl.when(i > 0)
        def _():
            ext[tm:tm + POOL_HALO, :] = ext[0:POOL_HALO, :]

        inv = _pool_counts(tile, tm)
        dpooled = []
        for g in range(HEADS):
            sl = slice(HD * g, HD * (g + 1))
            pooled_g = pl_ref[:, sl]
            do = do_ref[:, sl]
            slot_ref[0:1, sl] += jnp.sum(_mm(pooled_g, w_ref[g]) * do, axis=0, keepdims=True)
            dy = (do * sc_ref[:, sl]).astype(BF16)
            dw_ref[g] += _mm_tn(pooled_g, dy)
            dpo = _mm_nt(dy, w_ref[g])
            dpooled.append(dpo)
            ext[0:tm, sl] = dpo * inv[g]
        for g, w in enumerate(POOL_WINDOWS):
            sl = slice(HD * g, HD * (g + 1))
            win = ext[0:tm, sl]
            for d in range(1, w):
                win = win + ext[d:d + tm, sl]
            dz_ref[:, sl] = (win - dpooled[g]).astype(BF16)

    return _call_behind(
        deps, body, name="pool_bwd", grid=(n_t,),
        out_shape=[jax.ShapeDtypeStruct((s, IN_WIDTH), BF16), jax.ShapeDtypeStruct((HEADS, HD, HD), F32),
                   jax.ShapeDtypeStruct((SLOT, D_MODEL), F32)],
        in_specs=[pl.BlockSpec((tm, D_MODEL), lambda i: (n_t - 1 - i, 0)), VMEM_WHOLE,
                  pl.BlockSpec((tm, HW), lambda i: (n_t - 1 - i, 0)), _full((HEADS, HD, HD)), _full((1, HW))],
        out_specs=[pl.BlockSpec((tm, HW), lambda i: (n_t - 1 - i, 4)), _full((HEADS, HD, HD)), _full((SLOT, D_MODEL))],
        scratch_shapes=[pltpu.VMEM((tm + POOL_HALO, HW), F32), pltpu.VMEM((tm, HW), F32)],
        compiler_params=_params(("arbitrary",)),
    )(dx1, w_out, pooled, w_pool, scale)


def _hgrn_bwd(z, o, dx1, w_out, states, lb_logits, gn, dz_in, deps):
    s = z.shape[0]
    n_chunks = s // CHUNK

    def body(zq_ref, zf_ref, zi_ref, zg_ref, o_ref, dx_ref, wo_ref, st_ref, lbl_ref, gn_ref, dzin_ref,
             dz_ref, dlb_ref, dgn_ref, dstate, b_scr, dlb_acc, do_ref):
        i = pl.program_id(0)

        @pl.when(i == 0)
        def _():
            dstate[...] = jnp.zeros_like(dstate)
            dlb_acc[...] = jnp.zeros_like(dlb_acc)
            dgn_ref[...] = jnp.zeros_like(dgn_ref)
            dlb_ref[...] = jnp.zeros_like(dlb_ref)

        do_ref[...] = _mm_nt(dx_ref[...], wo_ref[0:HW, :])
        lb = _sigmoid(lbl_ref[0:1, :] - lbl_ref[1:2, :])
        row, col = _chunk_masks()
        causal = col <= row
        tri = _ones_where(causal)
        upper = _ones_where(col >= row)
        strict_lower = _ones_where(col < row)
        in_sub, in_sub_head = _sub_chunk_masks(HW), _sub_chunk_masks(HD)
        gn_row = _lanes([gn_ref[h:h + 1, :] for h in range(HEADS)])
        sums = {"dlb": 0.0, "dgn": 0.0}

        def front(c):
            r0 = CHUNK * c
            rs = slice(r0, r0 + CHUNK)
            p = {"rs": rs}
            p["zq"] = zq_ref[rs, :]
            p["q"], p["sq"], p["sig"], p["f"] = _hgrn_gates(p["zq"], zf_ref[rs, :], lb)
            p["kk"] = 1.0 - p["f"]
            b = _tri_dot(tri, jnp.log(p["f"]), 3)
            b_scr[rs, :] = b
            p["v"] = zi_ref[rs, :]
            o, zg, doa = o_ref[rs, :], zg_ref[rs, :], do_ref[rs, :]
            sg = _sigmoid(zg)
            rms = _head_rms(o)
            n = o * rms
            don = doa * (zg * sg)
            sums["dgn"] = sums["dgn"] + jnp.sum(don * n, axis=0, keepdims=True)
            dn = don * gn_row
            p["d_o"] = rms * (dn - n * _head_mean(dn * n))
            dz_ref[rs, 3 * HW:4 * HW] = (doa * (n * gn_row) * (sg * (1.0 + zg * (1.0 - sg)))).astype(BF16)
            p["eq"], p["ek"] = _hgrn_decay_factors(b_scr, r0, b, in_sub)
            b_last = b_scr[r0 + CHUNK - 1:r0 + CHUNK, :]
            p["lam"], p["e_last"], p["lam_last"] = jnp.exp(b), jnp.exp(b_last - b), jnp.exp(b_last)
            p["qe"], p["qg"], p["kd"] = p["q"] * p["eq"], p["q"] * p["lam"], p["kk"] * p["e_last"]
            p["ke"] = [p["kk"] * e for e in p["ek"]]
            p["q16"] = [_per_sub_chunk(_head(p["qe"], h), in_sub_head).astype(BF16) for h in range(HEADS)]
            p["ke16"] = [_lanes([_head(p["ke"][j], h) for j in range(N_SUB)]).astype(BF16) for h in range(HEADS)]
            return p

        def recurrence(c, p):
            m = {k: [] for k in ("dv", "gq", "gk", "dqi", "dkd", "st")}
            a, da, dv_state = [], [], []
            for h in range(HEADS):
                vh, doh = _head(p["v"], h), _head(p["d_o"], h)
                st0, ds1 = st_ref[c, h], dstate[h]
                a.append(jnp.where(causal, _mm_nt(p["q16"][h], p["ke16"][h]), 0.0))
                da.append(jnp.where(causal, _mm_nt(doh, vh), 0.0))
                dv_state.append(_mm_nt(_head(p["kd"], h), ds1))
                m["dqi"].append(_mm(doh, st0))
                m["dkd"].append(_mm(vh, ds1))
                m["st"].append(jnp.sum(st0 * ds1, axis=0, keepdims=True))
                dstate[h] = ds1 * _head(p["lam_last"], h) + _mm_tn(doh, _head(p["qg"], h))
            for h in range(HEADS):
                m["dv"].append(_mm_tn(a[h], _head(p["d_o"], h)) + dv_state[h])
                m["gq"].append(_own_lane_block(_mm(da[h], p["ke16"][h]), in_sub_head))
                m["gk"].append(_mm_tn(da[h], p["q16"][h]))
            return m

        def back(p, m):
            rs = p["rs"]
            dz_ref[rs, 2 * HW:3 * HW] = _lanes(m["dv"]).astype(BF16)
            gq = _lanes(m["gq"])
            gk = [_lanes([m["gk"][h][:, HD * j:HD * (j + 1)] for h in range(HEADS)]) for j in range(N_SUB)]
            dq_inter = p["lam"] * _lanes(m["dqi"])
            dq = p["eq"] * gq + dq_inter
            dk_intra = sum(p["ek"][j] * gk[j] for j in range(N_SUB))
            dk_state = _lanes(m["dkd"]) * p["e_last"]
            db_intra = (p["qe"].astype(BF16).astype(F32) * gq
                        - sum(p["ke"][j].astype(BF16).astype(F32) * gk[j] for j in range(N_SUB)))
            dlf = (_tri_dot(upper, db_intra + p["q"] * dq_inter, 2) + _tri_dot(strict_lower, p["kk"] * dk_state, 2)
                   + p["lam_last"] * _lanes(m["st"]))
            sig, sq, zq = p["sig"], p["sq"], p["zq"]
            df = dlf / p["f"] - (dk_intra + dk_state)
            sums["dlb"] = sums["dlb"] + jnp.sum(df * (1.0 - sig), axis=0, keepdims=True)
            dz_ref[rs, HW:2 * HW] = (df * (1.0 - lb) * sig * (1.0 - sig)).astype(BF16)
            dz_ref[rs, 0:HW] = (dq * (sq * (1.0 + zq * (1.0 - sq)))).astype(BF16)

        p = front(CHUNKS_PER_STEP - 1)
        for c in reversed(range(CHUNKS_PER_STEP)):
            m = recurrence(c, p)
            p_next = front(c - 1) if c > 0 else None
            back(p, m)
            p = p_next
        dlb_acc[...] += sums["dlb"]
        for h in range(HEADS):
            dgn_ref[h:h + 1, 0:HD] += _head(sums["dgn"], h)

        @pl.when(i == n_steps - 1)
        def _():
            dl0 = dlb_acc[...] * lb * (1.0 - lb)
            dlb_ref[0:1, 0:HW] = dl0
            dlb_ref[1:2, 0:HW] = -dl0

    rows = CHUNK * CHUNKS_PER_STEP
    n_steps = s // rows
    rev = lambda i: n_steps - 1 - i
    zspec = lambda cb: pl.BlockSpec((rows, HW), lambda i, cb=cb: (rev(i), cb))
    slot = jax.ShapeDtypeStruct((SLOT, D_MODEL), F32)
    return _call_behind(
        deps, body, name="hgrn_bwd", grid=(n_steps,),
        out_shape=[jax.ShapeDtypeStruct((s, IN_WIDTH), BF16), slot, slot],
        in_specs=[zspec(0), zspec(1), zspec(2), zspec(3), pl.BlockSpec((rows, HW), lambda i: (rev(i), 0)),
                  pl.BlockSpec((rows, D_MODEL), lambda i: (rev(i), 0)), VMEM_WHOLE,
                  pl.BlockSpec((CHUNKS_PER_STEP, HEADS, HD, HD), lambda i: (rev(i), 0, 0, 0)), _full((2, HW)),
                  _full((HEADS, HD)), ANY_SPACE],
        out_specs=[pl.BlockSpec((rows, 4 * HW), lambda i: (rev(i), 0)), _full((SLOT, D_MODEL)), _full((SLOT, D_MODEL))],
        scratch_shapes=[pltpu.VMEM((HEADS, HD, HD), F32), pltpu.VMEM((rows, HW), F32), pltpu.VMEM((1, HW), F32),
                        pltpu.VMEM((rows, HW), F32)],
        input_output_aliases={10: 0},
        compiler_params=_params(("arbitrary",)),
    )(z, z, z, z, o, dx1, w_out, states, lb_logits, gn, dz_in)


def _in_bwd(dz, w_t, x0, g, dx1, deps):
    s = x0.shape[0]
    tm = min(WIDE_ROW_TILE, s)

    def body(dz_ref, w_ref, x_ref, g_ref, d_ref, dx_ref, slot_ref):
        _zero_slot(slot_ref)
        dx, dg = _rms_bwd(x_ref[...], g_ref[...], _mm(dz_ref[...], w_ref[...]))
        dx_ref[...] = d_ref[...] + dx
        slot_ref[0:1, :] += dg

    row_f32 = pl.BlockSpec((tm, D_MODEL), lambda i: (i, 0))
    return _call_behind(
        deps, body, name="in_bwd", grid=(s // tm,),
        out_shape=[jax.ShapeDtypeStruct((s, D_MODEL), F32), jax.ShapeDtypeStruct((SLOT, D_MODEL), F32)],
        in_specs=[pl.BlockSpec((tm, IN_WIDTH), lambda i: (i, 0)), VMEM_WHOLE, row_f32, _full((1, D_MODEL)), row_f32],
        out_specs=[row_f32, _full((SLOT, D_MODEL))],
        compiler_params=_params(("arbitrary",)),
    )(dz, w_t, x0, g, dx1)


def kernel(x, mem, norm_mix_g, w_in, lb_logits, hgrn_norm_g, w_pool, pool_scale, w_out, norm_x_g, norm_mem_g, w_xq, w_xk, w_xv, w_xo, norm_ffn_g, w_ff1, w_ff2, final_norm_g, loss_target, m_norm_mix_g, m_w_in, m_lb_logits, m_hgrn_norm_g, m_w_pool, m_pool_scale, m_w_out, m_norm_x_g, m_norm_mem_g, m_w_xq, m_w_xk, m_w_xv, m_w_xo, m_norm_ffn_g, m_w_ff1, m_w_ff2, m_final_norm_g, v_norm_mix_g, v_w_in, v_lb_logits, v_hgrn_norm_g, v_w_pool, v_pool_scale, v_w_out, v_norm_x_g, v_norm_mem_g, v_w_xq, v_w_xk, v_w_xv, v_w_xo, v_norm_ffn_g, v_w_ff1, v_w_ff2, v_final_norm_g):
    x0 = x[0]
    mem0 = mem[0]
    tgt = loss_target[0]
    gn = hgrn_norm_g[0]
    gfin = final_norm_g.reshape(1, D_MODEL)
    wp = w_pool[0]
    heads_2d = lambda w: w.reshape(D_MODEL // N_DEV, D_MODEL)
    xo_2d = lambda w: w.reshape(D_MODEL, D_MODEL // N_DEV)

    first = _all_gather_weights([w_in[0].T], [w_out[0], heads_2d(w_xq), heads_2d(w_xk), heads_2d(w_xv), xo_2d(w_xo).T,
                                              w_ff1[0], w_ff2[0]])
    win_t = first[0].reshape(IN_WIDTH, D_MODEL)
    ga_attn, ga_mlp = _gather_first_start([first[1:6], first[6:8]], "gather_first_start")

    z, h = _in_proj(x0, norm_mix_g, win_t, deps=[ga_attn[3]])
    mixed_a, o_pre, states = _hgrn_fwd(z, lb_logits, gn)
    lands = _split_wait(_gather_first_copies, ga_attn, o_pre, "gather_attn_first_wait")
    gb_attn = _gather_forward_start(lands, "gather_attn_forward_start")
    mixed, pooled = _pool_fwd(z, wp, pool_scale, mixed_a, deps=[gb_attn[3]])
    lands = _split_wait(_gather_forward_copies, gb_attn, pooled, "gather_attn_forward_wait")
    wout_f, wq_f, wk_f, wv_f, wo_t = (t.reshape(D_MODEL, D_MODEL) for t in lands)
    hm, xk, xv = _mem_kv(mem0, norm_mem_g, wk_f, wv_f, deps=[])
    x1, x2, hq, xq, att = _mix_xattn_fwd(x0, mixed, wout_f, norm_x_g, wq_f, xk, xv, wo_t, deps=[])
    lands = _split_wait(_gather_first_copies, ga_mlp, x2, "gather_mlp_first_wait")
    gb_mlp = _gather_forward_start(lands, "gather_mlp_forward_start")
    w1_b, w2_b = _split_wait(_gather_forward_copies, gb_mlp, gb_mlp[3], "gather_mlp_forward_wait")
    dx3, dx3_16, u, hf, slot_fin = _mlp_fwd_loss(x2, norm_ffn_g, w1_b, w2_b.reshape(D_FF, D_MODEL), gfin, tgt)

    rows = lambda t, r: t.reshape(N_DEV, r, D_MODEL)
    dw2 = _wgrad(u, dx3_16, "wgrad_ff2")
    ex_ff2 = _all_to_all_start([rows(dw2, FF_BLK)], [], "exchange_ff2_start")
    da, dx2, slot_ffn = _mlp_bwd(dx3, u, x2, norm_ffn_g, w1_b, w2_b, deps=[ex_ff2[3]])
    dw1 = _wgrad(hf, da, "wgrad_ff1", col_blocks=True)
    ex_ff1 = _all_to_all_start([dw1], [], "exchange_ff1_start")
    dx1, dx1_16, dxq, dxk, dxv, slot_x = _xattn_bwd(dx2, x1, norm_x_g, xq, xk, xv, wq_f, wo_t, deps=[ex_ff1[3]])
    dwo_t = _wgrad(dx2, att, "wgrad_xo")
    dwq = _wgrad(hq, dxq, "wgrad_xq")
    dwk, dwv, slot_mem = _mem_bwd(mem0, norm_mem_g, hm, dxk, dxv, wk_f, wv_f)
    ex_attn = _all_to_all_start([rows(dwq, 128), rows(dwk, 128), rows(dwv, 128), rows(dwo_t, 128)], [],
                                "exchange_attn_start")
    dwout = _wgrad(mixed, dx1_16, "wgrad_out")
    dz_pool, d_wpool, slot_ps = _pool_bwd(dx1_16, wout_f, pooled, wp, pool_scale, deps=[ex_attn[3]])
    small0 = jnp.concatenate([slot_x, slot_mem, slot_ffn, slot_fin, slot_ps], axis=0)
    ex_out = _all_to_all_start([rows(dwout, 128)], [small0, d_wpool], "exchange_out_start")
    dz, slot_lb, slot_gn = _hgrn_bwd(z, o_pre, dx1_16, wout_f, states, lb_logits, gn, dz_pool, deps=[ex_out[3]])
    (r_2,) = _split_wait(_all_to_all_copies(1), ex_ff2, dz, "exchange_ff2_wait")
    dwin_t, ff2_update = _wgrad(dz, h, "wgrad_in", update=(r_2, w_ff2[0], m_w_ff2[0], v_w_ff2[0]))
    small1 = jnp.concatenate([slot_lb, slot_gn], axis=0)
    ex_in = _all_to_all_start([rows(dwin_t, 320)], [small1], "exchange_in_start")
    grad_x, slot_mix = _in_bwd(dz, win_t, x0, norm_mix_g, dx1, deps=[ex_in[3]])
    ex_mix = _all_to_all_start([], [slot_mix], "exchange_mix_start")

    out = {}
    out["w_ff2"] = ff2_update
    (r_1,) = _split_wait(_all_to_all_copies(1), ex_ff1, ex_mix[3], "exchange_ff1_wait")
    out["w_ff1"] = _sum_adamw(r_1, w_ff1[0], m_w_ff1[0], v_w_ff1[0], "adamw_ff1")
    r_q, r_k, r_v, r_o = _split_wait(_all_to_all_copies(4), ex_attn, out["w_ff1"][1], "exchange_attn_wait")
    sums = _sum_sources_whole([r_q, r_k, r_v, r_o], "sum_grad_attn")
    g_attn = [g.reshape(w_xq.shape) for g in sums[:3]] + [sums[3].T]
    attn = _adamw_whole([(g_attn[0], w_xq, m_w_xq, v_w_xq), (g_attn[1], w_xk, m_w_xk, v_w_xk),
                         (g_attn[2], w_xv, m_w_xv, v_w_xv),
                         (g_attn[3], xo_2d(w_xo), xo_2d(m_w_xo), xo_2d(v_w_xo))], "adamw_attn")
    for n, g, res in zip(("w_xq", "w_xk", "w_xv", "w_xo"), g_attn, attn):
        out[n] = (g, *res)
    r_out, r_small0, r_wpool = _split_wait(_all_to_all_copies(1), ex_out, attn[3][0], "exchange_out_wait")
    out["w_out"] = _sum_adamw(r_out, w_out[0], m_w_out[0], v_w_out[0], "adamw_out")
    r_in, r_small1 = _split_wait(_all_to_all_copies(1), ex_in, out["w_out"][1], "exchange_in_wait")
    in_t = _sum_adamw(r_in, w_in[0].T, m_w_in[0].T, v_w_in[0].T, "adamw_in")
    out["w_in"] = tuple(t.T for t in in_t)
    (r_small2,) = _split_wait(_all_to_all_copies(0), ex_mix, in_t[1], "exchange_mix_wait")
    row = lambda t: t.reshape(1, -1)
    small_params = {
        "norm_mix_g": (norm_mix_g, m_norm_mix_g, v_norm_mix_g),
        "lb_logits": (lb_logits, m_lb_logits, v_lb_logits),
        "hgrn_norm_g": (hgrn_norm_g[0], m_hgrn_norm_g[0], v_hgrn_norm_g[0]),
        "pool_scale": (pool_scale, m_pool_scale, v_pool_scale),
        "norm_x_g": (norm_x_g, m_norm_x_g, v_norm_x_g),
        "norm_mem_g": (norm_mem_g, m_norm_mem_g, v_norm_mem_g),
        "norm_ffn_g": (norm_ffn_g, m_norm_ffn_g, v_norm_ffn_g),
        "final_norm_g": (row(final_norm_g), row(m_final_norm_g), row(v_final_norm_g)),
        "w_pool": (wp, m_w_pool[0], v_w_pool[0]),
    }
    loss, small_out = _small_update([r_small0, r_small1, r_small2], r_wpool, small_params)
    out.update(small_out)

    shapes = dict(norm_mix_g=norm_mix_g, w_in=w_in, lb_logits=lb_logits, hgrn_norm_g=hgrn_norm_g, w_pool=w_pool,
                  pool_scale=pool_scale, w_out=w_out, norm_x_g=norm_x_g, norm_mem_g=norm_mem_g, w_xq=w_xq, w_xk=w_xk,
                  w_xv=w_xv, w_xo=w_xo, norm_ffn_g=norm_ffn_g, w_ff1=w_ff1, w_ff2=w_ff2, final_norm_g=final_norm_g)
    order = list(shapes)
    group = lambda k: [out[n][k].reshape(shapes[n].shape) for n in order]
    return (loss.reshape(()), grad_x.reshape(x.shape), *group(0), *group(1), *group(2), *group(3))
```

```python
import jax
import jax.numpy as jnp
from jax import lax
from jax.experimental import pallas as pl
from jax.experimental.pallas import tpu as pltpu

F32 = jnp.float32
BF16 = jnp.bfloat16

D_MODEL = 1024
N_DEV = 8
HEADS = 4
HD = 128
HW = HEADS * HD
IN_WIDTH = 5 * HW
XHD = 256
MEM_LEN = 256
D_FF = 4096
FF_BLK = D_FF // N_DEV
POOL_WINDOWS = (2, 4, 8, 16)
POOL_HALO = 16
CHUNK = 64
CHUNKS_PER_STEP = 8
SUB = 16
N_SUB = CHUNK // SUB
EXP_CAP = 80.0
EPS = 1e-6
TINY = 1e-30
ROW_TILE = 512
WIDE_ROW_TILE = 1024
SLOT = 8
V7X_VMEM_LIMIT = 56 * 1024 * 1024

ADAM_LR = 0.001
ADAM_B1 = 0.9
ADAM_B2 = 0.999
ADAM_EPS = 1e-08
ADAM_WD = 0.01
ADAM_STEP = 10

MESH_ID = pl.DeviceIdType.MESH


def _params(sem=None, vmem=V7X_VMEM_LIMIT):
    return pltpu.CompilerParams(dimension_semantics=sem, vmem_limit_bytes=vmem)


def _mm(a, b):
    return lax.dot_general(a.astype(BF16), b.astype(BF16), (((1,), (0,)), ((), ())), preferred_element_type=F32)


def _mm_nt(a, b):
    return lax.dot_general(a.astype(BF16), b.astype(BF16), (((1,), (1,)), ((), ())), preferred_element_type=F32)


def _mm_tn(a, b):
    return lax.dot_general(a.astype(BF16), b.astype(BF16), (((0,), (0,)), ((), ())), preferred_element_type=F32)


def _sigmoid(x):
    return 1.0 / (1.0 + jnp.exp(-x))


def _rms(x):
    return lax.rsqrt(jnp.mean(x * x, axis=-1, keepdims=True) + EPS)


def _rms_bwd(x, g, dh):
    r = _rms(x)
    n = x * r
    dn = dh * g
    dx = r * (dn - n * jnp.mean(dn * n, axis=-1, keepdims=True))
    return dx, jnp.sum(dh * n, axis=0, keepdims=True)


def _tri_dot(tri, x, passes):
    acc = None
    rest = x
    for _ in range(passes):
        piece = rest.astype(BF16)
        part = lax.dot_general(tri, piece, (((1,), (0,)), ((), ())), preferred_element_type=F32)
        acc = part if acc is None else acc + part
        rest = rest - piece.astype(F32)
    return acc


def _adam_update(g, w, m, v):
    nm = ADAM_B1 * m + (1.0 - ADAM_B1) * g
    nv = ADAM_B2 * v + (1.0 - ADAM_B2) * (g * g)
    m_hat = nm / (1.0 - ADAM_B1 ** ADAM_STEP)
    v_hat = nv / (1.0 - ADAM_B2 ** ADAM_STEP)
    return -ADAM_LR * (m_hat / (jnp.sqrt(v_hat) + ADAM_EPS) + ADAM_WD * w), nm, nv


def _full(shape):
    return pl.BlockSpec(shape, lambda *_: (0,) * len(shape))


VMEM_WHOLE = pl.BlockSpec(memory_space=pltpu.VMEM)
ANY_SPACE = pl.BlockSpec(memory_space=pl.ANY)


def _mesh_pos():
    return lax.axis_index("x"), lax.axis_index("y"), lax.axis_index("c")


def _flat(px, py, pc):
    return 4 * px + 2 * py + pc


def _all_gather_weights(shards, cast_only):
    n, nc = len(shards), len(cast_only)
    step = 64

    def body(*refs):
        x_refs, c_refs = refs[:n], refs[n:n + nc]
        out_refs, cast_refs = refs[n + nc:2 * n + nc], refs[2 * n + nc:2 * n + 2 * nc]
        bufs = refs[2 * n + 2 * nc:3 * n + 2 * nc]
        send_sems, recv_sems, local_sems = refs[3 * n + 2 * nc:]
        _handshake(_peers_first_level())
        x, y, c = _mesh_pos()
        me, sibling = (x, y, c), (x, y, 1 - c)
        chips = [(1 - x, y), (x, 1 - y), (1 - x, 1 - y)]

        def copy(a, k, blk, to, src=None):
            rows = out_refs[a].at[_flat(*blk)]
            return pltpu.make_async_remote_copy(
                src_ref=rows if src is None else src, dst_ref=rows,
                send_sem=send_sems.at[7 * a + k], recv_sem=recv_sems.at[7 * a + k], device_id=to, device_id_type=MESH_ID)

        def cast_rows(src, dst, rows):
            def cast(i, carry):
                r0 = pl.multiple_of(i * step, step)
                dst[pl.ds(r0, step), :] = src[pl.ds(r0, step), :].astype(BF16)
                return carry
            lax.fori_loop(0, rows // step, cast, 0)

        first, mine = [], []
        for a in range(n):
            cast_rows(x_refs[a], bufs[a], shards[a].shape[0])
            mine.append(pltpu.make_async_copy(bufs[a], out_refs[a].at[_flat(*me)], local_sems.at[a]))
            first.append(copy(a, 0, me, sibling, src=bufs[a]))
            first += [copy(a, 1 + j, me, (*chip, c), src=bufs[a]) for j, chip in enumerate(chips)]
            for cp in [mine[-1]] + first[-4:]:
                cp.start()
        for a in range(nc):
            cast_rows(c_refs[a], cast_refs[a], cast_only[a].shape[0])
        passed = []
        for j, chip in enumerate(chips):
            for a in range(n):
                copy(a, 1 + j, (*chip, c), me).wait_recv()
                passed.append(copy(a, 4 + j, (*chip, c), sibling))
                passed[-1].start()
        for a in range(n):
            copy(a, 0, sibling, me).wait_recv()
            for j, chip in enumerate(chips):
                copy(a, 4 + j, (*chip, 1 - c), me).wait_recv()
        for cp in first + passed:
            cp.wait_send()
        for cp in mine:
            cp.wait()

    return pl.pallas_call(
        body, name="all_gather_w_in",
        out_shape=[jax.ShapeDtypeStruct((N_DEV,) + s.shape, BF16) for s in shards]
        + [jax.ShapeDtypeStruct(s.shape, BF16) for s in cast_only],
        in_specs=[VMEM_WHOLE] * (n + nc), out_specs=[ANY_SPACE] * n + [VMEM_WHOLE] * nc,
        scratch_shapes=[pltpu.VMEM(s.shape, BF16) for s in shards]
        + [pltpu.SemaphoreType.DMA((7 * n,)), pltpu.SemaphoreType.DMA((7 * n,)), pltpu.SemaphoreType.DMA((n,))],
        compiler_params=pltpu.CompilerParams(vmem_limit_bytes=V7X_VMEM_LIMIT, collective_id=GATHER_W_IN_ID),
    )(*shards, *cast_only)


HBM_SPEC = pl.BlockSpec(memory_space=pltpu.HBM)
SEM_SPEC = pl.BlockSpec(memory_space=pltpu.SEMAPHORE)
EFFECT = pltpu.SideEffectType.DATAFLOW_SIDE_EFFECTING
TOKEN = jax.ShapeDtypeStruct((8, 128), F32)


def _in_hbm(a):
    return pltpu.with_memory_space_constraint(a, pltpu.HBM)


START_IDS = {name: i for i, name in enumerate((
    "gather_first_start", "gather_attn_forward_start", "gather_mlp_forward_start", "exchange_ff2_start",
    "exchange_ff1_start", "exchange_attn_start", "exchange_out_start", "exchange_in_start", "exchange_mix_start"))}


GATHER_W_IN_ID = len(START_IDS)


def _handshake(peers):
    barrier = pltpu.get_barrier_semaphore()
    for peer in peers:
        pl.semaphore_signal(barrier, inc=1, device_id=peer, device_id_type=MESH_ID)
    pl.semaphore_wait(barrier, len(peers))


def _peers_all():
    x, y, c = _mesh_pos()
    return [(1 - x if k & 4 else x, 1 - y if k & 2 else y, 1 - c if k & 1 else c) for k in range(1, N_DEV)]


def _peers_first_level():
    x, y, c = _mesh_pos()
    return [(x, y, 1 - c), (1 - x, y, c), (x, 1 - y, c), (1 - x, 1 - y, c)]


def _peers_sibling():
    x, y, c = _mesh_pos()
    return [(x, y, 1 - c)]


def _split_start(copies_of, srcs, lands, n_sems, name, peers_of, collective_id):
    ns, nl, k = len(srcs), len(lands), len(n_sems)

    def body(*refs):
        _handshake(peers_of())
        src_refs, land_refs = refs[:ns], refs[ns:ns + nl]
        sems = refs[ns + nl:ns + nl + k]
        token = refs[-1]
        for cp in copies_of(src_refs, land_refs, sems):
            cp.start()
        token[...] = jnp.zeros_like(token)

    outs = pl.pallas_call(
        body, name=name,
        out_shape=[pltpu.SemaphoreType.DMA((q,)) for q in n_sems]
        + [pltpu.HBM(a.shape, a.dtype) for a in list(srcs) + list(lands)] + [TOKEN],
        in_specs=[HBM_SPEC] * (ns + nl),
        out_specs=[SEM_SPEC] * k + [HBM_SPEC] * (ns + nl) + [VMEM_WHOLE],
        input_output_aliases={i: k + i for i in range(ns + nl)},
        compiler_params=pltpu.CompilerParams(has_side_effects=EFFECT, collective_id=collective_id),
    )(*[_in_hbm(a) for a in list(srcs) + list(lands)])
    return outs[:k], outs[k:k + ns], outs[k + ns:k + ns + nl], outs[-1]


def _split_wait(copies_of, handle, after, name):
    sems, srcs, lands, _ = handle
    ns, nl, k = len(srcs), len(lands), len(sems)

    def body(*refs):
        src_refs, land_refs = refs[:ns], refs[ns:ns + nl]
        sem_refs = refs[ns + nl:ns + nl + k]
        for cp in copies_of(src_refs, land_refs, sem_refs):
            cp.wait()

    outs = pl.pallas_call(
        body, name=name,
        out_shape=[pltpu.HBM(a.shape, a.dtype) for a in list(srcs) + list(lands)],
        in_specs=[HBM_SPEC] * (ns + nl) + [SEM_SPEC] * k + [ANY_SPACE],
        out_specs=[HBM_SPEC] * (ns + nl),
        input_output_aliases={i: i for i in range(ns + nl)},
        compiler_params=pltpu.CompilerParams(has_side_effects=EFFECT),
    )(*srcs, *lands, *sems, after)
    return outs[ns:]


def _gather_first_copies(shard_refs, land_refs, sems):
    send_sems, recv_sems, local_sems = sems
    x, y, c = _mesh_pos()
    me = _flat(x, y, c)
    peers = [(x, y, 1 - c), (1 - x, y, c), (x, 1 - y, c), (1 - x, 1 - y, c)]
    copies = []
    for a, (shard, land) in enumerate(zip(shard_refs, land_refs)):
        copies.append(pltpu.make_async_copy(shard, land.at[me], local_sems.at[a]))
        for k, peer in enumerate(peers):
            copies.append(pltpu.make_async_remote_copy(
                src_ref=shard, dst_ref=land.at[me], send_sem=send_sems.at[4 * a + k], recv_sem=recv_sems.at[4 * a + k],
                device_id=peer, device_id_type=MESH_ID))
    return copies


def _gather_forward_copies(src_refs, land_refs, sems):
    del src_refs
    send_sems, recv_sems = sems
    x, y, c = _mesh_pos()
    chips = [(1 - x, y), (x, 1 - y), (1 - x, 1 - y)]
    copies = []
    for a, land in enumerate(land_refs):
        for j, chip in enumerate(chips):
            rows = land.at[_flat(*chip, c)]
            copies.append(pltpu.make_async_remote_copy(
                src_ref=rows, dst_ref=rows, send_sem=send_sems.at[3 * a + j], recv_sem=recv_sems.at[3 * a + j],
                device_id=(x, y, 1 - c), device_id_type=MESH_ID))
    return copies


def _gather_first_start(groups, name):
    shards = [s for g in groups for s in g]
    lands = [lax.empty((N_DEV,) + s.shape, s.dtype) for s in shards]
    bounds = [sum(len(g) for g in groups[:i]) for i in range(len(groups) + 1)]

    def copies_of(src_refs, land_refs, sems):
        copies = []
        for i in range(len(groups)):
            lo, hi = bounds[i], bounds[i + 1]
            copies += _gather_first_copies(src_refs[lo:hi], land_refs[lo:hi], sems[3 * i:3 * i + 3])
        return copies

    n_sems = tuple(q for g in groups for q in (4 * len(g), 4 * len(g), len(g)))
    sems, srcs, lands, token = _split_start(copies_of, shards, lands, n_sems, name, _peers_first_level, START_IDS[name])
    return [(sems[3 * i:3 * i + 3], srcs[bounds[i]:bounds[i + 1]], lands[bounds[i]:bounds[i + 1]], token)
            for i in range(len(groups))]


def _gather_forward_start(lands, name):
    n = len(lands)
    return _split_start(_gather_forward_copies, [], lands, (3 * n, 3 * n), name, _peers_sibling, START_IDS[name])


def _all_to_all_copies(n_scattered):
    def copies_of(src_refs, land_refs, sems):
        send_sems, recv_sems, local_sems = sems
        x, y, c = _mesh_pos()
        me = _flat(x, y, c)
        copies = []
        for a, (src, land) in enumerate(zip(src_refs, land_refs)):
            scattered = a < n_scattered
            copies.append(pltpu.make_async_copy(src.at[me] if scattered else src, land.at[me], local_sems.at[a]))
            for k in range(1, N_DEV):
                peer = (1 - x if k & 4 else x, 1 - y if k & 2 else y, 1 - c if k & 1 else c)
                copies.append(pltpu.make_async_remote_copy(
                    src_ref=src.at[_flat(*peer)] if scattered else src, dst_ref=land.at[me],
                    send_sem=send_sems.at[7 * a + k - 1], recv_sem=recv_sems.at[7 * a + k - 1],
                    device_id=peer, device_id_type=MESH_ID))
        return copies
    return copies_of


def _all_to_all_start(scattered, broadcast, name):
    srcs = list(scattered) + list(broadcast)
    lands = [lax.empty(a.shape, a.dtype) for a in scattered] + [lax.empty((N_DEV,) + a.shape, a.dtype) for a in broadcast]
    n = len(srcs)
    return _split_start(_all_to_all_copies(len(scattered)), srcs, lands, (7 * n, 7 * n, n), name, _peers_all,
                        START_IDS[name])


def _call_behind(deps, body, *, in_specs, **kwargs):
    n_in, n_dep = len(in_specs), len(deps)

    def body_without_deps(*refs):
        return body(*refs[:n_in], *refs[n_in + n_dep:])

    call = pl.pallas_call(body_without_deps, in_specs=list(in_specs) + [ANY_SPACE] * n_dep, **kwargs)
    return lambda *operands: call(*operands, *deps)


def _row_tile(rows):
    for cand in (256, 128, 64, 32, 16):
        if rows % cand == 0:
            return cand
    return rows


def _adamw_whole(groups, name):
    n = len(groups)

    def body(*refs):
        for i in range(n):
            g_ref, w_ref, m_ref, v_ref = refs[4 * i:4 * i + 4]
            d_ref, nm_ref, nv_ref = refs[4 * n + 3 * i:4 * n + 3 * i + 3]
            d_ref[...], nm_ref[...], nv_ref[...] = _adam_update(g_ref[...], w_ref[...], m_ref[...], v_ref[...])

    outs = pl.pallas_call(
        body, name=name, out_shape=[jax.ShapeDtypeStruct(grp[0].shape, F32) for grp in groups for _ in range(3)],
        in_specs=[VMEM_WHOLE] * (4 * n), out_specs=[VMEM_WHOLE] * (3 * n),
        compiler_params=_params(),
    )(*[t for grp in groups for t in grp])
    return [outs[3 * i:3 * i + 3] for i in range(n)]


def _sum_sources_whole(recvs, name):
    n = len(recvs)

    def body(*refs):
        for r_ref, o_ref in zip(refs[:n], refs[n:]):
            acc = r_ref[0].astype(F32)
            for d in range(1, N_DEV):
                acc = acc + r_ref[d].astype(F32)
            o_ref[...] = acc

    return pl.pallas_call(
        body, name=name, out_shape=[jax.ShapeDtypeStruct(r.shape[1:], F32) for r in recvs],
        in_specs=[VMEM_WHOLE] * n, out_specs=[VMEM_WHOLE] * n,
        compiler_params=_params(),
    )(*recvs)


def _sum_adamw(recv, w, m, v, name):
    _, rows, cols = recv.shape
    tile = _row_tile(rows)

    def body(r_ref, w_ref, m_ref, v_ref, g_ref, d_ref, nm_ref, nv_ref):
        acc = r_ref[0].astype(F32)
        for d in range(1, N_DEV):
            acc = acc + r_ref[d].astype(F32)
        g_ref[...] = acc
        d_ref[...], nm_ref[...], nv_ref[...] = _adam_update(acc, w_ref[...], m_ref[...], v_ref[...])

    spec = pl.BlockSpec((tile, cols), lambda i: (i, 0))
    shp = jax.ShapeDtypeStruct((rows, cols), F32)
    return pl.pallas_call(
        body, name=name, grid=(rows // tile,), out_shape=[shp] * 4,
        in_specs=[pl.BlockSpec((N_DEV, tile, cols), lambda i: (0, i, 0)), spec, spec, spec], out_specs=[spec] * 4,
        compiler_params=_params(("parallel",)),
    )(recv, w, m, v)


SMALL_SLOTS = {"norm_x_g": (0, 0, 1, D_MODEL), "norm_mem_g": (0, 8, 1, D_MODEL), "norm_ffn_g": (0, 16, 1, D_MODEL),
               "final_norm_g": (0, 24, 1, D_MODEL), "pool_scale": (0, 32, 1, HW),
               "lb_logits": (1, 0, 2, HW), "hgrn_norm_g": (1, 8, HEADS, HD), "norm_mix_g": (2, 0, 1, D_MODEL)}
LOSS_ROW = 25
SMALL_ORDER = ("norm_mix_g", "lb_logits", "hgrn_norm_g", "pool_scale", "norm_x_g", "norm_mem_g", "norm_ffn_g",
               "final_norm_g", "w_pool")


def _small_update(srecvs, wprecv, params):
    flat = [t for n in SMALL_ORDER for t in params[n]]
    nb = len(srecvs)
    n_in = nb + 1 + len(flat)

    def body(*refs):
        s_refs, wp_ref = refs[0:nb], refs[nb]
        in_refs = refs[nb + 1:n_in]
        loss_ref = refs[n_in]
        out_refs = refs[n_in + 1:-nb]
        accs = refs[-nb:]
        for s_ref, acc in zip(s_refs, accs):
            total = s_ref[0]
            for d in range(1, N_DEV):
                total = total + s_ref[d]
            acc[...] = total
        loss_ref[...] = accs[0][LOSS_ROW:LOSS_ROW + 1, 0:1]
        for i, name in enumerate(SMALL_ORDER):
            w_ref, m_ref, v_ref = in_refs[3 * i:3 * i + 3]
            g_ref, d_ref, nm_ref, nv_ref = out_refs[4 * i:4 * i + 4]
            if name == "w_pool":
                g = wp_ref[0]
                for d in range(1, N_DEV):
                    g = g + wp_ref[d]
            else:
                buf, r0, nr, nc = SMALL_SLOTS[name]
                g = accs[buf][r0:r0 + nr, 0:nc]
            g_ref[...] = g
            d_ref[...], nm_ref[...], nv_ref[...] = _adam_update(g, w_ref[...], m_ref[...], v_ref[...])

    out_shape = [jax.ShapeDtypeStruct((1, 1), F32)]
    for n in SMALL_ORDER:
        out_shape += [jax.ShapeDtypeStruct(params[n][0].shape, F32)] * 4
    outs = pl.pallas_call(
        body, name="small_update", out_shape=out_shape,
        in_specs=[VMEM_WHOLE] * n_in, out_specs=[VMEM_WHOLE] * len(out_shape),
        scratch_shapes=[pltpu.VMEM(r.shape[1:], F32) for r in srecvs],
        compiler_params=_params(),
    )(*srecvs, wprecv, *flat)
    return outs[0], {n: outs[1 + 4 * i:5 + 4 * i] for i, n in enumerate(SMALL_ORDER)}


def _in_proj(x, g, w_t, deps):
    s = x.shape[0]
    tm = min(ROW_TILE, s)

    def body(x_ref, g_ref, w_ref, z_ref, h_ref):
        xv = x_ref[...]
        h = (xv * _rms(xv) * g_ref[...]).astype(BF16)
        h_ref[...] = h
        z_ref[...] = _mm_nt(h, w_ref[...])

    return _call_behind(
        deps, body, name="in_proj", grid=(s // tm,),
        out_shape=[jax.ShapeDtypeStruct((s, IN_WIDTH), F32), jax.ShapeDtypeStruct((s, D_MODEL), BF16)],
        in_specs=[pl.BlockSpec((tm, D_MODEL), lambda i: (i, 0)), _full((1, D_MODEL)), VMEM_WHOLE],
        out_specs=[pl.BlockSpec((tm, IN_WIDTH), lambda i: (i, 0)), pl.BlockSpec((tm, D_MODEL), lambda i: (i, 0))],
        compiler_params=_params(("parallel",)),
    )(x, g, w_t)


def _chunk_masks():
    row = lax.broadcasted_iota(jnp.int32, (CHUNK, CHUNK), 0)
    col = lax.broadcasted_iota(jnp.int32, (CHUNK, CHUNK), 1)
    return row, col


def _ones_where(mask):
    return jnp.where(mask, 1.0, 0.0).astype(BF16)


def _hgrn_gates(zq, zf, lb):
    sq = _sigmoid(zq)
    sig = _sigmoid(zf)
    f = lb + (1.0 - lb) * sig
    return zq * sq, sq, sig, f


def _sub_chunk_masks(width):
    trow = lax.broadcasted_iota(jnp.int32, (CHUNK, width), 0)
    return [(trow >= SUB * j) & (trow < SUB * (j + 1)) for j in range(N_SUB)]


def _head(a, h):
    return a[:, HD * h:HD * (h + 1)]


def _lanes(parts):
    return jnp.concatenate(parts, axis=1)


def _hgrn_decay_factors(b_scr, r0, b, in_sub):
    bases = [jnp.zeros((1, HW), F32)] + [b_scr[r0 + SUB * j - 1:r0 + SUB * j, :] for j in range(1, N_SUB)]
    own_base = bases[N_SUB - 1]
    for j in range(N_SUB - 2, -1, -1):
        own_base = jnp.where(in_sub[j], bases[j], own_base)
    eq = jnp.exp(b - own_base)
    ek = []
    for j in range(N_SUB):
        upto = SUB * (j + 1)
        e = jnp.exp(jnp.minimum(bases[j] - b[0:upto], EXP_CAP))
        ek.append(e if upto == CHUNK else jnp.concatenate([e, jnp.zeros((CHUNK - upto, HW), F32)], axis=0))
    return eq, ek


def _per_sub_chunk(x, in_sub):
    return _lanes([jnp.where(in_sub[j], x, 0.0) for j in range(N_SUB)])


def _own_lane_block(a, in_sub):
    out = a[:, HD * (N_SUB - 1):HD * N_SUB]
    for j in range(N_SUB - 2, -1, -1):
        out = jnp.where(in_sub[j], a[:, HD * j:HD * (j + 1)], out)
    return out


def _head_rms(o):
    return _lanes([jnp.broadcast_to(_rms(_head(o, h)), (CHUNK, HD)) for h in range(HEADS)])


def _head_mean(a):
    return _lanes([jnp.broadcast_to(jnp.mean(_head(a, h), axis=-1, keepdims=True), (CHUNK, HD)) for h in range(HEADS)])


def _hgrn_fwd(z, lb_logits, gn):
    s = z.shape[0]
    n_chunks = s // CHUNK

    def body(zq_ref, zf_ref, zi_ref, zg_ref, lbl_ref, gn_ref, oa_ref, o_ref, st_ref, state, b_scr):
        @pl.when(pl.program_id(0) == 0)
        def _():
            state[...] = jnp.zeros_like(state)

        lb = _sigmoid(lbl_ref[0:1, :] - lbl_ref[1:2, :])
        row, col = _chunk_masks()
        causal = col <= row
        tri = _ones_where(causal)
        in_sub, in_sub_head = _sub_chunk_masks(HW), _sub_chunk_masks(HD)
        gn_row = _lanes([gn_ref[h:h + 1, :] for h in range(HEADS)])
        def front(c):
            r0 = CHUNK * c
            rs = slice(r0, r0 + CHUNK)
            q, _, _, f = _hgrn_gates(zq_ref[rs, :], zf_ref[rs, :], lb)
            kk = 1.0 - f
            b = _tri_dot(tri, jnp.log(f), 3)
            b_scr[rs, :] = b
            eq, ek = _hgrn_decay_factors(b_scr, r0, b, in_sub)
            b_last = b_scr[r0 + CHUNK - 1:r0 + CHUNK, :]
            qe = q * eq
            return {"rs": rs, "v": zi_ref[rs, :], "qg": q * jnp.exp(b), "kd": kk * jnp.exp(b_last - b),
                    "lam_last": jnp.exp(b_last),
                    "q16": [_per_sub_chunk(_head(qe, h), in_sub_head).astype(BF16) for h in range(HEADS)],
                    "ke16": [_lanes([_head(kk * e, h) for e in ek]).astype(BF16) for h in range(HEADS)]}

        def recurrence(c, p):
            st_ref[c] = state[...]
            a, o_inter = [], []
            for h in range(HEADS):
                vh, st = _head(p["v"], h), state[h]
                a.append(jnp.where(causal, _mm_nt(p["q16"][h], p["ke16"][h]), 0.0))
                o_inter.append(_mm_nt(_head(p["qg"], h), st))
                state[h] = st * _head(p["lam_last"], h) + _mm_tn(vh, _head(p["kd"], h))
            return _lanes([_mm(a[h], _head(p["v"], h)) + o_inter[h] for h in range(HEADS)])

        def back(p, o):
            rs = p["rs"]
            o_ref[rs, :] = o
            zg = zg_ref[rs, :]
            oa_ref[rs, :] = (o * _head_rms(o) * gn_row * zg * _sigmoid(zg)).astype(BF16)

        p = front(0)
        for c in range(CHUNKS_PER_STEP):
            o = recurrence(c, p)
            p_next = front(c + 1) if c + 1 < CHUNKS_PER_STEP else None
            back(p, o)
            p = p_next

    rows = CHUNK * CHUNKS_PER_STEP
    zspec = lambda cb: pl.BlockSpec((rows, HW), lambda i, cb=cb: (i, cb))
    return pl.pallas_call(
        body, name="hgrn_fwd", grid=(s // rows,),
        out_shape=[jax.ShapeDtypeStruct((s, 2 * HW), BF16), jax.ShapeDtypeStruct((s, HW), F32),
                   jax.ShapeDtypeStruct((n_chunks, HEADS, HD, HD), F32)],
        in_specs=[zspec(0), zspec(1), zspec(2), zspec(3), _full((2, HW)), _full((HEADS, HD))],
        out_specs=[pl.BlockSpec((rows, HW), lambda i: (i, 0)), pl.BlockSpec((rows, HW), lambda i: (i, 0)),
                   pl.BlockSpec((CHUNKS_PER_STEP, HEADS, HD, HD), lambda i: (i, 0, 0, 0))],
        scratch_shapes=[pltpu.VMEM((HEADS, HD, HD), F32), pltpu.VMEM((rows, HW), F32)],
        compiler_params=_params(("arbitrary",)),
    )(z, z, z, z, lb_logits, gn)


def _pool_counts(tile_idx, tm):
    t = tile_idx * tm + lax.broadcasted_iota(jnp.int32, (tm, 1), 0)
    return [1.0 / jnp.minimum(t + 1, w).astype(F32) for w in POOL_WINDOWS]


def _pool_fwd(z, w_pool, scale, mixed_in, deps):
    s = z.shape[0]
    tm = min(ROW_TILE, s)

    def body(p_ref, w_ref, sc_ref, mixin_ref, ob_ref, pooled_ref, ext):
        i = pl.program_id(0)

        @pl.when(i == 0)
        def _():
            ext[0:POOL_HALO, :] = jnp.zeros((POOL_HALO, HW), F32)

        @pl.when(i > 0)
        def _():
            ext[0:POOL_HALO, :] = ext[tm:tm + POOL_HALO, :]

        ext[POOL_HALO:POOL_HALO + tm, :] = p_ref[...]
        inv = _pool_counts(i, tm)
        for g, w in enumerate(POOL_WINDOWS):
            sl = slice(HD * g, HD * (g + 1))
            p = ext[POOL_HALO:POOL_HALO + tm, sl]
            win = p
            for d in range(1, w):
                win = win + ext[POOL_HALO - d:POOL_HALO - d + tm, sl]
            pooled = (win * inv[g] - p).astype(BF16)
            pooled_ref[:, sl] = pooled
            ob_ref[:, sl] = (_mm(pooled, w_ref[g]) * sc_ref[:, sl]).astype(BF16)

    return _call_behind(
        deps, body, name="pool_fwd", grid=(s // tm,),
        out_shape=[jax.ShapeDtypeStruct((s, 2 * HW), BF16), jax.ShapeDtypeStruct((s, HW), BF16)],
        in_specs=[pl.BlockSpec((tm, HW), lambda i: (i, 4)), _full((HEADS, HD, HD)), _full((1, HW)), ANY_SPACE],
        out_specs=[pl.BlockSpec((tm, HW), lambda i: (i, 1)), pl.BlockSpec((tm, HW), lambda i: (i, 0))],
        scratch_shapes=[pltpu.VMEM((tm + POOL_HALO, HW), F32)],
        input_output_aliases={3: 0},
        compiler_params=_params(("arbitrary",)),
    )(z, w_pool, scale, mixed_in)


def _mem_kv(mem, g, wk, wv, deps):
    def body(m_ref, g_ref, wk_ref, wv_ref, hm_ref, k_ref, v_ref):
        m = m_ref[...]
        hm = (m * _rms(m) * g_ref[...]).astype(BF16)
        hm_ref[...] = hm
        k_ref[...] = _mm(hm, wk_ref[...]).astype(BF16)
        v_ref[...] = _mm(hm, wv_ref[...]).astype(BF16)

    shp = jax.ShapeDtypeStruct((MEM_LEN, D_MODEL), BF16)
    return _call_behind(
        deps, body, name="mem_kv", out_shape=[shp, shp, shp],
        in_specs=[VMEM_WHOLE] * 4, out_specs=[VMEM_WHOLE] * 3,
        compiler_params=_params(),
    )(mem, g, wk, wv)


def _softmax_rows(sc):
    e = jnp.exp(sc - jnp.max(sc, axis=-1, keepdims=True))
    return e / jnp.sum(e, axis=-1, keepdims=True)


def _mix_xattn_fwd(x0, mixed, w_out, g, wq, xk, xv, wo_t, deps):
    s = x0.shape[0]
    tm = min(ROW_TILE, s)
    scale = XHD ** -0.5

    def body(x_ref, mix_ref, wout_ref, g_ref, wq_ref, k_ref, v_ref, wo_ref, x1_ref, o_ref, hq_ref, q_ref, att_ref):
        xv_ = x_ref[...] + _mm(mix_ref[...], wout_ref[...])
        x1_ref[...] = xv_
        hq = (xv_ * _rms(xv_) * g_ref[...]).astype(BF16)
        hq_ref[...] = hq
        q_ref[...] = (_mm(hq, wq_ref[...]) * scale).astype(BF16)
        heads = [slice(XHD * h, XHD * (h + 1)) for h in range(HEADS)]
        scores = [_mm_nt(q_ref[:, sl], k_ref[:, sl]) for sl in heads]
        probs = [_softmax_rows(sc) for sc in scores]
        for sl, p in zip(heads, probs):
            att_ref[:, sl] = _mm(p, v_ref[:, sl]).astype(BF16)
        o_ref[...] = xv_ + _mm_nt(att_ref[...], wo_ref[...])

    row_f32 = pl.BlockSpec((tm, D_MODEL), lambda i: (i, 0))
    bshape = jax.ShapeDtypeStruct((s, D_MODEL), BF16)
    fshape = jax.ShapeDtypeStruct((s, D_MODEL), F32)
    return _call_behind(
        deps, body, name="mix_xattn_fwd", grid=(s // tm,),
        out_shape=[fshape, fshape, bshape, bshape, bshape],
        in_specs=[row_f32, row_f32, VMEM_WHOLE, _full((1, D_MODEL)), VMEM_WHOLE, VMEM_WHOLE, VMEM_WHOLE, VMEM_WHOLE],
        out_specs=[row_f32] * 5,
        compiler_params=_params(("parallel",)),
    )(x0, mixed, w_out, g, wq, xk, xv, wo_t)


def _mlp_fwd_loss(x, g, w1, w2, gf, target):
    s = x.shape[0]
    tm = min(ROW_TILE, s)

    def body(x_ref, g_ref, w1_ref, w2_ref, gf_ref, t_ref, dx_ref, dx16_ref, u_ref, hf_ref, slot_ref):
        @pl.when(pl.program_id(0) == 0)
        def _():
            slot_ref[...] = jnp.zeros_like(slot_ref)

        xv = x_ref[...]
        hf = (xv * _rms(xv) * g_ref[...]).astype(BF16)
        hf_ref[...] = hf
        a_next = _mm(hf, w1_ref[0])
        for j in range(N_DEV):
            a = jnp.maximum(a_next, 0.0)
            if j + 1 < N_DEV:
                a_next = _mm(hf, w1_ref[j + 1])
            u_ref[:, FF_BLK * j:FF_BLK * (j + 1)] = (a * a).astype(BF16)
        acc = xv + _mm(u_ref[...], w2_ref[...])
        gfv = gf_ref[...]
        r = _rms(acc)
        n = acc * r
        err = n * gfv - t_ref[...]
        slot_ref[1:2, :] += jnp.sum(jnp.mean(err * err, axis=-1, keepdims=True), axis=0, keepdims=True) * 0.5
        dy = err * (1.0 / D_MODEL)
        slot_ref[0:1, :] += jnp.sum(dy * n, axis=0, keepdims=True)
        dn = dy * gfv
        dx = r * (dn - n * jnp.mean(dn * n, axis=-1, keepdims=True))
        dx_ref[...] = dx
        dx16_ref[...] = dx.astype(BF16)

    row_f32 = pl.BlockSpec((tm, D_MODEL), lambda i: (i, 0))
    return pl.pallas_call(
        body, name="mlp_fwd_loss", grid=(s // tm,),
        out_shape=[jax.ShapeDtypeStruct((s, D_MODEL), F32), jax.ShapeDtypeStruct((s, D_MODEL), BF16),
                   jax.ShapeDtypeStruct((s, D_FF), BF16), jax.ShapeDtypeStruct((s, D_MODEL), BF16),
                   jax.ShapeDtypeStruct((SLOT, D_MODEL), F32)],
        in_specs=[row_f32, _full((1, D_MODEL)), VMEM_WHOLE, VMEM_WHOLE, _full((1, D_MODEL)), row_f32],
        out_specs=[row_f32, row_f32, pl.BlockSpec((tm, D_FF), lambda i: (i, 0)), row_f32, _full((SLOT, D_MODEL))],
        compiler_params=_params(("arbitrary",)),
    )(x, g, w1, w2, gf, target)


def _zero_slot(slot_ref):
    @pl.when(pl.program_id(0) == 0)
    def _():
        slot_ref[...] = jnp.zeros_like(slot_ref)


def _mlp_bwd(dx3, u, x2, g, w1, w2, deps):
    s = x2.shape[0]
    tm = min(ROW_TILE // 2, s)

    def body(d_ref, u_ref, x_ref, g_ref, w1_ref, w2_ref, da_ref, dx_ref, slot_ref):
        _zero_slot(slot_ref)
        d = d_ref[...]
        d16 = d.astype(BF16)
        du_next = _mm_nt(d16, w2_ref[0])
        dhf = jnp.zeros((tm, D_MODEL), F32)
        for j in range(N_DEV):
            sl = slice(FF_BLK * j, FF_BLK * (j + 1))
            du = du_next
            if j + 1 < N_DEV:
                du_next = _mm_nt(d16, w2_ref[j + 1])
            u = u_ref[:, sl].astype(F32)
            da = (du * (2.0 * u * lax.rsqrt(jnp.maximum(u, TINY)))).astype(BF16)
            da_ref[:, sl] = da
            dhf = dhf + _mm_nt(da, w1_ref[j])
        dx, dg = _rms_bwd(x_ref[...], g_ref[...], dhf)
        dx_ref[...] = d + dx
        slot_ref[0:1, :] += dg

    row_f32 = pl.BlockSpec((tm, D_MODEL), lambda i: (i, 0))
    return _call_behind(
        deps, body, name="mlp_bwd", grid=(s // tm,),
        out_shape=[jax.ShapeDtypeStruct((s, D_FF), BF16), jax.ShapeDtypeStruct((s, D_MODEL), F32),
                   jax.ShapeDtypeStruct((SLOT, D_MODEL), F32)],
        in_specs=[row_f32, pl.BlockSpec((tm, D_FF), lambda i: (i, 0)), row_f32, _full((1, D_MODEL)),
                  VMEM_WHOLE, VMEM_WHOLE],
        out_specs=[pl.BlockSpec((tm, D_FF), lambda i: (i, 0)), row_f32, _full((SLOT, D_MODEL))],
        compiler_params=_params(("arbitrary",)),
    )(dx3, u, x2, g, w1, w2)


def _wgrad(a, b, name, col_blocks=False, update=None):
    s, m = a.shape
    n = b.shape[1]
    tm = 1280 if m % 1280 == 0 else min(1024, m)
    tn = min(1024, n)
    blk = n // N_DEV
    per_step = tn // blk if col_blocks else 1
    ts = min((4 if m * n >= D_MODEL * D_FF else 2) * ROW_TILE, s)
    n_s = s // ts
    grid = (m // tm, n // tn, n_s)

    def body(a_ref, b_ref, *rest):
        o_ref, acc = rest[-2], rest[-1]
        k = pl.program_id(2)

        @pl.when(k == 0)
        def _():
            acc[...] = jnp.zeros_like(acc)

        acc[...] += _mm_tn(a_ref[...], b_ref[...])
        if update is not None:
            r_ref, w_ref, m_ref, v_ref, g_ref, d_ref, nm_ref, nv_ref = rest[:8]
            g = r_ref[0].astype(F32)
            for d in range(1, N_DEV):
                g = g + r_ref[d].astype(F32)
            g_ref[...] = g
            d_ref[...], nm_ref[...], nv_ref[...] = _adam_update(g, w_ref[...], m_ref[...], v_ref[...])

        @pl.when(k == n_s - 1)
        def _():
            if col_blocks:
                for p in range(per_step):
                    o_ref[p] = acc[:, blk * p:blk * (p + 1)].astype(BF16)
            else:
                o_ref[...] = acc[...].astype(BF16)

    if col_blocks:
        out_shape = jax.ShapeDtypeStruct((N_DEV, m, blk), BF16)
        out_spec = pl.BlockSpec((per_step, tm, blk), lambda i, j, k: (j, i, 0))
    else:
        out_shape = jax.ShapeDtypeStruct((m, n), BF16)
        out_spec = pl.BlockSpec((tm, tn), lambda i, j, k: (i, j))
    in_specs = [pl.BlockSpec((ts, tm), lambda i, j, k: (k, i)), pl.BlockSpec((ts, tn), lambda i, j, k: (k, j))]
    out_shapes, out_specs, operands = [out_shape], [out_spec], [a, b]
    if update is not None:
        rows, cols = update[1].shape
        steps = grid[0] * grid[1] * grid[2]
        tr = rows // steps
        step = lambda i, j, k: (i * grid[1] + j) * grid[2] + k
        piece = pl.BlockSpec((tr, cols), lambda i, j, k: (step(i, j, k), 0))
        in_specs += [pl.BlockSpec((N_DEV, tr, cols), lambda i, j, k: (0, step(i, j, k), 0)), piece, piece, piece]
        out_shapes = [jax.ShapeDtypeStruct((rows, cols), F32)] * 4 + out_shapes
        out_specs = [piece] * 4 + out_specs
        operands += list(update)
    outs = pl.pallas_call(
        body, name=name, grid=grid, out_shape=out_shapes, in_specs=in_specs, out_specs=out_specs,
        scratch_shapes=[pltpu.VMEM((tm, tn), F32)],
        compiler_params=_params(("parallel", "parallel", "arbitrary")),
    )(*operands)
    return outs[0] if update is None else (outs[4], tuple(outs[:4]))


def _xattn_bwd(dx2, x1, g, q, xk, xv, wq, wo_t, deps):
    s = x1.shape[0]
    tm = min(ROW_TILE, s)
    scale = XHD ** -0.5

    def body(d_ref, x_ref, g_ref, q_ref, k_ref, v_ref, wq_ref, wo_ref, dx_ref, dx16_ref, dq_ref, dk_ref, dv_ref, slot_ref,
             datt):
        _zero_slot(slot_ref)

        @pl.when(pl.program_id(0) == 0)
        def _():
            dk_ref[...] = jnp.zeros_like(dk_ref)
            dv_ref[...] = jnp.zeros_like(dv_ref)

        d = d_ref[...]
        datt[...] = _mm(d, wo_ref[...]).astype(BF16)
        heads = [slice(XHD * h, XHD * (h + 1)) for h in range(HEADS)]
        scores = [_mm_nt(q_ref[:, sl], k_ref[:, sl]) for sl in heads]
        dps = [_mm_nt(datt[:, sl], v_ref[:, sl]) for sl in heads]
        probs = [_softmax_rows(sc) for sc in scores]
        dss = [(p * (dp - jnp.sum(dp * p, axis=-1, keepdims=True))).astype(BF16) for p, dp in zip(probs, dps)]
        for sl, p, ds in zip(heads, probs, dss):
            dq_ref[:, sl] = (_mm(ds, k_ref[:, sl]) * scale).astype(BF16)
            dk_ref[:, sl] += _mm_tn(ds, q_ref[:, sl])
            dv_ref[:, sl] += _mm_tn(p, datt[:, sl])
        dx, dg = _rms_bwd(x_ref[...], g_ref[...], _mm_nt(dq_ref[...], wq_ref[...]))
        dx_ref[...] = d + dx
        dx16_ref[...] = (d + dx).astype(BF16)
        slot_ref[0:1, :] += dg

    row_f32 = pl.BlockSpec((tm, D_MODEL), lambda i: (i, 0))
    kv = jax.ShapeDtypeStruct((MEM_LEN, D_MODEL), F32)
    tokens16 = jax.ShapeDtypeStruct((s, D_MODEL), BF16)
    return _call_behind(
        deps, body, name="xattn_bwd", grid=(s // tm,),
        out_shape=[jax.ShapeDtypeStruct((s, D_MODEL), F32), tokens16, tokens16, kv, kv,
                   jax.ShapeDtypeStruct((SLOT, D_MODEL), F32)],
        in_specs=[row_f32, row_f32, _full((1, D_MODEL)), row_f32, VMEM_WHOLE, VMEM_WHOLE, VMEM_WHOLE, VMEM_WHOLE],
        out_specs=[row_f32, row_f32, row_f32, _full((MEM_LEN, D_MODEL)), _full((MEM_LEN, D_MODEL)),
                   _full((SLOT, D_MODEL))],
        scratch_shapes=[pltpu.VMEM((tm, D_MODEL), BF16)],
        compiler_params=_params(("arbitrary",)),
    )(dx2, x1, g, q, xk, xv, wq, wo_t)


def _mem_bwd(mem, g, hm, dxk, dxv, wk, wv):
    def body(m_ref, g_ref, hm_ref, dk_ref, dv_ref, wk_ref, wv_ref, dwk_ref, dwv_ref, slot_ref):
        dk, dv = dk_ref[...], dv_ref[...]
        hm_ = hm_ref[...]
        dwk_ref[...] = _mm_tn(hm_, dk).astype(BF16)
        dwv_ref[...] = _mm_tn(hm_, dv).astype(BF16)
        _, dg = _rms_bwd(m_ref[...], g_ref[...], _mm_nt(dk, wk_ref[...]) + _mm_nt(dv, wv_ref[...]))
        slot_ref[...] = jnp.zeros_like(slot_ref)
        slot_ref[0:1, :] = dg

    wshape = jax.ShapeDtypeStruct((D_MODEL, D_MODEL), BF16)
    return pl.pallas_call(
        body, name="mem_bwd", out_shape=[wshape, wshape, jax.ShapeDtypeStruct((SLOT, D_MODEL), F32)],
        in_specs=[VMEM_WHOLE] * 7, out_specs=[VMEM_WHOLE] * 3,
        compiler_params=_params(),
    )(mem, g, hm, dxk, dxv, wk, wv)


def _pool_bwd(dx1, w_out, pooled, w_pool, scale, deps):
    s = dx1.shape[0]
    tm = min(ROW_TILE, s)
    n_t = s // tm

    def body(dx_ref, wo_ref, pl_ref, w_ref, sc_ref, dz_ref, dw_ref, slot_ref, ext, do_ref):
        i = pl.program_id(0)
        tile = n_t - 1 - i
        _zero_slot(slot_ref)
        do_ref[...] = _mm_nt(dx_ref[...], wo_ref[HW:2 * HW, :])

        @pl.when(i == 0)
        def _():
            dw_ref[...] = jnp.zeros_like(dw_ref)
            ext[tm:tm + POOL_HALO, :] = jnp.zeros((POOL_HALO, HW), F32)

        @pl.when(i > 0)
        def _():
            ext[tm:tm + POOL_HALO, :] = ext[0:POOL_HALO, :]

        inv = _pool_counts(tile, tm)
        dpooled = []
        for g in range(HEADS):
            sl = slice(HD * g, HD * (g + 1))
            pooled_g = pl_ref[:, sl]
            do = do_ref[:, sl]
            slot_ref[0:1, sl] += jnp.sum(_mm(pooled_g, w_ref[g]) * do, axis=0, keepdims=True)
            dy = (do * sc_ref[:, sl]).astype(BF16)
            dw_ref[g] += _mm_tn(pooled_g, dy)
            dpo = _mm_nt(dy, w_ref[g])
            dpooled.append(dpo)
            ext[0:tm, sl] = dpo * inv[g]
        for g, w in enumerate(POOL_WINDOWS):
            sl = slice(HD * g, HD * (g + 1))
            win = ext[0:tm, sl]
            for d in range(1, w):
                win = win + ext[d:d + tm, sl]
            dz_ref[:, sl] = (win - dpooled[g]).astype(BF16)

    return _call_behind(
        deps, body, name="pool_bwd", grid=(n_t,),
        out_shape=[jax.ShapeDtypeStruct((s, IN_WIDTH), BF16), jax.ShapeDtypeStruct((HEADS, HD, HD), F32),
                   jax.ShapeDtypeStruct((SLOT, D_MODEL), F32)],
        in_specs=[pl.BlockSpec((tm, D_MODEL), lambda i: (n_t - 1 - i, 0)), VMEM_WHOLE,
                  pl.BlockSpec((tm, HW), lambda i: (n_t - 1 - i, 0)), _full((HEADS, HD, HD)), _full((1, HW))],
        out_specs=[pl.BlockSpec((tm, HW), lambda i: (n_t - 1 - i, 4)), _full((HEADS, HD, HD)), _full((SLOT, D_MODEL))],
        scratch_shapes=[pltpu.VMEM((tm + POOL_HALO, HW), F32), pltpu.VMEM((tm, HW), F32)],
        compiler_params=_params(("arbitrary",)),
    )(dx1, w_out, pooled, w_pool, scale)


def _hgrn_bwd(z, o, dx1, w_out, states, lb_logits, gn, dz_in, deps):
    s = z.shape[0]
    n_chunks = s // CHUNK

    def body(zq_ref, zf_ref, zi_ref, zg_ref, o_ref, dx_ref, wo_ref, st_ref, lbl_ref, gn_ref, dzin_ref,
             dz_ref, dlb_ref, dgn_ref, dstate, b_scr, dlb_acc, do_ref):
        i = pl.program_id(0)

        @pl.when(i == 0)
        def _():
            dstate[...] = jnp.zeros_like(dstate)
            dlb_acc[...] = jnp.zeros_like(dlb_acc)
            dgn_ref[...] = jnp.zeros_like(dgn_ref)
            dlb_ref[...] = jnp.zeros_like(dlb_ref)

        do_ref[...] = _mm_nt(dx_ref[...], wo_ref[0:HW, :])
        lb = _sigmoid(lbl_ref[0:1, :] - lbl_ref[1:2, :])
        row, col = _chunk_masks()
        causal = col <= row
        tri = _ones_where(causal)
        upper = _ones_where(col >= row)
        strict_lower = _ones_where(col < row)
        in_sub, in_sub_head = _sub_chunk_masks(HW), _sub_chunk_masks(HD)
        gn_row = _lanes([gn_ref[h:h + 1, :] for h in range(HEADS)])
        sums = {"dlb": 0.0, "dgn": 0.0}

        def front(c):
            r0 = CHUNK * c
            rs = slice(r0, r0 + CHUNK)
            p = {"rs": rs}
            p["zq"] = zq_ref[rs, :]
            p["q"], p["sq"], p["sig"], p["f"] = _hgrn_gates(p["zq"], zf_ref[rs, :], lb)
            p["kk"] = 1.0 - p["f"]
            b = _tri_dot(tri, jnp.log(p["f"]), 3)
            b_scr[rs, :] = b
            p["v"] = zi_ref[rs, :]
            o, zg, doa = o_ref[rs, :], zg_ref[rs, :], do_ref[rs, :]
            sg = _sigmoid(zg)
            rms = _head_rms(o)
            n = o * rms
            don = doa * (zg * sg)
            sums["dgn"] = sums["dgn"] + jnp.sum(don * n, axis=0, keepdims=True)
            dn = don * gn_row
            p["d_o"] = rms * (dn - n * _head_mean(dn * n))
            dz_ref[rs, 3 * HW:4 * HW] = (doa * (n * gn_row) * (sg * (1.0 + zg * (1.0 - sg)))).astype(BF16)
            p["eq"], p["ek"] = _hgrn_decay_factors(b_scr, r0, b, in_sub)
            b_last = b_scr[r0 + CHUNK - 1:r0 + CHUNK, :]
            p["lam"], p["e_last"], p["lam_last"] = jnp.exp(b), jnp.exp(b_last - b), jnp.exp(b_last)
            p["qe"], p["qg"], p["kd"] = p["q"] * p["eq"], p["q"] * p["lam"], p["kk"] * p["e_last"]
            p["ke"] = [p["kk"] * e for e in p["ek"]]
            p["q16"] = [_per_sub_chunk(_head(p["qe"], h), in_sub_head).astype(BF16) for h in range(HEADS)]
            p["ke16"] = [_lanes([_head(p["ke"][j], h) for j in range(N_SUB)]).astype(BF16) for h in range(HEADS)]
            return p

        def recurrence(c, p):
            m = {k: [] for k in ("dv", "gq", "gk", "dqi", "dkd", "st")}
            a, da, dv_state = [], [], []
            for h in range(HEADS):
                vh, doh = _head(p["v"], h), _head(p["d_o"], h)
                st0, ds1 = st_ref[c, h], dstate[h]
                a.append(jnp.where(causal, _mm_nt(p["q16"][h], p["ke16"][h]), 0.0))
                da.append(jnp.where(causal, _mm_nt(doh, vh), 0.0))
                dv_state.append(_mm_nt(_head(p["kd"], h), ds1))
                m["dqi"].append(_mm(doh, st0))
                m["dkd"].append(_mm(vh, ds1))
                m["st"].append(jnp.sum(st0 * ds1, axis=0, keepdims=True))
                dstate[h] = ds1 * _head(p["lam_last"], h) + _mm_tn(doh, _head(p["qg"], h))
            for h in range(HEADS):
                m["dv"].append(_mm_tn(a[h], _head(p["d_o"], h)) + dv_state[h])
                m["gq"].append(_own_lane_block(_mm(da[h], p["ke16"][h]), in_sub_head))
                m["gk"].append(_mm_tn(da[h], p["q16"][h]))
            return m

        def back(p, m):
            rs = p["rs"]
            dz_ref[rs, 2 * HW:3 * HW] = _lanes(m["dv"]).astype(BF16)
            gq = _lanes(m["gq"])
            gk = [_lanes([m["gk"][h][:, HD * j:HD * (j + 1)] for h in range(HEADS)]) for j in range(N_SUB)]
            dq_inter = p["lam"] * _lanes(m["dqi"])
            dq = p["eq"] * gq + dq_inter
            dk_intra = sum(p["ek"][j] * gk[j] for j in range(N_SUB))
            dk_state = _lanes(m["dkd"]) * p["e_last"]
            db_intra = (p["qe"].astype(BF16).astype(F32) * gq
                        - sum(p["ke"][j].astype(BF16).astype(F32) * gk[j] for j in range(N_SUB)))
            dlf = (_tri_dot(upper, db_intra + p["q"] * dq_inter, 2) + _tri_dot(strict_lower, p["kk"] * dk_state, 2)
                   + p["lam_last"] * _lanes(m["st"]))
            sig, sq, zq = p["sig"], p["sq"], p["zq"]
            df = dlf / p["f"] - (dk_intra + dk_state)
            sums["dlb"] = sums["dlb"] + jnp.sum(df * (1.0 - sig), axis=0, keepdims=True)
            dz_ref[rs, HW:2 * HW] = (df * (1.0 - lb) * sig * (1.0 - sig)).astype(BF16)
            dz_ref[rs, 0:HW] = (dq * (sq * (1.0 + zq * (1.0 - sq)))).astype(BF16)

        p = front(CHUNKS_PER_STEP - 1)
        for c in reversed(range(CHUNKS_PER_STEP)):
            m = recurrence(c, p)
            p_next = front(c - 1) if c > 0 else None
            back(p, m)
            p = p_next
        dlb_acc[...] += sums["dlb"]
        for h in range(HEADS):
            dgn_ref[h:h + 1, 0:HD] += _head(sums["dgn"], h)

        @pl.when(i == n_steps - 1)
        def _():
            dl0 = dlb_acc[...] * lb * (1.0 - lb)
            dlb_ref[0:1, 0:HW] = dl0
            dlb_ref[1:2, 0:HW] = -dl0

    rows = CHUNK * CHUNKS_PER_STEP
    n_steps = s // rows
    rev = lambda i: n_steps - 1 - i
    zspec = lambda cb: pl.BlockSpec((rows, HW), lambda i, cb=cb: (rev(i), cb))
    slot = jax.ShapeDtypeStruct((SLOT, D_MODEL), F32)
    return _call_behind(
        deps, body, name="hgrn_bwd", grid=(n_steps,),
        out_shape=[jax.ShapeDtypeStruct((s, IN_WIDTH), BF16), slot, slot],
        in_specs=[zspec(0), zspec(1), zspec(2), zspec(3), pl.BlockSpec((rows, HW), lambda i: (rev(i), 0)),
                  pl.BlockSpec((rows, D_MODEL), lambda i: (rev(i), 0)), VMEM_WHOLE,
                  pl.BlockSpec((CHUNKS_PER_STEP, HEADS, HD, HD), lambda i: (rev(i), 0, 0, 0)), _full((2, HW)),
                  _full((HEADS, HD)), ANY_SPACE],
        out_specs=[pl.BlockSpec((rows, 4 * HW), lambda i: (rev(i), 0)), _full((SLOT, D_MODEL)), _full((SLOT, D_MODEL))],
        scratch_shapes=[pltpu.VMEM((HEADS, HD, HD), F32), pltpu.VMEM((rows, HW), F32), pltpu.VMEM((1, HW), F32),
                        pltpu.VMEM((rows, HW), F32)],
        input_output_aliases={10: 0},
        compiler_params=_params(("arbitrary",)),
    )(z, z, z, z, o, dx1, w_out, states, lb_logits, gn, dz_in)


def _in_bwd(dz, w_t, x0, g, dx1, deps):
    s = x0.shape[0]
    tm = min(WIDE_ROW_TILE, s)

    def body(dz_ref, w_ref, x_ref, g_ref, d_ref, dx_ref, slot_ref):
        _zero_slot(slot_ref)
        dx, dg = _rms_bwd(x_ref[...], g_ref[...], _mm(dz_ref[...], w_ref[...]))
        dx_ref[...] = d_ref[...] + dx
        slot_ref[0:1, :] += dg

    row_f32 = pl.BlockSpec((tm, D_MODEL), lambda i: (i, 0))
    return _call_behind(
        deps, body, name="in_bwd", grid=(s // tm,),
        out_shape=[jax.ShapeDtypeStruct((s, D_MODEL), F32), jax.ShapeDtypeStruct((SLOT, D_MODEL), F32)],
        in_specs=[pl.BlockSpec((tm, IN_WIDTH), lambda i: (i, 0)), VMEM_WHOLE, row_f32, _full((1, D_MODEL)), row_f32],
        out_specs=[row_f32, _full((SLOT, D_MODEL))],
        compiler_params=_params(("arbitrary",)),
    )(dz, w_t, x0, g, dx1)


def kernel(x, mem, norm_mix_g, w_in, lb_logits, hgrn_norm_g, w_pool, pool_scale, w_out, norm_x_g, norm_mem_g, w_xq, w_xk, w_xv, w_xo, norm_ffn_g, w_ff1, w_ff2, final_norm_g, loss_target, m_norm_mix_g, m_w_in, m_lb_logits, m_hgrn_norm_g, m_w_pool, m_pool_scale, m_w_out, m_norm_x_g, m_norm_mem_g, m_w_xq, m_w_xk, m_w_xv, m_w_xo, m_norm_ffn_g, m_w_ff1, m_w_ff2, m_final_norm_g, v_norm_mix_g, v_w_in, v_lb_logits, v_hgrn_norm_g, v_w_pool, v_pool_scale, v_w_out, v_norm_x_g, v_norm_mem_g, v_w_xq, v_w_xk, v_w_xv, v_w_xo, v_norm_ffn_g, v_w_ff1, v_w_ff2, v_final_norm_g):
    x0 = x[0]
    mem0 = mem[0]
    tgt = loss_target[0]
    gn = hgrn_norm_g[0]
    gfin = final_norm_g.reshape(1, D_MODEL)
    wp = w_pool[0]
    heads_2d = lambda w: w.reshape(D_MODEL // N_DEV, D_MODEL)
    xo_2d = lambda w: w.reshape(D_MODEL, D_MODEL // N_DEV)

    first = _all_gather_weights([w_in[0].T], [w_out[0], heads_2d(w_xq), heads_2d(w_xk), heads_2d(w_xv), xo_2d(w_xo).T,
                                              w_ff1[0], w_ff2[0]])
    win_t = first[0].reshape(IN_WIDTH, D_MODEL)
    ga_attn, ga_mlp = _gather_first_start([first[1:6], first[6:8]], "gather_first_start")

    z, h = _in_proj(x0, norm_mix_g, win_t, deps=[ga_attn[3]])
    mixed_a, o_pre, states = _hgrn_fwd(z, lb_logits, gn)
    lands = _split_wait(_gather_first_copies, ga_attn, o_pre, "gather_attn_first_wait")
    gb_attn = _gather_forward_start(lands, "gather_attn_forward_start")
    mixed, pooled = _pool_fwd(z, wp, pool_scale, mixed_a, deps=[gb_attn[3]])
    lands = _split_wait(_gather_forward_copies, gb_attn, pooled, "gather_attn_forward_wait")
    wout_f, wq_f, wk_f, wv_f, wo_t = (t.reshape(D_MODEL, D_MODEL) for t in lands)
    hm, xk, xv = _mem_kv(mem0, norm_mem_g, wk_f, wv_f, deps=[])
    x1, x2, hq, xq, att = _mix_xattn_fwd(x0, mixed, wout_f, norm_x_g, wq_f, xk, xv, wo_t, deps=[])
    lands = _split_wait(_gather_first_copies, ga_mlp, x2, "gather_mlp_first_wait")
    gb_mlp = _gather_forward_start(lands, "gather_mlp_forward_start")
    w1_b, w2_b = _split_wait(_gather_forward_copies, gb_mlp, gb_mlp[3], "gather_mlp_forward_wait")
    dx3, dx3_16, u, hf, slot_fin = _mlp_fwd_loss(x2, norm_ffn_g, w1_b, w2_b.reshape(D_FF, D_MODEL), gfin, tgt)

    rows = lambda t, r: t.reshape(N_DEV, r, D_MODEL)
    dw2 = _wgrad(u, dx3_16, "wgrad_ff2")
    ex_ff2 = _all_to_all_start([rows(dw2, FF_BLK)], [], "exchange_ff2_start")
    da, dx2, slot_ffn = _mlp_bwd(dx3, u, x2, norm_ffn_g, w1_b, w2_b, deps=[ex_ff2[3]])
    dw1 = _wgrad(hf, da, "wgrad_ff1", col_blocks=True)
    ex_ff1 = _all_to_all_start([dw1], [], "exchange_ff1_start")
    dx1, dx1_16, dxq, dxk, dxv, slot_x = _xattn_bwd(dx2, x1, norm_x_g, xq, xk, xv, wq_f, wo_t, deps=[ex_ff1[3]])
    dwo_t = _wgrad(dx2, att, "wgrad_xo")
    dwq = _wgrad(hq, dxq, "wgrad_xq")
    dwk, dwv, slot_mem = _mem_bwd(mem0, norm_mem_g, hm, dxk, dxv, wk_f, wv_f)
    ex_attn = _all_to_all_start([rows(dwq, 128), rows(dwk, 128), rows(dwv, 128), rows(dwo_t, 128)], [],
                                "exchange_attn_start")
    dwout = _wgrad(mixed, dx1_16, "wgrad_out")
    dz_pool, d_wpool, slot_ps = _pool_bwd(dx1_16, wout_f, pooled, wp, pool_scale, deps=[ex_attn[3]])
    small0 = jnp.concatenate([slot_x, slot_mem, slot_ffn, slot_fin, slot_ps], axis=0)
    ex_out = _all_to_all_start([rows(dwout, 128)], [small0, d_wpool], "exchange_out_start")
    dz, slot_lb, slot_gn = _hgrn_bwd(z, o_pre, dx1_16, wout_f, states, lb_logits, gn, dz_pool, deps=[ex_out[3]])
    (r_2,) = _split_wait(_all_to_all_copies(1), ex_ff2, dz, "exchange_ff2_wait")
    dwin_t, ff2_update = _wgrad(dz, h, "wgrad_in", update=(r_2, w_ff2[0], m_w_ff2[0], v_w_ff2[0]))
    small1 = jnp.concatenate([slot_lb, slot_gn], axis=0)
    ex_in = _all_to_all_start([rows(dwin_t, 320)], [small1], "exchange_in_start")
    grad_x, slot_mix = _in_bwd(dz, win_t, x0, norm_mix_g, dx1, deps=[ex_in[3]])
    ex_mix = _all_to_all_start([], [slot_mix], "exchange_mix_start")

    out = {}
    out["w_ff2"] = ff2_update
    (r_1,) = _split_wait(_all_to_all_copies(1), ex_ff1, ex_mix[3], "exchange_ff1_wait")
    out["w_ff1"] = _sum_adamw(r_1, w_ff1[0], m_w_ff1[0], v_w_ff1[0], "adamw_ff1")
    r_q, r_k, r_v, r_o = _split_wait(_all_to_all_copies(4), ex_attn, out["w_ff1"][1], "exchange_attn_wait")
    sums = _sum_sources_whole([r_q, r_k, r_v, r_o], "sum_grad_attn")
    g_attn = [g.reshape(w_xq.shape) for g in sums[:3]] + [sums[3].T]
    attn = _adamw_whole([(g_attn[0], w_xq, m_w_xq, v_w_xq), (g_attn[1], w_xk, m_w_xk, v_w_xk),
                         (g_attn[2], w_xv, m_w_xv, v_w_xv),
                         (g_attn[3], xo_2d(w_xo), xo_2d(m_w_xo), xo_2d(v_w_xo))], "adamw_attn")
    for n, g, res in zip(("w_xq", "w_xk", "w_xv", "w_xo"), g_attn, attn):
        out[n] = (g, *res)
    r_out, r_small0, r_wpool = _split_wait(_all_to_all_copies(1), ex_out, attn[3][0], "exchange_out_wait")
    out["w_out"] = _sum_adamw(r_out, w_out[0], m_w_out[0], v_w_out[0], "adamw_out")
    r_in, r_small1 = _split_wait(_all_to_all_copies(1), ex_in, out["w_out"][1], "exchange_in_wait")
    in_t = _sum_adamw(r_in, w_in[0].T, m_w_in[0].T, v_w_in[0].T, "adamw_in")
    out["w_in"] = tuple(t.T for t in in_t)
    (r_small2,) = _split_wait(_all_to_all_copies(0), ex_mix, in_t[1], "exchange_mix_wait")
    row = lambda t: t.reshape(1, -1)
    small_params = {
        "norm_mix_g": (norm_mix_g, m_norm_mix_g, v_norm_mix_g),
        "lb_logits": (lb_logits, m_lb_logits, v_lb_logits),
        "hgrn_norm_g": (hgrn_norm_g[0], m_hgrn_norm_g[0], v_hgrn_norm_g[0]),
        "pool_scale": (pool_scale, m_pool_scale, v_pool_scale),
        "norm_x_g": (norm_x_g, m_norm_x_g, v_norm_x_g),
        "norm_mem_g": (norm_mem_g, m_norm_mem_g, v_norm_mem_g),
        "norm_ffn_g": (norm_ffn_g, m_norm_ffn_g, v_norm_ffn_g),
        "final_norm_g": (row(final_norm_g), row(m_final_norm_g), row(v_final_norm_g)),
        "w_pool": (wp, m_w_pool[0], v_w_pool[0]),
    }
    loss, small_out = _small_update([r_small0, r_small1, r_small2], r_wpool, small_params)
    out.update(small_out)

    shapes = dict(norm_mix_g=norm_mix_g, w_in=w_in, lb_logits=lb_logits, hgrn_norm_g=hgrn_norm_g, w_pool=w_pool,
                  pool_scale=pool_scale, w_out=w_out, norm_x_g=norm_x_g, norm_mem_g=norm_mem_g, w_xq=w_xq, w_xk=w_xk,
                  w_xv=w_xv, w_xo=w_xo, norm_ffn_g=norm_ffn_g, w_ff1=w_ff1, w_ff2=w_ff2, final_norm_g=final_norm_g)
    order = list(shapes)
    group = lambda k: [out[n][k].reshape(shapes[n].shape) for n in order]
    return (loss.reshape(()), grad_x.reshape(x.shape), *group(0), *group(1), *group(2), *group(3))
```

```python
import jax
import jax.numpy as jnp
from jax import lax
from jax.experimental import pallas as pl
from jax.experimental.pallas import tpu as pltpu

F32 = jnp.float32
BF16 = jnp.bfloat16

D_MODEL = 1024
N_DEV = 8
HEADS = 4
HD = 128
HW = HEADS * HD
IN_WIDTH = 5 * HW
XHD = 256
MEM_LEN = 256
D_FF = 4096
FF_BLK = D_FF // N_DEV
POOL_WINDOWS = (2, 4, 8, 16)
POOL_HALO = 16
CHUNK = 64
CHUNKS_PER_STEP = 8
SUB = 16
N_SUB = CHUNK // SUB
EXP_CAP = 80.0
EPS = 1e-6
TINY = 1e-30
ROW_TILE = 512
WIDE_ROW_TILE = 1024
SLOT = 8
V7X_VMEM_LIMIT = 56 * 1024 * 1024

ADAM_LR = 0.001
ADAM_B1 = 0.9
ADAM_B2 = 0.999
ADAM_EPS = 1e-08
ADAM_WD = 0.01
ADAM_STEP = 10

MESH_ID = pl.DeviceIdType.MESH


def _params(sem=None, vmem=V7X_VMEM_LIMIT):
    return pltpu.CompilerParams(dimension_semantics=sem, vmem_limit_bytes=vmem)


def _mm(a, b):
    return lax.dot_general(a.astype(BF16), b.astype(BF16), (((1,), (0,)), ((), ())), preferred_element_type=F32)


def _mm_nt(a, b):
    return lax.dot_general(a.astype(BF16), b.astype(BF16), (((1,), (1,)), ((), ())), preferred_element_type=F32)


def _mm_tn(a, b):
    return lax.dot_general(a.astype(BF16), b.astype(BF16), (((0,), (0,)), ((), ())), preferred_element_type=F32)


def _sigmoid(x):
    return 1.0 / (1.0 + jnp.exp(-x))


def _rms(x):
    return lax.rsqrt(jnp.mean(x * x, axis=-1, keepdims=True) + EPS)


def _rms_bwd(x, g, dh):
    r = _rms(x)
    n = x * r
    dn = dh * g
    dx = r * (dn - n * jnp.mean(dn * n, axis=-1, keepdims=True))
    return dx, jnp.sum(dh * n, axis=0, keepdims=True)


def _tri_dot(tri, x, passes):
    acc = None
    rest = x
    for _ in range(passes):
        piece = rest.astype(BF16)
        part = lax.dot_general(tri, piece, (((1,), (0,)), ((), ())), preferred_element_type=F32)
        acc = part if acc is None else acc + part
        rest = rest - piece.astype(F32)
    return acc


def _adam_update(g, w, m, v):
    nm = ADAM_B1 * m + (1.0 - ADAM_B1) * g
    nv = ADAM_B2 * v + (1.0 - ADAM_B2) * (g * g)
    m_hat = nm / (1.0 - ADAM_B1 ** ADAM_STEP)
    v_hat = nv / (1.0 - ADAM_B2 ** ADAM_STEP)
    return -ADAM_LR * (m_hat / (jnp.sqrt(v_hat) + ADAM_EPS) + ADAM_WD * w), nm, nv


def _full(shape):
    return pl.BlockSpec(shape, lambda *_: (0,) * len(shape))


VMEM_WHOLE = pl.BlockSpec(memory_space=pltpu.VMEM)
ANY_SPACE = pl.BlockSpec(memory_space=pl.ANY)


def _mesh_pos():
    return lax.axis_index("x"), lax.axis_index("y"), lax.axis_index("c")


def _flat(px, py, pc):
    return 4 * px + 2 * py + pc


def _all_gather_weights(shards, cast_only):
    n, nc = len(shards), len(cast_only)
    step = 64

    def body(*refs):
        x_refs, c_refs = refs[:n], refs[n:n + nc]
        out_refs, cast_refs = refs[n + nc:2 * n + nc], refs[2 * n + nc:2 * n + 2 * nc]
        bufs = refs[2 * n + 2 * nc:3 * n + 2 * nc]
        send_sems, recv_sems, local_sems = refs[3 * n + 2 * nc:]
        _handshake(_peers_first_level())
        x, y, c = _mesh_pos()
        me, sibling = (x, y, c), (x, y, 1 - c)
        chips = [(1 - x, y), (x, 1 - y), (1 - x, 1 - y)]

        def copy(a, k, blk, to, src=None):
            rows = out_refs[a].at[_flat(*blk)]
            return pltpu.make_async_remote_copy(
                src_ref=rows if src is None else src, dst_ref=rows,
                send_sem=send_sems.at[7 * a + k], recv_sem=recv_sems.at[7 * a + k], device_id=to, device_id_type=MESH_ID)

        def cast_rows(src, dst, rows):
            def cast(i, carry):
                r0 = pl.multiple_of(i * step, step)
                dst[pl.ds(r0, step), :] = src[pl.ds(r0, step), :].astype(BF16)
                return carry
            lax.fori_loop(0, rows // step, cast, 0)

        first, mine = [], []
        for a in range(n):
            cast_rows(x_refs[a], bufs[a], shards[a].shape[0])
            mine.append(pltpu.make_async_copy(bufs[a], out_refs[a].at[_flat(*me)], local_sems.at[a]))
            first.append(copy(a, 0, me, sibling, src=bufs[a]))
            first += [copy(a, 1 + j, me, (*chip, c), src=bufs[a]) for j, chip in enumerate(chips)]
            for cp in [mine[-1]] + first[-4:]:
                cp.start()
        for a in range(nc):
            cast_rows(c_refs[a], cast_refs[a], cast_only[a].shape[0])
        passed = []
        for j, chip in enumerate(chips):
            for a in range(n):
                copy(a, 1 + j, (*chip, c), me).wait_recv()
                passed.append(copy(a, 4 + j, (*chip, c), sibling))
                passed[-1].start()
        for a in range(n):
            copy(a, 0, sibling, me).wait_recv()
            for j, chip in enumerate(chips):
                copy(a, 4 + j, (*chip, 1 - c), me).wait_recv()
        for cp in first + passed:
            cp.wait_send()
        for cp in mine:
            cp.wait()

    return pl.pallas_call(
        body, name="all_gather_w_in",
        out_shape=[jax.ShapeDtypeStruct((N_DEV,) + s.shape, BF16) for s in shards]
        + [jax.ShapeDtypeStruct(s.shape, BF16) for s in cast_only],
        in_specs=[VMEM_WHOLE] * (n + nc), out_specs=[ANY_SPACE] * n + [VMEM_WHOLE] * nc,
        scratch_shapes=[pltpu.VMEM(s.shape, BF16) for s in shards]
        + [pltpu.SemaphoreType.DMA((7 * n,)), pltpu.SemaphoreType.DMA((7 * n,)), pltpu.SemaphoreType.DMA((n,))],
        compiler_params=pltpu.CompilerParams(vmem_limit_bytes=V7X_VMEM_LIMIT, collective_id=GATHER_W_IN_ID),
    )(*shards, *cast_only)


HBM_SPEC = pl.BlockSpec(memory_space=pltpu.HBM)
SEM_SPEC = pl.BlockSpec(memory_space=pltpu.SEMAPHORE)
EFFECT = pltpu.SideEffectType.DATAFLOW_SIDE_EFFECTING
TOKEN = jax.ShapeDtypeStruct((8, 128), F32)


def _in_hbm(a):
    return pltpu.with_memory_space_constraint(a, pltpu.HBM)


START_IDS = {name: i for i, name in enumerate((
    "gather_first_start", "gather_attn_forward_start", "gather_mlp_forward_start", "exchange_ff2_start",
    "exchange_ff1_start", "exchange_attn_start", "exchange_out_start", "exchange_in_start", "exchange_mix_start"))}


GATHER_W_IN_ID = len(START_IDS)


def _handshake(peers):
    barrier = pltpu.get_barrier_semaphore()
    for peer in peers:
        pl.semaphore_signal(barrier, inc=1, device_id=peer, device_id_type=MESH_ID)
    pl.semaphore_wait(barrier, len(peers))


def _peers_all():
    x, y, c = _mesh_pos()
    return [(1 - x if k & 4 else x, 1 - y if k & 2 else y, 1 - c if k & 1 else c) for k in range(1, N_DEV)]


def _peers_first_level():
    x, y, c = _mesh_pos()
    return [(x, y, 1 - c), (1 - x, y, c), (x, 1 - y, c), (1 - x, 1 - y, c)]


def _peers_sibling():
    x, y, c = _mesh_pos()
    return [(x, y, 1 - c)]


def _split_start(copies_of, srcs, lands, n_sems, name, peers_of, collective_id):
    ns, nl, k = len(srcs), len(lands), len(n_sems)

    def body(*refs):
        _handshake(peers_of())
        src_refs, land_refs = refs[:ns], refs[ns:ns + nl]
        sems = refs[ns + nl:ns + nl + k]
        token = refs[-1]
        for cp in copies_of(src_refs, land_refs, sems):
            cp.start()
        token[...] = jnp.zeros_like(token)

    outs = pl.pallas_call(
        body, name=name,
        out_shape=[pltpu.SemaphoreType.DMA((q,)) for q in n_sems]
        + [pltpu.HBM(a.shape, a.dtype) for a in list(srcs) + list(lands)] + [TOKEN],
        in_specs=[HBM_SPEC] * (ns + nl),
        out_specs=[SEM_SPEC] * k + [HBM_SPEC] * (ns + nl) + [VMEM_WHOLE],
        input_output_aliases={i: k + i for i in range(ns + nl)},
        compiler_params=pltpu.CompilerParams(has_side_effects=EFFECT, collective_id=collective_id),
    )(*[_in_hbm(a) for a in list(srcs) + list(lands)])
    return outs[:k], outs[k:k + ns], outs[k + ns:k + ns + nl], outs[-1]


def _split_wait(copies_of, handle, after, name):
    sems, srcs, lands, _ = handle
    ns, nl, k = len(srcs), len(lands), len(sems)

    def body(*refs):
        src_refs, land_refs = refs[:ns], refs[ns:ns + nl]
        sem_refs = refs[ns + nl:ns + nl + k]
        for cp in copies_of(src_refs, land_refs, sem_refs):
            cp.wait()

    outs = pl.pallas_call(
        body, name=name,
        out_shape=[pltpu.HBM(a.shape, a.dtype) for a in list(srcs) + list(lands)],
        in_specs=[HBM_SPEC] * (ns + nl) + [SEM_SPEC] * k + [ANY_SPACE],
        out_specs=[HBM_SPEC] * (ns + nl),
        input_output_aliases={i: i for i in range(ns + nl)},
        compiler_params=pltpu.CompilerParams(has_side_effects=EFFECT),
    )(*srcs, *lands, *sems, after)
    return outs[ns:]


def _gather_first_copies(shard_refs, land_refs, sems):
    send_sems, recv_sems, local_sems = sems
    x, y, c = _mesh_pos()
    me = _flat(x, y, c)
    peers = [(x, y, 1 - c), (1 - x, y, c), (x, 1 - y, c), (1 - x, 1 - y, c)]
    copies = []
    for a, (shard, land) in enumerate(zip(shard_refs, land_refs)):
        copies.append(pltpu.make_async_copy(shard, land.at[me], local_sems.at[a]))
        for k, peer in enumerate(peers):
            copies.append(pltpu.make_async_remote_copy(
                src_ref=shard, dst_ref=land.at[me], send_sem=send_sems.at[4 * a + k], recv_sem=recv_sems.at[4 * a + k],
                device_id=peer, device_id_type=MESH_ID))
    return copies


def _gather_forward_copies(src_refs, land_refs, sems):
    del src_refs
    send_sems, recv_sems = sems
    x, y, c = _mesh_pos()
    chips = [(1 - x, y), (x, 1 - y), (1 - x, 1 - y)]
    copies = []
    for a, land in enumerate(land_refs):
        for j, chip in enumerate(chips):
            rows = land.at[_flat(*chip, c)]
            copies.append(pltpu.make_async_remote_copy(
                src_ref=rows, dst_ref=rows, send_sem=send_sems.at[3 * a + j], recv_sem=recv_sems.at[3 * a + j],
                device_id=(x, y, 1 - c), device_id_type=MESH_ID))
    return copies


def _gather_first_start(groups, name):
    shards = [s for g in groups for s in g]
    lands = [lax.empty((N_DEV,) + s.shape, s.dtype) for s in shards]
    bounds = [sum(len(g) for g in groups[:i]) for i in range(len(groups) + 1)]

    def copies_of(src_refs, land_refs, sems):
        copies = []
        for i in range(len(groups)):
            lo, hi = bounds[i], bounds[i + 1]
            copies += _gather_first_copies(src_refs[lo:hi], land_refs[lo:hi], sems[3 * i:3 * i + 3])
        return copies

    n_sems = tuple(q for g in groups for q in (4 * len(g), 4 * len(g), len(g)))
    sems, srcs, lands, token = _split_start(copies_of, shards, lands, n_sems, name, _peers_first_level, START_IDS[name])
    return [(sems[3 * i:3 * i + 3], srcs[bounds[i]:bounds[i + 1]], lands[bounds[i]:bounds[i + 1]], token)
            for i in range(len(groups))]


def _gather_forward_start(lands, name):
    n = len(lands)
    return _split_start(_gather_forward_copies, [], lands, (3 * n, 3 * n), name, _peers_sibling, START_IDS[name])


def _all_to_all_copies(n_scattered):
    def copies_of(src_refs, land_refs, sems):
        send_sems, recv_sems, local_sems = sems
        x, y, c = _mesh_pos()
        me = _flat(x, y, c)
        copies = []
        for a, (src, land) in enumerate(zip(src_refs, land_refs)):
            scattered = a < n_scattered
            copies.append(pltpu.make_async_copy(src.at[me] if scattered else src, land.at[me], local_sems.at[a]))
            for k in range(1, N_DEV):
                peer = (1 - x if k & 4 else x, 1 - y if k & 2 else y, 1 - c if k & 1 else c)
                copies.append(pltpu.make_async_remote_copy(
                    src_ref=src.at[_flat(*peer)] if scattered else src, dst_ref=land.at[me],
                    send_sem=send_sems.at[7 * a + k - 1], recv_sem=recv_sems.at[7 * a + k - 1],
                    device_id=peer, device_id_type=MESH_ID))
        return copies
    return copies_of


def _all_to_all_start(scattered, broadcast, name):
    srcs = list(scattered) + list(broadcast)
    lands = [lax.empty(a.shape, a.dtype) for a in scattered] + [lax.empty((N_DEV,) + a.shape, a.dtype) for a in broadcast]
    n = len(srcs)
    return _split_start(_all_to_all_copies(len(scattered)), srcs, lands, (7 * n, 7 * n, n), name, _peers_all,
                        START_IDS[name])


def _call_behind(deps, body, *, in_specs, **kwargs):
    n_in, n_dep = len(in_specs), len(deps)

    def body_without_deps(*refs):
        return body(*refs[:n_in], *refs[n_in + n_dep:])

    call = pl.pallas_call(body_without_deps, in_specs=list(in_specs) + [ANY_SPACE] * n_dep, **kwargs)
    return lambda *operands: call(*operands, *deps)


def _row_tile(rows):
    if rows <= 2 * 256:
        return rows
    for cand in (256, 128, 64, 32, 16):
        if rows % cand == 0:
            return cand
    return rows


def _adamw_whole(groups, name):
    n = len(groups)

    def body(*refs):
        for i in range(n):
            g_ref, w_ref, m_ref, v_ref = refs[4 * i:4 * i + 4]
            d_ref, nm_ref, nv_ref = refs[4 * n + 3 * i:4 * n + 3 * i + 3]
            d_ref[...], nm_ref[...], nv_ref[...] = _adam_update(g_ref[...], w_ref[...], m_ref[...], v_ref[...])

    outs = pl.pallas_call(
        body, name=name, out_shape=[jax.ShapeDtypeStruct(grp[0].shape, F32) for grp in groups for _ in range(3)],
        in_specs=[VMEM_WHOLE] * (4 * n), out_specs=[VMEM_WHOLE] * (3 * n),
        compiler_params=_params(),
    )(*[t for grp in groups for t in grp])
    return [outs[3 * i:3 * i + 3] for i in range(n)]


def _sum_sources_whole(recvs, name):
    n = len(recvs)

    def body(*refs):
        for r_ref, o_ref in zip(refs[:n], refs[n:]):
            acc = r_ref[0].astype(F32)
            for d in range(1, N_DEV):
                acc = acc + r_ref[d].astype(F32)
            o_ref[...] = acc

    return pl.pallas_call(
        body, name=name, out_shape=[jax.ShapeDtypeStruct(r.shape[1:], F32) for r in recvs],
        in_specs=[VMEM_WHOLE] * n, out_specs=[VMEM_WHOLE] * n,
        compiler_params=_params(),
    )(*recvs)


def _sum_adamw(recv, w, m, v, name):
    _, rows, cols = recv.shape
    tile = _row_tile(rows)

    def body(r_ref, w_ref, m_ref, v_ref, g_ref, d_ref, nm_ref, nv_ref):
        acc = r_ref[0].astype(F32)
        for d in range(1, N_DEV):
            acc = acc + r_ref[d].astype(F32)
        g_ref[...] = acc
        d_ref[...], nm_ref[...], nv_ref[...] = _adam_update(acc, w_ref[...], m_ref[...], v_ref[...])

    spec = pl.BlockSpec((tile, cols), lambda i: (i, 0))
    shp = jax.ShapeDtypeStruct((rows, cols), F32)
    return pl.pallas_call(
        body, name=name, grid=(rows // tile,), out_shape=[shp] * 4,
        in_specs=[pl.BlockSpec((N_DEV, tile, cols), lambda i: (0, i, 0)), spec, spec, spec], out_specs=[spec] * 4,
        compiler_params=_params(("parallel",)),
    )(recv, w, m, v)


SMALL_SLOTS = {"norm_x_g": (0, 0, 1, D_MODEL), "norm_mem_g": (0, 8, 1, D_MODEL), "norm_ffn_g": (0, 16, 1, D_MODEL),
               "final_norm_g": (0, 24, 1, D_MODEL), "pool_scale": (0, 32, 1, HW),
               "norm_mix_g": (1, 0, 1, D_MODEL), "lb_logits": (1, 8, 2, HW), "hgrn_norm_g": (1, 16, HEADS, HD)}
LOSS_ROW = 25
SMALL_ORDER = ("norm_mix_g", "lb_logits", "hgrn_norm_g", "pool_scale", "norm_x_g", "norm_mem_g", "norm_ffn_g",
               "final_norm_g", "w_pool")


def _small_update(srecvs, wprecv, params):
    flat = [t for n in SMALL_ORDER for t in params[n]]
    nb = len(srecvs)
    n_in = nb + 1 + len(flat)

    def body(*refs):
        s_refs, wp_ref = refs[0:nb], refs[nb]
        in_refs = refs[nb + 1:n_in]
        loss_ref = refs[n_in]
        out_refs = refs[n_in + 1:-nb]
        accs = refs[-nb:]
        for s_ref, acc in zip(s_refs, accs):
            total = s_ref[0]
            for d in range(1, N_DEV):
                total = total + s_ref[d]
            acc[...] = total
        loss_ref[...] = accs[0][LOSS_ROW:LOSS_ROW + 1, 0:1]
        for i, name in enumerate(SMALL_ORDER):
            w_ref, m_ref, v_ref = in_refs[3 * i:3 * i + 3]
            g_ref, d_ref, nm_ref, nv_ref = out_refs[4 * i:4 * i + 4]
            if name == "w_pool":
                g = wp_ref[0]
                for d in range(1, N_DEV):
                    g = g + wp_ref[d]
            else:
                buf, r0, nr, nc = SMALL_SLOTS[name]
                g = accs[buf][r0:r0 + nr, 0:nc]
            g_ref[...] = g
            d_ref[...], nm_ref[...], nv_ref[...] = _adam_update(g, w_ref[...], m_ref[...], v_ref[...])

    out_shape = [jax.ShapeDtypeStruct((1, 1), F32)]
    for n in SMALL_ORDER:
        out_shape += [jax.ShapeDtypeStruct(params[n][0].shape, F32)] * 4
    outs = pl.pallas_call(
        body, name="small_update", out_shape=out_shape,
        in_specs=[VMEM_WHOLE] * n_in, out_specs=[VMEM_WHOLE] * len(out_shape),
        scratch_shapes=[pltpu.VMEM(r.shape[1:], F32) for r in srecvs],
        compiler_params=_params(),
    )(*srecvs, wprecv, *flat)
    return outs[0], {n: outs[1 + 4 * i:5 + 4 * i] for i, n in enumerate(SMALL_ORDER)}


def _in_proj(x, g, w_t, deps):
    s = x.shape[0]
    tm = min(ROW_TILE, s)

    def body(x_ref, g_ref, w_ref, z_ref, h_ref):
        xv = x_ref[...]
        h = (xv * _rms(xv) * g_ref[...]).astype(BF16)
        h_ref[...] = h
        z_ref[...] = _mm_nt(h, w_ref[...])

    return _call_behind(
        deps, body, name="in_proj", grid=(s // tm,),
        out_shape=[jax.ShapeDtypeStruct((s, IN_WIDTH), F32), jax.ShapeDtypeStruct((s, D_MODEL), BF16)],
        in_specs=[pl.BlockSpec((tm, D_MODEL), lambda i: (i, 0)), _full((1, D_MODEL)), VMEM_WHOLE],
        out_specs=[pl.BlockSpec((tm, IN_WIDTH), lambda i: (i, 0)), pl.BlockSpec((tm, D_MODEL), lambda i: (i, 0))],
        compiler_params=_params(("parallel",)),
    )(x, g, w_t)


def _chunk_masks():
    row = lax.broadcasted_iota(jnp.int32, (CHUNK, CHUNK), 0)
    col = lax.broadcasted_iota(jnp.int32, (CHUNK, CHUNK), 1)
    return row, col


def _ones_where(mask):
    return jnp.where(mask, 1.0, 0.0).astype(BF16)


def _hgrn_gates(zq, zf, lb):
    sq = _sigmoid(zq)
    sig = _sigmoid(zf)
    f = lb + (1.0 - lb) * sig
    return zq * sq, sq, sig, f


def _sub_chunk_masks(width):
    trow = lax.broadcasted_iota(jnp.int32, (CHUNK, width), 0)
    return [(trow >= SUB * j) & (trow < SUB * (j + 1)) for j in range(N_SUB)]


def _head(a, h):
    return a[:, HD * h:HD * (h + 1)]


def _lanes(parts):
    return jnp.concatenate(parts, axis=1)


def _hgrn_decay_factors(b_scr, r0, b, in_sub):
    bases = [jnp.zeros((1, HW), F32)] + [b_scr[r0 + SUB * j - 1:r0 + SUB * j, :] for j in range(1, N_SUB)]
    own_base = bases[N_SUB - 1]
    for j in range(N_SUB - 2, -1, -1):
        own_base = jnp.where(in_sub[j], bases[j], own_base)
    eq = jnp.exp(b - own_base)
    ek = []
    for j in range(N_SUB):
        upto = SUB * (j + 1)
        e = jnp.exp(jnp.minimum(bases[j] - b[0:upto], EXP_CAP))
        ek.append(e if upto == CHUNK else jnp.concatenate([e, jnp.zeros((CHUNK - upto, HW), F32)], axis=0))
    return eq, ek


def _per_sub_chunk(x, in_sub):
    return _lanes([jnp.where(in_sub[j], x, 0.0) for j in range(N_SUB)])


def _own_lane_block(a, in_sub):
    out = a[:, HD * (N_SUB - 1):HD * N_SUB]
    for j in range(N_SUB - 2, -1, -1):
        out = jnp.where(in_sub[j], a[:, HD * j:HD * (j + 1)], out)
    return out


def _head_rms(o):
    return _lanes([jnp.broadcast_to(_rms(_head(o, h)), (CHUNK, HD)) for h in range(HEADS)])


def _head_mean(a):
    return _lanes([jnp.broadcast_to(jnp.mean(_head(a, h), axis=-1, keepdims=True), (CHUNK, HD)) for h in range(HEADS)])


def _hgrn_fwd(z, lb_logits, gn):
    s = z.shape[0]
    n_chunks = s // CHUNK

    def body(zq_ref, zf_ref, zi_ref, zg_ref, lbl_ref, gn_ref, oa_ref, o_ref, st_ref, state, b_scr):
        @pl.when(pl.program_id(0) == 0)
        def _():
            state[...] = jnp.zeros_like(state)

        lb = _sigmoid(lbl_ref[0:1, :] - lbl_ref[1:2, :])
        row, col = _chunk_masks()
        causal = col <= row
        tri = _ones_where(causal)
        in_sub, in_sub_head = _sub_chunk_masks(HW), _sub_chunk_masks(HD)
        gn_row = _lanes([gn_ref[h:h + 1, :] for h in range(HEADS)])
        def front(c):
            r0 = CHUNK * c
            rs = slice(r0, r0 + CHUNK)
            q, _, _, f = _hgrn_gates(zq_ref[rs, :], zf_ref[rs, :], lb)
            kk = 1.0 - f
            b = _tri_dot(tri, jnp.log(f), 3)
            b_scr[rs, :] = b
            eq, ek = _hgrn_decay_factors(b_scr, r0, b, in_sub)
            b_last = b_scr[r0 + CHUNK - 1:r0 + CHUNK, :]
            qe = q * eq
            return {"rs": rs, "v": zi_ref[rs, :], "qg": q * jnp.exp(b), "kd": kk * jnp.exp(b_last - b),
                    "lam_last": jnp.exp(b_last),
                    "q16": [_per_sub_chunk(_head(qe, h), in_sub_head).astype(BF16) for h in range(HEADS)],
                    "ke16": [_lanes([_head(kk * e, h) for e in ek]).astype(BF16) for h in range(HEADS)]}

        def recurrence(c, p):
            st_ref[c] = state[...]
            a, o_inter = [], []
            for h in range(HEADS):
                vh, st = _head(p["v"], h), state[h]
                a.append(jnp.where(causal, _mm_nt(p["q16"][h], p["ke16"][h]), 0.0))
                o_inter.append(_mm_nt(_head(p["qg"], h), st))
                state[h] = st * _head(p["lam_last"], h) + _mm_tn(vh, _head(p["kd"], h))
            return _lanes([_mm(a[h], _head(p["v"], h)) + o_inter[h] for h in range(HEADS)])

        def back(p, o):
            rs = p["rs"]
            o_ref[rs, :] = o
            zg = zg_ref[rs, :]
            oa_ref[rs, :] = (o * _head_rms(o) * gn_row * zg * _sigmoid(zg)).astype(BF16)

        p = front(0)
        for c in range(CHUNKS_PER_STEP):
            o = recurrence(c, p)
            p_next = front(c + 1) if c + 1 < CHUNKS_PER_STEP else None
            back(p, o)
            p = p_next

    rows = CHUNK * CHUNKS_PER_STEP
    zspec = lambda cb: pl.BlockSpec((rows, HW), lambda i, cb=cb: (i, cb))
    return pl.pallas_call(
        body, name="hgrn_fwd", grid=(s // rows,),
        out_shape=[jax.ShapeDtypeStruct((s, 2 * HW), BF16), jax.ShapeDtypeStruct((s, HW), F32),
                   jax.ShapeDtypeStruct((n_chunks, HEADS, HD, HD), F32)],
        in_specs=[zspec(0), zspec(1), zspec(2), zspec(3), _full((2, HW)), _full((HEADS, HD))],
        out_specs=[pl.BlockSpec((rows, HW), lambda i: (i, 0)), pl.BlockSpec((rows, HW), lambda i: (i, 0)),
                   pl.BlockSpec((CHUNKS_PER_STEP, HEADS, HD, HD), lambda i: (i, 0, 0, 0))],
        scratch_shapes=[pltpu.VMEM((HEADS, HD, HD), F32), pltpu.VMEM((rows, HW), F32)],
        compiler_params=_params(("arbitrary",)),
    )(z, z, z, z, lb_logits, gn)


def _pool_counts(tile_idx, tm):
    t = tile_idx * tm + lax.broadcasted_iota(jnp.int32, (tm, 1), 0)
    return [1.0 / jnp.minimum(t + 1, w).astype(F32) for w in POOL_WINDOWS]


def _pool_fwd(z, w_pool, scale, mixed_in, deps):
    s = z.shape[0]
    tm = min(ROW_TILE, s)

    def body(p_ref, w_ref, sc_ref, mixin_ref, ob_ref, pooled_ref, ext):
        i = pl.program_id(0)

        @pl.when(i == 0)
        def _():
            ext[0:POOL_HALO, :] = jnp.zeros((POOL_HALO, HW), F32)

        @pl.when(i > 0)
        def _():
            ext[0:POOL_HALO, :] = ext[tm:tm + POOL_HALO, :]

        ext[POOL_HALO:POOL_HALO + tm, :] = p_ref[...]
        inv = _pool_counts(i, tm)
        for g, w in enumerate(POOL_WINDOWS):
            sl = slice(HD * g, HD * (g + 1))
            p = ext[POOL_HALO:POOL_HALO + tm, sl]
            win = p
            for d in range(1, w):
                win = win + ext[POOL_HALO - d:POOL_HALO - d + tm, sl]
            pooled = (win * inv[g] - p).astype(BF16)
            pooled_ref[:, sl] = pooled
            ob_ref[:, sl] = (_mm(pooled, w_ref[g]) * sc_ref[:, sl]).astype(BF16)

    return _call_behind(
        deps, body, name="pool_fwd", grid=(s // tm,),
        out_shape=[jax.ShapeDtypeStruct((s, 2 * HW), BF16), jax.ShapeDtypeStruct((s, HW), BF16)],
        in_specs=[pl.BlockSpec((tm, HW), lambda i: (i, 4)), _full((HEADS, HD, HD)), _full((1, HW)), ANY_SPACE],
        out_specs=[pl.BlockSpec((tm, HW), lambda i: (i, 1)), pl.BlockSpec((tm, HW), lambda i: (i, 0))],
        scratch_shapes=[pltpu.VMEM((tm + POOL_HALO, HW), F32)],
        input_output_aliases={3: 0},
        compiler_params=_params(("arbitrary",)),
    )(z, w_pool, scale, mixed_in)


def _mem_kv(mem, g, wk, wv, deps):
    def body(m_ref, g_ref, wk_ref, wv_ref, hm_ref, k_ref, v_ref):
        m = m_ref[...]
        hm = (m * _rms(m) * g_ref[...]).astype(BF16)
        hm_ref[...] = hm
        k_ref[...] = _mm(hm, wk_ref[...]).astype(BF16)
        v_ref[...] = _mm(hm, wv_ref[...]).astype(BF16)

    shp = jax.ShapeDtypeStruct((MEM_LEN, D_MODEL), BF16)
    return _call_behind(
        deps, body, name="mem_kv", out_shape=[shp, shp, shp],
        in_specs=[VMEM_WHOLE] * 4, out_specs=[VMEM_WHOLE] * 3,
        compiler_params=_params(),
    )(mem, g, wk, wv)


def _softmax_rows(sc):
    e = jnp.exp(sc - jnp.max(sc, axis=-1, keepdims=True))
    return e / jnp.sum(e, axis=-1, keepdims=True)


def _mix_xattn_fwd(x0, mixed, w_out, g, wq, xk, xv, wo_t, deps):
    s = x0.shape[0]
    tm = min(ROW_TILE, s)
    scale = XHD ** -0.5

    def body(x_ref, mix_ref, wout_ref, g_ref, wq_ref, k_ref, v_ref, wo_ref, x1_ref, o_ref, hq_ref, q_ref, att_ref):
        xv_ = x_ref[...] + _mm(mix_ref[...], wout_ref[...])
        x1_ref[...] = xv_
        hq = (xv_ * _rms(xv_) * g_ref[...]).astype(BF16)
        hq_ref[...] = hq
        q_ref[...] = (_mm(hq, wq_ref[...]) * scale).astype(BF16)
        heads = [slice(XHD * h, XHD * (h + 1)) for h in range(HEADS)]
        scores = [_mm_nt(q_ref[:, sl], k_ref[:, sl]) for sl in heads]
        probs = [_softmax_rows(sc) for sc in scores]
        for sl, p in zip(heads, probs):
            att_ref[:, sl] = _mm(p, v_ref[:, sl]).astype(BF16)
        o_ref[...] = xv_ + _mm_nt(att_ref[...], wo_ref[...])

    row_f32 = pl.BlockSpec((tm, D_MODEL), lambda i: (i, 0))
    bshape = jax.ShapeDtypeStruct((s, D_MODEL), BF16)
    fshape = jax.ShapeDtypeStruct((s, D_MODEL), F32)
    return _call_behind(
        deps, body, name="mix_xattn_fwd", grid=(s // tm,),
        out_shape=[fshape, fshape, bshape, bshape, bshape],
        in_specs=[row_f32, row_f32, VMEM_WHOLE, _full((1, D_MODEL)), VMEM_WHOLE, VMEM_WHOLE, VMEM_WHOLE, VMEM_WHOLE],
        out_specs=[row_f32] * 5,
        compiler_params=_params(("parallel",)),
    )(x0, mixed, w_out, g, wq, xk, xv, wo_t)


def _mlp_fwd_loss(x, g, w1, w2, gf, target):
    s = x.shape[0]
    tm = min(ROW_TILE, s)

    def body(x_ref, g_ref, w1_ref, w2_ref, gf_ref, t_ref, dx_ref, dx16_ref, u_ref, hf_ref, slot_ref):
        @pl.when(pl.program_id(0) == 0)
        def _():
            slot_ref[...] = jnp.zeros_like(slot_ref)

        xv = x_ref[...]
        hf = (xv * _rms(xv) * g_ref[...]).astype(BF16)
        hf_ref[...] = hf
        a_next = _mm(hf, w1_ref[0])
        for j in range(N_DEV):
            a = jnp.maximum(a_next, 0.0)
            if j + 1 < N_DEV:
                a_next = _mm(hf, w1_ref[j + 1])
            u_ref[:, FF_BLK * j:FF_BLK * (j + 1)] = (a * a).astype(BF16)
        acc = xv + _mm(u_ref[...], w2_ref[...])
        gfv = gf_ref[...]
        r = _rms(acc)
        n = acc * r
        err = n * gfv - t_ref[...]
        slot_ref[1:2, :] += jnp.sum(jnp.mean(err * err, axis=-1, keepdims=True), axis=0, keepdims=True) * 0.5
        dy = err * (1.0 / D_MODEL)
        slot_ref[0:1, :] += jnp.sum(dy * n, axis=0, keepdims=True)
        dn = dy * gfv
        dx = r * (dn - n * jnp.mean(dn * n, axis=-1, keepdims=True))
        dx_ref[...] = dx
        dx16_ref[...] = dx.astype(BF16)

    row_f32 = pl.BlockSpec((tm, D_MODEL), lambda i: (i, 0))
    return pl.pallas_call(
        body, name="mlp_fwd_loss", grid=(s // tm,),
        out_shape=[jax.ShapeDtypeStruct((s, D_MODEL), F32), jax.ShapeDtypeStruct((s, D_MODEL), BF16),
                   jax.ShapeDtypeStruct((s, D_FF), BF16), jax.ShapeDtypeStruct((s, D_MODEL), BF16),
                   jax.ShapeDtypeStruct((SLOT, D_MODEL), F32)],
        in_specs=[row_f32, _full((1, D_MODEL)), VMEM_WHOLE, VMEM_WHOLE, _full((1, D_MODEL)), row_f32],
        out_specs=[row_f32, row_f32, pl.BlockSpec((tm, D_FF), lambda i: (i, 0)), row_f32, _full((SLOT, D_MODEL))],
        compiler_params=_params(("arbitrary",)),
    )(x, g, w1, w2, gf, target)


def _zero_slot(slot_ref):
    @pl.when(pl.program_id(0) == 0)
    def _():
        slot_ref[...] = jnp.zeros_like(slot_ref)


def _mlp_bwd(dx3, u, x2, g, w1, w2, deps):
    s = x2.shape[0]
    tm = min(ROW_TILE // 2, s)

    def body(d_ref, u_ref, x_ref, g_ref, w1_ref, w2_ref, da_ref, dx_ref, slot_ref):
        _zero_slot(slot_ref)
        d = d_ref[...]
        d16 = d.astype(BF16)
        du_next = _mm_nt(d16, w2_ref[0])
        dhf = jnp.zeros((tm, D_MODEL), F32)
        for j in range(N_DEV):
            sl = slice(FF_BLK * j, FF_BLK * (j + 1))
            du = du_next
            if j + 1 < N_DEV:
                du_next = _mm_nt(d16, w2_ref[j + 1])
            u = u_ref[:, sl].astype(F32)
            da = (du * (2.0 * u * lax.rsqrt(jnp.maximum(u, TINY)))).astype(BF16)
            da_ref[:, sl] = da
            dhf = dhf + _mm_nt(da, w1_ref[j])
        dx, dg = _rms_bwd(x_ref[...], g_ref[...], dhf)
        dx_ref[...] = d + dx
        slot_ref[0:1, :] += dg

    row_f32 = pl.BlockSpec((tm, D_MODEL), lambda i: (i, 0))
    return _call_behind(
        deps, body, name="mlp_bwd", grid=(s // tm,),
        out_shape=[jax.ShapeDtypeStruct((s, D_FF), BF16), jax.ShapeDtypeStruct((s, D_MODEL), F32),
                   jax.ShapeDtypeStruct((SLOT, D_MODEL), F32)],
        in_specs=[row_f32, pl.BlockSpec((tm, D_FF), lambda i: (i, 0)), row_f32, _full((1, D_MODEL)),
                  VMEM_WHOLE, VMEM_WHOLE],
        out_specs=[pl.BlockSpec((tm, D_FF), lambda i: (i, 0)), row_f32, _full((SLOT, D_MODEL))],
        compiler_params=_params(("arbitrary",)),
    )(dx3, u, x2, g, w1, w2)


def _wgrad(a, b, name, col_blocks=False, update=None):
    s, m = a.shape
    n = b.shape[1]
    tm = 1280 if m % 1280 == 0 else min(1024, m)
    tn = min(1024, n)
    blk = n // N_DEV
    per_step = tn // blk if col_blocks else 1
    ts = min((4 if m * n >= D_MODEL * D_FF else 2) * ROW_TILE, s)
    n_s = s // ts
    grid = (m // tm, n // tn, n_s)

    def body(a_ref, b_ref, *rest):
        o_ref, acc = rest[-2], rest[-1]
        k = pl.program_id(2)

        @pl.when(k == 0)
        def _():
            acc[...] = jnp.zeros_like(acc)

        acc[...] += _mm_tn(a_ref[...], b_ref[...])
        if update is not None:
            r_ref, w_ref, m_ref, v_ref, g_ref, d_ref, nm_ref, nv_ref = rest[:8]
            g = r_ref[0].astype(F32)
            for d in range(1, N_DEV):
                g = g + r_ref[d].astype(F32)
            g_ref[...] = g
            d_ref[...], nm_ref[...], nv_ref[...] = _adam_update(g, w_ref[...], m_ref[...], v_ref[...])

        @pl.when(k == n_s - 1)
        def _():
            if col_blocks:
                for p in range(per_step):
                    o_ref[p] = acc[:, blk * p:blk * (p + 1)].astype(BF16)
            else:
                o_ref[...] = acc[...].astype(BF16)

    if col_blocks:
        out_shape = jax.ShapeDtypeStruct((N_DEV, m, blk), BF16)
        out_spec = pl.BlockSpec((per_step, tm, blk), lambda i, j, k: (j, i, 0))
    else:
        out_shape = jax.ShapeDtypeStruct((m, n), BF16)
        out_spec = pl.BlockSpec((tm, tn), lambda i, j, k: (i, j))
    in_specs = [pl.BlockSpec((ts, tm), lambda i, j, k: (k, i)), pl.BlockSpec((ts, tn), lambda i, j, k: (k, j))]
    out_shapes, out_specs, operands = [out_shape], [out_spec], [a, b]
    if update is not None:
        rows, cols = update[1].shape
        steps = grid[0] * grid[1] * grid[2]
        tr = rows // steps
        step = lambda i, j, k: (i * grid[1] + j) * grid[2] + k
        piece = pl.BlockSpec((tr, cols), lambda i, j, k: (step(i, j, k), 0))
        in_specs += [pl.BlockSpec((N_DEV, tr, cols), lambda i, j, k: (0, step(i, j, k), 0)), piece, piece, piece]
        out_shapes = [jax.ShapeDtypeStruct((rows, cols), F32)] * 4 + out_shapes
        out_specs = [piece] * 4 + out_specs
        operands += list(update)
    outs = pl.pallas_call(
        body, name=name, grid=grid, out_shape=out_shapes, in_specs=in_specs, out_specs=out_specs,
        scratch_shapes=[pltpu.VMEM((tm, tn), F32)],
        compiler_params=_params(("parallel", "parallel", "arbitrary")),
    )(*operands)
    return outs[0] if update is None else (outs[4], tuple(outs[:4]))


def _xattn_bwd(dx2, x1, g, q, xk, xv, wq, wo_t, deps):
    s = x1.shape[0]
    tm = min(ROW_TILE, s)
    scale = XHD ** -0.5

    def body(d_ref, x_ref, g_ref, q_ref, k_ref, v_ref, wq_ref, wo_ref, dx_ref, dx16_ref, dq_ref, dk_ref, dv_ref, slot_ref,
             datt):
        _zero_slot(slot_ref)

        @pl.when(pl.program_id(0) == 0)
        def _():
            dk_ref[...] = jnp.zeros_like(dk_ref)
            dv_ref[...] = jnp.zeros_like(dv_ref)

        d = d_ref[...]
        datt[...] = _mm(d, wo_ref[...]).astype(BF16)
        heads = [slice(XHD * h, XHD * (h + 1)) for h in range(HEADS)]
        scores = [_mm_nt(q_ref[:, sl], k_ref[:, sl]) for sl in heads]
        dps = [_mm_nt(datt[:, sl], v_ref[:, sl]) for sl in heads]
        probs = [_softmax_rows(sc) for sc in scores]
        dss = [(p * (dp - jnp.sum(dp * p, axis=-1, keepdims=True))).astype(BF16) for p, dp in zip(probs, dps)]
        for sl, p, ds in zip(heads, probs, dss):
            dq_ref[:, sl] = (_mm(ds, k_ref[:, sl]) * scale).astype(BF16)
            dk_ref[:, sl] += _mm_tn(ds, q_ref[:, sl])
            dv_ref[:, sl] += _mm_tn(p, datt[:, sl])
        dx, dg = _rms_bwd(x_ref[...], g_ref[...], _mm_nt(dq_ref[...], wq_ref[...]))
        dx_ref[...] = d + dx
        dx16_ref[...] = (d + dx).astype(BF16)
        slot_ref[0:1, :] += dg

    row_f32 = pl.BlockSpec((tm, D_MODEL), lambda i: (i, 0))
    kv = jax.ShapeDtypeStruct((MEM_LEN, D_MODEL), F32)
    tokens16 = jax.ShapeDtypeStruct((s, D_MODEL), BF16)
    return _call_behind(
        deps, body, name="xattn_bwd", grid=(s // tm,),
        out_shape=[jax.ShapeDtypeStruct((s, D_MODEL), F32), tokens16, tokens16, kv, kv,
                   jax.ShapeDtypeStruct((SLOT, D_MODEL), F32)],
        in_specs=[row_f32, row_f32, _full((1, D_MODEL)), row_f32, VMEM_WHOLE, VMEM_WHOLE, VMEM_WHOLE, VMEM_WHOLE],
        out_specs=[row_f32, row_f32, row_f32, _full((MEM_LEN, D_MODEL)), _full((MEM_LEN, D_MODEL)),
                   _full((SLOT, D_MODEL))],
        scratch_shapes=[pltpu.VMEM((tm, D_MODEL), BF16)],
        compiler_params=_params(("arbitrary",)),
    )(dx2, x1, g, q, xk, xv, wq, wo_t)


def _mem_bwd(mem, g, hm, dxk, dxv, wk, wv):
    def body(m_ref, g_ref, hm_ref, dk_ref, dv_ref, wk_ref, wv_ref, dwk_ref, dwv_ref, slot_ref):
        dk, dv = dk_ref[...], dv_ref[...]
        hm_ = hm_ref[...]
        dwk_ref[...] = _mm_tn(hm_, dk).astype(BF16)
        dwv_ref[...] = _mm_tn(hm_, dv).astype(BF16)
        _, dg = _rms_bwd(m_ref[...], g_ref[...], _mm_nt(dk, wk_ref[...]) + _mm_nt(dv, wv_ref[...]))
        slot_ref[...] = jnp.zeros_like(slot_ref)
        slot_ref[0:1, :] = dg

    wshape = jax.ShapeDtypeStruct((D_MODEL, D_MODEL), BF16)
    return pl.pallas_call(
        body, name="mem_bwd", out_shape=[wshape, wshape, jax.ShapeDtypeStruct((SLOT, D_MODEL), F32)],
        in_specs=[VMEM_WHOLE] * 7, out_specs=[VMEM_WHOLE] * 3,
        compiler_params=_params(),
    )(mem, g, hm, dxk, dxv, wk, wv)


def _pool_bwd(dx1, w_out, pooled, w_pool, scale, deps):
    s = dx1.shape[0]
    tm = min(ROW_TILE, s)
    n_t = s // tm

    def body(dx_ref, wo_ref, pl_ref, w_ref, sc_ref, dz_ref, dw_ref, slot_ref, ext, do_ref):
        i = pl.program_id(0)
        tile = n_t - 1 - i
        _zero_slot(slot_ref)
        do_ref[...] = _mm_nt(dx_ref[...], wo_ref[HW:2 * HW, :])

        @pl.when(i == 0)
        def _():
            dw_ref[...] = jnp.zeros_like(dw_ref)
            ext[tm:tm + POOL_HALO, :] = jnp.zeros((POOL_HALO, HW), F32)

        @pl.when(i > 0)
        def _():
            ext[tm:tm + POOL_HALO, :] = ext[0:POOL_HALO, :]

        inv = _pool_counts(tile, tm)
        dpooled = []
        for g in range(HEADS):
            sl = slice(HD * g, HD * (g + 1))
            pooled_g = pl_ref[:, sl]
            do = do_ref[:, sl]
            slot_ref[0:1, sl] += jnp.sum(_mm(pooled_g, w_ref[g]) * do, axis=0, keepdims=True)
            dy = (do * sc_ref[:, sl]).astype(BF16)
            dw_ref[g] += _mm_tn(pooled_g, dy)
            dpo = _mm_nt(dy, w_ref[g])
            dpooled.append(dpo)
            ext[0:tm, sl] = dpo * inv[g]
        for g, w in enumerate(POOL_WINDOWS):
            sl = slice(HD * g, HD * (g + 1))
            win = ext[0:tm, sl]
            for d in range(1, w):
                win = win + ext[d:d + tm, sl]
            dz_ref[:, sl] = (win - dpooled[g]).astype(BF16)

    return _call_behind(
        deps, body, name="pool_bwd", grid=(n_t,),
        out_shape=[jax.ShapeDtypeStruct((s, IN_WIDTH), BF16), jax.ShapeDtypeStruct((HEADS, HD, HD), F32),
                   jax.ShapeDtypeStruct((SLOT, D_MODEL), F32)],
        in_specs=[pl.BlockSpec((tm, D_MODEL), lambda i: (n_t - 1 - i, 0)), VMEM_WHOLE,
                  pl.BlockSpec((tm, HW), lambda i: (n_t - 1 - i, 0)), _full((HEADS, HD, HD)), _full((1, HW))],
        out_specs=[pl.BlockSpec((tm, HW), lambda i: (n_t - 1 - i, 4)), _full((HEADS, HD, HD)), _full((SLOT, D_MODEL))],
        scratch_shapes=[pltpu.VMEM((tm + POOL_HALO, HW), F32), pltpu.VMEM((tm, HW), F32)],
        compiler_params=_params(("arbitrary",)),
    )(dx1, w_out, pooled, w_pool, scale)


def _hgrn_bwd(z, o, dx1, w_out, states, lb_logits, gn, dz_in, deps):
    s = z.shape[0]
    n_chunks = s // CHUNK

    def body(zq_ref, zf_ref, zi_ref, zg_ref, o_ref, dx_ref, wo_ref, st_ref, lbl_ref, gn_ref, dzin_ref,
             dz_ref, dlb_ref, dgn_ref, dstate, b_scr, dlb_acc, do_ref):
        i = pl.program_id(0)

        @pl.when(i == 0)
        def _():
            dstate[...] = jnp.zeros_like(dstate)
            dlb_acc[...] = jnp.zeros_like(dlb_acc)
            dgn_ref[...] = jnp.zeros_like(dgn_ref)
            dlb_ref[...] = jnp.zeros_like(dlb_ref)

        do_ref[...] = _mm_nt(dx_ref[...], wo_ref[0:HW, :])
        lb = _sigmoid(lbl_ref[0:1, :] - lbl_ref[1:2, :])
        row, col = _chunk_masks()
        causal = col <= row
        tri = _ones_where(causal)
        upper = _ones_where(col >= row)
        strict_lower = _ones_where(col < row)
        in_sub, in_sub_head = _sub_chunk_masks(HW), _sub_chunk_masks(HD)
        gn_row = _lanes([gn_ref[h:h + 1, :] for h in range(HEADS)])
        sums = {"dlb": 0.0, "dgn": 0.0}

        def front(c):
            r0 = CHUNK * c
            rs = slice(r0, r0 + CHUNK)
            p = {"rs": rs}
            p["zq"] = zq_ref[rs, :]
            p["q"], p["sq"], p["sig"], p["f"] = _hgrn_gates(p["zq"], zf_ref[rs, :], lb)
            p["kk"] = 1.0 - p["f"]
            b = _tri_dot(tri, jnp.log(p["f"]), 3)
            b_scr[rs, :] = b
            p["v"] = zi_ref[rs, :]
            o, zg, doa = o_ref[rs, :], zg_ref[rs, :], do_ref[rs, :]
            sg = _sigmoid(zg)
            rms = _head_rms(o)
            n = o * rms
            don = doa * (zg * sg)
            sums["dgn"] = sums["dgn"] + jnp.sum(don * n, axis=0, keepdims=True)
            dn = don * gn_row
            p["d_o"] = rms * (dn - n * _head_mean(dn * n))
            dz_ref[rs, 3 * HW:4 * HW] = (doa * (n * gn_row) * (sg * (1.0 + zg * (1.0 - sg)))).astype(BF16)
            p["eq"], p["ek"] = _hgrn_decay_factors(b_scr, r0, b, in_sub)
            b_last = b_scr[r0 + CHUNK - 1:r0 + CHUNK, :]
            p["lam"], p["e_last"], p["lam_last"] = jnp.exp(b), jnp.exp(b_last - b), jnp.exp(b_last)
            p["qe"], p["qg"], p["kd"] = p["q"] * p["eq"], p["q"] * p["lam"], p["kk"] * p["e_last"]
            p["ke"] = [p["kk"] * e for e in p["ek"]]
            p["q16"] = [_per_sub_chunk(_head(p["qe"], h), in_sub_head).astype(BF16) for h in range(HEADS)]
            p["ke16"] = [_lanes([_head(p["ke"][j], h) for j in range(N_SUB)]).astype(BF16) for h in range(HEADS)]
            return p

        def recurrence(c, p):
            m = {k: [] for k in ("dv", "gq", "gk", "dqi", "dkd", "st")}
            a, da, dv_state = [], [], []
            for h in range(HEADS):
                vh, doh = _head(p["v"], h), _head(p["d_o"], h)
                st0, ds1 = st_ref[c, h], dstate[h]
                a.append(jnp.where(causal, _mm_nt(p["q16"][h], p["ke16"][h]), 0.0))
                da.append(jnp.where(causal, _mm_nt(doh, vh), 0.0))
                dv_state.append(_mm_nt(_head(p["kd"], h), ds1))
                m["dqi"].append(_mm(doh, st0))
                m["dkd"].append(_mm(vh, ds1))
                m["st"].append(jnp.sum(st0 * ds1, axis=0, keepdims=True))
                dstate[h] = ds1 * _head(p["lam_last"], h) + _mm_tn(doh, _head(p["qg"], h))
            for h in range(HEADS):
                m["dv"].append(_mm_tn(a[h], _head(p["d_o"], h)) + dv_state[h])
                m["gq"].append(_own_lane_block(_mm(da[h], p["ke16"][h]), in_sub_head))
                m["gk"].append(_mm_tn(da[h], p["q16"][h]))
            return m

        def back(p, m):
            rs = p["rs"]
            dz_ref[rs, 2 * HW:3 * HW] = _lanes(m["dv"]).astype(BF16)
            gq = _lanes(m["gq"])
            gk = [_lanes([m["gk"][h][:, HD * j:HD * (j + 1)] for h in range(HEADS)]) for j in range(N_SUB)]
            dq_inter = p["lam"] * _lanes(m["dqi"])
            dq = p["eq"] * gq + dq_inter
            dk_intra = sum(p["ek"][j] * gk[j] for j in range(N_SUB))
            dk_state = _lanes(m["dkd"]) * p["e_last"]
            db_intra = (p["qe"].astype(BF16).astype(F32) * gq
                        - sum(p["ke"][j].astype(BF16).astype(F32) * gk[j] for j in range(N_SUB)))
            dlf = (_tri_dot(upper, db_intra + p["q"] * dq_inter, 2) + _tri_dot(strict_lower, p["kk"] * dk_state, 2)
                   + p["lam_last"] * _lanes(m["st"]))
            sig, sq, zq = p["sig"], p["sq"], p["zq"]
            df = dlf / p["f"] - (dk_intra + dk_state)
            sums["dlb"] = sums["dlb"] + jnp.sum(df * (1.0 - sig), axis=0, keepdims=True)
            dz_ref[rs, HW:2 * HW] = (df * (1.0 - lb) * sig * (1.0 - sig)).astype(BF16)
            dz_ref[rs, 0:HW] = (dq * (sq * (1.0 + zq * (1.0 - sq)))).astype(BF16)

        p = front(CHUNKS_PER_STEP - 1)
        for c in reversed(range(CHUNKS_PER_STEP)):
            m = recurrence(c, p)
            p_next = front(c - 1) if c > 0 else None
            back(p, m)
            p = p_next
        dlb_acc[...] += sums["dlb"]
        for h in range(HEADS):
            dgn_ref[h:h + 1, 0:HD] += _head(sums["dgn"], h)

        @pl.when(i == n_steps - 1)
        def _():
            dl0 = dlb_acc[...] * lb * (1.0 - lb)
            dlb_ref[0:1, 0:HW] = dl0
            dlb_ref[1:2, 0:HW] = -dl0

    rows = CHUNK * CHUNKS_PER_STEP
    n_steps = s // rows
    rev = lambda i: n_steps - 1 - i
    zspec = lambda cb: pl.BlockSpec((rows, HW), lambda i, cb=cb: (rev(i), cb))
    slot = jax.ShapeDtypeStruct((SLOT, D_MODEL), F32)
    return _call_behind(
        deps, body, name="hgrn_bwd", grid=(n_steps,),
        out_shape=[jax.ShapeDtypeStruct((s, IN_WIDTH), BF16), slot, slot],
        in_specs=[zspec(0), zspec(1), zspec(2), zspec(3), pl.BlockSpec((rows, HW), lambda i: (rev(i), 0)),
                  pl.BlockSpec((rows, D_MODEL), lambda i: (rev(i), 0)), VMEM_WHOLE,
                  pl.BlockSpec((CHUNKS_PER_STEP, HEADS, HD, HD), lambda i: (rev(i), 0, 0, 0)), _full((2, HW)),
                  _full((HEADS, HD)), ANY_SPACE],
        out_specs=[pl.BlockSpec((rows, 4 * HW), lambda i: (rev(i), 0)), _full((SLOT, D_MODEL)), _full((SLOT, D_MODEL))],
        scratch_shapes=[pltpu.VMEM((HEADS, HD, HD), F32), pltpu.VMEM((rows, HW), F32), pltpu.VMEM((1, HW), F32),
                        pltpu.VMEM((rows, HW), F32)],
        input_output_aliases={10: 0},
        compiler_params=_params(("arbitrary",)),
    )(z, z, z, z, o, dx1, w_out, states, lb_logits, gn, dz_in)


def _in_bwd(dz, w_t, x0, g, dx1, deps):
    s = x0.shape[0]
    tm = min(WIDE_ROW_TILE, s)

    def body(dz_ref, w_ref, x_ref, g_ref, d_ref, dx_ref, slot_ref):
        _zero_slot(slot_ref)
        dx, dg = _rms_bwd(x_ref[...], g_ref[...], _mm(dz_ref[...], w_ref[...]))
        dx_ref[...] = d_ref[...] + dx
        slot_ref[0:1, :] += dg

    row_f32 = pl.BlockSpec((tm, D_MODEL), lambda i: (i, 0))
    return _call_behind(
        deps, body, name="in_bwd", grid=(s // tm,),
        out_shape=[jax.ShapeDtypeStruct((s, D_MODEL), F32), jax.ShapeDtypeStruct((SLOT, D_MODEL), F32)],
        in_specs=[pl.BlockSpec((tm, IN_WIDTH), lambda i: (i, 0)), VMEM_WHOLE, row_f32, _full((1, D_MODEL)), row_f32],
        out_specs=[row_f32, _full((SLOT, D_MODEL))],
        compiler_params=_params(("arbitrary",)),
    )(dz, w_t, x0, g, dx1)


def kernel(x, mem, norm_mix_g, w_in, lb_logits, hgrn_norm_g, w_pool, pool_scale, w_out, norm_x_g, norm_mem_g, w_xq, w_xk, w_xv, w_xo, norm_ffn_g, w_ff1, w_ff2, final_norm_g, loss_target, m_norm_mix_g, m_w_in, m_lb_logits, m_hgrn_norm_g, m_w_pool, m_pool_scale, m_w_out, m_norm_x_g, m_norm_mem_g, m_w_xq, m_w_xk, m_w_xv, m_w_xo, m_norm_ffn_g, m_w_ff1, m_w_ff2, m_final_norm_g, v_norm_mix_g, v_w_in, v_lb_logits, v_hgrn_norm_g, v_w_pool, v_pool_scale, v_w_out, v_norm_x_g, v_norm_mem_g, v_w_xq, v_w_xk, v_w_xv, v_w_xo, v_norm_ffn_g, v_w_ff1, v_w_ff2, v_final_norm_g):
    x0 = x[0]
    mem0 = mem[0]
    tgt = loss_target[0]
    gn = hgrn_norm_g[0]
    gfin = final_norm_g.reshape(1, D_MODEL)
    wp = w_pool[0]
    heads_2d = lambda w: w.reshape(D_MODEL // N_DEV, D_MODEL)
    xo_2d = lambda w: w.reshape(D_MODEL, D_MODEL // N_DEV)

    first = _all_gather_weights([w_in[0].T], [w_out[0], heads_2d(w_xq), heads_2d(w_xk), heads_2d(w_xv), xo_2d(w_xo).T,
                                              w_ff1[0], w_ff2[0]])
    win_t = first[0].reshape(IN_WIDTH, D_MODEL)
    ga_attn, ga_mlp = _gather_first_start([first[1:6], first[6:8]], "gather_first_start")

    z, h = _in_proj(x0, norm_mix_g, win_t, deps=[ga_attn[3]])
    mixed_a, o_pre, states = _hgrn_fwd(z, lb_logits, gn)
    lands = _split_wait(_gather_first_copies, ga_attn, o_pre, "gather_attn_first_wait")
    gb_attn = _gather_forward_start(lands, "gather_attn_forward_start")
    mixed, pooled = _pool_fwd(z, wp, pool_scale, mixed_a, deps=[gb_attn[3]])
    lands = _split_wait(_gather_forward_copies, gb_attn, pooled, "gather_attn_forward_wait")
    wout_f, wq_f, wk_f, wv_f, wo_t = (t.reshape(D_MODEL, D_MODEL) for t in lands)
    hm, xk, xv = _mem_kv(mem0, norm_mem_g, wk_f, wv_f, deps=[])
    x1, x2, hq, xq, att = _mix_xattn_fwd(x0, mixed, wout_f, norm_x_g, wq_f, xk, xv, wo_t, deps=[])
    lands = _split_wait(_gather_first_copies, ga_mlp, x2, "gather_mlp_first_wait")
    gb_mlp = _gather_forward_start(lands, "gather_mlp_forward_start")
    w1_b, w2_b = _split_wait(_gather_forward_copies, gb_mlp, gb_mlp[3], "gather_mlp_forward_wait")
    dx3, dx3_16, u, hf, slot_fin = _mlp_fwd_loss(x2, norm_ffn_g, w1_b, w2_b.reshape(D_FF, D_MODEL), gfin, tgt)

    rows = lambda t, r: t.reshape(N_DEV, r, D_MODEL)
    dw2 = _wgrad(u, dx3_16, "wgrad_ff2")
    ex_ff2 = _all_to_all_start([rows(dw2, FF_BLK)], [], "exchange_ff2_start")
    da, dx2, slot_ffn = _mlp_bwd(dx3, u, x2, norm_ffn_g, w1_b, w2_b, deps=[ex_ff2[3]])
    dw1 = _wgrad(hf, da, "wgrad_ff1", col_blocks=True)
    ex_ff1 = _all_to_all_start([dw1], [], "exchange_ff1_start")
    dx1, dx1_16, dxq, dxk, dxv, slot_x = _xattn_bwd(dx2, x1, norm_x_g, xq, xk, xv, wq_f, wo_t, deps=[ex_ff1[3]])
    dwo_t = _wgrad(dx2, att, "wgrad_xo")
    dwq = _wgrad(hq, dxq, "wgrad_xq")
    dwk, dwv, slot_mem = _mem_bwd(mem0, norm_mem_g, hm, dxk, dxv, wk_f, wv_f)
    ex_attn = _all_to_all_start([rows(dwq, 128), rows(dwk, 128), rows(dwv, 128), rows(dwo_t, 128)], [],
                                "exchange_attn_start")
    dwout = _wgrad(mixed, dx1_16, "wgrad_out")
    dz_pool, d_wpool, slot_ps = _pool_bwd(dx1_16, wout_f, pooled, wp, pool_scale, deps=[ex_attn[3]])
    small0 = jnp.concatenate([slot_x, slot_mem, slot_ffn, slot_fin, slot_ps], axis=0)
    ex_out = _all_to_all_start([rows(dwout, 128)], [small0, d_wpool], "exchange_out_start")
    dz, slot_lb, slot_gn = _hgrn_bwd(z, o_pre, dx1_16, wout_f, states, lb_logits, gn, dz_pool, deps=[ex_out[3]])
    (r_2,) = _split_wait(_all_to_all_copies(1), ex_ff2, dz, "exchange_ff2_wait")
    dwin_t, ff2_update = _wgrad(dz, h, "wgrad_in", update=(r_2, w_ff2[0], m_w_ff2[0], v_w_ff2[0]))
    ex_in = _all_to_all_start([rows(dwin_t, 320)], [], "exchange_in_start")
    grad_x, slot_mix = _in_bwd(dz, win_t, x0, norm_mix_g, dx1, deps=[ex_in[3]])
    small1 = jnp.concatenate([slot_mix, slot_lb, slot_gn], axis=0)
    ex_mix = _all_to_all_start([], [small1], "exchange_mix_start")

    out = {}
    out["w_ff2"] = ff2_update
    (r_1,) = _split_wait(_all_to_all_copies(1), ex_ff1, ex_mix[3], "exchange_ff1_wait")
    out["w_ff1"] = _sum_adamw(r_1, w_ff1[0], m_w_ff1[0], v_w_ff1[0], "adamw_ff1")
    r_q, r_k, r_v, r_o = _split_wait(_all_to_all_copies(4), ex_attn, out["w_ff1"][1], "exchange_attn_wait")
    sums = _sum_sources_whole([r_q, r_k, r_v, r_o], "sum_grad_attn")
    g_attn = [g.reshape(w_xq.shape) for g in sums[:3]] + [sums[3].T]
    attn = _adamw_whole([(g_attn[0], w_xq, m_w_xq, v_w_xq), (g_attn[1], w_xk, m_w_xk, v_w_xk),
                         (g_attn[2], w_xv, m_w_xv, v_w_xv),
                         (g_attn[3], xo_2d(w_xo), xo_2d(m_w_xo), xo_2d(v_w_xo))], "adamw_attn")
    for n, g, res in zip(("w_xq", "w_xk", "w_xv", "w_xo"), g_attn, attn):
        out[n] = (g, *res)
    r_out, r_small0, r_wpool = _split_wait(_all_to_all_copies(1), ex_out, attn[3][0], "exchange_out_wait")
    out["w_out"] = _sum_adamw(r_out, w_out[0], m_w_out[0], v_w_out[0], "adamw_out")
    (r_in,) = _split_wait(_all_to_all_copies(1), ex_in, out["w_out"][1], "exchange_in_wait")
    in_t = _sum_adamw(r_in, w_in[0].T, m_w_in[0].T, v_w_in[0].T, "adamw_in")
    out["w_in"] = tuple(t.T for t in in_t)
    (r_small1,) = _split_wait(_all_to_all_copies(0), ex_mix, in_t[1], "exchange_mix_wait")
    row = lambda t: t.reshape(1, -1)
    small_params = {
        "norm_mix_g": (norm_mix_g, m_norm_mix_g, v_norm_mix_g),
        "lb_logits": (lb_logits, m_lb_logits, v_lb_logits),
        "hgrn_norm_g": (hgrn_norm_g[0], m_hgrn_norm_g[0], v_hgrn_norm_g[0]),
        "pool_scale": (pool_scale, m_pool_scale, v_pool_scale),
        "norm_x_g": (norm_x_g, m_norm_x_g, v_norm_x_g),
        "norm_mem_g": (norm_mem_g, m_norm_mem_g, v_norm_mem_g),
        "norm_ffn_g": (norm_ffn_g, m_norm_ffn_g, v_norm_ffn_g),
        "final_norm_g": (row(final_norm_g), row(m_final_norm_g), row(v_final_norm_g)),
        "w_pool": (wp, m_w_pool[0], v_w_pool[0]),
    }
    loss, small_out = _small_update([r_small0, r_small1], r_wpool, small_params)
    out.update(small_out)

    shapes = dict(norm_mix_g=norm_mix_g, w_in=w_in, lb_logits=lb_logits, hgrn_norm_g=hgrn_norm_g, w_pool=w_pool,
                  pool_scale=pool_scale, w_out=w_out, norm_x_g=norm_x_g, norm_mem_g=norm_mem_g, w_xq=w_xq, w_xk=w_xk,
                  w_xv=w_xv, w_xo=w_xo, norm_ffn_g=norm_ffn_g, w_ff1=w_ff1, w_ff2=w_ff2, final_norm_g=final_norm_g)
    order = list(shapes)
    group = lambda k: [out[n][k].reshape(shapes[n].shape) for n in order]
    return (loss.reshape(()), grad_x.reshape(x.shape), *group(0), *group(1), *group(2), *group(3))
```

```python
import jax
import jax.numpy as jnp
from jax import lax
from jax.experimental import pallas as pl
from jax.experimental.pallas import tpu as pltpu

F32 = jnp.float32
BF16 = jnp.bfloat16

D_MODEL = 1024
N_DEV = 8
HEADS = 4
HD = 128
HW = HEADS * HD
IN_WIDTH = 5 * HW
XHD = 256
MEM_LEN = 256
D_FF = 4096
FF_BLK = D_FF // N_DEV
POOL_WINDOWS = (2, 4, 8, 16)
POOL_HALO = 16
CHUNK = 64
CHUNKS_PER_STEP = 8
SUB = 16
N_SUB = CHUNK // SUB
EXP_CAP = 80.0
EPS = 1e-6
TINY = 1e-30
ROW_TILE = 512
SLOT = 8
V7X_VMEM_LIMIT = 56 * 1024 * 1024

ADAM_LR = 0.001
ADAM_B1 = 0.9
ADAM_B2 = 0.999
ADAM_EPS = 1e-08
ADAM_WD = 0.01
ADAM_STEP = 10

MESH_ID = pl.DeviceIdType.MESH


def _params(sem=None, vmem=V7X_VMEM_LIMIT):
    return pltpu.CompilerParams(dimension_semantics=sem, vmem_limit_bytes=vmem)


def _mm(a, b):
    return lax.dot_general(a.astype(BF16), b.astype(BF16), (((1,), (0,)), ((), ())), preferred_element_type=F32)


def _mm_nt(a, b):
    return lax.dot_general(a.astype(BF16), b.astype(BF16), (((1,), (1,)), ((), ())), preferred_element_type=F32)


def _mm_tn(a, b):
    return lax.dot_general(a.astype(BF16), b.astype(BF16), (((0,), (0,)), ((), ())), preferred_element_type=F32)


def _sigmoid(x):
    return 1.0 / (1.0 + jnp.exp(-x))


def _rms(x):
    return lax.rsqrt(jnp.mean(x * x, axis=-1, keepdims=True) + EPS)


def _rms_bwd(x, g, dh):
    r = _rms(x)
    n = x * r
    dn = dh * g
    dx = r * (dn - n * jnp.mean(dn * n, axis=-1, keepdims=True))
    return dx, jnp.sum(dh * n, axis=0, keepdims=True)


def _tri_dot(tri, x, passes):
    acc = None
    rest = x
    for _ in range(passes):
        piece = rest.astype(BF16)
        part = lax.dot_general(tri, piece, (((1,), (0,)), ((), ())), preferred_element_type=F32)
        acc = part if acc is None else acc + part
        rest = rest - piece.astype(F32)
    return acc


def _adam_update(g, w, m, v):
    nm = ADAM_B1 * m + (1.0 - ADAM_B1) * g
    nv = ADAM_B2 * v + (1.0 - ADAM_B2) * (g * g)
    m_hat = nm / (1.0 - ADAM_B1 ** ADAM_STEP)
    v_hat = nv / (1.0 - ADAM_B2 ** ADAM_STEP)
    return -ADAM_LR * (m_hat / (jnp.sqrt(v_hat) + ADAM_EPS) + ADAM_WD * w), nm, nv


def _full(shape):
    return pl.BlockSpec(shape, lambda *_: (0,) * len(shape))


VMEM_WHOLE = pl.BlockSpec(memory_space=pltpu.VMEM)
ANY_SPACE = pl.BlockSpec(memory_space=pl.ANY)


def _mesh_pos():
    return lax.axis_index("x"), lax.axis_index("y"), lax.axis_index("c")


def _flat(px, py, pc):
    return 4 * px + 2 * py + pc


def _all_gather_weights(shards, cast_only):
    n, nc = len(shards), len(cast_only)
    step = 64

    def body(*refs):
        x_refs, c_refs = refs[:n], refs[n:n + nc]
        out_refs, cast_refs = refs[n + nc:2 * n + nc], refs[2 * n + nc:2 * n + 2 * nc]
        bufs = refs[2 * n + 2 * nc:3 * n + 2 * nc]
        send_sems, recv_sems, local_sems = refs[3 * n + 2 * nc:]
        _handshake(_peers_first_level())
        x, y, c = _mesh_pos()
        me, sibling = (x, y, c), (x, y, 1 - c)
        chips = [(1 - x, y), (x, 1 - y), (1 - x, 1 - y)]

        def copy(a, k, blk, to, src=None):
            rows = out_refs[a].at[_flat(*blk)]
            return pltpu.make_async_remote_copy(
                src_ref=rows if src is None else src, dst_ref=rows,
                send_sem=send_sems.at[7 * a + k], recv_sem=recv_sems.at[7 * a + k], device_id=to, device_id_type=MESH_ID)

        def cast_rows(src, dst, rows):
            def cast(i, carry):
                r0 = pl.multiple_of(i * step, step)
                dst[pl.ds(r0, step), :] = src[pl.ds(r0, step), :].astype(BF16)
                return carry
            lax.fori_loop(0, rows // step, cast, 0)

        first, mine = [], []
        for a in range(n):
            cast_rows(x_refs[a], bufs[a], shards[a].shape[0])
            mine.append(pltpu.make_async_copy(bufs[a], out_refs[a].at[_flat(*me)], local_sems.at[a]))
            first.append(copy(a, 0, me, sibling, src=bufs[a]))
            first += [copy(a, 1 + j, me, (*chip, c), src=bufs[a]) for j, chip in enumerate(chips)]
            for cp in [mine[-1]] + first[-4:]:
                cp.start()
        for a in range(nc):
            cast_rows(c_refs[a], cast_refs[a], cast_only[a].shape[0])
        passed = []
        for j, chip in enumerate(chips):
            for a in range(n):
                copy(a, 1 + j, (*chip, c), me).wait_recv()
                passed.append(copy(a, 4 + j, (*chip, c), sibling))
                passed[-1].start()
        for a in range(n):
            copy(a, 0, sibling, me).wait_recv()
            for j, chip in enumerate(chips):
                copy(a, 4 + j, (*chip, 1 - c), me).wait_recv()
        for cp in first + passed:
            cp.wait_send()
        for cp in mine:
            cp.wait()

    return pl.pallas_call(
        body, name="all_gather_w_in",
        out_shape=[jax.ShapeDtypeStruct((N_DEV,) + s.shape, BF16) for s in shards]
        + [jax.ShapeDtypeStruct(s.shape, BF16) for s in cast_only],
        in_specs=[VMEM_WHOLE] * (n + nc), out_specs=[ANY_SPACE] * n + [VMEM_WHOLE] * nc,
        scratch_shapes=[pltpu.VMEM(s.shape, BF16) for s in shards]
        + [pltpu.SemaphoreType.DMA((7 * n,)), pltpu.SemaphoreType.DMA((7 * n,)), pltpu.SemaphoreType.DMA((n,))],
        compiler_params=pltpu.CompilerParams(vmem_limit_bytes=V7X_VMEM_LIMIT, collective_id=GATHER_W_IN_ID),
    )(*shards, *cast_only)


HBM_SPEC = pl.BlockSpec(memory_space=pltpu.HBM)
SEM_SPEC = pl.BlockSpec(memory_space=pltpu.SEMAPHORE)
EFFECT = pltpu.SideEffectType.DATAFLOW_SIDE_EFFECTING
TOKEN = jax.ShapeDtypeStruct((8, 128), F32)


def _in_hbm(a):
    return pltpu.with_memory_space_constraint(a, pltpu.HBM)


START_IDS = {name: i for i, name in enumerate((
    "gather_first_start", "gather_attn_forward_start", "gather_mlp_forward_start", "exchange_ff2_start",
    "exchange_ff1_start", "exchange_attn_start", "exchange_out_start", "exchange_in_start", "exchange_mix_start"))}


GATHER_W_IN_ID = len(START_IDS)


def _handshake(peers):
    barrier = pltpu.get_barrier_semaphore()
    for peer in peers:
        pl.semaphore_signal(barrier, inc=1, device_id=peer, device_id_type=MESH_ID)
    pl.semaphore_wait(barrier, len(peers))


def _peers_all():
    x, y, c = _mesh_pos()
    return [(1 - x if k & 4 else x, 1 - y if k & 2 else y, 1 - c if k & 1 else c) for k in range(1, N_DEV)]


def _peers_first_level():
    x, y, c = _mesh_pos()
    return [(x, y, 1 - c), (1 - x, y, c), (x, 1 - y, c), (1 - x, 1 - y, c)]


def _peers_sibling():
    x, y, c = _mesh_pos()
    return [(x, y, 1 - c)]


def _split_start(copies_of, srcs, lands, n_sems, name, peers_of, collective_id):
    ns, nl, k = len(srcs), len(lands), len(n_sems)

    def body(*refs):
        _handshake(peers_of())
        src_refs, land_refs = refs[:ns], refs[ns:ns + nl]
        sems = refs[ns + nl:ns + nl + k]
        token = refs[-1]
        for cp in copies_of(src_refs, land_refs, sems):
            cp.start()
        token[...] = jnp.zeros_like(token)

    outs = pl.pallas_call(
        body, name=name,
        out_shape=[pltpu.SemaphoreType.DMA((q,)) for q in n_sems]
        + [pltpu.HBM(a.shape, a.dtype) for a in list(srcs) + list(lands)] + [TOKEN],
        in_specs=[HBM_SPEC] * (ns + nl),
        out_specs=[SEM_SPEC] * k + [HBM_SPEC] * (ns + nl) + [VMEM_WHOLE],
        input_output_aliases={i: k + i for i in range(ns + nl)},
        compiler_params=pltpu.CompilerParams(has_side_effects=EFFECT, collective_id=collective_id),
    )(*[_in_hbm(a) for a in list(srcs) + list(lands)])
    return outs[:k], outs[k:k + ns], outs[k + ns:k + ns + nl], outs[-1]


def _split_wait(copies_of, handle, after, name):
    sems, srcs, lands, _ = handle
    ns, nl, k = len(srcs), len(lands), len(sems)

    def body(*refs):
        src_refs, land_refs = refs[:ns], refs[ns:ns + nl]
        sem_refs = refs[ns + nl:ns + nl + k]
        for cp in copies_of(src_refs, land_refs, sem_refs):
            cp.wait()

    outs = pl.pallas_call(
        body, name=name,
        out_shape=[pltpu.HBM(a.shape, a.dtype) for a in list(srcs) + list(lands)],
        in_specs=[HBM_SPEC] * (ns + nl) + [SEM_SPEC] * k + [ANY_SPACE],
        out_specs=[HBM_SPEC] * (ns + nl),
        input_output_aliases={i: i for i in range(ns + nl)},
        compiler_params=pltpu.CompilerParams(has_side_effects=EFFECT),
    )(*srcs, *lands, *sems, after)
    return outs[ns:]


def _gather_first_copies(shard_refs, land_refs, sems):
    send_sems, recv_sems, local_sems = sems
    x, y, c = _mesh_pos()
    me = _flat(x, y, c)
    peers = [(x, y, 1 - c), (1 - x, y, c), (x, 1 - y, c), (1 - x, 1 - y, c)]
    copies = []
    for a, (shard, land) in enumerate(zip(shard_refs, land_refs)):
        copies.append(pltpu.make_async_copy(shard, land.at[me], local_sems.at[a]))
        for k, peer in enumerate(peers):
            copies.append(pltpu.make_async_remote_copy(
                src_ref=shard, dst_ref=land.at[me], send_sem=send_sems.at[4 * a + k], recv_sem=recv_sems.at[4 * a + k],
                device_id=peer, device_id_type=MESH_ID))
    return copies


def _gather_forward_copies(src_refs, land_refs, sems):
    del src_refs
    send_sems, recv_sems = sems
    x, y, c = _mesh_pos()
    chips = [(1 - x, y), (x, 1 - y), (1 - x, 1 - y)]
    copies = []
    for a, land in enumerate(land_refs):
        for j, chip in enumerate(chips):
            rows = land.at[_flat(*chip, c)]
            copies.append(pltpu.make_async_remote_copy(
                src_ref=rows, dst_ref=rows, send_sem=send_sems.at[3 * a + j], recv_sem=recv_sems.at[3 * a + j],
                device_id=(x, y, 1 - c), device_id_type=MESH_ID))
    return copies


def _gather_first_start(groups, name):
    shards = [s for g in groups for s in g]
    lands = [lax.empty((N_DEV,) + s.shape, s.dtype) for s in shards]
    bounds = [sum(len(g) for g in groups[:i]) for i in range(len(groups) + 1)]

    def copies_of(src_refs, land_refs, sems):
        copies = []
        for i in range(len(groups)):
            lo, hi = bounds[i], bounds[i + 1]
            copies += _gather_first_copies(src_refs[lo:hi], land_refs[lo:hi], sems[3 * i:3 * i + 3])
        return copies

    n_sems = tuple(q for g in groups for q in (4 * len(g), 4 * len(g), len(g)))
    sems, srcs, lands, token = _split_start(copies_of, shards, lands, n_sems, name, _peers_first_level, START_IDS[name])
    return [(sems[3 * i:3 * i + 3], srcs[bounds[i]:bounds[i + 1]], lands[bounds[i]:bounds[i + 1]], token)
            for i in range(len(groups))]


def _gather_forward_start(lands, name):
    n = len(lands)
    return _split_start(_gather_forward_copies, [], lands, (3 * n, 3 * n), name, _peers_sibling, START_IDS[name])


def _all_to_all_copies(n_scattered):
    def copies_of(src_refs, land_refs, sems):
        send_sems, recv_sems, local_sems = sems
        x, y, c = _mesh_pos()
        me = _flat(x, y, c)
        copies = []
        for a, (src, land) in enumerate(zip(src_refs, land_refs)):
            scattered = a < n_scattered
            copies.append(pltpu.make_async_copy(src.at[me] if scattered else src, land.at[me], local_sems.at[a]))
            for k in range(1, N_DEV):
                peer = (1 - x if k & 4 else x, 1 - y if k & 2 else y, 1 - c if k & 1 else c)
                copies.append(pltpu.make_async_remote_copy(
                    src_ref=src.at[_flat(*peer)] if scattered else src, dst_ref=land.at[me],
                    send_sem=send_sems.at[7 * a + k - 1], recv_sem=recv_sems.at[7 * a + k - 1],
                    device_id=peer, device_id_type=MESH_ID))
        return copies
    return copies_of


def _all_to_all_start(scattered, broadcast, name):
    srcs = list(scattered) + list(broadcast)
    lands = [lax.empty(a.shape, a.dtype) for a in scattered] + [lax.empty((N_DEV,) + a.shape, a.dtype) for a in broadcast]
    n = len(srcs)
    return _split_start(_all_to_all_copies(len(scattered)), srcs, lands, (7 * n, 7 * n, n), name, _peers_all,
                        START_IDS[name])


def _call_behind(deps, body, *, in_specs, **kwargs):
    n_in, n_dep = len(in_specs), len(deps)

    def body_without_deps(*refs):
        return body(*refs[:n_in], *refs[n_in + n_dep:])

    call = pl.pallas_call(body_without_deps, in_specs=list(in_specs) + [ANY_SPACE] * n_dep, **kwargs)
    return lambda *operands: call(*operands, *deps)


def _row_tile(rows):
    if rows <= 2 * 256:
        return rows
    for cand in (256, 128, 64, 32, 16):
        if rows % cand == 0:
            return cand
    return rows


def _adamw_whole(groups, name):
    n = len(groups)

    def body(*refs):
        for i in range(n):
            g_ref, w_ref, m_ref, v_ref = refs[4 * i:4 * i + 4]
            d_ref, nm_ref, nv_ref = refs[4 * n + 3 * i:4 * n + 3 * i + 3]
            d_ref[...], nm_ref[...], nv_ref[...] = _adam_update(g_ref[...], w_ref[...], m_ref[...], v_ref[...])

    outs = pl.pallas_call(
        body, name=name, out_shape=[jax.ShapeDtypeStruct(grp[0].shape, F32) for grp in groups for _ in range(3)],
        in_specs=[VMEM_WHOLE] * (4 * n), out_specs=[VMEM_WHOLE] * (3 * n),
        compiler_params=_params(),
    )(*[t for grp in groups for t in grp])
    return [outs[3 * i:3 * i + 3] for i in range(n)]


def _sum_sources_whole(recvs, name):
    n = len(recvs)

    def body(*refs):
        for r_ref, o_ref in zip(refs[:n], refs[n:]):
            acc = r_ref[0].astype(F32)
            for d in range(1, N_DEV):
                acc = acc + r_ref[d].astype(F32)
            o_ref[...] = acc

    return pl.pallas_call(
        body, name=name, out_shape=[jax.ShapeDtypeStruct(r.shape[1:], F32) for r in recvs],
        in_specs=[VMEM_WHOLE] * n, out_specs=[VMEM_WHOLE] * n,
        compiler_params=_params(),
    )(*recvs)


def _sum_adamw(recv, w, m, v, name):
    _, rows, cols = recv.shape
    tile = _row_tile(rows)

    def body(r_ref, w_ref, m_ref, v_ref, g_ref, d_ref, nm_ref, nv_ref):
        acc = r_ref[0].astype(F32)
        for d in range(1, N_DEV):
            acc = acc + r_ref[d].astype(F32)
        g_ref[...] = acc
        d_ref[...], nm_ref[...], nv_ref[...] = _adam_update(acc, w_ref[...], m_ref[...], v_ref[...])

    spec = pl.BlockSpec((tile, cols), lambda i: (i, 0))
    shp = jax.ShapeDtypeStruct((rows, cols), F32)
    return pl.pallas_call(
        body, name=name, grid=(rows // tile,), out_shape=[shp] * 4,
        in_specs=[pl.BlockSpec((N_DEV, tile, cols), lambda i: (0, i, 0)), spec, spec, spec], out_specs=[spec] * 4,
        compiler_params=_params(("parallel",)),
    )(recv, w, m, v)


SMALL_SLOTS = {"norm_x_g": (0, 0, 1, D_MODEL), "norm_mem_g": (0, 8, 1, D_MODEL), "norm_ffn_g": (0, 16, 1, D_MODEL),
               "final_norm_g": (0, 24, 1, D_MODEL), "pool_scale": (0, 32, 1, HW),
               "norm_mix_g": (1, 0, 1, D_MODEL), "lb_logits": (1, 8, 2, HW), "hgrn_norm_g": (1, 16, HEADS, HD)}
LOSS_ROW = 25
SMALL_ORDER = ("norm_mix_g", "lb_logits", "hgrn_norm_g", "pool_scale", "norm_x_g", "norm_mem_g", "norm_ffn_g",
               "final_norm_g", "w_pool")


def _small_update(srecvs, wprecv, params):
    flat = [t for n in SMALL_ORDER for t in params[n]]
    nb = len(srecvs)
    n_in = nb + 1 + len(flat)

    def body(*refs):
        s_refs, wp_ref = refs[0:nb], refs[nb]
        in_refs = refs[nb + 1:n_in]
        loss_ref = refs[n_in]
        out_refs = refs[n_in + 1:-nb]
        accs = refs[-nb:]
        for s_ref, acc in zip(s_refs, accs):
            total = s_ref[0]
            for d in range(1, N_DEV):
                total = total + s_ref[d]
            acc[...] = total
        loss_ref[...] = accs[0][LOSS_ROW:LOSS_ROW + 1, 0:1]
        for i, name in enumerate(SMALL_ORDER):
            w_ref, m_ref, v_ref = in_refs[3 * i:3 * i + 3]
            g_ref, d_ref, nm_ref, nv_ref = out_refs[4 * i:4 * i + 4]
            if name == "w_pool":
                g = wp_ref[0]
                for d in range(1, N_DEV):
                    g = g + wp_ref[d]
            else:
                buf, r0, nr, nc = SMALL_SLOTS[name]
                g = accs[buf][r0:r0 + nr, 0:nc]
            g_ref[...] = g
            d_ref[...], nm_ref[...], nv_ref[...] = _adam_update(g, w_ref[...], m_ref[...], v_ref[...])

    out_shape = [jax.ShapeDtypeStruct((1, 1), F32)]
    for n in SMALL_ORDER:
        out_shape += [jax.ShapeDtypeStruct(params[n][0].shape, F32)] * 4
    outs = pl.pallas_call(
        body, name="small_update", out_shape=out_shape,
        in_specs=[VMEM_WHOLE] * n_in, out_specs=[VMEM_WHOLE] * len(out_shape),
        scratch_shapes=[pltpu.VMEM(r.shape[1:], F32) for r in srecvs],
        compiler_params=_params(),
    )(*srecvs, wprecv, *flat)
    return outs[0], {n: outs[1 + 4 * i:5 + 4 * i] for i, n in enumerate(SMALL_ORDER)}


def _in_proj(x, g, w_t, deps):
    s = x.shape[0]
    tm = min(ROW_TILE, s)

    def body(x_ref, g_ref, w_ref, z_ref, h_ref):
        xv = x_ref[...]
        h = (xv * _rms(xv) * g_ref[...]).astype(BF16)
        h_ref[...] = h
        z_ref[...] = _mm_nt(h, w_ref[...])

    return _call_behind(
        deps, body, name="in_proj", grid=(s // tm,),
        out_shape=[jax.ShapeDtypeStruct((s, IN_WIDTH), F32), jax.ShapeDtypeStruct((s, D_MODEL), BF16)],
        in_specs=[pl.BlockSpec((tm, D_MODEL), lambda i: (i, 0)), _full((1, D_MODEL)), VMEM_WHOLE],
        out_specs=[pl.BlockSpec((tm, IN_WIDTH), lambda i: (i, 0)), pl.BlockSpec((tm, D_MODEL), lambda i: (i, 0))],
        compiler_params=_params(("parallel",)),
    )(x, g, w_t)


def _chunk_masks():
    row = lax.broadcasted_iota(jnp.int32, (CHUNK, CHUNK), 0)
    col = lax.broadcasted_iota(jnp.int32, (CHUNK, CHUNK), 1)
    return row, col


def _ones_where(mask):
    return jnp.where(mask, 1.0, 0.0).astype(BF16)


def _hgrn_gates(zq, zf, lb):
    sq = _sigmoid(zq)
    sig = _sigmoid(zf)
    f = lb + (1.0 - lb) * sig
    return zq * sq, sq, sig, f


def _sub_chunk_masks(width):
    trow = lax.broadcasted_iota(jnp.int32, (CHUNK, width), 0)
    return [(trow >= SUB * j) & (trow < SUB * (j + 1)) for j in range(N_SUB)]


def _head(a, h):
    return a[:, HD * h:HD * (h + 1)]


def _lanes(parts):
    return jnp.concatenate(parts, axis=1)


def _hgrn_decay_factors(b_scr, r0, b, in_sub):
    bases = [jnp.zeros((1, HW), F32)] + [b_scr[r0 + SUB * j - 1:r0 + SUB * j, :] for j in range(1, N_SUB)]
    own_base = bases[N_SUB - 1]
    for j in range(N_SUB - 2, -1, -1):
        own_base = jnp.where(in_sub[j], bases[j], own_base)
    eq = jnp.exp(b - own_base)
    ek = []
    for j in range(N_SUB):
        upto = SUB * (j + 1)
        e = jnp.exp(jnp.minimum(bases[j] - b[0:upto], EXP_CAP))
        ek.append(e if upto == CHUNK else jnp.concatenate([e, jnp.zeros((CHUNK - upto, HW), F32)], axis=0))
    return eq, ek


def _per_sub_chunk(x, in_sub):
    return _lanes([jnp.where(in_sub[j], x, 0.0) for j in range(N_SUB)])


def _own_lane_block(a, in_sub):
    out = a[:, HD * (N_SUB - 1):HD * N_SUB]
    for j in range(N_SUB - 2, -1, -1):
        out = jnp.where(in_sub[j], a[:, HD * j:HD * (j + 1)], out)
    return out


def _head_rms(o):
    return _lanes([jnp.broadcast_to(_rms(_head(o, h)), (CHUNK, HD)) for h in range(HEADS)])


def _head_mean(a):
    return _lanes([jnp.broadcast_to(jnp.mean(_head(a, h), axis=-1, keepdims=True), (CHUNK, HD)) for h in range(HEADS)])


def _hgrn_fwd(z, lb_logits, gn):
    s = z.shape[0]
    n_chunks = s // CHUNK

    def body(zq_ref, zf_ref, zi_ref, zg_ref, lbl_ref, gn_ref, oa_ref, o_ref, st_ref, state, b_scr):
        @pl.when(pl.program_id(0) == 0)
        def _():
            state[...] = jnp.zeros_like(state)

        lb = _sigmoid(lbl_ref[0:1, :] - lbl_ref[1:2, :])
        row, col = _chunk_masks()
        causal = col <= row
        tri = _ones_where(causal)
        in_sub, in_sub_head = _sub_chunk_masks(HW), _sub_chunk_masks(HD)
        gn_row = _lanes([gn_ref[h:h + 1, :] for h in range(HEADS)])
        def front(c):
            r0 = CHUNK * c
            rs = slice(r0, r0 + CHUNK)
            q, _, _, f = _hgrn_gates(zq_ref[rs, :], zf_ref[rs, :], lb)
            kk = 1.0 - f
            b = _tri_dot(tri, jnp.log(f), 3)
            b_scr[rs, :] = b
            eq, ek = _hgrn_decay_factors(b_scr, r0, b, in_sub)
            b_last = b_scr[r0 + CHUNK - 1:r0 + CHUNK, :]
            qe = q * eq
            return {"rs": rs, "v": zi_ref[rs, :], "qg": q * jnp.exp(b), "kd": kk * jnp.exp(b_last - b),
                    "lam_last": jnp.exp(b_last),
                    "q16": [_per_sub_chunk(_head(qe, h), in_sub_head).astype(BF16) for h in range(HEADS)],
                    "ke16": [_lanes([_head(kk * e, h) for e in ek]).astype(BF16) for h in range(HEADS)]}

        def recurrence(c, p):
            st_ref[c] = state[...]
            a, o_inter = [], []
            for h in range(HEADS):
                vh, st = _head(p["v"], h), state[h]
                a.append(jnp.where(causal, _mm_nt(p["q16"][h], p["ke16"][h]), 0.0))
                o_inter.append(_mm_nt(_head(p["qg"], h), st))
                state[h] = st * _head(p["lam_last"], h) + _mm_tn(vh, _head(p["kd"], h))
            return _lanes([_mm(a[h], _head(p["v"], h)) + o_inter[h] for h in range(HEADS)])

        def back(p, o):
            rs = p["rs"]
            o_ref[rs, :] = o
            zg = zg_ref[rs, :]
            oa_ref[rs, :] = (o * _head_rms(o) * gn_row * zg * _sigmoid(zg)).astype(BF16)

        p = front(0)
        for c in range(CHUNKS_PER_STEP):
            o = recurrence(c, p)
            p_next = front(c + 1) if c + 1 < CHUNKS_PER_STEP else None
            back(p, o)
            p = p_next

    rows = CHUNK * CHUNKS_PER_STEP
    zspec = lambda cb: pl.BlockSpec((rows, HW), lambda i, cb=cb: (i, cb))
    return pl.pallas_call(
        body, name="hgrn_fwd", grid=(s // rows,),
        out_shape=[jax.ShapeDtypeStruct((s, 2 * HW), BF16), jax.ShapeDtypeStruct((s, HW), F32),
                   jax.ShapeDtypeStruct((n_chunks, HEADS, HD, HD), F32)],
        in_specs=[zspec(0), zspec(1), zspec(2), zspec(3), _full((2, HW)), _full((HEADS, HD))],
        out_specs=[pl.BlockSpec((rows, HW), lambda i: (i, 0)), pl.BlockSpec((rows, HW), lambda i: (i, 0)),
                   pl.BlockSpec((CHUNKS_PER_STEP, HEADS, HD, HD), lambda i: (i, 0, 0, 0))],
        scratch_shapes=[pltpu.VMEM((HEADS, HD, HD), F32), pltpu.VMEM((rows, HW), F32)],
        compiler_params=_params(("arbitrary",)),
    )(z, z, z, z, lb_logits, gn)


def _pool_counts(tile_idx, tm):
    t = tile_idx * tm + lax.broadcasted_iota(jnp.int32, (tm, 1), 0)
    return [1.0 / jnp.minimum(t + 1, w).astype(F32) for w in POOL_WINDOWS]


def _pool_fwd(z, w_pool, scale, mixed_in, deps):
    s = z.shape[0]
    tm = min(ROW_TILE, s)

    def body(p_ref, w_ref, sc_ref, mixin_ref, ob_ref, pooled_ref, ext):
        i = pl.program_id(0)

        @pl.when(i == 0)
        def _():
            ext[0:POOL_HALO, :] = jnp.zeros((POOL_HALO, HW), F32)

        @pl.when(i > 0)
        def _():
            ext[0:POOL_HALO, :] = ext[tm:tm + POOL_HALO, :]

        ext[POOL_HALO:POOL_HALO + tm, :] = p_ref[...]
        inv = _pool_counts(i, tm)
        for g, w in enumerate(POOL_WINDOWS):
            sl = slice(HD * g, HD * (g + 1))
            p = ext[POOL_HALO:POOL_HALO + tm, sl]
            win = p
            for d in range(1, w):
                win = win + ext[POOL_HALO - d:POOL_HALO - d + tm, sl]
            pooled = (win * inv[g] - p).astype(BF16)
            pooled_ref[:, sl] = pooled
            ob_ref[:, sl] = (_mm(pooled, w_ref[g]) * sc_ref[:, sl]).astype(BF16)

    return _call_behind(
        deps, body, name="pool_fwd", grid=(s // tm,),
        out_shape=[jax.ShapeDtypeStruct((s, 2 * HW), BF16), jax.ShapeDtypeStruct((s, HW), BF16)],
        in_specs=[pl.BlockSpec((tm, HW), lambda i: (i, 4)), _full((HEADS, HD, HD)), _full((1, HW)), ANY_SPACE],
        out_specs=[pl.BlockSpec((tm, HW), lambda i: (i, 1)), pl.BlockSpec((tm, HW), lambda i: (i, 0))],
        scratch_shapes=[pltpu.VMEM((tm + POOL_HALO, HW), F32)],
        input_output_aliases={3: 0},
        compiler_params=_params(("arbitrary",)),
    )(z, w_pool, scale, mixed_in)


def _mem_kv(mem, g, wk, wv, deps):
    def body(m_ref, g_ref, wk_ref, wv_ref, hm_ref, k_ref, v_ref):
        m = m_ref[...]
        hm = (m * _rms(m) * g_ref[...]).astype(BF16)
        hm_ref[...] = hm
        k_ref[...] = _mm(hm, wk_ref[...]).astype(BF16)
        v_ref[...] = _mm(hm, wv_ref[...]).astype(BF16)

    shp = jax.ShapeDtypeStruct((MEM_LEN, D_MODEL), BF16)
    return _call_behind(
        deps, body, name="mem_kv", out_shape=[shp, shp, shp],
        in_specs=[VMEM_WHOLE] * 4, out_specs=[VMEM_WHOLE] * 3,
        compiler_params=_params(),
    )(mem, g, wk, wv)


def _softmax_rows(sc):
    e = jnp.exp(sc - jnp.max(sc, axis=-1, keepdims=True))
    return e / jnp.sum(e, axis=-1, keepdims=True)


def _mix_xattn_fwd(x0, mixed, w_out, g, wq, xk, xv, wo_t, deps):
    s = x0.shape[0]
    tm = min(ROW_TILE, s)
    scale = XHD ** -0.5

    def body(x_ref, mix_ref, wout_ref, g_ref, wq_ref, k_ref, v_ref, wo_ref, x1_ref, o_ref, hq_ref, q_ref, att_ref):
        xv_ = x_ref[...] + _mm(mix_ref[...], wout_ref[...])
        x1_ref[...] = xv_
        hq = (xv_ * _rms(xv_) * g_ref[...]).astype(BF16)
        hq_ref[...] = hq
        q_ref[...] = (_mm(hq, wq_ref[...]) * scale).astype(BF16)
        heads = [slice(XHD * h, XHD * (h + 1)) for h in range(HEADS)]
        scores = [_mm_nt(q_ref[:, sl], k_ref[:, sl]) for sl in heads]
        probs = [_softmax_rows(sc) for sc in scores]
        for sl, p in zip(heads, probs):
            att_ref[:, sl] = _mm(p, v_ref[:, sl]).astype(BF16)
        o_ref[...] = xv_ + _mm_nt(att_ref[...], wo_ref[...])

    row_f32 = pl.BlockSpec((tm, D_MODEL), lambda i: (i, 0))
    bshape = jax.ShapeDtypeStruct((s, D_MODEL), BF16)
    fshape = jax.ShapeDtypeStruct((s, D_MODEL), F32)
    return _call_behind(
        deps, body, name="mix_xattn_fwd", grid=(s // tm,),
        out_shape=[fshape, fshape, bshape, bshape, bshape],
        in_specs=[row_f32, row_f32, VMEM_WHOLE, _full((1, D_MODEL)), VMEM_WHOLE, VMEM_WHOLE, VMEM_WHOLE, VMEM_WHOLE],
        out_specs=[row_f32] * 5,
        compiler_params=_params(("parallel",)),
    )(x0, mixed, w_out, g, wq, xk, xv, wo_t)


def _mlp_fwd_loss(x, g, w1, w2, gf, target):
    s = x.shape[0]
    tm = min(ROW_TILE, s)

    def body(x_ref, g_ref, w1_ref, w2_ref, gf_ref, t_ref, dx_ref, dx16_ref, u_ref, hf_ref, slot_ref):
        @pl.when(pl.program_id(0) == 0)
        def _():
            slot_ref[...] = jnp.zeros_like(slot_ref)

        xv = x_ref[...]
        hf = (xv * _rms(xv) * g_ref[...]).astype(BF16)
        hf_ref[...] = hf
        a_next = _mm(hf, w1_ref[0])
        for j in range(N_DEV):
            a = jnp.maximum(a_next, 0.0)
            if j + 1 < N_DEV:
                a_next = _mm(hf, w1_ref[j + 1])
            u_ref[:, FF_BLK * j:FF_BLK * (j + 1)] = (a * a).astype(BF16)
        acc = xv + _mm(u_ref[...], w2_ref[...])
        gfv = gf_ref[...]
        r = _rms(acc)
        n = acc * r
        err = n * gfv - t_ref[...]
        slot_ref[1:2, :] += jnp.sum(jnp.mean(err * err, axis=-1, keepdims=True), axis=0, keepdims=True) * 0.5
        dy = err * (1.0 / D_MODEL)
        slot_ref[0:1, :] += jnp.sum(dy * n, axis=0, keepdims=True)
        dn = dy * gfv
        dx = r * (dn - n * jnp.mean(dn * n, axis=-1, keepdims=True))
        dx_ref[...] = dx
        dx16_ref[...] = dx.astype(BF16)

    row_f32 = pl.BlockSpec((tm, D_MODEL), lambda i: (i, 0))
    return pl.pallas_call(
        body, name="mlp_fwd_loss", grid=(s // tm,),
        out_shape=[jax.ShapeDtypeStruct((s, D_MODEL), F32), jax.ShapeDtypeStruct((s, D_MODEL), BF16),
                   jax.ShapeDtypeStruct((s, D_FF), BF16), jax.ShapeDtypeStruct((s, D_MODEL), BF16),
                   jax.ShapeDtypeStruct((SLOT, D_MODEL), F32)],
        in_specs=[row_f32, _full((1, D_MODEL)), VMEM_WHOLE, VMEM_WHOLE, _full((1, D_MODEL)), row_f32],
        out_specs=[row_f32, row_f32, pl.BlockSpec((tm, D_FF), lambda i: (i, 0)), row_f32, _full((SLOT, D_MODEL))],
        compiler_params=_params(("arbitrary",)),
    )(x, g, w1, w2, gf, target)


def _zero_slot(slot_ref):
    @pl.when(pl.program_id(0) == 0)
    def _():
        slot_ref[...] = jnp.zeros_like(slot_ref)


def _mlp_bwd(dx3, u, x2, g, w1, w2, deps):
    s = x2.shape[0]
    tm = min(ROW_TILE // 2, s)

    def body(d_ref, u_ref, x_ref, g_ref, w1_ref, w2_ref, da_ref, dx_ref, slot_ref):
        _zero_slot(slot_ref)
        d = d_ref[...]
        d16 = d.astype(BF16)
        du_next = _mm_nt(d16, w2_ref[0])
        dhf = jnp.zeros((tm, D_MODEL), F32)
        for j in range(N_DEV):
            sl = slice(FF_BLK * j, FF_BLK * (j + 1))
            du = du_next
            if j + 1 < N_DEV:
                du_next = _mm_nt(d16, w2_ref[j + 1])
            u = u_ref[:, sl].astype(F32)
            da = (du * (2.0 * u * lax.rsqrt(jnp.maximum(u, TINY)))).astype(BF16)
            da_ref[:, sl] = da
            dhf = dhf + _mm_nt(da, w1_ref[j])
        dx, dg = _rms_bwd(x_ref[...], g_ref[...], dhf)
        dx_ref[...] = d + dx
        slot_ref[0:1, :] += dg

    row_f32 = pl.BlockSpec((tm, D_MODEL), lambda i: (i, 0))
    return _call_behind(
        deps, body, name="mlp_bwd", grid=(s // tm,),
        out_shape=[jax.ShapeDtypeStruct((s, D_FF), BF16), jax.ShapeDtypeStruct((s, D_MODEL), F32),
                   jax.ShapeDtypeStruct((SLOT, D_MODEL), F32)],
        in_specs=[row_f32, pl.BlockSpec((tm, D_FF), lambda i: (i, 0)), row_f32, _full((1, D_MODEL)),
                  VMEM_WHOLE, VMEM_WHOLE],
        out_specs=[pl.BlockSpec((tm, D_FF), lambda i: (i, 0)), row_f32, _full((SLOT, D_MODEL))],
        compiler_params=_params(("arbitrary",)),
    )(dx3, u, x2, g, w1, w2)


def _wgrad(a, b, name, col_blocks=False, update=None):
    s, m = a.shape
    n = b.shape[1]
    tm = 1280 if m % 1280 == 0 else min(1024, m)
    tn = min(1024, n)
    blk = n // N_DEV
    per_step = tn // blk if col_blocks else 1
    ts = min((4 if m * n >= D_MODEL * D_FF else 2) * ROW_TILE, s)
    n_s = s // ts
    grid = (m // tm, n // tn, n_s)

    def body(a_ref, b_ref, *rest):
        o_ref, acc = rest[-2], rest[-1]
        k = pl.program_id(2)

        @pl.when(k == 0)
        def _():
            acc[...] = jnp.zeros_like(acc)

        acc[...] += _mm_tn(a_ref[...], b_ref[...])
        if update is not None:
            r_ref, w_ref, m_ref, v_ref, g_ref, d_ref, nm_ref, nv_ref = rest[:8]
            g = r_ref[0].astype(F32)
            for d in range(1, N_DEV):
                g = g + r_ref[d].astype(F32)
            g_ref[...] = g
            d_ref[...], nm_ref[...], nv_ref[...] = _adam_update(g, w_ref[...], m_ref[...], v_ref[...])

        @pl.when(k == n_s - 1)
        def _():
            if col_blocks:
                for p in range(per_step):
                    o_ref[p] = acc[:, blk * p:blk * (p + 1)].astype(BF16)
            else:
                o_ref[...] = acc[...].astype(BF16)

    if col_blocks:
        out_shape = jax.ShapeDtypeStruct((N_DEV, m, blk), BF16)
        out_spec = pl.BlockSpec((per_step, tm, blk), lambda i, j, k: (j, i, 0))
    else:
        out_shape = jax.ShapeDtypeStruct((m, n), BF16)
        out_spec = pl.BlockSpec((tm, tn), lambda i, j, k: (i, j))
    in_specs = [pl.BlockSpec((ts, tm), lambda i, j, k: (k, i)), pl.BlockSpec((ts, tn), lambda i, j, k: (k, j))]
    out_shapes, out_specs, operands = [out_shape], [out_spec], [a, b]
    if update is not None:
        rows, cols = update[1].shape
        steps = grid[0] * grid[1] * grid[2]
        tr = rows // steps
        step = lambda i, j, k: (i * grid[1] + j) * grid[2] + k
        piece = pl.BlockSpec((tr, cols), lambda i, j, k: (step(i, j, k), 0))
        in_specs += [pl.BlockSpec((N_DEV, tr, cols), lambda i, j, k: (0, step(i, j, k), 0)), piece, piece, piece]
        out_shapes = [jax.ShapeDtypeStruct((rows, cols), F32)] * 4 + out_shapes
        out_specs = [piece] * 4 + out_specs
        operands += list(update)
    outs = pl.pallas_call(
        body, name=name, grid=grid, out_shape=out_shapes, in_specs=in_specs, out_specs=out_specs,
        scratch_shapes=[pltpu.VMEM((tm, tn), F32)],
        compiler_params=_params(("parallel", "parallel", "arbitrary")),
    )(*operands)
    return outs[0] if update is None else (outs[4], tuple(outs[:4]))


def _xattn_bwd(dx2, x1, g, q, xk, xv, wq, wo_t, deps):
    s = x1.shape[0]
    tm = min(ROW_TILE, s)
    scale = XHD ** -0.5

    def body(d_ref, x_ref, g_ref, q_ref, k_ref, v_ref, wq_ref, wo_ref, dx_ref, dx16_ref, dq_ref, dk_ref, dv_ref, slot_ref,
             datt):
        _zero_slot(slot_ref)

        @pl.when(pl.program_id(0) == 0)
        def _():
            dk_ref[...] = jnp.zeros_like(dk_ref)
            dv_ref[...] = jnp.zeros_like(dv_ref)

        d = d_ref[...]
        datt[...] = _mm(d, wo_ref[...]).astype(BF16)
        heads = [slice(XHD * h, XHD * (h + 1)) for h in range(HEADS)]
        scores = [_mm_nt(q_ref[:, sl], k_ref[:, sl]) for sl in heads]
        dps = [_mm_nt(datt[:, sl], v_ref[:, sl]) for sl in heads]
        probs = [_softmax_rows(sc) for sc in scores]
        dss = [(p * (dp - jnp.sum(dp * p, axis=-1, keepdims=True))).astype(BF16) for p, dp in zip(probs, dps)]
        for sl, p, ds in zip(heads, probs, dss):
            dq_ref[:, sl] = (_mm(ds, k_ref[:, sl]) * scale).astype(BF16)
            dk_ref[:, sl] += _mm_tn(ds, q_ref[:, sl])
            dv_ref[:, sl] += _mm_tn(p, datt[:, sl])
        dx, dg = _rms_bwd(x_ref[...], g_ref[...], _mm_nt(dq_ref[...], wq_ref[...]))
        dx_ref[...] = d + dx
        dx16_ref[...] = (d + dx).astype(BF16)
        slot_ref[0:1, :] += dg

    row_f32 = pl.BlockSpec((tm, D_MODEL), lambda i: (i, 0))
    kv = jax.ShapeDtypeStruct((MEM_LEN, D_MODEL), F32)
    tokens16 = jax.ShapeDtypeStruct((s, D_MODEL), BF16)
    return _call_behind(
        deps, body, name="xattn_bwd", grid=(s // tm,),
        out_shape=[jax.ShapeDtypeStruct((s, D_MODEL), F32), tokens16, tokens16, kv, kv,
                   jax.ShapeDtypeStruct((SLOT, D_MODEL), F32)],
        in_specs=[row_f32, row_f32, _full((1, D_MODEL)), row_f32, VMEM_WHOLE, VMEM_WHOLE, VMEM_WHOLE, VMEM_WHOLE],
        out_specs=[row_f32, row_f32, row_f32, _full((MEM_LEN, D_MODEL)), _full((MEM_LEN, D_MODEL)),
                   _full((SLOT, D_MODEL))],
        scratch_shapes=[pltpu.VMEM((tm, D_MODEL), BF16)],
        compiler_params=_params(("arbitrary",)),
    )(dx2, x1, g, q, xk, xv, wq, wo_t)


def _mem_bwd(mem, g, hm, dxk, dxv, wk, wv):
    def body(m_ref, g_ref, hm_ref, dk_ref, dv_ref, wk_ref, wv_ref, dwk_ref, dwv_ref, slot_ref):
        dk, dv = dk_ref[...], dv_ref[...]
        hm_ = hm_ref[...]
        dwk_ref[...] = _mm_tn(hm_, dk).astype(BF16)
        dwv_ref[...] = _mm_tn(hm_, dv).astype(BF16)
        _, dg = _rms_bwd(m_ref[...], g_ref[...], _mm_nt(dk, wk_ref[...]) + _mm_nt(dv, wv_ref[...]))
        slot_ref[...] = jnp.zeros_like(slot_ref)
        slot_ref[0:1, :] = dg

    wshape = jax.ShapeDtypeStruct((D_MODEL, D_MODEL), BF16)
    return pl.pallas_call(
        body, name="mem_bwd", out_shape=[wshape, wshape, jax.ShapeDtypeStruct((SLOT, D_MODEL), F32)],
        in_specs=[VMEM_WHOLE] * 7, out_specs=[VMEM_WHOLE] * 3,
        compiler_params=_params(),
    )(mem, g, hm, dxk, dxv, wk, wv)


def _pool_bwd(dx1, w_out, pooled, w_pool, scale, deps):
    s = dx1.shape[0]
    tm = min(ROW_TILE, s)
    n_t = s // tm

    def body(dx_ref, wo_ref, pl_ref, w_ref, sc_ref, dz_ref, dw_ref, slot_ref, ext, do_ref):
        i = pl.program_id(0)
        tile = n_t - 1 - i
        _zero_slot(slot_ref)
        do_ref[...] = _mm_nt(dx_ref[...], wo_ref[HW:2 * HW, :])

        @pl.when(i == 0)
        def _():
            dw_ref[...] = jnp.zeros_like(dw_ref)
            ext[tm:tm + POOL_HALO, :] = jnp.zeros((POOL_HALO, HW), F32)

        @pl.when(i > 0)
        def _():
            ext[tm:tm + POOL_HALO, :] = ext[0:POOL_HALO, :]

        inv = _pool_counts(tile, tm)
        dpooled = []
        for g in range(HEADS):
            sl = slice(HD * g, HD * (g + 1))
            pooled_g = pl_ref[:, sl]
            do = do_ref[:, sl]
            slot_ref[0:1, sl] += jnp.sum(_mm(pooled_g, w_ref[g]) * do, axis=0, keepdims=True)
            dy = (do * sc_ref[:, sl]).astype(BF16)
            dw_ref[g] += _mm_tn(pooled_g, dy)
            dpo = _mm_nt(dy, w_ref[g])
            dpooled.append(dpo)
            ext[0:tm, sl] = dpo * inv[g]
        for g, w in enumerate(POOL_WINDOWS):
            sl = slice(HD * g, HD * (g + 1))
            win = ext[0:tm, sl]
            for d in range(1, w):
                win = win + ext[d:d + tm, sl]
            dz_ref[:, sl] = (win - dpooled[g]).astype(BF16)

    return _call_behind(
        deps, body, name="pool_bwd", grid=(n_t,),
        out_shape=[jax.ShapeDtypeStruct((s, IN_WIDTH), BF16), jax.ShapeDtypeStruct((HEADS, HD, HD), F32),
                   jax.ShapeDtypeStruct((SLOT, D_MODEL), F32)],
        in_specs=[pl.BlockSpec((tm, D_MODEL), lambda i: (n_t - 1 - i, 0)), VMEM_WHOLE,
                  pl.BlockSpec((tm, HW), lambda i: (n_t - 1 - i, 0)), _full((HEADS, HD, HD)), _full((1, HW))],
        out_specs=[pl.BlockSpec((tm, HW), lambda i: (n_t - 1 - i, 4)), _full((HEADS, HD, HD)), _full((SLOT, D_MODEL))],
        scratch_shapes=[pltpu.VMEM((tm + POOL_HALO, HW), F32), pltpu.VMEM((tm, HW), F32)],
        compiler_params=_params(("arbitrary",)),
    )(dx1, w_out, pooled, w_pool, scale)


def _hgrn_bwd(z, o, dx1, w_out, states, lb_logits, gn, dz_in, deps):
    s = z.shape[0]
    n_chunks = s // CHUNK

    def body(zq_ref, zf_ref, zi_ref, zg_ref, o_ref, dx_ref, wo_ref, st_ref, lbl_ref, gn_ref, dzin_ref,
             dz_ref, dlb_ref, dgn_ref, dstate, b_scr, dlb_acc, do_ref):
        i = pl.program_id(0)

        @pl.when(i == 0)
        def _():
            dstate[...] = jnp.zeros_like(dstate)
            dlb_acc[...] = jnp.zeros_like(dlb_acc)
            dgn_ref[...] = jnp.zeros_like(dgn_ref)
            dlb_ref[...] = jnp.zeros_like(dlb_ref)

        do_ref[...] = _mm_nt(dx_ref[...], wo_ref[0:HW, :])
        lb = _sigmoid(lbl_ref[0:1, :] - lbl_ref[1:2, :])
        row, col = _chunk_masks()
        causal = col <= row
        tri = _ones_where(causal)
        upper = _ones_where(col >= row)
        strict_lower = _ones_where(col < row)
        in_sub, in_sub_head = _sub_chunk_masks(HW), _sub_chunk_masks(HD)
        gn_row = _lanes([gn_ref[h:h + 1, :] for h in range(HEADS)])
        sums = {"dlb": 0.0, "dgn": 0.0}

        def front(c):
            r0 = CHUNK * c
            rs = slice(r0, r0 + CHUNK)
            p = {"rs": rs}
            p["zq"] = zq_ref[rs, :]
            p["q"], p["sq"], p["sig"], p["f"] = _hgrn_gates(p["zq"], zf_ref[rs, :], lb)
            p["kk"] = 1.0 - p["f"]
            b = _tri_dot(tri, jnp.log(p["f"]), 3)
            b_scr[rs, :] = b
            p["v"] = zi_ref[rs, :]
            o, zg, doa = o_ref[rs, :], zg_ref[rs, :], do_ref[rs, :]
            sg = _sigmoid(zg)
            rms = _head_rms(o)
            n = o * rms
            don = doa * (zg * sg)
            sums["dgn"] = sums["dgn"] + jnp.sum(don * n, axis=0, keepdims=True)
            dn = don * gn_row
            p["d_o"] = rms * (dn - n * _head_mean(dn * n))
            dz_ref[rs, 3 * HW:4 * HW] = (doa * (n * gn_row) * (sg * (1.0 + zg * (1.0 - sg)))).astype(BF16)
            p["eq"], p["ek"] = _hgrn_decay_factors(b_scr, r0, b, in_sub)
            b_last = b_scr[r0 + CHUNK - 1:r0 + CHUNK, :]
            p["lam"], p["e_last"], p["lam_last"] = jnp.exp(b), jnp.exp(b_last - b), jnp.exp(b_last)
            p["qe"], p["qg"], p["kd"] = p["q"] * p["eq"], p["q"] * p["lam"], p["kk"] * p["e_last"]
            p["ke"] = [p["kk"] * e for e in p["ek"]]
            p["q16"] = [_per_sub_chunk(_head(p["qe"], h), in_sub_head).astype(BF16) for h in range(HEADS)]
            p["ke16"] = [_lanes([_head(p["ke"][j], h) for j in range(N_SUB)]).astype(BF16) for h in range(HEADS)]
            return p

        def recurrence(c, p):
            m = {k: [] for k in ("dv", "gq", "gk", "dqi", "dkd", "st")}
            a, da, dv_state = [], [], []
            for h in range(HEADS):
                vh, doh = _head(p["v"], h), _head(p["d_o"], h)
                st0, ds1 = st_ref[c, h], dstate[h]
                a.append(jnp.where(causal, _mm_nt(p["q16"][h], p["ke16"][h]), 0.0))
                da.append(jnp.where(causal, _mm_nt(doh, vh), 0.0))
                dv_state.append(_mm_nt(_head(p["kd"], h), ds1))
                m["dqi"].append(_mm(doh, st0))
                m["dkd"].append(_mm(vh, ds1))
                m["st"].append(jnp.sum(st0 * ds1, axis=0, keepdims=True))
                dstate[h] = ds1 * _head(p["lam_last"], h) + _mm_tn(doh, _head(p["qg"], h))
            for h in range(HEADS):
                m["dv"].append(_mm_tn(a[h], _head(p["d_o"], h)) + dv_state[h])
                m["gq"].append(_own_lane_block(_mm(da[h], p["ke16"][h]), in_sub_head))
                m["gk"].append(_mm_tn(da[h], p["q16"][h]))
            return m

        def back(p, m):
            rs = p["rs"]
            dz_ref[rs, 2 * HW:3 * HW] = _lanes(m["dv"]).astype(BF16)
            gq = _lanes(m["gq"])
            gk = [_lanes([m["gk"][h][:, HD * j:HD * (j + 1)] for h in range(HEADS)]) for j in range(N_SUB)]
            dq_inter = p["lam"] * _lanes(m["dqi"])
            dq = p["eq"] * gq + dq_inter
            dk_intra = sum(p["ek"][j] * gk[j] for j in range(N_SUB))
            dk_state = _lanes(m["dkd"]) * p["e_last"]
            db_intra = (p["qe"].astype(BF16).astype(F32) * gq
                        - sum(p["ke"][j].astype(BF16).astype(F32) * gk[j] for j in range(N_SUB)))
            dlf = (_tri_dot(upper, db_intra + p["q"] * dq_inter, 2) + _tri_dot(strict_lower, p["kk"] * dk_state, 2)
                   + p["lam_last"] * _lanes(m["st"]))
            sig, sq, zq = p["sig"], p["sq"], p["zq"]
            df = dlf / p["f"] - (dk_intra + dk_state)
            sums["dlb"] = sums["dlb"] + jnp.sum(df * (1.0 - sig), axis=0, keepdims=True)
            dz_ref[rs, HW:2 * HW] = (df * (1.0 - lb) * sig * (1.0 - sig)).astype(BF16)
            dz_ref[rs, 0:HW] = (dq * (sq * (1.0 + zq * (1.0 - sq)))).astype(BF16)

        p = front(CHUNKS_PER_STEP - 1)
        for c in reversed(range(CHUNKS_PER_STEP)):
            m = recurrence(c, p)
            p_next = front(c - 1) if c > 0 else None
            back(p, m)
            p = p_next
        dlb_acc[...] += sums["dlb"]
        for h in range(HEADS):
            dgn_ref[h:h + 1, 0:HD] += _head(sums["dgn"], h)

        @pl.when(i == n_steps - 1)
        def _():
            dl0 = dlb_acc[...] * lb * (1.0 - lb)
            dlb_ref[0:1, 0:HW] = dl0
            dlb_ref[1:2, 0:HW] = -dl0

    rows = CHUNK * CHUNKS_PER_STEP
    n_steps = s // rows
    rev = lambda i: n_steps - 1 - i
    zspec = lambda cb: pl.BlockSpec((rows, HW), lambda i, cb=cb: (rev(i), cb))
    slot = jax.ShapeDtypeStruct((SLOT, D_MODEL), F32)
    return _call_behind(
        deps, body, name="hgrn_bwd", grid=(n_steps,),
        out_shape=[jax.ShapeDtypeStruct((s, IN_WIDTH), BF16), slot, slot],
        in_specs=[zspec(0), zspec(1), zspec(2), zspec(3), pl.BlockSpec((rows, HW), lambda i: (rev(i), 0)),
                  pl.BlockSpec((rows, D_MODEL), lambda i: (rev(i), 0)), VMEM_WHOLE,
                  pl.BlockSpec((CHUNKS_PER_STEP, HEADS, HD, HD), lambda i: (rev(i), 0, 0, 0)), _full((2, HW)),
                  _full((HEADS, HD)), ANY_SPACE],
        out_specs=[pl.BlockSpec((rows, 4 * HW), lambda i: (rev(i), 0)), _full((SLOT, D_MODEL)), _full((SLOT, D_MODEL))],
        scratch_shapes=[pltpu.VMEM((HEADS, HD, HD), F32), pltpu.VMEM((rows, HW), F32), pltpu.VMEM((1, HW), F32),
                        pltpu.VMEM((rows, HW), F32)],
        input_output_aliases={10: 0},
        compiler_params=_params(("arbitrary",)),
    )(z, z, z, z, o, dx1, w_out, states, lb_logits, gn, dz_in)


def _in_bwd(dz, w_t, x0, g, dx1, update, deps):
    s = x0.shape[0]
    tm = min(ROW_TILE, s)
    n_t = s // tm
    rows, cols = update[1].shape
    tr = rows // n_t

    def body(dz_ref, w_ref, x_ref, g_ref, d_ref, r_ref, uw_ref, um_ref, uv_ref,
             dx_ref, slot_ref, ug_ref, ud_ref, unm_ref, unv_ref):
        _zero_slot(slot_ref)
        dx, dg = _rms_bwd(x_ref[...], g_ref[...], _mm(dz_ref[...], w_ref[...]))
        dx_ref[...] = d_ref[...] + dx
        slot_ref[0:1, :] += dg
        grad = r_ref[0].astype(F32)
        for d in range(1, N_DEV):
            grad = grad + r_ref[d].astype(F32)
        ug_ref[...] = grad
        ud_ref[...], unm_ref[...], unv_ref[...] = _adam_update(grad, uw_ref[...], um_ref[...], uv_ref[...])

    row_f32 = pl.BlockSpec((tm, D_MODEL), lambda i: (i, 0))
    piece = pl.BlockSpec((tr, cols), lambda i: (i, 0))
    piece_shape = jax.ShapeDtypeStruct((rows, cols), F32)
    outs = _call_behind(
        deps, body, name="in_bwd", grid=(n_t,),
        out_shape=[jax.ShapeDtypeStruct((s, D_MODEL), F32), jax.ShapeDtypeStruct((SLOT, D_MODEL), F32)]
        + [piece_shape] * 4,
        in_specs=[pl.BlockSpec((tm, IN_WIDTH), lambda i: (i, 0)), VMEM_WHOLE, row_f32, _full((1, D_MODEL)), row_f32,
                  pl.BlockSpec((N_DEV, tr, cols), lambda i: (0, i, 0)), piece, piece, piece],
        out_specs=[row_f32, _full((SLOT, D_MODEL))] + [piece] * 4,
        compiler_params=_params(("arbitrary",)),
    )(dz, w_t, x0, g, dx1, *update)
    return outs[0], outs[1], tuple(outs[2:])


def kernel(x, mem, norm_mix_g, w_in, lb_logits, hgrn_norm_g, w_pool, pool_scale, w_out, norm_x_g, norm_mem_g, w_xq, w_xk, w_xv, w_xo, norm_ffn_g, w_ff1, w_ff2, final_norm_g, loss_target, m_norm_mix_g, m_w_in, m_lb_logits, m_hgrn_norm_g, m_w_pool, m_pool_scale, m_w_out, m_norm_x_g, m_norm_mem_g, m_w_xq, m_w_xk, m_w_xv, m_w_xo, m_norm_ffn_g, m_w_ff1, m_w_ff2, m_final_norm_g, v_norm_mix_g, v_w_in, v_lb_logits, v_hgrn_norm_g, v_w_pool, v_pool_scale, v_w_out, v_norm_x_g, v_norm_mem_g, v_w_xq, v_w_xk, v_w_xv, v_w_xo, v_norm_ffn_g, v_w_ff1, v_w_ff2, v_final_norm_g):
    x0 = x[0]
    mem0 = mem[0]
    tgt = loss_target[0]
    gn = hgrn_norm_g[0]
    gfin = final_norm_g.reshape(1, D_MODEL)
    wp = w_pool[0]
    heads_2d = lambda w: w.reshape(D_MODEL // N_DEV, D_MODEL)
    xo_2d = lambda w: w.reshape(D_MODEL, D_MODEL // N_DEV)

    first = _all_gather_weights([w_in[0].T], [w_out[0], heads_2d(w_xq), heads_2d(w_xk), heads_2d(w_xv), xo_2d(w_xo).T,
                                              w_ff1[0], w_ff2[0]])
    win_t = first[0].reshape(IN_WIDTH, D_MODEL)
    ga_attn, ga_mlp = _gather_first_start([first[1:6], first[6:8]], "gather_first_start")

    z, h = _in_proj(x0, norm_mix_g, win_t, deps=[ga_attn[3]])
    mixed_a, o_pre, states = _hgrn_fwd(z, lb_logits, gn)
    lands = _split_wait(_gather_first_copies, ga_attn, o_pre, "gather_attn_first_wait")
    gb_attn = _gather_forward_start(lands, "gather_attn_forward_start")
    mixed, pooled = _pool_fwd(z, wp, pool_scale, mixed_a, deps=[gb_attn[3]])
    lands = _split_wait(_gather_forward_copies, gb_attn, pooled, "gather_attn_forward_wait")
    wout_f, wq_f, wk_f, wv_f, wo_t = (t.reshape(D_MODEL, D_MODEL) for t in lands)
    hm, xk, xv = _mem_kv(mem0, norm_mem_g, wk_f, wv_f, deps=[])
    x1, x2, hq, xq, att = _mix_xattn_fwd(x0, mixed, wout_f, norm_x_g, wq_f, xk, xv, wo_t, deps=[])
    lands = _split_wait(_gather_first_copies, ga_mlp, x2, "gather_mlp_first_wait")
    gb_mlp = _gather_forward_start(lands, "gather_mlp_forward_start")
    w1_b, w2_b = _split_wait(_gather_forward_copies, gb_mlp, gb_mlp[3], "gather_mlp_forward_wait")
    dx3, dx3_16, u, hf, slot_fin = _mlp_fwd_loss(x2, norm_ffn_g, w1_b, w2_b.reshape(D_FF, D_MODEL), gfin, tgt)

    rows = lambda t, r: t.reshape(N_DEV, r, D_MODEL)
    dw2 = _wgrad(u, dx3_16, "wgrad_ff2")
    ex_ff2 = _all_to_all_start([rows(dw2, FF_BLK)], [], "exchange_ff2_start")
    da, dx2, slot_ffn = _mlp_bwd(dx3, u, x2, norm_ffn_g, w1_b, w2_b, deps=[ex_ff2[3]])
    dw1 = _wgrad(hf, da, "wgrad_ff1", col_blocks=True)
    ex_ff1 = _all_to_all_start([dw1], [], "exchange_ff1_start")
    dx1, dx1_16, dxq, dxk, dxv, slot_x = _xattn_bwd(dx2, x1, norm_x_g, xq, xk, xv, wq_f, wo_t, deps=[ex_ff1[3]])
    dwo_t = _wgrad(dx2, att, "wgrad_xo")
    dwq = _wgrad(hq, dxq, "wgrad_xq")
    dwk, dwv, slot_mem = _mem_bwd(mem0, norm_mem_g, hm, dxk, dxv, wk_f, wv_f)
    ex_attn = _all_to_all_start([rows(dwq, 128), rows(dwk, 128), rows(dwv, 128), rows(dwo_t, 128)], [],
                                "exchange_attn_start")
    dwout = _wgrad(mixed, dx1_16, "wgrad_out")
    dz_pool, d_wpool, slot_ps = _pool_bwd(dx1_16, wout_f, pooled, wp, pool_scale, deps=[ex_attn[3]])
    small0 = jnp.concatenate([slot_x, slot_mem, slot_ffn, slot_fin, slot_ps], axis=0)
    ex_out = _all_to_all_start([rows(dwout, 128)], [small0, d_wpool], "exchange_out_start")
    dz, slot_lb, slot_gn = _hgrn_bwd(z, o_pre, dx1_16, wout_f, states, lb_logits, gn, dz_pool, deps=[ex_out[3]])
    (r_2,) = _split_wait(_all_to_all_copies(1), ex_ff2, dz, "exchange_ff2_wait")
    dwin_t, ff2_update = _wgrad(dz, h, "wgrad_in", update=(r_2, w_ff2[0], m_w_ff2[0], v_w_ff2[0]))
    ex_in = _all_to_all_start([rows(dwin_t, 320)], [], "exchange_in_start")
    (r_1,) = _split_wait(_all_to_all_copies(1), ex_ff1, ex_in[3], "exchange_ff1_wait")
    grad_x, slot_mix, ff1_update = _in_bwd(dz, win_t, x0, norm_mix_g, dx1, (r_1, w_ff1[0], m_w_ff1[0], v_w_ff1[0]),
                                           deps=[ex_in[3]])
    small1 = jnp.concatenate([slot_mix, slot_lb, slot_gn], axis=0)
    ex_mix = _all_to_all_start([], [small1], "exchange_mix_start")

    out = {}
    out["w_ff2"] = ff2_update
    out["w_ff1"] = ff1_update
    r_q, r_k, r_v, r_o = _split_wait(_all_to_all_copies(4), ex_attn, ex_mix[3], "exchange_attn_wait")
    sums = _sum_sources_whole([r_q, r_k, r_v, r_o], "sum_grad_attn")
    g_attn = [g.reshape(w_xq.shape) for g in sums[:3]] + [sums[3].T]
    attn = _adamw_whole([(g_attn[0], w_xq, m_w_xq, v_w_xq), (g_attn[1], w_xk, m_w_xk, v_w_xk),
                         (g_attn[2], w_xv, m_w_xv, v_w_xv),
                         (g_attn[3], xo_2d(w_xo), xo_2d(m_w_xo), xo_2d(v_w_xo))], "adamw_attn")
    for n, g, res in zip(("w_xq", "w_xk", "w_xv", "w_xo"), g_attn, attn):
        out[n] = (g, *res)
    r_out, r_small0, r_wpool = _split_wait(_all_to_all_copies(1), ex_out, attn[3][0], "exchange_out_wait")
    out["w_out"] = _sum_adamw(r_out, w_out[0], m_w_out[0], v_w_out[0], "adamw_out")
    (r_in,) = _split_wait(_all_to_all_copies(1), ex_in, out["w_out"][1], "exchange_in_wait")
    in_t = _sum_adamw(r_in, w_in[0].T, m_w_in[0].T, v_w_in[0].T, "adamw_in")
    out["w_in"] = tuple(t.T for t in in_t)
    (r_small1,) = _split_wait(_all_to_all_copies(0), ex_mix, in_t[1], "exchange_mix_wait")
    row = lambda t: t.reshape(1, -1)
    small_params = {
        "norm_mix_g": (norm_mix_g, m_norm_mix_g, v_norm_mix_g),
        "lb_logits": (lb_logits, m_lb_logits, v_lb_logits),
        "hgrn_norm_g": (hgrn_norm_g[0], m_hgrn_norm_g[0], v_hgrn_norm_g[0]),
        "pool_scale": (pool_scale, m_pool_scale, v_pool_scale),
        "norm_x_g": (norm_x_g, m_norm_x_g, v_norm_x_g),
        "norm_mem_g": (norm_mem_g, m_norm_mem_g, v_norm_mem_g),
        "norm_ffn_g": (norm_ffn_g, m_norm_ffn_g, v_norm_ffn_g),
        "final_norm_g": (row(final_norm_g), row(m_final_norm_g), row(v_final_norm_g)),
        "w_pool": (wp, m_w_pool[0], v_w_pool[0]),
    }
    loss, small_out = _small_update([r_small0, r_small1], r_wpool, small_params)
    out.update(small_out)

    shapes = dict(norm_mix_g=norm_mix_g, w_in=w_in, lb_logits=lb_logits, hgrn_norm_g=hgrn_norm_g, w_pool=w_pool,
                  pool_scale=pool_scale, w_out=w_out, norm_x_g=norm_x_g, norm_mem_g=norm_mem_g, w_xq=w_xq, w_xk=w_xk,
                  w_xv=w_xv, w_xo=w_xo, norm_ffn_g=norm_ffn_g, w_ff1=w_ff1, w_ff2=w_ff2, final_norm_g=final_norm_g)
    order = list(shapes)
    group = lambda k: [out[n][k].reshape(shapes[n].shape) for n in order]
    return (loss.reshape(()), grad_x.reshape(x.shape), *group(0), *group(1), *group(2), *group(3))
```

```python
import jax
import jax.numpy as jnp
from jax import lax
from jax.experimental import pallas as pl
from jax.experimental.pallas import tpu as pltpu

F32 = jnp.float32
BF16 = jnp.bfloat16

D_MODEL = 1024
N_DEV = 8
HEADS = 4
HD = 128
HW = HEADS * HD
IN_WIDTH = 5 * HW
XHD = 256
MEM_LEN = 256
D_FF = 4096
FF_BLK = D_FF // N_DEV
POOL_WINDOWS = (2, 4, 8, 16)
POOL_HALO = 16
CHUNK = 64
CHUNKS_PER_STEP = 16
SUB = 16
N_SUB = CHUNK // SUB
EXP_CAP = 80.0
EPS = 1e-6
TINY = 1e-30
ROW_TILE = 512
WIDE_ROW_TILE = 1024
SLOT = 8
V7X_VMEM_LIMIT = 56 * 1024 * 1024

ADAM_LR = 0.001
ADAM_B1 = 0.9
ADAM_B2 = 0.999
ADAM_EPS = 1e-08
ADAM_WD = 0.01
ADAM_STEP = 10

MESH_ID = pl.DeviceIdType.MESH


def _params(sem=None, vmem=V7X_VMEM_LIMIT):
    return pltpu.CompilerParams(dimension_semantics=sem, vmem_limit_bytes=vmem)


def _mm(a, b):
    return lax.dot_general(a.astype(BF16), b.astype(BF16), (((1,), (0,)), ((), ())), preferred_element_type=F32)


def _mm_nt(a, b):
    return lax.dot_general(a.astype(BF16), b.astype(BF16), (((1,), (1,)), ((), ())), preferred_element_type=F32)


def _mm_tn(a, b):
    return lax.dot_general(a.astype(BF16), b.astype(BF16), (((0,), (0,)), ((), ())), preferred_element_type=F32)


def _sigmoid(x):
    return 1.0 / (1.0 + jnp.exp(-x))


def _rms(x):
    return lax.rsqrt(jnp.mean(x * x, axis=-1, keepdims=True) + EPS)


def _rms_bwd(x, g, dh):
    r = _rms(x)
    n = x * r
    dn = dh * g
    dx = r * (dn - n * jnp.mean(dn * n, axis=-1, keepdims=True))
    return dx, jnp.sum(dh * n, axis=0, keepdims=True)


def _tri_dot(tri, x, passes):
    acc = None
    rest = x
    for _ in range(passes):
        piece = rest.astype(BF16)
        part = lax.dot_general(tri, piece, (((1,), (0,)), ((), ())), preferred_element_type=F32)
        acc = part if acc is None else acc + part
        rest = rest - piece.astype(F32)
    return acc


def _adam_update(g, w, m, v):
    nm = ADAM_B1 * m + (1.0 - ADAM_B1) * g
    nv = ADAM_B2 * v + (1.0 - ADAM_B2) * (g * g)
    m_hat = nm / (1.0 - ADAM_B1 ** ADAM_STEP)
    v_hat = nv / (1.0 - ADAM_B2 ** ADAM_STEP)
    return -ADAM_LR * (m_hat / (jnp.sqrt(v_hat) + ADAM_EPS) + ADAM_WD * w), nm, nv


def _full(shape):
    return pl.BlockSpec(shape, lambda *_: (0,) * len(shape))


VMEM_WHOLE = pl.BlockSpec(memory_space=pltpu.VMEM)
ANY_SPACE = pl.BlockSpec(memory_space=pl.ANY)


def _mesh_pos():
    return lax.axis_index("x"), lax.axis_index("y"), lax.axis_index("c")


def _flat(px, py, pc):
    return 4 * px + 2 * py + pc


def _all_gather_weights(shards, cast_only):
    n, nc = len(shards), len(cast_only)
    step = 64

    def body(*refs):
        x_refs, c_refs = refs[:n], refs[n:n + nc]
        out_refs, cast_refs = refs[n + nc:2 * n + nc], refs[2 * n + nc:2 * n + 2 * nc]
        bufs = refs[2 * n + 2 * nc:3 * n + 2 * nc]
        send_sems, recv_sems, local_sems = refs[3 * n + 2 * nc:]
        _handshake(_peers_first_level())
        x, y, c = _mesh_pos()
        me, sibling = (x, y, c), (x, y, 1 - c)
        chips = [(1 - x, y), (x, 1 - y), (1 - x, 1 - y)]

        def copy(a, k, blk, to, src=None):
            rows = out_refs[a].at[_flat(*blk)]
            return pltpu.make_async_remote_copy(
                src_ref=rows if src is None else src, dst_ref=rows,
                send_sem=send_sems.at[7 * a + k], recv_sem=recv_sems.at[7 * a + k], device_id=to, device_id_type=MESH_ID)

        def cast_rows(src, dst, rows):
            def cast(i, carry):
                r0 = pl.multiple_of(i * step, step)
                dst[pl.ds(r0, step), :] = src[pl.ds(r0, step), :].astype(BF16)
                return carry
            lax.fori_loop(0, rows // step, cast, 0)

        first, mine = [], []
        for a in range(n):
            cast_rows(x_refs[a], bufs[a], shards[a].shape[0])
            mine.append(pltpu.make_async_copy(bufs[a], out_refs[a].at[_flat(*me)], local_sems.at[a]))
            first.append(copy(a, 0, me, sibling, src=bufs[a]))
            first += [copy(a, 1 + j, me, (*chip, c), src=bufs[a]) for j, chip in enumerate(chips)]
            for cp in [mine[-1]] + first[-4:]:
                cp.start()
        for a in range(nc):
            cast_rows(c_refs[a], cast_refs[a], cast_only[a].shape[0])
        passed = []
        for j, chip in enumerate(chips):
            for a in range(n):
                copy(a, 1 + j, (*chip, c), me).wait_recv()
                passed.append(copy(a, 4 + j, (*chip, c), sibling))
                passed[-1].start()
        for a in range(n):
            copy(a, 0, sibling, me).wait_recv()
            for j, chip in enumerate(chips):
                copy(a, 4 + j, (*chip, 1 - c), me).wait_recv()
        for cp in first + passed:
            cp.wait_send()
        for cp in mine:
            cp.wait()

    return pl.pallas_call(
        body, name="all_gather_w_in",
        out_shape=[jax.ShapeDtypeStruct((N_DEV,) + s.shape, BF16) for s in shards]
        + [jax.ShapeDtypeStruct(s.shape, BF16) for s in cast_only],
        in_specs=[VMEM_WHOLE] * (n + nc), out_specs=[ANY_SPACE] * n + [VMEM_WHOLE] * nc,
        scratch_shapes=[pltpu.VMEM(s.shape, BF16) for s in shards]
        + [pltpu.SemaphoreType.DMA((7 * n,)), pltpu.SemaphoreType.DMA((7 * n,)), pltpu.SemaphoreType.DMA((n,))],
        compiler_params=pltpu.CompilerParams(vmem_limit_bytes=V7X_VMEM_LIMIT, collective_id=GATHER_W_IN_ID),
    )(*shards, *cast_only)


HBM_SPEC = pl.BlockSpec(memory_space=pltpu.HBM)
SEM_SPEC = pl.BlockSpec(memory_space=pltpu.SEMAPHORE)
EFFECT = pltpu.SideEffectType.DATAFLOW_SIDE_EFFECTING
TOKEN = jax.ShapeDtypeStruct((8, 128), F32)


def _in_hbm(a):
    return pltpu.with_memory_space_constraint(a, pltpu.HBM)


START_IDS = {name: i for i, name in enumerate((
    "gather_first_start", "gather_attn_forward_start", "gather_mlp_forward_start", "exchange_ff2_start",
    "exchange_ff1_start", "exchange_attn_start", "exchange_out_start", "exchange_in_start", "exchange_mix_start"))}


GATHER_W_IN_ID = len(START_IDS)


def _handshake(peers):
    barrier = pltpu.get_barrier_semaphore()
    for peer in peers:
        pl.semaphore_signal(barrier, inc=1, device_id=peer, device_id_type=MESH_ID)
    pl.semaphore_wait(barrier, len(peers))


def _peers_all():
    x, y, c = _mesh_pos()
    return [(1 - x if k & 4 else x, 1 - y if k & 2 else y, 1 - c if k & 1 else c) for k in range(1, N_DEV)]


def _peers_first_level():
    x, y, c = _mesh_pos()
    return [(x, y, 1 - c), (1 - x, y, c), (x, 1 - y, c), (1 - x, 1 - y, c)]


def _peers_sibling():
    x, y, c = _mesh_pos()
    return [(x, y, 1 - c)]


def _split_start(copies_of, srcs, lands, n_sems, name, peers_of, collective_id):
    ns, nl, k = len(srcs), len(lands), len(n_sems)

    def body(*refs):
        _handshake(peers_of())
        src_refs, land_refs = refs[:ns], refs[ns:ns + nl]
        sems = refs[ns + nl:ns + nl + k]
        token = refs[-1]
        for cp in copies_of(src_refs, land_refs, sems):
            cp.start()
        token[...] = jnp.zeros_like(token)

    outs = pl.pallas_call(
        body, name=name,
        out_shape=[pltpu.SemaphoreType.DMA((q,)) for q in n_sems]
        + [pltpu.HBM(a.shape, a.dtype) for a in list(srcs) + list(lands)] + [TOKEN],
        in_specs=[HBM_SPEC] * (ns + nl),
        out_specs=[SEM_SPEC] * k + [HBM_SPEC] * (ns + nl) + [VMEM_WHOLE],
        input_output_aliases={i: k + i for i in range(ns + nl)},
        compiler_params=pltpu.CompilerParams(has_side_effects=EFFECT, collective_id=collective_id),
    )(*[_in_hbm(a) for a in list(srcs) + list(lands)])
    return outs[:k], outs[k:k + ns], outs[k + ns:k + ns + nl], outs[-1]


def _split_wait(copies_of, handle, after, name):
    sems, srcs, lands, _ = handle
    ns, nl, k = len(srcs), len(lands), len(sems)

    def body(*refs):
        src_refs, land_refs = refs[:ns], refs[ns:ns + nl]
        sem_refs = refs[ns + nl:ns + nl + k]
        for cp in copies_of(src_refs, land_refs, sem_refs):
            cp.wait()

    outs = pl.pallas_call(
        body, name=name,
        out_shape=[pltpu.HBM(a.shape, a.dtype) for a in list(srcs) + list(lands)],
        in_specs=[HBM_SPEC] * (ns + nl) + [SEM_SPEC] * k + [ANY_SPACE],
        out_specs=[HBM_SPEC] * (ns + nl),
        input_output_aliases={i: i for i in range(ns + nl)},
        compiler_params=pltpu.CompilerParams(has_side_effects=EFFECT),
    )(*srcs, *lands, *sems, after)
    return outs[ns:]


def _gather_first_copies(shard_refs, land_refs, sems):
    send_sems, recv_sems, local_sems = sems
    x, y, c = _mesh_pos()
    me = _flat(x, y, c)
    peers = [(x, y, 1 - c), (1 - x, y, c), (x, 1 - y, c), (1 - x, 1 - y, c)]
    copies = []
    for a, (shard, land) in enumerate(zip(shard_refs, land_refs)):
        copies.append(pltpu.make_async_copy(shard, land.at[me], local_sems.at[a]))
        for k, peer in enumerate(peers):
            copies.append(pltpu.make_async_remote_copy(
                src_ref=shard, dst_ref=land.at[me], send_sem=send_sems.at[4 * a + k], recv_sem=recv_sems.at[4 * a + k],
                device_id=peer, device_id_type=MESH_ID))
    return copies


def _gather_forward_copies(src_refs, land_refs, sems):
    del src_refs
    send_sems, recv_sems = sems
    x, y, c = _mesh_pos()
    chips = [(1 - x, y), (x, 1 - y), (1 - x, 1 - y)]
    copies = []
    for a, land in enumerate(land_refs):
        for j, chip in enumerate(chips):
            rows = land.at[_flat(*chip, c)]
            copies.append(pltpu.make_async_remote_copy(
                src_ref=rows, dst_ref=rows, send_sem=send_sems.at[3 * a + j], recv_sem=recv_sems.at[3 * a + j],
                device_id=(x, y, 1 - c), device_id_type=MESH_ID))
    return copies


def _gather_first_start(groups, name):
    shards = [s for g in groups for s in g]
    lands = [lax.empty((N_DEV,) + s.shape, s.dtype) for s in shards]
    bounds = [sum(len(g) for g in groups[:i]) for i in range(len(groups) + 1)]

    def copies_of(src_refs, land_refs, sems):
        copies = []
        for i in range(len(groups)):
            lo, hi = bounds[i], bounds[i + 1]
            copies += _gather_first_copies(src_refs[lo:hi], land_refs[lo:hi], sems[3 * i:3 * i + 3])
        return copies

    n_sems = tuple(q for g in groups for q in (4 * len(g), 4 * len(g), len(g)))
    sems, srcs, lands, token = _split_start(copies_of, shards, lands, n_sems, name, _peers_first_level, START_IDS[name])
    return [(sems[3 * i:3 * i + 3], srcs[bounds[i]:bounds[i + 1]], lands[bounds[i]:bounds[i + 1]], token)
            for i in range(len(groups))]


def _gather_forward_start(lands, name):
    n = len(lands)
    return _split_start(_gather_forward_copies, [], lands, (3 * n, 3 * n), name, _peers_sibling, START_IDS[name])


def _all_to_all_copies(n_scattered):
    def copies_of(src_refs, land_refs, sems):
        send_sems, recv_sems, local_sems = sems
        x, y, c = _mesh_pos()
        me = _flat(x, y, c)
        copies = []
        for a, (src, land) in enumerate(zip(src_refs, land_refs)):
            scattered = a < n_scattered
            copies.append(pltpu.make_async_copy(src.at[me] if scattered else src, land.at[me], local_sems.at[a]))
            for k in range(1, N_DEV):
                peer = (1 - x if k & 4 else x, 1 - y if k & 2 else y, 1 - c if k & 1 else c)
                copies.append(pltpu.make_async_remote_copy(
                    src_ref=src.at[_flat(*peer)] if scattered else src, dst_ref=land.at[me],
                    send_sem=send_sems.at[7 * a + k - 1], recv_sem=recv_sems.at[7 * a + k - 1],
                    device_id=peer, device_id_type=MESH_ID))
        return copies
    return copies_of


def _all_to_all_start(scattered, broadcast, name):
    srcs = list(scattered) + list(broadcast)
    lands = [lax.empty(a.shape, a.dtype) for a in scattered] + [lax.empty((N_DEV,) + a.shape, a.dtype) for a in broadcast]
    n = len(srcs)
    return _split_start(_all_to_all_copies(len(scattered)), srcs, lands, (7 * n, 7 * n, n), name, _peers_all,
                        START_IDS[name])


def _call_behind(deps, body, *, in_specs, **kwargs):
    n_in, n_dep = len(in_specs), len(deps)

    def body_without_deps(*refs):
        return body(*refs[:n_in], *refs[n_in + n_dep:])

    call = pl.pallas_call(body_without_deps, in_specs=list(in_specs) + [ANY_SPACE] * n_dep, **kwargs)
    return lambda *operands: call(*operands, *deps)


def _row_tile(rows):
    if rows <= 2 * 256:
        return rows
    for cand in (256, 128, 64, 32, 16):
        if rows % cand == 0:
            return cand
    return rows


def _adamw_whole(groups, name):
    n = len(groups)

    def body(*refs):
        for i in range(n):
            g_ref, w_ref, m_ref, v_ref = refs[4 * i:4 * i + 4]
            d_ref, nm_ref, nv_ref = refs[4 * n + 3 * i:4 * n + 3 * i + 3]
            d_ref[...], nm_ref[...], nv_ref[...] = _adam_update(g_ref[...], w_ref[...], m_ref[...], v_ref[...])

    outs = pl.pallas_call(
        body, name=name, out_shape=[jax.ShapeDtypeStruct(grp[0].shape, F32) for grp in groups for _ in range(3)],
        in_specs=[VMEM_WHOLE] * (4 * n), out_specs=[VMEM_WHOLE] * (3 * n),
        compiler_params=_params(),
    )(*[t for grp in groups for t in grp])
    return [outs[3 * i:3 * i + 3] for i in range(n)]


def _sum_sources_whole(recvs, name):
    n = len(recvs)

    def body(*refs):
        for r_ref, o_ref in zip(refs[:n], refs[n:]):
            acc = r_ref[0].astype(F32)
            for d in range(1, N_DEV):
                acc = acc + r_ref[d].astype(F32)
            o_ref[...] = acc

    return pl.pallas_call(
        body, name=name, out_shape=[jax.ShapeDtypeStruct(r.shape[1:], F32) for r in recvs],
        in_specs=[VMEM_WHOLE] * n, out_specs=[VMEM_WHOLE] * n,
        compiler_params=_params(),
    )(*recvs)


def _sum_adamw(recv, w, m, v, name):
    _, rows, cols = recv.shape
    tile = _row_tile(rows)

    def body(r_ref, w_ref, m_ref, v_ref, g_ref, d_ref, nm_ref, nv_ref):
        acc = r_ref[0].astype(F32)
        for d in range(1, N_DEV):
            acc = acc + r_ref[d].astype(F32)
        g_ref[...] = acc
        d_ref[...], nm_ref[...], nv_ref[...] = _adam_update(acc, w_ref[...], m_ref[...], v_ref[...])

    spec = pl.BlockSpec((tile, cols), lambda i: (i, 0))
    shp = jax.ShapeDtypeStruct((rows, cols), F32)
    return pl.pallas_call(
        body, name=name, grid=(rows // tile,), out_shape=[shp] * 4,
        in_specs=[pl.BlockSpec((N_DEV, tile, cols), lambda i: (0, i, 0)), spec, spec, spec], out_specs=[spec] * 4,
        compiler_params=_params(("parallel",)),
    )(recv, w, m, v)


SMALL_SLOTS = {"norm_x_g": (0, 0, 1, D_MODEL), "norm_mem_g": (0, 8, 1, D_MODEL), "norm_ffn_g": (0, 16, 1, D_MODEL),
               "final_norm_g": (0, 24, 1, D_MODEL), "pool_scale": (0, 32, 1, HW),
               "norm_mix_g": (1, 0, 1, D_MODEL), "lb_logits": (1, 8, 2, HW), "hgrn_norm_g": (1, 16, HEADS, HD)}
LOSS_ROW = 25
SMALL_ORDER = ("norm_mix_g", "lb_logits", "hgrn_norm_g", "pool_scale", "norm_x_g", "norm_mem_g", "norm_ffn_g",
               "final_norm_g", "w_pool")


def _small_update(srecvs, wprecv, params):
    flat = [t for n in SMALL_ORDER for t in params[n]]
    nb = len(srecvs)
    n_in = nb + 1 + len(flat)

    def body(*refs):
        s_refs, wp_ref = refs[0:nb], refs[nb]
        in_refs = refs[nb + 1:n_in]
        loss_ref = refs[n_in]
        out_refs = refs[n_in + 1:-nb]
        accs = refs[-nb:]
        for s_ref, acc in zip(s_refs, accs):
            total = s_ref[0]
            for d in range(1, N_DEV):
                total = total + s_ref[d]
            acc[...] = total
        loss_ref[...] = accs[0][LOSS_ROW:LOSS_ROW + 1, 0:1]
        for i, name in enumerate(SMALL_ORDER):
            w_ref, m_ref, v_ref = in_refs[3 * i:3 * i + 3]
            g_ref, d_ref, nm_ref, nv_ref = out_refs[4 * i:4 * i + 4]
            if name == "w_pool":
                g = wp_ref[0]
                for d in range(1, N_DEV):
                    g = g + wp_ref[d]
            else:
                buf, r0, nr, nc = SMALL_SLOTS[name]
                g = accs[buf][r0:r0 + nr, 0:nc]
            g_ref[...] = g
            d_ref[...], nm_ref[...], nv_ref[...] = _adam_update(g, w_ref[...], m_ref[...], v_ref[...])

    out_shape = [jax.ShapeDtypeStruct((1, 1), F32)]
    for n in SMALL_ORDER:
        out_shape += [jax.ShapeDtypeStruct(params[n][0].shape, F32)] * 4
    outs = pl.pallas_call(
        body, name="small_update", out_shape=out_shape,
        in_specs=[VMEM_WHOLE] * n_in, out_specs=[VMEM_WHOLE] * len(out_shape),
        scratch_shapes=[pltpu.VMEM(r.shape[1:], F32) for r in srecvs],
        compiler_params=_params(),
    )(*srecvs, wprecv, *flat)
    return outs[0], {n: outs[1 + 4 * i:5 + 4 * i] for i, n in enumerate(SMALL_ORDER)}


def _in_proj(x, g, w_t, deps):
    s = x.shape[0]
    tm = min(ROW_TILE, s)

    def body(x_ref, g_ref, w_ref, z_ref, h_ref):
        xv = x_ref[...]
        h = (xv * _rms(xv) * g_ref[...]).astype(BF16)
        h_ref[...] = h
        z_ref[...] = _mm_nt(h, w_ref[...])

    return _call_behind(
        deps, body, name="in_proj", grid=(s // tm,),
        out_shape=[jax.ShapeDtypeStruct((s, IN_WIDTH), F32), jax.ShapeDtypeStruct((s, D_MODEL), BF16)],
        in_specs=[pl.BlockSpec((tm, D_MODEL), lambda i: (i, 0)), _full((1, D_MODEL)), VMEM_WHOLE],
        out_specs=[pl.BlockSpec((tm, IN_WIDTH), lambda i: (i, 0)), pl.BlockSpec((tm, D_MODEL), lambda i: (i, 0))],
        compiler_params=_params(("parallel",)),
    )(x, g, w_t)


def _chunk_masks():
    row = lax.broadcasted_iota(jnp.int32, (CHUNK, CHUNK), 0)
    col = lax.broadcasted_iota(jnp.int32, (CHUNK, CHUNK), 1)
    return row, col


def _ones_where(mask):
    return jnp.where(mask, 1.0, 0.0).astype(BF16)


def _hgrn_gates(zq, zf, lb):
    sq = _sigmoid(zq)
    sig = _sigmoid(zf)
    f = lb + (1.0 - lb) * sig
    return zq * sq, sq, sig, f


def _sub_chunk_masks(width):
    trow = lax.broadcasted_iota(jnp.int32, (CHUNK, width), 0)
    return [(trow >= SUB * j) & (trow < SUB * (j + 1)) for j in range(N_SUB)]


def _head(a, h):
    return a[:, HD * h:HD * (h + 1)]


def _lanes(parts):
    return jnp.concatenate(parts, axis=1)


def _hgrn_decay_factors(b_scr, r0, b, in_sub):
    bases = [jnp.zeros((1, HW), F32)] + [b_scr[r0 + SUB * j - 1:r0 + SUB * j, :] for j in range(1, N_SUB)]
    own_base = bases[N_SUB - 1]
    for j in range(N_SUB - 2, -1, -1):
        own_base = jnp.where(in_sub[j], bases[j], own_base)
    eq = jnp.exp(b - own_base)
    ek = []
    for j in range(N_SUB):
        upto = SUB * (j + 1)
        e = jnp.exp(jnp.minimum(bases[j] - b[0:upto], EXP_CAP))
        ek.append(e if upto == CHUNK else jnp.concatenate([e, jnp.zeros((CHUNK - upto, HW), F32)], axis=0))
    return eq, ek


def _per_sub_chunk(x, in_sub):
    return _lanes([jnp.where(in_sub[j], x, 0.0) for j in range(N_SUB)])


def _own_lane_block(a, in_sub):
    out = a[:, HD * (N_SUB - 1):HD * N_SUB]
    for j in range(N_SUB - 2, -1, -1):
        out = jnp.where(in_sub[j], a[:, HD * j:HD * (j + 1)], out)
    return out


def _head_rms(o):
    return _lanes([jnp.broadcast_to(_rms(_head(o, h)), (CHUNK, HD)) for h in range(HEADS)])


def _head_mean(a):
    return _lanes([jnp.broadcast_to(jnp.mean(_head(a, h), axis=-1, keepdims=True), (CHUNK, HD)) for h in range(HEADS)])


def _hgrn_fwd(z, lb_logits, gn):
    s = z.shape[0]
    n_chunks = s // CHUNK

    def body(zq_ref, zf_ref, zi_ref, zg_ref, lbl_ref, gn_ref, oa_ref, o_ref, st_ref, state, b_scr):
        @pl.when(pl.program_id(0) == 0)
        def _():
            state[...] = jnp.zeros_like(state)

        lb = _sigmoid(lbl_ref[0:1, :] - lbl_ref[1:2, :])
        row, col = _chunk_masks()
        causal = col <= row
        tri = _ones_where(causal)
        in_sub, in_sub_head = _sub_chunk_masks(HW), _sub_chunk_masks(HD)
        gn_row = _lanes([gn_ref[h:h + 1, :] for h in range(HEADS)])
        def front(c):
            r0 = CHUNK * c
            rs = slice(r0, r0 + CHUNK)
            q, _, _, f = _hgrn_gates(zq_ref[rs, :], zf_ref[rs, :], lb)
            kk = 1.0 - f
            b = _tri_dot(tri, jnp.log(f), 3)
            b_scr[rs, :] = b
            eq, ek = _hgrn_decay_factors(b_scr, r0, b, in_sub)
            b_last = b_scr[r0 + CHUNK - 1:r0 + CHUNK, :]
            qe = q * eq
            return {"rs": rs, "v": zi_ref[rs, :], "qg": q * jnp.exp(b), "kd": kk * jnp.exp(b_last - b),
                    "lam_last": jnp.exp(b_last),
                    "q16": [_per_sub_chunk(_head(qe, h), in_sub_head).astype(BF16) for h in range(HEADS)],
                    "ke16": [_lanes([_head(kk * e, h) for e in ek]).astype(BF16) for h in range(HEADS)]}

        def recurrence(c, p):
            st_ref[c] = state[...]
            a, o_inter = [], []
            for h in range(HEADS):
                vh, st = _head(p["v"], h), state[h]
                a.append(jnp.where(causal, _mm_nt(p["q16"][h], p["ke16"][h]), 0.0))
                o_inter.append(_mm_nt(_head(p["qg"], h), st))
                state[h] = st * _head(p["lam_last"], h) + _mm_tn(vh, _head(p["kd"], h))
            return _lanes([_mm(a[h], _head(p["v"], h)) + o_inter[h] for h in range(HEADS)])

        def back(p, o):
            rs = p["rs"]
            o_ref[rs, :] = o
            zg = zg_ref[rs, :]
            oa_ref[rs, :] = (o * _head_rms(o) * gn_row * zg * _sigmoid(zg)).astype(BF16)

        p = front(0)
        for c in range(CHUNKS_PER_STEP):
            o = recurrence(c, p)
            p_next = front(c + 1) if c + 1 < CHUNKS_PER_STEP else None
            back(p, o)
            p = p_next

    rows = CHUNK * CHUNKS_PER_STEP
    zspec = lambda cb: pl.BlockSpec((rows, HW), lambda i, cb=cb: (i, cb))
    return pl.pallas_call(
        body, name="hgrn_fwd", grid=(s // rows,),
        out_shape=[jax.ShapeDtypeStruct((s, 2 * HW), BF16), jax.ShapeDtypeStruct((s, HW), F32),
                   jax.ShapeDtypeStruct((n_chunks, HEADS, HD, HD), F32)],
        in_specs=[zspec(0), zspec(1), zspec(2), zspec(3), _full((2, HW)), _full((HEADS, HD))],
        out_specs=[pl.BlockSpec((rows, HW), lambda i: (i, 0)), pl.BlockSpec((rows, HW), lambda i: (i, 0)),
                   pl.BlockSpec((CHUNKS_PER_STEP, HEADS, HD, HD), lambda i: (i, 0, 0, 0))],
        scratch_shapes=[pltpu.VMEM((HEADS, HD, HD), F32), pltpu.VMEM((rows, HW), F32)],
        compiler_params=_params(("arbitrary",)),
    )(z, z, z, z, lb_logits, gn)


def _pool_counts(tile_idx, tm):
    t = tile_idx * tm + lax.broadcasted_iota(jnp.int32, (tm, 1), 0)
    return [1.0 / jnp.minimum(t + 1, w).astype(F32) for w in POOL_WINDOWS]


def _pool_fwd(z, w_pool, scale, mixed_in, deps):
    s = z.shape[0]
    tm = min(ROW_TILE, s)

    def body(p_ref, w_ref, sc_ref, mixin_ref, ob_ref, pooled_ref, ext):
        i = pl.program_id(0)

        @pl.when(i == 0)
        def _():
            ext[0:POOL_HALO, :] = jnp.zeros((POOL_HALO, HW), F32)

        @pl.when(i > 0)
        def _():
            ext[0:POOL_HALO, :] = ext[tm:tm + POOL_HALO, :]

        ext[POOL_HALO:POOL_HALO + tm, :] = p_ref[...]
        inv = _pool_counts(i, tm)
        for g, w in enumerate(POOL_WINDOWS):
            sl = slice(HD * g, HD * (g + 1))
            p = ext[POOL_HALO:POOL_HALO + tm, sl]
            win = p
            for d in range(1, w):
                win = win + ext[POOL_HALO - d:POOL_HALO - d + tm, sl]
            pooled = (win * inv[g] - p).astype(BF16)
            pooled_ref[:, sl] = pooled
            ob_ref[:, sl] = (_mm(pooled, w_ref[g]) * sc_ref[:, sl]).astype(BF16)

    return _call_behind(
        deps, body, name="pool_fwd", grid=(s // tm,),
        out_shape=[jax.ShapeDtypeStruct((s, 2 * HW), BF16), jax.ShapeDtypeStruct((s, HW), BF16)],
        in_specs=[pl.BlockSpec((tm, HW), lambda i: (i, 4)), _full((HEADS, HD, HD)), _full((1, HW)), ANY_SPACE],
        out_specs=[pl.BlockSpec((tm, HW), lambda i: (i, 1)), pl.BlockSpec((tm, HW), lambda i: (i, 0))],
        scratch_shapes=[pltpu.VMEM((tm + POOL_HALO, HW), F32)],
        input_output_aliases={3: 0},
        compiler_params=_params(("arbitrary",)),
    )(z, w_pool, scale, mixed_in)


def _mem_kv(mem, g, wk, wv, deps):
    def body(m_ref, g_ref, wk_ref, wv_ref, hm_ref, k_ref, v_ref):
        m = m_ref[...]
        hm = (m * _rms(m) * g_ref[...]).astype(BF16)
        hm_ref[...] = hm
        k_ref[...] = _mm(hm, wk_ref[...]).astype(BF16)
        v_ref[...] = _mm(hm, wv_ref[...]).astype(BF16)

    shp = jax.ShapeDtypeStruct((MEM_LEN, D_MODEL), BF16)
    return _call_behind(
        deps, body, name="mem_kv", out_shape=[shp, shp, shp],
        in_specs=[VMEM_WHOLE] * 4, out_specs=[VMEM_WHOLE] * 3,
        compiler_params=_params(),
    )(mem, g, wk, wv)


def _softmax_rows(sc):
    e = jnp.exp(sc - jnp.max(sc, axis=-1, keepdims=True))
    return e / jnp.sum(e, axis=-1, keepdims=True)


def _mix_xattn_fwd(x0, mixed, w_out, g, wq, xk, xv, wo_t, deps):
    s = x0.shape[0]
    tm = min(ROW_TILE, s)
    scale = XHD ** -0.5

    def body(x_ref, mix_ref, wout_ref, g_ref, wq_ref, k_ref, v_ref, wo_ref, x1_ref, o_ref, hq_ref, q_ref, att_ref):
        xv_ = x_ref[...] + _mm(mix_ref[...], wout_ref[...])
        x1_ref[...] = xv_
        hq = (xv_ * _rms(xv_) * g_ref[...]).astype(BF16)
        hq_ref[...] = hq
        q_ref[...] = (_mm(hq, wq_ref[...]) * scale).astype(BF16)
        heads = [slice(XHD * h, XHD * (h + 1)) for h in range(HEADS)]
        scores = [_mm_nt(q_ref[:, sl], k_ref[:, sl]) for sl in heads]
        probs = [_softmax_rows(sc) for sc in scores]
        for sl, p in zip(heads, probs):
            att_ref[:, sl] = _mm(p, v_ref[:, sl]).astype(BF16)
        o_ref[...] = xv_ + _mm_nt(att_ref[...], wo_ref[...])

    row_f32 = pl.BlockSpec((tm, D_MODEL), lambda i: (i, 0))
    bshape = jax.ShapeDtypeStruct((s, D_MODEL), BF16)
    fshape = jax.ShapeDtypeStruct((s, D_MODEL), F32)
    return _call_behind(
        deps, body, name="mix_xattn_fwd", grid=(s // tm,),
        out_shape=[fshape, fshape, bshape, bshape, bshape],
        in_specs=[row_f32, row_f32, VMEM_WHOLE, _full((1, D_MODEL)), VMEM_WHOLE, VMEM_WHOLE, VMEM_WHOLE, VMEM_WHOLE],
        out_specs=[row_f32] * 5,
        compiler_params=_params(("parallel",)),
    )(x0, mixed, w_out, g, wq, xk, xv, wo_t)


def _mlp_fwd_loss(x, g, w1, w2, gf, target):
    s = x.shape[0]
    tm = min(ROW_TILE, s)

    def body(x_ref, g_ref, w1_ref, w2_ref, gf_ref, t_ref, dx_ref, dx16_ref, u_ref, hf_ref, slot_ref):
        @pl.when(pl.program_id(0) == 0)
        def _():
            slot_ref[...] = jnp.zeros_like(slot_ref)

        xv = x_ref[...]
        hf = (xv * _rms(xv) * g_ref[...]).astype(BF16)
        hf_ref[...] = hf
        a_next = _mm(hf, w1_ref[0])
        for j in range(N_DEV):
            a = jnp.maximum(a_next, 0.0)
            if j + 1 < N_DEV:
                a_next = _mm(hf, w1_ref[j + 1])
            u_ref[:, FF_BLK * j:FF_BLK * (j + 1)] = (a * a).astype(BF16)
        acc = xv + _mm(u_ref[...], w2_ref[...])
        gfv = gf_ref[...]
        r = _rms(acc)
        n = acc * r
        err = n * gfv - t_ref[...]
        slot_ref[1:2, :] += jnp.sum(jnp.mean(err * err, axis=-1, keepdims=True), axis=0, keepdims=True) * 0.5
        dy = err * (1.0 / D_MODEL)
        slot_ref[0:1, :] += jnp.sum(dy * n, axis=0, keepdims=True)
        dn = dy * gfv
        dx = r * (dn - n * jnp.mean(dn * n, axis=-1, keepdims=True))
        dx_ref[...] = dx
        dx16_ref[...] = dx.astype(BF16)

    row_f32 = pl.BlockSpec((tm, D_MODEL), lambda i: (i, 0))
    return pl.pallas_call(
        body, name="mlp_fwd_loss", grid=(s // tm,),
        out_shape=[jax.ShapeDtypeStruct((s, D_MODEL), F32), jax.ShapeDtypeStruct((s, D_MODEL), BF16),
                   jax.ShapeDtypeStruct((s, D_FF), BF16), jax.ShapeDtypeStruct((s, D_MODEL), BF16),
                   jax.ShapeDtypeStruct((SLOT, D_MODEL), F32)],
        in_specs=[row_f32, _full((1, D_MODEL)), VMEM_WHOLE, VMEM_WHOLE, _full((1, D_MODEL)), row_f32],
        out_specs=[row_f32, row_f32, pl.BlockSpec((tm, D_FF), lambda i: (i, 0)), row_f32, _full((SLOT, D_MODEL))],
        compiler_params=_params(("arbitrary",)),
    )(x, g, w1, w2, gf, target)


def _zero_slot(slot_ref):
    @pl.when(pl.program_id(0) == 0)
    def _():
        slot_ref[...] = jnp.zeros_like(slot_ref)


def _mlp_bwd(dx3, u, x2, g, w1, w2, deps):
    s = x2.shape[0]
    tm = min(ROW_TILE // 2, s)

    def body(d_ref, u_ref, x_ref, g_ref, w1_ref, w2_ref, da_ref, dx_ref, slot_ref):
        _zero_slot(slot_ref)
        d = d_ref[...]
        d16 = d.astype(BF16)
        du_next = _mm_nt(d16, w2_ref[0])
        dhf = jnp.zeros((tm, D_MODEL), F32)
        for j in range(N_DEV):
            sl = slice(FF_BLK * j, FF_BLK * (j + 1))
            du = du_next
            if j + 1 < N_DEV:
                du_next = _mm_nt(d16, w2_ref[j + 1])
            u = u_ref[:, sl].astype(F32)
            da = (du * (2.0 * u * lax.rsqrt(jnp.maximum(u, TINY)))).astype(BF16)
            da_ref[:, sl] = da
            dhf = dhf + _mm_nt(da, w1_ref[j])
        dx, dg = _rms_bwd(x_ref[...], g_ref[...], dhf)
        dx_ref[...] = d + dx
        slot_ref[0:1, :] += dg

    row_f32 = pl.BlockSpec((tm, D_MODEL), lambda i: (i, 0))
    return _call_behind(
        deps, body, name="mlp_bwd", grid=(s // tm,),
        out_shape=[jax.ShapeDtypeStruct((s, D_FF), BF16), jax.ShapeDtypeStruct((s, D_MODEL), F32),
                   jax.ShapeDtypeStruct((SLOT, D_MODEL), F32)],
        in_specs=[row_f32, pl.BlockSpec((tm, D_FF), lambda i: (i, 0)), row_f32, _full((1, D_MODEL)),
                  VMEM_WHOLE, VMEM_WHOLE],
        out_specs=[pl.BlockSpec((tm, D_FF), lambda i: (i, 0)), row_f32, _full((SLOT, D_MODEL))],
        compiler_params=_params(("arbitrary",)),
    )(dx3, u, x2, g, w1, w2)


def _wgrad(a, b, name, col_blocks=False, update=None):
    s, m = a.shape
    n = b.shape[1]
    tm = 1280 if m % 1280 == 0 else min(1024, m)
    tn = min(1024, n)
    blk = n // N_DEV
    per_step = tn // blk if col_blocks else 1
    ts = min((4 if m * n >= D_MODEL * D_FF else 2) * ROW_TILE, s)
    n_s = s // ts
    grid = (m // tm, n // tn, n_s)

    def body(a_ref, b_ref, *rest):
        o_ref, acc = rest[-2], rest[-1]
        k = pl.program_id(2)

        @pl.when(k == 0)
        def _():
            acc[...] = jnp.zeros_like(acc)

        acc[...] += _mm_tn(a_ref[...], b_ref[...])
        if update is not None:
            r_ref, w_ref, m_ref, v_ref, g_ref, d_ref, nm_ref, nv_ref = rest[:8]
            g = r_ref[0].astype(F32)
            for d in range(1, N_DEV):
                g = g + r_ref[d].astype(F32)
            g_ref[...] = g
            d_ref[...], nm_ref[...], nv_ref[...] = _adam_update(g, w_ref[...], m_ref[...], v_ref[...])

        @pl.when(k == n_s - 1)
        def _():
            if col_blocks:
                for p in range(per_step):
                    o_ref[p] = acc[:, blk * p:blk * (p + 1)].astype(BF16)
            else:
                o_ref[...] = acc[...].astype(BF16)

    if col_blocks:
        out_shape = jax.ShapeDtypeStruct((N_DEV, m, blk), BF16)
        out_spec = pl.BlockSpec((per_step, tm, blk), lambda i, j, k: (j, i, 0))
    else:
        out_shape = jax.ShapeDtypeStruct((m, n), BF16)
        out_spec = pl.BlockSpec((tm, tn), lambda i, j, k: (i, j))
    in_specs = [pl.BlockSpec((ts, tm), lambda i, j, k: (k, i)), pl.BlockSpec((ts, tn), lambda i, j, k: (k, j))]
    out_shapes, out_specs, operands = [out_shape], [out_spec], [a, b]
    if update is not None:
        rows, cols = update[1].shape
        steps = grid[0] * grid[1] * grid[2]
        tr = rows // steps
        step = lambda i, j, k: (i * grid[1] + j) * grid[2] + k
        piece = pl.BlockSpec((tr, cols), lambda i, j, k: (step(i, j, k), 0))
        in_specs += [pl.BlockSpec((N_DEV, tr, cols), lambda i, j, k: (0, step(i, j, k), 0)), piece, piece, piece]
        out_shapes = [jax.ShapeDtypeStruct((rows, cols), F32)] * 4 + out_shapes
        out_specs = [piece] * 4 + out_specs
        operands += list(update)
    outs = pl.pallas_call(
        body, name=name, grid=grid, out_shape=out_shapes, in_specs=in_specs, out_specs=out_specs,
        scratch_shapes=[pltpu.VMEM((tm, tn), F32)],
        compiler_params=_params(("parallel", "parallel", "arbitrary")),
    )(*operands)
    return outs[0] if update is None else (outs[4], tuple(outs[:4]))


def _xattn_bwd(dx2, x1, g, q, xk, xv, wq, wo_t, deps):
    s = x1.shape[0]
    tm = min(ROW_TILE, s)
    scale = XHD ** -0.5

    def body(d_ref, x_ref, g_ref, q_ref, k_ref, v_ref, wq_ref, wo_ref, dx_ref, dx16_ref, dq_ref, dk_ref, dv_ref, slot_ref,
             datt):
        _zero_slot(slot_ref)

        @pl.when(pl.program_id(0) == 0)
        def _():
            dk_ref[...] = jnp.zeros_like(dk_ref)
            dv_ref[...] = jnp.zeros_like(dv_ref)

        d = d_ref[...]
        datt[...] = _mm(d, wo_ref[...]).astype(BF16)
        heads = [slice(XHD * h, XHD * (h + 1)) for h in range(HEADS)]
        scores = [_mm_nt(q_ref[:, sl], k_ref[:, sl]) for sl in heads]
        dps = [_mm_nt(datt[:, sl], v_ref[:, sl]) for sl in heads]
        probs = [_softmax_rows(sc) for sc in scores]
        dss = [(p * (dp - jnp.sum(dp * p, axis=-1, keepdims=True))).astype(BF16) for p, dp in zip(probs, dps)]
        for sl, p, ds in zip(heads, probs, dss):
            dq_ref[:, sl] = (_mm(ds, k_ref[:, sl]) * scale).astype(BF16)
            dk_ref[:, sl] += _mm_tn(ds, q_ref[:, sl])
            dv_ref[:, sl] += _mm_tn(p, datt[:, sl])
        dx, dg = _rms_bwd(x_ref[...], g_ref[...], _mm_nt(dq_ref[...], wq_ref[...]))
        dx_ref[...] = d + dx
        dx16_ref[...] = (d + dx).astype(BF16)
        slot_ref[0:1, :] += dg

    row_f32 = pl.BlockSpec((tm, D_MODEL), lambda i: (i, 0))
    kv = jax.ShapeDtypeStruct((MEM_LEN, D_MODEL), F32)
    tokens16 = jax.ShapeDtypeStruct((s, D_MODEL), BF16)
    return _call_behind(
        deps, body, name="xattn_bwd", grid=(s // tm,),
        out_shape=[jax.ShapeDtypeStruct((s, D_MODEL), F32), tokens16, tokens16, kv, kv,
                   jax.ShapeDtypeStruct((SLOT, D_MODEL), F32)],
        in_specs=[row_f32, row_f32, _full((1, D_MODEL)), row_f32, VMEM_WHOLE, VMEM_WHOLE, VMEM_WHOLE, VMEM_WHOLE],
        out_specs=[row_f32, row_f32, row_f32, _full((MEM_LEN, D_MODEL)), _full((MEM_LEN, D_MODEL)),
                   _full((SLOT, D_MODEL))],
        scratch_shapes=[pltpu.VMEM((tm, D_MODEL), BF16)],
        compiler_params=_params(("arbitrary",)),
    )(dx2, x1, g, q, xk, xv, wq, wo_t)


def _mem_bwd(mem, g, hm, dxk, dxv, wk, wv):
    def body(m_ref, g_ref, hm_ref, dk_ref, dv_ref, wk_ref, wv_ref, dwk_ref, dwv_ref, slot_ref):
        dk, dv = dk_ref[...], dv_ref[...]
        hm_ = hm_ref[...]
        dwk_ref[...] = _mm_tn(hm_, dk).astype(BF16)
        dwv_ref[...] = _mm_tn(hm_, dv).astype(BF16)
        _, dg = _rms_bwd(m_ref[...], g_ref[...], _mm_nt(dk, wk_ref[...]) + _mm_nt(dv, wv_ref[...]))
        slot_ref[...] = jnp.zeros_like(slot_ref)
        slot_ref[0:1, :] = dg

    wshape = jax.ShapeDtypeStruct((D_MODEL, D_MODEL), BF16)
    return pl.pallas_call(
        body, name="mem_bwd", out_shape=[wshape, wshape, jax.ShapeDtypeStruct((SLOT, D_MODEL), F32)],
        in_specs=[VMEM_WHOLE] * 7, out_specs=[VMEM_WHOLE] * 3,
        compiler_params=_params(),
    )(mem, g, hm, dxk, dxv, wk, wv)


def _pool_bwd(dx1, w_out, pooled, w_pool, scale, deps):
    s = dx1.shape[0]
    tm = min(ROW_TILE, s)
    n_t = s // tm

    def body(dx_ref, wo_ref, pl_ref, w_ref, sc_ref, dz_ref, dw_ref, slot_ref, ext, do_ref):
        i = pl.program_id(0)
        tile = n_t - 1 - i
        _zero_slot(slot_ref)
        do_ref[...] = _mm_nt(dx_ref[...], wo_ref[HW:2 * HW, :])

        @pl.when(i == 0)
        def _():
            dw_ref[...] = jnp.zeros_like(dw_ref)
            ext[tm:tm + POOL_HALO, :] = jnp.zeros((POOL_HALO, HW), F32)

        @pl.when(i > 0)
        def _():
            ext[tm:tm + POOL_HALO, :] = ext[0:POOL_HALO, :]

        inv = _pool_counts(tile, tm)
        dpooled = []
        for g in range(HEADS):
            sl = slice(HD * g, HD * (g + 1))
            pooled_g = pl_ref[:, sl]
            do = do_ref[:, sl]
            slot_ref[0:1, sl] += jnp.sum(_mm(pooled_g, w_ref[g]) * do, axis=0, keepdims=True)
            dy = (do * sc_ref[:, sl]).astype(BF16)
            dw_ref[g] += _mm_tn(pooled_g, dy)
            dpo = _mm_nt(dy, w_ref[g])
            dpooled.append(dpo)
            ext[0:tm, sl] = dpo * inv[g]
        for g, w in enumerate(POOL_WINDOWS):
            sl = slice(HD * g, HD * (g + 1))
            win = ext[0:tm, sl]
            for d in range(1, w):
                win = win + ext[d:d + tm, sl]
            dz_ref[:, sl] = (win - dpooled[g]).astype(BF16)

    return _call_behind(
        deps, body, name="pool_bwd", grid=(n_t,),
        out_shape=[jax.ShapeDtypeStruct((s, IN_WIDTH), BF16), jax.ShapeDtypeStruct((HEADS, HD, HD), F32),
                   jax.ShapeDtypeStruct((SLOT, D_MODEL), F32)],
        in_specs=[pl.BlockSpec((tm, D_MODEL), lambda i: (n_t - 1 - i, 0)), VMEM_WHOLE,
                  pl.BlockSpec((tm, HW), lambda i: (n_t - 1 - i, 0)), _full((HEADS, HD, HD)), _full((1, HW))],
        out_specs=[pl.BlockSpec((tm, HW), lambda i: (n_t - 1 - i, 4)), _full((HEADS, HD, HD)), _full((SLOT, D_MODEL))],
        scratch_shapes=[pltpu.VMEM((tm + POOL_HALO, HW), F32), pltpu.VMEM((tm, HW), F32)],
        compiler_params=_params(("arbitrary",)),
    )(dx1, w_out, pooled, w_pool, scale)


def _hgrn_bwd(z, o, dx1, w_out, states, lb_logits, gn, dz_in, deps):
    s = z.shape[0]
    n_chunks = s // CHUNK

    def body(zq_ref, zf_ref, zi_ref, zg_ref, o_ref, dx_ref, wo_ref, st_ref, lbl_ref, gn_ref, dzin_ref,
             dz_ref, dlb_ref, dgn_ref, dstate, b_scr, dlb_acc, do_ref):
        i = pl.program_id(0)

        @pl.when(i == 0)
        def _():
            dstate[...] = jnp.zeros_like(dstate)
            dlb_acc[...] = jnp.zeros_like(dlb_acc)
            dgn_ref[...] = jnp.zeros_like(dgn_ref)
            dlb_ref[...] = jnp.zeros_like(dlb_ref)

        do_ref[...] = _mm_nt(dx_ref[...], wo_ref[0:HW, :])
        lb = _sigmoid(lbl_ref[0:1, :] - lbl_ref[1:2, :])
        row, col = _chunk_masks()
        causal = col <= row
        tri = _ones_where(causal)
        upper = _ones_where(col >= row)
        strict_lower = _ones_where(col < row)
        in_sub, in_sub_head = _sub_chunk_masks(HW), _sub_chunk_masks(HD)
        gn_row = _lanes([gn_ref[h:h + 1, :] for h in range(HEADS)])
        sums = {"dlb": 0.0, "dgn": 0.0}

        def front(c):
            r0 = CHUNK * c
            rs = slice(r0, r0 + CHUNK)
            p = {"rs": rs}
            p["zq"] = zq_ref[rs, :]
            p["q"], p["sq"], p["sig"], p["f"] = _hgrn_gates(p["zq"], zf_ref[rs, :], lb)
            p["kk"] = 1.0 - p["f"]
            b = _tri_dot(tri, jnp.log(p["f"]), 3)
            b_scr[rs, :] = b
            p["v"] = zi_ref[rs, :]
            o, zg, doa = o_ref[rs, :], zg_ref[rs, :], do_ref[rs, :]
            sg = _sigmoid(zg)
            rms = _head_rms(o)
            n = o * rms
            don = doa * (zg * sg)
            sums["dgn"] = sums["dgn"] + jnp.sum(don * n, axis=0, keepdims=True)
            dn = don * gn_row
            p["d_o"] = rms * (dn - n * _head_mean(dn * n))
            dz_ref[rs, 3 * HW:4 * HW] = (doa * (n * gn_row) * (sg * (1.0 + zg * (1.0 - sg)))).astype(BF16)
            p["eq"], p["ek"] = _hgrn_decay_factors(b_scr, r0, b, in_sub)
            b_last = b_scr[r0 + CHUNK - 1:r0 + CHUNK, :]
            p["lam"], p["e_last"], p["lam_last"] = jnp.exp(b), jnp.exp(b_last - b), jnp.exp(b_last)
            p["qe"], p["qg"], p["kd"] = p["q"] * p["eq"], p["q"] * p["lam"], p["kk"] * p["e_last"]
            p["ke"] = [p["kk"] * e for e in p["ek"]]
            p["q16"] = [_per_sub_chunk(_head(p["qe"], h), in_sub_head).astype(BF16) for h in range(HEADS)]
            p["ke16"] = [_lanes([_head(p["ke"][j], h) for j in range(N_SUB)]).astype(BF16) for h in range(HEADS)]
            return p

        def recurrence(c, p):
            m = {k: [] for k in ("dv", "gq", "gk", "dqi", "dkd", "st")}
            a, da, dv_state = [], [], []
            for h in range(HEADS):
                vh, doh = _head(p["v"], h), _head(p["d_o"], h)
                st0, ds1 = st_ref[c, h], dstate[h]
                a.append(jnp.where(causal, _mm_nt(p["q16"][h], p["ke16"][h]), 0.0))
                da.append(jnp.where(causal, _mm_nt(doh, vh), 0.0))
                dv_state.append(_mm_nt(_head(p["kd"], h), ds1))
                m["dqi"].append(_mm(doh, st0))
                m["dkd"].append(_mm(vh, ds1))
                m["st"].append(jnp.sum(st0 * ds1, axis=0, keepdims=True))
                dstate[h] = ds1 * _head(p["lam_last"], h) + _mm_tn(doh, _head(p["qg"], h))
            for h in range(HEADS):
                m["dv"].append(_mm_tn(a[h], _head(p["d_o"], h)) + dv_state[h])
                m["gq"].append(_own_lane_block(_mm(da[h], p["ke16"][h]), in_sub_head))
                m["gk"].append(_mm_tn(da[h], p["q16"][h]))
            return m

        def back(p, m):
            rs = p["rs"]
            dz_ref[rs, 2 * HW:3 * HW] = _lanes(m["dv"]).astype(BF16)
            gq = _lanes(m["gq"])
            gk = [_lanes([m["gk"][h][:, HD * j:HD * (j + 1)] for h in range(HEADS)]) for j in range(N_SUB)]
            dq_inter = p["lam"] * _lanes(m["dqi"])
            dq = p["eq"] * gq + dq_inter
            dk_intra = sum(p["ek"][j] * gk[j] for j in range(N_SUB))
            dk_state = _lanes(m["dkd"]) * p["e_last"]
            db_intra = (p["qe"].astype(BF16).astype(F32) * gq
                        - sum(p["ke"][j].astype(BF16).astype(F32) * gk[j] for j in range(N_SUB)))
            dlf = (_tri_dot(upper, db_intra + p["q"] * dq_inter, 2) + _tri_dot(strict_lower, p["kk"] * dk_state, 2)
                   + p["lam_last"] * _lanes(m["st"]))
            sig, sq, zq = p["sig"], p["sq"], p["zq"]
            df = dlf / p["f"] - (dk_intra + dk_state)
            sums["dlb"] = sums["dlb"] + jnp.sum(df * (1.0 - sig), axis=0, keepdims=True)
            dz_ref[rs, HW:2 * HW] = (df * (1.0 - lb) * sig * (1.0 - sig)).astype(BF16)
            dz_ref[rs, 0:HW] = (dq * (sq * (1.0 + zq * (1.0 - sq)))).astype(BF16)

        p = front(CHUNKS_PER_STEP - 1)
        for c in reversed(range(CHUNKS_PER_STEP)):
            m = recurrence(c, p)
            p_next = front(c - 1) if c > 0 else None
            back(p, m)
            p = p_next
        dlb_acc[...] += sums["dlb"]
        for h in range(HEADS):
            dgn_ref[h:h + 1, 0:HD] += _head(sums["dgn"], h)

        @pl.when(i == n_steps - 1)
        def _():
            dl0 = dlb_acc[...] * lb * (1.0 - lb)
            dlb_ref[0:1, 0:HW] = dl0
            dlb_ref[1:2, 0:HW] = -dl0

    rows = CHUNK * CHUNKS_PER_STEP
    n_steps = s // rows
    rev = lambda i: n_steps - 1 - i
    zspec = lambda cb: pl.BlockSpec((rows, HW), lambda i, cb=cb: (rev(i), cb))
    slot = jax.ShapeDtypeStruct((SLOT, D_MODEL), F32)
    return _call_behind(
        deps, body, name="hgrn_bwd", grid=(n_steps,),
        out_shape=[jax.ShapeDtypeStruct((s, IN_WIDTH), BF16), slot, slot],
        in_specs=[zspec(0), zspec(1), zspec(2), zspec(3), pl.BlockSpec((rows, HW), lambda i: (rev(i), 0)),
                  pl.BlockSpec((rows, D_MODEL), lambda i: (rev(i), 0)), VMEM_WHOLE,
                  pl.BlockSpec((CHUNKS_PER_STEP, HEADS, HD, HD), lambda i: (rev(i), 0, 0, 0)), _full((2, HW)),
                  _full((HEADS, HD)), ANY_SPACE],
        out_specs=[pl.BlockSpec((rows, 4 * HW), lambda i: (rev(i), 0)), _full((SLOT, D_MODEL)), _full((SLOT, D_MODEL))],
        scratch_shapes=[pltpu.VMEM((HEADS, HD, HD), F32), pltpu.VMEM((rows, HW), F32), pltpu.VMEM((1, HW), F32),
                        pltpu.VMEM((rows, HW), F32)],
        input_output_aliases={10: 0},
        compiler_params=_params(("arbitrary",)),
    )(z, z, z, z, o, dx1, w_out, states, lb_logits, gn, dz_in)


def _in_bwd(dz, w_t, x0, g, dx1, deps):
    s = x0.shape[0]
    tm = min(WIDE_ROW_TILE, s)

    def body(dz_ref, w_ref, x_ref, g_ref, d_ref, dx_ref, slot_ref):
        _zero_slot(slot_ref)
        dx, dg = _rms_bwd(x_ref[...], g_ref[...], _mm(dz_ref[...], w_ref[...]))
        dx_ref[...] = d_ref[...] + dx
        slot_ref[0:1, :] += dg

    row_f32 = pl.BlockSpec((tm, D_MODEL), lambda i: (i, 0))
    return _call_behind(
        deps, body, name="in_bwd", grid=(s // tm,),
        out_shape=[jax.ShapeDtypeStruct((s, D_MODEL), F32), jax.ShapeDtypeStruct((SLOT, D_MODEL), F32)],
        in_specs=[pl.BlockSpec((tm, IN_WIDTH), lambda i: (i, 0)), VMEM_WHOLE, row_f32, _full((1, D_MODEL)), row_f32],
        out_specs=[row_f32, _full((SLOT, D_MODEL))],
        compiler_params=_params(("arbitrary",)),
    )(dz, w_t, x0, g, dx1)


def kernel(x, mem, norm_mix_g, w_in, lb_logits, hgrn_norm_g, w_pool, pool_scale, w_out, norm_x_g, norm_mem_g, w_xq, w_xk, w_xv, w_xo, norm_ffn_g, w_ff1, w_ff2, final_norm_g, loss_target, m_norm_mix_g, m_w_in, m_lb_logits, m_hgrn_norm_g, m_w_pool, m_pool_scale, m_w_out, m_norm_x_g, m_norm_mem_g, m_w_xq, m_w_xk, m_w_xv, m_w_xo, m_norm_ffn_g, m_w_ff1, m_w_ff2, m_final_norm_g, v_norm_mix_g, v_w_in, v_lb_logits, v_hgrn_norm_g, v_w_pool, v_pool_scale, v_w_out, v_norm_x_g, v_norm_mem_g, v_w_xq, v_w_xk, v_w_xv, v_w_xo, v_norm_ffn_g, v_w_ff1, v_w_ff2, v_final_norm_g):
    x0 = x[0]
    mem0 = mem[0]
    tgt = loss_target[0]
    gn = hgrn_norm_g[0]
    gfin = final_norm_g.reshape(1, D_MODEL)
    wp = w_pool[0]
    heads_2d = lambda w: w.reshape(D_MODEL // N_DEV, D_MODEL)
    xo_2d = lambda w: w.reshape(D_MODEL, D_MODEL // N_DEV)

    first = _all_gather_weights([w_in[0].T], [w_out[0], heads_2d(w_xq), heads_2d(w_xk), heads_2d(w_xv), xo_2d(w_xo).T,
                                              w_ff1[0], w_ff2[0]])
    win_t = first[0].reshape(IN_WIDTH, D_MODEL)
    ga_attn, ga_mlp = _gather_first_start([first[1:6], first[6:8]], "gather_first_start")

    z, h = _in_proj(x0, norm_mix_g, win_t, deps=[ga_attn[3]])
    mixed_a, o_pre, states = _hgrn_fwd(z, lb_logits, gn)
    lands = _split_wait(_gather_first_copies, ga_attn, o_pre, "gather_attn_first_wait")
    gb_attn = _gather_forward_start(lands, "gather_attn_forward_start")
    mixed, pooled = _pool_fwd(z, wp, pool_scale, mixed_a, deps=[gb_attn[3]])
    lands = _split_wait(_gather_forward_copies, gb_attn, pooled, "gather_attn_forward_wait")
    wout_f, wq_f, wk_f, wv_f, wo_t = (t.reshape(D_MODEL, D_MODEL) for t in lands)
    hm, xk, xv = _mem_kv(mem0, norm_mem_g, wk_f, wv_f, deps=[])
    x1, x2, hq, xq, att = _mix_xattn_fwd(x0, mixed, wout_f, norm_x_g, wq_f, xk, xv, wo_t, deps=[])
    lands = _split_wait(_gather_first_copies, ga_mlp, x2, "gather_mlp_first_wait")
    gb_mlp = _gather_forward_start(lands, "gather_mlp_forward_start")
    w1_b, w2_b = _split_wait(_gather_forward_copies, gb_mlp, gb_mlp[3], "gather_mlp_forward_wait")
    dx3, dx3_16, u, hf, slot_fin = _mlp_fwd_loss(x2, norm_ffn_g, w1_b, w2_b.reshape(D_FF, D_MODEL), gfin, tgt)

    rows = lambda t, r: t.reshape(N_DEV, r, D_MODEL)
    dw2 = _wgrad(u, dx3_16, "wgrad_ff2")
    ex_ff2 = _all_to_all_start([rows(dw2, FF_BLK)], [], "exchange_ff2_start")
    da, dx2, slot_ffn = _mlp_bwd(dx3, u, x2, norm_ffn_g, w1_b, w2_b, deps=[ex_ff2[3]])
    dw1 = _wgrad(hf, da, "wgrad_ff1", col_blocks=True)
    ex_ff1 = _all_to_all_start([dw1], [], "exchange_ff1_start")
    dx1, dx1_16, dxq, dxk, dxv, slot_x = _xattn_bwd(dx2, x1, norm_x_g, xq, xk, xv, wq_f, wo_t, deps=[ex_ff1[3]])
    dwo_t = _wgrad(dx2, att, "wgrad_xo")
    dwq = _wgrad(hq, dxq, "wgrad_xq")
    dwk, dwv, slot_mem = _mem_bwd(mem0, norm_mem_g, hm, dxk, dxv, wk_f, wv_f)
    ex_attn = _all_to_all_start([rows(dwq, 128), rows(dwk, 128), rows(dwv, 128), rows(dwo_t, 128)], [],
                                "exchange_attn_start")
    dwout = _wgrad(mixed, dx1_16, "wgrad_out")
    dz_pool, d_wpool, slot_ps = _pool_bwd(dx1_16, wout_f, pooled, wp, pool_scale, deps=[ex_attn[3]])
    small0 = jnp.concatenate([slot_x, slot_mem, slot_ffn, slot_fin, slot_ps], axis=0)
    ex_out = _all_to_all_start([rows(dwout, 128)], [small0, d_wpool], "exchange_out_start")
    dz, slot_lb, slot_gn = _hgrn_bwd(z, o_pre, dx1_16, wout_f, states, lb_logits, gn, dz_pool, deps=[ex_out[3]])
    (r_2,) = _split_wait(_all_to_all_copies(1), ex_ff2, dz, "exchange_ff2_wait")
    dwin_t, ff2_update = _wgrad(dz, h, "wgrad_in", update=(r_2, w_ff2[0], m_w_ff2[0], v_w_ff2[0]))
    ex_in = _all_to_all_start([rows(dwin_t, 320)], [], "exchange_in_start")
    grad_x, slot_mix = _in_bwd(dz, win_t, x0, norm_mix_g, dx1, deps=[ex_in[3]])
    small1 = jnp.concatenate([slot_mix, slot_lb, slot_gn], axis=0)
    ex_mix = _all_to_all_start([], [small1], "exchange_mix_start")

    out = {}
    out["w_ff2"] = ff2_update
    (r_1,) = _split_wait(_all_to_all_copies(1), ex_ff1, ex_mix[3], "exchange_ff1_wait")
    out["w_ff1"] = _sum_adamw(r_1, w_ff1[0], m_w_ff1[0], v_w_ff1[0], "adamw_ff1")
    r_q, r_k, r_v, r_o = _split_wait(_all_to_all_copies(4), ex_attn, out["w_ff1"][1], "exchange_attn_wait")
    sums = _sum_sources_whole([r_q, r_k, r_v, r_o], "sum_grad_attn")
    g_attn = [g.reshape(w_xq.shape) for g in sums[:3]] + [sums[3].T]
    attn = _adamw_whole([(g_attn[0], w_xq, m_w_xq, v_w_xq), (g_attn[1], w_xk, m_w_xk, v_w_xk),
                         (g_attn[2], w_xv, m_w_xv, v_w_xv),
                         (g_attn[3], xo_2d(w_xo), xo_2d(m_w_xo), xo_2d(v_w_xo))], "adamw_attn")
    for n, g, res in zip(("w_xq", "w_xk", "w_xv", "w_xo"), g_attn, attn):
        out[n] = (g, *res)
    r_out, r_small0, r_wpool = _split_wait(_all_to_all_copies(1), ex_out, attn[3][0], "exchange_out_wait")
    out["w_out"] = _sum_adamw(r_out, w_out[0], m_w_out[0], v_w_out[0], "adamw_out")
    (r_in,) = _split_wait(_all_to_all_copies(1), ex_in, out["w_out"][1], "exchange_in_wait")
    in_t = _sum_adamw(r_in, w_in[0].T, m_w_in[0].T, v_w_in[0].T, "adamw_in")
    out["w_in"] = tuple(t.T for t in in_t)
    (r_small1,) = _split_wait(_all_to_all_copies(0), ex_mix, in_t[1], "exchange_mix_wait")
    row = lambda t: t.reshape(1, -1)
    small_params = {
        "norm_mix_g": (norm_mix_g, m_norm_mix_g, v_norm_mix_g),
        "lb_logits": (lb_logits, m_lb_logits, v_lb_logits),
        "hgrn_norm_g": (hgrn_norm_g[0], m_hgrn_norm_g[0], v_hgrn_norm_g[0]),
        "pool_scale": (pool_scale, m_pool_scale, v_pool_scale),
        "norm_x_g": (norm_x_g, m_norm_x_g, v_norm_x_g),
        "norm_mem_g": (norm_mem_g, m_norm_mem_g, v_norm_mem_g),
        "norm_ffn_g": (norm_ffn_g, m_norm_ffn_g, v_norm_ffn_g),
        "final_norm_g": (row(final_norm_g), row(m_final_norm_g), row(v_final_norm_g)),
        "w_pool": (wp, m_w_pool[0], v_w_pool[0]),
    }
    loss, small_out = _small_update([r_small0, r_small1], r_wpool, small_params)
    out.update(small_out)

    shapes = dict(norm_mix_g=norm_mix_g, w_in=w_in, lb_logits=lb_logits, hgrn_norm_g=hgrn_norm_g, w_pool=w_pool,
                  pool_scale=pool_scale, w_out=w_out, norm_x_g=norm_x_g, norm_mem_g=norm_mem_g, w_xq=w_xq, w_xk=w_xk,
                  w_xv=w_xv, w_xo=w_xo, norm_ffn_g=norm_ffn_g, w_ff1=w_ff1, w_ff2=w_ff2, final_norm_g=final_norm_g)
    order = list(shapes)
    group = lambda k: [out[n][k].reshape(shapes[n].shape) for n in order]
    return (loss.reshape(()), grad_x.reshape(x.shape), *group(0), *group(1), *group(2), *group(3))
```

```python
import jax
import jax.numpy as jnp
from jax import lax
from jax.experimental import pallas as pl
from jax.experimental.pallas import tpu as pltpu

F32 = jnp.float32
BF16 = jnp.bfloat16

D_MODEL = 1024
N_DEV = 8
HEADS = 4
HD = 128
HW = HEADS * HD
IN_WIDTH = 5 * HW
XHD = 256
MEM_LEN = 256
D_FF = 4096
FF_BLK = D_FF // N_DEV
POOL_WINDOWS = (2, 4, 8, 16)
POOL_HALO = 16
CHUNK = 64
CHUNKS_PER_STEP = 8
SUB = 16
N_SUB = CHUNK // SUB
EXP_CAP = 80.0
EPS = 1e-6
TINY = 1e-30
ROW_TILE = 512
WIDE_ROW_TILE = 1024
SLOT = 8
V7X_VMEM_LIMIT = 56 * 1024 * 1024

ADAM_LR = 0.001
ADAM_B1 = 0.9
ADAM_B2 = 0.999
ADAM_EPS = 1e-08
ADAM_WD = 0.01
ADAM_STEP = 10

MESH_ID = pl.DeviceIdType.MESH


def _params(sem=None, vmem=V7X_VMEM_LIMIT):
    return pltpu.CompilerParams(dimension_semantics=sem, vmem_limit_bytes=vmem)


def _mm(a, b):
    return lax.dot_general(a.astype(BF16), b.astype(BF16), (((1,), (0,)), ((), ())), preferred_element_type=F32)


def _mm_nt(a, b):
    return lax.dot_general(a.astype(BF16), b.astype(BF16), (((1,), (1,)), ((), ())), preferred_element_type=F32)


def _mm_tn(a, b):
    return lax.dot_general(a.astype(BF16), b.astype(BF16), (((0,), (0,)), ((), ())), preferred_element_type=F32)


def _sigmoid(x):
    return 1.0 / (1.0 + jnp.exp(-x))


def _rms(x):
    return lax.rsqrt(jnp.mean(x * x, axis=-1, keepdims=True) + EPS)


def _rms_bwd(x, g, dh):
    r = _rms(x)
    n = x * r
    dn = dh * g
    dx = r * (dn - n * jnp.mean(dn * n, axis=-1, keepdims=True))
    return dx, jnp.sum(dh * n, axis=0, keepdims=True)


def _tri_dot(tri, x, passes):
    acc = None
    rest = x
    for _ in range(passes):
        piece = rest.astype(BF16)
        part = lax.dot_general(tri, piece, (((1,), (0,)), ((), ())), preferred_element_type=F32)
        acc = part if acc is None else acc + part
        rest = rest - piece.astype(F32)
    return acc


def _adam_update(g, w, m, v):
    nm = ADAM_B1 * m + (1.0 - ADAM_B1) * g
    nv = ADAM_B2 * v + (1.0 - ADAM_B2) * (g * g)
    m_hat = nm / (1.0 - ADAM_B1 ** ADAM_STEP)
    v_hat = nv / (1.0 - ADAM_B2 ** ADAM_STEP)
    return -ADAM_LR * (m_hat / (jnp.sqrt(v_hat) + ADAM_EPS) + ADAM_WD * w), nm, nv


def _full(shape):
    return pl.BlockSpec(shape, lambda *_: (0,) * len(shape))


VMEM_WHOLE = pl.BlockSpec(memory_space=pltpu.VMEM)
ANY_SPACE = pl.BlockSpec(memory_space=pl.ANY)


def _mesh_pos():
    return lax.axis_index("x"), lax.axis_index("y"), lax.axis_index("c")


def _flat(px, py, pc):
    return 4 * px + 2 * py + pc


def _all_gather_weights(shards, cast_only):
    n, nc = len(shards), len(cast_only)
    step = 64

    def body(*refs):
        x_refs, c_refs = refs[:n], refs[n:n + nc]
        out_refs, cast_refs = refs[n + nc:2 * n + nc], refs[2 * n + nc:2 * n + 2 * nc]
        bufs = refs[2 * n + 2 * nc:3 * n + 2 * nc]
        send_sems, recv_sems, local_sems = refs[3 * n + 2 * nc:]
        _handshake(_peers_first_level())
        x, y, c = _mesh_pos()
        me, sibling = (x, y, c), (x, y, 1 - c)
        chips = [(1 - x, y), (x, 1 - y), (1 - x, 1 - y)]

        def copy(a, k, blk, to, src=None):
            rows = out_refs[a].at[_flat(*blk)]
            return pltpu.make_async_remote_copy(
                src_ref=rows if src is None else src, dst_ref=rows,
                send_sem=send_sems.at[7 * a + k], recv_sem=recv_sems.at[7 * a + k], device_id=to, device_id_type=MESH_ID)

        def cast_rows(src, dst, rows):
            def cast(i, carry):
                r0 = pl.multiple_of(i * step, step)
                dst[pl.ds(r0, step), :] = src[pl.ds(r0, step), :].astype(BF16)
                return carry
            lax.fori_loop(0, rows // step, cast, 0)

        first, mine = [], []
        for a in range(n):
            cast_rows(x_refs[a], bufs[a], shards[a].shape[0])
            mine.append(pltpu.make_async_copy(bufs[a], out_refs[a].at[_flat(*me)], local_sems.at[a]))
            first.append(copy(a, 0, me, sibling, src=bufs[a]))
            first += [copy(a, 1 + j, me, (*chip, c), src=bufs[a]) for j, chip in enumerate(chips)]
            for cp in [mine[-1]] + first[-4:]:
                cp.start()
        for a in range(nc):
            cast_rows(c_refs[a], cast_refs[a], cast_only[a].shape[0])
        passed = []
        for j, chip in enumerate(chips):
            for a in range(n):
                copy(a, 1 + j, (*chip, c), me).wait_recv()
                passed.append(copy(a, 4 + j, (*chip, c), sibling))
                passed[-1].start()
        for a in range(n):
            copy(a, 0, sibling, me).wait_recv()
            for j, chip in enumerate(chips):
                copy(a, 4 + j, (*chip, 1 - c), me).wait_recv()
        for cp in first + passed:
            cp.wait_send()
        for cp in mine:
            cp.wait()

    return pl.pallas_call(
        body, name="all_gather_w_in",
        out_shape=[jax.ShapeDtypeStruct((N_DEV,) + s.shape, BF16) for s in shards]
        + [jax.ShapeDtypeStruct(s.shape, BF16) for s in cast_only],
        in_specs=[VMEM_WHOLE] * (n + nc), out_specs=[ANY_SPACE] * n + [VMEM_WHOLE] * nc,
        scratch_shapes=[pltpu.VMEM(s.shape, BF16) for s in shards]
        + [pltpu.SemaphoreType.DMA((7 * n,)), pltpu.SemaphoreType.DMA((7 * n,)), pltpu.SemaphoreType.DMA((n,))],
        compiler_params=pltpu.CompilerParams(vmem_limit_bytes=V7X_VMEM_LIMIT, collective_id=GATHER_W_IN_ID),
    )(*shards, *cast_only)


HBM_SPEC = pl.BlockSpec(memory_space=pltpu.HBM)
SEM_SPEC = pl.BlockSpec(memory_space=pltpu.SEMAPHORE)
EFFECT = pltpu.SideEffectType.DATAFLOW_SIDE_EFFECTING
TOKEN = jax.ShapeDtypeStruct((8, 128), F32)


def _in_hbm(a):
    return pltpu.with_memory_space_constraint(a, pltpu.HBM)


START_IDS = {name: i for i, name in enumerate((
    "gather_first_start", "gather_attn_forward_start", "gather_mlp_forward_start", "exchange_ff2_start",
    "exchange_ff1_start", "exchange_attn_start", "exchange_out_start", "exchange_in_start", "exchange_mix_start"))}


GATHER_W_IN_ID = len(START_IDS)


def _handshake(peers):
    barrier = pltpu.get_barrier_semaphore()
    for peer in peers:
        pl.semaphore_signal(barrier, inc=1, device_id=peer, device_id_type=MESH_ID)
    pl.semaphore_wait(barrier, len(peers))


def _peers_all():
    x, y, c = _mesh_pos()
    return [(1 - x if k & 4 else x, 1 - y if k & 2 else y, 1 - c if k & 1 else c) for k in range(1, N_DEV)]


def _peers_first_level():
    x, y, c = _mesh_pos()
    return [(x, y, 1 - c), (1 - x, y, c), (x, 1 - y, c), (1 - x, 1 - y, c)]


def _peers_sibling():
    x, y, c = _mesh_pos()
    return [(x, y, 1 - c)]


def _split_start(copies_of, srcs, lands, n_sems, name, peers_of, collective_id):
    ns, nl, k = len(srcs), len(lands), len(n_sems)

    def body(*refs):
        _handshake(peers_of())
        src_refs, land_refs = refs[:ns], refs[ns:ns + nl]
        sems = refs[ns + nl:ns + nl + k]
        token = refs[-1]
        for cp in copies_of(src_refs, land_refs, sems):
            cp.start()
        token[...] = jnp.zeros_like(token)

    outs = pl.pallas_call(
        body, name=name,
        out_shape=[pltpu.SemaphoreType.DMA((q,)) for q in n_sems]
        + [pltpu.HBM(a.shape, a.dtype) for a in list(srcs) + list(lands)] + [TOKEN],
        in_specs=[HBM_SPEC] * (ns + nl),
        out_specs=[SEM_SPEC] * k + [HBM_SPEC] * (ns + nl) + [VMEM_WHOLE],
        input_output_aliases={i: k + i for i in range(ns + nl)},
        compiler_params=pltpu.CompilerParams(has_side_effects=EFFECT, collective_id=collective_id),
    )(*[_in_hbm(a) for a in list(srcs) + list(lands)])
    return outs[:k], outs[k:k + ns], outs[k + ns:k + ns + nl], outs[-1]


def _split_wait(copies_of, handle, after, name):
    sems, srcs, lands, _ = handle
    ns, nl, k = len(srcs), len(lands), len(sems)

    def body(*refs):
        src_refs, land_refs = refs[:ns], refs[ns:ns + nl]
        sem_refs = refs[ns + nl:ns + nl + k]
        for cp in copies_of(src_refs, land_refs, sem_refs):
            cp.wait()

    outs = pl.pallas_call(
        body, name=name,
        out_shape=[pltpu.HBM(a.shape, a.dtype) for a in list(srcs) + list(lands)],
        in_specs=[HBM_SPEC] * (ns + nl) + [SEM_SPEC] * k + [ANY_SPACE],
        out_specs=[HBM_SPEC] * (ns + nl),
        input_output_aliases={i: i for i in range(ns + nl)},
        compiler_params=pltpu.CompilerParams(has_side_effects=EFFECT),
    )(*srcs, *lands, *sems, after)
    return outs[ns:]


def _gather_first_copies(shard_refs, land_refs, sems):
    send_sems, recv_sems, local_sems = sems
    x, y, c = _mesh_pos()
    me = _flat(x, y, c)
    peers = [(x, y, 1 - c), (1 - x, y, c), (x, 1 - y, c), (1 - x, 1 - y, c)]
    copies = []
    for a, (shard, land) in enumerate(zip(shard_refs, land_refs)):
        copies.append(pltpu.make_async_copy(shard, land.at[me], local_sems.at[a]))
        for k, peer in enumerate(peers):
            copies.append(pltpu.make_async_remote_copy(
                src_ref=shard, dst_ref=land.at[me], send_sem=send_sems.at[4 * a + k], recv_sem=recv_sems.at[4 * a + k],
                device_id=peer, device_id_type=MESH_ID))
    return copies


def _gather_forward_copies(src_refs, land_refs, sems):
    del src_refs
    send_sems, recv_sems = sems
    x, y, c = _mesh_pos()
    chips = [(1 - x, y), (x, 1 - y), (1 - x, 1 - y)]
    copies = []
    for a, land in enumerate(land_refs):
        for j, chip in enumerate(chips):
            rows = land.at[_flat(*chip, c)]
            copies.append(pltpu.make_async_remote_copy(
                src_ref=rows, dst_ref=rows, send_sem=send_sems.at[3 * a + j], recv_sem=recv_sems.at[3 * a + j],
                device_id=(x, y, 1 - c), device_id_type=MESH_ID))
    return copies


def _gather_first_start(groups, name):
    shards = [s for g in groups for s in g]
    lands = [lax.empty((N_DEV,) + s.shape, s.dtype) for s in shards]
    bounds = [sum(len(g) for g in groups[:i]) for i in range(len(groups) + 1)]

    def copies_of(src_refs, land_refs, sems):
        copies = []
        for i in range(len(groups)):
            lo, hi = bounds[i], bounds[i + 1]
            copies += _gather_first_copies(src_refs[lo:hi], land_refs[lo:hi], sems[3 * i:3 * i + 3])
        return copies

    n_sems = tuple(q for g in groups for q in (4 * len(g), 4 * len(g), len(g)))
    sems, srcs, lands, token = _split_start(copies_of, shards, lands, n_sems, name, _peers_first_level, START_IDS[name])
    return [(sems[3 * i:3 * i + 3], srcs[bounds[i]:bounds[i + 1]], lands[bounds[i]:bounds[i + 1]], token)
            for i in range(len(groups))]


def _gather_forward_start(lands, name):
    n = len(lands)
    return _split_start(_gather_forward_copies, [], lands, (3 * n, 3 * n), name, _peers_sibling, START_IDS[name])


def _all_to_all_copies(n_scattered):
    def copies_of(src_refs, land_refs, sems):
        send_sems, recv_sems, local_sems = sems
        x, y, c = _mesh_pos()
        me = _flat(x, y, c)
        copies = []
        for a, (src, land) in enumerate(zip(src_refs, land_refs)):
            scattered = a < n_scattered
            copies.append(pltpu.make_async_copy(src.at[me] if scattered else src, land.at[me], local_sems.at[a]))
            for k in range(1, N_DEV):
                peer = (1 - x if k & 4 else x, 1 - y if k & 2 else y, 1 - c if k & 1 else c)
                copies.append(pltpu.make_async_remote_copy(
                    src_ref=src.at[_flat(*peer)] if scattered else src, dst_ref=land.at[me],
                    send_sem=send_sems.at[7 * a + k - 1], recv_sem=recv_sems.at[7 * a + k - 1],
                    device_id=peer, device_id_type=MESH_ID))
        return copies
    return copies_of


def _all_to_all_start(scattered, broadcast, name):
    srcs = list(scattered) + list(broadcast)
    lands = [lax.empty(a.shape, a.dtype) for a in scattered] + [lax.empty((N_DEV,) + a.shape, a.dtype) for a in broadcast]
    n = len(srcs)
    return _split_start(_all_to_all_copies(len(scattered)), srcs, lands, (7 * n, 7 * n, n), name, _peers_all,
                        START_IDS[name])


def _call_behind(deps, body, *, in_specs, **kwargs):
    n_in, n_dep = len(in_specs), len(deps)

    def body_without_deps(*refs):
        return body(*refs[:n_in], *refs[n_in + n_dep:])

    call = pl.pallas_call(body_without_deps, in_specs=list(in_specs) + [ANY_SPACE] * n_dep, **kwargs)
    return lambda *operands: call(*operands, *deps)


def _row_tile(rows):
    if rows <= 2 * 256:
        return rows
    for cand in (256, 128, 64, 32, 16):
        if rows % cand == 0:
            return cand
    return rows


def _adamw_whole(groups, name):
    n = len(groups)

    def body(*refs):
        for i in range(n):
            g_ref, w_ref, m_ref, v_ref = refs[4 * i:4 * i + 4]
            d_ref, nm_ref, nv_ref = refs[4 * n + 3 * i:4 * n + 3 * i + 3]
            d_ref[...], nm_ref[...], nv_ref[...] = _adam_update(g_ref[...], w_ref[...], m_ref[...], v_ref[...])

    outs = pl.pallas_call(
        body, name=name, out_shape=[jax.ShapeDtypeStruct(grp[0].shape, F32) for grp in groups for _ in range(3)],
        in_specs=[VMEM_WHOLE] * (4 * n), out_specs=[VMEM_WHOLE] * (3 * n),
        compiler_params=_params(),
    )(*[t for grp in groups for t in grp])
    return [outs[3 * i:3 * i + 3] for i in range(n)]


def _sum_sources_whole(recvs, name):
    n = len(recvs)

    def body(*refs):
        for r_ref, o_ref in zip(refs[:n], refs[n:]):
            acc = r_ref[0].astype(F32)
            for d in range(1, N_DEV):
                acc = acc + r_ref[d].astype(F32)
            o_ref[...] = acc

    return pl.pallas_call(
        body, name=name, out_shape=[jax.ShapeDtypeStruct(r.shape[1:], F32) for r in recvs],
        in_specs=[VMEM_WHOLE] * n, out_specs=[VMEM_WHOLE] * n,
        compiler_params=_params(),
    )(*recvs)


def _sum_adamw(recv, w, m, v, name):
    _, rows, cols = recv.shape
    tile = _row_tile(rows)

    def body(r_ref, w_ref, m_ref, v_ref, g_ref, d_ref, nm_ref, nv_ref):
        acc = r_ref[0].astype(F32)
        for d in range(1, N_DEV):
            acc = acc + r_ref[d].astype(F32)
        g_ref[...] = acc
        d_ref[...], nm_ref[...], nv_ref[...] = _adam_update(acc, w_ref[...], m_ref[...], v_ref[...])

    spec = pl.BlockSpec((tile, cols), lambda i: (i, 0))
    shp = jax.ShapeDtypeStruct((rows, cols), F32)
    return pl.pallas_call(
        body, name=name, grid=(rows // tile,), out_shape=[shp] * 4,
        in_specs=[pl.BlockSpec((N_DEV, tile, cols), lambda i: (0, i, 0)), spec, spec, spec], out_specs=[spec] * 4,
        compiler_params=_params(("parallel",)),
    )(recv, w, m, v)


SMALL_SLOTS = {"norm_x_g": (0, 0, 1, D_MODEL), "norm_mem_g": (0, 8, 1, D_MODEL), "norm_ffn_g": (0, 16, 1, D_MODEL),
               "final_norm_g": (0, 24, 1, D_MODEL), "pool_scale": (0, 32, 1, HW),
               "norm_mix_g": (1, 0, 1, D_MODEL), "lb_logits": (1, 8, 2, HW), "hgrn_norm_g": (1, 16, HEADS, HD)}
LOSS_ROW = 25
SMALL_ORDER = ("norm_mix_g", "lb_logits", "hgrn_norm_g", "pool_scale", "norm_x_g", "norm_mem_g", "norm_ffn_g",
               "final_norm_g", "w_pool")


def _small_update(srecvs, wprecv, params):
    flat = [t for n in SMALL_ORDER for t in params[n]]
    nb = len(srecvs)
    n_in = nb + 1 + len(flat)

    def body(*refs):
        s_refs, wp_ref = refs[0:nb], refs[nb]
        in_refs = refs[nb + 1:n_in]
        loss_ref = refs[n_in]
        out_refs = refs[n_in + 1:-nb]
        accs = refs[-nb:]
        for s_ref, acc in zip(s_refs, accs):
            total = s_ref[0]
            for d in range(1, N_DEV):
                total = total + s_ref[d]
            acc[...] = total
        loss_ref[...] = accs[0][LOSS_ROW:LOSS_ROW + 1, 0:1]
        for i, name in enumerate(SMALL_ORDER):
            w_ref, m_ref, v_ref = in_refs[3 * i:3 * i + 3]
            g_ref, d_ref, nm_ref, nv_ref = out_refs[4 * i:4 * i + 4]
            if name == "w_pool":
                g = wp_ref[0]
                for d in range(1, N_DEV):
                    g = g + wp_ref[d]
            else:
                buf, r0, nr, nc = SMALL_SLOTS[name]
                g = accs[buf][r0:r0 + nr, 0:nc]
            g_ref[...] = g
            d_ref[...], nm_ref[...], nv_ref[...] = _adam_update(g, w_ref[...], m_ref[...], v_ref[...])

    out_shape = [jax.ShapeDtypeStruct((1, 1), F32)]
    for n in SMALL_ORDER:
        out_shape += [jax.ShapeDtypeStruct(params[n][0].shape, F32)] * 4
    outs = pl.pallas_call(
        body, name="small_update", out_shape=out_shape,
        in_specs=[VMEM_WHOLE] * n_in, out_specs=[VMEM_WHOLE] * len(out_shape),
        scratch_shapes=[pltpu.VMEM(r.shape[1:], F32) for r in srecvs],
        compiler_params=_params(),
    )(*srecvs, wprecv, *flat)
    return outs[0], {n: outs[1 + 4 * i:5 + 4 * i] for i, n in enumerate(SMALL_ORDER)}


def _in_proj(x, g, w_t, deps):
    s = x.shape[0]
    tm = min(ROW_TILE, s)

    def body(x_ref, g_ref, w_ref, z_ref, h_ref):
        xv = x_ref[...]
        h = (xv * _rms(xv) * g_ref[...]).astype(BF16)
        h_ref[...] = h
        z_ref[...] = _mm_nt(h, w_ref[...])

    return _call_behind(
        deps, body, name="in_proj", grid=(s // tm,),
        out_shape=[jax.ShapeDtypeStruct((s, IN_WIDTH), F32), jax.ShapeDtypeStruct((s, D_MODEL), BF16)],
        in_specs=[pl.BlockSpec((tm, D_MODEL), lambda i: (i, 0)), _full((1, D_MODEL)), VMEM_WHOLE],
        out_specs=[pl.BlockSpec((tm, IN_WIDTH), lambda i: (i, 0)), pl.BlockSpec((tm, D_MODEL), lambda i: (i, 0))],
        compiler_params=_params(("parallel",)),
    )(x, g, w_t)


def _chunk_masks():
    row = lax.broadcasted_iota(jnp.int32, (CHUNK, CHUNK), 0)
    col = lax.broadcasted_iota(jnp.int32, (CHUNK, CHUNK), 1)
    return row, col


def _ones_where(mask):
    return jnp.where(mask, 1.0, 0.0).astype(BF16)


def _hgrn_gates(zq, zf, lb):
    sq = _sigmoid(zq)
    sig = _sigmoid(zf)
    f = lb + (1.0 - lb) * sig
    return zq * sq, sq, sig, f


def _sub_chunk_masks(width):
    trow = lax.broadcasted_iota(jnp.int32, (CHUNK, width), 0)
    return [(trow >= SUB * j) & (trow < SUB * (j + 1)) for j in range(N_SUB)]


def _head(a, h):
    return a[:, HD * h:HD * (h + 1)]


def _lanes(parts):
    return jnp.concatenate(parts, axis=1)


def _hgrn_decay_factors(b_scr, r0, b, in_sub):
    bases = [jnp.zeros((1, HW), F32)] + [b_scr[r0 + SUB * j - 1:r0 + SUB * j, :] for j in range(1, N_SUB)]
    own_base = bases[N_SUB - 1]
    for j in range(N_SUB - 2, -1, -1):
        own_base = jnp.where(in_sub[j], bases[j], own_base)
    eq = jnp.exp(b - own_base)
    ek = []
    for j in range(N_SUB):
        upto = SUB * (j + 1)
        e = jnp.exp(jnp.minimum(bases[j] - b[0:upto], EXP_CAP))
        ek.append(e if upto == CHUNK else jnp.concatenate([e, jnp.zeros((CHUNK - upto, HW), F32)], axis=0))
    return eq, ek


def _per_sub_chunk(x, in_sub):
    return _lanes([jnp.where(in_sub[j], x, 0.0) for j in range(N_SUB)])


def _own_lane_block(a, in_sub):
    out = a[:, HD * (N_SUB - 1):HD * N_SUB]
    for j in range(N_SUB - 2, -1, -1):
        out = jnp.where(in_sub[j], a[:, HD * j:HD * (j + 1)], out)
    return out


def _head_rms(o):
    return _lanes([jnp.broadcast_to(_rms(_head(o, h)), (CHUNK, HD)) for h in range(HEADS)])


def _head_mean(a):
    return _lanes([jnp.broadcast_to(jnp.mean(_head(a, h), axis=-1, keepdims=True), (CHUNK, HD)) for h in range(HEADS)])


def _hgrn_fwd(z, lb_logits, gn):
    s = z.shape[0]
    n_chunks = s // CHUNK

    def body(zq_ref, zf_ref, zi_ref, zg_ref, lbl_ref, gn_ref, oa_ref, o_ref, st_ref, state, b_scr):
        @pl.when(pl.program_id(0) == 0)
        def _():
            state[...] = jnp.zeros_like(state)

        lb = _sigmoid(lbl_ref[0:1, :] - lbl_ref[1:2, :])
        row, col = _chunk_masks()
        causal = col <= row
        tri = _ones_where(causal)
        in_sub, in_sub_head = _sub_chunk_masks(HW), _sub_chunk_masks(HD)
        gn_row = _lanes([gn_ref[h:h + 1, :] for h in range(HEADS)])
        def front(c):
            r0 = CHUNK * c
            rs = slice(r0, r0 + CHUNK)
            q, _, _, f = _hgrn_gates(zq_ref[rs, :], zf_ref[rs, :], lb)
            kk = 1.0 - f
            b = _tri_dot(tri, jnp.log(f), 3)
            b_scr[rs, :] = b
            eq, ek = _hgrn_decay_factors(b_scr, r0, b, in_sub)
            b_last = b_scr[r0 + CHUNK - 1:r0 + CHUNK, :]
            qe = q * eq
            return {"rs": rs, "v": zi_ref[rs, :], "qg": q * jnp.exp(b), "kd": kk * jnp.exp(b_last - b),
                    "lam_last": jnp.exp(b_last),
                    "q16": [_per_sub_chunk(_head(qe, h), in_sub_head).astype(BF16) for h in range(HEADS)],
                    "ke16": [_lanes([_head(kk * e, h) for e in ek]).astype(BF16) for h in range(HEADS)]}

        def recurrence(c, p):
            st_ref[c] = state[...]
            a, o_inter = [], []
            for h in range(HEADS):
                vh, st = _head(p["v"], h), state[h]
                a.append(jnp.where(causal, _mm_nt(p["q16"][h], p["ke16"][h]), 0.0))
                o_inter.append(_mm_nt(_head(p["qg"], h), st))
                state[h] = st * _head(p["lam_last"], h) + _mm_tn(vh, _head(p["kd"], h))
            return _lanes([_mm(a[h], _head(p["v"], h)) + o_inter[h] for h in range(HEADS)])

        def back(p, o):
            rs = p["rs"]
            o_ref[rs, :] = o
            zg = zg_ref[rs, :]
            oa_ref[rs, :] = (o * _head_rms(o) * gn_row * zg * _sigmoid(zg)).astype(BF16)

        p = front(0)
        for c in range(CHUNKS_PER_STEP):
            o = recurrence(c, p)
            p_next = front(c + 1) if c + 1 < CHUNKS_PER_STEP else None
            back(p, o)
            p = p_next

    rows = CHUNK * CHUNKS_PER_STEP
    zspec = lambda cb: pl.BlockSpec((rows, HW), lambda i, cb=cb: (i, cb))
    return pl.pallas_call(
        body, name="hgrn_fwd", grid=(s // rows,),
        out_shape=[jax.ShapeDtypeStruct((s, 2 * HW), BF16), jax.ShapeDtypeStruct((s, HW), F32),
                   jax.ShapeDtypeStruct((n_chunks, HEADS, HD, HD), F32)],
        in_specs=[zspec(0), zspec(1), zspec(2), zspec(3), _full((2, HW)), _full((HEADS, HD))],
        out_specs=[pl.BlockSpec((rows, HW), lambda i: (i, 0)), pl.BlockSpec((rows, HW), lambda i: (i, 0)),
                   pl.BlockSpec((CHUNKS_PER_STEP, HEADS, HD, HD), lambda i: (i, 0, 0, 0))],
        scratch_shapes=[pltpu.VMEM((HEADS, HD, HD), F32), pltpu.VMEM((rows, HW), F32)],
        compiler_params=_params(("arbitrary",)),
    )(z, z, z, z, lb_logits, gn)


def _pool_counts(tile_idx, tm):
    t = tile_idx * tm + lax.broadcasted_iota(jnp.int32, (tm, 1), 0)
    return [1.0 / jnp.minimum(t + 1, w).astype(F32) for w in POOL_WINDOWS]


def _pool_fwd(z, w_pool, scale, mixed_in, deps):
    s = z.shape[0]
    tm = min(ROW_TILE, s)

    def body(p_ref, w_ref, sc_ref, mixin_ref, ob_ref, pooled_ref, ext):
        i = pl.program_id(0)

        @pl.when(i == 0)
        def _():
            ext[0:POOL_HALO, :] = jnp.zeros((POOL_HALO, HW), F32)

        @pl.when(i > 0)
        def _():
            ext[0:POOL_HALO, :] = ext[tm:tm + POOL_HALO, :]

        ext[POOL_HALO:POOL_HALO + tm, :] = p_ref[...]
        inv = _pool_counts(i, tm)
        for g, w in enumerate(POOL_WINDOWS):
            sl = slice(HD * g, HD * (g + 1))
            p = ext[POOL_HALO:POOL_HALO + tm, sl]
            win = p
            for d in range(1, w):
                win = win + ext[POOL_HALO - d:POOL_HALO - d + tm, sl]
            pooled = (win * inv[g] - p).astype(BF16)
            pooled_ref[:, sl] = pooled
            ob_ref[:, sl] = (_mm(pooled, w_ref[g]) * sc_ref[:, sl]).astype(BF16)

    return _call_behind(
        deps, body, name="pool_fwd", grid=(s // tm,),
        out_shape=[jax.ShapeDtypeStruct((s, 2 * HW), BF16), jax.ShapeDtypeStruct((s, HW), BF16)],
        in_specs=[pl.BlockSpec((tm, HW), lambda i: (i, 4)), _full((HEADS, HD, HD)), _full((1, HW)), ANY_SPACE],
        out_specs=[pl.BlockSpec((tm, HW), lambda i: (i, 1)), pl.BlockSpec((tm, HW), lambda i: (i, 0))],
        scratch_shapes=[pltpu.VMEM((tm + POOL_HALO, HW), F32)],
        input_output_aliases={3: 0},
        compiler_params=_params(("arbitrary",)),
    )(z, w_pool, scale, mixed_in)


def _mem_kv(mem, g, wk, wv, deps):
    def body(m_ref, g_ref, wk_ref, wv_ref, hm_ref, k_ref, v_ref):
        m = m_ref[...]
        hm = (m * _rms(m) * g_ref[...]).astype(BF16)
        hm_ref[...] = hm
        k_ref[...] = _mm(hm, wk_ref[...]).astype(BF16)
        v_ref[...] = _mm(hm, wv_ref[...]).astype(BF16)

    shp = jax.ShapeDtypeStruct((MEM_LEN, D_MODEL), BF16)
    return _call_behind(
        deps, body, name="mem_kv", out_shape=[shp, shp, shp],
        in_specs=[VMEM_WHOLE] * 4, out_specs=[VMEM_WHOLE] * 3,
        compiler_params=_params(),
    )(mem, g, wk, wv)


def _softmax_rows(sc):
    e = jnp.exp(sc - jnp.max(sc, axis=-1, keepdims=True))
    return e / jnp.sum(e, axis=-1, keepdims=True)


def _mix_xattn_fwd(x0, mixed, w_out, g, wq, xk, xv, wo_t, deps):
    s = x0.shape[0]
    tm = min(ROW_TILE, s)
    scale = XHD ** -0.5

    def body(x_ref, mix_ref, wout_ref, g_ref, wq_ref, k_ref, v_ref, wo_ref, x1_ref, o_ref, hq_ref, q_ref, att_ref):
        xv_ = x_ref[...] + _mm(mix_ref[...], wout_ref[...])
        x1_ref[...] = xv_
        hq = (xv_ * _rms(xv_) * g_ref[...]).astype(BF16)
        hq_ref[...] = hq
        q_ref[...] = (_mm(hq, wq_ref[...]) * scale).astype(BF16)
        heads = [slice(XHD * h, XHD * (h + 1)) for h in range(HEADS)]
        scores = [_mm_nt(q_ref[:, sl], k_ref[:, sl]) for sl in heads]
        probs = [_softmax_rows(sc) for sc in scores]
        for sl, p in zip(heads, probs):
            att_ref[:, sl] = _mm(p, v_ref[:, sl]).astype(BF16)
        o_ref[...] = xv_ + _mm_nt(att_ref[...], wo_ref[...])

    row_f32 = pl.BlockSpec((tm, D_MODEL), lambda i: (i, 0))
    bshape = jax.ShapeDtypeStruct((s, D_MODEL), BF16)
    fshape = jax.ShapeDtypeStruct((s, D_MODEL), F32)
    return _call_behind(
        deps, body, name="mix_xattn_fwd", grid=(s // tm,),
        out_shape=[fshape, fshape, bshape, bshape, bshape],
        in_specs=[row_f32, row_f32, VMEM_WHOLE, _full((1, D_MODEL)), VMEM_WHOLE, VMEM_WHOLE, VMEM_WHOLE, VMEM_WHOLE],
        out_specs=[row_f32] * 5,
        compiler_params=_params(("parallel",)),
    )(x0, mixed, w_out, g, wq, xk, xv, wo_t)


def _mlp_fwd_loss(x, g, w1, w2, gf, target):
    s = x.shape[0]
    tm = min(ROW_TILE, s)

    def body(x_ref, g_ref, w1_ref, w2_ref, gf_ref, t_ref, dx_ref, dx16_ref, u_ref, hf_ref, slot_ref):
        @pl.when(pl.program_id(0) == 0)
        def _():
            slot_ref[...] = jnp.zeros_like(slot_ref)

        xv = x_ref[...]
        hf = (xv * _rms(xv) * g_ref[...]).astype(BF16)
        hf_ref[...] = hf
        a_next = _mm(hf, w1_ref[0])
        for j in range(N_DEV):
            a = jnp.maximum(a_next, 0.0)
            if j + 1 < N_DEV:
                a_next = _mm(hf, w1_ref[j + 1])
            u_ref[:, FF_BLK * j:FF_BLK * (j + 1)] = (a * a).astype(BF16)
        acc = xv + _mm(u_ref[...], w2_ref[...])
        gfv = gf_ref[...]
        r = _rms(acc)
        n = acc * r
        err = n * gfv - t_ref[...]
        slot_ref[1:2, :] += jnp.sum(jnp.mean(err * err, axis=-1, keepdims=True), axis=0, keepdims=True) * 0.5
        dy = err * (1.0 / D_MODEL)
        slot_ref[0:1, :] += jnp.sum(dy * n, axis=0, keepdims=True)
        dn = dy * gfv
        dx = r * (dn - n * jnp.mean(dn * n, axis=-1, keepdims=True))
        dx_ref[...] = dx
        dx16_ref[...] = dx.astype(BF16)

    row_f32 = pl.BlockSpec((tm, D_MODEL), lambda i: (i, 0))
    return pl.pallas_call(
        body, name="mlp_fwd_loss", grid=(s // tm,),
        out_shape=[jax.ShapeDtypeStruct((s, D_MODEL), F32), jax.ShapeDtypeStruct((s, D_MODEL), BF16),
                   jax.ShapeDtypeStruct((s, D_FF), BF16), jax.ShapeDtypeStruct((s, D_MODEL), BF16),
                   jax.ShapeDtypeStruct((SLOT, D_MODEL), F32)],
        in_specs=[row_f32, _full((1, D_MODEL)), VMEM_WHOLE, VMEM_WHOLE, _full((1, D_MODEL)), row_f32],
        out_specs=[row_f32, row_f32, pl.BlockSpec((tm, D_FF), lambda i: (i, 0)), row_f32, _full((SLOT, D_MODEL))],
        compiler_params=_params(("arbitrary",)),
    )(x, g, w1, w2, gf, target)


def _zero_slot(slot_ref):
    @pl.when(pl.program_id(0) == 0)
    def _():
        slot_ref[...] = jnp.zeros_like(slot_ref)


def _mlp_bwd(dx3, u, x2, g, w1, w2, deps):
    s = x2.shape[0]
    tm = min(ROW_TILE // 2, s)

    def body(d_ref, u_ref, x_ref, g_ref, w1_ref, w2_ref, da_ref, dx_ref, slot_ref):
        _zero_slot(slot_ref)
        d = d_ref[...]
        d16 = d.astype(BF16)
        du_next = _mm_nt(d16, w2_ref[0])
        dhf = jnp.zeros((tm, D_MODEL), F32)
        for j in range(N_DEV):
            sl = slice(FF_BLK * j, FF_BLK * (j + 1))
            du = du_next
            if j + 1 < N_DEV:
                du_next = _mm_nt(d16, w2_ref[j + 1])
            u = u_ref[:, sl].astype(F32)
            da = (du * (2.0 * u * lax.rsqrt(jnp.maximum(u, TINY)))).astype(BF16)
            da_ref[:, sl] = da
            dhf = dhf + _mm_nt(da, w1_ref[j])
        dx, dg = _rms_bwd(x_ref[...], g_ref[...], dhf)
        dx_ref[...] = d + dx
        slot_ref[0:1, :] += dg

    row_f32 = pl.BlockSpec((tm, D_MODEL), lambda i: (i, 0))
    return _call_behind(
        deps, body, name="mlp_bwd", grid=(s // tm,),
        out_shape=[jax.ShapeDtypeStruct((s, D_FF), BF16), jax.ShapeDtypeStruct((s, D_MODEL), F32),
                   jax.ShapeDtypeStruct((SLOT, D_MODEL), F32)],
        in_specs=[row_f32, pl.BlockSpec((tm, D_FF), lambda i: (i, 0)), row_f32, _full((1, D_MODEL)),
                  VMEM_WHOLE, VMEM_WHOLE],
        out_specs=[pl.BlockSpec((tm, D_FF), lambda i: (i, 0)), row_f32, _full((SLOT, D_MODEL))],
        compiler_params=_params(("arbitrary",)),
    )(dx3, u, x2, g, w1, w2)


def _wgrad(a, b, name, col_blocks=False, update=None):
    s, m = a.shape
    n = b.shape[1]
    tm = 1280 if m % 1280 == 0 else min(1024, m)
    tn = min(1024, n)
    blk = n // N_DEV
    per_step = tn // blk if col_blocks else 1
    ts = min((4 if m * n >= D_MODEL * D_FF else 2) * ROW_TILE, s)
    n_s = s // ts
    grid = (m // tm, n // tn, n_s)

    def body(a_ref, b_ref, *rest):
        o_ref, acc = rest[-2], rest[-1]
        k = pl.program_id(2)

        @pl.when(k == 0)
        def _():
            acc[...] = jnp.zeros_like(acc)

        acc[...] += _mm_tn(a_ref[...], b_ref[...])
        if update is not None:
            r_ref, w_ref, m_ref, v_ref, g_ref, d_ref, nm_ref, nv_ref = rest[:8]
            g = r_ref[0].astype(F32)
            for d in range(1, N_DEV):
                g = g + r_ref[d].astype(F32)
            g_ref[...] = g
            d_ref[...], nm_ref[...], nv_ref[...] = _adam_update(g, w_ref[...], m_ref[...], v_ref[...])

        @pl.when(k == n_s - 1)
        def _():
            if col_blocks:
                for p in range(per_step):
                    o_ref[p] = acc[:, blk * p:blk * (p + 1)].astype(BF16)
            else:
                o_ref[...] = acc[...].astype(BF16)

    if col_blocks:
        out_shape = jax.ShapeDtypeStruct((N_DEV, m, blk), BF16)
        out_spec = pl.BlockSpec((per_step, tm, blk), lambda i, j, k: (j, i, 0))
    else:
        out_shape = jax.ShapeDtypeStruct((m, n), BF16)
        out_spec = pl.BlockSpec((tm, tn), lambda i, j, k: (i, j))
    in_specs = [pl.BlockSpec((ts, tm), lambda i, j, k: (k, i)), pl.BlockSpec((ts, tn), lambda i, j, k: (k, j))]
    out_shapes, out_specs, operands = [out_shape], [out_spec], [a, b]
    if update is not None:
        rows, cols = update[1].shape
        steps = grid[0] * grid[1] * grid[2]
        tr = rows // steps
        step = lambda i, j, k: (i * grid[1] + j) * grid[2] + k
        piece = pl.BlockSpec((tr, cols), lambda i, j, k: (step(i, j, k), 0))
        in_specs += [pl.BlockSpec((N_DEV, tr, cols), lambda i, j, k: (0, step(i, j, k), 0)), piece, piece, piece]
        out_shapes = [jax.ShapeDtypeStruct((rows, cols), F32)] * 4 + out_shapes
        out_specs = [piece] * 4 + out_specs
        operands += list(update)
    outs = pl.pallas_call(
        body, name=name, grid=grid, out_shape=out_shapes, in_specs=in_specs, out_specs=out_specs,
        scratch_shapes=[pltpu.VMEM((tm, tn), F32)],
        compiler_params=_params(("parallel", "parallel", "arbitrary")),
    )(*operands)
    return outs[0] if update is None else (outs[4], tuple(outs[:4]))


def _xattn_bwd(dx2, x1, g, q, xk, xv, wq, wo_t, deps):
    s = x1.shape[0]
    tm = min(WIDE_ROW_TILE, s)
    scale = XHD ** -0.5

    def body(d_ref, x_ref, g_ref, q_ref, k_ref, v_ref, wq_ref, wo_ref, dx_ref, dx16_ref, dq_ref, dk_ref, dv_ref, slot_ref,
             datt):
        _zero_slot(slot_ref)

        @pl.when(pl.program_id(0) == 0)
        def _():
            dk_ref[...] = jnp.zeros_like(dk_ref)
            dv_ref[...] = jnp.zeros_like(dv_ref)

        d = d_ref[...]
        datt[...] = _mm(d, wo_ref[...]).astype(BF16)
        heads = [slice(XHD * h, XHD * (h + 1)) for h in range(HEADS)]
        scores = [_mm_nt(q_ref[:, sl], k_ref[:, sl]) for sl in heads]
        dps = [_mm_nt(datt[:, sl], v_ref[:, sl]) for sl in heads]
        probs = [_softmax_rows(sc) for sc in scores]
        dss = [(p * (dp - jnp.sum(dp * p, axis=-1, keepdims=True))).astype(BF16) for p, dp in zip(probs, dps)]
        for sl, p, ds in zip(heads, probs, dss):
            dq_ref[:, sl] = (_mm(ds, k_ref[:, sl]) * scale).astype(BF16)
            dk_ref[:, sl] += _mm_tn(ds, q_ref[:, sl])
            dv_ref[:, sl] += _mm_tn(p, datt[:, sl])
        dx, dg = _rms_bwd(x_ref[...], g_ref[...], _mm_nt(dq_ref[...], wq_ref[...]))
        dx_ref[...] = d + dx
        dx16_ref[...] = (d + dx).astype(BF16)
        slot_ref[0:1, :] += dg

    row_f32 = pl.BlockSpec((tm, D_MODEL), lambda i: (i, 0))
    kv = jax.ShapeDtypeStruct((MEM_LEN, D_MODEL), F32)
    tokens16 = jax.ShapeDtypeStruct((s, D_MODEL), BF16)
    return _call_behind(
        deps, body, name="xattn_bwd", grid=(s // tm,),
        out_shape=[jax.ShapeDtypeStruct((s, D_MODEL), F32), tokens16, tokens16, kv, kv,
                   jax.ShapeDtypeStruct((SLOT, D_MODEL), F32)],
        in_specs=[row_f32, row_f32, _full((1, D_MODEL)), row_f32, VMEM_WHOLE, VMEM_WHOLE, VMEM_WHOLE, VMEM_WHOLE],
        out_specs=[row_f32, row_f32, row_f32, _full((MEM_LEN, D_MODEL)), _full((MEM_LEN, D_MODEL)),
                   _full((SLOT, D_MODEL))],
        scratch_shapes=[pltpu.VMEM((tm, D_MODEL), BF16)],
        compiler_params=_params(("arbitrary",)),
    )(dx2, x1, g, q, xk, xv, wq, wo_t)


def _mem_bwd(mem, g, hm, dxk, dxv, wk, wv):
    def body(m_ref, g_ref, hm_ref, dk_ref, dv_ref, wk_ref, wv_ref, dwk_ref, dwv_ref, slot_ref):
        dk, dv = dk_ref[...], dv_ref[...]
        hm_ = hm_ref[...]
        dwk_ref[...] = _mm_tn(hm_, dk).astype(BF16)
        dwv_ref[...] = _mm_tn(hm_, dv).astype(BF16)
        _, dg = _rms_bwd(m_ref[...], g_ref[...], _mm_nt(dk, wk_ref[...]) + _mm_nt(dv, wv_ref[...]))
        slot_ref[...] = jnp.zeros_like(slot_ref)
        slot_ref[0:1, :] = dg

    wshape = jax.ShapeDtypeStruct((D_MODEL, D_MODEL), BF16)
    return pl.pallas_call(
        body, name="mem_bwd", out_shape=[wshape, wshape, jax.ShapeDtypeStruct((SLOT, D_MODEL), F32)],
        in_specs=[VMEM_WHOLE] * 7, out_specs=[VMEM_WHOLE] * 3,
        compiler_params=_params(),
    )(mem, g, hm, dxk, dxv, wk, wv)


def _pool_bwd(dx1, w_out, pooled, w_pool, scale, deps):
    s = dx1.shape[0]
    tm = min(ROW_TILE, s)
    n_t = s // tm

    def body(dx_ref, wo_ref, pl_ref, w_ref, sc_ref, dz_ref, dw_ref, slot_ref, ext, do_ref):
        i = pl.program_id(0)
        tile = n_t - 1 - i
        _zero_slot(slot_ref)
        do_ref[...] = _mm_nt(dx_ref[...], wo_ref[HW:2 * HW, :])

        @pl.when(i == 0)
        def _():
            dw_ref[...] = jnp.zeros_like(dw_ref)
            ext[tm:tm + POOL_HALO, :] = jnp.zeros((POOL_HALO, HW), F32)

        @pl.when(i > 0)
        def _():
            ext[tm:tm + POOL_HALO, :] = ext[0:POOL_HALO, :]

        inv = _pool_counts(tile, tm)
        dpooled = []
        for g in range(HEADS):
            sl = slice(HD * g, HD * (g + 1))
            pooled_g = pl_ref[:, sl]
            do = do_ref[:, sl]
            slot_ref[0:1, sl] += jnp.sum(_mm(pooled_g, w_ref[g]) * do, axis=0, keepdims=True)
            dy = (do * sc_ref[:, sl]).astype(BF16)
            dw_ref[g] += _mm_tn(pooled_g, dy)
            dpo = _mm_nt(dy, w_ref[g])
            dpooled.append(dpo)
            ext[0:tm, sl] = dpo * inv[g]
        for g, w in enumerate(POOL_WINDOWS):
            sl = slice(HD * g, HD * (g + 1))
            win = ext[0:tm, sl]
            for d in range(1, w):
                win = win + ext[d:d + tm, sl]
            dz_ref[:, sl] = (win - dpooled[g]).astype(BF16)

    return _call_behind(
        deps, body, name="pool_bwd", grid=(n_t,),
        out_shape=[jax.ShapeDtypeStruct((s, IN_WIDTH), BF16), jax.ShapeDtypeStruct((HEADS, HD, HD), F32),
                   jax.ShapeDtypeStruct((SLOT, D_MODEL), F32)],
        in_specs=[pl.BlockSpec((tm, D_MODEL), lambda i: (n_t - 1 - i, 0)), VMEM_WHOLE,
                  pl.BlockSpec((tm, HW), lambda i: (n_t - 1 - i, 0)), _full((HEADS, HD, HD)), _full((1, HW))],
        out_specs=[pl.BlockSpec((tm, HW), lambda i: (n_t - 1 - i, 4)), _full((HEADS, HD, HD)), _full((SLOT, D_MODEL))],
        scratch_shapes=[pltpu.VMEM((tm + POOL_HALO, HW), F32), pltpu.VMEM((tm, HW), F32)],
        compiler_params=_params(("arbitrary",)),
    )(dx1, w_out, pooled, w_pool, scale)


def _hgrn_bwd(z, o, dx1, w_out, states, lb_logits, gn, dz_in, deps):
    s = z.shape[0]
    n_chunks = s // CHUNK

    def body(zq_ref, zf_ref, zi_ref, zg_ref, o_ref, dx_ref, wo_ref, st_ref, lbl_ref, gn_ref, dzin_ref,
             dz_ref, dlb_ref, dgn_ref, dstate, b_scr, dlb_acc, do_ref):
        i = pl.program_id(0)

        @pl.when(i == 0)
        def _():
            dstate[...] = jnp.zeros_like(dstate)
            dlb_acc[...] = jnp.zeros_like(dlb_acc)
            dgn_ref[...] = jnp.zeros_like(dgn_ref)
            dlb_ref[...] = jnp.zeros_like(dlb_ref)

        do_ref[...] = _mm_nt(dx_ref[...], wo_ref[0:HW, :])
        lb = _sigmoid(lbl_ref[0:1, :] - lbl_ref[1:2, :])
        row, col = _chunk_masks()
        causal = col <= row
        tri = _ones_where(causal)
        upper = _ones_where(col >= row)
        strict_lower = _ones_where(col < row)
        in_sub, in_sub_head = _sub_chunk_masks(HW), _sub_chunk_masks(HD)
        gn_row = _lanes([gn_ref[h:h + 1, :] for h in range(HEADS)])
        sums = {"dlb": 0.0, "dgn": 0.0}

        def front(c):
            r0 = CHUNK * c
            rs = slice(r0, r0 + CHUNK)
            p = {"rs": rs}
            p["zq"] = zq_ref[rs, :]
            p["q"], p["sq"], p["sig"], p["f"] = _hgrn_gates(p["zq"], zf_ref[rs, :], lb)
            p["kk"] = 1.0 - p["f"]
            b = _tri_dot(tri, jnp.log(p["f"]), 3)
            b_scr[rs, :] = b
            p["v"] = zi_ref[rs, :]
            o, zg, doa = o_ref[rs, :], zg_ref[rs, :], do_ref[rs, :]
            sg = _sigmoid(zg)
            rms = _head_rms(o)
            n = o * rms
            don = doa * (zg * sg)
            sums["dgn"] = sums["dgn"] + jnp.sum(don * n, axis=0, keepdims=True)
            dn = don * gn_row
            p["d_o"] = rms * (dn - n * _head_mean(dn * n))
            dz_ref[rs, 3 * HW:4 * HW] = (doa * (n * gn_row) * (sg * (1.0 + zg * (1.0 - sg)))).astype(BF16)
            p["eq"], p["ek"] = _hgrn_decay_factors(b_scr, r0, b, in_sub)
            b_last = b_scr[r0 + CHUNK - 1:r0 + CHUNK, :]
            p["lam"], p["e_last"], p["lam_last"] = jnp.exp(b), jnp.exp(b_last - b), jnp.exp(b_last)
            p["qe"], p["qg"], p["kd"] = p["q"] * p["eq"], p["q"] * p["lam"], p["kk"] * p["e_last"]
            p["ke"] = [p["kk"] * e for e in p["ek"]]
            p["q16"] = [_per_sub_chunk(_head(p["qe"], h), in_sub_head).astype(BF16) for h in range(HEADS)]
            p["ke16"] = [_lanes([_head(p["ke"][j], h) for j in range(N_SUB)]).astype(BF16) for h in range(HEADS)]
            return p

        def recurrence(c, p):
            m = {k: [] for k in ("dv", "gq", "gk", "dqi", "dkd", "st")}
            a, da, dv_state = [], [], []
            for h in range(HEADS):
                vh, doh = _head(p["v"], h), _head(p["d_o"], h)
                st0, ds1 = st_ref[c, h], dstate[h]
                a.append(jnp.where(causal, _mm_nt(p["q16"][h], p["ke16"][h]), 0.0))
                da.append(jnp.where(causal, _mm_nt(doh, vh), 0.0))
                dv_state.append(_mm_nt(_head(p["kd"], h), ds1))
                m["dqi"].append(_mm(doh, st0))
                m["dkd"].append(_mm(vh, ds1))
                m["st"].append(jnp.sum(st0 * ds1, axis=0, keepdims=True))
                dstate[h] = ds1 * _head(p["lam_last"], h) + _mm_tn(doh, _head(p["qg"], h))
            for h in range(HEADS):
                m["dv"].append(_mm_tn(a[h], _head(p["d_o"], h)) + dv_state[h])
                m["gq"].append(_own_lane_block(_mm(da[h], p["ke16"][h]), in_sub_head))
                m["gk"].append(_mm_tn(da[h], p["q16"][h]))
            return m

        def back(p, m):
            rs = p["rs"]
            dz_ref[rs, 2 * HW:3 * HW] = _lanes(m["dv"]).astype(BF16)
            gq = _lanes(m["gq"])
            gk = [_lanes([m["gk"][h][:, HD * j:HD * (j + 1)] for h in range(HEADS)]) for j in range(N_SUB)]
            dq_inter = p["lam"] * _lanes(m["dqi"])
            dq = p["eq"] * gq + dq_inter
            dk_intra = sum(p["ek"][j] * gk[j] for j in range(N_SUB))
            dk_state = _lanes(m["dkd"]) * p["e_last"]
            db_intra = (p["qe"].astype(BF16).astype(F32) * gq
                        - sum(p["ke"][j].astype(BF16).astype(F32) * gk[j] for j in range(N_SUB)))
            dlf = (_tri_dot(upper, db_intra + p["q"] * dq_inter, 2) + _tri_dot(strict_lower, p["kk"] * dk_state, 2)
                   + p["lam_last"] * _lanes(m["st"]))
            sig, sq, zq = p["sig"], p["sq"], p["zq"]
            df = dlf / p["f"] - (dk_intra + dk_state)
            sums["dlb"] = sums["dlb"] + jnp.sum(df * (1.0 - sig), axis=0, keepdims=True)
            dz_ref[rs, HW:2 * HW] = (df * (1.0 - lb) * sig * (1.0 - sig)).astype(BF16)
            dz_ref[rs, 0:HW] = (dq * (sq * (1.0 + zq * (1.0 - sq)))).astype(BF16)

        p = front(CHUNKS_PER_STEP - 1)
        for c in reversed(range(CHUNKS_PER_STEP)):
            m = recurrence(c, p)
            p_next = front(c - 1) if c > 0 else None
            back(p, m)
            p = p_next
        dlb_acc[...] += sums["dlb"]
        for h in range(HEADS):
            dgn_ref[h:h + 1, 0:HD] += _head(sums["dgn"], h)

        @pl.when(i == n_steps - 1)
        def _():
            dl0 = dlb_acc[...] * lb * (1.0 - lb)
            dlb_ref[0:1, 0:HW] = dl0
            dlb_ref[1:2, 0:HW] = -dl0

    rows = CHUNK * CHUNKS_PER_STEP
    n_steps = s // rows
    rev = lambda i: n_steps - 1 - i
    zspec = lambda cb: pl.BlockSpec((rows, HW), lambda i, cb=cb: (rev(i), cb))
    slot = jax.ShapeDtypeStruct((SLOT, D_MODEL), F32)
    return _call_behind(
        deps, body, name="hgrn_bwd", grid=(n_steps,),
        out_shape=[jax.ShapeDtypeStruct((s, IN_WIDTH), BF16), slot, slot],
        in_specs=[zspec(0), zspec(1), zspec(2), zspec(3), pl.BlockSpec((rows, HW), lambda i: (rev(i), 0)),
                  pl.BlockSpec((rows, D_MODEL), lambda i: (rev(i), 0)), VMEM_WHOLE,
                  pl.BlockSpec((CHUNKS_PER_STEP, HEADS, HD, HD), lambda i: (rev(i), 0, 0, 0)), _full((2, HW)),
                  _full((HEADS, HD)), ANY_SPACE],
        out_specs=[pl.BlockSpec((rows, 4 * HW), lambda i: (rev(i), 0)), _full((SLOT, D_MODEL)), _full((SLOT, D_MODEL))],
        scratch_shapes=[pltpu.VMEM((HEADS, HD, HD), F32), pltpu.VMEM((rows, HW), F32), pltpu.VMEM((1, HW), F32),
                        pltpu.VMEM((rows, HW), F32)],
        input_output_aliases={10: 0},
        compiler_params=_params(("arbitrary",)),
    )(z, z, z, z, o, dx1, w_out, states, lb_logits, gn, dz_in)


def _in_bwd(dz, w_t, x0, g, dx1, deps):
    s = x0.shape[0]
    tm = min(WIDE_ROW_TILE, s)

    def body(dz_ref, w_ref, x_ref, g_ref, d_ref, dx_ref, slot_ref):
        _zero_slot(slot_ref)
        dx, dg = _rms_bwd(x_ref[...], g_ref[...], _mm(dz_ref[...], w_ref[...]))
        dx_ref[...] = d_ref[...] + dx
        slot_ref[0:1, :] += dg

    row_f32 = pl.BlockSpec((tm, D_MODEL), lambda i: (i, 0))
    return _call_behind(
        deps, body, name="in_bwd", grid=(s // tm,),
        out_shape=[jax.ShapeDtypeStruct((s, D_MODEL), F32), jax.ShapeDtypeStruct((SLOT, D_MODEL), F32)],
        in_specs=[pl.BlockSpec((tm, IN_WIDTH), lambda i: (i, 0)), VMEM_WHOLE, row_f32, _full((1, D_MODEL)), row_f32],
        out_specs=[row_f32, _full((SLOT, D_MODEL))],
        compiler_params=_params(("arbitrary",)),
    )(dz, w_t, x0, g, dx1)


def kernel(x, mem, norm_mix_g, w_in, lb_logits, hgrn_norm_g, w_pool, pool_scale, w_out, norm_x_g, norm_mem_g, w_xq, w_xk, w_xv, w_xo, norm_ffn_g, w_ff1, w_ff2, final_norm_g, loss_target, m_norm_mix_g, m_w_in, m_lb_logits, m_hgrn_norm_g, m_w_pool, m_pool_scale, m_w_out, m_norm_x_g, m_norm_mem_g, m_w_xq, m_w_xk, m_w_xv, m_w_xo, m_norm_ffn_g, m_w_ff1, m_w_ff2, m_final_norm_g, v_norm_mix_g, v_w_in, v_lb_logits, v_hgrn_norm_g, v_w_pool, v_pool_scale, v_w_out, v_norm_x_g, v_norm_mem_g, v_w_xq, v_w_xk, v_w_xv, v_w_xo, v_norm_ffn_g, v_w_ff1, v_w_ff2, v_final_norm_g):
    x0 = x[0]
    mem0 = mem[0]
    tgt = loss_target[0]
    gn = hgrn_norm_g[0]
    gfin = final_norm_g.reshape(1, D_MODEL)
    wp = w_pool[0]
    heads_2d = lambda w: w.reshape(D_MODEL // N_DEV, D_MODEL)
    xo_2d = lambda w: w.reshape(D_MODEL, D_MODEL // N_DEV)

    first = _all_gather_weights([w_in[0].T], [w_out[0], heads_2d(w_xq), heads_2d(w_xk), heads_2d(w_xv), xo_2d(w_xo).T,
                                              w_ff1[0], w_ff2[0]])
    win_t = first[0].reshape(IN_WIDTH, D_MODEL)
    ga_attn, ga_mlp = _gather_first_start([first[1:6], first[6:8]], "gather_first_start")

    z, h = _in_proj(x0, norm_mix_g, win_t, deps=[ga_attn[3]])
    mixed_a, o_pre, states = _hgrn_fwd(z, lb_logits, gn)
    lands = _split_wait(_gather_first_copies, ga_attn, o_pre, "gather_attn_first_wait")
    gb_attn = _gather_forward_start(lands, "gather_attn_forward_start")
    mixed, pooled = _pool_fwd(z, wp, pool_scale, mixed_a, deps=[gb_attn[3]])
    lands = _split_wait(_gather_forward_copies, gb_attn, pooled, "gather_attn_forward_wait")
    wout_f, wq_f, wk_f, wv_f, wo_t = (t.reshape(D_MODEL, D_MODEL) for t in lands)
    hm, xk, xv = _mem_kv(mem0, norm_mem_g, wk_f, wv_f, deps=[])
    x1, x2, hq, xq, att = _mix_xattn_fwd(x0, mixed, wout_f, norm_x_g, wq_f, xk, xv, wo_t, deps=[])
    lands = _split_wait(_gather_first_copies, ga_mlp, x2, "gather_mlp_first_wait")
    gb_mlp = _gather_forward_start(lands, "gather_mlp_forward_start")
    w1_b, w2_b = _split_wait(_gather_forward_copies, gb_mlp, gb_mlp[3], "gather_mlp_forward_wait")
    dx3, dx3_16, u, hf, slot_fin = _mlp_fwd_loss(x2, norm_ffn_g, w1_b, w2_b.reshape(D_FF, D_MODEL), gfin, tgt)

    rows = lambda t, r: t.reshape(N_DEV, r, D_MODEL)
    dw2 = _wgrad(u, dx3_16, "wgrad_ff2")
    ex_ff2 = _all_to_all_start([rows(dw2, FF_BLK)], [], "exchange_ff2_start")
    da, dx2, slot_ffn = _mlp_bwd(dx3, u, x2, norm_ffn_g, w1_b, w2_b, deps=[ex_ff2[3]])
    dw1 = _wgrad(hf, da, "wgrad_ff1", col_blocks=True)
    ex_ff1 = _all_to_all_start([dw1], [], "exchange_ff1_start")
    dx1, dx1_16, dxq, dxk, dxv, slot_x = _xattn_bwd(dx2, x1, norm_x_g, xq, xk, xv, wq_f, wo_t, deps=[ex_ff1[3]])
    dwo_t = _wgrad(dx2, att, "wgrad_xo")
    dwq = _wgrad(hq, dxq, "wgrad_xq")
    dwk, dwv, slot_mem = _mem_bwd(mem0, norm_mem_g, hm, dxk, dxv, wk_f, wv_f)
    ex_attn = _all_to_all_start([rows(dwq, 128), rows(dwk, 128), rows(dwv, 128), rows(dwo_t, 128)], [],
                                "exchange_attn_start")
    dwout = _wgrad(mixed, dx1_16, "wgrad_out")
    dz_pool, d_wpool, slot_ps = _pool_bwd(dx1_16, wout_f, pooled, wp, pool_scale, deps=[ex_attn[3]])
    small0 = jnp.concatenate([slot_x, slot_mem, slot_ffn, slot_fin, slot_ps], axis=0)
    ex_out = _all_to_all_start([rows(dwout, 128)], [small0, d_wpool], "exchange_out_start")
    dz, slot_lb, slot_gn = _hgrn_bwd(z, o_pre, dx1_16, wout_f, states, lb_logits, gn, dz_pool, deps=[ex_out[3]])
    (r_2,) = _split_wait(_all_to_all_copies(1), ex_ff2, dz, "exchange_ff2_wait")
    dwin_t, ff2_update = _wgrad(dz, h, "wgrad_in", update=(r_2, w_ff2[0], m_w_ff2[0], v_w_ff2[0]))
    ex_in = _all_to_all_start([rows(dwin_t, 320)], [], "exchange_in_start")
    grad_x, slot_mix = _in_bwd(dz, win_t, x0, norm_mix_g, dx1, deps=[ex_in[3]])
    small1 = jnp.concatenate([slot_mix, slot_lb, slot_gn], axis=0)
    ex_mix = _all_to_all_start([], [small1], "exchange_mix_start")

    out = {}
    out["w_ff2"] = ff2_update
    (r_1,) = _split_wait(_all_to_all_copies(1), ex_ff1, ex_mix[3], "exchange_ff1_wait")
    out["w_ff1"] = _sum_adamw(r_1, w_ff1[0], m_w_ff1[0], v_w_ff1[0], "adamw_ff1")
    r_q, r_k, r_v, r_o = _split_wait(_all_to_all_copies(4), ex_attn, out["w_ff1"][1], "exchange_attn_wait")
    sums = _sum_sources_whole([r_q, r_k, r_v, r_o], "sum_grad_attn")
    g_attn = [g.reshape(w_xq.shape) for g in sums[:3]] + [sums[3].T]
    attn = _adamw_whole([(g_attn[0], w_xq, m_w_xq, v_w_xq), (g_attn[1], w_xk, m_w_xk, v_w_xk),
                         (g_attn[2], w_xv, m_w_xv, v_w_xv),
                         (g_attn[3], xo_2d(w_xo), xo_2d(m_w_xo), xo_2d(v_w_xo))], "adamw_attn")
    for n, g, res in zip(("w_xq", "w_xk", "w_xv", "w_xo"), g_attn, attn):
        out[n] = (g, *res)
    r_out, r_small0, r_wpool = _split_wait(_all_to_all_copies(1), ex_out, attn[3][0], "exchange_out_wait")
    out["w_out"] = _sum_adamw(r_out, w_out[0], m_w_out[0], v_w_out[0], "adamw_out")
    (r_in,) = _split_wait(_all_to_all_copies(1), ex_in, out["w_out"][1], "exchange_in_wait")
    in_t = _sum_adamw(r_in, w_in[0].T, m_w_in[0].T, v_w_in[0].T, "adamw_in")
    out["w_in"] = tuple(t.T for t in in_t)
    (r_small1,) = _split_wait(_all_to_all_copies(0), ex_mix, in_t[1], "exchange_mix_wait")
    row = lambda t: t.reshape(1, -1)
    small_params = {
        "norm_mix_g": (norm_mix_g, m_norm_mix_g, v_norm_mix_g),
        "lb_logits": (lb_logits, m_lb_logits, v_lb_logits),
        "hgrn_norm_g": (hgrn_norm_g[0], m_hgrn_norm_g[0], v_hgrn_norm_g[0]),
        "pool_scale": (pool_scale, m_pool_scale, v_pool_scale),
        "norm_x_g": (norm_x_g, m_norm_x_g, v_norm_x_g),
        "norm_mem_g": (norm_mem_g, m_norm_mem_g, v_norm_mem_g),
        "norm_ffn_g": (norm_ffn_g, m_norm_ffn_g, v_norm_ffn_g),
        "final_norm_g": (row(final_norm_g), row(m_final_norm_g), row(v_final_norm_g)),
        "w_pool": (wp, m_w_pool[0], v_w_pool[0]),
    }
    loss, small_out = _small_update([r_small0, r_small1], r_wpool, small_params)
    out.update(small_out)

    shapes = dict(norm_mix_g=norm_mix_g, w_in=w_in, lb_logits=lb_logits, hgrn_norm_g=hgrn_norm_g, w_pool=w_pool,
                  pool_scale=pool_scale, w_out=w_out, norm_x_g=norm_x_g, norm_mem_g=norm_mem_g, w_xq=w_xq, w_xk=w_xk,
                  w_xv=w_xv, w_xo=w_xo, norm_ffn_g=norm_ffn_g, w_ff1=w_ff1, w_ff2=w_ff2, final_norm_g=final_norm_g)
    order = list(shapes)
    group = lambda k: [out[n][k].reshape(shapes[n].shape) for n in order]
    return (loss.reshape(()), grad_x.reshape(x.shape), *group(0), *group(1), *group(2), *group(3))
```

```python
import jax
import jax.numpy as jnp
from jax import lax
from jax.experimental import pallas as pl
from jax.experimental.pallas import tpu as pltpu

F32 = jnp.float32
BF16 = jnp.bfloat16

D_MODEL = 1024
N_DEV = 8
HEADS = 4
HD = 128
HW = HEADS * HD
IN_WIDTH = 5 * HW
XHD = 256
MEM_LEN = 256
D_FF = 4096
FF_BLK = D_FF // N_DEV
POOL_WINDOWS = (2, 4, 8, 16)
POOL_HALO = 16
CHUNK = 64
CHUNKS_PER_STEP = 8
SUB = 16
N_SUB = CHUNK // SUB
EXP_CAP = 80.0
EPS = 1e-6
TINY = 1e-30
ROW_TILE = 512
WIDE_ROW_TILE = 1024
SLOT = 8
V7X_VMEM_LIMIT = 56 * 1024 * 1024

ADAM_LR = 0.001
ADAM_B1 = 0.9
ADAM_B2 = 0.999
ADAM_EPS = 1e-08
ADAM_WD = 0.01
ADAM_STEP = 10

MESH_ID = pl.DeviceIdType.MESH


def _params(sem=None, vmem=V7X_VMEM_LIMIT):
    return pltpu.CompilerParams(dimension_semantics=sem, vmem_limit_bytes=vmem)


def _mm(a, b):
    return lax.dot_general(a.astype(BF16), b.astype(BF16), (((1,), (0,)), ((), ())), preferred_element_type=F32)


def _mm_nt(a, b):
    return lax.dot_general(a.astype(BF16), b.astype(BF16), (((1,), (1,)), ((), ())), preferred_element_type=F32)


def _mm_tn(a, b):
    return lax.dot_general(a.astype(BF16), b.astype(BF16), (((0,), (0,)), ((), ())), preferred_element_type=F32)


def _sigmoid(x):
    return 1.0 / (1.0 + jnp.exp(-x))


def _rms(x):
    return lax.rsqrt(jnp.mean(x * x, axis=-1, keepdims=True) + EPS)


def _rms_bwd(x, g, dh):
    r = _rms(x)
    n = x * r
    dn = dh * g
    dx = r * (dn - n * jnp.mean(dn * n, axis=-1, keepdims=True))
    return dx, jnp.sum(dh * n, axis=0, keepdims=True)


def _tri_dot(tri, x, passes):
    acc = None
    rest = x
    for _ in range(passes):
        piece = rest.astype(BF16)
        part = lax.dot_general(tri, piece, (((1,), (0,)), ((), ())), preferred_element_type=F32)
        acc = part if acc is None else acc + part
        rest = rest - piece.astype(F32)
    return acc


def _adam_update(g, w, m, v):
    nm = ADAM_B1 * m + (1.0 - ADAM_B1) * g
    nv = ADAM_B2 * v + (1.0 - ADAM_B2) * (g * g)
    m_hat = nm / (1.0 - ADAM_B1 ** ADAM_STEP)
    v_hat = nv / (1.0 - ADAM_B2 ** ADAM_STEP)
    return -ADAM_LR * (m_hat / (jnp.sqrt(v_hat) + ADAM_EPS) + ADAM_WD * w), nm, nv


def _full(shape):
    return pl.BlockSpec(shape, lambda *_: (0,) * len(shape))


VMEM_WHOLE = pl.BlockSpec(memory_space=pltpu.VMEM)
ANY_SPACE = pl.BlockSpec(memory_space=pl.ANY)


def _mesh_pos():
    return lax.axis_index("x"), lax.axis_index("y"), lax.axis_index("c")


def _flat(px, py, pc):
    return 4 * px + 2 * py + pc


def _all_gather_weights(shards, cast_only):
    n, nc = len(shards), len(cast_only)
    step = 64

    def body(*refs):
        x_refs, c_refs = refs[:n], refs[n:n + nc]
        out_refs, cast_refs = refs[n + nc:2 * n + nc], refs[2 * n + nc:2 * n + 2 * nc]
        bufs = refs[2 * n + 2 * nc:3 * n + 2 * nc]
        send_sems, recv_sems, local_sems = refs[3 * n + 2 * nc:]
        _handshake(_peers_first_level())
        x, y, c = _mesh_pos()
        me, sibling = (x, y, c), (x, y, 1 - c)
        chips = [(1 - x, y), (x, 1 - y), (1 - x, 1 - y)]

        def copy(a, k, blk, to, src=None):
            rows = out_refs[a].at[_flat(*blk)]
            return pltpu.make_async_remote_copy(
                src_ref=rows if src is None else src, dst_ref=rows,
                send_sem=send_sems.at[7 * a + k], recv_sem=recv_sems.at[7 * a + k], device_id=to, device_id_type=MESH_ID)

        def cast_rows(src, dst, rows):
            def cast(i, carry):
                r0 = pl.multiple_of(i * step, step)
                dst[pl.ds(r0, step), :] = src[pl.ds(r0, step), :].astype(BF16)
                return carry
            lax.fori_loop(0, rows // step, cast, 0)

        first, mine = [], []
        for a in range(n):
            cast_rows(x_refs[a], bufs[a], shards[a].shape[0])
            mine.append(pltpu.make_async_copy(bufs[a], out_refs[a].at[_flat(*me)], local_sems.at[a]))
            first.append(copy(a, 0, me, sibling, src=bufs[a]))
            first += [copy(a, 1 + j, me, (*chip, c), src=bufs[a]) for j, chip in enumerate(chips)]
            for cp in [mine[-1]] + first[-4:]:
                cp.start()
        for a in range(nc):
            cast_rows(c_refs[a], cast_refs[a], cast_only[a].shape[0])
        passed = []
        for j, chip in enumerate(chips):
            for a in range(n):
                copy(a, 1 + j, (*chip, c), me).wait_recv()
                passed.append(copy(a, 4 + j, (*chip, c), sibling))
                passed[-1].start()
        for a in range(n):
            copy(a, 0, sibling, me).wait_recv()
            for j, chip in enumerate(chips):
                copy(a, 4 + j, (*chip, 1 - c), me).wait_recv()
        for cp in first + passed:
            cp.wait_send()
        for cp in mine:
            cp.wait()

    return pl.pallas_call(
        body, name="all_gather_w_in",
        out_shape=[jax.ShapeDtypeStruct((N_DEV,) + s.shape, BF16) for s in shards]
        + [jax.ShapeDtypeStruct(s.shape, BF16) for s in cast_only],
        in_specs=[VMEM_WHOLE] * (n + nc), out_specs=[ANY_SPACE] * n + [VMEM_WHOLE] * nc,
        scratch_shapes=[pltpu.VMEM(s.shape, BF16) for s in shards]
        + [pltpu.SemaphoreType.DMA((7 * n,)), pltpu.SemaphoreType.DMA((7 * n,)), pltpu.SemaphoreType.DMA((n,))],
        compiler_params=pltpu.CompilerParams(vmem_limit_bytes=V7X_VMEM_LIMIT, collective_id=GATHER_W_IN_ID),
    )(*shards, *cast_only)


HBM_SPEC = pl.BlockSpec(memory_space=pltpu.HBM)
SEM_SPEC = pl.BlockSpec(memory_space=pltpu.SEMAPHORE)
EFFECT = pltpu.SideEffectType.DATAFLOW_SIDE_EFFECTING
TOKEN = jax.ShapeDtypeStruct((8, 128), F32)


def _in_hbm(a):
    return pltpu.with_memory_space_constraint(a, pltpu.HBM)


START_IDS = {name: i for i, name in enumerate((
    "gather_first_start", "gather_attn_relay_start", "gather_mlp_relay_start", "gather_attn_last_start",
    "gather_mlp_last_start", "exchange_ff2_start",
    "exchange_ff1_start", "exchange_attn_start", "exchange_out_start", "exchange_in_start", "exchange_mix_start"))}


GATHER_W_IN_ID = len(START_IDS)


def _handshake(peers):
    barrier = pltpu.get_barrier_semaphore()
    for peer in peers:
        pl.semaphore_signal(barrier, inc=1, device_id=peer, device_id_type=MESH_ID)
    pl.semaphore_wait(barrier, len(peers))


def _peers_all():
    x, y, c = _mesh_pos()
    return [(1 - x if k & 4 else x, 1 - y if k & 2 else y, 1 - c if k & 1 else c) for k in range(1, N_DEV)]


def _peers_first_level():
    x, y, c = _mesh_pos()
    return [(x, y, 1 - c), (1 - x, y, c), (x, 1 - y, c), (1 - x, 1 - y, c)]


def _peers_sibling():
    x, y, c = _mesh_pos()
    return [(x, y, 1 - c)]


def _split_start(copies_of, srcs, lands, n_sems, name, peers_of, collective_id):
    ns, nl, k = len(srcs), len(lands), len(n_sems)

    def body(*refs):
        _handshake(peers_of())
        src_refs, land_refs = refs[:ns], refs[ns:ns + nl]
        sems = refs[ns + nl:ns + nl + k]
        token = refs[-1]
        for cp in copies_of(src_refs, land_refs, sems):
            cp.start()
        token[...] = jnp.zeros_like(token)

    outs = pl.pallas_call(
        body, name=name,
        out_shape=[pltpu.SemaphoreType.DMA((q,)) for q in n_sems]
        + [pltpu.HBM(a.shape, a.dtype) for a in list(srcs) + list(lands)] + [TOKEN],
        in_specs=[HBM_SPEC] * (ns + nl),
        out_specs=[SEM_SPEC] * k + [HBM_SPEC] * (ns + nl) + [VMEM_WHOLE],
        input_output_aliases={i: k + i for i in range(ns + nl)},
        compiler_params=pltpu.CompilerParams(has_side_effects=EFFECT, collective_id=collective_id),
    )(*[_in_hbm(a) for a in list(srcs) + list(lands)])
    return outs[:k], outs[k:k + ns], outs[k + ns:k + ns + nl], outs[-1]


def _split_wait(copies_of, handle, after, name):
    sems, srcs, lands, _ = handle
    ns, nl, k = len(srcs), len(lands), len(sems)

    def body(*refs):
        src_refs, land_refs = refs[:ns], refs[ns:ns + nl]
        sem_refs = refs[ns + nl:ns + nl + k]
        for cp in copies_of(src_refs, land_refs, sem_refs):
            cp.wait()

    outs = pl.pallas_call(
        body, name=name,
        out_shape=[pltpu.HBM(a.shape, a.dtype) for a in list(srcs) + list(lands)],
        in_specs=[HBM_SPEC] * (ns + nl) + [SEM_SPEC] * k + [ANY_SPACE],
        out_specs=[HBM_SPEC] * (ns + nl),
        input_output_aliases={i: i for i in range(ns + nl)},
        compiler_params=pltpu.CompilerParams(has_side_effects=EFFECT),
    )(*srcs, *lands, *sems, after)
    return outs[ns:]


def _peers_near():
    x, y, c = _mesh_pos()
    return [(x, y, 1 - c), (1 - x, y, c), (x, 1 - y, c)]


def _relay_partner():
    x, y, c = _mesh_pos()
    return (x + c * (1 - 2 * x), y + (1 - c) * (1 - 2 * y), c)


def _peers_relay():
    x, y, c = _mesh_pos()
    return [(x, y, 1 - c), _relay_partner()]


def _gather_first_copies(shard_refs, land_refs, sems):
    send_sems, recv_sems, local_sems = sems
    x, y, c = _mesh_pos()
    me = _flat(x, y, c)
    copies = []
    for a, (shard, land) in enumerate(zip(shard_refs, land_refs)):
        copies.append(pltpu.make_async_copy(shard, land.at[me], local_sems.at[a]))
        for k, peer in enumerate(_peers_near()):
            copies.append(pltpu.make_async_remote_copy(
                src_ref=shard, dst_ref=land.at[me], send_sem=send_sems.at[3 * a + k], recv_sem=recv_sems.at[3 * a + k],
                device_id=peer, device_id_type=MESH_ID))
    return copies


def _gather_relay_copies(src_refs, land_refs, sems):
    del src_refs
    send_sems, recv_sems = sems
    x, y, c = _mesh_pos()
    from_x, from_y = _flat(1 - x, y, c), _flat(x, 1 - y, c)
    passed_on = c * from_y + (1 - c) * from_x
    copies = []
    for a, land in enumerate(land_refs):
        targets = [(land.at[passed_on], _relay_partner()), (land.at[from_x], (x, y, 1 - c)), (land.at[from_y], (x, y, 1 - c))]
        for j, (rows, to) in enumerate(targets):
            copies.append(pltpu.make_async_remote_copy(
                src_ref=rows, dst_ref=rows, send_sem=send_sems.at[3 * a + j], recv_sem=recv_sems.at[3 * a + j],
                device_id=to, device_id_type=MESH_ID))
    return copies


def _gather_last_copies(src_refs, land_refs, sems):
    del src_refs
    send_sems, recv_sems = sems
    x, y, c = _mesh_pos()
    copies = []
    for a, land in enumerate(land_refs):
        rows = land.at[_flat(1 - x, 1 - y, c)]
        copies.append(pltpu.make_async_remote_copy(
            src_ref=rows, dst_ref=rows, send_sem=send_sems.at[a], recv_sem=recv_sems.at[a],
            device_id=(x, y, 1 - c), device_id_type=MESH_ID))
    return copies


def _gather_first_start(groups, name):
    shards = [s for g in groups for s in g]
    lands = [lax.empty((N_DEV,) + s.shape, s.dtype) for s in shards]
    bounds = [sum(len(g) for g in groups[:i]) for i in range(len(groups) + 1)]

    def copies_of(src_refs, land_refs, sems):
        copies = []
        for i in range(len(groups)):
            lo, hi = bounds[i], bounds[i + 1]
            copies += _gather_first_copies(src_refs[lo:hi], land_refs[lo:hi], sems[3 * i:3 * i + 3])
        return copies

    n_sems = tuple(q for g in groups for q in (3 * len(g), 3 * len(g), len(g)))
    sems, srcs, lands, token = _split_start(copies_of, shards, lands, n_sems, name, _peers_near, START_IDS[name])
    return [(sems[3 * i:3 * i + 3], srcs[bounds[i]:bounds[i + 1]], lands[bounds[i]:bounds[i + 1]], token)
            for i in range(len(groups))]


def _gather_relay_start(lands, name):
    n = len(lands)
    return _split_start(_gather_relay_copies, [], lands, (3 * n, 3 * n), name, _peers_relay, START_IDS[name])


def _gather_last_start(lands, name):
    n = len(lands)
    return _split_start(_gather_last_copies, [], lands, (n, n), name, _peers_sibling, START_IDS[name])


def _all_to_all_copies(n_scattered):
    def copies_of(src_refs, land_refs, sems):
        send_sems, recv_sems, local_sems = sems
        x, y, c = _mesh_pos()
        me = _flat(x, y, c)
        copies = []
        for a, (src, land) in enumerate(zip(src_refs, land_refs)):
            scattered = a < n_scattered
            copies.append(pltpu.make_async_copy(src.at[me] if scattered else src, land.at[me], local_sems.at[a]))
            for k in range(1, N_DEV):
                peer = (1 - x if k & 4 else x, 1 - y if k & 2 else y, 1 - c if k & 1 else c)
                copies.append(pltpu.make_async_remote_copy(
                    src_ref=src.at[_flat(*peer)] if scattered else src, dst_ref=land.at[me],
                    send_sem=send_sems.at[7 * a + k - 1], recv_sem=recv_sems.at[7 * a + k - 1],
                    device_id=peer, device_id_type=MESH_ID))
        return copies
    return copies_of


def _all_to_all_start(scattered, broadcast, name):
    srcs = list(scattered) + list(broadcast)
    lands = [lax.empty(a.shape, a.dtype) for a in scattered] + [lax.empty((N_DEV,) + a.shape, a.dtype) for a in broadcast]
    n = len(srcs)
    return _split_start(_all_to_all_copies(len(scattered)), srcs, lands, (7 * n, 7 * n, n), name, _peers_all,
                        START_IDS[name])


def _call_behind(deps, body, *, in_specs, **kwargs):
    n_in, n_dep = len(in_specs), len(deps)

    def body_without_deps(*refs):
        return body(*refs[:n_in], *refs[n_in + n_dep:])

    call = pl.pallas_call(body_without_deps, in_specs=list(in_specs) + [ANY_SPACE] * n_dep, **kwargs)
    return lambda *operands: call(*operands, *deps)


def _row_tile(rows):
    if rows <= 2 * 256:
        return rows
    for cand in (256, 128, 64, 32, 16):
        if rows % cand == 0:
            return cand
    return rows


def _adamw_whole(groups, name):
    n = len(groups)

    def body(*refs):
        for i in range(n):
            g_ref, w_ref, m_ref, v_ref = refs[4 * i:4 * i + 4]
            d_ref, nm_ref, nv_ref = refs[4 * n + 3 * i:4 * n + 3 * i + 3]
            d_ref[...], nm_ref[...], nv_ref[...] = _adam_update(g_ref[...], w_ref[...], m_ref[...], v_ref[...])

    outs = pl.pallas_call(
        body, name=name, out_shape=[jax.ShapeDtypeStruct(grp[0].shape, F32) for grp in groups for _ in range(3)],
        in_specs=[VMEM_WHOLE] * (4 * n), out_specs=[VMEM_WHOLE] * (3 * n),
        compiler_params=_params(),
    )(*[t for grp in groups for t in grp])
    return [outs[3 * i:3 * i + 3] for i in range(n)]


def _sum_sources_whole(recvs, name):
    n = len(recvs)

    def body(*refs):
        for r_ref, o_ref in zip(refs[:n], refs[n:]):
            acc = r_ref[0].astype(F32)
            for d in range(1, N_DEV):
                acc = acc + r_ref[d].astype(F32)
            o_ref[...] = acc

    return pl.pallas_call(
        body, name=name, out_shape=[jax.ShapeDtypeStruct(r.shape[1:], F32) for r in recvs],
        in_specs=[VMEM_WHOLE] * n, out_specs=[VMEM_WHOLE] * n,
        compiler_params=_params(),
    )(*recvs)


def _sum_adamw(recv, w, m, v, name):
    _, rows, cols = recv.shape
    tile = _row_tile(rows)

    def body(r_ref, w_ref, m_ref, v_ref, g_ref, d_ref, nm_ref, nv_ref):
        acc = r_ref[0].astype(F32)
        for d in range(1, N_DEV):
            acc = acc + r_ref[d].astype(F32)
        g_ref[...] = acc
        d_ref[...], nm_ref[...], nv_ref[...] = _adam_update(acc, w_ref[...], m_ref[...], v_ref[...])

    spec = pl.BlockSpec((tile, cols), lambda i: (i, 0))
    shp = jax.ShapeDtypeStruct((rows, cols), F32)
    return pl.pallas_call(
        body, name=name, grid=(rows // tile,), out_shape=[shp] * 4,
        in_specs=[pl.BlockSpec((N_DEV, tile, cols), lambda i: (0, i, 0)), spec, spec, spec], out_specs=[spec] * 4,
        compiler_params=_params(("parallel",)),
    )(recv, w, m, v)


SMALL_SLOTS = {"norm_x_g": (0, 0, 1, D_MODEL), "norm_mem_g": (0, 8, 1, D_MODEL), "norm_ffn_g": (0, 16, 1, D_MODEL),
               "final_norm_g": (0, 24, 1, D_MODEL), "pool_scale": (0, 32, 1, HW),
               "norm_mix_g": (1, 0, 1, D_MODEL), "lb_logits": (1, 8, 2, HW), "hgrn_norm_g": (1, 16, HEADS, HD)}
LOSS_ROW = 25
SMALL_ORDER = ("norm_mix_g", "lb_logits", "hgrn_norm_g", "pool_scale", "norm_x_g", "norm_mem_g", "norm_ffn_g",
               "final_norm_g", "w_pool")


def _small_update(srecvs, wprecv, params):
    flat = [t for n in SMALL_ORDER for t in params[n]]
    nb = len(srecvs)
    n_in = nb + 1 + len(flat)

    def body(*refs):
        s_refs, wp_ref = refs[0:nb], refs[nb]
        in_refs = refs[nb + 1:n_in]
        loss_ref = refs[n_in]
        out_refs = refs[n_in + 1:-nb]
        accs = refs[-nb:]
        for s_ref, acc in zip(s_refs, accs):
            total = s_ref[0]
            for d in range(1, N_DEV):
                total = total + s_ref[d]
            acc[...] = total
        loss_ref[...] = accs[0][LOSS_ROW:LOSS_ROW + 1, 0:1]
        for i, name in enumerate(SMALL_ORDER):
            w_ref, m_ref, v_ref = in_refs[3 * i:3 * i + 3]
            g_ref, d_ref, nm_ref, nv_ref = out_refs[4 * i:4 * i + 4]
            if name == "w_pool":
                g = wp_ref[0]
                for d in range(1, N_DEV):
                    g = g + wp_ref[d]
            else:
                buf, r0, nr, nc = SMALL_SLOTS[name]
                g = accs[buf][r0:r0 + nr, 0:nc]
            g_ref[...] = g
            d_ref[...], nm_ref[...], nv_ref[...] = _adam_update(g, w_ref[...], m_ref[...], v_ref[...])

    out_shape = [jax.ShapeDtypeStruct((1, 1), F32)]
    for n in SMALL_ORDER:
        out_shape += [jax.ShapeDtypeStruct(params[n][0].shape, F32)] * 4
    outs = pl.pallas_call(
        body, name="small_update", out_shape=out_shape,
        in_specs=[VMEM_WHOLE] * n_in, out_specs=[VMEM_WHOLE] * len(out_shape),
        scratch_shapes=[pltpu.VMEM(r.shape[1:], F32) for r in srecvs],
        compiler_params=_params(),
    )(*srecvs, wprecv, *flat)
    return outs[0], {n: outs[1 + 4 * i:5 + 4 * i] for i, n in enumerate(SMALL_ORDER)}


def _in_proj(x, g, w_t, deps):
    s = x.shape[0]
    tm = min(ROW_TILE, s)

    def body(x_ref, g_ref, w_ref, z_ref, h_ref):
        xv = x_ref[...]
        h = (xv * _rms(xv) * g_ref[...]).astype(BF16)
        h_ref[...] = h
        z_ref[...] = _mm_nt(h, w_ref[...])

    return _call_behind(
        deps, body, name="in_proj", grid=(s // tm,),
        out_shape=[jax.ShapeDtypeStruct((s, IN_WIDTH), F32), jax.ShapeDtypeStruct((s, D_MODEL), BF16)],
        in_specs=[pl.BlockSpec((tm, D_MODEL), lambda i: (i, 0)), _full((1, D_MODEL)), VMEM_WHOLE],
        out_specs=[pl.BlockSpec((tm, IN_WIDTH), lambda i: (i, 0)), pl.BlockSpec((tm, D_MODEL), lambda i: (i, 0))],
        compiler_params=_params(("parallel",)),
    )(x, g, w_t)


def _chunk_masks():
    row = lax.broadcasted_iota(jnp.int32, (CHUNK, CHUNK), 0)
    col = lax.broadcasted_iota(jnp.int32, (CHUNK, CHUNK), 1)
    return row, col


def _ones_where(mask):
    return jnp.where(mask, 1.0, 0.0).astype(BF16)


def _hgrn_gates(zq, zf, lb):
    sq = _sigmoid(zq)
    sig = _sigmoid(zf)
    f = lb + (1.0 - lb) * sig
    return zq * sq, sq, sig, f


def _sub_chunk_masks(width):
    trow = lax.broadcasted_iota(jnp.int32, (CHUNK, width), 0)
    return [(trow >= SUB * j) & (trow < SUB * (j + 1)) for j in range(N_SUB)]


def _head(a, h):
    return a[:, HD * h:HD * (h + 1)]


def _lanes(parts):
    return jnp.concatenate(parts, axis=1)


def _hgrn_decay_factors(b_scr, r0, b, in_sub):
    bases = [jnp.zeros((1, HW), F32)] + [b_scr[r0 + SUB * j - 1:r0 + SUB * j, :] for j in range(1, N_SUB)]
    own_base = bases[N_SUB - 1]
    for j in range(N_SUB - 2, -1, -1):
        own_base = jnp.where(in_sub[j], bases[j], own_base)
    eq = jnp.exp(b - own_base)
    ek = []
    for j in range(N_SUB):
        upto = SUB * (j + 1)
        e = jnp.exp(jnp.minimum(bases[j] - b[0:upto], EXP_CAP))
        ek.append(e if upto == CHUNK else jnp.concatenate([e, jnp.zeros((CHUNK - upto, HW), F32)], axis=0))
    return eq, ek


def _per_sub_chunk(x, in_sub):
    return _lanes([jnp.where(in_sub[j], x, 0.0) for j in range(N_SUB)])


def _own_lane_block(a, in_sub):
    out = a[:, HD * (N_SUB - 1):HD * N_SUB]
    for j in range(N_SUB - 2, -1, -1):
        out = jnp.where(in_sub[j], a[:, HD * j:HD * (j + 1)], out)
    return out


def _head_rms(o):
    return _lanes([jnp.broadcast_to(_rms(_head(o, h)), (CHUNK, HD)) for h in range(HEADS)])


def _head_mean(a):
    return _lanes([jnp.broadcast_to(jnp.mean(_head(a, h), axis=-1, keepdims=True), (CHUNK, HD)) for h in range(HEADS)])


def _hgrn_fwd(z, lb_logits, gn, deps):
    s = z.shape[0]
    n_chunks = s // CHUNK

    def body(zq_ref, zf_ref, zi_ref, zg_ref, lbl_ref, gn_ref, oa_ref, o_ref, st_ref, state, b_scr):
        @pl.when(pl.program_id(0) == 0)
        def _():
            state[...] = jnp.zeros_like(state)

        lb = _sigmoid(lbl_ref[0:1, :] - lbl_ref[1:2, :])
        row, col = _chunk_masks()
        causal = col <= row
        tri = _ones_where(causal)
        in_sub, in_sub_head = _sub_chunk_masks(HW), _sub_chunk_masks(HD)
        gn_row = _lanes([gn_ref[h:h + 1, :] for h in range(HEADS)])
        def front(c):
            r0 = CHUNK * c
            rs = slice(r0, r0 + CHUNK)
            q, _, _, f = _hgrn_gates(zq_ref[rs, :], zf_ref[rs, :], lb)
            kk = 1.0 - f
            b = _tri_dot(tri, jnp.log(f), 3)
            b_scr[rs, :] = b
            eq, ek = _hgrn_decay_factors(b_scr, r0, b, in_sub)
            b_last = b_scr[r0 + CHUNK - 1:r0 + CHUNK, :]
            qe = q * eq
            return {"rs": rs, "v": zi_ref[rs, :], "qg": q * jnp.exp(b), "kd": kk * jnp.exp(b_last - b),
                    "lam_last": jnp.exp(b_last),
                    "q16": [_per_sub_chunk(_head(qe, h), in_sub_head).astype(BF16) for h in range(HEADS)],
                    "ke16": [_lanes([_head(kk * e, h) for e in ek]).astype(BF16) for h in range(HEADS)]}

        def recurrence(c, p):
            st_ref[c] = state[...]
            a, o_inter = [], []
            for h in range(HEADS):
                vh, st = _head(p["v"], h), state[h]
                a.append(jnp.where(causal, _mm_nt(p["q16"][h], p["ke16"][h]), 0.0))
                o_inter.append(_mm_nt(_head(p["qg"], h), st))
                state[h] = st * _head(p["lam_last"], h) + _mm_tn(vh, _head(p["kd"], h))
            return _lanes([_mm(a[h], _head(p["v"], h)) + o_inter[h] for h in range(HEADS)])

        def back(p, o):
            rs = p["rs"]
            o_ref[rs, :] = o
            zg = zg_ref[rs, :]
            oa_ref[rs, :] = (o * _head_rms(o) * gn_row * zg * _sigmoid(zg)).astype(BF16)

        p = front(0)
        for c in range(CHUNKS_PER_STEP):
            o = recurrence(c, p)
            p_next = front(c + 1) if c + 1 < CHUNKS_PER_STEP else None
            back(p, o)
            p = p_next

    rows = CHUNK * CHUNKS_PER_STEP
    zspec = lambda cb: pl.BlockSpec((rows, HW), lambda i, cb=cb: (i, cb))
    return _call_behind(
        deps, body, name="hgrn_fwd", grid=(s // rows,),
        out_shape=[jax.ShapeDtypeStruct((s, 2 * HW), BF16), jax.ShapeDtypeStruct((s, HW), F32),
                   jax.ShapeDtypeStruct((n_chunks, HEADS, HD, HD), F32)],
        in_specs=[zspec(0), zspec(1), zspec(2), zspec(3), _full((2, HW)), _full((HEADS, HD))],
        out_specs=[pl.BlockSpec((rows, HW), lambda i: (i, 0)), pl.BlockSpec((rows, HW), lambda i: (i, 0)),
                   pl.BlockSpec((CHUNKS_PER_STEP, HEADS, HD, HD), lambda i: (i, 0, 0, 0))],
        scratch_shapes=[pltpu.VMEM((HEADS, HD, HD), F32), pltpu.VMEM((rows, HW), F32)],
        compiler_params=_params(("arbitrary",)),
    )(z, z, z, z, lb_logits, gn)


def _pool_counts(tile_idx, tm):
    t = tile_idx * tm + lax.broadcasted_iota(jnp.int32, (tm, 1), 0)
    return [1.0 / jnp.minimum(t + 1, w).astype(F32) for w in POOL_WINDOWS]


def _pool_fwd(z, w_pool, scale, mixed_in, deps):
    s = z.shape[0]
    tm = min(ROW_TILE, s)

    def body(p_ref, w_ref, sc_ref, mixin_ref, ob_ref, pooled_ref, ext):
        i = pl.program_id(0)

        @pl.when(i == 0)
        def _():
            ext[0:POOL_HALO, :] = jnp.zeros((POOL_HALO, HW), F32)

        @pl.when(i > 0)
        def _():
            ext[0:POOL_HALO, :] = ext[tm:tm + POOL_HALO, :]

        ext[POOL_HALO:POOL_HALO + tm, :] = p_ref[...]
        inv = _pool_counts(i, tm)
        for g, w in enumerate(POOL_WINDOWS):
            sl = slice(HD * g, HD * (g + 1))
            p = ext[POOL_HALO:POOL_HALO + tm, sl]
            win = p
            for d in range(1, w):
                win = win + ext[POOL_HALO - d:POOL_HALO - d + tm, sl]
            pooled = (win * inv[g] - p).astype(BF16)
            pooled_ref[:, sl] = pooled
            ob_ref[:, sl] = (_mm(pooled, w_ref[g]) * sc_ref[:, sl]).astype(BF16)

    return _call_behind(
        deps, body, name="pool_fwd", grid=(s // tm,),
        out_shape=[jax.ShapeDtypeStruct((s, 2 * HW), BF16), jax.ShapeDtypeStruct((s, HW), BF16)],
        in_specs=[pl.BlockSpec((tm, HW), lambda i: (i, 4)), _full((HEADS, HD, HD)), _full((1, HW)), ANY_SPACE],
        out_specs=[pl.BlockSpec((tm, HW), lambda i: (i, 1)), pl.BlockSpec((tm, HW), lambda i: (i, 0))],
        scratch_shapes=[pltpu.VMEM((tm + POOL_HALO, HW), F32)],
        input_output_aliases={3: 0},
        compiler_params=_params(("arbitrary",)),
    )(z, w_pool, scale, mixed_in)


def _mem_kv(mem, g, wk, wv, deps):
    def body(m_ref, g_ref, wk_ref, wv_ref, hm_ref, k_ref, v_ref):
        m = m_ref[...]
        hm = (m * _rms(m) * g_ref[...]).astype(BF16)
        hm_ref[...] = hm
        k_ref[...] = _mm(hm, wk_ref[...]).astype(BF16)
        v_ref[...] = _mm(hm, wv_ref[...]).astype(BF16)

    shp = jax.ShapeDtypeStruct((MEM_LEN, D_MODEL), BF16)
    return _call_behind(
        deps, body, name="mem_kv", out_shape=[shp, shp, shp],
        in_specs=[VMEM_WHOLE] * 4, out_specs=[VMEM_WHOLE] * 3,
        compiler_params=_params(),
    )(mem, g, wk, wv)


def _softmax_rows(sc):
    e = jnp.exp(sc - jnp.max(sc, axis=-1, keepdims=True))
    return e / jnp.sum(e, axis=-1, keepdims=True)


def _mix_xattn_fwd(x0, mixed, w_out, g, wq, xk, xv, wo_t, deps):
    s = x0.shape[0]
    tm = min(ROW_TILE, s)
    scale = XHD ** -0.5

    def body(x_ref, mix_ref, wout_ref, g_ref, wq_ref, k_ref, v_ref, wo_ref, x1_ref, o_ref, hq_ref, q_ref, att_ref):
        xv_ = x_ref[...] + _mm(mix_ref[...], wout_ref[...])
        x1_ref[...] = xv_
        hq = (xv_ * _rms(xv_) * g_ref[...]).astype(BF16)
        hq_ref[...] = hq
        q_ref[...] = (_mm(hq, wq_ref[...]) * scale).astype(BF16)
        heads = [slice(XHD * h, XHD * (h + 1)) for h in range(HEADS)]
        scores = [_mm_nt(q_ref[:, sl], k_ref[:, sl]) for sl in heads]
        probs = [_softmax_rows(sc) for sc in scores]
        for sl, p in zip(heads, probs):
            att_ref[:, sl] = _mm(p, v_ref[:, sl]).astype(BF16)
        o_ref[...] = xv_ + _mm_nt(att_ref[...], wo_ref[...])

    row_f32 = pl.BlockSpec((tm, D_MODEL), lambda i: (i, 0))
    bshape = jax.ShapeDtypeStruct((s, D_MODEL), BF16)
    fshape = jax.ShapeDtypeStruct((s, D_MODEL), F32)
    return _call_behind(
        deps, body, name="mix_xattn_fwd", grid=(s // tm,),
        out_shape=[fshape, fshape, bshape, bshape, bshape],
        in_specs=[row_f32, row_f32, VMEM_WHOLE, _full((1, D_MODEL)), VMEM_WHOLE, VMEM_WHOLE, VMEM_WHOLE, VMEM_WHOLE],
        out_specs=[row_f32] * 5,
        compiler_params=_params(("parallel",)),
    )(x0, mixed, w_out, g, wq, xk, xv, wo_t)


def _mlp_fwd_loss(x, g, w1, w2, gf, target):
    s = x.shape[0]
    tm = min(ROW_TILE, s)

    def body(x_ref, g_ref, w1_ref, w2_ref, gf_ref, t_ref, dx_ref, dx16_ref, u_ref, hf_ref, slot_ref):
        @pl.when(pl.program_id(0) == 0)
        def _():
            slot_ref[...] = jnp.zeros_like(slot_ref)

        xv = x_ref[...]
        hf = (xv * _rms(xv) * g_ref[...]).astype(BF16)
        hf_ref[...] = hf
        a_next = _mm(hf, w1_ref[0])
        for j in range(N_DEV):
            a = jnp.maximum(a_next, 0.0)
            if j + 1 < N_DEV:
                a_next = _mm(hf, w1_ref[j + 1])
            u_ref[:, FF_BLK * j:FF_BLK * (j + 1)] = (a * a).astype(BF16)
        acc = xv + _mm(u_ref[...], w2_ref[...])
        gfv = gf_ref[...]
        r = _rms(acc)
        n = acc * r
        err = n * gfv - t_ref[...]
        slot_ref[1:2, :] += jnp.sum(jnp.mean(err * err, axis=-1, keepdims=True), axis=0, keepdims=True) * 0.5
        dy = err * (1.0 / D_MODEL)
        slot_ref[0:1, :] += jnp.sum(dy * n, axis=0, keepdims=True)
        dn = dy * gfv
        dx = r * (dn - n * jnp.mean(dn * n, axis=-1, keepdims=True))
        dx_ref[...] = dx
        dx16_ref[...] = dx.astype(BF16)

    row_f32 = pl.BlockSpec((tm, D_MODEL), lambda i: (i, 0))
    return pl.pallas_call(
        body, name="mlp_fwd_loss", grid=(s // tm,),
        out_shape=[jax.ShapeDtypeStruct((s, D_MODEL), F32), jax.ShapeDtypeStruct((s, D_MODEL), BF16),
                   jax.ShapeDtypeStruct((s, D_FF), BF16), jax.ShapeDtypeStruct((s, D_MODEL), BF16),
                   jax.ShapeDtypeStruct((SLOT, D_MODEL), F32)],
        in_specs=[row_f32, _full((1, D_MODEL)), VMEM_WHOLE, VMEM_WHOLE, _full((1, D_MODEL)), row_f32],
        out_specs=[row_f32, row_f32, pl.BlockSpec((tm, D_FF), lambda i: (i, 0)), row_f32, _full((SLOT, D_MODEL))],
        compiler_params=_params(("arbitrary",)),
    )(x, g, w1, w2, gf, target)


def _zero_slot(slot_ref):
    @pl.when(pl.program_id(0) == 0)
    def _():
        slot_ref[...] = jnp.zeros_like(slot_ref)


def _mlp_bwd(dx3, u, x2, g, w1, w2, deps):
    s = x2.shape[0]
    tm = min(ROW_TILE // 2, s)

    def body(d_ref, u_ref, x_ref, g_ref, w1_ref, w2_ref, da_ref, dx_ref, slot_ref):
        _zero_slot(slot_ref)
        d = d_ref[...]
        d16 = d.astype(BF16)
        du_next = _mm_nt(d16, w2_ref[0])
        dhf = jnp.zeros((tm, D_MODEL), F32)
        for j in range(N_DEV):
            sl = slice(FF_BLK * j, FF_BLK * (j + 1))
            du = du_next
            if j + 1 < N_DEV:
                du_next = _mm_nt(d16, w2_ref[j + 1])
            u = u_ref[:, sl].astype(F32)
            da = (du * (2.0 * u * lax.rsqrt(jnp.maximum(u, TINY)))).astype(BF16)
            da_ref[:, sl] = da
            dhf = dhf + _mm_nt(da, w1_ref[j])
        dx, dg = _rms_bwd(x_ref[...], g_ref[...], dhf)
        dx_ref[...] = d + dx
        slot_ref[0:1, :] += dg

    row_f32 = pl.BlockSpec((tm, D_MODEL), lambda i: (i, 0))
    return _call_behind(
        deps, body, name="mlp_bwd", grid=(s // tm,),
        out_shape=[jax.ShapeDtypeStruct((s, D_FF), BF16), jax.ShapeDtypeStruct((s, D_MODEL), F32),
                   jax.ShapeDtypeStruct((SLOT, D_MODEL), F32)],
        in_specs=[row_f32, pl.BlockSpec((tm, D_FF), lambda i: (i, 0)), row_f32, _full((1, D_MODEL)),
                  VMEM_WHOLE, VMEM_WHOLE],
        out_specs=[pl.BlockSpec((tm, D_FF), lambda i: (i, 0)), row_f32, _full((SLOT, D_MODEL))],
        compiler_params=_params(("arbitrary",)),
    )(dx3, u, x2, g, w1, w2)


def _wgrad(a, b, name, col_blocks=False, update=None):
    s, m = a.shape
    n = b.shape[1]
    tm = 1280 if m % 1280 == 0 else min(1024, m)
    tn = min(1024, n)
    blk = n // N_DEV
    per_step = tn // blk if col_blocks else 1
    ts = min((4 if m * n >= D_MODEL * D_FF else 2) * ROW_TILE, s)
    n_s = s // ts
    grid = (m // tm, n // tn, n_s)

    def body(a_ref, b_ref, *rest):
        o_ref, acc = rest[-2], rest[-1]
        k = pl.program_id(2)

        @pl.when(k == 0)
        def _():
            acc[...] = jnp.zeros_like(acc)

        acc[...] += _mm_tn(a_ref[...], b_ref[...])
        if update is not None:
            r_ref, w_ref, m_ref, v_ref, g_ref, d_ref, nm_ref, nv_ref = rest[:8]
            g = r_ref[0].astype(F32)
            for d in range(1, N_DEV):
                g = g + r_ref[d].astype(F32)
            g_ref[...] = g
            d_ref[...], nm_ref[...], nv_ref[...] = _adam_update(g, w_ref[...], m_ref[...], v_ref[...])

        @pl.when(k == n_s - 1)
        def _():
            if col_blocks:
                for p in range(per_step):
                    o_ref[p] = acc[:, blk * p:blk * (p + 1)].astype(BF16)
            else:
                o_ref[...] = acc[...].astype(BF16)

    if col_blocks:
        out_shape = jax.ShapeDtypeStruct((N_DEV, m, blk), BF16)
        out_spec = pl.BlockSpec((per_step, tm, blk), lambda i, j, k: (j, i, 0))
    else:
        out_shape = jax.ShapeDtypeStruct((m, n), BF16)
        out_spec = pl.BlockSpec((tm, tn), lambda i, j, k: (i, j))
    in_specs = [pl.BlockSpec((ts, tm), lambda i, j, k: (k, i)), pl.BlockSpec((ts, tn), lambda i, j, k: (k, j))]
    out_shapes, out_specs, operands = [out_shape], [out_spec], [a, b]
    if update is not None:
        rows, cols = update[1].shape
        steps = grid[0] * grid[1] * grid[2]
        tr = rows // steps
        step = lambda i, j, k: (i * grid[1] + j) * grid[2] + k
        piece = pl.BlockSpec((tr, cols), lambda i, j, k: (step(i, j, k), 0))
        in_specs += [pl.BlockSpec((N_DEV, tr, cols), lambda i, j, k: (0, step(i, j, k), 0)), piece, piece, piece]
        out_shapes = [jax.ShapeDtypeStruct((rows, cols), F32)] * 4 + out_shapes
        out_specs = [piece] * 4 + out_specs
        operands += list(update)
    outs = pl.pallas_call(
        body, name=name, grid=grid, out_shape=out_shapes, in_specs=in_specs, out_specs=out_specs,
        scratch_shapes=[pltpu.VMEM((tm, tn), F32)],
        compiler_params=_params(("parallel", "parallel", "arbitrary")),
    )(*operands)
    return outs[0] if update is None else (outs[4], tuple(outs[:4]))


def _xattn_bwd(dx2, x1, g, q, xk, xv, wq, wo_t, deps):
    s = x1.shape[0]
    tm = min(ROW_TILE, s)
    scale = XHD ** -0.5

    def body(d_ref, x_ref, g_ref, q_ref, k_ref, v_ref, wq_ref, wo_ref, dx_ref, dx16_ref, dq_ref, dk_ref, dv_ref, slot_ref,
             datt):
        _zero_slot(slot_ref)

        @pl.when(pl.program_id(0) == 0)
        def _():
            dk_ref[...] = jnp.zeros_like(dk_ref)
            dv_ref[...] = jnp.zeros_like(dv_ref)

        d = d_ref[...]
        datt[...] = _mm(d, wo_ref[...]).astype(BF16)
        heads = [slice(XHD * h, XHD * (h + 1)) for h in range(HEADS)]
        scores = [_mm_nt(q_ref[:, sl], k_ref[:, sl]) for sl in heads]
        dps = [_mm_nt(datt[:, sl], v_ref[:, sl]) for sl in heads]
        probs = [_softmax_rows(sc) for sc in scores]
        dss = [(p * (dp - jnp.sum(dp * p, axis=-1, keepdims=True))).astype(BF16) for p, dp in zip(probs, dps)]
        for sl, p, ds in zip(heads, probs, dss):
            dq_ref[:, sl] = (_mm(ds, k_ref[:, sl]) * scale).astype(BF16)
            dk_ref[:, sl] += _mm_tn(ds, q_ref[:, sl])
            dv_ref[:, sl] += _mm_tn(p, datt[:, sl])
        dx, dg = _rms_bwd(x_ref[...], g_ref[...], _mm_nt(dq_ref[...], wq_ref[...]))
        dx_ref[...] = d + dx
        dx16_ref[...] = (d + dx).astype(BF16)
        slot_ref[0:1, :] += dg

    row_f32 = pl.BlockSpec((tm, D_MODEL), lambda i: (i, 0))
    kv = jax.ShapeDtypeStruct((MEM_LEN, D_MODEL), F32)
    tokens16 = jax.ShapeDtypeStruct((s, D_MODEL), BF16)
    return _call_behind(
        deps, body, name="xattn_bwd", grid=(s // tm,),
        out_shape=[jax.ShapeDtypeStruct((s, D_MODEL), F32), tokens16, tokens16, kv, kv,
                   jax.ShapeDtypeStruct((SLOT, D_MODEL), F32)],
        in_specs=[row_f32, row_f32, _full((1, D_MODEL)), row_f32, VMEM_WHOLE, VMEM_WHOLE, VMEM_WHOLE, VMEM_WHOLE],
        out_specs=[row_f32, row_f32, row_f32, _full((MEM_LEN, D_MODEL)), _full((MEM_LEN, D_MODEL)),
                   _full((SLOT, D_MODEL))],
        scratch_shapes=[pltpu.VMEM((tm, D_MODEL), BF16)],
        compiler_params=_params(("arbitrary",)),
    )(dx2, x1, g, q, xk, xv, wq, wo_t)


def _mem_bwd(mem, g, hm, dxk, dxv, wk, wv):
    def body(m_ref, g_ref, hm_ref, dk_ref, dv_ref, wk_ref, wv_ref, dwk_ref, dwv_ref, slot_ref):
        dk, dv = dk_ref[...], dv_ref[...]
        hm_ = hm_ref[...]
        dwk_ref[...] = _mm_tn(hm_, dk).astype(BF16)
        dwv_ref[...] = _mm_tn(hm_, dv).astype(BF16)
        _, dg = _rms_bwd(m_ref[...], g_ref[...], _mm_nt(dk, wk_ref[...]) + _mm_nt(dv, wv_ref[...]))
        slot_ref[...] = jnp.zeros_like(slot_ref)
        slot_ref[0:1, :] = dg

    wshape = jax.ShapeDtypeStruct((D_MODEL, D_MODEL), BF16)
    return pl.pallas_call(
        body, name="mem_bwd", out_shape=[wshape, wshape, jax.ShapeDtypeStruct((SLOT, D_MODEL), F32)],
        in_specs=[VMEM_WHOLE] * 7, out_specs=[VMEM_WHOLE] * 3,
        compiler_params=_params(),
    )(mem, g, hm, dxk, dxv, wk, wv)


def _pool_bwd(dx1, w_out, pooled, w_pool, scale, deps):
    s = dx1.shape[0]
    tm = min(ROW_TILE, s)
    n_t = s // tm

    def body(dx_ref, wo_ref, pl_ref, w_ref, sc_ref, dz_ref, dw_ref, slot_ref, ext, do_ref):
        i = pl.program_id(0)
        tile = n_t - 1 - i
        _zero_slot(slot_ref)
        do_ref[...] = _mm_nt(dx_ref[...], wo_ref[HW:2 * HW, :])

        @pl.when(i == 0)
        def _():
            dw_ref[...] = jnp.zeros_like(dw_ref)
            ext[tm:tm + POOL_HALO, :] = jnp.zeros((POOL_HALO, HW), F32)

        @pl.when(i > 0)
        def _():
            ext[tm:tm + POOL_HALO, :] = ext[0:POOL_HALO, :]

        inv = _pool_counts(tile, tm)
        dpooled = []
        for g in range(HEADS):
            sl = slice(HD * g, HD * (g + 1))
            pooled_g = pl_ref[:, sl]
            do = do_ref[:, sl]
            slot_ref[0:1, sl] += jnp.sum(_mm(pooled_g, w_ref[g]) * do, axis=0, keepdims=True)
            dy = (do * sc_ref[:, sl]).astype(BF16)
            dw_ref[g] += _mm_tn(pooled_g, dy)
            dpo = _mm_nt(dy, w_ref[g])
            dpooled.append(dpo)
            ext[0:tm, sl] = dpo * inv[g]
        for g, w in enumerate(POOL_WINDOWS):
            sl = slice(HD * g, HD * (g + 1))
            win = ext[0:tm, sl]
            for d in range(1, w):
                win = win + ext[d:d + tm, sl]
            dz_ref[:, sl] = (win - dpooled[g]).astype(BF16)

    return _call_behind(
        deps, body, name="pool_bwd", grid=(n_t,),
        out_shape=[jax.ShapeDtypeStruct((s, IN_WIDTH), BF16), jax.ShapeDtypeStruct((HEADS, HD, HD), F32),
                   jax.ShapeDtypeStruct((SLOT, D_MODEL), F32)],
        in_specs=[pl.BlockSpec((tm, D_MODEL), lambda i: (n_t - 1 - i, 0)), VMEM_WHOLE,
                  pl.BlockSpec((tm, HW), lambda i: (n_t - 1 - i, 0)), _full((HEADS, HD, HD)), _full((1, HW))],
        out_specs=[pl.BlockSpec((tm, HW), lambda i: (n_t - 1 - i, 4)), _full((HEADS, HD, HD)), _full((SLOT, D_MODEL))],
        scratch_shapes=[pltpu.VMEM((tm + POOL_HALO, HW), F32), pltpu.VMEM((tm, HW), F32)],
        compiler_params=_params(("arbitrary",)),
    )(dx1, w_out, pooled, w_pool, scale)


def _hgrn_bwd(z, o, dx1, w_out, states, lb_logits, gn, dz_in, deps):
    s = z.shape[0]
    n_chunks = s // CHUNK

    def body(zq_ref, zf_ref, zi_ref, zg_ref, o_ref, dx_ref, wo_ref, st_ref, lbl_ref, gn_ref, dzin_ref,
             dz_ref, dlb_ref, dgn_ref, dstate, b_scr, dlb_acc, do_ref):
        i = pl.program_id(0)

        @pl.when(i == 0)
        def _():
            dstate[...] = jnp.zeros_like(dstate)
            dlb_acc[...] = jnp.zeros_like(dlb_acc)
            dgn_ref[...] = jnp.zeros_like(dgn_ref)
            dlb_ref[...] = jnp.zeros_like(dlb_ref)

        do_ref[...] = _mm_nt(dx_ref[...], wo_ref[0:HW, :])
        lb = _sigmoid(lbl_ref[0:1, :] - lbl_ref[1:2, :])
        row, col = _chunk_masks()
        causal = col <= row
        tri = _ones_where(causal)
        upper = _ones_where(col >= row)
        strict_lower = _ones_where(col < row)
        in_sub, in_sub_head = _sub_chunk_masks(HW), _sub_chunk_masks(HD)
        gn_row = _lanes([gn_ref[h:h + 1, :] for h in range(HEADS)])
        sums = {"dlb": 0.0, "dgn": 0.0}

        def front(c):
            r0 = CHUNK * c
            rs = slice(r0, r0 + CHUNK)
            p = {"rs": rs}
            p["zq"] = zq_ref[rs, :]
            p["q"], p["sq"], p["sig"], p["f"] = _hgrn_gates(p["zq"], zf_ref[rs, :], lb)
            p["kk"] = 1.0 - p["f"]
            b = _tri_dot(tri, jnp.log(p["f"]), 3)
            b_scr[rs, :] = b
            p["v"] = zi_ref[rs, :]
            o, zg, doa = o_ref[rs, :], zg_ref[rs, :], do_ref[rs, :]
            sg = _sigmoid(zg)
            rms = _head_rms(o)
            n = o * rms
            don = doa * (zg * sg)
            sums["dgn"] = sums["dgn"] + jnp.sum(don * n, axis=0, keepdims=True)
            dn = don * gn_row
            p["d_o"] = rms * (dn - n * _head_mean(dn * n))
            dz_ref[rs, 3 * HW:4 * HW] = (doa * (n * gn_row) * (sg * (1.0 + zg * (1.0 - sg)))).astype(BF16)
            p["eq"], p["ek"] = _hgrn_decay_factors(b_scr, r0, b, in_sub)
            b_last = b_scr[r0 + CHUNK - 1:r0 + CHUNK, :]
            p["lam"], p["e_last"], p["lam_last"] = jnp.exp(b), jnp.exp(b_last - b), jnp.exp(b_last)
            p["qe"], p["qg"], p["kd"] = p["q"] * p["eq"], p["q"] * p["lam"], p["kk"] * p["e_last"]
            p["ke"] = [p["kk"] * e for e in p["ek"]]
            p["q16"] = [_per_sub_chunk(_head(p["qe"], h), in_sub_head).astype(BF16) for h in range(HEADS)]
            p["ke16"] = [_lanes([_head(p["ke"][j], h) for j in range(N_SUB)]).astype(BF16) for h in range(HEADS)]
            return p

        def recurrence(c, p):
            m = {k: [] for k in ("dv", "gq", "gk", "dqi", "dkd", "st")}
            a, da, dv_state = [], [], []
            for h in range(HEADS):
                vh, doh = _head(p["v"], h), _head(p["d_o"], h)
                st0, ds1 = st_ref[c, h], dstate[h]
                a.append(jnp.where(causal, _mm_nt(p["q16"][h], p["ke16"][h]), 0.0))
                da.append(jnp.where(causal, _mm_nt(doh, vh), 0.0))
                dv_state.append(_mm_nt(_head(p["kd"], h), ds1))
                m["dqi"].append(_mm(doh, st0))
                m["dkd"].append(_mm(vh, ds1))
                m["st"].append(jnp.sum(st0 * ds1, axis=0, keepdims=True))
                dstate[h] = ds1 * _head(p["lam_last"], h) + _mm_tn(doh, _head(p["qg"], h))
            for h in range(HEADS):
                m["dv"].append(_mm_tn(a[h], _head(p["d_o"], h)) + dv_state[h])
                m["gq"].append(_own_lane_block(_mm(da[h], p["ke16"][h]), in_sub_head))
                m["gk"].append(_mm_tn(da[h], p["q16"][h]))
            return m

        def back(p, m):
            rs = p["rs"]
            dz_ref[rs, 2 * HW:3 * HW] = _lanes(m["dv"]).astype(BF16)
            gq = _lanes(m["gq"])
            gk = [_lanes([m["gk"][h][:, HD * j:HD * (j + 1)] for h in range(HEADS)]) for j in range(N_SUB)]
            dq_inter = p["lam"] * _lanes(m["dqi"])
            dq = p["eq"] * gq + dq_inter
            dk_intra = sum(p["ek"][j] * gk[j] for j in range(N_SUB))
            dk_state = _lanes(m["dkd"]) * p["e_last"]
            db_intra = (p["qe"].astype(BF16).astype(F32) * gq
                        - sum(p["ke"][j].astype(BF16).astype(F32) * gk[j] for j in range(N_SUB)))
            dlf = (_tri_dot(upper, db_intra + p["q"] * dq_inter, 2) + _tri_dot(strict_lower, p["kk"] * dk_state, 2)
                   + p["lam_last"] * _lanes(m["st"]))
            sig, sq, zq = p["sig"], p["sq"], p["zq"]
            df = dlf / p["f"] - (dk_intra + dk_state)
            sums["dlb"] = sums["dlb"] + jnp.sum(df * (1.0 - sig), axis=0, keepdims=True)
            dz_ref[rs, HW:2 * HW] = (df * (1.0 - lb) * sig * (1.0 - sig)).astype(BF16)
            dz_ref[rs, 0:HW] = (dq * (sq * (1.0 + zq * (1.0 - sq)))).astype(BF16)

        p = front(CHUNKS_PER_STEP - 1)
        for c in reversed(range(CHUNKS_PER_STEP)):
            m = recurrence(c, p)
            p_next = front(c - 1) if c > 0 else None
            back(p, m)
            p = p_next
        dlb_acc[...] += sums["dlb"]
        for h in range(HEADS):
            dgn_ref[h:h + 1, 0:HD] += _head(sums["dgn"], h)

        @pl.when(i == n_steps - 1)
        def _():
            dl0 = dlb_acc[...] * lb * (1.0 - lb)
            dlb_ref[0:1, 0:HW] = dl0
            dlb_ref[1:2, 0:HW] = -dl0

    rows = CHUNK * CHUNKS_PER_STEP
    n_steps = s // rows
    rev = lambda i: n_steps - 1 - i
    zspec = lambda cb: pl.BlockSpec((rows, HW), lambda i, cb=cb: (rev(i), cb))
    slot = jax.ShapeDtypeStruct((SLOT, D_MODEL), F32)
    return _call_behind(
        deps, body, name="hgrn_bwd", grid=(n_steps,),
        out_shape=[jax.ShapeDtypeStruct((s, IN_WIDTH), BF16), slot, slot],
        in_specs=[zspec(0), zspec(1), zspec(2), zspec(3), pl.BlockSpec((rows, HW), lambda i: (rev(i), 0)),
                  pl.BlockSpec((rows, D_MODEL), lambda i: (rev(i), 0)), VMEM_WHOLE,
                  pl.BlockSpec((CHUNKS_PER_STEP, HEADS, HD, HD), lambda i: (rev(i), 0, 0, 0)), _full((2, HW)),
                  _full((HEADS, HD)), ANY_SPACE],
        out_specs=[pl.BlockSpec((rows, 4 * HW), lambda i: (rev(i), 0)), _full((SLOT, D_MODEL)), _full((SLOT, D_MODEL))],
        scratch_shapes=[pltpu.VMEM((HEADS, HD, HD), F32), pltpu.VMEM((rows, HW), F32), pltpu.VMEM((1, HW), F32),
                        pltpu.VMEM((rows, HW), F32)],
        input_output_aliases={10: 0},
        compiler_params=_params(("arbitrary",)),
    )(z, z, z, z, o, dx1, w_out, states, lb_logits, gn, dz_in)


def _in_bwd(dz, w_t, x0, g, dx1, deps):
    s = x0.shape[0]
    tm = min(WIDE_ROW_TILE, s)

    def body(dz_ref, w_ref, x_ref, g_ref, d_ref, dx_ref, slot_ref):
        _zero_slot(slot_ref)
        dx, dg = _rms_bwd(x_ref[...], g_ref[...], _mm(dz_ref[...], w_ref[...]))
        dx_ref[...] = d_ref[...] + dx
        slot_ref[0:1, :] += dg

    row_f32 = pl.BlockSpec((tm, D_MODEL), lambda i: (i, 0))
    return _call_behind(
        deps, body, name="in_bwd", grid=(s // tm,),
        out_shape=[jax.ShapeDtypeStruct((s, D_MODEL), F32), jax.ShapeDtypeStruct((SLOT, D_MODEL), F32)],
        in_specs=[pl.BlockSpec((tm, IN_WIDTH), lambda i: (i, 0)), VMEM_WHOLE, row_f32, _full((1, D_MODEL)), row_f32],
        out_specs=[row_f32, _full((SLOT, D_MODEL))],
        compiler_params=_params(("arbitrary",)),
    )(dz, w_t, x0, g, dx1)


def kernel(x, mem, norm_mix_g, w_in, lb_logits, hgrn_norm_g, w_pool, pool_scale, w_out, norm_x_g, norm_mem_g, w_xq, w_xk, w_xv, w_xo, norm_ffn_g, w_ff1, w_ff2, final_norm_g, loss_target, m_norm_mix_g, m_w_in, m_lb_logits, m_hgrn_norm_g, m_w_pool, m_pool_scale, m_w_out, m_norm_x_g, m_norm_mem_g, m_w_xq, m_w_xk, m_w_xv, m_w_xo, m_norm_ffn_g, m_w_ff1, m_w_ff2, m_final_norm_g, v_norm_mix_g, v_w_in, v_lb_logits, v_hgrn_norm_g, v_w_pool, v_pool_scale, v_w_out, v_norm_x_g, v_norm_mem_g, v_w_xq, v_w_xk, v_w_xv, v_w_xo, v_norm_ffn_g, v_w_ff1, v_w_ff2, v_final_norm_g):
    x0 = x[0]
    mem0 = mem[0]
    tgt = loss_target[0]
    gn = hgrn_norm_g[0]
    gfin = final_norm_g.reshape(1, D_MODEL)
    wp = w_pool[0]
    heads_2d = lambda w: w.reshape(D_MODEL // N_DEV, D_MODEL)
    xo_2d = lambda w: w.reshape(D_MODEL, D_MODEL // N_DEV)

    first = _all_gather_weights([w_in[0].T], [w_out[0], heads_2d(w_xq), heads_2d(w_xk), heads_2d(w_xv), xo_2d(w_xo).T,
                                              w_ff1[0], w_ff2[0]])
    win_t = first[0].reshape(IN_WIDTH, D_MODEL)
    ga_attn, ga_mlp = _gather_first_start([first[1:6], first[6:8]], "gather_first_start")

    z, h = _in_proj(x0, norm_mix_g, win_t, deps=[ga_attn[3]])
    lands = _split_wait(_gather_first_copies, ga_attn, h, "gather_attn_first_wait")
    gr_attn = _gather_relay_start(lands, "gather_attn_relay_start")
    mixed_a, o_pre, states = _hgrn_fwd(z, lb_logits, gn, deps=[gr_attn[3]])
    lands = _split_wait(_gather_relay_copies, gr_attn, o_pre, "gather_attn_relay_wait")
    gl_attn = _gather_last_start(lands, "gather_attn_last_start")
    lands_mlp = _split_wait(_gather_first_copies, ga_mlp, gl_attn[3], "gather_mlp_first_wait")
    gr_mlp = _gather_relay_start(lands_mlp, "gather_mlp_relay_start")
    mixed, pooled = _pool_fwd(z, wp, pool_scale, mixed_a, deps=[gr_mlp[3]])
    lands = _split_wait(_gather_last_copies, gl_attn, pooled, "gather_attn_last_wait")
    wout_f, wq_f, wk_f, wv_f, wo_t = (t.reshape(D_MODEL, D_MODEL) for t in lands)
    hm, xk, xv = _mem_kv(mem0, norm_mem_g, wk_f, wv_f, deps=[])
    lands_mlp = _split_wait(_gather_relay_copies, gr_mlp, xk, "gather_mlp_relay_wait")
    gl_mlp = _gather_last_start(lands_mlp, "gather_mlp_last_start")
    x1, x2, hq, xq, att = _mix_xattn_fwd(x0, mixed, wout_f, norm_x_g, wq_f, xk, xv, wo_t, deps=[gl_mlp[3]])
    w1_b, w2_b = _split_wait(_gather_last_copies, gl_mlp, x2, "gather_mlp_last_wait")
    dx3, dx3_16, u, hf, slot_fin = _mlp_fwd_loss(x2, norm_ffn_g, w1_b, w2_b.reshape(D_FF, D_MODEL), gfin, tgt)

    rows = lambda t, r: t.reshape(N_DEV, r, D_MODEL)
    dw2 = _wgrad(u, dx3_16, "wgrad_ff2")
    ex_ff2 = _all_to_all_start([rows(dw2, FF_BLK)], [], "exchange_ff2_start")
    da, dx2, slot_ffn = _mlp_bwd(dx3, u, x2, norm_ffn_g, w1_b, w2_b, deps=[ex_ff2[3]])
    dw1 = _wgrad(hf, da, "wgrad_ff1", col_blocks=True)
    ex_ff1 = _all_to_all_start([dw1], [], "exchange_ff1_start")
    dx1, dx1_16, dxq, dxk, dxv, slot_x = _xattn_bwd(dx2, x1, norm_x_g, xq, xk, xv, wq_f, wo_t, deps=[ex_ff1[3]])
    dwo_t = _wgrad(dx2, att, "wgrad_xo")
    dwq = _wgrad(hq, dxq, "wgrad_xq")
    dwk, dwv, slot_mem = _mem_bwd(mem0, norm_mem_g, hm, dxk, dxv, wk_f, wv_f)
    ex_attn = _all_to_all_start([rows(dwq, 128), rows(dwk, 128), rows(dwv, 128), rows(dwo_t, 128)], [],
                                "exchange_attn_start")
    dwout = _wgrad(mixed, dx1_16, "wgrad_out")
    dz_pool, d_wpool, slot_ps = _pool_bwd(dx1_16, wout_f, pooled, wp, pool_scale, deps=[ex_attn[3]])
    small0 = jnp.concatenate([slot_x, slot_mem, slot_ffn, slot_fin, slot_ps], axis=0)
    ex_out = _all_to_all_start([rows(dwout, 128)], [small0, d_wpool], "exchange_out_start")
    dz, slot_lb, slot_gn = _hgrn_bwd(z, o_pre, dx1_16, wout_f, states, lb_logits, gn, dz_pool, deps=[ex_out[3]])
    (r_2,) = _split_wait(_all_to_all_copies(1), ex_ff2, dz, "exchange_ff2_wait")
    dwin_t, ff2_update = _wgrad(dz, h, "wgrad_in", update=(r_2, w_ff2[0], m_w_ff2[0], v_w_ff2[0]))
    ex_in = _all_to_all_start([rows(dwin_t, 320)], [], "exchange_in_start")
    grad_x, slot_mix = _in_bwd(dz, win_t, x0, norm_mix_g, dx1, deps=[ex_in[3]])
    small1 = jnp.concatenate([slot_mix, slot_lb, slot_gn], axis=0)
    ex_mix = _all_to_all_start([], [small1], "exchange_mix_start")

    out = {}
    out["w_ff2"] = ff2_update
    (r_1,) = _split_wait(_all_to_all_copies(1), ex_ff1, ex_mix[3], "exchange_ff1_wait")
    out["w_ff1"] = _sum_adamw(r_1, w_ff1[0], m_w_ff1[0], v_w_ff1[0], "adamw_ff1")
    r_q, r_k, r_v, r_o = _split_wait(_all_to_all_copies(4), ex_attn, out["w_ff1"][1], "exchange_attn_wait")
    sums = _sum_sources_whole([r_q, r_k, r_v, r_o], "sum_grad_attn")
    g_attn = [g.reshape(w_xq.shape) for g in sums[:3]] + [sums[3].T]
    attn = _adamw_whole([(g_attn[0], w_xq, m_w_xq, v_w_xq), (g_attn[1], w_xk, m_w_xk, v_w_xk),
                         (g_attn[2], w_xv, m_w_xv, v_w_xv),
                         (g_attn[3], xo_2d(w_xo), xo_2d(m_w_xo), xo_2d(v_w_xo))], "adamw_attn")
    for n, g, res in zip(("w_xq", "w_xk", "w_xv", "w_xo"), g_attn, attn):
        out[n] = (g, *res)
    r_out, r_small0, r_wpool = _split_wait(_all_to_all_copies(1), ex_out, attn[3][0], "exchange_out_wait")
    out["w_out"] = _sum_adamw(r_out, w_out[0], m_w_out[0], v_w_out[0], "adamw_out")
    (r_in,) = _split_wait(_all_to_all_copies(1), ex_in, out["w_out"][1], "exchange_in_wait")
    in_t = _sum_adamw(r_in, w_in[0].T, m_w_in[0].T, v_w_in[0].T, "adamw_in")
    out["w_in"] = tuple(t.T for t in in_t)
    (r_small1,) = _split_wait(_all_to_all_copies(0), ex_mix, in_t[1], "exchange_mix_wait")
    row = lambda t: t.reshape(1, -1)
    small_params = {
        "norm_mix_g": (norm_mix_g, m_norm_mix_g, v_norm_mix_g),
        "lb_logits": (lb_logits, m_lb_logits, v_lb_logits),
        "hgrn_norm_g": (hgrn_norm_g[0], m_hgrn_norm_g[0], v_hgrn_norm_g[0]),
        "pool_scale": (pool_scale, m_pool_scale, v_pool_scale),
        "norm_x_g": (norm_x_g, m_norm_x_g, v_norm_x_g),
        "norm_mem_g": (norm_mem_g, m_norm_mem_g, v_norm_mem_g),
        "norm_ffn_g": (norm_ffn_g, m_norm_ffn_g, v_norm_ffn_g),
        "final_norm_g": (row(final_norm_g), row(m_final_norm_g), row(v_final_norm_g)),
        "w_pool": (wp, m_w_pool[0], v_w_pool[0]),
    }
    loss, small_out = _small_update([r_small0, r_small1], r_wpool, small_params)
    out.update(small_out)

    shapes = dict(norm_mix_g=norm_mix_g, w_in=w_in, lb_logits=lb_logits, hgrn_norm_g=hgrn_norm_g, w_pool=w_pool,
                  pool_scale=pool_scale, w_out=w_out, norm_x_g=norm_x_g, norm_mem_g=norm_mem_g, w_xq=w_xq, w_xk=w_xk,
                  w_xv=w_xv, w_xo=w_xo, norm_ffn_g=norm_ffn_g, w_ff1=w_ff1, w_ff2=w_ff2, final_norm_g=final_norm_g)
    order = list(shapes)
    group = lambda k: [out[n][k].reshape(shapes[n].shape) for n in order]
    return (loss.reshape(()), grad_x.reshape(x.shape), *group(0), *group(1), *group(2), *group(3))
```

```python
import jax
import jax.numpy as jnp
from jax import lax
from jax.experimental import pallas as pl
from jax.experimental.pallas import tpu as pltpu

F32 = jnp.float32
BF16 = jnp.bfloat16

D_MODEL = 1024
N_DEV = 8
HEADS = 4
HD = 128
HW = HEADS * HD
IN_WIDTH = 5 * HW
XHD = 256
MEM_LEN = 256
D_FF = 4096
FF_BLK = D_FF // N_DEV
POOL_WINDOWS = (2, 4, 8, 16)
POOL_HALO = 16
CHUNK = 64
CHUNKS_PER_STEP = 8
SUB = 16
N_SUB = CHUNK // SUB
EXP_CAP = 80.0
EPS = 1e-6
TINY = 1e-30
ROW_TILE = 512
WIDE_ROW_TILE = 1024
SLOT = 8
V7X_VMEM_LIMIT = 56 * 1024 * 1024

ADAM_LR = 0.001
ADAM_B1 = 0.9
ADAM_B2 = 0.999
ADAM_EPS = 1e-08
ADAM_WD = 0.01
ADAM_STEP = 10

MESH_ID = pl.DeviceIdType.MESH


def _params(sem=None, vmem=V7X_VMEM_LIMIT):
    return pltpu.CompilerParams(dimension_semantics=sem, vmem_limit_bytes=vmem)


def _mm(a, b):
    return lax.dot_general(a.astype(BF16), b.astype(BF16), (((1,), (0,)), ((), ())), preferred_element_type=F32)


def _mm_nt(a, b):
    return lax.dot_general(a.astype(BF16), b.astype(BF16), (((1,), (1,)), ((), ())), preferred_element_type=F32)


def _mm_tn(a, b):
    return lax.dot_general(a.astype(BF16), b.astype(BF16), (((0,), (0,)), ((), ())), preferred_element_type=F32)


def _sigmoid(x):
    return 1.0 / (1.0 + jnp.exp(-x))


def _rms(x):
    return lax.rsqrt(jnp.mean(x * x, axis=-1, keepdims=True) + EPS)


def _rms_bwd(x, g, dh):
    r = _rms(x)
    n = x * r
    dn = dh * g
    dx = r * (dn - n * jnp.mean(dn * n, axis=-1, keepdims=True))
    return dx, jnp.sum(dh * n, axis=0, keepdims=True)


def _tri_dot(tri, x, passes):
    acc = None
    rest = x
    for _ in range(passes):
        piece = rest.astype(BF16)
        part = lax.dot_general(tri, piece, (((1,), (0,)), ((), ())), preferred_element_type=F32)
        acc = part if acc is None else acc + part
        rest = rest - piece.astype(F32)
    return acc


def _adam_update(g, w, m, v):
    nm = ADAM_B1 * m + (1.0 - ADAM_B1) * g
    nv = ADAM_B2 * v + (1.0 - ADAM_B2) * (g * g)
    m_hat = nm / (1.0 - ADAM_B1 ** ADAM_STEP)
    v_hat = nv / (1.0 - ADAM_B2 ** ADAM_STEP)
    return -ADAM_LR * (m_hat / (jnp.sqrt(v_hat) + ADAM_EPS) + ADAM_WD * w), nm, nv


def _full(shape):
    return pl.BlockSpec(shape, lambda *_: (0,) * len(shape))


VMEM_WHOLE = pl.BlockSpec(memory_space=pltpu.VMEM)
ANY_SPACE = pl.BlockSpec(memory_space=pl.ANY)


def _mesh_pos():
    return lax.axis_index("x"), lax.axis_index("y"), lax.axis_index("c")


def _flat(px, py, pc):
    return 4 * px + 2 * py + pc


def _all_gather_weights(shards, cast_only):
    n, nc = len(shards), len(cast_only)
    step = 64

    def body(*refs):
        x_refs, c_refs = refs[:n], refs[n:n + nc]
        out_refs, cast_refs = refs[n + nc:2 * n + nc], refs[2 * n + nc:2 * n + 2 * nc]
        bufs = refs[2 * n + 2 * nc:3 * n + 2 * nc]
        send_sems, recv_sems, local_sems = refs[3 * n + 2 * nc:]
        _handshake(_peers_first_level())
        x, y, c = _mesh_pos()
        me, sibling = (x, y, c), (x, y, 1 - c)
        chips = [(1 - x, y), (x, 1 - y), (1 - x, 1 - y)]

        def copy(a, k, blk, to, src=None):
            rows = out_refs[a].at[_flat(*blk)]
            return pltpu.make_async_remote_copy(
                src_ref=rows if src is None else src, dst_ref=rows,
                send_sem=send_sems.at[7 * a + k], recv_sem=recv_sems.at[7 * a + k], device_id=to, device_id_type=MESH_ID)

        def cast_rows(src, dst, rows):
            def cast(i, carry):
                r0 = pl.multiple_of(i * step, step)
                dst[pl.ds(r0, step), :] = src[pl.ds(r0, step), :].astype(BF16)
                return carry
            lax.fori_loop(0, rows // step, cast, 0)

        first, mine = [], []
        for a in range(n):
            cast_rows(x_refs[a], bufs[a], shards[a].shape[0])
            mine.append(pltpu.make_async_copy(bufs[a], out_refs[a].at[_flat(*me)], local_sems.at[a]))
            first.append(copy(a, 0, me, sibling, src=bufs[a]))
            first += [copy(a, 1 + j, me, (*chip, c), src=bufs[a]) for j, chip in enumerate(chips)]
            for cp in [mine[-1]] + first[-4:]:
                cp.start()
        for a in range(nc):
            cast_rows(c_refs[a], cast_refs[a], cast_only[a].shape[0])
        passed = []
        for j, chip in enumerate(chips):
            for a in range(n):
                copy(a, 1 + j, (*chip, c), me).wait_recv()
                passed.append(copy(a, 4 + j, (*chip, c), sibling))
                passed[-1].start()
        for a in range(n):
            copy(a, 0, sibling, me).wait_recv()
            for j, chip in enumerate(chips):
                copy(a, 4 + j, (*chip, 1 - c), me).wait_recv()
        for cp in first + passed:
            cp.wait_send()
        for cp in mine:
            cp.wait()

    return pl.pallas_call(
        body, name="all_gather_w_in",
        out_shape=[jax.ShapeDtypeStruct((N_DEV,) + s.shape, BF16) for s in shards]
        + [jax.ShapeDtypeStruct(s.shape, BF16) for s in cast_only],
        in_specs=[VMEM_WHOLE] * (n + nc), out_specs=[ANY_SPACE] * n + [VMEM_WHOLE] * nc,
        scratch_shapes=[pltpu.VMEM(s.shape, BF16) for s in shards]
        + [pltpu.SemaphoreType.DMA((7 * n,)), pltpu.SemaphoreType.DMA((7 * n,)), pltpu.SemaphoreType.DMA((n,))],
        compiler_params=pltpu.CompilerParams(vmem_limit_bytes=V7X_VMEM_LIMIT, collective_id=GATHER_W_IN_ID),
    )(*shards, *cast_only)


HBM_SPEC = pl.BlockSpec(memory_space=pltpu.HBM)
SEM_SPEC = pl.BlockSpec(memory_space=pltpu.SEMAPHORE)
EFFECT = pltpu.SideEffectType.DATAFLOW_SIDE_EFFECTING
TOKEN = jax.ShapeDtypeStruct((8, 128), F32)


def _in_hbm(a):
    return pltpu.with_memory_space_constraint(a, pltpu.HBM)


START_IDS = {name: i for i, name in enumerate((
    "gather_first_start", "gather_attn_forward_start", "gather_mlp_forward_start", "exchange_ff2_start",
    "exchange_ff1_start", "exchange_attn_start", "exchange_out_start", "exchange_in_start", "exchange_mix_start"))}


GATHER_W_IN_ID = len(START_IDS)


def _handshake(peers):
    barrier = pltpu.get_barrier_semaphore()
    for peer in peers:
        pl.semaphore_signal(barrier, inc=1, device_id=peer, device_id_type=MESH_ID)
    pl.semaphore_wait(barrier, len(peers))


def _peers_all():
    x, y, c = _mesh_pos()
    return [(1 - x if k & 4 else x, 1 - y if k & 2 else y, 1 - c if k & 1 else c) for k in range(1, N_DEV)]


def _peers_first_level():
    x, y, c = _mesh_pos()
    return [(x, y, 1 - c), (1 - x, y, c), (x, 1 - y, c), (1 - x, 1 - y, c)]


def _peers_sibling():
    x, y, c = _mesh_pos()
    return [(x, y, 1 - c)]


def _split_start(copies_of, srcs, lands, n_sems, name, peers_of, collective_id):
    ns, nl, k = len(srcs), len(lands), len(n_sems)

    def body(*refs):
        _handshake(peers_of())
        src_refs, land_refs = refs[:ns], refs[ns:ns + nl]
        sems = refs[ns + nl:ns + nl + k]
        token = refs[-1]
        for cp in copies_of(src_refs, land_refs, sems):
            cp.start()
        token[...] = jnp.zeros_like(token)

    outs = pl.pallas_call(
        body, name=name,
        out_shape=[pltpu.SemaphoreType.DMA((q,)) for q in n_sems]
        + [pltpu.HBM(a.shape, a.dtype) for a in list(srcs) + list(lands)] + [TOKEN],
        in_specs=[HBM_SPEC] * (ns + nl),
        out_specs=[SEM_SPEC] * k + [HBM_SPEC] * (ns + nl) + [VMEM_WHOLE],
        input_output_aliases={i: k + i for i in range(ns + nl)},
        compiler_params=pltpu.CompilerParams(has_side_effects=EFFECT, collective_id=collective_id),
    )(*[_in_hbm(a) for a in list(srcs) + list(lands)])
    return outs[:k], outs[k:k + ns], outs[k + ns:k + ns + nl], outs[-1]


def _split_wait(copies_of, handle, after, name):
    sems, srcs, lands, _ = handle
    ns, nl, k = len(srcs), len(lands), len(sems)

    def body(*refs):
        src_refs, land_refs = refs[:ns], refs[ns:ns + nl]
        sem_refs = refs[ns + nl:ns + nl + k]
        for cp in copies_of(src_refs, land_refs, sem_refs):
            cp.wait()

    outs = pl.pallas_call(
        body, name=name,
        out_shape=[pltpu.HBM(a.shape, a.dtype) for a in list(srcs) + list(lands)],
        in_specs=[HBM_SPEC] * (ns + nl) + [SEM_SPEC] * k + [ANY_SPACE],
        out_specs=[HBM_SPEC] * (ns + nl),
        input_output_aliases={i: i for i in range(ns + nl)},
        compiler_params=pltpu.CompilerParams(has_side_effects=EFFECT),
    )(*srcs, *lands, *sems, after)
    return outs[ns:]


def _gather_first_copies(shard_refs, land_refs, sems):
    send_sems, recv_sems, local_sems = sems
    x, y, c = _mesh_pos()
    me = _flat(x, y, c)
    peers = [(x, y, 1 - c), (1 - x, y, c), (x, 1 - y, c), (1 - x, 1 - y, c)]
    copies = []
    for a, (shard, land) in enumerate(zip(shard_refs, land_refs)):
        copies.append(pltpu.make_async_copy(shard, land.at[me], local_sems.at[a]))
        for k, peer in enumerate(peers):
            copies.append(pltpu.make_async_remote_copy(
                src_ref=shard, dst_ref=land.at[me], send_sem=send_sems.at[4 * a + k], recv_sem=recv_sems.at[4 * a + k],
                device_id=peer, device_id_type=MESH_ID))
    return copies


def _gather_forward_copies(src_refs, land_refs, sems):
    del src_refs
    send_sems, recv_sems = sems
    x, y, c = _mesh_pos()
    chips = [(1 - x, y), (x, 1 - y), (1 - x, 1 - y)]
    copies = []
    for a, land in enumerate(land_refs):
        for j, chip in enumerate(chips):
            rows = land.at[_flat(*chip, c)]
            copies.append(pltpu.make_async_remote_copy(
                src_ref=rows, dst_ref=rows, send_sem=send_sems.at[3 * a + j], recv_sem=recv_sems.at[3 * a + j],
                device_id=(x, y, 1 - c), device_id_type=MESH_ID))
    return copies


def _gather_first_start(groups, name):
    shards = [s for g in groups for s in g]
    lands = [lax.empty((N_DEV,) + s.shape, s.dtype) for s in shards]
    bounds = [sum(len(g) for g in groups[:i]) for i in range(len(groups) + 1)]

    def copies_of(src_refs, land_refs, sems):
        copies = []
        for i in range(len(groups)):
            lo, hi = bounds[i], bounds[i + 1]
            copies += _gather_first_copies(src_refs[lo:hi], land_refs[lo:hi], sems[3 * i:3 * i + 3])
        return copies

    n_sems = tuple(q for g in groups for q in (4 * len(g), 4 * len(g), len(g)))
    sems, srcs, lands, token = _split_start(copies_of, shards, lands, n_sems, name, _peers_first_level, START_IDS[name])
    return [(sems[3 * i:3 * i + 3], srcs[bounds[i]:bounds[i + 1]], lands[bounds[i]:bounds[i + 1]], token)
            for i in range(len(groups))]


def _gather_forward_start(lands, name):
    n = len(lands)
    return _split_start(_gather_forward_copies, [], lands, (3 * n, 3 * n), name, _peers_sibling, START_IDS[name])


def _all_to_all_copies(n_scattered):
    def copies_of(src_refs, land_refs, sems):
        send_sems, recv_sems, local_sems = sems
        x, y, c = _mesh_pos()
        me = _flat(x, y, c)
        copies = []
        for a, (src, land) in enumerate(zip(src_refs, land_refs)):
            scattered = a < n_scattered
            copies.append(pltpu.make_async_copy(src.at[me] if scattered else src, land.at[me], local_sems.at[a]))
            for k in range(1, N_DEV):
                peer = (1 - x if k & 4 else x, 1 - y if k & 2 else y, 1 - c if k & 1 else c)
                copies.append(pltpu.make_async_remote_copy(
                    src_ref=src.at[_flat(*peer)] if scattered else src, dst_ref=land.at[me],
                    send_sem=send_sems.at[7 * a + k - 1], recv_sem=recv_sems.at[7 * a + k - 1],
                    device_id=peer, device_id_type=MESH_ID))
        return copies
    return copies_of


def _all_to_all_start(scattered, broadcast, name):
    srcs = list(scattered) + list(broadcast)
    lands = [lax.empty(a.shape, a.dtype) for a in scattered] + [lax.empty((N_DEV,) + a.shape, a.dtype) for a in broadcast]
    n = len(srcs)
    return _split_start(_all_to_all_copies(len(scattered)), srcs, lands, (7 * n, 7 * n, n), name, _peers_all,
                        START_IDS[name])


def _call_behind(deps, body, *, in_specs, **kwargs):
    n_in, n_dep = len(in_specs), len(deps)

    def body_without_deps(*refs):
        return body(*refs[:n_in], *refs[n_in + n_dep:])

    call = pl.pallas_call(body_without_deps, in_specs=list(in_specs) + [ANY_SPACE] * n_dep, **kwargs)
    return lambda *operands: call(*operands, *deps)


def _row_tile(rows):
    if rows <= 2 * 256:
        return rows
    for cand in (256, 128, 64, 32, 16):
        if rows % cand == 0:
            return cand
    return rows


def _adamw_whole(groups, name):
    n = len(groups)

    def body(*refs):
        for i in range(n):
            g_ref, w_ref, m_ref, v_ref = refs[4 * i:4 * i + 4]
            d_ref, nm_ref, nv_ref = refs[4 * n + 3 * i:4 * n + 3 * i + 3]
            d_ref[...], nm_ref[...], nv_ref[...] = _adam_update(g_ref[...], w_ref[...], m_ref[...], v_ref[...])

    outs = pl.pallas_call(
        body, name=name, out_shape=[jax.ShapeDtypeStruct(grp[0].shape, F32) for grp in groups for _ in range(3)],
        in_specs=[VMEM_WHOLE] * (4 * n), out_specs=[VMEM_WHOLE] * (3 * n),
        compiler_params=_params(),
    )(*[t for grp in groups for t in grp])
    return [outs[3 * i:3 * i + 3] for i in range(n)]


def _sum_sources_whole(recvs, name):
    n = len(recvs)

    def body(*refs):
        for r_ref, o_ref in zip(refs[:n], refs[n:]):
            acc = r_ref[0].astype(F32)
            for d in range(1, N_DEV):
                acc = acc + r_ref[d].astype(F32)
            o_ref[...] = acc

    return pl.pallas_call(
        body, name=name, out_shape=[jax.ShapeDtypeStruct(r.shape[1:], F32) for r in recvs],
        in_specs=[VMEM_WHOLE] * n, out_specs=[VMEM_WHOLE] * n,
        compiler_params=_params(),
    )(*recvs)


def _sum_adamw(recv, w, m, v, name):
    _, rows, cols = recv.shape
    tile = _row_tile(rows)

    def body(r_ref, w_ref, m_ref, v_ref, g_ref, d_ref, nm_ref, nv_ref):
        acc = r_ref[0].astype(F32)
        for d in range(1, N_DEV):
            acc = acc + r_ref[d].astype(F32)
        g_ref[...] = acc
        d_ref[...], nm_ref[...], nv_ref[...] = _adam_update(acc, w_ref[...], m_ref[...], v_ref[...])

    spec = pl.BlockSpec((tile, cols), lambda i: (i, 0))
    shp = jax.ShapeDtypeStruct((rows, cols), F32)
    return pl.pallas_call(
        body, name=name, grid=(rows // tile,), out_shape=[shp] * 4,
        in_specs=[pl.BlockSpec((N_DEV, tile, cols), lambda i: (0, i, 0)), spec, spec, spec], out_specs=[spec] * 4,
        compiler_params=_params(("parallel",)),
    )(recv, w, m, v)


SMALL_SLOTS = {"norm_x_g": (0, 0, 1, D_MODEL), "norm_mem_g": (0, 8, 1, D_MODEL), "norm_ffn_g": (0, 16, 1, D_MODEL),
               "final_norm_g": (0, 24, 1, D_MODEL), "pool_scale": (0, 32, 1, HW),
               "norm_mix_g": (1, 0, 1, D_MODEL), "lb_logits": (1, 8, 2, HW), "hgrn_norm_g": (1, 16, HEADS, HD)}
LOSS_ROW = 25
SMALL_ORDER = ("norm_mix_g", "lb_logits", "hgrn_norm_g", "pool_scale", "norm_x_g", "norm_mem_g", "norm_ffn_g",
               "final_norm_g", "w_pool")


def _small_update(srecvs, wprecv, params):
    flat = [t for n in SMALL_ORDER for t in params[n]]
    nb = len(srecvs)
    n_in = nb + 1 + len(flat)

    def body(*refs):
        s_refs, wp_ref = refs[0:nb], refs[nb]
        in_refs = refs[nb + 1:n_in]
        loss_ref = refs[n_in]
        out_refs = refs[n_in + 1:-nb]
        accs = refs[-nb:]
        for s_ref, acc in zip(s_refs, accs):
            total = s_ref[0]
            for d in range(1, N_DEV):
                total = total + s_ref[d]
            acc[...] = total
        loss_ref[...] = accs[0][LOSS_ROW:LOSS_ROW + 1, 0:1]
        for i, name in enumerate(SMALL_ORDER):
            w_ref, m_ref, v_ref = in_refs[3 * i:3 * i + 3]
            g_ref, d_ref, nm_ref, nv_ref = out_refs[4 * i:4 * i + 4]
            if name == "w_pool":
                g = wp_ref[0]
                for d in range(1, N_DEV):
                    g = g + wp_ref[d]
            else:
                buf, r0, nr, nc = SMALL_SLOTS[name]
                g = accs[buf][r0:r0 + nr, 0:nc]
            g_ref[...] = g
            d_ref[...], nm_ref[...], nv_ref[...] = _adam_update(g, w_ref[...], m_ref[...], v_ref[...])

    out_shape = [jax.ShapeDtypeStruct((1, 1), F32)]
    for n in SMALL_ORDER:
        out_shape += [jax.ShapeDtypeStruct(params[n][0].shape, F32)] * 4
    outs = pl.pallas_call(
        body, name="small_update", out_shape=out_shape,
        in_specs=[VMEM_WHOLE] * n_in, out_specs=[VMEM_WHOLE] * len(out_shape),
        scratch_shapes=[pltpu.VMEM(r.shape[1:], F32) for r in srecvs],
        compiler_params=_params(),
    )(*srecvs, wprecv, *flat)
    return outs[0], {n: outs[1 + 4 * i:5 + 4 * i] for i, n in enumerate(SMALL_ORDER)}


def _in_proj(x, g, w_t, deps):
    s = x.shape[0]
    tm = min(ROW_TILE, s)

    n, depth = s // tm, 3

    def body(x_hbm, g_ref, w_ref, z_hbm, h_hbm, xb, zb, hb, sx, sz, sh):
        def fetch(i):
            return pltpu.make_async_copy(x_hbm.at[pl.ds(i * tm, tm)], xb.at[i % depth], sx.at[i % depth])

        def put_z(i):
            return pltpu.make_async_copy(zb.at[i % depth], z_hbm.at[pl.ds(i * tm, tm)], sz.at[i % depth])

        def put_h(i):
            return pltpu.make_async_copy(hb.at[i % depth], h_hbm.at[pl.ds(i * tm, tm)], sh.at[i % depth])

        for i in range(min(depth, n)):
            fetch(i).start()
        for i in range(n):
            b = i % depth
            fetch(i).wait()
            if i >= depth:
                put_z(i - depth).wait()
                put_h(i - depth).wait()
            xv = xb[b]
            h = (xv * _rms(xv) * g_ref[...]).astype(BF16)
            hb[b] = h
            zb[b] = _mm_nt(h, w_ref[...])
            put_z(i).start()
            put_h(i).start()
            if i + depth < n:
                fetch(i + depth).start()
        for i in range(max(0, n - depth), n):
            put_z(i).wait()
            put_h(i).wait()

    return _call_behind(
        deps, body, name="in_proj",
        out_shape=[jax.ShapeDtypeStruct((s, IN_WIDTH), F32), jax.ShapeDtypeStruct((s, D_MODEL), BF16)],
        in_specs=[ANY_SPACE, VMEM_WHOLE, VMEM_WHOLE],
        out_specs=[ANY_SPACE, ANY_SPACE],
        scratch_shapes=[pltpu.VMEM((depth, tm, D_MODEL), F32), pltpu.VMEM((depth, tm, IN_WIDTH), F32),
                        pltpu.VMEM((depth, tm, D_MODEL), BF16), pltpu.SemaphoreType.DMA((depth,)),
                        pltpu.SemaphoreType.DMA((depth,)), pltpu.SemaphoreType.DMA((depth,))],
        compiler_params=_params(),
    )(x, g, w_t)


def _chunk_masks():
    row = lax.broadcasted_iota(jnp.int32, (CHUNK, CHUNK), 0)
    col = lax.broadcasted_iota(jnp.int32, (CHUNK, CHUNK), 1)
    return row, col


def _ones_where(mask):
    return jnp.where(mask, 1.0, 0.0).astype(BF16)


def _hgrn_gates(zq, zf, lb):
    sq = _sigmoid(zq)
    sig = _sigmoid(zf)
    f = lb + (1.0 - lb) * sig
    return zq * sq, sq, sig, f


def _sub_chunk_masks(width):
    trow = lax.broadcasted_iota(jnp.int32, (CHUNK, width), 0)
    return [(trow >= SUB * j) & (trow < SUB * (j + 1)) for j in range(N_SUB)]


def _head(a, h):
    return a[:, HD * h:HD * (h + 1)]


def _lanes(parts):
    return jnp.concatenate(parts, axis=1)


def _hgrn_decay_factors(b_scr, r0, b, in_sub):
    bases = [jnp.zeros((1, HW), F32)] + [b_scr[r0 + SUB * j - 1:r0 + SUB * j, :] for j in range(1, N_SUB)]
    own_base = bases[N_SUB - 1]
    for j in range(N_SUB - 2, -1, -1):
        own_base = jnp.where(in_sub[j], bases[j], own_base)
    eq = jnp.exp(b - own_base)
    ek = []
    for j in range(N_SUB):
        upto = SUB * (j + 1)
        e = jnp.exp(jnp.minimum(bases[j] - b[0:upto], EXP_CAP))
        ek.append(e if upto == CHUNK else jnp.concatenate([e, jnp.zeros((CHUNK - upto, HW), F32)], axis=0))
    return eq, ek


def _per_sub_chunk(x, in_sub):
    return _lanes([jnp.where(in_sub[j], x, 0.0) for j in range(N_SUB)])


def _own_lane_block(a, in_sub):
    out = a[:, HD * (N_SUB - 1):HD * N_SUB]
    for j in range(N_SUB - 2, -1, -1):
        out = jnp.where(in_sub[j], a[:, HD * j:HD * (j + 1)], out)
    return out


def _head_rms(o):
    return _lanes([jnp.broadcast_to(_rms(_head(o, h)), (CHUNK, HD)) for h in range(HEADS)])


def _head_mean(a):
    return _lanes([jnp.broadcast_to(jnp.mean(_head(a, h), axis=-1, keepdims=True), (CHUNK, HD)) for h in range(HEADS)])


def _hgrn_fwd(z, lb_logits, gn):
    s = z.shape[0]
    n_chunks = s // CHUNK

    def body(zq_ref, zf_ref, zi_ref, zg_ref, lbl_ref, gn_ref, oa_ref, o_ref, st_ref, state, b_scr):
        @pl.when(pl.program_id(0) == 0)
        def _():
            state[...] = jnp.zeros_like(state)

        lb = _sigmoid(lbl_ref[0:1, :] - lbl_ref[1:2, :])
        row, col = _chunk_masks()
        causal = col <= row
        tri = _ones_where(causal)
        in_sub, in_sub_head = _sub_chunk_masks(HW), _sub_chunk_masks(HD)
        gn_row = _lanes([gn_ref[h:h + 1, :] for h in range(HEADS)])
        def front(c):
            r0 = CHUNK * c
            rs = slice(r0, r0 + CHUNK)
            q, _, _, f = _hgrn_gates(zq_ref[rs, :], zf_ref[rs, :], lb)
            kk = 1.0 - f
            b = _tri_dot(tri, jnp.log(f), 3)
            b_scr[rs, :] = b
            eq, ek = _hgrn_decay_factors(b_scr, r0, b, in_sub)
            b_last = b_scr[r0 + CHUNK - 1:r0 + CHUNK, :]
            qe = q * eq
            return {"rs": rs, "v": zi_ref[rs, :], "qg": q * jnp.exp(b), "kd": kk * jnp.exp(b_last - b),
                    "lam_last": jnp.exp(b_last),
                    "q16": [_per_sub_chunk(_head(qe, h), in_sub_head).astype(BF16) for h in range(HEADS)],
                    "ke16": [_lanes([_head(kk * e, h) for e in ek]).astype(BF16) for h in range(HEADS)]}

        def recurrence(c, p):
            st_ref[c] = state[...]
            a, o_inter = [], []
            for h in range(HEADS):
                vh, st = _head(p["v"], h), state[h]
                a.append(jnp.where(causal, _mm_nt(p["q16"][h], p["ke16"][h]), 0.0))
                o_inter.append(_mm_nt(_head(p["qg"], h), st))
                state[h] = st * _head(p["lam_last"], h) + _mm_tn(vh, _head(p["kd"], h))
            return _lanes([_mm(a[h], _head(p["v"], h)) + o_inter[h] for h in range(HEADS)])

        def back(p, o):
            rs = p["rs"]
            o_ref[rs, :] = o
            zg = zg_ref[rs, :]
            oa_ref[rs, :] = (o * _head_rms(o) * gn_row * zg * _sigmoid(zg)).astype(BF16)

        p = front(0)
        for c in range(CHUNKS_PER_STEP):
            o = recurrence(c, p)
            p_next = front(c + 1) if c + 1 < CHUNKS_PER_STEP else None
            back(p, o)
            p = p_next

    rows = CHUNK * CHUNKS_PER_STEP
    zspec = lambda cb: pl.BlockSpec((rows, HW), lambda i, cb=cb: (i, cb))
    return pl.pallas_call(
        body, name="hgrn_fwd", grid=(s // rows,),
        out_shape=[jax.ShapeDtypeStruct((s, 2 * HW), BF16), jax.ShapeDtypeStruct((s, HW), F32),
                   jax.ShapeDtypeStruct((n_chunks, HEADS, HD, HD), F32)],
        in_specs=[zspec(0), zspec(1), zspec(2), zspec(3), _full((2, HW)), _full((HEADS, HD))],
        out_specs=[pl.BlockSpec((rows, HW), lambda i: (i, 0)), pl.BlockSpec((rows, HW), lambda i: (i, 0)),
                   pl.BlockSpec((CHUNKS_PER_STEP, HEADS, HD, HD), lambda i: (i, 0, 0, 0))],
        scratch_shapes=[pltpu.VMEM((HEADS, HD, HD), F32), pltpu.VMEM((rows, HW), F32)],
        compiler_params=_params(("arbitrary",)),
    )(z, z, z, z, lb_logits, gn)


def _pool_counts(tile_idx, tm):
    t = tile_idx * tm + lax.broadcasted_iota(jnp.int32, (tm, 1), 0)
    return [1.0 / jnp.minimum(t + 1, w).astype(F32) for w in POOL_WINDOWS]


def _pool_fwd(z, w_pool, scale, mixed_in, deps):
    s = z.shape[0]
    tm = min(ROW_TILE, s)

    def body(p_ref, w_ref, sc_ref, mixin_ref, ob_ref, pooled_ref, ext):
        i = pl.program_id(0)

        @pl.when(i == 0)
        def _():
            ext[0:POOL_HALO, :] = jnp.zeros((POOL_HALO, HW), F32)

        @pl.when(i > 0)
        def _():
            ext[0:POOL_HALO, :] = ext[tm:tm + POOL_HALO, :]

        ext[POOL_HALO:POOL_HALO + tm, :] = p_ref[...]
        inv = _pool_counts(i, tm)
        for g, w in enumerate(POOL_WINDOWS):
            sl = slice(HD * g, HD * (g + 1))
            p = ext[POOL_HALO:POOL_HALO + tm, sl]
            win = p
            for d in range(1, w):
                win = win + ext[POOL_HALO - d:POOL_HALO - d + tm, sl]
            pooled = (win * inv[g] - p).astype(BF16)
            pooled_ref[:, sl] = pooled
            ob_ref[:, sl] = (_mm(pooled, w_ref[g]) * sc_ref[:, sl]).astype(BF16)

    return _call_behind(
        deps, body, name="pool_fwd", grid=(s // tm,),
        out_shape=[jax.ShapeDtypeStruct((s, 2 * HW), BF16), jax.ShapeDtypeStruct((s, HW), BF16)],
        in_specs=[pl.BlockSpec((tm, HW), lambda i: (i, 4)), _full((HEADS, HD, HD)), _full((1, HW)), ANY_SPACE],
        out_specs=[pl.BlockSpec((tm, HW), lambda i: (i, 1)), pl.BlockSpec((tm, HW), lambda i: (i, 0))],
        scratch_shapes=[pltpu.VMEM((tm + POOL_HALO, HW), F32)],
        input_output_aliases={3: 0},
        compiler_params=_params(("arbitrary",)),
    )(z, w_pool, scale, mixed_in)


def _mem_kv(mem, g, wk, wv, deps):
    def body(m_ref, g_ref, wk_ref, wv_ref, hm_ref, k_ref, v_ref):
        m = m_ref[...]
        hm = (m * _rms(m) * g_ref[...]).astype(BF16)
        hm_ref[...] = hm
        k_ref[...] = _mm(hm, wk_ref[...]).astype(BF16)
        v_ref[...] = _mm(hm, wv_ref[...]).astype(BF16)

    shp = jax.ShapeDtypeStruct((MEM_LEN, D_MODEL), BF16)
    return _call_behind(
        deps, body, name="mem_kv", out_shape=[shp, shp, shp],
        in_specs=[VMEM_WHOLE] * 4, out_specs=[VMEM_WHOLE] * 3,
        compiler_params=_params(),
    )(mem, g, wk, wv)


def _softmax_rows(sc):
    e = jnp.exp(sc - jnp.max(sc, axis=-1, keepdims=True))
    return e / jnp.sum(e, axis=-1, keepdims=True)


def _mix_xattn_fwd(x0, mixed, w_out, g, wq, xk, xv, wo_t, deps):
    s = x0.shape[0]
    tm = min(ROW_TILE, s)
    scale = XHD ** -0.5

    def body(x_ref, mix_ref, wout_ref, g_ref, wq_ref, k_ref, v_ref, wo_ref, x1_ref, o_ref, hq_ref, q_ref, att_ref):
        xv_ = x_ref[...] + _mm(mix_ref[...], wout_ref[...])
        x1_ref[...] = xv_
        hq = (xv_ * _rms(xv_) * g_ref[...]).astype(BF16)
        hq_ref[...] = hq
        q_ref[...] = (_mm(hq, wq_ref[...]) * scale).astype(BF16)
        heads = [slice(XHD * h, XHD * (h + 1)) for h in range(HEADS)]
        scores = [_mm_nt(q_ref[:, sl], k_ref[:, sl]) for sl in heads]
        probs = [_softmax_rows(sc) for sc in scores]
        for sl, p in zip(heads, probs):
            att_ref[:, sl] = _mm(p, v_ref[:, sl]).astype(BF16)
        o_ref[...] = xv_ + _mm_nt(att_ref[...], wo_ref[...])

    row_f32 = pl.BlockSpec((tm, D_MODEL), lambda i: (i, 0))
    bshape = jax.ShapeDtypeStruct((s, D_MODEL), BF16)
    fshape = jax.ShapeDtypeStruct((s, D_MODEL), F32)
    return _call_behind(
        deps, body, name="mix_xattn_fwd", grid=(s // tm,),
        out_shape=[fshape, fshape, bshape, bshape, bshape],
        in_specs=[row_f32, row_f32, VMEM_WHOLE, _full((1, D_MODEL)), VMEM_WHOLE, VMEM_WHOLE, VMEM_WHOLE, VMEM_WHOLE],
        out_specs=[row_f32] * 5,
        compiler_params=_params(("parallel",)),
    )(x0, mixed, w_out, g, wq, xk, xv, wo_t)


def _mlp_fwd_loss(x, g, w1, w2, gf, target):
    s = x.shape[0]
    tm = min(ROW_TILE, s)

    def body(x_ref, g_ref, w1_ref, w2_ref, gf_ref, t_ref, dx_ref, dx16_ref, u_ref, hf_ref, slot_ref):
        @pl.when(pl.program_id(0) == 0)
        def _():
            slot_ref[...] = jnp.zeros_like(slot_ref)

        xv = x_ref[...]
        hf = (xv * _rms(xv) * g_ref[...]).astype(BF16)
        hf_ref[...] = hf
        a_next = _mm(hf, w1_ref[0])
        for j in range(N_DEV):
            a = jnp.maximum(a_next, 0.0)
            if j + 1 < N_DEV:
                a_next = _mm(hf, w1_ref[j + 1])
            u_ref[:, FF_BLK * j:FF_BLK * (j + 1)] = (a * a).astype(BF16)
        acc = xv + _mm(u_ref[...], w2_ref[...])
        gfv = gf_ref[...]
        r = _rms(acc)
        n = acc * r
        err = n * gfv - t_ref[...]
        slot_ref[1:2, :] += jnp.sum(jnp.mean(err * err, axis=-1, keepdims=True), axis=0, keepdims=True) * 0.5
        dy = err * (1.0 / D_MODEL)
        slot_ref[0:1, :] += jnp.sum(dy * n, axis=0, keepdims=True)
        dn = dy * gfv
        dx = r * (dn - n * jnp.mean(dn * n, axis=-1, keepdims=True))
        dx_ref[...] = dx
        dx16_ref[...] = dx.astype(BF16)

    row_f32 = pl.BlockSpec((tm, D_MODEL), lambda i: (i, 0))
    return pl.pallas_call(
        body, name="mlp_fwd_loss", grid=(s // tm,),
        out_shape=[jax.ShapeDtypeStruct((s, D_MODEL), F32), jax.ShapeDtypeStruct((s, D_MODEL), BF16),
                   jax.ShapeDtypeStruct((s, D_FF), BF16), jax.ShapeDtypeStruct((s, D_MODEL), BF16),
                   jax.ShapeDtypeStruct((SLOT, D_MODEL), F32)],
        in_specs=[row_f32, _full((1, D_MODEL)), VMEM_WHOLE, VMEM_WHOLE, _full((1, D_MODEL)), row_f32],
        out_specs=[row_f32, row_f32, pl.BlockSpec((tm, D_FF), lambda i: (i, 0)), row_f32, _full((SLOT, D_MODEL))],
        compiler_params=_params(("arbitrary",)),
    )(x, g, w1, w2, gf, target)


def _zero_slot(slot_ref):
    @pl.when(pl.program_id(0) == 0)
    def _():
        slot_ref[...] = jnp.zeros_like(slot_ref)


def _mlp_bwd(dx3, u, x2, g, w1, w2, deps):
    s = x2.shape[0]
    tm = min(ROW_TILE // 2, s)

    def body(d_ref, u_ref, x_ref, g_ref, w1_ref, w2_ref, da_ref, dx_ref, slot_ref):
        _zero_slot(slot_ref)
        d = d_ref[...]
        d16 = d.astype(BF16)
        du_next = _mm_nt(d16, w2_ref[0])
        dhf = jnp.zeros((tm, D_MODEL), F32)
        for j in range(N_DEV):
            sl = slice(FF_BLK * j, FF_BLK * (j + 1))
            du = du_next
            if j + 1 < N_DEV:
                du_next = _mm_nt(d16, w2_ref[j + 1])
            u = u_ref[:, sl].astype(F32)
            da = (du * (2.0 * u * lax.rsqrt(jnp.maximum(u, TINY)))).astype(BF16)
            da_ref[:, sl] = da
            dhf = dhf + _mm_nt(da, w1_ref[j])
        dx, dg = _rms_bwd(x_ref[...], g_ref[...], dhf)
        dx_ref[...] = d + dx
        slot_ref[0:1, :] += dg

    row_f32 = pl.BlockSpec((tm, D_MODEL), lambda i: (i, 0))
    return _call_behind(
        deps, body, name="mlp_bwd", grid=(s // tm,),
        out_shape=[jax.ShapeDtypeStruct((s, D_FF), BF16), jax.ShapeDtypeStruct((s, D_MODEL), F32),
                   jax.ShapeDtypeStruct((SLOT, D_MODEL), F32)],
        in_specs=[row_f32, pl.BlockSpec((tm, D_FF), lambda i: (i, 0)), row_f32, _full((1, D_MODEL)),
                  VMEM_WHOLE, VMEM_WHOLE],
        out_specs=[pl.BlockSpec((tm, D_FF), lambda i: (i, 0)), row_f32, _full((SLOT, D_MODEL))],
        compiler_params=_params(("arbitrary",)),
    )(dx3, u, x2, g, w1, w2)


def _wgrad(a, b, name, col_blocks=False, update=None):
    s, m = a.shape
    n = b.shape[1]
    tm = 1280 if m % 1280 == 0 else min(1024, m)
    tn = min(1024, n)
    blk = n // N_DEV
    per_step = tn // blk if col_blocks else 1
    ts = min((4 if m * n >= D_MODEL * D_FF else 2) * ROW_TILE, s)
    n_s = s // ts
    grid = (m // tm, n // tn, n_s)

    def body(a_ref, b_ref, *rest):
        o_ref, acc = rest[-2], rest[-1]
        k = pl.program_id(2)

        @pl.when(k == 0)
        def _():
            acc[...] = jnp.zeros_like(acc)

        acc[...] += _mm_tn(a_ref[...], b_ref[...])
        if update is not None:
            r_ref, w_ref, m_ref, v_ref, g_ref, d_ref, nm_ref, nv_ref = rest[:8]
            g = r_ref[0].astype(F32)
            for d in range(1, N_DEV):
                g = g + r_ref[d].astype(F32)
            g_ref[...] = g
            d_ref[...], nm_ref[...], nv_ref[...] = _adam_update(g, w_ref[...], m_ref[...], v_ref[...])

        @pl.when(k == n_s - 1)
        def _():
            if col_blocks:
                for p in range(per_step):
                    o_ref[p] = acc[:, blk * p:blk * (p + 1)].astype(BF16)
            else:
                o_ref[...] = acc[...].astype(BF16)

    if col_blocks:
        out_shape = jax.ShapeDtypeStruct((N_DEV, m, blk), BF16)
        out_spec = pl.BlockSpec((per_step, tm, blk), lambda i, j, k: (j, i, 0))
    else:
        out_shape = jax.ShapeDtypeStruct((m, n), BF16)
        out_spec = pl.BlockSpec((tm, tn), lambda i, j, k: (i, j))
    in_specs = [pl.BlockSpec((ts, tm), lambda i, j, k: (k, i)), pl.BlockSpec((ts, tn), lambda i, j, k: (k, j))]
    out_shapes, out_specs, operands = [out_shape], [out_spec], [a, b]
    if update is not None:
        rows, cols = update[1].shape
        steps = grid[0] * grid[1] * grid[2]
        tr = rows // steps
        step = lambda i, j, k: (i * grid[1] + j) * grid[2] + k
        piece = pl.BlockSpec((tr, cols), lambda i, j, k: (step(i, j, k), 0))
        in_specs += [pl.BlockSpec((N_DEV, tr, cols), lambda i, j, k: (0, step(i, j, k), 0)), piece, piece, piece]
        out_shapes = [jax.ShapeDtypeStruct((rows, cols), F32)] * 4 + out_shapes
        out_specs = [piece] * 4 + out_specs
        operands += list(update)
    outs = pl.pallas_call(
        body, name=name, grid=grid, out_shape=out_shapes, in_specs=in_specs, out_specs=out_specs,
        scratch_shapes=[pltpu.VMEM((tm, tn), F32)],
        compiler_params=_params(("parallel", "parallel", "arbitrary")),
    )(*operands)
    return outs[0] if update is None else (outs[4], tuple(outs[:4]))


def _xattn_bwd(dx2, x1, g, q, xk, xv, wq, wo_t, deps):
    s = x1.shape[0]
    tm = min(ROW_TILE, s)
    scale = XHD ** -0.5

    def body(d_ref, x_ref, g_ref, q_ref, k_ref, v_ref, wq_ref, wo_ref, dx_ref, dx16_ref, dq_ref, dk_ref, dv_ref, slot_ref,
             datt):
        _zero_slot(slot_ref)

        @pl.when(pl.program_id(0) == 0)
        def _():
            dk_ref[...] = jnp.zeros_like(dk_ref)
            dv_ref[...] = jnp.zeros_like(dv_ref)

        d = d_ref[...]
        datt[...] = _mm(d, wo_ref[...]).astype(BF16)
        heads = [slice(XHD * h, XHD * (h + 1)) for h in range(HEADS)]
        scores = [_mm_nt(q_ref[:, sl], k_ref[:, sl]) for sl in heads]
        dps = [_mm_nt(datt[:, sl], v_ref[:, sl]) for sl in heads]
        probs = [_softmax_rows(sc) for sc in scores]
        dss = [(p * (dp - jnp.sum(dp * p, axis=-1, keepdims=True))).astype(BF16) for p, dp in zip(probs, dps)]
        for sl, p, ds in zip(heads, probs, dss):
            dq_ref[:, sl] = (_mm(ds, k_ref[:, sl]) * scale).astype(BF16)
            dk_ref[:, sl] += _mm_tn(ds, q_ref[:, sl])
            dv_ref[:, sl] += _mm_tn(p, datt[:, sl])
        dx, dg = _rms_bwd(x_ref[...], g_ref[...], _mm_nt(dq_ref[...], wq_ref[...]))
        dx_ref[...] = d + dx
        dx16_ref[...] = (d + dx).astype(BF16)
        slot_ref[0:1, :] += dg

    row_f32 = pl.BlockSpec((tm, D_MODEL), lambda i: (i, 0))
    kv = jax.ShapeDtypeStruct((MEM_LEN, D_MODEL), F32)
    tokens16 = jax.ShapeDtypeStruct((s, D_MODEL), BF16)
    return _call_behind(
        deps, body, name="xattn_bwd", grid=(s // tm,),
        out_shape=[jax.ShapeDtypeStruct((s, D_MODEL), F32), tokens16, tokens16, kv, kv,
                   jax.ShapeDtypeStruct((SLOT, D_MODEL), F32)],
        in_specs=[row_f32, row_f32, _full((1, D_MODEL)), row_f32, VMEM_WHOLE, VMEM_WHOLE, VMEM_WHOLE, VMEM_WHOLE],
        out_specs=[row_f32, row_f32, row_f32, _full((MEM_LEN, D_MODEL)), _full((MEM_LEN, D_MODEL)),
                   _full((SLOT, D_MODEL))],
        scratch_shapes=[pltpu.VMEM((tm, D_MODEL), BF16)],
        compiler_params=_params(("arbitrary",)),
    )(dx2, x1, g, q, xk, xv, wq, wo_t)


def _mem_bwd(mem, g, hm, dxk, dxv, wk, wv):
    def body(m_ref, g_ref, hm_ref, dk_ref, dv_ref, wk_ref, wv_ref, dwk_ref, dwv_ref, slot_ref):
        dk, dv = dk_ref[...], dv_ref[...]
        hm_ = hm_ref[...]
        dwk_ref[...] = _mm_tn(hm_, dk).astype(BF16)
        dwv_ref[...] = _mm_tn(hm_, dv).astype(BF16)
        _, dg = _rms_bwd(m_ref[...], g_ref[...], _mm_nt(dk, wk_ref[...]) + _mm_nt(dv, wv_ref[...]))
        slot_ref[...] = jnp.zeros_like(slot_ref)
        slot_ref[0:1, :] = dg

    wshape = jax.ShapeDtypeStruct((D_MODEL, D_MODEL), BF16)
    return pl.pallas_call(
        body, name="mem_bwd", out_shape=[wshape, wshape, jax.ShapeDtypeStruct((SLOT, D_MODEL), F32)],
        in_specs=[VMEM_WHOLE] * 7, out_specs=[VMEM_WHOLE] * 3,
        compiler_params=_params(),
    )(mem, g, hm, dxk, dxv, wk, wv)


def _pool_bwd(dx1, w_out, pooled, w_pool, scale, deps):
    s = dx1.shape[0]
    tm = min(ROW_TILE, s)
    n_t = s // tm

    def body(dx_ref, wo_ref, pl_ref, w_ref, sc_ref, dz_ref, dw_ref, slot_ref, ext, do_ref):
        i = pl.program_id(0)
        tile = n_t - 1 - i
        _zero_slot(slot_ref)
        do_ref[...] = _mm_nt(dx_ref[...], wo_ref[HW:2 * HW, :])

        @pl.when(i == 0)
        def _():
            dw_ref[...] = jnp.zeros_like(dw_ref)
            ext[tm:tm + POOL_HALO, :] = jnp.zeros((POOL_HALO, HW), F32)

        @pl.when(i > 0)
        def _():
            ext[tm:tm + POOL_HALO, :] = ext[0:POOL_HALO, :]

        inv = _pool_counts(tile, tm)
        dpooled = []
        for g in range(HEADS):
            sl = slice(HD * g, HD * (g + 1))
            pooled_g = pl_ref[:, sl]
            do = do_ref[:, sl]
            slot_ref[0:1, sl] += jnp.sum(_mm(pooled_g, w_ref[g]) * do, axis=0, keepdims=True)
            dy = (do * sc_ref[:, sl]).astype(BF16)
            dw_ref[g] += _mm_tn(pooled_g, dy)
            dpo = _mm_nt(dy, w_ref[g])
            dpooled.append(dpo)
            ext[0:tm, sl] = dpo * inv[g]
        for g, w in enumerate(POOL_WINDOWS):
            sl = slice(HD * g, HD * (g + 1))
            win = ext[0:tm, sl]
            for d in range(1, w):
                win = win + ext[d:d + tm, sl]
            dz_ref[:, sl] = (win - dpooled[g]).astype(BF16)

    return _call_behind(
        deps, body, name="pool_bwd", grid=(n_t,),
        out_shape=[jax.ShapeDtypeStruct((s, IN_WIDTH), BF16), jax.ShapeDtypeStruct((HEADS, HD, HD), F32),
                   jax.ShapeDtypeStruct((SLOT, D_MODEL), F32)],
        in_specs=[pl.BlockSpec((tm, D_MODEL), lambda i: (n_t - 1 - i, 0)), VMEM_WHOLE,
                  pl.BlockSpec((tm, HW), lambda i: (n_t - 1 - i, 0)), _full((HEADS, HD, HD)), _full((1, HW))],
        out_specs=[pl.BlockSpec((tm, HW), lambda i: (n_t - 1 - i, 4)), _full((HEADS, HD, HD)), _full((SLOT, D_MODEL))],
        scratch_shapes=[pltpu.VMEM((tm + POOL_HALO, HW), F32), pltpu.VMEM((tm, HW), F32)],
        compiler_params=_params(("arbitrary",)),
    )(dx1, w_out, pooled, w_pool, scale)


def _hgrn_bwd(z, o, dx1, w_out, states, lb_logits, gn, dz_in, deps):
    s = z.shape[0]
    n_chunks = s // CHUNK

    def body(zq_ref, zf_ref, zi_ref, zg_ref, o_ref, dx_ref, wo_ref, st_ref, lbl_ref, gn_ref, dzin_ref,
             dz_ref, dlb_ref, dgn_ref, dstate, b_scr, dlb_acc, do_ref):
        i = pl.program_id(0)

        @pl.when(i == 0)
        def _():
            dstate[...] = jnp.zeros_like(dstate)
            dlb_acc[...] = jnp.zeros_like(dlb_acc)
            dgn_ref[...] = jnp.zeros_like(dgn_ref)
            dlb_ref[...] = jnp.zeros_like(dlb_ref)

        do_ref[...] = _mm_nt(dx_ref[...], wo_ref[0:HW, :])
        lb = _sigmoid(lbl_ref[0:1, :] - lbl_ref[1:2, :])
        row, col = _chunk_masks()
        causal = col <= row
        tri = _ones_where(causal)
        upper = _ones_where(col >= row)
        strict_lower = _ones_where(col < row)
        in_sub, in_sub_head = _sub_chunk_masks(HW), _sub_chunk_masks(HD)
        gn_row = _lanes([gn_ref[h:h + 1, :] for h in range(HEADS)])
        sums = {"dlb": 0.0, "dgn": 0.0}

        def front(c):
            r0 = CHUNK * c
            rs = slice(r0, r0 + CHUNK)
            p = {"rs": rs}
            p["zq"] = zq_ref[rs, :]
            p["q"], p["sq"], p["sig"], p["f"] = _hgrn_gates(p["zq"], zf_ref[rs, :], lb)
            p["kk"] = 1.0 - p["f"]
            b = _tri_dot(tri, jnp.log(p["f"]), 3)
            b_scr[rs, :] = b
            p["v"] = zi_ref[rs, :]
            o, zg, doa = o_ref[rs, :], zg_ref[rs, :], do_ref[rs, :]
            sg = _sigmoid(zg)
            rms = _head_rms(o)
            n = o * rms
            don = doa * (zg * sg)
            sums["dgn"] = sums["dgn"] + jnp.sum(don * n, axis=0, keepdims=True)
            dn = don * gn_row
            p["d_o"] = rms * (dn - n * _head_mean(dn * n))
            dz_ref[rs, 3 * HW:4 * HW] = (doa * (n * gn_row) * (sg * (1.0 + zg * (1.0 - sg)))).astype(BF16)
            p["eq"], p["ek"] = _hgrn_decay_factors(b_scr, r0, b, in_sub)
            b_last = b_scr[r0 + CHUNK - 1:r0 + CHUNK, :]
            p["lam"], p["e_last"], p["lam_last"] = jnp.exp(b), jnp.exp(b_last - b), jnp.exp(b_last)
            p["qe"], p["qg"], p["kd"] = p["q"] * p["eq"], p["q"] * p["lam"], p["kk"] * p["e_last"]
            p["ke"] = [p["kk"] * e for e in p["ek"]]
            p["q16"] = [_per_sub_chunk(_head(p["qe"], h), in_sub_head).astype(BF16) for h in range(HEADS)]
            p["ke16"] = [_lanes([_head(p["ke"][j], h) for j in range(N_SUB)]).astype(BF16) for h in range(HEADS)]
            return p

        def recurrence(c, p):
            m = {k: [] for k in ("dv", "gq", "gk", "dqi", "dkd", "st")}
            a, da, dv_state = [], [], []
            for h in range(HEADS):
                vh, doh = _head(p["v"], h), _head(p["d_o"], h)
                st0, ds1 = st_ref[c, h], dstate[h]
                a.append(jnp.where(causal, _mm_nt(p["q16"][h], p["ke16"][h]), 0.0))
                da.append(jnp.where(causal, _mm_nt(doh, vh), 0.0))
                dv_state.append(_mm_nt(_head(p["kd"], h), ds1))
                m["dqi"].append(_mm(doh, st0))
                m["dkd"].append(_mm(vh, ds1))
                m["st"].append(jnp.sum(st0 * ds1, axis=0, keepdims=True))
                dstate[h] = ds1 * _head(p["lam_last"], h) + _mm_tn(doh, _head(p["qg"], h))
            for h in range(HEADS):
                m["dv"].append(_mm_tn(a[h], _head(p["d_o"], h)) + dv_state[h])
                m["gq"].append(_own_lane_block(_mm(da[h], p["ke16"][h]), in_sub_head))
                m["gk"].append(_mm_tn(da[h], p["q16"][h]))
            return m

        def back(p, m):
            rs = p["rs"]
            dz_ref[rs, 2 * HW:3 * HW] = _lanes(m["dv"]).astype(BF16)
            gq = _lanes(m["gq"])
            gk = [_lanes([m["gk"][h][:, HD * j:HD * (j + 1)] for h in range(HEADS)]) for j in range(N_SUB)]
            dq_inter = p["lam"] * _lanes(m["dqi"])
            dq = p["eq"] * gq + dq_inter
            dk_intra = sum(p["ek"][j] * gk[j] for j in range(N_SUB))
            dk_state = _lanes(m["dkd"]) * p["e_last"]
            db_intra = (p["qe"].astype(BF16).astype(F32) * gq
                        - sum(p["ke"][j].astype(BF16).astype(F32) * gk[j] for j in range(N_SUB)))
            dlf = (_tri_dot(upper, db_intra + p["q"] * dq_inter, 2) + _tri_dot(strict_lower, p["kk"] * dk_state, 2)
                   + p["lam_last"] * _lanes(m["st"]))
            sig, sq, zq = p["sig"], p["sq"], p["zq"]
            df = dlf / p["f"] - (dk_intra + dk_state)
            sums["dlb"] = sums["dlb"] + jnp.sum(df * (1.0 - sig), axis=0, keepdims=True)
            dz_ref[rs, HW:2 * HW] = (df * (1.0 - lb) * sig * (1.0 - sig)).astype(BF16)
            dz_ref[rs, 0:HW] = (dq * (sq * (1.0 + zq * (1.0 - sq)))).astype(BF16)

        p = front(CHUNKS_PER_STEP - 1)
        for c in reversed(range(CHUNKS_PER_STEP)):
            m = recurrence(c, p)
            p_next = front(c - 1) if c > 0 else None
            back(p, m)
            p = p_next
        dlb_acc[...] += sums["dlb"]
        for h in range(HEADS):
            dgn_ref[h:h + 1, 0:HD] += _head(sums["dgn"], h)

        @pl.when(i == n_steps - 1)
        def _():
            dl0 = dlb_acc[...] * lb * (1.0 - lb)
            dlb_ref[0:1, 0:HW] = dl0
            dlb_ref[1:2, 0:HW] = -dl0

    rows = CHUNK * CHUNKS_PER_STEP
    n_steps = s // rows
    rev = lambda i: n_steps - 1 - i
    zspec = lambda cb: pl.BlockSpec((rows, HW), lambda i, cb=cb: (rev(i), cb))
    slot = jax.ShapeDtypeStruct((SLOT, D_MODEL), F32)
    return _call_behind(
        deps, body, name="hgrn_bwd", grid=(n_steps,),
        out_shape=[jax.ShapeDtypeStruct((s, IN_WIDTH), BF16), slot, slot],
        in_specs=[zspec(0), zspec(1), zspec(2), zspec(3), pl.BlockSpec((rows, HW), lambda i: (rev(i), 0)),
                  pl.BlockSpec((rows, D_MODEL), lambda i: (rev(i), 0)), VMEM_WHOLE,
                  pl.BlockSpec((CHUNKS_PER_STEP, HEADS, HD, HD), lambda i: (rev(i), 0, 0, 0)), _full((2, HW)),
                  _full((HEADS, HD)), ANY_SPACE],
        out_specs=[pl.BlockSpec((rows, 4 * HW), lambda i: (rev(i), 0)), _full((SLOT, D_MODEL)), _full((SLOT, D_MODEL))],
        scratch_shapes=[pltpu.VMEM((HEADS, HD, HD), F32), pltpu.VMEM((rows, HW), F32), pltpu.VMEM((1, HW), F32),
                        pltpu.VMEM((rows, HW), F32)],
        input_output_aliases={10: 0},
        compiler_params=_params(("arbitrary",)),
    )(z, z, z, z, o, dx1, w_out, states, lb_logits, gn, dz_in)


def _in_bwd(dz, w_t, x0, g, dx1, deps):
    s = x0.shape[0]
    tm = min(WIDE_ROW_TILE, s)

    def body(dz_ref, w_ref, x_ref, g_ref, d_ref, dx_ref, slot_ref):
        _zero_slot(slot_ref)
        dx, dg = _rms_bwd(x_ref[...], g_ref[...], _mm(dz_ref[...], w_ref[...]))
        dx_ref[...] = d_ref[...] + dx
        slot_ref[0:1, :] += dg

    row_f32 = pl.BlockSpec((tm, D_MODEL), lambda i: (i, 0))
    return _call_behind(
        deps, body, name="in_bwd", grid=(s // tm,),
        out_shape=[jax.ShapeDtypeStruct((s, D_MODEL), F32), jax.ShapeDtypeStruct((SLOT, D_MODEL), F32)],
        in_specs=[pl.BlockSpec((tm, IN_WIDTH), lambda i: (i, 0)), VMEM_WHOLE, row_f32, _full((1, D_MODEL)), row_f32],
        out_specs=[row_f32, _full((SLOT, D_MODEL))],
        compiler_params=_params(("arbitrary",)),
    )(dz, w_t, x0, g, dx1)


def kernel(x, mem, norm_mix_g, w_in, lb_logits, hgrn_norm_g, w_pool, pool_scale, w_out, norm_x_g, norm_mem_g, w_xq, w_xk, w_xv, w_xo, norm_ffn_g, w_ff1, w_ff2, final_norm_g, loss_target, m_norm_mix_g, m_w_in, m_lb_logits, m_hgrn_norm_g, m_w_pool, m_pool_scale, m_w_out, m_norm_x_g, m_norm_mem_g, m_w_xq, m_w_xk, m_w_xv, m_w_xo, m_norm_ffn_g, m_w_ff1, m_w_ff2, m_final_norm_g, v_norm_mix_g, v_w_in, v_lb_logits, v_hgrn_norm_g, v_w_pool, v_pool_scale, v_w_out, v_norm_x_g, v_norm_mem_g, v_w_xq, v_w_xk, v_w_xv, v_w_xo, v_norm_ffn_g, v_w_ff1, v_w_ff2, v_final_norm_g):
    x0 = x[0]
    mem0 = mem[0]
    tgt = loss_target[0]
    gn = hgrn_norm_g[0]
    gfin = final_norm_g.reshape(1, D_MODEL)
    wp = w_pool[0]
    heads_2d = lambda w: w.reshape(D_MODEL // N_DEV, D_MODEL)
    xo_2d = lambda w: w.reshape(D_MODEL, D_MODEL // N_DEV)

    first = _all_gather_weights([w_in[0].T], [w_out[0], heads_2d(w_xq), heads_2d(w_xk), heads_2d(w_xv), xo_2d(w_xo).T,
                                              w_ff1[0], w_ff2[0]])
    win_t = first[0].reshape(IN_WIDTH, D_MODEL)
    ga_attn, ga_mlp = _gather_first_start([first[1:6], first[6:8]], "gather_first_start")

    z, h = _in_proj(x0, norm_mix_g, win_t, deps=[ga_attn[3]])
    mixed_a, o_pre, states = _hgrn_fwd(z, lb_logits, gn)
    lands = _split_wait(_gather_first_copies, ga_attn, o_pre, "gather_attn_first_wait")
    gb_attn = _gather_forward_start(lands, "gather_attn_forward_start")
    mixed, pooled = _pool_fwd(z, wp, pool_scale, mixed_a, deps=[gb_attn[3]])
    lands = _split_wait(_gather_forward_copies, gb_attn, pooled, "gather_attn_forward_wait")
    wout_f, wq_f, wk_f, wv_f, wo_t = (t.reshape(D_MODEL, D_MODEL) for t in lands)
    hm, xk, xv = _mem_kv(mem0, norm_mem_g, wk_f, wv_f, deps=[])
    x1, x2, hq, xq, att = _mix_xattn_fwd(x0, mixed, wout_f, norm_x_g, wq_f, xk, xv, wo_t, deps=[])
    lands = _split_wait(_gather_first_copies, ga_mlp, x2, "gather_mlp_first_wait")
    gb_mlp = _gather_forward_start(lands, "gather_mlp_forward_start")
    w1_b, w2_b = _split_wait(_gather_forward_copies, gb_mlp, gb_mlp[3], "gather_mlp_forward_wait")
    dx3, dx3_16, u, hf, slot_fin = _mlp_fwd_loss(x2, norm_ffn_g, w1_b, w2_b.reshape(D_FF, D_MODEL), gfin, tgt)

    rows = lambda t, r: t.reshape(N_DEV, r, D_MODEL)
    dw2 = _wgrad(u, dx3_16, "wgrad_ff2")
    ex_ff2 = _all_to_all_start([rows(dw2, FF_BLK)], [], "exchange_ff2_start")
    da, dx2, slot_ffn = _mlp_bwd(dx3, u, x2, norm_ffn_g, w1_b, w2_b, deps=[ex_ff2[3]])
    dw1 = _wgrad(hf, da, "wgrad_ff1", col_blocks=True)
    ex_ff1 = _all_to_all_start([dw1], [], "exchange_ff1_start")
    dx1, dx1_16, dxq, dxk, dxv, slot_x = _xattn_bwd(dx2, x1, norm_x_g, xq, xk, xv, wq_f, wo_t, deps=[ex_ff1[3]])
    dwo_t = _wgrad(dx2, att, "wgrad_xo")
    dwq = _wgrad(hq, dxq, "wgrad_xq")
    dwk, dwv, slot_mem = _mem_bwd(mem0, norm_mem_g, hm, dxk, dxv, wk_f, wv_f)
    ex_attn = _all_to_all_start([rows(dwq, 128), rows(dwk, 128), rows(dwv, 128), rows(dwo_t, 128)], [],
                                "exchange_attn_start")
    dwout = _wgrad(mixed, dx1_16, "wgrad_out")
    dz_pool, d_wpool, slot_ps = _pool_bwd(dx1_16, wout_f, pooled, wp, pool_scale, deps=[ex_attn[3]])
    small0 = jnp.concatenate([slot_x, slot_mem, slot_ffn, slot_fin, slot_ps], axis=0)
    ex_out = _all_to_all_start([rows(dwout, 128)], [small0, d_wpool], "exchange_out_start")
    dz, slot_lb, slot_gn = _hgrn_bwd(z, o_pre, dx1_16, wout_f, states, lb_logits, gn, dz_pool, deps=[ex_out[3]])
    (r_2,) = _split_wait(_all_to_all_copies(1), ex_ff2, dz, "exchange_ff2_wait")
    dwin_t, ff2_update = _wgrad(dz, h, "wgrad_in", update=(r_2, w_ff2[0], m_w_ff2[0], v_w_ff2[0]))
    ex_in = _all_to_all_start([rows(dwin_t, 320)], [], "exchange_in_start")
    grad_x, slot_mix = _in_bwd(dz, win_t, x0, norm_mix_g, dx1, deps=[ex_in[3]])
    small1 = jnp.concatenate([slot_mix, slot_lb, slot_gn], axis=0)
    ex_mix = _all_to_all_start([], [small1], "exchange_mix_start")

    out = {}
    out["w_ff2"] = ff2_update
    (r_1,) = _split_wait(_all_to_all_copies(1), ex_ff1, ex_mix[3], "exchange_ff1_wait")
    out["w_ff1"] = _sum_adamw(r_1, w_ff1[0], m_w_ff1[0], v_w_ff1[0], "adamw_ff1")
    r_q, r_k, r_v, r_o = _split_wait(_all_to_all_copies(4), ex_attn, out["w_ff1"][1], "exchange_attn_wait")
    sums = _sum_sources_whole([r_q, r_k, r_v, r_o], "sum_grad_attn")
    g_attn = [g.reshape(w_xq.shape) for g in sums[:3]] + [sums[3].T]
    attn = _adamw_whole([(g_attn[0], w_xq, m_w_xq, v_w_xq), (g_attn[1], w_xk, m_w_xk, v_w_xk),
                         (g_attn[2], w_xv, m_w_xv, v_w_xv),
                         (g_attn[3], xo_2d(w_xo), xo_2d(m_w_xo), xo_2d(v_w_xo))], "adamw_attn")
    for n, g, res in zip(("w_xq", "w_xk", "w_xv", "w_xo"), g_attn, attn):
        out[n] = (g, *res)
    r_out, r_small0, r_wpool = _split_wait(_all_to_all_copies(1), ex_out, attn[3][0], "exchange_out_wait")
    out["w_out"] = _sum_adamw(r_out, w_out[0], m_w_out[0], v_w_out[0], "adamw_out")
    (r_in,) = _split_wait(_all_to_all_copies(1), ex_in, out["w_out"][1], "exchange_in_wait")
    in_t = _sum_adamw(r_in, w_in[0].T, m_w_in[0].T, v_w_in[0].T, "adamw_in")
    out["w_in"] = tuple(t.T for t in in_t)
    (r_small1,) = _split_wait(_all_to_all_copies(0), ex_mix, in_t[1], "exchange_mix_wait")
    row = lambda t: t.reshape(1, -1)
    small_params = {
        "norm_mix_g": (norm_mix_g, m_norm_mix_g, v_norm_mix_g),
        "lb_logits": (lb_logits, m_lb_logits, v_lb_logits),
        "hgrn_norm_g": (hgrn_norm_g[0], m_hgrn_norm_g[0], v_hgrn_norm_g[0]),
        "pool_scale": (pool_scale, m_pool_scale, v_pool_scale),
        "norm_x_g": (norm_x_g, m_norm_x_g, v_norm_x_g),
        "norm_mem_g": (norm_mem_g, m_norm_mem_g, v_norm_mem_g),
        "norm_ffn_g": (norm_ffn_g, m_norm_ffn_g, v_norm_ffn_g),
        "final_norm_g": (row(final_norm_g), row(m_final_norm_g), row(v_final_norm_g)),
        "w_pool": (wp, m_w_pool[0], v_w_pool[0]),
    }
    loss, small_out = _small_update([r_small0, r_small1], r_wpool, small_params)
    out.update(small_out)

    shapes = dict(norm_mix_g=norm_mix_g, w_in=w_in, lb_logits=lb_logits, hgrn_norm_g=hgrn_norm_g, w_pool=w_pool,
                  pool_scale=pool_scale, w_out=w_out, norm_x_g=norm_x_g, norm_mem_g=norm_mem_g, w_xq=w_xq, w_xk=w_xk,
                  w_xv=w_xv, w_xo=w_xo, norm_ffn_g=norm_ffn_g, w_ff1=w_ff1, w_ff2=w_ff2, final_norm_g=final_norm_g)
    order = list(shapes)
    group = lambda k: [out[n][k].reshape(shapes[n].shape) for n in order]
    return (loss.reshape(()), grad_x.reshape(x.shape), *group(0), *group(1), *group(2), *group(3))
```

```python
import jax
import jax.numpy as jnp
from jax import lax
from jax.experimental import pallas as pl
from jax.experimental.pallas import tpu as pltpu

F32 = jnp.float32
BF16 = jnp.bfloat16

D_MODEL = 1024
N_DEV = 8
HEADS = 4
HD = 128
HW = HEADS * HD
IN_WIDTH = 5 * HW
XHD = 256
MEM_LEN = 256
D_FF = 4096
FF_BLK = D_FF // N_DEV
POOL_WINDOWS = (2, 4, 8, 16)
POOL_HALO = 16
CHUNK = 64
CHUNKS_PER_STEP = 8
SUB = 16
N_SUB = CHUNK // SUB
EXP_CAP = 80.0
EPS = 1e-6
TINY = 1e-30
ROW_TILE = 512
WIDE_ROW_TILE = 1024
SLOT = 8
V7X_VMEM_LIMIT = 56 * 1024 * 1024

ADAM_LR = 0.001
ADAM_B1 = 0.9
ADAM_B2 = 0.999
ADAM_EPS = 1e-08
ADAM_WD = 0.01
ADAM_STEP = 10

MESH_ID = pl.DeviceIdType.MESH


def _params(sem=None, vmem=V7X_VMEM_LIMIT):
    return pltpu.CompilerParams(dimension_semantics=sem, vmem_limit_bytes=vmem)


def _mm(a, b):
    return lax.dot_general(a.astype(BF16), b.astype(BF16), (((1,), (0,)), ((), ())), preferred_element_type=F32)


def _mm_nt(a, b):
    return lax.dot_general(a.astype(BF16), b.astype(BF16), (((1,), (1,)), ((), ())), preferred_element_type=F32)


def _mm_tn(a, b):
    return lax.dot_general(a.astype(BF16), b.astype(BF16), (((0,), (0,)), ((), ())), preferred_element_type=F32)


def _sigmoid(x):
    return 1.0 / (1.0 + jnp.exp(-x))


def _rms(x):
    return lax.rsqrt(jnp.mean(x * x, axis=-1, keepdims=True) + EPS)


def _rms_bwd(x, g, dh):
    r = _rms(x)
    n = x * r
    dn = dh * g
    dx = r * (dn - n * jnp.mean(dn * n, axis=-1, keepdims=True))
    return dx, jnp.sum(dh * n, axis=0, keepdims=True)


def _tri_dot(tri, x, passes):
    acc = None
    rest = x
    for _ in range(passes):
        piece = rest.astype(BF16)
        part = lax.dot_general(tri, piece, (((1,), (0,)), ((), ())), preferred_element_type=F32)
        acc = part if acc is None else acc + part
        rest = rest - piece.astype(F32)
    return acc


def _adam_update(g, w, m, v):
    nm = ADAM_B1 * m + (1.0 - ADAM_B1) * g
    nv = ADAM_B2 * v + (1.0 - ADAM_B2) * (g * g)
    m_hat = nm / (1.0 - ADAM_B1 ** ADAM_STEP)
    v_hat = nv / (1.0 - ADAM_B2 ** ADAM_STEP)
    return -ADAM_LR * (m_hat / (jnp.sqrt(v_hat) + ADAM_EPS) + ADAM_WD * w), nm, nv


def _full(shape):
    return pl.BlockSpec(shape, lambda *_: (0,) * len(shape))


VMEM_WHOLE = pl.BlockSpec(memory_space=pltpu.VMEM)
ANY_SPACE = pl.BlockSpec(memory_space=pl.ANY)


def _mesh_pos():
    return lax.axis_index("x"), lax.axis_index("y"), lax.axis_index("c")


def _flat(px, py, pc):
    return 4 * px + 2 * py + pc


def _all_gather_weights(shards, cast_only):
    n, nc = len(shards), len(cast_only)
    step = 64

    def body(*refs):
        x_refs, c_refs = refs[:n], refs[n:n + nc]
        out_refs, cast_refs = refs[n + nc:2 * n + nc], refs[2 * n + nc:2 * n + 2 * nc]
        bufs = refs[2 * n + 2 * nc:3 * n + 2 * nc]
        send_sems, recv_sems, local_sems = refs[3 * n + 2 * nc:]
        _handshake(_peers_first_level())
        x, y, c = _mesh_pos()
        me, sibling = (x, y, c), (x, y, 1 - c)
        chips = [(1 - x, y), (x, 1 - y), (1 - x, 1 - y)]

        def copy(a, k, blk, to, src=None):
            rows = out_refs[a].at[_flat(*blk)]
            return pltpu.make_async_remote_copy(
                src_ref=rows if src is None else src, dst_ref=rows,
                send_sem=send_sems.at[7 * a + k], recv_sem=recv_sems.at[7 * a + k], device_id=to, device_id_type=MESH_ID)

        def cast_rows(src, dst, rows):
            def cast(i, carry):
                r0 = pl.multiple_of(i * step, step)
                dst[pl.ds(r0, step), :] = src[pl.ds(r0, step), :].astype(BF16)
                return carry
            lax.fori_loop(0, rows // step, cast, 0)

        first, mine = [], []
        for a in range(n):
            cast_rows(x_refs[a], bufs[a], shards[a].shape[0])
            mine.append(pltpu.make_async_copy(bufs[a], out_refs[a].at[_flat(*me)], local_sems.at[a]))
            first.append(copy(a, 0, me, sibling, src=bufs[a]))
            first += [copy(a, 1 + j, me, (*chip, c), src=bufs[a]) for j, chip in enumerate(chips)]
            for cp in [mine[-1]] + first[-4:]:
                cp.start()
        for a in range(nc):
            cast_rows(c_refs[a], cast_refs[a], cast_only[a].shape[0])
        passed = []
        for j, chip in enumerate(chips):
            for a in range(n):
                copy(a, 1 + j, (*chip, c), me).wait_recv()
                passed.append(copy(a, 4 + j, (*chip, c), sibling))
                passed[-1].start()
        for a in range(n):
            copy(a, 0, sibling, me).wait_recv()
            for j, chip in enumerate(chips):
                copy(a, 4 + j, (*chip, 1 - c), me).wait_recv()
        for cp in first + passed:
            cp.wait_send()
        for cp in mine:
            cp.wait()

    return pl.pallas_call(
        body, name="all_gather_w_in",
        out_shape=[jax.ShapeDtypeStruct((N_DEV,) + s.shape, BF16) for s in shards]
        + [jax.ShapeDtypeStruct(s.shape, BF16) for s in cast_only],
        in_specs=[VMEM_WHOLE] * (n + nc), out_specs=[ANY_SPACE] * n + [VMEM_WHOLE] * nc,
        scratch_shapes=[pltpu.VMEM(s.shape, BF16) for s in shards]
        + [pltpu.SemaphoreType.DMA((7 * n,)), pltpu.SemaphoreType.DMA((7 * n,)), pltpu.SemaphoreType.DMA((n,))],
        compiler_params=pltpu.CompilerParams(vmem_limit_bytes=V7X_VMEM_LIMIT, collective_id=GATHER_W_IN_ID),
    )(*shards, *cast_only)


HBM_SPEC = pl.BlockSpec(memory_space=pltpu.HBM)
SEM_SPEC = pl.BlockSpec(memory_space=pltpu.SEMAPHORE)
EFFECT = pltpu.SideEffectType.DATAFLOW_SIDE_EFFECTING
TOKEN = jax.ShapeDtypeStruct((8, 128), F32)


def _in_hbm(a):
    return pltpu.with_memory_space_constraint(a, pltpu.HBM)


START_IDS = {name: i for i, name in enumerate((
    "gather_first_start", "gather_attn_forward_start", "gather_mlp_forward_start", "exchange_ff2_start",
    "exchange_ff1_start", "exchange_attn_start", "exchange_out_start", "exchange_in_start", "exchange_mix_start"))}


GATHER_W_IN_ID = len(START_IDS)


def _handshake(peers):
    barrier = pltpu.get_barrier_semaphore()
    for peer in peers:
        pl.semaphore_signal(barrier, inc=1, device_id=peer, device_id_type=MESH_ID)
    pl.semaphore_wait(barrier, len(peers))


def _peers_all():
    x, y, c = _mesh_pos()
    return [(1 - x if k & 4 else x, 1 - y if k & 2 else y, 1 - c if k & 1 else c) for k in range(1, N_DEV)]


def _peers_first_level():
    x, y, c = _mesh_pos()
    return [(x, y, 1 - c), (1 - x, y, c), (x, 1 - y, c), (1 - x, 1 - y, c)]


def _peers_sibling():
    x, y, c = _mesh_pos()
    return [(x, y, 1 - c)]


def _split_start(copies_of, srcs, lands, n_sems, name, peers_of, collective_id):
    ns, nl, k = len(srcs), len(lands), len(n_sems)

    def body(*refs):
        _handshake(peers_of())
        src_refs, land_refs = refs[:ns], refs[ns:ns + nl]
        sems = refs[ns + nl:ns + nl + k]
        token = refs[-1]
        for cp in copies_of(src_refs, land_refs, sems):
            cp.start()
        token[...] = jnp.zeros_like(token)

    outs = pl.pallas_call(
        body, name=name,
        out_shape=[pltpu.SemaphoreType.DMA((q,)) for q in n_sems]
        + [pltpu.HBM(a.shape, a.dtype) for a in list(srcs) + list(lands)] + [TOKEN],
        in_specs=[HBM_SPEC] * (ns + nl),
        out_specs=[SEM_SPEC] * k + [HBM_SPEC] * (ns + nl) + [VMEM_WHOLE],
        input_output_aliases={i: k + i for i in range(ns + nl)},
        compiler_params=pltpu.CompilerParams(has_side_effects=EFFECT, collective_id=collective_id),
    )(*[_in_hbm(a) for a in list(srcs) + list(lands)])
    return outs[:k], outs[k:k + ns], outs[k + ns:k + ns + nl], outs[-1]


def _split_wait(copies_of, handle, after, name):
    sems, srcs, lands, _ = handle
    ns, nl, k = len(srcs), len(lands), len(sems)

    def body(*refs):
        src_refs, land_refs = refs[:ns], refs[ns:ns + nl]
        sem_refs = refs[ns + nl:ns + nl + k]
        for cp in copies_of(src_refs, land_refs, sem_refs):
            cp.wait()

    outs = pl.pallas_call(
        body, name=name,
        out_shape=[pltpu.HBM(a.shape, a.dtype) for a in list(srcs) + list(lands)],
        in_specs=[HBM_SPEC] * (ns + nl) + [SEM_SPEC] * k + [ANY_SPACE],
        out_specs=[HBM_SPEC] * (ns + nl),
        input_output_aliases={i: i for i in range(ns + nl)},
        compiler_params=pltpu.CompilerParams(has_side_effects=EFFECT),
    )(*srcs, *lands, *sems, after)
    return outs[ns:]


def _gather_first_copies(shard_refs, land_refs, sems):
    send_sems, recv_sems, local_sems = sems
    x, y, c = _mesh_pos()
    me = _flat(x, y, c)
    peers = [(x, y, 1 - c), (1 - x, y, c), (x, 1 - y, c), (1 - x, 1 - y, c)]
    copies = []
    for a, (shard, land) in enumerate(zip(shard_refs, land_refs)):
        copies.append(pltpu.make_async_copy(shard, land.at[me], local_sems.at[a]))
        for k, peer in enumerate(peers):
            copies.append(pltpu.make_async_remote_copy(
                src_ref=shard, dst_ref=land.at[me], send_sem=send_sems.at[4 * a + k], recv_sem=recv_sems.at[4 * a + k],
                device_id=peer, device_id_type=MESH_ID))
    return copies


def _gather_forward_copies(src_refs, land_refs, sems):
    del src_refs
    send_sems, recv_sems = sems
    x, y, c = _mesh_pos()
    chips = [(1 - x, y), (x, 1 - y), (1 - x, 1 - y)]
    copies = []
    for a, land in enumerate(land_refs):
        for j, chip in enumerate(chips):
            rows = land.at[_flat(*chip, c)]
            copies.append(pltpu.make_async_remote_copy(
                src_ref=rows, dst_ref=rows, send_sem=send_sems.at[3 * a + j], recv_sem=recv_sems.at[3 * a + j],
                device_id=(x, y, 1 - c), device_id_type=MESH_ID))
    return copies


def _gather_first_start(groups, name):
    shards = [s for g in groups for s in g]
    lands = [lax.empty((N_DEV,) + s.shape, s.dtype) for s in shards]
    bounds = [sum(len(g) for g in groups[:i]) for i in range(len(groups) + 1)]

    def copies_of(src_refs, land_refs, sems):
        copies = []
        for i in range(len(groups)):
            lo, hi = bounds[i], bounds[i + 1]
            copies += _gather_first_copies(src_refs[lo:hi], land_refs[lo:hi], sems[3 * i:3 * i + 3])
        return copies

    n_sems = tuple(q for g in groups for q in (4 * len(g), 4 * len(g), len(g)))
    sems, srcs, lands, token = _split_start(copies_of, shards, lands, n_sems, name, _peers_first_level, START_IDS[name])
    return [(sems[3 * i:3 * i + 3], srcs[bounds[i]:bounds[i + 1]], lands[bounds[i]:bounds[i + 1]], token)
            for i in range(len(groups))]


def _gather_forward_start(lands, name):
    n = len(lands)
    return _split_start(_gather_forward_copies, [], lands, (3 * n, 3 * n), name, _peers_sibling, START_IDS[name])


def _all_to_all_copies(n_scattered):
    def copies_of(src_refs, land_refs, sems):
        send_sems, recv_sems, local_sems = sems
        x, y, c = _mesh_pos()
        me = _flat(x, y, c)
        copies = []
        for a, (src, land) in enumerate(zip(src_refs, land_refs)):
            scattered = a < n_scattered
            copies.append(pltpu.make_async_copy(src.at[me] if scattered else src, land.at[me], local_sems.at[a]))
            for k in range(1, N_DEV):
                peer = (1 - x if k & 4 else x, 1 - y if k & 2 else y, 1 - c if k & 1 else c)
                copies.append(pltpu.make_async_remote_copy(
                    src_ref=src.at[_flat(*peer)] if scattered else src, dst_ref=land.at[me],
                    send_sem=send_sems.at[7 * a + k - 1], recv_sem=recv_sems.at[7 * a + k - 1],
                    device_id=peer, device_id_type=MESH_ID))
        return copies
    return copies_of


def _all_to_all_start(scattered, broadcast, name):
    srcs = list(scattered) + list(broadcast)
    lands = [lax.empty(a.shape, a.dtype) for a in scattered] + [lax.empty((N_DEV,) + a.shape, a.dtype) for a in broadcast]
    n = len(srcs)
    return _split_start(_all_to_all_copies(len(scattered)), srcs, lands, (7 * n, 7 * n, n), name, _peers_all,
                        START_IDS[name])


def _call_behind(deps, body, *, in_specs, **kwargs):
    n_in, n_dep = len(in_specs), len(deps)

    def body_without_deps(*refs):
        return body(*refs[:n_in], *refs[n_in + n_dep:])

    call = pl.pallas_call(body_without_deps, in_specs=list(in_specs) + [ANY_SPACE] * n_dep, **kwargs)
    return lambda *operands: call(*operands, *deps)


def _row_tile(rows):
    if rows <= 2 * 256:
        return rows
    for cand in (256, 128, 64, 32, 16):
        if rows % cand == 0:
            return cand
    return rows


def _adamw_whole(groups, name):
    n = len(groups)

    def body(*refs):
        for i in range(n):
            g_ref, w_ref, m_ref, v_ref = refs[4 * i:4 * i + 4]
            d_ref, nm_ref, nv_ref = refs[4 * n + 3 * i:4 * n + 3 * i + 3]
            d_ref[...], nm_ref[...], nv_ref[...] = _adam_update(g_ref[...], w_ref[...], m_ref[...], v_ref[...])

    outs = pl.pallas_call(
        body, name=name, out_shape=[jax.ShapeDtypeStruct(grp[0].shape, F32) for grp in groups for _ in range(3)],
        in_specs=[VMEM_WHOLE] * (4 * n), out_specs=[VMEM_WHOLE] * (3 * n),
        compiler_params=_params(),
    )(*[t for grp in groups for t in grp])
    return [outs[3 * i:3 * i + 3] for i in range(n)]


def _sum_sources_whole(recvs, name):
    n = len(recvs)

    def body(*refs):
        for r_ref, o_ref in zip(refs[:n], refs[n:]):
            acc = r_ref[0].astype(F32)
            for d in range(1, N_DEV):
                acc = acc + r_ref[d].astype(F32)
            o_ref[...] = acc

    return pl.pallas_call(
        body, name=name, out_shape=[jax.ShapeDtypeStruct(r.shape[1:], F32) for r in recvs],
        in_specs=[VMEM_WHOLE] * n, out_specs=[VMEM_WHOLE] * n,
        compiler_params=_params(),
    )(*recvs)


def _sum_adamw(recv, w, m, v, name):
    _, rows, cols = recv.shape
    tile = _row_tile(rows)

    def body(r_ref, w_ref, m_ref, v_ref, g_ref, d_ref, nm_ref, nv_ref):
        acc = r_ref[0].astype(F32)
        for d in range(1, N_DEV):
            acc = acc + r_ref[d].astype(F32)
        g_ref[...] = acc
        d_ref[...], nm_ref[...], nv_ref[...] = _adam_update(acc, w_ref[...], m_ref[...], v_ref[...])

    spec = pl.BlockSpec((tile, cols), lambda i: (i, 0))
    shp = jax.ShapeDtypeStruct((rows, cols), F32)
    return pl.pallas_call(
        body, name=name, grid=(rows // tile,), out_shape=[shp] * 4,
        in_specs=[pl.BlockSpec((N_DEV, tile, cols), lambda i: (0, i, 0)), spec, spec, spec], out_specs=[spec] * 4,
        compiler_params=_params(("parallel",)),
    )(recv, w, m, v)


SMALL_SLOTS = {"norm_x_g": (0, 0, 1, D_MODEL), "norm_mem_g": (0, 8, 1, D_MODEL), "norm_ffn_g": (0, 16, 1, D_MODEL),
               "final_norm_g": (0, 24, 1, D_MODEL), "pool_scale": (0, 32, 1, HW),
               "norm_mix_g": (1, 0, 1, D_MODEL), "lb_logits": (1, 8, 2, HW), "hgrn_norm_g": (1, 16, HEADS, HD)}
LOSS_ROW = 25
SMALL_ORDER = ("norm_mix_g", "lb_logits", "hgrn_norm_g", "pool_scale", "norm_x_g", "norm_mem_g", "norm_ffn_g",
               "final_norm_g", "w_pool")


def _small_update(srecvs, wprecv, params):
    flat = [t for n in SMALL_ORDER for t in params[n]]
    nb = len(srecvs)
    n_in = nb + 1 + len(flat)

    def body(*refs):
        s_refs, wp_ref = refs[0:nb], refs[nb]
        in_refs = refs[nb + 1:n_in]
        loss_ref = refs[n_in]
        out_refs = refs[n_in + 1:-nb]
        accs = refs[-nb:]
        for s_ref, acc in zip(s_refs, accs):
            total = s_ref[0]
            for d in range(1, N_DEV):
                total = total + s_ref[d]
            acc[...] = total
        loss_ref[...] = accs[0][LOSS_ROW:LOSS_ROW + 1, 0:1]
        for i, name in enumerate(SMALL_ORDER):
            w_ref, m_ref, v_ref = in_refs[3 * i:3 * i + 3]
            g_ref, d_ref, nm_ref, nv_ref = out_refs[4 * i:4 * i + 4]
            if name == "w_pool":
                g = wp_ref[0]
                for d in range(1, N_DEV):
                    g = g + wp_ref[d]
            else:
                buf, r0, nr, nc = SMALL_SLOTS[name]
                g = accs[buf][r0:r0 + nr, 0:nc]
            g_ref[...] = g
            d_ref[...], nm_ref[...], nv_ref[...] = _adam_update(g, w_ref[...], m_ref[...], v_ref[...])

    out_shape = [jax.ShapeDtypeStruct((1, 1), F32)]
    for n in SMALL_ORDER:
        out_shape += [jax.ShapeDtypeStruct(params[n][0].shape, F32)] * 4
    outs = pl.pallas_call(
        body, name="small_update", out_shape=out_shape,
        in_specs=[VMEM_WHOLE] * n_in, out_specs=[VMEM_WHOLE] * len(out_shape),
        scratch_shapes=[pltpu.VMEM(r.shape[1:], F32) for r in srecvs],
        compiler_params=_params(),
    )(*srecvs, wprecv, *flat)
    return outs[0], {n: outs[1 + 4 * i:5 + 4 * i] for i, n in enumerate(SMALL_ORDER)}


def _in_proj(x, g, w_t, deps):
    s = x.shape[0]
    tm = min(ROW_TILE, s)

    def body(x_ref, g_ref, w_ref, z_ref, h_ref):
        xv = x_ref[...]
        h = (xv * _rms(xv) * g_ref[...]).astype(BF16)
        h_ref[...] = h
        z_ref[...] = _mm_nt(h, w_ref[...])

    return _call_behind(
        deps, body, name="in_proj", grid=(s // tm,),
        out_shape=[jax.ShapeDtypeStruct((s, IN_WIDTH), F32), jax.ShapeDtypeStruct((s, D_MODEL), BF16)],
        in_specs=[pl.BlockSpec((tm, D_MODEL), lambda i: (i, 0)), _full((1, D_MODEL)), VMEM_WHOLE],
        out_specs=[pl.BlockSpec((tm, IN_WIDTH), lambda i: (i, 0)), pl.BlockSpec((tm, D_MODEL), lambda i: (i, 0))],
        compiler_params=_params(("parallel",)),
    )(x, g, w_t)


def _chunk_masks():
    row = lax.broadcasted_iota(jnp.int32, (CHUNK, CHUNK), 0)
    col = lax.broadcasted_iota(jnp.int32, (CHUNK, CHUNK), 1)
    return row, col


def _ones_where(mask):
    return jnp.where(mask, 1.0, 0.0).astype(BF16)


def _hgrn_gates(zq, zf, lb):
    sq = _sigmoid(zq)
    sig = _sigmoid(zf)
    f = lb + (1.0 - lb) * sig
    return zq * sq, sq, sig, f


def _sub_chunk_masks(width):
    trow = lax.broadcasted_iota(jnp.int32, (CHUNK, width), 0)
    return [(trow >= SUB * j) & (trow < SUB * (j + 1)) for j in range(N_SUB)]


def _head(a, h):
    return a[:, HD * h:HD * (h + 1)]


def _lanes(parts):
    return jnp.concatenate(parts, axis=1)


def _hgrn_decay_factors(b_scr, r0, b, in_sub):
    bases = [jnp.zeros((1, HW), F32)] + [b_scr[r0 + SUB * j - 1:r0 + SUB * j, :] for j in range(1, N_SUB)]
    own_base = bases[N_SUB - 1]
    for j in range(N_SUB - 2, -1, -1):
        own_base = jnp.where(in_sub[j], bases[j], own_base)
    eq = jnp.exp(b - own_base)
    ek = []
    for j in range(N_SUB):
        upto = SUB * (j + 1)
        e = jnp.exp(jnp.minimum(bases[j] - b[0:upto], EXP_CAP))
        ek.append(e if upto == CHUNK else jnp.concatenate([e, jnp.zeros((CHUNK - upto, HW), F32)], axis=0))
    return eq, ek


def _per_sub_chunk(x, in_sub):
    return _lanes([jnp.where(in_sub[j], x, 0.0) for j in range(N_SUB)])


def _own_lane_block(a, in_sub):
    out = a[:, HD * (N_SUB - 1):HD * N_SUB]
    for j in range(N_SUB - 2, -1, -1):
        out = jnp.where(in_sub[j], a[:, HD * j:HD * (j + 1)], out)
    return out


def _head_rms(o):
    return _lanes([jnp.broadcast_to(_rms(_head(o, h)), (CHUNK, HD)) for h in range(HEADS)])


def _head_mean(a):
    return _lanes([jnp.broadcast_to(jnp.mean(_head(a, h), axis=-1, keepdims=True), (CHUNK, HD)) for h in range(HEADS)])


def _hgrn_fwd(z, lb_logits, gn):
    s = z.shape[0]
    n_chunks = s // CHUNK

    def body(zq_ref, zf_ref, zi_ref, zg_ref, lbl_ref, gn_ref, oa_ref, o_ref, st_ref, state, b_scr):
        @pl.when(pl.program_id(0) == 0)
        def _():
            state[...] = jnp.zeros_like(state)

        lb = _sigmoid(lbl_ref[0:1, :] - lbl_ref[1:2, :])
        row, col = _chunk_masks()
        causal = col <= row
        tri = _ones_where(causal)
        in_sub, in_sub_head = _sub_chunk_masks(HW), _sub_chunk_masks(HD)
        gn_row = _lanes([gn_ref[h:h + 1, :] for h in range(HEADS)])
        def front(c):
            r0 = CHUNK * c
            rs = slice(r0, r0 + CHUNK)
            q, _, _, f = _hgrn_gates(zq_ref[rs, :], zf_ref[rs, :], lb)
            kk = 1.0 - f
            b = _tri_dot(tri, jnp.log(f), 3)
            b_scr[rs, :] = b
            eq, ek = _hgrn_decay_factors(b_scr, r0, b, in_sub)
            b_last = b_scr[r0 + CHUNK - 1:r0 + CHUNK, :]
            qe = q * eq
            return {"rs": rs, "v": zi_ref[rs, :], "qg": q * jnp.exp(b), "kd": kk * jnp.exp(b_last - b),
                    "lam_last": jnp.exp(b_last),
                    "q16": [_per_sub_chunk(_head(qe, h), in_sub_head).astype(BF16) for h in range(HEADS)],
                    "ke16": [_lanes([_head(kk * e, h) for e in ek]).astype(BF16) for h in range(HEADS)]}

        def recurrence(c, p):
            st_ref[c] = state[...]
            a, o_inter = [], []
            for h in range(HEADS):
                vh, st = _head(p["v"], h), state[h]
                a.append(jnp.where(causal, _mm_nt(p["q16"][h], p["ke16"][h]), 0.0))
                o_inter.append(_mm_nt(_head(p["qg"], h), st))
                state[h] = st * _head(p["lam_last"], h) + _mm_tn(vh, _head(p["kd"], h))
            return _lanes([_mm(a[h], _head(p["v"], h)) + o_inter[h] for h in range(HEADS)])

        def back(p, o):
            rs = p["rs"]
            o_ref[rs, :] = o
            zg = zg_ref[rs, :]
            oa_ref[rs, :] = (o * _head_rms(o) * gn_row * zg * _sigmoid(zg)).astype(BF16)

        p = front(0)
        for c in range(CHUNKS_PER_STEP):
            o = recurrence(c, p)
            p_next = front(c + 1) if c + 1 < CHUNKS_PER_STEP else None
            back(p, o)
            p = p_next

    rows = CHUNK * CHUNKS_PER_STEP
    zspec = lambda cb: pl.BlockSpec((rows, HW), lambda i, cb=cb: (i, cb))
    return pl.pallas_call(
        body, name="hgrn_fwd", grid=(s // rows,),
        out_shape=[jax.ShapeDtypeStruct((s, 2 * HW), BF16), jax.ShapeDtypeStruct((s, HW), F32),
                   jax.ShapeDtypeStruct((n_chunks, HEADS, HD, HD), F32)],
        in_specs=[zspec(0), zspec(1), zspec(2), zspec(3), _full((2, HW)), _full((HEADS, HD))],
        out_specs=[pl.BlockSpec((rows, HW), lambda i: (i, 0)), pl.BlockSpec((rows, HW), lambda i: (i, 0)),
                   pl.BlockSpec((CHUNKS_PER_STEP, HEADS, HD, HD), lambda i: (i, 0, 0, 0))],
        scratch_shapes=[pltpu.VMEM((HEADS, HD, HD), F32), pltpu.VMEM((rows, HW), F32)],
        compiler_params=_params(("arbitrary",)),
    )(z, z, z, z, lb_logits, gn)


def _pool_counts(tile_idx, tm):
    t = tile_idx * tm + lax.broadcasted_iota(jnp.int32, (tm, 1), 0)
    return [1.0 / jnp.minimum(t + 1, w).astype(F32) for w in POOL_WINDOWS]


def _pool_fwd(z, w_pool, scale, mixed_in, deps):
    s = z.shape[0]
    tm = min(ROW_TILE, s)

    def body(p_ref, w_ref, sc_ref, mixin_ref, ob_ref, pooled_ref, ext):
        i = pl.program_id(0)

        @pl.when(i == 0)
        def _():
            ext[0:POOL_HALO, :] = jnp.zeros((POOL_HALO, HW), F32)

        @pl.when(i > 0)
        def _():
            ext[0:POOL_HALO, :] = ext[tm:tm + POOL_HALO, :]

        ext[POOL_HALO:POOL_HALO + tm, :] = p_ref[...]
        inv = _pool_counts(i, tm)
        for g, w in enumerate(POOL_WINDOWS):
            sl = slice(HD * g, HD * (g + 1))
            p = ext[POOL_HALO:POOL_HALO + tm, sl]
            win = p
            for d in range(1, w):
                win = win + ext[POOL_HALO - d:POOL_HALO - d + tm, sl]
            pooled = (win * inv[g] - p).astype(BF16)
            pooled_ref[:, sl] = pooled
            ob_ref[:, sl] = (_mm(pooled, w_ref[g]) * sc_ref[:, sl]).astype(BF16)

    return _call_behind(
        deps, body, name="pool_fwd", grid=(s // tm,),
        out_shape=[jax.ShapeDtypeStruct((s, 2 * HW), BF16), jax.ShapeDtypeStruct((s, HW), BF16)],
        in_specs=[pl.BlockSpec((tm, HW), lambda i: (i, 4)), _full((HEADS, HD, HD)), _full((1, HW)), ANY_SPACE],
        out_specs=[pl.BlockSpec((tm, HW), lambda i: (i, 1)), pl.BlockSpec((tm, HW), lambda i: (i, 0))],
        scratch_shapes=[pltpu.VMEM((tm + POOL_HALO, HW), F32)],
        input_output_aliases={3: 0},
        compiler_params=_params(("arbitrary",)),
    )(z, w_pool, scale, mixed_in)


def _mem_kv(mem, g, wk, wv, deps):
    def body(m_ref, g_ref, wk_ref, wv_ref, hm_ref, k_ref, v_ref):
        m = m_ref[...]
        hm = (m * _rms(m) * g_ref[...]).astype(BF16)
        hm_ref[...] = hm
        k_ref[...] = _mm(hm, wk_ref[...]).astype(BF16)
        v_ref[...] = _mm(hm, wv_ref[...]).astype(BF16)

    shp = jax.ShapeDtypeStruct((MEM_LEN, D_MODEL), BF16)
    return _call_behind(
        deps, body, name="mem_kv", out_shape=[shp, shp, shp],
        in_specs=[VMEM_WHOLE] * 4, out_specs=[VMEM_WHOLE] * 3,
        compiler_params=_params(),
    )(mem, g, wk, wv)


def _softmax_rows(sc):
    e = jnp.exp(sc - jnp.max(sc, axis=-1, keepdims=True))
    return e / jnp.sum(e, axis=-1, keepdims=True)


def _mix_xattn_fwd(x0, mixed, w_out, g, wq, xk, xv, wo_t, deps):
    s = x0.shape[0]
    tm = min(ROW_TILE, s)
    scale = XHD ** -0.5

    def body(x_ref, mix_ref, wout_ref, g_ref, wq_ref, k_ref, v_ref, wo_ref, x1_ref, o_ref, hq_ref, q_ref, att_ref):
        xv_ = x_ref[...] + _mm(mix_ref[...], wout_ref[...])
        x1_ref[...] = xv_
        hq = (xv_ * _rms(xv_) * g_ref[...]).astype(BF16)
        hq_ref[...] = hq
        q_ref[...] = (_mm(hq, wq_ref[...]) * scale).astype(BF16)
        heads = [slice(XHD * h, XHD * (h + 1)) for h in range(HEADS)]
        scores = [_mm_nt(q_ref[:, sl], k_ref[:, sl]) for sl in heads]
        probs = [_softmax_rows(sc) for sc in scores]
        for sl, p in zip(heads, probs):
            att_ref[:, sl] = _mm(p, v_ref[:, sl]).astype(BF16)
        o_ref[...] = xv_ + _mm_nt(att_ref[...], wo_ref[...])

    row_f32 = pl.BlockSpec((tm, D_MODEL), lambda i: (i, 0))
    bshape = jax.ShapeDtypeStruct((s, D_MODEL), BF16)
    fshape = jax.ShapeDtypeStruct((s, D_MODEL), F32)
    return _call_behind(
        deps, body, name="mix_xattn_fwd", grid=(s // tm,),
        out_shape=[fshape, fshape, bshape, bshape, bshape],
        in_specs=[row_f32, row_f32, VMEM_WHOLE, _full((1, D_MODEL)), VMEM_WHOLE, VMEM_WHOLE, VMEM_WHOLE, VMEM_WHOLE],
        out_specs=[row_f32] * 5,
        compiler_params=_params(("parallel",)),
    )(x0, mixed, w_out, g, wq, xk, xv, wo_t)


def _mlp_fwd_loss(x, g, w1, w2, gf, target):
    s = x.shape[0]
    tm = min(ROW_TILE, s)

    def body(x_ref, g_ref, w1_ref, w2_ref, gf_ref, t_ref, dx_ref, dx16_ref, u_ref, hf_ref, slot_ref):
        @pl.when(pl.program_id(0) == 0)
        def _():
            slot_ref[...] = jnp.zeros_like(slot_ref)

        xv = x_ref[...]
        hf = (xv * _rms(xv) * g_ref[...]).astype(BF16)
        hf_ref[...] = hf
        a_next = _mm(hf, w1_ref[0])
        for j in range(N_DEV):
            a = jnp.maximum(a_next, 0.0)
            if j + 1 < N_DEV:
                a_next = _mm(hf, w1_ref[j + 1])
            u_ref[:, FF_BLK * j:FF_BLK * (j + 1)] = (a * a).astype(BF16)
        acc = xv + _mm(u_ref[...], w2_ref[...])
        gfv = gf_ref[...]
        r = _rms(acc)
        n = acc * r
        err = n * gfv - t_ref[...]
        slot_ref[1:2, :] += jnp.sum(jnp.mean(err * err, axis=-1, keepdims=True), axis=0, keepdims=True) * 0.5
        dy = err * (1.0 / D_MODEL)
        slot_ref[0:1, :] += jnp.sum(dy * n, axis=0, keepdims=True)
        dn = dy * gfv
        dx = r * (dn - n * jnp.mean(dn * n, axis=-1, keepdims=True))
        dx_ref[...] = dx
        dx16_ref[...] = dx.astype(BF16)

    row_f32 = pl.BlockSpec((tm, D_MODEL), lambda i: (i, 0))
    return pl.pallas_call(
        body, name="mlp_fwd_loss", grid=(s // tm,),
        out_shape=[jax.ShapeDtypeStruct((s, D_MODEL), F32), jax.ShapeDtypeStruct((s, D_MODEL), BF16),
                   jax.ShapeDtypeStruct((s, D_FF), BF16), jax.ShapeDtypeStruct((s, D_MODEL), BF16),
                   jax.ShapeDtypeStruct((SLOT, D_MODEL), F32)],
        in_specs=[row_f32, _full((1, D_MODEL)), VMEM_WHOLE, VMEM_WHOLE, _full((1, D_MODEL)), row_f32],
        out_specs=[row_f32, row_f32, pl.BlockSpec((tm, D_FF), lambda i: (i, 0)), row_f32, _full((SLOT, D_MODEL))],
        compiler_params=_params(("arbitrary",)),
    )(x, g, w1, w2, gf, target)


def _zero_slot(slot_ref):
    @pl.when(pl.program_id(0) == 0)
    def _():
        slot_ref[...] = jnp.zeros_like(slot_ref)


def _mlp_bwd(dx3, u, x2, g, w1, w2, deps):
    s = x2.shape[0]
    tm = min(ROW_TILE // 2, s)

    n_steps, ring = s // tm, 3

    def body(d_hbm, u_hbm, x_hbm, g_ref, w1_ref, w2_ref, da_ref, dx_ref, slot_ref, d_buf, u_buf, x_buf, sems):
        i = pl.program_id(0)

        def fetch(step):
            b = step % ring
            rows = pl.ds(step * tm, tm) if isinstance(step, int) else pl.ds(pl.multiple_of(step * tm, tm), tm)
            return [pltpu.make_async_copy(d_hbm.at[rows], d_buf.at[b], sems.at[0, b]),
                    pltpu.make_async_copy(u_hbm.at[rows], u_buf.at[b], sems.at[1, b]),
                    pltpu.make_async_copy(x_hbm.at[rows], x_buf.at[b], sems.at[2, b])]

        @pl.when(i == 0)
        def _():
            for first in range(min(ring - 1, n_steps)):
                for cp in fetch(first):
                    cp.start()

        @pl.when(i + (ring - 1) < n_steps)
        def _():
            for cp in fetch(i + (ring - 1)):
                cp.start()

        for cp in fetch(i):
            cp.wait()
        b = i % ring
        d_ref, u_ref, x_ref = d_buf.at[b], u_buf.at[b], x_buf.at[b]
        _zero_slot(slot_ref)
        d = d_ref[...]
        d16 = d.astype(BF16)
        du_next = _mm_nt(d16, w2_ref[0])
        dhf = jnp.zeros((tm, D_MODEL), F32)
        for j in range(N_DEV):
            sl = slice(FF_BLK * j, FF_BLK * (j + 1))
            du = du_next
            if j + 1 < N_DEV:
                du_next = _mm_nt(d16, w2_ref[j + 1])
            u = u_ref[:, sl].astype(F32)
            da = (du * (2.0 * u * lax.rsqrt(jnp.maximum(u, TINY)))).astype(BF16)
            da_ref[:, sl] = da
            dhf = dhf + _mm_nt(da, w1_ref[j])
        dx, dg = _rms_bwd(x_ref[...], g_ref[...], dhf)
        dx_ref[...] = d + dx
        slot_ref[0:1, :] += dg

    row_f32 = pl.BlockSpec((tm, D_MODEL), lambda i: (i, 0))
    return _call_behind(
        deps, body, name="mlp_bwd", grid=(s // tm,),
        out_shape=[jax.ShapeDtypeStruct((s, D_FF), BF16), jax.ShapeDtypeStruct((s, D_MODEL), F32),
                   jax.ShapeDtypeStruct((SLOT, D_MODEL), F32)],
        in_specs=[ANY_SPACE, ANY_SPACE, ANY_SPACE, _full((1, D_MODEL)), VMEM_WHOLE, VMEM_WHOLE],
        out_specs=[pl.BlockSpec((tm, D_FF), lambda i: (i, 0)), row_f32, _full((SLOT, D_MODEL))],
        scratch_shapes=[pltpu.VMEM((ring, tm, D_MODEL), F32), pltpu.VMEM((ring, tm, D_FF), BF16),
                        pltpu.VMEM((ring, tm, D_MODEL), F32), pltpu.SemaphoreType.DMA((3, ring))],
        compiler_params=_params(("arbitrary",)),
    )(dx3, u, x2, g, w1, w2)


def _wgrad(a, b, name, col_blocks=False, update=None):
    s, m = a.shape
    n = b.shape[1]
    tm = 1280 if m % 1280 == 0 else min(1024, m)
    tn = min(1024, n)
    blk = n // N_DEV
    per_step = tn // blk if col_blocks else 1
    ts = min((4 if m * n >= D_MODEL * D_FF else 2) * ROW_TILE, s)
    n_s = s // ts
    grid = (m // tm, n // tn, n_s)

    def body(a_ref, b_ref, *rest):
        o_ref, acc = rest[-2], rest[-1]
        k = pl.program_id(2)

        @pl.when(k == 0)
        def _():
            acc[...] = jnp.zeros_like(acc)

        acc[...] += _mm_tn(a_ref[...], b_ref[...])
        if update is not None:
            r_ref, w_ref, m_ref, v_ref, g_ref, d_ref, nm_ref, nv_ref = rest[:8]
            g = r_ref[0].astype(F32)
            for d in range(1, N_DEV):
                g = g + r_ref[d].astype(F32)
            g_ref[...] = g
            d_ref[...], nm_ref[...], nv_ref[...] = _adam_update(g, w_ref[...], m_ref[...], v_ref[...])

        @pl.when(k == n_s - 1)
        def _():
            if col_blocks:
                for p in range(per_step):
                    o_ref[p] = acc[:, blk * p:blk * (p + 1)].astype(BF16)
            else:
                o_ref[...] = acc[...].astype(BF16)

    if col_blocks:
        out_shape = jax.ShapeDtypeStruct((N_DEV, m, blk), BF16)
        out_spec = pl.BlockSpec((per_step, tm, blk), lambda i, j, k: (j, i, 0))
    else:
        out_shape = jax.ShapeDtypeStruct((m, n), BF16)
        out_spec = pl.BlockSpec((tm, tn), lambda i, j, k: (i, j))
    in_specs = [pl.BlockSpec((ts, tm), lambda i, j, k: (k, i)), pl.BlockSpec((ts, tn), lambda i, j, k: (k, j))]
    out_shapes, out_specs, operands = [out_shape], [out_spec], [a, b]
    if update is not None:
        rows, cols = update[1].shape
        steps = grid[0] * grid[1] * grid[2]
        tr = rows // steps
        step = lambda i, j, k: (i * grid[1] + j) * grid[2] + k
        piece = pl.BlockSpec((tr, cols), lambda i, j, k: (step(i, j, k), 0))
        in_specs += [pl.BlockSpec((N_DEV, tr, cols), lambda i, j, k: (0, step(i, j, k), 0)), piece, piece, piece]
        out_shapes = [jax.ShapeDtypeStruct((rows, cols), F32)] * 4 + out_shapes
        out_specs = [piece] * 4 + out_specs
        operands += list(update)
    outs = pl.pallas_call(
        body, name=name, grid=grid, out_shape=out_shapes, in_specs=in_specs, out_specs=out_specs,
        scratch_shapes=[pltpu.VMEM((tm, tn), F32)],
        compiler_params=_params(("parallel", "parallel", "arbitrary")),
    )(*operands)
    return outs[0] if update is None else (outs[4], tuple(outs[:4]))


def _xattn_bwd(dx2, x1, g, q, xk, xv, wq, wo_t, deps):
    s = x1.shape[0]
    tm = min(ROW_TILE, s)
    scale = XHD ** -0.5

    def body(d_ref, x_ref, g_ref, q_ref, k_ref, v_ref, wq_ref, wo_ref, dx_ref, dx16_ref, dq_ref, dk_ref, dv_ref, slot_ref,
             datt):
        _zero_slot(slot_ref)

        @pl.when(pl.program_id(0) == 0)
        def _():
            dk_ref[...] = jnp.zeros_like(dk_ref)
            dv_ref[...] = jnp.zeros_like(dv_ref)

        d = d_ref[...]
        datt[...] = _mm(d, wo_ref[...]).astype(BF16)
        heads = [slice(XHD * h, XHD * (h + 1)) for h in range(HEADS)]
        scores = [_mm_nt(q_ref[:, sl], k_ref[:, sl]) for sl in heads]
        dps = [_mm_nt(datt[:, sl], v_ref[:, sl]) for sl in heads]
        probs = [_softmax_rows(sc) for sc in scores]
        dss = [(p * (dp - jnp.sum(dp * p, axis=-1, keepdims=True))).astype(BF16) for p, dp in zip(probs, dps)]
        for sl, p, ds in zip(heads, probs, dss):
            dq_ref[:, sl] = (_mm(ds, k_ref[:, sl]) * scale).astype(BF16)
            dk_ref[:, sl] += _mm_tn(ds, q_ref[:, sl])
            dv_ref[:, sl] += _mm_tn(p, datt[:, sl])
        dx, dg = _rms_bwd(x_ref[...], g_ref[...], _mm_nt(dq_ref[...], wq_ref[...]))
        dx_ref[...] = d + dx
        dx16_ref[...] = (d + dx).astype(BF16)
        slot_ref[0:1, :] += dg

    row_f32 = pl.BlockSpec((tm, D_MODEL), lambda i: (i, 0))
    kv = jax.ShapeDtypeStruct((MEM_LEN, D_MODEL), F32)
    tokens16 = jax.ShapeDtypeStruct((s, D_MODEL), BF16)
    return _call_behind(
        deps, body, name="xattn_bwd", grid=(s // tm,),
        out_shape=[jax.ShapeDtypeStruct((s, D_MODEL), F32), tokens16, tokens16, kv, kv,
                   jax.ShapeDtypeStruct((SLOT, D_MODEL), F32)],
        in_specs=[row_f32, row_f32, _full((1, D_MODEL)), row_f32, VMEM_WHOLE, VMEM_WHOLE, VMEM_WHOLE, VMEM_WHOLE],
        out_specs=[row_f32, row_f32, row_f32, _full((MEM_LEN, D_MODEL)), _full((MEM_LEN, D_MODEL)),
                   _full((SLOT, D_MODEL))],
        scratch_shapes=[pltpu.VMEM((tm, D_MODEL), BF16)],
        compiler_params=_params(("arbitrary",)),
    )(dx2, x1, g, q, xk, xv, wq, wo_t)


def _mem_bwd(mem, g, hm, dxk, dxv, wk, wv):
    def body(m_ref, g_ref, hm_ref, dk_ref, dv_ref, wk_ref, wv_ref, dwk_ref, dwv_ref, slot_ref):
        dk, dv = dk_ref[...], dv_ref[...]
        hm_ = hm_ref[...]
        dwk_ref[...] = _mm_tn(hm_, dk).astype(BF16)
        dwv_ref[...] = _mm_tn(hm_, dv).astype(BF16)
        _, dg = _rms_bwd(m_ref[...], g_ref[...], _mm_nt(dk, wk_ref[...]) + _mm_nt(dv, wv_ref[...]))
        slot_ref[...] = jnp.zeros_like(slot_ref)
        slot_ref[0:1, :] = dg

    wshape = jax.ShapeDtypeStruct((D_MODEL, D_MODEL), BF16)
    return pl.pallas_call(
        body, name="mem_bwd", out_shape=[wshape, wshape, jax.ShapeDtypeStruct((SLOT, D_MODEL), F32)],
        in_specs=[VMEM_WHOLE] * 7, out_specs=[VMEM_WHOLE] * 3,
        compiler_params=_params(),
    )(mem, g, hm, dxk, dxv, wk, wv)


def _pool_bwd(dx1, w_out, pooled, w_pool, scale, deps):
    s = dx1.shape[0]
    tm = min(ROW_TILE, s)
    n_t = s // tm

    def body(dx_ref, wo_ref, pl_ref, w_ref, sc_ref, dz_ref, dw_ref, slot_ref, ext, do_ref):
        i = pl.program_id(0)
        tile = n_t - 1 - i
        _zero_slot(slot_ref)
        do_ref[...] = _mm_nt(dx_ref[...], wo_ref[HW:2 * HW, :])

        @pl.when(i == 0)
        def _():
            dw_ref[...] = jnp.zeros_like(dw_ref)
            ext[tm:tm + POOL_HALO, :] = jnp.zeros((POOL_HALO, HW), F32)

        @pl.when(i > 0)
        def _():
            ext[tm:tm + POOL_HALO, :] = ext[0:POOL_HALO, :]

        inv = _pool_counts(tile, tm)
        dpooled = []
        for g in range(HEADS):
            sl = slice(HD * g, HD * (g + 1))
            pooled_g = pl_ref[:, sl]
            do = do_ref[:, sl]
            slot_ref[0:1, sl] += jnp.sum(_mm(pooled_g, w_ref[g]) * do, axis=0, keepdims=True)
            dy = (do * sc_ref[:, sl]).astype(BF16)
            dw_ref[g] += _mm_tn(pooled_g, dy)
            dpo = _mm_nt(dy, w_ref[g])
            dpooled.append(dpo)
            ext[0:tm, sl] = dpo * inv[g]
        for g, w in enumerate(POOL_WINDOWS):
            sl = slice(HD * g, HD * (g + 1))
            win = ext[0:tm, sl]
            for d in range(1, w):
                win = win + ext[d:d + tm, sl]
            dz_ref[:, sl] = (win - dpooled[g]).astype(BF16)

    return _call_behind(
        deps, body, name="pool_bwd", grid=(n_t,),
        out_shape=[jax.ShapeDtypeStruct((s, IN_WIDTH), BF16), jax.ShapeDtypeStruct((HEADS, HD, HD), F32),
                   jax.ShapeDtypeStruct((SLOT, D_MODEL), F32)],
        in_specs=[pl.BlockSpec((tm, D_MODEL), lambda i: (n_t - 1 - i, 0)), VMEM_WHOLE,
                  pl.BlockSpec((tm, HW), lambda i: (n_t - 1 - i, 0)), _full((HEADS, HD, HD)), _full((1, HW))],
        out_specs=[pl.BlockSpec((tm, HW), lambda i: (n_t - 1 - i, 4)), _full((HEADS, HD, HD)), _full((SLOT, D_MODEL))],
        scratch_shapes=[pltpu.VMEM((tm + POOL_HALO, HW), F32), pltpu.VMEM((tm, HW), F32)],
        compiler_params=_params(("arbitrary",)),
    )(dx1, w_out, pooled, w_pool, scale)


def _hgrn_bwd(z, o, dx1, w_out, states, lb_logits, gn, dz_in, deps):
    s = z.shape[0]
    n_chunks = s // CHUNK

    def body(zq_ref, zf_ref, zi_ref, zg_ref, o_ref, dx_ref, wo_ref, st_ref, lbl_ref, gn_ref, dzin_ref,
             dz_ref, dlb_ref, dgn_ref, dstate, b_scr, dlb_acc, do_ref):
        i = pl.program_id(0)

        @pl.when(i == 0)
        def _():
            dstate[...] = jnp.zeros_like(dstate)
            dlb_acc[...] = jnp.zeros_like(dlb_acc)
            dgn_ref[...] = jnp.zeros_like(dgn_ref)
            dlb_ref[...] = jnp.zeros_like(dlb_ref)

        do_ref[...] = _mm_nt(dx_ref[...], wo_ref[0:HW, :])
        lb = _sigmoid(lbl_ref[0:1, :] - lbl_ref[1:2, :])
        row, col = _chunk_masks()
        causal = col <= row
        tri = _ones_where(causal)
        upper = _ones_where(col >= row)
        strict_lower = _ones_where(col < row)
        in_sub, in_sub_head = _sub_chunk_masks(HW), _sub_chunk_masks(HD)
        gn_row = _lanes([gn_ref[h:h + 1, :] for h in range(HEADS)])
        sums = {"dlb": 0.0, "dgn": 0.0}

        def front(c):
            r0 = CHUNK * c
            rs = slice(r0, r0 + CHUNK)
            p = {"rs": rs}
            p["zq"] = zq_ref[rs, :]
            p["q"], p["sq"], p["sig"], p["f"] = _hgrn_gates(p["zq"], zf_ref[rs, :], lb)
            p["kk"] = 1.0 - p["f"]
            b = _tri_dot(tri, jnp.log(p["f"]), 3)
            b_scr[rs, :] = b
            p["v"] = zi_ref[rs, :]
            o, zg, doa = o_ref[rs, :], zg_ref[rs, :], do_ref[rs, :]
            sg = _sigmoid(zg)
            rms = _head_rms(o)
            n = o * rms
            don = doa * (zg * sg)
            sums["dgn"] = sums["dgn"] + jnp.sum(don * n, axis=0, keepdims=True)
            dn = don * gn_row
            p["d_o"] = rms * (dn - n * _head_mean(dn * n))
            dz_ref[rs, 3 * HW:4 * HW] = (doa * (n * gn_row) * (sg * (1.0 + zg * (1.0 - sg)))).astype(BF16)
            p["eq"], p["ek"] = _hgrn_decay_factors(b_scr, r0, b, in_sub)
            b_last = b_scr[r0 + CHUNK - 1:r0 + CHUNK, :]
            p["lam"], p["e_last"], p["lam_last"] = jnp.exp(b), jnp.exp(b_last - b), jnp.exp(b_last)
            p["qe"], p["qg"], p["kd"] = p["q"] * p["eq"], p["q"] * p["lam"], p["kk"] * p["e_last"]
            p["ke"] = [p["kk"] * e for e in p["ek"]]
            p["q16"] = [_per_sub_chunk(_head(p["qe"], h), in_sub_head).astype(BF16) for h in range(HEADS)]
            p["ke16"] = [_lanes([_head(p["ke"][j], h) for j in range(N_SUB)]).astype(BF16) for h in range(HEADS)]
            return p

        def recurrence(c, p):
            m = {k: [] for k in ("dv", "gq", "gk", "dqi", "dkd", "st")}
            a, da, dv_state = [], [], []
            for h in range(HEADS):
                vh, doh = _head(p["v"], h), _head(p["d_o"], h)
                st0, ds1 = st_ref[c, h], dstate[h]
                a.append(jnp.where(causal, _mm_nt(p["q16"][h], p["ke16"][h]), 0.0))
                da.append(jnp.where(causal, _mm_nt(doh, vh), 0.0))
                dv_state.append(_mm_nt(_head(p["kd"], h), ds1))
                m["dqi"].append(_mm(doh, st0))
                m["dkd"].append(_mm(vh, ds1))
                m["st"].append(jnp.sum(st0 * ds1, axis=0, keepdims=True))
                dstate[h] = ds1 * _head(p["lam_last"], h) + _mm_tn(doh, _head(p["qg"], h))
            for h in range(HEADS):
                m["dv"].append(_mm_tn(a[h], _head(p["d_o"], h)) + dv_state[h])
                m["gq"].append(_own_lane_block(_mm(da[h], p["ke16"][h]), in_sub_head))
                m["gk"].append(_mm_tn(da[h], p["q16"][h]))
            return m

        def back(p, m):
            rs = p["rs"]
            dz_ref[rs, 2 * HW:3 * HW] = _lanes(m["dv"]).astype(BF16)
            gq = _lanes(m["gq"])
            gk = [_lanes([m["gk"][h][:, HD * j:HD * (j + 1)] for h in range(HEADS)]) for j in range(N_SUB)]
            dq_inter = p["lam"] * _lanes(m["dqi"])
            dq = p["eq"] * gq + dq_inter
            dk_intra = sum(p["ek"][j] * gk[j] for j in range(N_SUB))
            dk_state = _lanes(m["dkd"]) * p["e_last"]
            db_intra = (p["qe"].astype(BF16).astype(F32) * gq
                        - sum(p["ke"][j].astype(BF16).astype(F32) * gk[j] for j in range(N_SUB)))
            dlf = (_tri_dot(upper, db_intra + p["q"] * dq_inter, 2) + _tri_dot(strict_lower, p["kk"] * dk_state, 2)
                   + p["lam_last"] * _lanes(m["st"]))
            sig, sq, zq = p["sig"], p["sq"], p["zq"]
            df = dlf / p["f"] - (dk_intra + dk_state)
            sums["dlb"] = sums["dlb"] + jnp.sum(df * (1.0 - sig), axis=0, keepdims=True)
            dz_ref[rs, HW:2 * HW] = (df * (1.0 - lb) * sig * (1.0 - sig)).astype(BF16)
            dz_ref[rs, 0:HW] = (dq * (sq * (1.0 + zq * (1.0 - sq)))).astype(BF16)

        p = front(CHUNKS_PER_STEP - 1)
        for c in reversed(range(CHUNKS_PER_STEP)):
            m = recurrence(c, p)
            p_next = front(c - 1) if c > 0 else None
            back(p, m)
            p = p_next
        dlb_acc[...] += sums["dlb"]
        for h in range(HEADS):
            dgn_ref[h:h + 1, 0:HD] += _head(sums["dgn"], h)

        @pl.when(i == n_steps - 1)
        def _():
            dl0 = dlb_acc[...] * lb * (1.0 - lb)
            dlb_ref[0:1, 0:HW] = dl0
            dlb_ref[1:2, 0:HW] = -dl0

    rows = CHUNK * CHUNKS_PER_STEP
    n_steps = s // rows
    rev = lambda i: n_steps - 1 - i
    zspec = lambda cb: pl.BlockSpec((rows, HW), lambda i, cb=cb: (rev(i), cb))
    slot = jax.ShapeDtypeStruct((SLOT, D_MODEL), F32)
    return _call_behind(
        deps, body, name="hgrn_bwd", grid=(n_steps,),
        out_shape=[jax.ShapeDtypeStruct((s, IN_WIDTH), BF16), slot, slot],
        in_specs=[zspec(0), zspec(1), zspec(2), zspec(3), pl.BlockSpec((rows, HW), lambda i: (rev(i), 0)),
                  pl.BlockSpec((rows, D_MODEL), lambda i: (rev(i), 0)), VMEM_WHOLE,
                  pl.BlockSpec((CHUNKS_PER_STEP, HEADS, HD, HD), lambda i: (rev(i), 0, 0, 0)), _full((2, HW)),
                  _full((HEADS, HD)), ANY_SPACE],
        out_specs=[pl.BlockSpec((rows, 4 * HW), lambda i: (rev(i), 0)), _full((SLOT, D_MODEL)), _full((SLOT, D_MODEL))],
        scratch_shapes=[pltpu.VMEM((HEADS, HD, HD), F32), pltpu.VMEM((rows, HW), F32), pltpu.VMEM((1, HW), F32),
                        pltpu.VMEM((rows, HW), F32)],
        input_output_aliases={10: 0},
        compiler_params=_params(("arbitrary",)),
    )(z, z, z, z, o, dx1, w_out, states, lb_logits, gn, dz_in)


def _in_bwd(dz, w_t, x0, g, dx1, deps):
    s = x0.shape[0]
    tm = min(WIDE_ROW_TILE, s)

    def body(dz_ref, w_ref, x_ref, g_ref, d_ref, dx_ref, slot_ref):
        _zero_slot(slot_ref)
        dx, dg = _rms_bwd(x_ref[...], g_ref[...], _mm(dz_ref[...], w_ref[...]))
        dx_ref[...] = d_ref[...] + dx
        slot_ref[0:1, :] += dg

    row_f32 = pl.BlockSpec((tm, D_MODEL), lambda i: (i, 0))
    return _call_behind(
        deps, body, name="in_bwd", grid=(s // tm,),
        out_shape=[jax.ShapeDtypeStruct((s, D_MODEL), F32), jax.ShapeDtypeStruct((SLOT, D_MODEL), F32)],
        in_specs=[pl.BlockSpec((tm, IN_WIDTH), lambda i: (i, 0)), VMEM_WHOLE, row_f32, _full((1, D_MODEL)), row_f32],
        out_specs=[row_f32, _full((SLOT, D_MODEL))],
        compiler_params=_params(("arbitrary",)),
    )(dz, w_t, x0, g, dx1)


def kernel(x, mem, norm_mix_g, w_in, lb_logits, hgrn_norm_g, w_pool, pool_scale, w_out, norm_x_g, norm_mem_g, w_xq, w_xk, w_xv, w_xo, norm_ffn_g, w_ff1, w_ff2, final_norm_g, loss_target, m_norm_mix_g, m_w_in, m_lb_logits, m_hgrn_norm_g, m_w_pool, m_pool_scale, m_w_out, m_norm_x_g, m_norm_mem_g, m_w_xq, m_w_xk, m_w_xv, m_w_xo, m_norm_ffn_g, m_w_ff1, m_w_ff2, m_final_norm_g, v_norm_mix_g, v_w_in, v_lb_logits, v_hgrn_norm_g, v_w_pool, v_pool_scale, v_w_out, v_norm_x_g, v_norm_mem_g, v_w_xq, v_w_xk, v_w_xv, v_w_xo, v_norm_ffn_g, v_w_ff1, v_w_ff2, v_final_norm_g):
    x0 = x[0]
    mem0 = mem[0]
    tgt = loss_target[0]
    gn = hgrn_norm_g[0]
    gfin = final_norm_g.reshape(1, D_MODEL)
    wp = w_pool[0]
    heads_2d = lambda w: w.reshape(D_MODEL // N_DEV, D_MODEL)
    xo_2d = lambda w: w.reshape(D_MODEL, D_MODEL // N_DEV)

    first = _all_gather_weights([w_in[0].T], [w_out[0], heads_2d(w_xq), heads_2d(w_xk), heads_2d(w_xv), xo_2d(w_xo).T,
                                              w_ff1[0], w_ff2[0]])
    win_t = first[0].reshape(IN_WIDTH, D_MODEL)
    ga_attn, ga_mlp = _gather_first_start([first[1:6], first[6:8]], "gather_first_start")

    z, h = _in_proj(x0, norm_mix_g, win_t, deps=[ga_attn[3]])
    mixed_a, o_pre, states = _hgrn_fwd(z, lb_logits, gn)
    lands = _split_wait(_gather_first_copies, ga_attn, o_pre, "gather_attn_first_wait")
    gb_attn = _gather_forward_start(lands, "gather_attn_forward_start")
    mixed, pooled = _pool_fwd(z, wp, pool_scale, mixed_a, deps=[gb_attn[3]])
    lands = _split_wait(_gather_forward_copies, gb_attn, pooled, "gather_attn_forward_wait")
    wout_f, wq_f, wk_f, wv_f, wo_t = (t.reshape(D_MODEL, D_MODEL) for t in lands)
    hm, xk, xv = _mem_kv(mem0, norm_mem_g, wk_f, wv_f, deps=[])
    x1, x2, hq, xq, att = _mix_xattn_fwd(x0, mixed, wout_f, norm_x_g, wq_f, xk, xv, wo_t, deps=[])
    lands = _split_wait(_gather_first_copies, ga_mlp, x2, "gather_mlp_first_wait")
    gb_mlp = _gather_forward_start(lands, "gather_mlp_forward_start")
    w1_b, w2_b = _split_wait(_gather_forward_copies, gb_mlp, gb_mlp[3], "gather_mlp_forward_wait")
    dx3, dx3_16, u, hf, slot_fin = _mlp_fwd_loss(x2, norm_ffn_g, w1_b, w2_b.reshape(D_FF, D_MODEL), gfin, tgt)

    rows = lambda t, r: t.reshape(N_DEV, r, D_MODEL)
    dw2 = _wgrad(u, dx3_16, "wgrad_ff2")
    ex_ff2 = _all_to_all_start([rows(dw2, FF_BLK)], [], "exchange_ff2_start")
    da, dx2, slot_ffn = _mlp_bwd(dx3, u, x2, norm_ffn_g, w1_b, w2_b, deps=[ex_ff2[3]])
    dw1 = _wgrad(hf, da, "wgrad_ff1", col_blocks=True)
    ex_ff1 = _all_to_all_start([dw1], [], "exchange_ff1_start")
    dx1, dx1_16, dxq, dxk, dxv, slot_x = _xattn_bwd(dx2, x1, norm_x_g, xq, xk, xv, wq_f, wo_t, deps=[ex_ff1[3]])
    dwo_t = _wgrad(dx2, att, "wgrad_xo")
    dwq = _wgrad(hq, dxq, "wgrad_xq")
    dwk, dwv, slot_mem = _mem_bwd(mem0, norm_mem_g, hm, dxk, dxv, wk_f, wv_f)
    ex_attn = _all_to_all_start([rows(dwq, 128), rows(dwk, 128), rows(dwv, 128), rows(dwo_t, 128)], [],
                                "exchange_attn_start")
    dwout = _wgrad(mixed, dx1_16, "wgrad_out")
    dz_pool, d_wpool, slot_ps = _pool_bwd(dx1_16, wout_f, pooled, wp, pool_scale, deps=[ex_attn[3]])
    small0 = jnp.concatenate([slot_x, slot_mem, slot_ffn, slot_fin, slot_ps], axis=0)
    ex_out = _all_to_all_start([rows(dwout, 128)], [small0, d_wpool], "exchange_out_start")
    dz, slot_lb, slot_gn = _hgrn_bwd(z, o_pre, dx1_16, wout_f, states, lb_logits, gn, dz_pool, deps=[ex_out[3]])
    (r_2,) = _split_wait(_all_to_all_copies(1), ex_ff2, dz, "exchange_ff2_wait")
    dwin_t, ff2_update = _wgrad(dz, h, "wgrad_in", update=(r_2, w_ff2[0], m_w_ff2[0], v_w_ff2[0]))
    ex_in = _all_to_all_start([rows(dwin_t, 320)], [], "exchange_in_start")
    grad_x, slot_mix = _in_bwd(dz, win_t, x0, norm_mix_g, dx1, deps=[ex_in[3]])
    small1 = jnp.concatenate([slot_mix, slot_lb, slot_gn], axis=0)
    ex_mix = _all_to_all_start([], [small1], "exchange_mix_start")

    out = {}
    out["w_ff2"] = ff2_update
    (r_1,) = _split_wait(_all_to_all_copies(1), ex_ff1, ex_mix[3], "exchange_ff1_wait")
    out["w_ff1"] = _sum_adamw(r_1, w_ff1[0], m_w_ff1[0], v_w_ff1[0], "adamw_ff1")
    r_q, r_k, r_v, r_o = _split_wait(_all_to_all_copies(4), ex_attn, out["w_ff1"][1], "exchange_attn_wait")
    sums = _sum_sources_whole([r_q, r_k, r_v, r_o], "sum_grad_attn")
    g_attn = [g.reshape(w_xq.shape) for g in sums[:3]] + [sums[3].T]
    attn = _adamw_whole([(g_attn[0], w_xq, m_w_xq, v_w_xq), (g_attn[1], w_xk, m_w_xk, v_w_xk),
                         (g_attn[2], w_xv, m_w_xv, v_w_xv),
                         (g_attn[3], xo_2d(w_xo), xo_2d(m_w_xo), xo_2d(v_w_xo))], "adamw_attn")
    for n, g, res in zip(("w_xq", "w_xk", "w_xv", "w_xo"), g_attn, attn):
        out[n] = (g, *res)
    r_out, r_small0, r_wpool = _split_wait(_all_to_all_copies(1), ex_out, attn[3][0], "exchange_out_wait")
    out["w_out"] = _sum_adamw(r_out, w_out[0], m_w_out[0], v_w_out[0], "adamw_out")
    (r_in,) = _split_wait(_all_to_all_copies(1), ex_in, out["w_out"][1], "exchange_in_wait")
    in_t = _sum_adamw(r_in, w_in[0].T, m_w_in[0].T, v_w_in[0].T, "adamw_in")
    out["w_in"] = tuple(t.T for t in in_t)
    (r_small1,) = _split_wait(_all_to_all_copies(0), ex_mix, in_t[1], "exchange_mix_wait")
    row = lambda t: t.reshape(1, -1)
    small_params = {
        "norm_mix_g": (norm_mix_g, m_norm_mix_g, v_norm_mix_g),
        "lb_logits": (lb_logits, m_lb_logits, v_lb_logits),
        "hgrn_norm_g": (hgrn_norm_g[0], m_hgrn_norm_g[0], v_hgrn_norm_g[0]),
        "pool_scale": (pool_scale, m_pool_scale, v_pool_scale),
        "norm_x_g": (norm_x_g, m_norm_x_g, v_norm_x_g),
        "norm_mem_g": (norm_mem_g, m_norm_mem_g, v_norm_mem_g),
        "norm_ffn_g": (norm_ffn_g, m_norm_ffn_g, v_norm_ffn_g),
        "final_norm_g": (row(final_norm_g), row(m_final_norm_g), row(v_final_norm_g)),
        "w_pool": (wp, m_w_pool[0], v_w_pool[0]),
    }
    loss, small_out = _small_update([r_small0, r_small1], r_wpool, small_params)
    out.update(small_out)

    shapes = dict(norm_mix_g=norm_mix_g, w_in=w_in, lb_logits=lb_logits, hgrn_norm_g=hgrn_norm_g, w_pool=w_pool,
                  pool_scale=pool_scale, w_out=w_out, norm_x_g=norm_x_g, norm_mem_g=norm_mem_g, w_xq=w_xq, w_xk=w_xk,
                  w_xv=w_xv, w_xo=w_xo, norm_ffn_g=norm_ffn_g, w_ff1=w_ff1, w_ff2=w_ff2, final_norm_g=final_norm_g)
    order = list(shapes)
    group = lambda k: [out[n][k].reshape(shapes[n].shape) for n in order]
    return (loss.reshape(()), grad_x.reshape(x.shape), *group(0), *group(1), *group(2), *group(3))
```

```python
import jax
import jax.numpy as jnp
from jax import lax
from jax.experimental import pallas as pl
from jax.experimental.pallas import tpu as pltpu

F32 = jnp.float32
BF16 = jnp.bfloat16

D_MODEL = 1024
N_DEV = 8
HEADS = 4
HD = 128
HW = HEADS * HD
IN_WIDTH = 5 * HW
XHD = 256
MEM_LEN = 256
D_FF = 4096
FF_BLK = D_FF // N_DEV
POOL_WINDOWS = (2, 4, 8, 16)
POOL_HALO = 16
CHUNK = 64
CHUNKS_PER_STEP = 8
SUB = 16
N_SUB = CHUNK // SUB
EXP_CAP = 80.0
EPS = 1e-6
TINY = 1e-30
ROW_TILE = 512
WIDE_ROW_TILE = 1024
SLOT = 8
V7X_VMEM_LIMIT = 56 * 1024 * 1024

ADAM_LR = 0.001
ADAM_B1 = 0.9
ADAM_B2 = 0.999
ADAM_EPS = 1e-08
ADAM_WD = 0.01
ADAM_STEP = 10

MESH_ID = pl.DeviceIdType.MESH


def _params(sem=None, vmem=V7X_VMEM_LIMIT):
    return pltpu.CompilerParams(dimension_semantics=sem, vmem_limit_bytes=vmem)


def _mm(a, b):
    return lax.dot_general(a.astype(BF16), b.astype(BF16), (((1,), (0,)), ((), ())), preferred_element_type=F32)


def _mm_nt(a, b):
    return lax.dot_general(a.astype(BF16), b.astype(BF16), (((1,), (1,)), ((), ())), preferred_element_type=F32)


def _mm_tn(a, b):
    return lax.dot_general(a.astype(BF16), b.astype(BF16), (((0,), (0,)), ((), ())), preferred_element_type=F32)


def _sigmoid(x):
    return 1.0 / (1.0 + jnp.exp(-x))


def _rms(x):
    return lax.rsqrt(jnp.mean(x * x, axis=-1, keepdims=True) + EPS)


def _rms_bwd(x, g, dh):
    r = _rms(x)
    n = x * r
    dn = dh * g
    dx = r * (dn - n * jnp.mean(dn * n, axis=-1, keepdims=True))
    return dx, jnp.sum(dh * n, axis=0, keepdims=True)


def _tri_dot(tri, x, passes):
    acc = None
    rest = x
    for _ in range(passes):
        piece = rest.astype(BF16)
        part = lax.dot_general(tri, piece, (((1,), (0,)), ((), ())), preferred_element_type=F32)
        acc = part if acc is None else acc + part
        rest = rest - piece.astype(F32)
    return acc


def _adam_update(g, w, m, v):
    nm = ADAM_B1 * m + (1.0 - ADAM_B1) * g
    nv = ADAM_B2 * v + (1.0 - ADAM_B2) * (g * g)
    m_hat = nm / (1.0 - ADAM_B1 ** ADAM_STEP)
    v_hat = nv / (1.0 - ADAM_B2 ** ADAM_STEP)
    return -ADAM_LR * (m_hat / (jnp.sqrt(v_hat) + ADAM_EPS) + ADAM_WD * w), nm, nv


def _full(shape):
    return pl.BlockSpec(shape, lambda *_: (0,) * len(shape))


VMEM_WHOLE = pl.BlockSpec(memory_space=pltpu.VMEM)
ANY_SPACE = pl.BlockSpec(memory_space=pl.ANY)


def _mesh_pos():
    return lax.axis_index("x"), lax.axis_index("y"), lax.axis_index("c")


def _flat(px, py, pc):
    return 4 * px + 2 * py + pc


def _all_gather_weights(shards, cast_only):
    n, nc = len(shards), len(cast_only)
    step = 64

    def body(*refs):
        x_refs, c_refs = refs[:n], refs[n:n + nc]
        out_refs, cast_refs = refs[n + nc:2 * n + nc], refs[2 * n + nc:2 * n + 2 * nc]
        bufs = refs[2 * n + 2 * nc:3 * n + 2 * nc]
        send_sems, recv_sems, local_sems = refs[3 * n + 2 * nc:]
        _handshake(_peers_first_level())
        x, y, c = _mesh_pos()
        me, sibling = (x, y, c), (x, y, 1 - c)
        chips = [(1 - x, y), (x, 1 - y), (1 - x, 1 - y)]

        def copy(a, k, blk, to, src=None):
            rows = out_refs[a].at[_flat(*blk)]
            return pltpu.make_async_remote_copy(
                src_ref=rows if src is None else src, dst_ref=rows,
                send_sem=send_sems.at[7 * a + k], recv_sem=recv_sems.at[7 * a + k], device_id=to, device_id_type=MESH_ID)

        def cast_rows(src, dst, rows):
            def cast(i, carry):
                r0 = pl.multiple_of(i * step, step)
                dst[pl.ds(r0, step), :] = src[pl.ds(r0, step), :].astype(BF16)
                return carry
            lax.fori_loop(0, rows // step, cast, 0)

        first, mine = [], []
        for a in range(n):
            cast_rows(x_refs[a], bufs[a], shards[a].shape[0])
            mine.append(pltpu.make_async_copy(bufs[a], out_refs[a].at[_flat(*me)], local_sems.at[a]))
            first.append(copy(a, 0, me, sibling, src=bufs[a]))
            first += [copy(a, 1 + j, me, (*chip, c), src=bufs[a]) for j, chip in enumerate(chips)]
            for cp in [mine[-1]] + first[-4:]:
                cp.start()
        for a in range(nc):
            cast_rows(c_refs[a], cast_refs[a], cast_only[a].shape[0])
        passed = []
        for j, chip in enumerate(chips):
            for a in range(n):
                copy(a, 1 + j, (*chip, c), me).wait_recv()
                passed.append(copy(a, 4 + j, (*chip, c), sibling))
                passed[-1].start()
        for a in range(n):
            copy(a, 0, sibling, me).wait_recv()
            for j, chip in enumerate(chips):
                copy(a, 4 + j, (*chip, 1 - c), me).wait_recv()
        for cp in first + passed:
            cp.wait_send()
        for cp in mine:
            cp.wait()

    return pl.pallas_call(
        body, name="all_gather_w_in",
        out_shape=[jax.ShapeDtypeStruct((N_DEV,) + s.shape, BF16) for s in shards]
        + [jax.ShapeDtypeStruct(s.shape, BF16) for s in cast_only],
        in_specs=[VMEM_WHOLE] * (n + nc), out_specs=[ANY_SPACE] * n + [VMEM_WHOLE] * nc,
        scratch_shapes=[pltpu.VMEM(s.shape, BF16) for s in shards]
        + [pltpu.SemaphoreType.DMA((7 * n,)), pltpu.SemaphoreType.DMA((7 * n,)), pltpu.SemaphoreType.DMA((n,))],
        compiler_params=pltpu.CompilerParams(vmem_limit_bytes=V7X_VMEM_LIMIT, collective_id=GATHER_W_IN_ID),
    )(*shards, *cast_only)


HBM_SPEC = pl.BlockSpec(memory_space=pltpu.HBM)
SEM_SPEC = pl.BlockSpec(memory_space=pltpu.SEMAPHORE)
EFFECT = pltpu.SideEffectType.DATAFLOW_SIDE_EFFECTING
TOKEN = jax.ShapeDtypeStruct((8, 128), F32)


def _in_hbm(a):
    return pltpu.with_memory_space_constraint(a, pltpu.HBM)


START_IDS = {name: i for i, name in enumerate((
    "gather_first_start", "gather_attn_forward_start", "gather_mlp_forward_start", "exchange_ff2_start",
    "exchange_ff1_start", "exchange_attn_start", "exchange_out_start", "exchange_in_start", "exchange_mix_start"))}


GATHER_W_IN_ID = len(START_IDS)


def _handshake(peers):
    barrier = pltpu.get_barrier_semaphore()
    for peer in peers:
        pl.semaphore_signal(barrier, inc=1, device_id=peer, device_id_type=MESH_ID)
    pl.semaphore_wait(barrier, len(peers))


def _peers_all():
    x, y, c = _mesh_pos()
    return [(1 - x if k & 4 else x, 1 - y if k & 2 else y, 1 - c if k & 1 else c) for k in range(1, N_DEV)]


def _peers_first_level():
    x, y, c = _mesh_pos()
    return [(x, y, 1 - c), (1 - x, y, c), (x, 1 - y, c), (1 - x, 1 - y, c)]


def _peers_sibling():
    x, y, c = _mesh_pos()
    return [(x, y, 1 - c)]


def _split_start(copies_of, srcs, lands, n_sems, name, peers_of, collective_id):
    ns, nl, k = len(srcs), len(lands), len(n_sems)

    def body(*refs):
        _handshake(peers_of())
        src_refs, land_refs = refs[:ns], refs[ns:ns + nl]
        sems = refs[ns + nl:ns + nl + k]
        token = refs[-1]
        for cp in copies_of(src_refs, land_refs, sems):
            cp.start()
        token[...] = jnp.zeros_like(token)

    outs = pl.pallas_call(
        body, name=name,
        out_shape=[pltpu.SemaphoreType.DMA((q,)) for q in n_sems]
        + [pltpu.HBM(a.shape, a.dtype) for a in list(srcs) + list(lands)] + [TOKEN],
        in_specs=[HBM_SPEC] * (ns + nl),
        out_specs=[SEM_SPEC] * k + [HBM_SPEC] * (ns + nl) + [VMEM_WHOLE],
        input_output_aliases={i: k + i for i in range(ns + nl)},
        compiler_params=pltpu.CompilerParams(has_side_effects=EFFECT, collective_id=collective_id),
    )(*[_in_hbm(a) for a in list(srcs) + list(lands)])
    return outs[:k], outs[k:k + ns], outs[k + ns:k + ns + nl], outs[-1]


def _split_wait(copies_of, handle, after, name):
    sems, srcs, lands, _ = handle
    ns, nl, k = len(srcs), len(lands), len(sems)

    def body(*refs):
        src_refs, land_refs = refs[:ns], refs[ns:ns + nl]
        sem_refs = refs[ns + nl:ns + nl + k]
        for cp in copies_of(src_refs, land_refs, sem_refs):
            cp.wait()

    outs = pl.pallas_call(
        body, name=name,
        out_shape=[pltpu.HBM(a.shape, a.dtype) for a in list(srcs) + list(lands)],
        in_specs=[HBM_SPEC] * (ns + nl) + [SEM_SPEC] * k + [ANY_SPACE],
        out_specs=[HBM_SPEC] * (ns + nl),
        input_output_aliases={i: i for i in range(ns + nl)},
        compiler_params=pltpu.CompilerParams(has_side_effects=EFFECT),
    )(*srcs, *lands, *sems, after)
    return outs[ns:]


def _gather_first_copies(shard_refs, land_refs, sems):
    send_sems, recv_sems, local_sems = sems
    x, y, c = _mesh_pos()
    me = _flat(x, y, c)
    peers = [(x, y, 1 - c), (1 - x, y, c), (x, 1 - y, c), (1 - x, 1 - y, c)]
    copies = []
    for a, (shard, land) in enumerate(zip(shard_refs, land_refs)):
        copies.append(pltpu.make_async_copy(shard, land.at[me], local_sems.at[a]))
        for k, peer in enumerate(peers):
            copies.append(pltpu.make_async_remote_copy(
                src_ref=shard, dst_ref=land.at[me], send_sem=send_sems.at[4 * a + k], recv_sem=recv_sems.at[4 * a + k],
                device_id=peer, device_id_type=MESH_ID))
    return copies


def _gather_forward_copies(src_refs, land_refs, sems):
    del src_refs
    send_sems, recv_sems = sems
    x, y, c = _mesh_pos()
    chips = [(1 - x, y), (x, 1 - y), (1 - x, 1 - y)]
    copies = []
    for a, land in enumerate(land_refs):
        for j, chip in enumerate(chips):
            rows = land.at[_flat(*chip, c)]
            copies.append(pltpu.make_async_remote_copy(
                src_ref=rows, dst_ref=rows, send_sem=send_sems.at[3 * a + j], recv_sem=recv_sems.at[3 * a + j],
                device_id=(x, y, 1 - c), device_id_type=MESH_ID))
    return copies


def _gather_first_start(groups, name):
    shards = [s for g in groups for s in g]
    lands = [lax.empty((N_DEV,) + s.shape, s.dtype) for s in shards]
    bounds = [sum(len(g) for g in groups[:i]) for i in range(len(groups) + 1)]

    def copies_of(src_refs, land_refs, sems):
        copies = []
        for i in range(len(groups)):
            lo, hi = bounds[i], bounds[i + 1]
            copies += _gather_first_copies(src_refs[lo:hi], land_refs[lo:hi], sems[3 * i:3 * i + 3])
        return copies

    n_sems = tuple(q for g in groups for q in (4 * len(g), 4 * len(g), len(g)))
    sems, srcs, lands, token = _split_start(copies_of, shards, lands, n_sems, name, _peers_first_level, START_IDS[name])
    return [(sems[3 * i:3 * i + 3], srcs[bounds[i]:bounds[i + 1]], lands[bounds[i]:bounds[i + 1]], token)
            for i in range(len(groups))]


def _gather_forward_start(lands, name):
    n = len(lands)
    return _split_start(_gather_forward_copies, [], lands, (3 * n, 3 * n), name, _peers_sibling, START_IDS[name])


def _all_to_all_copies(n_scattered):
    def copies_of(src_refs, land_refs, sems):
        send_sems, recv_sems, local_sems = sems
        x, y, c = _mesh_pos()
        me = _flat(x, y, c)
        copies = []
        for a, (src, land) in enumerate(zip(src_refs, land_refs)):
            scattered = a < n_scattered
            copies.append(pltpu.make_async_copy(src.at[me] if scattered else src, land.at[me], local_sems.at[a]))
            for k in range(1, N_DEV):
                peer = (1 - x if k & 4 else x, 1 - y if k & 2 else y, 1 - c if k & 1 else c)
                copies.append(pltpu.make_async_remote_copy(
                    src_ref=src.at[_flat(*peer)] if scattered else src, dst_ref=land.at[me],
                    send_sem=send_sems.at[7 * a + k - 1], recv_sem=recv_sems.at[7 * a + k - 1],
                    device_id=peer, device_id_type=MESH_ID))
        return copies
    return copies_of


def _all_to_all_start(scattered, broadcast, name):
    srcs = list(scattered) + list(broadcast)
    lands = [lax.empty(a.shape, a.dtype) for a in scattered] + [lax.empty((N_DEV,) + a.shape, a.dtype) for a in broadcast]
    n = len(srcs)
    return _split_start(_all_to_all_copies(len(scattered)), srcs, lands, (7 * n, 7 * n, n), name, _peers_all,
                        START_IDS[name])


def _call_behind(deps, body, *, in_specs, **kwargs):
    n_in, n_dep = len(in_specs), len(deps)

    def body_without_deps(*refs):
        return body(*refs[:n_in], *refs[n_in + n_dep:])

    call = pl.pallas_call(body_without_deps, in_specs=list(in_specs) + [ANY_SPACE] * n_dep, **kwargs)
    return lambda *operands: call(*operands, *deps)


def _row_tile(rows):
    if rows <= 2 * 256:
        return rows
    for cand in (256, 128, 64, 32, 16):
        if rows % cand == 0:
            return cand
    return rows


def _adamw_whole(groups, name):
    n = len(groups)

    def body(*refs):
        for i in range(n):
            g_ref, w_ref, m_ref, v_ref = refs[4 * i:4 * i + 4]
            d_ref, nm_ref, nv_ref = refs[4 * n + 3 * i:4 * n + 3 * i + 3]
            d_ref[...], nm_ref[...], nv_ref[...] = _adam_update(g_ref[...], w_ref[...], m_ref[...], v_ref[...])

    outs = pl.pallas_call(
        body, name=name, out_shape=[jax.ShapeDtypeStruct(grp[0].shape, F32) for grp in groups for _ in range(3)],
        in_specs=[VMEM_WHOLE] * (4 * n), out_specs=[VMEM_WHOLE] * (3 * n),
        compiler_params=_params(),
    )(*[t for grp in groups for t in grp])
    return [outs[3 * i:3 * i + 3] for i in range(n)]


def _sum_sources_whole(recvs, name):
    n = len(recvs)

    def body(*refs):
        for r_ref, o_ref in zip(refs[:n], refs[n:]):
            acc = r_ref[0].astype(F32)
            for d in range(1, N_DEV):
                acc = acc + r_ref[d].astype(F32)
            o_ref[...] = acc

    return pl.pallas_call(
        body, name=name, out_shape=[jax.ShapeDtypeStruct(r.shape[1:], F32) for r in recvs],
        in_specs=[VMEM_WHOLE] * n, out_specs=[VMEM_WHOLE] * n,
        compiler_params=_params(),
    )(*recvs)


def _sum_adamw(recv, w, m, v, name):
    _, rows, cols = recv.shape
    tile = _row_tile(rows)

    def body(r_ref, w_ref, m_ref, v_ref, g_ref, d_ref, nm_ref, nv_ref):
        acc = r_ref[0].astype(F32)
        for d in range(1, N_DEV):
            acc = acc + r_ref[d].astype(F32)
        g_ref[...] = acc
        d_ref[...], nm_ref[...], nv_ref[...] = _adam_update(acc, w_ref[...], m_ref[...], v_ref[...])

    spec = pl.BlockSpec((tile, cols), lambda i: (i, 0))
    shp = jax.ShapeDtypeStruct((rows, cols), F32)
    return pl.pallas_call(
        body, name=name, grid=(rows // tile,), out_shape=[shp] * 4,
        in_specs=[pl.BlockSpec((N_DEV, tile, cols), lambda i: (0, i, 0)), spec, spec, spec], out_specs=[spec] * 4,
        compiler_params=_params(("parallel",)),
    )(recv, w, m, v)


SMALL_SLOTS = {"norm_x_g": (0, 0, 1, D_MODEL), "norm_mem_g": (0, 8, 1, D_MODEL), "norm_ffn_g": (0, 16, 1, D_MODEL),
               "final_norm_g": (0, 24, 1, D_MODEL), "pool_scale": (0, 32, 1, HW),
               "norm_mix_g": (1, 0, 1, D_MODEL), "lb_logits": (1, 8, 2, HW), "hgrn_norm_g": (1, 16, HEADS, HD)}
LOSS_ROW = 25
SMALL_ORDER = ("norm_mix_g", "lb_logits", "hgrn_norm_g", "pool_scale", "norm_x_g", "norm_mem_g", "norm_ffn_g",
               "final_norm_g", "w_pool")


def _small_update(srecvs, wprecv, params):
    flat = [t for n in SMALL_ORDER for t in params[n]]
    nb = len(srecvs)
    n_in = nb + 1 + len(flat)

    def body(*refs):
        s_refs, wp_ref = refs[0:nb], refs[nb]
        in_refs = refs[nb + 1:n_in]
        loss_ref = refs[n_in]
        out_refs = refs[n_in + 1:-nb]
        accs = refs[-nb:]
        for s_ref, acc in zip(s_refs, accs):
            total = s_ref[0]
            for d in range(1, N_DEV):
                total = total + s_ref[d]
            acc[...] = total
        loss_ref[...] = accs[0][LOSS_ROW:LOSS_ROW + 1, 0:1]
        for i, name in enumerate(SMALL_ORDER):
            w_ref, m_ref, v_ref = in_refs[3 * i:3 * i + 3]
            g_ref, d_ref, nm_ref, nv_ref = out_refs[4 * i:4 * i + 4]
            if name == "w_pool":
                g = wp_ref[0]
                for d in range(1, N_DEV):
                    g = g + wp_ref[d]
            else:
                buf, r0, nr, nc = SMALL_SLOTS[name]
                g = accs[buf][r0:r0 + nr, 0:nc]
            g_ref[...] = g
            d_ref[...], nm_ref[...], nv_ref[...] = _adam_update(g, w_ref[...], m_ref[...], v_ref[...])

    out_shape = [jax.ShapeDtypeStruct((1, 1), F32)]
    for n in SMALL_ORDER:
        out_shape += [jax.ShapeDtypeStruct(params[n][0].shape, F32)] * 4
    outs = pl.pallas_call(
        body, name="small_update", out_shape=out_shape,
        in_specs=[VMEM_WHOLE] * n_in, out_specs=[VMEM_WHOLE] * len(out_shape),
        scratch_shapes=[pltpu.VMEM(r.shape[1:], F32) for r in srecvs],
        compiler_params=_params(),
    )(*srecvs, wprecv, *flat)
    return outs[0], {n: outs[1 + 4 * i:5 + 4 * i] for i, n in enumerate(SMALL_ORDER)}


def _in_proj(x, g, w_t, deps):
    s = x.shape[0]
    tm = min(ROW_TILE, s)

    def body(x_ref, g_ref, w_ref, z_ref, h_ref):
        xv = x_ref[...]
        h = (xv * _rms(xv) * g_ref[...]).astype(BF16)
        h_ref[...] = h
        z_ref[...] = _mm_nt(h, w_ref[...])

    return _call_behind(
        deps, body, name="in_proj", grid=(s // tm,),
        out_shape=[jax.ShapeDtypeStruct((s, IN_WIDTH), F32), jax.ShapeDtypeStruct((s, D_MODEL), BF16)],
        in_specs=[pl.BlockSpec((tm, D_MODEL), lambda i: (i, 0)), _full((1, D_MODEL)), VMEM_WHOLE],
        out_specs=[pl.BlockSpec((tm, IN_WIDTH), lambda i: (i, 0)), pl.BlockSpec((tm, D_MODEL), lambda i: (i, 0))],
        compiler_params=_params(("parallel",)),
    )(x, g, w_t)


def _chunk_masks():
    row = lax.broadcasted_iota(jnp.int32, (CHUNK, CHUNK), 0)
    col = lax.broadcasted_iota(jnp.int32, (CHUNK, CHUNK), 1)
    return row, col


def _ones_where(mask):
    return jnp.where(mask, 1.0, 0.0).astype(BF16)


def _hgrn_gates(zq, zf, lb):
    sq = _sigmoid(zq)
    sig = _sigmoid(zf)
    f = lb + (1.0 - lb) * sig
    return zq * sq, sq, sig, f


def _sub_chunk_masks(width):
    trow = lax.broadcasted_iota(jnp.int32, (CHUNK, width), 0)
    return [(trow >= SUB * j) & (trow < SUB * (j + 1)) for j in range(N_SUB)]


def _head(a, h):
    return a[:, HD * h:HD * (h + 1)]


def _lanes(parts):
    return jnp.concatenate(parts, axis=1)


def _hgrn_decay_factors(b_scr, r0, b, in_sub):
    bases = [jnp.zeros((1, HW), F32)] + [b_scr[r0 + SUB * j - 1:r0 + SUB * j, :] for j in range(1, N_SUB)]
    own_base = bases[N_SUB - 1]
    for j in range(N_SUB - 2, -1, -1):
        own_base = jnp.where(in_sub[j], bases[j], own_base)
    eq = jnp.exp(b - own_base)
    ek = []
    for j in range(N_SUB):
        upto = SUB * (j + 1)
        e = jnp.exp(jnp.minimum(bases[j] - b[0:upto], EXP_CAP))
        ek.append(e if upto == CHUNK else jnp.concatenate([e, jnp.zeros((CHUNK - upto, HW), F32)], axis=0))
    return eq, ek


def _per_sub_chunk(x, in_sub):
    return _lanes([jnp.where(in_sub[j], x, 0.0) for j in range(N_SUB)])


def _own_lane_block(a, in_sub):
    out = a[:, HD * (N_SUB - 1):HD * N_SUB]
    for j in range(N_SUB - 2, -1, -1):
        out = jnp.where(in_sub[j], a[:, HD * j:HD * (j + 1)], out)
    return out


def _head_rms(o):
    return _lanes([jnp.broadcast_to(_rms(_head(o, h)), (CHUNK, HD)) for h in range(HEADS)])


def _head_mean(a):
    return _lanes([jnp.broadcast_to(jnp.mean(_head(a, h), axis=-1, keepdims=True), (CHUNK, HD)) for h in range(HEADS)])


def _hgrn_fwd(z, lb_logits, gn):
    s = z.shape[0]
    n_chunks = s // CHUNK

    def body(zq_ref, zf_ref, zi_ref, zg_ref, lbl_ref, gn_ref, oa_ref, o_ref, st_ref, state, b_scr):
        @pl.when(pl.program_id(0) == 0)
        def _():
            state[...] = jnp.zeros_like(state)

        lb = _sigmoid(lbl_ref[0:1, :] - lbl_ref[1:2, :])
        row, col = _chunk_masks()
        causal = col <= row
        tri = _ones_where(causal)
        in_sub, in_sub_head = _sub_chunk_masks(HW), _sub_chunk_masks(HD)
        gn_row = _lanes([gn_ref[h:h + 1, :] for h in range(HEADS)])
        def front(c):
            r0 = CHUNK * c
            rs = slice(r0, r0 + CHUNK)
            q, _, _, f = _hgrn_gates(zq_ref[rs, :], zf_ref[rs, :], lb)
            kk = 1.0 - f
            b = _tri_dot(tri, jnp.log(f), 3)
            b_scr[rs, :] = b
            eq, ek = _hgrn_decay_factors(b_scr, r0, b, in_sub)
            b_last = b_scr[r0 + CHUNK - 1:r0 + CHUNK, :]
            qe = q * eq
            return {"rs": rs, "v": zi_ref[rs, :], "qg": q * jnp.exp(b), "kd": kk * jnp.exp(b_last - b),
                    "lam_last": jnp.exp(b_last),
                    "q16": [_per_sub_chunk(_head(qe, h), in_sub_head).astype(BF16) for h in range(HEADS)],
                    "ke16": [_lanes([_head(kk * e, h) for e in ek]).astype(BF16) for h in range(HEADS)]}

        def recurrence(c, p):
            st_ref[c] = state[...]
            a, o_inter = [], []
            for h in range(HEADS):
                vh, st = _head(p["v"], h), state[h]
                a.append(jnp.where(causal, _mm_nt(p["q16"][h], p["ke16"][h]), 0.0))
                o_inter.append(_mm_nt(_head(p["qg"], h), st))
                state[h] = st * _head(p["lam_last"], h) + _mm_tn(vh, _head(p["kd"], h))
            return _lanes([_mm(a[h], _head(p["v"], h)) + o_inter[h] for h in range(HEADS)])

        def back(p, o):
            rs = p["rs"]
            o_ref[rs, :] = o
            zg = zg_ref[rs, :]
            oa_ref[rs, :] = (o * _head_rms(o) * gn_row * zg * _sigmoid(zg)).astype(BF16)

        p = front(0)
        for c in range(CHUNKS_PER_STEP):
            o = recurrence(c, p)
            p_next = front(c + 1) if c + 1 < CHUNKS_PER_STEP else None
            back(p, o)
            p = p_next

    rows = CHUNK * CHUNKS_PER_STEP
    zspec = lambda cb: pl.BlockSpec((rows, HW), lambda i, cb=cb: (i, cb))
    return pl.pallas_call(
        body, name="hgrn_fwd", grid=(s // rows,),
        out_shape=[jax.ShapeDtypeStruct((s, 2 * HW), BF16), jax.ShapeDtypeStruct((s, HW), F32),
                   jax.ShapeDtypeStruct((n_chunks, HEADS, HD, HD), F32)],
        in_specs=[zspec(0), zspec(1), zspec(2), zspec(3), _full((2, HW)), _full((HEADS, HD))],
        out_specs=[pl.BlockSpec((rows, HW), lambda i: (i, 0)), pl.BlockSpec((rows, HW), lambda i: (i, 0)),
                   pl.BlockSpec((CHUNKS_PER_STEP, HEADS, HD, HD), lambda i: (i, 0, 0, 0))],
        scratch_shapes=[pltpu.VMEM((HEADS, HD, HD), F32), pltpu.VMEM((rows, HW), F32)],
        compiler_params=_params(("arbitrary",)),
    )(z, z, z, z, lb_logits, gn)


def _pool_counts(tile_idx, tm):
    t = tile_idx * tm + lax.broadcasted_iota(jnp.int32, (tm, 1), 0)
    return [1.0 / jnp.minimum(t + 1, w).astype(F32) for w in POOL_WINDOWS]


def _pool_fwd(z, w_pool, scale, mixed_in, deps):
    s = z.shape[0]
    tm = min(ROW_TILE, s)

    def body(p_ref, w_ref, sc_ref, mixin_ref, ob_ref, pooled_ref, ext):
        i = pl.program_id(0)

        @pl.when(i == 0)
        def _():
            ext[0:POOL_HALO, :] = jnp.zeros((POOL_HALO, HW), F32)

        @pl.when(i > 0)
        def _():
            ext[0:POOL_HALO, :] = ext[tm:tm + POOL_HALO, :]

        ext[POOL_HALO:POOL_HALO + tm, :] = p_ref[...]
        inv = _pool_counts(i, tm)
        for g, w in enumerate(POOL_WINDOWS):
            sl = slice(HD * g, HD * (g + 1))
            p = ext[POOL_HALO:POOL_HALO + tm, sl]
            win = p
            for d in range(1, w):
                win = win + ext[POOL_HALO - d:POOL_HALO - d + tm, sl]
            pooled = (win * inv[g] - p).astype(BF16)
            pooled_ref[:, sl] = pooled
            ob_ref[:, sl] = (_mm(pooled, w_ref[g]) * sc_ref[:, sl]).astype(BF16)

    return _call_behind(
        deps, body, name="pool_fwd", grid=(s // tm,),
        out_shape=[jax.ShapeDtypeStruct((s, 2 * HW), BF16), jax.ShapeDtypeStruct((s, HW), BF16)],
        in_specs=[pl.BlockSpec((tm, HW), lambda i: (i, 4)), _full((HEADS, HD, HD)), _full((1, HW)), ANY_SPACE],
        out_specs=[pl.BlockSpec((tm, HW), lambda i: (i, 1)), pl.BlockSpec((tm, HW), lambda i: (i, 0))],
        scratch_shapes=[pltpu.VMEM((tm + POOL_HALO, HW), F32)],
        input_output_aliases={3: 0},
        compiler_params=_params(("arbitrary",)),
    )(z, w_pool, scale, mixed_in)


def _mem_kv(mem, g, wk, wv, deps):
    def body(m_ref, g_ref, wk_ref, wv_ref, hm_ref, k_ref, v_ref):
        m = m_ref[...]
        hm = (m * _rms(m) * g_ref[...]).astype(BF16)
        hm_ref[...] = hm
        k_ref[...] = _mm(hm, wk_ref[...]).astype(BF16)
        v_ref[...] = _mm(hm, wv_ref[...]).astype(BF16)

    shp = jax.ShapeDtypeStruct((MEM_LEN, D_MODEL), BF16)
    return _call_behind(
        deps, body, name="mem_kv", out_shape=[shp, shp, shp],
        in_specs=[VMEM_WHOLE] * 4, out_specs=[VMEM_WHOLE] * 3,
        compiler_params=_params(),
    )(mem, g, wk, wv)


def _softmax_rows(sc):
    e = jnp.exp(sc - jnp.max(sc, axis=-1, keepdims=True))
    return e / jnp.sum(e, axis=-1, keepdims=True)


def _mix_xattn_fwd(x0, mixed, w_out, g, wq, xk, xv, wo_t, deps):
    s = x0.shape[0]
    tm = min(ROW_TILE, s)
    scale = XHD ** -0.5

    def body(x_ref, mix_ref, wout_ref, g_ref, wq_ref, k_ref, v_ref, wo_ref, x1_ref, o_ref, hq_ref, q_ref, att_ref):
        xv_ = x_ref[...] + _mm(mix_ref[...], wout_ref[...])
        x1_ref[...] = xv_
        hq = (xv_ * _rms(xv_) * g_ref[...]).astype(BF16)
        hq_ref[...] = hq
        q_ref[...] = (_mm(hq, wq_ref[...]) * scale).astype(BF16)
        heads = [slice(XHD * h, XHD * (h + 1)) for h in range(HEADS)]
        scores = [_mm_nt(q_ref[:, sl], k_ref[:, sl]) for sl in heads]
        probs = [_softmax_rows(sc) for sc in scores]
        for sl, p in zip(heads, probs):
            att_ref[:, sl] = _mm(p, v_ref[:, sl]).astype(BF16)
        o_ref[...] = xv_ + _mm_nt(att_ref[...], wo_ref[...])

    row_f32 = pl.BlockSpec((tm, D_MODEL), lambda i: (i, 0))
    bshape = jax.ShapeDtypeStruct((s, D_MODEL), BF16)
    fshape = jax.ShapeDtypeStruct((s, D_MODEL), F32)
    return _call_behind(
        deps, body, name="mix_xattn_fwd", grid=(s // tm,),
        out_shape=[fshape, fshape, bshape, bshape, bshape],
        in_specs=[row_f32, row_f32, VMEM_WHOLE, _full((1, D_MODEL)), VMEM_WHOLE, VMEM_WHOLE, VMEM_WHOLE, VMEM_WHOLE],
        out_specs=[row_f32] * 5,
        compiler_params=_params(("parallel",)),
    )(x0, mixed, w_out, g, wq, xk, xv, wo_t)


def _mlp_fwd_loss(x, g, w1, w2, gf, target):
    s = x.shape[0]
    tm = min(ROW_TILE, s)

    def body(x_ref, g_ref, w1_ref, w2_ref, gf_ref, t_ref, dx_ref, dx16_ref, u_ref, hf_ref, slot_ref):
        @pl.when(pl.program_id(0) == 0)
        def _():
            slot_ref[...] = jnp.zeros_like(slot_ref)

        xv = x_ref[...]
        hf = (xv * _rms(xv) * g_ref[...]).astype(BF16)
        hf_ref[...] = hf
        a_next = _mm(hf, w1_ref[0])
        for j in range(N_DEV):
            a = jnp.maximum(a_next, 0.0)
            if j + 1 < N_DEV:
                a_next = _mm(hf, w1_ref[j + 1])
            u_ref[:, FF_BLK * j:FF_BLK * (j + 1)] = (a * a).astype(BF16)
        acc = xv + _mm(u_ref[...], w2_ref[...])
        gfv = gf_ref[...]
        r = _rms(acc)
        n = acc * r
        err = n * gfv - t_ref[...]
        slot_ref[1:2, :] += jnp.sum(jnp.mean(err * err, axis=-1, keepdims=True), axis=0, keepdims=True) * 0.5
        dy = err * (1.0 / D_MODEL)
        slot_ref[0:1, :] += jnp.sum(dy * n, axis=0, keepdims=True)
        dn = dy * gfv
        dx = r * (dn - n * jnp.mean(dn * n, axis=-1, keepdims=True))
        dx_ref[...] = dx
        dx16_ref[...] = dx.astype(BF16)

    row_f32 = pl.BlockSpec((tm, D_MODEL), lambda i: (i, 0))
    return pl.pallas_call(
        body, name="mlp_fwd_loss", grid=(s // tm,),
        out_shape=[jax.ShapeDtypeStruct((s, D_MODEL), F32), jax.ShapeDtypeStruct((s, D_MODEL), BF16),
                   jax.ShapeDtypeStruct((s, D_FF), BF16), jax.ShapeDtypeStruct((s, D_MODEL), BF16),
                   jax.ShapeDtypeStruct((SLOT, D_MODEL), F32)],
        in_specs=[row_f32, _full((1, D_MODEL)), VMEM_WHOLE, VMEM_WHOLE, _full((1, D_MODEL)), row_f32],
        out_specs=[row_f32, row_f32, pl.BlockSpec((tm, D_FF), lambda i: (i, 0)), row_f32, _full((SLOT, D_MODEL))],
        compiler_params=_params(("arbitrary",)),
    )(x, g, w1, w2, gf, target)


def _zero_slot(slot_ref):
    @pl.when(pl.program_id(0) == 0)
    def _():
        slot_ref[...] = jnp.zeros_like(slot_ref)


def _mlp_bwd(dx3, u, x2, g, w1, w2, deps):
    s = x2.shape[0]
    tm = min(ROW_TILE // 2, s)

    def body(d_ref, u_ref, x_ref, g_ref, w1_ref, w2_ref, da_ref, dx_ref, slot_ref):
        _zero_slot(slot_ref)
        d = d_ref[...]
        d16 = d.astype(BF16)
        du_next = _mm_nt(d16, w2_ref[0])
        dhf = jnp.zeros((tm, D_MODEL), F32)
        for j in range(N_DEV):
            sl = slice(FF_BLK * j, FF_BLK * (j + 1))
            du = du_next
            if j + 1 < N_DEV:
                du_next = _mm_nt(d16, w2_ref[j + 1])
            u = u_ref[:, sl].astype(F32)
            da = (du * (2.0 * u * lax.rsqrt(jnp.maximum(u, TINY)))).astype(BF16)
            da_ref[:, sl] = da
            dhf = dhf + _mm_nt(da, w1_ref[j])
        dx, dg = _rms_bwd(x_ref[...], g_ref[...], dhf)
        dx_ref[...] = d + dx
        slot_ref[0:1, :] += dg

    row_f32 = pl.BlockSpec((tm, D_MODEL), lambda i: (i, 0))
    return _call_behind(
        deps, body, name="mlp_bwd", grid=(s // tm,),
        out_shape=[jax.ShapeDtypeStruct((s, D_FF), BF16), jax.ShapeDtypeStruct((s, D_MODEL), F32),
                   jax.ShapeDtypeStruct((SLOT, D_MODEL), F32)],
        in_specs=[row_f32, pl.BlockSpec((tm, D_FF), lambda i: (i, 0)), row_f32, _full((1, D_MODEL)),
                  VMEM_WHOLE, VMEM_WHOLE],
        out_specs=[pl.BlockSpec((tm, D_FF), lambda i: (i, 0)), row_f32, _full((SLOT, D_MODEL))],
        compiler_params=_params(("arbitrary",)),
    )(dx3, u, x2, g, w1, w2)


def _wgrad(a, b, name, col_blocks=False, update=None):
    s, m = a.shape
    n = b.shape[1]
    tm = 1280 if m % 1280 == 0 else min(1024, m)
    tn = min(1024, n)
    blk = n // N_DEV
    per_step = tn // blk if col_blocks else 1
    ts = min((4 if m * n >= D_MODEL * D_FF else 2) * ROW_TILE, s)
    n_s = s // ts
    grid = (m // tm, n // tn, n_s)

    def body(a_ref, b_ref, *rest):
        o_ref, acc = rest[-2], rest[-1]
        k = pl.program_id(2)

        @pl.when(k == 0)
        def _():
            acc[...] = jnp.zeros_like(acc)

        acc[...] += _mm_tn(a_ref[...], b_ref[...])
        if update is not None:
            r_ref, w_ref, m_ref, v_ref, g_ref, d_ref, nm_ref, nv_ref = rest[:8]
            g = r_ref[0].astype(F32)
            for d in range(1, N_DEV):
                g = g + r_ref[d].astype(F32)
            g_ref[...] = g
            d_ref[...], nm_ref[...], nv_ref[...] = _adam_update(g, w_ref[...], m_ref[...], v_ref[...])

        @pl.when(k == n_s - 1)
        def _():
            if col_blocks:
                for p in range(per_step):
                    o_ref[p] = acc[:, blk * p:blk * (p + 1)].astype(BF16)
            else:
                o_ref[...] = acc[...].astype(BF16)

    if col_blocks:
        out_shape = jax.ShapeDtypeStruct((N_DEV, m, blk), BF16)
        out_spec = pl.BlockSpec((per_step, tm, blk), lambda i, j, k: (j, i, 0))
    else:
        out_shape = jax.ShapeDtypeStruct((m, n), BF16)
        out_spec = pl.BlockSpec((tm, tn), lambda i, j, k: (i, j))
    in_specs = [pl.BlockSpec((ts, tm), lambda i, j, k: (k, i)), pl.BlockSpec((ts, tn), lambda i, j, k: (k, j))]
    out_shapes, out_specs, operands = [out_shape], [out_spec], [a, b]
    if update is not None:
        rows, cols = update[1].shape
        steps = grid[0] * grid[1] * grid[2]
        tr = rows // steps
        step = lambda i, j, k: (i * grid[1] + j) * grid[2] + k
        piece = pl.BlockSpec((tr, cols), lambda i, j, k: (step(i, j, k), 0))
        in_specs += [pl.BlockSpec((N_DEV, tr, cols), lambda i, j, k: (0, step(i, j, k), 0)), piece, piece, piece]
        out_shapes = [jax.ShapeDtypeStruct((rows, cols), F32)] * 4 + out_shapes
        out_specs = [piece] * 4 + out_specs
        operands += list(update)
    outs = pl.pallas_call(
        body, name=name, grid=grid, out_shape=out_shapes, in_specs=in_specs, out_specs=out_specs,
        scratch_shapes=[pltpu.VMEM((tm, tn), F32)],
        compiler_params=_params(("parallel", "parallel", "arbitrary")),
    )(*operands)
    return outs[0] if update is None else (outs[4], tuple(outs[:4]))


def _xattn_bwd(dx2, x1, g, q, xk, xv, wq, wo_t, deps):
    s = x1.shape[0]
    tm = min(ROW_TILE, s)
    scale = XHD ** -0.5

    def body(d_ref, x_ref, g_ref, q_ref, k_ref, v_ref, wq_ref, wo_ref, dx_ref, dx16_ref, dq_ref, dk_ref, dv_ref, slot_ref,
             datt):
        _zero_slot(slot_ref)

        @pl.when(pl.program_id(0) == 0)
        def _():
            dk_ref[...] = jnp.zeros_like(dk_ref)
            dv_ref[...] = jnp.zeros_like(dv_ref)

        d = d_ref[...]
        datt[...] = _mm(d, wo_ref[...]).astype(BF16)
        heads = [slice(XHD * h, XHD * (h + 1)) for h in range(HEADS)]
        scores = [_mm_nt(q_ref[:, sl], k_ref[:, sl]) for sl in heads]
        dps = [_mm_nt(datt[:, sl], v_ref[:, sl]) for sl in heads]
        probs = [_softmax_rows(sc) for sc in scores]
        dss = [(p * (dp - jnp.sum(dp * p, axis=-1, keepdims=True))).astype(BF16) for p, dp in zip(probs, dps)]
        for sl, p, ds in zip(heads, probs, dss):
            dq_ref[:, sl] = (_mm(ds, k_ref[:, sl]) * scale).astype(BF16)
            dk_ref[:, sl] += _mm_tn(ds, q_ref[:, sl])
            dv_ref[:, sl] += _mm_tn(p, datt[:, sl])
        dx, dg = _rms_bwd(x_ref[...], g_ref[...], _mm_nt(dq_ref[...], wq_ref[...]))
        dx_ref[...] = d + dx
        dx16_ref[...] = (d + dx).astype(BF16)
        slot_ref[0:1, :] += dg

    row_f32 = pl.BlockSpec((tm, D_MODEL), lambda i: (i, 0))
    kv = jax.ShapeDtypeStruct((MEM_LEN, D_MODEL), F32)
    tokens16 = jax.ShapeDtypeStruct((s, D_MODEL), BF16)
    return _call_behind(
        deps, body, name="xattn_bwd", grid=(s // tm,),
        out_shape=[jax.ShapeDtypeStruct((s, D_MODEL), F32), tokens16, tokens16, kv, kv,
                   jax.ShapeDtypeStruct((SLOT, D_MODEL), F32)],
        in_specs=[row_f32, row_f32, _full((1, D_MODEL)), row_f32, VMEM_WHOLE, VMEM_WHOLE, VMEM_WHOLE, VMEM_WHOLE],
        out_specs=[row_f32, row_f32, row_f32, _full((MEM_LEN, D_MODEL)), _full((MEM_LEN, D_MODEL)),
                   _full((SLOT, D_MODEL))],
        scratch_shapes=[pltpu.VMEM((tm, D_MODEL), BF16)],
        compiler_params=_params(("arbitrary",)),
    )(dx2, x1, g, q, xk, xv, wq, wo_t)


def _mem_bwd(mem, g, hm, dxk, dxv, wk, wv):
    def body(m_ref, g_ref, hm_ref, dk_ref, dv_ref, wk_ref, wv_ref, dwk_ref, dwv_ref, slot_ref):
        dk, dv = dk_ref[...], dv_ref[...]
        hm_ = hm_ref[...]
        dwk_ref[...] = _mm_tn(hm_, dk).astype(BF16)
        dwv_ref[...] = _mm_tn(hm_, dv).astype(BF16)
        _, dg = _rms_bwd(m_ref[...], g_ref[...], _mm_nt(dk, wk_ref[...]) + _mm_nt(dv, wv_ref[...]))
        slot_ref[...] = jnp.zeros_like(slot_ref)
        slot_ref[0:1, :] = dg

    wshape = jax.ShapeDtypeStruct((D_MODEL, D_MODEL), BF16)
    return pl.pallas_call(
        body, name="mem_bwd", out_shape=[wshape, wshape, jax.ShapeDtypeStruct((SLOT, D_MODEL), F32)],
        in_specs=[VMEM_WHOLE] * 7, out_specs=[VMEM_WHOLE] * 3,
        compiler_params=_params(),
    )(mem, g, hm, dxk, dxv, wk, wv)


def _pool_bwd(dx1, w_out, pooled, w_pool, scale, deps):
    s = dx1.shape[0]
    tm = min(ROW_TILE, s)
    n_t = s // tm

    def body(dx_ref, wo_ref, pl_ref, w_ref, sc_ref, dz_ref, dw_ref, slot_ref, ext, do_ref):
        i = pl.program_id(0)
        tile = n_t - 1 - i
        _zero_slot(slot_ref)
        do_ref[...] = _mm_nt(dx_ref[...], wo_ref[HW:2 * HW, :])

        @pl.when(i == 0)
        def _():
            dw_ref[...] = jnp.zeros_like(dw_ref)
            ext[tm:tm + POOL_HALO, :] = jnp.zeros((POOL_HALO, HW), F32)

        @pl.when(i > 0)
        def _():
            ext[tm:tm + POOL_HALO, :] = ext[0:POOL_HALO, :]

        inv = _pool_counts(tile, tm)
        dpooled = []
        for g in range(HEADS):
            sl = slice(HD * g, HD * (g + 1))
            pooled_g = pl_ref[:, sl]
            do = do_ref[:, sl]
            slot_ref[0:1, sl] += jnp.sum(_mm(pooled_g, w_ref[g]) * do, axis=0, keepdims=True)
            dy = (do * sc_ref[:, sl]).astype(BF16)
            dw_ref[g] += _mm_tn(pooled_g, dy)
            dpo = _mm_nt(dy, w_ref[g])
            dpooled.append(dpo)
            ext[0:tm, sl] = dpo * inv[g]
        for g, w in enumerate(POOL_WINDOWS):
            sl = slice(HD * g, HD * (g + 1))
            win = ext[0:tm, sl]
            for d in range(1, w):
                win = win + ext[d:d + tm, sl]
            dz_ref[:, sl] = (win - dpooled[g]).astype(BF16)

    return _call_behind(
        deps, body, name="pool_bwd", grid=(n_t,),
        out_shape=[jax.ShapeDtypeStruct((s, IN_WIDTH), BF16), jax.ShapeDtypeStruct((HEADS, HD, HD), F32),
                   jax.ShapeDtypeStruct((SLOT, D_MODEL), F32)],
        in_specs=[pl.BlockSpec((tm, D_MODEL), lambda i: (n_t - 1 - i, 0)), VMEM_WHOLE,
                  pl.BlockSpec((tm, HW), lambda i: (n_t - 1 - i, 0)), _full((HEADS, HD, HD)), _full((1, HW))],
        out_specs=[pl.BlockSpec((tm, HW), lambda i: (n_t - 1 - i, 4)), _full((HEADS, HD, HD)), _full((SLOT, D_MODEL))],
        scratch_shapes=[pltpu.VMEM((tm + POOL_HALO, HW), F32), pltpu.VMEM((tm, HW), F32)],
        compiler_params=_params(("arbitrary",)),
    )(dx1, w_out, pooled, w_pool, scale)


def _hgrn_bwd(z, o, dx1, w_out, states, lb_logits, gn, dz_in, deps):
    s = z.shape[0]
    n_chunks = s // CHUNK

    def body(zq_ref, zf_ref, zi_ref, zg_ref, o_ref, dx_ref, wo_ref, st_ref, lbl_ref, gn_ref, dzin_ref,
             dz_ref, dlb_ref, dgn_ref, dstate, b_scr, dlb_acc, do_ref):
        i = pl.program_id(0)

        @pl.when(i == 0)
        def _():
            dstate[...] = jnp.zeros_like(dstate)
            dlb_acc[...] = jnp.zeros_like(dlb_acc)
            dgn_ref[...] = jnp.zeros_like(dgn_ref)
            dlb_ref[...] = jnp.zeros_like(dlb_ref)

        do_ref[...] = _mm_nt(dx_ref[...], wo_ref[0:HW, :])
        lb = _sigmoid(lbl_ref[0:1, :] - lbl_ref[1:2, :])
        row, col = _chunk_masks()
        causal = col <= row
        tri = _ones_where(causal)
        upper = _ones_where(col >= row)
        strict_lower = _ones_where(col < row)
        in_sub, in_sub_head = _sub_chunk_masks(HW), _sub_chunk_masks(HD)
        gn_row = _lanes([gn_ref[h:h + 1, :] for h in range(HEADS)])
        sums = {"dlb": 0.0, "dgn": 0.0}

        def front(c):
            r0 = CHUNK * c
            rs = slice(r0, r0 + CHUNK)
            p = {"rs": rs}
            p["zq"] = zq_ref[rs, :]
            p["q"], p["sq"], p["sig"], p["f"] = _hgrn_gates(p["zq"], zf_ref[rs, :], lb)
            p["kk"] = 1.0 - p["f"]
            b = _tri_dot(tri, jnp.log(p["f"]), 3)
            b_scr[rs, :] = b
            p["v"] = zi_ref[rs, :]
            o, zg, doa = o_ref[rs, :], zg_ref[rs, :], do_ref[rs, :]
            sg = _sigmoid(zg)
            rms = _head_rms(o)
            n = o * rms
            don = doa * (zg * sg)
            sums["dgn"] = sums["dgn"] + jnp.sum(don * n, axis=0, keepdims=True)
            dn = don * gn_row
            p["d_o"] = rms * (dn - n * _head_mean(dn * n))
            dz_ref[rs, 3 * HW:4 * HW] = (doa * (n * gn_row) * (sg * (1.0 + zg * (1.0 - sg)))).astype(BF16)
            p["eq"], p["ek"] = _hgrn_decay_factors(b_scr, r0, b, in_sub)
            b_last = b_scr[r0 + CHUNK - 1:r0 + CHUNK, :]
            p["lam"], p["e_last"], p["lam_last"] = jnp.exp(b), jnp.exp(b_last - b), jnp.exp(b_last)
            p["qe"], p["qg"], p["kd"] = p["q"] * p["eq"], p["q"] * p["lam"], p["kk"] * p["e_last"]
            p["ke"] = [p["kk"] * e for e in p["ek"]]
            p["q16"] = [_per_sub_chunk(_head(p["qe"], h), in_sub_head).astype(BF16) for h in range(HEADS)]
            p["ke16"] = [_lanes([_head(p["ke"][j], h) for j in range(N_SUB)]).astype(BF16) for h in range(HEADS)]
            return p

        def recurrence(c, p):
            m = {k: [] for k in ("dv", "gq", "gk", "dqi", "dkd", "st")}
            a, da, dv_state = [], [], []
            for h in range(HEADS):
                vh, doh = _head(p["v"], h), _head(p["d_o"], h)
                st0, ds1 = st_ref[c, h], dstate[h]
                a.append(jnp.where(causal, _mm_nt(p["q16"][h], p["ke16"][h]), 0.0))
                da.append(jnp.where(causal, _mm_nt(doh, vh), 0.0))
                dv_state.append(_mm_nt(_head(p["kd"], h), ds1))
                m["dqi"].append(_mm(doh, st0))
                m["dkd"].append(_mm(vh, ds1))
                m["st"].append(jnp.sum(st0 * ds1, axis=0, keepdims=True))
                dstate[h] = ds1 * _head(p["lam_last"], h) + _mm_tn(doh, _head(p["qg"], h))
            for h in range(HEADS):
                m["dv"].append(_mm_tn(a[h], _head(p["d_o"], h)) + dv_state[h])
                m["gq"].append(_own_lane_block(_mm(da[h], p["ke16"][h]), in_sub_head))
                m["gk"].append(_mm_tn(da[h], p["q16"][h]))
            return m

        def back(p, m):
            rs = p["rs"]
            dz_ref[rs, 2 * HW:3 * HW] = _lanes(m["dv"]).astype(BF16)
            gq = _lanes(m["gq"])
            gk = [_lanes([m["gk"][h][:, HD * j:HD * (j + 1)] for h in range(HEADS)]) for j in range(N_SUB)]
            dq_inter = p["lam"] * _lanes(m["dqi"])
            dq = p["eq"] * gq + dq_inter
            dk_intra = sum(p["ek"][j] * gk[j] for j in range(N_SUB))
            dk_state = _lanes(m["dkd"]) * p["e_last"]
            db_intra = (p["qe"].astype(BF16).astype(F32) * gq
                        - sum(p["ke"][j].astype(BF16).astype(F32) * gk[j] for j in range(N_SUB)))
            dlf = (_tri_dot(upper, db_intra + p["q"] * dq_inter, 2) + _tri_dot(strict_lower, p["kk"] * dk_state, 2)
                   + p["lam_last"] * _lanes(m["st"]))
            sig, sq, zq = p["sig"], p["sq"], p["zq"]
            df = dlf / p["f"] - (dk_intra + dk_state)
            sums["dlb"] = sums["dlb"] + jnp.sum(df * (1.0 - sig), axis=0, keepdims=True)
            dz_ref[rs, HW:2 * HW] = (df * (1.0 - lb) * sig * (1.0 - sig)).astype(BF16)
            dz_ref[rs, 0:HW] = (dq * (sq * (1.0 + zq * (1.0 - sq)))).astype(BF16)

        p = front(CHUNKS_PER_STEP - 1)
        for c in reversed(range(CHUNKS_PER_STEP)):
            m = recurrence(c, p)
            p_next = front(c - 1) if c > 0 else None
            back(p, m)
            p = p_next
        dlb_acc[...] += sums["dlb"]
        for h in range(HEADS):
            dgn_ref[h:h + 1, 0:HD] += _head(sums["dgn"], h)

        @pl.when(i == n_steps - 1)
        def _():
            dl0 = dlb_acc[...] * lb * (1.0 - lb)
            dlb_ref[0:1, 0:HW] = dl0
            dlb_ref[1:2, 0:HW] = -dl0

    rows = CHUNK * CHUNKS_PER_STEP
    n_steps = s // rows
    rev = lambda i: n_steps - 1 - i
    zspec = lambda cb: pl.BlockSpec((rows, HW), lambda i, cb=cb: (rev(i), cb))
    slot = jax.ShapeDtypeStruct((SLOT, D_MODEL), F32)
    return _call_behind(
        deps, body, name="hgrn_bwd", grid=(n_steps,),
        out_shape=[jax.ShapeDtypeStruct((s, IN_WIDTH), BF16), slot, slot],
        in_specs=[zspec(0), zspec(1), zspec(2), zspec(3), pl.BlockSpec((rows, HW), lambda i: (rev(i), 0)),
                  pl.BlockSpec((rows, D_MODEL), lambda i: (rev(i), 0)), VMEM_WHOLE,
                  pl.BlockSpec((CHUNKS_PER_STEP, HEADS, HD, HD), lambda i: (rev(i), 0, 0, 0)), _full((2, HW)),
                  _full((HEADS, HD)), ANY_SPACE],
        out_specs=[pl.BlockSpec((rows, 4 * HW), lambda i: (rev(i), 0)), _full((SLOT, D_MODEL)), _full((SLOT, D_MODEL))],
        scratch_shapes=[pltpu.VMEM((HEADS, HD, HD), F32), pltpu.VMEM((rows, HW), F32), pltpu.VMEM((1, HW), F32),
                        pltpu.VMEM((rows, HW), F32)],
        input_output_aliases={10: 0},
        compiler_params=_params(("arbitrary",)),
    )(z, z, z, z, o, dx1, w_out, states, lb_logits, gn, dz_in)


def _in_bwd(dz, w_t, x0, g, dx1, deps):
    s = x0.shape[0]
    tm = min(WIDE_ROW_TILE, s)

    def body(dz_ref, w_ref, x_ref, g_ref, d_ref, dx_ref, slot_ref):
        _zero_slot(slot_ref)
        dx, dg = _rms_bwd(x_ref[...], g_ref[...], _mm(dz_ref[...], w_ref[...]))
        dx_ref[...] = d_ref[...] + dx
        slot_ref[0:1, :] += dg

    row_f32 = pl.BlockSpec((tm, D_MODEL), lambda i: (i, 0))
    return _call_behind(
        deps, body, name="in_bwd", grid=(s // tm,),
        out_shape=[jax.ShapeDtypeStruct((s, D_MODEL), F32), jax.ShapeDtypeStruct((SLOT, D_MODEL), F32)],
        in_specs=[pl.BlockSpec((tm, IN_WIDTH), lambda i: (i, 0)), VMEM_WHOLE, row_f32, _full((1, D_MODEL)), row_f32],
        out_specs=[row_f32, _full((SLOT, D_MODEL))],
        compiler_params=_params(("arbitrary",)),
    )(dz, w_t, x0, g, dx1)


def kernel(x, mem, norm_mix_g, w_in, lb_logits, hgrn_norm_g, w_pool, pool_scale, w_out, norm_x_g, norm_mem_g, w_xq, w_xk, w_xv, w_xo, norm_ffn_g, w_ff1, w_ff2, final_norm_g, loss_target, m_norm_mix_g, m_w_in, m_lb_logits, m_hgrn_norm_g, m_w_pool, m_pool_scale, m_w_out, m_norm_x_g, m_norm_mem_g, m_w_xq, m_w_xk, m_w_xv, m_w_xo, m_norm_ffn_g, m_w_ff1, m_w_ff2, m_final_norm_g, v_norm_mix_g, v_w_in, v_lb_logits, v_hgrn_norm_g, v_w_pool, v_pool_scale, v_w_out, v_norm_x_g, v_norm_mem_g, v_w_xq, v_w_xk, v_w_xv, v_w_xo, v_norm_ffn_g, v_w_ff1, v_w_ff2, v_final_norm_g):
    x0 = x[0]
    mem0 = mem[0]
    tgt = loss_target[0]
    gn = hgrn_norm_g[0]
    gfin = final_norm_g.reshape(1, D_MODEL)
    wp = w_pool[0]
    heads_2d = lambda w: w.reshape(D_MODEL // N_DEV, D_MODEL)
    xo_2d = lambda w: w.reshape(D_MODEL, D_MODEL // N_DEV)

    first = _all_gather_weights([w_in[0].T], [w_out[0], heads_2d(w_xq), heads_2d(w_xk), heads_2d(w_xv), xo_2d(w_xo).T,
                                              w_ff1[0], w_ff2[0]])
    win_t = first[0].reshape(IN_WIDTH, D_MODEL)
    ga_attn, ga_mlp = _gather_first_start([first[1:6], first[6:8]], "gather_first_start")

    z, h = _in_proj(x0, norm_mix_g, win_t, deps=[ga_attn[3]])
    mixed_a, o_pre, states = _hgrn_fwd(z, lb_logits, gn)
    lands = _split_wait(_gather_first_copies, ga_attn, o_pre, "gather_attn_first_wait")
    gb_attn = _gather_forward_start(lands, "gather_attn_forward_start")
    mixed, pooled = _pool_fwd(z, wp, pool_scale, mixed_a, deps=[gb_attn[3]])
    lands = _split_wait(_gather_forward_copies, gb_attn, pooled, "gather_attn_forward_wait")
    wout_f, wq_f, wk_f, wv_f, wo_t = (t.reshape(D_MODEL, D_MODEL) for t in lands)
    hm, xk, xv = _mem_kv(mem0, norm_mem_g, wk_f, wv_f, deps=[])
    x1, x2, hq, xq, att = _mix_xattn_fwd(x0, mixed, wout_f, norm_x_g, wq_f, xk, xv, wo_t, deps=[])
    lands = _split_wait(_gather_first_copies, ga_mlp, x2, "gather_mlp_first_wait")
    gb_mlp = _gather_forward_start(lands, "gather_mlp_forward_start")
    w1_b, w2_b = _split_wait(_gather_forward_copies, gb_mlp, gb_mlp[3], "gather_mlp_forward_wait")
    dx3, dx3_16, u, hf, slot_fin = _mlp_fwd_loss(x2, norm_ffn_g, w1_b, w2_b.reshape(D_FF, D_MODEL), gfin, tgt)

    rows = lambda t, r: t.reshape(N_DEV, r, D_MODEL)
    dw2 = _wgrad(u, dx3_16, "wgrad_ff2")
    ex_ff2 = _all_to_all_start([rows(dw2, FF_BLK)], [], "exchange_ff2_start")
    da, dx2, slot_ffn = _mlp_bwd(dx3, u, x2, norm_ffn_g, w1_b, w2_b, deps=[ex_ff2[3]])
    dw1 = _wgrad(hf, da, "wgrad_ff1", col_blocks=True)
    ex_ff1 = _all_to_all_start([dw1], [], "exchange_ff1_start")
    dx1, dx1_16, dxq, dxk, dxv, slot_x = _xattn_bwd(dx2, x1, norm_x_g, xq, xk, xv, wq_f, wo_t, deps=[ex_ff1[3]])
    dwo_t = _wgrad(dx2, att, "wgrad_xo")
    dwq = _wgrad(hq, dxq, "wgrad_xq")
    dwk, dwv, slot_mem = _mem_bwd(mem0, norm_mem_g, hm, dxk, dxv, wk_f, wv_f)
    ex_attn = _all_to_all_start([rows(dwq, 128), rows(dwk, 128), rows(dwv, 128), rows(dwo_t, 128)], [],
                                "exchange_attn_start")
    dwout = _wgrad(mixed, dx1_16, "wgrad_out")
    dz_pool, d_wpool, slot_ps = _pool_bwd(dx1_16, wout_f, pooled, wp, pool_scale, deps=[ex_attn[3]])
    small0 = jnp.concatenate([slot_x, slot_mem, slot_ffn, slot_fin, slot_ps], axis=0)
    ex_out = _all_to_all_start([rows(dwout, 128)], [small0, d_wpool], "exchange_out_start")
    dz, slot_lb, slot_gn = _hgrn_bwd(z, o_pre, dx1_16, wout_f, states, lb_logits, gn, dz_pool, deps=[ex_out[3]])
    (r_2,) = _split_wait(_all_to_all_copies(1), ex_ff2, dz, "exchange_ff2_wait")
    dwin_t, ff2_update = _wgrad(dz, h, "wgrad_in", update=(r_2, w_ff2[0], m_w_ff2[0], v_w_ff2[0]))
    ex_in = _all_to_all_start([rows(dwin_t, 320)], [], "exchange_in_start")
    grad_x, slot_mix = _in_bwd(dz, win_t, x0, norm_mix_g, dx1, deps=[ex_in[3]])
    small1 = jnp.concatenate([slot_mix, slot_lb, slot_gn], axis=0)
    ex_mix = _all_to_all_start([], [small1], "exchange_mix_start")

    out = {}
    out["w_ff2"] = ff2_update
    (r_1,) = _split_wait(_all_to_all_copies(1), ex_ff1, ex_mix[3], "exchange_ff1_wait")
    out["w_ff1"] = _sum_adamw(r_1, w_ff1[0], m_w_ff1[0], v_w_ff1[0], "adamw_ff1")
    r_q, r_k, r_v, r_o = _split_wait(_all_to_all_copies(4), ex_attn, out["w_ff1"][1], "exchange_attn_wait")
    sums = _sum_sources_whole([r_q, r_k, r_v, r_o], "sum_grad_attn")
    g_attn = [g.reshape(w_xq.shape) for g in sums[:3]] + [sums[3].T]
    attn = _adamw_whole([(g_attn[0], w_xq, m_w_xq, v_w_xq), (g_attn[1], w_xk, m_w_xk, v_w_xk),
                         (g_attn[2], w_xv, m_w_xv, v_w_xv),
                         (sums[3], xo_2d(w_xo).T, xo_2d(m_w_xo).T, xo_2d(v_w_xo).T)], "adamw_attn")
    attn = list(attn[:3]) + [tuple(t.T for t in attn[3])]
    for n, g, res in zip(("w_xq", "w_xk", "w_xv", "w_xo"), g_attn, attn):
        out[n] = (g, *res)
    r_out, r_small0, r_wpool = _split_wait(_all_to_all_copies(1), ex_out, attn[0][0], "exchange_out_wait")
    out["w_out"] = _sum_adamw(r_out, w_out[0], m_w_out[0], v_w_out[0], "adamw_out")
    (r_in,) = _split_wait(_all_to_all_copies(1), ex_in, out["w_out"][1], "exchange_in_wait")
    in_t = _sum_adamw(r_in, w_in[0].T, m_w_in[0].T, v_w_in[0].T, "adamw_in")
    out["w_in"] = tuple(t.T for t in in_t)
    (r_small1,) = _split_wait(_all_to_all_copies(0), ex_mix, in_t[1], "exchange_mix_wait")
    row = lambda t: t.reshape(1, -1)
    small_params = {
        "norm_mix_g": (norm_mix_g, m_norm_mix_g, v_norm_mix_g),
        "lb_logits": (lb_logits, m_lb_logits, v_lb_logits),
        "hgrn_norm_g": (hgrn_norm_g[0], m_hgrn_norm_g[0], v_hgrn_norm_g[0]),
        "pool_scale": (pool_scale, m_pool_scale, v_pool_scale),
        "norm_x_g": (norm_x_g, m_norm_x_g, v_norm_x_g),
        "norm_mem_g": (norm_mem_g, m_norm_mem_g, v_norm_mem_g),
        "norm_ffn_g": (norm_ffn_g, m_norm_ffn_g, v_norm_ffn_g),
        "final_norm_g": (row(final_norm_g), row(m_final_norm_g), row(v_final_norm_g)),
        "w_pool": (wp, m_w_pool[0], v_w_pool[0]),
    }
    loss, small_out = _small_update([r_small0, r_small1], r_wpool, small_params)
    out.update(small_out)

    shapes = dict(norm_mix_g=norm_mix_g, w_in=w_in, lb_logits=lb_logits, hgrn_norm_g=hgrn_norm_g, w_pool=w_pool,
                  pool_scale=pool_scale, w_out=w_out, norm_x_g=norm_x_g, norm_mem_g=norm_mem_g, w_xq=w_xq, w_xk=w_xk,
                  w_xv=w_xv, w_xo=w_xo, norm_ffn_g=norm_ffn_g, w_ff1=w_ff1, w_ff2=w_ff2, final_norm_g=final_norm_g)
    order = list(shapes)
    group = lambda k: [out[n][k].reshape(shapes[n].shape) for n in order]
    return (loss.reshape(()), grad_x.reshape(x.shape), *group(0), *group(1), *group(2), *group(3))
```
